```python
import math
import jax, jax.numpy as jnp
from jax import lax
import numpy as np

D_MODEL = 1024
BATCH = 8
SEQ = 2048
DEPTH = 4

N_MIXERS = 4
N_A = (DEPTH + 3) // 4
N_B = (DEPTH + 2) // 4
N_C = (DEPTH + 1) // 4
N_D = DEPTH // 4
EPS = 1e-6
NEG_INF = -1e30

SSM_WIDTH = D_MODEL
SSM_GROUP = 16
SSM_GROUPS = SSM_WIDTH // SSM_GROUP
SSM_STATE = 64
DT_MIN = 1e-3
DT_MAX = 1e-1

HEAD_DIM = 64
SWA_HEADS = D_MODEL // HEAD_DIM
SWA_KV_HEADS = SWA_HEADS // 8
SWA_WIDTH = SWA_HEADS * HEAD_DIM
WINDOW = 128

REL_BUCKETS = 32
REL_MAX_DIST = 128

MLA_HEADS = 16
MLA_NOPE = 64
MLA_ROPE = 32
MLA_V = 64
MLA_KV_RANK = 256
MLA_Q_RANK = 768
MLA_WIDTH = MLA_HEADS * MLA_V
ROPE_BASE = 10000.0
Q_BLOCK = 128

SGU_WIDTH = D_MODEL
SGU_CHUNK = 128
SGU_GROUPS = 16
SGU_GROUP_DIM = SGU_WIDTH // SGU_GROUPS

kernel_name = 'hybrid_interleaved_s5_swa_mla_sgu'


def rmsnorm(x, g):
    xf = x.astype(jnp.float32)
    y = xf * lax.rsqrt(jnp.mean(xf * xf, axis=-1, keepdims=True) + EPS)
    return (y * g.astype(jnp.float32)).astype(x.dtype)


def layernorm(x, g, b):
    xf = x.astype(jnp.float32)
    mu = jnp.mean(xf, axis=-1, keepdims=True)
    var = jnp.mean(jnp.square(xf - mu), axis=-1, keepdims=True)
    y = (xf - mu) * lax.rsqrt(var + EPS) * g.astype(jnp.float32) + b.astype(jnp.float32)
    return y.astype(x.dtype)


def _ssm_combine(left, right):
    a1r, a1i, b1r, b1i = left
    a2r, a2i, b2r, b2i = right
    return (a2r * a1r - a2i * a1i,
            a2r * a1i + a2i * a1r,
            a2r * b1r - a2i * b1i + b2r,
            a2r * b1i + a2i * b1r + b2i)


def s5_mixer(u, lam_re, lam_im, log_dt, b_re, b_im, c_re, c_im, d_skip, w_glu, b_glu):
    f32 = jnp.float32
    bsz, L, _ = u.shape
    ug = u.astype(f32).reshape(bsz, L, SSM_GROUPS, SSM_GROUP)
    lr = lam_re.astype(f32)
    li = lam_im.astype(f32)
    dt = jnp.exp(log_dt.astype(f32))[:, None]
    mag = jnp.exp(lr * dt)
    ab_re = mag * jnp.cos(li * dt)
    ab_im = mag * jnp.sin(li * dt)
    den = lr * lr + li * li
    nr = ab_re - 1.0
    f_re = (nr * lr + ab_im * li) / den
    f_im = (ab_im * lr - nr * li) / den
    br = b_re.astype(f32)
    bi = b_im.astype(f32)
    bb_re = f_re[..., None] * br - f_im[..., None] * bi
    bb_im = f_re[..., None] * bi + f_im[..., None] * br
    bu_re = jnp.einsum('blgh,gph->blgp', ug, bb_re)
    bu_im = jnp.einsum('blgh,gph->blgp', ug, bb_im)
    a_re = jnp.broadcast_to(ab_re, (1, L) + ab_re.shape)
    a_im = jnp.broadcast_to(ab_im, (1, L) + ab_im.shape)
    _, _, s_re, s_im = lax.associative_scan(_ssm_combine, (a_re, a_im, bu_re, bu_im), axis=1)
    y = (jnp.einsum('blgp,ghp->blgh', s_re, c_re.astype(f32))
         - jnp.einsum('blgp,ghp->blgh', s_im, c_im.astype(f32)))
    y = y.reshape(bsz, L, SSM_WIDTH) + d_skip.astype(f32) * u.astype(f32)
    y = jax.nn.gelu(y).astype(u.dtype)
    return y * jax.nn.sigmoid(y @ w_glu + b_glu)


def s5_branch(h, w_in, lam_re, lam_im, log_dt, b_re, b_im, c_re, c_im, d_skip, w_glu, b_glu, w_out):
    u, z = jnp.split(h @ w_in, [SSM_WIDTH], axis=-1)
    y = s5_mixer(u, lam_re, lam_im, log_dt, b_re, b_im, c_re, c_im, d_skip, w_glu, b_glu)
    return (y * jax.nn.silu(z)) @ w_out


def t5_bucket(dist):
    max_exact = REL_BUCKETS // 2
    dist_f = jnp.maximum(dist, 1).astype(jnp.float32)
    large = max_exact + (jnp.log(dist_f / max_exact) / math.log(REL_MAX_DIST / max_exact)
                         * (REL_BUCKETS - max_exact)).astype(jnp.int32)
    large = jnp.minimum(large, REL_BUCKETS - 1)
    return jnp.where(dist < max_exact, dist, large)


def sliding_window_attention(q, k, v, sinks, rel_bias):
    bsz, L = q.shape[0], q.shape[1]
    nb = L // WINDOW
    grp = SWA_HEADS // SWA_KV_HEADS
    qb = q.reshape(bsz, nb, WINDOW, SWA_KV_HEADS, grp, HEAD_DIM)

    def band(t):
        prev = jnp.pad(t, ((0, 0), (WINDOW, 0), (0, 0), (0, 0)))[:, :L]
        shp = (bsz, nb, WINDOW, SWA_KV_HEADS, HEAD_DIM)
        return jnp.concatenate([prev.reshape(shp), t.reshape(shp)], axis=2)

    kb = band(k)
    vb = band(v)
    s = jnp.einsum('bnqhgd,bnkhd->bnhgqk', qb, kb).astype(jnp.float32) * (HEAD_DIM ** -0.5)
    qi = jnp.arange(WINDOW)[:, None]
    kj = jnp.arange(2 * WINDOW)[None, :]
    dist = qi + WINDOW - kj
    blk = jnp.arange(nb)[:, None, None]
    valid = (dist >= 0) & (dist < WINDOW) & (blk * WINDOW + kj - WINDOW >= 0)
    bias = rel_bias[t5_bucket(jnp.maximum(dist, 0))]
    bias = jnp.transpose(bias, (2, 0, 1)).reshape(SWA_KV_HEADS, grp, WINDOW, 2 * WINDOW).astype(jnp.float32)
    s = jnp.where(valid[None, :, None, None], s + bias, NEG_INF)
    sink = jnp.broadcast_to(sinks.astype(jnp.float32).reshape(SWA_KV_HEADS, grp, 1, 1), s.shape[:-1] + (1,))
    p = jax.nn.softmax(jnp.concatenate([s, sink], axis=-1), axis=-1)[..., :-1]
    o = jnp.einsum('bnhgqk,bnkhd->bnqhgd', p.astype(v.dtype), vb)
    return o.reshape(bsz, L, SWA_WIDTH)


def swa_branch(h, w_in, sinks, w_out, rel_bias):
    bsz, L, _ = h.shape
    kv_w = SWA_KV_HEADS * HEAD_DIM
    q, k, v, z = jnp.split(h @ w_in, [SWA_WIDTH, SWA_WIDTH + kv_w, SWA_WIDTH + 2 * kv_w], axis=-1)
    q = q.reshape(bsz, L, SWA_HEADS, HEAD_DIM)
    k = k.reshape(bsz, L, SWA_KV_HEADS, HEAD_DIM)
    v = v.reshape(bsz, L, SWA_KV_HEADS, HEAD_DIM)
    o = sliding_window_attention(q, k, v, sinks, rel_bias)
    return (o * jax.nn.silu(z)) @ w_out


def rope_tables(L):
    inv = ROPE_BASE ** (-jnp.arange(0, MLA_ROPE, 2, dtype=jnp.float32) / MLA_ROPE)
    ang = jnp.arange(L, dtype=jnp.float32)[:, None] * inv[None, :]
    return jnp.cos(ang), jnp.sin(ang)


def apply_rope(x, cos, sin):
    xf = x.astype(jnp.float32)
    x1, x2 = jnp.split(xf, 2, axis=-1)
    return jnp.concatenate([x1 * cos - x2 * sin, x2 * cos + x1 * sin], axis=-1).astype(x.dtype)


def causal_block_attention(q, k, v):
    bsz, L, H, dk = q.shape
    nb = L // Q_BLOCK
    scale = dk ** -0.5
    qb = q.reshape(bsz, nb, Q_BLOCK, H, dk).transpose(1, 0, 2, 3, 4)
    kpos = jnp.arange(L)

    def one_block(args):
        qi, n = args
        s = jnp.einsum('bqhd,bkhd->bhqk', qi, k).astype(jnp.float32) * scale
        qpos = n * Q_BLOCK + jnp.arange(Q_BLOCK)
        s = jnp.where(kpos[None, :] <= qpos[:, None], s, NEG_INF)
        p = jax.nn.softmax(s, axis=-1).astype(v.dtype)
        return jnp.einsum('bhqk,bkhd->bqhd', p, v)

    o = lax.map(one_block, (qb, jnp.arange(nb)))
    return o.transpose(1, 0, 2, 3, 4).reshape(bsz, L, H, v.shape[-1])


def mla_branch(h, w_in, q_norm, kv_norm, w_uq, w_ukv, w_out):
    bsz, L, _ = h.shape
    c_q, c_kv, k_rope, z = jnp.split(
        h @ w_in, [MLA_Q_RANK, MLA_Q_RANK + MLA_KV_RANK, MLA_Q_RANK + MLA_KV_RANK + MLA_ROPE], axis=-1)
    q = (rmsnorm(c_q, q_norm) @ w_uq).reshape(bsz, L, MLA_HEADS, MLA_NOPE + MLA_ROPE)
    kv = (rmsnorm(c_kv, kv_norm) @ w_ukv).reshape(bsz, L, MLA_HEADS, MLA_NOPE + MLA_V)
    cos, sin = rope_tables(L)
    q = jnp.concatenate([q[..., :MLA_NOPE], apply_rope(q[..., MLA_NOPE:], cos[:, None], sin[:, None])], axis=-1)
    k_rope = apply_rope(k_rope, cos, sin)
    k = jnp.concatenate([kv[..., :MLA_NOPE],
                         jnp.broadcast_to(k_rope[:, :, None, :], (bsz, L, MLA_HEADS, MLA_ROPE))], axis=-1)
    o = causal_block_attention(q, k, kv[..., MLA_NOPE:])
    return (o.reshape(bsz, L, MLA_WIDTH) * jax.nn.silu(z)) @ w_out


def sgu_branch(h, w_in, ln_g, ln_b, w_s, b_s, w_out):
    bsz, L, _ = h.shape
    uv, z = jnp.split(h @ w_in, [2 * SGU_WIDTH], axis=-1)
    u, v = jnp.split(jax.nn.gelu(uv), 2, axis=-1)
    v = layernorm(v, ln_g, ln_b).reshape(bsz, L // SGU_CHUNK, SGU_CHUNK, SGU_GROUPS, SGU_GROUP_DIM)
    tril = jnp.tril(jnp.ones((SGU_CHUNK, SGU_CHUNK), dtype=bool))
    w = jnp.where(tril[None], w_s, 0.0)
    s = jnp.einsum('gts,bnsgc->bntgc', w, v) + b_s.T[:, :, None]
    s = s.reshape(bsz, L, SGU_WIDTH)
    return (u * s * jax.nn.silu(z)) @ w_out


def _fwd_setup_inputs(seed: int = 0) -> dict:
    key = jax.random.key(seed)
    ks = iter(jax.random.split(key, 40))
    f32 = jnp.float32

    def nrm(shape, scale):
        return jax.random.normal(next(ks), shape, f32) * scale

    x = nrm((BATCH, SEQ, D_MODEL), 1.0)
    pre_norm = 1.0 + nrm((DEPTH, D_MODEL), 0.05)
    post_norm = 1.0 + nrm((DEPTH, D_MODEL), 0.05)
    rel_bias = nrm((REL_BUCKETS, SWA_HEADS), 0.5)
    a_w_in = nrm((N_A, D_MODEL, 2 * SSM_WIDTH), D_MODEL ** -0.5)
    n_idx = jnp.arange(SSM_STATE, dtype=f32)
    a_lam_re = -0.5 + nrm((N_A, SSM_GROUPS, SSM_STATE), 0.01)
    a_lam_im = jnp.pi * n_idx + nrm((N_A, SSM_GROUPS, SSM_STATE), 0.01)
    a_log_dt = jax.random.uniform(next(ks), (N_A, SSM_GROUPS), f32, math.log(DT_MIN), math.log(DT_MAX))
    a_b_re = nrm((N_A, SSM_GROUPS, SSM_STATE, SSM_GROUP), (2 * SSM_GROUP) ** -0.5)
    a_b_im = nrm((N_A, SSM_GROUPS, SSM_STATE, SSM_GROUP), (2 * SSM_GROUP) ** -0.5)
    a_c_re = nrm((N_A, SSM_GROUPS, SSM_GROUP, SSM_STATE), SSM_STATE ** -0.5)
    a_c_im = nrm((N_A, SSM_GROUPS, SSM_GROUP, SSM_STATE), SSM_STATE ** -0.5)
    a_d = nrm((N_A, SSM_WIDTH), 1.0)
    a_w_glu = nrm((N_A, SSM_WIDTH, SSM_WIDTH), SSM_WIDTH ** -0.5)
    a_b_glu = nrm((N_A, SSM_WIDTH), 0.02)
    a_w_out = nrm((N_A, SSM_WIDTH, D_MODEL), SSM_WIDTH ** -0.5)
    b_w_in = nrm((N_B, D_MODEL, 2 * SWA_WIDTH + 2 * SWA_KV_HEADS * HEAD_DIM), D_MODEL ** -0.5)
    b_sinks = nrm((N_B, SWA_HEADS), 1.0)
    b_w_out = nrm((N_B, SWA_WIDTH, D_MODEL), SWA_WIDTH ** -0.5)
    c_w_in = nrm((N_C, D_MODEL, MLA_Q_RANK + MLA_KV_RANK + MLA_ROPE + MLA_WIDTH), D_MODEL ** -0.5)
    c_q_norm = 1.0 + nrm((N_C, MLA_Q_RANK), 0.05)
    c_kv_norm = 1.0 + nrm((N_C, MLA_KV_RANK), 0.05)
    c_w_uq = nrm((N_C, MLA_Q_RANK, MLA_HEADS * (MLA_NOPE + MLA_ROPE)), MLA_Q_RANK ** -0.5)
    c_w_ukv = nrm((N_C, MLA_KV_RANK, MLA_HEADS * (MLA_NOPE + MLA_V)), MLA_KV_RANK ** -0.5)
    c_w_out = nrm((N_C, MLA_WIDTH, D_MODEL), MLA_WIDTH ** -0.5)
    d_w_in = nrm((N_D, D_MODEL, 3 * SGU_WIDTH), D_MODEL ** -0.5)
    d_ln_g = 1.0 + nrm((N_D, SGU_WIDTH), 0.05)
    d_ln_b = nrm((N_D, SGU_WIDTH), 0.02)
    d_w_s = nrm((N_D, SGU_GROUPS, SGU_CHUNK, SGU_CHUNK), 0.5 * SGU_CHUNK ** -0.5)
    d_b_s = 1.0 + nrm((N_D, SGU_GROUPS, SGU_CHUNK), 0.1)
    d_w_out = nrm((N_D, SGU_WIDTH, D_MODEL), SGU_WIDTH ** -0.5)
    return {'x': x, 'pre_norm': pre_norm, 'post_norm': post_norm, 'rel_bias': rel_bias,
            'a_w_in': a_w_in, 'a_lam_re': a_lam_re, 'a_lam_im': a_lam_im, 'a_log_dt': a_log_dt,
            'a_b_re': a_b_re, 'a_b_im': a_b_im, 'a_c_re': a_c_re, 'a_c_im': a_c_im, 'a_d': a_d,
            'a_w_glu': a_w_glu, 'a_b_glu': a_b_glu, 'a_w_out': a_w_out,
            'b_w_in': b_w_in, 'b_sinks': b_sinks, 'b_w_out': b_w_out,
            'c_w_in': c_w_in, 'c_q_norm': c_q_norm, 'c_kv_norm': c_kv_norm, 'c_w_uq': c_w_uq,
            'c_w_ukv': c_w_ukv, 'c_w_out': c_w_out,
            'd_w_in': d_w_in, 'd_ln_g': d_ln_g, 'd_ln_b': d_ln_b, 'd_w_s': d_w_s, 'd_b_s': d_b_s,
            'd_w_out': d_w_out}


def _fwd_reference(x, pre_norm, post_norm, rel_bias,
              a_w_in, a_lam_re, a_lam_im, a_log_dt, a_b_re, a_b_im, a_c_re, a_c_im, a_d,
              a_w_glu, a_b_glu, a_w_out,
              b_w_in, b_sinks, b_w_out,
              c_w_in, c_q_norm, c_kv_norm, c_w_uq, c_w_ukv, c_w_out,
              d_w_in, d_ln_g, d_ln_b, d_w_s, d_b_s, d_w_out):
    for i in range(DEPTH):
        kind = i % N_MIXERS
        j = i // N_MIXERS
        h = rmsnorm(x, pre_norm[i])
        if kind == 0:
            y = s5_branch(h, a_w_in[j], a_lam_re[j], a_lam_im[j], a_log_dt[j], a_b_re[j], a_b_im[j],
                          a_c_re[j], a_c_im[j], a_d[j], a_w_glu[j], a_b_glu[j], a_w_out[j])
        elif kind == 1:
            y = swa_branch(h, b_w_in[j], b_sinks[j], b_w_out[j], rel_bias)
        elif kind == 2:
            y = mla_branch(h, c_w_in[j], c_q_norm[j], c_kv_norm[j], c_w_uq[j], c_w_ukv[j], c_w_out[j])
        else:
            y = sgu_branch(h, d_w_in[j], d_ln_g[j], d_ln_b[j], d_w_s[j], d_b_s[j], d_w_out[j])
        x = x + rmsnorm(y, post_norm[i])
    return x


import jax as _jax
import jax.numpy as _jnp

TWIN_FORMAT = 'train_step'
FWD_PARAMS = ['x', 'pre_norm', 'post_norm', 'rel_bias', 'a_w_in', 'a_lam_re', 'a_lam_im', 'a_log_dt', 'a_b_re', 'a_b_im', 'a_c_re', 'a_c_im', 'a_d', 'a_w_glu', 'a_b_glu', 'a_w_out', 'b_w_in', 'b_sinks', 'b_w_out', 'c_w_in', 'c_q_norm', 'c_kv_norm', 'c_w_uq', 'c_w_ukv', 'c_w_out', 'd_w_in', 'd_ln_g', 'd_ln_b', 'd_w_s', 'd_b_s', 'd_w_out']
TWIN_WEIGHTS = ['pre_norm', 'post_norm', 'rel_bias', 'a_w_in', 'a_lam_re', 'a_lam_im', 'a_log_dt', 'a_b_re', 'a_b_im', 'a_c_re', 'a_c_im', 'a_d', 'a_w_glu', 'a_b_glu', 'a_w_out', 'b_w_in', 'b_sinks', 'b_w_out', 'c_w_in', 'c_q_norm', 'c_kv_norm', 'c_w_uq', 'c_w_ukv', 'c_w_out', 'd_w_in', 'd_ln_g', 'd_ln_b', 'd_w_s', 'd_b_s', 'd_w_out']
TWIN_DIFF_INPUT = 'x'
TWIN_INPUTS = ['x', 'pre_norm', 'post_norm', 'rel_bias', 'a_w_in', 'a_lam_re', 'a_lam_im', 'a_log_dt', 'a_b_re', 'a_b_im', 'a_c_re', 'a_c_im', 'a_d', 'a_w_glu', 'a_b_glu', 'a_w_out', 'b_w_in', 'b_sinks', 'b_w_out', 'c_w_in', 'c_q_norm', 'c_kv_norm', 'c_w_uq', 'c_w_ukv', 'c_w_out', 'd_w_in', 'd_ln_g', 'd_ln_b', 'd_w_s', 'd_b_s', 'd_w_out', 'loss_target', 'm_pre_norm', 'm_post_norm', 'm_rel_bias', 'm_a_w_in', 'm_a_lam_re', 'm_a_lam_im', 'm_a_log_dt', 'm_a_b_re', 'm_a_b_im', 'm_a_c_re', 'm_a_c_im', 'm_a_d', 'm_a_w_glu', 'm_a_b_glu', 'm_a_w_out', 'm_b_w_in', 'm_b_sinks', 'm_b_w_out', 'm_c_w_in', 'm_c_q_norm', 'm_c_kv_norm', 'm_c_w_uq', 'm_c_w_ukv', 'm_c_w_out', 'm_d_w_in', 'm_d_ln_g', 'm_d_ln_b', 'm_d_w_s', 'm_d_b_s', 'm_d_w_out', 'v_pre_norm', 'v_post_norm', 'v_rel_bias', 'v_a_w_in', 'v_a_lam_re', 'v_a_lam_im', 'v_a_log_dt', 'v_a_b_re', 'v_a_b_im', 'v_a_c_re', 'v_a_c_im', 'v_a_d', 'v_a_w_glu', 'v_a_b_glu', 'v_a_w_out', 'v_b_w_in', 'v_b_sinks', 'v_b_w_out', 'v_c_w_in', 'v_c_q_norm', 'v_c_kv_norm', 'v_c_w_uq', 'v_c_w_ukv', 'v_c_w_out', 'v_d_w_in', 'v_d_ln_g', 'v_d_ln_b', 'v_d_w_s', 'v_d_b_s', 'v_d_w_out']
TWIN_OUTPUTS = ['loss', 'grad_x', 'grad_pre_norm', 'grad_post_norm', 'grad_rel_bias', 'grad_a_w_in', 'grad_a_lam_re', 'grad_a_lam_im', 'grad_a_log_dt', 'grad_a_b_re', 'grad_a_b_im', 'grad_a_c_re', 'grad_a_c_im', 'grad_a_d', 'grad_a_w_glu', 'grad_a_b_glu', 'grad_a_w_out', 'grad_b_w_in', 'grad_b_sinks', 'grad_b_w_out', 'grad_c_w_in', 'grad_c_q_norm', 'grad_c_kv_norm', 'grad_c_w_uq', 'grad_c_w_ukv', 'grad_c_w_out', 'grad_d_w_in', 'grad_d_ln_g', 'grad_d_ln_b', 'grad_d_w_s', 'grad_d_b_s', 'grad_d_w_out', 'delta_pre_norm', 'delta_post_norm', 'delta_rel_bias', 'delta_a_w_in', 'delta_a_lam_re', 'delta_a_lam_im', 'delta_a_log_dt', 'delta_a_b_re', 'delta_a_b_im', 'delta_a_c_re', 'delta_a_c_im', 'delta_a_d', 'delta_a_w_glu', 'delta_a_b_glu', 'delta_a_w_out', 'delta_b_w_in', 'delta_b_sinks', 'delta_b_w_out', 'delta_c_w_in', 'delta_c_q_norm', 'delta_c_kv_norm', 'delta_c_w_uq', 'delta_c_w_ukv', 'delta_c_w_out', 'delta_d_w_in', 'delta_d_ln_g', 'delta_d_ln_b', 'delta_d_w_s', 'delta_d_b_s', 'delta_d_w_out', 'new_m_pre_norm', 'new_m_post_norm', 'new_m_rel_bias', 'new_m_a_w_in', 'new_m_a_lam_re', 'new_m_a_lam_im', 'new_m_a_log_dt', 'new_m_a_b_re', 'new_m_a_b_im', 'new_m_a_c_re', 'new_m_a_c_im', 'new_m_a_d', 'new_m_a_w_glu', 'new_m_a_b_glu', 'new_m_a_w_out', 'new_m_b_w_in', 'new_m_b_sinks', 'new_m_b_w_out', 'new_m_c_w_in', 'new_m_c_q_norm', 'new_m_c_kv_norm', 'new_m_c_w_uq', 'new_m_c_w_ukv', 'new_m_c_w_out', 'new_m_d_w_in', 'new_m_d_ln_g', 'new_m_d_ln_b', 'new_m_d_w_s', 'new_m_d_b_s', 'new_m_d_w_out', 'new_v_pre_norm', 'new_v_post_norm', 'new_v_rel_bias', 'new_v_a_w_in', 'new_v_a_lam_re', 'new_v_a_lam_im', 'new_v_a_log_dt', 'new_v_a_b_re', 'new_v_a_b_im', 'new_v_a_c_re', 'new_v_a_c_im', 'new_v_a_d', 'new_v_a_w_glu', 'new_v_a_b_glu', 'new_v_a_w_out', 'new_v_b_w_in', 'new_v_b_sinks', 'new_v_b_w_out', 'new_v_c_w_in', 'new_v_c_q_norm', 'new_v_c_kv_norm', 'new_v_c_w_uq', 'new_v_c_w_ukv', 'new_v_c_w_out', 'new_v_d_w_in', 'new_v_d_ln_g', 'new_v_d_ln_b', 'new_v_d_w_s', 'new_v_d_b_s', 'new_v_d_w_out']
TWIN_LEAF_KINDS = {'loss': 'loss', 'grad_x': 'grad_x', 'grad_pre_norm': 'grad_w', 'grad_post_norm': 'grad_w', 'grad_rel_bias': 'grad_w', 'grad_a_w_in': 'grad_w', 'grad_a_lam_re': 'grad_w', 'grad_a_lam_im': 'grad_w', 'grad_a_log_dt': 'grad_w', 'grad_a_b_re': 'grad_w', 'grad_a_b_im': 'grad_w', 'grad_a_c_re': 'grad_w', 'grad_a_c_im': 'grad_w', 'grad_a_d': 'grad_w', 'grad_a_w_glu': 'grad_w', 'grad_a_b_glu': 'grad_w', 'grad_a_w_out': 'grad_w', 'grad_b_w_in': 'grad_w', 'grad_b_sinks': 'grad_w', 'grad_b_w_out': 'grad_w', 'grad_c_w_in': 'grad_w', 'grad_c_q_norm': 'grad_w', 'grad_c_kv_norm': 'grad_w', 'grad_c_w_uq': 'grad_w', 'grad_c_w_ukv': 'grad_w', 'grad_c_w_out': 'grad_w', 'grad_d_w_in': 'grad_w', 'grad_d_ln_g': 'grad_w', 'grad_d_ln_b': 'grad_w', 'grad_d_w_s': 'grad_w', 'grad_d_b_s': 'grad_w', 'grad_d_w_out': 'grad_w', 'delta_pre_norm': 'delta_w', 'delta_post_norm': 'delta_w', 'delta_rel_bias': 'delta_w', 'delta_a_w_in': 'delta_w', 'delta_a_lam_re': 'delta_w', 'delta_a_lam_im': 'delta_w', 'delta_a_log_dt': 'delta_w', 'delta_a_b_re': 'delta_w', 'delta_a_b_im': 'delta_w', 'delta_a_c_re': 'delta_w', 'delta_a_c_im': 'delta_w', 'delta_a_d': 'delta_w', 'delta_a_w_glu': 'delta_w', 'delta_a_b_glu': 'delta_w', 'delta_a_w_out': 'delta_w', 'delta_b_w_in': 'delta_w', 'delta_b_sinks': 'delta_w', 'delta_b_w_out': 'delta_w', 'delta_c_w_in': 'delta_w', 'delta_c_q_norm': 'delta_w', 'delta_c_kv_norm': 'delta_w', 'delta_c_w_uq': 'delta_w', 'delta_c_w_ukv': 'delta_w', 'delta_c_w_out': 'delta_w', 'delta_d_w_in': 'delta_w', 'delta_d_ln_g': 'delta_w', 'delta_d_ln_b': 'delta_w', 'delta_d_w_s': 'delta_w', 'delta_d_b_s': 'delta_w', 'delta_d_w_out': 'delta_w', 'new_m_pre_norm': 'new_m', 'new_m_post_norm': 'new_m', 'new_m_rel_bias': 'new_m', 'new_m_a_w_in': 'new_m', 'new_m_a_lam_re': 'new_m', 'new_m_a_lam_im': 'new_m', 'new_m_a_log_dt': 'new_m', 'new_m_a_b_re': 'new_m', 'new_m_a_b_im': 'new_m', 'new_m_a_c_re': 'new_m', 'new_m_a_c_im': 'new_m', 'new_m_a_d': 'new_m', 'new_m_a_w_glu': 'new_m', 'new_m_a_b_glu': 'new_m', 'new_m_a_w_out': 'new_m', 'new_m_b_w_in': 'new_m', 'new_m_b_sinks': 'new_m', 'new_m_b_w_out': 'new_m', 'new_m_c_w_in': 'new_m', 'new_m_c_q_norm': 'new_m', 'new_m_c_kv_norm': 'new_m', 'new_m_c_w_uq': 'new_m', 'new_m_c_w_ukv': 'new_m', 'new_m_c_w_out': 'new_m', 'new_m_d_w_in': 'new_m', 'new_m_d_ln_g': 'new_m', 'new_m_d_ln_b': 'new_m', 'new_m_d_w_s': 'new_m', 'new_m_d_b_s': 'new_m', 'new_m_d_w_out': 'new_m', 'new_v_pre_norm': 'new_v', 'new_v_post_norm': 'new_v', 'new_v_rel_bias': 'new_v', 'new_v_a_w_in': 'new_v', 'new_v_a_lam_re': 'new_v', 'new_v_a_lam_im': 'new_v', 'new_v_a_log_dt': 'new_v', 'new_v_a_b_re': 'new_v', 'new_v_a_b_im': 'new_v', 'new_v_a_c_re': 'new_v', 'new_v_a_c_im': 'new_v', 'new_v_a_d': 'new_v', 'new_v_a_w_glu': 'new_v', 'new_v_a_b_glu': 'new_v', 'new_v_a_w_out': 'new_v', 'new_v_b_w_in': 'new_v', 'new_v_b_sinks': 'new_v', 'new_v_b_w_out': 'new_v', 'new_v_c_w_in': 'new_v', 'new_v_c_q_norm': 'new_v', 'new_v_c_kv_norm': 'new_v', 'new_v_c_w_uq': 'new_v', 'new_v_c_w_ukv': 'new_v', 'new_v_c_w_out': 'new_v', 'new_v_d_w_in': 'new_v', 'new_v_d_ln_g': 'new_v', 'new_v_d_ln_b': 'new_v', 'new_v_d_w_s': 'new_v', 'new_v_d_b_s': 'new_v', 'new_v_d_w_out': 'new_v'}


def _forward(args):
    return _fwd_reference(*[args[k] for k in FWD_PARAMS])


def _output_shape():
    out = _jax.eval_shape(lambda: _forward(_fwd_setup_inputs(0)))
    return out.shape, out.dtype

N_MICROBATCH = 1
ADAM_LR = 0.001
ADAM_B1 = 0.9
ADAM_B2 = 0.999
ADAM_EPS = 1e-08
ADAM_WD = 0.01
ADAM_STEP = 10
PER_EXAMPLE_BATCH_AXIS = {'x': 0, 'loss_target': 0}
SHARED_INPUTS = []
_WEIGHT_DTYPES = {'pre_norm': _jnp.float32, 'post_norm': _jnp.float32, 'rel_bias': _jnp.float32, 'a_w_in': _jnp.float32, 'a_lam_re': _jnp.float32, 'a_lam_im': _jnp.float32, 'a_log_dt': _jnp.float32, 'a_b_re': _jnp.float32, 'a_b_im': _jnp.float32, 'a_c_re': _jnp.float32, 'a_c_im': _jnp.float32, 'a_d': _jnp.float32, 'a_w_glu': _jnp.float32, 'a_b_glu': _jnp.float32, 'a_w_out': _jnp.float32, 'b_w_in': _jnp.float32, 'b_sinks': _jnp.float32, 'b_w_out': _jnp.float32, 'c_w_in': _jnp.float32, 'c_q_norm': _jnp.float32, 'c_kv_norm': _jnp.float32, 'c_w_uq': _jnp.float32, 'c_w_ukv': _jnp.float32, 'c_w_out': _jnp.float32, 'd_w_in': _jnp.float32, 'd_ln_g': _jnp.float32, 'd_ln_b': _jnp.float32, 'd_w_s': _jnp.float32, 'd_b_s': _jnp.float32, 'd_w_out': _jnp.float32}
MOMENT_SCALE = {'pre_norm': 1.478622e+00, 'post_norm': 1.624690e+01, 'rel_bias': 5.177320e-01, 'a_w_in': 8.277101e-01, 'a_lam_re': 5.659082e-02, 'a_lam_im': 5.140396e-02, 'a_log_dt': 3.373168e+01, 'a_b_re': 3.631427e-02, 'a_b_im': 3.605762e-02, 'a_c_re': 5.106996e-02, 'a_c_im': 5.195636e-02, 'a_d': 2.850426e+00, 'a_w_glu': 3.311276e-01, 'a_b_glu': 1.023759e+00, 'a_w_out': 2.739718e+00, 'b_w_in': 1.422478e+00, 'b_sinks': 1.764785e-01, 'b_w_out': 1.946881e+00, 'c_w_in': 1.289128e+00, 'c_q_norm': 2.717629e-01, 'c_kv_norm': 3.181578e+00, 'c_w_uq': 1.907918e-01, 'c_w_ukv': 1.223114e+00, 'c_w_out': 1.862286e+00, 'd_w_in': 4.853376e-01, 'd_ln_g': 9.706910e-02, 'd_ln_b': 8.873856e-02, 'd_w_s': 1.294403e-01, 'd_b_s': 1.865849e-01, 'd_w_out': 1.214007e+00}


def _to_microbatches(a, axis):
    t = _jnp.moveaxis(a, axis, 0)
    t = t.reshape((N_MICROBATCH, t.shape[0] // N_MICROBATCH) + t.shape[1:])
    return _jnp.moveaxis(t, 1, axis + 1)


def setup_inputs(seed: int = 0) -> dict:
    inp = _fwd_setup_inputs(seed)
    key = _jax.random.fold_in(_jax.random.key(seed), 7919)
    shape, _ = _output_shape()
    out = dict(inp)
    out["loss_target"] = _jax.random.normal(_jax.random.fold_in(key, 0), shape, _jnp.float32)
    for i, name in enumerate(TWIN_WEIGHTS):
        w = inp[name].astype(_jnp.float32)
        if MOMENT_SCALE is None:
            s = _jnp.sqrt(_jnp.mean(_jnp.square(w)) + 1e-30)
        else:
            s = MOMENT_SCALE[name]
        km, kv = _jax.random.split(_jax.random.fold_in(key, i + 1))
        out[name] = w
        out["m_" + name] = s * _jax.random.normal(km, w.shape, _jnp.float32)
        out["v_" + name] = (s * s) * _jax.random.uniform(kv, w.shape, _jnp.float32, 0.5, 1.5)
    if N_MICROBATCH > 1:
        for name, axis in PER_EXAMPLE_BATCH_AXIS.items():
            out[name] = _to_microbatches(out[name], axis)
    return {'x': out['x'], 'pre_norm': out['pre_norm'], 'post_norm': out['post_norm'], 'rel_bias': out['rel_bias'], 'a_w_in': out['a_w_in'], 'a_lam_re': out['a_lam_re'], 'a_lam_im': out['a_lam_im'], 'a_log_dt': out['a_log_dt'], 'a_b_re': out['a_b_re'], 'a_b_im': out['a_b_im'], 'a_c_re': out['a_c_re'], 'a_c_im': out['a_c_im'], 'a_d': out['a_d'], 'a_w_glu': out['a_w_glu'], 'a_b_glu': out['a_b_glu'], 'a_w_out': out['a_w_out'], 'b_w_in': out['b_w_in'], 'b_sinks': out['b_sinks'], 'b_w_out': out['b_w_out'], 'c_w_in': out['c_w_in'], 'c_q_norm': out['c_q_norm'], 'c_kv_norm': out['c_kv_norm'], 'c_w_uq': out['c_w_uq'], 'c_w_ukv': out['c_w_ukv'], 'c_w_out': out['c_w_out'], 'd_w_in': out['d_w_in'], 'd_ln_g': out['d_ln_g'], 'd_ln_b': out['d_ln_b'], 'd_w_s': out['d_w_s'], 'd_b_s': out['d_b_s'], 'd_w_out': out['d_w_out'], 'loss_target': out['loss_target'], 'm_pre_norm': out['m_pre_norm'], 'm_post_norm': out['m_post_norm'], 'm_rel_bias': out['m_rel_bias'], 'm_a_w_in': out['m_a_w_in'], 'm_a_lam_re': out['m_a_lam_re'], 'm_a_lam_im': out['m_a_lam_im'], 'm_a_log_dt': out['m_a_log_dt'], 'm_a_b_re': out['m_a_b_re'], 'm_a_b_im': out['m_a_b_im'], 'm_a_c_re': out['m_a_c_re'], 'm_a_c_im': out['m_a_c_im'], 'm_a_d': out['m_a_d'], 'm_a_w_glu': out['m_a_w_glu'], 'm_a_b_glu': out['m_a_b_glu'], 'm_a_w_out': out['m_a_w_out'], 'm_b_w_in': out['m_b_w_in'], 'm_b_sinks': out['m_b_sinks'], 'm_b_w_out': out['m_b_w_out'], 'm_c_w_in': out['m_c_w_in'], 'm_c_q_norm': out['m_c_q_norm'], 'm_c_kv_norm': out['m_c_kv_norm'], 'm_c_w_uq': out['m_c_w_uq'], 'm_c_w_ukv': out['m_c_w_ukv'], 'm_c_w_out': out['m_c_w_out'], 'm_d_w_in': out['m_d_w_in'], 'm_d_ln_g': out['m_d_ln_g'], 'm_d_ln_b': out['m_d_ln_b'], 'm_d_w_s': out['m_d_w_s'], 'm_d_b_s': out['m_d_b_s'], 'm_d_w_out': out['m_d_w_out'], 'v_pre_norm': out['v_pre_norm'], 'v_post_norm': out['v_post_norm'], 'v_rel_bias': out['v_rel_bias'], 'v_a_w_in': out['v_a_w_in'], 'v_a_lam_re': out['v_a_lam_re'], 'v_a_lam_im': out['v_a_lam_im'], 'v_a_log_dt': out['v_a_log_dt'], 'v_a_b_re': out['v_a_b_re'], 'v_a_b_im': out['v_a_b_im'], 'v_a_c_re': out['v_a_c_re'], 'v_a_c_im': out['v_a_c_im'], 'v_a_d': out['v_a_d'], 'v_a_w_glu': out['v_a_w_glu'], 'v_a_b_glu': out['v_a_b_glu'], 'v_a_w_out': out['v_a_w_out'], 'v_b_w_in': out['v_b_w_in'], 'v_b_sinks': out['v_b_sinks'], 'v_b_w_out': out['v_b_w_out'], 'v_c_w_in': out['v_c_w_in'], 'v_c_q_norm': out['v_c_q_norm'], 'v_c_kv_norm': out['v_c_kv_norm'], 'v_c_w_uq': out['v_c_w_uq'], 'v_c_w_ukv': out['v_c_w_ukv'], 'v_c_w_out': out['v_c_w_out'], 'v_d_w_in': out['v_d_w_in'], 'v_d_ln_g': out['v_d_ln_g'], 'v_d_ln_b': out['v_d_ln_b'], 'v_d_w_s': out['v_d_w_s'], 'v_d_b_s': out['v_d_b_s'], 'v_d_w_out': out['v_d_w_out']}


def _loss(weights, diff, rest, loss_target):
    with _jax.named_scope("forward"):
        args = {**rest, TWIN_DIFF_INPUT: diff, **{k: w.astype(_WEIGHT_DTYPES[k]) for k, w in weights.items()}}
        y = _forward(args)
    with _jax.named_scope("loss_head"):
        err = _jnp.square(y.astype(_jnp.float32) - loss_target)
        return 0.5 * _jnp.sum(_jnp.mean(err, axis=-1)) if err.ndim else 0.5 * err


def _adamw(w, g, m, v):
    m = ADAM_B1 * m + (1.0 - ADAM_B1) * g
    v = ADAM_B2 * v + (1.0 - ADAM_B2) * _jnp.square(g)
    m_hat = m / (1.0 - ADAM_B1 ** ADAM_STEP)
    v_hat = v / (1.0 - ADAM_B2 ** ADAM_STEP)
    delta = -ADAM_LR * (m_hat / (_jnp.sqrt(v_hat) + ADAM_EPS) + ADAM_WD * w)
    return delta, m, v


def reference(x, pre_norm, post_norm, rel_bias, a_w_in, a_lam_re, a_lam_im, a_log_dt, a_b_re, a_b_im, a_c_re, a_c_im, a_d, a_w_glu, a_b_glu, a_w_out, b_w_in, b_sinks, b_w_out, c_w_in, c_q_norm, c_kv_norm, c_w_uq, c_w_ukv, c_w_out, d_w_in, d_ln_g, d_ln_b, d_w_s, d_b_s, d_w_out, loss_target, m_pre_norm, m_post_norm, m_rel_bias, m_a_w_in, m_a_lam_re, m_a_lam_im, m_a_log_dt, m_a_b_re, m_a_b_im, m_a_c_re, m_a_c_im, m_a_d, m_a_w_glu, m_a_b_glu, m_a_w_out, m_b_w_in, m_b_sinks, m_b_w_out, m_c_w_in, m_c_q_norm, m_c_kv_norm, m_c_w_uq, m_c_w_ukv, m_c_w_out, m_d_w_in, m_d_ln_g, m_d_ln_b, m_d_w_s, m_d_b_s, m_d_w_out, v_pre_norm, v_post_norm, v_rel_bias, v_a_w_in, v_a_lam_re, v_a_lam_im, v_a_log_dt, v_a_b_re, v_a_b_im, v_a_c_re, v_a_c_im, v_a_d, v_a_w_glu, v_a_b_glu, v_a_w_out, v_b_w_in, v_b_sinks, v_b_w_out, v_c_w_in, v_c_q_norm, v_c_kv_norm, v_c_w_uq, v_c_w_ukv, v_c_w_out, v_d_w_in, v_d_ln_g, v_d_ln_b, v_d_w_s, v_d_b_s, v_d_w_out):
    given = dict(x=x, pre_norm=pre_norm, post_norm=post_norm, rel_bias=rel_bias, a_w_in=a_w_in, a_lam_re=a_lam_re, a_lam_im=a_lam_im, a_log_dt=a_log_dt, a_b_re=a_b_re, a_b_im=a_b_im, a_c_re=a_c_re, a_c_im=a_c_im, a_d=a_d, a_w_glu=a_w_glu, a_b_glu=a_b_glu, a_w_out=a_w_out, b_w_in=b_w_in, b_sinks=b_sinks, b_w_out=b_w_out, c_w_in=c_w_in, c_q_norm=c_q_norm, c_kv_norm=c_kv_norm, c_w_uq=c_w_uq, c_w_ukv=c_w_ukv, c_w_out=c_w_out, d_w_in=d_w_in, d_ln_g=d_ln_g, d_ln_b=d_ln_b, d_w_s=d_w_s, d_b_s=d_b_s, d_w_out=d_w_out, loss_target=loss_target, m_pre_norm=m_pre_norm, m_post_norm=m_post_norm, m_rel_bias=m_rel_bias, m_a_w_in=m_a_w_in, m_a_lam_re=m_a_lam_re, m_a_lam_im=m_a_lam_im, m_a_log_dt=m_a_log_dt, m_a_b_re=m_a_b_re, m_a_b_im=m_a_b_im, m_a_c_re=m_a_c_re, m_a_c_im=m_a_c_im, m_a_d=m_a_d, m_a_w_glu=m_a_w_glu, m_a_b_glu=m_a_b_glu, m_a_w_out=m_a_w_out, m_b_w_in=m_b_w_in, m_b_sinks=m_b_sinks, m_b_w_out=m_b_w_out, m_c_w_in=m_c_w_in, m_c_q_norm=m_c_q_norm, m_c_kv_norm=m_c_kv_norm, m_c_w_uq=m_c_w_uq, m_c_w_ukv=m_c_w_ukv, m_c_w_out=m_c_w_out, m_d_w_in=m_d_w_in, m_d_ln_g=m_d_ln_g, m_d_ln_b=m_d_ln_b, m_d_w_s=m_d_w_s, m_d_b_s=m_d_b_s, m_d_w_out=m_d_w_out, v_pre_norm=v_pre_norm, v_post_norm=v_post_norm, v_rel_bias=v_rel_bias, v_a_w_in=v_a_w_in, v_a_lam_re=v_a_lam_re, v_a_lam_im=v_a_lam_im, v_a_log_dt=v_a_log_dt, v_a_b_re=v_a_b_re, v_a_b_im=v_a_b_im, v_a_c_re=v_a_c_re, v_a_c_im=v_a_c_im, v_a_d=v_a_d, v_a_w_glu=v_a_w_glu, v_a_b_glu=v_a_b_glu, v_a_w_out=v_a_w_out, v_b_w_in=v_b_w_in, v_b_sinks=v_b_sinks, v_b_w_out=v_b_w_out, v_c_w_in=v_c_w_in, v_c_q_norm=v_c_q_norm, v_c_kv_norm=v_c_kv_norm, v_c_w_uq=v_c_w_uq, v_c_w_ukv=v_c_w_ukv, v_c_w_out=v_c_w_out, v_d_w_in=v_d_w_in, v_d_ln_g=v_d_ln_g, v_d_ln_b=v_d_ln_b, v_d_w_s=v_d_w_s, v_d_b_s=v_d_b_s, v_d_w_out=v_d_w_out)
    weights = {n: given[n] for n in TWIN_WEIGHTS}
    shared = {n: given[n] for n in SHARED_INPUTS}
    per_example = {n: given[n] for n in ['x']}
    grad_fn = _jax.value_and_grad(_loss, argnums=(0, 1))

    def one_microbatch(ex, loss_target):
        ex = dict(ex)
        diff = ex.pop(TWIN_DIFF_INPUT)
        return grad_fn(weights, diff, {**shared, **ex}, loss_target)

    if N_MICROBATCH == 1:
        loss, (grad_w, grad_x) = one_microbatch(per_example, given["loss_target"])
    else:
        def body(carry, xs):
            loss_sum, grad_sum = carry
            l_k, (gw_k, gx_k) = one_microbatch(xs[0], xs[1])
            with _jax.named_scope("update"):
                return (loss_sum + l_k, _jax.tree.map(_jnp.add, grad_sum, gw_k)), gx_k

        init = (_jnp.zeros((), _jnp.float32), _jax.tree.map(_jnp.zeros_like, weights))
        (loss, grad_w), grad_x = _jax.lax.scan(body, init, (per_example, given["loss_target"]))
    with _jax.named_scope("update"):
        delta_w, new_m, new_v = {}, {}, {}
        for n in TWIN_WEIGHTS:
            delta_w[n], new_m[n], new_v[n] = _adamw(weights[n], grad_w[n], given["m_" + n], given["v_" + n])
    return (loss, grad_x, *[grad_w[n] for n in TWIN_WEIGHTS], *[delta_w[n] for n in TWIN_WEIGHTS],
            *[new_m[n] for n in TWIN_WEIGHTS], *[new_v[n] for n in TWIN_WEIGHTS])
```

```python
import functools
import math

import numpy as np
import jax
import jax.numpy as jnp
from jax import lax
from jax.experimental import pallas as pl
from jax.experimental.pallas import tpu as pltpu

F32 = jnp.float32
BF16 = jnp.bfloat16
MESH = pl.DeviceIdType.MESH
ANY = pl.BlockSpec(memory_space=pl.ANY)

N_DEV = 8
D_MODEL = 1024
EPS = 1e-6
NEG_INF = -1e30
SSM_G, SSM_P, SSM_H = 64, 64, 16
SSM_T = 256
SSM_WC = 512
HEAD_DIM = 64
SWA_HEADS, SWA_KV = 16, 2
WINDOW = 128
REL_BUCKETS, REL_MAX_DIST = 32, 128
MLA_HEADS, MLA_NOPE, MLA_ROPE, MLA_V = 16, 64, 32, 64
MLA_Q_RANK, MLA_KV_RANK = 768, 256
ROPE_BASE = 10000.0
QB = 128
SGU_G, SGU_C, SGU_T = 16, 64, 128
ADAM_LR, ADAM_B1, ADAM_B2, ADAM_EPS, ADAM_WD, ADAM_STEP = 0.001, 0.9, 0.999, 1e-08, 0.01, 10
PACK_C = 512

WEIGHTS = ['pre_norm', 'post_norm', 'rel_bias', 'a_w_in', 'a_lam_re', 'a_lam_im', 'a_log_dt', 'a_b_re', 'a_b_im',
           'a_c_re', 'a_c_im', 'a_d', 'a_w_glu', 'a_b_glu', 'a_w_out', 'b_w_in', 'b_sinks', 'b_w_out', 'c_w_in',
           'c_q_norm', 'c_kv_norm', 'c_w_uq', 'c_w_ukv', 'c_w_out', 'd_w_in', 'd_ln_g', 'd_ln_b', 'd_w_s', 'd_b_s',
           'd_w_out']
SHARDED = {'a_w_in': ((1024, 2048), 1), 'a_w_glu': ((1024, 1024), 0), 'a_w_out': ((1024, 1024), 0),
           'b_w_in': ((1024, 2304), 1), 'b_w_out': ((1024, 1024), 0), 'c_w_in': ((1024, 2080), 1),
           'c_q_norm': ((1, 768), 1), 'c_kv_norm': ((1, 256), 1), 'c_w_uq': ((768, 1536), 1),
           'c_w_ukv': ((256, 2048), 1), 'c_w_out': ((1024, 1024), 0), 'd_w_in': ((1024, 3072), 1),
           'd_ln_g': ((1, 1024), 1), 'd_ln_b': ((1, 1024), 1), 'd_w_out': ((1024, 1024), 0)}
SHARDED_F32 = ['c_q_norm', 'c_kv_norm', 'd_ln_g', 'd_ln_b']
REPLICATED = [n for n in WEIGHTS if n not in SHARDED]


def _cdiv(a, b):
    return -(-a // b)


def _block_shape(name):
    (r, c), ax = SHARDED[name]
    return (r // N_DEV, c) if ax == 0 else (r, c // N_DEV)


SH_OFF = {}
_o = 0
for _n in SHARDED:
    SH_OFF[_n] = _o
    _o += int(np.prod(_block_shape(_n)))
SH_ROWS = _cdiv(_o, PACK_C * 16) * 16
SMALL_OFF = {}
_o = 0
for _n in SHARDED_F32:
    SMALL_OFF[_n] = _o
    _o += int(np.prod(_block_shape(_n)))
SMALL_ROWS = _cdiv(_o, 128 * 8) * 8


def _pick(n, cands):
    for c in cands:
        if n % c == 0:
            return c
    return n


def mm(a, b, mode, name, out_dtype=F32):
    if mode == 'nn':
        (M, K), (K2, N) = a.shape, b.shape
    elif mode == 'nt':
        (M, K), (N, K2) = a.shape, b.shape
    else:
        (K, M), (K2, N) = a.shape, b.shape
    assert K == K2, (name, a.shape, b.shape)
    tm = _pick(M, (512, 256, 128))
    tn = _pick(N, (512, 384, 256, 128))
    dims = {'nn': ((1,), (0,)), 'nt': ((1,), (1,)), 'tn': ((0,), (0,))}[mode]

    def body(a_ref, b_ref, o_ref):
        o_ref[...] = lax.dot_general(a_ref[...].astype(BF16), b_ref[...].astype(BF16), (dims, ((), ())),
                                     preferred_element_type=F32).astype(out_dtype)

    a_spec = pl.BlockSpec((K, tm), lambda i, j: (0, i)) if mode == 'tn' else pl.BlockSpec((tm, K), lambda i, j: (i, 0))
    b_spec = pl.BlockSpec((tn, K), lambda i, j: (j, 0)) if mode == 'nt' else pl.BlockSpec((K, tn), lambda i, j: (0, j))
    return pl.pallas_call(
        body, grid=(M // tm, N // tn), in_specs=[a_spec, b_spec],
        out_specs=pl.BlockSpec((tm, tn), lambda i, j: (i, j)), out_shape=jax.ShapeDtypeStruct((M, N), out_dtype),
        compiler_params=pltpu.CompilerParams(dimension_semantics=("parallel", "parallel")), name=name)(a, b)


def rw(arr, width=None, cb=0):
    return (arr, arr.shape[1] if width is None else width, cb)


def rowwise(fn, rows, consts, outs, accs, tl, name, n_steps=None):
    if n_steps is None:
        n_steps = [r[0].shape[0] for r in rows if not isinstance(r[1], pl.BlockSpec)][0] // tl
    L = n_steps * tl
    nr, nc, no, na = len(rows), len(consts), len(outs), len(accs)
    in_specs, args = [], []
    for r in rows:
        if isinstance(r[1], pl.BlockSpec):
            in_specs.append(r[1])
        else:
            in_specs.append(pl.BlockSpec((tl, r[1]), functools.partial(lambda i, cb: (i, cb), cb=r[2])))
        args.append(r[0])
    for c in consts:
        in_specs.append(pl.BlockSpec(c.shape, functools.partial(lambda i, nd: (0,) * nd, nd=c.ndim)))
        args.append(c)
    out_specs = [pl.BlockSpec((tl, w), lambda i: (i, 0)) for w, _ in outs]
    out_shape = [jax.ShapeDtypeStruct((L, w), dt) for w, dt in outs]
    for s in accs:
        out_specs.append(pl.BlockSpec(s, functools.partial(lambda i, nd: (0,) * nd, nd=len(s))))
        out_shape.append(jax.ShapeDtypeStruct(s, F32))

    def body(*refs):
        ins = [r[...] for r in refs[:nr + nc]]
        o_refs = refs[nr + nc:nr + nc + no]
        a_refs = refs[nr + nc + no:]
        o_vals, a_vals = fn(*ins)
        for ref, val in zip(o_refs, o_vals):
            ref[...] = val.astype(ref.dtype)
        if na:
            @pl.when(pl.program_id(0) == 0)
            def _():
                for ref in a_refs:
                    ref[...] = jnp.zeros_like(ref)
            for ref, val in zip(a_refs, a_vals):
                ref[...] += val

    res = pl.pallas_call(
        body, grid=(n_steps,), in_specs=in_specs, out_specs=out_specs, out_shape=out_shape,
        compiler_params=pltpu.CompilerParams(dimension_semantics=("arbitrary",)), name=name)(*args)
    return res[:no], res[no:]


def tmm(a, b, mode, name, tl=512, wa=None, wb=None):
    L = a.shape[0]
    tl = min(tl, L)
    nt = 8
    if mode == 'tn':
        def body(a_ref, b_ref, o_ref):
            @pl.when(pl.program_id(1) == 0)
            def _():
                o_ref[...] = jnp.zeros_like(o_ref)
            o_ref[0] += lax.dot_general(a_ref[...].astype(BF16), b_ref[...].astype(BF16), (((0,), (0,)), ((), ())),
                                        preferred_element_type=F32)

        return pl.pallas_call(
            body, grid=(nt, L // tl),
            in_specs=[pl.BlockSpec((tl, wa), lambda k, i: (i, k)), pl.BlockSpec((tl, wb), lambda k, i: (i, k))],
            out_specs=pl.BlockSpec((1, wa, wb), lambda k, i: (k, 0, 0)),
            out_shape=jax.ShapeDtypeStruct((nt, wa, wb), F32),
            compiler_params=pltpu.CompilerParams(dimension_semantics=("parallel", "arbitrary")), name=name)(a, b)
    assert b.shape[0] == nt
    wa = b.shape[1] if mode == 'nn' else b.shape[2]
    wo = b.shape[2] if mode == 'nn' else b.shape[1]
    dims = ((1,), (0,)) if mode == 'nn' else ((1,), (1,))

    def body(a_ref, b_ref, o_ref):
        o_ref[...] = lax.dot_general(a_ref[...].astype(BF16), b_ref[0].astype(BF16), (dims, ((), ())),
                                     preferred_element_type=F32)

    return pl.pallas_call(
        body, grid=(L // tl, nt),
        in_specs=[pl.BlockSpec((tl, wa), lambda i, k: (i, k)), pl.BlockSpec((1,) + b.shape[1:], lambda i, k: (k, 0, 0))],
        out_specs=pl.BlockSpec((tl, wo), lambda i, k: (i, k)), out_shape=jax.ShapeDtypeStruct((L, nt * wo), F32),
        compiler_params=pltpu.CompilerParams(dimension_semantics=("parallel", "parallel")), name=name)(a, b)


_K0 = math.sqrt(2.0 / math.pi)
_K1 = 0.044715


def gelu(x):
    return x * (0.5 * (1.0 + jnp.tanh(_K0 * (x + _K1 * (x * x * x)))))


def gelu_grad(x):
    t = jnp.tanh(_K0 * (x + _K1 * (x * x * x)))
    return 0.5 * (1.0 + t) + 0.5 * x * (1.0 - t * t) * (_K0 * (1.0 + 3.0 * _K1 * x * x))


def sigmoid(x):
    return 1.0 / (1.0 + jnp.exp(-x))


def silu(z):
    return z * sigmoid(z)


def silu_grad(z):
    s = sigmoid(z)
    return s * (1.0 + z * (1.0 - s))


def rms_fwd(x, g):
    r = lax.rsqrt(jnp.mean(x * x, axis=-1, keepdims=True) + EPS)
    return x * r * g


def rms_bwd(x, g, dy):
    r = lax.rsqrt(jnp.mean(x * x, axis=-1, keepdims=True) + EPS)
    xh = x * r
    dg = jnp.sum(dy * xh, axis=0, keepdims=True)
    dxh = dy * g
    dx = r * (dxh - xh * jnp.mean(dxh * xh, axis=-1, keepdims=True))
    return dx, dg


def s5_scan(xr, xi, pr, pi, name, reverse=False, s_re=None, s_im=None):
    L, W = xr.shape
    T, WC = min(SSM_T, L), SSM_WC
    nT = L // T
    with_da = s_re is not None
    steps = [1 << k for k in range(int(math.log2(T)))]

    def body(*refs):
        if with_da:
            xr_ref, xi_ref, pr_ref, pi_ref, sr_ref, si_ref, spr_ref, spi_ref, or_ref, oi_ref, dar_ref, dai_ref, cr, ci = refs
        else:
            xr_ref, xi_ref, pr_ref, pi_ref, or_ref, oi_ref, cr, ci = refs
        i = pl.program_id(1)

        @pl.when(i == 0)
        def _():
            cr[...] = jnp.zeros_like(cr)
            ci[...] = jnp.zeros_like(ci)
            if with_da:
                dar_ref[...] = jnp.zeros_like(dar_ref)
                dai_ref[...] = jnp.zeros_like(dai_ref)

        a_r = xr_ref[...]
        a_i = xi_ref[...]
        row = lax.broadcasted_iota(jnp.int32, (T, WC), 0)
        sgn = -1.0 if reverse else 1.0
        for d in steps:
            wr = pr_ref[(T - d) if reverse else (d - 1):(T - d + 1) if reverse else d, :]
            wi = sgn * pi_ref[(T - d) if reverse else (d - 1):(T - d + 1) if reverse else d, :]
            if reverse:
                yr, yi, keep = pltpu.roll(a_r, T - d, 0), pltpu.roll(a_i, T - d, 0), row < T - d
            else:
                yr, yi, keep = pltpu.roll(a_r, d, 0), pltpu.roll(a_i, d, 0), row >= d
            a_r, a_i = (a_r + jnp.where(keep, wr * yr - wi * yi, 0.0), a_i + jnp.where(keep, wr * yi + wi * yr, 0.0))
        wr = pr_ref[...]
        wi = sgn * pi_ref[...]
        c_r, c_i = cr[...], ci[...]
        a_r, a_i = a_r + (wr * c_r - wi * c_i), a_i + (wr * c_i + wi * c_r)
        or_ref[...] = a_r
        oi_ref[...] = a_i
        if reverse:
            cr[...] = a_r[0:1, :]
            ci[...] = a_i[0:1, :]
        else:
            cr[...] = a_r[T - 1:T, :]
            ci[...] = a_i[T - 1:T, :]
        if with_da:
            first = (nT - 1 - i) == 0
            pv_r = jnp.where(first, 0.0, spr_ref[7:8, :])
            pv_i = jnp.where(first, 0.0, spi_ref[7:8, :])
            sp_r = jnp.where(row == 0, pv_r, pltpu.roll(sr_ref[...], 1, 0))
            sp_i = jnp.where(row == 0, pv_i, pltpu.roll(si_ref[...], 1, 0))
            dar_ref[...] += jnp.sum(a_r * sp_r + a_i * sp_i, axis=0, keepdims=True)
            dai_ref[...] += jnp.sum(a_i * sp_r - a_r * sp_i, axis=0, keepdims=True)

    if reverse:
        xmap = lambda j, i: (nT - 1 - i, j)
        pmap = lambda j, i: (jnp.maximum((nT - 1 - i) * (T // 8) - 1, 0), j)
    else:
        xmap = lambda j, i: (i, j)
    xspec = pl.BlockSpec((T, WC), xmap)
    pspec = pl.BlockSpec((T, WC), lambda j, i: (0, j))
    in_specs = [xspec, xspec, pspec, pspec]
    args = [xr, xi, pr, pi]
    out_specs = [xspec, xspec]
    out_shape = [jax.ShapeDtypeStruct((L, W), F32)] * 2
    if with_da:
        in_specs += [xspec, xspec, pl.BlockSpec((8, WC), pmap), pl.BlockSpec((8, WC), pmap)]
        args += [s_re, s_im, s_re, s_im]
        out_specs += [pl.BlockSpec((1, WC), lambda j, i: (0, j))] * 2
        out_shape += [jax.ShapeDtypeStruct((1, W), F32)] * 2
    return pl.pallas_call(
        body, grid=(W // WC, nT), in_specs=in_specs, out_specs=out_specs, out_shape=out_shape,
        scratch_shapes=[pltpu.VMEM((1, WC), F32), pltpu.VMEM((1, WC), F32)],
        compiler_params=pltpu.CompilerParams(dimension_semantics=("parallel", "arbitrary")), name=name)(*args)


def s5_discretize(lam_re, lam_im, log_dt, b_re, b_im):
    dt = jnp.exp(log_dt)[:, None]
    mag = jnp.exp(lam_re * dt)
    ab_re = mag * jnp.cos(lam_im * dt)
    ab_im = mag * jnp.sin(lam_im * dt)
    den = lam_re * lam_re + lam_im * lam_im
    nr = ab_re - 1.0
    f_re = (nr * lam_re + ab_im * lam_im) / den
    f_im = (ab_im * lam_re - nr * lam_im) / den
    bb_re = f_re[..., None] * b_re - f_im[..., None] * b_im
    bb_im = f_re[..., None] * b_im + f_im[..., None] * b_re
    return ab_re, ab_im, bb_re, bb_im


_EYE8 = np.eye(8, dtype=np.float32)


def _b_tiles(bb):
    t = bb.transpose(0, 2, 1).reshape(8, 8, SSM_H, SSM_P)
    return jnp.einsum('kghp,gG->kghGp', t, _EYE8).reshape(8, 8 * SSM_H, 8 * SSM_P)


def _b_untile(d):
    t = jnp.einsum('kghGp,gG->kghp', d.reshape(8, 8, SSM_H, 8, SSM_P), _EYE8)
    return t.reshape(SSM_G, SSM_H, SSM_P).transpose(0, 2, 1)


def _c_tiles(c):
    t = c.transpose(0, 2, 1).reshape(8, 8, SSM_P, SSM_H)
    return jnp.einsum('kgph,gG->kgpGh', t, _EYE8).reshape(8, 8 * SSM_P, 8 * SSM_H)


def _c_untile(d):
    t = jnp.einsum('kgpGh,gG->kgph', d.reshape(8, 8, SSM_P, 8, SSM_H), _EYE8)
    return t.reshape(SSM_G, SSM_P, SSM_H).transpose(0, 2, 1)


def _powers(ar, ai, T):
    pr, pi = ar, ai
    while pr.shape[0] < T:
        lr, li = pr[-1:], pi[-1:]
        pr, pi = (jnp.concatenate([pr, pr * lr - pi * li], 0), jnp.concatenate([pi, pr * li + pi * lr], 0))
    return pr, pi


def layer_a_fwd(h, w, p):
    L = h.shape[0]
    proj = mm(h, w['a_w_in'], 'nn', 'a_proj')
    disc = lambda *a: s5_discretize(*a)
    (ab_re, ab_im, bb_re, bb_im), disc_vjp = jax.vjp(disc, p['a_lam_re'][0], p['a_lam_im'][0], p['a_log_dt'][0],
                                                     p['a_b_re'][0], p['a_b_im'][0])
    Bre, Bim = _b_tiles(bb_re), _b_tiles(bb_im)
    Cre, Cim = _c_tiles(p['a_c_re'][0]), -_c_tiles(p['a_c_im'][0])
    T = min(SSM_T, L)
    pr, pi = _powers(ab_re.reshape(1, -1), ab_im.reshape(1, -1), T)
    bu_re = tmm(proj, Bre, 'nn', 'a_bu_re')
    bu_im = tmm(proj, Bim, 'nn', 'a_bu_im')
    s_re, s_im = s5_scan(bu_re, bu_im, pr, pi, 'a_scan')
    y_re = tmm(s_re, Cre, 'nn', 'a_y_re')
    y_im = tmm(s_im, Cim, 'nn', 'a_y_im')

    def f1(u, yre, yim, dsk):
        y = yre + yim + dsk * u
        return [y, gelu(y)], []
    (y, yg), _ = rowwise(f1, [rw(proj, 1024, 0), rw(y_re), rw(y_im)], [p['a_d']], [(1024, F32)] * 2, [], 256, 'a_gelu')
    gl = mm(yg, w['a_w_glu'], 'nn', 'a_glu')

    def f2(yg_, gl_, z, bg):
        return [yg_ * sigmoid(gl_ + bg) * silu(z)], []
    (po,), _ = rowwise(f2, [rw(yg), rw(gl), rw(proj, 1024, 1)], [p['a_b_glu']], [(1024, F32)], [], 256, 'a_gate')
    yb = mm(po, w['a_w_out'], 'nn', 'a_out')
    saved = dict(h=h, proj=proj, disc_vjp=disc_vjp, Bre=Bre, Bim=Bim, Cre=Cre, Cim=Cim, pr=pr, pi=pi, s_re=s_re,
                 s_im=s_im, y=y, yg=yg, gl=gl, po=po)
    return yb, saved


def layer_a_bwd(dyb, w, p, sv):
    g = {}
    dpo = mm(dyb, w['a_w_out'], 'nt', 'a_dpo')
    g['a_w_out'] = mm(sv['po'], dyb, 'tn', 'a_dwout')
    proj = sv['proj']

    def f1(dpo_, yg, gl, z, bg):
        sg = sigmoid(gl + bg)
        sz = silu(z)
        dm = dpo_ * sz
        dz = dpo_ * (yg * sg) * silu_grad(z)
        dgl = dm * yg * sg * (1.0 - sg)
        return [dz, dm * sg, dgl], [jnp.sum(dgl, axis=0, keepdims=True)]
    (dz, dyg1, dgl), (db_glu,) = rowwise(f1, [rw(dpo), rw(sv['yg']), rw(sv['gl']), rw(proj, 1024, 1)], [p['a_b_glu']],
                                          [(1024, F32)] * 3, [(1, 1024)], 256, 'a_gate_bwd')
    g['a_b_glu'] = db_glu
    g['a_w_glu'] = mm(sv['yg'], dgl, 'tn', 'a_dwglu')
    dyg2 = mm(dgl, w['a_w_glu'], 'nt', 'a_dyg2')

    def f2(dyg1_, dyg2_, y, u, dsk):
        dy = (dyg1_ + dyg2_) * gelu_grad(y)
        return [dy, dy * dsk], [jnp.sum(dy * u, axis=0, keepdims=True)]
    (dy, du1), (dd,) = rowwise(f2, [rw(dyg1), rw(dyg2), rw(sv['y']), rw(proj, 1024, 0)], [p['a_d']],
                               [(1024, F32)] * 2, [(1, 1024)], 256, 'a_gelu_bwd')
    g['a_d'] = dd
    ds_re = tmm(dy, sv['Cre'], 'nt', 'a_ds_re')
    ds_im = tmm(dy, sv['Cim'], 'nt', 'a_ds_im')
    dCre = tmm(sv['s_re'], dy, 'tn', 'a_dcre', wa=512, wb=128)
    dCim = -tmm(sv['s_im'], dy, 'tn', 'a_dcim', wa=512, wb=128)
    g_re, g_im, da_re, da_im = s5_scan(ds_re, ds_im, sv['pr'][::-1], sv['pi'][::-1], 'a_scan_rev', reverse=True,
                                       s_re=sv['s_re'], s_im=sv['s_im'])
    dBre = tmm(proj, g_re, 'tn', 'a_dbre', wa=128, wb=512)
    dBim = tmm(proj, g_im, 'tn', 'a_dbim', wa=128, wb=512)
    du2a = tmm(g_re, sv['Bre'], 'nt', 'a_du_re')
    du2b = tmm(g_im, sv['Bim'], 'nt', 'a_du_im')

    def f3(a, b, c, dz_):
        return [jnp.concatenate([a + b + c, dz_], axis=1)], []
    (dproj,), _ = rowwise(f3, [rw(du1), rw(du2a), rw(du2b), rw(dz)], [], [(2048, F32)], [], 256, 'a_dproj')
    dlr, dli, dldt, dbr, dbi = sv['disc_vjp']((da_re.reshape(SSM_G, SSM_P), da_im.reshape(SSM_G, SSM_P),
                                               _b_untile(dBre), _b_untile(dBim)))
    g['a_lam_re'], g['a_lam_im'], g['a_log_dt'] = dlr[None], dli[None], dldt[None]
    g['a_b_re'], g['a_b_im'] = dbr[None], dbi[None]
    g['a_c_re'], g['a_c_im'] = _c_untile(dCre)[None], _c_untile(dCim)[None]
    g['a_w_in'] = mm(sv['h'], dproj, 'tn', 'a_dwin')
    dh = mm(dproj, w['a_w_in'], 'nt', 'a_dh')
    return dh, g


def _t5_bucket_np():
    qi = np.arange(WINDOW)[:, None]
    kj = np.arange(2 * WINDOW)[None, :]
    dist = np.maximum(qi + WINDOW - kj, 0)
    max_exact = REL_BUCKETS // 2
    dist_f = np.maximum(dist, 1).astype(np.float32)
    large = max_exact + (np.log(dist_f / np.float32(max_exact)) / np.float32(math.log(REL_MAX_DIST / max_exact))
                         * np.float32(REL_BUCKETS - max_exact)).astype(np.int32)
    large = np.minimum(large, REL_BUCKETS - 1)
    return np.where(dist < max_exact, dist, large).astype(np.int32)


def _swa_probs(q, kb, bias_h, sink, valid):
    s = lax.dot_general(q, kb, (((1,), (1,)), ((), ())), preferred_element_type=F32) * (HEAD_DIM ** -0.5)
    s = jnp.where(valid, s + bias_h, NEG_INF)
    m = jnp.maximum(jnp.max(s, axis=-1, keepdims=True), sink)
    e = jnp.exp(s - m)
    es = jnp.exp(sink - m)
    den = jnp.sum(e, axis=-1, keepdims=True) + es
    return e / den, es / den


def _swa_valid(n):
    qi = lax.broadcasted_iota(jnp.int32, (WINDOW, 2 * WINDOW), 0)
    kj = lax.broadcasted_iota(jnp.int32, (WINDOW, 2 * WINDOW), 1)
    dist = qi + WINDOW - kj
    return (dist >= 0) & (dist < WINDOW) & ((kj >= WINDOW) | (n > 0))


def swa_fwd(proj, bias, sinks):
    L = proj.shape[0]

    def body(z_ref, q_ref, kvc_ref, kvp_ref, bias_ref, sink_ref, o_ref, po_ref):
        n = pl.program_id(0)
        valid = _swa_valid(n)
        q, kvc, kvp = q_ref[...], kvc_ref[...], kvp_ref[...]
        outs = []
        for kvh in range(SWA_KV):
            kb = jnp.concatenate([kvp[:, kvh * 64:(kvh + 1) * 64], kvc[:, kvh * 64:(kvh + 1) * 64]], 0).astype(BF16)
            vb = jnp.concatenate([kvp[:, 128 + kvh * 64:128 + (kvh + 1) * 64],
                                  kvc[:, 128 + kvh * 64:128 + (kvh + 1) * 64]], 0).astype(BF16)
            for gi in range(SWA_HEADS // SWA_KV):
                h = kvh * 8 + gi
                p, _ = _swa_probs(q[:, h * 64:(h + 1) * 64].astype(BF16), kb, bias_ref[h], sink_ref[0:1, h:h + 1], valid)
                outs.append(jnp.dot(p.astype(BF16), vb, preferred_element_type=F32))
        o = jnp.concatenate(outs, axis=1)
        o_ref[...] = o
        po_ref[...] = o * silu(z_ref[...])

    return pl.pallas_call(
        body, grid=(L // WINDOW,),
        in_specs=[pl.BlockSpec((WINDOW, 1024), lambda n: (n, 0)), pl.BlockSpec((WINDOW, 1024), lambda n: (n, 1)),
                  pl.BlockSpec((WINDOW, 256), lambda n: (n, 8)),
                  pl.BlockSpec((WINDOW, 256), lambda n: (jnp.maximum(n - 1, 0), 8)),
                  pl.BlockSpec((SWA_HEADS, WINDOW, 2 * WINDOW), lambda n: (0, 0, 0)),
                  pl.BlockSpec((1, SWA_HEADS), lambda n: (0, 0))],
        out_specs=[pl.BlockSpec((WINDOW, 1024), lambda n: (n, 0))] * 2,
        out_shape=[jax.ShapeDtypeStruct((L, 1024), F32)] * 2,
        compiler_params=pltpu.CompilerParams(dimension_semantics=("parallel",)), name='b_attn')(
            proj, proj, proj, proj, bias, sinks)


def swa_bwd(proj, do, bias, sinks):
    L = proj.shape[0]

    def body(q_ref, kvc_ref, kvp_ref, do_ref, bias_ref, sink_ref, dq_ref, dkv_ref, dbias_ref, dsink_ref):
        n = pl.program_id(0)

        @pl.when(n == 0)
        def _():
            dkv_ref[...] = jnp.zeros_like(dkv_ref)
            dbias_ref[...] = jnp.zeros_like(dbias_ref)
            dsink_ref[...] = jnp.zeros_like(dsink_ref)

        valid = _swa_valid(n)
        q, kvc, kvp, do_ = q_ref[...], kvc_ref[...], kvp_ref[...], do_ref[...]
        dqs, dks, dvs, dsk = [], [], [], []
        for kvh in range(SWA_KV):
            kb = jnp.concatenate([kvp[:, kvh * 64:(kvh + 1) * 64], kvc[:, kvh * 64:(kvh + 1) * 64]], 0).astype(BF16)
            vb = jnp.concatenate([kvp[:, 128 + kvh * 64:128 + (kvh + 1) * 64],
                                  kvc[:, 128 + kvh * 64:128 + (kvh + 1) * 64]], 0).astype(BF16)
            dk = jnp.zeros((2 * WINDOW, 64), F32)
            dv = jnp.zeros((2 * WINDOW, 64), F32)
            for gi in range(SWA_HEADS // SWA_KV):
                h = kvh * 8 + gi
                qh = q[:, h * 64:(h + 1) * 64].astype(BF16)
                doh = do_[:, h * 64:(h + 1) * 64].astype(BF16)
                p, ps = _swa_probs(qh, kb, bias_ref[h], sink_ref[0:1, h:h + 1], valid)
                dp = lax.dot_general(doh, vb, (((1,), (1,)), ((), ())), preferred_element_type=F32)
                delta = jnp.sum(p * dp, axis=-1, keepdims=True)
                ds = p * (dp - delta)
                dsk.append(jnp.sum(-ps * delta, axis=0, keepdims=True))
                dbias_ref[h] += ds
                dsb = (ds * (HEAD_DIM ** -0.5)).astype(BF16)
                dqs.append(jnp.dot(dsb, kb, preferred_element_type=F32))
                dk = dk + lax.dot_general(dsb, qh, (((0,), (0,)), ((), ())), preferred_element_type=F32)
                dv = dv + lax.dot_general(p.astype(BF16), doh, (((0,), (0,)), ((), ())), preferred_element_type=F32)
            dks.append(dk)
            dvs.append(dv)
        dq_ref[...] = jnp.concatenate(dqs, axis=1)
        dsink_ref[...] += jnp.concatenate(dsk, axis=1)
        both = jnp.concatenate(dks + dvs, axis=1)
        r_cur = pl.multiple_of(n * WINDOW, WINDOW)
        r_prev = pl.multiple_of(jnp.maximum(n - 1, 0) * WINDOW, WINDOW)
        dkv_ref[pl.ds(r_prev, WINDOW), :] += both[:WINDOW]
        dkv_ref[pl.ds(r_cur, WINDOW), :] += both[WINDOW:]

    return pl.pallas_call(
        body, grid=(L // WINDOW,),
        in_specs=[pl.BlockSpec((WINDOW, 1024), lambda n: (n, 1)), pl.BlockSpec((WINDOW, 256), lambda n: (n, 8)),
                  pl.BlockSpec((WINDOW, 256), lambda n: (jnp.maximum(n - 1, 0), 8)),
                  pl.BlockSpec((WINDOW, 1024), lambda n: (n, 0)),
                  pl.BlockSpec((SWA_HEADS, WINDOW, 2 * WINDOW), lambda n: (0, 0, 0)),
                  pl.BlockSpec((1, SWA_HEADS), lambda n: (0, 0))],
        out_specs=[pl.BlockSpec((WINDOW, 1024), lambda n: (n, 0)), pl.BlockSpec((L, 256), lambda n: (0, 0)),
                   pl.BlockSpec((SWA_HEADS, WINDOW, 2 * WINDOW), lambda n: (0, 0, 0)),
                   pl.BlockSpec((1, SWA_HEADS), lambda n: (0, 0))],
        out_shape=[jax.ShapeDtypeStruct((L, 1024), F32), jax.ShapeDtypeStruct((L, 256), F32),
                   jax.ShapeDtypeStruct((SWA_HEADS, WINDOW, 2 * WINDOW), F32), jax.ShapeDtypeStruct((1, SWA_HEADS), F32)],
        compiler_params=pltpu.CompilerParams(dimension_semantics=("arbitrary",)), name='b_attn_bwd')(
            proj, proj, proj, do, bias, sinks)


def layer_b_fwd(h, w, p):
    proj = mm(h, w['b_w_in'], 'nn', 'b_proj')
    bucket = _t5_bucket_np()
    bias = jnp.transpose(p['rel_bias'][bucket], (2, 0, 1))
    o, po = swa_fwd(proj, bias, p['b_sinks'])
    yb = mm(po, w['b_w_out'], 'nn', 'b_out')
    return yb, dict(h=h, proj=proj, bias=bias, o=o, po=po)


def layer_b_bwd(dyb, w, p, sv):
    g = {}
    dpo = mm(dyb, w['b_w_out'], 'nt', 'b_dpo')
    g['b_w_out'] = mm(sv['po'], dyb, 'tn', 'b_dwout')
    proj = sv['proj']

    def f1(dpo_, o, z):
        return [dpo_ * silu(z), dpo_ * o * silu_grad(z)], []
    (do, dz), _ = rowwise(f1, [rw(dpo), rw(sv['o']), rw(proj, 1024, 0)], [], [(1024, F32)] * 2, [], 256, 'b_gate_bwd')
    dq, dkv, dbias, dsinks = swa_bwd(proj, do, sv['bias'], p['b_sinks'])
    g['b_sinks'] = dsinks
    onehot = jnp.asarray(np.eye(REL_BUCKETS, dtype=np.float32)[_t5_bucket_np().reshape(-1)])

    def f2(db, oh):
        return [], [lax.dot_general(db, oh, (((1,), (0,)), ((), ())), preferred_element_type=F32,
                                    precision=lax.Precision.HIGHEST)]
    _, (drel,) = rowwise(f2, [(dbias.reshape(SWA_HEADS, -1), pl.BlockSpec((SWA_HEADS, 4096), lambda i: (0, i))),
                              (onehot, pl.BlockSpec((4096, REL_BUCKETS), lambda i: (i, 0)))], [], [],
                         [(SWA_HEADS, REL_BUCKETS)], 4096, 'b_drel', n_steps=(2 * WINDOW * WINDOW) // 4096)
    g['rel_bias'] = drel.T

    def f3(dz_, dq_, dkv_):
        return [jnp.concatenate([dz_, dq_, dkv_], axis=1)], []
    (dproj,), _ = rowwise(f3, [rw(dz), rw(dq), rw(dkv)], [], [(2304, F32)], [], 256, 'b_dproj')
    g['b_w_in'] = mm(sv['h'], dproj, 'tn', 'b_dwin')
    dh = mm(dproj, w['b_w_in'], 'nt', 'b_dh')
    return dh, g


MLA_SCALE = (MLA_NOPE + MLA_ROPE) ** -0.5


def _rope_tables(L):
    inv = ROPE_BASE ** (-jnp.arange(0, MLA_ROPE, 2, dtype=F32) / MLA_ROPE)
    ang = jnp.arange(L, dtype=F32)[:, None] * inv[None, :]
    c, s = jnp.cos(ang), jnp.sin(ang)
    one, zero, pad = jnp.ones((L, 128), F32), jnp.zeros((L, 128), F32), jnp.zeros((L, 64), F32)
    return (jnp.concatenate([one, c, c, c, c, pad], 1), jnp.concatenate([zero, s, s, s, s, pad], 1))


def _rot(x, transpose=False):
    w = x.shape[1]
    lane = lax.broadcasted_iota(jnp.int32, x.shape, 1)
    up = pltpu.roll(x, w - 16, 1)
    dn = pltpu.roll(x, 16, 1)
    first = (lane % 32) < 16
    return jnp.where(first, up, -dn) if transpose else jnp.where(first, -up, dn)


def _mla_probs(qf, kf, mask):
    s = lax.dot_general(qf, kf, (((1,), (1,)), ((), ())), preferred_element_type=F32) * MLA_SCALE
    s = jnp.where(mask, s, NEG_INF)
    e = jnp.exp(s - jnp.max(s, axis=-1, keepdims=True))
    return e / jnp.sum(e, axis=-1, keepdims=True)


def _mla_heads(q, kv, kr):
    out = []
    for j in range(2):
        qf = jnp.concatenate([q[:, j * 64:(j + 1) * 64], q[:, 128 + j * 32:128 + (j + 1) * 32]], axis=1)
        kf = jnp.concatenate([kv[:, j * 64:(j + 1) * 64], kr], axis=1)
        out.append((qf, kf, kv[:, 128 + j * 64:128 + (j + 1) * 64]))
    return out


def _causal_mask(n, L):
    qpos = n * QB + lax.broadcasted_iota(jnp.int32, (QB, L), 0)
    kpos = lax.broadcasted_iota(jnp.int32, (QB, L), 1)
    return kpos <= qpos


def mla_fwd(q, kv, kr):
    L = q.shape[0]

    def body(q_ref, kv_ref, kr_ref, o_ref):
        mask = _causal_mask(pl.program_id(1), L)
        outs = []
        for qf, kf, v in _mla_heads(q_ref[...], kv_ref[...], kr_ref[:, 0:MLA_ROPE]):
            p = _mla_probs(qf, kf, mask)
            outs.append(jnp.dot(p.astype(BF16), v, preferred_element_type=F32))
        o_ref[...] = jnp.concatenate(outs, axis=1)

    return pl.pallas_call(
        body, grid=(MLA_HEADS // 2, L // QB),
        in_specs=[pl.BlockSpec((QB, 256), lambda hp, n: (n, hp)), pl.BlockSpec((L, 256), lambda hp, n: (0, hp)),
                  pl.BlockSpec((L, 128), lambda hp, n: (0, 0))],
        out_specs=pl.BlockSpec((QB, 128), lambda hp, n: (n, hp)), out_shape=jax.ShapeDtypeStruct((L, 1024), F32),
        compiler_params=pltpu.CompilerParams(dimension_semantics=("parallel", "parallel")), name='c_attn')(q, kv, kr)


def mla_bwd(q, kv, kr, do):
    L = q.shape[0]

    def body(q_ref, kv_ref, kr_ref, do_ref, dq_ref, dkv_ref, dkr_ref):
        n = pl.program_id(1)

        @pl.when(n == 0)
        def _():
            dkv_ref[...] = jnp.zeros_like(dkv_ref)
            dkr_ref[...] = jnp.zeros_like(dkr_ref)

        mask = _causal_mask(n, L)
        do_ = do_ref[...]
        dqn, dqr, dkn, dvs = [], [], [], []
        dkr = jnp.zeros((L, MLA_ROPE), F32)
        for j, (qf, kf, v) in enumerate(_mla_heads(q_ref[...], kv_ref[...], kr_ref[:, 0:MLA_ROPE])):
            doh = do_[:, j * 64:(j + 1) * 64]
            p = _mla_probs(qf, kf, mask)
            dp = lax.dot_general(doh, v, (((1,), (1,)), ((), ())), preferred_element_type=F32)
            ds = (p * (dp - jnp.sum(p * dp, axis=-1, keepdims=True)) * MLA_SCALE).astype(BF16)
            dqf = jnp.dot(ds, kf, preferred_element_type=F32)
            dkf = lax.dot_general(ds, qf, (((0,), (0,)), ((), ())), preferred_element_type=F32)
            dvs.append(lax.dot_general(p.astype(BF16), doh, (((0,), (0,)), ((), ())), preferred_element_type=F32))
            dqn.append(dqf[:, :MLA_NOPE])
            dqr.append(dqf[:, MLA_NOPE:])
            dkn.append(dkf[:, :MLA_NOPE])
            dkr = dkr + dkf[:, MLA_NOPE:]
        dq_ref[...] = jnp.concatenate(dqn + dqr + [jnp.zeros((QB, 64), F32)], axis=1)
        dkv_ref[...] += jnp.concatenate(dkn + dvs, axis=1)
        dkr_ref[0] += jnp.concatenate([dkr, jnp.zeros((L, 128 - MLA_ROPE), F32)], axis=1)

    return pl.pallas_call(
        body, grid=(MLA_HEADS // 2, L // QB),
        in_specs=[pl.BlockSpec((QB, 256), lambda hp, n: (n, hp)), pl.BlockSpec((L, 256), lambda hp, n: (0, hp)),
                  pl.BlockSpec((L, 128), lambda hp, n: (0, 0)), pl.BlockSpec((QB, 128), lambda hp, n: (n, hp))],
        out_specs=[pl.BlockSpec((QB, 256), lambda hp, n: (n, hp)), pl.BlockSpec((L, 256), lambda hp, n: (0, hp)),
                   pl.BlockSpec((1, L, 128), lambda hp, n: (hp, 0, 0))],
        out_shape=[jax.ShapeDtypeStruct((L, 2048), F32), jax.ShapeDtypeStruct((L, 2048), F32),
                   jax.ShapeDtypeStruct((MLA_HEADS // 2, L, 128), F32)],
        compiler_params=pltpu.CompilerParams(dimension_semantics=("parallel", "arbitrary")), name='c_attn_bwd')(
            q, kv, kr, do)


def _perm_c_w_in(wf):
    return jnp.concatenate([wf[:, 1056:], wf[:, :1056], jnp.zeros((wf.shape[0], 96), wf.dtype)], axis=1)


def _unperm_c_w_in(d):
    return jnp.concatenate([d[:, 1024:2080], d[:, :1024]], axis=1)


def _perm_w_uq(wf):
    t = wf.reshape(wf.shape[0], 8, 2, 96)
    nope = t[..., :64].reshape(-1, 8, 128)
    rope = t[..., 64:].reshape(-1, 8, 64)
    return jnp.concatenate([nope, rope, jnp.zeros_like(rope)], axis=2).reshape(-1, 2048)


def _unperm_w_uq(d):
    t = d.reshape(d.shape[0], 8, 256)
    nope = t[..., :128].reshape(-1, 8, 2, 64)
    rope = t[..., 128:192].reshape(-1, 8, 2, 32)
    return jnp.concatenate([nope, rope], axis=3).reshape(-1, 1536)


def _perm_w_ukv(wf):
    return wf.reshape(-1, 8, 2, 2, 64).transpose(0, 1, 3, 2, 4).reshape(-1, 2048)


def _unperm_w_ukv(d):
    return d.reshape(-1, 8, 2, 2, 64).transpose(0, 1, 3, 2, 4).reshape(-1, 2048)


def layer_c_fwd(h, w, p):
    L = h.shape[0]
    proj = mm(h, w['c_w_in'], 'nn', 'c_proj')

    def f1(c, gq, gk):
        return [rms_fwd(c[:, :768], gq), rms_fwd(c[:, 768:], gk)], []
    (cqn, ckvn), _ = rowwise(f1, [rw(proj, 1024, 1)], [p['c_q_norm'], p['c_kv_norm']], [(768, BF16), (256, BF16)], [],
                             256, 'c_norms')
    qf = mm(cqn, w['c_w_uq'], 'nn', 'c_uq')
    kvf = mm(ckvn, w['c_w_ukv'], 'nn', 'c_ukv', out_dtype=BF16)
    cos, sin = _rope_tables(L)

    def f2(q_, kr_, c, s):
        c8, s8 = jnp.tile(c, (1, 8)), jnp.tile(s, (1, 8))
        return [q_ * c8 + _rot(q_) * s8, kr_ * c[:, 128:] + _rot(kr_) * s[:, 128:]], []
    (q, kr), _ = rowwise(f2, [rw(qf), rw(proj, 128, 16), rw(cos), rw(sin)], [], [(2048, BF16), (128, BF16)], [], 256,
                         'c_rope')
    o = mla_fwd(q, kvf, kr)

    def f3(o_, z):
        return [o_ * silu(z)], []
    (po,), _ = rowwise(f3, [rw(o), rw(proj, 1024, 0)], [], [(1024, F32)], [], 256, 'c_gate')
    yb = mm(po, w['c_w_out'], 'nn', 'c_out')
    return yb, dict(h=h, proj=proj, cqn=cqn, ckvn=ckvn, q=q, kv=kvf, kr=kr, o=o, po=po, cos=cos, sin=sin)


def layer_c_bwd(dyb, w, p, sv):
    g = {}
    dpo = mm(dyb, w['c_w_out'], 'nt', 'c_dpo')
    g['c_w_out'] = mm(sv['po'], dyb, 'tn', 'c_dwout')
    proj = sv['proj']
    L = proj.shape[0]

    def f1(dpo_, o, z):
        return [dpo_ * silu(z), dpo_ * o * silu_grad(z)], []
    (do, dz), _ = rowwise(f1, [rw(dpo), rw(sv['o']), rw(proj, 1024, 0)], [], [(1024, BF16), (1024, F32)], [], 256,
                          'c_gate_bwd')
    dq, dkvf, dkr8 = mla_bwd(sv['q'], sv['kv'], sv['kr'], do)

    def f2(dq_, dkr_, c, s):
        c8, s8 = jnp.tile(c, (1, 8)), jnp.tile(s, (1, 8))
        dk = jnp.sum(dkr_, axis=0)
        return [dq_ * c8 + _rot(dq_ * s8, True), dk * c[:, 128:] + _rot(dk * s[:, 128:], True)], []
    tl = 256
    (dqf, dkr), _ = rowwise(f2, [rw(dq), (dkr8, pl.BlockSpec((8, tl, 128), lambda i: (0, i, 0))), rw(sv['cos']),
                                 rw(sv['sin'])], [], [(2048, F32), (128, F32)], [], tl, 'c_rope_bwd')
    g['c_w_uq'] = mm(sv['cqn'], dqf, 'tn', 'c_dwuq')
    g['c_w_ukv'] = mm(sv['ckvn'], dkvf, 'tn', 'c_dwukv')
    dcqn = mm(dqf, w['c_w_uq'], 'nt', 'c_dcqn')
    dckvn = mm(dkvf, w['c_w_ukv'], 'nt', 'c_dckvn')

    def f3(c, dq_, dk_, dz_, dkr_, gq, gk):
        dcq, dgq = rms_bwd(c[:, :768], gq, dq_)
        dckv, dgk = rms_bwd(c[:, 768:], gk, dk_)
        return [jnp.concatenate([dz_, dcq, dckv, dkr_], axis=1)], [dgq, dgk]
    (dproj,), (dgq, dgk) = rowwise(f3, [rw(proj, 1024, 1), rw(dcqn), rw(dckvn), rw(dz), rw(dkr)],
                                   [p['c_q_norm'], p['c_kv_norm']], [(2176, F32)], [(1, 768), (1, 256)], 256, 'c_dproj')
    g['c_q_norm'], g['c_kv_norm'] = dgq, dgk
    g['c_w_in'] = mm(sv['h'], dproj, 'tn', 'c_dwin')
    dh = mm(dproj, w['c_w_in'], 'nt', 'c_dh')
    return dh, g


def _sgu_mix(wm, v, transpose):
    outs = []
    dims = (((0,), (0,)), ((), ())) if transpose else (((1,), (0,)), ((), ()))
    for gi in range(SGU_G):
        outs.append(lax.dot_general(wm[gi], v[:, gi * SGU_C:(gi + 1) * SGU_C].astype(BF16), dims,
                                    preferred_element_type=F32))
    return jnp.concatenate(outs, axis=1)


def _sgu_wmask(ws):
    t = lax.broadcasted_iota(jnp.int32, (SGU_T, SGU_T), 0)
    s = lax.broadcasted_iota(jnp.int32, (SGU_T, SGU_T), 1)
    return jnp.where((s <= t)[None], ws, 0.0).astype(BF16)


def _ln_stats(v):
    mu = jnp.mean(v, axis=-1, keepdims=True)
    vc = v - mu
    rstd = lax.rsqrt(jnp.mean(vc * vc, axis=-1, keepdims=True) + EPS)
    return vc * rstd, rstd


def layer_d_fwd(h, w, p):
    proj = mm(h, w['d_w_in'], 'nn', 'd_proj')
    bias = jnp.repeat(p['d_b_s'][0].T, SGU_C, axis=1)

    def f1(u_, v_, z, ws, lg, lb, bs):
        xh, _ = _ln_stats(gelu(v_))
        s = _sgu_mix(_sgu_wmask(ws), xh * lg + lb, False) + bs
        return [gelu(u_) * s * silu(z)], []
    (po,), _ = rowwise(f1, [rw(proj, 1024, 0), rw(proj, 1024, 1), rw(proj, 1024, 2)],
                       [p['d_w_s'][0], p['d_ln_g'], p['d_ln_b'], bias], [(1024, F32)], [], SGU_T, 'd_mix')
    yb = mm(po, w['d_w_out'], 'nn', 'd_out')
    return yb, dict(h=h, proj=proj, po=po, bias=bias)


def layer_d_bwd(dyb, w, p, sv):
    g = {}
    dpo = mm(dyb, w['d_w_out'], 'nt', 'd_dpo')
    g['d_w_out'] = mm(sv['po'], dyb, 'tn', 'd_dwout')
    proj = sv['proj']

    def f1(dpo_, u_, v_, z, ws, lg, lb, bs):
        wm = _sgu_wmask(ws)
        gv = gelu(v_)
        xh, rstd = _ln_stats(gv)
        vn = xh * lg + lb
        s = _sgu_mix(wm, vn, False) + bs
        gu, sz = gelu(u_), silu(z)
        du = dpo_ * s * sz
        ds = dpo_ * gu * sz
        dz = dpo_ * gu * s * silu_grad(z)
        dsb = ds.astype(BF16)
        dws = jnp.stack([lax.dot_general(dsb[:, gi * SGU_C:(gi + 1) * SGU_C], vn[:, gi * SGU_C:(gi + 1) * SGU_C].astype(BF16),
                                         (((1,), (1,)), ((), ())), preferred_element_type=F32) for gi in range(SGU_G)])
        dvn = _sgu_mix(wm, ds, True)
        dlg = jnp.sum(dvn * xh, axis=0, keepdims=True)
        dlb = jnp.sum(dvn, axis=0, keepdims=True)
        dxh = dvn * lg
        dgv = rstd * (dxh - jnp.mean(dxh, axis=-1, keepdims=True) - xh * jnp.mean(dxh * xh, axis=-1, keepdims=True))
        return ([jnp.concatenate([du * gelu_grad(u_), dgv * gelu_grad(v_), dz], axis=1)], [dws, ds, dlg, dlb])
    (dproj,), (dws, dbs, dlg, dlb) = rowwise(
        f1, [rw(dpo), rw(proj, 1024, 0), rw(proj, 1024, 1), rw(proj, 1024, 2)],
        [p['d_w_s'][0], p['d_ln_g'], p['d_ln_b'], sv['bias']], [(3072, F32)],
        [(SGU_G, SGU_T, SGU_T), (SGU_T, 1024), (1, 1024), (1, 1024)], SGU_T, 'd_mix_bwd')
    tril = np.tril(np.ones((SGU_T, SGU_T), dtype=bool))
    g['d_w_s'] = jnp.where(tril[None], dws, 0.0)[None]
    g['d_b_s'] = dbs.reshape(SGU_T, SGU_G, SGU_C).sum(-1).T[None]
    g['d_ln_g'], g['d_ln_b'] = dlg, dlb
    g['d_w_in'] = mm(sv['h'], dproj, 'tn', 'd_dwin')
    dh = mm(dproj, w['d_w_in'], 'nt', 'd_dh')
    return dh, g


def _coords():
    return lax.axis_index("x"), lax.axis_index("y"), lax.axis_index("c")


def all_gather(x, name):
    def body(x_ref, out_ref, send_sems, recv_sems, local_sem):
        x_, y_, c_ = _coords()
        me, sibling = (x_, y_, c_), (x_, y_, 1 - c_)
        chips = [(1 - x_, y_), (x_, 1 - y_), (1 - x_, 1 - y_)]

        def slot(px, py, pc):
            return out_ref.at[4 * px + 2 * py + pc]

        def copy(k, block, to, src=None):
            return pltpu.make_async_remote_copy(src_ref=slot(*block) if src is None else src, dst_ref=slot(*block),
                                                send_sem=send_sems.at[k], recv_sem=recv_sems.at[k], device_id=to,
                                                device_id_type=MESH)

        mine = pltpu.make_async_copy(x_ref, slot(*me), local_sem)
        mine.start()
        first = [copy(0, me, sibling, src=x_ref)]
        first += [copy(1 + j, me, (*chip, c_), src=x_ref) for j, chip in enumerate(chips)]
        for cp in first:
            cp.start()
        passed = [copy(4 + j, (*chip, c_), sibling) for j, chip in enumerate(chips)]
        for j, chip in enumerate(chips):
            copy(1 + j, (*chip, c_), me).wait_recv()
            passed[j].start()
        copy(0, sibling, me).wait_recv()
        for j, chip in enumerate(chips):
            copy(4 + j, (*chip, 1 - c_), me).wait_recv()
        for cp in first + passed:
            cp.wait_send()
        mine.wait()

    return pl.pallas_call(
        body, out_shape=jax.ShapeDtypeStruct((N_DEV,) + x.shape, x.dtype), in_specs=[ANY], out_specs=ANY,
        scratch_shapes=[pltpu.SemaphoreType.DMA((7,)), pltpu.SemaphoreType.DMA((7,)), pltpu.SemaphoreType.DMA(())],
        name=name)(x)


def rs_sibling(gfull):
    _, R, C = gfull.shape

    def body(g_ref, land_ref, send_sems, recv_sems):
        x_, y_, c_ = _coords()
        copies = []
        for k in range(4):
            cp = pltpu.make_async_remote_copy(src_ref=g_ref.at[2 * k + 1 - c_], dst_ref=land_ref.at[k],
                                              send_sem=send_sems.at[k], recv_sem=recv_sems.at[k],
                                              device_id=(x_, y_, 1 - c_), device_id_type=MESH)
            cp.start()
            copies.append(cp)
        for cp in copies:
            cp.wait_recv()
        for cp in copies:
            cp.wait_send()

    return pl.pallas_call(
        body, out_shape=jax.ShapeDtypeStruct((4, R, C), gfull.dtype), in_specs=[ANY], out_specs=ANY,
        scratch_shapes=[pltpu.SemaphoreType.DMA((4,)), pltpu.SemaphoreType.DMA((4,))], name='rs_sibling')(gfull)


def rs_pair_add(gfull, land, core):
    _, R, C = gfull.shape
    tl = _pick(R, (248, 240, 136, 8))

    def body(c_ref, g_ref, l_ref, o_ref):
        o_ref[...] = g_ref[...] + l_ref[...]

    return pl.pallas_call(
        body, out_shape=jax.ShapeDtypeStruct((4, R, C), F32),
        grid_spec=pltpu.PrefetchScalarGridSpec(
            num_scalar_prefetch=1, grid=(4, R // tl),
            in_specs=[pl.BlockSpec((1, tl, C), lambda k, i, c: (2 * k + c[0], i, 0)),
                      pl.BlockSpec((1, tl, C), lambda k, i, c: (k, i, 0))],
            out_specs=pl.BlockSpec((1, tl, C), lambda k, i, c: (k, i, 0))),
        compiler_params=pltpu.CompilerParams(dimension_semantics=("parallel", "parallel")), name='rs_pair_add')(
            core, gfull, land)


def rs_chips(part):
    _, R, C = part.shape

    def body(p_ref, land_ref, send_sems, recv_sems):
        x_, y_, c_ = _coords()
        copies = []
        for r, (fx, fy) in enumerate([(1, 0), (0, 1), (1, 1)]):
            tx = jnp.where(fx == 1, 1 - x_, x_)
            ty = jnp.where(fy == 1, 1 - y_, y_)
            cp = pltpu.make_async_remote_copy(src_ref=p_ref.at[2 * tx + ty], dst_ref=land_ref.at[r],
                                              send_sem=send_sems.at[r], recv_sem=recv_sems.at[r],
                                              device_id=(tx, ty, c_), device_id_type=MESH)
            cp.start()
            copies.append(cp)
        for cp in copies:
            cp.wait_recv()
        for cp in copies:
            cp.wait_send()

    return pl.pallas_call(
        body, out_shape=jax.ShapeDtypeStruct((3, R, C), part.dtype), in_specs=[ANY], out_specs=ANY,
        scratch_shapes=[pltpu.SemaphoreType.DMA((3,)), pltpu.SemaphoreType.DMA((3,))], name='rs_chips')(part)


def _adam(wv, gv, mv, vv):
    m = ADAM_B1 * mv + (1.0 - ADAM_B1) * gv
    v = ADAM_B2 * vv + (1.0 - ADAM_B2) * (gv * gv)
    m_hat = m / (1.0 - ADAM_B1 ** ADAM_STEP)
    v_hat = v / (1.0 - ADAM_B2 ** ADAM_STEP)
    delta = -ADAM_LR * (m_hat / (jnp.sqrt(v_hat) + ADAM_EPS) + ADAM_WD * wv)
    return delta, m, v


def rs_final_adam(part, land, chip, wp, mp, vp, rows, row0, with_adam, name):
    _, R, C = part.shape
    tl = _pick(rows, (240, 136, 8))
    b0 = row0 // tl
    assert row0 % tl == 0

    def body(*refs):
        if with_adam:
            _, p_ref, l_ref, w_ref, m_ref, v_ref, g_ref, d_ref, nm_ref, nv_ref = refs
        else:
            _, p_ref, l_ref, g_ref = refs
        gsum = ((p_ref[0] + l_ref[0]) + l_ref[1]) + l_ref[2]
        g_ref[...] = gsum
        if with_adam:
            d_ref[...], nm_ref[...], nv_ref[...] = _adam(w_ref[...], gsum, m_ref[...], v_ref[...])

    in_specs = [pl.BlockSpec((1, tl, C), lambda i, c: (c[0], b0 + i, 0)),
                pl.BlockSpec((3, tl, C), lambda i, c: (0, b0 + i, 0))]
    args = [part, land]
    n_out = 1
    if with_adam:
        in_specs += [pl.BlockSpec((tl, C), lambda i, c: (i, 0))] * 3
        args += [wp, mp, vp]
        n_out = 4
    return pl.pallas_call(
        body, out_shape=[jax.ShapeDtypeStruct((rows, C), F32)] * n_out,
        grid_spec=pltpu.PrefetchScalarGridSpec(num_scalar_prefetch=1, grid=(rows // tl,), in_specs=in_specs,
                                               out_specs=[pl.BlockSpec((tl, C), lambda i, c: (i, 0))] * n_out),
        compiler_params=pltpu.CompilerParams(dimension_semantics=("parallel",)), name=name)(chip, *args)


def _pack(blocks, order, rows, width, dtype):
    flat = jnp.concatenate([blocks[n].reshape(-1).astype(dtype) for n in order])
    return jnp.pad(flat, (0, rows * width - flat.shape[0])).reshape(rows, width)


def _unpack_gathered(gathered, name, off):
    (r, c), ax = SHARDED[name]
    br, bc = _block_shape(name)
    blk = gathered.reshape(N_DEV, -1)[:, off:off + br * bc].reshape(N_DEV, br, bc)
    return blk.reshape(r, c) if ax == 0 else blk.transpose(1, 0, 2).reshape(r, c)


def _chunk_full_grad(name, gfull):
    (r, c), ax = SHARDED[name]
    br, bc = _block_shape(name)
    if ax == 0:
        return gfull.reshape(N_DEV, br * bc)
    return gfull.reshape(r, N_DEV, bc).transpose(1, 0, 2).reshape(N_DEV, br * bc)


def kernel(x, pre_norm, post_norm, rel_bias, a_w_in, a_lam_re, a_lam_im, a_log_dt, a_b_re, a_b_im, a_c_re, a_c_im, a_d, a_w_glu, a_b_glu, a_w_out, b_w_in, b_sinks, b_w_out, c_w_in, c_q_norm, c_kv_norm, c_w_uq, c_w_ukv, c_w_out, d_w_in, d_ln_g, d_ln_b, d_w_s, d_b_s, d_w_out, loss_target, m_pre_norm, m_post_norm, m_rel_bias, m_a_w_in, m_a_lam_re, m_a_lam_im, m_a_log_dt, m_a_b_re, m_a_b_im, m_a_c_re, m_a_c_im, m_a_d, m_a_w_glu, m_a_b_glu, m_a_w_out, m_b_w_in, m_b_sinks, m_b_w_out, m_c_w_in, m_c_q_norm, m_c_kv_norm, m_c_w_uq, m_c_w_ukv, m_c_w_out, m_d_w_in, m_d_ln_g, m_d_ln_b, m_d_w_s, m_d_b_s, m_d_w_out, v_pre_norm, v_post_norm, v_rel_bias, v_a_w_in, v_a_lam_re, v_a_lam_im, v_a_log_dt, v_a_b_re, v_a_b_im, v_a_c_re, v_a_c_im, v_a_d, v_a_w_glu, v_a_b_glu, v_a_w_out, v_b_w_in, v_b_sinks, v_b_w_out, v_c_w_in, v_c_q_norm, v_c_kv_norm, v_c_w_uq, v_c_w_ukv, v_c_w_out, v_d_w_in, v_d_ln_g, v_d_ln_b, v_d_w_s, v_d_b_s, v_d_w_out):
    loc = locals()
    P = {n: loc[n] for n in WEIGHTS}
    M = {n: loc['m_' + n] for n in WEIGHTS}
    V = {n: loc['v_' + n] for n in WEIGHTS}
    xs = x[0]
    L = xs.shape[0]

    blocks = {n: P[n].reshape(_block_shape(n)) for n in SHARDED}
    gathered = all_gather(_pack(blocks, list(SHARDED), SH_ROWS, PACK_C, BF16), 'ag_weights')
    small = all_gather(_pack(blocks, SHARDED_F32, SMALL_ROWS, 128, F32), 'ag_small')
    W = {n: _unpack_gathered(gathered, n, SH_OFF[n]) for n in SHARDED if n not in SHARDED_F32}
    Pl = dict(P)
    for n in SHARDED_F32:
        c = SHARDED[n][0][1]
        bc = c // N_DEV
        Pl[n] = small.reshape(N_DEV, -1)[:, SMALL_OFF[n]:SMALL_OFF[n] + bc].reshape(1, c)
    W['b_w_in'] = jnp.concatenate([W['b_w_in'][:, 1280:], W['b_w_in'][:, :1280]], axis=1)
    W['c_w_in'] = _perm_c_w_in(W['c_w_in'])
    W['c_w_uq'] = _perm_w_uq(W['c_w_uq'])
    W['c_w_ukv'] = _perm_w_ukv(W['c_w_ukv'])

    fwd = [layer_a_fwd, layer_b_fwd, layer_c_fwd, layer_d_fwd]
    bwd = [layer_a_bwd, layer_b_bwd, layer_c_bwd, layer_d_bwd]
    saved = []
    xc = xs
    for i in range(4):
        def fpre(x_, g_):
            return [rms_fwd(x_, g_)], []
        (h,), _ = rowwise(fpre, [rw(xc)], [P['pre_norm'][i:i + 1]], [(D_MODEL, F32)], [], 256, f'pre_norm{i}')
        yb, sv = fwd[i](h, W, Pl)

        def fpost(x_, y_, g_):
            return [x_ + rms_fwd(y_, g_)], []
        (xn,), _ = rowwise(fpost, [rw(xc), rw(yb)], [P['post_norm'][i:i + 1]], [(D_MODEL, F32)], [], 256, f'post_norm{i}')
        sv['x'], sv['yb'] = xc, yb
        saved.append(sv)
        xc = xn

    def floss(y_, t_):
        d = y_ - t_
        return [d * (1.0 / D_MODEL)], [0.5 * jnp.sum(jnp.sum(d * d, axis=-1, keepdims=True) * (1.0 / D_MODEL), axis=0,
                                                      keepdims=True)]
    (dx,), (loss_loc,) = rowwise(floss, [rw(xc), rw(loss_target[0])], [], [(D_MODEL, F32)], [(1, 1)], 256, 'loss')
    loss = lax.psum(loss_loc[0, 0], ("x", "y", "c"))

    G = {}
    dpre, dpost = [None] * 4, [None] * 4
    for i in reversed(range(4)):
        sv = saved[i]

        def fpost_b(y_, d_, g_):
            dy, dg = rms_bwd(y_, g_, d_)
            return [dy], [dg]
        (dyb,), (dpost[i],) = rowwise(fpost_b, [rw(sv['yb']), rw(dx)], [P['post_norm'][i:i + 1]], [(D_MODEL, F32)],
                                      [(1, D_MODEL)], 256, f'post_norm_bwd{i}')
        dh, g = bwd[i](dyb, W, Pl, sv)
        G.update(g)

        def fpre_b(x_, dh_, d_, g_):
            dxl, dg = rms_bwd(x_, g_, dh_)
            return [d_ + dxl], [dg]
        (dx,), (dpre[i],) = rowwise(fpre_b, [rw(sv['x']), rw(dh), rw(dx)], [P['pre_norm'][i:i + 1]], [(D_MODEL, F32)],
                                    [(1, D_MODEL)], 256, f'pre_norm_bwd{i}')
    G['pre_norm'] = jnp.concatenate(dpre, axis=0)
    G['post_norm'] = jnp.concatenate(dpost, axis=0)
    G['b_w_in'] = jnp.concatenate([G['b_w_in'][:, 1024:], G['b_w_in'][:, :1024]], axis=1)
    G['c_w_in'] = _unperm_c_w_in(G['c_w_in'])
    G['c_w_uq'] = _unperm_w_uq(G['c_w_uq'])
    G['c_w_ukv'] = _unperm_w_ukv(G['c_w_ukv'])

    rep_flat = jnp.concatenate([G[n].reshape(-1) for n in REPLICATED])
    n_rep = rep_flat.shape[0]
    rep_rows = _cdiv(_cdiv(n_rep, N_DEV), PACK_C * 8) * 8
    rep_chunks = jnp.pad(rep_flat, (0, N_DEV * rep_rows * PACK_C - n_rep)).reshape(N_DEV, rep_rows * PACK_C)
    sh_chunks = jnp.concatenate([_chunk_full_grad(n, G[n]) for n in SHARDED], axis=1)
    sh_chunks = jnp.pad(sh_chunks, ((0, 0), (0, SH_ROWS * PACK_C - sh_chunks.shape[1])))
    R = SH_ROWS + rep_rows
    gfull = jnp.concatenate([sh_chunks, rep_chunks], axis=1).reshape(N_DEV, R, PACK_C)
    cx, cy, cc = _coords()
    core = jnp.reshape(cc, (1,)).astype(jnp.int32)
    chip = jnp.reshape(2 * cx + cy, (1,)).astype(jnp.int32)
    land = rs_sibling(gfull)
    part = rs_pair_add(gfull, land, core)
    land2 = rs_chips(part)
    wp = _pack(blocks, list(SHARDED), SH_ROWS, PACK_C, F32)
    mp = _pack({n: M[n] for n in SHARDED}, list(SHARDED), SH_ROWS, PACK_C, F32)
    vp = _pack({n: V[n] for n in SHARDED}, list(SHARDED), SH_ROWS, PACK_C, F32)
    g_sh, d_sh, m_sh, v_sh = rs_final_adam(part, land2, chip, wp, mp, vp, SH_ROWS, 0, True, 'adam_sharded')
    (g_rep_chunk,) = rs_final_adam(part, land2, chip, None, None, None, rep_rows, SH_ROWS, False, 'rs_rep')
    g_rep = all_gather(g_rep_chunk, 'ag_rep').reshape(-1)[:n_rep]

    rrows = _cdiv(n_rep, PACK_C * 16) * 16
    packr = lambda d: _pack(d, REPLICATED, rrows, PACK_C, F32)

    def fadam(w_, g_, m_, v_):
        return list(_adam(w_, g_, m_, v_)), []
    (d_rp, m_rp, v_rp), _ = rowwise(fadam, [rw(packr(P)), rw(jnp.pad(g_rep, (0, rrows * PACK_C - n_rep)).reshape(rrows, PACK_C)),
                                            rw(packr(M)), rw(packr(V))], [], [(PACK_C, F32)] * 3, [], rrows // 2, 'adam_rep')

    out = {'grad': {}, 'delta': {}, 'new_m': {}, 'new_v': {}}
    for n in SHARDED:
        sz = int(np.prod(_block_shape(n)))
        for kind, buf in (('grad', g_sh), ('delta', d_sh), ('new_m', m_sh), ('new_v', v_sh)):
            out[kind][n] = buf.reshape(-1)[SH_OFF[n]:SH_OFF[n] + sz].reshape(P[n].shape)
    off = 0
    for n in REPLICATED:
        sz = int(np.prod(P[n].shape))
        for kind, buf in (('grad', g_rep), ('delta', d_rp.reshape(-1)), ('new_m', m_rp.reshape(-1)),
                          ('new_v', v_rp.reshape(-1))):
            out[kind][n] = buf[off:off + sz].reshape(P[n].shape)
        off += sz
    res = [loss, dx[None]]
    for kind in ('grad', 'delta', 'new_m', 'new_v'):
        res += [out[kind][n] for n in WEIGHTS]
    return tuple(res)
```

```python
import functools
import math

import numpy as np
import jax
import jax.numpy as jnp
from jax import lax
from jax.experimental import pallas as pl
from jax.experimental.pallas import tpu as pltpu

F32 = jnp.float32
BF16 = jnp.bfloat16
MESH = pl.DeviceIdType.MESH
ANY = pl.BlockSpec(memory_space=pl.ANY)

N_DEV = 8
D_MODEL = 1024
EPS = 1e-6
NEG_INF = -1e30
SSM_G, SSM_P, SSM_H = 64, 64, 16
SSM_T = 256
SSM_WC = 512
HEAD_DIM = 64
SWA_HEADS, SWA_KV = 16, 2
WINDOW = 128
REL_BUCKETS, REL_MAX_DIST = 32, 128
MLA_HEADS, MLA_NOPE, MLA_ROPE, MLA_V = 16, 64, 32, 64
MLA_Q_RANK, MLA_KV_RANK = 768, 256
ROPE_BASE = 10000.0
SGU_G, SGU_C, SGU_T = 16, 64, 128
ADAM_LR, ADAM_B1, ADAM_B2, ADAM_EPS, ADAM_WD, ADAM_STEP = 0.001, 0.9, 0.999, 1e-08, 0.01, 10

WEIGHTS = ['pre_norm', 'post_norm', 'rel_bias', 'a_w_in', 'a_lam_re', 'a_lam_im', 'a_log_dt', 'a_b_re', 'a_b_im',
           'a_c_re', 'a_c_im', 'a_d', 'a_w_glu', 'a_b_glu', 'a_w_out', 'b_w_in', 'b_sinks', 'b_w_out', 'c_w_in',
           'c_q_norm', 'c_kv_norm', 'c_w_uq', 'c_w_ukv', 'c_w_out', 'd_w_in', 'd_ln_g', 'd_ln_b', 'd_w_s', 'd_b_s',
           'd_w_out']
SHARDED = {'a_w_in': ((1024, 2048), 1), 'a_w_glu': ((1024, 1024), 0), 'a_w_out': ((1024, 1024), 0),
           'b_w_in': ((1024, 2304), 1), 'b_w_out': ((1024, 1024), 0), 'c_w_in': ((1024, 2080), 1),
           'c_q_norm': ((1, 768), 1), 'c_kv_norm': ((1, 256), 1), 'c_w_uq': ((768, 1536), 1),
           'c_w_ukv': ((256, 2048), 1), 'c_w_out': ((1024, 1024), 0), 'd_w_in': ((1024, 3072), 1),
           'd_ln_g': ((1, 1024), 1), 'd_ln_b': ((1, 1024), 1), 'd_w_out': ((1024, 1024), 0)}
SHARDED_F32 = ['c_q_norm', 'c_kv_norm', 'd_ln_g', 'd_ln_b']
REPLICATED = [n for n in WEIGHTS if n not in SHARDED]


def _cdiv(a, b):
    return -(-a // b)


def _block_shape(name):
    (r, c), ax = SHARDED[name]
    return (r // N_DEV, c) if ax == 0 else (r, c // N_DEV)


LANES = 128
SH_ORDER = ['a_w_in', 'b_w_in', 'c_w_in', 'd_w_in', 'c_w_uq', 'c_w_ukv', 'a_w_glu', 'a_w_out', 'b_w_out', 'c_w_out',
            'd_w_out', 'c_q_norm', 'c_kv_norm', 'd_ln_g', 'd_ln_b']


def _tiles(shape):
    r, c = shape
    rp = max(r, 8)
    rb = 512 if rp % 512 == 0 else 256 if rp % 256 == 0 else rp
    return rp, _cdiv(c, LANES), rb


SH_OFF = {}
_o = 0
for _n in SH_ORDER:
    _rp, _nt, _rb = _tiles(_block_shape(_n))
    assert _o % _rb == 0
    SH_OFF[_n] = _o
    _o += _rp * _nt
SH_ROWS = _o
assert SH_ROWS % 16 == 0

REP_SHAPE = {'a_b_re': (4096, 16), 'a_b_im': (4096, 16), 'd_w_s': (2048, 128), 'a_c_re': (1024, 64),
             'a_c_im': (1024, 64), 'pre_norm': (4, 1024), 'post_norm': (4, 1024), 'a_lam_re': (64, 64),
             'a_lam_im': (64, 64), 'a_d': (1, 1024), 'a_b_glu': (1, 1024), 'rel_bias': (32, 16), 'd_b_s': (16, 128),
             'a_log_dt': (1, 64), 'b_sinks': (1, 16)}
REP_OFF = {}
_o = 0
for _n, _s in REP_SHAPE.items():
    _rp, _nt, _rb = _tiles(_s)
    assert _o % _rb == 0
    REP_OFF[_n] = _o
    _o += _rp * _nt
REP_ROWS = _o
REP_CHUNK = REP_ROWS // N_DEV
assert REP_ROWS % (8 * N_DEV) == 0
REP_SLOT = 2048
RS_TL = 512
RS_ROWS = _cdiv(REP_SLOT + SH_ROWS, RS_TL) * RS_TL
SMALL_OFF = {}
_o = 0
for _n in SHARDED_F32:
    SMALL_OFF[_n] = _o
    _o += int(np.prod(_block_shape(_n)))
SMALL_ROWS = _cdiv(_o, 128 * 8) * 8


def _pick(n, cands):
    for c in cands:
        if n % c == 0:
            return c
    return n


def mm(a, b, mode, name, out_dtype=F32):
    if mode == 'nn':
        (M, K), (K2, N) = a.shape, b.shape
    elif mode == 'nt':
        (M, K), (N, K2) = a.shape, b.shape
    else:
        (K, M), (K2, N) = a.shape, b.shape
    assert K == K2, (name, a.shape, b.shape)
    tm = _pick(M, (512, 256, 128))
    tn = _pick(N, (512, 384, 256, 128))
    dims = {'nn': ((1,), (0,)), 'nt': ((1,), (1,)), 'tn': ((0,), (0,))}[mode]

    def body(a_ref, b_ref, o_ref):
        o_ref[...] = lax.dot_general(a_ref[...].astype(BF16), b_ref[...].astype(BF16), (dims, ((), ())),
                                     preferred_element_type=F32).astype(out_dtype)

    a_spec = pl.BlockSpec((K, tm), lambda i, j: (0, i)) if mode == 'tn' else pl.BlockSpec((tm, K), lambda i, j: (i, 0))
    b_spec = pl.BlockSpec((tn, K), lambda i, j: (j, 0)) if mode == 'nt' else pl.BlockSpec((K, tn), lambda i, j: (0, j))
    return pl.pallas_call(
        body, grid=(M // tm, N // tn), in_specs=[a_spec, b_spec],
        out_specs=pl.BlockSpec((tm, tn), lambda i, j: (i, j)), out_shape=jax.ShapeDtypeStruct((M, N), out_dtype),
        compiler_params=pltpu.CompilerParams(dimension_semantics=("parallel", "parallel")), name=name)(a, b)


def rw(arr, width=None, cb=0):
    return (arr, arr.shape[1] if width is None else width, cb)


def rowwise(fn, rows, consts, outs, accs, tl, name, n_steps=None):
    if n_steps is None:
        n_steps = [r[0].shape[0] for r in rows if not isinstance(r[1], pl.BlockSpec)][0] // tl
    L = n_steps * tl
    nr, nc, no, na = len(rows), len(consts), len(outs), len(accs)
    in_specs, args = [], []
    for r in rows:
        if isinstance(r[1], pl.BlockSpec):
            in_specs.append(r[1])
        else:
            in_specs.append(pl.BlockSpec((tl, r[1]), functools.partial(lambda i, cb: (i, cb), cb=r[2])))
        args.append(r[0])
    for c in consts:
        in_specs.append(pl.BlockSpec(c.shape, functools.partial(lambda i, nd: (0,) * nd, nd=c.ndim)))
        args.append(c)
    out_specs = [pl.BlockSpec((tl, w), lambda i: (i, 0)) for w, _ in outs]
    out_shape = [jax.ShapeDtypeStruct((L, w), dt) for w, dt in outs]
    for s in accs:
        out_specs.append(pl.BlockSpec(s, functools.partial(lambda i, nd: (0,) * nd, nd=len(s))))
        out_shape.append(jax.ShapeDtypeStruct(s, F32))

    def body(*refs):
        ins = [r[...] for r in refs[:nr + nc]]
        o_refs = refs[nr + nc:nr + nc + no]
        a_refs = refs[nr + nc + no:]
        o_vals, a_vals = fn(*ins)
        for ref, val in zip(o_refs, o_vals):
            ref[...] = val.astype(ref.dtype)
        if na:
            @pl.when(pl.program_id(0) == 0)
            def _():
                for ref in a_refs:
                    ref[...] = jnp.zeros_like(ref)
            for ref, val in zip(a_refs, a_vals):
                ref[...] += val

    res = pl.pallas_call(
        body, grid=(n_steps,), in_specs=in_specs, out_specs=out_specs, out_shape=out_shape,
        compiler_params=pltpu.CompilerParams(dimension_semantics=("arbitrary",)), name=name)(*args)
    return res[:no], res[no:]


def tmm(a, b, mode, name, tl=512, wa=None, wb=None):
    L = a.shape[0]
    tl = min(tl, L)
    nt = 8
    if mode == 'tn':
        def body(a_ref, b_ref, o_ref):
            @pl.when(pl.program_id(1) == 0)
            def _():
                o_ref[...] = jnp.zeros_like(o_ref)
            o_ref[0] += lax.dot_general(a_ref[...].astype(BF16), b_ref[...].astype(BF16), (((0,), (0,)), ((), ())),
                                        preferred_element_type=F32)

        return pl.pallas_call(
            body, grid=(nt, L // tl),
            in_specs=[pl.BlockSpec((tl, wa), lambda k, i: (i, k)), pl.BlockSpec((tl, wb), lambda k, i: (i, k))],
            out_specs=pl.BlockSpec((1, wa, wb), lambda k, i: (k, 0, 0)),
            out_shape=jax.ShapeDtypeStruct((nt, wa, wb), F32),
            compiler_params=pltpu.CompilerParams(dimension_semantics=("parallel", "arbitrary")), name=name)(a, b)
    assert b.shape[0] == nt
    wa = b.shape[1] if mode == 'nn' else b.shape[2]
    wo = b.shape[2] if mode == 'nn' else b.shape[1]
    dims = ((1,), (0,)) if mode == 'nn' else ((1,), (1,))

    def body(a_ref, b_ref, o_ref):
        o_ref[...] = lax.dot_general(a_ref[...].astype(BF16), b_ref[0].astype(BF16), (dims, ((), ())),
                                     preferred_element_type=F32)

    return pl.pallas_call(
        body, grid=(L // tl, nt),
        in_specs=[pl.BlockSpec((tl, wa), lambda i, k: (i, k)), pl.BlockSpec((1,) + b.shape[1:], lambda i, k: (k, 0, 0))],
        out_specs=pl.BlockSpec((tl, wo), lambda i, k: (i, k)), out_shape=jax.ShapeDtypeStruct((L, nt * wo), F32),
        compiler_params=pltpu.CompilerParams(dimension_semantics=("parallel", "parallel")), name=name)(a, b)


_K0 = math.sqrt(2.0 / math.pi)
_K1 = 0.044715


def gelu(x):
    return x * (0.5 * (1.0 + jnp.tanh(_K0 * (x + _K1 * (x * x * x)))))


def gelu_grad(x):
    t = jnp.tanh(_K0 * (x + _K1 * (x * x * x)))
    return 0.5 * (1.0 + t) + 0.5 * x * (1.0 - t * t) * (_K0 * (1.0 + 3.0 * _K1 * x * x))


def sigmoid(x):
    return 1.0 / (1.0 + jnp.exp(-x))


def silu(z):
    return z * sigmoid(z)


def silu_grad(z):
    s = sigmoid(z)
    return s * (1.0 + z * (1.0 - s))


def rms_fwd(x, g):
    r = lax.rsqrt(jnp.mean(x * x, axis=-1, keepdims=True) + EPS)
    return x * r * g


def rms_bwd(x, g, dy):
    r = lax.rsqrt(jnp.mean(x * x, axis=-1, keepdims=True) + EPS)
    xh = x * r
    dg = jnp.sum(dy * xh, axis=0, keepdims=True)
    dxh = dy * g
    dx = r * (dxh - xh * jnp.mean(dxh * xh, axis=-1, keepdims=True))
    return dx, dg


def s5_scan(xr, xi, pr, pi, name, reverse=False, s_re=None, s_im=None):
    L, W = xr.shape
    T, WC = min(SSM_T, L), SSM_WC
    nT = L // T
    with_da = s_re is not None
    steps = [1 << k for k in range(int(math.log2(T)))]

    def body(*refs):
        if with_da:
            xr_ref, xi_ref, pr_ref, pi_ref, sr_ref, si_ref, spr_ref, spi_ref, or_ref, oi_ref, dar_ref, dai_ref, cr, ci = refs
        else:
            xr_ref, xi_ref, pr_ref, pi_ref, or_ref, oi_ref, cr, ci = refs
        i = pl.program_id(1)

        @pl.when(i == 0)
        def _():
            cr[...] = jnp.zeros_like(cr)
            ci[...] = jnp.zeros_like(ci)
            if with_da:
                dar_ref[...] = jnp.zeros_like(dar_ref)
                dai_ref[...] = jnp.zeros_like(dai_ref)

        a_r = xr_ref[...]
        a_i = xi_ref[...]
        row = lax.broadcasted_iota(jnp.int32, (T, WC), 0)
        sgn = -1.0 if reverse else 1.0
        for d in steps:
            wr = pr_ref[(T - d) if reverse else (d - 1):(T - d + 1) if reverse else d, :]
            wi = sgn * pi_ref[(T - d) if reverse else (d - 1):(T - d + 1) if reverse else d, :]
            if reverse:
                yr, yi, keep = pltpu.roll(a_r, T - d, 0), pltpu.roll(a_i, T - d, 0), row < T - d
            else:
                yr, yi, keep = pltpu.roll(a_r, d, 0), pltpu.roll(a_i, d, 0), row >= d
            a_r, a_i = (a_r + jnp.where(keep, wr * yr - wi * yi, 0.0), a_i + jnp.where(keep, wr * yi + wi * yr, 0.0))
        wr = pr_ref[...]
        wi = sgn * pi_ref[...]
        c_r, c_i = cr[...], ci[...]
        a_r, a_i = a_r + (wr * c_r - wi * c_i), a_i + (wr * c_i + wi * c_r)
        or_ref[...] = a_r
        oi_ref[...] = a_i
        if reverse:
            cr[...] = a_r[0:1, :]
            ci[...] = a_i[0:1, :]
        else:
            cr[...] = a_r[T - 1:T, :]
            ci[...] = a_i[T - 1:T, :]
        if with_da:
            first = (nT - 1 - i) == 0
            pv_r = jnp.where(first, 0.0, spr_ref[7:8, :])
            pv_i = jnp.where(first, 0.0, spi_ref[7:8, :])
            sp_r = jnp.where(row == 0, pv_r, pltpu.roll(sr_ref[...], 1, 0))
            sp_i = jnp.where(row == 0, pv_i, pltpu.roll(si_ref[...], 1, 0))
            dar_ref[...] += jnp.sum(a_r * sp_r + a_i * sp_i, axis=0, keepdims=True)
            dai_ref[...] += jnp.sum(a_i * sp_r - a_r * sp_i, axis=0, keepdims=True)

    if reverse:
        xmap = lambda j, i: (nT - 1 - i, j)
        pmap = lambda j, i: (jnp.maximum((nT - 1 - i) * (T // 8) - 1, 0), j)
    else:
        xmap = lambda j, i: (i, j)
    xspec = pl.BlockSpec((T, WC), xmap)
    pspec = pl.BlockSpec((T, WC), lambda j, i: (0, j))
    in_specs = [xspec, xspec, pspec, pspec]
    args = [xr, xi, pr, pi]
    out_specs = [xspec, xspec]
    out_shape = [jax.ShapeDtypeStruct((L, W), F32)] * 2
    if with_da:
        in_specs += [xspec, xspec, pl.BlockSpec((8, WC), pmap), pl.BlockSpec((8, WC), pmap)]
        args += [s_re, s_im, s_re, s_im]
        out_specs += [pl.BlockSpec((1, WC), lambda j, i: (0, j))] * 2
        out_shape += [jax.ShapeDtypeStruct((1, W), F32)] * 2
    return pl.pallas_call(
        body, grid=(W // WC, nT), in_specs=in_specs, out_specs=out_specs, out_shape=out_shape,
        scratch_shapes=[pltpu.VMEM((1, WC), F32), pltpu.VMEM((1, WC), F32)],
        compiler_params=pltpu.CompilerParams(dimension_semantics=("parallel", "arbitrary")), name=name)(*args)


def s5_discretize(lam_re, lam_im, log_dt, b_re, b_im):
    dt = jnp.exp(log_dt)[:, None]
    mag = jnp.exp(lam_re * dt)
    ab_re = mag * jnp.cos(lam_im * dt)
    ab_im = mag * jnp.sin(lam_im * dt)
    den = lam_re * lam_re + lam_im * lam_im
    nr = ab_re - 1.0
    f_re = (nr * lam_re + ab_im * lam_im) / den
    f_im = (ab_im * lam_re - nr * lam_im) / den
    bb_re = f_re[..., None] * b_re - f_im[..., None] * b_im
    bb_im = f_re[..., None] * b_im + f_im[..., None] * b_re
    return ab_re, ab_im, bb_re, bb_im


_EYE8 = np.eye(8, dtype=np.float32)


def _b_tiles(bb):
    t = bb.transpose(0, 2, 1).reshape(8, 8, SSM_H, SSM_P)
    return jnp.einsum('kghp,gG->kghGp', t, _EYE8).reshape(8, 8 * SSM_H, 8 * SSM_P)


def _b_untile(d):
    t = jnp.einsum('kghGp,gG->kghp', d.reshape(8, 8, SSM_H, 8, SSM_P), _EYE8)
    return t.reshape(SSM_G, SSM_H, SSM_P).transpose(0, 2, 1)


def _c_tiles(c):
    t = c.transpose(0, 2, 1).reshape(8, 8, SSM_P, SSM_H)
    return jnp.einsum('kgph,gG->kgpGh', t, _EYE8).reshape(8, 8 * SSM_P, 8 * SSM_H)


def _c_untile(d):
    t = jnp.einsum('kgpGh,gG->kgph', d.reshape(8, 8, SSM_P, 8, SSM_H), _EYE8)
    return t.reshape(SSM_G, SSM_P, SSM_H).transpose(0, 2, 1)


def _powers(ar, ai, T):
    pr, pi = ar, ai
    while pr.shape[0] < T:
        lr, li = pr[-1:], pi[-1:]
        pr, pi = (jnp.concatenate([pr, pr * lr - pi * li], 0), jnp.concatenate([pi, pr * li + pi * lr], 0))
    return pr, pi


def layer_a_fwd(h, w, p):
    L = h.shape[0]
    proj = mm(h, w['a_w_in'], 'nn', 'a_proj')
    disc = lambda *a: s5_discretize(*a)
    (ab_re, ab_im, bb_re, bb_im), disc_vjp = jax.vjp(disc, p['a_lam_re'][0], p['a_lam_im'][0], p['a_log_dt'][0],
                                                     p['a_b_re'][0], p['a_b_im'][0])
    Bre, Bim = _b_tiles(bb_re), _b_tiles(bb_im)
    Cre, Cim = _c_tiles(p['a_c_re'][0]), -_c_tiles(p['a_c_im'][0])
    T = min(SSM_T, L)
    pr, pi = _powers(ab_re.reshape(1, -1), ab_im.reshape(1, -1), T)
    bu_re = tmm(proj, Bre, 'nn', 'a_bu_re')
    bu_im = tmm(proj, Bim, 'nn', 'a_bu_im')
    s_re, s_im = s5_scan(bu_re, bu_im, pr, pi, 'a_scan')
    y_re = tmm(s_re, Cre, 'nn', 'a_y_re')
    y_im = tmm(s_im, Cim, 'nn', 'a_y_im')

    def f1(u, yre, yim, dsk):
        y = yre + yim + dsk * u
        return [y, gelu(y)], []
    (y, yg), _ = rowwise(f1, [rw(proj, 1024, 0), rw(y_re), rw(y_im)], [p['a_d']], [(1024, F32)] * 2, [], 256, 'a_gelu')
    gl = mm(yg, w['a_w_glu'], 'nn', 'a_glu')

    def f2(yg_, gl_, z, bg):
        return [yg_ * sigmoid(gl_ + bg) * silu(z)], []
    (po,), _ = rowwise(f2, [rw(yg), rw(gl), rw(proj, 1024, 1)], [p['a_b_glu']], [(1024, F32)], [], 256, 'a_gate')
    yb = mm(po, w['a_w_out'], 'nn', 'a_out')
    saved = dict(h=h, proj=proj, disc_vjp=disc_vjp, Bre=Bre, Bim=Bim, Cre=Cre, Cim=Cim, pr=pr, pi=pi, s_re=s_re,
                 s_im=s_im, y=y, yg=yg, gl=gl, po=po)
    return yb, saved


def layer_a_bwd(dyb, w, p, sv):
    g = {}
    dpo = mm(dyb, w['a_w_out'], 'nt', 'a_dpo')
    g['a_w_out'] = mm(sv['po'], dyb, 'tn', 'a_dwout')
    proj = sv['proj']

    def f1(dpo_, yg, gl, z, bg):
        sg = sigmoid(gl + bg)
        sz = silu(z)
        dm = dpo_ * sz
        dz = dpo_ * (yg * sg) * silu_grad(z)
        dgl = dm * yg * sg * (1.0 - sg)
        return [dz, dm * sg, dgl], [jnp.sum(dgl, axis=0, keepdims=True)]
    (dz, dyg1, dgl), (db_glu,) = rowwise(f1, [rw(dpo), rw(sv['yg']), rw(sv['gl']), rw(proj, 1024, 1)], [p['a_b_glu']],
                                          [(1024, F32)] * 3, [(1, 1024)], 256, 'a_gate_bwd')
    g['a_b_glu'] = db_glu
    g['a_w_glu'] = mm(sv['yg'], dgl, 'tn', 'a_dwglu')
    dyg2 = mm(dgl, w['a_w_glu'], 'nt', 'a_dyg2')

    def f2(dyg1_, dyg2_, y, u, dsk):
        dy = (dyg1_ + dyg2_) * gelu_grad(y)
        return [dy, dy * dsk], [jnp.sum(dy * u, axis=0, keepdims=True)]
    (dy, du1), (dd,) = rowwise(f2, [rw(dyg1), rw(dyg2), rw(sv['y']), rw(proj, 1024, 0)], [p['a_d']],
                               [(1024, F32)] * 2, [(1, 1024)], 256, 'a_gelu_bwd')
    g['a_d'] = dd
    ds_re = tmm(dy, sv['Cre'], 'nt', 'a_ds_re')
    ds_im = tmm(dy, sv['Cim'], 'nt', 'a_ds_im')
    dCre = tmm(sv['s_re'], dy, 'tn', 'a_dcre', wa=512, wb=128)
    dCim = -tmm(sv['s_im'], dy, 'tn', 'a_dcim', wa=512, wb=128)
    g_re, g_im, da_re, da_im = s5_scan(ds_re, ds_im, sv['pr'][::-1], sv['pi'][::-1], 'a_scan_rev', reverse=True,
                                       s_re=sv['s_re'], s_im=sv['s_im'])
    dBre = tmm(proj, g_re, 'tn', 'a_dbre', wa=128, wb=512)
    dBim = tmm(proj, g_im, 'tn', 'a_dbim', wa=128, wb=512)
    du2a = tmm(g_re, sv['Bre'], 'nt', 'a_du_re')
    du2b = tmm(g_im, sv['Bim'], 'nt', 'a_du_im')

    def f3(a, b, c, dz_):
        return [jnp.concatenate([a + b + c, dz_], axis=1)], []
    (dproj,), _ = rowwise(f3, [rw(du1), rw(du2a), rw(du2b), rw(dz)], [], [(2048, F32)], [], 256, 'a_dproj')
    dlr, dli, dldt, dbr, dbi = sv['disc_vjp']((da_re.reshape(SSM_G, SSM_P), da_im.reshape(SSM_G, SSM_P),
                                               _b_untile(dBre), _b_untile(dBim)))
    g['a_lam_re'], g['a_lam_im'], g['a_log_dt'] = dlr[None], dli[None], dldt[None]
    g['a_b_re'], g['a_b_im'] = dbr[None], dbi[None]
    g['a_c_re'], g['a_c_im'] = _c_untile(dCre)[None], _c_untile(dCim)[None]
    g['a_w_in'] = mm(sv['h'], dproj, 'tn', 'a_dwin')
    dh = mm(dproj, w['a_w_in'], 'nt', 'a_dh')
    return dh, g


def _t5_bucket_np():
    qi = np.arange(WINDOW)[:, None]
    kj = np.arange(2 * WINDOW)[None, :]
    dist = np.maximum(qi + WINDOW - kj, 0)
    max_exact = REL_BUCKETS // 2
    dist_f = np.maximum(dist, 1).astype(np.float32)
    large = max_exact + (np.log(dist_f / np.float32(max_exact)) / np.float32(math.log(REL_MAX_DIST / max_exact))
                         * np.float32(REL_BUCKETS - max_exact)).astype(np.int32)
    large = np.minimum(large, REL_BUCKETS - 1)
    return np.where(dist < max_exact, dist, large).astype(np.int32)


def _swa_probs(q, kb, bias_h, sink, valid):
    s = lax.dot_general(q, kb, (((1,), (1,)), ((), ())), preferred_element_type=F32) * (HEAD_DIM ** -0.5)
    s = jnp.where(valid, s + bias_h, NEG_INF)
    m = jnp.maximum(jnp.max(s, axis=-1, keepdims=True), sink)
    e = jnp.exp(s - m)
    es = jnp.exp(sink - m)
    den = jnp.sum(e, axis=-1, keepdims=True) + es
    return e / den, es / den


def _swa_valid(n):
    qi = lax.broadcasted_iota(jnp.int32, (WINDOW, 2 * WINDOW), 0)
    kj = lax.broadcasted_iota(jnp.int32, (WINDOW, 2 * WINDOW), 1)
    dist = qi + WINDOW - kj
    return (dist >= 0) & (dist < WINDOW) & ((kj >= WINDOW) | (n > 0))


def swa_fwd(proj, bias, sinks):
    L = proj.shape[0]

    def body(z_ref, q_ref, kvc_ref, kvp_ref, bias_ref, sink_ref, o_ref, po_ref):
        n = pl.program_id(0)
        valid = _swa_valid(n)
        q, kvc, kvp = q_ref[...], kvc_ref[...], kvp_ref[...]
        outs = []
        for kvh in range(SWA_KV):
            kb = jnp.concatenate([kvp[:, kvh * 64:(kvh + 1) * 64], kvc[:, kvh * 64:(kvh + 1) * 64]], 0).astype(BF16)
            vb = jnp.concatenate([kvp[:, 128 + kvh * 64:128 + (kvh + 1) * 64],
                                  kvc[:, 128 + kvh * 64:128 + (kvh + 1) * 64]], 0).astype(BF16)
            for gi in range(SWA_HEADS // SWA_KV):
                h = kvh * 8 + gi
                p, _ = _swa_probs(q[:, h * 64:(h + 1) * 64].astype(BF16), kb, bias_ref[h], sink_ref[0:1, h:h + 1], valid)
                outs.append(jnp.dot(p.astype(BF16), vb, preferred_element_type=F32))
        o = jnp.concatenate(outs, axis=1)
        o_ref[...] = o
        po_ref[...] = o * silu(z_ref[...])

    return pl.pallas_call(
        body, grid=(L // WINDOW,),
        in_specs=[pl.BlockSpec((WINDOW, 1024), lambda n: (n, 0)), pl.BlockSpec((WINDOW, 1024), lambda n: (n, 1)),
                  pl.BlockSpec((WINDOW, 256), lambda n: (n, 8)),
                  pl.BlockSpec((WINDOW, 256), lambda n: (jnp.maximum(n - 1, 0), 8)),
                  pl.BlockSpec((SWA_HEADS, WINDOW, 2 * WINDOW), lambda n: (0, 0, 0)),
                  pl.BlockSpec((1, SWA_HEADS), lambda n: (0, 0))],
        out_specs=[pl.BlockSpec((WINDOW, 1024), lambda n: (n, 0))] * 2,
        out_shape=[jax.ShapeDtypeStruct((L, 1024), F32)] * 2,
        compiler_params=pltpu.CompilerParams(dimension_semantics=("parallel",)), name='b_attn')(
            proj, proj, proj, proj, bias, sinks)


def swa_bwd(proj, do, bias, sinks):
    L = proj.shape[0]

    def body(q_ref, kvc_ref, kvp_ref, do_ref, bias_ref, sink_ref, dq_ref, dkv_ref, dbias_ref, dsink_ref):
        n = pl.program_id(0)

        @pl.when(n == 0)
        def _():
            dkv_ref[...] = jnp.zeros_like(dkv_ref)
            dbias_ref[...] = jnp.zeros_like(dbias_ref)
            dsink_ref[...] = jnp.zeros_like(dsink_ref)

        valid = _swa_valid(n)
        q, kvc, kvp, do_ = q_ref[...], kvc_ref[...], kvp_ref[...], do_ref[...]
        dqs, dks, dvs, dsk = [], [], [], []
        for kvh in range(SWA_KV):
            kb = jnp.concatenate([kvp[:, kvh * 64:(kvh + 1) * 64], kvc[:, kvh * 64:(kvh + 1) * 64]], 0).astype(BF16)
            vb = jnp.concatenate([kvp[:, 128 + kvh * 64:128 + (kvh + 1) * 64],
                                  kvc[:, 128 + kvh * 64:128 + (kvh + 1) * 64]], 0).astype(BF16)
            dk = jnp.zeros((2 * WINDOW, 64), F32)
            dv = jnp.zeros((2 * WINDOW, 64), F32)
            for gi in range(SWA_HEADS // SWA_KV):
                h = kvh * 8 + gi
                qh = q[:, h * 64:(h + 1) * 64].astype(BF16)
                doh = do_[:, h * 64:(h + 1) * 64].astype(BF16)
                p, ps = _swa_probs(qh, kb, bias_ref[h], sink_ref[0:1, h:h + 1], valid)
                dp = lax.dot_general(doh, vb, (((1,), (1,)), ((), ())), preferred_element_type=F32)
                delta = jnp.sum(p * dp, axis=-1, keepdims=True)
                ds = p * (dp - delta)
                dsk.append(jnp.sum(-ps * delta, axis=0, keepdims=True))
                dbias_ref[h] += ds
                dsb = (ds * (HEAD_DIM ** -0.5)).astype(BF16)
                dqs.append(jnp.dot(dsb, kb, preferred_element_type=F32))
                dk = dk + lax.dot_general(dsb, qh, (((0,), (0,)), ((), ())), preferred_element_type=F32)
                dv = dv + lax.dot_general(p.astype(BF16), doh, (((0,), (0,)), ((), ())), preferred_element_type=F32)
            dks.append(dk)
            dvs.append(dv)
        dq_ref[...] = jnp.concatenate(dqs, axis=1)
        dsink_ref[...] += jnp.concatenate(dsk, axis=1)
        both = jnp.concatenate(dks + dvs, axis=1)
        r_cur = pl.multiple_of(n * WINDOW, WINDOW)
        r_prev = pl.multiple_of(jnp.maximum(n - 1, 0) * WINDOW, WINDOW)
        dkv_ref[pl.ds(r_prev, WINDOW), :] += both[:WINDOW]
        dkv_ref[pl.ds(r_cur, WINDOW), :] += both[WINDOW:]

    return pl.pallas_call(
        body, grid=(L // WINDOW,),
        in_specs=[pl.BlockSpec((WINDOW, 1024), lambda n: (n, 1)), pl.BlockSpec((WINDOW, 256), lambda n: (n, 8)),
                  pl.BlockSpec((WINDOW, 256), lambda n: (jnp.maximum(n - 1, 0), 8)),
                  pl.BlockSpec((WINDOW, 1024), lambda n: (n, 0)),
                  pl.BlockSpec((SWA_HEADS, WINDOW, 2 * WINDOW), lambda n: (0, 0, 0)),
                  pl.BlockSpec((1, SWA_HEADS), lambda n: (0, 0))],
        out_specs=[pl.BlockSpec((WINDOW, 1024), lambda n: (n, 0)), pl.BlockSpec((L, 256), lambda n: (0, 0)),
                   pl.BlockSpec((SWA_HEADS, WINDOW, 2 * WINDOW), lambda n: (0, 0, 0)),
                   pl.BlockSpec((1, SWA_HEADS), lambda n: (0, 0))],
        out_shape=[jax.ShapeDtypeStruct((L, 1024), F32), jax.ShapeDtypeStruct((L, 256), F32),
                   jax.ShapeDtypeStruct((SWA_HEADS, WINDOW, 2 * WINDOW), F32), jax.ShapeDtypeStruct((1, SWA_HEADS), F32)],
        compiler_params=pltpu.CompilerParams(dimension_semantics=("arbitrary",)), name='b_attn_bwd')(
            proj, proj, proj, do, bias, sinks)


def swa_bias(rel_bias):
    def body(bk_ref, rb_ref, o_ref):
        bk = bk_ref[...]
        for h in range(SWA_HEADS):
            acc = jnp.zeros((WINDOW, 2 * WINDOW), F32)
            for b in range(REL_BUCKETS):
                acc = jnp.where(bk == b, rb_ref[b, h], acc)
            o_ref[h] = acc

    return pl.pallas_call(
        body, out_shape=jax.ShapeDtypeStruct((SWA_HEADS, WINDOW, 2 * WINDOW), F32),
        in_specs=[pl.BlockSpec(memory_space=pltpu.VMEM), pl.BlockSpec(memory_space=pltpu.SMEM)],
        out_specs=pl.BlockSpec(memory_space=pltpu.VMEM), name='b_bias')(jnp.asarray(_t5_bucket_np()), rel_bias)


def layer_b_fwd(h, w, p):
    proj = mm(h, w['b_w_in'], 'nn', 'b_proj')
    bias = swa_bias(p['rel_bias'])
    o, po = swa_fwd(proj, bias, p['b_sinks'])
    yb = mm(po, w['b_w_out'], 'nn', 'b_out')
    return yb, dict(h=h, proj=proj, bias=bias, o=o, po=po)


def layer_b_bwd(dyb, w, p, sv):
    g = {}
    dpo = mm(dyb, w['b_w_out'], 'nt', 'b_dpo')
    g['b_w_out'] = mm(sv['po'], dyb, 'tn', 'b_dwout')
    proj = sv['proj']

    def f1(dpo_, o, z):
        return [dpo_ * silu(z), dpo_ * o * silu_grad(z)], []
    (do, dz), _ = rowwise(f1, [rw(dpo), rw(sv['o']), rw(proj, 1024, 0)], [], [(1024, F32)] * 2, [], 256, 'b_gate_bwd')
    dq, dkv, dbias, dsinks = swa_bwd(proj, do, sv['bias'], p['b_sinks'])
    g['b_sinks'] = dsinks
    onehot = jnp.asarray(np.eye(REL_BUCKETS, dtype=np.float32)[_t5_bucket_np().reshape(-1)])

    def f2(db, oh):
        return [], [lax.dot_general(db, oh, (((1,), (0,)), ((), ())), preferred_element_type=F32,
                                    precision=lax.Precision.HIGHEST)]
    _, (drel,) = rowwise(f2, [(dbias.reshape(SWA_HEADS, -1), pl.BlockSpec((SWA_HEADS, 4096), lambda i: (0, i))),
                              (onehot, pl.BlockSpec((4096, REL_BUCKETS), lambda i: (i, 0)))], [], [],
                         [(SWA_HEADS, REL_BUCKETS)], 4096, 'b_drel', n_steps=(2 * WINDOW * WINDOW) // 4096)
    g['rel_bias'] = drel.T

    def f3(dz_, dq_, dkv_):
        return [jnp.concatenate([dz_, dq_, dkv_], axis=1)], []
    (dproj,), _ = rowwise(f3, [rw(dz), rw(dq), rw(dkv)], [], [(2304, F32)], [], 256, 'b_dproj')
    g['b_w_in'] = mm(sv['h'], dproj, 'tn', 'b_dwin')
    dh = mm(dproj, w['b_w_in'], 'nt', 'b_dh')
    return dh, g


MLA_SCALE = (MLA_NOPE + MLA_ROPE) ** -0.5


def _rope_tables(L):
    inv = ROPE_BASE ** (-jnp.arange(0, MLA_ROPE, 2, dtype=F32) / MLA_ROPE)
    ang = jnp.arange(L, dtype=F32)[:, None] * inv[None, :]
    c, s = jnp.cos(ang), jnp.sin(ang)
    one, zero, pad = jnp.ones((L, 128), F32), jnp.zeros((L, 128), F32), jnp.zeros((L, 64), F32)
    return (jnp.concatenate([one, c, c, c, c, pad], 1), jnp.concatenate([zero, s, s, s, s, pad], 1))


def _rot(x, transpose=False):
    w = x.shape[1]
    lane = lax.broadcasted_iota(jnp.int32, x.shape, 1)
    up = pltpu.roll(x, w - 16, 1)
    dn = pltpu.roll(x, 16, 1)
    first = (lane % 32) < 16
    return jnp.where(first, up, -dn) if transpose else jnp.where(first, -up, dn)


MLA_QT = 512


def _mla_exp(qf, kf, t, qt):
    n_k = kf.shape[0]
    s = lax.dot_general(qf, kf, (((1,), (1,)), ((), ())), preferred_element_type=F32) * MLA_SCALE
    qpos = t * qt + lax.broadcasted_iota(jnp.int32, (qt, n_k), 0)
    kpos = lax.broadcasted_iota(jnp.int32, (qt, n_k), 1)
    s = jnp.where(kpos <= qpos, s, NEG_INF)
    e = jnp.exp(s - jnp.max(s, axis=-1, keepdims=True))
    return e, jnp.sum(e, axis=-1, keepdims=True)


def _mla_heads(q, kv, kr):
    out = []
    for j in range(2):
        qf = jnp.concatenate([q[:, j * 64:(j + 1) * 64], q[:, 128 + j * 32:128 + (j + 1) * 32]], axis=1)
        kf = jnp.concatenate([kv[:, j * 64:(j + 1) * 64], kr], axis=1)
        out.append((qf, kf, kv[:, 128 + j * 64:128 + (j + 1) * 64]))
    return out


def mla_fwd(q, kv, kr):
    L = q.shape[0]
    qt = min(MLA_QT, L)
    nq = L // qt

    def body(q_ref, kv_ref, kr_ref, o_ref):
        for t in range(nq):
            @pl.when(pl.program_id(1) == t)
            def _(t=t):
                n_k = (t + 1) * qt
                outs = []
                for qf, kf, v in _mla_heads(q_ref[...], kv_ref[0:n_k, :], kr_ref[0:n_k, 0:MLA_ROPE]):
                    e, den = _mla_exp(qf, kf, t, qt)
                    outs.append(jnp.dot(e.astype(BF16), v, preferred_element_type=F32) / den)
                o_ref[...] = jnp.concatenate(outs, axis=1)

    return pl.pallas_call(
        body, grid=(MLA_HEADS // 2, nq),
        in_specs=[pl.BlockSpec((qt, 256), lambda hp, n: (n, hp)), pl.BlockSpec((L, 256), lambda hp, n: (0, hp)),
                  pl.BlockSpec((L, 128), lambda hp, n: (0, 0))],
        out_specs=pl.BlockSpec((qt, 128), lambda hp, n: (n, hp)), out_shape=jax.ShapeDtypeStruct((L, 1024), F32),
        compiler_params=pltpu.CompilerParams(dimension_semantics=("parallel", "parallel")), name='c_attn')(q, kv, kr)


def mla_bwd(q, kv, kr, do):
    L = q.shape[0]
    qt = min(MLA_QT, L)
    nq = L // qt

    def body(q_ref, kv_ref, kr_ref, do_ref, dq_ref, dkv_ref, dkr_ref):
        @pl.when(pl.program_id(1) == 0)
        def _():
            dkv_ref[...] = jnp.zeros_like(dkv_ref)
            dkr_ref[...] = jnp.zeros_like(dkr_ref)

        for t in range(nq):
            @pl.when(pl.program_id(1) == t)
            def _(t=t):
                n_k = (t + 1) * qt
                do_ = do_ref[...]
                dqn, dqr, dkn, dvs = [], [], [], []
                dkr = jnp.zeros((n_k, MLA_ROPE), F32)
                for j, (qf, kf, v) in enumerate(_mla_heads(q_ref[...], kv_ref[0:n_k, :], kr_ref[0:n_k, 0:MLA_ROPE])):
                    doh = do_[:, j * 64:(j + 1) * 64]
                    e, den = _mla_exp(qf, kf, t, qt)
                    p = e * (1.0 / den)
                    dp = lax.dot_general(doh, v, (((1,), (1,)), ((), ())), preferred_element_type=F32)
                    ds = (p * (dp - jnp.sum(p * dp, axis=-1, keepdims=True)) * MLA_SCALE).astype(BF16)
                    dqf = jnp.dot(ds, kf, preferred_element_type=F32)
                    dkf = lax.dot_general(ds, qf, (((0,), (0,)), ((), ())), preferred_element_type=F32)
                    dvs.append(lax.dot_general(p.astype(BF16), doh, (((0,), (0,)), ((), ())), preferred_element_type=F32))
                    dqn.append(dqf[:, :MLA_NOPE])
                    dqr.append(dqf[:, MLA_NOPE:])
                    dkn.append(dkf[:, :MLA_NOPE])
                    dkr = dkr + dkf[:, MLA_NOPE:]
                dq_ref[...] = jnp.concatenate(dqn + dqr + [jnp.zeros((qt, 64), F32)], axis=1)
                dkv_ref[0:n_k, :] += jnp.concatenate(dkn + dvs, axis=1)
                dkr_ref[0, 0:n_k, :] += jnp.concatenate([dkr, jnp.zeros((n_k, 128 - MLA_ROPE), F32)], axis=1)

    return pl.pallas_call(
        body, grid=(MLA_HEADS // 2, nq),
        in_specs=[pl.BlockSpec((qt, 256), lambda hp, n: (n, hp)), pl.BlockSpec((L, 256), lambda hp, n: (0, hp)),
                  pl.BlockSpec((L, 128), lambda hp, n: (0, 0)), pl.BlockSpec((qt, 128), lambda hp, n: (n, hp))],
        out_specs=[pl.BlockSpec((qt, 256), lambda hp, n: (n, hp)), pl.BlockSpec((L, 256), lambda hp, n: (0, hp)),
                   pl.BlockSpec((1, L, 128), lambda hp, n: (hp, 0, 0))],
        out_shape=[jax.ShapeDtypeStruct((L, 2048), F32), jax.ShapeDtypeStruct((L, 2048), F32),
                   jax.ShapeDtypeStruct((MLA_HEADS // 2, L, 128), F32)],
        compiler_params=pltpu.CompilerParams(dimension_semantics=("parallel", "arbitrary")), name='c_attn_bwd')(
            q, kv, kr, do)


def _perm_c_w_in(wf):
    return jnp.concatenate([wf[:, 1056:], wf[:, :1056], jnp.zeros((wf.shape[0], 96), wf.dtype)], axis=1)


def _unperm_c_w_in(d):
    return jnp.concatenate([d[:, 1024:2080], d[:, :1024]], axis=1)


def _perm_w_uq(wf):
    t = wf.reshape(wf.shape[0], 8, 2, 96)
    nope = t[..., :64].reshape(-1, 8, 128)
    rope = t[..., 64:].reshape(-1, 8, 64)
    return jnp.concatenate([nope, rope, jnp.zeros_like(rope)], axis=2).reshape(-1, 2048)


def _unperm_w_uq(d):
    t = d.reshape(d.shape[0], 8, 256)
    nope = t[..., :128].reshape(-1, 8, 2, 64)
    rope = t[..., 128:192].reshape(-1, 8, 2, 32)
    return jnp.concatenate([nope, rope], axis=3).reshape(-1, 1536)


def _perm_w_ukv(wf):
    return wf.reshape(-1, 8, 2, 2, 64).transpose(0, 1, 3, 2, 4).reshape(-1, 2048)


def _unperm_w_ukv(d):
    return d.reshape(-1, 8, 2, 2, 64).transpose(0, 1, 3, 2, 4).reshape(-1, 2048)


def layer_c_fwd(h, w, p):
    L = h.shape[0]
    proj = mm(h, w['c_w_in'], 'nn', 'c_proj')

    def f1(c, gq, gk):
        return [rms_fwd(c[:, :768], gq), rms_fwd(c[:, 768:], gk)], []
    (cqn, ckvn), _ = rowwise(f1, [rw(proj, 1024, 1)], [p['c_q_norm'], p['c_kv_norm']], [(768, BF16), (256, BF16)], [],
                             256, 'c_norms')
    qf = mm(cqn, w['c_w_uq'], 'nn', 'c_uq')
    kvf = mm(ckvn, w['c_w_ukv'], 'nn', 'c_ukv', out_dtype=BF16)
    cos, sin = _rope_tables(L)

    def f2(q_, kr_, c, s):
        c8, s8 = jnp.tile(c, (1, 8)), jnp.tile(s, (1, 8))
        return [q_ * c8 + _rot(q_) * s8, kr_ * c[:, 128:] + _rot(kr_) * s[:, 128:]], []
    (q, kr), _ = rowwise(f2, [rw(qf), rw(proj, 128, 16), rw(cos), rw(sin)], [], [(2048, BF16), (128, BF16)], [], 256,
                         'c_rope')
    o = mla_fwd(q, kvf, kr)

    def f3(o_, z):
        return [o_ * silu(z)], []
    (po,), _ = rowwise(f3, [rw(o), rw(proj, 1024, 0)], [], [(1024, F32)], [], 256, 'c_gate')
    yb = mm(po, w['c_w_out'], 'nn', 'c_out')
    return yb, dict(h=h, proj=proj, cqn=cqn, ckvn=ckvn, q=q, kv=kvf, kr=kr, o=o, po=po, cos=cos, sin=sin)


def layer_c_bwd(dyb, w, p, sv):
    g = {}
    dpo = mm(dyb, w['c_w_out'], 'nt', 'c_dpo')
    g['c_w_out'] = mm(sv['po'], dyb, 'tn', 'c_dwout')
    proj = sv['proj']
    L = proj.shape[0]

    def f1(dpo_, o, z):
        return [dpo_ * silu(z), dpo_ * o * silu_grad(z)], []
    (do, dz), _ = rowwise(f1, [rw(dpo), rw(sv['o']), rw(proj, 1024, 0)], [], [(1024, BF16), (1024, F32)], [], 256,
                          'c_gate_bwd')
    dq, dkvf, dkr8 = mla_bwd(sv['q'], sv['kv'], sv['kr'], do)

    def f2(dq_, dkr_, c, s):
        c8, s8 = jnp.tile(c, (1, 8)), jnp.tile(s, (1, 8))
        dk = jnp.sum(dkr_, axis=0)
        return [dq_ * c8 + _rot(dq_ * s8, True), dk * c[:, 128:] + _rot(dk * s[:, 128:], True)], []
    tl = 256
    (dqf, dkr), _ = rowwise(f2, [rw(dq), (dkr8, pl.BlockSpec((8, tl, 128), lambda i: (0, i, 0))), rw(sv['cos']),
                                 rw(sv['sin'])], [], [(2048, F32), (128, F32)], [], tl, 'c_rope_bwd')
    g['c_w_uq'] = mm(sv['cqn'], dqf, 'tn', 'c_dwuq')
    g['c_w_ukv'] = mm(sv['ckvn'], dkvf, 'tn', 'c_dwukv')
    dcqn = mm(dqf, w['c_w_uq'], 'nt', 'c_dcqn')
    dckvn = mm(dkvf, w['c_w_ukv'], 'nt', 'c_dckvn')

    def f3(c, dq_, dk_, dz_, dkr_, gq, gk):
        dcq, dgq = rms_bwd(c[:, :768], gq, dq_)
        dckv, dgk = rms_bwd(c[:, 768:], gk, dk_)
        return [jnp.concatenate([dz_, dcq, dckv, dkr_], axis=1)], [dgq, dgk]
    (dproj,), (dgq, dgk) = rowwise(f3, [rw(proj, 1024, 1), rw(dcqn), rw(dckvn), rw(dz), rw(dkr)],
                                   [p['c_q_norm'], p['c_kv_norm']], [(2176, F32)], [(1, 768), (1, 256)], 256, 'c_dproj')
    g['c_q_norm'], g['c_kv_norm'] = dgq, dgk
    g['c_w_in'] = mm(sv['h'], dproj, 'tn', 'c_dwin')
    dh = mm(dproj, w['c_w_in'], 'nt', 'c_dh')
    return dh, g


def _sgu_mix(wm, v, transpose):
    outs = []
    dims = (((0,), (0,)), ((), ())) if transpose else (((1,), (0,)), ((), ()))
    for gi in range(SGU_G):
        outs.append(lax.dot_general(wm[gi], v[:, gi * SGU_C:(gi + 1) * SGU_C].astype(BF16), dims,
                                    preferred_element_type=F32))
    return jnp.concatenate(outs, axis=1)


def _sgu_wmask(ws):
    t = lax.broadcasted_iota(jnp.int32, (SGU_T, SGU_T), 0)
    s = lax.broadcasted_iota(jnp.int32, (SGU_T, SGU_T), 1)
    return jnp.where((s <= t)[None], ws, 0.0).astype(BF16)


def _ln_stats(v):
    mu = jnp.mean(v, axis=-1, keepdims=True)
    vc = v - mu
    rstd = lax.rsqrt(jnp.mean(vc * vc, axis=-1, keepdims=True) + EPS)
    return vc * rstd, rstd


def layer_d_fwd(h, w, p):
    proj = mm(h, w['d_w_in'], 'nn', 'd_proj')
    bias = jnp.repeat(p['d_b_s'][0].T, SGU_C, axis=1)

    def f1(u_, v_, z, ws, lg, lb, bs):
        xh, _ = _ln_stats(gelu(v_))
        s = _sgu_mix(_sgu_wmask(ws), xh * lg + lb, False) + bs
        return [gelu(u_) * s * silu(z)], []
    (po,), _ = rowwise(f1, [rw(proj, 1024, 0), rw(proj, 1024, 1), rw(proj, 1024, 2)],
                       [p['d_w_s'][0], p['d_ln_g'], p['d_ln_b'], bias], [(1024, F32)], [], SGU_T, 'd_mix')
    yb = mm(po, w['d_w_out'], 'nn', 'd_out')
    return yb, dict(h=h, proj=proj, po=po, bias=bias)


def layer_d_bwd(dyb, w, p, sv):
    g = {}
    dpo = mm(dyb, w['d_w_out'], 'nt', 'd_dpo')
    g['d_w_out'] = mm(sv['po'], dyb, 'tn', 'd_dwout')
    proj = sv['proj']

    def f1(dpo_, u_, v_, z, ws, lg, lb, bs):
        wm = _sgu_wmask(ws)
        gv = gelu(v_)
        xh, rstd = _ln_stats(gv)
        vn = xh * lg + lb
        s = _sgu_mix(wm, vn, False) + bs
        gu, sz = gelu(u_), silu(z)
        du = dpo_ * s * sz
        ds = dpo_ * gu * sz
        dz = dpo_ * gu * s * silu_grad(z)
        dsb = ds.astype(BF16)
        dws = jnp.stack([lax.dot_general(dsb[:, gi * SGU_C:(gi + 1) * SGU_C], vn[:, gi * SGU_C:(gi + 1) * SGU_C].astype(BF16),
                                         (((1,), (1,)), ((), ())), preferred_element_type=F32) for gi in range(SGU_G)])
        dvn = _sgu_mix(wm, ds, True)
        dlg = jnp.sum(dvn * xh, axis=0, keepdims=True)
        dlb = jnp.sum(dvn, axis=0, keepdims=True)
        dxh = dvn * lg
        dgv = rstd * (dxh - jnp.mean(dxh, axis=-1, keepdims=True) - xh * jnp.mean(dxh * xh, axis=-1, keepdims=True))
        return ([jnp.concatenate([du * gelu_grad(u_), dgv * gelu_grad(v_), dz], axis=1)], [dws, ds, dlg, dlb])
    (dproj,), (dws, dbs, dlg, dlb) = rowwise(
        f1, [rw(dpo), rw(proj, 1024, 0), rw(proj, 1024, 1), rw(proj, 1024, 2)],
        [p['d_w_s'][0], p['d_ln_g'], p['d_ln_b'], sv['bias']], [(3072, F32)],
        [(SGU_G, SGU_T, SGU_T), (SGU_T, 1024), (1, 1024), (1, 1024)], SGU_T, 'd_mix_bwd')
    tril = np.tril(np.ones((SGU_T, SGU_T), dtype=bool))
    g['d_w_s'] = jnp.where(tril[None], dws, 0.0)[None]
    g['d_b_s'] = dbs.reshape(SGU_T, SGU_G, SGU_C).sum(-1).T[None]
    g['d_ln_g'], g['d_ln_b'] = dlg, dlb
    g['d_w_in'] = mm(sv['h'], dproj, 'tn', 'd_dwin')
    dh = mm(dproj, w['d_w_in'], 'nt', 'd_dh')
    return dh, g


def _coords():
    return lax.axis_index("x"), lax.axis_index("y"), lax.axis_index("c")


def all_gather(x, name):
    def body(x_ref, out_ref, send_sems, recv_sems, local_sem):
        x_, y_, c_ = _coords()
        me, sibling = (x_, y_, c_), (x_, y_, 1 - c_)
        chips = [(1 - x_, y_), (x_, 1 - y_), (1 - x_, 1 - y_)]

        def slot(px, py, pc):
            return out_ref.at[4 * px + 2 * py + pc]

        def copy(k, block, to, src=None):
            return pltpu.make_async_remote_copy(src_ref=slot(*block) if src is None else src, dst_ref=slot(*block),
                                                send_sem=send_sems.at[k], recv_sem=recv_sems.at[k], device_id=to,
                                                device_id_type=MESH)

        mine = pltpu.make_async_copy(x_ref, slot(*me), local_sem)
        mine.start()
        first = [copy(0, me, sibling, src=x_ref)]
        first += [copy(1 + j, me, (*chip, c_), src=x_ref) for j, chip in enumerate(chips)]
        for cp in first:
            cp.start()
        passed = [copy(4 + j, (*chip, c_), sibling) for j, chip in enumerate(chips)]
        for j, chip in enumerate(chips):
            copy(1 + j, (*chip, c_), me).wait_recv()
            passed[j].start()
        copy(0, sibling, me).wait_recv()
        for j, chip in enumerate(chips):
            copy(4 + j, (*chip, 1 - c_), me).wait_recv()
        for cp in first + passed:
            cp.wait_send()
        mine.wait()

    return pl.pallas_call(
        body, out_shape=jax.ShapeDtypeStruct((N_DEV,) + x.shape, x.dtype), in_specs=[ANY], out_specs=ANY,
        scratch_shapes=[pltpu.SemaphoreType.DMA((7,)), pltpu.SemaphoreType.DMA((7,)), pltpu.SemaphoreType.DMA(())],
        name=name)(x)


def rs_sibling(gfull):
    _, R, C = gfull.shape

    def body(g_ref, land_ref, send_sems, recv_sems):
        x_, y_, c_ = _coords()
        copies = []
        for k in range(4):
            cp = pltpu.make_async_remote_copy(src_ref=g_ref.at[2 * k + 1 - c_], dst_ref=land_ref.at[k],
                                              send_sem=send_sems.at[k], recv_sem=recv_sems.at[k],
                                              device_id=(x_, y_, 1 - c_), device_id_type=MESH)
            cp.start()
            copies.append(cp)
        for cp in copies:
            cp.wait_recv()
        for cp in copies:
            cp.wait_send()

    return pl.pallas_call(
        body, out_shape=jax.ShapeDtypeStruct((4, R, C), gfull.dtype), in_specs=[ANY], out_specs=ANY,
        scratch_shapes=[pltpu.SemaphoreType.DMA((4,)), pltpu.SemaphoreType.DMA((4,))], name='rs_sibling')(gfull)


def rs_pair_add(gfull, land, core):
    _, R, C = gfull.shape
    tl = RS_TL

    def body(c_ref, g_ref, l_ref, o_ref):
        o_ref[...] = (g_ref[...] + l_ref[...]).astype(BF16)

    return pl.pallas_call(
        body, out_shape=jax.ShapeDtypeStruct((4, R, C), BF16),
        grid_spec=pltpu.PrefetchScalarGridSpec(
            num_scalar_prefetch=1, grid=(4, R // tl),
            in_specs=[pl.BlockSpec((1, tl, C), lambda k, i, c: (2 * k + c[0], i, 0)),
                      pl.BlockSpec((1, tl, C), lambda k, i, c: (k, i, 0))],
            out_specs=pl.BlockSpec((1, tl, C), lambda k, i, c: (k, i, 0))),
        compiler_params=pltpu.CompilerParams(dimension_semantics=("parallel", "parallel")), name='rs_pair_add')(
            core, gfull, land)


def rs_chips(part):
    _, R, C = part.shape

    def body(p_ref, land_ref, send_sems, recv_sems):
        x_, y_, c_ = _coords()
        copies = []
        for r, (fx, fy) in enumerate([(1, 0), (0, 1), (1, 1)]):
            tx = jnp.where(fx == 1, 1 - x_, x_)
            ty = jnp.where(fy == 1, 1 - y_, y_)
            cp = pltpu.make_async_remote_copy(src_ref=p_ref.at[2 * tx + ty], dst_ref=land_ref.at[r],
                                              send_sem=send_sems.at[r], recv_sem=recv_sems.at[r],
                                              device_id=(tx, ty, c_), device_id_type=MESH)
            cp.start()
            copies.append(cp)
        for cp in copies:
            cp.wait_recv()
        for cp in copies:
            cp.wait_send()

    return pl.pallas_call(
        body, out_shape=jax.ShapeDtypeStruct((3, R, C), part.dtype), in_specs=[ANY], out_specs=ANY,
        scratch_shapes=[pltpu.SemaphoreType.DMA((3,)), pltpu.SemaphoreType.DMA((3,))], name='rs_chips')(part)


def _adam(wv, gv, mv, vv):
    m = ADAM_B1 * mv + (1.0 - ADAM_B1) * gv
    v = ADAM_B2 * vv + (1.0 - ADAM_B2) * (gv * gv)
    m_hat = m / (1.0 - ADAM_B1 ** ADAM_STEP)
    v_hat = v / (1.0 - ADAM_B2 ** ADAM_STEP)
    delta = -ADAM_LR * (m_hat / (jnp.sqrt(v_hat) + ADAM_EPS) + ADAM_WD * wv)
    return delta, m, v


def _sum4(p_ref, l_ref):
    return ((p_ref[0].astype(F32) + l_ref[0].astype(F32)) + l_ref[1].astype(F32)) + l_ref[2].astype(F32)


def rs_rep_sum(part, land, chip):
    def body(c_ref, p_ref, l_ref, o_ref):
        o_ref[...] = _sum4(p_ref, l_ref)

    return pl.pallas_call(
        body, out_shape=jax.ShapeDtypeStruct((REP_SLOT, LANES), F32),
        grid_spec=pltpu.PrefetchScalarGridSpec(
            num_scalar_prefetch=1, grid=(REP_SLOT // RS_TL,),
            in_specs=[pl.BlockSpec((1, RS_TL, LANES), lambda i, c: (c[0], i, 0)),
                      pl.BlockSpec((3, RS_TL, LANES), lambda i, c: (0, i, 0))],
            out_specs=pl.BlockSpec((RS_TL, LANES), lambda i, c: (i, 0))),
        compiler_params=pltpu.CompilerParams(dimension_semantics=("parallel",)), name='rs_rep')(chip, part, land)


def adam_param(name, shape, off, w, m, v, chip, part=None, land=None, grep=None):
    r, c = shape
    rp, nt, rb = _tiles(shape)
    rbw = min(r, rb)
    n_src = 2 if grep is None else 1

    def body(c_ref, *refs):
        srcs = refs[:n_src * nt]
        w_ref, m_ref, v_ref, g_ref, d_ref, nm_ref, nv_ref = refs[n_src * nt:]
        if grep is None:
            tiles = [_sum4(srcs[2 * t], srcs[2 * t + 1]) for t in range(nt)]
        else:
            tiles = [srcs[t][...] for t in range(nt)]
        g = (tiles[0] if nt == 1 else jnp.concatenate(tiles, axis=1))[:rbw, :c]
        g_ref[...] = g
        d_ref[...], nm_ref[...], nv_ref[...] = _adam(w_ref[...], g, m_ref[...], v_ref[...])

    in_specs, args = [], []
    for t in range(nt):
        b0 = (off + t * rp) // rb
        assert (off + t * rp) % rb == 0
        if grep is None:
            in_specs += [pl.BlockSpec((1, rb, LANES), functools.partial(lambda i, cr, b0: (cr[0], b0 + i, 0), b0=b0)),
                         pl.BlockSpec((3, rb, LANES), functools.partial(lambda i, cr, b0: (0, b0 + i, 0), b0=b0))]
            args += [part, land]
        else:
            in_specs.append(pl.BlockSpec((rb, LANES), functools.partial(lambda i, cr, b0: (b0 + i, 0), b0=b0)))
            args.append(grep)
    nat = pl.BlockSpec((rbw, c), lambda i, cr: (i, 0))
    return pl.pallas_call(
        body, out_shape=[jax.ShapeDtypeStruct((r, c), F32)] * 4,
        grid_spec=pltpu.PrefetchScalarGridSpec(num_scalar_prefetch=1, grid=(rp // rb,), in_specs=in_specs + [nat] * 3,
                                               out_specs=[nat] * 4),
        compiler_params=pltpu.CompilerParams(dimension_semantics=("parallel",)), name='adam_' + name)(
            chip, *args, w, m, v)


def _to_tiles(a, shape):
    r, c = shape
    rp, nt, _ = _tiles(shape)
    lead = [(0, 0)] * (a.ndim - 2)
    a = jnp.pad(a, lead + [(0, rp - r), (0, nt * LANES - c)])
    return a if nt == 1 else jnp.concatenate([a[..., t * LANES:(t + 1) * LANES] for t in range(nt)], axis=-2)


def _from_tiles(g, off, shape):
    r, c = shape
    rp, nt, _ = _tiles(shape)
    tiles = [g[..., off + t * rp:off + t * rp + r, :] for t in range(nt)]
    return (tiles[0] if nt == 1 else jnp.concatenate(tiles, axis=-1))[..., :c]


def _pack_small(blocks, order, rows, width, dtype):
    flat = jnp.concatenate([blocks[n].reshape(-1).astype(dtype) for n in order])
    return jnp.pad(flat, (0, rows * width - flat.shape[0])).reshape(rows, width)


def _device_blocks(name, gfull):
    (r, c), ax = SHARDED[name]
    br, bc = _block_shape(name)
    if ax == 0:
        return gfull.reshape(N_DEV, br, bc)
    return gfull.reshape(r, N_DEV, bc).transpose(1, 0, 2)


def _assemble(gathered, name):
    (r, c), ax = SHARDED[name]
    blk = _from_tiles(gathered, SH_OFF[name], _block_shape(name))
    return blk.reshape(r, c) if ax == 0 else blk.transpose(1, 0, 2).reshape(r, c)


def kernel(x, pre_norm, post_norm, rel_bias, a_w_in, a_lam_re, a_lam_im, a_log_dt, a_b_re, a_b_im, a_c_re, a_c_im, a_d, a_w_glu, a_b_glu, a_w_out, b_w_in, b_sinks, b_w_out, c_w_in, c_q_norm, c_kv_norm, c_w_uq, c_w_ukv, c_w_out, d_w_in, d_ln_g, d_ln_b, d_w_s, d_b_s, d_w_out, loss_target, m_pre_norm, m_post_norm, m_rel_bias, m_a_w_in, m_a_lam_re, m_a_lam_im, m_a_log_dt, m_a_b_re, m_a_b_im, m_a_c_re, m_a_c_im, m_a_d, m_a_w_glu, m_a_b_glu, m_a_w_out, m_b_w_in, m_b_sinks, m_b_w_out, m_c_w_in, m_c_q_norm, m_c_kv_norm, m_c_w_uq, m_c_w_ukv, m_c_w_out, m_d_w_in, m_d_ln_g, m_d_ln_b, m_d_w_s, m_d_b_s, m_d_w_out, v_pre_norm, v_post_norm, v_rel_bias, v_a_w_in, v_a_lam_re, v_a_lam_im, v_a_log_dt, v_a_b_re, v_a_b_im, v_a_c_re, v_a_c_im, v_a_d, v_a_w_glu, v_a_b_glu, v_a_w_out, v_b_w_in, v_b_sinks, v_b_w_out, v_c_w_in, v_c_q_norm, v_c_kv_norm, v_c_w_uq, v_c_w_ukv, v_c_w_out, v_d_w_in, v_d_ln_g, v_d_ln_b, v_d_w_s, v_d_b_s, v_d_w_out):
    loc = locals()
    P = {n: loc[n] for n in WEIGHTS}
    M = {n: loc['m_' + n] for n in WEIGHTS}
    V = {n: loc['v_' + n] for n in WEIGHTS}
    xs = x[0]
    L = xs.shape[0]

    blocks = {n: P[n].reshape(_block_shape(n)) for n in SHARDED}
    packed = jnp.concatenate([_to_tiles(blocks[n].astype(BF16), _block_shape(n)) for n in SH_ORDER], axis=0)
    gathered = all_gather(packed, 'ag_weights')
    small = all_gather(_pack_small(blocks, SHARDED_F32, SMALL_ROWS, 128, F32), 'ag_small')
    W = {n: _assemble(gathered, n) for n in SHARDED if n not in SHARDED_F32}
    Pl = dict(P)
    for n in SHARDED_F32:
        c = SHARDED[n][0][1]
        bc = c // N_DEV
        Pl[n] = small.reshape(N_DEV, -1)[:, SMALL_OFF[n]:SMALL_OFF[n] + bc].reshape(1, c)
    W['b_w_in'] = jnp.concatenate([W['b_w_in'][:, 1280:], W['b_w_in'][:, :1280]], axis=1)
    W['c_w_in'] = _perm_c_w_in(W['c_w_in'])
    W['c_w_uq'] = _perm_w_uq(W['c_w_uq'])
    W['c_w_ukv'] = _perm_w_ukv(W['c_w_ukv'])

    fwd = [layer_a_fwd, layer_b_fwd, layer_c_fwd, layer_d_fwd]
    bwd = [layer_a_bwd, layer_b_bwd, layer_c_bwd, layer_d_bwd]
    saved = []
    xc = xs
    for i in range(4):
        def fpre(x_, g_):
            return [rms_fwd(x_, g_)], []
        (h,), _ = rowwise(fpre, [rw(xc)], [P['pre_norm'][i:i + 1]], [(D_MODEL, F32)], [], 256, f'pre_norm{i}')
        yb, sv = fwd[i](h, W, Pl)

        def fpost(x_, y_, g_):
            return [x_ + rms_fwd(y_, g_)], []
        (xn,), _ = rowwise(fpost, [rw(xc), rw(yb)], [P['post_norm'][i:i + 1]], [(D_MODEL, F32)], [], 256, f'post_norm{i}')
        sv['x'], sv['yb'] = xc, yb
        saved.append(sv)
        xc = xn

    def floss(y_, t_):
        d = y_ - t_
        return [d * (1.0 / D_MODEL)], [0.5 * jnp.sum(jnp.sum(d * d, axis=-1, keepdims=True) * (1.0 / D_MODEL), axis=0,
                                                      keepdims=True)]
    (dx,), (loss_loc,) = rowwise(floss, [rw(xc), rw(loss_target[0])], [], [(D_MODEL, F32)], [(1, 1)], 256, 'loss')
    loss = lax.psum(loss_loc[0, 0], ("x", "y", "c"))

    G = {}
    dpre, dpost = [None] * 4, [None] * 4
    for i in reversed(range(4)):
        sv = saved[i]

        def fpost_b(y_, d_, g_):
            dy, dg = rms_bwd(y_, g_, d_)
            return [dy], [dg]
        (dyb,), (dpost[i],) = rowwise(fpost_b, [rw(sv['yb']), rw(dx)], [P['post_norm'][i:i + 1]], [(D_MODEL, F32)],
                                      [(1, D_MODEL)], 256, f'post_norm_bwd{i}')
        dh, g = bwd[i](dyb, W, Pl, sv)
        G.update(g)

        def fpre_b(x_, dh_, d_, g_):
            dxl, dg = rms_bwd(x_, g_, dh_)
            return [d_ + dxl], [dg]
        (dx,), (dpre[i],) = rowwise(fpre_b, [rw(sv['x']), rw(dh), rw(dx)], [P['pre_norm'][i:i + 1]], [(D_MODEL, F32)],
                                    [(1, D_MODEL)], 256, f'pre_norm_bwd{i}')
    G['pre_norm'] = jnp.concatenate(dpre, axis=0)
    G['post_norm'] = jnp.concatenate(dpost, axis=0)
    G['b_w_in'] = jnp.concatenate([G['b_w_in'][:, 1024:], G['b_w_in'][:, :1024]], axis=1)
    G['c_w_in'] = _unperm_c_w_in(G['c_w_in'])
    G['c_w_uq'] = _unperm_w_uq(G['c_w_uq'])
    G['c_w_ukv'] = _unperm_w_ukv(G['c_w_ukv'])

    rep = jnp.concatenate([_to_tiles(G[n].reshape(s), s) for n, s in REP_SHAPE.items()], axis=0)
    rep = jnp.pad(rep.reshape(N_DEV, REP_CHUNK, LANES), ((0, 0), (0, REP_SLOT - REP_CHUNK), (0, 0)))
    sh = [_to_tiles(_device_blocks(n, G[n]), _block_shape(n)) for n in SH_ORDER]
    tail = jnp.zeros((N_DEV, RS_ROWS - REP_SLOT - SH_ROWS, LANES), F32)
    gfull = jnp.concatenate([rep] + sh + [tail], axis=1)
    cx, cy, cc = _coords()
    core = jnp.reshape(cc, (1,)).astype(jnp.int32)
    chip = jnp.reshape(2 * cx + cy, (1,)).astype(jnp.int32)
    land = rs_sibling(gfull)
    part = rs_pair_add(gfull, land, core)
    land2 = rs_chips(part)
    out = {}
    for n in SH_ORDER:
        s = _block_shape(n)
        out[n] = adam_param(n, s, REP_SLOT + SH_OFF[n], blocks[n], M[n].reshape(s), V[n].reshape(s), chip,
                            part=part, land=land2)

    grep = all_gather(rs_rep_sum(part, land2, chip), 'ag_rep')[:, :REP_CHUNK].reshape(REP_ROWS, LANES)
    for n, s in REP_SHAPE.items():
        out[n] = adam_param(n, s, REP_OFF[n], P[n].reshape(s), M[n].reshape(s), V[n].reshape(s), chip, grep=grep)
    res = [loss, dx[None]]
    for kind in range(4):
        res += [out[n][kind].reshape(P[n].shape) for n in WEIGHTS]
    return tuple(res)
```

```python
import functools
import math

import numpy as np
import jax
import jax.numpy as jnp
from jax import lax
from jax.experimental import pallas as pl
from jax.experimental.pallas import tpu as pltpu

F32 = jnp.float32
BF16 = jnp.bfloat16
MESH = pl.DeviceIdType.MESH
ANY = pl.BlockSpec(memory_space=pl.ANY)

N_DEV = 8
D_MODEL = 1024
EPS = 1e-6
NEG_INF = -1e30
SSM_G, SSM_P, SSM_H = 64, 64, 16
SSM_T = 256
SSM_WC = 512
HEAD_DIM = 64
SWA_HEADS, SWA_KV = 16, 2
WINDOW = 128
REL_BUCKETS, REL_MAX_DIST = 32, 128
MLA_HEADS, MLA_NOPE, MLA_ROPE, MLA_V = 16, 64, 32, 64
MLA_Q_RANK, MLA_KV_RANK = 768, 256
ROPE_BASE = 10000.0
SGU_G, SGU_C, SGU_T = 16, 64, 128
ADAM_LR, ADAM_B1, ADAM_B2, ADAM_EPS, ADAM_WD, ADAM_STEP = 0.001, 0.9, 0.999, 1e-08, 0.01, 10

WEIGHTS = ['pre_norm', 'post_norm', 'rel_bias', 'a_w_in', 'a_lam_re', 'a_lam_im', 'a_log_dt', 'a_b_re', 'a_b_im',
           'a_c_re', 'a_c_im', 'a_d', 'a_w_glu', 'a_b_glu', 'a_w_out', 'b_w_in', 'b_sinks', 'b_w_out', 'c_w_in',
           'c_q_norm', 'c_kv_norm', 'c_w_uq', 'c_w_ukv', 'c_w_out', 'd_w_in', 'd_ln_g', 'd_ln_b', 'd_w_s', 'd_b_s',
           'd_w_out']
SHARDED = {'a_w_in': ((1024, 2048), 1), 'a_w_glu': ((1024, 1024), 0), 'a_w_out': ((1024, 1024), 0),
           'b_w_in': ((1024, 2304), 1), 'b_w_out': ((1024, 1024), 0), 'c_w_in': ((1024, 2080), 1),
           'c_q_norm': ((1, 768), 1), 'c_kv_norm': ((1, 256), 1), 'c_w_uq': ((768, 1536), 1),
           'c_w_ukv': ((256, 2048), 1), 'c_w_out': ((1024, 1024), 0), 'd_w_in': ((1024, 3072), 1),
           'd_ln_g': ((1, 1024), 1), 'd_ln_b': ((1, 1024), 1), 'd_w_out': ((1024, 1024), 0)}
SHARDED_F32 = ['c_q_norm', 'c_kv_norm', 'd_ln_g', 'd_ln_b']
REPLICATED = [n for n in WEIGHTS if n not in SHARDED]


def _cdiv(a, b):
    return -(-a // b)


def _block_shape(name):
    (r, c), ax = SHARDED[name]
    return (r // N_DEV, c) if ax == 0 else (r, c // N_DEV)


LANES = 128
SH_ORDER = ['a_w_in', 'b_w_in', 'c_w_in', 'd_w_in', 'c_w_uq', 'c_w_ukv', 'a_w_glu', 'a_w_out', 'b_w_out', 'c_w_out',
            'd_w_out', 'c_q_norm', 'c_kv_norm', 'd_ln_g', 'd_ln_b']


def _tiles(shape):
    r, c = shape
    rp = max(r, 8)
    rb = 512 if rp % 512 == 0 else 256 if rp % 256 == 0 else rp
    return rp, _cdiv(c, LANES), rb


SH_OFF = {}
_o = 0
for _n in SH_ORDER:
    _rp, _nt, _rb = _tiles(_block_shape(_n))
    assert _o % _rb == 0
    SH_OFF[_n] = _o
    _o += _rp * _nt
SH_ROWS = _o
assert SH_ROWS % 16 == 0

REP_SHAPE = {'a_b_re': (4096, 16), 'a_b_im': (4096, 16), 'd_w_s': (2048, 128), 'a_c_re': (1024, 64),
             'a_c_im': (1024, 64), 'pre_norm': (4, 1024), 'post_norm': (4, 1024), 'a_lam_re': (64, 64),
             'a_lam_im': (64, 64), 'a_d': (1, 1024), 'a_b_glu': (1, 1024), 'rel_bias': (32, 16), 'd_b_s': (16, 128),
             'a_log_dt': (1, 64), 'b_sinks': (1, 16)}
REP_OFF = {}
_o = 0
for _n, _s in REP_SHAPE.items():
    _rp, _nt, _rb = _tiles(_s)
    assert _o % _rb == 0
    REP_OFF[_n] = _o
    _o += _rp * _nt
REP_ROWS = _o
REP_CHUNK = REP_ROWS // N_DEV
assert REP_ROWS % (8 * N_DEV) == 0
REP_SLOT = 2048
RS_TL = 512
RS_ROWS = _cdiv(REP_SLOT + SH_ROWS, RS_TL) * RS_TL
SMALL_OFF = {}
_o = 0
for _n in SHARDED_F32:
    SMALL_OFF[_n] = _o
    _o += int(np.prod(_block_shape(_n)))
SMALL_ROWS = _cdiv(_o, 128 * 8) * 8


def _pick(n, cands):
    for c in cands:
        if n % c == 0:
            return c
    return n


def mm(a, b, mode, name, out_dtype=F32):
    if mode == 'nn':
        (M, K), (K2, N) = a.shape, b.shape
    elif mode == 'nt':
        (M, K), (N, K2) = a.shape, b.shape
    else:
        (K, M), (K2, N) = a.shape, b.shape
    assert K == K2, (name, a.shape, b.shape)
    tm = _pick(M, (512, 256, 128))
    tn = _pick(N, (512, 384, 256))
    dims = {'nn': ((1,), (0,)), 'nt': ((1,), (1,)), 'tn': ((0,), (0,))}[mode]

    def body(a_ref, b_ref, o_ref):
        o_ref[...] = lax.dot_general(a_ref[...].astype(BF16), b_ref[...].astype(BF16), (dims, ((), ())),
                                     preferred_element_type=F32).astype(out_dtype)

    a_spec = pl.BlockSpec((K, tm), lambda i, j: (0, i)) if mode == 'tn' else pl.BlockSpec((tm, K), lambda i, j: (i, 0))
    b_spec = pl.BlockSpec((tn, K), lambda i, j: (j, 0)) if mode == 'nt' else pl.BlockSpec((K, tn), lambda i, j: (0, j))
    return pl.pallas_call(
        body, grid=(M // tm, N // tn), in_specs=[a_spec, b_spec],
        out_specs=pl.BlockSpec((tm, tn), lambda i, j: (i, j)), out_shape=jax.ShapeDtypeStruct((M, N), out_dtype),
        compiler_params=pltpu.CompilerParams(dimension_semantics=("parallel", "parallel")), name=name)(a, b)


def rw(arr, width=None, cb=0):
    return (arr, arr.shape[1] if width is None else width, cb)


def rowwise(fn, rows, consts, outs, accs, tl, name, n_steps=None):
    if n_steps is None:
        n_steps = [r[0].shape[0] for r in rows if not isinstance(r[1], pl.BlockSpec)][0] // tl
    L = n_steps * tl
    nr, nc, no, na = len(rows), len(consts), len(outs), len(accs)
    in_specs, args = [], []
    for r in rows:
        if isinstance(r[1], pl.BlockSpec):
            in_specs.append(r[1])
        else:
            in_specs.append(pl.BlockSpec((tl, r[1]), functools.partial(lambda i, cb: (i, cb), cb=r[2])))
        args.append(r[0])
    for c in consts:
        in_specs.append(pl.BlockSpec(c.shape, functools.partial(lambda i, nd: (0,) * nd, nd=c.ndim)))
        args.append(c)
    out_specs = [pl.BlockSpec((tl, w), lambda i: (i, 0)) for w, _ in outs]
    out_shape = [jax.ShapeDtypeStruct((L, w), dt) for w, dt in outs]
    for s in accs:
        out_specs.append(pl.BlockSpec(s, functools.partial(lambda i, nd: (0,) * nd, nd=len(s))))
        out_shape.append(jax.ShapeDtypeStruct(s, F32))

    def body(*refs):
        ins = [r[...] for r in refs[:nr + nc]]
        o_refs = refs[nr + nc:nr + nc + no]
        a_refs = refs[nr + nc + no:]
        o_vals, a_vals = fn(*ins)
        for ref, val in zip(o_refs, o_vals):
            ref[...] = val.astype(ref.dtype)
        if na:
            @pl.when(pl.program_id(0) == 0)
            def _():
                for ref in a_refs:
                    ref[...] = jnp.zeros_like(ref)
            for ref, val in zip(a_refs, a_vals):
                ref[...] += val

    res = pl.pallas_call(
        body, grid=(n_steps,), in_specs=in_specs, out_specs=out_specs, out_shape=out_shape,
        compiler_params=pltpu.CompilerParams(dimension_semantics=("arbitrary",)), name=name)(*args)
    return res[:no], res[no:]


def tmm(a, b, mode, name, tl=512, wa=None, wb=None):
    L = a.shape[0]
    tl = min(tl, L)
    nt = 8
    if mode == 'tn':
        def body(a_ref, b_ref, o_ref):
            @pl.when(pl.program_id(1) == 0)
            def _():
                o_ref[...] = jnp.zeros_like(o_ref)
            o_ref[0] += lax.dot_general(a_ref[...].astype(BF16), b_ref[...].astype(BF16), (((0,), (0,)), ((), ())),
                                        preferred_element_type=F32)

        return pl.pallas_call(
            body, grid=(nt, L // tl),
            in_specs=[pl.BlockSpec((tl, wa), lambda k, i: (i, k)), pl.BlockSpec((tl, wb), lambda k, i: (i, k))],
            out_specs=pl.BlockSpec((1, wa, wb), lambda k, i: (k, 0, 0)),
            out_shape=jax.ShapeDtypeStruct((nt, wa, wb), F32),
            compiler_params=pltpu.CompilerParams(dimension_semantics=("parallel", "arbitrary")), name=name)(a, b)
    assert b.shape[0] == nt
    wa = b.shape[1] if mode == 'nn' else b.shape[2]
    wo = b.shape[2] if mode == 'nn' else b.shape[1]
    dims = ((1,), (0,)) if mode == 'nn' else ((1,), (1,))

    def body(a_ref, b_ref, o_ref):
        o_ref[...] = lax.dot_general(a_ref[...].astype(BF16), b_ref[0].astype(BF16), (dims, ((), ())),
                                     preferred_element_type=F32)

    return pl.pallas_call(
        body, grid=(L // tl, nt),
        in_specs=[pl.BlockSpec((tl, wa), lambda i, k: (i, k)), pl.BlockSpec((1,) + b.shape[1:], lambda i, k: (k, 0, 0))],
        out_specs=pl.BlockSpec((tl, wo), lambda i, k: (i, k)), out_shape=jax.ShapeDtypeStruct((L, nt * wo), F32),
        compiler_params=pltpu.CompilerParams(dimension_semantics=("parallel", "parallel")), name=name)(a, b)


_K0 = math.sqrt(2.0 / math.pi)
_K1 = 0.044715


def gelu(x):
    return x * (0.5 * (1.0 + jnp.tanh(_K0 * (x + _K1 * (x * x * x)))))


def gelu_grad(x):
    t = jnp.tanh(_K0 * (x + _K1 * (x * x * x)))
    return 0.5 * (1.0 + t) + 0.5 * x * (1.0 - t * t) * (_K0 * (1.0 + 3.0 * _K1 * x * x))


def sigmoid(x):
    return 1.0 / (1.0 + jnp.exp(-x))


def silu(z):
    return z * sigmoid(z)


def silu_grad(z):
    s = sigmoid(z)
    return s * (1.0 + z * (1.0 - s))


def rms_fwd(x, g):
    r = lax.rsqrt(jnp.mean(x * x, axis=-1, keepdims=True) + EPS)
    return x * r * g


def rms_bwd(x, g, dy):
    r = lax.rsqrt(jnp.mean(x * x, axis=-1, keepdims=True) + EPS)
    xh = x * r
    dg = jnp.sum(dy * xh, axis=0, keepdims=True)
    dxh = dy * g
    dx = r * (dxh - xh * jnp.mean(dxh * xh, axis=-1, keepdims=True))
    return dx, dg


def s5_scan(xr, xi, pr, pi, name, reverse=False, s_re=None, s_im=None):
    L, W = xr.shape
    T, WC = min(SSM_T, L), SSM_WC
    nT = L // T
    with_da = s_re is not None
    steps = [1 << k for k in range(int(math.log2(T)))]

    def body(*refs):
        if with_da:
            xr_ref, xi_ref, pr_ref, pi_ref, sr_ref, si_ref, spr_ref, spi_ref, or_ref, oi_ref, dar_ref, dai_ref, cr, ci = refs
        else:
            xr_ref, xi_ref, pr_ref, pi_ref, or_ref, oi_ref, cr, ci = refs
        i = pl.program_id(1)

        @pl.when(i == 0)
        def _():
            cr[...] = jnp.zeros_like(cr)
            ci[...] = jnp.zeros_like(ci)
            if with_da:
                dar_ref[...] = jnp.zeros_like(dar_ref)
                dai_ref[...] = jnp.zeros_like(dai_ref)

        a_r = xr_ref[...]
        a_i = xi_ref[...]
        row = lax.broadcasted_iota(jnp.int32, (T, WC), 0)
        sgn = -1.0 if reverse else 1.0
        for d in steps:
            wr = pr_ref[(T - d) if reverse else (d - 1):(T - d + 1) if reverse else d, :]
            wi = sgn * pi_ref[(T - d) if reverse else (d - 1):(T - d + 1) if reverse else d, :]
            if reverse:
                yr, yi, keep = pltpu.roll(a_r, T - d, 0), pltpu.roll(a_i, T - d, 0), row < T - d
            else:
                yr, yi, keep = pltpu.roll(a_r, d, 0), pltpu.roll(a_i, d, 0), row >= d
            a_r, a_i = (a_r + jnp.where(keep, wr * yr - wi * yi, 0.0), a_i + jnp.where(keep, wr * yi + wi * yr, 0.0))
        wr = pr_ref[...]
        wi = sgn * pi_ref[...]
        c_r, c_i = cr[...], ci[...]
        a_r, a_i = a_r + (wr * c_r - wi * c_i), a_i + (wr * c_i + wi * c_r)
        or_ref[...] = a_r
        oi_ref[...] = a_i
        if reverse:
            cr[...] = a_r[0:1, :]
            ci[...] = a_i[0:1, :]
        else:
            cr[...] = a_r[T - 1:T, :]
            ci[...] = a_i[T - 1:T, :]
        if with_da:
            first = (nT - 1 - i) == 0
            pv_r = jnp.where(first, 0.0, spr_ref[7:8, :])
            pv_i = jnp.where(first, 0.0, spi_ref[7:8, :])
            sp_r = jnp.where(row == 0, pv_r, pltpu.roll(sr_ref[...], 1, 0))
            sp_i = jnp.where(row == 0, pv_i, pltpu.roll(si_ref[...], 1, 0))
            dar_ref[...] += jnp.sum(a_r * sp_r + a_i * sp_i, axis=0, keepdims=True)
            dai_ref[...] += jnp.sum(a_i * sp_r - a_r * sp_i, axis=0, keepdims=True)

    if reverse:
        xmap = lambda j, i: (nT - 1 - i, j)
        pmap = lambda j, i: (jnp.maximum((nT - 1 - i) * (T // 8) - 1, 0), j)
    else:
        xmap = lambda j, i: (i, j)
    xspec = pl.BlockSpec((T, WC), xmap)
    pspec = pl.BlockSpec((T, WC), lambda j, i: (0, j))
    in_specs = [xspec, xspec, pspec, pspec]
    args = [xr, xi, pr, pi]
    out_specs = [xspec, xspec]
    out_shape = [jax.ShapeDtypeStruct((L, W), F32)] * 2
    if with_da:
        in_specs += [xspec, xspec, pl.BlockSpec((8, WC), pmap), pl.BlockSpec((8, WC), pmap)]
        args += [s_re, s_im, s_re, s_im]
        out_specs += [pl.BlockSpec((1, WC), lambda j, i: (0, j))] * 2
        out_shape += [jax.ShapeDtypeStruct((1, W), F32)] * 2
    return pl.pallas_call(
        body, grid=(W // WC, nT), in_specs=in_specs, out_specs=out_specs, out_shape=out_shape,
        scratch_shapes=[pltpu.VMEM((1, WC), F32), pltpu.VMEM((1, WC), F32)],
        compiler_params=pltpu.CompilerParams(dimension_semantics=("parallel", "arbitrary")), name=name)(*args)


def s5_discretize(lam_re, lam_im, log_dt, b_re, b_im):
    dt = jnp.exp(log_dt)[:, None]
    mag = jnp.exp(lam_re * dt)
    ab_re = mag * jnp.cos(lam_im * dt)
    ab_im = mag * jnp.sin(lam_im * dt)
    den = lam_re * lam_re + lam_im * lam_im
    nr = ab_re - 1.0
    f_re = (nr * lam_re + ab_im * lam_im) / den
    f_im = (ab_im * lam_re - nr * lam_im) / den
    bb_re = f_re[..., None] * b_re - f_im[..., None] * b_im
    bb_im = f_re[..., None] * b_im + f_im[..., None] * b_re
    return ab_re, ab_im, bb_re, bb_im


_EYE8 = np.eye(8, dtype=np.float32)


def _b_tiles(bb):
    t = bb.transpose(0, 2, 1).reshape(8, 8, SSM_H, SSM_P)
    return jnp.einsum('kghp,gG->kghGp', t, _EYE8).reshape(8, 8 * SSM_H, 8 * SSM_P)


def _b_untile(d):
    t = jnp.einsum('kghGp,gG->kghp', d.reshape(8, 8, SSM_H, 8, SSM_P), _EYE8)
    return t.reshape(SSM_G, SSM_H, SSM_P).transpose(0, 2, 1)


def _c_tiles(c):
    t = c.transpose(0, 2, 1).reshape(8, 8, SSM_P, SSM_H)
    return jnp.einsum('kgph,gG->kgpGh', t, _EYE8).reshape(8, 8 * SSM_P, 8 * SSM_H)


def _c_untile(d):
    t = jnp.einsum('kgpGh,gG->kgph', d.reshape(8, 8, SSM_P, 8, SSM_H), _EYE8)
    return t.reshape(SSM_G, SSM_P, SSM_H).transpose(0, 2, 1)


def s5_powers(ar, ai, T):
    W = ar.shape[1]

    def body(ar_ref, ai_ref, fr_ref, fi_ref, rr_ref, ri_ref):
        fr_ref[0:1, :] = ar_ref[...]
        fi_ref[0:1, :] = ai_ref[...]
        rr_ref[T - 1:T, :] = ar_ref[...]
        ri_ref[T - 1:T, :] = ai_ref[...]
        n = 1
        while n < T:
            cr, ci = fr_ref[0:n, :], fi_ref[0:n, :]
            lr, li = fr_ref[n - 1:n, :], fi_ref[n - 1:n, :]
            fr_ref[n:2 * n, :] = cr * lr - ci * li
            fi_ref[n:2 * n, :] = cr * li + ci * lr
            cr, ci = rr_ref[T - n:T, :], ri_ref[T - n:T, :]
            rr_ref[T - 2 * n:T - n, :] = cr * lr - ci * li
            ri_ref[T - 2 * n:T - n, :] = cr * li + ci * lr
            n *= 2

    spec = pl.BlockSpec((T, SSM_WC), lambda j: (0, j))
    aspec = pl.BlockSpec((1, SSM_WC), lambda j: (0, j))
    return pl.pallas_call(
        body, grid=(W // SSM_WC,), in_specs=[aspec, aspec], out_specs=[spec] * 4,
        out_shape=[jax.ShapeDtypeStruct((T, W), F32)] * 4,
        compiler_params=pltpu.CompilerParams(dimension_semantics=("parallel",)), name='a_powers')(ar, ai)


def layer_a_fwd(h, w, p):
    L = h.shape[0]
    proj = mm(h, w['a_w_in'], 'nn', 'a_proj')
    disc = lambda *a: s5_discretize(*a)
    (ab_re, ab_im, bb_re, bb_im), disc_vjp = jax.vjp(disc, p['a_lam_re'][0], p['a_lam_im'][0], p['a_log_dt'][0],
                                                     p['a_b_re'][0], p['a_b_im'][0])
    Bre, Bim = _b_tiles(bb_re), _b_tiles(bb_im)
    Cre, Cim = _c_tiles(p['a_c_re'][0]), -_c_tiles(p['a_c_im'][0])
    T = min(SSM_T, L)
    pr, pi, prr, pir = s5_powers(ab_re.reshape(1, -1), ab_im.reshape(1, -1), T)
    bu_re = tmm(proj, Bre, 'nn', 'a_bu_re')
    bu_im = tmm(proj, Bim, 'nn', 'a_bu_im')
    s_re, s_im = s5_scan(bu_re, bu_im, pr, pi, 'a_scan')
    y_re = tmm(s_re, Cre, 'nn', 'a_y_re')
    y_im = tmm(s_im, Cim, 'nn', 'a_y_im')

    def f1(u, yre, yim, dsk):
        y = yre + yim + dsk * u
        return [y, gelu(y)], []
    (y, yg), _ = rowwise(f1, [rw(proj, 1024, 0), rw(y_re), rw(y_im)], [p['a_d']], [(1024, F32)] * 2, [], 256, 'a_gelu')
    gl = mm(yg, w['a_w_glu'], 'nn', 'a_glu')

    def f2(yg_, gl_, z, bg):
        return [yg_ * sigmoid(gl_ + bg) * silu(z)], []
    (po,), _ = rowwise(f2, [rw(yg), rw(gl), rw(proj, 1024, 1)], [p['a_b_glu']], [(1024, F32)], [], 256, 'a_gate')
    yb = mm(po, w['a_w_out'], 'nn', 'a_out')
    saved = dict(h=h, proj=proj, disc_vjp=disc_vjp, Bre=Bre, Bim=Bim, Cre=Cre, Cim=Cim, prr=prr, pir=pir, s_re=s_re,
                 s_im=s_im, y=y, yg=yg, gl=gl, po=po)
    return yb, saved


def layer_a_bwd(dyb, w, p, sv):
    g = {}
    dpo = mm(dyb, w['a_w_out'], 'nt', 'a_dpo')
    g['a_w_out'] = mm(sv['po'], dyb, 'tn', 'a_dwout')
    proj = sv['proj']

    def f1(dpo_, yg, gl, z, bg):
        sg = sigmoid(gl + bg)
        sz = silu(z)
        dm = dpo_ * sz
        dz = dpo_ * (yg * sg) * silu_grad(z)
        dgl = dm * yg * sg * (1.0 - sg)
        return [dz, dm * sg, dgl], [jnp.sum(dgl, axis=0, keepdims=True)]
    (dz, dyg1, dgl), (db_glu,) = rowwise(f1, [rw(dpo), rw(sv['yg']), rw(sv['gl']), rw(proj, 1024, 1)], [p['a_b_glu']],
                                          [(1024, F32)] * 3, [(1, 1024)], 256, 'a_gate_bwd')
    g['a_b_glu'] = db_glu
    g['a_w_glu'] = mm(sv['yg'], dgl, 'tn', 'a_dwglu')
    dyg2 = mm(dgl, w['a_w_glu'], 'nt', 'a_dyg2')

    def f2(dyg1_, dyg2_, y, u, dsk):
        dy = (dyg1_ + dyg2_) * gelu_grad(y)
        return [dy, dy * dsk], [jnp.sum(dy * u, axis=0, keepdims=True)]
    (dy, du1), (dd,) = rowwise(f2, [rw(dyg1), rw(dyg2), rw(sv['y']), rw(proj, 1024, 0)], [p['a_d']],
                               [(1024, F32)] * 2, [(1, 1024)], 256, 'a_gelu_bwd')
    g['a_d'] = dd
    ds_re = tmm(dy, sv['Cre'], 'nt', 'a_ds_re')
    ds_im = tmm(dy, sv['Cim'], 'nt', 'a_ds_im')
    dCre = tmm(sv['s_re'], dy, 'tn', 'a_dcre', wa=512, wb=128)
    dCim = -tmm(sv['s_im'], dy, 'tn', 'a_dcim', wa=512, wb=128)
    g_re, g_im, da_re, da_im = s5_scan(ds_re, ds_im, sv['prr'], sv['pir'], 'a_scan_rev', reverse=True,
                                       s_re=sv['s_re'], s_im=sv['s_im'])
    dBre = tmm(proj, g_re, 'tn', 'a_dbre', wa=128, wb=512)
    dBim = tmm(proj, g_im, 'tn', 'a_dbim', wa=128, wb=512)
    du2a = tmm(g_re, sv['Bre'], 'nt', 'a_du_re')
    du2b = tmm(g_im, sv['Bim'], 'nt', 'a_du_im')

    def f3(a, b, c, dz_):
        return [jnp.concatenate([a + b + c, dz_], axis=1)], []
    (dproj,), _ = rowwise(f3, [rw(du1), rw(du2a), rw(du2b), rw(dz)], [], [(2048, F32)], [], 256, 'a_dproj')
    dlr, dli, dldt, dbr, dbi = sv['disc_vjp']((da_re.reshape(SSM_G, SSM_P), da_im.reshape(SSM_G, SSM_P),
                                               _b_untile(dBre), _b_untile(dBim)))
    g['a_lam_re'], g['a_lam_im'], g['a_log_dt'] = dlr[None], dli[None], dldt[None]
    g['a_b_re'], g['a_b_im'] = dbr[None], dbi[None]
    g['a_c_re'], g['a_c_im'] = _c_untile(dCre)[None], _c_untile(dCim)[None]
    g['a_w_in'] = mm(sv['h'], dproj, 'tn', 'a_dwin')
    dh = mm(dproj, w['a_w_in'], 'nt', 'a_dh')
    return dh, g


def _t5_bucket_np():
    qi = np.arange(WINDOW)[:, None]
    kj = np.arange(2 * WINDOW)[None, :]
    dist = np.maximum(qi + WINDOW - kj, 0)
    max_exact = REL_BUCKETS // 2
    dist_f = np.maximum(dist, 1).astype(np.float32)
    large = max_exact + (np.log(dist_f / np.float32(max_exact)) / np.float32(math.log(REL_MAX_DIST / max_exact))
                         * np.float32(REL_BUCKETS - max_exact)).astype(np.int32)
    large = np.minimum(large, REL_BUCKETS - 1)
    return np.where(dist < max_exact, dist, large).astype(np.int32)


SWA_GRP = SWA_HEADS // SWA_KV


def _swa_kv(kvp, kvc, kvh):
    kb = jnp.concatenate([kvp[:, kvh * 64:(kvh + 1) * 64], kvc[:, kvh * 64:(kvh + 1) * 64]], 0).astype(BF16)
    vb = jnp.concatenate([kvp[:, 128 + kvh * 64:128 + (kvh + 1) * 64], kvc[:, 128 + kvh * 64:128 + (kvh + 1) * 64]],
                         0).astype(BF16)
    return kb, vb


def _swa_stack(x, kvh):
    return jnp.concatenate([x[:, (kvh * SWA_GRP + g) * 64:(kvh * SWA_GRP + g + 1) * 64] for g in range(SWA_GRP)],
                           axis=0).astype(BF16)


def _swa_group(bias_ref, kvh):
    return bias_ref[kvh * SWA_GRP:(kvh + 1) * SWA_GRP].reshape(SWA_GRP * WINDOW, 2 * WINDOW)


def _swa_sinks(sink_ref, kvh):
    return jnp.concatenate([jnp.broadcast_to(sink_ref[0:1, kvh * SWA_GRP + g:kvh * SWA_GRP + g + 1], (WINDOW, 1))
                            for g in range(SWA_GRP)], axis=0)


def _swa_probs(q, kb, bias_h, sink, valid):
    s = lax.dot_general(q, kb, (((1,), (1,)), ((), ())), preferred_element_type=F32) * (HEAD_DIM ** -0.5)
    s = jnp.where(valid, s + bias_h, NEG_INF)
    m = jnp.maximum(jnp.max(s, axis=-1, keepdims=True), sink)
    e = jnp.exp(s - m)
    es = jnp.exp(sink - m)
    den = jnp.sum(e, axis=-1, keepdims=True) + es
    return e / den, es / den


def _swa_valid(n):
    qi = lax.broadcasted_iota(jnp.int32, (SWA_GRP * WINDOW, 2 * WINDOW), 0) & (WINDOW - 1)
    kj = lax.broadcasted_iota(jnp.int32, (SWA_GRP * WINDOW, 2 * WINDOW), 1)
    dist = qi + WINDOW - kj
    return (dist >= 0) & (dist < WINDOW) & ((kj >= WINDOW) | (n > 0))


def swa_fwd(proj, bias, sinks):
    L = proj.shape[0]

    def body(z_ref, q_ref, kvc_ref, kvp_ref, bias_ref, sink_ref, o_ref, po_ref):
        n = pl.program_id(0)
        valid = _swa_valid(n)
        q, kvc, kvp = q_ref[...], kvc_ref[...], kvp_ref[...]
        outs = []
        for kvh in range(SWA_KV):
            kb, vb = _swa_kv(kvp, kvc, kvh)
            p, _ = _swa_probs(_swa_stack(q, kvh), kb, _swa_group(bias_ref, kvh), _swa_sinks(sink_ref, kvh), valid)
            o8 = jnp.dot(p.astype(BF16), vb, preferred_element_type=F32)
            outs += [o8[g * WINDOW:(g + 1) * WINDOW] for g in range(SWA_GRP)]
        o = jnp.concatenate(outs, axis=1)
        o_ref[...] = o
        po_ref[...] = o * silu(z_ref[...])

    return pl.pallas_call(
        body, grid=(L // WINDOW,),
        in_specs=[pl.BlockSpec((WINDOW, 1024), lambda n: (n, 0)), pl.BlockSpec((WINDOW, 1024), lambda n: (n, 1)),
                  pl.BlockSpec((WINDOW, 256), lambda n: (n, 8)),
                  pl.BlockSpec((WINDOW, 256), lambda n: (jnp.maximum(n - 1, 0), 8)),
                  pl.BlockSpec((SWA_HEADS, WINDOW, 2 * WINDOW), lambda n: (0, 0, 0)),
                  pl.BlockSpec((1, SWA_HEADS), lambda n: (0, 0))],
        out_specs=[pl.BlockSpec((WINDOW, 1024), lambda n: (n, 0))] * 2,
        out_shape=[jax.ShapeDtypeStruct((L, 1024), F32)] * 2,
        compiler_params=pltpu.CompilerParams(dimension_semantics=("parallel",)), name='b_attn')(
            proj, proj, proj, proj, bias, sinks)


def swa_bwd(proj, do, bias, sinks):
    L = proj.shape[0]

    def body(q_ref, kvc_ref, kvp_ref, do_ref, bias_ref, sink_ref, dq_ref, dkv_ref, dbias_ref, dsink_ref):
        n = pl.program_id(0)

        @pl.when(n == 0)
        def _():
            dkv_ref[...] = jnp.zeros_like(dkv_ref)
            dbias_ref[...] = jnp.zeros_like(dbias_ref)
            dsink_ref[...] = jnp.zeros_like(dsink_ref)

        valid = _swa_valid(n)
        q, kvc, kvp, do_ = q_ref[...], kvc_ref[...], kvp_ref[...], do_ref[...]
        dqs, dks, dvs, dsk = [], [], [], []
        for kvh in range(SWA_KV):
            kb, vb = _swa_kv(kvp, kvc, kvh)
            q8, do8 = _swa_stack(q, kvh), _swa_stack(do_, kvh)
            p, ps = _swa_probs(q8, kb, _swa_group(bias_ref, kvh), _swa_sinks(sink_ref, kvh), valid)
            dp = lax.dot_general(do8, vb, (((1,), (1,)), ((), ())), preferred_element_type=F32)
            delta = jnp.sum(p * dp, axis=-1, keepdims=True)
            ds = p * (dp - delta)
            col = -ps * delta
            dsk += [jnp.sum(col[g * WINDOW:(g + 1) * WINDOW], axis=0, keepdims=True) for g in range(SWA_GRP)]
            dbias_ref[kvh * SWA_GRP:(kvh + 1) * SWA_GRP] += ds.reshape(SWA_GRP, WINDOW, 2 * WINDOW)
            dsb = (ds * (HEAD_DIM ** -0.5)).astype(BF16)
            dq8 = jnp.dot(dsb, kb, preferred_element_type=F32)
            dqs += [dq8[g * WINDOW:(g + 1) * WINDOW] for g in range(SWA_GRP)]
            dks.append(lax.dot_general(dsb, q8, (((0,), (0,)), ((), ())), preferred_element_type=F32))
            dvs.append(lax.dot_general(p.astype(BF16), do8, (((0,), (0,)), ((), ())), preferred_element_type=F32))
        dq_ref[...] = jnp.concatenate(dqs, axis=1)
        dsink_ref[...] += jnp.concatenate(dsk, axis=1)
        both = jnp.concatenate(dks + dvs, axis=1)
        r_cur = pl.multiple_of(n * WINDOW, WINDOW)
        r_prev = pl.multiple_of(jnp.maximum(n - 1, 0) * WINDOW, WINDOW)
        dkv_ref[pl.ds(r_prev, WINDOW), :] += both[:WINDOW]
        dkv_ref[pl.ds(r_cur, WINDOW), :] += both[WINDOW:]

    return pl.pallas_call(
        body, grid=(L // WINDOW,),
        in_specs=[pl.BlockSpec((WINDOW, 1024), lambda n: (n, 1)), pl.BlockSpec((WINDOW, 256), lambda n: (n, 8)),
                  pl.BlockSpec((WINDOW, 256), lambda n: (jnp.maximum(n - 1, 0), 8)),
                  pl.BlockSpec((WINDOW, 1024), lambda n: (n, 0)),
                  pl.BlockSpec((SWA_HEADS, WINDOW, 2 * WINDOW), lambda n: (0, 0, 0)),
                  pl.BlockSpec((1, SWA_HEADS), lambda n: (0, 0))],
        out_specs=[pl.BlockSpec((WINDOW, 1024), lambda n: (n, 0)), pl.BlockSpec((L, 256), lambda n: (0, 0)),
                   pl.BlockSpec((SWA_HEADS, WINDOW, 2 * WINDOW), lambda n: (0, 0, 0)),
                   pl.BlockSpec((1, SWA_HEADS), lambda n: (0, 0))],
        out_shape=[jax.ShapeDtypeStruct((L, 1024), F32), jax.ShapeDtypeStruct((L, 256), F32),
                   jax.ShapeDtypeStruct((SWA_HEADS, WINDOW, 2 * WINDOW), F32), jax.ShapeDtypeStruct((1, SWA_HEADS), F32)],
        compiler_params=pltpu.CompilerParams(dimension_semantics=("arbitrary",)), name='b_attn_bwd')(
            proj, proj, proj, do, bias, sinks)


def swa_bias(rel_bias):
    def body(bk_ref, rb_ref, o_ref):
        bk = bk_ref[...]
        for h in range(SWA_HEADS):
            acc = jnp.zeros((WINDOW, 2 * WINDOW), F32)
            for b in range(REL_BUCKETS):
                acc = jnp.where(bk == b, rb_ref[b, h], acc)
            o_ref[h] = acc

    return pl.pallas_call(
        body, out_shape=jax.ShapeDtypeStruct((SWA_HEADS, WINDOW, 2 * WINDOW), F32),
        in_specs=[pl.BlockSpec(memory_space=pltpu.VMEM), pl.BlockSpec(memory_space=pltpu.SMEM)],
        out_specs=pl.BlockSpec(memory_space=pltpu.VMEM), name='b_bias')(jnp.asarray(_t5_bucket_np()), rel_bias)


def layer_b_fwd(h, w, p):
    proj = mm(h, w['b_w_in'], 'nn', 'b_proj')
    bias = swa_bias(p['rel_bias'])
    o, po = swa_fwd(proj, bias, p['b_sinks'])
    yb = mm(po, w['b_w_out'], 'nn', 'b_out')
    return yb, dict(h=h, proj=proj, bias=bias, o=o, po=po)


def layer_b_bwd(dyb, w, p, sv):
    g = {}
    dpo = mm(dyb, w['b_w_out'], 'nt', 'b_dpo')
    g['b_w_out'] = mm(sv['po'], dyb, 'tn', 'b_dwout')
    proj = sv['proj']

    def f1(dpo_, o, z):
        return [dpo_ * silu(z), dpo_ * o * silu_grad(z)], []
    (do, dz), _ = rowwise(f1, [rw(dpo), rw(sv['o']), rw(proj, 1024, 0)], [], [(1024, F32)] * 2, [], 256, 'b_gate_bwd')
    dq, dkv, dbias, dsinks = swa_bwd(proj, do, sv['bias'], p['b_sinks'])
    g['b_sinks'] = dsinks
    onehot = jnp.asarray(np.eye(REL_BUCKETS, dtype=np.float32)[_t5_bucket_np().reshape(-1)])

    def f2(db, oh):
        return [], [lax.dot_general(db, oh, (((1,), (0,)), ((), ())), preferred_element_type=F32,
                                    precision=lax.Precision.HIGHEST)]
    _, (drel,) = rowwise(f2, [(dbias.reshape(SWA_HEADS, -1), pl.BlockSpec((SWA_HEADS, 4096), lambda i: (0, i))),
                              (onehot, pl.BlockSpec((4096, REL_BUCKETS), lambda i: (i, 0)))], [], [],
                         [(SWA_HEADS, REL_BUCKETS)], 4096, 'b_drel', n_steps=(2 * WINDOW * WINDOW) // 4096)
    g['rel_bias'] = drel.T

    def f3(dz_, dq_, dkv_):
        return [jnp.concatenate([dz_, dq_, dkv_], axis=1)], []
    (dproj,), _ = rowwise(f3, [rw(dz), rw(dq), rw(dkv)], [], [(2304, F32)], [], 256, 'b_dproj')
    g['b_w_in'] = mm(sv['h'], dproj, 'tn', 'b_dwin')
    dh = mm(dproj, w['b_w_in'], 'nt', 'b_dh')
    return dh, g


MLA_SCALE = (MLA_NOPE + MLA_ROPE) ** -0.5


def _rope_tables(L):
    inv = ROPE_BASE ** (-jnp.arange(0, MLA_ROPE, 2, dtype=F32) / MLA_ROPE)
    ang = jnp.arange(L, dtype=F32)[:, None] * inv[None, :]
    c, s = jnp.cos(ang), jnp.sin(ang)
    one, zero, pad = jnp.ones((L, 128), F32), jnp.zeros((L, 128), F32), jnp.zeros((L, 64), F32)
    return (jnp.concatenate([one, c, c, c, c, pad], 1), jnp.concatenate([zero, s, s, s, s, pad], 1))


def _rot(x, transpose=False):
    w = x.shape[1]
    lane = lax.broadcasted_iota(jnp.int32, x.shape, 1)
    up = pltpu.roll(x, w - 16, 1)
    dn = pltpu.roll(x, 16, 1)
    first = (lane % 32) < 16
    return jnp.where(first, up, -dn) if transpose else jnp.where(first, -up, dn)


MLA_QT = 512


def _mla_exp(qf, kf, t, qt):
    n_k = kf.shape[0]
    s = lax.dot_general(qf, kf, (((1,), (1,)), ((), ())), preferred_element_type=F32) * MLA_SCALE
    qpos = t * qt + lax.broadcasted_iota(jnp.int32, (qt, n_k), 0)
    kpos = lax.broadcasted_iota(jnp.int32, (qt, n_k), 1)
    s = jnp.where(kpos <= qpos, s, NEG_INF)
    e = jnp.exp(s - jnp.max(s, axis=-1, keepdims=True))
    return e, jnp.sum(e, axis=-1, keepdims=True)


def _mla_heads(q, kv, kr):
    out = []
    for j in range(2):
        qf = jnp.concatenate([q[:, j * 64:(j + 1) * 64], q[:, 128 + j * 32:128 + (j + 1) * 32]], axis=1)
        kf = jnp.concatenate([kv[:, j * 64:(j + 1) * 64], kr], axis=1)
        out.append((qf, kf, kv[:, 128 + j * 64:128 + (j + 1) * 64]))
    return out


def mla_fwd(q, kv, kr):
    L = q.shape[0]
    qt = min(MLA_QT, L)
    nq = L // qt

    def body(q_ref, kv_ref, kr_ref, o_ref):
        for t in range(nq):
            @pl.when(pl.program_id(1) == t)
            def _(t=t):
                n_k = (t + 1) * qt
                outs = []
                for qf, kf, v in _mla_heads(q_ref[...], kv_ref[0:n_k, :], kr_ref[0:n_k, 0:MLA_ROPE]):
                    e, den = _mla_exp(qf, kf, t, qt)
                    outs.append(jnp.dot(e.astype(BF16), v, preferred_element_type=F32) / den)
                o_ref[...] = jnp.concatenate(outs, axis=1)

    return pl.pallas_call(
        body, grid=(MLA_HEADS // 2, nq),
        in_specs=[pl.BlockSpec((qt, 256), lambda hp, n: (n, hp)), pl.BlockSpec((L, 256), lambda hp, n: (0, hp)),
                  pl.BlockSpec((L, 128), lambda hp, n: (0, 0))],
        out_specs=pl.BlockSpec((qt, 128), lambda hp, n: (n, hp)), out_shape=jax.ShapeDtypeStruct((L, 1024), F32),
        compiler_params=pltpu.CompilerParams(dimension_semantics=("parallel", "parallel")), name='c_attn')(q, kv, kr)


def mla_bwd(q, kv, kr, do):
    L = q.shape[0]
    qt = min(MLA_QT, L)
    nq = L // qt

    def body(q_ref, kv_ref, kr_ref, do_ref, dq_ref, dkv_ref, dkr_ref):
        @pl.when(pl.program_id(1) == 0)
        def _():
            dkv_ref[...] = jnp.zeros_like(dkv_ref)
            dkr_ref[...] = jnp.zeros_like(dkr_ref)

        for t in range(nq):
            @pl.when(pl.program_id(1) == t)
            def _(t=t):
                n_k = (t + 1) * qt
                do_ = do_ref[...]
                dqn, dqr, dkn, dvs = [], [], [], []
                dkr = jnp.zeros((n_k, MLA_ROPE), F32)
                for j, (qf, kf, v) in enumerate(_mla_heads(q_ref[...], kv_ref[0:n_k, :], kr_ref[0:n_k, 0:MLA_ROPE])):
                    doh = do_[:, j * 64:(j + 1) * 64]
                    e, den = _mla_exp(qf, kf, t, qt)
                    p = e * (1.0 / den)
                    dp = lax.dot_general(doh, v, (((1,), (1,)), ((), ())), preferred_element_type=F32)
                    ds = (p * (dp - jnp.sum(p * dp, axis=-1, keepdims=True)) * MLA_SCALE).astype(BF16)
                    dqf = jnp.dot(ds, kf, preferred_element_type=F32)
                    dkf = lax.dot_general(ds, qf, (((0,), (0,)), ((), ())), preferred_element_type=F32)
                    dvs.append(lax.dot_general(p.astype(BF16), doh, (((0,), (0,)), ((), ())), preferred_element_type=F32))
                    dqn.append(dqf[:, :MLA_NOPE])
                    dqr.append(dqf[:, MLA_NOPE:])
                    dkn.append(dkf[:, :MLA_NOPE])
                    dkr = dkr + dkf[:, MLA_NOPE:]
                dq_ref[...] = jnp.concatenate(dqn + dqr + [jnp.zeros((qt, 64), F32)], axis=1)
                dkv_ref[0:n_k, :] += jnp.concatenate(dkn + dvs, axis=1)
                dkr_ref[0, 0:n_k, :] += jnp.concatenate([dkr, jnp.zeros((n_k, 128 - MLA_ROPE), F32)], axis=1)

    return pl.pallas_call(
        body, grid=(MLA_HEADS // 2, nq),
        in_specs=[pl.BlockSpec((qt, 256), lambda hp, n: (n, hp)), pl.BlockSpec((L, 256), lambda hp, n: (0, hp)),
                  pl.BlockSpec((L, 128), lambda hp, n: (0, 0)), pl.BlockSpec((qt, 128), lambda hp, n: (n, hp))],
        out_specs=[pl.BlockSpec((qt, 256), lambda hp, n: (n, hp)), pl.BlockSpec((L, 256), lambda hp, n: (0, hp)),
                   pl.BlockSpec((1, L, 128), lambda hp, n: (hp, 0, 0))],
        out_shape=[jax.ShapeDtypeStruct((L, 2048), F32), jax.ShapeDtypeStruct((L, 2048), F32),
                   jax.ShapeDtypeStruct((MLA_HEADS // 2, L, 128), F32)],
        compiler_params=pltpu.CompilerParams(dimension_semantics=("parallel", "arbitrary")), name='c_attn_bwd')(
            q, kv, kr, do)


def _perm_c_w_in(wf):
    return jnp.concatenate([wf[:, 1056:], wf[:, :1056], jnp.zeros((wf.shape[0], 96), wf.dtype)], axis=1)


def _unperm_c_w_in(d):
    return jnp.concatenate([d[:, 1024:2080], d[:, :1024]], axis=1)


def _perm_w_uq(wf):
    t = wf.reshape(wf.shape[0], 8, 2, 96)
    nope = t[..., :64].reshape(-1, 8, 128)
    rope = t[..., 64:].reshape(-1, 8, 64)
    return jnp.concatenate([nope, rope, jnp.zeros_like(rope)], axis=2).reshape(-1, 2048)


def _unperm_w_uq(d):
    t = d.reshape(d.shape[0], 8, 256)
    nope = t[..., :128].reshape(-1, 8, 2, 64)
    rope = t[..., 128:192].reshape(-1, 8, 2, 32)
    return jnp.concatenate([nope, rope], axis=3).reshape(-1, 1536)


def _perm_w_ukv(wf):
    return wf.reshape(-1, 8, 2, 2, 64).transpose(0, 1, 3, 2, 4).reshape(-1, 2048)


def _unperm_w_ukv(d):
    return d.reshape(-1, 8, 2, 2, 64).transpose(0, 1, 3, 2, 4).reshape(-1, 2048)


def layer_c_fwd(h, w, p):
    L = h.shape[0]
    proj = mm(h, w['c_w_in'], 'nn', 'c_proj')

    def f1(c, gq, gk):
        return [rms_fwd(c[:, :768], gq), rms_fwd(c[:, 768:], gk)], []
    (cqn, ckvn), _ = rowwise(f1, [rw(proj, 1024, 1)], [p['c_q_norm'], p['c_kv_norm']], [(768, BF16), (256, BF16)], [],
                             256, 'c_norms')
    qf = mm(cqn, w['c_w_uq'], 'nn', 'c_uq')
    kvf = mm(ckvn, w['c_w_ukv'], 'nn', 'c_ukv', out_dtype=BF16)
    cos, sin = _rope_tables(L)

    def f2(q_, kr_, c, s):
        c8, s8 = jnp.tile(c, (1, 8)), jnp.tile(s, (1, 8))
        return [q_ * c8 + _rot(q_) * s8, kr_ * c[:, 128:] + _rot(kr_) * s[:, 128:]], []
    (q, kr), _ = rowwise(f2, [rw(qf), rw(proj, 128, 16), rw(cos), rw(sin)], [], [(2048, BF16), (128, BF16)], [], 256,
                         'c_rope')
    o = mla_fwd(q, kvf, kr)

    def f3(o_, z):
        return [o_ * silu(z)], []
    (po,), _ = rowwise(f3, [rw(o), rw(proj, 1024, 0)], [], [(1024, F32)], [], 256, 'c_gate')
    yb = mm(po, w['c_w_out'], 'nn', 'c_out')
    return yb, dict(h=h, proj=proj, cqn=cqn, ckvn=ckvn, q=q, kv=kvf, kr=kr, o=o, po=po, cos=cos, sin=sin)


def layer_c_bwd(dyb, w, p, sv):
    g = {}
    dpo = mm(dyb, w['c_w_out'], 'nt', 'c_dpo')
    g['c_w_out'] = mm(sv['po'], dyb, 'tn', 'c_dwout')
    proj = sv['proj']
    L = proj.shape[0]

    def f1(dpo_, o, z):
        return [dpo_ * silu(z), dpo_ * o * silu_grad(z)], []
    (do, dz), _ = rowwise(f1, [rw(dpo), rw(sv['o']), rw(proj, 1024, 0)], [], [(1024, BF16), (1024, F32)], [], 256,
                          'c_gate_bwd')
    dq, dkvf, dkr8 = mla_bwd(sv['q'], sv['kv'], sv['kr'], do)

    def f2(dq_, dkr_, c, s):
        c8, s8 = jnp.tile(c, (1, 8)), jnp.tile(s, (1, 8))
        dk = jnp.sum(dkr_, axis=0)
        return [dq_ * c8 + _rot(dq_ * s8, True), dk * c[:, 128:] + _rot(dk * s[:, 128:], True)], []
    tl = 256
    (dqf, dkr), _ = rowwise(f2, [rw(dq), (dkr8, pl.BlockSpec((8, tl, 128), lambda i: (0, i, 0))), rw(sv['cos']),
                                 rw(sv['sin'])], [], [(2048, F32), (128, F32)], [], tl, 'c_rope_bwd')
    g['c_w_uq'] = mm(sv['cqn'], dqf, 'tn', 'c_dwuq')
    g['c_w_ukv'] = mm(sv['ckvn'], dkvf, 'tn', 'c_dwukv')
    dcqn = mm(dqf, w['c_w_uq'], 'nt', 'c_dcqn')
    dckvn = mm(dkvf, w['c_w_ukv'], 'nt', 'c_dckvn')

    def f3(c, dq_, dk_, dz_, dkr_, gq, gk):
        dcq, dgq = rms_bwd(c[:, :768], gq, dq_)
        dckv, dgk = rms_bwd(c[:, 768:], gk, dk_)
        return [jnp.concatenate([dz_, dcq, dckv, dkr_], axis=1)], [dgq, dgk]
    (dproj,), (dgq, dgk) = rowwise(f3, [rw(proj, 1024, 1), rw(dcqn), rw(dckvn), rw(dz), rw(dkr)],
                                   [p['c_q_norm'], p['c_kv_norm']], [(2176, F32)], [(1, 768), (1, 256)], 256, 'c_dproj')
    g['c_q_norm'], g['c_kv_norm'] = dgq, dgk
    g['c_w_in'] = mm(sv['h'], dproj, 'tn', 'c_dwin')
    dh = mm(dproj, w['c_w_in'], 'nt', 'c_dh')
    return dh, g


def _sgu_mix(wm, v, transpose):
    outs = []
    dims = (((0,), (0,)), ((), ())) if transpose else (((1,), (0,)), ((), ()))
    for gi in range(SGU_G):
        outs.append(lax.dot_general(wm[gi], v[:, gi * SGU_C:(gi + 1) * SGU_C].astype(BF16), dims,
                                    preferred_element_type=F32))
    return jnp.concatenate(outs, axis=1)


def _sgu_wmask(ws):
    t = lax.broadcasted_iota(jnp.int32, (SGU_T, SGU_T), 0)
    s = lax.broadcasted_iota(jnp.int32, (SGU_T, SGU_T), 1)
    return jnp.where((s <= t)[None], ws, 0.0).astype(BF16)


def _ln_stats(v):
    mu = jnp.mean(v, axis=-1, keepdims=True)
    vc = v - mu
    rstd = lax.rsqrt(jnp.mean(vc * vc, axis=-1, keepdims=True) + EPS)
    return vc * rstd, rstd


def layer_d_fwd(h, w, p):
    proj = mm(h, w['d_w_in'], 'nn', 'd_proj')
    bias = jnp.repeat(p['d_b_s'][0].T, SGU_C, axis=1)

    def f1(u_, v_, z, ws, lg, lb, bs):
        xh, _ = _ln_stats(gelu(v_))
        s = _sgu_mix(_sgu_wmask(ws), xh * lg + lb, False) + bs
        return [gelu(u_) * s * silu(z)], []
    (po,), _ = rowwise(f1, [rw(proj, 1024, 0), rw(proj, 1024, 1), rw(proj, 1024, 2)],
                       [p['d_w_s'][0], p['d_ln_g'], p['d_ln_b'], bias], [(1024, F32)], [], SGU_T, 'd_mix')
    yb = mm(po, w['d_w_out'], 'nn', 'd_out')
    return yb, dict(h=h, proj=proj, po=po, bias=bias)


def layer_d_bwd(dyb, w, p, sv):
    g = {}
    dpo = mm(dyb, w['d_w_out'], 'nt', 'd_dpo')
    g['d_w_out'] = mm(sv['po'], dyb, 'tn', 'd_dwout')
    proj = sv['proj']

    def f1(dpo_, u_, v_, z, ws, lg, lb, bs):
        wm = _sgu_wmask(ws)
        gv = gelu(v_)
        xh, rstd = _ln_stats(gv)
        vn = xh * lg + lb
        s = _sgu_mix(wm, vn, False) + bs
        gu, sz = gelu(u_), silu(z)
        du = dpo_ * s * sz
        ds = dpo_ * gu * sz
        dz = dpo_ * gu * s * silu_grad(z)
        dsb = ds.astype(BF16)
        dws = jnp.stack([lax.dot_general(dsb[:, gi * SGU_C:(gi + 1) * SGU_C], vn[:, gi * SGU_C:(gi + 1) * SGU_C].astype(BF16),
                                         (((1,), (1,)), ((), ())), preferred_element_type=F32) for gi in range(SGU_G)])
        dvn = _sgu_mix(wm, ds, True)
        dlg = jnp.sum(dvn * xh, axis=0, keepdims=True)
        dlb = jnp.sum(dvn, axis=0, keepdims=True)
        dxh = dvn * lg
        dgv = rstd * (dxh - jnp.mean(dxh, axis=-1, keepdims=True) - xh * jnp.mean(dxh * xh, axis=-1, keepdims=True))
        return ([jnp.concatenate([du * gelu_grad(u_), dgv * gelu_grad(v_), dz], axis=1)], [dws, ds, dlg, dlb])
    (dproj,), (dws, dbs, dlg, dlb) = rowwise(
        f1, [rw(dpo), rw(proj, 1024, 0), rw(proj, 1024, 1), rw(proj, 1024, 2)],
        [p['d_w_s'][0], p['d_ln_g'], p['d_ln_b'], sv['bias']], [(3072, F32)],
        [(SGU_G, SGU_T, SGU_T), (SGU_T, 1024), (1, 1024), (1, 1024)], SGU_T, 'd_mix_bwd')
    tril = np.tril(np.ones((SGU_T, SGU_T), dtype=bool))
    g['d_w_s'] = jnp.where(tril[None], dws, 0.0)[None]
    g['d_b_s'] = dbs.reshape(SGU_T, SGU_G, SGU_C).sum(-1).T[None]
    g['d_ln_g'], g['d_ln_b'] = dlg, dlb
    g['d_w_in'] = mm(sv['h'], dproj, 'tn', 'd_dwin')
    dh = mm(dproj, w['d_w_in'], 'nt', 'd_dh')
    return dh, g


def _coords():
    return lax.axis_index("x"), lax.axis_index("y"), lax.axis_index("c")


def all_gather(x, name):
    def body(x_ref, out_ref, send_sems, recv_sems, local_sem):
        x_, y_, c_ = _coords()
        me, sibling = (x_, y_, c_), (x_, y_, 1 - c_)
        chips = [(1 - x_, y_), (x_, 1 - y_), (1 - x_, 1 - y_)]

        def slot(px, py, pc):
            return out_ref.at[4 * px + 2 * py + pc]

        def copy(k, block, to, src=None):
            return pltpu.make_async_remote_copy(src_ref=slot(*block) if src is None else src, dst_ref=slot(*block),
                                                send_sem=send_sems.at[k], recv_sem=recv_sems.at[k], device_id=to,
                                                device_id_type=MESH)

        mine = pltpu.make_async_copy(x_ref, slot(*me), local_sem)
        mine.start()
        first = [copy(0, me, sibling, src=x_ref)]
        first += [copy(1 + j, me, (*chip, c_), src=x_ref) for j, chip in enumerate(chips)]
        for cp in first:
            cp.start()
        passed = [copy(4 + j, (*chip, c_), sibling) for j, chip in enumerate(chips)]
        for j, chip in enumerate(chips):
            copy(1 + j, (*chip, c_), me).wait_recv()
            passed[j].start()
        copy(0, sibling, me).wait_recv()
        for j, chip in enumerate(chips):
            copy(4 + j, (*chip, 1 - c_), me).wait_recv()
        for cp in first + passed:
            cp.wait_send()
        mine.wait()

    return pl.pallas_call(
        body, out_shape=jax.ShapeDtypeStruct((N_DEV,) + x.shape, x.dtype), in_specs=[ANY], out_specs=ANY,
        scratch_shapes=[pltpu.SemaphoreType.DMA((7,)), pltpu.SemaphoreType.DMA((7,)), pltpu.SemaphoreType.DMA(())],
        name=name)(x)


def rs_sibling(gfull):
    _, R, C = gfull.shape

    def body(g_ref, land_ref, send_sems, recv_sems):
        x_, y_, c_ = _coords()
        copies = []
        for k in range(4):
            cp = pltpu.make_async_remote_copy(src_ref=g_ref.at[2 * k + 1 - c_], dst_ref=land_ref.at[k],
                                              send_sem=send_sems.at[k], recv_sem=recv_sems.at[k],
                                              device_id=(x_, y_, 1 - c_), device_id_type=MESH)
            cp.start()
            copies.append(cp)
        for cp in copies:
            cp.wait_recv()
        for cp in copies:
            cp.wait_send()

    return pl.pallas_call(
        body, out_shape=jax.ShapeDtypeStruct((4, R, C), gfull.dtype), in_specs=[ANY], out_specs=ANY,
        scratch_shapes=[pltpu.SemaphoreType.DMA((4,)), pltpu.SemaphoreType.DMA((4,))], name='rs_sibling')(gfull)


def rs_pair_add(gfull, land, core):
    _, R, C = gfull.shape
    tl = R // 4
    assert R % 64 == 0

    def body(c_ref, g_ref, l_ref, o_ref):
        o_ref[...] = (g_ref[...] + l_ref[...]).astype(BF16)

    return pl.pallas_call(
        body, out_shape=jax.ShapeDtypeStruct((4, R, C), BF16),
        grid_spec=pltpu.PrefetchScalarGridSpec(
            num_scalar_prefetch=1, grid=(4, R // tl),
            in_specs=[pl.BlockSpec((1, tl, C), lambda k, i, c: (2 * k + c[0], i, 0)),
                      pl.BlockSpec((1, tl, C), lambda k, i, c: (k, i, 0))],
            out_specs=pl.BlockSpec((1, tl, C), lambda k, i, c: (k, i, 0))),
        compiler_params=pltpu.CompilerParams(dimension_semantics=("parallel", "parallel")), name='rs_pair_add')(
            core, gfull, land)


def rs_chips(part):
    _, R, C = part.shape

    def body(p_ref, land_ref, send_sems, recv_sems):
        x_, y_, c_ = _coords()
        copies = []
        for r, (fx, fy) in enumerate([(1, 0), (0, 1), (1, 1)]):
            tx = jnp.where(fx == 1, 1 - x_, x_)
            ty = jnp.where(fy == 1, 1 - y_, y_)
            cp = pltpu.make_async_remote_copy(src_ref=p_ref.at[2 * tx + ty], dst_ref=land_ref.at[r],
                                              send_sem=send_sems.at[r], recv_sem=recv_sems.at[r],
                                              device_id=(tx, ty, c_), device_id_type=MESH)
            cp.start()
            copies.append(cp)
        for cp in copies:
            cp.wait_recv()
        for cp in copies:
            cp.wait_send()

    return pl.pallas_call(
        body, out_shape=jax.ShapeDtypeStruct((3, R, C), part.dtype), in_specs=[ANY], out_specs=ANY,
        scratch_shapes=[pltpu.SemaphoreType.DMA((3,)), pltpu.SemaphoreType.DMA((3,))], name='rs_chips')(part)


def _adam(wv, gv, mv, vv):
    m = ADAM_B1 * mv + (1.0 - ADAM_B1) * gv
    v = ADAM_B2 * vv + (1.0 - ADAM_B2) * (gv * gv)
    m_hat = m / (1.0 - ADAM_B1 ** ADAM_STEP)
    v_hat = v / (1.0 - ADAM_B2 ** ADAM_STEP)
    delta = -ADAM_LR * (m_hat / (jnp.sqrt(v_hat) + ADAM_EPS) + ADAM_WD * wv)
    return delta, m, v


def _sum4(p_ref, l_ref):
    return ((p_ref[0].astype(F32) + l_ref[0].astype(F32)) + l_ref[1].astype(F32)) + l_ref[2].astype(F32)


def rs_rep_sum(part, land, chip):
    def body(c_ref, p_ref, l_ref, o_ref):
        o_ref[...] = _sum4(p_ref, l_ref)

    return pl.pallas_call(
        body, out_shape=jax.ShapeDtypeStruct((REP_SLOT, LANES), F32),
        grid_spec=pltpu.PrefetchScalarGridSpec(
            num_scalar_prefetch=1, grid=(REP_SLOT // RS_TL,),
            in_specs=[pl.BlockSpec((1, RS_TL, LANES), lambda i, c: (c[0], i, 0)),
                      pl.BlockSpec((3, RS_TL, LANES), lambda i, c: (0, i, 0))],
            out_specs=pl.BlockSpec((RS_TL, LANES), lambda i, c: (i, 0))),
        compiler_params=pltpu.CompilerParams(dimension_semantics=("parallel",)), name='rs_rep')(chip, part, land)


def adam_param(name, shape, off, w, m, v, chip, part=None, land=None, grep=None):
    r, c = shape
    rp, nt, rb = _tiles(shape)
    rbw = min(r, rb)
    n_src = 2 if grep is None else 1

    def body(c_ref, *refs):
        srcs = refs[:n_src * nt]
        w_ref, m_ref, v_ref, g_ref, d_ref, nm_ref, nv_ref = refs[n_src * nt:]
        if grep is None:
            tiles = [_sum4(srcs[2 * t], srcs[2 * t + 1]) for t in range(nt)]
        else:
            tiles = [srcs[t][...] for t in range(nt)]
        g = (tiles[0] if nt == 1 else jnp.concatenate(tiles, axis=1))[:rbw, :c]
        g_ref[...] = g
        d_ref[...], nm_ref[...], nv_ref[...] = _adam(w_ref[...], g, m_ref[...], v_ref[...])

    in_specs, args = [], []
    for t in range(nt):
        b0 = (off + t * rp) // rb
        assert (off + t * rp) % rb == 0
        if grep is None:
            in_specs += [pl.BlockSpec((1, rb, LANES), functools.partial(lambda i, cr, b0: (cr[0], b0 + i, 0), b0=b0)),
                         pl.BlockSpec((3, rb, LANES), functools.partial(lambda i, cr, b0: (0, b0 + i, 0), b0=b0))]
            args += [part, land]
        else:
            in_specs.append(pl.BlockSpec((rb, LANES), functools.partial(lambda i, cr, b0: (b0 + i, 0), b0=b0)))
            args.append(grep)
    nat = pl.BlockSpec((rbw, c), lambda i, cr: (i, 0))
    return pl.pallas_call(
        body, out_shape=[jax.ShapeDtypeStruct((r, c), F32)] * 4,
        grid_spec=pltpu.PrefetchScalarGridSpec(num_scalar_prefetch=1, grid=(rp // rb,), in_specs=in_specs + [nat] * 3,
                                               out_specs=[nat] * 4),
        compiler_params=pltpu.CompilerParams(dimension_semantics=("parallel",)), name='adam_' + name)(
            chip, *args, w, m, v)


def _to_tiles(a, shape):
    r, c = shape
    rp, nt, _ = _tiles(shape)
    lead = [(0, 0)] * (a.ndim - 2)
    a = jnp.pad(a, lead + [(0, rp - r), (0, nt * LANES - c)])
    return a if nt == 1 else jnp.concatenate([a[..., t * LANES:(t + 1) * LANES] for t in range(nt)], axis=-2)


def _from_tiles(g, off, shape):
    r, c = shape
    rp, nt, _ = _tiles(shape)
    tiles = [g[..., off + t * rp:off + t * rp + r, :] for t in range(nt)]
    return (tiles[0] if nt == 1 else jnp.concatenate(tiles, axis=-1))[..., :c]


def _pack_small(blocks, order, rows, width, dtype):
    flat = jnp.concatenate([blocks[n].reshape(-1).astype(dtype) for n in order])
    return jnp.pad(flat, (0, rows * width - flat.shape[0])).reshape(rows, width)


def _device_blocks(name, gfull):
    (r, c), ax = SHARDED[name]
    br, bc = _block_shape(name)
    if ax == 0:
        return gfull.reshape(N_DEV, br, bc)
    return gfull.reshape(r, N_DEV, bc).transpose(1, 0, 2)


def _assemble(gathered, name):
    (r, c), ax = SHARDED[name]
    blk = _from_tiles(gathered, SH_OFF[name], _block_shape(name))
    return blk.reshape(r, c) if ax == 0 else blk.transpose(1, 0, 2).reshape(r, c)


def kernel(x, pre_norm, post_norm, rel_bias, a_w_in, a_lam_re, a_lam_im, a_log_dt, a_b_re, a_b_im, a_c_re, a_c_im, a_d, a_w_glu, a_b_glu, a_w_out, b_w_in, b_sinks, b_w_out, c_w_in, c_q_norm, c_kv_norm, c_w_uq, c_w_ukv, c_w_out, d_w_in, d_ln_g, d_ln_b, d_w_s, d_b_s, d_w_out, loss_target, m_pre_norm, m_post_norm, m_rel_bias, m_a_w_in, m_a_lam_re, m_a_lam_im, m_a_log_dt, m_a_b_re, m_a_b_im, m_a_c_re, m_a_c_im, m_a_d, m_a_w_glu, m_a_b_glu, m_a_w_out, m_b_w_in, m_b_sinks, m_b_w_out, m_c_w_in, m_c_q_norm, m_c_kv_norm, m_c_w_uq, m_c_w_ukv, m_c_w_out, m_d_w_in, m_d_ln_g, m_d_ln_b, m_d_w_s, m_d_b_s, m_d_w_out, v_pre_norm, v_post_norm, v_rel_bias, v_a_w_in, v_a_lam_re, v_a_lam_im, v_a_log_dt, v_a_b_re, v_a_b_im, v_a_c_re, v_a_c_im, v_a_d, v_a_w_glu, v_a_b_glu, v_a_w_out, v_b_w_in, v_b_sinks, v_b_w_out, v_c_w_in, v_c_q_norm, v_c_kv_norm, v_c_w_uq, v_c_w_ukv, v_c_w_out, v_d_w_in, v_d_ln_g, v_d_ln_b, v_d_w_s, v_d_b_s, v_d_w_out):
    loc = locals()
    P = {n: loc[n] for n in WEIGHTS}
    M = {n: loc['m_' + n] for n in WEIGHTS}
    V = {n: loc['v_' + n] for n in WEIGHTS}
    xs = x[0]
    L = xs.shape[0]

    blocks = {n: P[n].reshape(_block_shape(n)) for n in SHARDED}
    packed = jnp.concatenate([_to_tiles(blocks[n].astype(BF16), _block_shape(n)) for n in SH_ORDER], axis=0)
    gathered = all_gather(packed, 'ag_weights')
    small = all_gather(_pack_small(blocks, SHARDED_F32, SMALL_ROWS, 128, F32), 'ag_small')
    W = {n: _assemble(gathered, n) for n in SHARDED if n not in SHARDED_F32}
    Pl = dict(P)
    for n in SHARDED_F32:
        c = SHARDED[n][0][1]
        bc = c // N_DEV
        Pl[n] = small.reshape(N_DEV, -1)[:, SMALL_OFF[n]:SMALL_OFF[n] + bc].reshape(1, c)
    W['b_w_in'] = jnp.concatenate([W['b_w_in'][:, 1280:], W['b_w_in'][:, :1280]], axis=1)
    W['c_w_in'] = _perm_c_w_in(W['c_w_in'])
    W['c_w_uq'] = _perm_w_uq(W['c_w_uq'])
    W['c_w_ukv'] = _perm_w_ukv(W['c_w_ukv'])

    fwd = [layer_a_fwd, layer_b_fwd, layer_c_fwd, layer_d_fwd]
    bwd = [layer_a_bwd, layer_b_bwd, layer_c_bwd, layer_d_bwd]
    saved = []
    xc = xs
    for i in range(4):
        def fpre(x_, g_):
            return [rms_fwd(x_, g_)], []
        (h,), _ = rowwise(fpre, [rw(xc)], [P['pre_norm'][i:i + 1]], [(D_MODEL, F32)], [], 256, f'pre_norm{i}')
        yb, sv = fwd[i](h, W, Pl)

        def fpost(x_, y_, g_):
            return [x_ + rms_fwd(y_, g_)], []
        (xn,), _ = rowwise(fpost, [rw(xc), rw(yb)], [P['post_norm'][i:i + 1]], [(D_MODEL, F32)], [], 256, f'post_norm{i}')
        sv['x'], sv['yb'] = xc, yb
        saved.append(sv)
        xc = xn

    def floss(y_, t_):
        d = y_ - t_
        return [d * (1.0 / D_MODEL)], [0.5 * jnp.sum(jnp.sum(d * d, axis=-1, keepdims=True) * (1.0 / D_MODEL), axis=0,
                                                      keepdims=True)]
    (dx,), (loss_loc,) = rowwise(floss, [rw(xc), rw(loss_target[0])], [], [(D_MODEL, F32)], [(1, 1)], 256, 'loss')
    loss = lax.psum(loss_loc[0, 0], ("x", "y", "c"))

    G = {}
    dpre, dpost = [None] * 4, [None] * 4
    for i in reversed(range(4)):
        sv = saved[i]

        def fpost_b(y_, d_, g_):
            dy, dg = rms_bwd(y_, g_, d_)
            return [dy], [dg]
        (dyb,), (dpost[i],) = rowwise(fpost_b, [rw(sv['yb']), rw(dx)], [P['post_norm'][i:i + 1]], [(D_MODEL, F32)],
                                      [(1, D_MODEL)], 256, f'post_norm_bwd{i}')
        dh, g = bwd[i](dyb, W, Pl, sv)
        G.update(g)

        def fpre_b(x_, dh_, d_, g_):
            dxl, dg = rms_bwd(x_, g_, dh_)
            return [d_ + dxl], [dg]
        (dx,), (dpre[i],) = rowwise(fpre_b, [rw(sv['x']), rw(dh), rw(dx)], [P['pre_norm'][i:i + 1]], [(D_MODEL, F32)],
                                    [(1, D_MODEL)], 256, f'pre_norm_bwd{i}')
    G['pre_norm'] = jnp.concatenate(dpre, axis=0)
    G['post_norm'] = jnp.concatenate(dpost, axis=0)
    G['b_w_in'] = jnp.concatenate([G['b_w_in'][:, 1024:], G['b_w_in'][:, :1024]], axis=1)
    G['c_w_in'] = _unperm_c_w_in(G['c_w_in'])
    G['c_w_uq'] = _unperm_w_uq(G['c_w_uq'])
    G['c_w_ukv'] = _unperm_w_ukv(G['c_w_ukv'])

    rep = jnp.concatenate([_to_tiles(G[n].reshape(s), s) for n, s in REP_SHAPE.items()], axis=0)
    rep = jnp.pad(rep.reshape(N_DEV, REP_CHUNK, LANES), ((0, 0), (0, REP_SLOT - REP_CHUNK), (0, 0)))
    sh = [_to_tiles(_device_blocks(n, G[n]), _block_shape(n)) for n in SH_ORDER]
    tail = jnp.zeros((N_DEV, RS_ROWS - REP_SLOT - SH_ROWS, LANES), F32)
    gfull = jnp.concatenate([rep] + sh + [tail], axis=1)
    cx, cy, cc = _coords()
    core = jnp.reshape(cc, (1,)).astype(jnp.int32)
    chip = jnp.reshape(2 * cx + cy, (1,)).astype(jnp.int32)
    land = rs_sibling(gfull)
    part = rs_pair_add(gfull, land, core)
    land2 = rs_chips(part)
    out = {}
    for n in SH_ORDER:
        s = _block_shape(n)
        out[n] = adam_param(n, s, REP_SLOT + SH_OFF[n], blocks[n], M[n].reshape(s), V[n].reshape(s), chip,
                            part=part, land=land2)

    grep = all_gather(rs_rep_sum(part, land2, chip), 'ag_rep')[:, :REP_CHUNK].reshape(REP_ROWS, LANES)
    for n, s in REP_SHAPE.items():
        out[n] = adam_param(n, s, REP_OFF[n], P[n].reshape(s), M[n].reshape(s), V[n].reshape(s), chip, grep=grep)
    res = [loss, dx[None]]
    for kind in range(4):
        res += [out[n][kind].reshape(P[n].shape) for n in WEIGHTS]
    return tuple(res)
```

```python
import functools
import math

import numpy as np
import jax
import jax.numpy as jnp
from jax import lax
from jax.experimental import pallas as pl
from jax.experimental.pallas import tpu as pltpu

F32 = jnp.float32
BF16 = jnp.bfloat16
MESH = pl.DeviceIdType.MESH
ANY = pl.BlockSpec(memory_space=pl.ANY)

N_DEV = 8
D_MODEL = 1024
EPS = 1e-6
NEG_INF = -1e30
SSM_G, SSM_P, SSM_H = 64, 64, 16
SSM_T = 256
SSM_WC = 512
HEAD_DIM = 64
SWA_HEADS, SWA_KV = 16, 2
WINDOW = 128
REL_BUCKETS, REL_MAX_DIST = 32, 128
MLA_HEADS, MLA_NOPE, MLA_ROPE, MLA_V = 16, 64, 32, 64
MLA_Q_RANK, MLA_KV_RANK = 768, 256
ROPE_BASE = 10000.0
SGU_G, SGU_C, SGU_T = 16, 64, 128
ADAM_LR, ADAM_B1, ADAM_B2, ADAM_EPS, ADAM_WD, ADAM_STEP = 0.001, 0.9, 0.999, 1e-08, 0.01, 10

WEIGHTS = ['pre_norm', 'post_norm', 'rel_bias', 'a_w_in', 'a_lam_re', 'a_lam_im', 'a_log_dt', 'a_b_re', 'a_b_im',
           'a_c_re', 'a_c_im', 'a_d', 'a_w_glu', 'a_b_glu', 'a_w_out', 'b_w_in', 'b_sinks', 'b_w_out', 'c_w_in',
           'c_q_norm', 'c_kv_norm', 'c_w_uq', 'c_w_ukv', 'c_w_out', 'd_w_in', 'd_ln_g', 'd_ln_b', 'd_w_s', 'd_b_s',
           'd_w_out']
SHARDED = {'a_w_in': ((1024, 2048), 1), 'a_w_glu': ((1024, 1024), 0), 'a_w_out': ((1024, 1024), 0),
           'b_w_in': ((1024, 2304), 1), 'b_w_out': ((1024, 1024), 0), 'c_w_in': ((1024, 2080), 1),
           'c_q_norm': ((1, 768), 1), 'c_kv_norm': ((1, 256), 1), 'c_w_uq': ((768, 1536), 1),
           'c_w_ukv': ((256, 2048), 1), 'c_w_out': ((1024, 1024), 0), 'd_w_in': ((1024, 3072), 1),
           'd_ln_g': ((1, 1024), 1), 'd_ln_b': ((1, 1024), 1), 'd_w_out': ((1024, 1024), 0)}
SHARDED_F32 = ['c_q_norm', 'c_kv_norm', 'd_ln_g', 'd_ln_b']
REPLICATED = [n for n in WEIGHTS if n not in SHARDED]


def _cdiv(a, b):
    return -(-a // b)


def _block_shape(name):
    (r, c), ax = SHARDED[name]
    return (r // N_DEV, c) if ax == 0 else (r, c // N_DEV)


LANES = 128
SH_ORDER = ['a_w_in', 'b_w_in', 'c_w_in', 'd_w_in', 'c_w_uq', 'c_w_ukv', 'a_w_glu', 'a_w_out', 'b_w_out', 'c_w_out',
            'd_w_out', 'c_q_norm', 'c_kv_norm', 'd_ln_g', 'd_ln_b']


def _tiles(shape):
    r, c = shape
    rp = max(r, 8)
    rb = 512 if rp % 512 == 0 else 256 if rp % 256 == 0 else rp
    return rp, _cdiv(c, LANES), rb


SH_OFF = {}
_o = 0
for _n in SH_ORDER:
    _rp, _nt, _rb = _tiles(_block_shape(_n))
    assert _o % _rb == 0
    SH_OFF[_n] = _o
    _o += _rp * _nt
SH_ROWS = _o
assert SH_ROWS % 16 == 0

REP_SHAPE = {'a_b_re': (4096, 16), 'a_b_im': (4096, 16), 'd_w_s': (2048, 128), 'a_c_re': (1024, 64),
             'a_c_im': (1024, 64), 'pre_norm': (4, 1024), 'post_norm': (4, 1024), 'a_lam_re': (64, 64),
             'a_lam_im': (64, 64), 'a_d': (1, 1024), 'a_b_glu': (1, 1024), 'rel_bias': (32, 16), 'd_b_s': (16, 128),
             'a_log_dt': (1, 64), 'b_sinks': (1, 16)}
REP_OFF = {}
_o = 0
for _n, _s in REP_SHAPE.items():
    _rp, _nt, _rb = _tiles(_s)
    assert _o % _rb == 0
    REP_OFF[_n] = _o
    _o += _rp * _nt
REP_ROWS = _o
REP_CHUNK = REP_ROWS // N_DEV
assert REP_ROWS % (8 * N_DEV) == 0
REP_SLOT = 2048
RS_TL = 512
RS_ROWS = _cdiv(REP_SLOT + SH_ROWS, RS_TL) * RS_TL
SMALL_OFF = {}
_o = 0
for _n in SHARDED_F32:
    SMALL_OFF[_n] = _o
    _o += int(np.prod(_block_shape(_n)))
SMALL_ROWS = _cdiv(_o, 128 * 8) * 8


def _pick(n, cands):
    for c in cands:
        if n % c == 0:
            return c
    return n


def mm(a, b, mode, name, out_dtype=F32):
    if mode == 'nn':
        (M, K), (K2, N) = a.shape, b.shape
    elif mode == 'nt':
        (M, K), (N, K2) = a.shape, b.shape
    else:
        (K, M), (K2, N) = a.shape, b.shape
    assert K == K2, (name, a.shape, b.shape)
    tm = _pick(M, (512, 256, 128))
    tn = _pick(N, (512, 384, 256))
    dims = {'nn': ((1,), (0,)), 'nt': ((1,), (1,)), 'tn': ((0,), (0,))}[mode]

    def body(a_ref, b_ref, o_ref):
        o_ref[...] = lax.dot_general(a_ref[...].astype(BF16), b_ref[...].astype(BF16), (dims, ((), ())),
                                     preferred_element_type=F32).astype(out_dtype)

    a_spec = pl.BlockSpec((K, tm), lambda i, j: (0, i)) if mode == 'tn' else pl.BlockSpec((tm, K), lambda i, j: (i, 0))
    b_spec = pl.BlockSpec((tn, K), lambda i, j: (j, 0)) if mode == 'nt' else pl.BlockSpec((K, tn), lambda i, j: (0, j))
    return pl.pallas_call(
        body, grid=(M // tm, N // tn), in_specs=[a_spec, b_spec],
        out_specs=pl.BlockSpec((tm, tn), lambda i, j: (i, j)), out_shape=jax.ShapeDtypeStruct((M, N), out_dtype),
        compiler_params=pltpu.CompilerParams(dimension_semantics=("parallel", "parallel")), name=name)(a, b)


def rw(arr, width=None, cb=0):
    return (arr, arr.shape[1] if width is None else width, cb)


def rowwise(fn, rows, consts, outs, accs, tl, name, n_steps=None):
    if n_steps is None:
        n_steps = [r[0].shape[0] for r in rows if not isinstance(r[1], pl.BlockSpec)][0] // tl
    L = n_steps * tl
    nr, nc, no, na = len(rows), len(consts), len(outs), len(accs)
    in_specs, args = [], []
    for r in rows:
        if isinstance(r[1], pl.BlockSpec):
            in_specs.append(r[1])
        else:
            in_specs.append(pl.BlockSpec((tl, r[1]), functools.partial(lambda i, cb: (i, cb), cb=r[2])))
        args.append(r[0])
    for c in consts:
        in_specs.append(pl.BlockSpec(c.shape, functools.partial(lambda i, nd: (0,) * nd, nd=c.ndim)))
        args.append(c)
    out_specs = [pl.BlockSpec((tl, w), lambda i: (i, 0)) for w, _ in outs]
    out_shape = [jax.ShapeDtypeStruct((L, w), dt) for w, dt in outs]
    for s in accs:
        out_specs.append(pl.BlockSpec(s, functools.partial(lambda i, nd: (0,) * nd, nd=len(s))))
        out_shape.append(jax.ShapeDtypeStruct(s, F32))

    def body(*refs):
        ins = [r[...] for r in refs[:nr + nc]]
        o_refs = refs[nr + nc:nr + nc + no]
        a_refs = refs[nr + nc + no:]
        o_vals, a_vals = fn(*ins)
        for ref, val in zip(o_refs, o_vals):
            ref[...] = val.astype(ref.dtype)
        if na:
            @pl.when(pl.program_id(0) == 0)
            def _():
                for ref in a_refs:
                    ref[...] = jnp.zeros_like(ref)
            for ref, val in zip(a_refs, a_vals):
                ref[...] += val

    res = pl.pallas_call(
        body, grid=(n_steps,), in_specs=in_specs, out_specs=out_specs, out_shape=out_shape,
        compiler_params=pltpu.CompilerParams(dimension_semantics=("arbitrary",)), name=name)(*args)
    return res[:no], res[no:]


_K0 = math.sqrt(2.0 / math.pi)
_K1 = 0.044715


def gelu(x):
    return x * (0.5 * (1.0 + jnp.tanh(_K0 * (x + _K1 * (x * x * x)))))


def gelu_grad(x):
    t = jnp.tanh(_K0 * (x + _K1 * (x * x * x)))
    return 0.5 * (1.0 + t) + 0.5 * x * (1.0 - t * t) * (_K0 * (1.0 + 3.0 * _K1 * x * x))


def sigmoid(x):
    return 1.0 / (1.0 + jnp.exp(-x))


def silu(z):
    return z * sigmoid(z)


def silu_grad(z):
    s = sigmoid(z)
    return s * (1.0 + z * (1.0 - s))


def rms_fwd(x, g):
    r = lax.rsqrt(jnp.mean(x * x, axis=-1, keepdims=True) + EPS)
    return x * r * g


def rms_bwd(x, g, dy):
    r = lax.rsqrt(jnp.mean(x * x, axis=-1, keepdims=True) + EPS)
    xh = x * r
    dg = jnp.sum(dy * xh, axis=0, keepdims=True)
    dxh = dy * g
    dx = r * (dxh - xh * jnp.mean(dxh * xh, axis=-1, keepdims=True))
    return dx, dg


def _scan_chunk(a_r, a_i, pr_ref, pi_ref, cr, ci, T, reverse):
    row = lax.broadcasted_iota(jnp.int32, a_r.shape, 0)
    sgn = -1.0 if reverse else 1.0
    d = 1
    while d < T:
        k = (T - d) if reverse else (d - 1)
        wr = pr_ref[k:k + 1, :]
        wi = sgn * pi_ref[k:k + 1, :]
        if reverse:
            yr, yi, keep = pltpu.roll(a_r, T - d, 0), pltpu.roll(a_i, T - d, 0), row < T - d
        else:
            yr, yi, keep = pltpu.roll(a_r, d, 0), pltpu.roll(a_i, d, 0), row >= d
        a_r, a_i = (a_r + jnp.where(keep, wr * yr - wi * yi, 0.0), a_i + jnp.where(keep, wr * yi + wi * yr, 0.0))
        d *= 2
    wr = pr_ref[...]
    wi = sgn * pi_ref[...]
    c_r, c_i = cr[...], ci[...]
    a_r, a_i = a_r + (wr * c_r - wi * c_i), a_i + (wr * c_i + wi * c_r)
    k = 0 if reverse else T - 1
    cr[...] = a_r[k:k + 1, :]
    ci[...] = a_i[k:k + 1, :]
    return a_r, a_i


_NT = (((1,), (1,)), ((), ()))
_TN = (((0,), (0,)), ((), ()))


def s5_fwd(proj, d_skip, Bre, Bim, Cre, Cim, pr, pi):
    L = proj.shape[0]
    T, WC = min(SSM_T, L), SSM_WC
    nT = L // T

    def body(u_ref, d_ref, bre_ref, bim_ref, cre_ref, cim_ref, pr_ref, pi_ref, y_ref, yg_ref, sr_ref, si_ref, cr, ci):
        @pl.when(pl.program_id(1) == 0)
        def _():
            cr[...] = jnp.zeros_like(cr)
            ci[...] = jnp.zeros_like(ci)

        u = u_ref[...]
        ub = u.astype(BF16)
        a_r = jnp.dot(ub, bre_ref[0].astype(BF16), preferred_element_type=F32)
        a_i = jnp.dot(ub, bim_ref[0].astype(BF16), preferred_element_type=F32)
        a_r, a_i = _scan_chunk(a_r, a_i, pr_ref, pi_ref, cr, ci, T, False)
        sr_ref[...] = a_r
        si_ref[...] = a_i
        y = (jnp.dot(a_r.astype(BF16), cre_ref[0].astype(BF16), preferred_element_type=F32)
             + jnp.dot(a_i.astype(BF16), cim_ref[0].astype(BF16), preferred_element_type=F32) + d_ref[...] * u)
        y_ref[...] = y
        yg_ref[...] = gelu(y)

    uspec = pl.BlockSpec((T, 128), lambda k, i: (i, k))
    sspec = pl.BlockSpec((T, WC), lambda k, i: (i, k))
    return pl.pallas_call(
        body, grid=(8, nT),
        in_specs=[uspec, pl.BlockSpec((1, 128), lambda k, i: (0, k)),
                  pl.BlockSpec((1, 128, WC), lambda k, i: (k, 0, 0)), pl.BlockSpec((1, 128, WC), lambda k, i: (k, 0, 0)),
                  pl.BlockSpec((1, WC, 128), lambda k, i: (k, 0, 0)), pl.BlockSpec((1, WC, 128), lambda k, i: (k, 0, 0)),
                  pl.BlockSpec((T, WC), lambda k, i: (0, k)), pl.BlockSpec((T, WC), lambda k, i: (0, k))],
        out_specs=[uspec, uspec, sspec, sspec],
        out_shape=[jax.ShapeDtypeStruct((L, 1024), F32)] * 2 + [jax.ShapeDtypeStruct((L, 8 * WC), F32)] * 2,
        scratch_shapes=[pltpu.VMEM((1, WC), F32), pltpu.VMEM((1, WC), F32)],
        compiler_params=pltpu.CompilerParams(dimension_semantics=("parallel", "arbitrary")), name='a_ssm')(
            proj, d_skip, Bre, Bim, Cre, Cim, pr, pi)


def s5_bwd(proj, dyg1, dyg2, y, d_skip, s_re, s_im, Bre, Bim, Cre, Cim, prr, pir):
    L = proj.shape[0]
    T, WC = min(SSM_T, L), SSM_WC
    nT = L // T

    def body(u_ref, g1_ref, g2_ref, y_ref, d_ref, sr_ref, si_ref, spr_ref, spi_ref, bre_ref, bim_ref, cre_ref, cim_ref,
             pr_ref, pi_ref, du_ref, dd_ref, dbre_ref, dbim_ref, dcre_ref, dcim_ref, dar_ref, dai_ref, cr, ci):
        i = pl.program_id(1)

        @pl.when(i == 0)
        def _():
            for ref in (cr, ci, dd_ref, dbre_ref, dbim_ref, dcre_ref, dcim_ref, dar_ref, dai_ref):
                ref[...] = jnp.zeros_like(ref)

        u = u_ref[...]
        dy = (g1_ref[...] + g2_ref[...]) * gelu_grad(y_ref[...])
        dd_ref[...] += jnp.sum(dy * u, axis=0, keepdims=True)
        dyb, ub = dy.astype(BF16), u.astype(BF16)
        bre, bim, cre, cim = (r[0].astype(BF16) for r in (bre_ref, bim_ref, cre_ref, cim_ref))
        g_r = lax.dot_general(dyb, cre, _NT, preferred_element_type=F32)
        g_i = lax.dot_general(dyb, cim, _NT, preferred_element_type=F32)
        g_r, g_i = _scan_chunk(g_r, g_i, pr_ref, pi_ref, cr, ci, T, True)
        s_r, s_i = sr_ref[...], si_ref[...]
        row = lax.broadcasted_iota(jnp.int32, (T, WC), 0)
        first = (nT - 1 - i) == 0
        sp_r = jnp.where(row == 0, jnp.where(first, 0.0, spr_ref[7:8, :]), pltpu.roll(s_r, 1, 0))
        sp_i = jnp.where(row == 0, jnp.where(first, 0.0, spi_ref[7:8, :]), pltpu.roll(s_i, 1, 0))
        dar_ref[...] += jnp.sum(g_r * sp_r + g_i * sp_i, axis=0, keepdims=True)
        dai_ref[...] += jnp.sum(g_i * sp_r - g_r * sp_i, axis=0, keepdims=True)
        grb, gib = g_r.astype(BF16), g_i.astype(BF16)
        dcre_ref[0] += lax.dot_general(s_r.astype(BF16), dyb, _TN, preferred_element_type=F32)
        dcim_ref[0] += lax.dot_general(s_i.astype(BF16), dyb, _TN, preferred_element_type=F32)
        dbre_ref[0] += lax.dot_general(ub, grb, _TN, preferred_element_type=F32)
        dbim_ref[0] += lax.dot_general(ub, gib, _TN, preferred_element_type=F32)
        du_ref[...] = (dy * d_ref[...] + lax.dot_general(grb, bre, _NT, preferred_element_type=F32)
                       + lax.dot_general(gib, bim, _NT, preferred_element_type=F32))

    uspec = pl.BlockSpec((T, 128), lambda k, i: (nT - 1 - i, k))
    sspec = pl.BlockSpec((T, WC), lambda k, i: (nT - 1 - i, k))
    pspec = pl.BlockSpec((8, WC), lambda k, i: (jnp.maximum((nT - 1 - i) * (T // 8) - 1, 0), k))
    tab = pl.BlockSpec((T, WC), lambda k, i: (0, k))
    bspec = pl.BlockSpec((1, 128, WC), lambda k, i: (k, 0, 0))
    cspec = pl.BlockSpec((1, WC, 128), lambda k, i: (k, 0, 0))
    return pl.pallas_call(
        body, grid=(8, nT),
        in_specs=[uspec, uspec, uspec, uspec, pl.BlockSpec((1, 128), lambda k, i: (0, k)), sspec, sspec, pspec, pspec,
                  bspec, bspec, cspec, cspec, tab, tab],
        out_specs=[uspec, pl.BlockSpec((1, 128), lambda k, i: (0, k)), bspec, bspec, cspec, cspec,
                   pl.BlockSpec((1, WC), lambda k, i: (0, k)), pl.BlockSpec((1, WC), lambda k, i: (0, k))],
        out_shape=[jax.ShapeDtypeStruct((L, 1024), F32), jax.ShapeDtypeStruct((1, 1024), F32),
                   jax.ShapeDtypeStruct((8, 128, WC), F32), jax.ShapeDtypeStruct((8, 128, WC), F32),
                   jax.ShapeDtypeStruct((8, WC, 128), F32), jax.ShapeDtypeStruct((8, WC, 128), F32),
                   jax.ShapeDtypeStruct((1, 8 * WC), F32), jax.ShapeDtypeStruct((1, 8 * WC), F32)],
        scratch_shapes=[pltpu.VMEM((1, WC), F32), pltpu.VMEM((1, WC), F32)],
        compiler_params=pltpu.CompilerParams(dimension_semantics=("parallel", "arbitrary")), name='a_ssm_bwd')(
            proj, dyg1, dyg2, y, d_skip, s_re, s_im, s_re, s_im, Bre, Bim, Cre, Cim, prr, pir)


def s5_discretize(lam_re, lam_im, log_dt, b_re, b_im):
    dt = jnp.exp(log_dt)[:, None]
    mag = jnp.exp(lam_re * dt)
    ab_re = mag * jnp.cos(lam_im * dt)
    ab_im = mag * jnp.sin(lam_im * dt)
    den = lam_re * lam_re + lam_im * lam_im
    nr = ab_re - 1.0
    f_re = (nr * lam_re + ab_im * lam_im) / den
    f_im = (ab_im * lam_re - nr * lam_im) / den
    bb_re = f_re[..., None] * b_re - f_im[..., None] * b_im
    bb_im = f_re[..., None] * b_im + f_im[..., None] * b_re
    return ab_re, ab_im, bb_re, bb_im


_EYE8 = np.eye(8, dtype=np.float32)


def _b_tiles(bb):
    t = bb.transpose(0, 2, 1).reshape(8, 8, SSM_H, SSM_P)
    return jnp.einsum('kghp,gG->kghGp', t, _EYE8).reshape(8, 8 * SSM_H, 8 * SSM_P)


def _b_untile(d):
    t = jnp.einsum('kghGp,gG->kghp', d.reshape(8, 8, SSM_H, 8, SSM_P), _EYE8)
    return t.reshape(SSM_G, SSM_H, SSM_P).transpose(0, 2, 1)


def _c_tiles(c):
    t = c.transpose(0, 2, 1).reshape(8, 8, SSM_P, SSM_H)
    return jnp.einsum('kgph,gG->kgpGh', t, _EYE8).reshape(8, 8 * SSM_P, 8 * SSM_H)


def _c_untile(d):
    t = jnp.einsum('kgpGh,gG->kgph', d.reshape(8, 8, SSM_P, 8, SSM_H), _EYE8)
    return t.reshape(SSM_G, SSM_P, SSM_H).transpose(0, 2, 1)


def s5_powers(ar, ai, T):
    W = ar.shape[1]

    def body(ar_ref, ai_ref, fr_ref, fi_ref, rr_ref, ri_ref):
        fr_ref[0:1, :] = ar_ref[...]
        fi_ref[0:1, :] = ai_ref[...]
        rr_ref[T - 1:T, :] = ar_ref[...]
        ri_ref[T - 1:T, :] = ai_ref[...]
        n = 1
        while n < T:
            cr, ci = fr_ref[0:n, :], fi_ref[0:n, :]
            lr, li = fr_ref[n - 1:n, :], fi_ref[n - 1:n, :]
            fr_ref[n:2 * n, :] = cr * lr - ci * li
            fi_ref[n:2 * n, :] = cr * li + ci * lr
            cr, ci = rr_ref[T - n:T, :], ri_ref[T - n:T, :]
            rr_ref[T - 2 * n:T - n, :] = cr * lr - ci * li
            ri_ref[T - 2 * n:T - n, :] = cr * li + ci * lr
            n *= 2

    spec = pl.BlockSpec((T, SSM_WC), lambda j: (0, j))
    aspec = pl.BlockSpec((1, SSM_WC), lambda j: (0, j))
    return pl.pallas_call(
        body, grid=(W // SSM_WC,), in_specs=[aspec, aspec], out_specs=[spec] * 4,
        out_shape=[jax.ShapeDtypeStruct((T, W), F32)] * 4,
        compiler_params=pltpu.CompilerParams(dimension_semantics=("parallel",)), name='a_powers')(ar, ai)


def layer_a_fwd(h, w, p):
    L = h.shape[0]
    proj = mm(h, w['a_w_in'], 'nn', 'a_proj')
    disc = lambda *a: s5_discretize(*a)
    (ab_re, ab_im, bb_re, bb_im), disc_vjp = jax.vjp(disc, p['a_lam_re'][0], p['a_lam_im'][0], p['a_log_dt'][0],
                                                     p['a_b_re'][0], p['a_b_im'][0])
    Bre, Bim = _b_tiles(bb_re), _b_tiles(bb_im)
    Cre, Cim = _c_tiles(p['a_c_re'][0]), -_c_tiles(p['a_c_im'][0])
    T = min(SSM_T, L)
    pr, pi, prr, pir = s5_powers(ab_re.reshape(1, -1), ab_im.reshape(1, -1), T)
    y, yg, s_re, s_im = s5_fwd(proj, p['a_d'], Bre, Bim, Cre, Cim, pr, pi)
    gl = mm(yg, w['a_w_glu'], 'nn', 'a_glu')

    def f2(yg_, gl_, z, bg):
        return [yg_ * sigmoid(gl_ + bg) * silu(z)], []
    (po,), _ = rowwise(f2, [rw(yg), rw(gl), rw(proj, 1024, 1)], [p['a_b_glu']], [(1024, F32)], [], 256, 'a_gate')
    yb = mm(po, w['a_w_out'], 'nn', 'a_out')
    saved = dict(h=h, proj=proj, disc_vjp=disc_vjp, Bre=Bre, Bim=Bim, Cre=Cre, Cim=Cim, prr=prr, pir=pir, s_re=s_re,
                 s_im=s_im, y=y, yg=yg, gl=gl, po=po)
    return yb, saved


def layer_a_bwd(dyb, w, p, sv):
    g = {}
    dpo = mm(dyb, w['a_w_out'], 'nt', 'a_dpo')
    g['a_w_out'] = mm(sv['po'], dyb, 'tn', 'a_dwout')
    proj = sv['proj']

    def f1(dpo_, yg, gl, z, bg):
        sg = sigmoid(gl + bg)
        sz = silu(z)
        dm = dpo_ * sz
        dz = dpo_ * (yg * sg) * silu_grad(z)
        dgl = dm * yg * sg * (1.0 - sg)
        return [dz, dm * sg, dgl], [jnp.sum(dgl, axis=0, keepdims=True)]
    (dz, dyg1, dgl), (db_glu,) = rowwise(f1, [rw(dpo), rw(sv['yg']), rw(sv['gl']), rw(proj, 1024, 1)], [p['a_b_glu']],
                                          [(1024, F32)] * 3, [(1, 1024)], 256, 'a_gate_bwd')
    g['a_b_glu'] = db_glu
    g['a_w_glu'] = mm(sv['yg'], dgl, 'tn', 'a_dwglu')
    dyg2 = mm(dgl, w['a_w_glu'], 'nt', 'a_dyg2')

    du, dd, dBre, dBim, dCre, dCim, da_re, da_im = s5_bwd(proj, dyg1, dyg2, sv['y'], p['a_d'], sv['s_re'], sv['s_im'],
                                                           sv['Bre'], sv['Bim'], sv['Cre'], sv['Cim'], sv['prr'],
                                                           sv['pir'])
    g['a_d'] = dd
    dCim = -dCim

    def f3(du_, dz_):
        return [jnp.concatenate([du_, dz_], axis=1)], []
    (dproj,), _ = rowwise(f3, [rw(du), rw(dz)], [], [(2048, F32)], [], 256, 'a_dproj')
    dlr, dli, dldt, dbr, dbi = sv['disc_vjp']((da_re.reshape(SSM_G, SSM_P), da_im.reshape(SSM_G, SSM_P),
                                               _b_untile(dBre), _b_untile(dBim)))
    g['a_lam_re'], g['a_lam_im'], g['a_log_dt'] = dlr[None], dli[None], dldt[None]
    g['a_b_re'], g['a_b_im'] = dbr[None], dbi[None]
    g['a_c_re'], g['a_c_im'] = _c_untile(dCre)[None], _c_untile(dCim)[None]
    g['a_w_in'] = mm(sv['h'], dproj, 'tn', 'a_dwin')
    dh = mm(dproj, w['a_w_in'], 'nt', 'a_dh')
    return dh, g


def _t5_bucket_np():
    qi = np.arange(WINDOW)[:, None]
    kj = np.arange(2 * WINDOW)[None, :]
    dist = np.maximum(qi + WINDOW - kj, 0)
    max_exact = REL_BUCKETS // 2
    dist_f = np.maximum(dist, 1).astype(np.float32)
    large = max_exact + (np.log(dist_f / np.float32(max_exact)) / np.float32(math.log(REL_MAX_DIST / max_exact))
                         * np.float32(REL_BUCKETS - max_exact)).astype(np.int32)
    large = np.minimum(large, REL_BUCKETS - 1)
    return np.where(dist < max_exact, dist, large).astype(np.int32)


SWA_GRP = SWA_HEADS // SWA_KV


def _swa_kv(kvp, kvc, kvh):
    kb = jnp.concatenate([kvp[:, kvh * 64:(kvh + 1) * 64], kvc[:, kvh * 64:(kvh + 1) * 64]], 0).astype(BF16)
    vb = jnp.concatenate([kvp[:, 128 + kvh * 64:128 + (kvh + 1) * 64], kvc[:, 128 + kvh * 64:128 + (kvh + 1) * 64]],
                         0).astype(BF16)
    return kb, vb


def _swa_stack(x, kvh):
    return jnp.concatenate([x[:, (kvh * SWA_GRP + g) * 64:(kvh * SWA_GRP + g + 1) * 64] for g in range(SWA_GRP)],
                           axis=0).astype(BF16)


def _swa_group(bias_ref, kvh):
    return bias_ref[kvh * SWA_GRP:(kvh + 1) * SWA_GRP].reshape(SWA_GRP * WINDOW, 2 * WINDOW)


def _swa_sinks(sink_ref, kvh):
    return jnp.concatenate([jnp.broadcast_to(sink_ref[0:1, kvh * SWA_GRP + g:kvh * SWA_GRP + g + 1], (WINDOW, 1))
                            for g in range(SWA_GRP)], axis=0)


def _swa_probs(q, kb, bias_h, sink, valid):
    s = lax.dot_general(q, kb, (((1,), (1,)), ((), ())), preferred_element_type=F32) * (HEAD_DIM ** -0.5)
    s = jnp.where(valid, s + bias_h, NEG_INF)
    m = jnp.maximum(jnp.max(s, axis=-1, keepdims=True), sink)
    e = jnp.exp(s - m)
    es = jnp.exp(sink - m)
    den = jnp.sum(e, axis=-1, keepdims=True) + es
    return e / den, es / den


def _swa_valid(n):
    qi = lax.broadcasted_iota(jnp.int32, (SWA_GRP * WINDOW, 2 * WINDOW), 0) & (WINDOW - 1)
    kj = lax.broadcasted_iota(jnp.int32, (SWA_GRP * WINDOW, 2 * WINDOW), 1)
    dist = qi + WINDOW - kj
    return (dist >= 0) & (dist < WINDOW) & ((kj >= WINDOW) | (n > 0))


def swa_fwd(proj, bias, sinks):
    L = proj.shape[0]

    def body(z_ref, q_ref, kvc_ref, kvp_ref, bias_ref, sink_ref, o_ref, po_ref):
        n = pl.program_id(0)
        valid = _swa_valid(n)
        q, kvc, kvp = q_ref[...], kvc_ref[...], kvp_ref[...]
        outs = []
        for kvh in range(SWA_KV):
            kb, vb = _swa_kv(kvp, kvc, kvh)
            p, _ = _swa_probs(_swa_stack(q, kvh), kb, _swa_group(bias_ref, kvh), _swa_sinks(sink_ref, kvh), valid)
            o8 = jnp.dot(p.astype(BF16), vb, preferred_element_type=F32)
            outs += [o8[g * WINDOW:(g + 1) * WINDOW] for g in range(SWA_GRP)]
        o = jnp.concatenate(outs, axis=1)
        o_ref[...] = o
        po_ref[...] = o * silu(z_ref[...])

    return pl.pallas_call(
        body, grid=(L // WINDOW,),
        in_specs=[pl.BlockSpec((WINDOW, 1024), lambda n: (n, 0)), pl.BlockSpec((WINDOW, 1024), lambda n: (n, 1)),
                  pl.BlockSpec((WINDOW, 256), lambda n: (n, 8)),
                  pl.BlockSpec((WINDOW, 256), lambda n: (jnp.maximum(n - 1, 0), 8)),
                  pl.BlockSpec((SWA_HEADS, WINDOW, 2 * WINDOW), lambda n: (0, 0, 0)),
                  pl.BlockSpec((1, SWA_HEADS), lambda n: (0, 0))],
        out_specs=[pl.BlockSpec((WINDOW, 1024), lambda n: (n, 0))] * 2,
        out_shape=[jax.ShapeDtypeStruct((L, 1024), F32)] * 2,
        compiler_params=pltpu.CompilerParams(dimension_semantics=("parallel",)), name='b_attn')(
            proj, proj, proj, proj, bias, sinks)


def swa_bwd(proj, do, bias, sinks):
    L = proj.shape[0]

    def body(q_ref, kvc_ref, kvp_ref, do_ref, bias_ref, sink_ref, dq_ref, dkv_ref, dbias_ref, dsink_ref):
        n = pl.program_id(0)

        @pl.when(n == 0)
        def _():
            dkv_ref[...] = jnp.zeros_like(dkv_ref)
            dbias_ref[...] = jnp.zeros_like(dbias_ref)
            dsink_ref[...] = jnp.zeros_like(dsink_ref)

        valid = _swa_valid(n)
        q, kvc, kvp, do_ = q_ref[...], kvc_ref[...], kvp_ref[...], do_ref[...]
        dqs, dks, dvs, dsk = [], [], [], []
        for kvh in range(SWA_KV):
            kb, vb = _swa_kv(kvp, kvc, kvh)
            q8, do8 = _swa_stack(q, kvh), _swa_stack(do_, kvh)
            p, ps = _swa_probs(q8, kb, _swa_group(bias_ref, kvh), _swa_sinks(sink_ref, kvh), valid)
            dp = lax.dot_general(do8, vb, (((1,), (1,)), ((), ())), preferred_element_type=F32)
            delta = jnp.sum(p * dp, axis=-1, keepdims=True)
            ds = p * (dp - delta)
            col = -ps * delta
            dsk += [jnp.sum(col[g * WINDOW:(g + 1) * WINDOW], axis=0, keepdims=True) for g in range(SWA_GRP)]
            dbias_ref[kvh * SWA_GRP:(kvh + 1) * SWA_GRP] += ds.reshape(SWA_GRP, WINDOW, 2 * WINDOW)
            dsb = (ds * (HEAD_DIM ** -0.5)).astype(BF16)
            dq8 = jnp.dot(dsb, kb, preferred_element_type=F32)
            dqs += [dq8[g * WINDOW:(g + 1) * WINDOW] for g in range(SWA_GRP)]
            dks.append(lax.dot_general(dsb, q8, (((0,), (0,)), ((), ())), preferred_element_type=F32))
            dvs.append(lax.dot_general(p.astype(BF16), do8, (((0,), (0,)), ((), ())), preferred_element_type=F32))
        dq_ref[...] = jnp.concatenate(dqs, axis=1)
        dsink_ref[...] += jnp.concatenate(dsk, axis=1)
        both = jnp.concatenate(dks + dvs, axis=1)
        r_cur = pl.multiple_of(n * WINDOW, WINDOW)
        r_prev = pl.multiple_of(jnp.maximum(n - 1, 0) * WINDOW, WINDOW)
        dkv_ref[pl.ds(r_prev, WINDOW), :] += both[:WINDOW]
        dkv_ref[pl.ds(r_cur, WINDOW), :] += both[WINDOW:]

    return pl.pallas_call(
        body, grid=(L // WINDOW,),
        in_specs=[pl.BlockSpec((WINDOW, 1024), lambda n: (n, 1)), pl.BlockSpec((WINDOW, 256), lambda n: (n, 8)),
                  pl.BlockSpec((WINDOW, 256), lambda n: (jnp.maximum(n - 1, 0), 8)),
                  pl.BlockSpec((WINDOW, 1024), lambda n: (n, 0)),
                  pl.BlockSpec((SWA_HEADS, WINDOW, 2 * WINDOW), lambda n: (0, 0, 0)),
                  pl.BlockSpec((1, SWA_HEADS), lambda n: (0, 0))],
        out_specs=[pl.BlockSpec((WINDOW, 1024), lambda n: (n, 0)), pl.BlockSpec((L, 256), lambda n: (0, 0)),
                   pl.BlockSpec((SWA_HEADS, WINDOW, 2 * WINDOW), lambda n: (0, 0, 0)),
                   pl.BlockSpec((1, SWA_HEADS), lambda n: (0, 0))],
        out_shape=[jax.ShapeDtypeStruct((L, 1024), F32), jax.ShapeDtypeStruct((L, 256), F32),
                   jax.ShapeDtypeStruct((SWA_HEADS, WINDOW, 2 * WINDOW), F32), jax.ShapeDtypeStruct((1, SWA_HEADS), F32)],
        compiler_params=pltpu.CompilerParams(dimension_semantics=("arbitrary",)), name='b_attn_bwd')(
            proj, proj, proj, do, bias, sinks)


def swa_bias(rel_bias):
    def body(bk_ref, rb_ref, o_ref):
        bk = bk_ref[...]
        for h in range(SWA_HEADS):
            acc = jnp.zeros((WINDOW, 2 * WINDOW), F32)
            for b in range(REL_BUCKETS):
                acc = jnp.where(bk == b, rb_ref[b, h], acc)
            o_ref[h] = acc

    return pl.pallas_call(
        body, out_shape=jax.ShapeDtypeStruct((SWA_HEADS, WINDOW, 2 * WINDOW), F32),
        in_specs=[pl.BlockSpec(memory_space=pltpu.VMEM), pl.BlockSpec(memory_space=pltpu.SMEM)],
        out_specs=pl.BlockSpec(memory_space=pltpu.VMEM), name='b_bias')(jnp.asarray(_t5_bucket_np()), rel_bias)


def layer_b_fwd(h, w, p):
    proj = mm(h, w['b_w_in'], 'nn', 'b_proj')
    bias = swa_bias(p['rel_bias'])
    o, po = swa_fwd(proj, bias, p['b_sinks'])
    yb = mm(po, w['b_w_out'], 'nn', 'b_out')
    return yb, dict(h=h, proj=proj, bias=bias, o=o, po=po)


def layer_b_bwd(dyb, w, p, sv):
    g = {}
    dpo = mm(dyb, w['b_w_out'], 'nt', 'b_dpo')
    g['b_w_out'] = mm(sv['po'], dyb, 'tn', 'b_dwout')
    proj = sv['proj']

    def f1(dpo_, o, z):
        return [dpo_ * silu(z), dpo_ * o * silu_grad(z)], []
    (do, dz), _ = rowwise(f1, [rw(dpo), rw(sv['o']), rw(proj, 1024, 0)], [], [(1024, F32)] * 2, [], 256, 'b_gate_bwd')
    dq, dkv, dbias, dsinks = swa_bwd(proj, do, sv['bias'], p['b_sinks'])
    g['b_sinks'] = dsinks
    onehot = jnp.asarray(np.eye(REL_BUCKETS, dtype=np.float32)[_t5_bucket_np().reshape(-1)])

    def f2(db, oh):
        return [], [lax.dot_general(db, oh, (((1,), (0,)), ((), ())), preferred_element_type=F32,
                                    precision=lax.Precision.HIGHEST)]
    _, (drel,) = rowwise(f2, [(dbias.reshape(SWA_HEADS, -1), pl.BlockSpec((SWA_HEADS, 4096), lambda i: (0, i))),
                              (onehot, pl.BlockSpec((4096, REL_BUCKETS), lambda i: (i, 0)))], [], [],
                         [(SWA_HEADS, REL_BUCKETS)], 4096, 'b_drel', n_steps=(2 * WINDOW * WINDOW) // 4096)
    g['rel_bias'] = drel.T

    def f3(dz_, dq_, dkv_):
        return [jnp.concatenate([dz_, dq_, dkv_], axis=1)], []
    (dproj,), _ = rowwise(f3, [rw(dz), rw(dq), rw(dkv)], [], [(2304, F32)], [], 256, 'b_dproj')
    g['b_w_in'] = mm(sv['h'], dproj, 'tn', 'b_dwin')
    dh = mm(dproj, w['b_w_in'], 'nt', 'b_dh')
    return dh, g


MLA_SCALE = (MLA_NOPE + MLA_ROPE) ** -0.5


def _rope_tables(L):
    inv = ROPE_BASE ** (-jnp.arange(0, MLA_ROPE, 2, dtype=F32) / MLA_ROPE)
    ang = jnp.arange(L, dtype=F32)[:, None] * inv[None, :]
    c, s = jnp.cos(ang), jnp.sin(ang)
    one, zero, pad = jnp.ones((L, 128), F32), jnp.zeros((L, 128), F32), jnp.zeros((L, 64), F32)
    return (jnp.concatenate([one, c, c, c, c, pad], 1), jnp.concatenate([zero, s, s, s, s, pad], 1))


def _rot(x, transpose=False):
    w = x.shape[1]
    lane = lax.broadcasted_iota(jnp.int32, x.shape, 1)
    up = pltpu.roll(x, w - 16, 1)
    dn = pltpu.roll(x, 16, 1)
    first = (lane % 32) < 16
    return jnp.where(first, up, -dn) if transpose else jnp.where(first, -up, dn)


MLA_QT = 512


def _mla_exp(qf, kf, t, qt):
    n_k = kf.shape[0]
    s = lax.dot_general(qf, kf, (((1,), (1,)), ((), ())), preferred_element_type=F32) * MLA_SCALE
    qpos = t * qt + lax.broadcasted_iota(jnp.int32, (qt, n_k), 0)
    kpos = lax.broadcasted_iota(jnp.int32, (qt, n_k), 1)
    s = jnp.where(kpos <= qpos, s, NEG_INF)
    e = jnp.exp(s - jnp.max(s, axis=-1, keepdims=True))
    return e, jnp.sum(e, axis=-1, keepdims=True)


def _mla_heads(q, kv, kr):
    out = []
    for j in range(2):
        qf = jnp.concatenate([q[:, j * 64:(j + 1) * 64], q[:, 128 + j * 32:128 + (j + 1) * 32]], axis=1)
        kf = jnp.concatenate([kv[:, j * 64:(j + 1) * 64], kr], axis=1)
        out.append((qf, kf, kv[:, 128 + j * 64:128 + (j + 1) * 64]))
    return out


def mla_fwd(q, kv, kr):
    L = q.shape[0]
    qt = min(MLA_QT, L)
    nq = L // qt

    def body(q_ref, kv_ref, kr_ref, o_ref):
        for t in range(nq):
            @pl.when(pl.program_id(1) == t)
            def _(t=t):
                n_k = (t + 1) * qt
                outs = []
                for qf, kf, v in _mla_heads(q_ref[...], kv_ref[0:n_k, :], kr_ref[0:n_k, 0:MLA_ROPE]):
                    e, den = _mla_exp(qf, kf, t, qt)
                    outs.append(jnp.dot(e.astype(BF16), v, preferred_element_type=F32) / den)
                o_ref[...] = jnp.concatenate(outs, axis=1)

    return pl.pallas_call(
        body, grid=(MLA_HEADS // 2, nq),
        in_specs=[pl.BlockSpec((qt, 256), lambda hp, n: (n, hp)), pl.BlockSpec((L, 256), lambda hp, n: (0, hp)),
                  pl.BlockSpec((L, 128), lambda hp, n: (0, 0))],
        out_specs=pl.BlockSpec((qt, 128), lambda hp, n: (n, hp)), out_shape=jax.ShapeDtypeStruct((L, 1024), F32),
        compiler_params=pltpu.CompilerParams(dimension_semantics=("parallel", "parallel")), name='c_attn')(q, kv, kr)


def mla_bwd(q, kv, kr, do):
    L = q.shape[0]
    qt = min(MLA_QT, L)
    nq = L // qt

    def body(q_ref, kv_ref, kr_ref, do_ref, dq_ref, dkv_ref, dkr_ref):
        @pl.when(pl.program_id(1) == 0)
        def _():
            dkv_ref[...] = jnp.zeros_like(dkv_ref)
            dkr_ref[...] = jnp.zeros_like(dkr_ref)

        for t in range(nq):
            @pl.when(pl.program_id(1) == t)
            def _(t=t):
                n_k = (t + 1) * qt
                do_ = do_ref[...]
                dqn, dqr, dkn, dvs = [], [], [], []
                dkr = jnp.zeros((n_k, MLA_ROPE), F32)
                for j, (qf, kf, v) in enumerate(_mla_heads(q_ref[...], kv_ref[0:n_k, :], kr_ref[0:n_k, 0:MLA_ROPE])):
                    doh = do_[:, j * 64:(j + 1) * 64]
                    e, den = _mla_exp(qf, kf, t, qt)
                    p = e * (1.0 / den)
                    dp = lax.dot_general(doh, v, (((1,), (1,)), ((), ())), preferred_element_type=F32)
                    ds = (p * (dp - jnp.sum(p * dp, axis=-1, keepdims=True)) * MLA_SCALE).astype(BF16)
                    dqf = jnp.dot(ds, kf, preferred_element_type=F32)
                    dkf = lax.dot_general(ds, qf, (((0,), (0,)), ((), ())), preferred_element_type=F32)
                    dvs.append(lax.dot_general(p.astype(BF16), doh, (((0,), (0,)), ((), ())), preferred_element_type=F32))
                    dqn.append(dqf[:, :MLA_NOPE])
                    dqr.append(dqf[:, MLA_NOPE:])
                    dkn.append(dkf[:, :MLA_NOPE])
                    dkr = dkr + dkf[:, MLA_NOPE:]
                dq_ref[...] = jnp.concatenate(dqn + dqr + [jnp.zeros((qt, 64), F32)], axis=1)
                dkv_ref[0:n_k, :] += jnp.concatenate(dkn + dvs, axis=1)
                dkr_ref[0, 0:n_k, :] += jnp.concatenate([dkr, jnp.zeros((n_k, 128 - MLA_ROPE), F32)], axis=1)

    return pl.pallas_call(
        body, grid=(MLA_HEADS // 2, nq),
        in_specs=[pl.BlockSpec((qt, 256), lambda hp, n: (n, hp)), pl.BlockSpec((L, 256), lambda hp, n: (0, hp)),
                  pl.BlockSpec((L, 128), lambda hp, n: (0, 0)), pl.BlockSpec((qt, 128), lambda hp, n: (n, hp))],
        out_specs=[pl.BlockSpec((qt, 256), lambda hp, n: (n, hp)), pl.BlockSpec((L, 256), lambda hp, n: (0, hp)),
                   pl.BlockSpec((1, L, 128), lambda hp, n: (hp, 0, 0))],
        out_shape=[jax.ShapeDtypeStruct((L, 2048), F32), jax.ShapeDtypeStruct((L, 2048), F32),
                   jax.ShapeDtypeStruct((MLA_HEADS // 2, L, 128), F32)],
        compiler_params=pltpu.CompilerParams(dimension_semantics=("parallel", "arbitrary")), name='c_attn_bwd')(
            q, kv, kr, do)


def _perm_c_w_in(wf):
    return jnp.concatenate([wf[:, 1056:], wf[:, :1056], jnp.zeros((wf.shape[0], 96), wf.dtype)], axis=1)


def _unperm_c_w_in(d):
    return jnp.concatenate([d[:, 1024:2080], d[:, :1024]], axis=1)


def _perm_w_uq(wf):
    t = wf.reshape(wf.shape[0], 8, 2, 96)
    nope = t[..., :64].reshape(-1, 8, 128)
    rope = t[..., 64:].reshape(-1, 8, 64)
    return jnp.concatenate([nope, rope, jnp.zeros_like(rope)], axis=2).reshape(-1, 2048)


def _unperm_w_uq(d):
    t = d.reshape(d.shape[0], 8, 256)
    nope = t[..., :128].reshape(-1, 8, 2, 64)
    rope = t[..., 128:192].reshape(-1, 8, 2, 32)
    return jnp.concatenate([nope, rope], axis=3).reshape(-1, 1536)


def _perm_w_ukv(wf):
    return wf.reshape(-1, 8, 2, 2, 64).transpose(0, 1, 3, 2, 4).reshape(-1, 2048)


def _unperm_w_ukv(d):
    return d.reshape(-1, 8, 2, 2, 64).transpose(0, 1, 3, 2, 4).reshape(-1, 2048)


def layer_c_fwd(h, w, p):
    L = h.shape[0]
    proj = mm(h, w['c_w_in'], 'nn', 'c_proj')

    def f1(c, gq, gk):
        return [rms_fwd(c[:, :768], gq), rms_fwd(c[:, 768:], gk)], []
    (cqn, ckvn), _ = rowwise(f1, [rw(proj, 1024, 1)], [p['c_q_norm'], p['c_kv_norm']], [(768, BF16), (256, BF16)], [],
                             256, 'c_norms')
    qf = mm(cqn, w['c_w_uq'], 'nn', 'c_uq')
    kvf = mm(ckvn, w['c_w_ukv'], 'nn', 'c_ukv', out_dtype=BF16)
    cos, sin = _rope_tables(L)

    def f2(q_, kr_, c, s):
        c8, s8 = jnp.tile(c, (1, 8)), jnp.tile(s, (1, 8))
        return [q_ * c8 + _rot(q_) * s8, kr_ * c[:, 128:] + _rot(kr_) * s[:, 128:]], []
    (q, kr), _ = rowwise(f2, [rw(qf), rw(proj, 128, 16), rw(cos), rw(sin)], [], [(2048, BF16), (128, BF16)], [], 256,
                         'c_rope')
    o = mla_fwd(q, kvf, kr)

    def f3(o_, z):
        return [o_ * silu(z)], []
    (po,), _ = rowwise(f3, [rw(o), rw(proj, 1024, 0)], [], [(1024, F32)], [], 256, 'c_gate')
    yb = mm(po, w['c_w_out'], 'nn', 'c_out')
    return yb, dict(h=h, proj=proj, cqn=cqn, ckvn=ckvn, q=q, kv=kvf, kr=kr, o=o, po=po, cos=cos, sin=sin)


def layer_c_bwd(dyb, w, p, sv):
    g = {}
    dpo = mm(dyb, w['c_w_out'], 'nt', 'c_dpo')
    g['c_w_out'] = mm(sv['po'], dyb, 'tn', 'c_dwout')
    proj = sv['proj']
    L = proj.shape[0]

    def f1(dpo_, o, z):
        return [dpo_ * silu(z), dpo_ * o * silu_grad(z)], []
    (do, dz), _ = rowwise(f1, [rw(dpo), rw(sv['o']), rw(proj, 1024, 0)], [], [(1024, BF16), (1024, F32)], [], 256,
                          'c_gate_bwd')
    dq, dkvf, dkr8 = mla_bwd(sv['q'], sv['kv'], sv['kr'], do)

    def f2(dq_, dkr_, c, s):
        c8, s8 = jnp.tile(c, (1, 8)), jnp.tile(s, (1, 8))
        dk = jnp.sum(dkr_, axis=0)
        return [dq_ * c8 + _rot(dq_ * s8, True), dk * c[:, 128:] + _rot(dk * s[:, 128:], True)], []
    tl = 256
    (dqf, dkr), _ = rowwise(f2, [rw(dq), (dkr8, pl.BlockSpec((8, tl, 128), lambda i: (0, i, 0))), rw(sv['cos']),
                                 rw(sv['sin'])], [], [(2048, F32), (128, F32)], [], tl, 'c_rope_bwd')
    g['c_w_uq'] = mm(sv['cqn'], dqf, 'tn', 'c_dwuq')
    g['c_w_ukv'] = mm(sv['ckvn'], dkvf, 'tn', 'c_dwukv')
    dcqn = mm(dqf, w['c_w_uq'], 'nt', 'c_dcqn')
    dckvn = mm(dkvf, w['c_w_ukv'], 'nt', 'c_dckvn')

    def f3(c, dq_, dk_, dz_, dkr_, gq, gk):
        dcq, dgq = rms_bwd(c[:, :768], gq, dq_)
        dckv, dgk = rms_bwd(c[:, 768:], gk, dk_)
        return [jnp.concatenate([dz_, dcq, dckv, dkr_], axis=1)], [dgq, dgk]
    (dproj,), (dgq, dgk) = rowwise(f3, [rw(proj, 1024, 1), rw(dcqn), rw(dckvn), rw(dz), rw(dkr)],
                                   [p['c_q_norm'], p['c_kv_norm']], [(2176, F32)], [(1, 768), (1, 256)], 256, 'c_dproj')
    g['c_q_norm'], g['c_kv_norm'] = dgq, dgk
    g['c_w_in'] = mm(sv['h'], dproj, 'tn', 'c_dwin')
    dh = mm(dproj, w['c_w_in'], 'nt', 'c_dh')
    return dh, g


def _sgu_mix(wm, v, transpose):
    outs = []
    dims = (((0,), (0,)), ((), ())) if transpose else (((1,), (0,)), ((), ()))
    for gi in range(SGU_G):
        outs.append(lax.dot_general(wm[gi], v[:, gi * SGU_C:(gi + 1) * SGU_C].astype(BF16), dims,
                                    preferred_element_type=F32))
    return jnp.concatenate(outs, axis=1)


def _sgu_wmask(ws):
    t = lax.broadcasted_iota(jnp.int32, (SGU_T, SGU_T), 0)
    s = lax.broadcasted_iota(jnp.int32, (SGU_T, SGU_T), 1)
    return jnp.where((s <= t)[None], ws, 0.0).astype(BF16)


def _ln_stats(v):
    mu = jnp.mean(v, axis=-1, keepdims=True)
    vc = v - mu
    rstd = lax.rsqrt(jnp.mean(vc * vc, axis=-1, keepdims=True) + EPS)
    return vc * rstd, rstd


def layer_d_fwd(h, w, p):
    proj = mm(h, w['d_w_in'], 'nn', 'd_proj')
    bias = jnp.repeat(p['d_b_s'][0].T, SGU_C, axis=1)

    def f1(u_, v_, z, ws, lg, lb, bs):
        xh, _ = _ln_stats(gelu(v_))
        s = _sgu_mix(_sgu_wmask(ws), xh * lg + lb, False) + bs
        return [gelu(u_) * s * silu(z)], []
    (po,), _ = rowwise(f1, [rw(proj, 1024, 0), rw(proj, 1024, 1), rw(proj, 1024, 2)],
                       [p['d_w_s'][0], p['d_ln_g'], p['d_ln_b'], bias], [(1024, F32)], [], SGU_T, 'd_mix')
    yb = mm(po, w['d_w_out'], 'nn', 'd_out')
    return yb, dict(h=h, proj=proj, po=po, bias=bias)


def layer_d_bwd(dyb, w, p, sv):
    g = {}
    dpo = mm(dyb, w['d_w_out'], 'nt', 'd_dpo')
    g['d_w_out'] = mm(sv['po'], dyb, 'tn', 'd_dwout')
    proj = sv['proj']

    def f1(dpo_, u_, v_, z, ws, lg, lb, bs):
        wm = _sgu_wmask(ws)
        gv = gelu(v_)
        xh, rstd = _ln_stats(gv)
        vn = xh * lg + lb
        s = _sgu_mix(wm, vn, False) + bs
        gu, sz = gelu(u_), silu(z)
        du = dpo_ * s * sz
        ds = dpo_ * gu * sz
        dz = dpo_ * gu * s * silu_grad(z)
        dsb = ds.astype(BF16)
        dws = jnp.stack([lax.dot_general(dsb[:, gi * SGU_C:(gi + 1) * SGU_C], vn[:, gi * SGU_C:(gi + 1) * SGU_C].astype(BF16),
                                         (((1,), (1,)), ((), ())), preferred_element_type=F32) for gi in range(SGU_G)])
        dvn = _sgu_mix(wm, ds, True)
        dlg = jnp.sum(dvn * xh, axis=0, keepdims=True)
        dlb = jnp.sum(dvn, axis=0, keepdims=True)
        dxh = dvn * lg
        dgv = rstd * (dxh - jnp.mean(dxh, axis=-1, keepdims=True) - xh * jnp.mean(dxh * xh, axis=-1, keepdims=True))
        return ([jnp.concatenate([du * gelu_grad(u_), dgv * gelu_grad(v_), dz], axis=1)], [dws, ds, dlg, dlb])
    (dproj,), (dws, dbs, dlg, dlb) = rowwise(
        f1, [rw(dpo), rw(proj, 1024, 0), rw(proj, 1024, 1), rw(proj, 1024, 2)],
        [p['d_w_s'][0], p['d_ln_g'], p['d_ln_b'], sv['bias']], [(3072, F32)],
        [(SGU_G, SGU_T, SGU_T), (SGU_T, 1024), (1, 1024), (1, 1024)], SGU_T, 'd_mix_bwd')
    tril = np.tril(np.ones((SGU_T, SGU_T), dtype=bool))
    g['d_w_s'] = jnp.where(tril[None], dws, 0.0)[None]
    g['d_b_s'] = dbs.reshape(SGU_T, SGU_G, SGU_C).sum(-1).T[None]
    g['d_ln_g'], g['d_ln_b'] = dlg, dlb
    g['d_w_in'] = mm(sv['h'], dproj, 'tn', 'd_dwin')
    dh = mm(dproj, w['d_w_in'], 'nt', 'd_dh')
    return dh, g


def _coords():
    return lax.axis_index("x"), lax.axis_index("y"), lax.axis_index("c")


def all_gather(x, name):
    def body(x_ref, out_ref, send_sems, recv_sems, local_sem):
        x_, y_, c_ = _coords()
        me, sibling = (x_, y_, c_), (x_, y_, 1 - c_)
        chips = [(1 - x_, y_), (x_, 1 - y_), (1 - x_, 1 - y_)]

        def slot(px, py, pc):
            return out_ref.at[4 * px + 2 * py + pc]

        def copy(k, block, to, src=None):
            return pltpu.make_async_remote_copy(src_ref=slot(*block) if src is None else src, dst_ref=slot(*block),
                                                send_sem=send_sems.at[k], recv_sem=recv_sems.at[k], device_id=to,
                                                device_id_type=MESH)

        mine = pltpu.make_async_copy(x_ref, slot(*me), local_sem)
        mine.start()
        first = [copy(0, me, sibling, src=x_ref)]
        first += [copy(1 + j, me, (*chip, c_), src=x_ref) for j, chip in enumerate(chips)]
        for cp in first:
            cp.start()
        passed = [copy(4 + j, (*chip, c_), sibling) for j, chip in enumerate(chips)]
        for j, chip in enumerate(chips):
            copy(1 + j, (*chip, c_), me).wait_recv()
            passed[j].start()
        copy(0, sibling, me).wait_recv()
        for j, chip in enumerate(chips):
            copy(4 + j, (*chip, 1 - c_), me).wait_recv()
        for cp in first + passed:
            cp.wait_send()
        mine.wait()

    return pl.pallas_call(
        body, out_shape=jax.ShapeDtypeStruct((N_DEV,) + x.shape, x.dtype), in_specs=[ANY], out_specs=ANY,
        scratch_shapes=[pltpu.SemaphoreType.DMA((7,)), pltpu.SemaphoreType.DMA((7,)), pltpu.SemaphoreType.DMA(())],
        name=name)(x)


def rs_sibling(gfull):
    _, R, C = gfull.shape

    def body(g_ref, land_ref, send_sems, recv_sems):
        x_, y_, c_ = _coords()
        copies = []
        for k in range(4):
            cp = pltpu.make_async_remote_copy(src_ref=g_ref.at[2 * k + 1 - c_], dst_ref=land_ref.at[k],
                                              send_sem=send_sems.at[k], recv_sem=recv_sems.at[k],
                                              device_id=(x_, y_, 1 - c_), device_id_type=MESH)
            cp.start()
            copies.append(cp)
        for cp in copies:
            cp.wait_recv()
        for cp in copies:
            cp.wait_send()

    return pl.pallas_call(
        body, out_shape=jax.ShapeDtypeStruct((4, R, C), gfull.dtype), in_specs=[ANY], out_specs=ANY,
        scratch_shapes=[pltpu.SemaphoreType.DMA((4,)), pltpu.SemaphoreType.DMA((4,))], name='rs_sibling')(gfull)


def rs_pair_add(gfull, land, core):
    _, R, C = gfull.shape
    tl = R // 4
    assert R % 64 == 0

    def body(c_ref, g_ref, l_ref, o_ref):
        o_ref[...] = (g_ref[...] + l_ref[...]).astype(BF16)

    return pl.pallas_call(
        body, out_shape=jax.ShapeDtypeStruct((4, R, C), BF16),
        grid_spec=pltpu.PrefetchScalarGridSpec(
            num_scalar_prefetch=1, grid=(4, R // tl),
            in_specs=[pl.BlockSpec((1, tl, C), lambda k, i, c: (2 * k + c[0], i, 0)),
                      pl.BlockSpec((1, tl, C), lambda k, i, c: (k, i, 0))],
            out_specs=pl.BlockSpec((1, tl, C), lambda k, i, c: (k, i, 0))),
        compiler_params=pltpu.CompilerParams(dimension_semantics=("parallel", "parallel")), name='rs_pair_add')(
            core, gfull, land)


def rs_chips(part):
    _, R, C = part.shape

    def body(p_ref, land_ref, send_sems, recv_sems):
        x_, y_, c_ = _coords()
        copies = []
        for r, (fx, fy) in enumerate([(1, 0), (0, 1), (1, 1)]):
            tx = jnp.where(fx == 1, 1 - x_, x_)
            ty = jnp.where(fy == 1, 1 - y_, y_)
            cp = pltpu.make_async_remote_copy(src_ref=p_ref.at[2 * tx + ty], dst_ref=land_ref.at[r],
                                              send_sem=send_sems.at[r], recv_sem=recv_sems.at[r],
                                              device_id=(tx, ty, c_), device_id_type=MESH)
            cp.start()
            copies.append(cp)
        for cp in copies:
            cp.wait_recv()
        for cp in copies:
            cp.wait_send()

    return pl.pallas_call(
        body, out_shape=jax.ShapeDtypeStruct((3, R, C), part.dtype), in_specs=[ANY], out_specs=ANY,
        scratch_shapes=[pltpu.SemaphoreType.DMA((3,)), pltpu.SemaphoreType.DMA((3,))], name='rs_chips')(part)


def _adam(wv, gv, mv, vv):
    m = ADAM_B1 * mv + (1.0 - ADAM_B1) * gv
    v = ADAM_B2 * vv + (1.0 - ADAM_B2) * (gv * gv)
    m_hat = m / (1.0 - ADAM_B1 ** ADAM_STEP)
    v_hat = v / (1.0 - ADAM_B2 ** ADAM_STEP)
    delta = -ADAM_LR * (m_hat / (jnp.sqrt(v_hat) + ADAM_EPS) + ADAM_WD * wv)
    return delta, m, v


def _sum4(p_ref, l_ref):
    return ((p_ref[0].astype(F32) + l_ref[0].astype(F32)) + l_ref[1].astype(F32)) + l_ref[2].astype(F32)


def rs_rep_sum(part, land, chip):
    def body(c_ref, p_ref, l_ref, o_ref):
        o_ref[...] = _sum4(p_ref, l_ref)

    return pl.pallas_call(
        body, out_shape=jax.ShapeDtypeStruct((REP_SLOT, LANES), F32),
        grid_spec=pltpu.PrefetchScalarGridSpec(
            num_scalar_prefetch=1, grid=(REP_SLOT // RS_TL,),
            in_specs=[pl.BlockSpec((1, RS_TL, LANES), lambda i, c: (c[0], i, 0)),
                      pl.BlockSpec((3, RS_TL, LANES), lambda i, c: (0, i, 0))],
            out_specs=pl.BlockSpec((RS_TL, LANES), lambda i, c: (i, 0))),
        compiler_params=pltpu.CompilerParams(dimension_semantics=("parallel",)), name='rs_rep')(chip, part, land)


def adam_param(name, shape, off, w, m, v, chip, part=None, land=None, grep=None):
    r, c = shape
    rp, nt, rb = _tiles(shape)
    rbw = min(r, rb)
    n_src = 2 if grep is None else 1

    def body(c_ref, *refs):
        srcs = refs[:n_src * nt]
        w_ref, m_ref, v_ref, g_ref, d_ref, nm_ref, nv_ref = refs[n_src * nt:]
        if grep is None:
            tiles = [_sum4(srcs[2 * t], srcs[2 * t + 1]) for t in range(nt)]
        else:
            tiles = [srcs[t][...] for t in range(nt)]
        g = (tiles[0] if nt == 1 else jnp.concatenate(tiles, axis=1))[:rbw, :c]
        g_ref[...] = g
        d_ref[...], nm_ref[...], nv_ref[...] = _adam(w_ref[...], g, m_ref[...], v_ref[...])

    in_specs, args = [], []
    for t in range(nt):
        b0 = (off + t * rp) // rb
        assert (off + t * rp) % rb == 0
        if grep is None:
            in_specs += [pl.BlockSpec((1, rb, LANES), functools.partial(lambda i, cr, b0: (cr[0], b0 + i, 0), b0=b0)),
                         pl.BlockSpec((3, rb, LANES), functools.partial(lambda i, cr, b0: (0, b0 + i, 0), b0=b0))]
            args += [part, land]
        else:
            in_specs.append(pl.BlockSpec((rb, LANES), functools.partial(lambda i, cr, b0: (b0 + i, 0), b0=b0)))
            args.append(grep)
    nat = pl.BlockSpec((rbw, c), lambda i, cr: (i, 0))
    return pl.pallas_call(
        body, out_shape=[jax.ShapeDtypeStruct((r, c), F32)] * 4,
        grid_spec=pltpu.PrefetchScalarGridSpec(num_scalar_prefetch=1, grid=(rp // rb,), in_specs=in_specs + [nat] * 3,
                                               out_specs=[nat] * 4),
        compiler_params=pltpu.CompilerParams(dimension_semantics=("parallel",)), name='adam_' + name)(
            chip, *args, w, m, v)


def _to_tiles(a, shape):
    r, c = shape
    rp, nt, _ = _tiles(shape)
    lead = [(0, 0)] * (a.ndim - 2)
    a = jnp.pad(a, lead + [(0, rp - r), (0, nt * LANES - c)])
    return a if nt == 1 else jnp.concatenate([a[..., t * LANES:(t + 1) * LANES] for t in range(nt)], axis=-2)


def _from_tiles(g, off, shape):
    r, c = shape
    rp, nt, _ = _tiles(shape)
    tiles = [g[..., off + t * rp:off + t * rp + r, :] for t in range(nt)]
    return (tiles[0] if nt == 1 else jnp.concatenate(tiles, axis=-1))[..., :c]


def _pack_small(blocks, order, rows, width, dtype):
    flat = jnp.concatenate([blocks[n].reshape(-1).astype(dtype) for n in order])
    return jnp.pad(flat, (0, rows * width - flat.shape[0])).reshape(rows, width)


def _device_blocks(name, gfull):
    (r, c), ax = SHARDED[name]
    br, bc = _block_shape(name)
    if ax == 0:
        return gfull.reshape(N_DEV, br, bc)
    return gfull.reshape(r, N_DEV, bc).transpose(1, 0, 2)


def _assemble(gathered, name):
    (r, c), ax = SHARDED[name]
    blk = _from_tiles(gathered, SH_OFF[name], _block_shape(name))
    return blk.reshape(r, c) if ax == 0 else blk.transpose(1, 0, 2).reshape(r, c)


def kernel(x, pre_norm, post_norm, rel_bias, a_w_in, a_lam_re, a_lam_im, a_log_dt, a_b_re, a_b_im, a_c_re, a_c_im, a_d, a_w_glu, a_b_glu, a_w_out, b_w_in, b_sinks, b_w_out, c_w_in, c_q_norm, c_kv_norm, c_w_uq, c_w_ukv, c_w_out, d_w_in, d_ln_g, d_ln_b, d_w_s, d_b_s, d_w_out, loss_target, m_pre_norm, m_post_norm, m_rel_bias, m_a_w_in, m_a_lam_re, m_a_lam_im, m_a_log_dt, m_a_b_re, m_a_b_im, m_a_c_re, m_a_c_im, m_a_d, m_a_w_glu, m_a_b_glu, m_a_w_out, m_b_w_in, m_b_sinks, m_b_w_out, m_c_w_in, m_c_q_norm, m_c_kv_norm, m_c_w_uq, m_c_w_ukv, m_c_w_out, m_d_w_in, m_d_ln_g, m_d_ln_b, m_d_w_s, m_d_b_s, m_d_w_out, v_pre_norm, v_post_norm, v_rel_bias, v_a_w_in, v_a_lam_re, v_a_lam_im, v_a_log_dt, v_a_b_re, v_a_b_im, v_a_c_re, v_a_c_im, v_a_d, v_a_w_glu, v_a_b_glu, v_a_w_out, v_b_w_in, v_b_sinks, v_b_w_out, v_c_w_in, v_c_q_norm, v_c_kv_norm, v_c_w_uq, v_c_w_ukv, v_c_w_out, v_d_w_in, v_d_ln_g, v_d_ln_b, v_d_w_s, v_d_b_s, v_d_w_out):
    loc = locals()
    P = {n: loc[n] for n in WEIGHTS}
    M = {n: loc['m_' + n] for n in WEIGHTS}
    V = {n: loc['v_' + n] for n in WEIGHTS}
    xs = x[0]
    L = xs.shape[0]

    blocks = {n: P[n].reshape(_block_shape(n)) for n in SHARDED}
    packed = jnp.concatenate([_to_tiles(blocks[n].astype(BF16), _block_shape(n)) for n in SH_ORDER], axis=0)
    gathered = all_gather(packed, 'ag_weights')
    small = all_gather(_pack_small(blocks, SHARDED_F32, SMALL_ROWS, 128, F32), 'ag_small')
    W = {n: _assemble(gathered, n) for n in SHARDED if n not in SHARDED_F32}
    Pl = dict(P)
    for n in SHARDED_F32:
        c = SHARDED[n][0][1]
        bc = c // N_DEV
        Pl[n] = small.reshape(N_DEV, -1)[:, SMALL_OFF[n]:SMALL_OFF[n] + bc].reshape(1, c)
    W['b_w_in'] = jnp.concatenate([W['b_w_in'][:, 1280:], W['b_w_in'][:, :1280]], axis=1)
    W['c_w_in'] = _perm_c_w_in(W['c_w_in'])
    W['c_w_uq'] = _perm_w_uq(W['c_w_uq'])
    W['c_w_ukv'] = _perm_w_ukv(W['c_w_ukv'])

    fwd = [layer_a_fwd, layer_b_fwd, layer_c_fwd, layer_d_fwd]
    bwd = [layer_a_bwd, layer_b_bwd, layer_c_bwd, layer_d_bwd]
    saved = []
    xc = xs
    for i in range(4):
        def fpre(x_, g_):
            return [rms_fwd(x_, g_)], []
        (h,), _ = rowwise(fpre, [rw(xc)], [P['pre_norm'][i:i + 1]], [(D_MODEL, F32)], [], 256, f'pre_norm{i}')
        yb, sv = fwd[i](h, W, Pl)

        def fpost(x_, y_, g_):
            return [x_ + rms_fwd(y_, g_)], []
        (xn,), _ = rowwise(fpost, [rw(xc), rw(yb)], [P['post_norm'][i:i + 1]], [(D_MODEL, F32)], [], 256, f'post_norm{i}')
        sv['x'], sv['yb'] = xc, yb
        saved.append(sv)
        xc = xn

    def floss(y_, t_):
        d = y_ - t_
        return [d * (1.0 / D_MODEL)], [0.5 * jnp.sum(jnp.sum(d * d, axis=-1, keepdims=True) * (1.0 / D_MODEL), axis=0,
                                                      keepdims=True)]
    (dx,), (loss_loc,) = rowwise(floss, [rw(xc), rw(loss_target[0])], [], [(D_MODEL, F32)], [(1, 1)], 256, 'loss')
    loss = lax.psum(loss_loc[0, 0], ("x", "y", "c"))

    G = {}
    dpre, dpost = [None] * 4, [None] * 4
    for i in reversed(range(4)):
        sv = saved[i]

        def fpost_b(y_, d_, g_):
            dy, dg = rms_bwd(y_, g_, d_)
            return [dy], [dg]
        (dyb,), (dpost[i],) = rowwise(fpost_b, [rw(sv['yb']), rw(dx)], [P['post_norm'][i:i + 1]], [(D_MODEL, F32)],
                                      [(1, D_MODEL)], 256, f'post_norm_bwd{i}')
        dh, g = bwd[i](dyb, W, Pl, sv)
        G.update(g)

        def fpre_b(x_, dh_, d_, g_):
            dxl, dg = rms_bwd(x_, g_, dh_)
            return [d_ + dxl], [dg]
        (dx,), (dpre[i],) = rowwise(fpre_b, [rw(sv['x']), rw(dh), rw(dx)], [P['pre_norm'][i:i + 1]], [(D_MODEL, F32)],
                                    [(1, D_MODEL)], 256, f'pre_norm_bwd{i}')
    G['pre_norm'] = jnp.concatenate(dpre, axis=0)
    G['post_norm'] = jnp.concatenate(dpost, axis=0)
    G['b_w_in'] = jnp.concatenate([G['b_w_in'][:, 1024:], G['b_w_in'][:, :1024]], axis=1)
    G['c_w_in'] = _unperm_c_w_in(G['c_w_in'])
    G['c_w_uq'] = _unperm_w_uq(G['c_w_uq'])
    G['c_w_ukv'] = _unperm_w_ukv(G['c_w_ukv'])

    rep = jnp.concatenate([_to_tiles(G[n].reshape(s), s) for n, s in REP_SHAPE.items()], axis=0)
    rep = jnp.pad(rep.reshape(N_DEV, REP_CHUNK, LANES), ((0, 0), (0, REP_SLOT - REP_CHUNK), (0, 0)))
    sh = [_to_tiles(_device_blocks(n, G[n]), _block_shape(n)) for n in SH_ORDER]
    tail = jnp.zeros((N_DEV, RS_ROWS - REP_SLOT - SH_ROWS, LANES), F32)
    gfull = jnp.concatenate([rep] + sh + [tail], axis=1)
    cx, cy, cc = _coords()
    core = jnp.reshape(cc, (1,)).astype(jnp.int32)
    chip = jnp.reshape(2 * cx + cy, (1,)).astype(jnp.int32)
    land = rs_sibling(gfull)
    part = rs_pair_add(gfull, land, core)
    land2 = rs_chips(part)
    out = {}
    for n in SH_ORDER:
        s = _block_shape(n)
        out[n] = adam_param(n, s, REP_SLOT + SH_OFF[n], blocks[n], M[n].reshape(s), V[n].reshape(s), chip,
                            part=part, land=land2)

    grep = all_gather(rs_rep_sum(part, land2, chip), 'ag_rep')[:, :REP_CHUNK].reshape(REP_ROWS, LANES)
    for n, s in REP_SHAPE.items():
        out[n] = adam_param(n, s, REP_OFF[n], P[n].reshape(s), M[n].reshape(s), V[n].reshape(s), chip, grep=grep)
    res = [loss, dx[None]]
    for kind in range(4):
        res += [out[n][kind].reshape(P[n].shape) for n in WEIGHTS]
    return tuple(res)
```

```python
import functools
import math

import numpy as np
import jax
import jax.numpy as jnp
from jax import lax
from jax.experimental import pallas as pl
from jax.experimental.pallas import tpu as pltpu

F32 = jnp.float32
BF16 = jnp.bfloat16
MESH = pl.DeviceIdType.MESH
ANY = pl.BlockSpec(memory_space=pl.ANY)

N_DEV = 8
D_MODEL = 1024
EPS = 1e-6
NEG_INF = -1e30
SSM_G, SSM_P, SSM_H = 64, 64, 16
SSM_T = 256
SSM_WC = 512
HEAD_DIM = 64
SWA_HEADS, SWA_KV = 16, 2
WINDOW = 128
REL_BUCKETS, REL_MAX_DIST = 32, 128
MLA_HEADS, MLA_NOPE, MLA_ROPE, MLA_V = 16, 64, 32, 64
MLA_Q_RANK, MLA_KV_RANK = 768, 256
ROPE_BASE = 10000.0
SGU_G, SGU_C, SGU_T = 16, 64, 128
ADAM_LR, ADAM_B1, ADAM_B2, ADAM_EPS, ADAM_WD, ADAM_STEP = 0.001, 0.9, 0.999, 1e-08, 0.01, 10

WEIGHTS = ['pre_norm', 'post_norm', 'rel_bias', 'a_w_in', 'a_lam_re', 'a_lam_im', 'a_log_dt', 'a_b_re', 'a_b_im',
           'a_c_re', 'a_c_im', 'a_d', 'a_w_glu', 'a_b_glu', 'a_w_out', 'b_w_in', 'b_sinks', 'b_w_out', 'c_w_in',
           'c_q_norm', 'c_kv_norm', 'c_w_uq', 'c_w_ukv', 'c_w_out', 'd_w_in', 'd_ln_g', 'd_ln_b', 'd_w_s', 'd_b_s',
           'd_w_out']
SHARDED = {'a_w_in': ((1024, 2048), 1), 'a_w_glu': ((1024, 1024), 0), 'a_w_out': ((1024, 1024), 0),
           'b_w_in': ((1024, 2304), 1), 'b_w_out': ((1024, 1024), 0), 'c_w_in': ((1024, 2080), 1),
           'c_q_norm': ((1, 768), 1), 'c_kv_norm': ((1, 256), 1), 'c_w_uq': ((768, 1536), 1),
           'c_w_ukv': ((256, 2048), 1), 'c_w_out': ((1024, 1024), 0), 'd_w_in': ((1024, 3072), 1),
           'd_ln_g': ((1, 1024), 1), 'd_ln_b': ((1, 1024), 1), 'd_w_out': ((1024, 1024), 0)}
SHARDED_F32 = ['c_q_norm', 'c_kv_norm', 'd_ln_g', 'd_ln_b']
REPLICATED = [n for n in WEIGHTS if n not in SHARDED]


def _cdiv(a, b):
    return -(-a // b)


def _block_shape(name):
    (r, c), ax = SHARDED[name]
    return (r // N_DEV, c) if ax == 0 else (r, c // N_DEV)


LANES = 128
LAYER_PARAMS = {'a': ['a_w_in', 'a_w_glu', 'a_w_out'], 'b': ['b_w_in', 'b_w_out'],
                'c': ['c_w_in', 'c_w_uq', 'c_w_ukv', 'c_w_out', 'c_q_norm', 'c_kv_norm'],
                'd': ['d_w_in', 'd_w_out', 'd_ln_g', 'd_ln_b']}


def _tiles(shape):
    r, c = shape
    rp = max(r, 8)
    rb = 512 if rp % 512 == 0 else 256 if rp % 256 == 0 else rp
    return rp, _cdiv(c, LANES), rb


SH_OFF, LAYER_ROWS = {}, {}
for _l, _names in LAYER_PARAMS.items():
    _o = 0
    for _n in _names:
        _rp, _nt, _rb = _tiles(_block_shape(_n))
        assert _o % _rb == 0
        SH_OFF[_n] = _o
        _o += _rp * _nt
    assert _o % 16 == 0
    LAYER_ROWS[_l] = _o

REP_SHAPE = {'a_b_re': (4096, 16), 'a_b_im': (4096, 16), 'd_w_s': (2048, 128), 'a_c_re': (1024, 64),
             'a_c_im': (1024, 64), 'pre_norm': (4, 1024), 'post_norm': (4, 1024), 'a_lam_re': (64, 64),
             'a_lam_im': (64, 64), 'a_d': (1, 1024), 'a_b_glu': (1, 1024), 'rel_bias': (32, 16), 'd_b_s': (16, 128),
             'a_log_dt': (1, 64), 'b_sinks': (1, 16)}
REP_OFF = {}
_o = 0
for _n, _s in REP_SHAPE.items():
    _rp, _nt, _rb = _tiles(_s)
    assert _o % _rb == 0
    REP_OFF[_n] = _o
    _o += _rp * _nt
REP_ROWS = _o
REP_CHUNK = REP_ROWS // N_DEV
assert REP_ROWS % (8 * N_DEV) == 0
REP_SLOT = _cdiv(REP_CHUNK, 16) * 16

PERM = {'a_w_in': [(0, 2048)], 'd_w_in': [(0, 3072)], 'b_w_in': [(1280, 1024), (0, 1280)],
        'c_w_in': [(1056, 1024), (0, 1056), ('z', 96)],
        'c_w_uq': sum([[(2 * hp * 96, 64), ((2 * hp + 1) * 96, 64), (2 * hp * 96 + 64, 32), ((2 * hp + 1) * 96 + 64, 32),
                        ('z', 64)] for hp in range(8)], []),
        'c_w_ukv': sum([[(2 * hp * 128, 64), ((2 * hp + 1) * 128, 64), (2 * hp * 128 + 64, 64),
                         ((2 * hp + 1) * 128 + 64, 64)] for hp in range(8)], [])}


def perm_index(name):
    return np.concatenate([np.full(p[1], -1) if p[0] == 'z' else np.arange(p[0], p[0] + p[1]) for p in PERM[name]])


SMALL_OFF = {}
_o = 0
for _n in SHARDED_F32:
    SMALL_OFF[_n] = _o
    _o += int(np.prod(_block_shape(_n)))
SMALL_ROWS = _cdiv(_o, 128 * 8) * 8


def _pick(n, cands):
    for c in cands:
        if n % c == 0:
            return c
    return n


def mm(a, b, mode, name, out_dtype=F32):
    if mode == 'nn':
        (M, K), (K2, N) = a.shape, b.shape
    elif mode == 'nt':
        (M, K), (N, K2) = a.shape, b.shape
    else:
        (K, M), (K2, N) = a.shape, b.shape
    assert K == K2, (name, a.shape, b.shape)
    tm = _pick(M, (512, 256, 128))
    tn = _pick(N, (512, 384, 256))
    dims = {'nn': ((1,), (0,)), 'nt': ((1,), (1,)), 'tn': ((0,), (0,))}[mode]

    def body(a_ref, b_ref, o_ref):
        o_ref[...] = lax.dot_general(a_ref[...].astype(BF16), b_ref[...].astype(BF16), (dims, ((), ())),
                                     preferred_element_type=F32).astype(out_dtype)

    a_spec = pl.BlockSpec((K, tm), lambda i, j: (0, i)) if mode == 'tn' else pl.BlockSpec((tm, K), lambda i, j: (i, 0))
    b_spec = pl.BlockSpec((tn, K), lambda i, j: (j, 0)) if mode == 'nt' else pl.BlockSpec((K, tn), lambda i, j: (0, j))
    return pl.pallas_call(
        body, grid=(M // tm, N // tn), in_specs=[a_spec, b_spec],
        out_specs=pl.BlockSpec((tm, tn), lambda i, j: (i, j)), out_shape=jax.ShapeDtypeStruct((M, N), out_dtype),
        compiler_params=pltpu.CompilerParams(dimension_semantics=("parallel", "parallel")), name=name)(a, b)


def rw(arr, width=None, cb=0):
    return (arr, arr.shape[1] if width is None else width, cb)


def rowwise(fn, rows, consts, outs, accs, tl, name, n_steps=None):
    if n_steps is None:
        n_steps = [r[0].shape[0] for r in rows if not isinstance(r[1], pl.BlockSpec)][0] // tl
    L = n_steps * tl
    nr, nc, no, na = len(rows), len(consts), len(outs), len(accs)
    in_specs, args = [], []
    for r in rows:
        if isinstance(r[1], pl.BlockSpec):
            in_specs.append(r[1])
        else:
            in_specs.append(pl.BlockSpec((tl, r[1]), functools.partial(lambda i, cb: (i, cb), cb=r[2])))
        args.append(r[0])
    for c in consts:
        in_specs.append(pl.BlockSpec(c.shape, functools.partial(lambda i, nd: (0,) * nd, nd=c.ndim)))
        args.append(c)
    out_specs = [pl.BlockSpec((tl, w), lambda i: (i, 0)) for w, _ in outs]
    out_shape = [jax.ShapeDtypeStruct((L, w), dt) for w, dt in outs]
    for s in accs:
        out_specs.append(pl.BlockSpec(s, functools.partial(lambda i, nd: (0,) * nd, nd=len(s))))
        out_shape.append(jax.ShapeDtypeStruct(s, F32))

    def body(*refs):
        ins = [r[...] for r in refs[:nr + nc]]
        o_refs = refs[nr + nc:nr + nc + no]
        a_refs = refs[nr + nc + no:]
        o_vals, a_vals = fn(*ins)
        for ref, val in zip(o_refs, o_vals):
            ref[...] = val.astype(ref.dtype)
        if na:
            @pl.when(pl.program_id(0) == 0)
            def _():
                for ref in a_refs:
                    ref[...] = jnp.zeros_like(ref)
            for ref, val in zip(a_refs, a_vals):
                ref[...] += val

    res = pl.pallas_call(
        body, grid=(n_steps,), in_specs=in_specs, out_specs=out_specs, out_shape=out_shape,
        compiler_params=pltpu.CompilerParams(dimension_semantics=("arbitrary",)), name=name)(*args)
    return res[:no], res[no:]


_K0 = math.sqrt(2.0 / math.pi)
_K1 = 0.044715


def gelu(x):
    return x * (0.5 * (1.0 + jnp.tanh(_K0 * (x + _K1 * (x * x * x)))))


def gelu_grad(x):
    t = jnp.tanh(_K0 * (x + _K1 * (x * x * x)))
    return 0.5 * (1.0 + t) + 0.5 * x * (1.0 - t * t) * (_K0 * (1.0 + 3.0 * _K1 * x * x))


def sigmoid(x):
    return 1.0 / (1.0 + jnp.exp(-x))


def silu(z):
    return z * sigmoid(z)


def silu_grad(z):
    s = sigmoid(z)
    return s * (1.0 + z * (1.0 - s))


def rms_fwd(x, g):
    r = lax.rsqrt(jnp.mean(x * x, axis=-1, keepdims=True) + EPS)
    return x * r * g


def rms_bwd(x, g, dy):
    r = lax.rsqrt(jnp.mean(x * x, axis=-1, keepdims=True) + EPS)
    xh = x * r
    dg = jnp.sum(dy * xh, axis=0, keepdims=True)
    dxh = dy * g
    dx = r * (dxh - xh * jnp.mean(dxh * xh, axis=-1, keepdims=True))
    return dx, dg


def _scan_chunk(a_r, a_i, pr_ref, pi_ref, cr, ci, T, reverse):
    row = lax.broadcasted_iota(jnp.int32, a_r.shape, 0)
    sgn = -1.0 if reverse else 1.0
    d = 1
    while d < T:
        k = (T - d) if reverse else (d - 1)
        wr = pr_ref[k:k + 1, :]
        wi = sgn * pi_ref[k:k + 1, :]
        if reverse:
            yr, yi, keep = pltpu.roll(a_r, T - d, 0), pltpu.roll(a_i, T - d, 0), row < T - d
        else:
            yr, yi, keep = pltpu.roll(a_r, d, 0), pltpu.roll(a_i, d, 0), row >= d
        a_r, a_i = (a_r + jnp.where(keep, wr * yr - wi * yi, 0.0), a_i + jnp.where(keep, wr * yi + wi * yr, 0.0))
        d *= 2
    wr = pr_ref[...]
    wi = sgn * pi_ref[...]
    c_r, c_i = cr[...], ci[...]
    a_r, a_i = a_r + (wr * c_r - wi * c_i), a_i + (wr * c_i + wi * c_r)
    k = 0 if reverse else T - 1
    cr[...] = a_r[k:k + 1, :]
    ci[...] = a_i[k:k + 1, :]
    return a_r, a_i


_NT = (((1,), (1,)), ((), ()))
_TN = (((0,), (0,)), ((), ()))


def s5_fwd(proj, d_skip, Bre, Bim, Cre, Cim, pr, pi):
    L = proj.shape[0]
    T, WC = min(SSM_T, L), SSM_WC
    nT = L // T

    def body(u_ref, d_ref, bre_ref, bim_ref, cre_ref, cim_ref, pr_ref, pi_ref, y_ref, yg_ref, sr_ref, si_ref, cr, ci):
        @pl.when(pl.program_id(1) == 0)
        def _():
            cr[...] = jnp.zeros_like(cr)
            ci[...] = jnp.zeros_like(ci)

        u = u_ref[...]
        ub = u.astype(BF16)
        a_r = jnp.dot(ub, bre_ref[0].astype(BF16), preferred_element_type=F32)
        a_i = jnp.dot(ub, bim_ref[0].astype(BF16), preferred_element_type=F32)
        a_r, a_i = _scan_chunk(a_r, a_i, pr_ref, pi_ref, cr, ci, T, False)
        sr_ref[...] = a_r
        si_ref[...] = a_i
        y = (jnp.dot(a_r.astype(BF16), cre_ref[0].astype(BF16), preferred_element_type=F32)
             + jnp.dot(a_i.astype(BF16), cim_ref[0].astype(BF16), preferred_element_type=F32) + d_ref[...] * u)
        y_ref[...] = y
        yg_ref[...] = gelu(y)

    uspec = pl.BlockSpec((T, 128), lambda k, i: (i, k))
    sspec = pl.BlockSpec((T, WC), lambda k, i: (i, k))
    return pl.pallas_call(
        body, grid=(8, nT),
        in_specs=[uspec, pl.BlockSpec((1, 128), lambda k, i: (0, k)),
                  pl.BlockSpec((1, 128, WC), lambda k, i: (k, 0, 0)), pl.BlockSpec((1, 128, WC), lambda k, i: (k, 0, 0)),
                  pl.BlockSpec((1, WC, 128), lambda k, i: (k, 0, 0)), pl.BlockSpec((1, WC, 128), lambda k, i: (k, 0, 0)),
                  pl.BlockSpec((T, WC), lambda k, i: (0, k)), pl.BlockSpec((T, WC), lambda k, i: (0, k))],
        out_specs=[uspec, uspec, sspec, sspec],
        out_shape=[jax.ShapeDtypeStruct((L, 1024), F32)] * 2 + [jax.ShapeDtypeStruct((L, 8 * WC), F32)] * 2,
        scratch_shapes=[pltpu.VMEM((1, WC), F32), pltpu.VMEM((1, WC), F32)],
        compiler_params=pltpu.CompilerParams(dimension_semantics=("parallel", "arbitrary")), name='a_ssm')(
            proj, d_skip, Bre, Bim, Cre, Cim, pr, pi)


def s5_bwd(proj, dyg1, dyg2, y, d_skip, s_re, s_im, Bre, Bim, Cre, Cim, prr, pir):
    L = proj.shape[0]
    T, WC = min(SSM_T, L), SSM_WC
    nT = L // T

    def body(u_ref, g1_ref, g2_ref, y_ref, d_ref, sr_ref, si_ref, spr_ref, spi_ref, bre_ref, bim_ref, cre_ref, cim_ref,
             pr_ref, pi_ref, du_ref, dd_ref, dbre_ref, dbim_ref, dcre_ref, dcim_ref, dar_ref, dai_ref, cr, ci):
        i = pl.program_id(1)

        @pl.when(i == 0)
        def _():
            for ref in (cr, ci, dd_ref, dbre_ref, dbim_ref, dcre_ref, dcim_ref, dar_ref, dai_ref):
                ref[...] = jnp.zeros_like(ref)

        u = u_ref[...]
        dy = (g1_ref[...] + g2_ref[...]) * gelu_grad(y_ref[...])
        dd_ref[...] += jnp.sum(dy * u, axis=0, keepdims=True)
        dyb, ub = dy.astype(BF16), u.astype(BF16)
        bre, bim, cre, cim = (r[0].astype(BF16) for r in (bre_ref, bim_ref, cre_ref, cim_ref))
        g_r = lax.dot_general(dyb, cre, _NT, preferred_element_type=F32)
        g_i = lax.dot_general(dyb, cim, _NT, preferred_element_type=F32)
        g_r, g_i = _scan_chunk(g_r, g_i, pr_ref, pi_ref, cr, ci, T, True)
        s_r, s_i = sr_ref[...], si_ref[...]
        row = lax.broadcasted_iota(jnp.int32, (T, WC), 0)
        first = (nT - 1 - i) == 0
        sp_r = jnp.where(row == 0, jnp.where(first, 0.0, spr_ref[7:8, :]), pltpu.roll(s_r, 1, 0))
        sp_i = jnp.where(row == 0, jnp.where(first, 0.0, spi_ref[7:8, :]), pltpu.roll(s_i, 1, 0))
        dar_ref[...] += jnp.sum(g_r * sp_r + g_i * sp_i, axis=0, keepdims=True)
        dai_ref[...] += jnp.sum(g_i * sp_r - g_r * sp_i, axis=0, keepdims=True)
        grb, gib = g_r.astype(BF16), g_i.astype(BF16)
        dcre_ref[0] += lax.dot_general(s_r.astype(BF16), dyb, _TN, preferred_element_type=F32)
        dcim_ref[0] += lax.dot_general(s_i.astype(BF16), dyb, _TN, preferred_element_type=F32)
        dbre_ref[0] += lax.dot_general(ub, grb, _TN, preferred_element_type=F32)
        dbim_ref[0] += lax.dot_general(ub, gib, _TN, preferred_element_type=F32)
        du_ref[...] = (dy * d_ref[...] + lax.dot_general(grb, bre, _NT, preferred_element_type=F32)
                       + lax.dot_general(gib, bim, _NT, preferred_element_type=F32))

    uspec = pl.BlockSpec((T, 128), lambda k, i: (nT - 1 - i, k))
    sspec = pl.BlockSpec((T, WC), lambda k, i: (nT - 1 - i, k))
    pspec = pl.BlockSpec((8, WC), lambda k, i: (jnp.maximum((nT - 1 - i) * (T // 8) - 1, 0), k))
    tab = pl.BlockSpec((T, WC), lambda k, i: (0, k))
    bspec = pl.BlockSpec((1, 128, WC), lambda k, i: (k, 0, 0))
    cspec = pl.BlockSpec((1, WC, 128), lambda k, i: (k, 0, 0))
    return pl.pallas_call(
        body, grid=(8, nT),
        in_specs=[uspec, uspec, uspec, uspec, pl.BlockSpec((1, 128), lambda k, i: (0, k)), sspec, sspec, pspec, pspec,
                  bspec, bspec, cspec, cspec, tab, tab],
        out_specs=[uspec, pl.BlockSpec((1, 128), lambda k, i: (0, k)), bspec, bspec, cspec, cspec,
                   pl.BlockSpec((1, WC), lambda k, i: (0, k)), pl.BlockSpec((1, WC), lambda k, i: (0, k))],
        out_shape=[jax.ShapeDtypeStruct((L, 1024), F32), jax.ShapeDtypeStruct((1, 1024), F32),
                   jax.ShapeDtypeStruct((8, 128, WC), F32), jax.ShapeDtypeStruct((8, 128, WC), F32),
                   jax.ShapeDtypeStruct((8, WC, 128), F32), jax.ShapeDtypeStruct((8, WC, 128), F32),
                   jax.ShapeDtypeStruct((1, 8 * WC), F32), jax.ShapeDtypeStruct((1, 8 * WC), F32)],
        scratch_shapes=[pltpu.VMEM((1, WC), F32), pltpu.VMEM((1, WC), F32)],
        compiler_params=pltpu.CompilerParams(dimension_semantics=("parallel", "arbitrary")), name='a_ssm_bwd')(
            proj, dyg1, dyg2, y, d_skip, s_re, s_im, s_re, s_im, Bre, Bim, Cre, Cim, prr, pir)


def s5_discretize(lam_re, lam_im, log_dt, b_re, b_im):
    dt = jnp.exp(log_dt)[:, None]
    mag = jnp.exp(lam_re * dt)
    ab_re = mag * jnp.cos(lam_im * dt)
    ab_im = mag * jnp.sin(lam_im * dt)
    den = lam_re * lam_re + lam_im * lam_im
    nr = ab_re - 1.0
    f_re = (nr * lam_re + ab_im * lam_im) / den
    f_im = (ab_im * lam_re - nr * lam_im) / den
    bb_re = f_re[..., None] * b_re - f_im[..., None] * b_im
    bb_im = f_re[..., None] * b_im + f_im[..., None] * b_re
    return ab_re, ab_im, bb_re, bb_im


_EYE8 = np.eye(8, dtype=np.float32)


def _b_tiles(bb):
    t = bb.transpose(0, 2, 1).reshape(8, 8, SSM_H, SSM_P)
    return jnp.einsum('kghp,gG->kghGp', t, _EYE8).reshape(8, 8 * SSM_H, 8 * SSM_P)


def _b_untile(d):
    t = jnp.einsum('kghGp,gG->kghp', d.reshape(8, 8, SSM_H, 8, SSM_P), _EYE8)
    return t.reshape(SSM_G, SSM_H, SSM_P).transpose(0, 2, 1)


def _c_tiles(c):
    t = c.transpose(0, 2, 1).reshape(8, 8, SSM_P, SSM_H)
    return jnp.einsum('kgph,gG->kgpGh', t, _EYE8).reshape(8, 8 * SSM_P, 8 * SSM_H)


def _c_untile(d):
    t = jnp.einsum('kgpGh,gG->kgph', d.reshape(8, 8, SSM_P, 8, SSM_H), _EYE8)
    return t.reshape(SSM_G, SSM_P, SSM_H).transpose(0, 2, 1)


def s5_powers(ar, ai, T):
    W = ar.shape[1]

    def body(ar_ref, ai_ref, fr_ref, fi_ref, rr_ref, ri_ref):
        fr_ref[0:1, :] = ar_ref[...]
        fi_ref[0:1, :] = ai_ref[...]
        rr_ref[T - 1:T, :] = ar_ref[...]
        ri_ref[T - 1:T, :] = ai_ref[...]
        n = 1
        while n < T:
            cr, ci = fr_ref[0:n, :], fi_ref[0:n, :]
            lr, li = fr_ref[n - 1:n, :], fi_ref[n - 1:n, :]
            fr_ref[n:2 * n, :] = cr * lr - ci * li
            fi_ref[n:2 * n, :] = cr * li + ci * lr
            cr, ci = rr_ref[T - n:T, :], ri_ref[T - n:T, :]
            rr_ref[T - 2 * n:T - n, :] = cr * lr - ci * li
            ri_ref[T - 2 * n:T - n, :] = cr * li + ci * lr
            n *= 2

    spec = pl.BlockSpec((T, SSM_WC), lambda j: (0, j))
    aspec = pl.BlockSpec((1, SSM_WC), lambda j: (0, j))
    return pl.pallas_call(
        body, grid=(W // SSM_WC,), in_specs=[aspec, aspec], out_specs=[spec] * 4,
        out_shape=[jax.ShapeDtypeStruct((T, W), F32)] * 4,
        compiler_params=pltpu.CompilerParams(dimension_semantics=("parallel",)), name='a_powers')(ar, ai)


def layer_a_fwd(h, w, p):
    L = h.shape[0]
    proj = mm(h, w['a_w_in'], 'nn', 'a_proj')
    disc = lambda *a: s5_discretize(*a)
    (ab_re, ab_im, bb_re, bb_im), disc_vjp = jax.vjp(disc, p['a_lam_re'][0], p['a_lam_im'][0], p['a_log_dt'][0],
                                                     p['a_b_re'][0], p['a_b_im'][0])
    Bre, Bim = _b_tiles(bb_re), _b_tiles(bb_im)
    Cre, Cim = _c_tiles(p['a_c_re'][0]), -_c_tiles(p['a_c_im'][0])
    T = min(SSM_T, L)
    pr, pi, prr, pir = s5_powers(ab_re.reshape(1, -1), ab_im.reshape(1, -1), T)
    y, yg, s_re, s_im = s5_fwd(proj, p['a_d'], Bre, Bim, Cre, Cim, pr, pi)
    gl = mm(yg, w['a_w_glu'], 'nn', 'a_glu')

    def f2(yg_, gl_, z, bg):
        return [yg_ * sigmoid(gl_ + bg) * silu(z)], []
    (po,), _ = rowwise(f2, [rw(yg), rw(gl), rw(proj, 1024, 1)], [p['a_b_glu']], [(1024, F32)], [], 256, 'a_gate')
    yb = mm(po, w['a_w_out'], 'nn', 'a_out')
    saved = dict(h=h, proj=proj, disc_vjp=disc_vjp, Bre=Bre, Bim=Bim, Cre=Cre, Cim=Cim, prr=prr, pir=pir, s_re=s_re,
                 s_im=s_im, y=y, yg=yg, gl=gl, po=po)
    return yb, saved


def layer_a_bwd(dyb, w, p, sv):
    g = {}
    dpo = mm(dyb, w['a_w_out'], 'nt', 'a_dpo')
    g['a_w_out'] = mm(sv['po'], dyb, 'tn', 'a_dwout')
    proj = sv['proj']

    def f1(dpo_, yg, gl, z, bg):
        sg = sigmoid(gl + bg)
        sz = silu(z)
        dm = dpo_ * sz
        dz = dpo_ * (yg * sg) * silu_grad(z)
        dgl = dm * yg * sg * (1.0 - sg)
        return [dz, dm * sg, dgl], [jnp.sum(dgl, axis=0, keepdims=True)]
    (dz, dyg1, dgl), (db_glu,) = rowwise(f1, [rw(dpo), rw(sv['yg']), rw(sv['gl']), rw(proj, 1024, 1)], [p['a_b_glu']],
                                          [(1024, F32)] * 3, [(1, 1024)], 256, 'a_gate_bwd')
    g['a_b_glu'] = db_glu
    g['a_w_glu'] = mm(sv['yg'], dgl, 'tn', 'a_dwglu')
    dyg2 = mm(dgl, w['a_w_glu'], 'nt', 'a_dyg2')

    du, dd, dBre, dBim, dCre, dCim, da_re, da_im = s5_bwd(proj, dyg1, dyg2, sv['y'], p['a_d'], sv['s_re'], sv['s_im'],
                                                           sv['Bre'], sv['Bim'], sv['Cre'], sv['Cim'], sv['prr'],
                                                           sv['pir'])
    g['a_d'] = dd
    dCim = -dCim

    def f3(du_, dz_):
        return [jnp.concatenate([du_, dz_], axis=1)], []
    (dproj,), _ = rowwise(f3, [rw(du), rw(dz)], [], [(2048, F32)], [], 256, 'a_dproj')
    dlr, dli, dldt, dbr, dbi = sv['disc_vjp']((da_re.reshape(SSM_G, SSM_P), da_im.reshape(SSM_G, SSM_P),
                                               _b_untile(dBre), _b_untile(dBim)))
    g['a_lam_re'], g['a_lam_im'], g['a_log_dt'] = dlr[None], dli[None], dldt[None]
    g['a_b_re'], g['a_b_im'] = dbr[None], dbi[None]
    g['a_c_re'], g['a_c_im'] = _c_untile(dCre)[None], _c_untile(dCim)[None]
    g['a_w_in'] = mm(sv['h'], dproj, 'tn', 'a_dwin')
    dh = mm(dproj, w['a_w_in'], 'nt', 'a_dh')
    return dh, g


def _t5_bucket_np():
    qi = np.arange(WINDOW)[:, None]
    kj = np.arange(2 * WINDOW)[None, :]
    dist = np.maximum(qi + WINDOW - kj, 0)
    max_exact = REL_BUCKETS // 2
    dist_f = np.maximum(dist, 1).astype(np.float32)
    large = max_exact + (np.log(dist_f / np.float32(max_exact)) / np.float32(math.log(REL_MAX_DIST / max_exact))
                         * np.float32(REL_BUCKETS - max_exact)).astype(np.int32)
    large = np.minimum(large, REL_BUCKETS - 1)
    return np.where(dist < max_exact, dist, large).astype(np.int32)


SWA_GRP = SWA_HEADS // SWA_KV


def _swa_kv(kvp, kvc, kvh):
    kb = jnp.concatenate([kvp[:, kvh * 64:(kvh + 1) * 64], kvc[:, kvh * 64:(kvh + 1) * 64]], 0).astype(BF16)
    vb = jnp.concatenate([kvp[:, 128 + kvh * 64:128 + (kvh + 1) * 64], kvc[:, 128 + kvh * 64:128 + (kvh + 1) * 64]],
                         0).astype(BF16)
    return kb, vb


def _swa_stack(x, kvh):
    return jnp.concatenate([x[:, (kvh * SWA_GRP + g) * 64:(kvh * SWA_GRP + g + 1) * 64] for g in range(SWA_GRP)],
                           axis=0).astype(BF16)


def _swa_group(bias_ref, kvh):
    return bias_ref[kvh * SWA_GRP:(kvh + 1) * SWA_GRP].reshape(SWA_GRP * WINDOW, 2 * WINDOW)


def _swa_sinks(sink_ref, kvh):
    return jnp.concatenate([jnp.broadcast_to(sink_ref[0:1, kvh * SWA_GRP + g:kvh * SWA_GRP + g + 1], (WINDOW, 1))
                            for g in range(SWA_GRP)], axis=0)


def _swa_probs(q, kb, bias_h, sink, valid):
    s = lax.dot_general(q, kb, (((1,), (1,)), ((), ())), preferred_element_type=F32) * (HEAD_DIM ** -0.5)
    s = jnp.where(valid, s + bias_h, NEG_INF)
    m = jnp.maximum(jnp.max(s, axis=-1, keepdims=True), sink)
    e = jnp.exp(s - m)
    es = jnp.exp(sink - m)
    den = jnp.sum(e, axis=-1, keepdims=True) + es
    return e / den, es / den


def _swa_valid(n):
    qi = lax.broadcasted_iota(jnp.int32, (SWA_GRP * WINDOW, 2 * WINDOW), 0) & (WINDOW - 1)
    kj = lax.broadcasted_iota(jnp.int32, (SWA_GRP * WINDOW, 2 * WINDOW), 1)
    dist = qi + WINDOW - kj
    return (dist >= 0) & (dist < WINDOW) & ((kj >= WINDOW) | (n > 0))


def swa_fwd(proj, bias, sinks):
    L = proj.shape[0]

    def body(z_ref, q_ref, kvc_ref, kvp_ref, bias_ref, sink_ref, o_ref, po_ref):
        n = pl.program_id(0)
        valid = _swa_valid(n)
        q, kvc, kvp = q_ref[...], kvc_ref[...], kvp_ref[...]
        outs = []
        for kvh in range(SWA_KV):
            kb, vb = _swa_kv(kvp, kvc, kvh)
            p, _ = _swa_probs(_swa_stack(q, kvh), kb, _swa_group(bias_ref, kvh), _swa_sinks(sink_ref, kvh), valid)
            o8 = jnp.dot(p.astype(BF16), vb, preferred_element_type=F32)
            outs += [o8[g * WINDOW:(g + 1) * WINDOW] for g in range(SWA_GRP)]
        o = jnp.concatenate(outs, axis=1)
        o_ref[...] = o
        po_ref[...] = o * silu(z_ref[...])

    return pl.pallas_call(
        body, grid=(L // WINDOW,),
        in_specs=[pl.BlockSpec((WINDOW, 1024), lambda n: (n, 0)), pl.BlockSpec((WINDOW, 1024), lambda n: (n, 1)),
                  pl.BlockSpec((WINDOW, 256), lambda n: (n, 8)),
                  pl.BlockSpec((WINDOW, 256), lambda n: (jnp.maximum(n - 1, 0), 8)),
                  pl.BlockSpec((SWA_HEADS, WINDOW, 2 * WINDOW), lambda n: (0, 0, 0)),
                  pl.BlockSpec((1, SWA_HEADS), lambda n: (0, 0))],
        out_specs=[pl.BlockSpec((WINDOW, 1024), lambda n: (n, 0))] * 2,
        out_shape=[jax.ShapeDtypeStruct((L, 1024), F32)] * 2,
        compiler_params=pltpu.CompilerParams(dimension_semantics=("parallel",)), name='b_attn')(
            proj, proj, proj, proj, bias, sinks)


def swa_bwd(proj, do, bias, sinks):
    L = proj.shape[0]

    def body(q_ref, kvc_ref, kvp_ref, do_ref, bias_ref, sink_ref, dq_ref, dkv_ref, dbias_ref, dsink_ref):
        n = pl.program_id(0)

        @pl.when(n == 0)
        def _():
            dkv_ref[...] = jnp.zeros_like(dkv_ref)
            dbias_ref[...] = jnp.zeros_like(dbias_ref)
            dsink_ref[...] = jnp.zeros_like(dsink_ref)

        valid = _swa_valid(n)
        q, kvc, kvp, do_ = q_ref[...], kvc_ref[...], kvp_ref[...], do_ref[...]
        dqs, dks, dvs, dsk = [], [], [], []
        for kvh in range(SWA_KV):
            kb, vb = _swa_kv(kvp, kvc, kvh)
            q8, do8 = _swa_stack(q, kvh), _swa_stack(do_, kvh)
            p, ps = _swa_probs(q8, kb, _swa_group(bias_ref, kvh), _swa_sinks(sink_ref, kvh), valid)
            dp = lax.dot_general(do8, vb, (((1,), (1,)), ((), ())), preferred_element_type=F32)
            delta = jnp.sum(p * dp, axis=-1, keepdims=True)
            ds = p * (dp - delta)
            col = -ps * delta
            dsk += [jnp.sum(col[g * WINDOW:(g + 1) * WINDOW], axis=0, keepdims=True) for g in range(SWA_GRP)]
            dbias_ref[kvh * SWA_GRP:(kvh + 1) * SWA_GRP] += ds.reshape(SWA_GRP, WINDOW, 2 * WINDOW)
            dsb = (ds * (HEAD_DIM ** -0.5)).astype(BF16)
            dq8 = jnp.dot(dsb, kb, preferred_element_type=F32)
            dqs += [dq8[g * WINDOW:(g + 1) * WINDOW] for g in range(SWA_GRP)]
            dks.append(lax.dot_general(dsb, q8, (((0,), (0,)), ((), ())), preferred_element_type=F32))
            dvs.append(lax.dot_general(p.astype(BF16), do8, (((0,), (0,)), ((), ())), preferred_element_type=F32))
        dq_ref[...] = jnp.concatenate(dqs, axis=1)
        dsink_ref[...] += jnp.concatenate(dsk, axis=1)
        both = jnp.concatenate(dks + dvs, axis=1)
        r_cur = pl.multiple_of(n * WINDOW, WINDOW)
        r_prev = pl.multiple_of(jnp.maximum(n - 1, 0) * WINDOW, WINDOW)
        dkv_ref[pl.ds(r_prev, WINDOW), :] += both[:WINDOW]
        dkv_ref[pl.ds(r_cur, WINDOW), :] += both[WINDOW:]

    return pl.pallas_call(
        body, grid=(L // WINDOW,),
        in_specs=[pl.BlockSpec((WINDOW, 1024), lambda n: (n, 1)), pl.BlockSpec((WINDOW, 256), lambda n: (n, 8)),
                  pl.BlockSpec((WINDOW, 256), lambda n: (jnp.maximum(n - 1, 0), 8)),
                  pl.BlockSpec((WINDOW, 1024), lambda n: (n, 0)),
                  pl.BlockSpec((SWA_HEADS, WINDOW, 2 * WINDOW), lambda n: (0, 0, 0)),
                  pl.BlockSpec((1, SWA_HEADS), lambda n: (0, 0))],
        out_specs=[pl.BlockSpec((WINDOW, 1024), lambda n: (n, 0)), pl.BlockSpec((L, 256), lambda n: (0, 0)),
                   pl.BlockSpec((SWA_HEADS, WINDOW, 2 * WINDOW), lambda n: (0, 0, 0)),
                   pl.BlockSpec((1, SWA_HEADS), lambda n: (0, 0))],
        out_shape=[jax.ShapeDtypeStruct((L, 1024), F32), jax.ShapeDtypeStruct((L, 256), F32),
                   jax.ShapeDtypeStruct((SWA_HEADS, WINDOW, 2 * WINDOW), F32), jax.ShapeDtypeStruct((1, SWA_HEADS), F32)],
        compiler_params=pltpu.CompilerParams(dimension_semantics=("arbitrary",)), name='b_attn_bwd')(
            proj, proj, proj, do, bias, sinks)


def swa_bias(rel_bias):
    def body(bk_ref, rb_ref, o_ref):
        bk = bk_ref[...]
        for h in range(SWA_HEADS):
            acc = jnp.zeros((WINDOW, 2 * WINDOW), F32)
            for b in range(REL_BUCKETS):
                acc = jnp.where(bk == b, rb_ref[b, h], acc)
            o_ref[h] = acc

    return pl.pallas_call(
        body, out_shape=jax.ShapeDtypeStruct((SWA_HEADS, WINDOW, 2 * WINDOW), F32),
        in_specs=[pl.BlockSpec(memory_space=pltpu.VMEM), pl.BlockSpec(memory_space=pltpu.SMEM)],
        out_specs=pl.BlockSpec(memory_space=pltpu.VMEM), name='b_bias')(jnp.asarray(_t5_bucket_np()), rel_bias)


def layer_b_fwd(h, w, p):
    proj = mm(h, w['b_w_in'], 'nn', 'b_proj')
    bias = swa_bias(p['rel_bias'])
    o, po = swa_fwd(proj, bias, p['b_sinks'])
    yb = mm(po, w['b_w_out'], 'nn', 'b_out')
    return yb, dict(h=h, proj=proj, bias=bias, o=o, po=po)


def layer_b_bwd(dyb, w, p, sv):
    g = {}
    dpo = mm(dyb, w['b_w_out'], 'nt', 'b_dpo')
    g['b_w_out'] = mm(sv['po'], dyb, 'tn', 'b_dwout')
    proj = sv['proj']

    def f1(dpo_, o, z):
        return [dpo_ * silu(z), dpo_ * o * silu_grad(z)], []
    (do, dz), _ = rowwise(f1, [rw(dpo), rw(sv['o']), rw(proj, 1024, 0)], [], [(1024, F32)] * 2, [], 256, 'b_gate_bwd')
    dq, dkv, dbias, dsinks = swa_bwd(proj, do, sv['bias'], p['b_sinks'])
    g['b_sinks'] = dsinks
    onehot = jnp.asarray(np.eye(REL_BUCKETS, dtype=np.float32)[_t5_bucket_np().reshape(-1)])

    def f2(db, oh):
        return [], [lax.dot_general(db, oh, (((1,), (0,)), ((), ())), preferred_element_type=F32,
                                    precision=lax.Precision.HIGHEST)]
    _, (drel,) = rowwise(f2, [(dbias.reshape(SWA_HEADS, -1), pl.BlockSpec((SWA_HEADS, 4096), lambda i: (0, i))),
                              (onehot, pl.BlockSpec((4096, REL_BUCKETS), lambda i: (i, 0)))], [], [],
                         [(SWA_HEADS, REL_BUCKETS)], 4096, 'b_drel', n_steps=(2 * WINDOW * WINDOW) // 4096)
    g['rel_bias'] = drel.T

    def f3(dz_, dq_, dkv_):
        return [jnp.concatenate([dz_, dq_, dkv_], axis=1)], []
    (dproj,), _ = rowwise(f3, [rw(dz), rw(dq), rw(dkv)], [], [(2304, F32)], [], 256, 'b_dproj')
    g['b_w_in'] = mm(sv['h'], dproj, 'tn', 'b_dwin')
    dh = mm(dproj, w['b_w_in'], 'nt', 'b_dh')
    return dh, g


MLA_SCALE = (MLA_NOPE + MLA_ROPE) ** -0.5


def _rope_tables(L):
    inv = ROPE_BASE ** (-jnp.arange(0, MLA_ROPE, 2, dtype=F32) / MLA_ROPE)
    ang = jnp.arange(L, dtype=F32)[:, None] * inv[None, :]
    c, s = jnp.cos(ang), jnp.sin(ang)
    one, zero, pad = jnp.ones((L, 128), F32), jnp.zeros((L, 128), F32), jnp.zeros((L, 64), F32)
    return (jnp.concatenate([one, c, c, c, c, pad], 1), jnp.concatenate([zero, s, s, s, s, pad], 1))


def _rot(x, transpose=False):
    w = x.shape[1]
    lane = lax.broadcasted_iota(jnp.int32, x.shape, 1)
    up = pltpu.roll(x, w - 16, 1)
    dn = pltpu.roll(x, 16, 1)
    first = (lane % 32) < 16
    return jnp.where(first, up, -dn) if transpose else jnp.where(first, -up, dn)


MLA_QT = 512


def _mla_exp(qf, kf, t, qt):
    n_k = kf.shape[0]
    s = lax.dot_general(qf, kf, (((1,), (1,)), ((), ())), preferred_element_type=F32) * MLA_SCALE
    qpos = t * qt + lax.broadcasted_iota(jnp.int32, (qt, n_k), 0)
    kpos = lax.broadcasted_iota(jnp.int32, (qt, n_k), 1)
    s = jnp.where(kpos <= qpos, s, NEG_INF)
    e = jnp.exp(s - jnp.max(s, axis=-1, keepdims=True))
    return e, jnp.sum(e, axis=-1, keepdims=True)


def _mla_heads(q, kv, kr):
    out = []
    for j in range(2):
        qf = jnp.concatenate([q[:, j * 64:(j + 1) * 64], q[:, 128 + j * 32:128 + (j + 1) * 32]], axis=1)
        kf = jnp.concatenate([kv[:, j * 64:(j + 1) * 64], kr], axis=1)
        out.append((qf, kf, kv[:, 128 + j * 64:128 + (j + 1) * 64]))
    return out


def mla_fwd(q, kv, kr):
    L = q.shape[0]
    qt = min(MLA_QT, L)
    nq = L // qt

    def body(q_ref, kv_ref, kr_ref, o_ref):
        for t in range(nq):
            @pl.when(pl.program_id(1) == t)
            def _(t=t):
                n_k = (t + 1) * qt
                outs = []
                for qf, kf, v in _mla_heads(q_ref[...], kv_ref[0:n_k, :], kr_ref[0:n_k, 0:MLA_ROPE]):
                    e, den = _mla_exp(qf, kf, t, qt)
                    outs.append(jnp.dot(e.astype(BF16), v, preferred_element_type=F32) / den)
                o_ref[...] = jnp.concatenate(outs, axis=1)

    return pl.pallas_call(
        body, grid=(MLA_HEADS // 2, nq),
        in_specs=[pl.BlockSpec((qt, 256), lambda hp, n: (n, hp)), pl.BlockSpec((L, 256), lambda hp, n: (0, hp)),
                  pl.BlockSpec((L, 128), lambda hp, n: (0, 0))],
        out_specs=pl.BlockSpec((qt, 128), lambda hp, n: (n, hp)), out_shape=jax.ShapeDtypeStruct((L, 1024), F32),
        compiler_params=pltpu.CompilerParams(dimension_semantics=("parallel", "parallel")), name='c_attn')(q, kv, kr)


def mla_bwd(q, kv, kr, do):
    L = q.shape[0]
    qt = min(MLA_QT, L)
    nq = L // qt

    def body(q_ref, kv_ref, kr_ref, do_ref, dq_ref, dkv_ref, dkr_ref):
        @pl.when(pl.program_id(1) == 0)
        def _():
            dkv_ref[...] = jnp.zeros_like(dkv_ref)
            dkr_ref[...] = jnp.zeros_like(dkr_ref)

        for t in range(nq):
            @pl.when(pl.program_id(1) == t)
            def _(t=t):
                n_k = (t + 1) * qt
                do_ = do_ref[...]
                dqn, dqr, dkn, dvs = [], [], [], []
                dkr = jnp.zeros((n_k, MLA_ROPE), F32)
                for j, (qf, kf, v) in enumerate(_mla_heads(q_ref[...], kv_ref[0:n_k, :], kr_ref[0:n_k, 0:MLA_ROPE])):
                    doh = do_[:, j * 64:(j + 1) * 64]
                    e, den = _mla_exp(qf, kf, t, qt)
                    p = e * (1.0 / den)
                    dp = lax.dot_general(doh, v, (((1,), (1,)), ((), ())), preferred_element_type=F32)
                    ds = (p * (dp - jnp.sum(p * dp, axis=-1, keepdims=True)) * MLA_SCALE).astype(BF16)
                    dqf = jnp.dot(ds, kf, preferred_element_type=F32)
                    dkf = lax.dot_general(ds, qf, (((0,), (0,)), ((), ())), preferred_element_type=F32)
                    dvs.append(lax.dot_general(p.astype(BF16), doh, (((0,), (0,)), ((), ())), preferred_element_type=F32))
                    dqn.append(dqf[:, :MLA_NOPE])
                    dqr.append(dqf[:, MLA_NOPE:])
                    dkn.append(dkf[:, :MLA_NOPE])
                    dkr = dkr + dkf[:, MLA_NOPE:]
                dq_ref[...] = jnp.concatenate(dqn + dqr + [jnp.zeros((qt, 64), F32)], axis=1)
                dkv_ref[0:n_k, :] += jnp.concatenate(dkn + dvs, axis=1)
                dkr_ref[0, 0:n_k, :] += jnp.concatenate([dkr, jnp.zeros((n_k, 128 - MLA_ROPE), F32)], axis=1)

    return pl.pallas_call(
        body, grid=(MLA_HEADS // 2, nq),
        in_specs=[pl.BlockSpec((qt, 256), lambda hp, n: (n, hp)), pl.BlockSpec((L, 256), lambda hp, n: (0, hp)),
                  pl.BlockSpec((L, 128), lambda hp, n: (0, 0)), pl.BlockSpec((qt, 128), lambda hp, n: (n, hp))],
        out_specs=[pl.BlockSpec((qt, 256), lambda hp, n: (n, hp)), pl.BlockSpec((L, 256), lambda hp, n: (0, hp)),
                   pl.BlockSpec((1, L, 128), lambda hp, n: (hp, 0, 0))],
        out_shape=[jax.ShapeDtypeStruct((L, 2048), F32), jax.ShapeDtypeStruct((L, 2048), F32),
                   jax.ShapeDtypeStruct((MLA_HEADS // 2, L, 128), F32)],
        compiler_params=pltpu.CompilerParams(dimension_semantics=("parallel", "arbitrary")), name='c_attn_bwd')(
            q, kv, kr, do)


def layer_c_fwd(h, w, p):
    L = h.shape[0]
    proj = mm(h, w['c_w_in'], 'nn', 'c_proj')

    def f1(c, gq, gk):
        return [rms_fwd(c[:, :768], gq), rms_fwd(c[:, 768:], gk)], []
    (cqn, ckvn), _ = rowwise(f1, [rw(proj, 1024, 1)], [p['c_q_norm'], p['c_kv_norm']], [(768, BF16), (256, BF16)], [],
                             256, 'c_norms')
    qf = mm(cqn, w['c_w_uq'], 'nn', 'c_uq')
    kvf = mm(ckvn, w['c_w_ukv'], 'nn', 'c_ukv', out_dtype=BF16)
    cos, sin = _rope_tables(L)

    def f2(q_, kr_, c, s):
        c8, s8 = jnp.tile(c, (1, 8)), jnp.tile(s, (1, 8))
        return [q_ * c8 + _rot(q_) * s8, kr_ * c[:, 128:] + _rot(kr_) * s[:, 128:]], []
    (q, kr), _ = rowwise(f2, [rw(qf), rw(proj, 128, 16), rw(cos), rw(sin)], [], [(2048, BF16), (128, BF16)], [], 256,
                         'c_rope')
    o = mla_fwd(q, kvf, kr)

    def f3(o_, z):
        return [o_ * silu(z)], []
    (po,), _ = rowwise(f3, [rw(o), rw(proj, 1024, 0)], [], [(1024, F32)], [], 256, 'c_gate')
    yb = mm(po, w['c_w_out'], 'nn', 'c_out')
    return yb, dict(h=h, proj=proj, cqn=cqn, ckvn=ckvn, q=q, kv=kvf, kr=kr, o=o, po=po, cos=cos, sin=sin)


def layer_c_bwd(dyb, w, p, sv):
    g = {}
    dpo = mm(dyb, w['c_w_out'], 'nt', 'c_dpo')
    g['c_w_out'] = mm(sv['po'], dyb, 'tn', 'c_dwout')
    proj = sv['proj']
    L = proj.shape[0]

    def f1(dpo_, o, z):
        return [dpo_ * silu(z), dpo_ * o * silu_grad(z)], []
    (do, dz), _ = rowwise(f1, [rw(dpo), rw(sv['o']), rw(proj, 1024, 0)], [], [(1024, BF16), (1024, F32)], [], 256,
                          'c_gate_bwd')
    dq, dkvf, dkr8 = mla_bwd(sv['q'], sv['kv'], sv['kr'], do)

    def f2(dq_, dkr_, c, s):
        c8, s8 = jnp.tile(c, (1, 8)), jnp.tile(s, (1, 8))
        dk = jnp.sum(dkr_, axis=0)
        return [dq_ * c8 + _rot(dq_ * s8, True), dk * c[:, 128:] + _rot(dk * s[:, 128:], True)], []
    tl = 256
    (dqf, dkr), _ = rowwise(f2, [rw(dq), (dkr8, pl.BlockSpec((8, tl, 128), lambda i: (0, i, 0))), rw(sv['cos']),
                                 rw(sv['sin'])], [], [(2048, F32), (128, F32)], [], tl, 'c_rope_bwd')
    g['c_w_uq'] = mm(sv['cqn'], dqf, 'tn', 'c_dwuq')
    g['c_w_ukv'] = mm(sv['ckvn'], dkvf, 'tn', 'c_dwukv')
    dcqn = mm(dqf, w['c_w_uq'], 'nt', 'c_dcqn')
    dckvn = mm(dkvf, w['c_w_ukv'], 'nt', 'c_dckvn')

    def f3(c, dq_, dk_, dz_, dkr_, gq, gk):
        dcq, dgq = rms_bwd(c[:, :768], gq, dq_)
        dckv, dgk = rms_bwd(c[:, 768:], gk, dk_)
        return [jnp.concatenate([dz_, dcq, dckv, dkr_], axis=1)], [dgq, dgk]
    (dproj,), (dgq, dgk) = rowwise(f3, [rw(proj, 1024, 1), rw(dcqn), rw(dckvn), rw(dz), rw(dkr)],
                                   [p['c_q_norm'], p['c_kv_norm']], [(2176, F32)], [(1, 768), (1, 256)], 256, 'c_dproj')
    g['c_q_norm'], g['c_kv_norm'] = dgq, dgk
    g['c_w_in'] = mm(sv['h'], dproj, 'tn', 'c_dwin')
    dh = mm(dproj, w['c_w_in'], 'nt', 'c_dh')
    return dh, g


def _sgu_mix(wm, v, transpose):
    outs = []
    dims = (((0,), (0,)), ((), ())) if transpose else (((1,), (0,)), ((), ()))
    for gi in range(SGU_G):
        outs.append(lax.dot_general(wm[gi], v[:, gi * SGU_C:(gi + 1) * SGU_C].astype(BF16), dims,
                                    preferred_element_type=F32))
    return jnp.concatenate(outs, axis=1)


def _sgu_wmask(ws):
    t = lax.broadcasted_iota(jnp.int32, (SGU_T, SGU_T), 0)
    s = lax.broadcasted_iota(jnp.int32, (SGU_T, SGU_T), 1)
    return jnp.where((s <= t)[None], ws, 0.0).astype(BF16)


def _ln_stats(v):
    mu = jnp.mean(v, axis=-1, keepdims=True)
    vc = v - mu
    rstd = lax.rsqrt(jnp.mean(vc * vc, axis=-1, keepdims=True) + EPS)
    return vc * rstd, rstd


def layer_d_fwd(h, w, p):
    proj = mm(h, w['d_w_in'], 'nn', 'd_proj')
    bias = jnp.repeat(p['d_b_s'][0].T, SGU_C, axis=1)

    def f1(u_, v_, z, ws, lg, lb, bs):
        xh, _ = _ln_stats(gelu(v_))
        s = _sgu_mix(_sgu_wmask(ws), xh * lg + lb, False) + bs
        return [gelu(u_) * s * silu(z)], []
    (po,), _ = rowwise(f1, [rw(proj, 1024, 0), rw(proj, 1024, 1), rw(proj, 1024, 2)],
                       [p['d_w_s'][0], p['d_ln_g'], p['d_ln_b'], bias], [(1024, F32)], [], SGU_T, 'd_mix')
    yb = mm(po, w['d_w_out'], 'nn', 'd_out')
    return yb, dict(h=h, proj=proj, po=po, bias=bias)


def layer_d_bwd(dyb, w, p, sv):
    g = {}
    dpo = mm(dyb, w['d_w_out'], 'nt', 'd_dpo')
    g['d_w_out'] = mm(sv['po'], dyb, 'tn', 'd_dwout')
    proj = sv['proj']

    def f1(dpo_, u_, v_, z, ws, lg, lb, bs):
        wm = _sgu_wmask(ws)
        gv = gelu(v_)
        xh, rstd = _ln_stats(gv)
        vn = xh * lg + lb
        s = _sgu_mix(wm, vn, False) + bs
        gu, sz = gelu(u_), silu(z)
        du = dpo_ * s * sz
        ds = dpo_ * gu * sz
        dz = dpo_ * gu * s * silu_grad(z)
        dsb = ds.astype(BF16)
        dws = jnp.stack([lax.dot_general(dsb[:, gi * SGU_C:(gi + 1) * SGU_C], vn[:, gi * SGU_C:(gi + 1) * SGU_C].astype(BF16),
                                         (((1,), (1,)), ((), ())), preferred_element_type=F32) for gi in range(SGU_G)])
        dvn = _sgu_mix(wm, ds, True)
        dlg = jnp.sum(dvn * xh, axis=0, keepdims=True)
        dlb = jnp.sum(dvn, axis=0, keepdims=True)
        dxh = dvn * lg
        dgv = rstd * (dxh - jnp.mean(dxh, axis=-1, keepdims=True) - xh * jnp.mean(dxh * xh, axis=-1, keepdims=True))
        return ([jnp.concatenate([du * gelu_grad(u_), dgv * gelu_grad(v_), dz], axis=1)], [dws, ds, dlg, dlb])
    (dproj,), (dws, dbs, dlg, dlb) = rowwise(
        f1, [rw(dpo), rw(proj, 1024, 0), rw(proj, 1024, 1), rw(proj, 1024, 2)],
        [p['d_w_s'][0], p['d_ln_g'], p['d_ln_b'], sv['bias']], [(3072, F32)],
        [(SGU_G, SGU_T, SGU_T), (SGU_T, 1024), (1, 1024), (1, 1024)], SGU_T, 'd_mix_bwd')
    tril = np.tril(np.ones((SGU_T, SGU_T), dtype=bool))
    g['d_w_s'] = jnp.where(tril[None], dws, 0.0)[None]
    g['d_b_s'] = dbs.reshape(SGU_T, SGU_G, SGU_C).sum(-1).T[None]
    g['d_ln_g'], g['d_ln_b'] = dlg, dlb
    g['d_w_in'] = mm(sv['h'], dproj, 'tn', 'd_dwin')
    dh = mm(dproj, w['d_w_in'], 'nt', 'd_dh')
    return dh, g


def _coords():
    return lax.axis_index("x"), lax.axis_index("y"), lax.axis_index("c")


def all_gather(x, name):
    def body(x_ref, out_ref, send_sems, recv_sems, local_sem):
        x_, y_, c_ = _coords()
        me, sibling = (x_, y_, c_), (x_, y_, 1 - c_)
        chips = [(1 - x_, y_), (x_, 1 - y_), (1 - x_, 1 - y_)]

        def slot(px, py, pc):
            return out_ref.at[4 * px + 2 * py + pc]

        def copy(k, block, to, src=None):
            return pltpu.make_async_remote_copy(src_ref=slot(*block) if src is None else src, dst_ref=slot(*block),
                                                send_sem=send_sems.at[k], recv_sem=recv_sems.at[k], device_id=to,
                                                device_id_type=MESH)

        mine = pltpu.make_async_copy(x_ref, slot(*me), local_sem)
        mine.start()
        first = [copy(0, me, sibling, src=x_ref)]
        first += [copy(1 + j, me, (*chip, c_), src=x_ref) for j, chip in enumerate(chips)]
        for cp in first:
            cp.start()
        passed = [copy(4 + j, (*chip, c_), sibling) for j, chip in enumerate(chips)]
        for j, chip in enumerate(chips):
            copy(1 + j, (*chip, c_), me).wait_recv()
            passed[j].start()
        copy(0, sibling, me).wait_recv()
        for j, chip in enumerate(chips):
            copy(4 + j, (*chip, 1 - c_), me).wait_recv()
        for cp in first + passed:
            cp.wait_send()
        mine.wait()

    return pl.pallas_call(
        body, out_shape=jax.ShapeDtypeStruct((N_DEV,) + x.shape, x.dtype), in_specs=[ANY], out_specs=ANY,
        scratch_shapes=[pltpu.SemaphoreType.DMA((7,)), pltpu.SemaphoreType.DMA((7,)), pltpu.SemaphoreType.DMA(())],
        name=name)(x)


def rs_sibling(gfull, tag):
    _, R, C = gfull.shape

    def body(g_ref, land_ref, send_sems, recv_sems):
        x_, y_, c_ = _coords()
        copies = []
        for k in range(4):
            cp = pltpu.make_async_remote_copy(src_ref=g_ref.at[2 * k + 1 - c_], dst_ref=land_ref.at[k],
                                              send_sem=send_sems.at[k], recv_sem=recv_sems.at[k],
                                              device_id=(x_, y_, 1 - c_), device_id_type=MESH)
            cp.start()
            copies.append(cp)
        for cp in copies:
            cp.wait_recv()
        for cp in copies:
            cp.wait_send()

    return pl.pallas_call(
        body, out_shape=jax.ShapeDtypeStruct((4, R, C), gfull.dtype), in_specs=[ANY], out_specs=ANY,
        scratch_shapes=[pltpu.SemaphoreType.DMA((4,)), pltpu.SemaphoreType.DMA((4,))], name='rs_sibling_' + tag)(gfull)


def rs_pair_add(gfull, land, core, tag):
    _, R, C = gfull.shape
    tl = R

    def body(c_ref, g_ref, l_ref, o_ref):
        o_ref[...] = (g_ref[...] + l_ref[...]).astype(BF16)

    return pl.pallas_call(
        body, out_shape=jax.ShapeDtypeStruct((4, R, C), BF16),
        grid_spec=pltpu.PrefetchScalarGridSpec(
            num_scalar_prefetch=1, grid=(4, R // tl),
            in_specs=[pl.BlockSpec((1, tl, C), lambda k, i, c: (2 * k + c[0], i, 0)),
                      pl.BlockSpec((1, tl, C), lambda k, i, c: (k, i, 0))],
            out_specs=pl.BlockSpec((1, tl, C), lambda k, i, c: (k, i, 0))),
        compiler_params=pltpu.CompilerParams(dimension_semantics=("parallel", "parallel")), name='rs_pair_add_' + tag)(
            core, gfull, land)


def rs_chips(part, tag):
    _, R, C = part.shape

    def body(p_ref, land_ref, send_sems, recv_sems):
        x_, y_, c_ = _coords()
        copies = []
        for r, (fx, fy) in enumerate([(1, 0), (0, 1), (1, 1)]):
            tx = jnp.where(fx == 1, 1 - x_, x_)
            ty = jnp.where(fy == 1, 1 - y_, y_)
            cp = pltpu.make_async_remote_copy(src_ref=p_ref.at[2 * tx + ty], dst_ref=land_ref.at[r],
                                              send_sem=send_sems.at[r], recv_sem=recv_sems.at[r],
                                              device_id=(tx, ty, c_), device_id_type=MESH)
            cp.start()
            copies.append(cp)
        for cp in copies:
            cp.wait_recv()
        for cp in copies:
            cp.wait_send()

    return pl.pallas_call(
        body, out_shape=jax.ShapeDtypeStruct((3, R, C), part.dtype), in_specs=[ANY], out_specs=ANY,
        scratch_shapes=[pltpu.SemaphoreType.DMA((3,)), pltpu.SemaphoreType.DMA((3,))], name='rs_chips_' + tag)(part)


def _adam(wv, gv, mv, vv):
    m = ADAM_B1 * mv + (1.0 - ADAM_B1) * gv
    v = ADAM_B2 * vv + (1.0 - ADAM_B2) * (gv * gv)
    m_hat = m / (1.0 - ADAM_B1 ** ADAM_STEP)
    v_hat = v / (1.0 - ADAM_B2 ** ADAM_STEP)
    delta = -ADAM_LR * (m_hat / (jnp.sqrt(v_hat) + ADAM_EPS) + ADAM_WD * wv)
    return delta, m, v


def _sum4(p_ref, l_ref):
    return ((p_ref[0].astype(F32) + l_ref[0].astype(F32)) + l_ref[1].astype(F32)) + l_ref[2].astype(F32)


def rs_rep_sum(part, land, chip):
    def body(c_ref, p_ref, l_ref, o_ref):
        o_ref[...] = _sum4(p_ref, l_ref)

    return pl.pallas_call(
        body, out_shape=jax.ShapeDtypeStruct((REP_SLOT, LANES), F32),
        grid_spec=pltpu.PrefetchScalarGridSpec(
            num_scalar_prefetch=1, grid=(1,),
            in_specs=[pl.BlockSpec((1, REP_SLOT, LANES), lambda i, c: (c[0], 0, 0)),
                      pl.BlockSpec((3, REP_SLOT, LANES), lambda i, c: (0, 0, 0))],
            out_specs=pl.BlockSpec((REP_SLOT, LANES), lambda i, c: (0, 0))),
        compiler_params=pltpu.CompilerParams(dimension_semantics=("parallel",)), name='rs_rep')(chip, part, land)


def adam_param(name, shape, off, w, m, v, chip, part=None, land=None, grep=None):
    r, c = shape
    rp, nt, rb = _tiles(shape)
    rbw = min(r, rb)
    n_src = 2 if grep is None else 1

    def body(c_ref, *refs):
        srcs = refs[:n_src * nt]
        w_ref, m_ref, v_ref, g_ref, d_ref, nm_ref, nv_ref = refs[n_src * nt:]
        if grep is None:
            tiles = [_sum4(srcs[2 * t], srcs[2 * t + 1]) for t in range(nt)]
        else:
            tiles = [srcs[t][...] for t in range(nt)]
        g = (tiles[0] if nt == 1 else jnp.concatenate(tiles, axis=1))[:rbw, :c]
        g_ref[...] = g
        d_ref[...], nm_ref[...], nv_ref[...] = _adam(w_ref[...], g, m_ref[...], v_ref[...])

    in_specs, args = [], []
    for t in range(nt):
        b0 = (off + t * rp) // rb
        assert (off + t * rp) % rb == 0
        if grep is None:
            in_specs += [pl.BlockSpec((1, rb, LANES), functools.partial(lambda i, cr, b0: (cr[0], b0 + i, 0), b0=b0)),
                         pl.BlockSpec((3, rb, LANES), functools.partial(lambda i, cr, b0: (0, b0 + i, 0), b0=b0))]
            args += [part, land]
        else:
            in_specs.append(pl.BlockSpec((rb, LANES), functools.partial(lambda i, cr, b0: (b0 + i, 0), b0=b0)))
            args.append(grep)
    nat = pl.BlockSpec((rbw, c), lambda i, cr: (i, 0))
    return pl.pallas_call(
        body, out_shape=[jax.ShapeDtypeStruct((r, c), F32)] * 4,
        grid_spec=pltpu.PrefetchScalarGridSpec(num_scalar_prefetch=1, grid=(rp // rb,), in_specs=in_specs + [nat] * 3,
                                               out_specs=[nat] * 4),
        compiler_params=pltpu.CompilerParams(dimension_semantics=("parallel",)), name='adam_' + name)(
            chip, *args, w, m, v)


VM = pl.BlockSpec(memory_space=pltpu.VMEM)


def _tile_value(w, t, rp):
    r, c = w.shape
    wt = min(LANES, c - t * LANES)
    tile = w[:, t * LANES:t * LANES + wt]
    if wt < LANES:
        tile = jnp.concatenate([tile, jnp.zeros((r, LANES - wt), tile.dtype)], axis=1)
    if rp > r:
        tile = jnp.concatenate([tile, jnp.zeros((rp - r, LANES), tile.dtype)], axis=0)
    return tile


def pack_layer(layer, blocks):
    names = LAYER_PARAMS[layer]

    def body(*refs):
        tiles = []
        for ref, n in zip(refs[:-1], names):
            rp, nt, _ = _tiles(_block_shape(n))
            w = ref[...]
            tiles += [_tile_value(w, t, rp) for t in range(nt)]
        refs[-1][...] = jnp.concatenate(tiles, axis=0).astype(BF16)

    return pl.pallas_call(body, out_shape=jax.ShapeDtypeStruct((LAYER_ROWS[layer], LANES), BF16),
                          in_specs=[VM] * len(names), out_specs=VM, name='pack_' + layer)(*[blocks[n] for n in names])


def assemble(name, gathered):
    (rf, cf), ax = SHARDED[name]
    r, c = _block_shape(name)
    rp, nt, _ = _tiles((r, c))
    off = SH_OFF[name]
    out_cols = cf if ax == 0 else len(perm_index(name))

    def body(g_ref, o_ref, buf, sem):
        cp = pltpu.make_async_copy(g_ref.at[:, pl.ds(off, nt * rp), :], buf, sem)
        cp.start()
        cp.wait()
        if ax == 0:
            for j in range(N_DEV):
                o_ref[j * r:(j + 1) * r, :] = jnp.concatenate([buf[j, t * rp:(t + 1) * rp, :] for t in range(nt)], axis=1)
            return
        pieces = []
        for p in PERM[name]:
            if p[0] == 'z':
                pieces.append(jnp.zeros((r, p[1]), BF16))
                continue
            n0, w = p
            while w > 0:
                j, cb = divmod(n0, c)
                t, lane = divmod(cb, LANES)
                wl = min(w, LANES - lane, c - cb)
                pieces.append(buf[j, t * rp:t * rp + r, lane:lane + wl])
                n0, w = n0 + wl, w - wl
        o_ref[...] = jnp.concatenate(pieces, axis=1)

    return pl.pallas_call(
        body, out_shape=jax.ShapeDtypeStruct((rf, out_cols), BF16), in_specs=[ANY], out_specs=VM,
        scratch_shapes=[pltpu.VMEM((N_DEV, nt * rp, LANES), BF16), pltpu.SemaphoreType.DMA(())], name='asm_' + name)(
            gathered)


def chunk_grad(layer, name, dw, gfull):
    (rf, cf), ax = SHARDED[name]
    r, c = _block_shape(name)
    rp, nt, _ = _tiles((r, c))
    off = SH_OFF[name]
    if ax == 1:
        idx = perm_index(name) if name in PERM else np.arange(cf)
        inv = np.full(cf, -1)
        inv[idx[idx >= 0]] = np.nonzero(idx >= 0)[0]

    def body(*refs):
        dw_ref, o_ref, buf, sem = refs[0], refs[-3], refs[-2], refs[-1]
        for j in range(N_DEV):
            for t in range(nt):
                if ax == 0:
                    tile = dw_ref[j * r:(j + 1) * r, t * LANES:(t + 1) * LANES]
                else:
                    cols = inv[j * c + t * LANES:j * c + min((t + 1) * LANES, c)]
                    cuts = [0] + [k for k in range(1, len(cols)) if cols[k] != cols[k - 1] + 1] + [len(cols)]
                    pieces = [dw_ref[:, int(cols[a]):int(cols[b - 1]) + 1] for a, b in zip(cuts[:-1], cuts[1:])]
                    if len(cols) < LANES:
                        pieces.append(jnp.zeros((r, LANES - len(cols)), F32))
                    tile = pieces[0] if len(pieces) == 1 else jnp.concatenate(pieces, axis=1)
                    if rp > r:
                        tile = jnp.concatenate([tile, jnp.zeros((rp - r, LANES), F32)], axis=0)
                buf[j, t * rp:(t + 1) * rp, :] = tile
        cp = pltpu.make_async_copy(buf, o_ref.at[:, pl.ds(off, nt * rp), :], sem)
        cp.start()
        cp.wait()

    shape = jax.ShapeDtypeStruct((N_DEV, LAYER_ROWS[layer], LANES), F32)
    scratch = [pltpu.VMEM((N_DEV, nt * rp, LANES), F32), pltpu.SemaphoreType.DMA(())]
    if gfull is None:
        return pl.pallas_call(body, out_shape=shape, in_specs=[VM], out_specs=ANY, scratch_shapes=scratch,
                              name='chunk_' + name)(dw)
    return pl.pallas_call(body, out_shape=shape, in_specs=[VM, ANY], out_specs=ANY, scratch_shapes=scratch,
                          input_output_aliases={1: 0}, name='chunk_' + name)(dw, gfull)


def pack_rep(G):
    def body(*refs):
        tiles = []
        for ref, s in zip(refs[:-1], REP_SHAPE.values()):
            rp, nt, _ = _tiles(s)
            g = ref[...]
            tiles += [_tile_value(g, t, rp) for t in range(nt)]
        full = jnp.concatenate(tiles, axis=0)
        for j in range(N_DEV):
            refs[-1][j, 0:REP_CHUNK, :] = full[j * REP_CHUNK:(j + 1) * REP_CHUNK]
            if REP_SLOT > REP_CHUNK:
                refs[-1][j, REP_CHUNK:REP_SLOT, :] = jnp.zeros((REP_SLOT - REP_CHUNK, LANES), F32)

    return pl.pallas_call(body, out_shape=jax.ShapeDtypeStruct((N_DEV, REP_SLOT, LANES), F32),
                          in_specs=[VM] * len(REP_SHAPE), out_specs=VM, name='pack_rep')(
                              *[G[n].reshape(s) for n, s in REP_SHAPE.items()])


def _pack_small(blocks, order, rows, width, dtype):
    flat = jnp.concatenate([blocks[n].reshape(-1).astype(dtype) for n in order])
    return jnp.pad(flat, (0, rows * width - flat.shape[0])).reshape(rows, width)


def kernel(x, pre_norm, post_norm, rel_bias, a_w_in, a_lam_re, a_lam_im, a_log_dt, a_b_re, a_b_im, a_c_re, a_c_im, a_d, a_w_glu, a_b_glu, a_w_out, b_w_in, b_sinks, b_w_out, c_w_in, c_q_norm, c_kv_norm, c_w_uq, c_w_ukv, c_w_out, d_w_in, d_ln_g, d_ln_b, d_w_s, d_b_s, d_w_out, loss_target, m_pre_norm, m_post_norm, m_rel_bias, m_a_w_in, m_a_lam_re, m_a_lam_im, m_a_log_dt, m_a_b_re, m_a_b_im, m_a_c_re, m_a_c_im, m_a_d, m_a_w_glu, m_a_b_glu, m_a_w_out, m_b_w_in, m_b_sinks, m_b_w_out, m_c_w_in, m_c_q_norm, m_c_kv_norm, m_c_w_uq, m_c_w_ukv, m_c_w_out, m_d_w_in, m_d_ln_g, m_d_ln_b, m_d_w_s, m_d_b_s, m_d_w_out, v_pre_norm, v_post_norm, v_rel_bias, v_a_w_in, v_a_lam_re, v_a_lam_im, v_a_log_dt, v_a_b_re, v_a_b_im, v_a_c_re, v_a_c_im, v_a_d, v_a_w_glu, v_a_b_glu, v_a_w_out, v_b_w_in, v_b_sinks, v_b_w_out, v_c_w_in, v_c_q_norm, v_c_kv_norm, v_c_w_uq, v_c_w_ukv, v_c_w_out, v_d_w_in, v_d_ln_g, v_d_ln_b, v_d_w_s, v_d_b_s, v_d_w_out):
    loc = locals()
    P = {n: loc[n] for n in WEIGHTS}
    M = {n: loc['m_' + n] for n in WEIGHTS}
    V = {n: loc['v_' + n] for n in WEIGHTS}
    xs = x[0]
    L = xs.shape[0]

    blocks = {n: P[n].reshape(_block_shape(n)) for n in SHARDED}
    W = {}
    for layer, names in LAYER_PARAMS.items():
        gathered = all_gather(pack_layer(layer, blocks), 'ag_' + layer)
        for n in names:
            if n not in SHARDED_F32:
                W[n] = assemble(n, gathered)
    small = all_gather(_pack_small(blocks, SHARDED_F32, SMALL_ROWS, 128, F32), 'ag_small')
    Pl = dict(P)
    for n in SHARDED_F32:
        c = SHARDED[n][0][1]
        bc = c // N_DEV
        Pl[n] = small.reshape(N_DEV, -1)[:, SMALL_OFF[n]:SMALL_OFF[n] + bc].reshape(1, c)
    cx, cy, cc = _coords()
    core = jnp.reshape(cc, (1,)).astype(jnp.int32)
    chip = jnp.reshape(2 * cx + cy, (1,)).astype(jnp.int32)

    def reduce_scatter(gfull, tag):
        part = rs_pair_add(gfull, rs_sibling(gfull, tag), core, tag)
        return part, rs_chips(part, tag)

    fwd = [layer_a_fwd, layer_b_fwd, layer_c_fwd, layer_d_fwd]
    bwd = [layer_a_bwd, layer_b_bwd, layer_c_bwd, layer_d_bwd]
    saved = []
    xc = xs
    for i in range(4):
        def fpre(x_, g_):
            return [rms_fwd(x_, g_)], []
        (h,), _ = rowwise(fpre, [rw(xc)], [P['pre_norm'][i:i + 1]], [(D_MODEL, F32)], [], 256, f'pre_norm{i}')
        yb, sv = fwd[i](h, W, Pl)

        def fpost(x_, y_, g_):
            return [x_ + rms_fwd(y_, g_)], []
        (xn,), _ = rowwise(fpost, [rw(xc), rw(yb)], [P['post_norm'][i:i + 1]], [(D_MODEL, F32)], [], 256, f'post_norm{i}')
        sv['x'], sv['yb'] = xc, yb
        saved.append(sv)
        xc = xn

    def floss(y_, t_):
        d = y_ - t_
        return [d * (1.0 / D_MODEL)], [0.5 * jnp.sum(jnp.sum(d * d, axis=-1, keepdims=True) * (1.0 / D_MODEL), axis=0,
                                                      keepdims=True)]
    (dx,), (loss_loc,) = rowwise(floss, [rw(xc), rw(loss_target[0])], [], [(D_MODEL, F32)], [(1, 1)], 256, 'loss')
    loss = lax.psum(loss_loc[0, 0], ("x", "y", "c"))

    G, out = {}, {}
    dpre, dpost = [None] * 4, [None] * 4
    for i in reversed(range(4)):
        sv = saved[i]

        def fpost_b(y_, d_, g_):
            dy, dg = rms_bwd(y_, g_, d_)
            return [dy], [dg]
        (dyb,), (dpost[i],) = rowwise(fpost_b, [rw(sv['yb']), rw(dx)], [P['post_norm'][i:i + 1]], [(D_MODEL, F32)],
                                      [(1, D_MODEL)], 256, f'post_norm_bwd{i}')
        dh, g = bwd[i](dyb, W, Pl, sv)
        G.update(g)

        def fpre_b(x_, dh_, d_, g_):
            dxl, dg = rms_bwd(x_, g_, dh_)
            return [d_ + dxl], [dg]
        (dx,), (dpre[i],) = rowwise(fpre_b, [rw(sv['x']), rw(dh), rw(dx)], [P['pre_norm'][i:i + 1]], [(D_MODEL, F32)],
                                    [(1, D_MODEL)], 256, f'pre_norm_bwd{i}')

        layer = 'abcd'[i]
        gfull = None
        for n in LAYER_PARAMS[layer]:
            gfull = chunk_grad(layer, n, G[n], gfull)
        part, land2 = reduce_scatter(gfull, layer)
        for n in LAYER_PARAMS[layer]:
            s = _block_shape(n)
            out[n] = adam_param(n, s, SH_OFF[n], blocks[n], M[n].reshape(s), V[n].reshape(s), chip, part=part,
                                land=land2)
    G['pre_norm'] = jnp.concatenate(dpre, axis=0)
    G['post_norm'] = jnp.concatenate(dpost, axis=0)

    part, land2 = reduce_scatter(pack_rep(G), 'rep')
    grep = all_gather(rs_rep_sum(part, land2, chip), 'ag_rep')[:, :REP_CHUNK].reshape(REP_ROWS, LANES)
    for n, s in REP_SHAPE.items():
        out[n] = adam_param(n, s, REP_OFF[n], P[n].reshape(s), M[n].reshape(s), V[n].reshape(s), chip, grep=grep)
    res = [loss, dx[None]]
    for kind in range(4):
        res += [out[n][kind].reshape(P[n].shape) for n in WEIGHTS]
    return tuple(res)
```

```python
import functools
import math

import numpy as np
import jax
import jax.numpy as jnp
from jax import lax
from jax.experimental import pallas as pl
from jax.experimental.pallas import tpu as pltpu

F32 = jnp.float32
BF16 = jnp.bfloat16
MESH = pl.DeviceIdType.MESH
ANY = pl.BlockSpec(memory_space=pl.ANY)

N_DEV = 8
D_MODEL = 1024
EPS = 1e-6
NEG_INF = -1e30
SSM_G, SSM_P, SSM_H = 64, 64, 16
SSM_T = 256
SSM_WC = 512
HEAD_DIM = 64
SWA_HEADS, SWA_KV = 16, 2
WINDOW = 128
REL_BUCKETS, REL_MAX_DIST = 32, 128
MLA_HEADS, MLA_NOPE, MLA_ROPE, MLA_V = 16, 64, 32, 64
MLA_Q_RANK, MLA_KV_RANK = 768, 256
ROPE_BASE = 10000.0
SGU_G, SGU_C, SGU_T = 16, 64, 128
ADAM_LR, ADAM_B1, ADAM_B2, ADAM_EPS, ADAM_WD, ADAM_STEP = 0.001, 0.9, 0.999, 1e-08, 0.01, 10

WEIGHTS = ['pre_norm', 'post_norm', 'rel_bias', 'a_w_in', 'a_lam_re', 'a_lam_im', 'a_log_dt', 'a_b_re', 'a_b_im',
           'a_c_re', 'a_c_im', 'a_d', 'a_w_glu', 'a_b_glu', 'a_w_out', 'b_w_in', 'b_sinks', 'b_w_out', 'c_w_in',
           'c_q_norm', 'c_kv_norm', 'c_w_uq', 'c_w_ukv', 'c_w_out', 'd_w_in', 'd_ln_g', 'd_ln_b', 'd_w_s', 'd_b_s',
           'd_w_out']
SHARDED = {'a_w_in': ((1024, 2048), 1), 'a_w_glu': ((1024, 1024), 0), 'a_w_out': ((1024, 1024), 0),
           'b_w_in': ((1024, 2304), 1), 'b_w_out': ((1024, 1024), 0), 'c_w_in': ((1024, 2080), 1),
           'c_q_norm': ((1, 768), 1), 'c_kv_norm': ((1, 256), 1), 'c_w_uq': ((768, 1536), 1),
           'c_w_ukv': ((256, 2048), 1), 'c_w_out': ((1024, 1024), 0), 'd_w_in': ((1024, 3072), 1),
           'd_ln_g': ((1, 1024), 1), 'd_ln_b': ((1, 1024), 1), 'd_w_out': ((1024, 1024), 0)}
SHARDED_F32 = ['c_q_norm', 'c_kv_norm', 'd_ln_g', 'd_ln_b']
REPLICATED = [n for n in WEIGHTS if n not in SHARDED]


def _cdiv(a, b):
    return -(-a // b)


def _block_shape(name):
    (r, c), ax = SHARDED[name]
    return (r // N_DEV, c) if ax == 0 else (r, c // N_DEV)


LANES = 128
LAYER_PARAMS = {'a': ['a_w_in', 'a_w_glu', 'a_w_out'], 'b': ['b_w_in', 'b_w_out'],
                'c': ['c_w_in', 'c_w_uq', 'c_w_ukv', 'c_w_out', 'c_q_norm', 'c_kv_norm'],
                'd': ['d_w_in', 'd_w_out', 'd_ln_g', 'd_ln_b']}


def _tiles(shape):
    r, c = shape
    rp = max(r, 8)
    rb = 512 if rp % 512 == 0 else 256 if rp % 256 == 0 else rp
    return rp, _cdiv(c, LANES), rb


SH_OFF, LAYER_ROWS = {}, {}
for _l, _names in LAYER_PARAMS.items():
    _o = 0
    for _n in _names:
        _rp, _nt, _rb = _tiles(_block_shape(_n))
        assert _o % _rb == 0
        SH_OFF[_n] = _o
        _o += _rp * _nt
    assert _o % 16 == 0
    LAYER_ROWS[_l] = _o

REP_SHAPE = {'a_b_re': (4096, 16), 'a_b_im': (4096, 16), 'd_w_s': (2048, 128), 'a_c_re': (1024, 64),
             'a_c_im': (1024, 64), 'pre_norm': (4, 1024), 'post_norm': (4, 1024), 'a_lam_re': (64, 64),
             'a_lam_im': (64, 64), 'a_d': (1, 1024), 'a_b_glu': (1, 1024), 'rel_bias': (32, 16), 'd_b_s': (16, 128),
             'a_log_dt': (1, 64), 'b_sinks': (1, 16)}
REP_OFF = {}
_o = 0
for _n, _s in REP_SHAPE.items():
    _rp, _nt, _rb = _tiles(_s)
    assert _o % _rb == 0
    REP_OFF[_n] = _o
    _o += _rp * _nt
REP_ROWS = _o
REP_CHUNK = REP_ROWS // N_DEV
assert REP_ROWS % (8 * N_DEV) == 0
REP_SLOT = _cdiv(REP_CHUNK, 16) * 16

PERM = {'a_w_in': [(0, 2048)], 'd_w_in': [(0, 3072)], 'b_w_in': [(1280, 1024), (0, 1280)],
        'c_w_in': [(1056, 1024), (0, 1056), ('z', 96)],
        'c_w_uq': sum([[(2 * hp * 96, 64), ((2 * hp + 1) * 96, 64), (2 * hp * 96 + 64, 32), ((2 * hp + 1) * 96 + 64, 32),
                        ('z', 64)] for hp in range(8)], []),
        'c_w_ukv': sum([[(2 * hp * 128, 64), ((2 * hp + 1) * 128, 64), (2 * hp * 128 + 64, 64),
                         ((2 * hp + 1) * 128 + 64, 64)] for hp in range(8)], [])}


def perm_index(name):
    return np.concatenate([np.full(p[1], -1) if p[0] == 'z' else np.arange(p[0], p[0] + p[1]) for p in PERM[name]])


SMALL_OFF = {}
_o = 0
for _n in SHARDED_F32:
    SMALL_OFF[_n] = _o
    _o += int(np.prod(_block_shape(_n)))
SMALL_ROWS = _cdiv(_o, 128 * 8) * 8


def _pick(n, cands):
    for c in cands:
        if n % c == 0:
            return c
    return n


def mm(a, b, mode, name, out_dtype=F32):
    if mode == 'nn':
        (M, K), (K2, N) = a.shape, b.shape
    elif mode == 'nt':
        (M, K), (N, K2) = a.shape, b.shape
    else:
        (K, M), (K2, N) = a.shape, b.shape
    assert K == K2, (name, a.shape, b.shape)
    tm = _pick(M, (512, 256, 128))
    tn = _pick(N, (512, 384, 256))
    dims = {'nn': ((1,), (0,)), 'nt': ((1,), (1,)), 'tn': ((0,), (0,))}[mode]

    def body(a_ref, b_ref, o_ref):
        o_ref[...] = lax.dot_general(a_ref[...].astype(BF16), b_ref[...].astype(BF16), (dims, ((), ())),
                                     preferred_element_type=F32).astype(out_dtype)

    a_spec = pl.BlockSpec((K, tm), lambda i, j: (0, i)) if mode == 'tn' else pl.BlockSpec((tm, K), lambda i, j: (i, 0))
    b_spec = pl.BlockSpec((tn, K), lambda i, j: (j, 0)) if mode == 'nt' else pl.BlockSpec((K, tn), lambda i, j: (0, j))
    return pl.pallas_call(
        body, grid=(M // tm, N // tn), in_specs=[a_spec, b_spec],
        out_specs=pl.BlockSpec((tm, tn), lambda i, j: (i, j)), out_shape=jax.ShapeDtypeStruct((M, N), out_dtype),
        compiler_params=pltpu.CompilerParams(dimension_semantics=("parallel", "parallel")), name=name)(a, b)


def rw(arr, width=None, cb=0):
    return (arr, arr.shape[1] if width is None else width, cb)


def rowwise(fn, rows, consts, outs, accs, tl, name, n_steps=None):
    if n_steps is None:
        n_steps = [r[0].shape[0] for r in rows if not isinstance(r[1], pl.BlockSpec)][0] // tl
    L = n_steps * tl
    nr, nc, no, na = len(rows), len(consts), len(outs), len(accs)
    in_specs, args = [], []
    for r in rows:
        if isinstance(r[1], pl.BlockSpec):
            in_specs.append(r[1])
        else:
            in_specs.append(pl.BlockSpec((tl, r[1]), functools.partial(lambda i, cb: (i, cb), cb=r[2])))
        args.append(r[0])
    for c in consts:
        in_specs.append(pl.BlockSpec(c.shape, functools.partial(lambda i, nd: (0,) * nd, nd=c.ndim)))
        args.append(c)
    out_specs = [pl.BlockSpec((tl, w), lambda i: (i, 0)) for w, _ in outs]
    out_shape = [jax.ShapeDtypeStruct((L, w), dt) for w, dt in outs]
    for s in accs:
        out_specs.append(pl.BlockSpec(s, functools.partial(lambda i, nd: (0,) * nd, nd=len(s))))
        out_shape.append(jax.ShapeDtypeStruct(s, F32))

    def body(*refs):
        ins = [r[...] for r in refs[:nr + nc]]
        o_refs = refs[nr + nc:nr + nc + no]
        a_refs = refs[nr + nc + no:]
        o_vals, a_vals = fn(*ins)
        for ref, val in zip(o_refs, o_vals):
            ref[...] = val.astype(ref.dtype)
        if na:
            @pl.when(pl.program_id(0) == 0)
            def _():
                for ref in a_refs:
                    ref[...] = jnp.zeros_like(ref)
            for ref, val in zip(a_refs, a_vals):
                ref[...] += val

    res = pl.pallas_call(
        body, grid=(n_steps,), in_specs=in_specs, out_specs=out_specs, out_shape=out_shape,
        compiler_params=pltpu.CompilerParams(dimension_semantics=("arbitrary",)), name=name)(*args)
    return res[:no], res[no:]


def carried(body, comm, *, grid, in_specs, out_specs, out_shape, name, semantics, scratch_shapes=()):
    single = not isinstance(out_shape, (list, tuple))
    o_specs = [out_specs] if single else list(out_specs)
    o_shape = [out_shape] if single else list(out_shape)
    if comm is None:
        call = pl.pallas_call(body, grid=grid, in_specs=in_specs, out_specs=out_specs, out_shape=out_shape,
                              scratch_shapes=list(scratch_shapes),
                              compiler_params=pltpu.CompilerParams(dimension_semantics=semantics), name=name)
        return lambda *args: (call(*args), None)
    n_in, n_out, n_sc = len(in_specs), len(o_specs), len(scratch_shapes)
    ci, co = len(comm.ins), len(comm.outs)
    n_steps = int(np.prod(grid))
    hooks = comm.hooks(n_steps)

    def wrapped(*refs):
        ins, cins = refs[:n_in], refs[n_in:n_in + ci]
        outs, couts = refs[n_in + ci:n_in + ci + n_out], refs[n_in + ci + n_out:n_in + ci + n_out + co]
        sc, csc = refs[n_in + ci + n_out + co:n_in + ci + n_out + co + n_sc], refs[n_in + ci + n_out + co + n_sc:]
        step = pl.program_id(0)
        for ax in range(1, len(grid)):
            step = step * grid[ax] + pl.program_id(ax)
        for at, fn, after in hooks:
            if not after:
                pl.when(step == at)(functools.partial(fn, cins, couts, csc))
        body(*ins, *outs, *sc)
        for at, fn, after in hooks:
            if after:
                pl.when(step == at)(functools.partial(fn, cins, couts, csc))

    call = pl.pallas_call(wrapped, grid=grid, in_specs=list(in_specs) + [ANY] * ci, out_specs=o_specs + [ANY] * co,
                          out_shape=o_shape + list(comm.outs), scratch_shapes=list(scratch_shapes) + list(comm.scratch),
                          compiler_params=pltpu.CompilerParams(dimension_semantics=("arbitrary",) * len(grid)), name=name)

    def run(*args):
        res = call(*args, *comm.ins)
        return (res[0] if single else res[:n_out]), res[n_out:]
    return run


_K0 = math.sqrt(2.0 / math.pi)
_K1 = 0.044715


def gelu(x):
    return x * (0.5 * (1.0 + jnp.tanh(_K0 * (x + _K1 * (x * x * x)))))


def gelu_grad(x):
    t = jnp.tanh(_K0 * (x + _K1 * (x * x * x)))
    return 0.5 * (1.0 + t) + 0.5 * x * (1.0 - t * t) * (_K0 * (1.0 + 3.0 * _K1 * x * x))


def sigmoid(x):
    return 1.0 / (1.0 + jnp.exp(-x))


def silu(z):
    return z * sigmoid(z)


def silu_grad(z):
    s = sigmoid(z)
    return s * (1.0 + z * (1.0 - s))


def rms_fwd(x, g):
    r = lax.rsqrt(jnp.mean(x * x, axis=-1, keepdims=True) + EPS)
    return x * r * g


def rms_bwd(x, g, dy):
    r = lax.rsqrt(jnp.mean(x * x, axis=-1, keepdims=True) + EPS)
    xh = x * r
    dg = jnp.sum(dy * xh, axis=0, keepdims=True)
    dxh = dy * g
    dx = r * (dxh - xh * jnp.mean(dxh * xh, axis=-1, keepdims=True))
    return dx, dg


def _scan_chunk(a_r, a_i, pr_ref, pi_ref, cr, ci, T, reverse):
    row = lax.broadcasted_iota(jnp.int32, a_r.shape, 0)
    sgn = -1.0 if reverse else 1.0
    d = 1
    while d < T:
        k = (T - d) if reverse else (d - 1)
        wr = pr_ref[k:k + 1, :]
        wi = sgn * pi_ref[k:k + 1, :]
        if reverse:
            yr, yi, keep = pltpu.roll(a_r, T - d, 0), pltpu.roll(a_i, T - d, 0), row < T - d
        else:
            yr, yi, keep = pltpu.roll(a_r, d, 0), pltpu.roll(a_i, d, 0), row >= d
        a_r, a_i = (a_r + jnp.where(keep, wr * yr - wi * yi, 0.0), a_i + jnp.where(keep, wr * yi + wi * yr, 0.0))
        d *= 2
    wr = pr_ref[...]
    wi = sgn * pi_ref[...]
    c_r, c_i = cr[...], ci[...]
    a_r, a_i = a_r + (wr * c_r - wi * c_i), a_i + (wr * c_i + wi * c_r)
    k = 0 if reverse else T - 1
    cr[...] = a_r[k:k + 1, :]
    ci[...] = a_i[k:k + 1, :]
    return a_r, a_i


_NT = (((1,), (1,)), ((), ()))
_TN = (((0,), (0,)), ((), ()))


def s5_fwd(proj, d_skip, Bre, Bim, Cre, Cim, pr, pi, comm=None):
    L = proj.shape[0]
    T, WC = min(SSM_T, L), SSM_WC
    nT = L // T

    def body(u_ref, d_ref, bre_ref, bim_ref, cre_ref, cim_ref, pr_ref, pi_ref, y_ref, yg_ref, sr_ref, si_ref, cr, ci):
        @pl.when(pl.program_id(1) == 0)
        def _():
            cr[...] = jnp.zeros_like(cr)
            ci[...] = jnp.zeros_like(ci)

        u = u_ref[...]
        ub = u.astype(BF16)
        a_r = jnp.dot(ub, bre_ref[0].astype(BF16), preferred_element_type=F32)
        a_i = jnp.dot(ub, bim_ref[0].astype(BF16), preferred_element_type=F32)
        a_r, a_i = _scan_chunk(a_r, a_i, pr_ref, pi_ref, cr, ci, T, False)
        sr_ref[...] = a_r
        si_ref[...] = a_i
        y = (jnp.dot(a_r.astype(BF16), cre_ref[0].astype(BF16), preferred_element_type=F32)
             + jnp.dot(a_i.astype(BF16), cim_ref[0].astype(BF16), preferred_element_type=F32) + d_ref[...] * u)
        y_ref[...] = y
        yg_ref[...] = gelu(y)

    uspec = pl.BlockSpec((T, 128), lambda k, i: (i, k))
    sspec = pl.BlockSpec((T, WC), lambda k, i: (i, k))
    return carried(
        body, comm, grid=(8, nT),
        in_specs=[uspec, pl.BlockSpec((1, 128), lambda k, i: (0, k)),
                  pl.BlockSpec((1, 128, WC), lambda k, i: (k, 0, 0)), pl.BlockSpec((1, 128, WC), lambda k, i: (k, 0, 0)),
                  pl.BlockSpec((1, WC, 128), lambda k, i: (k, 0, 0)), pl.BlockSpec((1, WC, 128), lambda k, i: (k, 0, 0)),
                  pl.BlockSpec((T, WC), lambda k, i: (0, k)), pl.BlockSpec((T, WC), lambda k, i: (0, k))],
        out_specs=[uspec, uspec, sspec, sspec],
        out_shape=[jax.ShapeDtypeStruct((L, 1024), F32)] * 2 + [jax.ShapeDtypeStruct((L, 8 * WC), F32)] * 2,
        scratch_shapes=[pltpu.VMEM((1, WC), F32), pltpu.VMEM((1, WC), F32)],
        semantics=("parallel", "arbitrary"), name='a_ssm')(proj, d_skip, Bre, Bim, Cre, Cim, pr, pi)


def s5_bwd(proj, dyg1, dyg2, y, d_skip, s_re, s_im, Bre, Bim, Cre, Cim, prr, pir, comm=None):
    L = proj.shape[0]
    T, WC = min(SSM_T, L), SSM_WC
    nT = L // T

    def body(u_ref, g1_ref, g2_ref, y_ref, d_ref, sr_ref, si_ref, spr_ref, spi_ref, bre_ref, bim_ref, cre_ref, cim_ref,
             pr_ref, pi_ref, du_ref, dd_ref, dbre_ref, dbim_ref, dcre_ref, dcim_ref, dar_ref, dai_ref, cr, ci):
        i = pl.program_id(1)

        @pl.when(i == 0)
        def _():
            for ref in (cr, ci, dd_ref, dbre_ref, dbim_ref, dcre_ref, dcim_ref, dar_ref, dai_ref):
                ref[...] = jnp.zeros_like(ref)

        u = u_ref[...]
        dy = (g1_ref[...] + g2_ref[...]) * gelu_grad(y_ref[...])
        dd_ref[...] += jnp.sum(dy * u, axis=0, keepdims=True)
        dyb, ub = dy.astype(BF16), u.astype(BF16)
        bre, bim, cre, cim = (r[0].astype(BF16) for r in (bre_ref, bim_ref, cre_ref, cim_ref))
        g_r = lax.dot_general(dyb, cre, _NT, preferred_element_type=F32)
        g_i = lax.dot_general(dyb, cim, _NT, preferred_element_type=F32)
        g_r, g_i = _scan_chunk(g_r, g_i, pr_ref, pi_ref, cr, ci, T, True)
        s_r, s_i = sr_ref[...], si_ref[...]
        row = lax.broadcasted_iota(jnp.int32, (T, WC), 0)
        first = (nT - 1 - i) == 0
        sp_r = jnp.where(row == 0, jnp.where(first, 0.0, spr_ref[7:8, :]), pltpu.roll(s_r, 1, 0))
        sp_i = jnp.where(row == 0, jnp.where(first, 0.0, spi_ref[7:8, :]), pltpu.roll(s_i, 1, 0))
        dar_ref[...] += jnp.sum(g_r * sp_r + g_i * sp_i, axis=0, keepdims=True)
        dai_ref[...] += jnp.sum(g_i * sp_r - g_r * sp_i, axis=0, keepdims=True)
        grb, gib = g_r.astype(BF16), g_i.astype(BF16)
        dcre_ref[0] += lax.dot_general(s_r.astype(BF16), dyb, _TN, preferred_element_type=F32)
        dcim_ref[0] += lax.dot_general(s_i.astype(BF16), dyb, _TN, preferred_element_type=F32)
        dbre_ref[0] += lax.dot_general(ub, grb, _TN, preferred_element_type=F32)
        dbim_ref[0] += lax.dot_general(ub, gib, _TN, preferred_element_type=F32)
        du_ref[...] = (dy * d_ref[...] + lax.dot_general(grb, bre, _NT, preferred_element_type=F32)
                       + lax.dot_general(gib, bim, _NT, preferred_element_type=F32))

    uspec = pl.BlockSpec((T, 128), lambda k, i: (nT - 1 - i, k))
    sspec = pl.BlockSpec((T, WC), lambda k, i: (nT - 1 - i, k))
    pspec = pl.BlockSpec((8, WC), lambda k, i: (jnp.maximum((nT - 1 - i) * (T // 8) - 1, 0), k))
    tab = pl.BlockSpec((T, WC), lambda k, i: (0, k))
    bspec = pl.BlockSpec((1, 128, WC), lambda k, i: (k, 0, 0))
    cspec = pl.BlockSpec((1, WC, 128), lambda k, i: (k, 0, 0))
    return carried(
        body, comm, grid=(8, nT),
        in_specs=[uspec, uspec, uspec, uspec, pl.BlockSpec((1, 128), lambda k, i: (0, k)), sspec, sspec, pspec, pspec,
                  bspec, bspec, cspec, cspec, tab, tab],
        out_specs=[uspec, pl.BlockSpec((1, 128), lambda k, i: (0, k)), bspec, bspec, cspec, cspec,
                   pl.BlockSpec((1, WC), lambda k, i: (0, k)), pl.BlockSpec((1, WC), lambda k, i: (0, k))],
        out_shape=[jax.ShapeDtypeStruct((L, 1024), F32), jax.ShapeDtypeStruct((1, 1024), F32),
                   jax.ShapeDtypeStruct((8, 128, WC), F32), jax.ShapeDtypeStruct((8, 128, WC), F32),
                   jax.ShapeDtypeStruct((8, WC, 128), F32), jax.ShapeDtypeStruct((8, WC, 128), F32),
                   jax.ShapeDtypeStruct((1, 8 * WC), F32), jax.ShapeDtypeStruct((1, 8 * WC), F32)],
        scratch_shapes=[pltpu.VMEM((1, WC), F32), pltpu.VMEM((1, WC), F32)],
        semantics=("parallel", "arbitrary"), name='a_ssm_bwd')(
            proj, dyg1, dyg2, y, d_skip, s_re, s_im, s_re, s_im, Bre, Bim, Cre, Cim, prr, pir)


def s5_discretize(lam_re, lam_im, log_dt, b_re, b_im):
    dt = jnp.exp(log_dt)[:, None]
    mag = jnp.exp(lam_re * dt)
    ab_re = mag * jnp.cos(lam_im * dt)
    ab_im = mag * jnp.sin(lam_im * dt)
    den = lam_re * lam_re + lam_im * lam_im
    nr = ab_re - 1.0
    f_re = (nr * lam_re + ab_im * lam_im) / den
    f_im = (ab_im * lam_re - nr * lam_im) / den
    bb_re = f_re[..., None] * b_re - f_im[..., None] * b_im
    bb_im = f_re[..., None] * b_im + f_im[..., None] * b_re
    return ab_re, ab_im, bb_re, bb_im


_EYE8 = np.eye(8, dtype=np.float32)


def _b_tiles(bb):
    t = bb.transpose(0, 2, 1).reshape(8, 8, SSM_H, SSM_P)
    return jnp.einsum('kghp,gG->kghGp', t, _EYE8).reshape(8, 8 * SSM_H, 8 * SSM_P)


def _b_untile(d):
    t = jnp.einsum('kghGp,gG->kghp', d.reshape(8, 8, SSM_H, 8, SSM_P), _EYE8)
    return t.reshape(SSM_G, SSM_H, SSM_P).transpose(0, 2, 1)


def _c_tiles(c):
    t = c.transpose(0, 2, 1).reshape(8, 8, SSM_P, SSM_H)
    return jnp.einsum('kgph,gG->kgpGh', t, _EYE8).reshape(8, 8 * SSM_P, 8 * SSM_H)


def _c_untile(d):
    t = jnp.einsum('kgpGh,gG->kgph', d.reshape(8, 8, SSM_P, 8, SSM_H), _EYE8)
    return t.reshape(SSM_G, SSM_P, SSM_H).transpose(0, 2, 1)


def s5_powers(ar, ai, T):
    W = ar.shape[1]

    def body(ar_ref, ai_ref, fr_ref, fi_ref, rr_ref, ri_ref):
        fr_ref[0:1, :] = ar_ref[...]
        fi_ref[0:1, :] = ai_ref[...]
        rr_ref[T - 1:T, :] = ar_ref[...]
        ri_ref[T - 1:T, :] = ai_ref[...]
        n = 1
        while n < T:
            cr, ci = fr_ref[0:n, :], fi_ref[0:n, :]
            lr, li = fr_ref[n - 1:n, :], fi_ref[n - 1:n, :]
            fr_ref[n:2 * n, :] = cr * lr - ci * li
            fi_ref[n:2 * n, :] = cr * li + ci * lr
            cr, ci = rr_ref[T - n:T, :], ri_ref[T - n:T, :]
            rr_ref[T - 2 * n:T - n, :] = cr * lr - ci * li
            ri_ref[T - 2 * n:T - n, :] = cr * li + ci * lr
            n *= 2

    spec = pl.BlockSpec((T, SSM_WC), lambda j: (0, j))
    aspec = pl.BlockSpec((1, SSM_WC), lambda j: (0, j))
    return pl.pallas_call(
        body, grid=(W // SSM_WC,), in_specs=[aspec, aspec], out_specs=[spec] * 4,
        out_shape=[jax.ShapeDtypeStruct((T, W), F32)] * 4,
        compiler_params=pltpu.CompilerParams(dimension_semantics=("parallel",)), name='a_powers')(ar, ai)


def layer_a_fwd(h, w, p, comm=None):
    L = h.shape[0]
    proj = mm(h, w['a_w_in'], 'nn', 'a_proj')
    disc = lambda *a: s5_discretize(*a)
    (ab_re, ab_im, bb_re, bb_im), disc_vjp = jax.vjp(disc, p['a_lam_re'][0], p['a_lam_im'][0], p['a_log_dt'][0],
                                                     p['a_b_re'][0], p['a_b_im'][0])
    Bre, Bim = _b_tiles(bb_re), _b_tiles(bb_im)
    Cre, Cim = _c_tiles(p['a_c_re'][0]), -_c_tiles(p['a_c_im'][0])
    T = min(SSM_T, L)
    pr, pi, prr, pir = s5_powers(ab_re.reshape(1, -1), ab_im.reshape(1, -1), T)
    (y, yg, s_re, s_im), carried_out = s5_fwd(proj, p['a_d'], Bre, Bim, Cre, Cim, pr, pi, comm=comm)
    gl = mm(yg, w['a_w_glu'], 'nn', 'a_glu')

    def f2(yg_, gl_, z, bg):
        return [yg_ * sigmoid(gl_ + bg) * silu(z)], []
    (po,), _ = rowwise(f2, [rw(yg), rw(gl), rw(proj, 1024, 1)], [p['a_b_glu']], [(1024, F32)], [], 256, 'a_gate')
    yb = mm(po, w['a_w_out'], 'nn', 'a_out')
    saved = dict(carried=carried_out, h=h, proj=proj, disc_vjp=disc_vjp, Bre=Bre, Bim=Bim, Cre=Cre, Cim=Cim, prr=prr, pir=pir, s_re=s_re,
                 s_im=s_im, y=y, yg=yg, gl=gl, po=po)
    return yb, saved


def layer_a_bwd(dyb, w, p, sv, comm=None):
    g = {}
    dpo = mm(dyb, w['a_w_out'], 'nt', 'a_dpo')
    g['a_w_out'] = mm(sv['po'], dyb, 'tn', 'a_dwout')
    proj = sv['proj']

    def f1(dpo_, yg, gl, z, bg):
        sg = sigmoid(gl + bg)
        sz = silu(z)
        dm = dpo_ * sz
        dz = dpo_ * (yg * sg) * silu_grad(z)
        dgl = dm * yg * sg * (1.0 - sg)
        return [dz, dm * sg, dgl], [jnp.sum(dgl, axis=0, keepdims=True)]
    (dz, dyg1, dgl), (db_glu,) = rowwise(f1, [rw(dpo), rw(sv['yg']), rw(sv['gl']), rw(proj, 1024, 1)], [p['a_b_glu']],
                                          [(1024, F32)] * 3, [(1, 1024)], 256, 'a_gate_bwd')
    g['a_b_glu'] = db_glu
    g['a_w_glu'] = mm(sv['yg'], dgl, 'tn', 'a_dwglu')
    dyg2 = mm(dgl, w['a_w_glu'], 'nt', 'a_dyg2')

    (du, dd, dBre, dBim, dCre, dCim, da_re, da_im), g['carried'] = s5_bwd(
        proj, dyg1, dyg2, sv['y'], p['a_d'], sv['s_re'], sv['s_im'], sv['Bre'], sv['Bim'], sv['Cre'], sv['Cim'],
        sv['prr'], sv['pir'], comm=comm)
    g['a_d'] = dd
    dCim = -dCim

    def f3(du_, dz_):
        return [jnp.concatenate([du_, dz_], axis=1)], []
    (dproj,), _ = rowwise(f3, [rw(du), rw(dz)], [], [(2048, F32)], [], 256, 'a_dproj')
    dlr, dli, dldt, dbr, dbi = sv['disc_vjp']((da_re.reshape(SSM_G, SSM_P), da_im.reshape(SSM_G, SSM_P),
                                               _b_untile(dBre), _b_untile(dBim)))
    g['a_lam_re'], g['a_lam_im'], g['a_log_dt'] = dlr[None], dli[None], dldt[None]
    g['a_b_re'], g['a_b_im'] = dbr[None], dbi[None]
    g['a_c_re'], g['a_c_im'] = _c_untile(dCre)[None], _c_untile(dCim)[None]
    g['a_w_in'] = mm(sv['h'], dproj, 'tn', 'a_dwin')
    dh = mm(dproj, w['a_w_in'], 'nt', 'a_dh')
    return dh, g


def _t5_bucket_np():
    qi = np.arange(WINDOW)[:, None]
    kj = np.arange(2 * WINDOW)[None, :]
    dist = np.maximum(qi + WINDOW - kj, 0)
    max_exact = REL_BUCKETS // 2
    dist_f = np.maximum(dist, 1).astype(np.float32)
    large = max_exact + (np.log(dist_f / np.float32(max_exact)) / np.float32(math.log(REL_MAX_DIST / max_exact))
                         * np.float32(REL_BUCKETS - max_exact)).astype(np.int32)
    large = np.minimum(large, REL_BUCKETS - 1)
    return np.where(dist < max_exact, dist, large).astype(np.int32)


SWA_GRP = SWA_HEADS // SWA_KV


def _swa_kv(kvp, kvc, kvh):
    kb = jnp.concatenate([kvp[:, kvh * 64:(kvh + 1) * 64], kvc[:, kvh * 64:(kvh + 1) * 64]], 0).astype(BF16)
    vb = jnp.concatenate([kvp[:, 128 + kvh * 64:128 + (kvh + 1) * 64], kvc[:, 128 + kvh * 64:128 + (kvh + 1) * 64]],
                         0).astype(BF16)
    return kb, vb


def _swa_stack(x, kvh):
    return jnp.concatenate([x[:, (kvh * SWA_GRP + g) * 64:(kvh * SWA_GRP + g + 1) * 64] for g in range(SWA_GRP)],
                           axis=0).astype(BF16)


def _swa_group(bias_ref, kvh):
    return bias_ref[kvh * SWA_GRP:(kvh + 1) * SWA_GRP].reshape(SWA_GRP * WINDOW, 2 * WINDOW)


def _swa_sinks(sink_ref, kvh):
    return jnp.concatenate([jnp.broadcast_to(sink_ref[0:1, kvh * SWA_GRP + g:kvh * SWA_GRP + g + 1], (WINDOW, 1))
                            for g in range(SWA_GRP)], axis=0)


def _swa_probs(q, kb, bias_h, sink, valid):
    s = lax.dot_general(q, kb, (((1,), (1,)), ((), ())), preferred_element_type=F32) * (HEAD_DIM ** -0.5)
    s = jnp.where(valid, s + bias_h, NEG_INF)
    m = jnp.maximum(jnp.max(s, axis=-1, keepdims=True), sink)
    e = jnp.exp(s - m)
    es = jnp.exp(sink - m)
    den = jnp.sum(e, axis=-1, keepdims=True) + es
    return e / den, es / den


def _swa_valid(n):
    qi = lax.broadcasted_iota(jnp.int32, (SWA_GRP * WINDOW, 2 * WINDOW), 0) & (WINDOW - 1)
    kj = lax.broadcasted_iota(jnp.int32, (SWA_GRP * WINDOW, 2 * WINDOW), 1)
    dist = qi + WINDOW - kj
    return (dist >= 0) & (dist < WINDOW) & ((kj >= WINDOW) | (n > 0))


def swa_fwd(proj, bias, sinks, comm=None):
    L = proj.shape[0]

    def body(z_ref, q_ref, kvc_ref, kvp_ref, bias_ref, sink_ref, o_ref, po_ref):
        n = pl.program_id(0)
        valid = _swa_valid(n)
        q, kvc, kvp = q_ref[...], kvc_ref[...], kvp_ref[...]
        outs = []
        for kvh in range(SWA_KV):
            kb, vb = _swa_kv(kvp, kvc, kvh)
            p, _ = _swa_probs(_swa_stack(q, kvh), kb, _swa_group(bias_ref, kvh), _swa_sinks(sink_ref, kvh), valid)
            o8 = jnp.dot(p.astype(BF16), vb, preferred_element_type=F32)
            outs += [o8[g * WINDOW:(g + 1) * WINDOW] for g in range(SWA_GRP)]
        o = jnp.concatenate(outs, axis=1)
        o_ref[...] = o
        po_ref[...] = o * silu(z_ref[...])

    return carried(
        body, comm, grid=(L // WINDOW,),
        in_specs=[pl.BlockSpec((WINDOW, 1024), lambda n: (n, 0)), pl.BlockSpec((WINDOW, 1024), lambda n: (n, 1)),
                  pl.BlockSpec((WINDOW, 256), lambda n: (n, 8)),
                  pl.BlockSpec((WINDOW, 256), lambda n: (jnp.maximum(n - 1, 0), 8)),
                  pl.BlockSpec((SWA_HEADS, WINDOW, 2 * WINDOW), lambda n: (0, 0, 0)),
                  pl.BlockSpec((1, SWA_HEADS), lambda n: (0, 0))],
        out_specs=[pl.BlockSpec((WINDOW, 1024), lambda n: (n, 0))] * 2,
        out_shape=[jax.ShapeDtypeStruct((L, 1024), F32)] * 2,
        semantics=("parallel",), name='b_attn')(proj, proj, proj, proj, bias, sinks)


def swa_bwd(proj, do, bias, sinks, comm=None):
    L = proj.shape[0]

    def body(q_ref, kvc_ref, kvp_ref, do_ref, bias_ref, sink_ref, dq_ref, dkv_ref, dbias_ref, dsink_ref):
        n = pl.program_id(0)

        @pl.when(n == 0)
        def _():
            dkv_ref[...] = jnp.zeros_like(dkv_ref)
            dbias_ref[...] = jnp.zeros_like(dbias_ref)
            dsink_ref[...] = jnp.zeros_like(dsink_ref)

        valid = _swa_valid(n)
        q, kvc, kvp, do_ = q_ref[...], kvc_ref[...], kvp_ref[...], do_ref[...]
        dqs, dks, dvs, dsk = [], [], [], []
        for kvh in range(SWA_KV):
            kb, vb = _swa_kv(kvp, kvc, kvh)
            q8, do8 = _swa_stack(q, kvh), _swa_stack(do_, kvh)
            p, ps = _swa_probs(q8, kb, _swa_group(bias_ref, kvh), _swa_sinks(sink_ref, kvh), valid)
            dp = lax.dot_general(do8, vb, (((1,), (1,)), ((), ())), preferred_element_type=F32)
            delta = jnp.sum(p * dp, axis=-1, keepdims=True)
            ds = p * (dp - delta)
            col = -ps * delta
            dsk += [jnp.sum(col[g * WINDOW:(g + 1) * WINDOW], axis=0, keepdims=True) for g in range(SWA_GRP)]
            dbias_ref[kvh * SWA_GRP:(kvh + 1) * SWA_GRP] += ds.reshape(SWA_GRP, WINDOW, 2 * WINDOW)
            dsb = (ds * (HEAD_DIM ** -0.5)).astype(BF16)
            dq8 = jnp.dot(dsb, kb, preferred_element_type=F32)
            dqs += [dq8[g * WINDOW:(g + 1) * WINDOW] for g in range(SWA_GRP)]
            dks.append(lax.dot_general(dsb, q8, (((0,), (0,)), ((), ())), preferred_element_type=F32))
            dvs.append(lax.dot_general(p.astype(BF16), do8, (((0,), (0,)), ((), ())), preferred_element_type=F32))
        dq_ref[...] = jnp.concatenate(dqs, axis=1)
        dsink_ref[...] += jnp.concatenate(dsk, axis=1)
        both = jnp.concatenate(dks + dvs, axis=1)
        r_cur = pl.multiple_of(n * WINDOW, WINDOW)
        r_prev = pl.multiple_of(jnp.maximum(n - 1, 0) * WINDOW, WINDOW)
        dkv_ref[pl.ds(r_prev, WINDOW), :] += both[:WINDOW]
        dkv_ref[pl.ds(r_cur, WINDOW), :] += both[WINDOW:]

    return carried(
        body, comm, grid=(L // WINDOW,),
        in_specs=[pl.BlockSpec((WINDOW, 1024), lambda n: (n, 1)), pl.BlockSpec((WINDOW, 256), lambda n: (n, 8)),
                  pl.BlockSpec((WINDOW, 256), lambda n: (jnp.maximum(n - 1, 0), 8)),
                  pl.BlockSpec((WINDOW, 1024), lambda n: (n, 0)),
                  pl.BlockSpec((SWA_HEADS, WINDOW, 2 * WINDOW), lambda n: (0, 0, 0)),
                  pl.BlockSpec((1, SWA_HEADS), lambda n: (0, 0))],
        out_specs=[pl.BlockSpec((WINDOW, 1024), lambda n: (n, 0)), pl.BlockSpec((L, 256), lambda n: (0, 0)),
                   pl.BlockSpec((SWA_HEADS, WINDOW, 2 * WINDOW), lambda n: (0, 0, 0)),
                   pl.BlockSpec((1, SWA_HEADS), lambda n: (0, 0))],
        out_shape=[jax.ShapeDtypeStruct((L, 1024), F32), jax.ShapeDtypeStruct((L, 256), F32),
                   jax.ShapeDtypeStruct((SWA_HEADS, WINDOW, 2 * WINDOW), F32), jax.ShapeDtypeStruct((1, SWA_HEADS), F32)],
        semantics=("arbitrary",), name='b_attn_bwd')(proj, proj, proj, do, bias, sinks)


def swa_bias(rel_bias):
    def body(bk_ref, rb_ref, o_ref):
        bk = bk_ref[...]
        for h in range(SWA_HEADS):
            acc = jnp.zeros((WINDOW, 2 * WINDOW), F32)
            for b in range(REL_BUCKETS):
                acc = jnp.where(bk == b, rb_ref[b, h], acc)
            o_ref[h] = acc

    return pl.pallas_call(
        body, out_shape=jax.ShapeDtypeStruct((SWA_HEADS, WINDOW, 2 * WINDOW), F32),
        in_specs=[pl.BlockSpec(memory_space=pltpu.VMEM), pl.BlockSpec(memory_space=pltpu.SMEM)],
        out_specs=pl.BlockSpec(memory_space=pltpu.VMEM), name='b_bias')(jnp.asarray(_t5_bucket_np()), rel_bias)


def layer_b_fwd(h, w, p, comm=None):
    proj = mm(h, w['b_w_in'], 'nn', 'b_proj')
    bias = swa_bias(p['rel_bias'])
    (o, po), carried_out = swa_fwd(proj, bias, p['b_sinks'], comm=comm)
    yb = mm(po, w['b_w_out'], 'nn', 'b_out')
    return yb, dict(carried=carried_out, h=h, proj=proj, bias=bias, o=o, po=po)


def layer_b_bwd(dyb, w, p, sv, comm=None):
    g = {}
    dpo = mm(dyb, w['b_w_out'], 'nt', 'b_dpo')
    g['b_w_out'] = mm(sv['po'], dyb, 'tn', 'b_dwout')
    proj = sv['proj']

    def f1(dpo_, o, z):
        return [dpo_ * silu(z), dpo_ * o * silu_grad(z)], []
    (do, dz), _ = rowwise(f1, [rw(dpo), rw(sv['o']), rw(proj, 1024, 0)], [], [(1024, F32)] * 2, [], 256, 'b_gate_bwd')
    (dq, dkv, dbias, dsinks), g['carried'] = swa_bwd(proj, do, sv['bias'], p['b_sinks'], comm=comm)
    g['b_sinks'] = dsinks
    onehot = jnp.asarray(np.eye(REL_BUCKETS, dtype=np.float32)[_t5_bucket_np().reshape(-1)])

    def f2(db, oh):
        return [], [lax.dot_general(db, oh, (((1,), (0,)), ((), ())), preferred_element_type=F32,
                                    precision=lax.Precision.HIGHEST)]
    _, (drel,) = rowwise(f2, [(dbias.reshape(SWA_HEADS, -1), pl.BlockSpec((SWA_HEADS, 4096), lambda i: (0, i))),
                              (onehot, pl.BlockSpec((4096, REL_BUCKETS), lambda i: (i, 0)))], [], [],
                         [(SWA_HEADS, REL_BUCKETS)], 4096, 'b_drel', n_steps=(2 * WINDOW * WINDOW) // 4096)
    g['rel_bias'] = drel.T

    def f3(dz_, dq_, dkv_):
        return [jnp.concatenate([dz_, dq_, dkv_], axis=1)], []
    (dproj,), _ = rowwise(f3, [rw(dz), rw(dq), rw(dkv)], [], [(2304, F32)], [], 256, 'b_dproj')
    g['b_w_in'] = mm(sv['h'], dproj, 'tn', 'b_dwin')
    dh = mm(dproj, w['b_w_in'], 'nt', 'b_dh')
    return dh, g


MLA_SCALE = (MLA_NOPE + MLA_ROPE) ** -0.5


def _rope_tables(L):
    inv = ROPE_BASE ** (-jnp.arange(0, MLA_ROPE, 2, dtype=F32) / MLA_ROPE)
    ang = jnp.arange(L, dtype=F32)[:, None] * inv[None, :]
    c, s = jnp.cos(ang), jnp.sin(ang)
    one, zero, pad = jnp.ones((L, 128), F32), jnp.zeros((L, 128), F32), jnp.zeros((L, 64), F32)
    return (jnp.concatenate([one, c, c, c, c, pad], 1), jnp.concatenate([zero, s, s, s, s, pad], 1))


def _rot(x, transpose=False):
    w = x.shape[1]
    lane = lax.broadcasted_iota(jnp.int32, x.shape, 1)
    up = pltpu.roll(x, w - 16, 1)
    dn = pltpu.roll(x, 16, 1)
    first = (lane % 32) < 16
    return jnp.where(first, up, -dn) if transpose else jnp.where(first, -up, dn)


MLA_QT = 512


def _mla_exp(qf, kf, t, qt):
    n_k = kf.shape[0]
    s = lax.dot_general(qf, kf, (((1,), (1,)), ((), ())), preferred_element_type=F32) * MLA_SCALE
    qpos = t * qt + lax.broadcasted_iota(jnp.int32, (qt, n_k), 0)
    kpos = lax.broadcasted_iota(jnp.int32, (qt, n_k), 1)
    s = jnp.where(kpos <= qpos, s, NEG_INF)
    e = jnp.exp(s - jnp.max(s, axis=-1, keepdims=True))
    return e, jnp.sum(e, axis=-1, keepdims=True)


def _mla_heads(q, kv, kr):
    out = []
    for j in range(2):
        qf = jnp.concatenate([q[:, j * 64:(j + 1) * 64], q[:, 128 + j * 32:128 + (j + 1) * 32]], axis=1)
        kf = jnp.concatenate([kv[:, j * 64:(j + 1) * 64], kr], axis=1)
        out.append((qf, kf, kv[:, 128 + j * 64:128 + (j + 1) * 64]))
    return out


def mla_fwd(q, kv, kr, comm=None):
    L = q.shape[0]
    qt = min(MLA_QT, L)
    nq = L // qt

    def body(q_ref, kv_ref, kr_ref, o_ref):
        for t in range(nq):
            @pl.when(pl.program_id(1) == t)
            def _(t=t):
                n_k = (t + 1) * qt
                outs = []
                for qf, kf, v in _mla_heads(q_ref[...], kv_ref[0:n_k, :], kr_ref[0:n_k, 0:MLA_ROPE]):
                    e, den = _mla_exp(qf, kf, t, qt)
                    outs.append(jnp.dot(e.astype(BF16), v, preferred_element_type=F32) / den)
                o_ref[...] = jnp.concatenate(outs, axis=1)

    return carried(
        body, comm, grid=(MLA_HEADS // 2, nq),
        in_specs=[pl.BlockSpec((qt, 256), lambda hp, n: (n, hp)), pl.BlockSpec((L, 256), lambda hp, n: (0, hp)),
                  pl.BlockSpec((L, 128), lambda hp, n: (0, 0))],
        out_specs=pl.BlockSpec((qt, 128), lambda hp, n: (n, hp)), out_shape=jax.ShapeDtypeStruct((L, 1024), F32),
        semantics=("parallel", "parallel"), name='c_attn')(q, kv, kr)


def mla_bwd(q, kv, kr, do, comm=None):
    L = q.shape[0]
    qt = min(MLA_QT, L)
    nq = L // qt

    def body(q_ref, kv_ref, kr_ref, do_ref, dq_ref, dkv_ref, dkr_ref):
        @pl.when(pl.program_id(1) == 0)
        def _():
            dkv_ref[...] = jnp.zeros_like(dkv_ref)
            dkr_ref[...] = jnp.zeros_like(dkr_ref)

        for t in range(nq):
            @pl.when(pl.program_id(1) == t)
            def _(t=t):
                n_k = (t + 1) * qt
                do_ = do_ref[...]
                dqn, dqr, dkn, dvs = [], [], [], []
                dkr = jnp.zeros((n_k, MLA_ROPE), F32)
                for j, (qf, kf, v) in enumerate(_mla_heads(q_ref[...], kv_ref[0:n_k, :], kr_ref[0:n_k, 0:MLA_ROPE])):
                    doh = do_[:, j * 64:(j + 1) * 64]
                    e, den = _mla_exp(qf, kf, t, qt)
                    p = e * (1.0 / den)
                    dp = lax.dot_general(doh, v, (((1,), (1,)), ((), ())), preferred_element_type=F32)
                    ds = (p * (dp - jnp.sum(p * dp, axis=-1, keepdims=True)) * MLA_SCALE).astype(BF16)
                    dqf = jnp.dot(ds, kf, preferred_element_type=F32)
                    dkf = lax.dot_general(ds, qf, (((0,), (0,)), ((), ())), preferred_element_type=F32)
                    dvs.append(lax.dot_general(p.astype(BF16), doh, (((0,), (0,)), ((), ())), preferred_element_type=F32))
                    dqn.append(dqf[:, :MLA_NOPE])
                    dqr.append(dqf[:, MLA_NOPE:])
                    dkn.append(dkf[:, :MLA_NOPE])
                    dkr = dkr + dkf[:, MLA_NOPE:]
                dq_ref[...] = jnp.concatenate(dqn + dqr + [jnp.zeros((qt, 64), F32)], axis=1)
                dkv_ref[0:n_k, :] += jnp.concatenate(dkn + dvs, axis=1)
                dkr_ref[0, 0:n_k, :] += jnp.concatenate([dkr, jnp.zeros((n_k, 128 - MLA_ROPE), F32)], axis=1)

    return carried(
        body, comm, grid=(MLA_HEADS // 2, nq),
        in_specs=[pl.BlockSpec((qt, 256), lambda hp, n: (n, hp)), pl.BlockSpec((L, 256), lambda hp, n: (0, hp)),
                  pl.BlockSpec((L, 128), lambda hp, n: (0, 0)), pl.BlockSpec((qt, 128), lambda hp, n: (n, hp))],
        out_specs=[pl.BlockSpec((qt, 256), lambda hp, n: (n, hp)), pl.BlockSpec((L, 256), lambda hp, n: (0, hp)),
                   pl.BlockSpec((1, L, 128), lambda hp, n: (hp, 0, 0))],
        out_shape=[jax.ShapeDtypeStruct((L, 2048), F32), jax.ShapeDtypeStruct((L, 2048), F32),
                   jax.ShapeDtypeStruct((MLA_HEADS // 2, L, 128), F32)],
        semantics=("parallel", "arbitrary"), name='c_attn_bwd')(q, kv, kr, do)


def layer_c_fwd(h, w, p, comm=None):
    L = h.shape[0]
    proj = mm(h, w['c_w_in'], 'nn', 'c_proj')

    def f1(c, gq, gk):
        return [rms_fwd(c[:, :768], gq), rms_fwd(c[:, 768:], gk)], []
    (cqn, ckvn), _ = rowwise(f1, [rw(proj, 1024, 1)], [p['c_q_norm'], p['c_kv_norm']], [(768, BF16), (256, BF16)], [],
                             256, 'c_norms')
    qf = mm(cqn, w['c_w_uq'], 'nn', 'c_uq')
    kvf = mm(ckvn, w['c_w_ukv'], 'nn', 'c_ukv', out_dtype=BF16)
    cos, sin = _rope_tables(L)

    def f2(q_, kr_, c, s):
        c8, s8 = jnp.tile(c, (1, 8)), jnp.tile(s, (1, 8))
        return [q_ * c8 + _rot(q_) * s8, kr_ * c[:, 128:] + _rot(kr_) * s[:, 128:]], []
    (q, kr), _ = rowwise(f2, [rw(qf), rw(proj, 128, 16), rw(cos), rw(sin)], [], [(2048, BF16), (128, BF16)], [], 256,
                         'c_rope')
    o, carried_out = mla_fwd(q, kvf, kr, comm=comm)

    def f3(o_, z):
        return [o_ * silu(z)], []
    (po,), _ = rowwise(f3, [rw(o), rw(proj, 1024, 0)], [], [(1024, F32)], [], 256, 'c_gate')
    yb = mm(po, w['c_w_out'], 'nn', 'c_out')
    return yb, dict(carried=carried_out, h=h, proj=proj, cqn=cqn, ckvn=ckvn, q=q, kv=kvf, kr=kr, o=o, po=po, cos=cos, sin=sin)


def layer_c_bwd(dyb, w, p, sv, comm=None):
    g = {}
    dpo = mm(dyb, w['c_w_out'], 'nt', 'c_dpo')
    g['c_w_out'] = mm(sv['po'], dyb, 'tn', 'c_dwout')
    proj = sv['proj']
    L = proj.shape[0]

    def f1(dpo_, o, z):
        return [dpo_ * silu(z), dpo_ * o * silu_grad(z)], []
    (do, dz), _ = rowwise(f1, [rw(dpo), rw(sv['o']), rw(proj, 1024, 0)], [], [(1024, BF16), (1024, F32)], [], 256,
                          'c_gate_bwd')
    (dq, dkvf, dkr8), g['carried'] = mla_bwd(sv['q'], sv['kv'], sv['kr'], do, comm=comm)

    def f2(dq_, dkr_, c, s):
        c8, s8 = jnp.tile(c, (1, 8)), jnp.tile(s, (1, 8))
        dk = jnp.sum(dkr_, axis=0)
        return [dq_ * c8 + _rot(dq_ * s8, True), dk * c[:, 128:] + _rot(dk * s[:, 128:], True)], []
    tl = 256
    (dqf, dkr), _ = rowwise(f2, [rw(dq), (dkr8, pl.BlockSpec((8, tl, 128), lambda i: (0, i, 0))), rw(sv['cos']),
                                 rw(sv['sin'])], [], [(2048, F32), (128, F32)], [], tl, 'c_rope_bwd')
    g['c_w_uq'] = mm(sv['cqn'], dqf, 'tn', 'c_dwuq')
    g['c_w_ukv'] = mm(sv['ckvn'], dkvf, 'tn', 'c_dwukv')
    dcqn = mm(dqf, w['c_w_uq'], 'nt', 'c_dcqn')
    dckvn = mm(dkvf, w['c_w_ukv'], 'nt', 'c_dckvn')

    def f3(c, dq_, dk_, dz_, dkr_, gq, gk):
        dcq, dgq = rms_bwd(c[:, :768], gq, dq_)
        dckv, dgk = rms_bwd(c[:, 768:], gk, dk_)
        return [jnp.concatenate([dz_, dcq, dckv, dkr_], axis=1)], [dgq, dgk]
    (dproj,), (dgq, dgk) = rowwise(f3, [rw(proj, 1024, 1), rw(dcqn), rw(dckvn), rw(dz), rw(dkr)],
                                   [p['c_q_norm'], p['c_kv_norm']], [(2176, F32)], [(1, 768), (1, 256)], 256, 'c_dproj')
    g['c_q_norm'], g['c_kv_norm'] = dgq, dgk
    g['c_w_in'] = mm(sv['h'], dproj, 'tn', 'c_dwin')
    dh = mm(dproj, w['c_w_in'], 'nt', 'c_dh')
    return dh, g


def _sgu_mix(wm, v, transpose):
    outs = []
    dims = (((0,), (0,)), ((), ())) if transpose else (((1,), (0,)), ((), ()))
    for gi in range(SGU_G):
        outs.append(lax.dot_general(wm[gi], v[:, gi * SGU_C:(gi + 1) * SGU_C].astype(BF16), dims,
                                    preferred_element_type=F32))
    return jnp.concatenate(outs, axis=1)


def _sgu_wmask(ws):
    t = lax.broadcasted_iota(jnp.int32, (SGU_T, SGU_T), 0)
    s = lax.broadcasted_iota(jnp.int32, (SGU_T, SGU_T), 1)
    return jnp.where((s <= t)[None], ws, 0.0).astype(BF16)


def _ln_stats(v):
    mu = jnp.mean(v, axis=-1, keepdims=True)
    vc = v - mu
    rstd = lax.rsqrt(jnp.mean(vc * vc, axis=-1, keepdims=True) + EPS)
    return vc * rstd, rstd


def layer_d_fwd(h, w, p):
    proj = mm(h, w['d_w_in'], 'nn', 'd_proj')
    bias = jnp.repeat(p['d_b_s'][0].T, SGU_C, axis=1)

    def f1(u_, v_, z, ws, lg, lb, bs):
        xh, _ = _ln_stats(gelu(v_))
        s = _sgu_mix(_sgu_wmask(ws), xh * lg + lb, False) + bs
        return [gelu(u_) * s * silu(z)], []
    (po,), _ = rowwise(f1, [rw(proj, 1024, 0), rw(proj, 1024, 1), rw(proj, 1024, 2)],
                       [p['d_w_s'][0], p['d_ln_g'], p['d_ln_b'], bias], [(1024, F32)], [], SGU_T, 'd_mix')
    yb = mm(po, w['d_w_out'], 'nn', 'd_out')
    return yb, dict(h=h, proj=proj, po=po, bias=bias)


def layer_d_bwd(dyb, w, p, sv):
    g = {}
    dpo = mm(dyb, w['d_w_out'], 'nt', 'd_dpo')
    g['d_w_out'] = mm(sv['po'], dyb, 'tn', 'd_dwout')
    proj = sv['proj']

    def f1(dpo_, u_, v_, z, ws, lg, lb, bs):
        wm = _sgu_wmask(ws)
        gv = gelu(v_)
        xh, rstd = _ln_stats(gv)
        vn = xh * lg + lb
        s = _sgu_mix(wm, vn, False) + bs
        gu, sz = gelu(u_), silu(z)
        du = dpo_ * s * sz
        ds = dpo_ * gu * sz
        dz = dpo_ * gu * s * silu_grad(z)
        dsb = ds.astype(BF16)
        dws = jnp.stack([lax.dot_general(dsb[:, gi * SGU_C:(gi + 1) * SGU_C], vn[:, gi * SGU_C:(gi + 1) * SGU_C].astype(BF16),
                                         (((1,), (1,)), ((), ())), preferred_element_type=F32) for gi in range(SGU_G)])
        dvn = _sgu_mix(wm, ds, True)
        dlg = jnp.sum(dvn * xh, axis=0, keepdims=True)
        dlb = jnp.sum(dvn, axis=0, keepdims=True)
        dxh = dvn * lg
        dgv = rstd * (dxh - jnp.mean(dxh, axis=-1, keepdims=True) - xh * jnp.mean(dxh * xh, axis=-1, keepdims=True))
        return ([jnp.concatenate([du * gelu_grad(u_), dgv * gelu_grad(v_), dz], axis=1)], [dws, ds, dlg, dlb])
    (dproj,), (dws, dbs, dlg, dlb) = rowwise(
        f1, [rw(dpo), rw(proj, 1024, 0), rw(proj, 1024, 1), rw(proj, 1024, 2)],
        [p['d_w_s'][0], p['d_ln_g'], p['d_ln_b'], sv['bias']], [(3072, F32)],
        [(SGU_G, SGU_T, SGU_T), (SGU_T, 1024), (1, 1024), (1, 1024)], SGU_T, 'd_mix_bwd')
    tril = np.tril(np.ones((SGU_T, SGU_T), dtype=bool))
    g['d_w_s'] = jnp.where(tril[None], dws, 0.0)[None]
    g['d_b_s'] = dbs.reshape(SGU_T, SGU_G, SGU_C).sum(-1).T[None]
    g['d_ln_g'], g['d_ln_b'] = dlg, dlb
    g['d_w_in'] = mm(sv['h'], dproj, 'tn', 'd_dwin')
    dh = mm(dproj, w['d_w_in'], 'nt', 'd_dh')
    return dh, g


def _coords():
    return lax.axis_index("x"), lax.axis_index("y"), lax.axis_index("c")


class AllGather:
    def __init__(self, x):
        self.ins = [x]
        self.outs = [jax.ShapeDtypeStruct((N_DEV,) + x.shape, x.dtype)]
        self.scratch = [pltpu.SemaphoreType.DMA((7,)), pltpu.SemaphoreType.DMA((7,)), pltpu.SemaphoreType.DMA(())]

    def hooks(self, n_steps):
        return [(0, functools.partial(self.phase, 0), False), ((n_steps * 5) // 8, functools.partial(self.phase, 1), False),
                (n_steps - 1, functools.partial(self.phase, 2), True)]

    @staticmethod
    def phase(which, ins, outs, scratch):
        (x_ref,), (out_ref,), (send_sems, recv_sems, local_sem) = ins, outs, scratch
        x_, y_, c_ = _coords()
        me, sibling = (x_, y_, c_), (x_, y_, 1 - c_)
        chips = [(1 - x_, y_), (x_, 1 - y_), (1 - x_, 1 - y_)]

        def slot(px, py, pc):
            return out_ref.at[4 * px + 2 * py + pc]

        def copy(k, block, to, src=None):
            return pltpu.make_async_remote_copy(src_ref=slot(*block) if src is None else src, dst_ref=slot(*block),
                                                send_sem=send_sems.at[k], recv_sem=recv_sems.at[k], device_id=to,
                                                device_id_type=MESH)

        mine = pltpu.make_async_copy(x_ref, slot(*me), local_sem)
        first = [copy(0, me, sibling, src=x_ref)]
        first += [copy(1 + j, me, (*chip, c_), src=x_ref) for j, chip in enumerate(chips)]
        passed = [copy(4 + j, (*chip, c_), sibling) for j, chip in enumerate(chips)]
        if which == 0:
            mine.start()
            for cp in first:
                cp.start()
        elif which == 1:
            for j, chip in enumerate(chips):
                copy(1 + j, (*chip, c_), me).wait_recv()
                passed[j].start()
        else:
            copy(0, sibling, me).wait_recv()
            for j, chip in enumerate(chips):
                copy(4 + j, (*chip, 1 - c_), me).wait_recv()
            for cp in first + passed:
                cp.wait_send()
            mine.wait()


class ChipExchange:
    def __init__(self, part):
        self.ins = [part]
        self.outs = [jax.ShapeDtypeStruct((3,) + part.shape[1:], part.dtype)]
        self.scratch = [pltpu.SemaphoreType.DMA((3,)), pltpu.SemaphoreType.DMA((3,))]

    def hooks(self, n_steps):
        return [(0, functools.partial(self.phase, 0), False), (n_steps - 1, functools.partial(self.phase, 1), True)]

    @staticmethod
    def phase(which, ins, outs, scratch):
        (p_ref,), (land_ref,), (send_sems, recv_sems) = ins, outs, scratch
        x_, y_, c_ = _coords()
        copies = []
        for r, (fx, fy) in enumerate([(1, 0), (0, 1), (1, 1)]):
            tx = jnp.where(fx == 1, 1 - x_, x_)
            ty = jnp.where(fy == 1, 1 - y_, y_)
            copies.append(pltpu.make_async_remote_copy(src_ref=p_ref.at[2 * tx + ty], dst_ref=land_ref.at[r],
                                                       send_sem=send_sems.at[r], recv_sem=recv_sems.at[r],
                                                       device_id=(tx, ty, c_), device_id_type=MESH))
        if which == 0:
            for cp in copies:
                cp.start()
        else:
            for cp in copies:
                cp.wait_recv()
            for cp in copies:
                cp.wait_send()


def run_comm(comm, name):
    def body(*refs):
        ci, co = len(comm.ins), len(comm.outs)
        for _, fn, _ in comm.hooks(1):
            fn(refs[:ci], refs[ci:ci + co], refs[ci + co:])

    return pl.pallas_call(body, out_shape=list(comm.outs), in_specs=[ANY] * len(comm.ins),
                          out_specs=[ANY] * len(comm.outs), scratch_shapes=list(comm.scratch), name=name)(*comm.ins)


def all_gather(x, name):
    return run_comm(AllGather(x), name)[0]


def rs_sibling(gfull, tag):
    _, R, C = gfull.shape

    def body(g_ref, land_ref, send_sems, recv_sems):
        x_, y_, c_ = _coords()
        copies = []
        for k in range(4):
            cp = pltpu.make_async_remote_copy(src_ref=g_ref.at[2 * k + 1 - c_], dst_ref=land_ref.at[k],
                                              send_sem=send_sems.at[k], recv_sem=recv_sems.at[k],
                                              device_id=(x_, y_, 1 - c_), device_id_type=MESH)
            cp.start()
            copies.append(cp)
        for cp in copies:
            cp.wait_recv()
        for cp in copies:
            cp.wait_send()

    return pl.pallas_call(
        body, out_shape=jax.ShapeDtypeStruct((4, R, C), gfull.dtype), in_specs=[ANY], out_specs=ANY,
        scratch_shapes=[pltpu.SemaphoreType.DMA((4,)), pltpu.SemaphoreType.DMA((4,))], name='rs_sibling_' + tag)(gfull)


def rs_pair_add(gfull, land, core, tag):
    _, R, C = gfull.shape
    tl = R

    def body(c_ref, g_ref, l_ref, o_ref):
        o_ref[...] = (g_ref[...] + l_ref[...]).astype(BF16)

    return pl.pallas_call(
        body, out_shape=jax.ShapeDtypeStruct((4, R, C), BF16),
        grid_spec=pltpu.PrefetchScalarGridSpec(
            num_scalar_prefetch=1, grid=(4, R // tl),
            in_specs=[pl.BlockSpec((1, tl, C), lambda k, i, c: (2 * k + c[0], i, 0)),
                      pl.BlockSpec((1, tl, C), lambda k, i, c: (k, i, 0))],
            out_specs=pl.BlockSpec((1, tl, C), lambda k, i, c: (k, i, 0))),
        compiler_params=pltpu.CompilerParams(dimension_semantics=("parallel", "parallel")), name='rs_pair_add_' + tag)(
            core, gfull, land)


def rs_chips(part, tag):
    return run_comm(ChipExchange(part), 'rs_chips_' + tag)[0]


def _adam(wv, gv, mv, vv):
    m = ADAM_B1 * mv + (1.0 - ADAM_B1) * gv
    v = ADAM_B2 * vv + (1.0 - ADAM_B2) * (gv * gv)
    m_hat = m / (1.0 - ADAM_B1 ** ADAM_STEP)
    v_hat = v / (1.0 - ADAM_B2 ** ADAM_STEP)
    delta = -ADAM_LR * (m_hat / (jnp.sqrt(v_hat) + ADAM_EPS) + ADAM_WD * wv)
    return delta, m, v


def _sum4(p_ref, l_ref):
    return ((p_ref[0].astype(F32) + l_ref[0].astype(F32)) + l_ref[1].astype(F32)) + l_ref[2].astype(F32)


def rs_rep_sum(part, land, chip):
    def body(c_ref, p_ref, l_ref, o_ref):
        o_ref[...] = _sum4(p_ref, l_ref)

    return pl.pallas_call(
        body, out_shape=jax.ShapeDtypeStruct((REP_SLOT, LANES), F32),
        grid_spec=pltpu.PrefetchScalarGridSpec(
            num_scalar_prefetch=1, grid=(1,),
            in_specs=[pl.BlockSpec((1, REP_SLOT, LANES), lambda i, c: (c[0], 0, 0)),
                      pl.BlockSpec((3, REP_SLOT, LANES), lambda i, c: (0, 0, 0))],
            out_specs=pl.BlockSpec((REP_SLOT, LANES), lambda i, c: (0, 0))),
        compiler_params=pltpu.CompilerParams(dimension_semantics=("parallel",)), name='rs_rep')(chip, part, land)


def adam_param(name, shape, off, w, m, v, chip, part=None, land=None, grep=None):
    r, c = shape
    rp, nt, rb = _tiles(shape)
    rbw = min(r, rb)
    n_src = 2 if grep is None else 1

    def body(c_ref, *refs):
        srcs = refs[:n_src * nt]
        w_ref, m_ref, v_ref, g_ref, d_ref, nm_ref, nv_ref = refs[n_src * nt:]
        if grep is None:
            tiles = [_sum4(srcs[2 * t], srcs[2 * t + 1]) for t in range(nt)]
        else:
            tiles = [srcs[t][...] for t in range(nt)]
        g = (tiles[0] if nt == 1 else jnp.concatenate(tiles, axis=1))[:rbw, :c]
        g_ref[...] = g
        d_ref[...], nm_ref[...], nv_ref[...] = _adam(w_ref[...], g, m_ref[...], v_ref[...])

    in_specs, args = [], []
    for t in range(nt):
        b0 = (off + t * rp) // rb
        assert (off + t * rp) % rb == 0
        if grep is None:
            in_specs += [pl.BlockSpec((1, rb, LANES), functools.partial(lambda i, cr, b0: (cr[0], b0 + i, 0), b0=b0)),
                         pl.BlockSpec((3, rb, LANES), functools.partial(lambda i, cr, b0: (0, b0 + i, 0), b0=b0))]
            args += [part, land]
        else:
            in_specs.append(pl.BlockSpec((rb, LANES), functools.partial(lambda i, cr, b0: (b0 + i, 0), b0=b0)))
            args.append(grep)
    nat = pl.BlockSpec((rbw, c), lambda i, cr: (i, 0))
    return pl.pallas_call(
        body, out_shape=[jax.ShapeDtypeStruct((r, c), F32)] * 4,
        grid_spec=pltpu.PrefetchScalarGridSpec(num_scalar_prefetch=1, grid=(rp // rb,), in_specs=in_specs + [nat] * 3,
                                               out_specs=[nat] * 4),
        compiler_params=pltpu.CompilerParams(dimension_semantics=("parallel",)), name='adam_' + name)(
            chip, *args, w, m, v)


VM = pl.BlockSpec(memory_space=pltpu.VMEM)


def _tile_value(w, t, rp):
    r, c = w.shape
    wt = min(LANES, c - t * LANES)
    tile = w[:, t * LANES:t * LANES + wt]
    if wt < LANES:
        tile = jnp.concatenate([tile, jnp.zeros((r, LANES - wt), tile.dtype)], axis=1)
    if rp > r:
        tile = jnp.concatenate([tile, jnp.zeros((rp - r, LANES), tile.dtype)], axis=0)
    return tile


def pack_layer(layer, blocks):
    names = LAYER_PARAMS[layer]

    def body(*refs):
        tiles = []
        for ref, n in zip(refs[:-1], names):
            rp, nt, _ = _tiles(_block_shape(n))
            w = ref[...]
            tiles += [_tile_value(w, t, rp) for t in range(nt)]
        refs[-1][...] = jnp.concatenate(tiles, axis=0).astype(BF16)

    return pl.pallas_call(body, out_shape=jax.ShapeDtypeStruct((LAYER_ROWS[layer], LANES), BF16),
                          in_specs=[VM] * len(names), out_specs=VM, name='pack_' + layer)(*[blocks[n] for n in names])


def assemble(name, gathered):
    (rf, cf), ax = SHARDED[name]
    r, c = _block_shape(name)
    rp, nt, _ = _tiles((r, c))
    off = SH_OFF[name]
    out_cols = cf if ax == 0 else len(perm_index(name))

    def body(g_ref, o_ref, buf, sem):
        cp = pltpu.make_async_copy(g_ref.at[:, pl.ds(off, nt * rp), :], buf, sem)
        cp.start()
        cp.wait()
        if ax == 0:
            for j in range(N_DEV):
                o_ref[j * r:(j + 1) * r, :] = jnp.concatenate([buf[j, t * rp:(t + 1) * rp, :] for t in range(nt)], axis=1)
            return
        pieces = []
        for p in PERM[name]:
            if p[0] == 'z':
                pieces.append(jnp.zeros((r, p[1]), BF16))
                continue
            n0, w = p
            while w > 0:
                j, cb = divmod(n0, c)
                t, lane = divmod(cb, LANES)
                wl = min(w, LANES - lane, c - cb)
                pieces.append(buf[j, t * rp:t * rp + r, lane:lane + wl])
                n0, w = n0 + wl, w - wl
        o_ref[...] = jnp.concatenate(pieces, axis=1)

    return pl.pallas_call(
        body, out_shape=jax.ShapeDtypeStruct((rf, out_cols), BF16), in_specs=[ANY], out_specs=VM,
        scratch_shapes=[pltpu.VMEM((N_DEV, nt * rp, LANES), BF16), pltpu.SemaphoreType.DMA(())], name='asm_' + name)(
            gathered)


def chunk_grad(layer, name, dw, gfull):
    (rf, cf), ax = SHARDED[name]
    r, c = _block_shape(name)
    rp, nt, _ = _tiles((r, c))
    off = SH_OFF[name]
    if ax == 1:
        idx = perm_index(name) if name in PERM else np.arange(cf)
        inv = np.full(cf, -1)
        inv[idx[idx >= 0]] = np.nonzero(idx >= 0)[0]

    def body(*refs):
        dw_ref, o_ref, buf, sem = refs[0], refs[-3], refs[-2], refs[-1]
        for j in range(N_DEV):
            for t in range(nt):
                if ax == 0:
                    tile = dw_ref[j * r:(j + 1) * r, t * LANES:(t + 1) * LANES]
                else:
                    cols = inv[j * c + t * LANES:j * c + min((t + 1) * LANES, c)]
                    cuts = [0] + [k for k in range(1, len(cols)) if cols[k] != cols[k - 1] + 1] + [len(cols)]
                    pieces = [dw_ref[:, int(cols[a]):int(cols[b - 1]) + 1] for a, b in zip(cuts[:-1], cuts[1:])]
                    if len(cols) < LANES:
                        pieces.append(jnp.zeros((r, LANES - len(cols)), F32))
                    tile = pieces[0] if len(pieces) == 1 else jnp.concatenate(pieces, axis=1)
                    if rp > r:
                        tile = jnp.concatenate([tile, jnp.zeros((rp - r, LANES), F32)], axis=0)
                buf[j, t * rp:(t + 1) * rp, :] = tile
        cp = pltpu.make_async_copy(buf, o_ref.at[:, pl.ds(off, nt * rp), :], sem)
        cp.start()
        cp.wait()

    shape = jax.ShapeDtypeStruct((N_DEV, LAYER_ROWS[layer], LANES), F32)
    scratch = [pltpu.VMEM((N_DEV, nt * rp, LANES), F32), pltpu.SemaphoreType.DMA(())]
    if gfull is None:
        return pl.pallas_call(body, out_shape=shape, in_specs=[VM], out_specs=ANY, scratch_shapes=scratch,
                              name='chunk_' + name)(dw)
    return pl.pallas_call(body, out_shape=shape, in_specs=[VM, ANY], out_specs=ANY, scratch_shapes=scratch,
                          input_output_aliases={1: 0}, name='chunk_' + name)(dw, gfull)


def pack_rep(G):
    def body(*refs):
        tiles = []
        for ref, s in zip(refs[:-1], REP_SHAPE.values()):
            rp, nt, _ = _tiles(s)
            g = ref[...]
            tiles += [_tile_value(g, t, rp) for t in range(nt)]
        full = jnp.concatenate(tiles, axis=0)
        for j in range(N_DEV):
            refs[-1][j, 0:REP_CHUNK, :] = full[j * REP_CHUNK:(j + 1) * REP_CHUNK]
            if REP_SLOT > REP_CHUNK:
                refs[-1][j, REP_CHUNK:REP_SLOT, :] = jnp.zeros((REP_SLOT - REP_CHUNK, LANES), F32)

    return pl.pallas_call(body, out_shape=jax.ShapeDtypeStruct((N_DEV, REP_SLOT, LANES), F32),
                          in_specs=[VM] * len(REP_SHAPE), out_specs=VM, name='pack_rep')(
                              *[G[n].reshape(s) for n, s in REP_SHAPE.items()])


def _pack_small(blocks, order, rows, width, dtype):
    flat = jnp.concatenate([blocks[n].reshape(-1).astype(dtype) for n in order])
    return jnp.pad(flat, (0, rows * width - flat.shape[0])).reshape(rows, width)


def kernel(x, pre_norm, post_norm, rel_bias, a_w_in, a_lam_re, a_lam_im, a_log_dt, a_b_re, a_b_im, a_c_re, a_c_im, a_d, a_w_glu, a_b_glu, a_w_out, b_w_in, b_sinks, b_w_out, c_w_in, c_q_norm, c_kv_norm, c_w_uq, c_w_ukv, c_w_out, d_w_in, d_ln_g, d_ln_b, d_w_s, d_b_s, d_w_out, loss_target, m_pre_norm, m_post_norm, m_rel_bias, m_a_w_in, m_a_lam_re, m_a_lam_im, m_a_log_dt, m_a_b_re, m_a_b_im, m_a_c_re, m_a_c_im, m_a_d, m_a_w_glu, m_a_b_glu, m_a_w_out, m_b_w_in, m_b_sinks, m_b_w_out, m_c_w_in, m_c_q_norm, m_c_kv_norm, m_c_w_uq, m_c_w_ukv, m_c_w_out, m_d_w_in, m_d_ln_g, m_d_ln_b, m_d_w_s, m_d_b_s, m_d_w_out, v_pre_norm, v_post_norm, v_rel_bias, v_a_w_in, v_a_lam_re, v_a_lam_im, v_a_log_dt, v_a_b_re, v_a_b_im, v_a_c_re, v_a_c_im, v_a_d, v_a_w_glu, v_a_b_glu, v_a_w_out, v_b_w_in, v_b_sinks, v_b_w_out, v_c_w_in, v_c_q_norm, v_c_kv_norm, v_c_w_uq, v_c_w_ukv, v_c_w_out, v_d_w_in, v_d_ln_g, v_d_ln_b, v_d_w_s, v_d_b_s, v_d_w_out):
    loc = locals()
    P = {n: loc[n] for n in WEIGHTS}
    M = {n: loc['m_' + n] for n in WEIGHTS}
    V = {n: loc['v_' + n] for n in WEIGHTS}
    xs = x[0]
    L = xs.shape[0]

    blocks = {n: P[n].reshape(_block_shape(n)) for n in SHARDED}
    packed = {layer: pack_layer(layer, blocks) for layer in LAYER_PARAMS}
    W = {}

    def assemble_layer(layer, gathered):
        for n in LAYER_PARAMS[layer]:
            if n not in SHARDED_F32:
                W[n] = assemble(n, gathered)

    assemble_layer('a', all_gather(packed['a'], 'ag_a'))
    small = all_gather(_pack_small(blocks, SHARDED_F32, SMALL_ROWS, 128, F32), 'ag_small')
    Pl = dict(P)
    for n in SHARDED_F32:
        c = SHARDED[n][0][1]
        bc = c // N_DEV
        Pl[n] = small.reshape(N_DEV, -1)[:, SMALL_OFF[n]:SMALL_OFF[n] + bc].reshape(1, c)
    cx, cy, cc = _coords()
    core = jnp.reshape(cc, (1,)).astype(jnp.int32)
    chip = jnp.reshape(2 * cx + cy, (1,)).astype(jnp.int32)

    def pair_sums(gfull, tag):
        return rs_pair_add(gfull, rs_sibling(gfull, tag), core, tag)

    fwd = [layer_a_fwd, layer_b_fwd, layer_c_fwd, layer_d_fwd]
    bwd = [layer_a_bwd, layer_b_bwd, layer_c_bwd, layer_d_bwd]
    saved = []
    xc = xs
    for i in range(4):
        def fpre(x_, g_):
            return [rms_fwd(x_, g_)], []
        (h,), _ = rowwise(fpre, [rw(xc)], [P['pre_norm'][i:i + 1]], [(D_MODEL, F32)], [], 256, f'pre_norm{i}')
        if i < 3:
            nxt = 'bcd'[i]
            yb, sv = fwd[i](h, W, Pl, comm=AllGather(packed[nxt]))
            assemble_layer(nxt, sv['carried'][0])
        else:
            yb, sv = fwd[i](h, W, Pl)

        def fpost(x_, y_, g_):
            return [x_ + rms_fwd(y_, g_)], []
        (xn,), _ = rowwise(fpost, [rw(xc), rw(yb)], [P['post_norm'][i:i + 1]], [(D_MODEL, F32)], [], 256, f'post_norm{i}')
        sv['x'], sv['yb'] = xc, yb
        saved.append(sv)
        xc = xn

    def floss(y_, t_):
        d = y_ - t_
        return [d * (1.0 / D_MODEL)], [0.5 * jnp.sum(jnp.sum(d * d, axis=-1, keepdims=True) * (1.0 / D_MODEL), axis=0,
                                                      keepdims=True)]
    (dx,), (loss_loc,) = rowwise(floss, [rw(xc), rw(loss_target[0])], [], [(D_MODEL, F32)], [(1, 1)], 256, 'loss')
    loss = lax.psum(loss_loc[0, 0], ("x", "y", "c"))

    G, out = {}, {}
    dpre, dpost = [None] * 4, [None] * 4

    def adam_layer(layer, part, land2):
        for n in LAYER_PARAMS[layer]:
            s = _block_shape(n)
            out[n] = adam_param(n, s, SH_OFF[n], blocks[n], M[n].reshape(s), V[n].reshape(s), chip, part=part,
                                land=land2)

    pending = None
    for i in reversed(range(4)):
        sv = saved[i]

        def fpost_b(y_, d_, g_):
            dy, dg = rms_bwd(y_, g_, d_)
            return [dy], [dg]
        (dyb,), (dpost[i],) = rowwise(fpost_b, [rw(sv['yb']), rw(dx)], [P['post_norm'][i:i + 1]], [(D_MODEL, F32)],
                                      [(1, D_MODEL)], 256, f'post_norm_bwd{i}')
        if pending is None:
            dh, g = bwd[i](dyb, W, Pl, sv)
        else:
            dh, g = bwd[i](dyb, W, Pl, sv, comm=ChipExchange(pending[1]))
            adam_layer(pending[0], pending[1], g['carried'][0])
        g.pop('carried', None)
        G.update(g)

        def fpre_b(x_, dh_, d_, g_):
            dxl, dg = rms_bwd(x_, g_, dh_)
            return [d_ + dxl], [dg]
        (dx,), (dpre[i],) = rowwise(fpre_b, [rw(sv['x']), rw(dh), rw(dx)], [P['pre_norm'][i:i + 1]], [(D_MODEL, F32)],
                                    [(1, D_MODEL)], 256, f'pre_norm_bwd{i}')

        layer = 'abcd'[i]
        gfull = None
        for n in LAYER_PARAMS[layer]:
            gfull = chunk_grad(layer, n, G[n], gfull)
        pending = (layer, pair_sums(gfull, layer))
    adam_layer(pending[0], pending[1], rs_chips(pending[1], pending[0]))
    G['pre_norm'] = jnp.concatenate(dpre, axis=0)
    G['post_norm'] = jnp.concatenate(dpost, axis=0)

    part = pair_sums(pack_rep(G), 'rep')
    land2 = rs_chips(part, 'rep')
    grep = all_gather(rs_rep_sum(part, land2, chip), 'ag_rep')[:, :REP_CHUNK].reshape(REP_ROWS, LANES)
    for n, s in REP_SHAPE.items():
        out[n] = adam_param(n, s, REP_OFF[n], P[n].reshape(s), M[n].reshape(s), V[n].reshape(s), chip, grep=grep)
    res = [loss, dx[None]]
    for kind in range(4):
        res += [out[n][kind].reshape(P[n].shape) for n in WEIGHTS]
    return tuple(res)
```

```python
import functools
import math

import numpy as np
import jax
import jax.numpy as jnp
from jax import lax
from jax.experimental import pallas as pl
from jax.experimental.pallas import tpu as pltpu

F32 = jnp.float32
BF16 = jnp.bfloat16
MESH = pl.DeviceIdType.MESH
ANY = pl.BlockSpec(memory_space=pl.ANY)

N_DEV = 8
D_MODEL = 1024
EPS = 1e-6
NEG_INF = -1e30
SSM_G, SSM_P, SSM_H = 64, 64, 16
SSM_T = 256
SSM_WC = 512
HEAD_DIM = 64
SWA_HEADS, SWA_KV = 16, 2
WINDOW = 128
REL_BUCKETS, REL_MAX_DIST = 32, 128
MLA_HEADS, MLA_NOPE, MLA_ROPE, MLA_V = 16, 64, 32, 64
MLA_Q_RANK, MLA_KV_RANK = 768, 256
ROPE_BASE = 10000.0
SGU_G, SGU_C, SGU_T = 16, 64, 128
ADAM_LR, ADAM_B1, ADAM_B2, ADAM_EPS, ADAM_WD, ADAM_STEP = 0.001, 0.9, 0.999, 1e-08, 0.01, 10

WEIGHTS = ['pre_norm', 'post_norm', 'rel_bias', 'a_w_in', 'a_lam_re', 'a_lam_im', 'a_log_dt', 'a_b_re', 'a_b_im',
           'a_c_re', 'a_c_im', 'a_d', 'a_w_glu', 'a_b_glu', 'a_w_out', 'b_w_in', 'b_sinks', 'b_w_out', 'c_w_in',
           'c_q_norm', 'c_kv_norm', 'c_w_uq', 'c_w_ukv', 'c_w_out', 'd_w_in', 'd_ln_g', 'd_ln_b', 'd_w_s', 'd_b_s',
           'd_w_out']
SHARDED = {'a_w_in': ((1024, 2048), 1), 'a_w_glu': ((1024, 1024), 0), 'a_w_out': ((1024, 1024), 0),
           'b_w_in': ((1024, 2304), 1), 'b_w_out': ((1024, 1024), 0), 'c_w_in': ((1024, 2080), 1),
           'c_q_norm': ((1, 768), 1), 'c_kv_norm': ((1, 256), 1), 'c_w_uq': ((768, 1536), 1),
           'c_w_ukv': ((256, 2048), 1), 'c_w_out': ((1024, 1024), 0), 'd_w_in': ((1024, 3072), 1),
           'd_ln_g': ((1, 1024), 1), 'd_ln_b': ((1, 1024), 1), 'd_w_out': ((1024, 1024), 0)}
SHARDED_F32 = ['c_q_norm', 'c_kv_norm', 'd_ln_g', 'd_ln_b']
REPLICATED = [n for n in WEIGHTS if n not in SHARDED]


def _cdiv(a, b):
    return -(-a // b)


def _block_shape(name):
    (r, c), ax = SHARDED[name]
    return (r // N_DEV, c) if ax == 0 else (r, c // N_DEV)


LANES = 128
LAYER_PARAMS = {'a': ['a_w_in', 'a_w_glu', 'a_w_out'], 'b': ['b_w_in', 'b_w_out'],
                'c': ['c_w_in', 'c_w_uq', 'c_w_ukv', 'c_w_out', 'c_q_norm', 'c_kv_norm'],
                'd': ['d_w_in', 'd_w_out', 'd_ln_g', 'd_ln_b']}


def _tiles(shape):
    r, c = shape
    rp = max(r, 8)
    rb = 512 if rp % 512 == 0 else 256 if rp % 256 == 0 else rp
    return rp, _cdiv(c, LANES), rb


SH_OFF, LAYER_ROWS = {}, {}
for _l, _names in LAYER_PARAMS.items():
    _o = 0
    for _n in _names:
        _rp, _nt, _rb = _tiles(_block_shape(_n))
        assert _o % _rb == 0
        SH_OFF[_n] = _o
        _o += _rp * _nt
    assert _o % 16 == 0
    LAYER_ROWS[_l] = _o

REP_SHAPE = {'a_b_re': (4096, 16), 'a_b_im': (4096, 16), 'd_w_s': (2048, 128), 'a_c_re': (1024, 64),
             'a_c_im': (1024, 64), 'pre_norm': (4, 1024), 'post_norm': (4, 1024), 'a_lam_re': (64, 64),
             'a_lam_im': (64, 64), 'a_d': (1, 1024), 'a_b_glu': (1, 1024), 'rel_bias': (32, 16), 'd_b_s': (16, 128),
             'a_log_dt': (1, 64), 'b_sinks': (1, 16)}
REP_OFF = {}
_o = 0
for _n, _s in REP_SHAPE.items():
    _rp, _nt, _rb = _tiles(_s)
    assert _o % _rb == 0
    REP_OFF[_n] = _o
    _o += _rp * _nt
REP_ROWS = _o
REP_CHUNK = REP_ROWS // N_DEV
assert REP_ROWS % (8 * N_DEV) == 0
REP_SLOT = _cdiv(REP_CHUNK, 16) * 16

PERM = {'a_w_in': [(0, 2048)], 'd_w_in': [(0, 3072)], 'b_w_in': [(1280, 1024), (0, 1280)],
        'c_w_in': [(1056, 1024), (0, 1056), ('z', 96)],
        'c_w_uq': sum([[(2 * hp * 96, 64), ((2 * hp + 1) * 96, 64), (2 * hp * 96 + 64, 32), ((2 * hp + 1) * 96 + 64, 32),
                        ('z', 64)] for hp in range(8)], []),
        'c_w_ukv': sum([[(2 * hp * 128, 64), ((2 * hp + 1) * 128, 64), (2 * hp * 128 + 64, 64),
                         ((2 * hp + 1) * 128 + 64, 64)] for hp in range(8)], [])}


def perm_index(name):
    return np.concatenate([np.full(p[1], -1) if p[0] == 'z' else np.arange(p[0], p[0] + p[1]) for p in PERM[name]])


SMALL_OFF = {}
_o = 0
for _n in SHARDED_F32:
    SMALL_OFF[_n] = _o
    _o += int(np.prod(_block_shape(_n)))
SMALL_ROWS = _cdiv(_o, 128 * 8) * 8


def _pick(n, cands):
    for c in cands:
        if n % c == 0:
            return c
    return n


def mm(a, b, mode, name, out_dtype=F32):
    if mode == 'nn':
        (M, K), (K2, N) = a.shape, b.shape
    elif mode == 'nt':
        (M, K), (N, K2) = a.shape, b.shape
    else:
        (K, M), (K2, N) = a.shape, b.shape
    assert K == K2, (name, a.shape, b.shape)
    tm = _pick(M, (512, 256, 128))
    tn = _pick(N, (512, 384, 256))
    dims = {'nn': ((1,), (0,)), 'nt': ((1,), (1,)), 'tn': ((0,), (0,))}[mode]

    def body(a_ref, b_ref, o_ref):
        o_ref[...] = lax.dot_general(a_ref[...].astype(BF16), b_ref[...].astype(BF16), (dims, ((), ())),
                                     preferred_element_type=F32).astype(out_dtype)

    a_spec = pl.BlockSpec((K, tm), lambda i, j: (0, i)) if mode == 'tn' else pl.BlockSpec((tm, K), lambda i, j: (i, 0))
    b_spec = pl.BlockSpec((tn, K), lambda i, j: (j, 0)) if mode == 'nt' else pl.BlockSpec((K, tn), lambda i, j: (0, j))
    return pl.pallas_call(
        body, grid=(M // tm, N // tn), in_specs=[a_spec, b_spec],
        out_specs=pl.BlockSpec((tm, tn), lambda i, j: (i, j)), out_shape=jax.ShapeDtypeStruct((M, N), out_dtype),
        compiler_params=pltpu.CompilerParams(dimension_semantics=("parallel", "parallel")), name=name)(a, b)


def rw(arr, width=None, cb=0):
    return (arr, arr.shape[1] if width is None else width, cb)


def rowwise(fn, rows, consts, outs, accs, tl, name, n_steps=None):
    if n_steps is None:
        n_steps = [r[0].shape[0] for r in rows if not isinstance(r[1], pl.BlockSpec)][0] // tl
    L = n_steps * tl
    nr, nc, no, na = len(rows), len(consts), len(outs), len(accs)
    in_specs, args = [], []
    for r in rows:
        if isinstance(r[1], pl.BlockSpec):
            in_specs.append(r[1])
        else:
            in_specs.append(pl.BlockSpec((tl, r[1]), functools.partial(lambda i, cb: (i, cb), cb=r[2])))
        args.append(r[0])
    for c in consts:
        in_specs.append(pl.BlockSpec(c.shape, functools.partial(lambda i, nd: (0,) * nd, nd=c.ndim)))
        args.append(c)
    out_specs = [pl.BlockSpec((tl, w), lambda i: (i, 0)) for w, _ in outs]
    out_shape = [jax.ShapeDtypeStruct((L, w), dt) for w, dt in outs]
    for s in accs:
        out_specs.append(pl.BlockSpec(s, functools.partial(lambda i, nd: (0,) * nd, nd=len(s))))
        out_shape.append(jax.ShapeDtypeStruct(s, F32))

    def body(*refs):
        ins = [r[...] for r in refs[:nr + nc]]
        o_refs = refs[nr + nc:nr + nc + no]
        a_refs = refs[nr + nc + no:]
        o_vals, a_vals = fn(*ins)
        for ref, val in zip(o_refs, o_vals):
            ref[...] = val.astype(ref.dtype)
        if na:
            @pl.when(pl.program_id(0) == 0)
            def _():
                for ref in a_refs:
                    ref[...] = jnp.zeros_like(ref)
            for ref, val in zip(a_refs, a_vals):
                ref[...] += val

    res = pl.pallas_call(
        body, grid=(n_steps,), in_specs=in_specs, out_specs=out_specs, out_shape=out_shape,
        compiler_params=pltpu.CompilerParams(dimension_semantics=("arbitrary",)), name=name)(*args)
    return res[:no], res[no:]


def carried(body, comm, *, grid, in_specs, out_specs, out_shape, name, semantics, scratch_shapes=()):
    single = not isinstance(out_shape, (list, tuple))
    o_specs = [out_specs] if single else list(out_specs)
    o_shape = [out_shape] if single else list(out_shape)
    if comm is None:
        call = pl.pallas_call(body, grid=grid, in_specs=in_specs, out_specs=out_specs, out_shape=out_shape,
                              scratch_shapes=list(scratch_shapes),
                              compiler_params=pltpu.CompilerParams(dimension_semantics=semantics), name=name)
        return lambda *args: (call(*args), None)
    n_in, n_out, n_sc = len(in_specs), len(o_specs), len(scratch_shapes)
    ci, co = len(comm.ins), len(comm.outs)
    n_steps = int(np.prod(grid))
    hooks = comm.hooks(n_steps)

    def wrapped(*refs):
        ins, cins = refs[:n_in], refs[n_in:n_in + ci]
        outs, couts = refs[n_in + ci:n_in + ci + n_out], refs[n_in + ci + n_out:n_in + ci + n_out + co]
        sc, csc = refs[n_in + ci + n_out + co:n_in + ci + n_out + co + n_sc], refs[n_in + ci + n_out + co + n_sc:]
        step = pl.program_id(0)
        for ax in range(1, len(grid)):
            step = step * grid[ax] + pl.program_id(ax)
        for at, fn, after in hooks:
            if not after:
                pl.when(step == at)(functools.partial(fn, cins, couts, csc))
        body(*ins, *outs, *sc)
        for at, fn, after in hooks:
            if after:
                pl.when(step == at)(functools.partial(fn, cins, couts, csc))

    call = pl.pallas_call(wrapped, grid=grid, in_specs=list(in_specs) + [ANY] * ci, out_specs=o_specs + [ANY] * co,
                          out_shape=o_shape + list(comm.outs), scratch_shapes=list(scratch_shapes) + list(comm.scratch),
                          compiler_params=pltpu.CompilerParams(dimension_semantics=("arbitrary",) * len(grid)), name=name)

    def run(*args):
        res = call(*args, *comm.ins)
        return (res[0] if single else res[:n_out]), res[n_out:]
    return run


_K0 = math.sqrt(2.0 / math.pi)
_K1 = 0.044715


def gelu(x):
    return x * (0.5 * (1.0 + jnp.tanh(_K0 * (x + _K1 * (x * x * x)))))


def gelu_grad(x):
    t = jnp.tanh(_K0 * (x + _K1 * (x * x * x)))
    return 0.5 * (1.0 + t) + 0.5 * x * (1.0 - t * t) * (_K0 * (1.0 + 3.0 * _K1 * x * x))


def sigmoid(x):
    return 1.0 / (1.0 + jnp.exp(-x))


def silu(z):
    return z * sigmoid(z)


def silu_grad(z):
    s = sigmoid(z)
    return s * (1.0 + z * (1.0 - s))


def rms_fwd(x, g):
    r = lax.rsqrt(jnp.mean(x * x, axis=-1, keepdims=True) + EPS)
    return x * r * g


def rms_bwd(x, g, dy):
    r = lax.rsqrt(jnp.mean(x * x, axis=-1, keepdims=True) + EPS)
    xh = x * r
    dg = jnp.sum(dy * xh, axis=0, keepdims=True)
    dxh = dy * g
    dx = r * (dxh - xh * jnp.mean(dxh * xh, axis=-1, keepdims=True))
    return dx, dg


def _scan_steps(v_r, v_i, pr_ref, pi_ref, sgn, steps, reverse, idx, n):
    rows = v_r.shape[0]
    for d, k in steps:
        wr = pr_ref[k:k + 1, :]
        wi = sgn * pi_ref[k:k + 1, :]
        if reverse:
            yr, yi, keep = pltpu.roll(v_r, rows - d, 0), pltpu.roll(v_i, rows - d, 0), idx < n - d
        else:
            yr, yi, keep = pltpu.roll(v_r, d, 0), pltpu.roll(v_i, d, 0), idx >= d
        v_r, v_i = (v_r + jnp.where(keep, wr * yr - wi * yi, 0.0), v_i + jnp.where(keep, wr * yi + wi * yr, 0.0))
    return v_r, v_i


def _scan_chunk(a_r, a_i, pr_ref, pi_ref, cr, ci, T, reverse):
    G = T // 8
    sgn = -1.0 if reverse else 1.0
    sub = lax.broadcasted_iota(jnp.int32, a_r.shape, 0) & 7
    pw = (lambda e: T - e) if reverse else (lambda e: e - 1)
    a_r, a_i = _scan_steps(a_r, a_i, pr_ref, pi_ref, sgn, [(d, pw(d)) for d in (1, 2, 4)], reverse, sub, 8)
    e0 = 0 if reverse else 7
    pick = (lax.broadcasted_iota(jnp.int32, (G, T), 1) == 8 * lax.broadcasted_iota(jnp.int32, (G, T), 0) + e0).astype(F32)
    spread = ((lax.broadcasted_iota(jnp.int32, (T, G), 0) >> 3) == lax.broadcasted_iota(jnp.int32, (T, G), 1)).astype(F32)
    exact = functools.partial(jnp.dot, preferred_element_type=F32, precision=lax.Precision.HIGHEST)
    e_r, e_i = exact(pick, a_r), exact(pick, a_i)
    grp = lax.broadcasted_iota(jnp.int32, e_r.shape, 0)
    steps = [(d, pw(8 * d)) for d in (1, 2, 4, 8, 16, 32, 64, 128) if d < G]
    e_r, e_i = _scan_steps(e_r, e_i, pr_ref, pi_ref, sgn, steps, reverse, grp, G)
    c_r, c_i = cr[...], ci[...]
    tr, ti = exact(pick, pr_ref[...]), sgn * exact(pick, pi_ref[...])
    e_r, e_i = e_r + (tr * c_r - ti * c_i), e_i + (tr * c_i + ti * c_r)
    if reverse:
        p_r, p_i = (jnp.where(grp == G - 1, c_r, pltpu.roll(e_r, G - 1, 0)),
                    jnp.where(grp == G - 1, c_i, pltpu.roll(e_i, G - 1, 0)))
        cr[...], ci[...] = e_r[0:1, :], e_i[0:1, :]
        w8r, w8i = pr_ref[T - 8:T, :], sgn * pi_ref[T - 8:T, :]
    else:
        p_r, p_i = jnp.where(grp == 0, c_r, pltpu.roll(e_r, 1, 0)), jnp.where(grp == 0, c_i, pltpu.roll(e_i, 1, 0))
        cr[...], ci[...] = e_r[G - 1:G, :], e_i[G - 1:G, :]
        w8r, w8i = pr_ref[0:8, :], sgn * pi_ref[0:8, :]
    b_r, b_i = exact(spread, p_r), exact(spread, p_i)
    wr, wi = jnp.tile(w8r, (G, 1)), jnp.tile(w8i, (G, 1))
    return a_r + (wr * b_r - wi * b_i), a_i + (wr * b_i + wi * b_r)


_NT = (((1,), (1,)), ((), ()))
_TN = (((0,), (0,)), ((), ()))


def s5_fwd(proj, d_skip, Bre, Bim, Cre, Cim, pr, pi, comm=None):
    L = proj.shape[0]
    T, WC = min(SSM_T, L), SSM_WC
    nT = L // T

    def body(u_ref, d_ref, bre_ref, bim_ref, cre_ref, cim_ref, pr_ref, pi_ref, y_ref, yg_ref, sr_ref, si_ref, cr, ci):
        @pl.when(pl.program_id(1) == 0)
        def _():
            cr[...] = jnp.zeros_like(cr)
            ci[...] = jnp.zeros_like(ci)

        u = u_ref[...]
        ub = u.astype(BF16)
        a_r = jnp.dot(ub, bre_ref[0].astype(BF16), preferred_element_type=F32)
        a_i = jnp.dot(ub, bim_ref[0].astype(BF16), preferred_element_type=F32)
        a_r, a_i = _scan_chunk(a_r, a_i, pr_ref, pi_ref, cr, ci, T, False)
        sr_ref[...] = a_r
        si_ref[...] = a_i
        y = (jnp.dot(a_r.astype(BF16), cre_ref[0].astype(BF16), preferred_element_type=F32)
             + jnp.dot(a_i.astype(BF16), cim_ref[0].astype(BF16), preferred_element_type=F32) + d_ref[...] * u)
        y_ref[...] = y
        yg_ref[...] = gelu(y)

    uspec = pl.BlockSpec((T, 128), lambda k, i: (i, k))
    sspec = pl.BlockSpec((T, WC), lambda k, i: (i, k))
    return carried(
        body, comm, grid=(8, nT),
        in_specs=[uspec, pl.BlockSpec((1, 128), lambda k, i: (0, k)),
                  pl.BlockSpec((1, 128, WC), lambda k, i: (k, 0, 0)), pl.BlockSpec((1, 128, WC), lambda k, i: (k, 0, 0)),
                  pl.BlockSpec((1, WC, 128), lambda k, i: (k, 0, 0)), pl.BlockSpec((1, WC, 128), lambda k, i: (k, 0, 0)),
                  pl.BlockSpec((T, WC), lambda k, i: (0, k)), pl.BlockSpec((T, WC), lambda k, i: (0, k))],
        out_specs=[uspec, uspec, sspec, sspec],
        out_shape=[jax.ShapeDtypeStruct((L, 1024), F32)] * 2 + [jax.ShapeDtypeStruct((L, 8 * WC), F32)] * 2,
        scratch_shapes=[pltpu.VMEM((1, WC), F32), pltpu.VMEM((1, WC), F32)],
        semantics=("parallel", "arbitrary"), name='a_ssm')(proj, d_skip, Bre, Bim, Cre, Cim, pr, pi)


def s5_bwd(proj, dyg1, dyg2, y, d_skip, s_re, s_im, Bre, Bim, Cre, Cim, prr, pir, comm=None):
    L = proj.shape[0]
    T, WC = min(SSM_T, L), SSM_WC
    nT = L // T

    def body(u_ref, g1_ref, g2_ref, y_ref, d_ref, sr_ref, si_ref, spr_ref, spi_ref, bre_ref, bim_ref, cre_ref, cim_ref,
             pr_ref, pi_ref, du_ref, dd_ref, dbre_ref, dbim_ref, dcre_ref, dcim_ref, dar_ref, dai_ref, cr, ci):
        i = pl.program_id(1)

        @pl.when(i == 0)
        def _():
            for ref in (cr, ci, dd_ref, dbre_ref, dbim_ref, dcre_ref, dcim_ref, dar_ref, dai_ref):
                ref[...] = jnp.zeros_like(ref)

        u = u_ref[...]
        dy = (g1_ref[...] + g2_ref[...]) * gelu_grad(y_ref[...])
        dd_ref[...] += jnp.sum(dy * u, axis=0, keepdims=True)
        dyb, ub = dy.astype(BF16), u.astype(BF16)
        bre, bim, cre, cim = (r[0].astype(BF16) for r in (bre_ref, bim_ref, cre_ref, cim_ref))
        g_r = lax.dot_general(dyb, cre, _NT, preferred_element_type=F32)
        g_i = lax.dot_general(dyb, cim, _NT, preferred_element_type=F32)
        g_r, g_i = _scan_chunk(g_r, g_i, pr_ref, pi_ref, cr, ci, T, True)
        s_r, s_i = sr_ref[...], si_ref[...]
        row = lax.broadcasted_iota(jnp.int32, (T, WC), 0)
        first = (nT - 1 - i) == 0
        sp_r = jnp.where(row == 0, jnp.where(first, 0.0, spr_ref[7:8, :]), pltpu.roll(s_r, 1, 0))
        sp_i = jnp.where(row == 0, jnp.where(first, 0.0, spi_ref[7:8, :]), pltpu.roll(s_i, 1, 0))
        dar_ref[...] += jnp.sum(g_r * sp_r + g_i * sp_i, axis=0, keepdims=True)
        dai_ref[...] += jnp.sum(g_i * sp_r - g_r * sp_i, axis=0, keepdims=True)
        grb, gib = g_r.astype(BF16), g_i.astype(BF16)
        dcre_ref[0] += lax.dot_general(s_r.astype(BF16), dyb, _TN, preferred_element_type=F32)
        dcim_ref[0] += lax.dot_general(s_i.astype(BF16), dyb, _TN, preferred_element_type=F32)
        dbre_ref[0] += lax.dot_general(ub, grb, _TN, preferred_element_type=F32)
        dbim_ref[0] += lax.dot_general(ub, gib, _TN, preferred_element_type=F32)
        du_ref[...] = (dy * d_ref[...] + lax.dot_general(grb, bre, _NT, preferred_element_type=F32)
                       + lax.dot_general(gib, bim, _NT, preferred_element_type=F32))

    uspec = pl.BlockSpec((T, 128), lambda k, i: (nT - 1 - i, k))
    sspec = pl.BlockSpec((T, WC), lambda k, i: (nT - 1 - i, k))
    pspec = pl.BlockSpec((8, WC), lambda k, i: (jnp.maximum((nT - 1 - i) * (T // 8) - 1, 0), k))
    tab = pl.BlockSpec((T, WC), lambda k, i: (0, k))
    bspec = pl.BlockSpec((1, 128, WC), lambda k, i: (k, 0, 0))
    cspec = pl.BlockSpec((1, WC, 128), lambda k, i: (k, 0, 0))
    return carried(
        body, comm, grid=(8, nT),
        in_specs=[uspec, uspec, uspec, uspec, pl.BlockSpec((1, 128), lambda k, i: (0, k)), sspec, sspec, pspec, pspec,
                  bspec, bspec, cspec, cspec, tab, tab],
        out_specs=[uspec, pl.BlockSpec((1, 128), lambda k, i: (0, k)), bspec, bspec, cspec, cspec,
                   pl.BlockSpec((1, WC), lambda k, i: (0, k)), pl.BlockSpec((1, WC), lambda k, i: (0, k))],
        out_shape=[jax.ShapeDtypeStruct((L, 1024), F32), jax.ShapeDtypeStruct((1, 1024), F32),
                   jax.ShapeDtypeStruct((8, 128, WC), F32), jax.ShapeDtypeStruct((8, 128, WC), F32),
                   jax.ShapeDtypeStruct((8, WC, 128), F32), jax.ShapeDtypeStruct((8, WC, 128), F32),
                   jax.ShapeDtypeStruct((1, 8 * WC), F32), jax.ShapeDtypeStruct((1, 8 * WC), F32)],
        scratch_shapes=[pltpu.VMEM((1, WC), F32), pltpu.VMEM((1, WC), F32)],
        semantics=("parallel", "arbitrary"), name='a_ssm_bwd')(
            proj, dyg1, dyg2, y, d_skip, s_re, s_im, s_re, s_im, Bre, Bim, Cre, Cim, prr, pir)


def s5_discretize(lam_re, lam_im, log_dt, b_re, b_im):
    dt = jnp.exp(log_dt)[:, None]
    mag = jnp.exp(lam_re * dt)
    ab_re = mag * jnp.cos(lam_im * dt)
    ab_im = mag * jnp.sin(lam_im * dt)
    den = lam_re * lam_re + lam_im * lam_im
    nr = ab_re - 1.0
    f_re = (nr * lam_re + ab_im * lam_im) / den
    f_im = (ab_im * lam_re - nr * lam_im) / den
    bb_re = f_re[..., None] * b_re - f_im[..., None] * b_im
    bb_im = f_re[..., None] * b_im + f_im[..., None] * b_re
    return ab_re, ab_im, bb_re, bb_im


_EYE8 = np.eye(8, dtype=np.float32)


def _b_tiles(bb):
    t = bb.transpose(0, 2, 1).reshape(8, 8, SSM_H, SSM_P)
    return jnp.einsum('kghp,gG->kghGp', t, _EYE8).reshape(8, 8 * SSM_H, 8 * SSM_P)


def _b_untile(d):
    t = jnp.einsum('kghGp,gG->kghp', d.reshape(8, 8, SSM_H, 8, SSM_P), _EYE8)
    return t.reshape(SSM_G, SSM_H, SSM_P).transpose(0, 2, 1)


def _c_tiles(c):
    t = c.transpose(0, 2, 1).reshape(8, 8, SSM_P, SSM_H)
    return jnp.einsum('kgph,gG->kgpGh', t, _EYE8).reshape(8, 8 * SSM_P, 8 * SSM_H)


def _c_untile(d):
    t = jnp.einsum('kgpGh,gG->kgph', d.reshape(8, 8, SSM_P, 8, SSM_H), _EYE8)
    return t.reshape(SSM_G, SSM_P, SSM_H).transpose(0, 2, 1)


def s5_powers(ar, ai, T):
    W = ar.shape[1]

    def body(ar_ref, ai_ref, fr_ref, fi_ref, rr_ref, ri_ref):
        fr_ref[0:1, :] = ar_ref[...]
        fi_ref[0:1, :] = ai_ref[...]
        rr_ref[T - 1:T, :] = ar_ref[...]
        ri_ref[T - 1:T, :] = ai_ref[...]
        n = 1
        while n < T:
            cr, ci = fr_ref[0:n, :], fi_ref[0:n, :]
            lr, li = fr_ref[n - 1:n, :], fi_ref[n - 1:n, :]
            fr_ref[n:2 * n, :] = cr * lr - ci * li
            fi_ref[n:2 * n, :] = cr * li + ci * lr
            cr, ci = rr_ref[T - n:T, :], ri_ref[T - n:T, :]
            rr_ref[T - 2 * n:T - n, :] = cr * lr - ci * li
            ri_ref[T - 2 * n:T - n, :] = cr * li + ci * lr
            n *= 2

    spec = pl.BlockSpec((T, SSM_WC), lambda j: (0, j))
    aspec = pl.BlockSpec((1, SSM_WC), lambda j: (0, j))
    return pl.pallas_call(
        body, grid=(W // SSM_WC,), in_specs=[aspec, aspec], out_specs=[spec] * 4,
        out_shape=[jax.ShapeDtypeStruct((T, W), F32)] * 4,
        compiler_params=pltpu.CompilerParams(dimension_semantics=("parallel",)), name='a_powers')(ar, ai)


def layer_a_fwd(h, w, p, comm=None):
    L = h.shape[0]
    proj = mm(h, w['a_w_in'], 'nn', 'a_proj')
    disc = lambda *a: s5_discretize(*a)
    (ab_re, ab_im, bb_re, bb_im), disc_vjp = jax.vjp(disc, p['a_lam_re'][0], p['a_lam_im'][0], p['a_log_dt'][0],
                                                     p['a_b_re'][0], p['a_b_im'][0])
    Bre, Bim = _b_tiles(bb_re), _b_tiles(bb_im)
    Cre, Cim = _c_tiles(p['a_c_re'][0]), -_c_tiles(p['a_c_im'][0])
    T = min(SSM_T, L)
    pr, pi, prr, pir = s5_powers(ab_re.reshape(1, -1), ab_im.reshape(1, -1), T)
    (y, yg, s_re, s_im), carried_out = s5_fwd(proj, p['a_d'], Bre, Bim, Cre, Cim, pr, pi, comm=comm)
    gl = mm(yg, w['a_w_glu'], 'nn', 'a_glu')

    def f2(yg_, gl_, z, bg):
        return [yg_ * sigmoid(gl_ + bg) * silu(z)], []
    (po,), _ = rowwise(f2, [rw(yg), rw(gl), rw(proj, 1024, 1)], [p['a_b_glu']], [(1024, F32)], [], 256, 'a_gate')
    yb = mm(po, w['a_w_out'], 'nn', 'a_out')
    saved = dict(carried=carried_out, h=h, proj=proj, disc_vjp=disc_vjp, Bre=Bre, Bim=Bim, Cre=Cre, Cim=Cim, prr=prr, pir=pir, s_re=s_re,
                 s_im=s_im, y=y, yg=yg, gl=gl, po=po)
    return yb, saved


def layer_a_bwd(dyb, w, p, sv, comm=None):
    g = {}
    dpo = mm(dyb, w['a_w_out'], 'nt', 'a_dpo')
    g['a_w_out'] = mm(sv['po'], dyb, 'tn', 'a_dwout')
    proj = sv['proj']

    def f1(dpo_, yg, gl, z, bg):
        sg = sigmoid(gl + bg)
        sz = silu(z)
        dm = dpo_ * sz
        dz = dpo_ * (yg * sg) * silu_grad(z)
        dgl = dm * yg * sg * (1.0 - sg)
        return [dz, dm * sg, dgl], [jnp.sum(dgl, axis=0, keepdims=True)]
    (dz, dyg1, dgl), (db_glu,) = rowwise(f1, [rw(dpo), rw(sv['yg']), rw(sv['gl']), rw(proj, 1024, 1)], [p['a_b_glu']],
                                          [(1024, F32)] * 3, [(1, 1024)], 256, 'a_gate_bwd')
    g['a_b_glu'] = db_glu
    g['a_w_glu'] = mm(sv['yg'], dgl, 'tn', 'a_dwglu')
    dyg2 = mm(dgl, w['a_w_glu'], 'nt', 'a_dyg2')

    (du, dd, dBre, dBim, dCre, dCim, da_re, da_im), g['carried'] = s5_bwd(
        proj, dyg1, dyg2, sv['y'], p['a_d'], sv['s_re'], sv['s_im'], sv['Bre'], sv['Bim'], sv['Cre'], sv['Cim'],
        sv['prr'], sv['pir'], comm=comm)
    g['a_d'] = dd
    dCim = -dCim

    def f3(du_, dz_):
        return [jnp.concatenate([du_, dz_], axis=1)], []
    (dproj,), _ = rowwise(f3, [rw(du), rw(dz)], [], [(2048, F32)], [], 256, 'a_dproj')
    dlr, dli, dldt, dbr, dbi = sv['disc_vjp']((da_re.reshape(SSM_G, SSM_P), da_im.reshape(SSM_G, SSM_P),
                                               _b_untile(dBre), _b_untile(dBim)))
    g['a_lam_re'], g['a_lam_im'], g['a_log_dt'] = dlr[None], dli[None], dldt[None]
    g['a_b_re'], g['a_b_im'] = dbr[None], dbi[None]
    g['a_c_re'], g['a_c_im'] = _c_untile(dCre)[None], _c_untile(dCim)[None]
    g['a_w_in'] = mm(sv['h'], dproj, 'tn', 'a_dwin')
    dh = mm(dproj, w['a_w_in'], 'nt', 'a_dh')
    return dh, g


def _t5_bucket_np():
    qi = np.arange(WINDOW)[:, None]
    kj = np.arange(2 * WINDOW)[None, :]
    dist = np.maximum(qi + WINDOW - kj, 0)
    max_exact = REL_BUCKETS // 2
    dist_f = np.maximum(dist, 1).astype(np.float32)
    large = max_exact + (np.log(dist_f / np.float32(max_exact)) / np.float32(math.log(REL_MAX_DIST / max_exact))
                         * np.float32(REL_BUCKETS - max_exact)).astype(np.int32)
    large = np.minimum(large, REL_BUCKETS - 1)
    return np.where(dist < max_exact, dist, large).astype(np.int32)


SWA_GRP = SWA_HEADS // SWA_KV


def _swa_kv(kvp, kvc, kvh):
    kb = jnp.concatenate([kvp[:, kvh * 64:(kvh + 1) * 64], kvc[:, kvh * 64:(kvh + 1) * 64]], 0).astype(BF16)
    vb = jnp.concatenate([kvp[:, 128 + kvh * 64:128 + (kvh + 1) * 64], kvc[:, 128 + kvh * 64:128 + (kvh + 1) * 64]],
                         0).astype(BF16)
    return kb, vb


def _swa_stack(x, kvh):
    return jnp.concatenate([x[:, (kvh * SWA_GRP + g) * 64:(kvh * SWA_GRP + g + 1) * 64] for g in range(SWA_GRP)],
                           axis=0).astype(BF16)


def _swa_group(bias_ref, kvh):
    return bias_ref[kvh * SWA_GRP:(kvh + 1) * SWA_GRP].reshape(SWA_GRP * WINDOW, 2 * WINDOW)


def _swa_sinks(sink_ref, kvh):
    return jnp.concatenate([jnp.broadcast_to(sink_ref[0:1, kvh * SWA_GRP + g:kvh * SWA_GRP + g + 1], (WINDOW, 1))
                            for g in range(SWA_GRP)], axis=0)


def _swa_probs(q, kb, bias_h, sink, valid):
    s = lax.dot_general(q, kb, (((1,), (1,)), ((), ())), preferred_element_type=F32) * (HEAD_DIM ** -0.5)
    s = jnp.where(valid, s + bias_h, NEG_INF)
    m = jnp.maximum(jnp.max(s, axis=-1, keepdims=True), sink)
    e = jnp.exp(s - m)
    es = jnp.exp(sink - m)
    den = jnp.sum(e, axis=-1, keepdims=True) + es
    return e / den, es / den


def _swa_valid(n):
    qi = lax.broadcasted_iota(jnp.int32, (SWA_GRP * WINDOW, 2 * WINDOW), 0) & (WINDOW - 1)
    kj = lax.broadcasted_iota(jnp.int32, (SWA_GRP * WINDOW, 2 * WINDOW), 1)
    dist = qi + WINDOW - kj
    return (dist >= 0) & (dist < WINDOW) & ((kj >= WINDOW) | (n > 0))


def swa_fwd(proj, bias, sinks, comm=None):
    L = proj.shape[0]

    def body(z_ref, q_ref, kvc_ref, kvp_ref, bias_ref, sink_ref, o_ref, po_ref):
        n = pl.program_id(0)
        valid = _swa_valid(n)
        q, kvc, kvp = q_ref[...], kvc_ref[...], kvp_ref[...]
        outs = []
        for kvh in range(SWA_KV):
            kb, vb = _swa_kv(kvp, kvc, kvh)
            p, _ = _swa_probs(_swa_stack(q, kvh), kb, _swa_group(bias_ref, kvh), _swa_sinks(sink_ref, kvh), valid)
            o8 = jnp.dot(p.astype(BF16), vb, preferred_element_type=F32)
            outs += [o8[g * WINDOW:(g + 1) * WINDOW] for g in range(SWA_GRP)]
        o = jnp.concatenate(outs, axis=1)
        o_ref[...] = o
        po_ref[...] = o * silu(z_ref[...])

    return carried(
        body, comm, grid=(L // WINDOW,),
        in_specs=[pl.BlockSpec((WINDOW, 1024), lambda n: (n, 0)), pl.BlockSpec((WINDOW, 1024), lambda n: (n, 1)),
                  pl.BlockSpec((WINDOW, 256), lambda n: (n, 8)),
                  pl.BlockSpec((WINDOW, 256), lambda n: (jnp.maximum(n - 1, 0), 8)),
                  pl.BlockSpec((SWA_HEADS, WINDOW, 2 * WINDOW), lambda n: (0, 0, 0)),
                  pl.BlockSpec((1, SWA_HEADS), lambda n: (0, 0))],
        out_specs=[pl.BlockSpec((WINDOW, 1024), lambda n: (n, 0))] * 2,
        out_shape=[jax.ShapeDtypeStruct((L, 1024), F32)] * 2,
        semantics=("parallel",), name='b_attn')(proj, proj, proj, proj, bias, sinks)


def swa_bwd(proj, do, bias, sinks, comm=None):
    L = proj.shape[0]

    def body(q_ref, kvc_ref, kvp_ref, do_ref, bias_ref, sink_ref, dq_ref, dkv_ref, dbias_ref, dsink_ref):
        n = pl.program_id(0)

        @pl.when(n == 0)
        def _():
            dkv_ref[...] = jnp.zeros_like(dkv_ref)
            dbias_ref[...] = jnp.zeros_like(dbias_ref)
            dsink_ref[...] = jnp.zeros_like(dsink_ref)

        valid = _swa_valid(n)
        q, kvc, kvp, do_ = q_ref[...], kvc_ref[...], kvp_ref[...], do_ref[...]
        dqs, dks, dvs, dsk = [], [], [], []
        for kvh in range(SWA_KV):
            kb, vb = _swa_kv(kvp, kvc, kvh)
            q8, do8 = _swa_stack(q, kvh), _swa_stack(do_, kvh)
            p, ps = _swa_probs(q8, kb, _swa_group(bias_ref, kvh), _swa_sinks(sink_ref, kvh), valid)
            dp = lax.dot_general(do8, vb, (((1,), (1,)), ((), ())), preferred_element_type=F32)
            delta = jnp.sum(p * dp, axis=-1, keepdims=True)
            ds = p * (dp - delta)
            col = -ps * delta
            dsk += [jnp.sum(col[g * WINDOW:(g + 1) * WINDOW], axis=0, keepdims=True) for g in range(SWA_GRP)]
            dbias_ref[kvh * SWA_GRP:(kvh + 1) * SWA_GRP] += ds.reshape(SWA_GRP, WINDOW, 2 * WINDOW)
            dsb = (ds * (HEAD_DIM ** -0.5)).astype(BF16)
            dq8 = jnp.dot(dsb, kb, preferred_element_type=F32)
            dqs += [dq8[g * WINDOW:(g + 1) * WINDOW] for g in range(SWA_GRP)]
            dks.append(lax.dot_general(dsb, q8, (((0,), (0,)), ((), ())), preferred_element_type=F32))
            dvs.append(lax.dot_general(p.astype(BF16), do8, (((0,), (0,)), ((), ())), preferred_element_type=F32))
        dq_ref[...] = jnp.concatenate(dqs, axis=1)
        dsink_ref[...] += jnp.concatenate(dsk, axis=1)
        both = jnp.concatenate(dks + dvs, axis=1)
        r_cur = pl.multiple_of(n * WINDOW, WINDOW)
        r_prev = pl.multiple_of(jnp.maximum(n - 1, 0) * WINDOW, WINDOW)
        dkv_ref[pl.ds(r_prev, WINDOW), :] += both[:WINDOW]
        dkv_ref[pl.ds(r_cur, WINDOW), :] += both[WINDOW:]

    return carried(
        body, comm, grid=(L // WINDOW,),
        in_specs=[pl.BlockSpec((WINDOW, 1024), lambda n: (n, 1)), pl.BlockSpec((WINDOW, 256), lambda n: (n, 8)),
                  pl.BlockSpec((WINDOW, 256), lambda n: (jnp.maximum(n - 1, 0), 8)),
                  pl.BlockSpec((WINDOW, 1024), lambda n: (n, 0)),
                  pl.BlockSpec((SWA_HEADS, WINDOW, 2 * WINDOW), lambda n: (0, 0, 0)),
                  pl.BlockSpec((1, SWA_HEADS), lambda n: (0, 0))],
        out_specs=[pl.BlockSpec((WINDOW, 1024), lambda n: (n, 0)), pl.BlockSpec((L, 256), lambda n: (0, 0)),
                   pl.BlockSpec((SWA_HEADS, WINDOW, 2 * WINDOW), lambda n: (0, 0, 0)),
                   pl.BlockSpec((1, SWA_HEADS), lambda n: (0, 0))],
        out_shape=[jax.ShapeDtypeStruct((L, 1024), F32), jax.ShapeDtypeStruct((L, 256), F32),
                   jax.ShapeDtypeStruct((SWA_HEADS, WINDOW, 2 * WINDOW), F32), jax.ShapeDtypeStruct((1, SWA_HEADS), F32)],
        semantics=("arbitrary",), name='b_attn_bwd')(proj, proj, proj, do, bias, sinks)


def swa_bias(rel_bias):
    def body(bk_ref, rb_ref, o_ref):
        bk = bk_ref[...]
        for h in range(SWA_HEADS):
            acc = jnp.zeros((WINDOW, 2 * WINDOW), F32)
            for b in range(REL_BUCKETS):
                acc = jnp.where(bk == b, rb_ref[b, h], acc)
            o_ref[h] = acc

    return pl.pallas_call(
        body, out_shape=jax.ShapeDtypeStruct((SWA_HEADS, WINDOW, 2 * WINDOW), F32),
        in_specs=[pl.BlockSpec(memory_space=pltpu.VMEM), pl.BlockSpec(memory_space=pltpu.SMEM)],
        out_specs=pl.BlockSpec(memory_space=pltpu.VMEM), name='b_bias')(jnp.asarray(_t5_bucket_np()), rel_bias)


def layer_b_fwd(h, w, p, comm=None):
    proj = mm(h, w['b_w_in'], 'nn', 'b_proj')
    bias = swa_bias(p['rel_bias'])
    (o, po), carried_out = swa_fwd(proj, bias, p['b_sinks'], comm=comm)
    yb = mm(po, w['b_w_out'], 'nn', 'b_out')
    return yb, dict(carried=carried_out, h=h, proj=proj, bias=bias, o=o, po=po)


def layer_b_bwd(dyb, w, p, sv, comm=None):
    g = {}
    dpo = mm(dyb, w['b_w_out'], 'nt', 'b_dpo')
    g['b_w_out'] = mm(sv['po'], dyb, 'tn', 'b_dwout')
    proj = sv['proj']

    def f1(dpo_, o, z):
        return [dpo_ * silu(z), dpo_ * o * silu_grad(z)], []
    (do, dz), _ = rowwise(f1, [rw(dpo), rw(sv['o']), rw(proj, 1024, 0)], [], [(1024, F32)] * 2, [], 256, 'b_gate_bwd')
    (dq, dkv, dbias, dsinks), g['carried'] = swa_bwd(proj, do, sv['bias'], p['b_sinks'], comm=comm)
    g['b_sinks'] = dsinks
    onehot = jnp.asarray(np.eye(REL_BUCKETS, dtype=np.float32)[_t5_bucket_np().reshape(-1)])

    def f2(db, oh):
        return [], [lax.dot_general(db, oh, (((1,), (0,)), ((), ())), preferred_element_type=F32,
                                    precision=lax.Precision.HIGHEST)]
    _, (drel,) = rowwise(f2, [(dbias.reshape(SWA_HEADS, -1), pl.BlockSpec((SWA_HEADS, 4096), lambda i: (0, i))),
                              (onehot, pl.BlockSpec((4096, REL_BUCKETS), lambda i: (i, 0)))], [], [],
                         [(SWA_HEADS, REL_BUCKETS)], 4096, 'b_drel', n_steps=(2 * WINDOW * WINDOW) // 4096)
    g['rel_bias'] = drel.T

    def f3(dz_, dq_, dkv_):
        return [jnp.concatenate([dz_, dq_, dkv_], axis=1)], []
    (dproj,), _ = rowwise(f3, [rw(dz), rw(dq), rw(dkv)], [], [(2304, F32)], [], 256, 'b_dproj')
    g['b_w_in'] = mm(sv['h'], dproj, 'tn', 'b_dwin')
    dh = mm(dproj, w['b_w_in'], 'nt', 'b_dh')
    return dh, g


MLA_SCALE = (MLA_NOPE + MLA_ROPE) ** -0.5


def _rope_tables(L):
    inv = ROPE_BASE ** (-jnp.arange(0, MLA_ROPE, 2, dtype=F32) / MLA_ROPE)
    ang = jnp.arange(L, dtype=F32)[:, None] * inv[None, :]
    c, s = jnp.cos(ang), jnp.sin(ang)
    one, zero, pad = jnp.ones((L, 128), F32), jnp.zeros((L, 128), F32), jnp.zeros((L, 64), F32)
    return (jnp.concatenate([one, c, c, c, c, pad], 1), jnp.concatenate([zero, s, s, s, s, pad], 1))


def _rot(x, transpose=False):
    w = x.shape[1]
    lane = lax.broadcasted_iota(jnp.int32, x.shape, 1)
    up = pltpu.roll(x, w - 16, 1)
    dn = pltpu.roll(x, 16, 1)
    first = (lane % 32) < 16
    return jnp.where(first, up, -dn) if transpose else jnp.where(first, -up, dn)


MLA_QT = 512


def _mla_exp(qf, kf, t, qt):
    n_k = kf.shape[0]
    s = lax.dot_general(qf, kf, (((1,), (1,)), ((), ())), preferred_element_type=F32) * MLA_SCALE
    qpos = t * qt + lax.broadcasted_iota(jnp.int32, (qt, n_k), 0)
    kpos = lax.broadcasted_iota(jnp.int32, (qt, n_k), 1)
    s = jnp.where(kpos <= qpos, s, NEG_INF)
    e = jnp.exp(s - jnp.max(s, axis=-1, keepdims=True))
    return e, jnp.sum(e, axis=-1, keepdims=True)


def _mla_heads(q, kv, kr):
    out = []
    for j in range(2):
        qf = jnp.concatenate([q[:, j * 64:(j + 1) * 64], q[:, 128 + j * 32:128 + (j + 1) * 32]], axis=1)
        kf = jnp.concatenate([kv[:, j * 64:(j + 1) * 64], kr], axis=1)
        out.append((qf, kf, kv[:, 128 + j * 64:128 + (j + 1) * 64]))
    return out


def mla_fwd(q, kv, kr, comm=None):
    L = q.shape[0]
    qt = min(MLA_QT, L)
    nq = L // qt

    def body(q_ref, kv_ref, kr_ref, o_ref):
        for t in range(nq):
            @pl.when(pl.program_id(1) == t)
            def _(t=t):
                n_k = (t + 1) * qt
                outs = []
                for qf, kf, v in _mla_heads(q_ref[...], kv_ref[0:n_k, :], kr_ref[0:n_k, 0:MLA_ROPE]):
                    e, den = _mla_exp(qf, kf, t, qt)
                    outs.append(jnp.dot(e.astype(BF16), v, preferred_element_type=F32) / den)
                o_ref[...] = jnp.concatenate(outs, axis=1)

    return carried(
        body, comm, grid=(MLA_HEADS // 2, nq),
        in_specs=[pl.BlockSpec((qt, 256), lambda hp, n: (n, hp)), pl.BlockSpec((L, 256), lambda hp, n: (0, hp)),
                  pl.BlockSpec((L, 128), lambda hp, n: (0, 0))],
        out_specs=pl.BlockSpec((qt, 128), lambda hp, n: (n, hp)), out_shape=jax.ShapeDtypeStruct((L, 1024), F32),
        semantics=("parallel", "parallel"), name='c_attn')(q, kv, kr)


def mla_bwd(q, kv, kr, do, comm=None):
    L = q.shape[0]
    qt = min(MLA_QT, L)
    nq = L // qt

    def body(q_ref, kv_ref, kr_ref, do_ref, dq_ref, dkv_ref, dkr_ref):
        @pl.when(pl.program_id(1) == 0)
        def _():
            dkv_ref[...] = jnp.zeros_like(dkv_ref)
            dkr_ref[...] = jnp.zeros_like(dkr_ref)

        for t in range(nq):
            @pl.when(pl.program_id(1) == t)
            def _(t=t):
                n_k = (t + 1) * qt
                do_ = do_ref[...]
                dqn, dqr, dkn, dvs = [], [], [], []
                dkr = jnp.zeros((n_k, MLA_ROPE), F32)
                for j, (qf, kf, v) in enumerate(_mla_heads(q_ref[...], kv_ref[0:n_k, :], kr_ref[0:n_k, 0:MLA_ROPE])):
                    doh = do_[:, j * 64:(j + 1) * 64]
                    e, den = _mla_exp(qf, kf, t, qt)
                    p = e * (1.0 / den)
                    dp = lax.dot_general(doh, v, (((1,), (1,)), ((), ())), preferred_element_type=F32)
                    ds = (p * (dp - jnp.sum(p * dp, axis=-1, keepdims=True)) * MLA_SCALE).astype(BF16)
                    dqf = jnp.dot(ds, kf, preferred_element_type=F32)
                    dkf = lax.dot_general(ds, qf, (((0,), (0,)), ((), ())), preferred_element_type=F32)
                    dvs.append(lax.dot_general(p.astype(BF16), doh, (((0,), (0,)), ((), ())), preferred_element_type=F32))
                    dqn.append(dqf[:, :MLA_NOPE])
                    dqr.append(dqf[:, MLA_NOPE:])
                    dkn.append(dkf[:, :MLA_NOPE])
                    dkr = dkr + dkf[:, MLA_NOPE:]
                dq_ref[...] = jnp.concatenate(dqn + dqr + [jnp.zeros((qt, 64), F32)], axis=1)
                dkv_ref[0:n_k, :] += jnp.concatenate(dkn + dvs, axis=1)
                dkr_ref[0, 0:n_k, :] += jnp.concatenate([dkr, jnp.zeros((n_k, 128 - MLA_ROPE), F32)], axis=1)

    return carried(
        body, comm, grid=(MLA_HEADS // 2, nq),
        in_specs=[pl.BlockSpec((qt, 256), lambda hp, n: (n, hp)), pl.BlockSpec((L, 256), lambda hp, n: (0, hp)),
                  pl.BlockSpec((L, 128), lambda hp, n: (0, 0)), pl.BlockSpec((qt, 128), lambda hp, n: (n, hp))],
        out_specs=[pl.BlockSpec((qt, 256), lambda hp, n: (n, hp)), pl.BlockSpec((L, 256), lambda hp, n: (0, hp)),
                   pl.BlockSpec((1, L, 128), lambda hp, n: (hp, 0, 0))],
        out_shape=[jax.ShapeDtypeStruct((L, 2048), F32), jax.ShapeDtypeStruct((L, 2048), F32),
                   jax.ShapeDtypeStruct((MLA_HEADS // 2, L, 128), F32)],
        semantics=("parallel", "arbitrary"), name='c_attn_bwd')(q, kv, kr, do)


def layer_c_fwd(h, w, p, comm=None):
    L = h.shape[0]
    proj = mm(h, w['c_w_in'], 'nn', 'c_proj')

    def f1(c, gq, gk):
        return [rms_fwd(c[:, :768], gq), rms_fwd(c[:, 768:], gk)], []
    (cqn, ckvn), _ = rowwise(f1, [rw(proj, 1024, 1)], [p['c_q_norm'], p['c_kv_norm']], [(768, BF16), (256, BF16)], [],
                             256, 'c_norms')
    qf = mm(cqn, w['c_w_uq'], 'nn', 'c_uq')
    kvf = mm(ckvn, w['c_w_ukv'], 'nn', 'c_ukv', out_dtype=BF16)
    cos, sin = _rope_tables(L)

    def f2(q_, kr_, c, s):
        c8, s8 = jnp.tile(c, (1, 8)), jnp.tile(s, (1, 8))
        return [q_ * c8 + _rot(q_) * s8, kr_ * c[:, 128:] + _rot(kr_) * s[:, 128:]], []
    (q, kr), _ = rowwise(f2, [rw(qf), rw(proj, 128, 16), rw(cos), rw(sin)], [], [(2048, BF16), (128, BF16)], [], 256,
                         'c_rope')
    o, carried_out = mla_fwd(q, kvf, kr, comm=comm)

    def f3(o_, z):
        return [o_ * silu(z)], []
    (po,), _ = rowwise(f3, [rw(o), rw(proj, 1024, 0)], [], [(1024, F32)], [], 256, 'c_gate')
    yb = mm(po, w['c_w_out'], 'nn', 'c_out')
    return yb, dict(carried=carried_out, h=h, proj=proj, cqn=cqn, ckvn=ckvn, q=q, kv=kvf, kr=kr, o=o, po=po, cos=cos, sin=sin)


def layer_c_bwd(dyb, w, p, sv, comm=None):
    g = {}
    dpo = mm(dyb, w['c_w_out'], 'nt', 'c_dpo')
    g['c_w_out'] = mm(sv['po'], dyb, 'tn', 'c_dwout')
    proj = sv['proj']
    L = proj.shape[0]

    def f1(dpo_, o, z):
        return [dpo_ * silu(z), dpo_ * o * silu_grad(z)], []
    (do, dz), _ = rowwise(f1, [rw(dpo), rw(sv['o']), rw(proj, 1024, 0)], [], [(1024, BF16), (1024, F32)], [], 256,
                          'c_gate_bwd')
    (dq, dkvf, dkr8), g['carried'] = mla_bwd(sv['q'], sv['kv'], sv['kr'], do, comm=comm)

    def f2(dq_, dkr_, c, s):
        c8, s8 = jnp.tile(c, (1, 8)), jnp.tile(s, (1, 8))
        dk = jnp.sum(dkr_, axis=0)
        return [dq_ * c8 + _rot(dq_ * s8, True), dk * c[:, 128:] + _rot(dk * s[:, 128:], True)], []
    tl = 256
    (dqf, dkr), _ = rowwise(f2, [rw(dq), (dkr8, pl.BlockSpec((8, tl, 128), lambda i: (0, i, 0))), rw(sv['cos']),
                                 rw(sv['sin'])], [], [(2048, F32), (128, F32)], [], tl, 'c_rope_bwd')
    g['c_w_uq'] = mm(sv['cqn'], dqf, 'tn', 'c_dwuq')
    g['c_w_ukv'] = mm(sv['ckvn'], dkvf, 'tn', 'c_dwukv')
    dcqn = mm(dqf, w['c_w_uq'], 'nt', 'c_dcqn')
    dckvn = mm(dkvf, w['c_w_ukv'], 'nt', 'c_dckvn')

    def f3(c, dq_, dk_, dz_, dkr_, gq, gk):
        dcq, dgq = rms_bwd(c[:, :768], gq, dq_)
        dckv, dgk = rms_bwd(c[:, 768:], gk, dk_)
        return [jnp.concatenate([dz_, dcq, dckv, dkr_], axis=1)], [dgq, dgk]
    (dproj,), (dgq, dgk) = rowwise(f3, [rw(proj, 1024, 1), rw(dcqn), rw(dckvn), rw(dz), rw(dkr)],
                                   [p['c_q_norm'], p['c_kv_norm']], [(2176, F32)], [(1, 768), (1, 256)], 256, 'c_dproj')
    g['c_q_norm'], g['c_kv_norm'] = dgq, dgk
    g['c_w_in'] = mm(sv['h'], dproj, 'tn', 'c_dwin')
    dh = mm(dproj, w['c_w_in'], 'nt', 'c_dh')
    return dh, g


def _sgu_mix(wm, v, transpose):
    outs = []
    dims = (((0,), (0,)), ((), ())) if transpose else (((1,), (0,)), ((), ()))
    for gi in range(SGU_G):
        outs.append(lax.dot_general(wm[gi], v[:, gi * SGU_C:(gi + 1) * SGU_C].astype(BF16), dims,
                                    preferred_element_type=F32))
    return jnp.concatenate(outs, axis=1)


def _sgu_wmask(ws):
    t = lax.broadcasted_iota(jnp.int32, (SGU_T, SGU_T), 0)
    s = lax.broadcasted_iota(jnp.int32, (SGU_T, SGU_T), 1)
    return jnp.where((s <= t)[None], ws, 0.0).astype(BF16)


def _ln_stats(v):
    mu = jnp.mean(v, axis=-1, keepdims=True)
    vc = v - mu
    rstd = lax.rsqrt(jnp.mean(vc * vc, axis=-1, keepdims=True) + EPS)
    return vc * rstd, rstd


def layer_d_fwd(h, w, p):
    proj = mm(h, w['d_w_in'], 'nn', 'd_proj')
    bias = jnp.repeat(p['d_b_s'][0].T, SGU_C, axis=1)

    def f1(u_, v_, z, ws, lg, lb, bs):
        xh, _ = _ln_stats(gelu(v_))
        s = _sgu_mix(_sgu_wmask(ws), xh * lg + lb, False) + bs
        return [gelu(u_) * s * silu(z)], []
    (po,), _ = rowwise(f1, [rw(proj, 1024, 0), rw(proj, 1024, 1), rw(proj, 1024, 2)],
                       [p['d_w_s'][0], p['d_ln_g'], p['d_ln_b'], bias], [(1024, F32)], [], SGU_T, 'd_mix')
    yb = mm(po, w['d_w_out'], 'nn', 'd_out')
    return yb, dict(h=h, proj=proj, po=po, bias=bias)


def layer_d_bwd(dyb, w, p, sv):
    g = {}
    dpo = mm(dyb, w['d_w_out'], 'nt', 'd_dpo')
    g['d_w_out'] = mm(sv['po'], dyb, 'tn', 'd_dwout')
    proj = sv['proj']

    def f1(dpo_, u_, v_, z, ws, lg, lb, bs):
        wm = _sgu_wmask(ws)
        gv = gelu(v_)
        xh, rstd = _ln_stats(gv)
        vn = xh * lg + lb
        s = _sgu_mix(wm, vn, False) + bs
        gu, sz = gelu(u_), silu(z)
        du = dpo_ * s * sz
        ds = dpo_ * gu * sz
        dz = dpo_ * gu * s * silu_grad(z)
        dsb = ds.astype(BF16)
        dws = jnp.stack([lax.dot_general(dsb[:, gi * SGU_C:(gi + 1) * SGU_C], vn[:, gi * SGU_C:(gi + 1) * SGU_C].astype(BF16),
                                         (((1,), (1,)), ((), ())), preferred_element_type=F32) for gi in range(SGU_G)])
        dvn = _sgu_mix(wm, ds, True)
        dlg = jnp.sum(dvn * xh, axis=0, keepdims=True)
        dlb = jnp.sum(dvn, axis=0, keepdims=True)
        dxh = dvn * lg
        dgv = rstd * (dxh - jnp.mean(dxh, axis=-1, keepdims=True) - xh * jnp.mean(dxh * xh, axis=-1, keepdims=True))
        return ([jnp.concatenate([du * gelu_grad(u_), dgv * gelu_grad(v_), dz], axis=1)], [dws, ds, dlg, dlb])
    (dproj,), (dws, dbs, dlg, dlb) = rowwise(
        f1, [rw(dpo), rw(proj, 1024, 0), rw(proj, 1024, 1), rw(proj, 1024, 2)],
        [p['d_w_s'][0], p['d_ln_g'], p['d_ln_b'], sv['bias']], [(3072, F32)],
        [(SGU_G, SGU_T, SGU_T), (SGU_T, 1024), (1, 1024), (1, 1024)], SGU_T, 'd_mix_bwd')
    tril = np.tril(np.ones((SGU_T, SGU_T), dtype=bool))
    g['d_w_s'] = jnp.where(tril[None], dws, 0.0)[None]
    g['d_b_s'] = dbs.reshape(SGU_T, SGU_G, SGU_C).sum(-1).T[None]
    g['d_ln_g'], g['d_ln_b'] = dlg, dlb
    g['d_w_in'] = mm(sv['h'], dproj, 'tn', 'd_dwin')
    dh = mm(dproj, w['d_w_in'], 'nt', 'd_dh')
    return dh, g


def _coords():
    return lax.axis_index("x"), lax.axis_index("y"), lax.axis_index("c")


class AllGather:
    def __init__(self, x):
        self.ins = [x]
        self.outs = [jax.ShapeDtypeStruct((N_DEV,) + x.shape, x.dtype)]
        self.scratch = [pltpu.SemaphoreType.DMA((7,)), pltpu.SemaphoreType.DMA((7,)), pltpu.SemaphoreType.DMA(())]

    def hooks(self, n_steps):
        return [(0, functools.partial(self.phase, 0), False), ((n_steps * 5) // 8, functools.partial(self.phase, 1), False),
                (n_steps - 1, functools.partial(self.phase, 2), True)]

    @staticmethod
    def phase(which, ins, outs, scratch):
        (x_ref,), (out_ref,), (send_sems, recv_sems, local_sem) = ins, outs, scratch
        x_, y_, c_ = _coords()
        me, sibling = (x_, y_, c_), (x_, y_, 1 - c_)
        chips = [(1 - x_, y_), (x_, 1 - y_), (1 - x_, 1 - y_)]

        def slot(px, py, pc):
            return out_ref.at[4 * px + 2 * py + pc]

        def copy(k, block, to, src=None):
            return pltpu.make_async_remote_copy(src_ref=slot(*block) if src is None else src, dst_ref=slot(*block),
                                                send_sem=send_sems.at[k], recv_sem=recv_sems.at[k], device_id=to,
                                                device_id_type=MESH)

        mine = pltpu.make_async_copy(x_ref, slot(*me), local_sem)
        first = [copy(0, me, sibling, src=x_ref)]
        first += [copy(1 + j, me, (*chip, c_), src=x_ref) for j, chip in enumerate(chips)]
        passed = [copy(4 + j, (*chip, c_), sibling) for j, chip in enumerate(chips)]
        if which == 0:
            mine.start()
            for cp in first:
                cp.start()
        elif which == 1:
            for j, chip in enumerate(chips):
                copy(1 + j, (*chip, c_), me).wait_recv()
                passed[j].start()
        else:
            copy(0, sibling, me).wait_recv()
            for j, chip in enumerate(chips):
                copy(4 + j, (*chip, 1 - c_), me).wait_recv()
            for cp in first + passed:
                cp.wait_send()
            mine.wait()


class ChipExchange:
    def __init__(self, part):
        self.ins = [part]
        self.outs = [jax.ShapeDtypeStruct((3,) + part.shape[1:], part.dtype)]
        self.scratch = [pltpu.SemaphoreType.DMA((3,)), pltpu.SemaphoreType.DMA((3,))]

    def hooks(self, n_steps):
        return [(0, functools.partial(self.phase, 0), False), (n_steps - 1, functools.partial(self.phase, 1), True)]

    @staticmethod
    def phase(which, ins, outs, scratch):
        (p_ref,), (land_ref,), (send_sems, recv_sems) = ins, outs, scratch
        x_, y_, c_ = _coords()
        copies = []
        for r, (fx, fy) in enumerate([(1, 0), (0, 1), (1, 1)]):
            tx = jnp.where(fx == 1, 1 - x_, x_)
            ty = jnp.where(fy == 1, 1 - y_, y_)
            copies.append(pltpu.make_async_remote_copy(src_ref=p_ref.at[2 * tx + ty], dst_ref=land_ref.at[r],
                                                       send_sem=send_sems.at[r], recv_sem=recv_sems.at[r],
                                                       device_id=(tx, ty, c_), device_id_type=MESH))
        if which == 0:
            for cp in copies:
                cp.start()
        else:
            for cp in copies:
                cp.wait_recv()
            for cp in copies:
                cp.wait_send()


class Both:
    def __init__(self, a, b):
        self.parts = (a, b)
        self.ins, self.outs, self.scratch = a.ins + b.ins, a.outs + b.outs, a.scratch + b.scratch

    def hooks(self, n_steps):
        res, oi, oo, osc = [], 0, 0, 0
        for p in self.parts:
            sl = (slice(oi, oi + len(p.ins)), slice(oo, oo + len(p.outs)), slice(osc, osc + len(p.scratch)))
            res += [(at, functools.partial(self.sub, fn, sl), after) for at, fn, after in p.hooks(n_steps)]
            oi, oo, osc = oi + len(p.ins), oo + len(p.outs), osc + len(p.scratch)
        return res

    @staticmethod
    def sub(fn, sl, ins, outs, scratch):
        fn(ins[sl[0]], outs[sl[1]], scratch[sl[2]])


def run_comm(comm, name):
    def body(*refs):
        ci, co = len(comm.ins), len(comm.outs)
        for _, fn, _ in comm.hooks(1):
            fn(refs[:ci], refs[ci:ci + co], refs[ci + co:])

    return pl.pallas_call(body, out_shape=list(comm.outs), in_specs=[ANY] * len(comm.ins),
                          out_specs=[ANY] * len(comm.outs), scratch_shapes=list(comm.scratch), name=name)(*comm.ins)


def all_gather(x, name):
    return run_comm(AllGather(x), name)[0]


def rs_sibling(gfull, tag):
    _, R, C = gfull.shape

    def body(g_ref, land_ref, send_sems, recv_sems):
        x_, y_, c_ = _coords()
        copies = []
        for k in range(4):
            cp = pltpu.make_async_remote_copy(src_ref=g_ref.at[2 * k + 1 - c_], dst_ref=land_ref.at[k],
                                              send_sem=send_sems.at[k], recv_sem=recv_sems.at[k],
                                              device_id=(x_, y_, 1 - c_), device_id_type=MESH)
            cp.start()
            copies.append(cp)
        for cp in copies:
            cp.wait_recv()
        for cp in copies:
            cp.wait_send()

    return pl.pallas_call(
        body, out_shape=jax.ShapeDtypeStruct((4, R, C), gfull.dtype), in_specs=[ANY], out_specs=ANY,
        scratch_shapes=[pltpu.SemaphoreType.DMA((4,)), pltpu.SemaphoreType.DMA((4,))], name='rs_sibling_' + tag)(gfull)


def rs_pair_add(gfull, land, core, tag):
    _, R, C = gfull.shape
    tl = R

    def body(c_ref, g_ref, l_ref, o_ref):
        o_ref[...] = (g_ref[...] + l_ref[...]).astype(BF16)

    return pl.pallas_call(
        body, out_shape=jax.ShapeDtypeStruct((4, R, C), BF16),
        grid_spec=pltpu.PrefetchScalarGridSpec(
            num_scalar_prefetch=1, grid=(4, R // tl),
            in_specs=[pl.BlockSpec((1, tl, C), lambda k, i, c: (2 * k + c[0], i, 0)),
                      pl.BlockSpec((1, tl, C), lambda k, i, c: (k, i, 0))],
            out_specs=pl.BlockSpec((1, tl, C), lambda k, i, c: (k, i, 0))),
        compiler_params=pltpu.CompilerParams(dimension_semantics=("parallel", "parallel")), name='rs_pair_add_' + tag)(
            core, gfull, land)


def rs_chips(part, tag):
    return run_comm(ChipExchange(part), 'rs_chips_' + tag)[0]


def _adam(wv, gv, mv, vv):
    m = ADAM_B1 * mv + (1.0 - ADAM_B1) * gv
    v = ADAM_B2 * vv + (1.0 - ADAM_B2) * (gv * gv)
    m_hat = m / (1.0 - ADAM_B1 ** ADAM_STEP)
    v_hat = v / (1.0 - ADAM_B2 ** ADAM_STEP)
    delta = -ADAM_LR * (m_hat / (jnp.sqrt(v_hat) + ADAM_EPS) + ADAM_WD * wv)
    return delta, m, v


def _sum4(p_ref, l_ref):
    return ((p_ref[0].astype(F32) + l_ref[0].astype(F32)) + l_ref[1].astype(F32)) + l_ref[2].astype(F32)


def rs_rep_sum(part, land, chip):
    def body(c_ref, p_ref, l_ref, o_ref):
        o_ref[...] = _sum4(p_ref, l_ref)

    return pl.pallas_call(
        body, out_shape=jax.ShapeDtypeStruct((REP_SLOT, LANES), F32),
        grid_spec=pltpu.PrefetchScalarGridSpec(
            num_scalar_prefetch=1, grid=(1,),
            in_specs=[pl.BlockSpec((1, REP_SLOT, LANES), lambda i, c: (c[0], 0, 0)),
                      pl.BlockSpec((3, REP_SLOT, LANES), lambda i, c: (0, 0, 0))],
            out_specs=pl.BlockSpec((REP_SLOT, LANES), lambda i, c: (0, 0))),
        compiler_params=pltpu.CompilerParams(dimension_semantics=("parallel",)), name='rs_rep')(chip, part, land)


def adam_param(name, shape, off, w, m, v, chip, part=None, land=None, grep=None):
    r, c = shape
    rp, nt, rb = _tiles(shape)
    rbw = min(r, rb)
    n_src = 2 if grep is None else 1
    ns = w.shape
    assert int(np.prod(ns[:-1])) == r and ns[-1] == c
    if len(ns) == 2:
        nat_block, nat_map = (rbw, c), lambda i, cr: (i, 0)
    elif int(np.prod(ns[:-2])) == 1:
        nat_block, nat_map = (1,) * (len(ns) - 2) + (rbw, c), lambda i, cr: (0,) * (len(ns) - 2) + (i, 0)
    else:
        assert len(ns) == 4 and ns[0] == 1 and rbw % ns[2] == 0
        nat_block, nat_map = (1, rbw // ns[2], ns[2], c), lambda i, cr: (0, i, 0, 0)

    def body(c_ref, *refs):
        srcs = refs[:n_src * nt]
        w_ref, m_ref, v_ref, g_ref, d_ref, nm_ref, nv_ref = refs[n_src * nt:]
        if grep is None:
            tiles = [_sum4(srcs[2 * t], srcs[2 * t + 1]) for t in range(nt)]
        else:
            tiles = [srcs[t][...] for t in range(nt)]
        g = (tiles[0] if nt == 1 else jnp.concatenate(tiles, axis=1))[:rbw, :c]
        g_ref[...] = g.reshape(nat_block)
        res = _adam(w_ref[...].reshape(rbw, c), g, m_ref[...].reshape(rbw, c), v_ref[...].reshape(rbw, c))
        for ref, val in zip((d_ref, nm_ref, nv_ref), res):
            ref[...] = val.reshape(nat_block)

    in_specs, args = [], []
    for t in range(nt):
        b0 = (off + t * rp) // rb
        assert (off + t * rp) % rb == 0
        if grep is None:
            in_specs += [pl.BlockSpec((1, rb, LANES), functools.partial(lambda i, cr, b0: (cr[0], b0 + i, 0), b0=b0)),
                         pl.BlockSpec((3, rb, LANES), functools.partial(lambda i, cr, b0: (0, b0 + i, 0), b0=b0))]
            args += [part, land]
        else:
            in_specs.append(pl.BlockSpec((rb, LANES), functools.partial(lambda i, cr, b0: (b0 + i, 0), b0=b0)))
            args.append(grep)
    nat = pl.BlockSpec(nat_block, nat_map)
    return pl.pallas_call(
        body, out_shape=[jax.ShapeDtypeStruct(ns, F32)] * 4,
        grid_spec=pltpu.PrefetchScalarGridSpec(num_scalar_prefetch=1, grid=(rp // rb,), in_specs=in_specs + [nat] * 3,
                                               out_specs=[nat] * 4),
        compiler_params=pltpu.CompilerParams(dimension_semantics=("parallel",)), name='adam_' + name)(
            chip, *args, w, m, v)


VM = pl.BlockSpec(memory_space=pltpu.VMEM)


def _tile_value(w, t, rp):
    r, c = w.shape
    wt = min(LANES, c - t * LANES)
    tile = w[:, t * LANES:t * LANES + wt]
    if wt < LANES:
        tile = jnp.concatenate([tile, jnp.zeros((r, LANES - wt), tile.dtype)], axis=1)
    if rp > r:
        tile = jnp.concatenate([tile, jnp.zeros((rp - r, LANES), tile.dtype)], axis=0)
    return tile


def pack_layer(layer, blocks):
    names = LAYER_PARAMS[layer]

    def body(*refs):
        tiles = []
        for ref, n in zip(refs[:-1], names):
            rp, nt, _ = _tiles(_block_shape(n))
            w = ref[...].reshape(_block_shape(n))
            tiles += [_tile_value(w, t, rp) for t in range(nt)]
        refs[-1][...] = jnp.concatenate(tiles, axis=0).astype(BF16)

    return pl.pallas_call(body, out_shape=jax.ShapeDtypeStruct((LAYER_ROWS[layer], LANES), BF16),
                          in_specs=[VM] * len(names), out_specs=VM, name='pack_' + layer)(*[blocks[n] for n in names])


def assemble(name, gathered):
    (rf, cf), ax = SHARDED[name]
    r, c = _block_shape(name)
    rp, nt, _ = _tiles((r, c))
    off = SH_OFF[name]
    out_cols = cf if ax == 0 else len(perm_index(name))

    def body(g_ref, o_ref, buf, sem):
        cp = pltpu.make_async_copy(g_ref.at[:, pl.ds(off, nt * rp), :], buf, sem)
        cp.start()
        cp.wait()
        if ax == 0:
            for j in range(N_DEV):
                o_ref[j * r:(j + 1) * r, :] = jnp.concatenate([buf[j, t * rp:(t + 1) * rp, :] for t in range(nt)], axis=1)
            return
        pieces = []
        for p in PERM[name]:
            if p[0] == 'z':
                pieces.append(jnp.zeros((r, p[1]), BF16))
                continue
            n0, w = p
            while w > 0:
                j, cb = divmod(n0, c)
                t, lane = divmod(cb, LANES)
                wl = min(w, LANES - lane, c - cb)
                pieces.append(buf[j, t * rp:t * rp + r, lane:lane + wl])
                n0, w = n0 + wl, w - wl
        o_ref[...] = jnp.concatenate(pieces, axis=1)

    return pl.pallas_call(
        body, out_shape=jax.ShapeDtypeStruct((rf, out_cols), BF16), in_specs=[ANY], out_specs=VM,
        scratch_shapes=[pltpu.VMEM((N_DEV, nt * rp, LANES), BF16), pltpu.SemaphoreType.DMA(())], name='asm_' + name)(
            gathered)


def chunk_grad(layer, name, dw, gfull):
    (rf, cf), ax = SHARDED[name]
    r, c = _block_shape(name)
    rp, nt, _ = _tiles((r, c))
    off = SH_OFF[name]
    if ax == 1:
        idx = perm_index(name) if name in PERM else np.arange(cf)
        inv = np.full(cf, -1)
        inv[idx[idx >= 0]] = np.nonzero(idx >= 0)[0]

    def body(*refs):
        dw_ref, o_ref, buf, sem = refs[0], refs[-3], refs[-2], refs[-1]
        for j in range(N_DEV):
            for t in range(nt):
                if ax == 0:
                    tile = dw_ref[j * r:(j + 1) * r, t * LANES:(t + 1) * LANES]
                else:
                    cols = inv[j * c + t * LANES:j * c + min((t + 1) * LANES, c)]
                    cuts = [0] + [k for k in range(1, len(cols)) if cols[k] != cols[k - 1] + 1] + [len(cols)]
                    pieces = [dw_ref[:, int(cols[a]):int(cols[b - 1]) + 1] for a, b in zip(cuts[:-1], cuts[1:])]
                    if len(cols) < LANES:
                        pieces.append(jnp.zeros((r, LANES - len(cols)), F32))
                    tile = pieces[0] if len(pieces) == 1 else jnp.concatenate(pieces, axis=1)
                    if rp > r:
                        tile = jnp.concatenate([tile, jnp.zeros((rp - r, LANES), F32)], axis=0)
                buf[j, t * rp:(t + 1) * rp, :] = tile
        cp = pltpu.make_async_copy(buf, o_ref.at[:, pl.ds(off, nt * rp), :], sem)
        cp.start()
        cp.wait()

    shape = jax.ShapeDtypeStruct((N_DEV, LAYER_ROWS[layer], LANES), F32)
    scratch = [pltpu.VMEM((N_DEV, nt * rp, LANES), F32), pltpu.SemaphoreType.DMA(())]
    if gfull is None:
        return pl.pallas_call(body, out_shape=shape, in_specs=[VM], out_specs=ANY, scratch_shapes=scratch,
                              name='chunk_' + name)(dw)
    return pl.pallas_call(body, out_shape=shape, in_specs=[VM, ANY], out_specs=ANY, scratch_shapes=scratch,
                          input_output_aliases={1: 0}, name='chunk_' + name)(dw, gfull)


def pack_rep(G):
    def body(*refs):
        tiles = []
        for ref, s in zip(refs[:-1], REP_SHAPE.values()):
            rp, nt, _ = _tiles(s)
            g = ref[...]
            tiles += [_tile_value(g, t, rp) for t in range(nt)]
        full = jnp.concatenate(tiles, axis=0)
        for j in range(N_DEV):
            refs[-1][j, 0:REP_CHUNK, :] = full[j * REP_CHUNK:(j + 1) * REP_CHUNK]
            if REP_SLOT > REP_CHUNK:
                refs[-1][j, REP_CHUNK:REP_SLOT, :] = jnp.zeros((REP_SLOT - REP_CHUNK, LANES), F32)

    return pl.pallas_call(body, out_shape=jax.ShapeDtypeStruct((N_DEV, REP_SLOT, LANES), F32),
                          in_specs=[VM] * len(REP_SHAPE), out_specs=VM, name='pack_rep')(
                              *[G[n].reshape(s) for n, s in REP_SHAPE.items()])


def _pack_small(blocks, order, rows, width, dtype):
    flat = jnp.concatenate([blocks[n].reshape(-1).astype(dtype) for n in order])
    return jnp.pad(flat, (0, rows * width - flat.shape[0])).reshape(rows, width)


def kernel(x, pre_norm, post_norm, rel_bias, a_w_in, a_lam_re, a_lam_im, a_log_dt, a_b_re, a_b_im, a_c_re, a_c_im, a_d, a_w_glu, a_b_glu, a_w_out, b_w_in, b_sinks, b_w_out, c_w_in, c_q_norm, c_kv_norm, c_w_uq, c_w_ukv, c_w_out, d_w_in, d_ln_g, d_ln_b, d_w_s, d_b_s, d_w_out, loss_target, m_pre_norm, m_post_norm, m_rel_bias, m_a_w_in, m_a_lam_re, m_a_lam_im, m_a_log_dt, m_a_b_re, m_a_b_im, m_a_c_re, m_a_c_im, m_a_d, m_a_w_glu, m_a_b_glu, m_a_w_out, m_b_w_in, m_b_sinks, m_b_w_out, m_c_w_in, m_c_q_norm, m_c_kv_norm, m_c_w_uq, m_c_w_ukv, m_c_w_out, m_d_w_in, m_d_ln_g, m_d_ln_b, m_d_w_s, m_d_b_s, m_d_w_out, v_pre_norm, v_post_norm, v_rel_bias, v_a_w_in, v_a_lam_re, v_a_lam_im, v_a_log_dt, v_a_b_re, v_a_b_im, v_a_c_re, v_a_c_im, v_a_d, v_a_w_glu, v_a_b_glu, v_a_w_out, v_b_w_in, v_b_sinks, v_b_w_out, v_c_w_in, v_c_q_norm, v_c_kv_norm, v_c_w_uq, v_c_w_ukv, v_c_w_out, v_d_w_in, v_d_ln_g, v_d_ln_b, v_d_w_s, v_d_b_s, v_d_w_out):
    loc = locals()
    P = {n: loc[n] for n in WEIGHTS}
    M = {n: loc['m_' + n] for n in WEIGHTS}
    V = {n: loc['v_' + n] for n in WEIGHTS}
    xs = x[0]
    L = xs.shape[0]

    blocks = {n: P[n].reshape(_block_shape(n)) for n in SHARDED}
    packed = {layer: pack_layer(layer, P) for layer in LAYER_PARAMS}
    W = {}

    def assemble_layer(layer, gathered):
        for n in LAYER_PARAMS[layer]:
            if n not in SHARDED_F32:
                W[n] = assemble(n, gathered)

    assemble_layer('a', all_gather(packed['a'], 'ag_a'))
    small = all_gather(_pack_small(blocks, SHARDED_F32, SMALL_ROWS, 128, F32), 'ag_small')
    Pl = dict(P)
    for n in SHARDED_F32:
        c = SHARDED[n][0][1]
        bc = c // N_DEV
        Pl[n] = small.reshape(N_DEV, -1)[:, SMALL_OFF[n]:SMALL_OFF[n] + bc].reshape(1, c)
    cx, cy, cc = _coords()
    core = jnp.reshape(cc, (1,)).astype(jnp.int32)
    chip = jnp.reshape(2 * cx + cy, (1,)).astype(jnp.int32)

    def pair_sums(gfull, tag):
        return rs_pair_add(gfull, rs_sibling(gfull, tag), core, tag)

    fwd = [layer_a_fwd, layer_b_fwd, layer_c_fwd, layer_d_fwd]
    bwd = [layer_a_bwd, layer_b_bwd, layer_c_bwd, layer_d_bwd]
    saved = []
    xc = xs
    for i in range(4):
        def fpre(x_, g_):
            return [rms_fwd(x_, g_)], []
        (h,), _ = rowwise(fpre, [rw(xc)], [P['pre_norm'][i:i + 1]], [(D_MODEL, F32)], [], 256, f'pre_norm{i}')
        if i == 0:
            yb, sv = fwd[i](h, W, Pl, comm=Both(AllGather(packed['b']), AllGather(packed['c'])))
            assemble_layer('b', sv['carried'][0])
            assemble_layer('c', sv['carried'][1])
        elif i == 1:
            yb, sv = fwd[i](h, W, Pl, comm=AllGather(packed['d']))
            assemble_layer('d', sv['carried'][0])
        else:
            yb, sv = fwd[i](h, W, Pl)

        def fpost(x_, y_, g_):
            return [x_ + rms_fwd(y_, g_)], []
        (xn,), _ = rowwise(fpost, [rw(xc), rw(yb)], [P['post_norm'][i:i + 1]], [(D_MODEL, F32)], [], 256, f'post_norm{i}')
        sv['x'], sv['yb'] = xc, yb
        saved.append(sv)
        xc = xn

    def floss(y_, t_):
        d = y_ - t_
        return [d * (1.0 / D_MODEL)], [0.5 * jnp.sum(jnp.sum(d * d, axis=-1, keepdims=True) * (1.0 / D_MODEL), axis=0,
                                                      keepdims=True)]
    (dx,), (loss_loc,) = rowwise(floss, [rw(xc), rw(loss_target[0])], [], [(D_MODEL, F32)], [(1, 1)], 256, 'loss')
    loss = lax.psum(loss_loc[0, 0], ("x", "y", "c"))

    G, out = {}, {}
    dpre, dpost = [None] * 4, [None] * 4

    def adam_layer(layer, part, land2):
        for n in LAYER_PARAMS[layer]:
            s = _block_shape(n)
            out[n] = adam_param(n, s, SH_OFF[n], P[n], M[n], V[n], chip, part=part, land=land2)

    pending = None
    for i in reversed(range(4)):
        sv = saved[i]

        def fpost_b(y_, d_, g_):
            dy, dg = rms_bwd(y_, g_, d_)
            return [dy], [dg]
        (dyb,), (dpost[i],) = rowwise(fpost_b, [rw(sv['yb']), rw(dx)], [P['post_norm'][i:i + 1]], [(D_MODEL, F32)],
                                      [(1, D_MODEL)], 256, f'post_norm_bwd{i}')
        if pending is None:
            dh, g = bwd[i](dyb, W, Pl, sv)
        else:
            dh, g = bwd[i](dyb, W, Pl, sv, comm=ChipExchange(pending[1]))
            adam_layer(pending[0], pending[1], g['carried'][0])
        g.pop('carried', None)
        G.update(g)

        def fpre_b(x_, dh_, d_, g_):
            dxl, dg = rms_bwd(x_, g_, dh_)
            return [d_ + dxl], [dg]
        (dx,), (dpre[i],) = rowwise(fpre_b, [rw(sv['x']), rw(dh), rw(dx)], [P['pre_norm'][i:i + 1]], [(D_MODEL, F32)],
                                    [(1, D_MODEL)], 256, f'pre_norm_bwd{i}')

        layer = 'abcd'[i]
        gfull = None
        for n in LAYER_PARAMS[layer]:
            gfull = chunk_grad(layer, n, G[n], gfull)
        pending = (layer, pair_sums(gfull, layer))
    adam_layer(pending[0], pending[1], rs_chips(pending[1], pending[0]))
    G['pre_norm'] = jnp.concatenate(dpre, axis=0)
    G['post_norm'] = jnp.concatenate(dpost, axis=0)

    part = pair_sums(pack_rep(G), 'rep')
    land2 = rs_chips(part, 'rep')
    grep = all_gather(rs_rep_sum(part, land2, chip), 'ag_rep')[:, :REP_CHUNK].reshape(REP_ROWS, LANES)
    for n, s in REP_SHAPE.items():
        out[n] = adam_param(n, s, REP_OFF[n], P[n], M[n], V[n], chip, grep=grep)
    res = [loss, dx[None]]
    for kind in range(4):
        res += [out[n][kind].reshape(P[n].shape) for n in WEIGHTS]
    return tuple(res)
```

```python
import functools
import math

import numpy as np
import jax
import jax.numpy as jnp
from jax import lax
from jax.experimental import pallas as pl
from jax.experimental.pallas import tpu as pltpu

F32 = jnp.float32
BF16 = jnp.bfloat16
MESH = pl.DeviceIdType.MESH
ANY = pl.BlockSpec(memory_space=pl.ANY)

N_DEV = 8
D_MODEL = 1024
EPS = 1e-6
NEG_INF = -1e30
SSM_G, SSM_P, SSM_H = 64, 64, 16
SSM_T = 256
SSM_WC = 512
HEAD_DIM = 64
SWA_HEADS, SWA_KV = 16, 2
WINDOW = 128
REL_BUCKETS, REL_MAX_DIST = 32, 128
MLA_HEADS, MLA_NOPE, MLA_ROPE, MLA_V = 16, 64, 32, 64
MLA_Q_RANK, MLA_KV_RANK = 768, 256
ROPE_BASE = 10000.0
SGU_G, SGU_C, SGU_T = 16, 64, 128
ADAM_LR, ADAM_B1, ADAM_B2, ADAM_EPS, ADAM_WD, ADAM_STEP = 0.001, 0.9, 0.999, 1e-08, 0.01, 10

WEIGHTS = ['pre_norm', 'post_norm', 'rel_bias', 'a_w_in', 'a_lam_re', 'a_lam_im', 'a_log_dt', 'a_b_re', 'a_b_im',
           'a_c_re', 'a_c_im', 'a_d', 'a_w_glu', 'a_b_glu', 'a_w_out', 'b_w_in', 'b_sinks', 'b_w_out', 'c_w_in',
           'c_q_norm', 'c_kv_norm', 'c_w_uq', 'c_w_ukv', 'c_w_out', 'd_w_in', 'd_ln_g', 'd_ln_b', 'd_w_s', 'd_b_s',
           'd_w_out']
SHARDED = {'a_w_in': ((1024, 2048), 1), 'a_w_glu': ((1024, 1024), 0), 'a_w_out': ((1024, 1024), 0),
           'b_w_in': ((1024, 2304), 1), 'b_w_out': ((1024, 1024), 0), 'c_w_in': ((1024, 2080), 1),
           'c_q_norm': ((1, 768), 1), 'c_kv_norm': ((1, 256), 1), 'c_w_uq': ((768, 1536), 1),
           'c_w_ukv': ((256, 2048), 1), 'c_w_out': ((1024, 1024), 0), 'd_w_in': ((1024, 3072), 1),
           'd_ln_g': ((1, 1024), 1), 'd_ln_b': ((1, 1024), 1), 'd_w_out': ((1024, 1024), 0)}
SHARDED_F32 = ['c_q_norm', 'c_kv_norm', 'd_ln_g', 'd_ln_b']
REPLICATED = [n for n in WEIGHTS if n not in SHARDED]


def _cdiv(a, b):
    return -(-a // b)


def _block_shape(name):
    (r, c), ax = SHARDED[name]
    return (r // N_DEV, c) if ax == 0 else (r, c // N_DEV)


LANES = 128
LAYER_PARAMS = {'a': ['a_w_in', 'a_w_glu', 'a_w_out'], 'b': ['b_w_in', 'b_w_out'],
                'c': ['c_w_in', 'c_w_uq', 'c_w_ukv', 'c_w_out', 'c_q_norm', 'c_kv_norm'],
                'd': ['d_w_in', 'd_w_out', 'd_ln_g', 'd_ln_b']}


def _tiles(shape):
    r, c = shape
    rp = max(r, 8)
    rb = 512 if rp % 512 == 0 else 256 if rp % 256 == 0 else rp
    return rp, _cdiv(c, LANES), rb


SH_OFF, LAYER_ROWS = {}, {}
for _l, _names in LAYER_PARAMS.items():
    _o = 0
    for _n in _names:
        _rp, _nt, _rb = _tiles(_block_shape(_n))
        assert _o % _rb == 0
        SH_OFF[_n] = _o
        _o += _rp * _nt
    assert _o % 16 == 0
    LAYER_ROWS[_l] = _o

REP_SHAPE = {'a_b_re': (4096, 16), 'a_b_im': (4096, 16), 'd_w_s': (2048, 128), 'a_c_re': (1024, 64),
             'a_c_im': (1024, 64), 'pre_norm': (4, 1024), 'post_norm': (4, 1024), 'a_lam_re': (64, 64),
             'a_lam_im': (64, 64), 'a_d': (1, 1024), 'a_b_glu': (1, 1024), 'rel_bias': (32, 16), 'd_b_s': (16, 128),
             'a_log_dt': (1, 64), 'b_sinks': (1, 16)}
REP_OFF = {}
_o = 0
for _n, _s in REP_SHAPE.items():
    _rp, _nt, _rb = _tiles(_s)
    assert _o % _rb == 0
    REP_OFF[_n] = _o
    _o += _rp * _nt
REP_ROWS = _o
REP_CHUNK = REP_ROWS // N_DEV
assert REP_ROWS % (8 * N_DEV) == 0
REP_SLOT = _cdiv(REP_CHUNK, 16) * 16

PERM = {'a_w_in': [(0, 2048)], 'd_w_in': [(0, 3072)], 'b_w_in': [(1280, 1024), (0, 1280)],
        'c_w_in': [(1056, 1024), (0, 1056), ('z', 96)],
        'c_w_uq': sum([[(2 * hp * 96, 64), ((2 * hp + 1) * 96, 64), (2 * hp * 96 + 64, 32), ((2 * hp + 1) * 96 + 64, 32),
                        ('z', 64)] for hp in range(8)], []),
        'c_w_ukv': sum([[(2 * hp * 128, 64), ((2 * hp + 1) * 128, 64), (2 * hp * 128 + 64, 64),
                         ((2 * hp + 1) * 128 + 64, 64)] for hp in range(8)], [])}


def perm_index(name):
    return np.concatenate([np.full(p[1], -1) if p[0] == 'z' else np.arange(p[0], p[0] + p[1]) for p in PERM[name]])


SMALL_OFF = {}
_o = 0
for _n in SHARDED_F32:
    SMALL_OFF[_n] = _o
    _o += int(np.prod(_block_shape(_n)))
SMALL_ROWS = _cdiv(_o, 128 * 8) * 8


def _pick(n, cands):
    for c in cands:
        if n % c == 0:
            return c
    return n


def mm(a, b, mode, name, out_dtype=F32):
    if mode == 'nn':
        (M, K), (K2, N) = a.shape, b.shape
    elif mode == 'nt':
        (M, K), (N, K2) = a.shape, b.shape
    else:
        (K, M), (K2, N) = a.shape, b.shape
    assert K == K2, (name, a.shape, b.shape)
    tm = _pick(M, (1024, 768, 512, 256, 128))
    tn = _pick(N, (512, 384, 256))
    dims = {'nn': ((1,), (0,)), 'nt': ((1,), (1,)), 'tn': ((0,), (0,))}[mode]

    def body(a_ref, b_ref, o_ref):
        o_ref[...] = lax.dot_general(a_ref[...].astype(BF16), b_ref[...].astype(BF16), (dims, ((), ())),
                                     preferred_element_type=F32).astype(out_dtype)

    a_spec = pl.BlockSpec((K, tm), lambda i, j: (0, i)) if mode == 'tn' else pl.BlockSpec((tm, K), lambda i, j: (i, 0))
    b_spec = pl.BlockSpec((tn, K), lambda i, j: (j, 0)) if mode == 'nt' else pl.BlockSpec((K, tn), lambda i, j: (0, j))
    return pl.pallas_call(
        body, grid=(M // tm, N // tn), in_specs=[a_spec, b_spec],
        out_specs=pl.BlockSpec((tm, tn), lambda i, j: (i, j)), out_shape=jax.ShapeDtypeStruct((M, N), out_dtype),
        compiler_params=pltpu.CompilerParams(dimension_semantics=("parallel", "parallel")), name=name)(a, b)


def rw(arr, width=None, cb=0):
    return (arr, arr.shape[1] if width is None else width, cb)


def rowwise(fn, rows, consts, outs, accs, tl, name, n_steps=None):
    if n_steps is None:
        n_steps = [r[0].shape[0] for r in rows if not isinstance(r[1], pl.BlockSpec)][0] // tl
    L = n_steps * tl
    nr, nc, no, na = len(rows), len(consts), len(outs), len(accs)
    in_specs, args = [], []
    for r in rows:
        if isinstance(r[1], pl.BlockSpec):
            in_specs.append(r[1])
        else:
            in_specs.append(pl.BlockSpec((tl, r[1]), functools.partial(lambda i, cb: (i, cb), cb=r[2])))
        args.append(r[0])
    for c in consts:
        in_specs.append(pl.BlockSpec(c.shape, functools.partial(lambda i, nd: (0,) * nd, nd=c.ndim)))
        args.append(c)
    out_specs = [pl.BlockSpec((tl, w), lambda i: (i, 0)) for w, _ in outs]
    out_shape = [jax.ShapeDtypeStruct((L, w), dt) for w, dt in outs]
    for s in accs:
        out_specs.append(pl.BlockSpec(s, functools.partial(lambda i, nd: (0,) * nd, nd=len(s))))
        out_shape.append(jax.ShapeDtypeStruct(s, F32))

    def body(*refs):
        ins = [r[...] for r in refs[:nr + nc]]
        o_refs = refs[nr + nc:nr + nc + no]
        a_refs = refs[nr + nc + no:]
        o_vals, a_vals = fn(*ins)
        for ref, val in zip(o_refs, o_vals):
            ref[...] = val.astype(ref.dtype)
        if na:
            @pl.when(pl.program_id(0) == 0)
            def _():
                for ref in a_refs:
                    ref[...] = jnp.zeros_like(ref)
            for ref, val in zip(a_refs, a_vals):
                ref[...] += val

    res = pl.pallas_call(
        body, grid=(n_steps,), in_specs=in_specs, out_specs=out_specs, out_shape=out_shape,
        compiler_params=pltpu.CompilerParams(dimension_semantics=("arbitrary",)), name=name)(*args)
    return res[:no], res[no:]


def carried(body, comm, *, grid, in_specs, out_specs, out_shape, name, semantics, scratch_shapes=()):
    single = not isinstance(out_shape, (list, tuple))
    o_specs = [out_specs] if single else list(out_specs)
    o_shape = [out_shape] if single else list(out_shape)
    if comm is None:
        call = pl.pallas_call(body, grid=grid, in_specs=in_specs, out_specs=out_specs, out_shape=out_shape,
                              scratch_shapes=list(scratch_shapes),
                              compiler_params=pltpu.CompilerParams(dimension_semantics=semantics), name=name)
        return lambda *args: (call(*args), None)
    n_in, n_out, n_sc = len(in_specs), len(o_specs), len(scratch_shapes)
    ci, co = len(comm.ins), len(comm.outs)
    n_steps = int(np.prod(grid))
    hooks = comm.hooks(n_steps)

    def wrapped(*refs):
        ins, cins = refs[:n_in], refs[n_in:n_in + ci]
        outs, couts = refs[n_in + ci:n_in + ci + n_out], refs[n_in + ci + n_out:n_in + ci + n_out + co]
        sc, csc = refs[n_in + ci + n_out + co:n_in + ci + n_out + co + n_sc], refs[n_in + ci + n_out + co + n_sc:]
        step = pl.program_id(0)
        for ax in range(1, len(grid)):
            step = step * grid[ax] + pl.program_id(ax)
        for at, fn, after in hooks:
            if not after:
                pl.when(step == at)(functools.partial(fn, cins, couts, csc))
        body(*ins, *outs, *sc)
        for at, fn, after in hooks:
            if after:
                pl.when(step == at)(functools.partial(fn, cins, couts, csc))

    call = pl.pallas_call(wrapped, grid=grid, in_specs=list(in_specs) + [ANY] * ci, out_specs=o_specs + [ANY] * co,
                          out_shape=o_shape + list(comm.outs), scratch_shapes=list(scratch_shapes) + list(comm.scratch),
                          compiler_params=pltpu.CompilerParams(dimension_semantics=("arbitrary",) * len(grid)), name=name)

    def run(*args):
        res = call(*args, *comm.ins)
        return (res[0] if single else res[:n_out]), res[n_out:]
    return run


_K0 = math.sqrt(2.0 / math.pi)
_K1 = 0.044715


def gelu(x):
    return x * (0.5 * (1.0 + jnp.tanh(_K0 * (x + _K1 * (x * x * x)))))


def gelu_grad(x):
    t = jnp.tanh(_K0 * (x + _K1 * (x * x * x)))
    return 0.5 * (1.0 + t) + 0.5 * x * (1.0 - t * t) * (_K0 * (1.0 + 3.0 * _K1 * x * x))


def sigmoid(x):
    return 1.0 / (1.0 + jnp.exp(-x))


def silu(z):
    return z * sigmoid(z)


def silu_grad(z):
    s = sigmoid(z)
    return s * (1.0 + z * (1.0 - s))


def rms_fwd(x, g):
    r = lax.rsqrt(jnp.mean(x * x, axis=-1, keepdims=True) + EPS)
    return x * r * g


def rms_bwd(x, g, dy):
    r = lax.rsqrt(jnp.mean(x * x, axis=-1, keepdims=True) + EPS)
    xh = x * r
    dg = jnp.sum(dy * xh, axis=0, keepdims=True)
    dxh = dy * g
    dx = r * (dxh - xh * jnp.mean(dxh * xh, axis=-1, keepdims=True))
    return dx, dg


def _scan_chunk(a_r, a_i, pr_ref, pi_ref, cr, ci, T, reverse):
    row = lax.broadcasted_iota(jnp.int32, a_r.shape, 0)
    sgn = -1.0 if reverse else 1.0
    d = 1
    while d < T:
        k = (T - d) if reverse else (d - 1)
        wr = pr_ref[k:k + 1, :]
        wi = sgn * pi_ref[k:k + 1, :]
        if reverse:
            yr, yi, keep = pltpu.roll(a_r, T - d, 0), pltpu.roll(a_i, T - d, 0), row < T - d
        else:
            yr, yi, keep = pltpu.roll(a_r, d, 0), pltpu.roll(a_i, d, 0), row >= d
        a_r, a_i = (a_r + jnp.where(keep, wr * yr - wi * yi, 0.0), a_i + jnp.where(keep, wr * yi + wi * yr, 0.0))
        d *= 2
    wr = pr_ref[...]
    wi = sgn * pi_ref[...]
    c_r, c_i = cr[...], ci[...]
    a_r, a_i = a_r + (wr * c_r - wi * c_i), a_i + (wr * c_i + wi * c_r)
    k = 0 if reverse else T - 1
    cr[...] = a_r[k:k + 1, :]
    ci[...] = a_i[k:k + 1, :]
    return a_r, a_i


_NT = (((1,), (1,)), ((), ()))
_TN = (((0,), (0,)), ((), ()))


def s5_fwd(proj, d_skip, Bre, Bim, Cre, Cim, pr, pi, comm=None):
    L = proj.shape[0]
    T, WC = min(SSM_T, L), SSM_WC
    nT = L // T

    def body(u_ref, d_ref, bre_ref, bim_ref, cre_ref, cim_ref, pr_ref, pi_ref, y_ref, yg_ref, sr_ref, si_ref, cr, ci):
        @pl.when(pl.program_id(1) == 0)
        def _():
            cr[...] = jnp.zeros_like(cr)
            ci[...] = jnp.zeros_like(ci)

        u = u_ref[...]
        ub = u.astype(BF16)
        a_r = jnp.dot(ub, bre_ref[0].astype(BF16), preferred_element_type=F32)
        a_i = jnp.dot(ub, bim_ref[0].astype(BF16), preferred_element_type=F32)
        a_r, a_i = _scan_chunk(a_r, a_i, pr_ref, pi_ref, cr, ci, T, False)
        sr_ref[...] = a_r
        si_ref[...] = a_i
        y = (jnp.dot(a_r.astype(BF16), cre_ref[0].astype(BF16), preferred_element_type=F32)
             + jnp.dot(a_i.astype(BF16), cim_ref[0].astype(BF16), preferred_element_type=F32) + d_ref[...] * u)
        y_ref[...] = y
        yg_ref[...] = gelu(y)

    uspec = pl.BlockSpec((T, 128), lambda k, i: (i, k))
    sspec = pl.BlockSpec((T, WC), lambda k, i: (i, k))
    return carried(
        body, comm, grid=(8, nT),
        in_specs=[uspec, pl.BlockSpec((1, 128), lambda k, i: (0, k)),
                  pl.BlockSpec((1, 128, WC), lambda k, i: (k, 0, 0)), pl.BlockSpec((1, 128, WC), lambda k, i: (k, 0, 0)),
                  pl.BlockSpec((1, WC, 128), lambda k, i: (k, 0, 0)), pl.BlockSpec((1, WC, 128), lambda k, i: (k, 0, 0)),
                  pl.BlockSpec((T, WC), lambda k, i: (0, k)), pl.BlockSpec((T, WC), lambda k, i: (0, k))],
        out_specs=[uspec, uspec, sspec, sspec],
        out_shape=[jax.ShapeDtypeStruct((L, 1024), F32)] * 2 + [jax.ShapeDtypeStruct((L, 8 * WC), F32)] * 2,
        scratch_shapes=[pltpu.VMEM((1, WC), F32), pltpu.VMEM((1, WC), F32)],
        semantics=("parallel", "arbitrary"), name='a_ssm')(proj, d_skip, Bre, Bim, Cre, Cim, pr, pi)


def s5_bwd(proj, dyg1, dyg2, y, d_skip, s_re, s_im, Bre, Bim, Cre, Cim, prr, pir, comm=None):
    L = proj.shape[0]
    T, WC = min(SSM_T, L), SSM_WC
    nT = L // T

    def body(u_ref, g1_ref, g2_ref, y_ref, d_ref, sr_ref, si_ref, spr_ref, spi_ref, bre_ref, bim_ref, cre_ref, cim_ref,
             pr_ref, pi_ref, du_ref, dd_ref, dbre_ref, dbim_ref, dcre_ref, dcim_ref, dar_ref, dai_ref, cr, ci):
        i = pl.program_id(1)

        @pl.when(i == 0)
        def _():
            for ref in (cr, ci, dd_ref, dbre_ref, dbim_ref, dcre_ref, dcim_ref, dar_ref, dai_ref):
                ref[...] = jnp.zeros_like(ref)

        u = u_ref[...]
        dy = (g1_ref[...] + g2_ref[...]) * gelu_grad(y_ref[...])
        dd_ref[...] += jnp.sum(dy * u, axis=0, keepdims=True)
        dyb, ub = dy.astype(BF16), u.astype(BF16)
        bre, bim, cre, cim = (r[0].astype(BF16) for r in (bre_ref, bim_ref, cre_ref, cim_ref))
        g_r = lax.dot_general(dyb, cre, _NT, preferred_element_type=F32)
        g_i = lax.dot_general(dyb, cim, _NT, preferred_element_type=F32)
        g_r, g_i = _scan_chunk(g_r, g_i, pr_ref, pi_ref, cr, ci, T, True)
        s_r, s_i = sr_ref[...], si_ref[...]
        row = lax.broadcasted_iota(jnp.int32, (T, WC), 0)
        first = (nT - 1 - i) == 0
        sp_r = jnp.where(row == 0, jnp.where(first, 0.0, spr_ref[7:8, :]), pltpu.roll(s_r, 1, 0))
        sp_i = jnp.where(row == 0, jnp.where(first, 0.0, spi_ref[7:8, :]), pltpu.roll(s_i, 1, 0))
        dar_ref[...] += jnp.sum(g_r * sp_r + g_i * sp_i, axis=0, keepdims=True)
        dai_ref[...] += jnp.sum(g_i * sp_r - g_r * sp_i, axis=0, keepdims=True)
        grb, gib = g_r.astype(BF16), g_i.astype(BF16)
        dcre_ref[0] += lax.dot_general(s_r.astype(BF16), dyb, _TN, preferred_element_type=F32)
        dcim_ref[0] += lax.dot_general(s_i.astype(BF16), dyb, _TN, preferred_element_type=F32)
        dbre_ref[0] += lax.dot_general(ub, grb, _TN, preferred_element_type=F32)
        dbim_ref[0] += lax.dot_general(ub, gib, _TN, preferred_element_type=F32)
        du_ref[...] = (dy * d_ref[...] + lax.dot_general(grb, bre, _NT, preferred_element_type=F32)
                       + lax.dot_general(gib, bim, _NT, preferred_element_type=F32))

    uspec = pl.BlockSpec((T, 128), lambda k, i: (nT - 1 - i, k))
    sspec = pl.BlockSpec((T, WC), lambda k, i: (nT - 1 - i, k))
    pspec = pl.BlockSpec((8, WC), lambda k, i: (jnp.maximum((nT - 1 - i) * (T // 8) - 1, 0), k))
    tab = pl.BlockSpec((T, WC), lambda k, i: (0, k))
    bspec = pl.BlockSpec((1, 128, WC), lambda k, i: (k, 0, 0))
    cspec = pl.BlockSpec((1, WC, 128), lambda k, i: (k, 0, 0))
    return carried(
        body, comm, grid=(8, nT),
        in_specs=[uspec, uspec, uspec, uspec, pl.BlockSpec((1, 128), lambda k, i: (0, k)), sspec, sspec, pspec, pspec,
                  bspec, bspec, cspec, cspec, tab, tab],
        out_specs=[uspec, pl.BlockSpec((1, 128), lambda k, i: (0, k)), bspec, bspec, cspec, cspec,
                   pl.BlockSpec((1, WC), lambda k, i: (0, k)), pl.BlockSpec((1, WC), lambda k, i: (0, k))],
        out_shape=[jax.ShapeDtypeStruct((L, 1024), F32), jax.ShapeDtypeStruct((1, 1024), F32),
                   jax.ShapeDtypeStruct((8, 128, WC), F32), jax.ShapeDtypeStruct((8, 128, WC), F32),
                   jax.ShapeDtypeStruct((8, WC, 128), F32), jax.ShapeDtypeStruct((8, WC, 128), F32),
                   jax.ShapeDtypeStruct((1, 8 * WC), F32), jax.ShapeDtypeStruct((1, 8 * WC), F32)],
        scratch_shapes=[pltpu.VMEM((1, WC), F32), pltpu.VMEM((1, WC), F32)],
        semantics=("parallel", "arbitrary"), name='a_ssm_bwd')(
            proj, dyg1, dyg2, y, d_skip, s_re, s_im, s_re, s_im, Bre, Bim, Cre, Cim, prr, pir)


def s5_discretize(lam_re, lam_im, log_dt, b_re, b_im):
    dt = jnp.exp(log_dt)[:, None]
    mag = jnp.exp(lam_re * dt)
    ab_re = mag * jnp.cos(lam_im * dt)
    ab_im = mag * jnp.sin(lam_im * dt)
    den = lam_re * lam_re + lam_im * lam_im
    nr = ab_re - 1.0
    f_re = (nr * lam_re + ab_im * lam_im) / den
    f_im = (ab_im * lam_re - nr * lam_im) / den
    bb_re = f_re[..., None] * b_re - f_im[..., None] * b_im
    bb_im = f_re[..., None] * b_im + f_im[..., None] * b_re
    return ab_re, ab_im, bb_re, bb_im


_EYE8 = np.eye(8, dtype=np.float32)


def _b_tiles(bb):
    t = bb.transpose(0, 2, 1).reshape(8, 8, SSM_H, SSM_P)
    return jnp.einsum('kghp,gG->kghGp', t, _EYE8).reshape(8, 8 * SSM_H, 8 * SSM_P)


def _b_untile(d):
    t = jnp.einsum('kghGp,gG->kghp', d.reshape(8, 8, SSM_H, 8, SSM_P), _EYE8)
    return t.reshape(SSM_G, SSM_H, SSM_P).transpose(0, 2, 1)


def _c_tiles(c):
    t = c.transpose(0, 2, 1).reshape(8, 8, SSM_P, SSM_H)
    return jnp.einsum('kgph,gG->kgpGh', t, _EYE8).reshape(8, 8 * SSM_P, 8 * SSM_H)


def _c_untile(d):
    t = jnp.einsum('kgpGh,gG->kgph', d.reshape(8, 8, SSM_P, 8, SSM_H), _EYE8)
    return t.reshape(SSM_G, SSM_P, SSM_H).transpose(0, 2, 1)


def s5_powers(ar, ai, T):
    W = ar.shape[1]

    def body(ar_ref, ai_ref, fr_ref, fi_ref, rr_ref, ri_ref):
        fr_ref[0:1, :] = ar_ref[...]
        fi_ref[0:1, :] = ai_ref[...]
        rr_ref[T - 1:T, :] = ar_ref[...]
        ri_ref[T - 1:T, :] = ai_ref[...]
        n = 1
        while n < T:
            cr, ci = fr_ref[0:n, :], fi_ref[0:n, :]
            lr, li = fr_ref[n - 1:n, :], fi_ref[n - 1:n, :]
            fr_ref[n:2 * n, :] = cr * lr - ci * li
            fi_ref[n:2 * n, :] = cr * li + ci * lr
            cr, ci = rr_ref[T - n:T, :], ri_ref[T - n:T, :]
            rr_ref[T - 2 * n:T - n, :] = cr * lr - ci * li
            ri_ref[T - 2 * n:T - n, :] = cr * li + ci * lr
            n *= 2

    spec = pl.BlockSpec((T, SSM_WC), lambda j: (0, j))
    aspec = pl.BlockSpec((1, SSM_WC), lambda j: (0, j))
    return pl.pallas_call(
        body, grid=(W // SSM_WC,), in_specs=[aspec, aspec], out_specs=[spec] * 4,
        out_shape=[jax.ShapeDtypeStruct((T, W), F32)] * 4,
        compiler_params=pltpu.CompilerParams(dimension_semantics=("parallel",)), name='a_powers')(ar, ai)


def layer_a_fwd(h, w, p, comm=None):
    L = h.shape[0]
    proj = mm(h, w['a_w_in'], 'nn', 'a_proj')
    disc = lambda *a: s5_discretize(*a)
    (ab_re, ab_im, bb_re, bb_im), disc_vjp = jax.vjp(disc, p['a_lam_re'][0], p['a_lam_im'][0], p['a_log_dt'][0],
                                                     p['a_b_re'][0], p['a_b_im'][0])
    Bre, Bim = _b_tiles(bb_re), _b_tiles(bb_im)
    Cre, Cim = _c_tiles(p['a_c_re'][0]), -_c_tiles(p['a_c_im'][0])
    T = min(SSM_T, L)
    pr, pi, prr, pir = s5_powers(ab_re.reshape(1, -1), ab_im.reshape(1, -1), T)
    (y, yg, s_re, s_im), carried_out = s5_fwd(proj, p['a_d'], Bre, Bim, Cre, Cim, pr, pi, comm=comm)
    gl = mm(yg, w['a_w_glu'], 'nn', 'a_glu')

    def f2(yg_, gl_, z, bg):
        return [yg_ * sigmoid(gl_ + bg) * silu(z)], []
    (po,), _ = rowwise(f2, [rw(yg), rw(gl), rw(proj, 1024, 1)], [p['a_b_glu']], [(1024, BF16)], [], 256, 'a_gate')
    yb = mm(po, w['a_w_out'], 'nn', 'a_out')
    saved = dict(carried=carried_out, h=h, proj=proj, disc_vjp=disc_vjp, Bre=Bre, Bim=Bim, Cre=Cre, Cim=Cim, prr=prr, pir=pir, s_re=s_re,
                 s_im=s_im, y=y, yg=yg, gl=gl, po=po)
    return yb, saved


def layer_a_bwd(dyb, w, p, sv, comm=None):
    g = {}
    dpo = mm(dyb, w['a_w_out'], 'nt', 'a_dpo')
    g['a_w_out'] = mm(sv['po'], dyb, 'tn', 'a_dwout')
    proj = sv['proj']

    def f1(dpo_, yg, gl, z, bg):
        sg = sigmoid(gl + bg)
        sz = silu(z)
        dm = dpo_ * sz
        dz = dpo_ * (yg * sg) * silu_grad(z)
        dgl = dm * yg * sg * (1.0 - sg)
        return [dz, dm * sg, dgl], [jnp.sum(dgl, axis=0, keepdims=True)]
    (dz, dyg1, dgl), (db_glu,) = rowwise(f1, [rw(dpo), rw(sv['yg']), rw(sv['gl']), rw(proj, 1024, 1)], [p['a_b_glu']],
                                          [(1024, F32), (1024, F32), (1024, BF16)], [(1, 1024)], 256, 'a_gate_bwd')
    g['a_b_glu'] = db_glu
    g['a_w_glu'] = mm(sv['yg'], dgl, 'tn', 'a_dwglu')
    dyg2 = mm(dgl, w['a_w_glu'], 'nt', 'a_dyg2')

    (du, dd, dBre, dBim, dCre, dCim, da_re, da_im), g['carried'] = s5_bwd(
        proj, dyg1, dyg2, sv['y'], p['a_d'], sv['s_re'], sv['s_im'], sv['Bre'], sv['Bim'], sv['Cre'], sv['Cim'],
        sv['prr'], sv['pir'], comm=comm)
    g['a_d'] = dd
    dCim = -dCim

    def f3(du_, dz_):
        return [jnp.concatenate([du_, dz_], axis=1)], []
    (dproj,), _ = rowwise(f3, [rw(du), rw(dz)], [], [(2048, BF16)], [], 256, 'a_dproj')
    dlr, dli, dldt, dbr, dbi = sv['disc_vjp']((da_re.reshape(SSM_G, SSM_P), da_im.reshape(SSM_G, SSM_P),
                                               _b_untile(dBre), _b_untile(dBim)))
    g['a_lam_re'], g['a_lam_im'], g['a_log_dt'] = dlr[None], dli[None], dldt[None]
    g['a_b_re'], g['a_b_im'] = dbr[None], dbi[None]
    g['a_c_re'], g['a_c_im'] = _c_untile(dCre)[None], _c_untile(dCim)[None]
    g['a_w_in'] = mm(sv['h'], dproj, 'tn', 'a_dwin')
    dh = mm(dproj, w['a_w_in'], 'nt', 'a_dh')
    return dh, g


def _t5_bucket_np():
    qi = np.arange(WINDOW)[:, None]
    kj = np.arange(2 * WINDOW)[None, :]
    dist = np.maximum(qi + WINDOW - kj, 0)
    max_exact = REL_BUCKETS // 2
    dist_f = np.maximum(dist, 1).astype(np.float32)
    large = max_exact + (np.log(dist_f / np.float32(max_exact)) / np.float32(math.log(REL_MAX_DIST / max_exact))
                         * np.float32(REL_BUCKETS - max_exact)).astype(np.int32)
    large = np.minimum(large, REL_BUCKETS - 1)
    return np.where(dist < max_exact, dist, large).astype(np.int32)


SWA_GRP = SWA_HEADS // SWA_KV


def _swa_kv(kvp, kvc, kvh):
    kb = jnp.concatenate([kvp[:, kvh * 64:(kvh + 1) * 64], kvc[:, kvh * 64:(kvh + 1) * 64]], 0).astype(BF16)
    vb = jnp.concatenate([kvp[:, 128 + kvh * 64:128 + (kvh + 1) * 64], kvc[:, 128 + kvh * 64:128 + (kvh + 1) * 64]],
                         0).astype(BF16)
    return kb, vb


def _swa_stack(x, kvh):
    return jnp.concatenate([x[:, (kvh * SWA_GRP + g) * 64:(kvh * SWA_GRP + g + 1) * 64] for g in range(SWA_GRP)],
                           axis=0).astype(BF16)


def _swa_group(bias_ref, kvh):
    return bias_ref[kvh * SWA_GRP:(kvh + 1) * SWA_GRP].reshape(SWA_GRP * WINDOW, 2 * WINDOW)


def _swa_sinks(sink_ref, kvh):
    return jnp.concatenate([jnp.broadcast_to(sink_ref[0:1, kvh * SWA_GRP + g:kvh * SWA_GRP + g + 1], (WINDOW, 1))
                            for g in range(SWA_GRP)], axis=0)


def _swa_probs(q, kb, bias_h, sink, valid):
    s = lax.dot_general(q, kb, (((1,), (1,)), ((), ())), preferred_element_type=F32) * (HEAD_DIM ** -0.5)
    s = jnp.where(valid, s + bias_h, NEG_INF)
    m = jnp.maximum(jnp.max(s, axis=-1, keepdims=True), sink)
    e = jnp.exp(s - m)
    es = jnp.exp(sink - m)
    den = jnp.sum(e, axis=-1, keepdims=True) + es
    return e / den, es / den


def _swa_valid(n):
    qi = lax.broadcasted_iota(jnp.int32, (SWA_GRP * WINDOW, 2 * WINDOW), 0) & (WINDOW - 1)
    kj = lax.broadcasted_iota(jnp.int32, (SWA_GRP * WINDOW, 2 * WINDOW), 1)
    dist = qi + WINDOW - kj
    return (dist >= 0) & (dist < WINDOW) & ((kj >= WINDOW) | (n > 0))


def swa_fwd(proj, bias, sinks, comm=None):
    L = proj.shape[0]

    def body(z_ref, q_ref, kvc_ref, kvp_ref, bias_ref, sink_ref, o_ref, po_ref):
        n = pl.program_id(0)
        valid = _swa_valid(n)
        q, kvc, kvp = q_ref[...], kvc_ref[...], kvp_ref[...]
        outs = []
        for kvh in range(SWA_KV):
            kb, vb = _swa_kv(kvp, kvc, kvh)
            p, _ = _swa_probs(_swa_stack(q, kvh), kb, _swa_group(bias_ref, kvh), _swa_sinks(sink_ref, kvh), valid)
            o8 = jnp.dot(p.astype(BF16), vb, preferred_element_type=F32)
            outs += [o8[g * WINDOW:(g + 1) * WINDOW] for g in range(SWA_GRP)]
        o = jnp.concatenate(outs, axis=1)
        o_ref[...] = o
        po_ref[...] = (o * silu(z_ref[...])).astype(po_ref.dtype)

    return carried(
        body, comm, grid=(L // WINDOW,),
        in_specs=[pl.BlockSpec((WINDOW, 1024), lambda n: (n, 0)), pl.BlockSpec((WINDOW, 1024), lambda n: (n, 1)),
                  pl.BlockSpec((WINDOW, 256), lambda n: (n, 8)),
                  pl.BlockSpec((WINDOW, 256), lambda n: (jnp.maximum(n - 1, 0), 8)),
                  pl.BlockSpec((SWA_HEADS, WINDOW, 2 * WINDOW), lambda n: (0, 0, 0)),
                  pl.BlockSpec((1, SWA_HEADS), lambda n: (0, 0))],
        out_specs=[pl.BlockSpec((WINDOW, 1024), lambda n: (n, 0))] * 2,
        out_shape=[jax.ShapeDtypeStruct((L, 1024), F32), jax.ShapeDtypeStruct((L, 1024), BF16)],
        semantics=("parallel",), name='b_attn')(proj, proj, proj, proj, bias, sinks)


def swa_bwd(proj, do, bias, sinks, comm=None):
    L = proj.shape[0]

    def body(q_ref, kvc_ref, kvp_ref, do_ref, bias_ref, sink_ref, dq_ref, dkv_ref, dbias_ref, dsink_ref):
        n = pl.program_id(0)

        @pl.when(n == 0)
        def _():
            dkv_ref[...] = jnp.zeros_like(dkv_ref)
            dbias_ref[...] = jnp.zeros_like(dbias_ref)
            dsink_ref[...] = jnp.zeros_like(dsink_ref)

        valid = _swa_valid(n)
        q, kvc, kvp, do_ = q_ref[...], kvc_ref[...], kvp_ref[...], do_ref[...]
        dqs, dks, dvs, dsk = [], [], [], []
        for kvh in range(SWA_KV):
            kb, vb = _swa_kv(kvp, kvc, kvh)
            q8, do8 = _swa_stack(q, kvh), _swa_stack(do_, kvh)
            p, ps = _swa_probs(q8, kb, _swa_group(bias_ref, kvh), _swa_sinks(sink_ref, kvh), valid)
            dp = lax.dot_general(do8, vb, (((1,), (1,)), ((), ())), preferred_element_type=F32)
            delta = jnp.sum(p * dp, axis=-1, keepdims=True)
            ds = p * (dp - delta)
            col = -ps * delta
            dsk += [jnp.sum(col[g * WINDOW:(g + 1) * WINDOW], axis=0, keepdims=True) for g in range(SWA_GRP)]
            dbias_ref[kvh * SWA_GRP:(kvh + 1) * SWA_GRP] += ds.reshape(SWA_GRP, WINDOW, 2 * WINDOW)
            dsb = (ds * (HEAD_DIM ** -0.5)).astype(BF16)
            dq8 = jnp.dot(dsb, kb, preferred_element_type=F32)
            dqs += [dq8[g * WINDOW:(g + 1) * WINDOW] for g in range(SWA_GRP)]
            dks.append(lax.dot_general(dsb, q8, (((0,), (0,)), ((), ())), preferred_element_type=F32))
            dvs.append(lax.dot_general(p.astype(BF16), do8, (((0,), (0,)), ((), ())), preferred_element_type=F32))
        dq_ref[...] = jnp.concatenate(dqs, axis=1)
        dsink_ref[...] += jnp.concatenate(dsk, axis=1)
        both = jnp.concatenate(dks + dvs, axis=1)
        r_cur = pl.multiple_of(n * WINDOW, WINDOW)
        r_prev = pl.multiple_of(jnp.maximum(n - 1, 0) * WINDOW, WINDOW)
        dkv_ref[pl.ds(r_prev, WINDOW), :] += both[:WINDOW]
        dkv_ref[pl.ds(r_cur, WINDOW), :] += both[WINDOW:]

    return carried(
        body, comm, grid=(L // WINDOW,),
        in_specs=[pl.BlockSpec((WINDOW, 1024), lambda n: (n, 1)), pl.BlockSpec((WINDOW, 256), lambda n: (n, 8)),
                  pl.BlockSpec((WINDOW, 256), lambda n: (jnp.maximum(n - 1, 0), 8)),
                  pl.BlockSpec((WINDOW, 1024), lambda n: (n, 0)),
                  pl.BlockSpec((SWA_HEADS, WINDOW, 2 * WINDOW), lambda n: (0, 0, 0)),
                  pl.BlockSpec((1, SWA_HEADS), lambda n: (0, 0))],
        out_specs=[pl.BlockSpec((WINDOW, 1024), lambda n: (n, 0)), pl.BlockSpec((L, 256), lambda n: (0, 0)),
                   pl.BlockSpec((SWA_HEADS, WINDOW, 2 * WINDOW), lambda n: (0, 0, 0)),
                   pl.BlockSpec((1, SWA_HEADS), lambda n: (0, 0))],
        out_shape=[jax.ShapeDtypeStruct((L, 1024), F32), jax.ShapeDtypeStruct((L, 256), F32),
                   jax.ShapeDtypeStruct((SWA_HEADS, WINDOW, 2 * WINDOW), F32), jax.ShapeDtypeStruct((1, SWA_HEADS), F32)],
        semantics=("arbitrary",), name='b_attn_bwd')(proj, proj, proj, do, bias, sinks)


def swa_bias(rel_bias):
    def body(bk_ref, rb_ref, o_ref):
        bk = bk_ref[...]
        for h in range(SWA_HEADS):
            acc = jnp.zeros((WINDOW, 2 * WINDOW), F32)
            for b in range(REL_BUCKETS):
                acc = jnp.where(bk == b, rb_ref[b, h], acc)
            o_ref[h] = acc

    return pl.pallas_call(
        body, out_shape=jax.ShapeDtypeStruct((SWA_HEADS, WINDOW, 2 * WINDOW), F32),
        in_specs=[pl.BlockSpec(memory_space=pltpu.VMEM), pl.BlockSpec(memory_space=pltpu.SMEM)],
        out_specs=pl.BlockSpec(memory_space=pltpu.VMEM), name='b_bias')(jnp.asarray(_t5_bucket_np()), rel_bias)


def layer_b_fwd(h, w, p, comm=None):
    proj = mm(h, w['b_w_in'], 'nn', 'b_proj')
    bias = swa_bias(p['rel_bias'])
    (o, po), carried_out = swa_fwd(proj, bias, p['b_sinks'], comm=comm)
    yb = mm(po, w['b_w_out'], 'nn', 'b_out')
    return yb, dict(carried=carried_out, h=h, proj=proj, bias=bias, o=o, po=po)


def layer_b_bwd(dyb, w, p, sv, comm=None):
    g = {}
    dpo = mm(dyb, w['b_w_out'], 'nt', 'b_dpo')
    g['b_w_out'] = mm(sv['po'], dyb, 'tn', 'b_dwout')
    proj = sv['proj']

    def f1(dpo_, o, z):
        return [dpo_ * silu(z), dpo_ * o * silu_grad(z)], []
    (do, dz), _ = rowwise(f1, [rw(dpo), rw(sv['o']), rw(proj, 1024, 0)], [], [(1024, BF16), (1024, F32)], [], 256, 'b_gate_bwd')
    (dq, dkv, dbias, dsinks), g['carried'] = swa_bwd(proj, do, sv['bias'], p['b_sinks'], comm=comm)
    g['b_sinks'] = dsinks
    onehot = jnp.asarray(np.eye(REL_BUCKETS, dtype=np.float32)[_t5_bucket_np().reshape(-1)])

    def f2(db, oh):
        return [], [lax.dot_general(db, oh, (((1,), (0,)), ((), ())), preferred_element_type=F32,
                                    precision=lax.Precision.HIGHEST)]
    _, (drel,) = rowwise(f2, [(dbias.reshape(SWA_HEADS, -1), pl.BlockSpec((SWA_HEADS, 4096), lambda i: (0, i))),
                              (onehot, pl.BlockSpec((4096, REL_BUCKETS), lambda i: (i, 0)))], [], [],
                         [(SWA_HEADS, REL_BUCKETS)], 4096, 'b_drel', n_steps=(2 * WINDOW * WINDOW) // 4096)
    g['rel_bias'] = drel.T

    def f3(dz_, dq_, dkv_):
        return [jnp.concatenate([dz_, dq_, dkv_], axis=1)], []
    (dproj,), _ = rowwise(f3, [rw(dz), rw(dq), rw(dkv)], [], [(2304, BF16)], [], 256, 'b_dproj')
    g['b_w_in'] = mm(sv['h'], dproj, 'tn', 'b_dwin')
    dh = mm(dproj, w['b_w_in'], 'nt', 'b_dh')
    return dh, g


MLA_SCALE = (MLA_NOPE + MLA_ROPE) ** -0.5


def _rope_tables(L):
    inv = ROPE_BASE ** (-jnp.arange(0, MLA_ROPE, 2, dtype=F32) / MLA_ROPE)
    ang = jnp.arange(L, dtype=F32)[:, None] * inv[None, :]
    c, s = jnp.cos(ang), jnp.sin(ang)
    one, zero, pad = jnp.ones((L, 128), F32), jnp.zeros((L, 128), F32), jnp.zeros((L, 64), F32)
    return (jnp.concatenate([one, c, c, c, c, pad], 1), jnp.concatenate([zero, s, s, s, s, pad], 1))


def _rot(x, transpose=False):
    w = x.shape[1]
    lane = lax.broadcasted_iota(jnp.int32, x.shape, 1)
    up = pltpu.roll(x, w - 16, 1)
    dn = pltpu.roll(x, 16, 1)
    first = (lane % 32) < 16
    return jnp.where(first, up, -dn) if transpose else jnp.where(first, -up, dn)


MLA_QT = 512


def _mla_exp(qf, kf, t, qt):
    n_k = kf.shape[0]
    s = lax.dot_general(qf, kf, (((1,), (1,)), ((), ())), preferred_element_type=F32) * MLA_SCALE
    qpos = t * qt + lax.broadcasted_iota(jnp.int32, (qt, n_k), 0)
    kpos = lax.broadcasted_iota(jnp.int32, (qt, n_k), 1)
    s = jnp.where(kpos <= qpos, s, NEG_INF)
    e = jnp.exp(s - jnp.max(s, axis=-1, keepdims=True))
    return e, jnp.sum(e, axis=-1, keepdims=True)


def _mla_heads(q, kv, kr):
    out = []
    for j in range(2):
        qf = jnp.concatenate([q[:, j * 64:(j + 1) * 64], q[:, 128 + j * 32:128 + (j + 1) * 32]], axis=1)
        kf = jnp.concatenate([kv[:, j * 64:(j + 1) * 64], kr], axis=1)
        out.append((qf, kf, kv[:, 128 + j * 64:128 + (j + 1) * 64]))
    return out


def mla_fwd(q, kv, kr, comm=None):
    L = q.shape[0]
    qt = min(MLA_QT, L)
    nq = L // qt

    def body(q_ref, kv_ref, kr_ref, o_ref):
        for t in range(nq):
            @pl.when(pl.program_id(1) == t)
            def _(t=t):
                n_k = (t + 1) * qt
                outs = []
                for qf, kf, v in _mla_heads(q_ref[...], kv_ref[0:n_k, :], kr_ref[0:n_k, 0:MLA_ROPE]):
                    e, den = _mla_exp(qf, kf, t, qt)
                    outs.append(jnp.dot(e.astype(BF16), v, preferred_element_type=F32) / den)
                o_ref[...] = jnp.concatenate(outs, axis=1)

    return carried(
        body, comm, grid=(MLA_HEADS // 2, nq),
        in_specs=[pl.BlockSpec((qt, 256), lambda hp, n: (n, hp)), pl.BlockSpec((L, 256), lambda hp, n: (0, hp)),
                  pl.BlockSpec((L, 128), lambda hp, n: (0, 0))],
        out_specs=pl.BlockSpec((qt, 128), lambda hp, n: (n, hp)), out_shape=jax.ShapeDtypeStruct((L, 1024), F32),
        semantics=("parallel", "parallel"), name='c_attn')(q, kv, kr)


def mla_bwd(q, kv, kr, do, comm=None):
    L = q.shape[0]
    qt = min(MLA_QT, L)
    nq = L // qt

    def body(q_ref, kv_ref, kr_ref, do_ref, dq_ref, dkv_ref, dkr_ref):
        @pl.when(pl.program_id(1) == 0)
        def _():
            dkv_ref[...] = jnp.zeros_like(dkv_ref)
            dkr_ref[...] = jnp.zeros_like(dkr_ref)

        for t in range(nq):
            @pl.when(pl.program_id(1) == t)
            def _(t=t):
                n_k = (t + 1) * qt
                do_ = do_ref[...]
                dqn, dqr, dkn, dvs = [], [], [], []
                dkr = jnp.zeros((n_k, MLA_ROPE), F32)
                for j, (qf, kf, v) in enumerate(_mla_heads(q_ref[...], kv_ref[0:n_k, :], kr_ref[0:n_k, 0:MLA_ROPE])):
                    doh = do_[:, j * 64:(j + 1) * 64]
                    e, den = _mla_exp(qf, kf, t, qt)
                    p = e * (1.0 / den)
                    dp = lax.dot_general(doh, v, (((1,), (1,)), ((), ())), preferred_element_type=F32)
                    ds = (p * (dp - jnp.sum(p * dp, axis=-1, keepdims=True)) * MLA_SCALE).astype(BF16)
                    dqf = jnp.dot(ds, kf, preferred_element_type=F32)
                    dkf = lax.dot_general(ds, qf, (((0,), (0,)), ((), ())), preferred_element_type=F32)
                    dvs.append(lax.dot_general(p.astype(BF16), doh, (((0,), (0,)), ((), ())), preferred_element_type=F32))
                    dqn.append(dqf[:, :MLA_NOPE])
                    dqr.append(dqf[:, MLA_NOPE:])
                    dkn.append(dkf[:, :MLA_NOPE])
                    dkr = dkr + dkf[:, MLA_NOPE:]
                dq_ref[...] = jnp.concatenate(dqn + dqr + [jnp.zeros((qt, 64), F32)], axis=1)
                dkv_ref[0:n_k, :] += jnp.concatenate(dkn + dvs, axis=1)
                dkr_ref[0, 0:n_k, :] += jnp.concatenate([dkr, jnp.zeros((n_k, 128 - MLA_ROPE), F32)], axis=1)

    return carried(
        body, comm, grid=(MLA_HEADS // 2, nq),
        in_specs=[pl.BlockSpec((qt, 256), lambda hp, n: (n, hp)), pl.BlockSpec((L, 256), lambda hp, n: (0, hp)),
                  pl.BlockSpec((L, 128), lambda hp, n: (0, 0)), pl.BlockSpec((qt, 128), lambda hp, n: (n, hp))],
        out_specs=[pl.BlockSpec((qt, 256), lambda hp, n: (n, hp)), pl.BlockSpec((L, 256), lambda hp, n: (0, hp)),
                   pl.BlockSpec((1, L, 128), lambda hp, n: (hp, 0, 0))],
        out_shape=[jax.ShapeDtypeStruct((L, 2048), F32), jax.ShapeDtypeStruct((L, 2048), F32),
                   jax.ShapeDtypeStruct((MLA_HEADS // 2, L, 128), F32)],
        semantics=("parallel", "arbitrary"), name='c_attn_bwd')(q, kv, kr, do)


def layer_c_fwd(h, w, p, comm=None):
    L = h.shape[0]
    proj = mm(h, w['c_w_in'], 'nn', 'c_proj')

    def f1(c, gq, gk):
        return [rms_fwd(c[:, :768], gq), rms_fwd(c[:, 768:], gk)], []
    (cqn, ckvn), _ = rowwise(f1, [rw(proj, 1024, 1)], [p['c_q_norm'], p['c_kv_norm']], [(768, BF16), (256, BF16)], [],
                             256, 'c_norms')
    qf = mm(cqn, w['c_w_uq'], 'nn', 'c_uq')
    kvf = mm(ckvn, w['c_w_ukv'], 'nn', 'c_ukv', out_dtype=BF16)
    cos, sin = _rope_tables(L)

    def f2(q_, kr_, c, s):
        c8, s8 = jnp.tile(c, (1, 8)), jnp.tile(s, (1, 8))
        return [q_ * c8 + _rot(q_) * s8, kr_ * c[:, 128:] + _rot(kr_) * s[:, 128:]], []
    (q, kr), _ = rowwise(f2, [rw(qf), rw(proj, 128, 16), rw(cos), rw(sin)], [], [(2048, BF16), (128, BF16)], [], 256,
                         'c_rope')
    o, carried_out = mla_fwd(q, kvf, kr, comm=comm)

    def f3(o_, z):
        return [o_ * silu(z)], []
    (po,), _ = rowwise(f3, [rw(o), rw(proj, 1024, 0)], [], [(1024, BF16)], [], 256, 'c_gate')
    yb = mm(po, w['c_w_out'], 'nn', 'c_out')
    return yb, dict(carried=carried_out, h=h, proj=proj, cqn=cqn, ckvn=ckvn, q=q, kv=kvf, kr=kr, o=o, po=po, cos=cos, sin=sin)


def layer_c_bwd(dyb, w, p, sv, comm=None):
    g = {}
    dpo = mm(dyb, w['c_w_out'], 'nt', 'c_dpo')
    g['c_w_out'] = mm(sv['po'], dyb, 'tn', 'c_dwout')
    proj = sv['proj']
    L = proj.shape[0]

    def f1(dpo_, o, z):
        return [dpo_ * silu(z), dpo_ * o * silu_grad(z)], []
    (do, dz), _ = rowwise(f1, [rw(dpo), rw(sv['o']), rw(proj, 1024, 0)], [], [(1024, BF16), (1024, F32)], [], 256,
                          'c_gate_bwd')
    (dq, dkvf, dkr8), g['carried'] = mla_bwd(sv['q'], sv['kv'], sv['kr'], do, comm=comm)

    def f2(dq_, dkr_, c, s):
        c8, s8 = jnp.tile(c, (1, 8)), jnp.tile(s, (1, 8))
        dk = jnp.sum(dkr_, axis=0)
        return [dq_ * c8 + _rot(dq_ * s8, True), dk * c[:, 128:] + _rot(dk * s[:, 128:], True)], []
    tl = 256
    (dqf, dkr), _ = rowwise(f2, [rw(dq), (dkr8, pl.BlockSpec((8, tl, 128), lambda i: (0, i, 0))), rw(sv['cos']),
                                 rw(sv['sin'])], [], [(2048, BF16), (128, F32)], [], tl, 'c_rope_bwd')
    g['c_w_uq'] = mm(sv['cqn'], dqf, 'tn', 'c_dwuq')
    g['c_w_ukv'] = mm(sv['ckvn'], dkvf, 'tn', 'c_dwukv')
    dcqn = mm(dqf, w['c_w_uq'], 'nt', 'c_dcqn')
    dckvn = mm(dkvf, w['c_w_ukv'], 'nt', 'c_dckvn')

    def f3(c, dq_, dk_, dz_, dkr_, gq, gk):
        dcq, dgq = rms_bwd(c[:, :768], gq, dq_)
        dckv, dgk = rms_bwd(c[:, 768:], gk, dk_)
        return [jnp.concatenate([dz_, dcq, dckv, dkr_], axis=1)], [dgq, dgk]
    (dproj,), (dgq, dgk) = rowwise(f3, [rw(proj, 1024, 1), rw(dcqn), rw(dckvn), rw(dz), rw(dkr)],
                                   [p['c_q_norm'], p['c_kv_norm']], [(2176, BF16)], [(1, 768), (1, 256)], 256, 'c_dproj')
    g['c_q_norm'], g['c_kv_norm'] = dgq, dgk
    g['c_w_in'] = mm(sv['h'], dproj, 'tn', 'c_dwin')
    dh = mm(dproj, w['c_w_in'], 'nt', 'c_dh')
    return dh, g


def _sgu_mix(wm, v, transpose):
    outs = []
    dims = (((0,), (0,)), ((), ())) if transpose else (((1,), (0,)), ((), ()))
    for gi in range(SGU_G):
        outs.append(lax.dot_general(wm[gi], v[:, gi * SGU_C:(gi + 1) * SGU_C].astype(BF16), dims,
                                    preferred_element_type=F32))
    return jnp.concatenate(outs, axis=1)


def _sgu_wmask(ws):
    t = lax.broadcasted_iota(jnp.int32, (SGU_T, SGU_T), 0)
    s = lax.broadcasted_iota(jnp.int32, (SGU_T, SGU_T), 1)
    return jnp.where((s <= t)[None], ws, 0.0).astype(BF16)


def _ln_stats(v):
    mu = jnp.mean(v, axis=-1, keepdims=True)
    vc = v - mu
    rstd = lax.rsqrt(jnp.mean(vc * vc, axis=-1, keepdims=True) + EPS)
    return vc * rstd, rstd


def layer_d_fwd(h, w, p):
    proj = mm(h, w['d_w_in'], 'nn', 'd_proj')
    bias = jnp.repeat(p['d_b_s'][0].T, SGU_C, axis=1)

    def f1(u_, v_, z, ws, lg, lb, bs):
        xh, _ = _ln_stats(gelu(v_))
        s = _sgu_mix(_sgu_wmask(ws), xh * lg + lb, False) + bs
        return [gelu(u_) * s * silu(z)], []
    (po,), _ = rowwise(f1, [rw(proj, 1024, 0), rw(proj, 1024, 1), rw(proj, 1024, 2)],
                       [p['d_w_s'][0], p['d_ln_g'], p['d_ln_b'], bias], [(1024, BF16)], [], SGU_T, 'd_mix')
    yb = mm(po, w['d_w_out'], 'nn', 'd_out')
    return yb, dict(h=h, proj=proj, po=po, bias=bias)


def layer_d_bwd(dyb, w, p, sv):
    g = {}
    dpo = mm(dyb, w['d_w_out'], 'nt', 'd_dpo')
    g['d_w_out'] = mm(sv['po'], dyb, 'tn', 'd_dwout')
    proj = sv['proj']

    def f1(dpo_, u_, v_, z, ws, lg, lb, bs):
        wm = _sgu_wmask(ws)
        gv = gelu(v_)
        xh, rstd = _ln_stats(gv)
        vn = xh * lg + lb
        s = _sgu_mix(wm, vn, False) + bs
        gu, sz = gelu(u_), silu(z)
        du = dpo_ * s * sz
        ds = dpo_ * gu * sz
        dz = dpo_ * gu * s * silu_grad(z)
        dsb = ds.astype(BF16)
        dws = jnp.stack([lax.dot_general(dsb[:, gi * SGU_C:(gi + 1) * SGU_C], vn[:, gi * SGU_C:(gi + 1) * SGU_C].astype(BF16),
                                         (((1,), (1,)), ((), ())), preferred_element_type=F32) for gi in range(SGU_G)])
        dvn = _sgu_mix(wm, ds, True)
        dlg = jnp.sum(dvn * xh, axis=0, keepdims=True)
        dlb = jnp.sum(dvn, axis=0, keepdims=True)
        dxh = dvn * lg
        dgv = rstd * (dxh - jnp.mean(dxh, axis=-1, keepdims=True) - xh * jnp.mean(dxh * xh, axis=-1, keepdims=True))
        return ([jnp.concatenate([du * gelu_grad(u_), dgv * gelu_grad(v_), dz], axis=1)], [dws, ds, dlg, dlb])
    (dproj,), (dws, dbs, dlg, dlb) = rowwise(
        f1, [rw(dpo), rw(proj, 1024, 0), rw(proj, 1024, 1), rw(proj, 1024, 2)],
        [p['d_w_s'][0], p['d_ln_g'], p['d_ln_b'], sv['bias']], [(3072, BF16)],
        [(SGU_G, SGU_T, SGU_T), (SGU_T, 1024), (1, 1024), (1, 1024)], SGU_T, 'd_mix_bwd')
    tril = np.tril(np.ones((SGU_T, SGU_T), dtype=bool))
    g['d_w_s'] = jnp.where(tril[None], dws, 0.0)[None]
    g['d_b_s'] = dbs.reshape(SGU_T, SGU_G, SGU_C).sum(-1).T[None]
    g['d_ln_g'], g['d_ln_b'] = dlg, dlb
    g['d_w_in'] = mm(sv['h'], dproj, 'tn', 'd_dwin')
    dh = mm(dproj, w['d_w_in'], 'nt', 'd_dh')
    return dh, g


def _coords():
    return lax.axis_index("x"), lax.axis_index("y"), lax.axis_index("c")


class AllGather:
    def __init__(self, x):
        self.ins = [x]
        self.outs = [jax.ShapeDtypeStruct((N_DEV,) + x.shape, x.dtype)]
        self.scratch = [pltpu.SemaphoreType.DMA((7,)), pltpu.SemaphoreType.DMA((7,)), pltpu.SemaphoreType.DMA(())]

    def hooks(self, n_steps):
        return [(0, functools.partial(self.phase, 0), False), ((n_steps * 5) // 8, functools.partial(self.phase, 1), False),
                (n_steps - 1, functools.partial(self.phase, 2), True)]

    @staticmethod
    def phase(which, ins, outs, scratch):
        (x_ref,), (out_ref,), (send_sems, recv_sems, local_sem) = ins, outs, scratch
        x_, y_, c_ = _coords()
        me, sibling = (x_, y_, c_), (x_, y_, 1 - c_)
        chips = [(1 - x_, y_), (x_, 1 - y_), (1 - x_, 1 - y_)]

        def slot(px, py, pc):
            return out_ref.at[4 * px + 2 * py + pc]

        def copy(k, block, to, src=None):
            return pltpu.make_async_remote_copy(src_ref=slot(*block) if src is None else src, dst_ref=slot(*block),
                                                send_sem=send_sems.at[k], recv_sem=recv_sems.at[k], device_id=to,
                                                device_id_type=MESH)

        mine = pltpu.make_async_copy(x_ref, slot(*me), local_sem)
        first = [copy(0, me, sibling, src=x_ref)]
        first += [copy(1 + j, me, (*chip, c_), src=x_ref) for j, chip in enumerate(chips)]
        passed = [copy(4 + j, (*chip, c_), sibling) for j, chip in enumerate(chips)]
        if which == 0:
            mine.start()
            for cp in first:
                cp.start()
        elif which == 1:
            for j, chip in enumerate(chips):
                copy(1 + j, (*chip, c_), me).wait_recv()
                passed[j].start()
        else:
            copy(0, sibling, me).wait_recv()
            for j, chip in enumerate(chips):
                copy(4 + j, (*chip, 1 - c_), me).wait_recv()
            for cp in first + passed:
                cp.wait_send()
            mine.wait()


class ChipExchange:
    def __init__(self, part):
        self.ins = [part]
        self.outs = [jax.ShapeDtypeStruct((3,) + part.shape[1:], part.dtype)]
        self.scratch = [pltpu.SemaphoreType.DMA((3,)), pltpu.SemaphoreType.DMA((3,))]

    def hooks(self, n_steps):
        return [(0, functools.partial(self.phase, 0), False), (n_steps - 1, functools.partial(self.phase, 1), True)]

    @staticmethod
    def phase(which, ins, outs, scratch):
        (p_ref,), (land_ref,), (send_sems, recv_sems) = ins, outs, scratch
        x_, y_, c_ = _coords()
        copies = []
        for r, (fx, fy) in enumerate([(1, 0), (0, 1), (1, 1)]):
            tx = jnp.where(fx == 1, 1 - x_, x_)
            ty = jnp.where(fy == 1, 1 - y_, y_)
            copies.append(pltpu.make_async_remote_copy(src_ref=p_ref.at[2 * tx + ty], dst_ref=land_ref.at[r],
                                                       send_sem=send_sems.at[r], recv_sem=recv_sems.at[r],
                                                       device_id=(tx, ty, c_), device_id_type=MESH))
        if which == 0:
            for cp in copies:
                cp.start()
        else:
            for cp in copies:
                cp.wait_recv()
            for cp in copies:
                cp.wait_send()


class Both:
    def __init__(self, a, b):
        self.parts = (a, b)
        self.ins, self.outs, self.scratch = a.ins + b.ins, a.outs + b.outs, a.scratch + b.scratch

    def hooks(self, n_steps):
        res, oi, oo, osc = [], 0, 0, 0
        for p in self.parts:
            sl = (slice(oi, oi + len(p.ins)), slice(oo, oo + len(p.outs)), slice(osc, osc + len(p.scratch)))
            res += [(at, functools.partial(self.sub, fn, sl), after) for at, fn, after in p.hooks(n_steps)]
            oi, oo, osc = oi + len(p.ins), oo + len(p.outs), osc + len(p.scratch)
        return res

    @staticmethod
    def sub(fn, sl, ins, outs, scratch):
        fn(ins[sl[0]], outs[sl[1]], scratch[sl[2]])


def run_comm(comm, name):
    def body(*refs):
        ci, co = len(comm.ins), len(comm.outs)
        for _, fn, _ in comm.hooks(1):
            fn(refs[:ci], refs[ci:ci + co], refs[ci + co:])

    return pl.pallas_call(body, out_shape=list(comm.outs), in_specs=[ANY] * len(comm.ins),
                          out_specs=[ANY] * len(comm.outs), scratch_shapes=list(comm.scratch), name=name)(*comm.ins)


def all_gather(x, name):
    return run_comm(AllGather(x), name)[0]


def rs_sibling(gfull, tag):
    _, R, C = gfull.shape

    def body(g_ref, land_ref, send_sems, recv_sems):
        x_, y_, c_ = _coords()
        copies = []
        for k in range(4):
            cp = pltpu.make_async_remote_copy(src_ref=g_ref.at[2 * k + 1 - c_], dst_ref=land_ref.at[k],
                                              send_sem=send_sems.at[k], recv_sem=recv_sems.at[k],
                                              device_id=(x_, y_, 1 - c_), device_id_type=MESH)
            cp.start()
            copies.append(cp)
        for cp in copies:
            cp.wait_recv()
        for cp in copies:
            cp.wait_send()

    return pl.pallas_call(
        body, out_shape=jax.ShapeDtypeStruct((4, R, C), gfull.dtype), in_specs=[ANY], out_specs=ANY,
        scratch_shapes=[pltpu.SemaphoreType.DMA((4,)), pltpu.SemaphoreType.DMA((4,))], name='rs_sibling_' + tag)(gfull)


def rs_pair_add(gfull, land, core, tag):
    _, R, C = gfull.shape
    tl = R

    def body(c_ref, g_ref, l_ref, o_ref):
        o_ref[...] = (g_ref[...] + l_ref[...]).astype(BF16)

    return pl.pallas_call(
        body, out_shape=jax.ShapeDtypeStruct((4, R, C), BF16),
        grid_spec=pltpu.PrefetchScalarGridSpec(
            num_scalar_prefetch=1, grid=(4, R // tl),
            in_specs=[pl.BlockSpec((1, tl, C), lambda k, i, c: (2 * k + c[0], i, 0)),
                      pl.BlockSpec((1, tl, C), lambda k, i, c: (k, i, 0))],
            out_specs=pl.BlockSpec((1, tl, C), lambda k, i, c: (k, i, 0))),
        compiler_params=pltpu.CompilerParams(dimension_semantics=("parallel", "parallel")), name='rs_pair_add_' + tag)(
            core, gfull, land)


def rs_chips(part, tag):
    return run_comm(ChipExchange(part), 'rs_chips_' + tag)[0]


def _adam(wv, gv, mv, vv):
    m = ADAM_B1 * mv + (1.0 - ADAM_B1) * gv
    v = ADAM_B2 * vv + (1.0 - ADAM_B2) * (gv * gv)
    m_hat = m / (1.0 - ADAM_B1 ** ADAM_STEP)
    v_hat = v / (1.0 - ADAM_B2 ** ADAM_STEP)
    delta = -ADAM_LR * (m_hat / (jnp.sqrt(v_hat) + ADAM_EPS) + ADAM_WD * wv)
    return delta, m, v


def _sum4(p_ref, l_ref):
    return ((p_ref[0].astype(F32) + l_ref[0].astype(F32)) + l_ref[1].astype(F32)) + l_ref[2].astype(F32)


def rs_rep_sum(part, land, chip):
    def body(c_ref, p_ref, l_ref, o_ref):
        o_ref[...] = _sum4(p_ref, l_ref)

    return pl.pallas_call(
        body, out_shape=jax.ShapeDtypeStruct((REP_SLOT, LANES), F32),
        grid_spec=pltpu.PrefetchScalarGridSpec(
            num_scalar_prefetch=1, grid=(1,),
            in_specs=[pl.BlockSpec((1, REP_SLOT, LANES), lambda i, c: (c[0], 0, 0)),
                      pl.BlockSpec((3, REP_SLOT, LANES), lambda i, c: (0, 0, 0))],
            out_specs=pl.BlockSpec((REP_SLOT, LANES), lambda i, c: (0, 0))),
        compiler_params=pltpu.CompilerParams(dimension_semantics=("parallel",)), name='rs_rep')(chip, part, land)


def adam_param(name, shape, off, w, m, v, chip, part=None, land=None, grep=None):
    r, c = shape
    rp, nt, rb = _tiles(shape)
    rbw = min(r, rb)
    n_src = 2 if grep is None else 1
    ns = w.shape
    assert int(np.prod(ns[:-1])) == r and ns[-1] == c
    if len(ns) == 2:
        nat_block, nat_map = (rbw, c), lambda i, cr: (i, 0)
    elif int(np.prod(ns[:-2])) == 1:
        nat_block, nat_map = (1,) * (len(ns) - 2) + (rbw, c), lambda i, cr: (0,) * (len(ns) - 2) + (i, 0)
    else:
        assert len(ns) == 4 and ns[0] == 1 and rbw % ns[2] == 0
        nat_block, nat_map = (1, rbw // ns[2], ns[2], c), lambda i, cr: (0, i, 0, 0)

    def body(c_ref, *refs):
        srcs = refs[:n_src * nt]
        w_ref, m_ref, v_ref, g_ref, d_ref, nm_ref, nv_ref = refs[n_src * nt:]
        if grep is None:
            tiles = [_sum4(srcs[2 * t], srcs[2 * t + 1]) for t in range(nt)]
        else:
            tiles = [srcs[t][...] for t in range(nt)]
        g = (tiles[0] if nt == 1 else jnp.concatenate(tiles, axis=1))[:rbw, :c]
        g_ref[...] = g.reshape(nat_block)
        res = _adam(w_ref[...].reshape(rbw, c), g, m_ref[...].reshape(rbw, c), v_ref[...].reshape(rbw, c))
        for ref, val in zip((d_ref, nm_ref, nv_ref), res):
            ref[...] = val.reshape(nat_block)

    in_specs, args = [], []
    for t in range(nt):
        b0 = (off + t * rp) // rb
        assert (off + t * rp) % rb == 0
        if grep is None:
            in_specs += [pl.BlockSpec((1, rb, LANES), functools.partial(lambda i, cr, b0: (cr[0], b0 + i, 0), b0=b0)),
                         pl.BlockSpec((3, rb, LANES), functools.partial(lambda i, cr, b0: (0, b0 + i, 0), b0=b0))]
            args += [part, land]
        else:
            in_specs.append(pl.BlockSpec((rb, LANES), functools.partial(lambda i, cr, b0: (b0 + i, 0), b0=b0)))
            args.append(grep)
    nat = pl.BlockSpec(nat_block, nat_map)
    return pl.pallas_call(
        body, out_shape=[jax.ShapeDtypeStruct(ns, F32)] * 4,
        grid_spec=pltpu.PrefetchScalarGridSpec(num_scalar_prefetch=1, grid=(rp // rb,), in_specs=in_specs + [nat] * 3,
                                               out_specs=[nat] * 4),
        compiler_params=pltpu.CompilerParams(dimension_semantics=("parallel",)), name='adam_' + name)(
            chip, *args, w, m, v)


VM = pl.BlockSpec(memory_space=pltpu.VMEM)


def _tile_value(w, t, rp):
    r, c = w.shape
    wt = min(LANES, c - t * LANES)
    tile = w[:, t * LANES:t * LANES + wt]
    if wt < LANES:
        tile = jnp.concatenate([tile, jnp.zeros((r, LANES - wt), tile.dtype)], axis=1)
    if rp > r:
        tile = jnp.concatenate([tile, jnp.zeros((rp - r, LANES), tile.dtype)], axis=0)
    return tile


def pack_layer(layer, blocks):
    names = LAYER_PARAMS[layer]

    def body(*refs):
        tiles = []
        for ref, n in zip(refs[:-1], names):
            rp, nt, _ = _tiles(_block_shape(n))
            w = ref[...].reshape(_block_shape(n))
            tiles += [_tile_value(w, t, rp) for t in range(nt)]
        refs[-1][...] = jnp.concatenate(tiles, axis=0).astype(BF16)

    return pl.pallas_call(body, out_shape=jax.ShapeDtypeStruct((LAYER_ROWS[layer], LANES), BF16),
                          in_specs=[VM] * len(names), out_specs=VM, name='pack_' + layer)(*[blocks[n] for n in names])


def assemble(name, gathered):
    (rf, cf), ax = SHARDED[name]
    r, c = _block_shape(name)
    rp, nt, _ = _tiles((r, c))
    off = SH_OFF[name]
    out_cols = cf if ax == 0 else len(perm_index(name))

    def body(g_ref, o_ref, buf, sem):
        cp = pltpu.make_async_copy(g_ref.at[:, pl.ds(off, nt * rp), :], buf, sem)
        cp.start()
        cp.wait()
        if ax == 0:
            for j in range(N_DEV):
                o_ref[j * r:(j + 1) * r, :] = jnp.concatenate([buf[j, t * rp:(t + 1) * rp, :] for t in range(nt)], axis=1)
            return
        pieces = []
        for p in PERM[name]:
            if p[0] == 'z':
                pieces.append(jnp.zeros((r, p[1]), BF16))
                continue
            n0, w = p
            while w > 0:
                j, cb = divmod(n0, c)
                t, lane = divmod(cb, LANES)
                wl = min(w, LANES - lane, c - cb)
                pieces.append(buf[j, t * rp:t * rp + r, lane:lane + wl])
                n0, w = n0 + wl, w - wl
        o_ref[...] = jnp.concatenate(pieces, axis=1)

    return pl.pallas_call(
        body, out_shape=jax.ShapeDtypeStruct((rf, out_cols), BF16), in_specs=[ANY], out_specs=VM,
        scratch_shapes=[pltpu.VMEM((N_DEV, nt * rp, LANES), BF16), pltpu.SemaphoreType.DMA(())], name='asm_' + name)(
            gathered)


def chunk_grad(layer, name, dw, gfull):
    (rf, cf), ax = SHARDED[name]
    r, c = _block_shape(name)
    rp, nt, _ = _tiles((r, c))
    off = SH_OFF[name]
    if ax == 1:
        idx = perm_index(name) if name in PERM else np.arange(cf)
        inv = np.full(cf, -1)
        inv[idx[idx >= 0]] = np.nonzero(idx >= 0)[0]

    def body(*refs):
        dw_ref, o_ref, buf, sem = refs[0], refs[-3], refs[-2], refs[-1]
        for j in range(N_DEV):
            for t in range(nt):
                if ax == 0:
                    tile = dw_ref[j * r:(j + 1) * r, t * LANES:(t + 1) * LANES]
                else:
                    cols = inv[j * c + t * LANES:j * c + min((t + 1) * LANES, c)]
                    cuts = [0] + [k for k in range(1, len(cols)) if cols[k] != cols[k - 1] + 1] + [len(cols)]
                    pieces = [dw_ref[:, int(cols[a]):int(cols[b - 1]) + 1] for a, b in zip(cuts[:-1], cuts[1:])]
                    if len(cols) < LANES:
                        pieces.append(jnp.zeros((r, LANES - len(cols)), F32))
                    tile = pieces[0] if len(pieces) == 1 else jnp.concatenate(pieces, axis=1)
                    if rp > r:
                        tile = jnp.concatenate([tile, jnp.zeros((rp - r, LANES), F32)], axis=0)
                buf[j, t * rp:(t + 1) * rp, :] = tile
        cp = pltpu.make_async_copy(buf, o_ref.at[:, pl.ds(off, nt * rp), :], sem)
        cp.start()
        cp.wait()

    shape = jax.ShapeDtypeStruct((N_DEV, LAYER_ROWS[layer], LANES), F32)
    scratch = [pltpu.VMEM((N_DEV, nt * rp, LANES), F32), pltpu.SemaphoreType.DMA(())]
    if gfull is None:
        return pl.pallas_call(body, out_shape=shape, in_specs=[VM], out_specs=ANY, scratch_shapes=scratch,
                              name='chunk_' + name)(dw)
    return pl.pallas_call(body, out_shape=shape, in_specs=[VM, ANY], out_specs=ANY, scratch_shapes=scratch,
                          input_output_aliases={1: 0}, name='chunk_' + name)(dw, gfull)


def pack_rep(G):
    def body(*refs):
        tiles = []
        for ref, s in zip(refs[:-1], REP_SHAPE.values()):
            rp, nt, _ = _tiles(s)
            g = ref[...]
            tiles += [_tile_value(g, t, rp) for t in range(nt)]
        full = jnp.concatenate(tiles, axis=0)
        for j in range(N_DEV):
            refs[-1][j, 0:REP_CHUNK, :] = full[j * REP_CHUNK:(j + 1) * REP_CHUNK]
            if REP_SLOT > REP_CHUNK:
                refs[-1][j, REP_CHUNK:REP_SLOT, :] = jnp.zeros((REP_SLOT - REP_CHUNK, LANES), F32)

    return pl.pallas_call(body, out_shape=jax.ShapeDtypeStruct((N_DEV, REP_SLOT, LANES), F32),
                          in_specs=[VM] * len(REP_SHAPE), out_specs=VM, name='pack_rep')(
                              *[G[n].reshape(s) for n, s in REP_SHAPE.items()])


def _pack_small(blocks, order, rows, width, dtype):
    flat = jnp.concatenate([blocks[n].reshape(-1).astype(dtype) for n in order])
    return jnp.pad(flat, (0, rows * width - flat.shape[0])).reshape(rows, width)


def kernel(x, pre_norm, post_norm, rel_bias, a_w_in, a_lam_re, a_lam_im, a_log_dt, a_b_re, a_b_im, a_c_re, a_c_im, a_d, a_w_glu, a_b_glu, a_w_out, b_w_in, b_sinks, b_w_out, c_w_in, c_q_norm, c_kv_norm, c_w_uq, c_w_ukv, c_w_out, d_w_in, d_ln_g, d_ln_b, d_w_s, d_b_s, d_w_out, loss_target, m_pre_norm, m_post_norm, m_rel_bias, m_a_w_in, m_a_lam_re, m_a_lam_im, m_a_log_dt, m_a_b_re, m_a_b_im, m_a_c_re, m_a_c_im, m_a_d, m_a_w_glu, m_a_b_glu, m_a_w_out, m_b_w_in, m_b_sinks, m_b_w_out, m_c_w_in, m_c_q_norm, m_c_kv_norm, m_c_w_uq, m_c_w_ukv, m_c_w_out, m_d_w_in, m_d_ln_g, m_d_ln_b, m_d_w_s, m_d_b_s, m_d_w_out, v_pre_norm, v_post_norm, v_rel_bias, v_a_w_in, v_a_lam_re, v_a_lam_im, v_a_log_dt, v_a_b_re, v_a_b_im, v_a_c_re, v_a_c_im, v_a_d, v_a_w_glu, v_a_b_glu, v_a_w_out, v_b_w_in, v_b_sinks, v_b_w_out, v_c_w_in, v_c_q_norm, v_c_kv_norm, v_c_w_uq, v_c_w_ukv, v_c_w_out, v_d_w_in, v_d_ln_g, v_d_ln_b, v_d_w_s, v_d_b_s, v_d_w_out):
    loc = locals()
    P = {n: loc[n] for n in WEIGHTS}
    M = {n: loc['m_' + n] for n in WEIGHTS}
    V = {n: loc['v_' + n] for n in WEIGHTS}
    xs = x[0]
    L = xs.shape[0]

    blocks = {n: P[n].reshape(_block_shape(n)) for n in SHARDED}
    packed = {layer: pack_layer(layer, P) for layer in LAYER_PARAMS}
    W = {}

    def assemble_layer(layer, gathered):
        for n in LAYER_PARAMS[layer]:
            if n not in SHARDED_F32:
                W[n] = assemble(n, gathered)

    assemble_layer('a', all_gather(packed['a'], 'ag_a'))
    small = all_gather(_pack_small(blocks, SHARDED_F32, SMALL_ROWS, 128, F32), 'ag_small')
    Pl = dict(P)
    for n in SHARDED_F32:
        c = SHARDED[n][0][1]
        bc = c // N_DEV
        Pl[n] = small.reshape(N_DEV, -1)[:, SMALL_OFF[n]:SMALL_OFF[n] + bc].reshape(1, c)
    cx, cy, cc = _coords()
    core = jnp.reshape(cc, (1,)).astype(jnp.int32)
    chip = jnp.reshape(2 * cx + cy, (1,)).astype(jnp.int32)

    def pair_sums(gfull, tag):
        return rs_pair_add(gfull, rs_sibling(gfull, tag), core, tag)

    fwd = [layer_a_fwd, layer_b_fwd, layer_c_fwd, layer_d_fwd]
    bwd = [layer_a_bwd, layer_b_bwd, layer_c_bwd, layer_d_bwd]
    saved = []
    xc = xs
    for i in range(4):
        def fpre(x_, g_):
            return [rms_fwd(x_, g_)], []
        (h,), _ = rowwise(fpre, [rw(xc)], [P['pre_norm'][i:i + 1]], [(D_MODEL, BF16)], [], 256, f'pre_norm{i}')
        if i == 0:
            yb, sv = fwd[i](h, W, Pl, comm=Both(AllGather(packed['b']), AllGather(packed['c'])))
            assemble_layer('b', sv['carried'][0])
            assemble_layer('c', sv['carried'][1])
        elif i == 1:
            yb, sv = fwd[i](h, W, Pl, comm=AllGather(packed['d']))
            assemble_layer('d', sv['carried'][0])
        else:
            yb, sv = fwd[i](h, W, Pl)

        def fpost(x_, y_, g_):
            return [x_ + rms_fwd(y_, g_)], []
        (xn,), _ = rowwise(fpost, [rw(xc), rw(yb)], [P['post_norm'][i:i + 1]], [(D_MODEL, F32)], [], 256, f'post_norm{i}')
        sv['x'], sv['yb'] = xc, yb
        saved.append(sv)
        xc = xn

    def floss(y_, t_):
        d = y_ - t_
        return [d * (1.0 / D_MODEL)], [0.5 * jnp.sum(jnp.sum(d * d, axis=-1, keepdims=True) * (1.0 / D_MODEL), axis=0,
                                                      keepdims=True)]
    (dx,), (loss_loc,) = rowwise(floss, [rw(xc), rw(loss_target[0])], [], [(D_MODEL, F32)], [(1, 1)], 256, 'loss')
    loss = lax.psum(loss_loc[0, 0], ("x", "y", "c"))

    G, out = {}, {}
    dpre, dpost = [None] * 4, [None] * 4

    def adam_layer(layer, part, land2):
        for n in LAYER_PARAMS[layer]:
            s = _block_shape(n)
            out[n] = adam_param(n, s, SH_OFF[n], P[n], M[n], V[n], chip, part=part, land=land2)

    pending = None
    for i in reversed(range(4)):
        sv = saved[i]

        def fpost_b(y_, d_, g_):
            dy, dg = rms_bwd(y_, g_, d_)
            return [dy], [dg]
        (dyb,), (dpost[i],) = rowwise(fpost_b, [rw(sv['yb']), rw(dx)], [P['post_norm'][i:i + 1]], [(D_MODEL, BF16)],
                                      [(1, D_MODEL)], 256, f'post_norm_bwd{i}')
        if pending is None:
            dh, g = bwd[i](dyb, W, Pl, sv)
        else:
            dh, g = bwd[i](dyb, W, Pl, sv, comm=ChipExchange(pending[1]))
            adam_layer(pending[0], pending[1], g['carried'][0])
        g.pop('carried', None)
        G.update(g)

        def fpre_b(x_, dh_, d_, g_):
            dxl, dg = rms_bwd(x_, g_, dh_)
            return [d_ + dxl], [dg]
        (dx,), (dpre[i],) = rowwise(fpre_b, [rw(sv['x']), rw(dh), rw(dx)], [P['pre_norm'][i:i + 1]], [(D_MODEL, F32)],
                                    [(1, D_MODEL)], 256, f'pre_norm_bwd{i}')

        layer = 'abcd'[i]
        gfull = None
        for n in LAYER_PARAMS[layer]:
            gfull = chunk_grad(layer, n, G[n], gfull)
        pending = (layer, pair_sums(gfull, layer))
    adam_layer(pending[0], pending[1], rs_chips(pending[1], pending[0]))
    G['pre_norm'] = jnp.concatenate(dpre, axis=0)
    G['post_norm'] = jnp.concatenate(dpost, axis=0)

    part = pair_sums(pack_rep(G), 'rep')
    land2 = rs_chips(part, 'rep')
    grep = all_gather(rs_rep_sum(part, land2, chip), 'ag_rep')[:, :REP_CHUNK].reshape(REP_ROWS, LANES)
    for n, s in REP_SHAPE.items():
        out[n] = adam_param(n, s, REP_OFF[n], P[n], M[n], V[n], chip, grep=grep)
    res = [loss, dx[None]]
    for kind in range(4):
        res += [out[n][kind].reshape(P[n].shape) for n in WEIGHTS]
    return tuple(res)
```

```python
import functools
import math

import numpy as np
import jax
import jax.numpy as jnp
from jax import lax
from jax.experimental import pallas as pl
from jax.experimental.pallas import tpu as pltpu

F32 = jnp.float32
BF16 = jnp.bfloat16
MESH = pl.DeviceIdType.MESH
ANY = pl.BlockSpec(memory_space=pl.ANY)

N_DEV = 8
D_MODEL = 1024
EPS = 1e-6
NEG_INF = -1e30
SSM_G, SSM_P, SSM_H = 64, 64, 16
SSM_T = 256
SSM_WC = 512
HEAD_DIM = 64
SWA_HEADS, SWA_KV = 16, 2
WINDOW = 128
REL_BUCKETS, REL_MAX_DIST = 32, 128
MLA_HEADS, MLA_NOPE, MLA_ROPE, MLA_V = 16, 64, 32, 64
MLA_Q_RANK, MLA_KV_RANK = 768, 256
ROPE_BASE = 10000.0
SGU_G, SGU_C, SGU_T = 16, 64, 128
ADAM_LR, ADAM_B1, ADAM_B2, ADAM_EPS, ADAM_WD, ADAM_STEP = 0.001, 0.9, 0.999, 1e-08, 0.01, 10

WEIGHTS = ['pre_norm', 'post_norm', 'rel_bias', 'a_w_in', 'a_lam_re', 'a_lam_im', 'a_log_dt', 'a_b_re', 'a_b_im',
           'a_c_re', 'a_c_im', 'a_d', 'a_w_glu', 'a_b_glu', 'a_w_out', 'b_w_in', 'b_sinks', 'b_w_out', 'c_w_in',
           'c_q_norm', 'c_kv_norm', 'c_w_uq', 'c_w_ukv', 'c_w_out', 'd_w_in', 'd_ln_g', 'd_ln_b', 'd_w_s', 'd_b_s',
           'd_w_out']
SHARDED = {'a_w_in': ((1024, 2048), 1), 'a_w_glu': ((1024, 1024), 0), 'a_w_out': ((1024, 1024), 0),
           'b_w_in': ((1024, 2304), 1), 'b_w_out': ((1024, 1024), 0), 'c_w_in': ((1024, 2080), 1),
           'c_q_norm': ((1, 768), 1), 'c_kv_norm': ((1, 256), 1), 'c_w_uq': ((768, 1536), 1),
           'c_w_ukv': ((256, 2048), 1), 'c_w_out': ((1024, 1024), 0), 'd_w_in': ((1024, 3072), 1),
           'd_ln_g': ((1, 1024), 1), 'd_ln_b': ((1, 1024), 1), 'd_w_out': ((1024, 1024), 0)}
SHARDED_F32 = ['c_q_norm', 'c_kv_norm', 'd_ln_g', 'd_ln_b']
REPLICATED = [n for n in WEIGHTS if n not in SHARDED]


def _cdiv(a, b):
    return -(-a // b)


def _block_shape(name):
    (r, c), ax = SHARDED[name]
    return (r // N_DEV, c) if ax == 0 else (r, c // N_DEV)


LANES = 128
LAYER_PARAMS = {'a': ['a_w_in', 'a_w_glu', 'a_w_out'], 'b': ['b_w_in', 'b_w_out'],
                'c': ['c_w_in', 'c_w_uq', 'c_w_ukv', 'c_w_out', 'c_q_norm', 'c_kv_norm'],
                'd': ['d_w_in', 'd_w_out', 'd_ln_g', 'd_ln_b']}


def _tiles(shape):
    r, c = shape
    rp = max(r, 16)
    rb = 512 if rp % 512 == 0 else 256 if rp % 256 == 0 else rp
    return rp, _cdiv(c, LANES), rb


SH_OFF, LAYER_ROWS = {}, {}
for _l, _names in LAYER_PARAMS.items():
    _o = 0
    for _n in _names:
        _rp, _nt, _rb = _tiles(_block_shape(_n))
        assert _o % _rb == 0
        SH_OFF[_n] = _o
        _o += _rp * _nt
    assert _o % 16 == 0
    LAYER_ROWS[_l] = _o

REP_SHAPE = {'a_b_re': (4096, 16), 'a_b_im': (4096, 16), 'd_w_s': (2048, 128), 'a_c_re': (1024, 64),
             'a_c_im': (1024, 64), 'pre_norm': (4, 1024), 'post_norm': (4, 1024), 'a_lam_re': (64, 64),
             'a_lam_im': (64, 64), 'a_d': (1, 1024), 'a_b_glu': (1, 1024), 'rel_bias': (32, 16), 'd_b_s': (16, 128),
             'a_log_dt': (1, 64), 'b_sinks': (1, 16)}
REP_OFF = {}
_o = 0
for _n, _s in REP_SHAPE.items():
    _rp, _nt, _rb = _tiles(_s)
    assert _o % _rb == 0
    REP_OFF[_n] = _o
    _o += _rp * _nt
REP_ROWS = _cdiv(_o, 16 * N_DEV) * 16 * N_DEV
REP_CHUNK = REP_ROWS // N_DEV
REP_SLOT = REP_CHUNK

PERM = {'a_w_in': [(0, 2048)], 'd_w_in': [(0, 3072)], 'b_w_in': [(1280, 1024), (0, 1280)],
        'c_w_in': [(1056, 1024), (0, 1056), ('z', 96)],
        'c_w_uq': sum([[(2 * hp * 96, 64), ((2 * hp + 1) * 96, 64), (2 * hp * 96 + 64, 32), ((2 * hp + 1) * 96 + 64, 32),
                        ('z', 64)] for hp in range(8)], []),
        'c_w_ukv': sum([[(2 * hp * 128, 64), ((2 * hp + 1) * 128, 64), (2 * hp * 128 + 64, 64),
                         ((2 * hp + 1) * 128 + 64, 64)] for hp in range(8)], [])}


def perm_index(name):
    return np.concatenate([np.full(p[1], -1) if p[0] == 'z' else np.arange(p[0], p[0] + p[1]) for p in PERM[name]])


SMALL_OFF = {}
_o = 0
for _n in SHARDED_F32:
    SMALL_OFF[_n] = _o
    _o += int(np.prod(_block_shape(_n)))
SMALL_ROWS = _cdiv(_o, 128 * 8) * 8


def _pick(n, cands):
    for c in cands:
        if n % c == 0:
            return c
    return n


def mm(a, b, mode, name, out_dtype=F32):
    if mode == 'nn':
        (M, K), (K2, N) = a.shape, b.shape
    elif mode == 'nt':
        (M, K), (N, K2) = a.shape, b.shape
    else:
        (K, M), (K2, N) = a.shape, b.shape
    assert K == K2, (name, a.shape, b.shape)
    tm = _pick(M, (1024, 768, 512, 256, 128))
    tn = _pick(N, (512, 384, 256))
    dims = {'nn': ((1,), (0,)), 'nt': ((1,), (1,)), 'tn': ((0,), (0,))}[mode]

    def body(a_ref, b_ref, o_ref):
        o_ref[...] = lax.dot_general(a_ref[...].astype(BF16), b_ref[...].astype(BF16), (dims, ((), ())),
                                     preferred_element_type=F32).astype(out_dtype)

    a_spec = pl.BlockSpec((K, tm), lambda i, j: (0, i)) if mode == 'tn' else pl.BlockSpec((tm, K), lambda i, j: (i, 0))
    b_spec = pl.BlockSpec((tn, K), lambda i, j: (j, 0)) if mode == 'nt' else pl.BlockSpec((K, tn), lambda i, j: (0, j))
    return pl.pallas_call(
        body, grid=(M // tm, N // tn), in_specs=[a_spec, b_spec],
        out_specs=pl.BlockSpec((tm, tn), lambda i, j: (i, j)), out_shape=jax.ShapeDtypeStruct((M, N), out_dtype),
        compiler_params=pltpu.CompilerParams(dimension_semantics=("parallel", "parallel")), name=name)(a, b)


def rw(arr, width=None, cb=0):
    return (arr, arr.shape[1] if width is None else width, cb)


def rowwise(fn, rows, consts, outs, accs, tl, name, n_steps=None):
    if n_steps is None:
        n_steps = [r[0].shape[0] for r in rows if not isinstance(r[1], pl.BlockSpec)][0] // tl
    L = n_steps * tl
    nr, nc, no, na = len(rows), len(consts), len(outs), len(accs)
    in_specs, args = [], []
    for r in rows:
        if isinstance(r[1], pl.BlockSpec):
            in_specs.append(r[1])
        else:
            in_specs.append(pl.BlockSpec((tl, r[1]), functools.partial(lambda i, cb: (i, cb), cb=r[2])))
        args.append(r[0])
    for c in consts:
        in_specs.append(pl.BlockSpec(c.shape, functools.partial(lambda i, nd: (0,) * nd, nd=c.ndim)))
        args.append(c)
    out_specs = [pl.BlockSpec((tl, w), lambda i: (i, 0)) for w, _ in outs]
    out_shape = [jax.ShapeDtypeStruct((L, w), dt) for w, dt in outs]
    for s in accs:
        out_specs.append(pl.BlockSpec(s, functools.partial(lambda i, nd: (0,) * nd, nd=len(s))))
        out_shape.append(jax.ShapeDtypeStruct(s, F32))

    def body(*refs):
        ins = [r[...] for r in refs[:nr + nc]]
        o_refs = refs[nr + nc:nr + nc + no]
        a_refs = refs[nr + nc + no:]
        o_vals, a_vals = fn(*ins)
        for ref, val in zip(o_refs, o_vals):
            ref[...] = val.astype(ref.dtype)
        if na:
            @pl.when(pl.program_id(0) == 0)
            def _():
                for ref in a_refs:
                    ref[...] = jnp.zeros_like(ref)
            for ref, val in zip(a_refs, a_vals):
                ref[...] += val

    res = pl.pallas_call(
        body, grid=(n_steps,), in_specs=in_specs, out_specs=out_specs, out_shape=out_shape,
        compiler_params=pltpu.CompilerParams(dimension_semantics=("arbitrary",)), name=name)(*args)
    return res[:no], res[no:]


def carried(body, comm, *, grid, in_specs, out_specs, out_shape, name, semantics, scratch_shapes=()):
    single = not isinstance(out_shape, (list, tuple))
    o_specs = [out_specs] if single else list(out_specs)
    o_shape = [out_shape] if single else list(out_shape)
    if comm is None:
        call = pl.pallas_call(body, grid=grid, in_specs=in_specs, out_specs=out_specs, out_shape=out_shape,
                              scratch_shapes=list(scratch_shapes),
                              compiler_params=pltpu.CompilerParams(dimension_semantics=semantics), name=name)
        return lambda *args: (call(*args), None)
    n_in, n_out, n_sc = len(in_specs), len(o_specs), len(scratch_shapes)
    ci, co = len(comm.ins), len(comm.outs)
    n_steps = int(np.prod(grid))
    hooks = comm.hooks(n_steps)

    def wrapped(*refs):
        ins, cins = refs[:n_in], refs[n_in:n_in + ci]
        outs, couts = refs[n_in + ci:n_in + ci + n_out], refs[n_in + ci + n_out:n_in + ci + n_out + co]
        sc, csc = refs[n_in + ci + n_out + co:n_in + ci + n_out + co + n_sc], refs[n_in + ci + n_out + co + n_sc:]
        step = pl.program_id(0)
        for ax in range(1, len(grid)):
            step = step * grid[ax] + pl.program_id(ax)
        for at, fn, after in hooks:
            if not after:
                pl.when(step == at)(functools.partial(fn, cins, couts, csc))
        body(*ins, *outs, *sc)
        for at, fn, after in hooks:
            if after:
                pl.when(step == at)(functools.partial(fn, cins, couts, csc))

    call = pl.pallas_call(wrapped, grid=grid, in_specs=list(in_specs) + [ANY] * ci, out_specs=o_specs + [ANY] * co,
                          out_shape=o_shape + list(comm.outs), scratch_shapes=list(scratch_shapes) + list(comm.scratch),
                          compiler_params=pltpu.CompilerParams(dimension_semantics=("arbitrary",) * len(grid)), name=name)

    def run(*args):
        res = call(*args, *comm.ins)
        return (res[0] if single else res[:n_out]), res[n_out:]
    return run


_K0 = math.sqrt(2.0 / math.pi)
_K1 = 0.044715


def gelu(x):
    return x * (0.5 * (1.0 + jnp.tanh(_K0 * (x + _K1 * (x * x * x)))))


def gelu_grad(x):
    t = jnp.tanh(_K0 * (x + _K1 * (x * x * x)))
    return 0.5 * (1.0 + t) + 0.5 * x * (1.0 - t * t) * (_K0 * (1.0 + 3.0 * _K1 * x * x))


def sigmoid(x):
    return 1.0 / (1.0 + jnp.exp(-x))


def silu(z):
    return z * sigmoid(z)


def silu_grad(z):
    s = sigmoid(z)
    return s * (1.0 + z * (1.0 - s))


def rms_fwd(x, g):
    r = lax.rsqrt(jnp.mean(x * x, axis=-1, keepdims=True) + EPS)
    return x * r * g


def rms_bwd(x, g, dy):
    r = lax.rsqrt(jnp.mean(x * x, axis=-1, keepdims=True) + EPS)
    xh = x * r
    dg = jnp.sum(dy * xh, axis=0, keepdims=True)
    dxh = dy * g
    dx = r * (dxh - xh * jnp.mean(dxh * xh, axis=-1, keepdims=True))
    return dx, dg


def _scan_chunk(a_r, a_i, pr_ref, pi_ref, cr, ci, T, reverse):
    row = lax.broadcasted_iota(jnp.int32, a_r.shape, 0)
    sgn = -1.0 if reverse else 1.0
    d = 1
    while d < T:
        k = (T - d) if reverse else (d - 1)
        wr = pr_ref[k:k + 1, :]
        wi = sgn * pi_ref[k:k + 1, :]
        if reverse:
            yr, yi, keep = pltpu.roll(a_r, T - d, 0), pltpu.roll(a_i, T - d, 0), row < T - d
        else:
            yr, yi, keep = pltpu.roll(a_r, d, 0), pltpu.roll(a_i, d, 0), row >= d
        a_r, a_i = (a_r + jnp.where(keep, wr * yr - wi * yi, 0.0), a_i + jnp.where(keep, wr * yi + wi * yr, 0.0))
        d *= 2
    wr = pr_ref[...]
    wi = sgn * pi_ref[...]
    c_r, c_i = cr[...], ci[...]
    a_r, a_i = a_r + (wr * c_r - wi * c_i), a_i + (wr * c_i + wi * c_r)
    k = 0 if reverse else T - 1
    cr[...] = a_r[k:k + 1, :]
    ci[...] = a_i[k:k + 1, :]
    return a_r, a_i


_NT = (((1,), (1,)), ((), ()))
_TN = (((0,), (0,)), ((), ()))


def s5_fwd(proj, d_skip, Bre, Bim, Cre, Cim, pr, pi, comm=None):
    L = proj.shape[0]
    T, WC = min(SSM_T, L), SSM_WC
    nT = L // T

    def body(u_ref, d_ref, bre_ref, bim_ref, cre_ref, cim_ref, pr_ref, pi_ref, y_ref, yg_ref, sr_ref, si_ref, cr, ci):
        @pl.when(pl.program_id(1) == 0)
        def _():
            cr[...] = jnp.zeros_like(cr)
            ci[...] = jnp.zeros_like(ci)

        u = u_ref[...]
        ub = u.astype(BF16)
        a_r = jnp.dot(ub, bre_ref[0].astype(BF16), preferred_element_type=F32)
        a_i = jnp.dot(ub, bim_ref[0].astype(BF16), preferred_element_type=F32)
        a_r, a_i = _scan_chunk(a_r, a_i, pr_ref, pi_ref, cr, ci, T, False)
        sr_ref[...] = a_r
        si_ref[...] = a_i
        y = (jnp.dot(a_r.astype(BF16), cre_ref[0].astype(BF16), preferred_element_type=F32)
             + jnp.dot(a_i.astype(BF16), cim_ref[0].astype(BF16), preferred_element_type=F32) + d_ref[...] * u)
        y_ref[...] = y
        yg_ref[...] = gelu(y)

    uspec = pl.BlockSpec((T, 128), lambda k, i: (i, k))
    sspec = pl.BlockSpec((T, WC), lambda k, i: (i, k))
    return carried(
        body, comm, grid=(8, nT),
        in_specs=[uspec, pl.BlockSpec((1, 128), lambda k, i: (0, k)),
                  pl.BlockSpec((1, 128, WC), lambda k, i: (k, 0, 0)), pl.BlockSpec((1, 128, WC), lambda k, i: (k, 0, 0)),
                  pl.BlockSpec((1, WC, 128), lambda k, i: (k, 0, 0)), pl.BlockSpec((1, WC, 128), lambda k, i: (k, 0, 0)),
                  pl.BlockSpec((T, WC), lambda k, i: (0, k)), pl.BlockSpec((T, WC), lambda k, i: (0, k))],
        out_specs=[uspec, uspec, sspec, sspec],
        out_shape=[jax.ShapeDtypeStruct((L, 1024), F32)] * 2 + [jax.ShapeDtypeStruct((L, 8 * WC), F32)] * 2,
        scratch_shapes=[pltpu.VMEM((1, WC), F32), pltpu.VMEM((1, WC), F32)],
        semantics=("parallel", "arbitrary"), name='a_ssm')(proj, d_skip, Bre, Bim, Cre, Cim, pr, pi)


def s5_bwd(proj, dyg1, dyg2, y, d_skip, s_re, s_im, Bre, Bim, Cre, Cim, prr, pir, comm=None):
    L = proj.shape[0]
    T, WC = min(SSM_T, L), SSM_WC
    nT = L // T

    def body(u_ref, g1_ref, g2_ref, y_ref, d_ref, sr_ref, si_ref, spr_ref, spi_ref, bre_ref, bim_ref, cre_ref, cim_ref,
             pr_ref, pi_ref, du_ref, dd_ref, dbre_ref, dbim_ref, dcre_ref, dcim_ref, dar_ref, dai_ref, cr, ci):
        i = pl.program_id(1)

        @pl.when(i == 0)
        def _():
            for ref in (cr, ci, dd_ref, dbre_ref, dbim_ref, dcre_ref, dcim_ref, dar_ref, dai_ref):
                ref[...] = jnp.zeros_like(ref)

        u = u_ref[...]
        dy = (g1_ref[...] + g2_ref[...]) * gelu_grad(y_ref[...])
        dd_ref[...] += jnp.sum(dy * u, axis=0, keepdims=True)
        dyb, ub = dy.astype(BF16), u.astype(BF16)
        bre, bim, cre, cim = (r[0].astype(BF16) for r in (bre_ref, bim_ref, cre_ref, cim_ref))
        g_r = lax.dot_general(dyb, cre, _NT, preferred_element_type=F32)
        g_i = lax.dot_general(dyb, cim, _NT, preferred_element_type=F32)
        g_r, g_i = _scan_chunk(g_r, g_i, pr_ref, pi_ref, cr, ci, T, True)
        s_r, s_i = sr_ref[...], si_ref[...]
        row = lax.broadcasted_iota(jnp.int32, (T, WC), 0)
        first = (nT - 1 - i) == 0
        sp_r = jnp.where(row == 0, jnp.where(first, 0.0, spr_ref[7:8, :]), pltpu.roll(s_r, 1, 0))
        sp_i = jnp.where(row == 0, jnp.where(first, 0.0, spi_ref[7:8, :]), pltpu.roll(s_i, 1, 0))
        dar_ref[...] += jnp.sum(g_r * sp_r + g_i * sp_i, axis=0, keepdims=True)
        dai_ref[...] += jnp.sum(g_i * sp_r - g_r * sp_i, axis=0, keepdims=True)
        grb, gib = g_r.astype(BF16), g_i.astype(BF16)
        dcre_ref[0] += lax.dot_general(s_r.astype(BF16), dyb, _TN, preferred_element_type=F32)
        dcim_ref[0] += lax.dot_general(s_i.astype(BF16), dyb, _TN, preferred_element_type=F32)
        dbre_ref[0] += lax.dot_general(ub, grb, _TN, preferred_element_type=F32)
        dbim_ref[0] += lax.dot_general(ub, gib, _TN, preferred_element_type=F32)
        du_ref[...] = (dy * d_ref[...] + lax.dot_general(grb, bre, _NT, preferred_element_type=F32)
                       + lax.dot_general(gib, bim, _NT, preferred_element_type=F32))

    uspec = pl.BlockSpec((T, 128), lambda k, i: (nT - 1 - i, k))
    sspec = pl.BlockSpec((T, WC), lambda k, i: (nT - 1 - i, k))
    pspec = pl.BlockSpec((8, WC), lambda k, i: (jnp.maximum((nT - 1 - i) * (T // 8) - 1, 0), k))
    tab = pl.BlockSpec((T, WC), lambda k, i: (0, k))
    bspec = pl.BlockSpec((1, 128, WC), lambda k, i: (k, 0, 0))
    cspec = pl.BlockSpec((1, WC, 128), lambda k, i: (k, 0, 0))
    return carried(
        body, comm, grid=(8, nT),
        in_specs=[uspec, uspec, uspec, uspec, pl.BlockSpec((1, 128), lambda k, i: (0, k)), sspec, sspec, pspec, pspec,
                  bspec, bspec, cspec, cspec, tab, tab],
        out_specs=[uspec, pl.BlockSpec((1, 128), lambda k, i: (0, k)), bspec, bspec, cspec, cspec,
                   pl.BlockSpec((1, WC), lambda k, i: (0, k)), pl.BlockSpec((1, WC), lambda k, i: (0, k))],
        out_shape=[jax.ShapeDtypeStruct((L, 1024), F32), jax.ShapeDtypeStruct((1, 1024), F32),
                   jax.ShapeDtypeStruct((8, 128, WC), F32), jax.ShapeDtypeStruct((8, 128, WC), F32),
                   jax.ShapeDtypeStruct((8, WC, 128), F32), jax.ShapeDtypeStruct((8, WC, 128), F32),
                   jax.ShapeDtypeStruct((1, 8 * WC), F32), jax.ShapeDtypeStruct((1, 8 * WC), F32)],
        scratch_shapes=[pltpu.VMEM((1, WC), F32), pltpu.VMEM((1, WC), F32)],
        semantics=("parallel", "arbitrary"), name='a_ssm_bwd')(
            proj, dyg1, dyg2, y, d_skip, s_re, s_im, s_re, s_im, Bre, Bim, Cre, Cim, prr, pir)


def s5_discretize(lam_re, lam_im, log_dt, b_re, b_im):
    dt = jnp.exp(log_dt)[:, None]
    mag = jnp.exp(lam_re * dt)
    ab_re = mag * jnp.cos(lam_im * dt)
    ab_im = mag * jnp.sin(lam_im * dt)
    den = lam_re * lam_re + lam_im * lam_im
    nr = ab_re - 1.0
    f_re = (nr * lam_re + ab_im * lam_im) / den
    f_im = (ab_im * lam_re - nr * lam_im) / den
    bb_re = f_re[..., None] * b_re - f_im[..., None] * b_im
    bb_im = f_re[..., None] * b_im + f_im[..., None] * b_re
    return ab_re, ab_im, bb_re, bb_im


_EYE8 = np.eye(8, dtype=np.float32)


def _b_tiles(bb):
    t = bb.transpose(0, 2, 1).reshape(8, 8, SSM_H, SSM_P)
    return jnp.einsum('kghp,gG->kghGp', t, _EYE8).reshape(8, 8 * SSM_H, 8 * SSM_P)


def _b_untile(d):
    t = jnp.einsum('kghGp,gG->kghp', d.reshape(8, 8, SSM_H, 8, SSM_P), _EYE8)
    return t.reshape(SSM_G, SSM_H, SSM_P).transpose(0, 2, 1)


def _c_tiles(c):
    t = c.transpose(0, 2, 1).reshape(8, 8, SSM_P, SSM_H)
    return jnp.einsum('kgph,gG->kgpGh', t, _EYE8).reshape(8, 8 * SSM_P, 8 * SSM_H)


def _c_untile(d):
    t = jnp.einsum('kgpGh,gG->kgph', d.reshape(8, 8, SSM_P, 8, SSM_H), _EYE8)
    return t.reshape(SSM_G, SSM_P, SSM_H).transpose(0, 2, 1)


def s5_powers(ar, ai, T):
    W = ar.shape[1]

    def body(ar_ref, ai_ref, fr_ref, fi_ref, rr_ref, ri_ref):
        fr_ref[0:1, :] = ar_ref[...]
        fi_ref[0:1, :] = ai_ref[...]
        rr_ref[T - 1:T, :] = ar_ref[...]
        ri_ref[T - 1:T, :] = ai_ref[...]
        n = 1
        while n < T:
            cr, ci = fr_ref[0:n, :], fi_ref[0:n, :]
            lr, li = fr_ref[n - 1:n, :], fi_ref[n - 1:n, :]
            fr_ref[n:2 * n, :] = cr * lr - ci * li
            fi_ref[n:2 * n, :] = cr * li + ci * lr
            cr, ci = rr_ref[T - n:T, :], ri_ref[T - n:T, :]
            rr_ref[T - 2 * n:T - n, :] = cr * lr - ci * li
            ri_ref[T - 2 * n:T - n, :] = cr * li + ci * lr
            n *= 2

    spec = pl.BlockSpec((T, SSM_WC), lambda j: (0, j))
    aspec = pl.BlockSpec((1, SSM_WC), lambda j: (0, j))
    return pl.pallas_call(
        body, grid=(W // SSM_WC,), in_specs=[aspec, aspec], out_specs=[spec] * 4,
        out_shape=[jax.ShapeDtypeStruct((T, W), F32)] * 4,
        compiler_params=pltpu.CompilerParams(dimension_semantics=("parallel",)), name='a_powers')(ar, ai)


def layer_a_fwd(h, w, p, comm=None):
    L = h.shape[0]
    proj = mm(h, w['a_w_in'], 'nn', 'a_proj')
    disc = lambda *a: s5_discretize(*a)
    (ab_re, ab_im, bb_re, bb_im), disc_vjp = jax.vjp(disc, p['a_lam_re'][0], p['a_lam_im'][0], p['a_log_dt'][0],
                                                     p['a_b_re'][0], p['a_b_im'][0])
    Bre, Bim = _b_tiles(bb_re), _b_tiles(bb_im)
    Cre, Cim = _c_tiles(p['a_c_re'][0]), -_c_tiles(p['a_c_im'][0])
    T = min(SSM_T, L)
    pr, pi, prr, pir = s5_powers(ab_re.reshape(1, -1), ab_im.reshape(1, -1), T)
    (y, yg, s_re, s_im), carried_out = s5_fwd(proj, p['a_d'], Bre, Bim, Cre, Cim, pr, pi, comm=comm)
    gl = mm(yg, w['a_w_glu'], 'nn', 'a_glu')

    def f2(yg_, gl_, z, bg):
        return [yg_ * sigmoid(gl_ + bg) * silu(z)], []
    (po,), _ = rowwise(f2, [rw(yg), rw(gl), rw(proj, 1024, 1)], [p['a_b_glu']], [(1024, BF16)], [], 256, 'a_gate')
    yb = mm(po, w['a_w_out'], 'nn', 'a_out')
    saved = dict(carried=carried_out, h=h, proj=proj, disc_vjp=disc_vjp, Bre=Bre, Bim=Bim, Cre=Cre, Cim=Cim, prr=prr, pir=pir, s_re=s_re,
                 s_im=s_im, y=y, yg=yg, gl=gl, po=po)
    return yb, saved


def layer_a_bwd(dyb, w, p, sv, comm=None):
    g = {}
    dpo = mm(dyb, w['a_w_out'], 'nt', 'a_dpo')
    g['a_w_out'] = mm(sv['po'], dyb, 'tn', 'a_dwout')
    proj = sv['proj']

    def f1(dpo_, yg, gl, z, bg):
        sg = sigmoid(gl + bg)
        sz = silu(z)
        dm = dpo_ * sz
        dz = dpo_ * (yg * sg) * silu_grad(z)
        dgl = dm * yg * sg * (1.0 - sg)
        return [dz, dm * sg, dgl], [jnp.sum(dgl, axis=0, keepdims=True)]
    (dz, dyg1, dgl), (db_glu,) = rowwise(f1, [rw(dpo), rw(sv['yg']), rw(sv['gl']), rw(proj, 1024, 1)], [p['a_b_glu']],
                                          [(1024, F32), (1024, F32), (1024, BF16)], [(1, 1024)], 256, 'a_gate_bwd')
    g['a_b_glu'] = db_glu
    g['a_w_glu'] = mm(sv['yg'], dgl, 'tn', 'a_dwglu')
    dyg2 = mm(dgl, w['a_w_glu'], 'nt', 'a_dyg2')

    (du, dd, dBre, dBim, dCre, dCim, da_re, da_im), g['carried'] = s5_bwd(
        proj, dyg1, dyg2, sv['y'], p['a_d'], sv['s_re'], sv['s_im'], sv['Bre'], sv['Bim'], sv['Cre'], sv['Cim'],
        sv['prr'], sv['pir'], comm=comm)
    g['a_d'] = dd
    dCim = -dCim

    def f3(du_, dz_):
        return [jnp.concatenate([du_, dz_], axis=1)], []
    (dproj,), _ = rowwise(f3, [rw(du), rw(dz)], [], [(2048, BF16)], [], 256, 'a_dproj')
    dlr, dli, dldt, dbr, dbi = sv['disc_vjp']((da_re.reshape(SSM_G, SSM_P), da_im.reshape(SSM_G, SSM_P),
                                               _b_untile(dBre), _b_untile(dBim)))
    g['a_lam_re'], g['a_lam_im'], g['a_log_dt'] = dlr[None], dli[None], dldt[None]
    g['a_b_re'], g['a_b_im'] = dbr[None], dbi[None]
    g['a_c_re'], g['a_c_im'] = _c_untile(dCre)[None], _c_untile(dCim)[None]
    g['a_w_in'] = mm(sv['h'], dproj, 'tn', 'a_dwin')
    dh = mm(dproj, w['a_w_in'], 'nt', 'a_dh')
    return dh, g


def _t5_bucket_np():
    qi = np.arange(WINDOW)[:, None]
    kj = np.arange(2 * WINDOW)[None, :]
    dist = np.maximum(qi + WINDOW - kj, 0)
    max_exact = REL_BUCKETS // 2
    dist_f = np.maximum(dist, 1).astype(np.float32)
    large = max_exact + (np.log(dist_f / np.float32(max_exact)) / np.float32(math.log(REL_MAX_DIST / max_exact))
                         * np.float32(REL_BUCKETS - max_exact)).astype(np.int32)
    large = np.minimum(large, REL_BUCKETS - 1)
    return np.where(dist < max_exact, dist, large).astype(np.int32)


SWA_GRP = SWA_HEADS // SWA_KV


def _swa_kv(kvp, kvc, kvh):
    kb = jnp.concatenate([kvp[:, kvh * 64:(kvh + 1) * 64], kvc[:, kvh * 64:(kvh + 1) * 64]], 0).astype(BF16)
    vb = jnp.concatenate([kvp[:, 128 + kvh * 64:128 + (kvh + 1) * 64], kvc[:, 128 + kvh * 64:128 + (kvh + 1) * 64]],
                         0).astype(BF16)
    return kb, vb


def _swa_stack(x, kvh):
    return jnp.concatenate([x[:, (kvh * SWA_GRP + g) * 64:(kvh * SWA_GRP + g + 1) * 64] for g in range(SWA_GRP)],
                           axis=0).astype(BF16)


def _swa_group(bias_ref, kvh):
    return bias_ref[kvh * SWA_GRP:(kvh + 1) * SWA_GRP].reshape(SWA_GRP * WINDOW, 2 * WINDOW)


def _swa_sinks(sink_ref, kvh):
    return jnp.concatenate([jnp.broadcast_to(sink_ref[0:1, kvh * SWA_GRP + g:kvh * SWA_GRP + g + 1], (WINDOW, 1))
                            for g in range(SWA_GRP)], axis=0)


def _swa_probs(q, kb, bias_h, sink, valid):
    s = lax.dot_general(q, kb, (((1,), (1,)), ((), ())), preferred_element_type=F32) * (HEAD_DIM ** -0.5)
    s = jnp.where(valid, s + bias_h, NEG_INF)
    m = jnp.maximum(jnp.max(s, axis=-1, keepdims=True), sink)
    e = jnp.exp(s - m)
    es = jnp.exp(sink - m)
    den = jnp.sum(e, axis=-1, keepdims=True) + es
    return e / den, es / den


def _swa_valid(n):
    qi = lax.broadcasted_iota(jnp.int32, (SWA_GRP * WINDOW, 2 * WINDOW), 0) & (WINDOW - 1)
    kj = lax.broadcasted_iota(jnp.int32, (SWA_GRP * WINDOW, 2 * WINDOW), 1)
    dist = qi + WINDOW - kj
    return (dist >= 0) & (dist < WINDOW) & ((kj >= WINDOW) | (n > 0))


def swa_fwd(proj, bias, sinks, comm=None):
    L = proj.shape[0]

    def body(z_ref, q_ref, kvc_ref, kvp_ref, bias_ref, sink_ref, o_ref, po_ref):
        n = pl.program_id(0)
        valid = _swa_valid(n)
        q, kvc, kvp = q_ref[...], kvc_ref[...], kvp_ref[...]
        outs = []
        for kvh in range(SWA_KV):
            kb, vb = _swa_kv(kvp, kvc, kvh)
            p, _ = _swa_probs(_swa_stack(q, kvh), kb, _swa_group(bias_ref, kvh), _swa_sinks(sink_ref, kvh), valid)
            o8 = jnp.dot(p.astype(BF16), vb, preferred_element_type=F32)
            outs += [o8[g * WINDOW:(g + 1) * WINDOW] for g in range(SWA_GRP)]
        o = jnp.concatenate(outs, axis=1)
        o_ref[...] = o
        po_ref[...] = (o * silu(z_ref[...])).astype(po_ref.dtype)

    return carried(
        body, comm, grid=(L // WINDOW,),
        in_specs=[pl.BlockSpec((WINDOW, 1024), lambda n: (n, 0)), pl.BlockSpec((WINDOW, 1024), lambda n: (n, 1)),
                  pl.BlockSpec((WINDOW, 256), lambda n: (n, 8)),
                  pl.BlockSpec((WINDOW, 256), lambda n: (jnp.maximum(n - 1, 0), 8)),
                  pl.BlockSpec((SWA_HEADS, WINDOW, 2 * WINDOW), lambda n: (0, 0, 0)),
                  pl.BlockSpec((1, SWA_HEADS), lambda n: (0, 0))],
        out_specs=[pl.BlockSpec((WINDOW, 1024), lambda n: (n, 0))] * 2,
        out_shape=[jax.ShapeDtypeStruct((L, 1024), F32), jax.ShapeDtypeStruct((L, 1024), BF16)],
        semantics=("parallel",), name='b_attn')(proj, proj, proj, proj, bias, sinks)


def swa_bwd(proj, do, bias, sinks, comm=None):
    L = proj.shape[0]

    def body(q_ref, kvc_ref, kvp_ref, do_ref, bias_ref, sink_ref, dq_ref, dkv_ref, dbias_ref, dsink_ref):
        n = pl.program_id(0)

        @pl.when(n == 0)
        def _():
            dkv_ref[...] = jnp.zeros_like(dkv_ref)
            dbias_ref[...] = jnp.zeros_like(dbias_ref)
            dsink_ref[...] = jnp.zeros_like(dsink_ref)

        valid = _swa_valid(n)
        q, kvc, kvp, do_ = q_ref[...], kvc_ref[...], kvp_ref[...], do_ref[...]
        dqs, dks, dvs, dsk = [], [], [], []
        for kvh in range(SWA_KV):
            kb, vb = _swa_kv(kvp, kvc, kvh)
            q8, do8 = _swa_stack(q, kvh), _swa_stack(do_, kvh)
            p, ps = _swa_probs(q8, kb, _swa_group(bias_ref, kvh), _swa_sinks(sink_ref, kvh), valid)
            dp = lax.dot_general(do8, vb, (((1,), (1,)), ((), ())), preferred_element_type=F32)
            delta = jnp.sum(p * dp, axis=-1, keepdims=True)
            ds = p * (dp - delta)
            col = -ps * delta
            dsk += [jnp.sum(col[g * WINDOW:(g + 1) * WINDOW], axis=0, keepdims=True) for g in range(SWA_GRP)]
            dbias_ref[kvh * SWA_GRP:(kvh + 1) * SWA_GRP] += ds.reshape(SWA_GRP, WINDOW, 2 * WINDOW)
            dsb = (ds * (HEAD_DIM ** -0.5)).astype(BF16)
            dq8 = jnp.dot(dsb, kb, preferred_element_type=F32)
            dqs += [dq8[g * WINDOW:(g + 1) * WINDOW] for g in range(SWA_GRP)]
            dks.append(lax.dot_general(dsb, q8, (((0,), (0,)), ((), ())), preferred_element_type=F32))
            dvs.append(lax.dot_general(p.astype(BF16), do8, (((0,), (0,)), ((), ())), preferred_element_type=F32))
        dq_ref[...] = jnp.concatenate(dqs, axis=1)
        dsink_ref[...] += jnp.concatenate(dsk, axis=1)
        both = jnp.concatenate(dks + dvs, axis=1)
        r_cur = pl.multiple_of(n * WINDOW, WINDOW)
        r_prev = pl.multiple_of(jnp.maximum(n - 1, 0) * WINDOW, WINDOW)
        dkv_ref[pl.ds(r_prev, WINDOW), :] += both[:WINDOW]
        dkv_ref[pl.ds(r_cur, WINDOW), :] += both[WINDOW:]

    return carried(
        body, comm, grid=(L // WINDOW,),
        in_specs=[pl.BlockSpec((WINDOW, 1024), lambda n: (n, 1)), pl.BlockSpec((WINDOW, 256), lambda n: (n, 8)),
                  pl.BlockSpec((WINDOW, 256), lambda n: (jnp.maximum(n - 1, 0), 8)),
                  pl.BlockSpec((WINDOW, 1024), lambda n: (n, 0)),
                  pl.BlockSpec((SWA_HEADS, WINDOW, 2 * WINDOW), lambda n: (0, 0, 0)),
                  pl.BlockSpec((1, SWA_HEADS), lambda n: (0, 0))],
        out_specs=[pl.BlockSpec((WINDOW, 1024), lambda n: (n, 0)), pl.BlockSpec((L, 256), lambda n: (0, 0)),
                   pl.BlockSpec((SWA_HEADS, WINDOW, 2 * WINDOW), lambda n: (0, 0, 0)),
                   pl.BlockSpec((1, SWA_HEADS), lambda n: (0, 0))],
        out_shape=[jax.ShapeDtypeStruct((L, 1024), F32), jax.ShapeDtypeStruct((L, 256), F32),
                   jax.ShapeDtypeStruct((SWA_HEADS, WINDOW, 2 * WINDOW), F32), jax.ShapeDtypeStruct((1, SWA_HEADS), F32)],
        semantics=("arbitrary",), name='b_attn_bwd')(proj, proj, proj, do, bias, sinks)


def swa_bias(rel_bias):
    def body(bk_ref, rb_ref, o_ref):
        bk = bk_ref[...]
        for h in range(SWA_HEADS):
            acc = jnp.zeros((WINDOW, 2 * WINDOW), F32)
            for b in range(REL_BUCKETS):
                acc = jnp.where(bk == b, rb_ref[b, h], acc)
            o_ref[h] = acc

    return pl.pallas_call(
        body, out_shape=jax.ShapeDtypeStruct((SWA_HEADS, WINDOW, 2 * WINDOW), F32),
        in_specs=[pl.BlockSpec(memory_space=pltpu.VMEM), pl.BlockSpec(memory_space=pltpu.SMEM)],
        out_specs=pl.BlockSpec(memory_space=pltpu.VMEM), name='b_bias')(jnp.asarray(_t5_bucket_np()), rel_bias)


def layer_b_fwd(h, w, p, comm=None):
    proj = mm(h, w['b_w_in'], 'nn', 'b_proj')
    bias = swa_bias(p['rel_bias'])
    (o, po), carried_out = swa_fwd(proj, bias, p['b_sinks'], comm=comm)
    yb = mm(po, w['b_w_out'], 'nn', 'b_out')
    return yb, dict(carried=carried_out, h=h, proj=proj, bias=bias, o=o, po=po)


def layer_b_bwd(dyb, w, p, sv, comm=None):
    g = {}
    dpo = mm(dyb, w['b_w_out'], 'nt', 'b_dpo')
    g['b_w_out'] = mm(sv['po'], dyb, 'tn', 'b_dwout')
    proj = sv['proj']

    def f1(dpo_, o, z):
        return [dpo_ * silu(z), dpo_ * o * silu_grad(z)], []
    (do, dz), _ = rowwise(f1, [rw(dpo), rw(sv['o']), rw(proj, 1024, 0)], [], [(1024, BF16), (1024, F32)], [], 256, 'b_gate_bwd')
    (dq, dkv, dbias, dsinks), g['carried'] = swa_bwd(proj, do, sv['bias'], p['b_sinks'], comm=comm)
    g['b_sinks'] = dsinks
    onehot = jnp.asarray(np.eye(REL_BUCKETS, dtype=np.float32)[_t5_bucket_np().reshape(-1)])

    def f2(db, oh):
        return [], [lax.dot_general(db, oh, (((1,), (0,)), ((), ())), preferred_element_type=F32,
                                    precision=lax.Precision.HIGHEST)]
    _, (drel,) = rowwise(f2, [(dbias.reshape(SWA_HEADS, -1), pl.BlockSpec((SWA_HEADS, 4096), lambda i: (0, i))),
                              (onehot, pl.BlockSpec((4096, REL_BUCKETS), lambda i: (i, 0)))], [], [],
                         [(SWA_HEADS, REL_BUCKETS)], 4096, 'b_drel', n_steps=(2 * WINDOW * WINDOW) // 4096)
    g['rel_bias'] = drel.T

    def f3(dz_, dq_, dkv_):
        return [jnp.concatenate([dz_, dq_, dkv_], axis=1)], []
    (dproj,), _ = rowwise(f3, [rw(dz), rw(dq), rw(dkv)], [], [(2304, BF16)], [], 256, 'b_dproj')
    g['b_w_in'] = mm(sv['h'], dproj, 'tn', 'b_dwin')
    dh = mm(dproj, w['b_w_in'], 'nt', 'b_dh')
    return dh, g


MLA_SCALE = (MLA_NOPE + MLA_ROPE) ** -0.5


def _rope_tables(L):
    inv = ROPE_BASE ** (-jnp.arange(0, MLA_ROPE, 2, dtype=F32) / MLA_ROPE)
    ang = jnp.arange(L, dtype=F32)[:, None] * inv[None, :]
    c, s = jnp.cos(ang), jnp.sin(ang)
    one, zero, pad = jnp.ones((L, 128), F32), jnp.zeros((L, 128), F32), jnp.zeros((L, 64), F32)
    return (jnp.concatenate([one, c, c, c, c, pad], 1), jnp.concatenate([zero, s, s, s, s, pad], 1))


def _rot(x, transpose=False):
    w = x.shape[1]
    lane = lax.broadcasted_iota(jnp.int32, x.shape, 1)
    up = pltpu.roll(x, w - 16, 1)
    dn = pltpu.roll(x, 16, 1)
    first = (lane % 32) < 16
    return jnp.where(first, up, -dn) if transpose else jnp.where(first, -up, dn)


MLA_QT = 512


def _mla_exp(qf, kf, t, qt):
    n_k = kf.shape[0]
    s = lax.dot_general(qf, kf, (((1,), (1,)), ((), ())), preferred_element_type=F32) * MLA_SCALE
    qpos = t * qt + lax.broadcasted_iota(jnp.int32, (qt, n_k), 0)
    kpos = lax.broadcasted_iota(jnp.int32, (qt, n_k), 1)
    s = jnp.where(kpos <= qpos, s, NEG_INF)
    e = jnp.exp(s - jnp.max(s, axis=-1, keepdims=True))
    return e, jnp.sum(e, axis=-1, keepdims=True)


def _mla_heads(q, kv, kr):
    out = []
    for j in range(2):
        qf = jnp.concatenate([q[:, j * 64:(j + 1) * 64], q[:, 128 + j * 32:128 + (j + 1) * 32]], axis=1)
        kf = jnp.concatenate([kv[:, j * 64:(j + 1) * 64], kr], axis=1)
        out.append((qf, kf, kv[:, 128 + j * 64:128 + (j + 1) * 64]))
    return out


def mla_fwd(q, kv, kr, comm=None):
    L = q.shape[0]
    qt = min(MLA_QT, L)
    nq = L // qt

    def body(q_ref, kv_ref, kr_ref, o_ref):
        for t in range(nq):
            @pl.when(pl.program_id(1) == t)
            def _(t=t):
                n_k = (t + 1) * qt
                outs = []
                for qf, kf, v in _mla_heads(q_ref[...], kv_ref[0:n_k, :], kr_ref[0:n_k, 0:MLA_ROPE]):
                    e, den = _mla_exp(qf, kf, t, qt)
                    outs.append(jnp.dot(e.astype(BF16), v, preferred_element_type=F32) / den)
                o_ref[...] = jnp.concatenate(outs, axis=1)

    return carried(
        body, comm, grid=(MLA_HEADS // 2, nq),
        in_specs=[pl.BlockSpec((qt, 256), lambda hp, n: (n, hp)), pl.BlockSpec((L, 256), lambda hp, n: (0, hp)),
                  pl.BlockSpec((L, 128), lambda hp, n: (0, 0))],
        out_specs=pl.BlockSpec((qt, 128), lambda hp, n: (n, hp)), out_shape=jax.ShapeDtypeStruct((L, 1024), F32),
        semantics=("parallel", "parallel"), name='c_attn')(q, kv, kr)


def mla_bwd(q, kv, kr, do, comm=None):
    L = q.shape[0]
    qt = min(MLA_QT, L)
    nq = L // qt

    def body(q_ref, kv_ref, kr_ref, do_ref, dq_ref, dkv_ref, dkr_ref):
        @pl.when(pl.program_id(1) == 0)
        def _():
            dkv_ref[...] = jnp.zeros_like(dkv_ref)
            dkr_ref[...] = jnp.zeros_like(dkr_ref)

        for t in range(nq):
            @pl.when(pl.program_id(1) == t)
            def _(t=t):
                n_k = (t + 1) * qt
                do_ = do_ref[...]
                dqn, dqr, dkn, dvs = [], [], [], []
                dkr = jnp.zeros((n_k, MLA_ROPE), F32)
                for j, (qf, kf, v) in enumerate(_mla_heads(q_ref[...], kv_ref[0:n_k, :], kr_ref[0:n_k, 0:MLA_ROPE])):
                    doh = do_[:, j * 64:(j + 1) * 64]
                    e, den = _mla_exp(qf, kf, t, qt)
                    p = e * (1.0 / den)
                    dp = lax.dot_general(doh, v, (((1,), (1,)), ((), ())), preferred_element_type=F32)
                    ds = (p * (dp - jnp.sum(p * dp, axis=-1, keepdims=True)) * MLA_SCALE).astype(BF16)
                    dqf = jnp.dot(ds, kf, preferred_element_type=F32)
                    dkf = lax.dot_general(ds, qf, (((0,), (0,)), ((), ())), preferred_element_type=F32)
                    dvs.append(lax.dot_general(p.astype(BF16), doh, (((0,), (0,)), ((), ())), preferred_element_type=F32))
                    dqn.append(dqf[:, :MLA_NOPE])
                    dqr.append(dqf[:, MLA_NOPE:])
                    dkn.append(dkf[:, :MLA_NOPE])
                    dkr = dkr + dkf[:, MLA_NOPE:]
                dq_ref[...] = jnp.concatenate(dqn + dqr + [jnp.zeros((qt, 64), F32)], axis=1)
                dkv_ref[0:n_k, :] += jnp.concatenate(dkn + dvs, axis=1)
                dkr_ref[0, 0:n_k, :] += jnp.concatenate([dkr, jnp.zeros((n_k, 128 - MLA_ROPE), F32)], axis=1)

    return carried(
        body, comm, grid=(MLA_HEADS // 2, nq),
        in_specs=[pl.BlockSpec((qt, 256), lambda hp, n: (n, hp)), pl.BlockSpec((L, 256), lambda hp, n: (0, hp)),
                  pl.BlockSpec((L, 128), lambda hp, n: (0, 0)), pl.BlockSpec((qt, 128), lambda hp, n: (n, hp))],
        out_specs=[pl.BlockSpec((qt, 256), lambda hp, n: (n, hp)), pl.BlockSpec((L, 256), lambda hp, n: (0, hp)),
                   pl.BlockSpec((1, L, 128), lambda hp, n: (hp, 0, 0))],
        out_shape=[jax.ShapeDtypeStruct((L, 2048), F32), jax.ShapeDtypeStruct((L, 2048), F32),
                   jax.ShapeDtypeStruct((MLA_HEADS // 2, L, 128), F32)],
        semantics=("parallel", "arbitrary"), name='c_attn_bwd')(q, kv, kr, do)


def layer_c_fwd(h, w, p, comm=None):
    L = h.shape[0]
    proj = mm(h, w['c_w_in'], 'nn', 'c_proj')

    def f1(c, gq, gk):
        return [rms_fwd(c[:, :768], gq), rms_fwd(c[:, 768:], gk)], []
    (cqn, ckvn), _ = rowwise(f1, [rw(proj, 1024, 1)], [p['c_q_norm'], p['c_kv_norm']], [(768, BF16), (256, BF16)], [],
                             256, 'c_norms')
    qf = mm(cqn, w['c_w_uq'], 'nn', 'c_uq')
    kvf = mm(ckvn, w['c_w_ukv'], 'nn', 'c_ukv', out_dtype=BF16)
    cos, sin = _rope_tables(L)

    def f2(q_, kr_, c, s):
        c8, s8 = jnp.tile(c, (1, 8)), jnp.tile(s, (1, 8))
        return [q_ * c8 + _rot(q_) * s8, kr_ * c[:, 128:] + _rot(kr_) * s[:, 128:]], []
    (q, kr), _ = rowwise(f2, [rw(qf), rw(proj, 128, 16), rw(cos), rw(sin)], [], [(2048, BF16), (128, BF16)], [], 256,
                         'c_rope')
    o, carried_out = mla_fwd(q, kvf, kr, comm=comm)

    def f3(o_, z):
        return [o_ * silu(z)], []
    (po,), _ = rowwise(f3, [rw(o), rw(proj, 1024, 0)], [], [(1024, BF16)], [], 256, 'c_gate')
    yb = mm(po, w['c_w_out'], 'nn', 'c_out')
    return yb, dict(carried=carried_out, h=h, proj=proj, cqn=cqn, ckvn=ckvn, q=q, kv=kvf, kr=kr, o=o, po=po, cos=cos, sin=sin)


def layer_c_bwd(dyb, w, p, sv, comm=None):
    g = {}
    dpo = mm(dyb, w['c_w_out'], 'nt', 'c_dpo')
    g['c_w_out'] = mm(sv['po'], dyb, 'tn', 'c_dwout')
    proj = sv['proj']
    L = proj.shape[0]

    def f1(dpo_, o, z):
        return [dpo_ * silu(z), dpo_ * o * silu_grad(z)], []
    (do, dz), _ = rowwise(f1, [rw(dpo), rw(sv['o']), rw(proj, 1024, 0)], [], [(1024, BF16), (1024, F32)], [], 256,
                          'c_gate_bwd')
    (dq, dkvf, dkr8), g['carried'] = mla_bwd(sv['q'], sv['kv'], sv['kr'], do, comm=comm)

    def f2(dq_, dkr_, c, s):
        c8, s8 = jnp.tile(c, (1, 8)), jnp.tile(s, (1, 8))
        dk = jnp.sum(dkr_, axis=0)
        return [dq_ * c8 + _rot(dq_ * s8, True), dk * c[:, 128:] + _rot(dk * s[:, 128:], True)], []
    tl = 256
    (dqf, dkr), _ = rowwise(f2, [rw(dq), (dkr8, pl.BlockSpec((8, tl, 128), lambda i: (0, i, 0))), rw(sv['cos']),
                                 rw(sv['sin'])], [], [(2048, BF16), (128, F32)], [], tl, 'c_rope_bwd')
    g['c_w_uq'] = mm(sv['cqn'], dqf, 'tn', 'c_dwuq')
    g['c_w_ukv'] = mm(sv['ckvn'], dkvf, 'tn', 'c_dwukv')
    dcqn = mm(dqf, w['c_w_uq'], 'nt', 'c_dcqn')
    dckvn = mm(dkvf, w['c_w_ukv'], 'nt', 'c_dckvn')

    def f3(c, dq_, dk_, dz_, dkr_, gq, gk):
        dcq, dgq = rms_bwd(c[:, :768], gq, dq_)
        dckv, dgk = rms_bwd(c[:, 768:], gk, dk_)
        return [jnp.concatenate([dz_, dcq, dckv, dkr_], axis=1)], [dgq, dgk]
    (dproj,), (dgq, dgk) = rowwise(f3, [rw(proj, 1024, 1), rw(dcqn), rw(dckvn), rw(dz), rw(dkr)],
                                   [p['c_q_norm'], p['c_kv_norm']], [(2176, BF16)], [(1, 768), (1, 256)], 256, 'c_dproj')
    g['c_q_norm'], g['c_kv_norm'] = dgq, dgk
    g['c_w_in'] = mm(sv['h'], dproj, 'tn', 'c_dwin')
    dh = mm(dproj, w['c_w_in'], 'nt', 'c_dh')
    return dh, g


def _sgu_mix(wm, v, transpose):
    outs = []
    dims = (((0,), (0,)), ((), ())) if transpose else (((1,), (0,)), ((), ()))
    for gi in range(SGU_G):
        outs.append(lax.dot_general(wm[gi], v[:, gi * SGU_C:(gi + 1) * SGU_C].astype(BF16), dims,
                                    preferred_element_type=F32))
    return jnp.concatenate(outs, axis=1)


def _sgu_wmask(ws):
    t = lax.broadcasted_iota(jnp.int32, (SGU_T, SGU_T), 0)
    s = lax.broadcasted_iota(jnp.int32, (SGU_T, SGU_T), 1)
    return jnp.where((s <= t)[None], ws, 0.0).astype(BF16)


def _ln_stats(v):
    mu = jnp.mean(v, axis=-1, keepdims=True)
    vc = v - mu
    rstd = lax.rsqrt(jnp.mean(vc * vc, axis=-1, keepdims=True) + EPS)
    return vc * rstd, rstd


def layer_d_fwd(h, w, p):
    proj = mm(h, w['d_w_in'], 'nn', 'd_proj')
    bias = jnp.repeat(p['d_b_s'][0].T, SGU_C, axis=1)

    def f1(u_, v_, z, ws, lg, lb, bs):
        xh, _ = _ln_stats(gelu(v_))
        s = _sgu_mix(_sgu_wmask(ws), xh * lg + lb, False) + bs
        return [gelu(u_) * s * silu(z)], []
    (po,), _ = rowwise(f1, [rw(proj, 1024, 0), rw(proj, 1024, 1), rw(proj, 1024, 2)],
                       [p['d_w_s'][0], p['d_ln_g'], p['d_ln_b'], bias], [(1024, BF16)], [], SGU_T, 'd_mix')
    yb = mm(po, w['d_w_out'], 'nn', 'd_out')
    return yb, dict(h=h, proj=proj, po=po, bias=bias)


def layer_d_bwd(dyb, w, p, sv):
    g = {}
    dpo = mm(dyb, w['d_w_out'], 'nt', 'd_dpo')
    g['d_w_out'] = mm(sv['po'], dyb, 'tn', 'd_dwout')
    proj = sv['proj']

    def f1(dpo_, u_, v_, z, ws, lg, lb, bs):
        wm = _sgu_wmask(ws)
        gv = gelu(v_)
        xh, rstd = _ln_stats(gv)
        vn = xh * lg + lb
        s = _sgu_mix(wm, vn, False) + bs
        gu, sz = gelu(u_), silu(z)
        du = dpo_ * s * sz
        ds = dpo_ * gu * sz
        dz = dpo_ * gu * s * silu_grad(z)
        dsb = ds.astype(BF16)
        dws = jnp.stack([lax.dot_general(dsb[:, gi * SGU_C:(gi + 1) * SGU_C], vn[:, gi * SGU_C:(gi + 1) * SGU_C].astype(BF16),
                                         (((1,), (1,)), ((), ())), preferred_element_type=F32) for gi in range(SGU_G)])
        dvn = _sgu_mix(wm, ds, True)
        dlg = jnp.sum(dvn * xh, axis=0, keepdims=True)
        dlb = jnp.sum(dvn, axis=0, keepdims=True)
        dxh = dvn * lg
        dgv = rstd * (dxh - jnp.mean(dxh, axis=-1, keepdims=True) - xh * jnp.mean(dxh * xh, axis=-1, keepdims=True))
        return ([jnp.concatenate([du * gelu_grad(u_), dgv * gelu_grad(v_), dz], axis=1)], [dws, ds, dlg, dlb])
    (dproj,), (dws, dbs, dlg, dlb) = rowwise(
        f1, [rw(dpo), rw(proj, 1024, 0), rw(proj, 1024, 1), rw(proj, 1024, 2)],
        [p['d_w_s'][0], p['d_ln_g'], p['d_ln_b'], sv['bias']], [(3072, BF16)],
        [(SGU_G, SGU_T, SGU_T), (SGU_T, 1024), (1, 1024), (1, 1024)], SGU_T, 'd_mix_bwd')
    tril = np.tril(np.ones((SGU_T, SGU_T), dtype=bool))
    g['d_w_s'] = jnp.where(tril[None], dws, 0.0)[None]
    g['d_b_s'] = dbs.reshape(SGU_T, SGU_G, SGU_C).sum(-1).T[None]
    g['d_ln_g'], g['d_ln_b'] = dlg, dlb
    g['d_w_in'] = mm(sv['h'], dproj, 'tn', 'd_dwin')
    dh = mm(dproj, w['d_w_in'], 'nt', 'd_dh')
    return dh, g


def _coords():
    return lax.axis_index("x"), lax.axis_index("y"), lax.axis_index("c")


class AllGather:
    def __init__(self, x):
        self.ins = [x]
        self.outs = [jax.ShapeDtypeStruct((N_DEV,) + x.shape, x.dtype)]
        self.scratch = [pltpu.SemaphoreType.DMA((7,)), pltpu.SemaphoreType.DMA((7,)), pltpu.SemaphoreType.DMA(())]

    def hooks(self, n_steps):
        return [(0, functools.partial(self.phase, 0), False), ((n_steps * 5) // 8, functools.partial(self.phase, 1), False),
                (n_steps - 1, functools.partial(self.phase, 2), True)]

    @staticmethod
    def phase(which, ins, outs, scratch):
        (x_ref,), (out_ref,), (send_sems, recv_sems, local_sem) = ins, outs, scratch
        x_, y_, c_ = _coords()
        me, sibling = (x_, y_, c_), (x_, y_, 1 - c_)
        chips = [(1 - x_, y_), (x_, 1 - y_), (1 - x_, 1 - y_)]

        def slot(px, py, pc):
            return out_ref.at[4 * px + 2 * py + pc]

        def copy(k, block, to, src=None):
            return pltpu.make_async_remote_copy(src_ref=slot(*block) if src is None else src, dst_ref=slot(*block),
                                                send_sem=send_sems.at[k], recv_sem=recv_sems.at[k], device_id=to,
                                                device_id_type=MESH)

        mine = pltpu.make_async_copy(x_ref, slot(*me), local_sem)
        first = [copy(0, me, sibling, src=x_ref)]
        first += [copy(1 + j, me, (*chip, c_), src=x_ref) for j, chip in enumerate(chips)]
        passed = [copy(4 + j, (*chip, c_), sibling) for j, chip in enumerate(chips)]
        if which == 0:
            mine.start()
            for cp in first:
                cp.start()
        elif which == 1:
            for j, chip in enumerate(chips):
                copy(1 + j, (*chip, c_), me).wait_recv()
                passed[j].start()
        else:
            copy(0, sibling, me).wait_recv()
            for j, chip in enumerate(chips):
                copy(4 + j, (*chip, 1 - c_), me).wait_recv()
            for cp in first + passed:
                cp.wait_send()
            mine.wait()


class ChipExchange:
    def __init__(self, part):
        self.ins = [part]
        self.outs = [jax.ShapeDtypeStruct((3,) + part.shape[1:], part.dtype)]
        self.scratch = [pltpu.SemaphoreType.DMA((3,)), pltpu.SemaphoreType.DMA((3,))]

    def hooks(self, n_steps):
        return [(0, functools.partial(self.phase, 0), False), (n_steps - 1, functools.partial(self.phase, 1), True)]

    @staticmethod
    def phase(which, ins, outs, scratch):
        (p_ref,), (land_ref,), (send_sems, recv_sems) = ins, outs, scratch
        x_, y_, c_ = _coords()
        copies = []
        for r, (fx, fy) in enumerate([(1, 0), (0, 1), (1, 1)]):
            tx = jnp.where(fx == 1, 1 - x_, x_)
            ty = jnp.where(fy == 1, 1 - y_, y_)
            copies.append(pltpu.make_async_remote_copy(src_ref=p_ref.at[2 * tx + ty], dst_ref=land_ref.at[r],
                                                       send_sem=send_sems.at[r], recv_sem=recv_sems.at[r],
                                                       device_id=(tx, ty, c_), device_id_type=MESH))
        if which == 0:
            for cp in copies:
                cp.start()
        else:
            for cp in copies:
                cp.wait_recv()
            for cp in copies:
                cp.wait_send()


class Both:
    def __init__(self, a, b):
        self.parts = (a, b)
        self.ins, self.outs, self.scratch = a.ins + b.ins, a.outs + b.outs, a.scratch + b.scratch

    def hooks(self, n_steps):
        res, oi, oo, osc = [], 0, 0, 0
        for p in self.parts:
            sl = (slice(oi, oi + len(p.ins)), slice(oo, oo + len(p.outs)), slice(osc, osc + len(p.scratch)))
            res += [(at, functools.partial(self.sub, fn, sl), after) for at, fn, after in p.hooks(n_steps)]
            oi, oo, osc = oi + len(p.ins), oo + len(p.outs), osc + len(p.scratch)
        return res

    @staticmethod
    def sub(fn, sl, ins, outs, scratch):
        fn(ins[sl[0]], outs[sl[1]], scratch[sl[2]])


def run_comm(comm, name):
    def body(*refs):
        ci, co = len(comm.ins), len(comm.outs)
        for _, fn, _ in comm.hooks(1):
            fn(refs[:ci], refs[ci:ci + co], refs[ci + co:])

    return pl.pallas_call(body, out_shape=list(comm.outs), in_specs=[ANY] * len(comm.ins),
                          out_specs=[ANY] * len(comm.outs), scratch_shapes=list(comm.scratch), name=name)(*comm.ins)


def all_gather(x, name):
    return run_comm(AllGather(x), name)[0]


def rs_sibling(gfull, tag):
    _, R, C = gfull.shape

    def body(g_ref, land_ref, send_sems, recv_sems):
        x_, y_, c_ = _coords()
        copies = []
        for k in range(4):
            cp = pltpu.make_async_remote_copy(src_ref=g_ref.at[2 * k + 1 - c_], dst_ref=land_ref.at[k],
                                              send_sem=send_sems.at[k], recv_sem=recv_sems.at[k],
                                              device_id=(x_, y_, 1 - c_), device_id_type=MESH)
            cp.start()
            copies.append(cp)
        for cp in copies:
            cp.wait_recv()
        for cp in copies:
            cp.wait_send()

    return pl.pallas_call(
        body, out_shape=jax.ShapeDtypeStruct((4, R, C), gfull.dtype), in_specs=[ANY], out_specs=ANY,
        scratch_shapes=[pltpu.SemaphoreType.DMA((4,)), pltpu.SemaphoreType.DMA((4,))], name='rs_sibling_' + tag)(gfull)


def rs_pair_add(gfull, land, core, tag):
    _, R, C = gfull.shape
    tl = R

    def body(c_ref, g_ref, l_ref, o_ref):
        o_ref[...] = (g_ref[...].astype(F32) + l_ref[...].astype(F32)).astype(BF16)

    return pl.pallas_call(
        body, out_shape=jax.ShapeDtypeStruct((4, R, C), BF16),
        grid_spec=pltpu.PrefetchScalarGridSpec(
            num_scalar_prefetch=1, grid=(4, R // tl),
            in_specs=[pl.BlockSpec((1, tl, C), lambda k, i, c: (2 * k + c[0], i, 0)),
                      pl.BlockSpec((1, tl, C), lambda k, i, c: (k, i, 0))],
            out_specs=pl.BlockSpec((1, tl, C), lambda k, i, c: (k, i, 0))),
        compiler_params=pltpu.CompilerParams(dimension_semantics=("parallel", "parallel")), name='rs_pair_add_' + tag)(
            core, gfull, land)


def rs_chips(part, tag):
    return run_comm(ChipExchange(part), 'rs_chips_' + tag)[0]


def _adam(wv, gv, mv, vv):
    m = ADAM_B1 * mv + (1.0 - ADAM_B1) * gv
    v = ADAM_B2 * vv + (1.0 - ADAM_B2) * (gv * gv)
    m_hat = m / (1.0 - ADAM_B1 ** ADAM_STEP)
    v_hat = v / (1.0 - ADAM_B2 ** ADAM_STEP)
    delta = -ADAM_LR * (m_hat / (jnp.sqrt(v_hat) + ADAM_EPS) + ADAM_WD * wv)
    return delta, m, v


def _sum4(p_ref, l_ref):
    return ((p_ref[0].astype(F32) + l_ref[0].astype(F32)) + l_ref[1].astype(F32)) + l_ref[2].astype(F32)


def rs_rep_sum(part, land, chip):
    def body(c_ref, p_ref, l_ref, o_ref):
        o_ref[...] = _sum4(p_ref, l_ref)

    return pl.pallas_call(
        body, out_shape=jax.ShapeDtypeStruct((REP_SLOT, LANES), F32),
        grid_spec=pltpu.PrefetchScalarGridSpec(
            num_scalar_prefetch=1, grid=(1,),
            in_specs=[pl.BlockSpec((1, REP_SLOT, LANES), lambda i, c: (c[0], 0, 0)),
                      pl.BlockSpec((3, REP_SLOT, LANES), lambda i, c: (0, 0, 0))],
            out_specs=pl.BlockSpec((REP_SLOT, LANES), lambda i, c: (0, 0))),
        compiler_params=pltpu.CompilerParams(dimension_semantics=("parallel",)), name='rs_rep')(chip, part, land)


def adam_param(name, shape, off, w, m, v, chip, part=None, land=None, grep=None):
    r, c = shape
    rp, nt, rb = _tiles(shape)
    rbw = min(r, rb)
    n_src = 2 if grep is None else 1
    ns = w.shape
    assert int(np.prod(ns[:-1])) == r and ns[-1] == c
    if len(ns) == 2:
        nat_block, nat_map = (rbw, c), lambda i, cr: (i, 0)
    elif int(np.prod(ns[:-2])) == 1:
        nat_block, nat_map = (1,) * (len(ns) - 2) + (rbw, c), lambda i, cr: (0,) * (len(ns) - 2) + (i, 0)
    else:
        assert len(ns) == 4 and ns[0] == 1 and rbw % ns[2] == 0
        nat_block, nat_map = (1, rbw // ns[2], ns[2], c), lambda i, cr: (0, i, 0, 0)

    def body(c_ref, *refs):
        srcs = refs[:n_src * nt]
        w_ref, m_ref, v_ref, g_ref, d_ref, nm_ref, nv_ref = refs[n_src * nt:]
        if grep is None:
            tiles = [_sum4(srcs[2 * t], srcs[2 * t + 1]) for t in range(nt)]
        else:
            tiles = [srcs[t][...] for t in range(nt)]
        g = (tiles[0] if nt == 1 else jnp.concatenate(tiles, axis=1))[:rbw, :c]
        g_ref[...] = g.reshape(nat_block)
        res = _adam(w_ref[...].reshape(rbw, c), g, m_ref[...].reshape(rbw, c), v_ref[...].reshape(rbw, c))
        for ref, val in zip((d_ref, nm_ref, nv_ref), res):
            ref[...] = val.reshape(nat_block)

    in_specs, args = [], []
    for t in range(nt):
        b0 = (off + t * rp) // rb
        assert (off + t * rp) % rb == 0
        if grep is None:
            in_specs += [pl.BlockSpec((1, rb, LANES), functools.partial(lambda i, cr, b0: (cr[0], b0 + i, 0), b0=b0)),
                         pl.BlockSpec((3, rb, LANES), functools.partial(lambda i, cr, b0: (0, b0 + i, 0), b0=b0))]
            args += [part, land]
        else:
            in_specs.append(pl.BlockSpec((rb, LANES), functools.partial(lambda i, cr, b0: (b0 + i, 0), b0=b0)))
            args.append(grep)
    nat = pl.BlockSpec(nat_block, nat_map)
    return pl.pallas_call(
        body, out_shape=[jax.ShapeDtypeStruct(ns, F32)] * 4,
        grid_spec=pltpu.PrefetchScalarGridSpec(num_scalar_prefetch=1, grid=(rp // rb,), in_specs=in_specs + [nat] * 3,
                                               out_specs=[nat] * 4),
        compiler_params=pltpu.CompilerParams(dimension_semantics=("parallel",)), name='adam_' + name)(
            chip, *args, w, m, v)


def adam_small(names, grep, P, M, V):
    in_specs, args, out_specs, out_shape, meta = [], [], [], [], []
    for n in names:
        s = REP_SHAPE[n]
        rp, nt, _ = _tiles(s)
        ns = P[n].shape
        for t in range(nt):
            b0 = (REP_OFF[n] + t * rp) // rp
            assert (REP_OFF[n] + t * rp) % rp == 0
            in_specs.append(pl.BlockSpec((rp, LANES), functools.partial(lambda i, b0: (b0, 0), b0=b0)))
            args.append(grep)
        nat = pl.BlockSpec(ns, functools.partial(lambda i, nd: (0,) * nd, nd=len(ns)))
        in_specs += [nat] * 3
        args += [P[n], M[n], V[n]]
        out_specs += [nat] * 4
        out_shape += [jax.ShapeDtypeStruct(ns, F32)] * 4
        meta.append((s, nt, ns))
    n_in = len(in_specs)

    def body(*refs):
        ins, outs = refs[:n_in], refs[n_in:]
        k = 0
        for p, ((r, c), nt, ns) in enumerate(meta):
            tiles = [ins[k + t][...] for t in range(nt)]
            w_ref, m_ref, v_ref = ins[k + nt:k + nt + 3]
            k += nt + 3
            g = (tiles[0] if nt == 1 else jnp.concatenate(tiles, axis=1))[:r, :c]
            res = (g,) + _adam(w_ref[...].reshape(r, c), g, m_ref[...].reshape(r, c), v_ref[...].reshape(r, c))
            for ref, val in zip(outs[4 * p:4 * p + 4], res):
                ref[...] = val.reshape(ns)

    res = pl.pallas_call(body, grid=(1,), in_specs=in_specs, out_specs=out_specs, out_shape=out_shape,
                         compiler_params=pltpu.CompilerParams(dimension_semantics=("arbitrary",)), name='adam_small')(*args)
    return {n: tuple(res[4 * p:4 * p + 4]) for p, n in enumerate(names)}


VM = pl.BlockSpec(memory_space=pltpu.VMEM)


def _tile_value(w, t, rp):
    r, c = w.shape
    wt = min(LANES, c - t * LANES)
    tile = w[:, t * LANES:t * LANES + wt]
    if wt < LANES:
        tile = jnp.concatenate([tile, jnp.zeros((r, LANES - wt), tile.dtype)], axis=1)
    if rp > r:
        tile = jnp.concatenate([tile, jnp.zeros((rp - r, LANES), tile.dtype)], axis=0)
    return tile


def pack_layer(layer, blocks):
    names = LAYER_PARAMS[layer]

    def body(*refs):
        tiles = []
        for ref, n in zip(refs[:-1], names):
            rp, nt, _ = _tiles(_block_shape(n))
            w = ref[...].reshape(_block_shape(n))
            tiles += [_tile_value(w, t, rp) for t in range(nt)]
        refs[-1][...] = jnp.concatenate(tiles, axis=0).astype(BF16)

    return pl.pallas_call(body, out_shape=jax.ShapeDtypeStruct((LAYER_ROWS[layer], LANES), BF16),
                          in_specs=[VM] * len(names), out_specs=VM, name='pack_' + layer)(*[blocks[n] for n in names])


def assemble(name, gathered):
    (rf, cf), ax = SHARDED[name]
    r, c = _block_shape(name)
    rp, nt, _ = _tiles((r, c))
    off = SH_OFF[name]
    out_cols = cf if ax == 0 else len(perm_index(name))

    def body(g_ref, o_ref, buf, sem):
        cp = pltpu.make_async_copy(g_ref.at[:, pl.ds(off, nt * rp), :], buf, sem)
        cp.start()
        cp.wait()
        if ax == 0:
            for j in range(N_DEV):
                o_ref[j * r:(j + 1) * r, :] = jnp.concatenate([buf[j, t * rp:(t + 1) * rp, :] for t in range(nt)], axis=1)
            return
        pieces = []
        for p in PERM[name]:
            if p[0] == 'z':
                pieces.append(jnp.zeros((r, p[1]), BF16))
                continue
            n0, w = p
            while w > 0:
                j, cb = divmod(n0, c)
                t, lane = divmod(cb, LANES)
                wl = min(w, LANES - lane, c - cb)
                pieces.append(buf[j, t * rp:t * rp + r, lane:lane + wl])
                n0, w = n0 + wl, w - wl
        o_ref[...] = jnp.concatenate(pieces, axis=1)

    return pl.pallas_call(
        body, out_shape=jax.ShapeDtypeStruct((rf, out_cols), BF16), in_specs=[ANY], out_specs=VM,
        scratch_shapes=[pltpu.VMEM((N_DEV, nt * rp, LANES), BF16), pltpu.SemaphoreType.DMA(())], name='asm_' + name)(
            gathered)


def chunk_grad(layer, name, dw, gfull):
    (rf, cf), ax = SHARDED[name]
    r, c = _block_shape(name)
    rp, nt, _ = _tiles((r, c))
    off = SH_OFF[name]
    if ax == 1:
        idx = perm_index(name) if name in PERM else np.arange(cf)
        inv = np.full(cf, -1)
        inv[idx[idx >= 0]] = np.nonzero(idx >= 0)[0]

    def body(*refs):
        dw_ref, o_ref, buf, sem = refs[0], refs[-3], refs[-2], refs[-1]
        for j in range(N_DEV):
            for t in range(nt):
                if ax == 0:
                    tile = dw_ref[j * r:(j + 1) * r, t * LANES:(t + 1) * LANES]
                else:
                    cols = inv[j * c + t * LANES:j * c + min((t + 1) * LANES, c)]
                    cuts = [0] + [k for k in range(1, len(cols)) if cols[k] != cols[k - 1] + 1] + [len(cols)]
                    pieces = [dw_ref[:, int(cols[a]):int(cols[b - 1]) + 1] for a, b in zip(cuts[:-1], cuts[1:])]
                    if len(cols) < LANES:
                        pieces.append(jnp.zeros((r, LANES - len(cols)), F32))
                    tile = pieces[0] if len(pieces) == 1 else jnp.concatenate(pieces, axis=1)
                    if rp > r:
                        tile = jnp.concatenate([tile, jnp.zeros((rp - r, LANES), F32)], axis=0)
                buf[j, t * rp:(t + 1) * rp, :] = tile.astype(BF16)
        cp = pltpu.make_async_copy(buf, o_ref.at[:, pl.ds(off, nt * rp), :], sem)
        cp.start()
        cp.wait()

    shape = jax.ShapeDtypeStruct((N_DEV, LAYER_ROWS[layer], LANES), BF16)
    scratch = [pltpu.VMEM((N_DEV, nt * rp, LANES), BF16), pltpu.SemaphoreType.DMA(())]
    if gfull is None:
        return pl.pallas_call(body, out_shape=shape, in_specs=[VM], out_specs=ANY, scratch_shapes=scratch,
                              name='chunk_' + name)(dw)
    return pl.pallas_call(body, out_shape=shape, in_specs=[VM, ANY], out_specs=ANY, scratch_shapes=scratch,
                          input_output_aliases={1: 0}, name='chunk_' + name)(dw, gfull)


def pack_rep(G):
    def body(*refs):
        tiles = []
        for ref, s in zip(refs[:-1], REP_SHAPE.values()):
            rp, nt, _ = _tiles(s)
            g = ref[...]
            tiles += [_tile_value(g, t, rp) for t in range(nt)]
        rows = sum(t.shape[0] for t in tiles)
        if rows < REP_ROWS:
            tiles.append(jnp.zeros((REP_ROWS - rows, LANES), F32))
        full = jnp.concatenate(tiles, axis=0)
        for j in range(N_DEV):
            refs[-1][j] = full[j * REP_CHUNK:(j + 1) * REP_CHUNK]

    return pl.pallas_call(body, out_shape=jax.ShapeDtypeStruct((N_DEV, REP_SLOT, LANES), F32),
                          in_specs=[VM] * len(REP_SHAPE), out_specs=VM, name='pack_rep')(
                              *[G[n].reshape(s) for n, s in REP_SHAPE.items()])


def _pack_small(blocks, order, rows, width, dtype):
    flat = jnp.concatenate([blocks[n].reshape(-1).astype(dtype) for n in order])
    return jnp.pad(flat, (0, rows * width - flat.shape[0])).reshape(rows, width)


def kernel(x, pre_norm, post_norm, rel_bias, a_w_in, a_lam_re, a_lam_im, a_log_dt, a_b_re, a_b_im, a_c_re, a_c_im, a_d, a_w_glu, a_b_glu, a_w_out, b_w_in, b_sinks, b_w_out, c_w_in, c_q_norm, c_kv_norm, c_w_uq, c_w_ukv, c_w_out, d_w_in, d_ln_g, d_ln_b, d_w_s, d_b_s, d_w_out, loss_target, m_pre_norm, m_post_norm, m_rel_bias, m_a_w_in, m_a_lam_re, m_a_lam_im, m_a_log_dt, m_a_b_re, m_a_b_im, m_a_c_re, m_a_c_im, m_a_d, m_a_w_glu, m_a_b_glu, m_a_w_out, m_b_w_in, m_b_sinks, m_b_w_out, m_c_w_in, m_c_q_norm, m_c_kv_norm, m_c_w_uq, m_c_w_ukv, m_c_w_out, m_d_w_in, m_d_ln_g, m_d_ln_b, m_d_w_s, m_d_b_s, m_d_w_out, v_pre_norm, v_post_norm, v_rel_bias, v_a_w_in, v_a_lam_re, v_a_lam_im, v_a_log_dt, v_a_b_re, v_a_b_im, v_a_c_re, v_a_c_im, v_a_d, v_a_w_glu, v_a_b_glu, v_a_w_out, v_b_w_in, v_b_sinks, v_b_w_out, v_c_w_in, v_c_q_norm, v_c_kv_norm, v_c_w_uq, v_c_w_ukv, v_c_w_out, v_d_w_in, v_d_ln_g, v_d_ln_b, v_d_w_s, v_d_b_s, v_d_w_out):
    loc = locals()
    P = {n: loc[n] for n in WEIGHTS}
    M = {n: loc['m_' + n] for n in WEIGHTS}
    V = {n: loc['v_' + n] for n in WEIGHTS}
    xs = x[0]
    L = xs.shape[0]

    blocks = {n: P[n].reshape(_block_shape(n)) for n in SHARDED}
    packed = {layer: pack_layer(layer, P) for layer in LAYER_PARAMS}
    W = {}

    def assemble_layer(layer, gathered):
        for n in LAYER_PARAMS[layer]:
            if n not in SHARDED_F32:
                W[n] = assemble(n, gathered)

    assemble_layer('a', all_gather(packed['a'], 'ag_a'))
    small = all_gather(_pack_small(blocks, SHARDED_F32, SMALL_ROWS, 128, F32), 'ag_small')
    Pl = dict(P)
    for n in SHARDED_F32:
        c = SHARDED[n][0][1]
        bc = c // N_DEV
        Pl[n] = small.reshape(N_DEV, -1)[:, SMALL_OFF[n]:SMALL_OFF[n] + bc].reshape(1, c)
    cx, cy, cc = _coords()
    core = jnp.reshape(cc, (1,)).astype(jnp.int32)
    chip = jnp.reshape(2 * cx + cy, (1,)).astype(jnp.int32)

    def pair_sums(gfull, tag):
        return rs_pair_add(gfull, rs_sibling(gfull, tag), core, tag)

    fwd = [layer_a_fwd, layer_b_fwd, layer_c_fwd, layer_d_fwd]
    bwd = [layer_a_bwd, layer_b_bwd, layer_c_bwd, layer_d_bwd]
    saved = []
    xc = xs

    def fpre(x_, g_):
        return [rms_fwd(x_, g_)], []
    (h,), _ = rowwise(fpre, [rw(xc)], [P['pre_norm'][0:1]], [(D_MODEL, BF16)], [], 256, 'pre_norm0')
    for i in range(4):
        if i == 0:
            yb, sv = fwd[i](h, W, Pl, comm=Both(AllGather(packed['b']), AllGather(packed['c'])))
            assemble_layer('b', sv['carried'][0])
            assemble_layer('c', sv['carried'][1])
        elif i == 1:
            yb, sv = fwd[i](h, W, Pl, comm=AllGather(packed['d']))
            assemble_layer('d', sv['carried'][0])
        else:
            yb, sv = fwd[i](h, W, Pl)

        sv['x'], sv['yb'] = xc, yb
        saved.append(sv)
        if i < 3:

            def fpost(x_, y_, gpost, gpre):
                xn_ = x_ + rms_fwd(y_, gpost)
                return [xn_, rms_fwd(xn_, gpre)], []
            (xc, h), _ = rowwise(fpost, [rw(xc), rw(yb)], [P['post_norm'][i:i + 1], P['pre_norm'][i + 1:i + 2]],
                                 [(D_MODEL, F32), (D_MODEL, BF16)], [], 256, f'post_pre_norm{i}')
        else:

            def floss(x_, y_, t_, gpost):
                d = x_ + rms_fwd(y_, gpost) - t_
                return [d * (1.0 / D_MODEL)], [0.5 * jnp.sum(jnp.sum(d * d, axis=-1, keepdims=True) * (1.0 / D_MODEL),
                                                             axis=0, keepdims=True)]
            (dx,), (loss_loc,) = rowwise(floss, [rw(xc), rw(yb), rw(loss_target[0])], [P['post_norm'][i:i + 1]],
                                         [(D_MODEL, F32)], [(1, 1)], 256, 'post_norm_loss')
    loss = lax.psum(loss_loc[0, 0], ("x", "y", "c"))

    G, out = {}, {}
    dpre, dpost = [None] * 4, [None] * 4

    def adam_layer(layer, part, land2):
        for n in LAYER_PARAMS[layer]:
            s = _block_shape(n)
            out[n] = adam_param(n, s, SH_OFF[n], P[n], M[n], V[n], chip, part=part, land=land2)

    def fpost_b(y_, d_, g_):
        dy, dg = rms_bwd(y_, g_, d_)
        return [dy], [dg]
    (dyb,), (dpost[3],) = rowwise(fpost_b, [rw(saved[3]['yb']), rw(dx)], [P['post_norm'][3:4]], [(D_MODEL, BF16)],
                                  [(1, D_MODEL)], 256, 'post_norm_bwd3')
    pending = None
    for i in reversed(range(4)):
        sv = saved[i]
        if pending is None:
            dh, g = bwd[i](dyb, W, Pl, sv)
        else:
            dh, g = bwd[i](dyb, W, Pl, sv, comm=ChipExchange(pending[1]))
            adam_layer(pending[0], pending[1], g['carried'][0])
        g.pop('carried', None)
        G.update(g)

        if i > 0:

            def fpre_b(x_, dh_, d_, y_, gpre, gpost):
                dxl, dg = rms_bwd(x_, gpre, dh_)
                dy, dgp = rms_bwd(y_, gpost, d_ + dxl)
                return [d_ + dxl, dy], [dg, dgp]
            (dx, dyb), (dpre[i], dpost[i - 1]) = rowwise(
                fpre_b, [rw(sv['x']), rw(dh), rw(dx), rw(saved[i - 1]['yb'])],
                [P['pre_norm'][i:i + 1], P['post_norm'][i - 1:i]], [(D_MODEL, F32), (D_MODEL, BF16)],
                [(1, D_MODEL), (1, D_MODEL)], 256, f'pre_post_norm_bwd{i}')
        else:

            def fpre_b0(x_, dh_, d_, g_):
                dxl, dg = rms_bwd(x_, g_, dh_)
                return [d_ + dxl], [dg]
            (dx,), (dpre[i],) = rowwise(fpre_b0, [rw(sv['x']), rw(dh), rw(dx)], [P['pre_norm'][i:i + 1]],
                                        [(D_MODEL, F32)], [(1, D_MODEL)], 256, 'pre_norm_bwd0')

        layer = 'abcd'[i]
        gfull = None
        for n in LAYER_PARAMS[layer]:
            gfull = chunk_grad(layer, n, G[n], gfull)
        pending = (layer, pair_sums(gfull, layer))
    adam_layer(pending[0], pending[1], rs_chips(pending[1], pending[0]))
    G['pre_norm'] = jnp.concatenate(dpre, axis=0)
    G['post_norm'] = jnp.concatenate(dpost, axis=0)

    part = pair_sums(pack_rep(G), 'rep')
    land2 = rs_chips(part, 'rep')
    grep = all_gather(rs_rep_sum(part, land2, chip), 'ag_rep')[:, :REP_CHUNK].reshape(REP_ROWS, LANES)
    small_names = [n for n, s in REP_SHAPE.items() if s[0] <= 64]
    out.update(adam_small(small_names, grep, P, M, V))
    for n, s in REP_SHAPE.items():
        if n not in small_names:
            out[n] = adam_param(n, s, REP_OFF[n], P[n], M[n], V[n], chip, grep=grep)
    res = [loss, dx[None]]
    for kind in range(4):
        res += [out[n][kind].reshape(P[n].shape) for n in WEIGHTS]
    return tuple(res)
```

```python
import functools
import math

import numpy as np
import jax
import jax.numpy as jnp
from jax import lax
from jax.experimental import pallas as pl
from jax.experimental.pallas import tpu as pltpu

F32 = jnp.float32
BF16 = jnp.bfloat16
MESH = pl.DeviceIdType.MESH
ANY = pl.BlockSpec(memory_space=pl.ANY)

N_DEV = 8
D_MODEL = 1024
EPS = 1e-6
NEG_INF = -1e30
SSM_G, SSM_P, SSM_H = 64, 64, 16
SSM_T = 256
SSM_WC = 512
HEAD_DIM = 64
SWA_HEADS, SWA_KV = 16, 2
WINDOW = 128
REL_BUCKETS, REL_MAX_DIST = 32, 128
MLA_HEADS, MLA_NOPE, MLA_ROPE, MLA_V = 16, 64, 32, 64
MLA_Q_RANK, MLA_KV_RANK = 768, 256
ROPE_BASE = 10000.0
SGU_G, SGU_C, SGU_T = 16, 64, 128
ADAM_LR, ADAM_B1, ADAM_B2, ADAM_EPS, ADAM_WD, ADAM_STEP = 0.001, 0.9, 0.999, 1e-08, 0.01, 10

WEIGHTS = ['pre_norm', 'post_norm', 'rel_bias', 'a_w_in', 'a_lam_re', 'a_lam_im', 'a_log_dt', 'a_b_re', 'a_b_im',
           'a_c_re', 'a_c_im', 'a_d', 'a_w_glu', 'a_b_glu', 'a_w_out', 'b_w_in', 'b_sinks', 'b_w_out', 'c_w_in',
           'c_q_norm', 'c_kv_norm', 'c_w_uq', 'c_w_ukv', 'c_w_out', 'd_w_in', 'd_ln_g', 'd_ln_b', 'd_w_s', 'd_b_s',
           'd_w_out']
SHARDED = {'a_w_in': ((1024, 2048), 1), 'a_w_glu': ((1024, 1024), 0), 'a_w_out': ((1024, 1024), 0),
           'b_w_in': ((1024, 2304), 1), 'b_w_out': ((1024, 1024), 0), 'c_w_in': ((1024, 2080), 1),
           'c_q_norm': ((1, 768), 1), 'c_kv_norm': ((1, 256), 1), 'c_w_uq': ((768, 1536), 1),
           'c_w_ukv': ((256, 2048), 1), 'c_w_out': ((1024, 1024), 0), 'd_w_in': ((1024, 3072), 1),
           'd_ln_g': ((1, 1024), 1), 'd_ln_b': ((1, 1024), 1), 'd_w_out': ((1024, 1024), 0)}
SHARDED_F32 = ['c_q_norm', 'c_kv_norm', 'd_ln_g', 'd_ln_b']
REPLICATED = [n for n in WEIGHTS if n not in SHARDED]


def _cdiv(a, b):
    return -(-a // b)


def _block_shape(name):
    (r, c), ax = SHARDED[name]
    return (r // N_DEV, c) if ax == 0 else (r, c // N_DEV)


LANES = 128
LAYER_PARAMS = {'a': ['a_w_in', 'a_w_glu', 'a_w_out'], 'b': ['b_w_in', 'b_w_out'],
                'c': ['c_w_in', 'c_w_uq', 'c_w_ukv', 'c_w_out', 'c_q_norm', 'c_kv_norm'],
                'd': ['d_w_in', 'd_w_out', 'd_ln_g', 'd_ln_b']}


def _tiles(shape):
    r, c = shape
    rp = max(r, 16)
    rb = 512 if rp % 512 == 0 else 256 if rp % 256 == 0 else rp
    return rp, _cdiv(c, LANES), rb


SH_OFF, LAYER_ROWS = {}, {}
for _l, _names in LAYER_PARAMS.items():
    _o = 0
    for _n in _names:
        _rp, _nt, _rb = _tiles(_block_shape(_n))
        assert _o % _rb == 0
        SH_OFF[_n] = _o
        _o += _rp * _nt
    assert _o % 16 == 0
    LAYER_ROWS[_l] = _o

REP_SHAPE = {'a_b_re': (4096, 16), 'a_b_im': (4096, 16), 'd_w_s': (2048, 128), 'a_c_re': (1024, 64),
             'a_c_im': (1024, 64), 'pre_norm': (4, 1024), 'post_norm': (4, 1024), 'a_lam_re': (64, 64),
             'a_lam_im': (64, 64), 'a_d': (1, 1024), 'a_b_glu': (1, 1024), 'rel_bias': (32, 16), 'd_b_s': (16, 128),
             'a_log_dt': (1, 64), 'b_sinks': (1, 16)}
REP_OFF = {}
_o = 0
for _n, _s in REP_SHAPE.items():
    _rp, _nt, _rb = _tiles(_s)
    assert _o % _rb == 0
    REP_OFF[_n] = _o
    _o += _rp * _nt
REP_ROWS = _cdiv(_o, 16 * N_DEV) * 16 * N_DEV
REP_CHUNK = REP_ROWS // N_DEV
REP_SLOT = REP_CHUNK

PERM = {'a_w_in': [(0, 2048)], 'd_w_in': [(0, 3072)], 'b_w_in': [(1280, 1024), (0, 1280)],
        'c_w_in': [(1056, 1024), (0, 1056), ('z', 96)],
        'c_w_uq': sum([[(2 * hp * 96, 64), ((2 * hp + 1) * 96, 64), (2 * hp * 96 + 64, 32), ((2 * hp + 1) * 96 + 64, 32),
                        ('z', 64)] for hp in range(8)], []),
        'c_w_ukv': sum([[(2 * hp * 128, 64), ((2 * hp + 1) * 128, 64), (2 * hp * 128 + 64, 64),
                         ((2 * hp + 1) * 128 + 64, 64)] for hp in range(8)], [])}


def perm_index(name):
    return np.concatenate([np.full(p[1], -1) if p[0] == 'z' else np.arange(p[0], p[0] + p[1]) for p in PERM[name]])


SMALL_OFF = {}
_o = 0
for _n in SHARDED_F32:
    SMALL_OFF[_n] = _o
    _o += int(np.prod(_block_shape(_n)))
SMALL_ROWS = _cdiv(_o, 128 * 8) * 8


def _pick(n, cands):
    for c in cands:
        if n % c == 0:
            return c
    return n


def mm(a, b, mode, name, out_dtype=F32):
    if mode == 'nn':
        (M, K), (K2, N) = a.shape, b.shape
    elif mode == 'nt':
        (M, K), (N, K2) = a.shape, b.shape
    else:
        (K, M), (K2, N) = a.shape, b.shape
    assert K == K2, (name, a.shape, b.shape)
    tm = _pick(M, (1024, 768, 512, 256, 128))
    tn = _pick(N, (512, 384, 256))
    dims = {'nn': ((1,), (0,)), 'nt': ((1,), (1,)), 'tn': ((0,), (0,))}[mode]

    def body(a_ref, b_ref, o_ref):
        o_ref[...] = lax.dot_general(a_ref[...].astype(BF16), b_ref[...].astype(BF16), (dims, ((), ())),
                                     preferred_element_type=F32).astype(out_dtype)

    a_spec = pl.BlockSpec((K, tm), lambda i, j: (0, i)) if mode == 'tn' else pl.BlockSpec((tm, K), lambda i, j: (i, 0))
    b_spec = pl.BlockSpec((tn, K), lambda i, j: (j, 0)) if mode == 'nt' else pl.BlockSpec((K, tn), lambda i, j: (0, j))
    return pl.pallas_call(
        body, grid=(M // tm, N // tn), in_specs=[a_spec, b_spec],
        out_specs=pl.BlockSpec((tm, tn), lambda i, j: (i, j)), out_shape=jax.ShapeDtypeStruct((M, N), out_dtype),
        compiler_params=pltpu.CompilerParams(dimension_semantics=("parallel", "parallel")), name=name)(a, b)


def rw(arr, width=None, cb=0):
    return (arr, arr.shape[1] if width is None else width, cb)


def rowwise(fn, rows, consts, outs, accs, tl, name, n_steps=None):
    if n_steps is None:
        n_steps = [r[0].shape[0] for r in rows if not isinstance(r[1], pl.BlockSpec)][0] // tl
    L = n_steps * tl
    nr, nc, no, na = len(rows), len(consts), len(outs), len(accs)
    in_specs, args = [], []
    for r in rows:
        if isinstance(r[1], pl.BlockSpec):
            in_specs.append(r[1])
        else:
            in_specs.append(pl.BlockSpec((tl, r[1]), functools.partial(lambda i, cb: (i, cb), cb=r[2])))
        args.append(r[0])
    for c in consts:
        in_specs.append(pl.BlockSpec(c.shape, functools.partial(lambda i, nd: (0,) * nd, nd=c.ndim)))
        args.append(c)
    out_specs = [pl.BlockSpec((tl, w), lambda i: (i, 0)) for w, _ in outs]
    out_shape = [jax.ShapeDtypeStruct((L, w), dt) for w, dt in outs]
    for s in accs:
        out_specs.append(pl.BlockSpec(s, functools.partial(lambda i, nd: (0,) * nd, nd=len(s))))
        out_shape.append(jax.ShapeDtypeStruct(s, F32))

    def body(*refs):
        ins = [r[...] for r in refs[:nr + nc]]
        o_refs = refs[nr + nc:nr + nc + no]
        a_refs = refs[nr + nc + no:]
        o_vals, a_vals = fn(*ins)
        for ref, val in zip(o_refs, o_vals):
            ref[...] = val.astype(ref.dtype)
        if na:
            @pl.when(pl.program_id(0) == 0)
            def _():
                for ref in a_refs:
                    ref[...] = jnp.zeros_like(ref)
            for ref, val in zip(a_refs, a_vals):
                ref[...] += val

    res = pl.pallas_call(
        body, grid=(n_steps,), in_specs=in_specs, out_specs=out_specs, out_shape=out_shape,
        compiler_params=pltpu.CompilerParams(dimension_semantics=("arbitrary",)), name=name)(*args)
    return res[:no], res[no:]


def carried(body, comm, *, grid, in_specs, out_specs, out_shape, name, semantics, scratch_shapes=()):
    single = not isinstance(out_shape, (list, tuple))
    o_specs = [out_specs] if single else list(out_specs)
    o_shape = [out_shape] if single else list(out_shape)
    if comm is None:
        call = pl.pallas_call(body, grid=grid, in_specs=in_specs, out_specs=out_specs, out_shape=out_shape,
                              scratch_shapes=list(scratch_shapes),
                              compiler_params=pltpu.CompilerParams(dimension_semantics=semantics), name=name)
        return lambda *args: (call(*args), None)
    n_in, n_out, n_sc = len(in_specs), len(o_specs), len(scratch_shapes)
    ci, co = len(comm.ins), len(comm.outs)
    n_steps = int(np.prod(grid))
    hooks = comm.hooks(n_steps)

    def wrapped(*refs):
        ins, cins = refs[:n_in], refs[n_in:n_in + ci]
        outs, couts = refs[n_in + ci:n_in + ci + n_out], refs[n_in + ci + n_out:n_in + ci + n_out + co]
        sc, csc = refs[n_in + ci + n_out + co:n_in + ci + n_out + co + n_sc], refs[n_in + ci + n_out + co + n_sc:]
        step = pl.program_id(0)
        for ax in range(1, len(grid)):
            step = step * grid[ax] + pl.program_id(ax)
        for at, fn, after in hooks:
            if not after:
                pl.when(step == at)(functools.partial(fn, cins, couts, csc))
        body(*ins, *outs, *sc)
        for at, fn, after in hooks:
            if after:
                pl.when(step == at)(functools.partial(fn, cins, couts, csc))

    call = pl.pallas_call(wrapped, grid=grid, in_specs=list(in_specs) + [ANY] * ci, out_specs=o_specs + [ANY] * co,
                          out_shape=o_shape + list(comm.outs), scratch_shapes=list(scratch_shapes) + list(comm.scratch),
                          compiler_params=pltpu.CompilerParams(dimension_semantics=("arbitrary",) * len(grid)), name=name)

    def run(*args):
        res = call(*args, *comm.ins)
        return (res[0] if single else res[:n_out]), res[n_out:]
    return run


_K0 = math.sqrt(2.0 / math.pi)
_K1 = 0.044715


def gelu(x):
    return x * (0.5 * (1.0 + jnp.tanh(_K0 * (x + _K1 * (x * x * x)))))


def gelu_grad(x):
    t = jnp.tanh(_K0 * (x + _K1 * (x * x * x)))
    return 0.5 * (1.0 + t) + 0.5 * x * (1.0 - t * t) * (_K0 * (1.0 + 3.0 * _K1 * x * x))


def sigmoid(x):
    return 1.0 / (1.0 + jnp.exp(-x))


def silu(z):
    return z * sigmoid(z)


def silu_grad(z):
    s = sigmoid(z)
    return s * (1.0 + z * (1.0 - s))


def rms_fwd(x, g):
    r = lax.rsqrt(jnp.mean(x * x, axis=-1, keepdims=True) + EPS)
    return x * r * g


def rms_bwd(x, g, dy):
    r = lax.rsqrt(jnp.mean(x * x, axis=-1, keepdims=True) + EPS)
    xh = x * r
    dg = jnp.sum(dy * xh, axis=0, keepdims=True)
    dxh = dy * g
    dx = r * (dxh - xh * jnp.mean(dxh * xh, axis=-1, keepdims=True))
    return dx, dg


def _scan_chunk(a_r, a_i, pr_ref, pi_ref, cr, ci, T, reverse):
    row = lax.broadcasted_iota(jnp.int32, a_r.shape, 0)
    sgn = -1.0 if reverse else 1.0
    d = 1
    while d < T:
        k = (T - d) if reverse else (d - 1)
        wr = pr_ref[k:k + 1, :]
        wi = sgn * pi_ref[k:k + 1, :]
        if reverse:
            yr, yi, keep = pltpu.roll(a_r, T - d, 0), pltpu.roll(a_i, T - d, 0), row < T - d
        else:
            yr, yi, keep = pltpu.roll(a_r, d, 0), pltpu.roll(a_i, d, 0), row >= d
        a_r, a_i = (a_r + jnp.where(keep, wr * yr - wi * yi, 0.0), a_i + jnp.where(keep, wr * yi + wi * yr, 0.0))
        d *= 2
    wr = pr_ref[...]
    wi = sgn * pi_ref[...]
    c_r, c_i = cr[...], ci[...]
    a_r, a_i = a_r + (wr * c_r - wi * c_i), a_i + (wr * c_i + wi * c_r)
    k = 0 if reverse else T - 1
    cr[...] = a_r[k:k + 1, :]
    ci[...] = a_i[k:k + 1, :]
    return a_r, a_i


_NT = (((1,), (1,)), ((), ()))
_TN = (((0,), (0,)), ((), ()))


def s5_fwd(proj, d_skip, Bre, Bim, Cre, Cim, pr, pi, comm=None):
    L = proj.shape[0]
    T, WC = min(SSM_T, L), SSM_WC
    nT = L // T

    def body(u_ref, d_ref, bre_ref, bim_ref, cre_ref, cim_ref, pr_ref, pi_ref, y_ref, yg_ref, sr_ref, si_ref, cr, ci):
        @pl.when(pl.program_id(1) == 0)
        def _():
            cr[...] = jnp.zeros_like(cr)
            ci[...] = jnp.zeros_like(ci)

        u = u_ref[...]
        ub = u.astype(BF16)
        a_r = jnp.dot(ub, bre_ref[0].astype(BF16), preferred_element_type=F32)
        a_i = jnp.dot(ub, bim_ref[0].astype(BF16), preferred_element_type=F32)
        a_r, a_i = _scan_chunk(a_r, a_i, pr_ref, pi_ref, cr, ci, T, False)
        sr_ref[...] = a_r
        si_ref[...] = a_i
        y = (jnp.dot(a_r.astype(BF16), cre_ref[0].astype(BF16), preferred_element_type=F32)
             + jnp.dot(a_i.astype(BF16), cim_ref[0].astype(BF16), preferred_element_type=F32) + d_ref[...] * u)
        y_ref[...] = y
        yg_ref[...] = gelu(y)

    uspec = pl.BlockSpec((T, 128), lambda k, i: (i, k))
    sspec = pl.BlockSpec((T, WC), lambda k, i: (i, k))
    return carried(
        body, comm, grid=(8, nT),
        in_specs=[uspec, pl.BlockSpec((1, 128), lambda k, i: (0, k)),
                  pl.BlockSpec((1, 128, WC), lambda k, i: (k, 0, 0)), pl.BlockSpec((1, 128, WC), lambda k, i: (k, 0, 0)),
                  pl.BlockSpec((1, WC, 128), lambda k, i: (k, 0, 0)), pl.BlockSpec((1, WC, 128), lambda k, i: (k, 0, 0)),
                  pl.BlockSpec((T, WC), lambda k, i: (0, k)), pl.BlockSpec((T, WC), lambda k, i: (0, k))],
        out_specs=[uspec, uspec, sspec, sspec],
        out_shape=[jax.ShapeDtypeStruct((L, 1024), F32)] * 2 + [jax.ShapeDtypeStruct((L, 8 * WC), F32)] * 2,
        scratch_shapes=[pltpu.VMEM((1, WC), F32), pltpu.VMEM((1, WC), F32)],
        semantics=("parallel", "arbitrary"), name='a_ssm')(proj, d_skip, Bre, Bim, Cre, Cim, pr, pi)


def s5_bwd(proj, dyg1, dyg2, y, d_skip, s_re, s_im, Bre, Bim, Cre, Cim, prr, pir, comm=None):
    L = proj.shape[0]
    T, WC = min(SSM_T, L), SSM_WC
    nT = L // T

    def body(u_ref, g1_ref, g2_ref, y_ref, d_ref, sr_ref, si_ref, spr_ref, spi_ref, bre_ref, bim_ref, cre_ref, cim_ref,
             pr_ref, pi_ref, du_ref, dd_ref, dbre_ref, dbim_ref, dcre_ref, dcim_ref, dar_ref, dai_ref, cr, ci):
        i = pl.program_id(1)

        @pl.when(i == 0)
        def _():
            for ref in (cr, ci, dd_ref, dbre_ref, dbim_ref, dcre_ref, dcim_ref, dar_ref, dai_ref):
                ref[...] = jnp.zeros_like(ref)

        u = u_ref[...]
        dy = (g1_ref[...] + g2_ref[...]) * gelu_grad(y_ref[...])
        dd_ref[...] += jnp.sum(dy * u, axis=0, keepdims=True)
        dyb, ub = dy.astype(BF16), u.astype(BF16)
        bre, bim, cre, cim = (r[0].astype(BF16) for r in (bre_ref, bim_ref, cre_ref, cim_ref))
        g_r = lax.dot_general(dyb, cre, _NT, preferred_element_type=F32)
        g_i = lax.dot_general(dyb, cim, _NT, preferred_element_type=F32)
        g_r, g_i = _scan_chunk(g_r, g_i, pr_ref, pi_ref, cr, ci, T, True)
        s_r, s_i = sr_ref[...], si_ref[...]
        row = lax.broadcasted_iota(jnp.int32, (T, WC), 0)
        first = (nT - 1 - i) == 0
        sp_r = jnp.where(row == 0, jnp.where(first, 0.0, spr_ref[7:8, :]), pltpu.roll(s_r, 1, 0))
        sp_i = jnp.where(row == 0, jnp.where(first, 0.0, spi_ref[7:8, :]), pltpu.roll(s_i, 1, 0))
        dar_ref[...] += jnp.sum(g_r * sp_r + g_i * sp_i, axis=0, keepdims=True)
        dai_ref[...] += jnp.sum(g_i * sp_r - g_r * sp_i, axis=0, keepdims=True)
        grb, gib = g_r.astype(BF16), g_i.astype(BF16)
        dcre_ref[0] += lax.dot_general(s_r.astype(BF16), dyb, _TN, preferred_element_type=F32)
        dcim_ref[0] += lax.dot_general(s_i.astype(BF16), dyb, _TN, preferred_element_type=F32)
        dbre_ref[0] += lax.dot_general(ub, grb, _TN, preferred_element_type=F32)
        dbim_ref[0] += lax.dot_general(ub, gib, _TN, preferred_element_type=F32)
        du_ref[...] = (dy * d_ref[...] + lax.dot_general(grb, bre, _NT, preferred_element_type=F32)
                       + lax.dot_general(gib, bim, _NT, preferred_element_type=F32))

    uspec = pl.BlockSpec((T, 128), lambda k, i: (nT - 1 - i, k))
    sspec = pl.BlockSpec((T, WC), lambda k, i: (nT - 1 - i, k))
    pspec = pl.BlockSpec((8, WC), lambda k, i: (jnp.maximum((nT - 1 - i) * (T // 8) - 1, 0), k))
    tab = pl.BlockSpec((T, WC), lambda k, i: (0, k))
    bspec = pl.BlockSpec((1, 128, WC), lambda k, i: (k, 0, 0))
    cspec = pl.BlockSpec((1, WC, 128), lambda k, i: (k, 0, 0))
    return carried(
        body, comm, grid=(8, nT),
        in_specs=[uspec, uspec, uspec, uspec, pl.BlockSpec((1, 128), lambda k, i: (0, k)), sspec, sspec, pspec, pspec,
                  bspec, bspec, cspec, cspec, tab, tab],
        out_specs=[uspec, pl.BlockSpec((1, 128), lambda k, i: (0, k)), bspec, bspec, cspec, cspec,
                   pl.BlockSpec((1, WC), lambda k, i: (0, k)), pl.BlockSpec((1, WC), lambda k, i: (0, k))],
        out_shape=[jax.ShapeDtypeStruct((L, 1024), F32), jax.ShapeDtypeStruct((1, 1024), F32),
                   jax.ShapeDtypeStruct((8, 128, WC), F32), jax.ShapeDtypeStruct((8, 128, WC), F32),
                   jax.ShapeDtypeStruct((8, WC, 128), F32), jax.ShapeDtypeStruct((8, WC, 128), F32),
                   jax.ShapeDtypeStruct((1, 8 * WC), F32), jax.ShapeDtypeStruct((1, 8 * WC), F32)],
        scratch_shapes=[pltpu.VMEM((1, WC), F32), pltpu.VMEM((1, WC), F32)],
        semantics=("parallel", "arbitrary"), name='a_ssm_bwd')(
            proj, dyg1, dyg2, y, d_skip, s_re, s_im, s_re, s_im, Bre, Bim, Cre, Cim, prr, pir)


def s5_discretize(lam_re, lam_im, log_dt, b_re, b_im):
    dt = jnp.exp(log_dt)[:, None]
    mag = jnp.exp(lam_re * dt)
    ab_re = mag * jnp.cos(lam_im * dt)
    ab_im = mag * jnp.sin(lam_im * dt)
    den = lam_re * lam_re + lam_im * lam_im
    nr = ab_re - 1.0
    f_re = (nr * lam_re + ab_im * lam_im) / den
    f_im = (ab_im * lam_re - nr * lam_im) / den
    bb_re = f_re[..., None] * b_re - f_im[..., None] * b_im
    bb_im = f_re[..., None] * b_im + f_im[..., None] * b_re
    return ab_re, ab_im, bb_re, bb_im


_EYE8 = np.eye(8, dtype=np.float32)


def _b_tiles(bb):
    t = bb.transpose(0, 2, 1).reshape(8, 8, SSM_H, SSM_P)
    return jnp.einsum('kghp,gG->kghGp', t, _EYE8).reshape(8, 8 * SSM_H, 8 * SSM_P)


def _b_untile(d):
    t = jnp.einsum('kghGp,gG->kghp', d.reshape(8, 8, SSM_H, 8, SSM_P), _EYE8)
    return t.reshape(SSM_G, SSM_H, SSM_P).transpose(0, 2, 1)


def _c_tiles(c):
    t = c.transpose(0, 2, 1).reshape(8, 8, SSM_P, SSM_H)
    return jnp.einsum('kgph,gG->kgpGh', t, _EYE8).reshape(8, 8 * SSM_P, 8 * SSM_H)


def _c_untile(d):
    t = jnp.einsum('kgpGh,gG->kgph', d.reshape(8, 8, SSM_P, 8, SSM_H), _EYE8)
    return t.reshape(SSM_G, SSM_P, SSM_H).transpose(0, 2, 1)


def s5_powers(ar, ai, T):
    W = ar.shape[1]

    def body(ar_ref, ai_ref, fr_ref, fi_ref, rr_ref, ri_ref):
        fr_ref[0:1, :] = ar_ref[...]
        fi_ref[0:1, :] = ai_ref[...]
        rr_ref[T - 1:T, :] = ar_ref[...]
        ri_ref[T - 1:T, :] = ai_ref[...]
        n = 1
        while n < T:
            cr, ci = fr_ref[0:n, :], fi_ref[0:n, :]
            lr, li = fr_ref[n - 1:n, :], fi_ref[n - 1:n, :]
            fr_ref[n:2 * n, :] = cr * lr - ci * li
            fi_ref[n:2 * n, :] = cr * li + ci * lr
            cr, ci = rr_ref[T - n:T, :], ri_ref[T - n:T, :]
            rr_ref[T - 2 * n:T - n, :] = cr * lr - ci * li
            ri_ref[T - 2 * n:T - n, :] = cr * li + ci * lr
            n *= 2

    spec = pl.BlockSpec((T, SSM_WC), lambda j: (0, j))
    aspec = pl.BlockSpec((1, SSM_WC), lambda j: (0, j))
    return pl.pallas_call(
        body, grid=(W // SSM_WC,), in_specs=[aspec, aspec], out_specs=[spec] * 4,
        out_shape=[jax.ShapeDtypeStruct((T, W), F32)] * 4,
        compiler_params=pltpu.CompilerParams(dimension_semantics=("parallel",)), name='a_powers')(ar, ai)


def layer_a_fwd(h, w, p, comm=None):
    L = h.shape[0]
    proj = mm(h, w['a_w_in'], 'nn', 'a_proj')
    disc = lambda *a: s5_discretize(*a)
    (ab_re, ab_im, bb_re, bb_im), disc_vjp = jax.vjp(disc, p['a_lam_re'][0], p['a_lam_im'][0], p['a_log_dt'][0],
                                                     p['a_b_re'][0], p['a_b_im'][0])
    Bre, Bim = _b_tiles(bb_re), _b_tiles(bb_im)
    Cre, Cim = _c_tiles(p['a_c_re'][0]), -_c_tiles(p['a_c_im'][0])
    T = min(SSM_T, L)
    pr, pi, prr, pir = s5_powers(ab_re.reshape(1, -1), ab_im.reshape(1, -1), T)
    (y, yg, s_re, s_im), carried_out = s5_fwd(proj, p['a_d'], Bre, Bim, Cre, Cim, pr, pi, comm=comm)
    gl = mm(yg, w['a_w_glu'], 'nn', 'a_glu')

    def f2(yg_, gl_, z, bg):
        return [yg_ * sigmoid(gl_ + bg) * silu(z)], []
    (po,), _ = rowwise(f2, [rw(yg), rw(gl), rw(proj, 1024, 1)], [p['a_b_glu']], [(1024, BF16)], [], 256, 'a_gate')
    yb = mm(po, w['a_w_out'], 'nn', 'a_out')
    saved = dict(carried=carried_out, h=h, proj=proj, disc_vjp=disc_vjp, Bre=Bre, Bim=Bim, Cre=Cre, Cim=Cim, prr=prr, pir=pir, s_re=s_re,
                 s_im=s_im, y=y, yg=yg, gl=gl, po=po)
    return yb, saved


def _dw(g, sink, name, a, b, mm_name):
    if sink is None:
        g[name] = mm(a, b, 'tn', mm_name)
    else:
        sink.put(name, a, b, mm_name)


def layer_a_bwd(dyb, w, p, sv, comm=None, sink=None):
    g = {}
    dpo = mm(dyb, w['a_w_out'], 'nt', 'a_dpo')
    _dw(g, sink, 'a_w_out', sv['po'], dyb, 'a_dwout')
    proj = sv['proj']

    def f1(dpo_, yg, gl, z, bg):
        sg = sigmoid(gl + bg)
        sz = silu(z)
        dm = dpo_ * sz
        dz = dpo_ * (yg * sg) * silu_grad(z)
        dgl = dm * yg * sg * (1.0 - sg)
        return [dz, dm * sg, dgl], [jnp.sum(dgl, axis=0, keepdims=True)]
    (dz, dyg1, dgl), (db_glu,) = rowwise(f1, [rw(dpo), rw(sv['yg']), rw(sv['gl']), rw(proj, 1024, 1)], [p['a_b_glu']],
                                          [(1024, F32), (1024, F32), (1024, BF16)], [(1, 1024)], 256, 'a_gate_bwd')
    g['a_b_glu'] = db_glu
    _dw(g, sink, 'a_w_glu', sv['yg'], dgl, 'a_dwglu')
    dyg2 = mm(dgl, w['a_w_glu'], 'nt', 'a_dyg2')

    (du, dd, dBre, dBim, dCre, dCim, da_re, da_im), g['carried'] = s5_bwd(
        proj, dyg1, dyg2, sv['y'], p['a_d'], sv['s_re'], sv['s_im'], sv['Bre'], sv['Bim'], sv['Cre'], sv['Cim'],
        sv['prr'], sv['pir'], comm=comm)
    g['a_d'] = dd
    dCim = -dCim

    def f3(du_, dz_):
        return [jnp.concatenate([du_, dz_], axis=1)], []
    (dproj,), _ = rowwise(f3, [rw(du), rw(dz)], [], [(2048, BF16)], [], 256, 'a_dproj')
    dlr, dli, dldt, dbr, dbi = sv['disc_vjp']((da_re.reshape(SSM_G, SSM_P), da_im.reshape(SSM_G, SSM_P),
                                               _b_untile(dBre), _b_untile(dBim)))
    g['a_lam_re'], g['a_lam_im'], g['a_log_dt'] = dlr[None], dli[None], dldt[None]
    g['a_b_re'], g['a_b_im'] = dbr[None], dbi[None]
    g['a_c_re'], g['a_c_im'] = _c_untile(dCre)[None], _c_untile(dCim)[None]
    _dw(g, sink, 'a_w_in', sv['h'], dproj, 'a_dwin')
    dh = mm(dproj, w['a_w_in'], 'nt', 'a_dh')
    return dh, g


def _t5_bucket_np():
    qi = np.arange(WINDOW)[:, None]
    kj = np.arange(2 * WINDOW)[None, :]
    dist = np.maximum(qi + WINDOW - kj, 0)
    max_exact = REL_BUCKETS // 2
    dist_f = np.maximum(dist, 1).astype(np.float32)
    large = max_exact + (np.log(dist_f / np.float32(max_exact)) / np.float32(math.log(REL_MAX_DIST / max_exact))
                         * np.float32(REL_BUCKETS - max_exact)).astype(np.int32)
    large = np.minimum(large, REL_BUCKETS - 1)
    return np.where(dist < max_exact, dist, large).astype(np.int32)


SWA_GRP = SWA_HEADS // SWA_KV


def _swa_kv(kvp, kvc, kvh):
    kb = jnp.concatenate([kvp[:, kvh * 64:(kvh + 1) * 64], kvc[:, kvh * 64:(kvh + 1) * 64]], 0).astype(BF16)
    vb = jnp.concatenate([kvp[:, 128 + kvh * 64:128 + (kvh + 1) * 64], kvc[:, 128 + kvh * 64:128 + (kvh + 1) * 64]],
                         0).astype(BF16)
    return kb, vb


def _swa_stack(x, kvh):
    return jnp.concatenate([x[:, (kvh * SWA_GRP + g) * 64:(kvh * SWA_GRP + g + 1) * 64] for g in range(SWA_GRP)],
                           axis=0).astype(BF16)


def _swa_group(bias_ref, kvh):
    return bias_ref[kvh * SWA_GRP:(kvh + 1) * SWA_GRP].reshape(SWA_GRP * WINDOW, 2 * WINDOW)


def _swa_sinks(sink_ref, kvh):
    return jnp.concatenate([jnp.broadcast_to(sink_ref[0:1, kvh * SWA_GRP + g:kvh * SWA_GRP + g + 1], (WINDOW, 1))
                            for g in range(SWA_GRP)], axis=0)


def _swa_probs(q, kb, bias_h, sink, valid):
    s = lax.dot_general(q, kb, (((1,), (1,)), ((), ())), preferred_element_type=F32) * (HEAD_DIM ** -0.5)
    s = jnp.where(valid, s + bias_h, NEG_INF)
    m = jnp.maximum(jnp.max(s, axis=-1, keepdims=True), sink)
    e = jnp.exp(s - m)
    es = jnp.exp(sink - m)
    den = jnp.sum(e, axis=-1, keepdims=True) + es
    return e / den, es / den


def _swa_valid(n):
    qi = lax.broadcasted_iota(jnp.int32, (SWA_GRP * WINDOW, 2 * WINDOW), 0) & (WINDOW - 1)
    kj = lax.broadcasted_iota(jnp.int32, (SWA_GRP * WINDOW, 2 * WINDOW), 1)
    dist = qi + WINDOW - kj
    return (dist >= 0) & (dist < WINDOW) & ((kj >= WINDOW) | (n > 0))


def swa_fwd(proj, bias, sinks, comm=None):
    L = proj.shape[0]

    def body(z_ref, q_ref, kvc_ref, kvp_ref, bias_ref, sink_ref, o_ref, po_ref):
        n = pl.program_id(0)
        valid = _swa_valid(n)
        q, kvc, kvp = q_ref[...], kvc_ref[...], kvp_ref[...]
        outs = []
        for kvh in range(SWA_KV):
            kb, vb = _swa_kv(kvp, kvc, kvh)
            p, _ = _swa_probs(_swa_stack(q, kvh), kb, _swa_group(bias_ref, kvh), _swa_sinks(sink_ref, kvh), valid)
            o8 = jnp.dot(p.astype(BF16), vb, preferred_element_type=F32)
            outs += [o8[g * WINDOW:(g + 1) * WINDOW] for g in range(SWA_GRP)]
        o = jnp.concatenate(outs, axis=1)
        o_ref[...] = o
        po_ref[...] = (o * silu(z_ref[...])).astype(po_ref.dtype)

    return carried(
        body, comm, grid=(L // WINDOW,),
        in_specs=[pl.BlockSpec((WINDOW, 1024), lambda n: (n, 0)), pl.BlockSpec((WINDOW, 1024), lambda n: (n, 1)),
                  pl.BlockSpec((WINDOW, 256), lambda n: (n, 8)),
                  pl.BlockSpec((WINDOW, 256), lambda n: (jnp.maximum(n - 1, 0), 8)),
                  pl.BlockSpec((SWA_HEADS, WINDOW, 2 * WINDOW), lambda n: (0, 0, 0)),
                  pl.BlockSpec((1, SWA_HEADS), lambda n: (0, 0))],
        out_specs=[pl.BlockSpec((WINDOW, 1024), lambda n: (n, 0))] * 2,
        out_shape=[jax.ShapeDtypeStruct((L, 1024), F32), jax.ShapeDtypeStruct((L, 1024), BF16)],
        semantics=("parallel",), name='b_attn')(proj, proj, proj, proj, bias, sinks)


def swa_bwd(proj, do, bias, sinks, comm=None):
    L = proj.shape[0]

    def body(q_ref, kvc_ref, kvp_ref, do_ref, bias_ref, sink_ref, dq_ref, dkv_ref, dbias_ref, dsink_ref):
        n = pl.program_id(0)

        @pl.when(n == 0)
        def _():
            dkv_ref[...] = jnp.zeros_like(dkv_ref)
            dbias_ref[...] = jnp.zeros_like(dbias_ref)
            dsink_ref[...] = jnp.zeros_like(dsink_ref)

        valid = _swa_valid(n)
        q, kvc, kvp, do_ = q_ref[...], kvc_ref[...], kvp_ref[...], do_ref[...]
        dqs, dks, dvs, dsk = [], [], [], []
        for kvh in range(SWA_KV):
            kb, vb = _swa_kv(kvp, kvc, kvh)
            q8, do8 = _swa_stack(q, kvh), _swa_stack(do_, kvh)
            p, ps = _swa_probs(q8, kb, _swa_group(bias_ref, kvh), _swa_sinks(sink_ref, kvh), valid)
            dp = lax.dot_general(do8, vb, (((1,), (1,)), ((), ())), preferred_element_type=F32)
            delta = jnp.sum(p * dp, axis=-1, keepdims=True)
            ds = p * (dp - delta)
            col = -ps * delta
            dsk += [jnp.sum(col[g * WINDOW:(g + 1) * WINDOW], axis=0, keepdims=True) for g in range(SWA_GRP)]
            dbias_ref[kvh * SWA_GRP:(kvh + 1) * SWA_GRP] += ds.reshape(SWA_GRP, WINDOW, 2 * WINDOW)
            dsb = (ds * (HEAD_DIM ** -0.5)).astype(BF16)
            dq8 = jnp.dot(dsb, kb, preferred_element_type=F32)
            dqs += [dq8[g * WINDOW:(g + 1) * WINDOW] for g in range(SWA_GRP)]
            dks.append(lax.dot_general(dsb, q8, (((0,), (0,)), ((), ())), preferred_element_type=F32))
            dvs.append(lax.dot_general(p.astype(BF16), do8, (((0,), (0,)), ((), ())), preferred_element_type=F32))
        dq_ref[...] = jnp.concatenate(dqs, axis=1)
        dsink_ref[...] += jnp.concatenate(dsk, axis=1)
        both = jnp.concatenate(dks + dvs, axis=1)
        r_cur = pl.multiple_of(n * WINDOW, WINDOW)
        r_prev = pl.multiple_of(jnp.maximum(n - 1, 0) * WINDOW, WINDOW)
        dkv_ref[pl.ds(r_prev, WINDOW), :] += both[:WINDOW]
        dkv_ref[pl.ds(r_cur, WINDOW), :] += both[WINDOW:]

    return carried(
        body, comm, grid=(L // WINDOW,),
        in_specs=[pl.BlockSpec((WINDOW, 1024), lambda n: (n, 1)), pl.BlockSpec((WINDOW, 256), lambda n: (n, 8)),
                  pl.BlockSpec((WINDOW, 256), lambda n: (jnp.maximum(n - 1, 0), 8)),
                  pl.BlockSpec((WINDOW, 1024), lambda n: (n, 0)),
                  pl.BlockSpec((SWA_HEADS, WINDOW, 2 * WINDOW), lambda n: (0, 0, 0)),
                  pl.BlockSpec((1, SWA_HEADS), lambda n: (0, 0))],
        out_specs=[pl.BlockSpec((WINDOW, 1024), lambda n: (n, 0)), pl.BlockSpec((L, 256), lambda n: (0, 0)),
                   pl.BlockSpec((SWA_HEADS, WINDOW, 2 * WINDOW), lambda n: (0, 0, 0)),
                   pl.BlockSpec((1, SWA_HEADS), lambda n: (0, 0))],
        out_shape=[jax.ShapeDtypeStruct((L, 1024), F32), jax.ShapeDtypeStruct((L, 256), F32),
                   jax.ShapeDtypeStruct((SWA_HEADS, WINDOW, 2 * WINDOW), F32), jax.ShapeDtypeStruct((1, SWA_HEADS), F32)],
        semantics=("arbitrary",), name='b_attn_bwd')(proj, proj, proj, do, bias, sinks)


def swa_bias(rel_bias):
    def body(bk_ref, rb_ref, o_ref):
        bk = bk_ref[...]
        for h in range(SWA_HEADS):
            acc = jnp.zeros((WINDOW, 2 * WINDOW), F32)
            for b in range(REL_BUCKETS):
                acc = jnp.where(bk == b, rb_ref[b, h], acc)
            o_ref[h] = acc

    return pl.pallas_call(
        body, out_shape=jax.ShapeDtypeStruct((SWA_HEADS, WINDOW, 2 * WINDOW), F32),
        in_specs=[pl.BlockSpec(memory_space=pltpu.VMEM), pl.BlockSpec(memory_space=pltpu.SMEM)],
        out_specs=pl.BlockSpec(memory_space=pltpu.VMEM), name='b_bias')(jnp.asarray(_t5_bucket_np()), rel_bias)


def layer_b_fwd(h, w, p, comm=None):
    proj = mm(h, w['b_w_in'], 'nn', 'b_proj')
    bias = swa_bias(p['rel_bias'])
    (o, po), carried_out = swa_fwd(proj, bias, p['b_sinks'], comm=comm)
    yb = mm(po, w['b_w_out'], 'nn', 'b_out')
    return yb, dict(carried=carried_out, h=h, proj=proj, bias=bias, o=o, po=po)


def layer_b_bwd(dyb, w, p, sv, comm=None, sink=None):
    g = {}
    dpo = mm(dyb, w['b_w_out'], 'nt', 'b_dpo')
    _dw(g, sink, 'b_w_out', sv['po'], dyb, 'b_dwout')
    proj = sv['proj']

    def f1(dpo_, o, z):
        return [dpo_ * silu(z), dpo_ * o * silu_grad(z)], []
    (do, dz), _ = rowwise(f1, [rw(dpo), rw(sv['o']), rw(proj, 1024, 0)], [], [(1024, BF16), (1024, F32)], [], 256, 'b_gate_bwd')
    (dq, dkv, dbias, dsinks), g['carried'] = swa_bwd(proj, do, sv['bias'], p['b_sinks'], comm=comm)
    g['b_sinks'] = dsinks
    onehot = jnp.asarray(np.eye(REL_BUCKETS, dtype=np.float32)[_t5_bucket_np().reshape(-1)])

    def f2(db, oh):
        return [], [lax.dot_general(db, oh, (((1,), (0,)), ((), ())), preferred_element_type=F32,
                                    precision=lax.Precision.HIGHEST)]
    _, (drel,) = rowwise(f2, [(dbias.reshape(SWA_HEADS, -1), pl.BlockSpec((SWA_HEADS, 4096), lambda i: (0, i))),
                              (onehot, pl.BlockSpec((4096, REL_BUCKETS), lambda i: (i, 0)))], [], [],
                         [(SWA_HEADS, REL_BUCKETS)], 4096, 'b_drel', n_steps=(2 * WINDOW * WINDOW) // 4096)
    g['rel_bias'] = drel.T

    def f3(dz_, dq_, dkv_):
        return [jnp.concatenate([dz_, dq_, dkv_], axis=1)], []
    (dproj,), _ = rowwise(f3, [rw(dz), rw(dq), rw(dkv)], [], [(2304, BF16)], [], 256, 'b_dproj')
    _dw(g, sink, 'b_w_in', sv['h'], dproj, 'b_dwin')
    dh = mm(dproj, w['b_w_in'], 'nt', 'b_dh')
    return dh, g


MLA_SCALE = (MLA_NOPE + MLA_ROPE) ** -0.5


def _rope_tables(L):
    inv = ROPE_BASE ** (-jnp.arange(0, MLA_ROPE, 2, dtype=F32) / MLA_ROPE)
    ang = jnp.arange(L, dtype=F32)[:, None] * inv[None, :]
    c, s = jnp.cos(ang), jnp.sin(ang)
    one, zero, pad = jnp.ones((L, 128), F32), jnp.zeros((L, 128), F32), jnp.zeros((L, 64), F32)
    return (jnp.concatenate([one, c, c, c, c, pad], 1), jnp.concatenate([zero, s, s, s, s, pad], 1))


def _rot(x, transpose=False):
    w = x.shape[1]
    lane = lax.broadcasted_iota(jnp.int32, x.shape, 1)
    up = pltpu.roll(x, w - 16, 1)
    dn = pltpu.roll(x, 16, 1)
    first = (lane % 32) < 16
    return jnp.where(first, up, -dn) if transpose else jnp.where(first, -up, dn)


MLA_QT = 256


def _mla_exp(qf, kf, t, qt):
    n_k = kf.shape[0]
    s = lax.dot_general(qf, kf, (((1,), (1,)), ((), ())), preferred_element_type=F32) * MLA_SCALE
    qpos = t * qt + lax.broadcasted_iota(jnp.int32, (qt, n_k), 0)
    kpos = lax.broadcasted_iota(jnp.int32, (qt, n_k), 1)
    s = jnp.where(kpos <= qpos, s, NEG_INF)
    e = jnp.exp(s - jnp.max(s, axis=-1, keepdims=True))
    return e, jnp.sum(e, axis=-1, keepdims=True)


def _mla_heads(q, kv, kr):
    out = []
    for j in range(2):
        qf = jnp.concatenate([q[:, j * 64:(j + 1) * 64], q[:, 128 + j * 32:128 + (j + 1) * 32]], axis=1)
        kf = jnp.concatenate([kv[:, j * 64:(j + 1) * 64], kr], axis=1)
        out.append((qf, kf, kv[:, 128 + j * 64:128 + (j + 1) * 64]))
    return out


def mla_fwd(q, kv, kr, comm=None):
    L = q.shape[0]
    qt = min(MLA_QT, L)
    nq = L // qt

    def body(q_ref, kv_ref, kr_ref, o_ref):
        for t in range(nq):
            @pl.when(pl.program_id(1) == t)
            def _(t=t):
                n_k = (t + 1) * qt
                outs = []
                for qf, kf, v in _mla_heads(q_ref[...], kv_ref[0:n_k, :], kr_ref[0:n_k, 0:MLA_ROPE]):
                    e, den = _mla_exp(qf, kf, t, qt)
                    outs.append(jnp.dot(e.astype(BF16), v, preferred_element_type=F32) / den)
                o_ref[...] = jnp.concatenate(outs, axis=1)

    return carried(
        body, comm, grid=(MLA_HEADS // 2, nq),
        in_specs=[pl.BlockSpec((qt, 256), lambda hp, n: (n, hp)), pl.BlockSpec((L, 256), lambda hp, n: (0, hp)),
                  pl.BlockSpec((L, 128), lambda hp, n: (0, 0))],
        out_specs=pl.BlockSpec((qt, 128), lambda hp, n: (n, hp)), out_shape=jax.ShapeDtypeStruct((L, 1024), F32),
        semantics=("parallel", "parallel"), name='c_attn')(q, kv, kr)


def mla_bwd(q, kv, kr, do, comm=None):
    L = q.shape[0]
    qt = min(MLA_QT, L)
    nq = L // qt

    def body(q_ref, kv_ref, kr_ref, do_ref, dq_ref, dkv_ref, dkr_ref):
        @pl.when(pl.program_id(1) == 0)
        def _():
            dkv_ref[...] = jnp.zeros_like(dkv_ref)
            dkr_ref[...] = jnp.zeros_like(dkr_ref)

        for t in range(nq):
            @pl.when(pl.program_id(1) == t)
            def _(t=t):
                n_k = (t + 1) * qt
                do_ = do_ref[...]
                dqn, dqr, dkn, dvs = [], [], [], []
                dkr = jnp.zeros((n_k, MLA_ROPE), F32)
                for j, (qf, kf, v) in enumerate(_mla_heads(q_ref[...], kv_ref[0:n_k, :], kr_ref[0:n_k, 0:MLA_ROPE])):
                    doh = do_[:, j * 64:(j + 1) * 64]
                    e, den = _mla_exp(qf, kf, t, qt)
                    p = e * (1.0 / den)
                    dp = lax.dot_general(doh, v, (((1,), (1,)), ((), ())), preferred_element_type=F32)
                    ds = (p * (dp - jnp.sum(p * dp, axis=-1, keepdims=True)) * MLA_SCALE).astype(BF16)
                    dqf = jnp.dot(ds, kf, preferred_element_type=F32)
                    dkf = lax.dot_general(ds, qf, (((0,), (0,)), ((), ())), preferred_element_type=F32)
                    dvs.append(lax.dot_general(p.astype(BF16), doh, (((0,), (0,)), ((), ())), preferred_element_type=F32))
                    dqn.append(dqf[:, :MLA_NOPE])
                    dqr.append(dqf[:, MLA_NOPE:])
                    dkn.append(dkf[:, :MLA_NOPE])
                    dkr = dkr + dkf[:, MLA_NOPE:]
                dq_ref[...] = jnp.concatenate(dqn + dqr + [jnp.zeros((qt, 64), F32)], axis=1)
                dkv_ref[0:n_k, :] += jnp.concatenate(dkn + dvs, axis=1)
                dkr_ref[0, 0:n_k, :] += jnp.concatenate([dkr, jnp.zeros((n_k, 128 - MLA_ROPE), F32)], axis=1)

    return carried(
        body, comm, grid=(MLA_HEADS // 2, nq),
        in_specs=[pl.BlockSpec((qt, 256), lambda hp, n: (n, hp)), pl.BlockSpec((L, 256), lambda hp, n: (0, hp)),
                  pl.BlockSpec((L, 128), lambda hp, n: (0, 0)), pl.BlockSpec((qt, 128), lambda hp, n: (n, hp))],
        out_specs=[pl.BlockSpec((qt, 256), lambda hp, n: (n, hp)), pl.BlockSpec((L, 256), lambda hp, n: (0, hp)),
                   pl.BlockSpec((1, L, 128), lambda hp, n: (hp, 0, 0))],
        out_shape=[jax.ShapeDtypeStruct((L, 2048), F32), jax.ShapeDtypeStruct((L, 2048), F32),
                   jax.ShapeDtypeStruct((MLA_HEADS // 2, L, 128), F32)],
        semantics=("parallel", "arbitrary"), name='c_attn_bwd')(q, kv, kr, do)


def layer_c_fwd(h, w, p, comm=None):
    L = h.shape[0]
    proj = mm(h, w['c_w_in'], 'nn', 'c_proj')

    def f1(c, gq, gk):
        return [rms_fwd(c[:, :768], gq), rms_fwd(c[:, 768:], gk)], []
    (cqn, ckvn), _ = rowwise(f1, [rw(proj, 1024, 1)], [p['c_q_norm'], p['c_kv_norm']], [(768, BF16), (256, BF16)], [],
                             256, 'c_norms')
    qf = mm(cqn, w['c_w_uq'], 'nn', 'c_uq')
    kvf = mm(ckvn, w['c_w_ukv'], 'nn', 'c_ukv', out_dtype=BF16)
    cos, sin = _rope_tables(L)

    def f2(q_, kr_, c, s):
        c8, s8 = jnp.tile(c, (1, 8)), jnp.tile(s, (1, 8))
        return [q_ * c8 + _rot(q_) * s8, kr_ * c[:, 128:] + _rot(kr_) * s[:, 128:]], []
    (q, kr), _ = rowwise(f2, [rw(qf), rw(proj, 128, 16), rw(cos), rw(sin)], [], [(2048, BF16), (128, BF16)], [], 256,
                         'c_rope')
    o, carried_out = mla_fwd(q, kvf, kr, comm=comm)

    def f3(o_, z):
        return [o_ * silu(z)], []
    (po,), _ = rowwise(f3, [rw(o), rw(proj, 1024, 0)], [], [(1024, BF16)], [], 256, 'c_gate')
    yb = mm(po, w['c_w_out'], 'nn', 'c_out')
    return yb, dict(carried=carried_out, h=h, proj=proj, cqn=cqn, ckvn=ckvn, q=q, kv=kvf, kr=kr, o=o, po=po, cos=cos, sin=sin)


def layer_c_bwd(dyb, w, p, sv, comm=None, sink=None):
    g = {}
    dpo = mm(dyb, w['c_w_out'], 'nt', 'c_dpo')
    _dw(g, sink, 'c_w_out', sv['po'], dyb, 'c_dwout')
    proj = sv['proj']
    L = proj.shape[0]

    def f1(dpo_, o, z):
        return [dpo_ * silu(z), dpo_ * o * silu_grad(z)], []
    (do, dz), _ = rowwise(f1, [rw(dpo), rw(sv['o']), rw(proj, 1024, 0)], [], [(1024, BF16), (1024, F32)], [], 256,
                          'c_gate_bwd')
    (dq, dkvf, dkr8), g['carried'] = mla_bwd(sv['q'], sv['kv'], sv['kr'], do, comm=comm)

    def f2(dq_, dkr_, c, s):
        c8, s8 = jnp.tile(c, (1, 8)), jnp.tile(s, (1, 8))
        dk = jnp.sum(dkr_, axis=0)
        return [dq_ * c8 + _rot(dq_ * s8, True), dk * c[:, 128:] + _rot(dk * s[:, 128:], True)], []
    tl = 256
    (dqf, dkr), _ = rowwise(f2, [rw(dq), (dkr8, pl.BlockSpec((8, tl, 128), lambda i: (0, i, 0))), rw(sv['cos']),
                                 rw(sv['sin'])], [], [(2048, BF16), (128, F32)], [], tl, 'c_rope_bwd')
    _dw(g, sink, 'c_w_uq', sv['cqn'], dqf, 'c_dwuq')
    _dw(g, sink, 'c_w_ukv', sv['ckvn'], dkvf, 'c_dwukv')
    dcqn = mm(dqf, w['c_w_uq'], 'nt', 'c_dcqn')
    dckvn = mm(dkvf, w['c_w_ukv'], 'nt', 'c_dckvn')

    def f3(c, dq_, dk_, dz_, dkr_, gq, gk):
        dcq, dgq = rms_bwd(c[:, :768], gq, dq_)
        dckv, dgk = rms_bwd(c[:, 768:], gk, dk_)
        return [jnp.concatenate([dz_, dcq, dckv, dkr_], axis=1)], [dgq, dgk]
    (dproj,), (dgq, dgk) = rowwise(f3, [rw(proj, 1024, 1), rw(dcqn), rw(dckvn), rw(dz), rw(dkr)],
                                   [p['c_q_norm'], p['c_kv_norm']], [(2176, BF16)], [(1, 768), (1, 256)], 256, 'c_dproj')
    g['c_q_norm'], g['c_kv_norm'] = dgq, dgk
    _dw(g, sink, 'c_w_in', sv['h'], dproj, 'c_dwin')
    dh = mm(dproj, w['c_w_in'], 'nt', 'c_dh')
    return dh, g


def _sgu_mix(wm, v, transpose):
    outs = []
    dims = (((0,), (0,)), ((), ())) if transpose else (((1,), (0,)), ((), ()))
    for gi in range(SGU_G):
        outs.append(lax.dot_general(wm[gi], v[:, gi * SGU_C:(gi + 1) * SGU_C].astype(BF16), dims,
                                    preferred_element_type=F32))
    return jnp.concatenate(outs, axis=1)


def _sgu_wmask(ws):
    t = lax.broadcasted_iota(jnp.int32, (SGU_T, SGU_T), 0)
    s = lax.broadcasted_iota(jnp.int32, (SGU_T, SGU_T), 1)
    return jnp.where((s <= t)[None], ws, 0.0).astype(BF16)


def _ln_stats(v):
    mu = jnp.mean(v, axis=-1, keepdims=True)
    vc = v - mu
    rstd = lax.rsqrt(jnp.mean(vc * vc, axis=-1, keepdims=True) + EPS)
    return vc * rstd, rstd


def layer_d_fwd(h, w, p):
    proj = mm(h, w['d_w_in'], 'nn', 'd_proj')
    bias = jnp.repeat(p['d_b_s'][0].T, SGU_C, axis=1)

    def f1(u_, v_, z, ws, lg, lb, bs):
        xh, _ = _ln_stats(gelu(v_))
        s = _sgu_mix(_sgu_wmask(ws), xh * lg + lb, False) + bs
        return [gelu(u_) * s * silu(z)], []
    (po,), _ = rowwise(f1, [rw(proj, 1024, 0), rw(proj, 1024, 1), rw(proj, 1024, 2)],
                       [p['d_w_s'][0], p['d_ln_g'], p['d_ln_b'], bias], [(1024, BF16)], [], SGU_T, 'd_mix')
    yb = mm(po, w['d_w_out'], 'nn', 'd_out')
    return yb, dict(h=h, proj=proj, po=po, bias=bias)


def layer_d_bwd(dyb, w, p, sv, sink=None):
    g = {}
    dpo = mm(dyb, w['d_w_out'], 'nt', 'd_dpo')
    _dw(g, sink, 'd_w_out', sv['po'], dyb, 'd_dwout')
    proj = sv['proj']

    def f1(dpo_, u_, v_, z, ws, lg, lb, bs):
        wm = _sgu_wmask(ws)
        gv = gelu(v_)
        xh, rstd = _ln_stats(gv)
        vn = xh * lg + lb
        s = _sgu_mix(wm, vn, False) + bs
        gu, sz = gelu(u_), silu(z)
        du = dpo_ * s * sz
        ds = dpo_ * gu * sz
        dz = dpo_ * gu * s * silu_grad(z)
        dsb = ds.astype(BF16)
        dws = jnp.stack([lax.dot_general(dsb[:, gi * SGU_C:(gi + 1) * SGU_C], vn[:, gi * SGU_C:(gi + 1) * SGU_C].astype(BF16),
                                         (((1,), (1,)), ((), ())), preferred_element_type=F32) for gi in range(SGU_G)])
        dvn = _sgu_mix(wm, ds, True)
        dlg = jnp.sum(dvn * xh, axis=0, keepdims=True)
        dlb = jnp.sum(dvn, axis=0, keepdims=True)
        dxh = dvn * lg
        dgv = rstd * (dxh - jnp.mean(dxh, axis=-1, keepdims=True) - xh * jnp.mean(dxh * xh, axis=-1, keepdims=True))
        return ([jnp.concatenate([du * gelu_grad(u_), dgv * gelu_grad(v_), dz], axis=1)], [dws, ds, dlg, dlb])
    (dproj,), (dws, dbs, dlg, dlb) = rowwise(
        f1, [rw(dpo), rw(proj, 1024, 0), rw(proj, 1024, 1), rw(proj, 1024, 2)],
        [p['d_w_s'][0], p['d_ln_g'], p['d_ln_b'], sv['bias']], [(3072, BF16)],
        [(SGU_G, SGU_T, SGU_T), (SGU_T, 1024), (1, 1024), (1, 1024)], SGU_T, 'd_mix_bwd')
    tril = np.tril(np.ones((SGU_T, SGU_T), dtype=bool))
    g['d_w_s'] = jnp.where(tril[None], dws, 0.0)[None]
    g['d_b_s'] = dbs.reshape(SGU_T, SGU_G, SGU_C).sum(-1).T[None]
    g['d_ln_g'], g['d_ln_b'] = dlg, dlb
    _dw(g, sink, 'd_w_in', sv['h'], dproj, 'd_dwin')
    dh = mm(dproj, w['d_w_in'], 'nt', 'd_dh')
    return dh, g


def _coords():
    return lax.axis_index("x"), lax.axis_index("y"), lax.axis_index("c")


class AllGather:
    def __init__(self, x):
        self.ins = [x]
        self.outs = [jax.ShapeDtypeStruct((N_DEV,) + x.shape, x.dtype)]
        self.scratch = [pltpu.SemaphoreType.DMA((7,)), pltpu.SemaphoreType.DMA((7,)), pltpu.SemaphoreType.DMA(())]

    def hooks(self, n_steps):
        return [(0, functools.partial(self.phase, 0), False), ((n_steps * 5) // 8, functools.partial(self.phase, 1), False),
                (n_steps - 1, functools.partial(self.phase, 2), True)]

    @staticmethod
    def phase(which, ins, outs, scratch):
        (x_ref,), (out_ref,), (send_sems, recv_sems, local_sem) = ins, outs, scratch
        x_, y_, c_ = _coords()
        me, sibling = (x_, y_, c_), (x_, y_, 1 - c_)
        chips = [(1 - x_, y_), (x_, 1 - y_), (1 - x_, 1 - y_)]

        def slot(px, py, pc):
            return out_ref.at[4 * px + 2 * py + pc]

        def copy(k, block, to, src=None):
            return pltpu.make_async_remote_copy(src_ref=slot(*block) if src is None else src, dst_ref=slot(*block),
                                                send_sem=send_sems.at[k], recv_sem=recv_sems.at[k], device_id=to,
                                                device_id_type=MESH)

        mine = pltpu.make_async_copy(x_ref, slot(*me), local_sem)
        first = [copy(0, me, sibling, src=x_ref)]
        first += [copy(1 + j, me, (*chip, c_), src=x_ref) for j, chip in enumerate(chips)]
        passed = [copy(4 + j, (*chip, c_), sibling) for j, chip in enumerate(chips)]
        if which == 0:
            mine.start()
            for cp in first:
                cp.start()
        elif which == 1:
            for j, chip in enumerate(chips):
                copy(1 + j, (*chip, c_), me).wait_recv()
                passed[j].start()
        else:
            copy(0, sibling, me).wait_recv()
            for j, chip in enumerate(chips):
                copy(4 + j, (*chip, 1 - c_), me).wait_recv()
            for cp in first + passed:
                cp.wait_send()
            mine.wait()


class ChipExchange:
    def __init__(self, part):
        self.ins = [part]
        self.outs = [jax.ShapeDtypeStruct((3,) + part.shape[1:], part.dtype)]
        self.scratch = [pltpu.SemaphoreType.DMA((3,)), pltpu.SemaphoreType.DMA((3,))]

    def hooks(self, n_steps):
        return [(0, functools.partial(self.phase, 0), False), (n_steps - 1, functools.partial(self.phase, 1), True)]

    @staticmethod
    def phase(which, ins, outs, scratch):
        (p_ref,), (land_ref,), (send_sems, recv_sems) = ins, outs, scratch
        x_, y_, c_ = _coords()
        copies = []
        for r, (fx, fy) in enumerate([(1, 0), (0, 1), (1, 1)]):
            tx = jnp.where(fx == 1, 1 - x_, x_)
            ty = jnp.where(fy == 1, 1 - y_, y_)
            copies.append(pltpu.make_async_remote_copy(src_ref=p_ref.at[2 * tx + ty], dst_ref=land_ref.at[r],
                                                       send_sem=send_sems.at[r], recv_sem=recv_sems.at[r],
                                                       device_id=(tx, ty, c_), device_id_type=MESH))
        if which == 0:
            for cp in copies:
                cp.start()
        else:
            for cp in copies:
                cp.wait_recv()
            for cp in copies:
                cp.wait_send()


class Both:
    def __init__(self, a, b):
        self.parts = (a, b)
        self.ins, self.outs, self.scratch = a.ins + b.ins, a.outs + b.outs, a.scratch + b.scratch

    def hooks(self, n_steps):
        res, oi, oo, osc = [], 0, 0, 0
        for p in self.parts:
            sl = (slice(oi, oi + len(p.ins)), slice(oo, oo + len(p.outs)), slice(osc, osc + len(p.scratch)))
            res += [(at, functools.partial(self.sub, fn, sl), after) for at, fn, after in p.hooks(n_steps)]
            oi, oo, osc = oi + len(p.ins), oo + len(p.outs), osc + len(p.scratch)
        return res

    @staticmethod
    def sub(fn, sl, ins, outs, scratch):
        fn(ins[sl[0]], outs[sl[1]], scratch[sl[2]])


def run_comm(comm, name):
    def body(*refs):
        ci, co = len(comm.ins), len(comm.outs)
        for _, fn, _ in comm.hooks(1):
            fn(refs[:ci], refs[ci:ci + co], refs[ci + co:])

    return pl.pallas_call(body, out_shape=list(comm.outs), in_specs=[ANY] * len(comm.ins),
                          out_specs=[ANY] * len(comm.outs), scratch_shapes=list(comm.scratch), name=name)(*comm.ins)


def all_gather(x, name):
    return run_comm(AllGather(x), name)[0]


def rs_sibling(gfull, tag):
    _, R, C = gfull.shape

    def body(g_ref, land_ref, send_sems, recv_sems):
        x_, y_, c_ = _coords()
        copies = []
        for k in range(4):
            cp = pltpu.make_async_remote_copy(src_ref=g_ref.at[2 * k + 1 - c_], dst_ref=land_ref.at[k],
                                              send_sem=send_sems.at[k], recv_sem=recv_sems.at[k],
                                              device_id=(x_, y_, 1 - c_), device_id_type=MESH)
            cp.start()
            copies.append(cp)
        for cp in copies:
            cp.wait_recv()
        for cp in copies:
            cp.wait_send()

    return pl.pallas_call(
        body, out_shape=jax.ShapeDtypeStruct((4, R, C), gfull.dtype), in_specs=[ANY], out_specs=ANY,
        scratch_shapes=[pltpu.SemaphoreType.DMA((4,)), pltpu.SemaphoreType.DMA((4,))], name='rs_sibling_' + tag)(gfull)


def rs_pair_add(gfull, land, core, tag):
    _, R, C = gfull.shape
    tl = R

    def body(c_ref, g_ref, l_ref, o_ref):
        o_ref[...] = (g_ref[...].astype(F32) + l_ref[...].astype(F32)).astype(BF16)

    return pl.pallas_call(
        body, out_shape=jax.ShapeDtypeStruct((4, R, C), BF16),
        grid_spec=pltpu.PrefetchScalarGridSpec(
            num_scalar_prefetch=1, grid=(4, R // tl),
            in_specs=[pl.BlockSpec((1, tl, C), lambda k, i, c: (2 * k + c[0], i, 0)),
                      pl.BlockSpec((1, tl, C), lambda k, i, c: (k, i, 0))],
            out_specs=pl.BlockSpec((1, tl, C), lambda k, i, c: (k, i, 0))),
        compiler_params=pltpu.CompilerParams(dimension_semantics=("parallel", "parallel")), name='rs_pair_add_' + tag)(
            core, gfull, land)


def rs_chips(part, tag):
    return run_comm(ChipExchange(part), 'rs_chips_' + tag)[0]


def _adam(wv, gv, mv, vv):
    m = ADAM_B1 * mv + (1.0 - ADAM_B1) * gv
    v = ADAM_B2 * vv + (1.0 - ADAM_B2) * (gv * gv)
    m_hat = m / (1.0 - ADAM_B1 ** ADAM_STEP)
    v_hat = v / (1.0 - ADAM_B2 ** ADAM_STEP)
    delta = -ADAM_LR * (m_hat / (jnp.sqrt(v_hat) + ADAM_EPS) + ADAM_WD * wv)
    return delta, m, v


def _sum4(p_ref, l_ref):
    return ((p_ref[0].astype(F32) + l_ref[0].astype(F32)) + l_ref[1].astype(F32)) + l_ref[2].astype(F32)


def rs_rep_sum(part, land, chip):
    def body(c_ref, p_ref, l_ref, o_ref):
        o_ref[...] = _sum4(p_ref, l_ref).astype(BF16)

    return pl.pallas_call(
        body, out_shape=jax.ShapeDtypeStruct((REP_SLOT, LANES), BF16),
        grid_spec=pltpu.PrefetchScalarGridSpec(
            num_scalar_prefetch=1, grid=(1,),
            in_specs=[pl.BlockSpec((1, REP_SLOT, LANES), lambda i, c: (c[0], 0, 0)),
                      pl.BlockSpec((3, REP_SLOT, LANES), lambda i, c: (0, 0, 0))],
            out_specs=pl.BlockSpec((REP_SLOT, LANES), lambda i, c: (0, 0))),
        compiler_params=pltpu.CompilerParams(dimension_semantics=("parallel",)), name='rs_rep')(chip, part, land)


def adam_param(name, shape, off, w, m, v, chip, part=None, land=None, grep=None):
    r, c = shape
    rp, nt, rb = _tiles(shape)
    rbw = min(r, rb)
    n_src = 2 if grep is None else 1
    ns = w.shape
    assert int(np.prod(ns[:-1])) == r and ns[-1] == c
    if len(ns) == 2:
        nat_block, nat_map = (rbw, c), lambda i, cr: (i, 0)
    elif int(np.prod(ns[:-2])) == 1:
        nat_block, nat_map = (1,) * (len(ns) - 2) + (rbw, c), lambda i, cr: (0,) * (len(ns) - 2) + (i, 0)
    else:
        assert len(ns) == 4 and ns[0] == 1 and rbw % ns[2] == 0
        nat_block, nat_map = (1, rbw // ns[2], ns[2], c), lambda i, cr: (0, i, 0, 0)

    def body(c_ref, *refs):
        srcs = refs[:n_src * nt]
        w_ref, m_ref, v_ref, g_ref, d_ref, nm_ref, nv_ref = refs[n_src * nt:]
        if grep is None:
            tiles = [_sum4(srcs[2 * t], srcs[2 * t + 1]) for t in range(nt)]
        else:
            tiles = [srcs[t][...].astype(F32) for t in range(nt)]
        g = (tiles[0] if nt == 1 else jnp.concatenate(tiles, axis=1))[:rbw, :c]
        g_ref[...] = g.reshape(nat_block)
        res = _adam(w_ref[...].reshape(rbw, c), g, m_ref[...].reshape(rbw, c), v_ref[...].reshape(rbw, c))
        for ref, val in zip((d_ref, nm_ref, nv_ref), res):
            ref[...] = val.reshape(nat_block)

    in_specs, args = [], []
    for t in range(nt):
        b0 = (off + t * rp) // rb
        assert (off + t * rp) % rb == 0
        if grep is None:
            in_specs += [pl.BlockSpec((1, rb, LANES), functools.partial(lambda i, cr, b0: (cr[0], b0 + i, 0), b0=b0)),
                         pl.BlockSpec((3, rb, LANES), functools.partial(lambda i, cr, b0: (0, b0 + i, 0), b0=b0))]
            args += [part, land]
        else:
            in_specs.append(pl.BlockSpec((rb, LANES), functools.partial(lambda i, cr, b0: (b0 + i, 0), b0=b0)))
            args.append(grep)
    nat = pl.BlockSpec(nat_block, nat_map)
    return pl.pallas_call(
        body, out_shape=[jax.ShapeDtypeStruct(ns, F32)] * 4,
        grid_spec=pltpu.PrefetchScalarGridSpec(num_scalar_prefetch=1, grid=(rp // rb,), in_specs=in_specs + [nat] * 3,
                                               out_specs=[nat] * 4),
        compiler_params=pltpu.CompilerParams(dimension_semantics=("parallel",)), name='adam_' + name)(
            chip, *args, w, m, v)


def adam_small(names, grep, P, M, V):
    in_specs, args, out_specs, out_shape, meta = [], [], [], [], []
    for n in names:
        s = REP_SHAPE[n]
        rp, nt, _ = _tiles(s)
        ns = P[n].shape
        for t in range(nt):
            b0 = (REP_OFF[n] + t * rp) // rp
            assert (REP_OFF[n] + t * rp) % rp == 0
            in_specs.append(pl.BlockSpec((rp, LANES), functools.partial(lambda i, b0: (b0, 0), b0=b0)))
            args.append(grep)
        nat = pl.BlockSpec(ns, functools.partial(lambda i, nd: (0,) * nd, nd=len(ns)))
        in_specs += [nat] * 3
        args += [P[n], M[n], V[n]]
        out_specs += [nat] * 4
        out_shape += [jax.ShapeDtypeStruct(ns, F32)] * 4
        meta.append((s, nt, ns))
    n_in = len(in_specs)

    def body(*refs):
        ins, outs = refs[:n_in], refs[n_in:]
        k = 0
        for p, ((r, c), nt, ns) in enumerate(meta):
            tiles = [ins[k + t][...].astype(F32) for t in range(nt)]
            w_ref, m_ref, v_ref = ins[k + nt:k + nt + 3]
            k += nt + 3
            g = (tiles[0] if nt == 1 else jnp.concatenate(tiles, axis=1))[:r, :c]
            res = (g,) + _adam(w_ref[...].reshape(r, c), g, m_ref[...].reshape(r, c), v_ref[...].reshape(r, c))
            for ref, val in zip(outs[4 * p:4 * p + 4], res):
                ref[...] = val.reshape(ns)

    res = pl.pallas_call(body, grid=(1,), in_specs=in_specs, out_specs=out_specs, out_shape=out_shape,
                         compiler_params=pltpu.CompilerParams(dimension_semantics=("arbitrary",)), name='adam_small')(*args)
    return {n: tuple(res[4 * p:4 * p + 4]) for p, n in enumerate(names)}


VM = pl.BlockSpec(memory_space=pltpu.VMEM)


def _tile_value(w, t, rp):
    r, c = w.shape
    wt = min(LANES, c - t * LANES)
    tile = w[:, t * LANES:t * LANES + wt]
    if wt < LANES:
        tile = jnp.concatenate([tile, jnp.zeros((r, LANES - wt), tile.dtype)], axis=1)
    if rp > r:
        tile = jnp.concatenate([tile, jnp.zeros((rp - r, LANES), tile.dtype)], axis=0)
    return tile


def pack_layer(layer, blocks):
    names = LAYER_PARAMS[layer]

    def body(*refs):
        tiles = []
        for ref, n in zip(refs[:-1], names):
            rp, nt, _ = _tiles(_block_shape(n))
            w = ref[...].reshape(_block_shape(n))
            tiles += [_tile_value(w, t, rp) for t in range(nt)]
        refs[-1][...] = jnp.concatenate(tiles, axis=0).astype(BF16)

    return pl.pallas_call(body, out_shape=jax.ShapeDtypeStruct((LAYER_ROWS[layer], LANES), BF16),
                          in_specs=[VM] * len(names), out_specs=VM, name='pack_' + layer)(*[blocks[n] for n in names])


def assemble(name, gathered):
    (rf, cf), ax = SHARDED[name]
    r, c = _block_shape(name)
    rp, nt, _ = _tiles((r, c))
    off = SH_OFF[name]
    out_cols = cf if ax == 0 else len(perm_index(name))

    def body(g_ref, o_ref, buf, sem):
        cp = pltpu.make_async_copy(g_ref.at[:, pl.ds(off, nt * rp), :], buf, sem)
        cp.start()
        cp.wait()
        if ax == 0:
            for j in range(N_DEV):
                o_ref[j * r:(j + 1) * r, :] = jnp.concatenate([buf[j, t * rp:(t + 1) * rp, :] for t in range(nt)], axis=1)
            return
        pieces = []
        for p in PERM[name]:
            if p[0] == 'z':
                pieces.append(jnp.zeros((r, p[1]), BF16))
                continue
            n0, w = p
            while w > 0:
                j, cb = divmod(n0, c)
                t, lane = divmod(cb, LANES)
                wl = min(w, LANES - lane, c - cb)
                pieces.append(buf[j, t * rp:t * rp + r, lane:lane + wl])
                n0, w = n0 + wl, w - wl
        o_ref[...] = jnp.concatenate(pieces, axis=1)

    return pl.pallas_call(
        body, out_shape=jax.ShapeDtypeStruct((rf, out_cols), BF16), in_specs=[ANY], out_specs=VM,
        scratch_shapes=[pltpu.VMEM((N_DEV, nt * rp, LANES), BF16), pltpu.SemaphoreType.DMA(())], name='asm_' + name)(
            gathered)


def chunk_grad(layer, name, dw, gfull):
    (rf, cf), ax = SHARDED[name]
    r, c = _block_shape(name)
    rp, nt, _ = _tiles((r, c))
    off = SH_OFF[name]
    if ax == 1:
        idx = perm_index(name) if name in PERM else np.arange(cf)
        inv = np.full(cf, -1)
        inv[idx[idx >= 0]] = np.nonzero(idx >= 0)[0]

    def body(*refs):
        dw_ref, o_ref, buf, sem = refs[0], refs[-3], refs[-2], refs[-1]
        for j in range(N_DEV):
            for t in range(nt):
                if ax == 0:
                    tile = dw_ref[j * r:(j + 1) * r, t * LANES:(t + 1) * LANES]
                else:
                    cols = inv[j * c + t * LANES:j * c + min((t + 1) * LANES, c)]
                    cuts = [0] + [k for k in range(1, len(cols)) if cols[k] != cols[k - 1] + 1] + [len(cols)]
                    pieces = [dw_ref[:, int(cols[a]):int(cols[b - 1]) + 1] for a, b in zip(cuts[:-1], cuts[1:])]
                    if len(cols) < LANES:
                        pieces.append(jnp.zeros((r, LANES - len(cols)), F32))
                    tile = pieces[0] if len(pieces) == 1 else jnp.concatenate(pieces, axis=1)
                    if rp > r:
                        tile = jnp.concatenate([tile, jnp.zeros((rp - r, LANES), F32)], axis=0)
                buf[j, t * rp:(t + 1) * rp, :] = tile.astype(BF16)
        cp = pltpu.make_async_copy(buf, o_ref.at[:, pl.ds(off, nt * rp), :], sem)
        cp.start()
        cp.wait()

    shape = jax.ShapeDtypeStruct((N_DEV, LAYER_ROWS[layer], LANES), BF16)
    scratch = [pltpu.VMEM((N_DEV, nt * rp, LANES), BF16), pltpu.SemaphoreType.DMA(())]
    if gfull is None:
        return pl.pallas_call(body, out_shape=shape, in_specs=[VM], out_specs=ANY, scratch_shapes=scratch,
                              name='chunk_' + name)(dw)
    return pl.pallas_call(body, out_shape=shape, in_specs=[VM, ANY], out_specs=ANY, scratch_shapes=scratch,
                          input_output_aliases={1: 0}, name='chunk_' + name)(dw, gfull)


class GradSink:
    def __init__(self, layer):
        self.layer, self.buf = layer, None

    def put(self, name, a, b, mm_name):
        (rf, cf), ax = SHARDED[name]
        r, c = _block_shape(name)
        direct = ax == 0 or (c % LANES == 0 and PERM[name] == [(0, cf)])
        if direct:
            self.buf = mm_tn_chunked(a, b, mm_name, self.layer, name, self.buf)
        else:
            self.add(name, mm(a, b, 'tn', mm_name))

    def add(self, name, dw):
        self.buf = chunk_grad(self.layer, name, dw, self.buf)


def mm_tn_chunked(a, b, mm_name, layer, wname, gfull):
    (rf, cf), ax = SHARDED[wname]
    r, c = _block_shape(wname)
    rp, nt, _ = _tiles((r, c))
    off = SH_OFF[wname]
    K, M = a.shape
    N = b.shape[1]
    assert (M, N) == (rf, cf) and rp == r
    if ax == 0:
        tn = 4 * LANES
        grid, bspec = (N // tn,), pl.BlockSpec((K, tn), lambda g: (0, g))
        ospec = pl.BlockSpec((N_DEV, 4 * r, LANES), lambda g: (0, off // (4 * r) + g, 0))
        assert off % (4 * r) == 0 and nt % 4 == 0

        def store(res, o_ref):
            for j in range(N_DEV):
                for q in range(4):
                    o_ref[j, q * r:(q + 1) * r, :] = res[j * r:(j + 1) * r, q * LANES:(q + 1) * LANES].astype(BF16)
    else:
        tn = c
        grid, bspec = (N_DEV,), pl.BlockSpec((K, tn), lambda g: (0, g))
        ospec = pl.BlockSpec((1, nt * r, LANES), lambda g: (g, off // (nt * r), 0))
        assert off % (nt * r) == 0

        def store(res, o_ref):
            for t in range(nt):
                o_ref[0, t * r:(t + 1) * r, :] = res[:, t * LANES:(t + 1) * LANES].astype(BF16)

    def body(*refs):
        a_ref, b_ref, o_ref = refs[0], refs[1], refs[-1]
        store(lax.dot_general(a_ref[...].astype(BF16), b_ref[...].astype(BF16), _TN, preferred_element_type=F32), o_ref)

    shape = jax.ShapeDtypeStruct((N_DEV, LAYER_ROWS[layer], LANES), BF16)
    aspec = pl.BlockSpec((K, M), lambda g: (0, 0))
    params = pltpu.CompilerParams(dimension_semantics=("parallel",))
    if gfull is None:
        return pl.pallas_call(body, grid=grid, in_specs=[aspec, bspec], out_specs=ospec, out_shape=shape,
                              compiler_params=params, name=mm_name)(a, b)
    return pl.pallas_call(body, grid=grid, in_specs=[aspec, bspec, ANY], out_specs=ospec, out_shape=shape,
                          input_output_aliases={2: 0}, compiler_params=params, name=mm_name)(a, b, gfull)


def pack_rep(G):
    def body(*refs):
        tiles = []
        for ref, s in zip(refs[:-1], REP_SHAPE.values()):
            rp, nt, _ = _tiles(s)
            g = ref[...]
            tiles += [_tile_value(g, t, rp) for t in range(nt)]
        rows = sum(t.shape[0] for t in tiles)
        if rows < REP_ROWS:
            tiles.append(jnp.zeros((REP_ROWS - rows, LANES), F32))
        full = jnp.concatenate(tiles, axis=0)
        for j in range(N_DEV):
            refs[-1][j] = full[j * REP_CHUNK:(j + 1) * REP_CHUNK]

    return pl.pallas_call(body, out_shape=jax.ShapeDtypeStruct((N_DEV, REP_SLOT, LANES), F32),
                          in_specs=[VM] * len(REP_SHAPE), out_specs=VM, name='pack_rep')(
                              *[G[n].reshape(s) for n, s in REP_SHAPE.items()])


def _pack_small(blocks, order, rows, width, dtype):
    flat = jnp.concatenate([blocks[n].reshape(-1).astype(dtype) for n in order])
    return jnp.pad(flat, (0, rows * width - flat.shape[0])).reshape(rows, width)


def kernel(x, pre_norm, post_norm, rel_bias, a_w_in, a_lam_re, a_lam_im, a_log_dt, a_b_re, a_b_im, a_c_re, a_c_im, a_d, a_w_glu, a_b_glu, a_w_out, b_w_in, b_sinks, b_w_out, c_w_in, c_q_norm, c_kv_norm, c_w_uq, c_w_ukv, c_w_out, d_w_in, d_ln_g, d_ln_b, d_w_s, d_b_s, d_w_out, loss_target, m_pre_norm, m_post_norm, m_rel_bias, m_a_w_in, m_a_lam_re, m_a_lam_im, m_a_log_dt, m_a_b_re, m_a_b_im, m_a_c_re, m_a_c_im, m_a_d, m_a_w_glu, m_a_b_glu, m_a_w_out, m_b_w_in, m_b_sinks, m_b_w_out, m_c_w_in, m_c_q_norm, m_c_kv_norm, m_c_w_uq, m_c_w_ukv, m_c_w_out, m_d_w_in, m_d_ln_g, m_d_ln_b, m_d_w_s, m_d_b_s, m_d_w_out, v_pre_norm, v_post_norm, v_rel_bias, v_a_w_in, v_a_lam_re, v_a_lam_im, v_a_log_dt, v_a_b_re, v_a_b_im, v_a_c_re, v_a_c_im, v_a_d, v_a_w_glu, v_a_b_glu, v_a_w_out, v_b_w_in, v_b_sinks, v_b_w_out, v_c_w_in, v_c_q_norm, v_c_kv_norm, v_c_w_uq, v_c_w_ukv, v_c_w_out, v_d_w_in, v_d_ln_g, v_d_ln_b, v_d_w_s, v_d_b_s, v_d_w_out):
    loc = locals()
    P = {n: loc[n] for n in WEIGHTS}
    M = {n: loc['m_' + n] for n in WEIGHTS}
    V = {n: loc['v_' + n] for n in WEIGHTS}
    xs = x[0]
    L = xs.shape[0]

    blocks = {n: P[n].reshape(_block_shape(n)) for n in SHARDED}
    packed = {layer: pack_layer(layer, P) for layer in LAYER_PARAMS}
    W = {}

    def assemble_layer(layer, gathered):
        for n in LAYER_PARAMS[layer]:
            if n not in SHARDED_F32:
                W[n] = assemble(n, gathered)

    assemble_layer('a', all_gather(packed['a'], 'ag_a'))
    small = all_gather(_pack_small(blocks, SHARDED_F32, SMALL_ROWS, 128, F32), 'ag_small')
    Pl = dict(P)
    for n in SHARDED_F32:
        c = SHARDED[n][0][1]
        bc = c // N_DEV
        Pl[n] = small.reshape(N_DEV, -1)[:, SMALL_OFF[n]:SMALL_OFF[n] + bc].reshape(1, c)
    cx, cy, cc = _coords()
    core = jnp.reshape(cc, (1,)).astype(jnp.int32)
    chip = jnp.reshape(2 * cx + cy, (1,)).astype(jnp.int32)

    def pair_sums(gfull, tag):
        return rs_pair_add(gfull, rs_sibling(gfull, tag), core, tag)

    fwd = [layer_a_fwd, layer_b_fwd, layer_c_fwd, layer_d_fwd]
    bwd = [layer_a_bwd, layer_b_bwd, layer_c_bwd, layer_d_bwd]
    saved = []
    xc = xs

    def fpre(x_, g_):
        return [rms_fwd(x_, g_)], []
    (h,), _ = rowwise(fpre, [rw(xc)], [P['pre_norm'][0:1]], [(D_MODEL, BF16)], [], 256, 'pre_norm0')
    for i in range(4):
        if i == 0:
            yb, sv = fwd[i](h, W, Pl, comm=Both(AllGather(packed['b']), AllGather(packed['c'])))
            assemble_layer('b', sv['carried'][0])
            assemble_layer('c', sv['carried'][1])
        elif i == 1:
            yb, sv = fwd[i](h, W, Pl, comm=AllGather(packed['d']))
            assemble_layer('d', sv['carried'][0])
        else:
            yb, sv = fwd[i](h, W, Pl)

        sv['x'], sv['yb'] = xc, yb
        saved.append(sv)
        if i < 3:

            def fpost(x_, y_, gpost, gpre):
                xn_ = x_ + rms_fwd(y_, gpost)
                return [xn_, rms_fwd(xn_, gpre)], []
            (xc, h), _ = rowwise(fpost, [rw(xc), rw(yb)], [P['post_norm'][i:i + 1], P['pre_norm'][i + 1:i + 2]],
                                 [(D_MODEL, F32), (D_MODEL, BF16)], [], 256, f'post_pre_norm{i}')
        else:

            def floss(x_, y_, t_, gpost):
                d = x_ + rms_fwd(y_, gpost) - t_
                return [d * (1.0 / D_MODEL)], [0.5 * jnp.sum(jnp.sum(d * d, axis=-1, keepdims=True) * (1.0 / D_MODEL),
                                                             axis=0, keepdims=True)]
            (dx,), (loss_loc,) = rowwise(floss, [rw(xc), rw(yb), rw(loss_target[0])], [P['post_norm'][i:i + 1]],
                                         [(D_MODEL, F32)], [(1, 1)], 256, 'post_norm_loss')
    loss = lax.psum(loss_loc[0, 0], ("x", "y", "c"))

    G, out = {}, {}
    dpre, dpost = [None] * 4, [None] * 4

    def adam_layer(layer, part, land2):
        for n in LAYER_PARAMS[layer]:
            s = _block_shape(n)
            out[n] = adam_param(n, s, SH_OFF[n], P[n], M[n], V[n], chip, part=part, land=land2)

    def fpost_b(y_, d_, g_):
        dy, dg = rms_bwd(y_, g_, d_)
        return [dy], [dg]
    (dyb,), (dpost[3],) = rowwise(fpost_b, [rw(saved[3]['yb']), rw(dx)], [P['post_norm'][3:4]], [(D_MODEL, BF16)],
                                  [(1, D_MODEL)], 256, 'post_norm_bwd3')
    pending = None
    for i in reversed(range(4)):
        sv = saved[i]
        layer = 'abcd'[i]
        sink = GradSink(layer)
        if pending is None:
            dh, g = bwd[i](dyb, W, Pl, sv, sink=sink)
        else:
            dh, g = bwd[i](dyb, W, Pl, sv, comm=ChipExchange(pending[1]), sink=sink)
            adam_layer(pending[0], pending[1], g['carried'][0])
        g.pop('carried', None)
        G.update(g)

        if i > 0:

            def fpre_b(x_, dh_, d_, y_, gpre, gpost):
                dxl, dg = rms_bwd(x_, gpre, dh_)
                dy, dgp = rms_bwd(y_, gpost, d_ + dxl)
                return [d_ + dxl, dy], [dg, dgp]
            (dx, dyb), (dpre[i], dpost[i - 1]) = rowwise(
                fpre_b, [rw(sv['x']), rw(dh), rw(dx), rw(saved[i - 1]['yb'])],
                [P['pre_norm'][i:i + 1], P['post_norm'][i - 1:i]], [(D_MODEL, F32), (D_MODEL, BF16)],
                [(1, D_MODEL), (1, D_MODEL)], 256, f'pre_post_norm_bwd{i}')
        else:

            def fpre_b0(x_, dh_, d_, g_):
                dxl, dg = rms_bwd(x_, g_, dh_)
                return [d_ + dxl], [dg]
            (dx,), (dpre[i],) = rowwise(fpre_b0, [rw(sv['x']), rw(dh), rw(dx)], [P['pre_norm'][i:i + 1]],
                                        [(D_MODEL, F32)], [(1, D_MODEL)], 256, 'pre_norm_bwd0')

        for n in LAYER_PARAMS[layer]:
            if n in g:
                sink.add(n, g[n])
        pending = (layer, pair_sums(sink.buf, layer))
    adam_layer(pending[0], pending[1], rs_chips(pending[1], pending[0]))
    G['pre_norm'] = jnp.concatenate(dpre, axis=0)
    G['post_norm'] = jnp.concatenate(dpost, axis=0)

    part = pair_sums(pack_rep(G), 'rep')
    land2 = rs_chips(part, 'rep')
    grep = all_gather(rs_rep_sum(part, land2, chip), 'ag_rep')[:, :REP_CHUNK].reshape(REP_ROWS, LANES)
    small_names = [n for n, s in REP_SHAPE.items() if s[0] <= 64]
    out.update(adam_small(small_names, grep, P, M, V))
    for n, s in REP_SHAPE.items():
        if n not in small_names:
            out[n] = adam_param(n, s, REP_OFF[n], P[n], M[n], V[n], chip, grep=grep)
    res = [loss, dx[None]]
    for kind in range(4):
        res += [out[n][kind].reshape(P[n].shape) for n in WEIGHTS]
    return tuple(res)
```

```python
import functools
import math

import numpy as np
import jax
import jax.numpy as jnp
from jax import lax
from jax.experimental import pallas as pl
from jax.experimental.pallas import tpu as pltpu

F32 = jnp.float32
BF16 = jnp.bfloat16
MESH = pl.DeviceIdType.MESH
ANY = pl.BlockSpec(memory_space=pl.ANY)

N_DEV = 8
D_MODEL = 1024
EPS = 1e-6
NEG_INF = -1e30
SSM_G, SSM_P, SSM_H = 64, 64, 16
SSM_T = 256
SSM_TS = 8
SSM_WC = 512
HEAD_DIM = 64
SWA_HEADS, SWA_KV = 16, 2
WINDOW = 128
REL_BUCKETS, REL_MAX_DIST = 32, 128
MLA_HEADS, MLA_NOPE, MLA_ROPE, MLA_V = 16, 64, 32, 64
MLA_Q_RANK, MLA_KV_RANK = 768, 256
ROPE_BASE = 10000.0
SGU_G, SGU_C, SGU_T = 16, 64, 128
ADAM_LR, ADAM_B1, ADAM_B2, ADAM_EPS, ADAM_WD, ADAM_STEP = 0.001, 0.9, 0.999, 1e-08, 0.01, 10

WEIGHTS = ['pre_norm', 'post_norm', 'rel_bias', 'a_w_in', 'a_lam_re', 'a_lam_im', 'a_log_dt', 'a_b_re', 'a_b_im',
           'a_c_re', 'a_c_im', 'a_d', 'a_w_glu', 'a_b_glu', 'a_w_out', 'b_w_in', 'b_sinks', 'b_w_out', 'c_w_in',
           'c_q_norm', 'c_kv_norm', 'c_w_uq', 'c_w_ukv', 'c_w_out', 'd_w_in', 'd_ln_g', 'd_ln_b', 'd_w_s', 'd_b_s',
           'd_w_out']
SHARDED = {'a_w_in': ((1024, 2048), 1), 'a_w_glu': ((1024, 1024), 0), 'a_w_out': ((1024, 1024), 0),
           'b_w_in': ((1024, 2304), 1), 'b_w_out': ((1024, 1024), 0), 'c_w_in': ((1024, 2080), 1),
           'c_q_norm': ((1, 768), 1), 'c_kv_norm': ((1, 256), 1), 'c_w_uq': ((768, 1536), 1),
           'c_w_ukv': ((256, 2048), 1), 'c_w_out': ((1024, 1024), 0), 'd_w_in': ((1024, 3072), 1),
           'd_ln_g': ((1, 1024), 1), 'd_ln_b': ((1, 1024), 1), 'd_w_out': ((1024, 1024), 0)}
SHARDED_F32 = ['c_q_norm', 'c_kv_norm', 'd_ln_g', 'd_ln_b']
REPLICATED = [n for n in WEIGHTS if n not in SHARDED]


def _cdiv(a, b):
    return -(-a // b)


def _block_shape(name):
    (r, c), ax = SHARDED[name]
    return (r // N_DEV, c) if ax == 0 else (r, c // N_DEV)


LANES = 128
LAYER_PARAMS = {'a': ['a_w_in', 'a_w_glu', 'a_w_out'], 'b': ['b_w_in', 'b_w_out'],
                'c': ['c_w_in', 'c_w_uq', 'c_w_ukv', 'c_w_out', 'c_q_norm', 'c_kv_norm'],
                'd': ['d_w_in', 'd_w_out', 'd_ln_g', 'd_ln_b']}


def _tiles(shape):
    r, c = shape
    rp = max(r, 16)
    rb = 512 if rp % 512 == 0 else 256 if rp % 256 == 0 else rp
    return rp, _cdiv(c, LANES), rb


SH_OFF, LAYER_ROWS = {}, {}
for _l, _names in LAYER_PARAMS.items():
    _o = 0
    for _n in _names:
        _rp, _nt, _rb = _tiles(_block_shape(_n))
        assert _o % _rb == 0
        SH_OFF[_n] = _o
        _o += _rp * _nt
    assert _o % 16 == 0
    LAYER_ROWS[_l] = _o

REP_SHAPE = {'a_b_re': (4096, 16), 'a_b_im': (4096, 16), 'd_w_s': (2048, 128), 'a_c_re': (1024, 64),
             'a_c_im': (1024, 64), 'pre_norm': (4, 1024), 'post_norm': (4, 1024), 'a_lam_re': (64, 64),
             'a_lam_im': (64, 64), 'a_d': (1, 1024), 'a_b_glu': (1, 1024), 'rel_bias': (32, 16), 'd_b_s': (16, 128),
             'a_log_dt': (1, 64), 'b_sinks': (1, 16)}
REP_OFF = {}
_o = 0
for _n, _s in REP_SHAPE.items():
    _rp, _nt, _rb = _tiles(_s)
    assert _o % _rb == 0
    REP_OFF[_n] = _o
    _o += _rp * _nt
REP_ROWS = _cdiv(_o, 16 * N_DEV) * 16 * N_DEV
REP_CHUNK = REP_ROWS // N_DEV
REP_SLOT = REP_CHUNK

PERM = {'a_w_in': [(0, 2048)], 'd_w_in': [(0, 3072)], 'b_w_in': [(1280, 1024), (0, 1280)],
        'c_w_in': [(1056, 1024), (0, 1056), ('z', 96)],
        'c_w_uq': sum([[(2 * hp * 96, 64), ((2 * hp + 1) * 96, 64), (2 * hp * 96 + 64, 32), ((2 * hp + 1) * 96 + 64, 32),
                        ('z', 64)] for hp in range(8)], []),
        'c_w_ukv': sum([[(2 * hp * 128, 64), ((2 * hp + 1) * 128, 64), (2 * hp * 128 + 64, 64),
                         ((2 * hp + 1) * 128 + 64, 64)] for hp in range(8)], [])}


def perm_index(name):
    return np.concatenate([np.full(p[1], -1) if p[0] == 'z' else np.arange(p[0], p[0] + p[1]) for p in PERM[name]])


SMALL_OFF = {}
_o = 0
for _n in SHARDED_F32:
    SMALL_OFF[_n] = _o
    _o += int(np.prod(_block_shape(_n)))
SMALL_ROWS = _cdiv(_o, 128 * 8) * 8


def _pick(n, cands):
    for c in cands:
        if n % c == 0:
            return c
    return n


def mm(a, b, mode, name, out_dtype=F32):
    if mode == 'nn':
        (M, K), (K2, N) = a.shape, b.shape
    elif mode == 'nt':
        (M, K), (N, K2) = a.shape, b.shape
    else:
        (K, M), (K2, N) = a.shape, b.shape
    assert K == K2, (name, a.shape, b.shape)
    tm = _pick(M, (1024, 768, 512, 256, 128))
    tn = _pick(N, (512, 384, 256))
    dims = {'nn': ((1,), (0,)), 'nt': ((1,), (1,)), 'tn': ((0,), (0,))}[mode]

    def body(a_ref, b_ref, o_ref):
        o_ref[...] = lax.dot_general(a_ref[...].astype(BF16), b_ref[...].astype(BF16), (dims, ((), ())),
                                     preferred_element_type=F32).astype(out_dtype)

    a_spec = pl.BlockSpec((K, tm), lambda i, j: (0, i)) if mode == 'tn' else pl.BlockSpec((tm, K), lambda i, j: (i, 0))
    b_spec = pl.BlockSpec((tn, K), lambda i, j: (j, 0)) if mode == 'nt' else pl.BlockSpec((K, tn), lambda i, j: (0, j))
    return pl.pallas_call(
        body, grid=(M // tm, N // tn), in_specs=[a_spec, b_spec],
        out_specs=pl.BlockSpec((tm, tn), lambda i, j: (i, j)), out_shape=jax.ShapeDtypeStruct((M, N), out_dtype),
        compiler_params=pltpu.CompilerParams(dimension_semantics=("parallel", "parallel")), name=name)(a, b)


def rw(arr, width=None, cb=0):
    return (arr, arr.shape[1] if width is None else width, cb)


def rowwise(fn, rows, consts, outs, accs, tl, name, n_steps=None):
    if n_steps is None:
        n_steps = [r[0].shape[0] for r in rows if not isinstance(r[1], pl.BlockSpec)][0] // tl
    L = n_steps * tl
    nr, nc, no, na = len(rows), len(consts), len(outs), len(accs)
    in_specs, args = [], []
    for r in rows:
        if isinstance(r[1], pl.BlockSpec):
            in_specs.append(r[1])
        else:
            in_specs.append(pl.BlockSpec((tl, r[1]), functools.partial(lambda i, cb: (i, cb), cb=r[2])))
        args.append(r[0])
    for c in consts:
        in_specs.append(pl.BlockSpec(c.shape, functools.partial(lambda i, nd: (0,) * nd, nd=c.ndim)))
        args.append(c)
    out_specs = [pl.BlockSpec((tl, w), lambda i: (i, 0)) for w, _ in outs]
    out_shape = [jax.ShapeDtypeStruct((L, w), dt) for w, dt in outs]
    for s in accs:
        out_specs.append(pl.BlockSpec(s, functools.partial(lambda i, nd: (0,) * nd, nd=len(s))))
        out_shape.append(jax.ShapeDtypeStruct(s, F32))

    def body(*refs):
        ins = [r[...] for r in refs[:nr + nc]]
        o_refs = refs[nr + nc:nr + nc + no]
        a_refs = refs[nr + nc + no:]
        o_vals, a_vals = fn(*ins)
        for ref, val in zip(o_refs, o_vals):
            ref[...] = val.astype(ref.dtype)
        if na:
            @pl.when(pl.program_id(0) == 0)
            def _():
                for ref in a_refs:
                    ref[...] = jnp.zeros_like(ref)
            for ref, val in zip(a_refs, a_vals):
                ref[...] += val

    res = pl.pallas_call(
        body, grid=(n_steps,), in_specs=in_specs, out_specs=out_specs, out_shape=out_shape,
        compiler_params=pltpu.CompilerParams(dimension_semantics=("arbitrary",)), name=name)(*args)
    return res[:no], res[no:]


def carried(body, comm, *, grid, in_specs, out_specs, out_shape, name, semantics, scratch_shapes=()):
    single = not isinstance(out_shape, (list, tuple))
    o_specs = [out_specs] if single else list(out_specs)
    o_shape = [out_shape] if single else list(out_shape)
    if comm is None:
        call = pl.pallas_call(body, grid=grid, in_specs=in_specs, out_specs=out_specs, out_shape=out_shape,
                              scratch_shapes=list(scratch_shapes),
                              compiler_params=pltpu.CompilerParams(dimension_semantics=semantics), name=name)
        return lambda *args: (call(*args), None)
    n_in, n_out, n_sc = len(in_specs), len(o_specs), len(scratch_shapes)
    ci, co = len(comm.ins), len(comm.outs)
    n_steps = int(np.prod(grid))
    hooks = comm.hooks(n_steps)

    def wrapped(*refs):
        ins, cins = refs[:n_in], refs[n_in:n_in + ci]
        outs, couts = refs[n_in + ci:n_in + ci + n_out], refs[n_in + ci + n_out:n_in + ci + n_out + co]
        sc, csc = refs[n_in + ci + n_out + co:n_in + ci + n_out + co + n_sc], refs[n_in + ci + n_out + co + n_sc:]
        step = pl.program_id(0)
        for ax in range(1, len(grid)):
            step = step * grid[ax] + pl.program_id(ax)
        for at, fn, after in hooks:
            if not after:
                pl.when(step == at)(functools.partial(fn, cins, couts, csc))
        body(*ins, *outs, *sc)
        for at, fn, after in hooks:
            if after:
                pl.when(step == at)(functools.partial(fn, cins, couts, csc))

    call = pl.pallas_call(wrapped, grid=grid, in_specs=list(in_specs) + [ANY] * ci, out_specs=o_specs + [ANY] * co,
                          out_shape=o_shape + list(comm.outs), scratch_shapes=list(scratch_shapes) + list(comm.scratch),
                          compiler_params=pltpu.CompilerParams(dimension_semantics=("arbitrary",) * len(grid)), name=name)

    def run(*args):
        res = call(*args, *comm.ins)
        return (res[0] if single else res[:n_out]), res[n_out:]
    return run


_K0 = math.sqrt(2.0 / math.pi)
_K1 = 0.044715


def gelu(x):
    return x * (0.5 * (1.0 + jnp.tanh(_K0 * (x + _K1 * (x * x * x)))))


def gelu_grad(x):
    t = jnp.tanh(_K0 * (x + _K1 * (x * x * x)))
    return 0.5 * (1.0 + t) + 0.5 * x * (1.0 - t * t) * (_K0 * (1.0 + 3.0 * _K1 * x * x))


def sigmoid(x):
    return 1.0 / (1.0 + jnp.exp(-x))


def silu(z):
    return z * sigmoid(z)


def silu_grad(z):
    s = sigmoid(z)
    return s * (1.0 + z * (1.0 - s))


def rms_fwd(x, g):
    r = lax.rsqrt(jnp.mean(x * x, axis=-1, keepdims=True) + EPS)
    return x * r * g


def rms_bwd(x, g, dy):
    r = lax.rsqrt(jnp.mean(x * x, axis=-1, keepdims=True) + EPS)
    xh = x * r
    dg = jnp.sum(dy * xh, axis=0, keepdims=True)
    dxh = dy * g
    dx = r * (dxh - xh * jnp.mean(dxh * xh, axis=-1, keepdims=True))
    return dx, dg


def _scan_chunk(a_r, a_i, pr_ref, pi_ref, cr, ci, T, reverse):
    ts = min(SSM_TS, T)
    sgn = -1.0 if reverse else 1.0
    row = lax.broadcasted_iota(jnp.int32, (ts, a_r.shape[1]), 0)
    pw = (lambda e: T - e) if reverse else (lambda e: e - 1)
    if reverse:
        wr_c, wi_c = pr_ref[T - ts:T, :], sgn * pi_ref[T - ts:T, :]
    else:
        wr_c, wi_c = pr_ref[0:ts, :], sgn * pi_ref[0:ts, :]
    c_r, c_i = cr[...], ci[...]
    outs = []
    subs = range(T // ts)
    for sub in (reversed(subs) if reverse else subs):
        v_r, v_i = a_r[sub * ts:(sub + 1) * ts], a_i[sub * ts:(sub + 1) * ts]
        d = 1
        while d < ts:
            wr = pr_ref[pw(d):pw(d) + 1, :]
            wi = sgn * pi_ref[pw(d):pw(d) + 1, :]
            if reverse:
                yr, yi, keep = pltpu.roll(v_r, ts - d, 0), pltpu.roll(v_i, ts - d, 0), row < ts - d
            else:
                yr, yi, keep = pltpu.roll(v_r, d, 0), pltpu.roll(v_i, d, 0), row >= d
            v_r, v_i = (v_r + jnp.where(keep, wr * yr - wi * yi, 0.0), v_i + jnp.where(keep, wr * yi + wi * yr, 0.0))
            d *= 2
        v_r, v_i = v_r + (wr_c * c_r - wi_c * c_i), v_i + (wr_c * c_i + wi_c * c_r)
        k = 0 if reverse else ts - 1
        c_r, c_i = v_r[k:k + 1, :], v_i[k:k + 1, :]
        outs.append((v_r, v_i))
    if reverse:
        outs = outs[::-1]
    cr[...] = c_r
    ci[...] = c_i
    return jnp.concatenate([o[0] for o in outs], axis=0), jnp.concatenate([o[1] for o in outs], axis=0)


_NT = (((1,), (1,)), ((), ()))
_TN = (((0,), (0,)), ((), ()))


def s5_fwd(proj, d_skip, Bre, Bim, Cre, Cim, pr, pi, comm=None):
    L = proj.shape[0]
    T, WC = min(SSM_T, L), SSM_WC
    nT = L // T

    def body(u_ref, d_ref, bre_ref, bim_ref, cre_ref, cim_ref, pr_ref, pi_ref, y_ref, yg_ref, sr_ref, si_ref, cr, ci):
        @pl.when(pl.program_id(1) == 0)
        def _():
            cr[...] = jnp.zeros_like(cr)
            ci[...] = jnp.zeros_like(ci)

        u = u_ref[...]
        ub = u.astype(BF16)
        a_r = jnp.dot(ub, bre_ref[0].astype(BF16), preferred_element_type=F32)
        a_i = jnp.dot(ub, bim_ref[0].astype(BF16), preferred_element_type=F32)
        a_r, a_i = _scan_chunk(a_r, a_i, pr_ref, pi_ref, cr, ci, T, False)
        sr_ref[...] = a_r
        si_ref[...] = a_i
        y = (jnp.dot(a_r.astype(BF16), cre_ref[0].astype(BF16), preferred_element_type=F32)
             + jnp.dot(a_i.astype(BF16), cim_ref[0].astype(BF16), preferred_element_type=F32) + d_ref[...] * u)
        y_ref[...] = y
        yg_ref[...] = gelu(y)

    uspec = pl.BlockSpec((T, 128), lambda k, i: (i, k))
    sspec = pl.BlockSpec((T, WC), lambda k, i: (i, k))
    return carried(
        body, comm, grid=(8, nT),
        in_specs=[uspec, pl.BlockSpec((1, 128), lambda k, i: (0, k)),
                  pl.BlockSpec((1, 128, WC), lambda k, i: (k, 0, 0)), pl.BlockSpec((1, 128, WC), lambda k, i: (k, 0, 0)),
                  pl.BlockSpec((1, WC, 128), lambda k, i: (k, 0, 0)), pl.BlockSpec((1, WC, 128), lambda k, i: (k, 0, 0)),
                  pl.BlockSpec((T, WC), lambda k, i: (0, k)), pl.BlockSpec((T, WC), lambda k, i: (0, k))],
        out_specs=[uspec, uspec, sspec, sspec],
        out_shape=[jax.ShapeDtypeStruct((L, 1024), F32)] * 2 + [jax.ShapeDtypeStruct((L, 8 * WC), F32)] * 2,
        scratch_shapes=[pltpu.VMEM((1, WC), F32), pltpu.VMEM((1, WC), F32)],
        semantics=("parallel", "arbitrary"), name='a_ssm')(proj, d_skip, Bre, Bim, Cre, Cim, pr, pi)


def s5_bwd(proj, dyg1, dyg2, y, d_skip, s_re, s_im, Bre, Bim, Cre, Cim, prr, pir, comm=None):
    L = proj.shape[0]
    T, WC = min(SSM_T, L), SSM_WC
    nT = L // T

    def body(u_ref, g1_ref, g2_ref, y_ref, d_ref, sr_ref, si_ref, spr_ref, spi_ref, bre_ref, bim_ref, cre_ref, cim_ref,
             pr_ref, pi_ref, du_ref, dd_ref, dbre_ref, dbim_ref, dcre_ref, dcim_ref, dar_ref, dai_ref, cr, ci):
        i = pl.program_id(1)

        @pl.when(i == 0)
        def _():
            for ref in (cr, ci, dd_ref, dbre_ref, dbim_ref, dcre_ref, dcim_ref, dar_ref, dai_ref):
                ref[...] = jnp.zeros_like(ref)

        u = u_ref[...]
        dy = (g1_ref[...] + g2_ref[...]) * gelu_grad(y_ref[...])
        dd_ref[...] += jnp.sum(dy * u, axis=0, keepdims=True)
        dyb, ub = dy.astype(BF16), u.astype(BF16)
        bre, bim, cre, cim = (r[0].astype(BF16) for r in (bre_ref, bim_ref, cre_ref, cim_ref))
        g_r = lax.dot_general(dyb, cre, _NT, preferred_element_type=F32)
        g_i = lax.dot_general(dyb, cim, _NT, preferred_element_type=F32)
        g_r, g_i = _scan_chunk(g_r, g_i, pr_ref, pi_ref, cr, ci, T, True)
        s_r, s_i = sr_ref[...], si_ref[...]
        row = lax.broadcasted_iota(jnp.int32, (T, WC), 0)
        first = (nT - 1 - i) == 0
        sp_r = jnp.where(row == 0, jnp.where(first, 0.0, spr_ref[7:8, :]), pltpu.roll(s_r, 1, 0))
        sp_i = jnp.where(row == 0, jnp.where(first, 0.0, spi_ref[7:8, :]), pltpu.roll(s_i, 1, 0))
        dar_ref[...] += jnp.sum(g_r * sp_r + g_i * sp_i, axis=0, keepdims=True)
        dai_ref[...] += jnp.sum(g_i * sp_r - g_r * sp_i, axis=0, keepdims=True)
        grb, gib = g_r.astype(BF16), g_i.astype(BF16)
        dcre_ref[0] += lax.dot_general(s_r.astype(BF16), dyb, _TN, preferred_element_type=F32)
        dcim_ref[0] += lax.dot_general(s_i.astype(BF16), dyb, _TN, preferred_element_type=F32)
        dbre_ref[0] += lax.dot_general(ub, grb, _TN, preferred_element_type=F32)
        dbim_ref[0] += lax.dot_general(ub, gib, _TN, preferred_element_type=F32)
        du_ref[...] = (dy * d_ref[...] + lax.dot_general(grb, bre, _NT, preferred_element_type=F32)
                       + lax.dot_general(gib, bim, _NT, preferred_element_type=F32))

    uspec = pl.BlockSpec((T, 128), lambda k, i: (nT - 1 - i, k))
    sspec = pl.BlockSpec((T, WC), lambda k, i: (nT - 1 - i, k))
    pspec = pl.BlockSpec((8, WC), lambda k, i: (jnp.maximum((nT - 1 - i) * (T // 8) - 1, 0), k))
    tab = pl.BlockSpec((T, WC), lambda k, i: (0, k))
    bspec = pl.BlockSpec((1, 128, WC), lambda k, i: (k, 0, 0))
    cspec = pl.BlockSpec((1, WC, 128), lambda k, i: (k, 0, 0))
    return carried(
        body, comm, grid=(8, nT),
        in_specs=[uspec, uspec, uspec, uspec, pl.BlockSpec((1, 128), lambda k, i: (0, k)), sspec, sspec, pspec, pspec,
                  bspec, bspec, cspec, cspec, tab, tab],
        out_specs=[uspec, pl.BlockSpec((1, 128), lambda k, i: (0, k)), bspec, bspec, cspec, cspec,
                   pl.BlockSpec((1, WC), lambda k, i: (0, k)), pl.BlockSpec((1, WC), lambda k, i: (0, k))],
        out_shape=[jax.ShapeDtypeStruct((L, 1024), F32), jax.ShapeDtypeStruct((1, 1024), F32),
                   jax.ShapeDtypeStruct((8, 128, WC), F32), jax.ShapeDtypeStruct((8, 128, WC), F32),
                   jax.ShapeDtypeStruct((8, WC, 128), F32), jax.ShapeDtypeStruct((8, WC, 128), F32),
                   jax.ShapeDtypeStruct((1, 8 * WC), F32), jax.ShapeDtypeStruct((1, 8 * WC), F32)],
        scratch_shapes=[pltpu.VMEM((1, WC), F32), pltpu.VMEM((1, WC), F32)],
        semantics=("parallel", "arbitrary"), name='a_ssm_bwd')(
            proj, dyg1, dyg2, y, d_skip, s_re, s_im, s_re, s_im, Bre, Bim, Cre, Cim, prr, pir)


def s5_discretize(lam_re, lam_im, log_dt, b_re, b_im):
    dt = jnp.exp(log_dt)[:, None]
    mag = jnp.exp(lam_re * dt)
    ab_re = mag * jnp.cos(lam_im * dt)
    ab_im = mag * jnp.sin(lam_im * dt)
    den = lam_re * lam_re + lam_im * lam_im
    nr = ab_re - 1.0
    f_re = (nr * lam_re + ab_im * lam_im) / den
    f_im = (ab_im * lam_re - nr * lam_im) / den
    bb_re = f_re[..., None] * b_re - f_im[..., None] * b_im
    bb_im = f_re[..., None] * b_im + f_im[..., None] * b_re
    return ab_re, ab_im, bb_re, bb_im


_EYE8 = np.eye(8, dtype=np.float32)


def _b_tiles(bb):
    t = bb.transpose(0, 2, 1).reshape(8, 8, SSM_H, SSM_P)
    return jnp.einsum('kghp,gG->kghGp', t, _EYE8).reshape(8, 8 * SSM_H, 8 * SSM_P)


def _b_untile(d):
    t = jnp.einsum('kghGp,gG->kghp', d.reshape(8, 8, SSM_H, 8, SSM_P), _EYE8)
    return t.reshape(SSM_G, SSM_H, SSM_P).transpose(0, 2, 1)


def _c_tiles(c):
    t = c.transpose(0, 2, 1).reshape(8, 8, SSM_P, SSM_H)
    return jnp.einsum('kgph,gG->kgpGh', t, _EYE8).reshape(8, 8 * SSM_P, 8 * SSM_H)


def _c_untile(d):
    t = jnp.einsum('kgpGh,gG->kgph', d.reshape(8, 8, SSM_P, 8, SSM_H), _EYE8)
    return t.reshape(SSM_G, SSM_P, SSM_H).transpose(0, 2, 1)


def s5_powers(ar, ai, T):
    W = ar.shape[1]

    def body(ar_ref, ai_ref, fr_ref, fi_ref, rr_ref, ri_ref):
        fr_ref[0:1, :] = ar_ref[...]
        fi_ref[0:1, :] = ai_ref[...]
        rr_ref[T - 1:T, :] = ar_ref[...]
        ri_ref[T - 1:T, :] = ai_ref[...]
        n = 1
        while n < T:
            cr, ci = fr_ref[0:n, :], fi_ref[0:n, :]
            lr, li = fr_ref[n - 1:n, :], fi_ref[n - 1:n, :]
            fr_ref[n:2 * n, :] = cr * lr - ci * li
            fi_ref[n:2 * n, :] = cr * li + ci * lr
            cr, ci = rr_ref[T - n:T, :], ri_ref[T - n:T, :]
            rr_ref[T - 2 * n:T - n, :] = cr * lr - ci * li
            ri_ref[T - 2 * n:T - n, :] = cr * li + ci * lr
            n *= 2

    spec = pl.BlockSpec((T, SSM_WC), lambda j: (0, j))
    aspec = pl.BlockSpec((1, SSM_WC), lambda j: (0, j))
    return pl.pallas_call(
        body, grid=(W // SSM_WC,), in_specs=[aspec, aspec], out_specs=[spec] * 4,
        out_shape=[jax.ShapeDtypeStruct((T, W), F32)] * 4,
        compiler_params=pltpu.CompilerParams(dimension_semantics=("parallel",)), name='a_powers')(ar, ai)


def layer_a_fwd(h, w, p, comm=None):
    L = h.shape[0]
    proj = mm(h, w['a_w_in'], 'nn', 'a_proj')
    disc = lambda *a: s5_discretize(*a)
    (ab_re, ab_im, bb_re, bb_im), disc_vjp = jax.vjp(disc, p['a_lam_re'][0], p['a_lam_im'][0], p['a_log_dt'][0],
                                                     p['a_b_re'][0], p['a_b_im'][0])
    Bre, Bim = _b_tiles(bb_re), _b_tiles(bb_im)
    Cre, Cim = _c_tiles(p['a_c_re'][0]), -_c_tiles(p['a_c_im'][0])
    T = min(SSM_T, L)
    pr, pi, prr, pir = s5_powers(ab_re.reshape(1, -1), ab_im.reshape(1, -1), T)
    (y, yg, s_re, s_im), carried_out = s5_fwd(proj, p['a_d'], Bre, Bim, Cre, Cim, pr, pi, comm=comm)
    gl = mm(yg, w['a_w_glu'], 'nn', 'a_glu')

    def f2(yg_, gl_, z, bg):
        return [yg_ * sigmoid(gl_ + bg) * silu(z)], []
    (po,), _ = rowwise(f2, [rw(yg), rw(gl), rw(proj, 1024, 1)], [p['a_b_glu']], [(1024, BF16)], [], 256, 'a_gate')
    yb = mm(po, w['a_w_out'], 'nn', 'a_out')
    saved = dict(carried=carried_out, h=h, proj=proj, disc_vjp=disc_vjp, Bre=Bre, Bim=Bim, Cre=Cre, Cim=Cim, prr=prr, pir=pir, s_re=s_re,
                 s_im=s_im, y=y, yg=yg, gl=gl, po=po)
    return yb, saved


def _dw(g, sink, name, a, b, mm_name):
    if sink is None:
        g[name] = mm(a, b, 'tn', mm_name)
    else:
        sink.put(name, a, b, mm_name)


def layer_a_bwd(dyb, w, p, sv, comm=None, sink=None):
    g = {}
    dpo = mm(dyb, w['a_w_out'], 'nt', 'a_dpo')
    _dw(g, sink, 'a_w_out', sv['po'], dyb, 'a_dwout')
    proj = sv['proj']

    def f1(dpo_, yg, gl, z, bg):
        sg = sigmoid(gl + bg)
        sz = silu(z)
        dm = dpo_ * sz
        dz = dpo_ * (yg * sg) * silu_grad(z)
        dgl = dm * yg * sg * (1.0 - sg)
        return [dz, dm * sg, dgl], [jnp.sum(dgl, axis=0, keepdims=True)]
    (dz, dyg1, dgl), (db_glu,) = rowwise(f1, [rw(dpo), rw(sv['yg']), rw(sv['gl']), rw(proj, 1024, 1)], [p['a_b_glu']],
                                          [(1024, F32), (1024, F32), (1024, BF16)], [(1, 1024)], 256, 'a_gate_bwd')
    g['a_b_glu'] = db_glu
    _dw(g, sink, 'a_w_glu', sv['yg'], dgl, 'a_dwglu')
    dyg2 = mm(dgl, w['a_w_glu'], 'nt', 'a_dyg2')

    (du, dd, dBre, dBim, dCre, dCim, da_re, da_im), g['carried'] = s5_bwd(
        proj, dyg1, dyg2, sv['y'], p['a_d'], sv['s_re'], sv['s_im'], sv['Bre'], sv['Bim'], sv['Cre'], sv['Cim'],
        sv['prr'], sv['pir'], comm=comm)
    g['a_d'] = dd
    dCim = -dCim

    def f3(du_, dz_):
        return [jnp.concatenate([du_, dz_], axis=1)], []
    (dproj,), _ = rowwise(f3, [rw(du), rw(dz)], [], [(2048, BF16)], [], 256, 'a_dproj')
    dlr, dli, dldt, dbr, dbi = sv['disc_vjp']((da_re.reshape(SSM_G, SSM_P), da_im.reshape(SSM_G, SSM_P),
                                               _b_untile(dBre), _b_untile(dBim)))
    g['a_lam_re'], g['a_lam_im'], g['a_log_dt'] = dlr[None], dli[None], dldt[None]
    g['a_b_re'], g['a_b_im'] = dbr[None], dbi[None]
    g['a_c_re'], g['a_c_im'] = _c_untile(dCre)[None], _c_untile(dCim)[None]
    _dw(g, sink, 'a_w_in', sv['h'], dproj, 'a_dwin')
    dh = mm(dproj, w['a_w_in'], 'nt', 'a_dh')
    return dh, g


def _t5_bucket_np():
    qi = np.arange(WINDOW)[:, None]
    kj = np.arange(2 * WINDOW)[None, :]
    dist = np.maximum(qi + WINDOW - kj, 0)
    max_exact = REL_BUCKETS // 2
    dist_f = np.maximum(dist, 1).astype(np.float32)
    large = max_exact + (np.log(dist_f / np.float32(max_exact)) / np.float32(math.log(REL_MAX_DIST / max_exact))
                         * np.float32(REL_BUCKETS - max_exact)).astype(np.int32)
    large = np.minimum(large, REL_BUCKETS - 1)
    return np.where(dist < max_exact, dist, large).astype(np.int32)


SWA_GRP = SWA_HEADS // SWA_KV


def _swa_kv(kvp, kvc, kvh):
    kb = jnp.concatenate([kvp[:, kvh * 64:(kvh + 1) * 64], kvc[:, kvh * 64:(kvh + 1) * 64]], 0).astype(BF16)
    vb = jnp.concatenate([kvp[:, 128 + kvh * 64:128 + (kvh + 1) * 64], kvc[:, 128 + kvh * 64:128 + (kvh + 1) * 64]],
                         0).astype(BF16)
    return kb, vb


def _swa_stack(x, kvh):
    return jnp.concatenate([x[:, (kvh * SWA_GRP + g) * 64:(kvh * SWA_GRP + g + 1) * 64] for g in range(SWA_GRP)],
                           axis=0).astype(BF16)


def _swa_group(bias_ref, kvh):
    return bias_ref[kvh * SWA_GRP:(kvh + 1) * SWA_GRP].reshape(SWA_GRP * WINDOW, 2 * WINDOW)


def _swa_sinks(sink_ref, kvh):
    return jnp.concatenate([jnp.broadcast_to(sink_ref[0:1, kvh * SWA_GRP + g:kvh * SWA_GRP + g + 1], (WINDOW, 1))
                            for g in range(SWA_GRP)], axis=0)


def _swa_probs(q, kb, bias_h, sink, valid):
    s = lax.dot_general(q, kb, (((1,), (1,)), ((), ())), preferred_element_type=F32) * (HEAD_DIM ** -0.5)
    s = jnp.where(valid, s + bias_h, NEG_INF)
    m = jnp.maximum(jnp.max(s, axis=-1, keepdims=True), sink)
    e = jnp.exp(s - m)
    es = jnp.exp(sink - m)
    den = jnp.sum(e, axis=-1, keepdims=True) + es
    return e / den, es / den


def _swa_valid(n):
    qi = lax.broadcasted_iota(jnp.int32, (SWA_GRP * WINDOW, 2 * WINDOW), 0) & (WINDOW - 1)
    kj = lax.broadcasted_iota(jnp.int32, (SWA_GRP * WINDOW, 2 * WINDOW), 1)
    dist = qi + WINDOW - kj
    return (dist >= 0) & (dist < WINDOW) & ((kj >= WINDOW) | (n > 0))


def swa_fwd(proj, bias, sinks, comm=None):
    L = proj.shape[0]

    def body(z_ref, q_ref, kvc_ref, kvp_ref, bias_ref, sink_ref, o_ref, po_ref):
        n = pl.program_id(0)
        valid = _swa_valid(n)
        q, kvc, kvp = q_ref[...], kvc_ref[...], kvp_ref[...]
        outs = []
        for kvh in range(SWA_KV):
            kb, vb = _swa_kv(kvp, kvc, kvh)
            p, _ = _swa_probs(_swa_stack(q, kvh), kb, _swa_group(bias_ref, kvh), _swa_sinks(sink_ref, kvh), valid)
            o8 = jnp.dot(p.astype(BF16), vb, preferred_element_type=F32)
            outs += [o8[g * WINDOW:(g + 1) * WINDOW] for g in range(SWA_GRP)]
        o = jnp.concatenate(outs, axis=1)
        o_ref[...] = o
        po_ref[...] = (o * silu(z_ref[...])).astype(po_ref.dtype)

    return carried(
        body, comm, grid=(L // WINDOW,),
        in_specs=[pl.BlockSpec((WINDOW, 1024), lambda n: (n, 0)), pl.BlockSpec((WINDOW, 1024), lambda n: (n, 1)),
                  pl.BlockSpec((WINDOW, 256), lambda n: (n, 8)),
                  pl.BlockSpec((WINDOW, 256), lambda n: (jnp.maximum(n - 1, 0), 8)),
                  pl.BlockSpec((SWA_HEADS, WINDOW, 2 * WINDOW), lambda n: (0, 0, 0)),
                  pl.BlockSpec((1, SWA_HEADS), lambda n: (0, 0))],
        out_specs=[pl.BlockSpec((WINDOW, 1024), lambda n: (n, 0))] * 2,
        out_shape=[jax.ShapeDtypeStruct((L, 1024), F32), jax.ShapeDtypeStruct((L, 1024), BF16)],
        semantics=("parallel",), name='b_attn')(proj, proj, proj, proj, bias, sinks)


def swa_bwd(proj, do, bias, sinks, comm=None):
    L = proj.shape[0]

    def body(q_ref, kvc_ref, kvp_ref, do_ref, bias_ref, sink_ref, dq_ref, dkv_ref, dbias_ref, dsink_ref):
        n = pl.program_id(0)

        @pl.when(n == 0)
        def _():
            dkv_ref[...] = jnp.zeros_like(dkv_ref)
            dbias_ref[...] = jnp.zeros_like(dbias_ref)
            dsink_ref[...] = jnp.zeros_like(dsink_ref)

        valid = _swa_valid(n)
        q, kvc, kvp, do_ = q_ref[...], kvc_ref[...], kvp_ref[...], do_ref[...]
        dqs, dks, dvs, dsk = [], [], [], []
        for kvh in range(SWA_KV):
            kb, vb = _swa_kv(kvp, kvc, kvh)
            q8, do8 = _swa_stack(q, kvh), _swa_stack(do_, kvh)
            p, ps = _swa_probs(q8, kb, _swa_group(bias_ref, kvh), _swa_sinks(sink_ref, kvh), valid)
            dp = lax.dot_general(do8, vb, (((1,), (1,)), ((), ())), preferred_element_type=F32)
            delta = jnp.sum(p * dp, axis=-1, keepdims=True)
            ds = p * (dp - delta)
            col = -ps * delta
            dsk += [jnp.sum(col[g * WINDOW:(g + 1) * WINDOW], axis=0, keepdims=True) for g in range(SWA_GRP)]
            dbias_ref[kvh * SWA_GRP:(kvh + 1) * SWA_GRP] += ds.reshape(SWA_GRP, WINDOW, 2 * WINDOW)
            dsb = (ds * (HEAD_DIM ** -0.5)).astype(BF16)
            dq8 = jnp.dot(dsb, kb, preferred_element_type=F32)
            dqs += [dq8[g * WINDOW:(g + 1) * WINDOW] for g in range(SWA_GRP)]
            dks.append(lax.dot_general(dsb, q8, (((0,), (0,)), ((), ())), preferred_element_type=F32))
            dvs.append(lax.dot_general(p.astype(BF16), do8, (((0,), (0,)), ((), ())), preferred_element_type=F32))
        dq_ref[...] = jnp.concatenate(dqs, axis=1)
        dsink_ref[...] += jnp.concatenate(dsk, axis=1)
        both = jnp.concatenate(dks + dvs, axis=1)
        r_cur = pl.multiple_of(n * WINDOW, WINDOW)
        r_prev = pl.multiple_of(jnp.maximum(n - 1, 0) * WINDOW, WINDOW)
        dkv_ref[pl.ds(r_prev, WINDOW), :] += both[:WINDOW]
        dkv_ref[pl.ds(r_cur, WINDOW), :] += both[WINDOW:]

    return carried(
        body, comm, grid=(L // WINDOW,),
        in_specs=[pl.BlockSpec((WINDOW, 1024), lambda n: (n, 1)), pl.BlockSpec((WINDOW, 256), lambda n: (n, 8)),
                  pl.BlockSpec((WINDOW, 256), lambda n: (jnp.maximum(n - 1, 0), 8)),
                  pl.BlockSpec((WINDOW, 1024), lambda n: (n, 0)),
                  pl.BlockSpec((SWA_HEADS, WINDOW, 2 * WINDOW), lambda n: (0, 0, 0)),
                  pl.BlockSpec((1, SWA_HEADS), lambda n: (0, 0))],
        out_specs=[pl.BlockSpec((WINDOW, 1024), lambda n: (n, 0)), pl.BlockSpec((L, 256), lambda n: (0, 0)),
                   pl.BlockSpec((SWA_HEADS, WINDOW, 2 * WINDOW), lambda n: (0, 0, 0)),
                   pl.BlockSpec((1, SWA_HEADS), lambda n: (0, 0))],
        out_shape=[jax.ShapeDtypeStruct((L, 1024), F32), jax.ShapeDtypeStruct((L, 256), F32),
                   jax.ShapeDtypeStruct((SWA_HEADS, WINDOW, 2 * WINDOW), F32), jax.ShapeDtypeStruct((1, SWA_HEADS), F32)],
        semantics=("arbitrary",), name='b_attn_bwd')(proj, proj, proj, do, bias, sinks)


def swa_bias(rel_bias):
    def body(bk_ref, rb_ref, o_ref):
        bk = bk_ref[...]
        for h in range(SWA_HEADS):
            acc = jnp.zeros((WINDOW, 2 * WINDOW), F32)
            for b in range(REL_BUCKETS):
                acc = jnp.where(bk == b, rb_ref[b, h], acc)
            o_ref[h] = acc

    return pl.pallas_call(
        body, out_shape=jax.ShapeDtypeStruct((SWA_HEADS, WINDOW, 2 * WINDOW), F32),
        in_specs=[pl.BlockSpec(memory_space=pltpu.VMEM), pl.BlockSpec(memory_space=pltpu.SMEM)],
        out_specs=pl.BlockSpec(memory_space=pltpu.VMEM), name='b_bias')(jnp.asarray(_t5_bucket_np()), rel_bias)


def layer_b_fwd(h, w, p, comm=None):
    proj = mm(h, w['b_w_in'], 'nn', 'b_proj')
    bias = swa_bias(p['rel_bias'])
    (o, po), carried_out = swa_fwd(proj, bias, p['b_sinks'], comm=comm)
    yb = mm(po, w['b_w_out'], 'nn', 'b_out')
    return yb, dict(carried=carried_out, h=h, proj=proj, bias=bias, o=o, po=po)


def layer_b_bwd(dyb, w, p, sv, comm=None, sink=None):
    g = {}
    dpo = mm(dyb, w['b_w_out'], 'nt', 'b_dpo')
    _dw(g, sink, 'b_w_out', sv['po'], dyb, 'b_dwout')
    proj = sv['proj']

    def f1(dpo_, o, z):
        return [dpo_ * silu(z), dpo_ * o * silu_grad(z)], []
    (do, dz), _ = rowwise(f1, [rw(dpo), rw(sv['o']), rw(proj, 1024, 0)], [], [(1024, BF16), (1024, F32)], [], 256, 'b_gate_bwd')
    (dq, dkv, dbias, dsinks), g['carried'] = swa_bwd(proj, do, sv['bias'], p['b_sinks'], comm=comm)
    g['b_sinks'] = dsinks
    onehot = jnp.asarray(np.eye(REL_BUCKETS, dtype=np.float32)[_t5_bucket_np().reshape(-1)])

    def f2(db, oh):
        return [], [lax.dot_general(db, oh, (((1,), (0,)), ((), ())), preferred_element_type=F32,
                                    precision=lax.Precision.HIGHEST)]
    _, (drel,) = rowwise(f2, [(dbias.reshape(SWA_HEADS, -1), pl.BlockSpec((SWA_HEADS, 4096), lambda i: (0, i))),
                              (onehot, pl.BlockSpec((4096, REL_BUCKETS), lambda i: (i, 0)))], [], [],
                         [(SWA_HEADS, REL_BUCKETS)], 4096, 'b_drel', n_steps=(2 * WINDOW * WINDOW) // 4096)
    g['rel_bias'] = drel.T

    def f3(dz_, dq_, dkv_):
        return [jnp.concatenate([dz_, dq_, dkv_], axis=1)], []
    (dproj,), _ = rowwise(f3, [rw(dz), rw(dq), rw(dkv)], [], [(2304, BF16)], [], 256, 'b_dproj')
    _dw(g, sink, 'b_w_in', sv['h'], dproj, 'b_dwin')
    dh = mm(dproj, w['b_w_in'], 'nt', 'b_dh')
    return dh, g


MLA_SCALE = (MLA_NOPE + MLA_ROPE) ** -0.5


def _rope_tables(L):
    inv = ROPE_BASE ** (-jnp.arange(0, MLA_ROPE, 2, dtype=F32) / MLA_ROPE)
    ang = jnp.arange(L, dtype=F32)[:, None] * inv[None, :]
    c, s = jnp.cos(ang), jnp.sin(ang)
    one, zero, pad = jnp.ones((L, 128), F32), jnp.zeros((L, 128), F32), jnp.zeros((L, 64), F32)
    return (jnp.concatenate([one, c, c, c, c, pad], 1), jnp.concatenate([zero, s, s, s, s, pad], 1))


def _rot(x, transpose=False):
    w = x.shape[1]
    lane = lax.broadcasted_iota(jnp.int32, x.shape, 1)
    up = pltpu.roll(x, w - 16, 1)
    dn = pltpu.roll(x, 16, 1)
    first = (lane % 32) < 16
    return jnp.where(first, up, -dn) if transpose else jnp.where(first, -up, dn)


MLA_QT = 512


def _mla_exp(qf, kf, t, qt):
    n_k = kf.shape[0]
    s = lax.dot_general(qf, kf, (((1,), (1,)), ((), ())), preferred_element_type=F32) * MLA_SCALE
    qpos = t * qt + lax.broadcasted_iota(jnp.int32, (qt, n_k), 0)
    kpos = lax.broadcasted_iota(jnp.int32, (qt, n_k), 1)
    s = jnp.where(kpos <= qpos, s, NEG_INF)
    e = jnp.exp(s - jnp.max(s, axis=-1, keepdims=True))
    return e, jnp.sum(e, axis=-1, keepdims=True)


def _mla_heads(q, kv, kr):
    out = []
    for j in range(2):
        qf = jnp.concatenate([q[:, j * 64:(j + 1) * 64], q[:, 128 + j * 32:128 + (j + 1) * 32]], axis=1)
        kf = jnp.concatenate([kv[:, j * 64:(j + 1) * 64], kr], axis=1)
        out.append((qf, kf, kv[:, 128 + j * 64:128 + (j + 1) * 64]))
    return out


def mla_fwd(q, kv, kr, comm=None):
    L = q.shape[0]
    qt = min(MLA_QT, L)
    nq = L // qt

    def body(q_ref, kv_ref, kr_ref, o_ref):
        for t in range(nq):
            @pl.when(pl.program_id(1) == t)
            def _(t=t):
                n_k = (t + 1) * qt
                outs = []
                for qf, kf, v in _mla_heads(q_ref[...], kv_ref[0:n_k, :], kr_ref[0:n_k, 0:MLA_ROPE]):
                    e, den = _mla_exp(qf, kf, t, qt)
                    outs.append(jnp.dot(e.astype(BF16), v, preferred_element_type=F32) / den)
                o_ref[...] = jnp.concatenate(outs, axis=1)

    return carried(
        body, comm, grid=(MLA_HEADS // 2, nq),
        in_specs=[pl.BlockSpec((qt, 256), lambda hp, n: (n, hp)), pl.BlockSpec((L, 256), lambda hp, n: (0, hp)),
                  pl.BlockSpec((L, 128), lambda hp, n: (0, 0))],
        out_specs=pl.BlockSpec((qt, 128), lambda hp, n: (n, hp)), out_shape=jax.ShapeDtypeStruct((L, 1024), F32),
        semantics=("parallel", "parallel"), name='c_attn')(q, kv, kr)


def mla_bwd(q, kv, kr, do, comm=None):
    L = q.shape[0]
    qt = min(MLA_QT, L)
    nq = L // qt

    def body(q_ref, kv_ref, kr_ref, do_ref, dq_ref, dkv_ref, dkr_ref):
        @pl.when(pl.program_id(1) == 0)
        def _():
            dkv_ref[...] = jnp.zeros_like(dkv_ref)
            dkr_ref[...] = jnp.zeros_like(dkr_ref)

        for t in range(nq):
            @pl.when(pl.program_id(1) == t)
            def _(t=t):
                n_k = (t + 1) * qt
                do_ = do_ref[...]
                dqn, dqr, dkn, dvs = [], [], [], []
                dkr = jnp.zeros((n_k, MLA_ROPE), F32)
                for j, (qf, kf, v) in enumerate(_mla_heads(q_ref[...], kv_ref[0:n_k, :], kr_ref[0:n_k, 0:MLA_ROPE])):
                    doh = do_[:, j * 64:(j + 1) * 64]
                    e, den = _mla_exp(qf, kf, t, qt)
                    p = e * (1.0 / den)
                    dp = lax.dot_general(doh, v, (((1,), (1,)), ((), ())), preferred_element_type=F32)
                    ds = (p * (dp - jnp.sum(p * dp, axis=-1, keepdims=True)) * MLA_SCALE).astype(BF16)
                    dqf = jnp.dot(ds, kf, preferred_element_type=F32)
                    dkf = lax.dot_general(ds, qf, (((0,), (0,)), ((), ())), preferred_element_type=F32)
                    dvs.append(lax.dot_general(p.astype(BF16), doh, (((0,), (0,)), ((), ())), preferred_element_type=F32))
                    dqn.append(dqf[:, :MLA_NOPE])
                    dqr.append(dqf[:, MLA_NOPE:])
                    dkn.append(dkf[:, :MLA_NOPE])
                    dkr = dkr + dkf[:, MLA_NOPE:]
                dq_ref[...] = jnp.concatenate(dqn + dqr + [jnp.zeros((qt, 64), F32)], axis=1)
                dkv_ref[0:n_k, :] += jnp.concatenate(dkn + dvs, axis=1)
                dkr_ref[0, 0:n_k, :] += jnp.concatenate([dkr, jnp.zeros((n_k, 128 - MLA_ROPE), F32)], axis=1)

    return carried(
        body, comm, grid=(MLA_HEADS // 2, nq),
        in_specs=[pl.BlockSpec((qt, 256), lambda hp, n: (n, hp)), pl.BlockSpec((L, 256), lambda hp, n: (0, hp)),
                  pl.BlockSpec((L, 128), lambda hp, n: (0, 0)), pl.BlockSpec((qt, 128), lambda hp, n: (n, hp))],
        out_specs=[pl.BlockSpec((qt, 256), lambda hp, n: (n, hp)), pl.BlockSpec((L, 256), lambda hp, n: (0, hp)),
                   pl.BlockSpec((1, L, 128), lambda hp, n: (hp, 0, 0))],
        out_shape=[jax.ShapeDtypeStruct((L, 2048), F32), jax.ShapeDtypeStruct((L, 2048), F32),
                   jax.ShapeDtypeStruct((MLA_HEADS // 2, L, 128), F32)],
        semantics=("parallel", "arbitrary"), name='c_attn_bwd')(q, kv, kr, do)


def layer_c_fwd(h, w, p, comm=None):
    L = h.shape[0]
    proj = mm(h, w['c_w_in'], 'nn', 'c_proj')

    def f1(c, gq, gk):
        return [rms_fwd(c[:, :768], gq), rms_fwd(c[:, 768:], gk)], []
    (cqn, ckvn), _ = rowwise(f1, [rw(proj, 1024, 1)], [p['c_q_norm'], p['c_kv_norm']], [(768, BF16), (256, BF16)], [],
                             256, 'c_norms')
    qf = mm(cqn, w['c_w_uq'], 'nn', 'c_uq')
    kvf = mm(ckvn, w['c_w_ukv'], 'nn', 'c_ukv', out_dtype=BF16)
    cos, sin = _rope_tables(L)

    def f2(q_, kr_, c, s):
        c8, s8 = jnp.tile(c, (1, 8)), jnp.tile(s, (1, 8))
        return [q_ * c8 + _rot(q_) * s8, kr_ * c[:, 128:] + _rot(kr_) * s[:, 128:]], []
    (q, kr), _ = rowwise(f2, [rw(qf), rw(proj, 128, 16), rw(cos), rw(sin)], [], [(2048, BF16), (128, BF16)], [], 256,
                         'c_rope')
    o, carried_out = mla_fwd(q, kvf, kr, comm=comm)

    def f3(o_, z):
        return [o_ * silu(z)], []
    (po,), _ = rowwise(f3, [rw(o), rw(proj, 1024, 0)], [], [(1024, BF16)], [], 256, 'c_gate')
    yb = mm(po, w['c_w_out'], 'nn', 'c_out')
    return yb, dict(carried=carried_out, h=h, proj=proj, cqn=cqn, ckvn=ckvn, q=q, kv=kvf, kr=kr, o=o, po=po, cos=cos, sin=sin)


def layer_c_bwd(dyb, w, p, sv, comm=None, sink=None):
    g = {}
    dpo = mm(dyb, w['c_w_out'], 'nt', 'c_dpo')
    _dw(g, sink, 'c_w_out', sv['po'], dyb, 'c_dwout')
    proj = sv['proj']
    L = proj.shape[0]

    def f1(dpo_, o, z):
        return [dpo_ * silu(z), dpo_ * o * silu_grad(z)], []
    (do, dz), _ = rowwise(f1, [rw(dpo), rw(sv['o']), rw(proj, 1024, 0)], [], [(1024, BF16), (1024, F32)], [], 256,
                          'c_gate_bwd')
    (dq, dkvf, dkr8), g['carried'] = mla_bwd(sv['q'], sv['kv'], sv['kr'], do, comm=comm)

    def f2(dq_, dkr_, c, s):
        c8, s8 = jnp.tile(c, (1, 8)), jnp.tile(s, (1, 8))
        dk = jnp.sum(dkr_, axis=0)
        return [dq_ * c8 + _rot(dq_ * s8, True), dk * c[:, 128:] + _rot(dk * s[:, 128:], True)], []
    tl = 256
    (dqf, dkr), _ = rowwise(f2, [rw(dq), (dkr8, pl.BlockSpec((8, tl, 128), lambda i: (0, i, 0))), rw(sv['cos']),
                                 rw(sv['sin'])], [], [(2048, BF16), (128, F32)], [], tl, 'c_rope_bwd')
    _dw(g, sink, 'c_w_uq', sv['cqn'], dqf, 'c_dwuq')
    _dw(g, sink, 'c_w_ukv', sv['ckvn'], dkvf, 'c_dwukv')
    dcqn = mm(dqf, w['c_w_uq'], 'nt', 'c_dcqn')
    dckvn = mm(dkvf, w['c_w_ukv'], 'nt', 'c_dckvn')

    def f3(c, dq_, dk_, dz_, dkr_, gq, gk):
        dcq, dgq = rms_bwd(c[:, :768], gq, dq_)
        dckv, dgk = rms_bwd(c[:, 768:], gk, dk_)
        return [jnp.concatenate([dz_, dcq, dckv, dkr_], axis=1)], [dgq, dgk]
    (dproj,), (dgq, dgk) = rowwise(f3, [rw(proj, 1024, 1), rw(dcqn), rw(dckvn), rw(dz), rw(dkr)],
                                   [p['c_q_norm'], p['c_kv_norm']], [(2176, BF16)], [(1, 768), (1, 256)], 256, 'c_dproj')
    g['c_q_norm'], g['c_kv_norm'] = dgq, dgk
    _dw(g, sink, 'c_w_in', sv['h'], dproj, 'c_dwin')
    dh = mm(dproj, w['c_w_in'], 'nt', 'c_dh')
    return dh, g


def _sgu_mix(wm, v, transpose):
    outs = []
    dims = (((0,), (0,)), ((), ())) if transpose else (((1,), (0,)), ((), ()))
    for gi in range(SGU_G):
        outs.append(lax.dot_general(wm[gi], v[:, gi * SGU_C:(gi + 1) * SGU_C].astype(BF16), dims,
                                    preferred_element_type=F32))
    return jnp.concatenate(outs, axis=1)


def _sgu_wmask(ws):
    t = lax.broadcasted_iota(jnp.int32, (SGU_T, SGU_T), 0)
    s = lax.broadcasted_iota(jnp.int32, (SGU_T, SGU_T), 1)
    return jnp.where((s <= t)[None], ws, 0.0).astype(BF16)


def _ln_stats(v):
    mu = jnp.mean(v, axis=-1, keepdims=True)
    vc = v - mu
    rstd = lax.rsqrt(jnp.mean(vc * vc, axis=-1, keepdims=True) + EPS)
    return vc * rstd, rstd


def layer_d_fwd(h, w, p):
    proj = mm(h, w['d_w_in'], 'nn', 'd_proj')
    bias = jnp.repeat(p['d_b_s'][0].T, SGU_C, axis=1)

    def f1(u_, v_, z, ws, lg, lb, bs):
        xh, _ = _ln_stats(gelu(v_))
        s = _sgu_mix(_sgu_wmask(ws), xh * lg + lb, False) + bs
        return [gelu(u_) * s * silu(z)], []
    (po,), _ = rowwise(f1, [rw(proj, 1024, 0), rw(proj, 1024, 1), rw(proj, 1024, 2)],
                       [p['d_w_s'][0], p['d_ln_g'], p['d_ln_b'], bias], [(1024, BF16)], [], SGU_T, 'd_mix')
    yb = mm(po, w['d_w_out'], 'nn', 'd_out')
    return yb, dict(h=h, proj=proj, po=po, bias=bias)


def layer_d_bwd(dyb, w, p, sv, sink=None):
    g = {}
    dpo = mm(dyb, w['d_w_out'], 'nt', 'd_dpo')
    _dw(g, sink, 'd_w_out', sv['po'], dyb, 'd_dwout')
    proj = sv['proj']

    def f1(dpo_, u_, v_, z, ws, lg, lb, bs):
        wm = _sgu_wmask(ws)
        gv = gelu(v_)
        xh, rstd = _ln_stats(gv)
        vn = xh * lg + lb
        s = _sgu_mix(wm, vn, False) + bs
        gu, sz = gelu(u_), silu(z)
        du = dpo_ * s * sz
        ds = dpo_ * gu * sz
        dz = dpo_ * gu * s * silu_grad(z)
        dsb = ds.astype(BF16)
        dws = jnp.stack([lax.dot_general(dsb[:, gi * SGU_C:(gi + 1) * SGU_C], vn[:, gi * SGU_C:(gi + 1) * SGU_C].astype(BF16),
                                         (((1,), (1,)), ((), ())), preferred_element_type=F32) for gi in range(SGU_G)])
        dvn = _sgu_mix(wm, ds, True)
        dlg = jnp.sum(dvn * xh, axis=0, keepdims=True)
        dlb = jnp.sum(dvn, axis=0, keepdims=True)
        dxh = dvn * lg
        dgv = rstd * (dxh - jnp.mean(dxh, axis=-1, keepdims=True) - xh * jnp.mean(dxh * xh, axis=-1, keepdims=True))
        return ([jnp.concatenate([du * gelu_grad(u_), dgv * gelu_grad(v_), dz], axis=1)], [dws, ds, dlg, dlb])
    (dproj,), (dws, dbs, dlg, dlb) = rowwise(
        f1, [rw(dpo), rw(proj, 1024, 0), rw(proj, 1024, 1), rw(proj, 1024, 2)],
        [p['d_w_s'][0], p['d_ln_g'], p['d_ln_b'], sv['bias']], [(3072, BF16)],
        [(SGU_G, SGU_T, SGU_T), (SGU_T, 1024), (1, 1024), (1, 1024)], SGU_T, 'd_mix_bwd')
    tril = np.tril(np.ones((SGU_T, SGU_T), dtype=bool))
    g['d_w_s'] = jnp.where(tril[None], dws, 0.0)[None]
    g['d_b_s'] = dbs.reshape(SGU_T, SGU_G, SGU_C).sum(-1).T[None]
    g['d_ln_g'], g['d_ln_b'] = dlg, dlb
    _dw(g, sink, 'd_w_in', sv['h'], dproj, 'd_dwin')
    dh = mm(dproj, w['d_w_in'], 'nt', 'd_dh')
    return dh, g


def _coords():
    return lax.axis_index("x"), lax.axis_index("y"), lax.axis_index("c")


class AllGather:
    def __init__(self, x):
        self.ins = [x]
        self.outs = [jax.ShapeDtypeStruct((N_DEV,) + x.shape, x.dtype)]
        self.scratch = [pltpu.SemaphoreType.DMA((7,)), pltpu.SemaphoreType.DMA((7,)), pltpu.SemaphoreType.DMA(())]

    def hooks(self, n_steps):
        return [(0, functools.partial(self.phase, 0), False), ((n_steps * 5) // 8, functools.partial(self.phase, 1), False),
                (n_steps - 1, functools.partial(self.phase, 2), True)]

    @staticmethod
    def phase(which, ins, outs, scratch):
        (x_ref,), (out_ref,), (send_sems, recv_sems, local_sem) = ins, outs, scratch
        x_, y_, c_ = _coords()
        me, sibling = (x_, y_, c_), (x_, y_, 1 - c_)
        chips = [(1 - x_, y_), (x_, 1 - y_), (1 - x_, 1 - y_)]

        def slot(px, py, pc):
            return out_ref.at[4 * px + 2 * py + pc]

        def copy(k, block, to, src=None):
            return pltpu.make_async_remote_copy(src_ref=slot(*block) if src is None else src, dst_ref=slot(*block),
                                                send_sem=send_sems.at[k], recv_sem=recv_sems.at[k], device_id=to,
                                                device_id_type=MESH)

        mine = pltpu.make_async_copy(x_ref, slot(*me), local_sem)
        first = [copy(0, me, sibling, src=x_ref)]
        first += [copy(1 + j, me, (*chip, c_), src=x_ref) for j, chip in enumerate(chips)]
        passed = [copy(4 + j, (*chip, c_), sibling) for j, chip in enumerate(chips)]
        if which == 0:
            mine.start()
            for cp in first:
                cp.start()
        elif which == 1:
            for j, chip in enumerate(chips):
                copy(1 + j, (*chip, c_), me).wait_recv()
                passed[j].start()
        else:
            copy(0, sibling, me).wait_recv()
            for j, chip in enumerate(chips):
                copy(4 + j, (*chip, 1 - c_), me).wait_recv()
            for cp in first + passed:
                cp.wait_send()
            mine.wait()


class ChipExchange:
    def __init__(self, part):
        self.ins = [part]
        self.outs = [jax.ShapeDtypeStruct((3,) + part.shape[1:], part.dtype)]
        self.scratch = [pltpu.SemaphoreType.DMA((3,)), pltpu.SemaphoreType.DMA((3,))]

    def hooks(self, n_steps):
        return [(0, functools.partial(self.phase, 0), False), (n_steps - 1, functools.partial(self.phase, 1), True)]

    @staticmethod
    def phase(which, ins, outs, scratch):
        (p_ref,), (land_ref,), (send_sems, recv_sems) = ins, outs, scratch
        x_, y_, c_ = _coords()
        copies = []
        for r, (fx, fy) in enumerate([(1, 0), (0, 1), (1, 1)]):
            tx = jnp.where(fx == 1, 1 - x_, x_)
            ty = jnp.where(fy == 1, 1 - y_, y_)
            copies.append(pltpu.make_async_remote_copy(src_ref=p_ref.at[2 * tx + ty], dst_ref=land_ref.at[r],
                                                       send_sem=send_sems.at[r], recv_sem=recv_sems.at[r],
                                                       device_id=(tx, ty, c_), device_id_type=MESH))
        if which == 0:
            for cp in copies:
                cp.start()
        else:
            for cp in copies:
                cp.wait_recv()
            for cp in copies:
                cp.wait_send()


class Both:
    def __init__(self, a, b):
        self.parts = (a, b)
        self.ins, self.outs, self.scratch = a.ins + b.ins, a.outs + b.outs, a.scratch + b.scratch

    def hooks(self, n_steps):
        res, oi, oo, osc = [], 0, 0, 0
        for p in self.parts:
            sl = (slice(oi, oi + len(p.ins)), slice(oo, oo + len(p.outs)), slice(osc, osc + len(p.scratch)))
            res += [(at, functools.partial(self.sub, fn, sl), after) for at, fn, after in p.hooks(n_steps)]
            oi, oo, osc = oi + len(p.ins), oo + len(p.outs), osc + len(p.scratch)
        return res

    @staticmethod
    def sub(fn, sl, ins, outs, scratch):
        fn(ins[sl[0]], outs[sl[1]], scratch[sl[2]])


def run_comm(comm, name):
    def body(*refs):
        ci, co = len(comm.ins), len(comm.outs)
        for _, fn, _ in comm.hooks(1):
            fn(refs[:ci], refs[ci:ci + co], refs[ci + co:])

    return pl.pallas_call(body, out_shape=list(comm.outs), in_specs=[ANY] * len(comm.ins),
                          out_specs=[ANY] * len(comm.outs), scratch_shapes=list(comm.scratch), name=name)(*comm.ins)


def all_gather(x, name):
    return run_comm(AllGather(x), name)[0]


def rs_sibling(gfull, tag):
    _, R, C = gfull.shape

    def body(g_ref, land_ref, send_sems, recv_sems):
        x_, y_, c_ = _coords()
        copies = []
        for k in range(4):
            cp = pltpu.make_async_remote_copy(src_ref=g_ref.at[2 * k + 1 - c_], dst_ref=land_ref.at[k],
                                              send_sem=send_sems.at[k], recv_sem=recv_sems.at[k],
                                              device_id=(x_, y_, 1 - c_), device_id_type=MESH)
            cp.start()
            copies.append(cp)
        for cp in copies:
            cp.wait_recv()
        for cp in copies:
            cp.wait_send()

    return pl.pallas_call(
        body, out_shape=jax.ShapeDtypeStruct((4, R, C), gfull.dtype), in_specs=[ANY], out_specs=ANY,
        scratch_shapes=[pltpu.SemaphoreType.DMA((4,)), pltpu.SemaphoreType.DMA((4,))], name='rs_sibling_' + tag)(gfull)


def rs_pair_add(gfull, land, core, tag):
    _, R, C = gfull.shape
    tl = R

    def body(c_ref, g_ref, l_ref, o_ref):
        o_ref[...] = (g_ref[...].astype(F32) + l_ref[...].astype(F32)).astype(BF16)

    return pl.pallas_call(
        body, out_shape=jax.ShapeDtypeStruct((4, R, C), BF16),
        grid_spec=pltpu.PrefetchScalarGridSpec(
            num_scalar_prefetch=1, grid=(4, R // tl),
            in_specs=[pl.BlockSpec((1, tl, C), lambda k, i, c: (2 * k + c[0], i, 0)),
                      pl.BlockSpec((1, tl, C), lambda k, i, c: (k, i, 0))],
            out_specs=pl.BlockSpec((1, tl, C), lambda k, i, c: (k, i, 0))),
        compiler_params=pltpu.CompilerParams(dimension_semantics=("parallel", "parallel")), name='rs_pair_add_' + tag)(
            core, gfull, land)


def rs_chips(part, tag):
    return run_comm(ChipExchange(part), 'rs_chips_' + tag)[0]


def _adam(wv, gv, mv, vv):
    m = ADAM_B1 * mv + (1.0 - ADAM_B1) * gv
    v = ADAM_B2 * vv + (1.0 - ADAM_B2) * (gv * gv)
    m_hat = m / (1.0 - ADAM_B1 ** ADAM_STEP)
    v_hat = v / (1.0 - ADAM_B2 ** ADAM_STEP)
    delta = -ADAM_LR * (m_hat / (jnp.sqrt(v_hat) + ADAM_EPS) + ADAM_WD * wv)
    return delta, m, v


def _sum4(p_ref, l_ref):
    return ((p_ref[0].astype(F32) + l_ref[0].astype(F32)) + l_ref[1].astype(F32)) + l_ref[2].astype(F32)


def rs_rep_sum(part, land, chip):
    def body(c_ref, p_ref, l_ref, o_ref):
        o_ref[...] = _sum4(p_ref, l_ref).astype(BF16)

    return pl.pallas_call(
        body, out_shape=jax.ShapeDtypeStruct((REP_SLOT, LANES), BF16),
        grid_spec=pltpu.PrefetchScalarGridSpec(
            num_scalar_prefetch=1, grid=(1,),
            in_specs=[pl.BlockSpec((1, REP_SLOT, LANES), lambda i, c: (c[0], 0, 0)),
                      pl.BlockSpec((3, REP_SLOT, LANES), lambda i, c: (0, 0, 0))],
            out_specs=pl.BlockSpec((REP_SLOT, LANES), lambda i, c: (0, 0))),
        compiler_params=pltpu.CompilerParams(dimension_semantics=("parallel",)), name='rs_rep')(chip, part, land)


def adam_param(name, shape, off, w, m, v, chip, part=None, land=None, grep=None):
    r, c = shape
    rp, nt, rb = _tiles(shape)
    rbw = min(r, rb)
    n_src = 2 if grep is None else 1
    ns = w.shape
    assert int(np.prod(ns[:-1])) == r and ns[-1] == c
    if len(ns) == 2:
        nat_block, nat_map = (rbw, c), lambda i, cr: (i, 0)
    elif int(np.prod(ns[:-2])) == 1:
        nat_block, nat_map = (1,) * (len(ns) - 2) + (rbw, c), lambda i, cr: (0,) * (len(ns) - 2) + (i, 0)
    else:
        assert len(ns) == 4 and ns[0] == 1 and rbw % ns[2] == 0
        nat_block, nat_map = (1, rbw // ns[2], ns[2], c), lambda i, cr: (0, i, 0, 0)

    def body(c_ref, *refs):
        srcs = refs[:n_src * nt]
        w_ref, m_ref, v_ref, g_ref, d_ref, nm_ref, nv_ref = refs[n_src * nt:]
        if grep is None:
            tiles = [_sum4(srcs[2 * t], srcs[2 * t + 1]) for t in range(nt)]
        else:
            tiles = [srcs[t][...].astype(F32) for t in range(nt)]
        g = (tiles[0] if nt == 1 else jnp.concatenate(tiles, axis=1))[:rbw, :c]
        g_ref[...] = g.reshape(nat_block)
        res = _adam(w_ref[...].reshape(rbw, c), g, m_ref[...].reshape(rbw, c), v_ref[...].reshape(rbw, c))
        for ref, val in zip((d_ref, nm_ref, nv_ref), res):
            ref[...] = val.reshape(nat_block)

    in_specs, args = [], []
    for t in range(nt):
        b0 = (off + t * rp) // rb
        assert (off + t * rp) % rb == 0
        if grep is None:
            in_specs += [pl.BlockSpec((1, rb, LANES), functools.partial(lambda i, cr, b0: (cr[0], b0 + i, 0), b0=b0)),
                         pl.BlockSpec((3, rb, LANES), functools.partial(lambda i, cr, b0: (0, b0 + i, 0), b0=b0))]
            args += [part, land]
        else:
            in_specs.append(pl.BlockSpec((rb, LANES), functools.partial(lambda i, cr, b0: (b0 + i, 0), b0=b0)))
            args.append(grep)
    nat = pl.BlockSpec(nat_block, nat_map)
    return pl.pallas_call(
        body, out_shape=[jax.ShapeDtypeStruct(ns, F32)] * 4,
        grid_spec=pltpu.PrefetchScalarGridSpec(num_scalar_prefetch=1, grid=(rp // rb,), in_specs=in_specs + [nat] * 3,
                                               out_specs=[nat] * 4),
        compiler_params=pltpu.CompilerParams(dimension_semantics=("parallel",)), name='adam_' + name)(
            chip, *args, w, m, v)


def adam_small(names, grep, P, M, V):
    in_specs, args, out_specs, out_shape, meta = [], [], [], [], []
    for n in names:
        s = REP_SHAPE[n]
        rp, nt, _ = _tiles(s)
        ns = P[n].shape
        for t in range(nt):
            b0 = (REP_OFF[n] + t * rp) // rp
            assert (REP_OFF[n] + t * rp) % rp == 0
            in_specs.append(pl.BlockSpec((rp, LANES), functools.partial(lambda i, b0: (b0, 0), b0=b0)))
            args.append(grep)
        nat = pl.BlockSpec(ns, functools.partial(lambda i, nd: (0,) * nd, nd=len(ns)))
        in_specs += [nat] * 3
        args += [P[n], M[n], V[n]]
        out_specs += [nat] * 4
        out_shape += [jax.ShapeDtypeStruct(ns, F32)] * 4
        meta.append((s, nt, ns))
    n_in = len(in_specs)

    def body(*refs):
        ins, outs = refs[:n_in], refs[n_in:]
        k = 0
        for p, ((r, c), nt, ns) in enumerate(meta):
            tiles = [ins[k + t][...].astype(F32) for t in range(nt)]
            w_ref, m_ref, v_ref = ins[k + nt:k + nt + 3]
            k += nt + 3
            g = (tiles[0] if nt == 1 else jnp.concatenate(tiles, axis=1))[:r, :c]
            res = (g,) + _adam(w_ref[...].reshape(r, c), g, m_ref[...].reshape(r, c), v_ref[...].reshape(r, c))
            for ref, val in zip(outs[4 * p:4 * p + 4], res):
                ref[...] = val.reshape(ns)

    res = pl.pallas_call(body, grid=(1,), in_specs=in_specs, out_specs=out_specs, out_shape=out_shape,
                         compiler_params=pltpu.CompilerParams(dimension_semantics=("arbitrary",)), name='adam_small')(*args)
    return {n: tuple(res[4 * p:4 * p + 4]) for p, n in enumerate(names)}


VM = pl.BlockSpec(memory_space=pltpu.VMEM)


def _tile_value(w, t, rp):
    r, c = w.shape
    wt = min(LANES, c - t * LANES)
    tile = w[:, t * LANES:t * LANES + wt]
    if wt < LANES:
        tile = jnp.concatenate([tile, jnp.zeros((r, LANES - wt), tile.dtype)], axis=1)
    if rp > r:
        tile = jnp.concatenate([tile, jnp.zeros((rp - r, LANES), tile.dtype)], axis=0)
    return tile


def pack_layer(layer, blocks):
    names = LAYER_PARAMS[layer]

    def body(*refs):
        tiles = []
        for ref, n in zip(refs[:-1], names):
            rp, nt, _ = _tiles(_block_shape(n))
            w = ref[...].reshape(_block_shape(n))
            tiles += [_tile_value(w, t, rp) for t in range(nt)]
        refs[-1][...] = jnp.concatenate(tiles, axis=0).astype(BF16)

    return pl.pallas_call(body, out_shape=jax.ShapeDtypeStruct((LAYER_ROWS[layer], LANES), BF16),
                          in_specs=[VM] * len(names), out_specs=VM, name='pack_' + layer)(*[blocks[n] for n in names])


def assemble(name, gathered):
    (rf, cf), ax = SHARDED[name]
    r, c = _block_shape(name)
    rp, nt, _ = _tiles((r, c))
    off = SH_OFF[name]
    out_cols = cf if ax == 0 else len(perm_index(name))

    def body(g_ref, o_ref, buf, sem):
        cp = pltpu.make_async_copy(g_ref.at[:, pl.ds(off, nt * rp), :], buf, sem)
        cp.start()
        cp.wait()
        if ax == 0:
            for j in range(N_DEV):
                o_ref[j * r:(j + 1) * r, :] = jnp.concatenate([buf[j, t * rp:(t + 1) * rp, :] for t in range(nt)], axis=1)
            return
        pieces = []
        for p in PERM[name]:
            if p[0] == 'z':
                pieces.append(jnp.zeros((r, p[1]), BF16))
                continue
            n0, w = p
            while w > 0:
                j, cb = divmod(n0, c)
                t, lane = divmod(cb, LANES)
                wl = min(w, LANES - lane, c - cb)
                pieces.append(buf[j, t * rp:t * rp + r, lane:lane + wl])
                n0, w = n0 + wl, w - wl
        o_ref[...] = jnp.concatenate(pieces, axis=1)

    return pl.pallas_call(
        body, out_shape=jax.ShapeDtypeStruct((rf, out_cols), BF16), in_specs=[ANY], out_specs=VM,
        scratch_shapes=[pltpu.VMEM((N_DEV, nt * rp, LANES), BF16), pltpu.SemaphoreType.DMA(())], name='asm_' + name)(
            gathered)


def chunk_grad(layer, name, dw, gfull):
    (rf, cf), ax = SHARDED[name]
    r, c = _block_shape(name)
    rp, nt, _ = _tiles((r, c))
    off = SH_OFF[name]
    if ax == 1:
        idx = perm_index(name) if name in PERM else np.arange(cf)
        inv = np.full(cf, -1)
        inv[idx[idx >= 0]] = np.nonzero(idx >= 0)[0]

    def body(*refs):
        dw_ref, o_ref, buf, sem = refs[0], refs[-3], refs[-2], refs[-1]
        for j in range(N_DEV):
            for t in range(nt):
                if ax == 0:
                    tile = dw_ref[j * r:(j + 1) * r, t * LANES:(t + 1) * LANES]
                else:
                    cols = inv[j * c + t * LANES:j * c + min((t + 1) * LANES, c)]
                    cuts = [0] + [k for k in range(1, len(cols)) if cols[k] != cols[k - 1] + 1] + [len(cols)]
                    pieces = [dw_ref[:, int(cols[a]):int(cols[b - 1]) + 1] for a, b in zip(cuts[:-1], cuts[1:])]
                    if len(cols) < LANES:
                        pieces.append(jnp.zeros((r, LANES - len(cols)), F32))
                    tile = pieces[0] if len(pieces) == 1 else jnp.concatenate(pieces, axis=1)
                    if rp > r:
                        tile = jnp.concatenate([tile, jnp.zeros((rp - r, LANES), F32)], axis=0)
                buf[j, t * rp:(t + 1) * rp, :] = tile.astype(BF16)
        cp = pltpu.make_async_copy(buf, o_ref.at[:, pl.ds(off, nt * rp), :], sem)
        cp.start()
        cp.wait()

    shape = jax.ShapeDtypeStruct((N_DEV, LAYER_ROWS[layer], LANES), BF16)
    scratch = [pltpu.VMEM((N_DEV, nt * rp, LANES), BF16), pltpu.SemaphoreType.DMA(())]
    if gfull is None:
        return pl.pallas_call(body, out_shape=shape, in_specs=[VM], out_specs=ANY, scratch_shapes=scratch,
                              name='chunk_' + name)(dw)
    return pl.pallas_call(body, out_shape=shape, in_specs=[VM, ANY], out_specs=ANY, scratch_shapes=scratch,
                          input_output_aliases={1: 0}, name='chunk_' + name)(dw, gfull)


class GradSink:
    def __init__(self, layer):
        self.layer, self.buf = layer, None

    def put(self, name, a, b, mm_name):
        (rf, cf), ax = SHARDED[name]
        r, c = _block_shape(name)
        direct = ax == 0 or (c % LANES == 0 and PERM[name] == [(0, cf)])
        if direct:
            self.buf = mm_tn_chunked(a, b, mm_name, self.layer, name, self.buf)
        else:
            self.add(name, mm(a, b, 'tn', mm_name))

    def add(self, name, dw):
        self.buf = chunk_grad(self.layer, name, dw, self.buf)


def mm_tn_chunked(a, b, mm_name, layer, wname, gfull):
    (rf, cf), ax = SHARDED[wname]
    r, c = _block_shape(wname)
    rp, nt, _ = _tiles((r, c))
    off = SH_OFF[wname]
    K, M = a.shape
    N = b.shape[1]
    assert (M, N) == (rf, cf) and rp == r
    if ax == 0:
        tn = 4 * LANES
        grid, bspec = (N // tn,), pl.BlockSpec((K, tn), lambda g: (0, g))
        ospec = pl.BlockSpec((N_DEV, 4 * r, LANES), lambda g: (0, off // (4 * r) + g, 0))
        assert off % (4 * r) == 0 and nt % 4 == 0

        def store(res, o_ref):
            for j in range(N_DEV):
                for q in range(4):
                    o_ref[j, q * r:(q + 1) * r, :] = res[j * r:(j + 1) * r, q * LANES:(q + 1) * LANES].astype(BF16)
    else:
        tn = c
        grid, bspec = (N_DEV,), pl.BlockSpec((K, tn), lambda g: (0, g))
        ospec = pl.BlockSpec((1, nt * r, LANES), lambda g: (g, off // (nt * r), 0))
        assert off % (nt * r) == 0

        def store(res, o_ref):
            for t in range(nt):
                o_ref[0, t * r:(t + 1) * r, :] = res[:, t * LANES:(t + 1) * LANES].astype(BF16)

    def body(*refs):
        a_ref, b_ref, o_ref = refs[0], refs[1], refs[-1]
        store(lax.dot_general(a_ref[...].astype(BF16), b_ref[...].astype(BF16), _TN, preferred_element_type=F32), o_ref)

    shape = jax.ShapeDtypeStruct((N_DEV, LAYER_ROWS[layer], LANES), BF16)
    aspec = pl.BlockSpec((K, M), lambda g: (0, 0))
    params = pltpu.CompilerParams(dimension_semantics=("parallel",))
    if gfull is None:
        return pl.pallas_call(body, grid=grid, in_specs=[aspec, bspec], out_specs=ospec, out_shape=shape,
                              compiler_params=params, name=mm_name)(a, b)
    return pl.pallas_call(body, grid=grid, in_specs=[aspec, bspec, ANY], out_specs=ospec, out_shape=shape,
                          input_output_aliases={2: 0}, compiler_params=params, name=mm_name)(a, b, gfull)


def pack_rep(G):
    def body(*refs):
        tiles = []
        for ref, s in zip(refs[:-1], REP_SHAPE.values()):
            rp, nt, _ = _tiles(s)
            g = ref[...]
            tiles += [_tile_value(g, t, rp) for t in range(nt)]
        rows = sum(t.shape[0] for t in tiles)
        if rows < REP_ROWS:
            tiles.append(jnp.zeros((REP_ROWS - rows, LANES), F32))
        full = jnp.concatenate(tiles, axis=0)
        for j in range(N_DEV):
            refs[-1][j] = full[j * REP_CHUNK:(j + 1) * REP_CHUNK]

    return pl.pallas_call(body, out_shape=jax.ShapeDtypeStruct((N_DEV, REP_SLOT, LANES), F32),
                          in_specs=[VM] * len(REP_SHAPE), out_specs=VM, name='pack_rep')(
                              *[G[n].reshape(s) for n, s in REP_SHAPE.items()])


def _pack_small(blocks, order, rows, width, dtype):
    flat = jnp.concatenate([blocks[n].reshape(-1).astype(dtype) for n in order])
    return jnp.pad(flat, (0, rows * width - flat.shape[0])).reshape(rows, width)


def kernel(x, pre_norm, post_norm, rel_bias, a_w_in, a_lam_re, a_lam_im, a_log_dt, a_b_re, a_b_im, a_c_re, a_c_im, a_d, a_w_glu, a_b_glu, a_w_out, b_w_in, b_sinks, b_w_out, c_w_in, c_q_norm, c_kv_norm, c_w_uq, c_w_ukv, c_w_out, d_w_in, d_ln_g, d_ln_b, d_w_s, d_b_s, d_w_out, loss_target, m_pre_norm, m_post_norm, m_rel_bias, m_a_w_in, m_a_lam_re, m_a_lam_im, m_a_log_dt, m_a_b_re, m_a_b_im, m_a_c_re, m_a_c_im, m_a_d, m_a_w_glu, m_a_b_glu, m_a_w_out, m_b_w_in, m_b_sinks, m_b_w_out, m_c_w_in, m_c_q_norm, m_c_kv_norm, m_c_w_uq, m_c_w_ukv, m_c_w_out, m_d_w_in, m_d_ln_g, m_d_ln_b, m_d_w_s, m_d_b_s, m_d_w_out, v_pre_norm, v_post_norm, v_rel_bias, v_a_w_in, v_a_lam_re, v_a_lam_im, v_a_log_dt, v_a_b_re, v_a_b_im, v_a_c_re, v_a_c_im, v_a_d, v_a_w_glu, v_a_b_glu, v_a_w_out, v_b_w_in, v_b_sinks, v_b_w_out, v_c_w_in, v_c_q_norm, v_c_kv_norm, v_c_w_uq, v_c_w_ukv, v_c_w_out, v_d_w_in, v_d_ln_g, v_d_ln_b, v_d_w_s, v_d_b_s, v_d_w_out):
    loc = locals()
    P = {n: loc[n] for n in WEIGHTS}
    M = {n: loc['m_' + n] for n in WEIGHTS}
    V = {n: loc['v_' + n] for n in WEIGHTS}
    xs = x[0]
    L = xs.shape[0]

    blocks = {n: P[n].reshape(_block_shape(n)) for n in SHARDED}
    packed = {layer: pack_layer(layer, P) for layer in LAYER_PARAMS}
    W = {}

    def assemble_layer(layer, gathered):
        for n in LAYER_PARAMS[layer]:
            if n not in SHARDED_F32:
                W[n] = assemble(n, gathered)

    assemble_layer('a', all_gather(packed['a'], 'ag_a'))
    small = all_gather(_pack_small(blocks, SHARDED_F32, SMALL_ROWS, 128, F32), 'ag_small')
    Pl = dict(P)
    for n in SHARDED_F32:
        c = SHARDED[n][0][1]
        bc = c // N_DEV
        Pl[n] = small.reshape(N_DEV, -1)[:, SMALL_OFF[n]:SMALL_OFF[n] + bc].reshape(1, c)
    cx, cy, cc = _coords()
    core = jnp.reshape(cc, (1,)).astype(jnp.int32)
    chip = jnp.reshape(2 * cx + cy, (1,)).astype(jnp.int32)

    def pair_sums(gfull, tag):
        return rs_pair_add(gfull, rs_sibling(gfull, tag), core, tag)

    fwd = [layer_a_fwd, layer_b_fwd, layer_c_fwd, layer_d_fwd]
    bwd = [layer_a_bwd, layer_b_bwd, layer_c_bwd, layer_d_bwd]
    saved = []
    xc = xs

    def fpre(x_, g_):
        return [rms_fwd(x_, g_)], []
    (h,), _ = rowwise(fpre, [rw(xc)], [P['pre_norm'][0:1]], [(D_MODEL, BF16)], [], 256, 'pre_norm0')
    for i in range(4):
        if i == 0:
            yb, sv = fwd[i](h, W, Pl, comm=Both(AllGather(packed['b']), AllGather(packed['c'])))
            assemble_layer('b', sv['carried'][0])
            assemble_layer('c', sv['carried'][1])
        elif i == 1:
            yb, sv = fwd[i](h, W, Pl, comm=AllGather(packed['d']))
            assemble_layer('d', sv['carried'][0])
        else:
            yb, sv = fwd[i](h, W, Pl)

        sv['x'], sv['yb'] = xc, yb
        saved.append(sv)
        if i < 3:

            def fpost(x_, y_, gpost, gpre):
                xn_ = x_ + rms_fwd(y_, gpost)
                return [xn_, rms_fwd(xn_, gpre)], []
            (xc, h), _ = rowwise(fpost, [rw(xc), rw(yb)], [P['post_norm'][i:i + 1], P['pre_norm'][i + 1:i + 2]],
                                 [(D_MODEL, F32), (D_MODEL, BF16)], [], 256, f'post_pre_norm{i}')
        else:

            def floss(x_, y_, t_, gpost):
                d = x_ + rms_fwd(y_, gpost) - t_
                return [d * (1.0 / D_MODEL)], [0.5 * jnp.sum(jnp.sum(d * d, axis=-1, keepdims=True) * (1.0 / D_MODEL),
                                                             axis=0, keepdims=True)]
            (dx,), (loss_loc,) = rowwise(floss, [rw(xc), rw(yb), rw(loss_target[0])], [P['post_norm'][i:i + 1]],
                                         [(D_MODEL, F32)], [(1, 1)], 256, 'post_norm_loss')
    loss = lax.psum(loss_loc[0, 0], ("x", "y", "c"))

    G, out = {}, {}
    dpre, dpost = [None] * 4, [None] * 4

    def adam_layer(layer, part, land2):
        for n in LAYER_PARAMS[layer]:
            s = _block_shape(n)
            out[n] = adam_param(n, s, SH_OFF[n], P[n], M[n], V[n], chip, part=part, land=land2)

    def fpost_b(y_, d_, g_):
        dy, dg = rms_bwd(y_, g_, d_)
        return [dy], [dg]
    (dyb,), (dpost[3],) = rowwise(fpost_b, [rw(saved[3]['yb']), rw(dx)], [P['post_norm'][3:4]], [(D_MODEL, BF16)],
                                  [(1, D_MODEL)], 256, 'post_norm_bwd3')
    pending = None
    for i in reversed(range(4)):
        sv = saved[i]
        layer = 'abcd'[i]
        sink = GradSink(layer)
        if pending is None:
            dh, g = bwd[i](dyb, W, Pl, sv, sink=sink)
        else:
            dh, g = bwd[i](dyb, W, Pl, sv, comm=ChipExchange(pending[1]), sink=sink)
            adam_layer(pending[0], pending[1], g['carried'][0])
        g.pop('carried', None)
        G.update(g)

        if i > 0:

            def fpre_b(x_, dh_, d_, y_, gpre, gpost):
                dxl, dg = rms_bwd(x_, gpre, dh_)
                dy, dgp = rms_bwd(y_, gpost, d_ + dxl)
                return [d_ + dxl, dy], [dg, dgp]
            (dx, dyb), (dpre[i], dpost[i - 1]) = rowwise(
                fpre_b, [rw(sv['x']), rw(dh), rw(dx), rw(saved[i - 1]['yb'])],
                [P['pre_norm'][i:i + 1], P['post_norm'][i - 1:i]], [(D_MODEL, F32), (D_MODEL, BF16)],
                [(1, D_MODEL), (1, D_MODEL)], 256, f'pre_post_norm_bwd{i}')
        else:

            def fpre_b0(x_, dh_, d_, g_):
                dxl, dg = rms_bwd(x_, g_, dh_)
                return [d_ + dxl], [dg]
            (dx,), (dpre[i],) = rowwise(fpre_b0, [rw(sv['x']), rw(dh), rw(dx)], [P['pre_norm'][i:i + 1]],
                                        [(D_MODEL, F32)], [(1, D_MODEL)], 256, 'pre_norm_bwd0')

        for n in LAYER_PARAMS[layer]:
            if n in g:
                sink.add(n, g[n])
        pending = (layer, pair_sums(sink.buf, layer))
    adam_layer(pending[0], pending[1], rs_chips(pending[1], pending[0]))
    G['pre_norm'] = jnp.concatenate(dpre, axis=0)
    G['post_norm'] = jnp.concatenate(dpost, axis=0)

    part = pair_sums(pack_rep(G), 'rep')
    land2 = rs_chips(part, 'rep')
    grep = all_gather(rs_rep_sum(part, land2, chip), 'ag_rep')[:, :REP_CHUNK].reshape(REP_ROWS, LANES)
    small_names = [n for n, s in REP_SHAPE.items() if s[0] <= 64]
    out.update(adam_small(small_names, grep, P, M, V))
    for n, s in REP_SHAPE.items():
        if n not in small_names:
            out[n] = adam_param(n, s, REP_OFF[n], P[n], M[n], V[n], chip, grep=grep)
    res = [loss, dx[None]]
    for kind in range(4):
        res += [out[n][kind].reshape(P[n].shape) for n in WEIGHTS]
    return tuple(res)
```

```python
import functools
import math

import numpy as np
import jax
import jax.numpy as jnp
from jax import lax
from jax.experimental import pallas as pl
from jax.experimental.pallas import tpu as pltpu

F32 = jnp.float32
BF16 = jnp.bfloat16
MESH = pl.DeviceIdType.MESH
ANY = pl.BlockSpec(memory_space=pl.ANY)

N_DEV = 8
D_MODEL = 1024
EPS = 1e-6
NEG_INF = -1e30
SSM_G, SSM_P, SSM_H = 64, 64, 16
SSM_T = 256
SSM_TS = 8
SSM_WC = 512
HEAD_DIM = 64
SWA_HEADS, SWA_KV = 16, 2
WINDOW = 128
REL_BUCKETS, REL_MAX_DIST = 32, 128
MLA_HEADS, MLA_NOPE, MLA_ROPE, MLA_V = 16, 64, 32, 64
MLA_Q_RANK, MLA_KV_RANK = 768, 256
ROPE_BASE = 10000.0
SGU_G, SGU_C, SGU_T = 16, 64, 128
ADAM_LR, ADAM_B1, ADAM_B2, ADAM_EPS, ADAM_WD, ADAM_STEP = 0.001, 0.9, 0.999, 1e-08, 0.01, 10

WEIGHTS = ['pre_norm', 'post_norm', 'rel_bias', 'a_w_in', 'a_lam_re', 'a_lam_im', 'a_log_dt', 'a_b_re', 'a_b_im',
           'a_c_re', 'a_c_im', 'a_d', 'a_w_glu', 'a_b_glu', 'a_w_out', 'b_w_in', 'b_sinks', 'b_w_out', 'c_w_in',
           'c_q_norm', 'c_kv_norm', 'c_w_uq', 'c_w_ukv', 'c_w_out', 'd_w_in', 'd_ln_g', 'd_ln_b', 'd_w_s', 'd_b_s',
           'd_w_out']
SHARDED = {'a_w_in': ((1024, 2048), 1), 'a_w_glu': ((1024, 1024), 0), 'a_w_out': ((1024, 1024), 0),
           'b_w_in': ((1024, 2304), 1), 'b_w_out': ((1024, 1024), 0), 'c_w_in': ((1024, 2080), 1),
           'c_q_norm': ((1, 768), 1), 'c_kv_norm': ((1, 256), 1), 'c_w_uq': ((768, 1536), 1),
           'c_w_ukv': ((256, 2048), 1), 'c_w_out': ((1024, 1024), 0), 'd_w_in': ((1024, 3072), 1),
           'd_ln_g': ((1, 1024), 1), 'd_ln_b': ((1, 1024), 1), 'd_w_out': ((1024, 1024), 0)}
SHARDED_F32 = ['c_q_norm', 'c_kv_norm', 'd_ln_g', 'd_ln_b']
REPLICATED = [n for n in WEIGHTS if n not in SHARDED]


def _cdiv(a, b):
    return -(-a // b)


def _block_shape(name):
    (r, c), ax = SHARDED[name]
    return (r // N_DEV, c) if ax == 0 else (r, c // N_DEV)


LANES = 128
LAYER_PARAMS = {'a1': ['a_w_in'], 'a2': ['a_w_glu', 'a_w_out'], 'b': ['b_w_in', 'b_w_out'],
                'c': ['c_w_in', 'c_w_uq', 'c_w_ukv', 'c_w_out', 'c_q_norm', 'c_kv_norm'],
                'd': ['d_w_in', 'd_w_out', 'd_ln_g', 'd_ln_b']}


def _tiles(shape):
    r, c = shape
    rp = max(r, 16)
    rb = 512 if rp % 512 == 0 else 256 if rp % 256 == 0 else rp
    return rp, _cdiv(c, LANES), rb


SH_OFF, LAYER_ROWS = {}, {}
for _l, _names in LAYER_PARAMS.items():
    _o = 0
    for _n in _names:
        _rp, _nt, _rb = _tiles(_block_shape(_n))
        assert _o % _rb == 0
        SH_OFF[_n] = _o
        _o += _rp * _nt
    assert _o % 16 == 0
    LAYER_ROWS[_l] = _o
GROUP_OF = {_n: _l for _l, _names in LAYER_PARAMS.items() for _n in _names}
LAYER_GROUPS = {'a': ['a1', 'a2'], 'b': ['b'], 'c': ['c'], 'd': ['d']}

REP_SHAPE = {'a_b_re': (4096, 16), 'a_b_im': (4096, 16), 'd_w_s': (2048, 128), 'a_c_re': (1024, 64),
             'a_c_im': (1024, 64), 'pre_norm': (4, 1024), 'post_norm': (4, 1024), 'a_lam_re': (64, 64),
             'a_lam_im': (64, 64), 'a_d': (1, 1024), 'a_b_glu': (1, 1024), 'rel_bias': (32, 16), 'd_b_s': (16, 128),
             'a_log_dt': (1, 64), 'b_sinks': (1, 16)}
REP_OFF = {}
_o = 0
for _n, _s in REP_SHAPE.items():
    _rp, _nt, _rb = _tiles(_s)
    assert _o % _rb == 0
    REP_OFF[_n] = _o
    _o += _rp * _nt
REP_ROWS = _cdiv(_o, 16 * N_DEV) * 16 * N_DEV
REP_CHUNK = REP_ROWS // N_DEV
REP_SLOT = REP_CHUNK

PERM = {'a_w_in': [(0, 2048)], 'd_w_in': [(0, 3072)], 'b_w_in': [(1280, 1024), (0, 1280)],
        'c_w_in': [(1056, 1024), (0, 1056), ('z', 96)],
        'c_w_uq': sum([[(2 * hp * 96, 64), ((2 * hp + 1) * 96, 64), (2 * hp * 96 + 64, 32), ((2 * hp + 1) * 96 + 64, 32),
                        ('z', 64)] for hp in range(8)], []),
        'c_w_ukv': sum([[(2 * hp * 128, 64), ((2 * hp + 1) * 128, 64), (2 * hp * 128 + 64, 64),
                         ((2 * hp + 1) * 128 + 64, 64)] for hp in range(8)], [])}


def perm_index(name):
    return np.concatenate([np.full(p[1], -1) if p[0] == 'z' else np.arange(p[0], p[0] + p[1]) for p in PERM[name]])


SMALL_OFF = {}
_o = 0
for _n in SHARDED_F32:
    SMALL_OFF[_n] = _o
    _o += int(np.prod(_block_shape(_n)))
SMALL_ROWS = _cdiv(_o, 128 * 8) * 8


def _pick(n, cands):
    for c in cands:
        if n % c == 0:
            return c
    return n


def mm(a, b, mode, name, out_dtype=F32):
    if mode == 'nn':
        (M, K), (K2, N) = a.shape, b.shape
    elif mode == 'nt':
        (M, K), (N, K2) = a.shape, b.shape
    else:
        (K, M), (K2, N) = a.shape, b.shape
    assert K == K2, (name, a.shape, b.shape)
    tm = _pick(M, (1024, 768, 512, 256, 128))
    tn = _pick(N, (512, 384, 256))
    dims = {'nn': ((1,), (0,)), 'nt': ((1,), (1,)), 'tn': ((0,), (0,))}[mode]

    def body(a_ref, b_ref, o_ref):
        o_ref[...] = lax.dot_general(a_ref[...].astype(BF16), b_ref[...].astype(BF16), (dims, ((), ())),
                                     preferred_element_type=F32).astype(out_dtype)

    a_spec = pl.BlockSpec((K, tm), lambda i, j: (0, i)) if mode == 'tn' else pl.BlockSpec((tm, K), lambda i, j: (i, 0))
    b_spec = pl.BlockSpec((tn, K), lambda i, j: (j, 0)) if mode == 'nt' else pl.BlockSpec((K, tn), lambda i, j: (0, j))
    return pl.pallas_call(
        body, grid=(M // tm, N // tn), in_specs=[a_spec, b_spec],
        out_specs=pl.BlockSpec((tm, tn), lambda i, j: (i, j)), out_shape=jax.ShapeDtypeStruct((M, N), out_dtype),
        compiler_params=pltpu.CompilerParams(dimension_semantics=("parallel", "parallel")), name=name)(a, b)


def rw(arr, width=None, cb=0):
    return (arr, arr.shape[1] if width is None else width, cb)


def rowwise(fn, rows, consts, outs, accs, tl, name, n_steps=None):
    if n_steps is None:
        n_steps = [r[0].shape[0] for r in rows if not isinstance(r[1], pl.BlockSpec)][0] // tl
    L = n_steps * tl
    nr, nc, no, na = len(rows), len(consts), len(outs), len(accs)
    in_specs, args = [], []
    for r in rows:
        if isinstance(r[1], pl.BlockSpec):
            in_specs.append(r[1])
        else:
            in_specs.append(pl.BlockSpec((tl, r[1]), functools.partial(lambda i, cb: (i, cb), cb=r[2])))
        args.append(r[0])
    for c in consts:
        in_specs.append(pl.BlockSpec(c.shape, functools.partial(lambda i, nd: (0,) * nd, nd=c.ndim)))
        args.append(c)
    out_specs = [pl.BlockSpec((tl, w), lambda i: (i, 0)) for w, _ in outs]
    out_shape = [jax.ShapeDtypeStruct((L, w), dt) for w, dt in outs]
    for s in accs:
        out_specs.append(pl.BlockSpec(s, functools.partial(lambda i, nd: (0,) * nd, nd=len(s))))
        out_shape.append(jax.ShapeDtypeStruct(s, F32))

    def body(*refs):
        ins = [r[...] for r in refs[:nr + nc]]
        o_refs = refs[nr + nc:nr + nc + no]
        a_refs = refs[nr + nc + no:]
        o_vals, a_vals = fn(*ins)
        for ref, val in zip(o_refs, o_vals):
            ref[...] = val.astype(ref.dtype)
        if na:
            @pl.when(pl.program_id(0) == 0)
            def _():
                for ref in a_refs:
                    ref[...] = jnp.zeros_like(ref)
            for ref, val in zip(a_refs, a_vals):
                ref[...] += val

    res = pl.pallas_call(
        body, grid=(n_steps,), in_specs=in_specs, out_specs=out_specs, out_shape=out_shape,
        compiler_params=pltpu.CompilerParams(dimension_semantics=("arbitrary",)), name=name)(*args)
    return res[:no], res[no:]


def carried(body, comm, *, grid, in_specs, out_specs, out_shape, name, semantics, scratch_shapes=()):
    single = not isinstance(out_shape, (list, tuple))
    o_specs = [out_specs] if single else list(out_specs)
    o_shape = [out_shape] if single else list(out_shape)
    if comm is None:
        call = pl.pallas_call(body, grid=grid, in_specs=in_specs, out_specs=out_specs, out_shape=out_shape,
                              scratch_shapes=list(scratch_shapes),
                              compiler_params=pltpu.CompilerParams(dimension_semantics=semantics), name=name)
        return lambda *args: (call(*args), None)
    n_in, n_out, n_sc = len(in_specs), len(o_specs), len(scratch_shapes)
    ci, co = len(comm.ins), len(comm.outs)
    n_steps = int(np.prod(grid))
    hooks = comm.hooks(n_steps)

    def wrapped(*refs):
        ins, cins = refs[:n_in], refs[n_in:n_in + ci]
        outs, couts = refs[n_in + ci:n_in + ci + n_out], refs[n_in + ci + n_out:n_in + ci + n_out + co]
        sc, csc = refs[n_in + ci + n_out + co:n_in + ci + n_out + co + n_sc], refs[n_in + ci + n_out + co + n_sc:]
        step = pl.program_id(0)
        for ax in range(1, len(grid)):
            step = step * grid[ax] + pl.program_id(ax)
        for at, fn, after in hooks:
            if not after:
                pl.when(step == at)(functools.partial(fn, cins, couts, csc))
        body(*ins, *outs, *sc)
        for at, fn, after in hooks:
            if after:
                pl.when(step == at)(functools.partial(fn, cins, couts, csc))

    call = pl.pallas_call(wrapped, grid=grid, in_specs=list(in_specs) + [ANY] * ci, out_specs=o_specs + [ANY] * co,
                          out_shape=o_shape + list(comm.outs), scratch_shapes=list(scratch_shapes) + list(comm.scratch),
                          compiler_params=pltpu.CompilerParams(dimension_semantics=("arbitrary",) * len(grid)), name=name)

    def run(*args):
        res = call(*args, *comm.ins)
        return (res[0] if single else res[:n_out]), res[n_out:]
    return run


_K0 = math.sqrt(2.0 / math.pi)
_K1 = 0.044715


def gelu(x):
    return x * (0.5 * (1.0 + jnp.tanh(_K0 * (x + _K1 * (x * x * x)))))


def gelu_grad(x):
    t = jnp.tanh(_K0 * (x + _K1 * (x * x * x)))
    return 0.5 * (1.0 + t) + 0.5 * x * (1.0 - t * t) * (_K0 * (1.0 + 3.0 * _K1 * x * x))


def sigmoid(x):
    return 1.0 / (1.0 + jnp.exp(-x))


def silu(z):
    return z * sigmoid(z)


def silu_grad(z):
    s = sigmoid(z)
    return s * (1.0 + z * (1.0 - s))


def rms_fwd(x, g):
    r = lax.rsqrt(jnp.mean(x * x, axis=-1, keepdims=True) + EPS)
    return x * r * g


def rms_bwd(x, g, dy):
    r = lax.rsqrt(jnp.mean(x * x, axis=-1, keepdims=True) + EPS)
    xh = x * r
    dg = jnp.sum(dy * xh, axis=0, keepdims=True)
    dxh = dy * g
    dx = r * (dxh - xh * jnp.mean(dxh * xh, axis=-1, keepdims=True))
    return dx, dg


def _scan_chunk(a_r, a_i, pr_ref, pi_ref, cr, ci, T, reverse):
    ts = min(SSM_TS, T)
    sgn = -1.0 if reverse else 1.0
    row = lax.broadcasted_iota(jnp.int32, (ts, a_r.shape[1]), 0)
    pw = (lambda e: T - e) if reverse else (lambda e: e - 1)
    if reverse:
        wr_c, wi_c = pr_ref[T - ts:T, :], sgn * pi_ref[T - ts:T, :]
    else:
        wr_c, wi_c = pr_ref[0:ts, :], sgn * pi_ref[0:ts, :]
    c_r, c_i = cr[...], ci[...]
    outs = []
    subs = range(T // ts)
    for sub in (reversed(subs) if reverse else subs):
        v_r, v_i = a_r[sub * ts:(sub + 1) * ts], a_i[sub * ts:(sub + 1) * ts]
        d = 1
        while d < ts:
            wr = pr_ref[pw(d):pw(d) + 1, :]
            wi = sgn * pi_ref[pw(d):pw(d) + 1, :]
            if reverse:
                yr, yi, keep = pltpu.roll(v_r, ts - d, 0), pltpu.roll(v_i, ts - d, 0), row < ts - d
            else:
                yr, yi, keep = pltpu.roll(v_r, d, 0), pltpu.roll(v_i, d, 0), row >= d
            v_r, v_i = (v_r + jnp.where(keep, wr * yr - wi * yi, 0.0), v_i + jnp.where(keep, wr * yi + wi * yr, 0.0))
            d *= 2
        v_r, v_i = v_r + (wr_c * c_r - wi_c * c_i), v_i + (wr_c * c_i + wi_c * c_r)
        k = 0 if reverse else ts - 1
        c_r, c_i = v_r[k:k + 1, :], v_i[k:k + 1, :]
        outs.append((v_r, v_i))
    if reverse:
        outs = outs[::-1]
    cr[...] = c_r
    ci[...] = c_i
    return jnp.concatenate([o[0] for o in outs], axis=0), jnp.concatenate([o[1] for o in outs], axis=0)


_NT = (((1,), (1,)), ((), ()))
_TN = (((0,), (0,)), ((), ()))


def s5_fwd(proj, d_skip, Bre, Bim, Cre, Cim, pr, pi, comm=None):
    L = proj.shape[0]
    T, WC = min(SSM_T, L), SSM_WC
    nT = L // T

    def body(u_ref, d_ref, bre_ref, bim_ref, cre_ref, cim_ref, pr_ref, pi_ref, y_ref, yg_ref, sr_ref, si_ref, cr, ci):
        @pl.when(pl.program_id(1) == 0)
        def _():
            cr[...] = jnp.zeros_like(cr)
            ci[...] = jnp.zeros_like(ci)

        u = u_ref[...]
        ub = u.astype(BF16)
        a_r = jnp.dot(ub, bre_ref[0].astype(BF16), preferred_element_type=F32)
        a_i = jnp.dot(ub, bim_ref[0].astype(BF16), preferred_element_type=F32)
        a_r, a_i = _scan_chunk(a_r, a_i, pr_ref, pi_ref, cr, ci, T, False)
        sr_ref[...] = a_r
        si_ref[...] = a_i
        y = (jnp.dot(a_r.astype(BF16), cre_ref[0].astype(BF16), preferred_element_type=F32)
             + jnp.dot(a_i.astype(BF16), cim_ref[0].astype(BF16), preferred_element_type=F32) + d_ref[...] * u)
        y_ref[...] = y
        yg_ref[...] = gelu(y)

    uspec = pl.BlockSpec((T, 128), lambda k, i: (i, k))
    sspec = pl.BlockSpec((T, WC), lambda k, i: (i, k))
    return carried(
        body, comm, grid=(8, nT),
        in_specs=[uspec, pl.BlockSpec((1, 128), lambda k, i: (0, k)),
                  pl.BlockSpec((1, 128, WC), lambda k, i: (k, 0, 0)), pl.BlockSpec((1, 128, WC), lambda k, i: (k, 0, 0)),
                  pl.BlockSpec((1, WC, 128), lambda k, i: (k, 0, 0)), pl.BlockSpec((1, WC, 128), lambda k, i: (k, 0, 0)),
                  pl.BlockSpec((T, WC), lambda k, i: (0, k)), pl.BlockSpec((T, WC), lambda k, i: (0, k))],
        out_specs=[uspec, uspec, sspec, sspec],
        out_shape=[jax.ShapeDtypeStruct((L, 1024), F32)] * 2 + [jax.ShapeDtypeStruct((L, 8 * WC), F32)] * 2,
        scratch_shapes=[pltpu.VMEM((1, WC), F32), pltpu.VMEM((1, WC), F32)],
        semantics=("parallel", "arbitrary"), name='a_ssm')(proj, d_skip, Bre, Bim, Cre, Cim, pr, pi)


def s5_bwd(proj, dyg1, dyg2, y, d_skip, s_re, s_im, Bre, Bim, Cre, Cim, prr, pir, comm=None):
    L = proj.shape[0]
    T, WC = min(SSM_T, L), SSM_WC
    nT = L // T

    def body(u_ref, g1_ref, g2_ref, y_ref, d_ref, sr_ref, si_ref, spr_ref, spi_ref, bre_ref, bim_ref, cre_ref, cim_ref,
             pr_ref, pi_ref, du_ref, dd_ref, dbre_ref, dbim_ref, dcre_ref, dcim_ref, dar_ref, dai_ref, cr, ci):
        i = pl.program_id(1)

        @pl.when(i == 0)
        def _():
            for ref in (cr, ci, dd_ref, dbre_ref, dbim_ref, dcre_ref, dcim_ref, dar_ref, dai_ref):
                ref[...] = jnp.zeros_like(ref)

        u = u_ref[...]
        dy = (g1_ref[...] + g2_ref[...]) * gelu_grad(y_ref[...])
        dd_ref[...] += jnp.sum(dy * u, axis=0, keepdims=True)
        dyb, ub = dy.astype(BF16), u.astype(BF16)
        bre, bim, cre, cim = (r[0].astype(BF16) for r in (bre_ref, bim_ref, cre_ref, cim_ref))
        g_r = lax.dot_general(dyb, cre, _NT, preferred_element_type=F32)
        g_i = lax.dot_general(dyb, cim, _NT, preferred_element_type=F32)
        g_r, g_i = _scan_chunk(g_r, g_i, pr_ref, pi_ref, cr, ci, T, True)
        s_r, s_i = sr_ref[...], si_ref[...]
        row = lax.broadcasted_iota(jnp.int32, (T, WC), 0)
        first = (nT - 1 - i) == 0
        sp_r = jnp.where(row == 0, jnp.where(first, 0.0, spr_ref[7:8, :]), pltpu.roll(s_r, 1, 0))
        sp_i = jnp.where(row == 0, jnp.where(first, 0.0, spi_ref[7:8, :]), pltpu.roll(s_i, 1, 0))
        dar_ref[...] += jnp.sum(g_r * sp_r + g_i * sp_i, axis=0, keepdims=True)
        dai_ref[...] += jnp.sum(g_i * sp_r - g_r * sp_i, axis=0, keepdims=True)
        grb, gib = g_r.astype(BF16), g_i.astype(BF16)
        dcre_ref[0] += lax.dot_general(s_r.astype(BF16), dyb, _TN, preferred_element_type=F32)
        dcim_ref[0] += lax.dot_general(s_i.astype(BF16), dyb, _TN, preferred_element_type=F32)
        dbre_ref[0] += lax.dot_general(ub, grb, _TN, preferred_element_type=F32)
        dbim_ref[0] += lax.dot_general(ub, gib, _TN, preferred_element_type=F32)
        du_ref[...] = (dy * d_ref[...] + lax.dot_general(grb, bre, _NT, preferred_element_type=F32)
                       + lax.dot_general(gib, bim, _NT, preferred_element_type=F32))

    uspec = pl.BlockSpec((T, 128), lambda k, i: (nT - 1 - i, k))
    sspec = pl.BlockSpec((T, WC), lambda k, i: (nT - 1 - i, k))
    pspec = pl.BlockSpec((8, WC), lambda k, i: (jnp.maximum((nT - 1 - i) * (T // 8) - 1, 0), k))
    tab = pl.BlockSpec((T, WC), lambda k, i: (0, k))
    bspec = pl.BlockSpec((1, 128, WC), lambda k, i: (k, 0, 0))
    cspec = pl.BlockSpec((1, WC, 128), lambda k, i: (k, 0, 0))
    return carried(
        body, comm, grid=(8, nT),
        in_specs=[uspec, uspec, uspec, uspec, pl.BlockSpec((1, 128), lambda k, i: (0, k)), sspec, sspec, pspec, pspec,
                  bspec, bspec, cspec, cspec, tab, tab],
        out_specs=[uspec, pl.BlockSpec((1, 128), lambda k, i: (0, k)), bspec, bspec, cspec, cspec,
                   pl.BlockSpec((1, WC), lambda k, i: (0, k)), pl.BlockSpec((1, WC), lambda k, i: (0, k))],
        out_shape=[jax.ShapeDtypeStruct((L, 1024), F32), jax.ShapeDtypeStruct((1, 1024), F32),
                   jax.ShapeDtypeStruct((8, 128, WC), F32), jax.ShapeDtypeStruct((8, 128, WC), F32),
                   jax.ShapeDtypeStruct((8, WC, 128), F32), jax.ShapeDtypeStruct((8, WC, 128), F32),
                   jax.ShapeDtypeStruct((1, 8 * WC), F32), jax.ShapeDtypeStruct((1, 8 * WC), F32)],
        scratch_shapes=[pltpu.VMEM((1, WC), F32), pltpu.VMEM((1, WC), F32)],
        semantics=("parallel", "arbitrary"), name='a_ssm_bwd')(
            proj, dyg1, dyg2, y, d_skip, s_re, s_im, s_re, s_im, Bre, Bim, Cre, Cim, prr, pir)


def s5_discretize(lam_re, lam_im, log_dt, b_re, b_im):
    dt = jnp.exp(log_dt)[:, None]
    mag = jnp.exp(lam_re * dt)
    ab_re = mag * jnp.cos(lam_im * dt)
    ab_im = mag * jnp.sin(lam_im * dt)
    den = lam_re * lam_re + lam_im * lam_im
    nr = ab_re - 1.0
    f_re = (nr * lam_re + ab_im * lam_im) / den
    f_im = (ab_im * lam_re - nr * lam_im) / den
    bb_re = f_re[..., None] * b_re - f_im[..., None] * b_im
    bb_im = f_re[..., None] * b_im + f_im[..., None] * b_re
    return ab_re, ab_im, bb_re, bb_im


_EYE8 = np.eye(8, dtype=np.float32)


def _b_tiles(bb):
    t = bb.transpose(0, 2, 1).reshape(8, 8, SSM_H, SSM_P)
    return jnp.einsum('kghp,gG->kghGp', t, _EYE8).reshape(8, 8 * SSM_H, 8 * SSM_P)


def _b_untile(d):
    t = jnp.einsum('kghGp,gG->kghp', d.reshape(8, 8, SSM_H, 8, SSM_P), _EYE8)
    return t.reshape(SSM_G, SSM_H, SSM_P).transpose(0, 2, 1)


def _c_tiles(c):
    t = c.transpose(0, 2, 1).reshape(8, 8, SSM_P, SSM_H)
    return jnp.einsum('kgph,gG->kgpGh', t, _EYE8).reshape(8, 8 * SSM_P, 8 * SSM_H)


def _c_untile(d):
    t = jnp.einsum('kgpGh,gG->kgph', d.reshape(8, 8, SSM_P, 8, SSM_H), _EYE8)
    return t.reshape(SSM_G, SSM_P, SSM_H).transpose(0, 2, 1)


def s5_powers(ar, ai, T):
    W = ar.shape[1]

    def body(ar_ref, ai_ref, fr_ref, fi_ref, rr_ref, ri_ref):
        fr_ref[0:1, :] = ar_ref[...]
        fi_ref[0:1, :] = ai_ref[...]
        rr_ref[T - 1:T, :] = ar_ref[...]
        ri_ref[T - 1:T, :] = ai_ref[...]
        n = 1
        while n < T:
            cr, ci = fr_ref[0:n, :], fi_ref[0:n, :]
            lr, li = fr_ref[n - 1:n, :], fi_ref[n - 1:n, :]
            fr_ref[n:2 * n, :] = cr * lr - ci * li
            fi_ref[n:2 * n, :] = cr * li + ci * lr
            cr, ci = rr_ref[T - n:T, :], ri_ref[T - n:T, :]
            rr_ref[T - 2 * n:T - n, :] = cr * lr - ci * li
            ri_ref[T - 2 * n:T - n, :] = cr * li + ci * lr
            n *= 2

    spec = pl.BlockSpec((T, SSM_WC), lambda j: (0, j))
    aspec = pl.BlockSpec((1, SSM_WC), lambda j: (0, j))
    return pl.pallas_call(
        body, grid=(W // SSM_WC,), in_specs=[aspec, aspec], out_specs=[spec] * 4,
        out_shape=[jax.ShapeDtypeStruct((T, W), F32)] * 4,
        compiler_params=pltpu.CompilerParams(dimension_semantics=("parallel",)), name='a_powers')(ar, ai)


def layer_a_fwd(h, w, p, comm=None, on_carried=None):
    L = h.shape[0]
    proj = mm(h, w['a_w_in'], 'nn', 'a_proj')
    disc = lambda *a: s5_discretize(*a)
    (ab_re, ab_im, bb_re, bb_im), disc_vjp = jax.vjp(disc, p['a_lam_re'][0], p['a_lam_im'][0], p['a_log_dt'][0],
                                                     p['a_b_re'][0], p['a_b_im'][0])
    Bre, Bim = _b_tiles(bb_re), _b_tiles(bb_im)
    Cre, Cim = _c_tiles(p['a_c_re'][0]), -_c_tiles(p['a_c_im'][0])
    T = min(SSM_T, L)
    pr, pi, prr, pir = s5_powers(ab_re.reshape(1, -1), ab_im.reshape(1, -1), T)
    (y, yg, s_re, s_im), carried_out = s5_fwd(proj, p['a_d'], Bre, Bim, Cre, Cim, pr, pi, comm=comm)
    if on_carried is not None:
        on_carried(carried_out)
    gl = mm(yg, w['a_w_glu'], 'nn', 'a_glu')

    def f2(yg_, gl_, z, bg):
        return [yg_ * sigmoid(gl_ + bg) * silu(z)], []
    (po,), _ = rowwise(f2, [rw(yg), rw(gl), rw(proj, 1024, 1)], [p['a_b_glu']], [(1024, BF16)], [], 256, 'a_gate')
    yb = mm(po, w['a_w_out'], 'nn', 'a_out')
    saved = dict(carried=carried_out, h=h, proj=proj, disc_vjp=disc_vjp, Bre=Bre, Bim=Bim, Cre=Cre, Cim=Cim, prr=prr, pir=pir, s_re=s_re,
                 s_im=s_im, y=y, yg=yg, gl=gl, po=po)
    return yb, saved


def _dw(g, sink, name, a, b, mm_name):
    if sink is None:
        g[name] = mm(a, b, 'tn', mm_name)
    else:
        sink.put(name, a, b, mm_name)


def layer_a_bwd(dyb, w, p, sv, comm=None, sink=None):
    g = {}
    dpo = mm(dyb, w['a_w_out'], 'nt', 'a_dpo')
    _dw(g, sink, 'a_w_out', sv['po'], dyb, 'a_dwout')
    proj = sv['proj']

    def f1(dpo_, yg, gl, z, bg):
        sg = sigmoid(gl + bg)
        sz = silu(z)
        dm = dpo_ * sz
        dz = dpo_ * (yg * sg) * silu_grad(z)
        dgl = dm * yg * sg * (1.0 - sg)
        return [dz, dm * sg, dgl], [jnp.sum(dgl, axis=0, keepdims=True)]
    (dz, dyg1, dgl), (db_glu,) = rowwise(f1, [rw(dpo), rw(sv['yg']), rw(sv['gl']), rw(proj, 1024, 1)], [p['a_b_glu']],
                                          [(1024, F32), (1024, F32), (1024, BF16)], [(1, 1024)], 256, 'a_gate_bwd')
    g['a_b_glu'] = db_glu
    _dw(g, sink, 'a_w_glu', sv['yg'], dgl, 'a_dwglu')
    dyg2 = mm(dgl, w['a_w_glu'], 'nt', 'a_dyg2')

    if callable(comm):
        comm = comm()
    (du, dd, dBre, dBim, dCre, dCim, da_re, da_im), g['carried'] = s5_bwd(
        proj, dyg1, dyg2, sv['y'], p['a_d'], sv['s_re'], sv['s_im'], sv['Bre'], sv['Bim'], sv['Cre'], sv['Cim'],
        sv['prr'], sv['pir'], comm=comm)
    g['a_d'] = dd
    dCim = -dCim

    def f3(du_, dz_):
        return [jnp.concatenate([du_, dz_], axis=1)], []
    (dproj,), _ = rowwise(f3, [rw(du), rw(dz)], [], [(2048, BF16)], [], 256, 'a_dproj')
    dlr, dli, dldt, dbr, dbi = sv['disc_vjp']((da_re.reshape(SSM_G, SSM_P), da_im.reshape(SSM_G, SSM_P),
                                               _b_untile(dBre), _b_untile(dBim)))
    g['a_lam_re'], g['a_lam_im'], g['a_log_dt'] = dlr[None], dli[None], dldt[None]
    g['a_b_re'], g['a_b_im'] = dbr[None], dbi[None]
    g['a_c_re'], g['a_c_im'] = _c_untile(dCre)[None], _c_untile(dCim)[None]
    _dw(g, sink, 'a_w_in', sv['h'], dproj, 'a_dwin')
    dh = mm(dproj, w['a_w_in'], 'nt', 'a_dh')
    return dh, g


def _t5_bucket_np():
    qi = np.arange(WINDOW)[:, None]
    kj = np.arange(2 * WINDOW)[None, :]
    dist = np.maximum(qi + WINDOW - kj, 0)
    max_exact = REL_BUCKETS // 2
    dist_f = np.maximum(dist, 1).astype(np.float32)
    large = max_exact + (np.log(dist_f / np.float32(max_exact)) / np.float32(math.log(REL_MAX_DIST / max_exact))
                         * np.float32(REL_BUCKETS - max_exact)).astype(np.int32)
    large = np.minimum(large, REL_BUCKETS - 1)
    return np.where(dist < max_exact, dist, large).astype(np.int32)


SWA_GRP = SWA_HEADS // SWA_KV


def _swa_kv(kvp, kvc, kvh):
    kb = jnp.concatenate([kvp[:, kvh * 64:(kvh + 1) * 64], kvc[:, kvh * 64:(kvh + 1) * 64]], 0).astype(BF16)
    vb = jnp.concatenate([kvp[:, 128 + kvh * 64:128 + (kvh + 1) * 64], kvc[:, 128 + kvh * 64:128 + (kvh + 1) * 64]],
                         0).astype(BF16)
    return kb, vb


def _swa_stack(x, kvh):
    return jnp.concatenate([x[:, (kvh * SWA_GRP + g) * 64:(kvh * SWA_GRP + g + 1) * 64] for g in range(SWA_GRP)],
                           axis=0).astype(BF16)


def _swa_group(bias_ref, kvh):
    return bias_ref[kvh * SWA_GRP:(kvh + 1) * SWA_GRP].reshape(SWA_GRP * WINDOW, 2 * WINDOW)


def _swa_sinks(sink_ref, kvh):
    return jnp.concatenate([jnp.broadcast_to(sink_ref[0:1, kvh * SWA_GRP + g:kvh * SWA_GRP + g + 1], (WINDOW, 1))
                            for g in range(SWA_GRP)], axis=0)


def _swa_probs(q, kb, bias_h, sink, valid):
    s = lax.dot_general(q, kb, (((1,), (1,)), ((), ())), preferred_element_type=F32) * (HEAD_DIM ** -0.5)
    s = jnp.where(valid, s + bias_h, NEG_INF)
    m = jnp.maximum(jnp.max(s, axis=-1, keepdims=True), sink)
    e = jnp.exp(s - m)
    es = jnp.exp(sink - m)
    den = jnp.sum(e, axis=-1, keepdims=True) + es
    return e / den, es / den


def _swa_valid(n):
    qi = lax.broadcasted_iota(jnp.int32, (SWA_GRP * WINDOW, 2 * WINDOW), 0) & (WINDOW - 1)
    kj = lax.broadcasted_iota(jnp.int32, (SWA_GRP * WINDOW, 2 * WINDOW), 1)
    dist = qi + WINDOW - kj
    return (dist >= 0) & (dist < WINDOW) & ((kj >= WINDOW) | (n > 0))


def swa_fwd(proj, bias, sinks, comm=None):
    L = proj.shape[0]

    def body(z_ref, q_ref, kvc_ref, kvp_ref, bias_ref, sink_ref, o_ref, po_ref):
        n = pl.program_id(0)
        valid = _swa_valid(n)
        q, kvc, kvp = q_ref[...], kvc_ref[...], kvp_ref[...]
        outs = []
        for kvh in range(SWA_KV):
            kb, vb = _swa_kv(kvp, kvc, kvh)
            p, _ = _swa_probs(_swa_stack(q, kvh), kb, _swa_group(bias_ref, kvh), _swa_sinks(sink_ref, kvh), valid)
            o8 = jnp.dot(p.astype(BF16), vb, preferred_element_type=F32)
            outs += [o8[g * WINDOW:(g + 1) * WINDOW] for g in range(SWA_GRP)]
        o = jnp.concatenate(outs, axis=1)
        o_ref[...] = o
        po_ref[...] = (o * silu(z_ref[...])).astype(po_ref.dtype)

    return carried(
        body, comm, grid=(L // WINDOW,),
        in_specs=[pl.BlockSpec((WINDOW, 1024), lambda n: (n, 0)), pl.BlockSpec((WINDOW, 1024), lambda n: (n, 1)),
                  pl.BlockSpec((WINDOW, 256), lambda n: (n, 8)),
                  pl.BlockSpec((WINDOW, 256), lambda n: (jnp.maximum(n - 1, 0), 8)),
                  pl.BlockSpec((SWA_HEADS, WINDOW, 2 * WINDOW), lambda n: (0, 0, 0)),
                  pl.BlockSpec((1, SWA_HEADS), lambda n: (0, 0))],
        out_specs=[pl.BlockSpec((WINDOW, 1024), lambda n: (n, 0))] * 2,
        out_shape=[jax.ShapeDtypeStruct((L, 1024), F32), jax.ShapeDtypeStruct((L, 1024), BF16)],
        semantics=("parallel",), name='b_attn')(proj, proj, proj, proj, bias, sinks)


def swa_bwd(proj, do, bias, sinks, comm=None):
    L = proj.shape[0]

    def body(q_ref, kvc_ref, kvp_ref, do_ref, bias_ref, sink_ref, dq_ref, dkv_ref, dbias_ref, dsink_ref):
        n = pl.program_id(0)

        @pl.when(n == 0)
        def _():
            dkv_ref[...] = jnp.zeros_like(dkv_ref)
            dbias_ref[...] = jnp.zeros_like(dbias_ref)
            dsink_ref[...] = jnp.zeros_like(dsink_ref)

        valid = _swa_valid(n)
        q, kvc, kvp, do_ = q_ref[...], kvc_ref[...], kvp_ref[...], do_ref[...]
        dqs, dks, dvs, dsk = [], [], [], []
        for kvh in range(SWA_KV):
            kb, vb = _swa_kv(kvp, kvc, kvh)
            q8, do8 = _swa_stack(q, kvh), _swa_stack(do_, kvh)
            p, ps = _swa_probs(q8, kb, _swa_group(bias_ref, kvh), _swa_sinks(sink_ref, kvh), valid)
            dp = lax.dot_general(do8, vb, (((1,), (1,)), ((), ())), preferred_element_type=F32)
            delta = jnp.sum(p * dp, axis=-1, keepdims=True)
            ds = p * (dp - delta)
            col = -ps * delta
            dsk += [jnp.sum(col[g * WINDOW:(g + 1) * WINDOW], axis=0, keepdims=True) for g in range(SWA_GRP)]
            dbias_ref[kvh * SWA_GRP:(kvh + 1) * SWA_GRP] += ds.reshape(SWA_GRP, WINDOW, 2 * WINDOW)
            dsb = (ds * (HEAD_DIM ** -0.5)).astype(BF16)
            dq8 = jnp.dot(dsb, kb, preferred_element_type=F32)
            dqs += [dq8[g * WINDOW:(g + 1) * WINDOW] for g in range(SWA_GRP)]
            dks.append(lax.dot_general(dsb, q8, (((0,), (0,)), ((), ())), preferred_element_type=F32))
            dvs.append(lax.dot_general(p.astype(BF16), do8, (((0,), (0,)), ((), ())), preferred_element_type=F32))
        dq_ref[...] = jnp.concatenate(dqs, axis=1)
        dsink_ref[...] += jnp.concatenate(dsk, axis=1)
        both = jnp.concatenate(dks + dvs, axis=1)
        r_cur = pl.multiple_of(n * WINDOW, WINDOW)
        r_prev = pl.multiple_of(jnp.maximum(n - 1, 0) * WINDOW, WINDOW)
        dkv_ref[pl.ds(r_prev, WINDOW), :] += both[:WINDOW]
        dkv_ref[pl.ds(r_cur, WINDOW), :] += both[WINDOW:]

    return carried(
        body, comm, grid=(L // WINDOW,),
        in_specs=[pl.BlockSpec((WINDOW, 1024), lambda n: (n, 1)), pl.BlockSpec((WINDOW, 256), lambda n: (n, 8)),
                  pl.BlockSpec((WINDOW, 256), lambda n: (jnp.maximum(n - 1, 0), 8)),
                  pl.BlockSpec((WINDOW, 1024), lambda n: (n, 0)),
                  pl.BlockSpec((SWA_HEADS, WINDOW, 2 * WINDOW), lambda n: (0, 0, 0)),
                  pl.BlockSpec((1, SWA_HEADS), lambda n: (0, 0))],
        out_specs=[pl.BlockSpec((WINDOW, 1024), lambda n: (n, 0)), pl.BlockSpec((L, 256), lambda n: (0, 0)),
                   pl.BlockSpec((SWA_HEADS, WINDOW, 2 * WINDOW), lambda n: (0, 0, 0)),
                   pl.BlockSpec((1, SWA_HEADS), lambda n: (0, 0))],
        out_shape=[jax.ShapeDtypeStruct((L, 1024), F32), jax.ShapeDtypeStruct((L, 256), F32),
                   jax.ShapeDtypeStruct((SWA_HEADS, WINDOW, 2 * WINDOW), F32), jax.ShapeDtypeStruct((1, SWA_HEADS), F32)],
        semantics=("arbitrary",), name='b_attn_bwd')(proj, proj, proj, do, bias, sinks)


def swa_bias(rel_bias):
    def body(bk_ref, rb_ref, o_ref):
        bk = bk_ref[...]
        for h in range(SWA_HEADS):
            acc = jnp.zeros((WINDOW, 2 * WINDOW), F32)
            for b in range(REL_BUCKETS):
                acc = jnp.where(bk == b, rb_ref[b, h], acc)
            o_ref[h] = acc

    return pl.pallas_call(
        body, out_shape=jax.ShapeDtypeStruct((SWA_HEADS, WINDOW, 2 * WINDOW), F32),
        in_specs=[pl.BlockSpec(memory_space=pltpu.VMEM), pl.BlockSpec(memory_space=pltpu.SMEM)],
        out_specs=pl.BlockSpec(memory_space=pltpu.VMEM), name='b_bias')(jnp.asarray(_t5_bucket_np()), rel_bias)


def layer_b_fwd(h, w, p, comm=None):
    proj = mm(h, w['b_w_in'], 'nn', 'b_proj')
    bias = swa_bias(p['rel_bias'])
    (o, po), carried_out = swa_fwd(proj, bias, p['b_sinks'], comm=comm)
    yb = mm(po, w['b_w_out'], 'nn', 'b_out')
    return yb, dict(carried=carried_out, h=h, proj=proj, bias=bias, o=o, po=po)


def layer_b_bwd(dyb, w, p, sv, comm=None, sink=None):
    g = {}
    dpo = mm(dyb, w['b_w_out'], 'nt', 'b_dpo')
    _dw(g, sink, 'b_w_out', sv['po'], dyb, 'b_dwout')
    proj = sv['proj']

    def f1(dpo_, o, z):
        return [dpo_ * silu(z), dpo_ * o * silu_grad(z)], []
    (do, dz), _ = rowwise(f1, [rw(dpo), rw(sv['o']), rw(proj, 1024, 0)], [], [(1024, BF16), (1024, F32)], [], 256, 'b_gate_bwd')
    (dq, dkv, dbias, dsinks), g['carried'] = swa_bwd(proj, do, sv['bias'], p['b_sinks'], comm=comm)
    g['b_sinks'] = dsinks
    onehot = jnp.asarray(np.eye(REL_BUCKETS, dtype=np.float32)[_t5_bucket_np().reshape(-1)])

    def f2(db, oh):
        return [], [lax.dot_general(db, oh, (((1,), (0,)), ((), ())), preferred_element_type=F32,
                                    precision=lax.Precision.HIGHEST)]
    _, (drel,) = rowwise(f2, [(dbias.reshape(SWA_HEADS, -1), pl.BlockSpec((SWA_HEADS, 4096), lambda i: (0, i))),
                              (onehot, pl.BlockSpec((4096, REL_BUCKETS), lambda i: (i, 0)))], [], [],
                         [(SWA_HEADS, REL_BUCKETS)], 4096, 'b_drel', n_steps=(2 * WINDOW * WINDOW) // 4096)
    g['rel_bias'] = drel.T

    def f3(dz_, dq_, dkv_):
        return [jnp.concatenate([dz_, dq_, dkv_], axis=1)], []
    (dproj,), _ = rowwise(f3, [rw(dz), rw(dq), rw(dkv)], [], [(2304, BF16)], [], 256, 'b_dproj')
    _dw(g, sink, 'b_w_in', sv['h'], dproj, 'b_dwin')
    dh = mm(dproj, w['b_w_in'], 'nt', 'b_dh')
    return dh, g


MLA_SCALE = (MLA_NOPE + MLA_ROPE) ** -0.5


def _rope_tables(L):
    inv = ROPE_BASE ** (-jnp.arange(0, MLA_ROPE, 2, dtype=F32) / MLA_ROPE)
    ang = jnp.arange(L, dtype=F32)[:, None] * inv[None, :]
    c, s = jnp.cos(ang), jnp.sin(ang)
    one, zero, pad = jnp.ones((L, 128), F32), jnp.zeros((L, 128), F32), jnp.zeros((L, 64), F32)
    return (jnp.concatenate([one, c, c, c, c, pad], 1), jnp.concatenate([zero, s, s, s, s, pad], 1))


def _rot(x, transpose=False):
    w = x.shape[1]
    lane = lax.broadcasted_iota(jnp.int32, x.shape, 1)
    up = pltpu.roll(x, w - 16, 1)
    dn = pltpu.roll(x, 16, 1)
    first = (lane % 32) < 16
    return jnp.where(first, up, -dn) if transpose else jnp.where(first, -up, dn)


MLA_QT = 512


def _mla_exp(qf, kf, t, qt):
    n_k = kf.shape[0]
    s = lax.dot_general(qf, kf, (((1,), (1,)), ((), ())), preferred_element_type=F32) * MLA_SCALE
    qpos = t * qt + lax.broadcasted_iota(jnp.int32, (qt, n_k), 0)
    kpos = lax.broadcasted_iota(jnp.int32, (qt, n_k), 1)
    s = jnp.where(kpos <= qpos, s, NEG_INF)
    e = jnp.exp(s - jnp.max(s, axis=-1, keepdims=True))
    return e, jnp.sum(e, axis=-1, keepdims=True)


def _mla_heads(q, kv, kr):
    out = []
    for j in range(2):
        qf = jnp.concatenate([q[:, j * 64:(j + 1) * 64], q[:, 128 + j * 32:128 + (j + 1) * 32]], axis=1)
        kf = jnp.concatenate([kv[:, j * 64:(j + 1) * 64], kr], axis=1)
        out.append((qf, kf, kv[:, 128 + j * 64:128 + (j + 1) * 64]))
    return out


def mla_fwd(q, kv, kr, comm=None):
    L = q.shape[0]
    qt = min(MLA_QT, L)
    nq = L // qt

    def body(q_ref, kv_ref, kr_ref, o_ref):
        for t in range(nq):
            @pl.when(pl.program_id(1) == t)
            def _(t=t):
                n_k = (t + 1) * qt
                outs = []
                for qf, kf, v in _mla_heads(q_ref[...], kv_ref[0:n_k, :], kr_ref[0:n_k, 0:MLA_ROPE]):
                    e, den = _mla_exp(qf, kf, t, qt)
                    outs.append(jnp.dot(e.astype(BF16), v, preferred_element_type=F32) / den)
                o_ref[...] = jnp.concatenate(outs, axis=1)

    return carried(
        body, comm, grid=(MLA_HEADS // 2, nq),
        in_specs=[pl.BlockSpec((qt, 256), lambda hp, n: (n, hp)), pl.BlockSpec((L, 256), lambda hp, n: (0, hp)),
                  pl.BlockSpec((L, 128), lambda hp, n: (0, 0))],
        out_specs=pl.BlockSpec((qt, 128), lambda hp, n: (n, hp)), out_shape=jax.ShapeDtypeStruct((L, 1024), F32),
        semantics=("parallel", "parallel"), name='c_attn')(q, kv, kr)


def mla_bwd(q, kv, kr, do, comm=None):
    L = q.shape[0]
    qt = min(MLA_QT, L)
    nq = L // qt

    def body(q_ref, kv_ref, kr_ref, do_ref, dq_ref, dkv_ref, dkr_ref):
        @pl.when(pl.program_id(1) == 0)
        def _():
            dkv_ref[...] = jnp.zeros_like(dkv_ref)
            dkr_ref[...] = jnp.zeros_like(dkr_ref)

        for t in range(nq):
            @pl.when(pl.program_id(1) == t)
            def _(t=t):
                n_k = (t + 1) * qt
                do_ = do_ref[...]
                dqn, dqr, dkn, dvs = [], [], [], []
                dkr = jnp.zeros((n_k, MLA_ROPE), F32)
                for j, (qf, kf, v) in enumerate(_mla_heads(q_ref[...], kv_ref[0:n_k, :], kr_ref[0:n_k, 0:MLA_ROPE])):
                    doh = do_[:, j * 64:(j + 1) * 64]
                    e, den = _mla_exp(qf, kf, t, qt)
                    p = e * (1.0 / den)
                    dp = lax.dot_general(doh, v, (((1,), (1,)), ((), ())), preferred_element_type=F32)
                    ds = (p * (dp - jnp.sum(p * dp, axis=-1, keepdims=True)) * MLA_SCALE).astype(BF16)
                    dqf = jnp.dot(ds, kf, preferred_element_type=F32)
                    dkf = lax.dot_general(ds, qf, (((0,), (0,)), ((), ())), preferred_element_type=F32)
                    dvs.append(lax.dot_general(p.astype(BF16), doh, (((0,), (0,)), ((), ())), preferred_element_type=F32))
                    dqn.append(dqf[:, :MLA_NOPE])
                    dqr.append(dqf[:, MLA_NOPE:])
                    dkn.append(dkf[:, :MLA_NOPE])
                    dkr = dkr + dkf[:, MLA_NOPE:]
                dq_ref[...] = jnp.concatenate(dqn + dqr + [jnp.zeros((qt, 64), F32)], axis=1)
                dkv_ref[0:n_k, :] += jnp.concatenate(dkn + dvs, axis=1)
                dkr_ref[0, 0:n_k, :] += jnp.concatenate([dkr, jnp.zeros((n_k, 128 - MLA_ROPE), F32)], axis=1)

    return carried(
        body, comm, grid=(MLA_HEADS // 2, nq),
        in_specs=[pl.BlockSpec((qt, 256), lambda hp, n: (n, hp)), pl.BlockSpec((L, 256), lambda hp, n: (0, hp)),
                  pl.BlockSpec((L, 128), lambda hp, n: (0, 0)), pl.BlockSpec((qt, 128), lambda hp, n: (n, hp))],
        out_specs=[pl.BlockSpec((qt, 256), lambda hp, n: (n, hp)), pl.BlockSpec((L, 256), lambda hp, n: (0, hp)),
                   pl.BlockSpec((1, L, 128), lambda hp, n: (hp, 0, 0))],
        out_shape=[jax.ShapeDtypeStruct((L, 2048), F32), jax.ShapeDtypeStruct((L, 2048), F32),
                   jax.ShapeDtypeStruct((MLA_HEADS // 2, L, 128), F32)],
        semantics=("parallel", "arbitrary"), name='c_attn_bwd')(q, kv, kr, do)


def layer_c_fwd(h, w, p, comm=None):
    L = h.shape[0]
    proj = mm(h, w['c_w_in'], 'nn', 'c_proj')

    def f1(c, gq, gk):
        return [rms_fwd(c[:, :768], gq), rms_fwd(c[:, 768:], gk)], []
    (cqn, ckvn), _ = rowwise(f1, [rw(proj, 1024, 1)], [p['c_q_norm'], p['c_kv_norm']], [(768, BF16), (256, BF16)], [],
                             256, 'c_norms')
    qf = mm(cqn, w['c_w_uq'], 'nn', 'c_uq')
    kvf = mm(ckvn, w['c_w_ukv'], 'nn', 'c_ukv', out_dtype=BF16)
    cos, sin = _rope_tables(L)

    def f2(q_, kr_, c, s):
        c8, s8 = jnp.tile(c, (1, 8)), jnp.tile(s, (1, 8))
        return [q_ * c8 + _rot(q_) * s8, kr_ * c[:, 128:] + _rot(kr_) * s[:, 128:]], []
    (q, kr), _ = rowwise(f2, [rw(qf), rw(proj, 128, 16), rw(cos), rw(sin)], [], [(2048, BF16), (128, BF16)], [], 256,
                         'c_rope')
    o, carried_out = mla_fwd(q, kvf, kr, comm=comm)

    def f3(o_, z):
        return [o_ * silu(z)], []
    (po,), _ = rowwise(f3, [rw(o), rw(proj, 1024, 0)], [], [(1024, BF16)], [], 256, 'c_gate')
    yb = mm(po, w['c_w_out'], 'nn', 'c_out')
    return yb, dict(carried=carried_out, h=h, proj=proj, cqn=cqn, ckvn=ckvn, q=q, kv=kvf, kr=kr, o=o, po=po, cos=cos, sin=sin)


def layer_c_bwd(dyb, w, p, sv, comm=None, sink=None):
    g = {}
    dpo = mm(dyb, w['c_w_out'], 'nt', 'c_dpo')
    _dw(g, sink, 'c_w_out', sv['po'], dyb, 'c_dwout')
    proj = sv['proj']
    L = proj.shape[0]

    def f1(dpo_, o, z):
        return [dpo_ * silu(z), dpo_ * o * silu_grad(z)], []
    (do, dz), _ = rowwise(f1, [rw(dpo), rw(sv['o']), rw(proj, 1024, 0)], [], [(1024, BF16), (1024, F32)], [], 256,
                          'c_gate_bwd')
    (dq, dkvf, dkr8), g['carried'] = mla_bwd(sv['q'], sv['kv'], sv['kr'], do, comm=comm)

    def f2(dq_, dkr_, c, s):
        c8, s8 = jnp.tile(c, (1, 8)), jnp.tile(s, (1, 8))
        dk = jnp.sum(dkr_, axis=0)
        return [dq_ * c8 + _rot(dq_ * s8, True), dk * c[:, 128:] + _rot(dk * s[:, 128:], True)], []
    tl = 256
    (dqf, dkr), _ = rowwise(f2, [rw(dq), (dkr8, pl.BlockSpec((8, tl, 128), lambda i: (0, i, 0))), rw(sv['cos']),
                                 rw(sv['sin'])], [], [(2048, BF16), (128, F32)], [], tl, 'c_rope_bwd')
    _dw(g, sink, 'c_w_uq', sv['cqn'], dqf, 'c_dwuq')
    _dw(g, sink, 'c_w_ukv', sv['ckvn'], dkvf, 'c_dwukv')
    dcqn = mm(dqf, w['c_w_uq'], 'nt', 'c_dcqn')
    dckvn = mm(dkvf, w['c_w_ukv'], 'nt', 'c_dckvn')

    def f3(c, dq_, dk_, dz_, dkr_, gq, gk):
        dcq, dgq = rms_bwd(c[:, :768], gq, dq_)
        dckv, dgk = rms_bwd(c[:, 768:], gk, dk_)
        return [jnp.concatenate([dz_, dcq, dckv, dkr_], axis=1)], [dgq, dgk]
    (dproj,), (dgq, dgk) = rowwise(f3, [rw(proj, 1024, 1), rw(dcqn), rw(dckvn), rw(dz), rw(dkr)],
                                   [p['c_q_norm'], p['c_kv_norm']], [(2176, BF16)], [(1, 768), (1, 256)], 256, 'c_dproj')
    g['c_q_norm'], g['c_kv_norm'] = dgq, dgk
    _dw(g, sink, 'c_w_in', sv['h'], dproj, 'c_dwin')
    dh = mm(dproj, w['c_w_in'], 'nt', 'c_dh')
    return dh, g


def _sgu_mix(wm, v, transpose):
    outs = []
    dims = (((0,), (0,)), ((), ())) if transpose else (((1,), (0,)), ((), ()))
    for gi in range(SGU_G):
        outs.append(lax.dot_general(wm[gi], v[:, gi * SGU_C:(gi + 1) * SGU_C].astype(BF16), dims,
                                    preferred_element_type=F32))
    return jnp.concatenate(outs, axis=1)


def _sgu_wmask(ws):
    t = lax.broadcasted_iota(jnp.int32, (SGU_T, SGU_T), 0)
    s = lax.broadcasted_iota(jnp.int32, (SGU_T, SGU_T), 1)
    return jnp.where((s <= t)[None], ws, 0.0).astype(BF16)


def _ln_stats(v):
    mu = jnp.mean(v, axis=-1, keepdims=True)
    vc = v - mu
    rstd = lax.rsqrt(jnp.mean(vc * vc, axis=-1, keepdims=True) + EPS)
    return vc * rstd, rstd


def layer_d_fwd(h, w, p):
    proj = mm(h, w['d_w_in'], 'nn', 'd_proj')
    bias = jnp.repeat(p['d_b_s'][0].T, SGU_C, axis=1)

    def f1(u_, v_, z, ws, lg, lb, bs):
        xh, _ = _ln_stats(gelu(v_))
        s = _sgu_mix(_sgu_wmask(ws), xh * lg + lb, False) + bs
        return [gelu(u_) * s * silu(z)], []
    (po,), _ = rowwise(f1, [rw(proj, 1024, 0), rw(proj, 1024, 1), rw(proj, 1024, 2)],
                       [p['d_w_s'][0], p['d_ln_g'], p['d_ln_b'], bias], [(1024, BF16)], [], SGU_T, 'd_mix')
    yb = mm(po, w['d_w_out'], 'nn', 'd_out')
    return yb, dict(h=h, proj=proj, po=po, bias=bias)


def layer_d_bwd(dyb, w, p, sv, sink=None):
    g = {}
    dpo = mm(dyb, w['d_w_out'], 'nt', 'd_dpo')
    _dw(g, sink, 'd_w_out', sv['po'], dyb, 'd_dwout')
    proj = sv['proj']

    def f1(dpo_, u_, v_, z, ws, lg, lb, bs):
        wm = _sgu_wmask(ws)
        gv = gelu(v_)
        xh, rstd = _ln_stats(gv)
        vn = xh * lg + lb
        s = _sgu_mix(wm, vn, False) + bs
        gu, sz = gelu(u_), silu(z)
        du = dpo_ * s * sz
        ds = dpo_ * gu * sz
        dz = dpo_ * gu * s * silu_grad(z)
        dsb = ds.astype(BF16)
        dws = jnp.stack([lax.dot_general(dsb[:, gi * SGU_C:(gi + 1) * SGU_C], vn[:, gi * SGU_C:(gi + 1) * SGU_C].astype(BF16),
                                         (((1,), (1,)), ((), ())), preferred_element_type=F32) for gi in range(SGU_G)])
        dvn = _sgu_mix(wm, ds, True)
        dlg = jnp.sum(dvn * xh, axis=0, keepdims=True)
        dlb = jnp.sum(dvn, axis=0, keepdims=True)
        dxh = dvn * lg
        dgv = rstd * (dxh - jnp.mean(dxh, axis=-1, keepdims=True) - xh * jnp.mean(dxh * xh, axis=-1, keepdims=True))
        return ([jnp.concatenate([du * gelu_grad(u_), dgv * gelu_grad(v_), dz], axis=1)], [dws, ds, dlg, dlb])
    (dproj,), (dws, dbs, dlg, dlb) = rowwise(
        f1, [rw(dpo), rw(proj, 1024, 0), rw(proj, 1024, 1), rw(proj, 1024, 2)],
        [p['d_w_s'][0], p['d_ln_g'], p['d_ln_b'], sv['bias']], [(3072, BF16)],
        [(SGU_G, SGU_T, SGU_T), (SGU_T, 1024), (1, 1024), (1, 1024)], SGU_T, 'd_mix_bwd')
    tril = np.tril(np.ones((SGU_T, SGU_T), dtype=bool))
    g['d_w_s'] = jnp.where(tril[None], dws, 0.0)[None]
    g['d_b_s'] = dbs.reshape(SGU_T, SGU_G, SGU_C).sum(-1).T[None]
    g['d_ln_g'], g['d_ln_b'] = dlg, dlb
    _dw(g, sink, 'd_w_in', sv['h'], dproj, 'd_dwin')
    dh = mm(dproj, w['d_w_in'], 'nt', 'd_dh')
    return dh, g


def _coords():
    return lax.axis_index("x"), lax.axis_index("y"), lax.axis_index("c")


class AllGather:
    def __init__(self, x):
        self.ins = [x]
        self.outs = [jax.ShapeDtypeStruct((N_DEV,) + x.shape, x.dtype)]
        self.scratch = [pltpu.SemaphoreType.DMA((7,)), pltpu.SemaphoreType.DMA((7,)), pltpu.SemaphoreType.DMA(())]

    def hooks(self, n_steps):
        return [(0, functools.partial(self.phase, 0), False), (n_steps - 1, functools.partial(self.phase, 1), True),
                (n_steps - 1, functools.partial(self.phase, 2), True)]

    @staticmethod
    def phase(which, ins, outs, scratch):
        (x_ref,), (out_ref,), (send_sems, recv_sems, local_sem) = ins, outs, scratch
        x_, y_, c_ = _coords()
        me, sibling = (x_, y_, c_), (x_, y_, 1 - c_)
        chips = [(1 - x_, y_), (x_, 1 - y_), (1 - x_, 1 - y_)]

        def slot(px, py, pc):
            return out_ref.at[4 * px + 2 * py + pc]

        def copy(k, block, to, src=None):
            return pltpu.make_async_remote_copy(src_ref=slot(*block) if src is None else src, dst_ref=slot(*block),
                                                send_sem=send_sems.at[k], recv_sem=recv_sems.at[k], device_id=to,
                                                device_id_type=MESH)

        mine = pltpu.make_async_copy(x_ref, slot(*me), local_sem)
        first = [copy(0, me, sibling, src=x_ref)]
        first += [copy(1 + j, me, (*chip, c_), src=x_ref) for j, chip in enumerate(chips)]
        passed = [copy(4 + j, (*chip, c_), sibling) for j, chip in enumerate(chips)]
        if which == 0:
            mine.start()
            for cp in first:
                cp.start()
        elif which == 1:
            for j, chip in enumerate(chips):
                copy(1 + j, (*chip, c_), me).wait_recv()
                passed[j].start()
        else:
            copy(0, sibling, me).wait_recv()
            for j, chip in enumerate(chips):
                copy(4 + j, (*chip, 1 - c_), me).wait_recv()
            for cp in first + passed:
                cp.wait_send()
            mine.wait()


class ChipExchange:
    def __init__(self, part):
        self.ins = [part]
        self.outs = [jax.ShapeDtypeStruct((3,) + part.shape[1:], part.dtype)]
        self.scratch = [pltpu.SemaphoreType.DMA((3,)), pltpu.SemaphoreType.DMA((3,))]

    def hooks(self, n_steps):
        return [(0, functools.partial(self.phase, 0), False), (n_steps - 1, functools.partial(self.phase, 1), True)]

    @staticmethod
    def phase(which, ins, outs, scratch):
        (p_ref,), (land_ref,), (send_sems, recv_sems) = ins, outs, scratch
        x_, y_, c_ = _coords()
        copies = []
        for r, (fx, fy) in enumerate([(1, 0), (0, 1), (1, 1)]):
            tx = jnp.where(fx == 1, 1 - x_, x_)
            ty = jnp.where(fy == 1, 1 - y_, y_)
            copies.append(pltpu.make_async_remote_copy(src_ref=p_ref.at[2 * tx + ty], dst_ref=land_ref.at[r],
                                                       send_sem=send_sems.at[r], recv_sem=recv_sems.at[r],
                                                       device_id=(tx, ty, c_), device_id_type=MESH))
        if which == 0:
            for cp in copies:
                cp.start()
        else:
            for cp in copies:
                cp.wait_recv()
            for cp in copies:
                cp.wait_send()


class Both:
    def __init__(self, a, b):
        self.parts = (a, b)
        self.ins, self.outs, self.scratch = a.ins + b.ins, a.outs + b.outs, a.scratch + b.scratch

    def hooks(self, n_steps):
        res, oi, oo, osc = [], 0, 0, 0
        for p in self.parts:
            sl = (slice(oi, oi + len(p.ins)), slice(oo, oo + len(p.outs)), slice(osc, osc + len(p.scratch)))
            res += [(at, functools.partial(self.sub, fn, sl), after) for at, fn, after in p.hooks(n_steps)]
            oi, oo, osc = oi + len(p.ins), oo + len(p.outs), osc + len(p.scratch)
        return res

    @staticmethod
    def sub(fn, sl, ins, outs, scratch):
        fn(ins[sl[0]], outs[sl[1]], scratch[sl[2]])


def run_comm(comm, name):
    def body(*refs):
        ci, co = len(comm.ins), len(comm.outs)
        for _, fn, _ in comm.hooks(1):
            fn(refs[:ci], refs[ci:ci + co], refs[ci + co:])

    return pl.pallas_call(body, out_shape=list(comm.outs), in_specs=[ANY] * len(comm.ins),
                          out_specs=[ANY] * len(comm.outs), scratch_shapes=list(comm.scratch), name=name)(*comm.ins)


def all_gather(x, name):
    return run_comm(AllGather(x), name)[0]


def rs_sibling(gfull, tag):
    _, R, C = gfull.shape

    def body(g_ref, land_ref, send_sems, recv_sems):
        x_, y_, c_ = _coords()
        copies = []
        for k in range(4):
            cp = pltpu.make_async_remote_copy(src_ref=g_ref.at[2 * k + 1 - c_], dst_ref=land_ref.at[k],
                                              send_sem=send_sems.at[k], recv_sem=recv_sems.at[k],
                                              device_id=(x_, y_, 1 - c_), device_id_type=MESH)
            cp.start()
            copies.append(cp)
        for cp in copies:
            cp.wait_recv()
        for cp in copies:
            cp.wait_send()

    return pl.pallas_call(
        body, out_shape=jax.ShapeDtypeStruct((4, R, C), gfull.dtype), in_specs=[ANY], out_specs=ANY,
        scratch_shapes=[pltpu.SemaphoreType.DMA((4,)), pltpu.SemaphoreType.DMA((4,))], name='rs_sibling_' + tag)(gfull)


def rs_pair_add(gfull, land, core, tag):
    _, R, C = gfull.shape
    tl = R

    def body(c_ref, g_ref, l_ref, o_ref):
        o_ref[...] = (g_ref[...].astype(F32) + l_ref[...].astype(F32)).astype(BF16)

    return pl.pallas_call(
        body, out_shape=jax.ShapeDtypeStruct((4, R, C), BF16),
        grid_spec=pltpu.PrefetchScalarGridSpec(
            num_scalar_prefetch=1, grid=(4, R // tl),
            in_specs=[pl.BlockSpec((1, tl, C), lambda k, i, c: (2 * k + c[0], i, 0)),
                      pl.BlockSpec((1, tl, C), lambda k, i, c: (k, i, 0))],
            out_specs=pl.BlockSpec((1, tl, C), lambda k, i, c: (k, i, 0))),
        compiler_params=pltpu.CompilerParams(dimension_semantics=("parallel", "parallel")), name='rs_pair_add_' + tag)(
            core, gfull, land)


def rs_chips(part, tag):
    return run_comm(ChipExchange(part), 'rs_chips_' + tag)[0]


def _adam(wv, gv, mv, vv):
    m = ADAM_B1 * mv + (1.0 - ADAM_B1) * gv
    v = ADAM_B2 * vv + (1.0 - ADAM_B2) * (gv * gv)
    m_hat = m / (1.0 - ADAM_B1 ** ADAM_STEP)
    v_hat = v / (1.0 - ADAM_B2 ** ADAM_STEP)
    delta = -ADAM_LR * (m_hat / (jnp.sqrt(v_hat) + ADAM_EPS) + ADAM_WD * wv)
    return delta, m, v


def _sum4(p_ref, l_ref):
    return ((p_ref[0].astype(F32) + l_ref[0].astype(F32)) + l_ref[1].astype(F32)) + l_ref[2].astype(F32)


def rs_rep_sum(part, land, chip):
    def body(c_ref, p_ref, l_ref, o_ref):
        o_ref[...] = _sum4(p_ref, l_ref).astype(BF16)

    return pl.pallas_call(
        body, out_shape=jax.ShapeDtypeStruct((REP_SLOT, LANES), BF16),
        grid_spec=pltpu.PrefetchScalarGridSpec(
            num_scalar_prefetch=1, grid=(1,),
            in_specs=[pl.BlockSpec((1, REP_SLOT, LANES), lambda i, c: (c[0], 0, 0)),
                      pl.BlockSpec((3, REP_SLOT, LANES), lambda i, c: (0, 0, 0))],
            out_specs=pl.BlockSpec((REP_SLOT, LANES), lambda i, c: (0, 0))),
        compiler_params=pltpu.CompilerParams(dimension_semantics=("parallel",)), name='rs_rep')(chip, part, land)


def adam_param(name, shape, off, w, m, v, chip, part=None, land=None, grep=None):
    r, c = shape
    rp, nt, rb = _tiles(shape)
    rbw = min(r, rb)
    n_src = 2 if grep is None else 1
    ns = w.shape
    assert int(np.prod(ns[:-1])) == r and ns[-1] == c
    if len(ns) == 2:
        nat_block, nat_map = (rbw, c), lambda i, cr: (i, 0)
    elif int(np.prod(ns[:-2])) == 1:
        nat_block, nat_map = (1,) * (len(ns) - 2) + (rbw, c), lambda i, cr: (0,) * (len(ns) - 2) + (i, 0)
    else:
        assert len(ns) == 4 and ns[0] == 1 and rbw % ns[2] == 0
        nat_block, nat_map = (1, rbw // ns[2], ns[2], c), lambda i, cr: (0, i, 0, 0)

    def body(c_ref, *refs):
        srcs = refs[:n_src * nt]
        w_ref, m_ref, v_ref, g_ref, d_ref, nm_ref, nv_ref = refs[n_src * nt:]
        if grep is None:
            tiles = [_sum4(srcs[2 * t], srcs[2 * t + 1]) for t in range(nt)]
        else:
            tiles = [srcs[t][...].astype(F32) for t in range(nt)]
        g = (tiles[0] if nt == 1 else jnp.concatenate(tiles, axis=1))[:rbw, :c]
        g_ref[...] = g.reshape(nat_block)
        res = _adam(w_ref[...].reshape(rbw, c), g, m_ref[...].reshape(rbw, c), v_ref[...].reshape(rbw, c))
        for ref, val in zip((d_ref, nm_ref, nv_ref), res):
            ref[...] = val.reshape(nat_block)

    in_specs, args = [], []
    for t in range(nt):
        b0 = (off + t * rp) // rb
        assert (off + t * rp) % rb == 0
        if grep is None:
            in_specs += [pl.BlockSpec((1, rb, LANES), functools.partial(lambda i, cr, b0: (cr[0], b0 + i, 0), b0=b0)),
                         pl.BlockSpec((3, rb, LANES), functools.partial(lambda i, cr, b0: (0, b0 + i, 0), b0=b0))]
            args += [part, land]
        else:
            in_specs.append(pl.BlockSpec((rb, LANES), functools.partial(lambda i, cr, b0: (b0 + i, 0), b0=b0)))
            args.append(grep)
    nat = pl.BlockSpec(nat_block, nat_map)
    return pl.pallas_call(
        body, out_shape=[jax.ShapeDtypeStruct(ns, F32)] * 4,
        grid_spec=pltpu.PrefetchScalarGridSpec(num_scalar_prefetch=1, grid=(rp // rb,), in_specs=in_specs + [nat] * 3,
                                               out_specs=[nat] * 4),
        compiler_params=pltpu.CompilerParams(dimension_semantics=("parallel",)), name='adam_' + name)(
            chip, *args, w, m, v)


def adam_small(names, grep, P, M, V):
    in_specs, args, out_specs, out_shape, meta = [], [], [], [], []
    for n in names:
        s = REP_SHAPE[n]
        rp, nt, _ = _tiles(s)
        ns = P[n].shape
        for t in range(nt):
            b0 = (REP_OFF[n] + t * rp) // rp
            assert (REP_OFF[n] + t * rp) % rp == 0
            in_specs.append(pl.BlockSpec((rp, LANES), functools.partial(lambda i, b0: (b0, 0), b0=b0)))
            args.append(grep)
        nat = pl.BlockSpec(ns, functools.partial(lambda i, nd: (0,) * nd, nd=len(ns)))
        in_specs += [nat] * 3
        args += [P[n], M[n], V[n]]
        out_specs += [nat] * 4
        out_shape += [jax.ShapeDtypeStruct(ns, F32)] * 4
        meta.append((s, nt, ns))
    n_in = len(in_specs)

    def body(*refs):
        ins, outs = refs[:n_in], refs[n_in:]
        k = 0
        for p, ((r, c), nt, ns) in enumerate(meta):
            tiles = [ins[k + t][...].astype(F32) for t in range(nt)]
            w_ref, m_ref, v_ref = ins[k + nt:k + nt + 3]
            k += nt + 3
            g = (tiles[0] if nt == 1 else jnp.concatenate(tiles, axis=1))[:r, :c]
            res = (g,) + _adam(w_ref[...].reshape(r, c), g, m_ref[...].reshape(r, c), v_ref[...].reshape(r, c))
            for ref, val in zip(outs[4 * p:4 * p + 4], res):
                ref[...] = val.reshape(ns)

    res = pl.pallas_call(body, grid=(1,), in_specs=in_specs, out_specs=out_specs, out_shape=out_shape,
                         compiler_params=pltpu.CompilerParams(dimension_semantics=("arbitrary",)), name='adam_small')(*args)
    return {n: tuple(res[4 * p:4 * p + 4]) for p, n in enumerate(names)}


VM = pl.BlockSpec(memory_space=pltpu.VMEM)


def _tile_value(w, t, rp):
    r, c = w.shape
    wt = min(LANES, c - t * LANES)
    tile = w[:, t * LANES:t * LANES + wt]
    if wt < LANES:
        tile = jnp.concatenate([tile, jnp.zeros((r, LANES - wt), tile.dtype)], axis=1)
    if rp > r:
        tile = jnp.concatenate([tile, jnp.zeros((rp - r, LANES), tile.dtype)], axis=0)
    return tile


def pack_layer(layer, blocks):
    names = LAYER_PARAMS[layer]

    def body(*refs):
        tiles = []
        for ref, n in zip(refs[:-1], names):
            rp, nt, _ = _tiles(_block_shape(n))
            w = ref[...].reshape(_block_shape(n))
            tiles += [_tile_value(w, t, rp) for t in range(nt)]
        refs[-1][...] = jnp.concatenate(tiles, axis=0).astype(BF16)

    return pl.pallas_call(body, out_shape=jax.ShapeDtypeStruct((LAYER_ROWS[layer], LANES), BF16),
                          in_specs=[VM] * len(names), out_specs=VM, name='pack_' + layer)(*[blocks[n] for n in names])


def assemble(name, gathered):
    (rf, cf), ax = SHARDED[name]
    r, c = _block_shape(name)
    rp, nt, _ = _tiles((r, c))
    off = SH_OFF[name]
    out_cols = cf if ax == 0 else len(perm_index(name))

    def body(g_ref, o_ref, buf, sem):
        cp = pltpu.make_async_copy(g_ref.at[:, pl.ds(off, nt * rp), :], buf, sem)
        cp.start()
        cp.wait()
        if ax == 0:
            for j in range(N_DEV):
                o_ref[j * r:(j + 1) * r, :] = jnp.concatenate([buf[j, t * rp:(t + 1) * rp, :] for t in range(nt)], axis=1)
            return
        pieces = []
        for p in PERM[name]:
            if p[0] == 'z':
                pieces.append(jnp.zeros((r, p[1]), BF16))
                continue
            n0, w = p
            while w > 0:
                j, cb = divmod(n0, c)
                t, lane = divmod(cb, LANES)
                wl = min(w, LANES - lane, c - cb)
                pieces.append(buf[j, t * rp:t * rp + r, lane:lane + wl])
                n0, w = n0 + wl, w - wl
        o_ref[...] = jnp.concatenate(pieces, axis=1)

    return pl.pallas_call(
        body, out_shape=jax.ShapeDtypeStruct((rf, out_cols), BF16), in_specs=[ANY], out_specs=VM,
        scratch_shapes=[pltpu.VMEM((N_DEV, nt * rp, LANES), BF16), pltpu.SemaphoreType.DMA(())], name='asm_' + name)(
            gathered)


def chunk_grad(layer, name, dw, gfull):
    (rf, cf), ax = SHARDED[name]
    r, c = _block_shape(name)
    rp, nt, _ = _tiles((r, c))
    off = SH_OFF[name]
    if ax == 1:
        idx = perm_index(name) if name in PERM else np.arange(cf)
        inv = np.full(cf, -1)
        inv[idx[idx >= 0]] = np.nonzero(idx >= 0)[0]

    def body(*refs):
        dw_ref, o_ref, buf, sem = refs[0], refs[-3], refs[-2], refs[-1]
        for j in range(N_DEV):
            for t in range(nt):
                if ax == 0:
                    tile = dw_ref[j * r:(j + 1) * r, t * LANES:(t + 1) * LANES]
                else:
                    cols = inv[j * c + t * LANES:j * c + min((t + 1) * LANES, c)]
                    cuts = [0] + [k for k in range(1, len(cols)) if cols[k] != cols[k - 1] + 1] + [len(cols)]
                    pieces = [dw_ref[:, int(cols[a]):int(cols[b - 1]) + 1] for a, b in zip(cuts[:-1], cuts[1:])]
                    if len(cols) < LANES:
                        pieces.append(jnp.zeros((r, LANES - len(cols)), F32))
                    tile = pieces[0] if len(pieces) == 1 else jnp.concatenate(pieces, axis=1)
                    if rp > r:
                        tile = jnp.concatenate([tile, jnp.zeros((rp - r, LANES), F32)], axis=0)
                buf[j, t * rp:(t + 1) * rp, :] = tile.astype(BF16)
        cp = pltpu.make_async_copy(buf, o_ref.at[:, pl.ds(off, nt * rp), :], sem)
        cp.start()
        cp.wait()

    shape = jax.ShapeDtypeStruct((N_DEV, LAYER_ROWS[layer], LANES), BF16)
    scratch = [pltpu.VMEM((N_DEV, nt * rp, LANES), BF16), pltpu.SemaphoreType.DMA(())]
    if gfull is None:
        return pl.pallas_call(body, out_shape=shape, in_specs=[VM], out_specs=ANY, scratch_shapes=scratch,
                              name='chunk_' + name)(dw)
    return pl.pallas_call(body, out_shape=shape, in_specs=[VM, ANY], out_specs=ANY, scratch_shapes=scratch,
                          input_output_aliases={1: 0}, name='chunk_' + name)(dw, gfull)


class GradSink:
    def __init__(self):
        self.bufs = {}

    def put(self, name, a, b, mm_name):
        (rf, cf), ax = SHARDED[name]
        r, c = _block_shape(name)
        group = GROUP_OF[name]
        direct = ax == 0 or (c % LANES == 0 and PERM[name] == [(0, cf)])
        if direct:
            self.bufs[group] = mm_tn_chunked(a, b, mm_name, group, name, self.bufs.get(group))
        else:
            self.add(name, mm(a, b, 'tn', mm_name))

    def add(self, name, dw):
        group = GROUP_OF[name]
        self.bufs[group] = chunk_grad(group, name, dw, self.bufs.get(group))


def mm_tn_chunked(a, b, mm_name, layer, wname, gfull):
    (rf, cf), ax = SHARDED[wname]
    r, c = _block_shape(wname)
    rp, nt, _ = _tiles((r, c))
    off = SH_OFF[wname]
    K, M = a.shape
    N = b.shape[1]
    assert (M, N) == (rf, cf) and rp == r
    if ax == 0:
        tn = 4 * LANES
        grid, bspec = (N // tn,), pl.BlockSpec((K, tn), lambda g: (0, g))
        ospec = pl.BlockSpec((N_DEV, 4 * r, LANES), lambda g: (0, off // (4 * r) + g, 0))
        assert off % (4 * r) == 0 and nt % 4 == 0

        def store(res, o_ref):
            for j in range(N_DEV):
                for q in range(4):
                    o_ref[j, q * r:(q + 1) * r, :] = res[j * r:(j + 1) * r, q * LANES:(q + 1) * LANES].astype(BF16)
    else:
        tn = c
        grid, bspec = (N_DEV,), pl.BlockSpec((K, tn), lambda g: (0, g))
        ospec = pl.BlockSpec((1, nt * r, LANES), lambda g: (g, off // (nt * r), 0))
        assert off % (nt * r) == 0

        def store(res, o_ref):
            for t in range(nt):
                o_ref[0, t * r:(t + 1) * r, :] = res[:, t * LANES:(t + 1) * LANES].astype(BF16)

    def body(*refs):
        a_ref, b_ref, o_ref = refs[0], refs[1], refs[-1]
        store(lax.dot_general(a_ref[...].astype(BF16), b_ref[...].astype(BF16), _TN, preferred_element_type=F32), o_ref)

    shape = jax.ShapeDtypeStruct((N_DEV, LAYER_ROWS[layer], LANES), BF16)
    aspec = pl.BlockSpec((K, M), lambda g: (0, 0))
    params = pltpu.CompilerParams(dimension_semantics=("parallel",))
    if gfull is None:
        return pl.pallas_call(body, grid=grid, in_specs=[aspec, bspec], out_specs=ospec, out_shape=shape,
                              compiler_params=params, name=mm_name)(a, b)
    return pl.pallas_call(body, grid=grid, in_specs=[aspec, bspec, ANY], out_specs=ospec, out_shape=shape,
                          input_output_aliases={2: 0}, compiler_params=params, name=mm_name)(a, b, gfull)


def pack_rep(G):
    def body(*refs):
        tiles = []
        for ref, s in zip(refs[:-1], REP_SHAPE.values()):
            rp, nt, _ = _tiles(s)
            g = ref[...]
            tiles += [_tile_value(g, t, rp) for t in range(nt)]
        rows = sum(t.shape[0] for t in tiles)
        if rows < REP_ROWS:
            tiles.append(jnp.zeros((REP_ROWS - rows, LANES), F32))
        full = jnp.concatenate(tiles, axis=0)
        for j in range(N_DEV):
            refs[-1][j] = full[j * REP_CHUNK:(j + 1) * REP_CHUNK]

    return pl.pallas_call(body, out_shape=jax.ShapeDtypeStruct((N_DEV, REP_SLOT, LANES), F32),
                          in_specs=[VM] * len(REP_SHAPE), out_specs=VM, name='pack_rep')(
                              *[G[n].reshape(s) for n, s in REP_SHAPE.items()])


def _pack_small(blocks, order, rows, width, dtype):
    flat = jnp.concatenate([blocks[n].reshape(-1).astype(dtype) for n in order])
    return jnp.pad(flat, (0, rows * width - flat.shape[0])).reshape(rows, width)


def kernel(x, pre_norm, post_norm, rel_bias, a_w_in, a_lam_re, a_lam_im, a_log_dt, a_b_re, a_b_im, a_c_re, a_c_im, a_d, a_w_glu, a_b_glu, a_w_out, b_w_in, b_sinks, b_w_out, c_w_in, c_q_norm, c_kv_norm, c_w_uq, c_w_ukv, c_w_out, d_w_in, d_ln_g, d_ln_b, d_w_s, d_b_s, d_w_out, loss_target, m_pre_norm, m_post_norm, m_rel_bias, m_a_w_in, m_a_lam_re, m_a_lam_im, m_a_log_dt, m_a_b_re, m_a_b_im, m_a_c_re, m_a_c_im, m_a_d, m_a_w_glu, m_a_b_glu, m_a_w_out, m_b_w_in, m_b_sinks, m_b_w_out, m_c_w_in, m_c_q_norm, m_c_kv_norm, m_c_w_uq, m_c_w_ukv, m_c_w_out, m_d_w_in, m_d_ln_g, m_d_ln_b, m_d_w_s, m_d_b_s, m_d_w_out, v_pre_norm, v_post_norm, v_rel_bias, v_a_w_in, v_a_lam_re, v_a_lam_im, v_a_log_dt, v_a_b_re, v_a_b_im, v_a_c_re, v_a_c_im, v_a_d, v_a_w_glu, v_a_b_glu, v_a_w_out, v_b_w_in, v_b_sinks, v_b_w_out, v_c_w_in, v_c_q_norm, v_c_kv_norm, v_c_w_uq, v_c_w_ukv, v_c_w_out, v_d_w_in, v_d_ln_g, v_d_ln_b, v_d_w_s, v_d_b_s, v_d_w_out):
    loc = locals()
    P = {n: loc[n] for n in WEIGHTS}
    M = {n: loc['m_' + n] for n in WEIGHTS}
    V = {n: loc['v_' + n] for n in WEIGHTS}
    xs = x[0]
    L = xs.shape[0]

    blocks = {n: P[n].reshape(_block_shape(n)) for n in SHARDED}
    packed = {layer: pack_layer(layer, P) for layer in LAYER_PARAMS}
    W = {}

    def assemble_layer(layer, gathered):
        for n in LAYER_PARAMS[layer]:
            if n not in SHARDED_F32:
                W[n] = assemble(n, gathered)

    assemble_layer('a1', all_gather(packed['a1'], 'ag_a1'))
    small = all_gather(_pack_small(blocks, SHARDED_F32, SMALL_ROWS, 128, F32), 'ag_small')
    Pl = dict(P)
    for n in SHARDED_F32:
        c = SHARDED[n][0][1]
        bc = c // N_DEV
        Pl[n] = small.reshape(N_DEV, -1)[:, SMALL_OFF[n]:SMALL_OFF[n] + bc].reshape(1, c)
    cx, cy, cc = _coords()
    core = jnp.reshape(cc, (1,)).astype(jnp.int32)
    chip = jnp.reshape(2 * cx + cy, (1,)).astype(jnp.int32)

    def pair_sums(gfull, tag):
        return rs_pair_add(gfull, rs_sibling(gfull, tag), core, tag)

    fwd = [layer_a_fwd, layer_b_fwd, layer_c_fwd, layer_d_fwd]
    bwd = [layer_a_bwd, layer_b_bwd, layer_c_bwd, layer_d_bwd]
    saved = []
    xc = xs

    def fpre(x_, g_):
        return [rms_fwd(x_, g_)], []
    (h,), _ = rowwise(fpre, [rw(xc)], [P['pre_norm'][0:1]], [(D_MODEL, BF16)], [], 256, 'pre_norm0')
    for i in range(4):
        if i == 0:
            yb, sv = fwd[i](h, W, Pl, comm=Both(AllGather(packed['a2']), AllGather(packed['b'])),
                            on_carried=lambda got: assemble_layer('a2', got[0]))
            assemble_layer('b', sv['carried'][1])
        elif i < 3:
            nxt = 'abcd'[i + 1]
            yb, sv = fwd[i](h, W, Pl, comm=AllGather(packed[nxt]))
            assemble_layer(nxt, sv['carried'][0])
        else:
            yb, sv = fwd[i](h, W, Pl)

        sv['x'], sv['yb'] = xc, yb
        saved.append(sv)
        if i < 3:

            def fpost(x_, y_, gpost, gpre):
                xn_ = x_ + rms_fwd(y_, gpost)
                return [xn_, rms_fwd(xn_, gpre)], []
            (xc, h), _ = rowwise(fpost, [rw(xc), rw(yb)], [P['post_norm'][i:i + 1], P['pre_norm'][i + 1:i + 2]],
                                 [(D_MODEL, F32), (D_MODEL, BF16)], [], 256, f'post_pre_norm{i}')
        else:

            def floss(x_, y_, t_, gpost):
                d = x_ + rms_fwd(y_, gpost) - t_
                return [d * (1.0 / D_MODEL)], [0.5 * jnp.sum(jnp.sum(d * d, axis=-1, keepdims=True) * (1.0 / D_MODEL),
                                                             axis=0, keepdims=True)]
            (dx,), (loss_loc,) = rowwise(floss, [rw(xc), rw(yb), rw(loss_target[0])], [P['post_norm'][i:i + 1]],
                                         [(D_MODEL, F32)], [(1, 1)], 256, 'post_norm_loss')
    loss = lax.psum(loss_loc[0, 0], ("x", "y", "c"))

    G, out = {}, {}
    dpre, dpost = [None] * 4, [None] * 4

    def adam_layer(layer, part, land2):
        for n in LAYER_PARAMS[layer]:
            s = _block_shape(n)
            out[n] = adam_param(n, s, SH_OFF[n], P[n], M[n], V[n], chip, part=part, land=land2)

    def fpost_b(y_, d_, g_):
        dy, dg = rms_bwd(y_, g_, d_)
        return [dy], [dg]
    (dyb,), (dpost[3],) = rowwise(fpost_b, [rw(saved[3]['yb']), rw(dx)], [P['post_norm'][3:4]], [(D_MODEL, BF16)],
                                  [(1, D_MODEL)], 256, 'post_norm_bwd3')
    pending = None
    sink = GradSink()
    for i in reversed(range(4)):
        sv = saved[i]
        if pending is None:
            dh, g = bwd[i](dyb, W, Pl, sv, sink=sink)
        elif i > 0:
            dh, g = bwd[i](dyb, W, Pl, sv, comm=ChipExchange(pending[1]), sink=sink)
            adam_layer(pending[0], pending[1], g['carried'][0])
        else:
            early = {}

            def both():
                early['part'] = pair_sums(sink.bufs['a2'], 'a2')
                return Both(ChipExchange(pending[1]), ChipExchange(early['part']))
            dh, g = bwd[i](dyb, W, Pl, sv, comm=both, sink=sink)
            adam_layer(pending[0], pending[1], g['carried'][0])
            adam_layer('a2', early['part'], g['carried'][1])
        g.pop('carried', None)
        G.update(g)

        if i > 0:

            def fpre_b(x_, dh_, d_, y_, gpre, gpost):
                dxl, dg = rms_bwd(x_, gpre, dh_)
                dy, dgp = rms_bwd(y_, gpost, d_ + dxl)
                return [d_ + dxl, dy], [dg, dgp]
            (dx, dyb), (dpre[i], dpost[i - 1]) = rowwise(
                fpre_b, [rw(sv['x']), rw(dh), rw(dx), rw(saved[i - 1]['yb'])],
                [P['pre_norm'][i:i + 1], P['post_norm'][i - 1:i]], [(D_MODEL, F32), (D_MODEL, BF16)],
                [(1, D_MODEL), (1, D_MODEL)], 256, f'pre_post_norm_bwd{i}')
        else:

            def fpre_b0(x_, dh_, d_, g_):
                dxl, dg = rms_bwd(x_, g_, dh_)
                return [d_ + dxl], [dg]
            (dx,), (dpre[i],) = rowwise(fpre_b0, [rw(sv['x']), rw(dh), rw(dx)], [P['pre_norm'][i:i + 1]],
                                        [(D_MODEL, F32)], [(1, D_MODEL)], 256, 'pre_norm_bwd0')

        group = LAYER_GROUPS['abcd'[i]][0]
        for n in LAYER_PARAMS[group]:
            if n in g:
                sink.add(n, g[n])
        pending = (group, pair_sums(sink.bufs[group], group))
    adam_layer(pending[0], pending[1], rs_chips(pending[1], pending[0]))
    G['pre_norm'] = jnp.concatenate(dpre, axis=0)
    G['post_norm'] = jnp.concatenate(dpost, axis=0)

    part = pair_sums(pack_rep(G), 'rep')
    land2 = rs_chips(part, 'rep')
    grep = all_gather(rs_rep_sum(part, land2, chip), 'ag_rep')[:, :REP_CHUNK].reshape(REP_ROWS, LANES)
    small_names = [n for n, s in REP_SHAPE.items() if s[0] <= 64]
    out.update(adam_small(small_names, grep, P, M, V))
    for n, s in REP_SHAPE.items():
        if n not in small_names:
            out[n] = adam_param(n, s, REP_OFF[n], P[n], M[n], V[n], chip, grep=grep)
    res = [loss, dx[None]]
    for kind in range(4):
        res += [out[n][kind].reshape(P[n].shape) for n in WEIGHTS]
    return tuple(res)
```

```python
import functools
import math

import numpy as np
import jax
import jax.numpy as jnp
from jax import lax
from jax.experimental import pallas as pl
from jax.experimental.pallas import tpu as pltpu

F32 = jnp.float32
BF16 = jnp.bfloat16
MESH = pl.DeviceIdType.MESH
ANY = pl.BlockSpec(memory_space=pl.ANY)

N_DEV = 8
D_MODEL = 1024
EPS = 1e-6
NEG_INF = -1e30
SSM_G, SSM_P, SSM_H = 64, 64, 16
SSM_T = 256
SSM_TS = 8
SSM_WC = 512
HEAD_DIM = 64
SWA_HEADS, SWA_KV = 16, 2
WINDOW = 128
REL_BUCKETS, REL_MAX_DIST = 32, 128
MLA_HEADS, MLA_NOPE, MLA_ROPE, MLA_V = 16, 64, 32, 64
MLA_Q_RANK, MLA_KV_RANK = 768, 256
ROPE_BASE = 10000.0
SGU_G, SGU_C, SGU_T = 16, 64, 128
ADAM_LR, ADAM_B1, ADAM_B2, ADAM_EPS, ADAM_WD, ADAM_STEP = 0.001, 0.9, 0.999, 1e-08, 0.01, 10

WEIGHTS = ['pre_norm', 'post_norm', 'rel_bias', 'a_w_in', 'a_lam_re', 'a_lam_im', 'a_log_dt', 'a_b_re', 'a_b_im',
           'a_c_re', 'a_c_im', 'a_d', 'a_w_glu', 'a_b_glu', 'a_w_out', 'b_w_in', 'b_sinks', 'b_w_out', 'c_w_in',
           'c_q_norm', 'c_kv_norm', 'c_w_uq', 'c_w_ukv', 'c_w_out', 'd_w_in', 'd_ln_g', 'd_ln_b', 'd_w_s', 'd_b_s',
           'd_w_out']
SHARDED = {'a_w_in': ((1024, 2048), 1), 'a_w_glu': ((1024, 1024), 0), 'a_w_out': ((1024, 1024), 0),
           'b_w_in': ((1024, 2304), 1), 'b_w_out': ((1024, 1024), 0), 'c_w_in': ((1024, 2080), 1),
           'c_q_norm': ((1, 768), 1), 'c_kv_norm': ((1, 256), 1), 'c_w_uq': ((768, 1536), 1),
           'c_w_ukv': ((256, 2048), 1), 'c_w_out': ((1024, 1024), 0), 'd_w_in': ((1024, 3072), 1),
           'd_ln_g': ((1, 1024), 1), 'd_ln_b': ((1, 1024), 1), 'd_w_out': ((1024, 1024), 0)}
SHARDED_F32 = ['c_q_norm', 'c_kv_norm', 'd_ln_g', 'd_ln_b']
REPLICATED = [n for n in WEIGHTS if n not in SHARDED]


def _cdiv(a, b):
    return -(-a // b)


def _block_shape(name):
    (r, c), ax = SHARDED[name]
    return (r // N_DEV, c) if ax == 0 else (r, c // N_DEV)


LANES = 128
LAYER_PARAMS = {'a1': ['a_w_in'], 'a2': ['a_w_glu', 'a_w_out'], 'b': ['b_w_in', 'b_w_out'],
                'c': ['c_w_in', 'c_w_uq', 'c_w_ukv', 'c_w_out', 'c_q_norm', 'c_kv_norm'],
                'd': ['d_w_in', 'd_w_out', 'd_ln_g', 'd_ln_b']}


def _tiles(shape):
    r, c = shape
    rp = max(r, 16)
    rb = 512 if rp % 512 == 0 else 256 if rp % 256 == 0 else rp
    return rp, _cdiv(c, LANES), rb


SH_OFF, LAYER_ROWS = {}, {}
for _l, _names in LAYER_PARAMS.items():
    _o = 0
    for _n in _names:
        _rp, _nt, _rb = _tiles(_block_shape(_n))
        assert _o % _rb == 0
        SH_OFF[_n] = _o
        _o += _rp * _nt
    assert _o % 16 == 0
    LAYER_ROWS[_l] = _o
GROUP_OF = {_n: _l for _l, _names in LAYER_PARAMS.items() for _n in _names}
LAYER_GROUPS = {'a': ['a1', 'a2'], 'b': ['b'], 'c': ['c'], 'd': ['d']}

REP_SHAPE = {'a_b_re': (4096, 16), 'a_b_im': (4096, 16), 'd_w_s': (2048, 128), 'a_c_re': (1024, 64),
             'a_c_im': (1024, 64), 'pre_norm': (4, 1024), 'post_norm': (4, 1024), 'a_lam_re': (64, 64),
             'a_lam_im': (64, 64), 'a_d': (1, 1024), 'a_b_glu': (1, 1024), 'rel_bias': (32, 16), 'd_b_s': (16, 128),
             'a_log_dt': (1, 64), 'b_sinks': (1, 16)}
REP_OFF = {}
_o = 0
for _n, _s in REP_SHAPE.items():
    _rp, _nt, _rb = _tiles(_s)
    assert _o % _rb == 0
    REP_OFF[_n] = _o
    _o += _rp * _nt
REP_ROWS = _cdiv(_o, 16 * N_DEV) * 16 * N_DEV
REP_CHUNK = REP_ROWS // N_DEV
REP_SLOT = REP_CHUNK

PERM = {'a_w_in': [(0, 2048)], 'd_w_in': [(0, 3072)], 'b_w_in': [(1280, 1024), (0, 1280)],
        'c_w_in': [(1056, 1024), (0, 1056), ('z', 96)],
        'c_w_uq': sum([[(2 * hp * 96, 64), ((2 * hp + 1) * 96, 64), (2 * hp * 96 + 64, 32), ((2 * hp + 1) * 96 + 64, 32),
                        ('z', 64)] for hp in range(8)], []),
        'c_w_ukv': sum([[(2 * hp * 128, 64), ((2 * hp + 1) * 128, 64), (2 * hp * 128 + 64, 64),
                         ((2 * hp + 1) * 128 + 64, 64)] for hp in range(8)], [])}


def perm_index(name):
    return np.concatenate([np.full(p[1], -1) if p[0] == 'z' else np.arange(p[0], p[0] + p[1]) for p in PERM[name]])


SMALL_OFF = {}
_o = 0
for _n in SHARDED_F32:
    SMALL_OFF[_n] = _o
    _o += int(np.prod(_block_shape(_n)))
SMALL_ROWS = _cdiv(_o, 128 * 8) * 8


def _pick(n, cands):
    for c in cands:
        if n % c == 0:
            return c
    return n


def mm(a, b, mode, name, out_dtype=F32):
    if mode == 'nn':
        (M, K), (K2, N) = a.shape, b.shape
    elif mode == 'nt':
        (M, K), (N, K2) = a.shape, b.shape
    else:
        (K, M), (K2, N) = a.shape, b.shape
    assert K == K2, (name, a.shape, b.shape)
    tm = _pick(M, (1024, 768, 512, 256, 128))
    tn = _pick(N, (512, 384, 256))
    dims = {'nn': ((1,), (0,)), 'nt': ((1,), (1,)), 'tn': ((0,), (0,))}[mode]

    def body(a_ref, b_ref, o_ref):
        o_ref[...] = lax.dot_general(a_ref[...].astype(BF16), b_ref[...].astype(BF16), (dims, ((), ())),
                                     preferred_element_type=F32).astype(out_dtype)

    a_spec = pl.BlockSpec((K, tm), lambda i, j: (0, i)) if mode == 'tn' else pl.BlockSpec((tm, K), lambda i, j: (i, 0))
    b_spec = pl.BlockSpec((tn, K), lambda i, j: (j, 0)) if mode == 'nt' else pl.BlockSpec((K, tn), lambda i, j: (0, j))
    return pl.pallas_call(
        body, grid=(M // tm, N // tn), in_specs=[a_spec, b_spec],
        out_specs=pl.BlockSpec((tm, tn), lambda i, j: (i, j)), out_shape=jax.ShapeDtypeStruct((M, N), out_dtype),
        compiler_params=pltpu.CompilerParams(dimension_semantics=("parallel", "parallel")), name=name)(a, b)


def rw(arr, width=None, cb=0):
    return (arr, arr.shape[1] if width is None else width, cb)


def rowwise(fn, rows, consts, outs, accs, tl, name, n_steps=None, comm=None):
    if n_steps is None:
        n_steps = [r[0].shape[0] for r in rows if not isinstance(r[1], pl.BlockSpec)][0] // tl
    L = n_steps * tl
    nr, nc, no, na = len(rows), len(consts), len(outs), len(accs)
    in_specs, args = [], []
    for r in rows:
        if isinstance(r[1], pl.BlockSpec):
            in_specs.append(r[1])
        else:
            in_specs.append(pl.BlockSpec((tl, r[1]), functools.partial(lambda i, cb: (i, cb), cb=r[2])))
        args.append(r[0])
    for c in consts:
        in_specs.append(pl.BlockSpec(c.shape, functools.partial(lambda i, nd: (0,) * nd, nd=c.ndim)))
        args.append(c)
    out_specs = [pl.BlockSpec((tl, w), lambda i: (i, 0)) for w, _ in outs]
    out_shape = [jax.ShapeDtypeStruct((L, w), dt) for w, dt in outs]
    for s in accs:
        out_specs.append(pl.BlockSpec(s, functools.partial(lambda i, nd: (0,) * nd, nd=len(s))))
        out_shape.append(jax.ShapeDtypeStruct(s, F32))

    def body(*refs):
        ins = [r[...] for r in refs[:nr + nc]]
        o_refs = refs[nr + nc:nr + nc + no]
        a_refs = refs[nr + nc + no:]
        o_vals, a_vals = fn(*ins)
        for ref, val in zip(o_refs, o_vals):
            ref[...] = val.astype(ref.dtype)
        if na:
            @pl.when(pl.program_id(0) == 0)
            def _():
                for ref in a_refs:
                    ref[...] = jnp.zeros_like(ref)
            for ref, val in zip(a_refs, a_vals):
                ref[...] += val

    res, carried_out = carried(body, comm, grid=(n_steps,), in_specs=in_specs, out_specs=out_specs, out_shape=out_shape,
                               name=name, semantics=("arbitrary",))(*args)
    if comm is None:
        return res[:no], res[no:]
    return res[:no], res[no:], carried_out


def carried(body, comm, *, grid, in_specs, out_specs, out_shape, name, semantics, scratch_shapes=()):
    single = not isinstance(out_shape, (list, tuple))
    o_specs = [out_specs] if single else list(out_specs)
    o_shape = [out_shape] if single else list(out_shape)
    if comm is None:
        call = pl.pallas_call(body, grid=grid, in_specs=in_specs, out_specs=out_specs, out_shape=out_shape,
                              scratch_shapes=list(scratch_shapes),
                              compiler_params=pltpu.CompilerParams(dimension_semantics=semantics), name=name)
        return lambda *args: (call(*args), None)
    n_in, n_out, n_sc = len(in_specs), len(o_specs), len(scratch_shapes)
    ci, co = len(comm.ins), len(comm.outs)
    n_steps = int(np.prod(grid))
    hooks = comm.hooks(n_steps)

    def wrapped(*refs):
        ins, cins = refs[:n_in], refs[n_in:n_in + ci]
        outs, couts = refs[n_in + ci:n_in + ci + n_out], refs[n_in + ci + n_out:n_in + ci + n_out + co]
        sc, csc = refs[n_in + ci + n_out + co:n_in + ci + n_out + co + n_sc], refs[n_in + ci + n_out + co + n_sc:]
        step = pl.program_id(0)
        for ax in range(1, len(grid)):
            step = step * grid[ax] + pl.program_id(ax)
        for at, fn, after in hooks:
            if not after:
                pl.when(step == at)(functools.partial(fn, cins, couts, csc))
        body(*ins, *outs, *sc)
        for at, fn, after in hooks:
            if after:
                pl.when(step == at)(functools.partial(fn, cins, couts, csc))

    call = pl.pallas_call(wrapped, grid=grid, in_specs=list(in_specs) + [ANY] * ci, out_specs=o_specs + [ANY] * co,
                          out_shape=o_shape + list(comm.outs), scratch_shapes=list(scratch_shapes) + list(comm.scratch),
                          compiler_params=pltpu.CompilerParams(dimension_semantics=("arbitrary",) * len(grid)), name=name)

    def run(*args):
        res = call(*args, *comm.ins)
        return (res[0] if single else res[:n_out]), res[n_out:]
    return run


_K0 = math.sqrt(2.0 / math.pi)
_K1 = 0.044715


def gelu(x):
    return x * (0.5 * (1.0 + jnp.tanh(_K0 * (x + _K1 * (x * x * x)))))


def gelu_grad(x):
    t = jnp.tanh(_K0 * (x + _K1 * (x * x * x)))
    return 0.5 * (1.0 + t) + 0.5 * x * (1.0 - t * t) * (_K0 * (1.0 + 3.0 * _K1 * x * x))


def sigmoid(x):
    return 1.0 / (1.0 + jnp.exp(-x))


def silu(z):
    return z * sigmoid(z)


def silu_grad(z):
    s = sigmoid(z)
    return s * (1.0 + z * (1.0 - s))


def rms_fwd(x, g):
    r = lax.rsqrt(jnp.mean(x * x, axis=-1, keepdims=True) + EPS)
    return x * r * g


def rms_bwd(x, g, dy):
    r = lax.rsqrt(jnp.mean(x * x, axis=-1, keepdims=True) + EPS)
    xh = x * r
    dg = jnp.sum(dy * xh, axis=0, keepdims=True)
    dxh = dy * g
    dx = r * (dxh - xh * jnp.mean(dxh * xh, axis=-1, keepdims=True))
    return dx, dg


def _scan_chunk(a_r, a_i, pr_ref, pi_ref, cr, ci, T, reverse):
    ts = min(SSM_TS, T)
    sgn = -1.0 if reverse else 1.0
    row = lax.broadcasted_iota(jnp.int32, (ts, a_r.shape[1]), 0)
    pw = (lambda e: T - e) if reverse else (lambda e: e - 1)
    if reverse:
        wr_c, wi_c = pr_ref[T - ts:T, :], sgn * pi_ref[T - ts:T, :]
    else:
        wr_c, wi_c = pr_ref[0:ts, :], sgn * pi_ref[0:ts, :]
    c_r, c_i = cr[...], ci[...]
    outs = []
    subs = range(T // ts)
    for sub in (reversed(subs) if reverse else subs):
        v_r, v_i = a_r[sub * ts:(sub + 1) * ts], a_i[sub * ts:(sub + 1) * ts]
        d = 1
        while d < ts:
            wr = pr_ref[pw(d):pw(d) + 1, :]
            wi = sgn * pi_ref[pw(d):pw(d) + 1, :]
            if reverse:
                yr, yi, keep = pltpu.roll(v_r, ts - d, 0), pltpu.roll(v_i, ts - d, 0), row < ts - d
            else:
                yr, yi, keep = pltpu.roll(v_r, d, 0), pltpu.roll(v_i, d, 0), row >= d
            v_r, v_i = (v_r + jnp.where(keep, wr * yr - wi * yi, 0.0), v_i + jnp.where(keep, wr * yi + wi * yr, 0.0))
            d *= 2
        v_r, v_i = v_r + (wr_c * c_r - wi_c * c_i), v_i + (wr_c * c_i + wi_c * c_r)
        k = 0 if reverse else ts - 1
        c_r, c_i = v_r[k:k + 1, :], v_i[k:k + 1, :]
        outs.append((v_r, v_i))
    if reverse:
        outs = outs[::-1]
    cr[...] = c_r
    ci[...] = c_i
    return jnp.concatenate([o[0] for o in outs], axis=0), jnp.concatenate([o[1] for o in outs], axis=0)


_NT = (((1,), (1,)), ((), ()))
_TN = (((0,), (0,)), ((), ()))


def s5_fwd(proj, d_skip, Bre, Bim, Cre, Cim, pr, pi, comm=None):
    L = proj.shape[0]
    T, WC = min(SSM_T, L), SSM_WC
    nT = L // T

    def body(u_ref, d_ref, bre_ref, bim_ref, cre_ref, cim_ref, pr_ref, pi_ref, y_ref, yg_ref, sr_ref, si_ref, cr, ci):
        @pl.when(pl.program_id(1) == 0)
        def _():
            cr[...] = jnp.zeros_like(cr)
            ci[...] = jnp.zeros_like(ci)

        u = u_ref[...]
        ub = u.astype(BF16)
        a_r = jnp.dot(ub, bre_ref[0].astype(BF16), preferred_element_type=F32)
        a_i = jnp.dot(ub, bim_ref[0].astype(BF16), preferred_element_type=F32)
        a_r, a_i = _scan_chunk(a_r, a_i, pr_ref, pi_ref, cr, ci, T, False)
        sr_ref[...] = a_r
        si_ref[...] = a_i
        y = (jnp.dot(a_r.astype(BF16), cre_ref[0].astype(BF16), preferred_element_type=F32)
             + jnp.dot(a_i.astype(BF16), cim_ref[0].astype(BF16), preferred_element_type=F32) + d_ref[...] * u)
        y_ref[...] = y
        yg_ref[...] = gelu(y)

    uspec = pl.BlockSpec((T, 128), lambda k, i: (i, k))
    sspec = pl.BlockSpec((T, WC), lambda k, i: (i, k))
    return carried(
        body, comm, grid=(8, nT),
        in_specs=[uspec, pl.BlockSpec((1, 128), lambda k, i: (0, k)),
                  pl.BlockSpec((1, 128, WC), lambda k, i: (k, 0, 0)), pl.BlockSpec((1, 128, WC), lambda k, i: (k, 0, 0)),
                  pl.BlockSpec((1, WC, 128), lambda k, i: (k, 0, 0)), pl.BlockSpec((1, WC, 128), lambda k, i: (k, 0, 0)),
                  pl.BlockSpec((T, WC), lambda k, i: (0, k)), pl.BlockSpec((T, WC), lambda k, i: (0, k))],
        out_specs=[uspec, uspec, sspec, sspec],
        out_shape=[jax.ShapeDtypeStruct((L, 1024), F32)] * 2 + [jax.ShapeDtypeStruct((L, 8 * WC), F32)] * 2,
        scratch_shapes=[pltpu.VMEM((1, WC), F32), pltpu.VMEM((1, WC), F32)],
        semantics=("parallel", "arbitrary"), name='a_ssm')(proj, d_skip, Bre, Bim, Cre, Cim, pr, pi)


def s5_bwd(proj, dyg1, dyg2, y, d_skip, s_re, s_im, Bre, Bim, Cre, Cim, prr, pir, comm=None):
    L = proj.shape[0]
    T, WC = min(SSM_T, L), SSM_WC
    nT = L // T

    def body(u_ref, g1_ref, g2_ref, y_ref, d_ref, sr_ref, si_ref, spr_ref, spi_ref, bre_ref, bim_ref, cre_ref, cim_ref,
             pr_ref, pi_ref, du_ref, dd_ref, dbre_ref, dbim_ref, dcre_ref, dcim_ref, dar_ref, dai_ref, cr, ci):
        i = pl.program_id(1)

        @pl.when(i == 0)
        def _():
            for ref in (cr, ci, dd_ref, dbre_ref, dbim_ref, dcre_ref, dcim_ref, dar_ref, dai_ref):
                ref[...] = jnp.zeros_like(ref)

        u = u_ref[...]
        dy = (g1_ref[...] + g2_ref[...]) * gelu_grad(y_ref[...])
        dd_ref[...] += jnp.sum(dy * u, axis=0, keepdims=True)
        dyb, ub = dy.astype(BF16), u.astype(BF16)
        bre, bim, cre, cim = (r[0].astype(BF16) for r in (bre_ref, bim_ref, cre_ref, cim_ref))
        g_r = lax.dot_general(dyb, cre, _NT, preferred_element_type=F32)
        g_i = lax.dot_general(dyb, cim, _NT, preferred_element_type=F32)
        g_r, g_i = _scan_chunk(g_r, g_i, pr_ref, pi_ref, cr, ci, T, True)
        s_r, s_i = sr_ref[...], si_ref[...]
        row = lax.broadcasted_iota(jnp.int32, (T, WC), 0)
        first = (nT - 1 - i) == 0
        sp_r = jnp.where(row == 0, jnp.where(first, 0.0, spr_ref[7:8, :]), pltpu.roll(s_r, 1, 0))
        sp_i = jnp.where(row == 0, jnp.where(first, 0.0, spi_ref[7:8, :]), pltpu.roll(s_i, 1, 0))
        dar_ref[...] += jnp.sum(g_r * sp_r + g_i * sp_i, axis=0, keepdims=True)
        dai_ref[...] += jnp.sum(g_i * sp_r - g_r * sp_i, axis=0, keepdims=True)
        grb, gib = g_r.astype(BF16), g_i.astype(BF16)
        dcre_ref[0] += lax.dot_general(s_r.astype(BF16), dyb, _TN, preferred_element_type=F32)
        dcim_ref[0] += lax.dot_general(s_i.astype(BF16), dyb, _TN, preferred_element_type=F32)
        dbre_ref[0] += lax.dot_general(ub, grb, _TN, preferred_element_type=F32)
        dbim_ref[0] += lax.dot_general(ub, gib, _TN, preferred_element_type=F32)
        du_ref[...] = (dy * d_ref[...] + lax.dot_general(grb, bre, _NT, preferred_element_type=F32)
                       + lax.dot_general(gib, bim, _NT, preferred_element_type=F32))

    uspec = pl.BlockSpec((T, 128), lambda k, i: (nT - 1 - i, k))
    sspec = pl.BlockSpec((T, WC), lambda k, i: (nT - 1 - i, k))
    pspec = pl.BlockSpec((8, WC), lambda k, i: (jnp.maximum((nT - 1 - i) * (T // 8) - 1, 0), k))
    tab = pl.BlockSpec((T, WC), lambda k, i: (0, k))
    bspec = pl.BlockSpec((1, 128, WC), lambda k, i: (k, 0, 0))
    cspec = pl.BlockSpec((1, WC, 128), lambda k, i: (k, 0, 0))
    return carried(
        body, comm, grid=(8, nT),
        in_specs=[uspec, uspec, uspec, uspec, pl.BlockSpec((1, 128), lambda k, i: (0, k)), sspec, sspec, pspec, pspec,
                  bspec, bspec, cspec, cspec, tab, tab],
        out_specs=[uspec, pl.BlockSpec((1, 128), lambda k, i: (0, k)), bspec, bspec, cspec, cspec,
                   pl.BlockSpec((1, WC), lambda k, i: (0, k)), pl.BlockSpec((1, WC), lambda k, i: (0, k))],
        out_shape=[jax.ShapeDtypeStruct((L, 1024), F32), jax.ShapeDtypeStruct((1, 1024), F32),
                   jax.ShapeDtypeStruct((8, 128, WC), F32), jax.ShapeDtypeStruct((8, 128, WC), F32),
                   jax.ShapeDtypeStruct((8, WC, 128), F32), jax.ShapeDtypeStruct((8, WC, 128), F32),
                   jax.ShapeDtypeStruct((1, 8 * WC), F32), jax.ShapeDtypeStruct((1, 8 * WC), F32)],
        scratch_shapes=[pltpu.VMEM((1, WC), F32), pltpu.VMEM((1, WC), F32)],
        semantics=("parallel", "arbitrary"), name='a_ssm_bwd')(
            proj, dyg1, dyg2, y, d_skip, s_re, s_im, s_re, s_im, Bre, Bim, Cre, Cim, prr, pir)


def s5_discretize(lam_re, lam_im, log_dt, b_re, b_im):
    dt = jnp.exp(log_dt)[:, None]
    mag = jnp.exp(lam_re * dt)
    ab_re = mag * jnp.cos(lam_im * dt)
    ab_im = mag * jnp.sin(lam_im * dt)
    den = lam_re * lam_re + lam_im * lam_im
    nr = ab_re - 1.0
    f_re = (nr * lam_re + ab_im * lam_im) / den
    f_im = (ab_im * lam_re - nr * lam_im) / den
    bb_re = f_re[..., None] * b_re - f_im[..., None] * b_im
    bb_im = f_re[..., None] * b_im + f_im[..., None] * b_re
    return ab_re, ab_im, bb_re, bb_im


_EYE8 = np.eye(8, dtype=np.float32)


def _b_tiles(bb):
    t = bb.transpose(0, 2, 1).reshape(8, 8, SSM_H, SSM_P)
    return jnp.einsum('kghp,gG->kghGp', t, _EYE8).reshape(8, 8 * SSM_H, 8 * SSM_P)


def _b_untile(d):
    t = jnp.einsum('kghGp,gG->kghp', d.reshape(8, 8, SSM_H, 8, SSM_P), _EYE8)
    return t.reshape(SSM_G, SSM_H, SSM_P).transpose(0, 2, 1)


def _c_tiles(c):
    t = c.transpose(0, 2, 1).reshape(8, 8, SSM_P, SSM_H)
    return jnp.einsum('kgph,gG->kgpGh', t, _EYE8).reshape(8, 8 * SSM_P, 8 * SSM_H)


def _c_untile(d):
    t = jnp.einsum('kgpGh,gG->kgph', d.reshape(8, 8, SSM_P, 8, SSM_H), _EYE8)
    return t.reshape(SSM_G, SSM_P, SSM_H).transpose(0, 2, 1)


def s5_powers(ar, ai, T):
    W = ar.shape[1]

    def body(ar_ref, ai_ref, fr_ref, fi_ref, rr_ref, ri_ref):
        fr_ref[0:1, :] = ar_ref[...]
        fi_ref[0:1, :] = ai_ref[...]
        rr_ref[T - 1:T, :] = ar_ref[...]
        ri_ref[T - 1:T, :] = ai_ref[...]
        n = 1
        while n < T:
            cr, ci = fr_ref[0:n, :], fi_ref[0:n, :]
            lr, li = fr_ref[n - 1:n, :], fi_ref[n - 1:n, :]
            fr_ref[n:2 * n, :] = cr * lr - ci * li
            fi_ref[n:2 * n, :] = cr * li + ci * lr
            cr, ci = rr_ref[T - n:T, :], ri_ref[T - n:T, :]
            rr_ref[T - 2 * n:T - n, :] = cr * lr - ci * li
            ri_ref[T - 2 * n:T - n, :] = cr * li + ci * lr
            n *= 2

    spec = pl.BlockSpec((T, SSM_WC), lambda j: (0, j))
    aspec = pl.BlockSpec((1, SSM_WC), lambda j: (0, j))
    return pl.pallas_call(
        body, grid=(W // SSM_WC,), in_specs=[aspec, aspec], out_specs=[spec] * 4,
        out_shape=[jax.ShapeDtypeStruct((T, W), F32)] * 4,
        compiler_params=pltpu.CompilerParams(dimension_semantics=("parallel",)), name='a_powers')(ar, ai)


def layer_a_fwd(h, w, p, comm=None, on_carried=None):
    L = h.shape[0]
    proj = mm(h, w['a_w_in'], 'nn', 'a_proj')
    disc = lambda *a: s5_discretize(*a)
    (ab_re, ab_im, bb_re, bb_im), disc_vjp = jax.vjp(disc, p['a_lam_re'][0], p['a_lam_im'][0], p['a_log_dt'][0],
                                                     p['a_b_re'][0], p['a_b_im'][0])
    Bre, Bim = _b_tiles(bb_re), _b_tiles(bb_im)
    Cre, Cim = _c_tiles(p['a_c_re'][0]), -_c_tiles(p['a_c_im'][0])
    T = min(SSM_T, L)
    pr, pi, prr, pir = s5_powers(ab_re.reshape(1, -1), ab_im.reshape(1, -1), T)
    (y, yg, s_re, s_im), carried_out = s5_fwd(proj, p['a_d'], Bre, Bim, Cre, Cim, pr, pi, comm=comm)
    if on_carried is not None:
        on_carried(carried_out)
    gl = mm(yg, w['a_w_glu'], 'nn', 'a_glu')

    def f2(yg_, gl_, z, bg):
        return [yg_ * sigmoid(gl_ + bg) * silu(z)], []
    (po,), _ = rowwise(f2, [rw(yg), rw(gl), rw(proj, 1024, 1)], [p['a_b_glu']], [(1024, BF16)], [], 256, 'a_gate')
    yb = mm(po, w['a_w_out'], 'nn', 'a_out')
    saved = dict(carried=carried_out, h=h, proj=proj, disc_vjp=disc_vjp, Bre=Bre, Bim=Bim, Cre=Cre, Cim=Cim, prr=prr, pir=pir, s_re=s_re,
                 s_im=s_im, y=y, yg=yg, gl=gl, po=po)
    return yb, saved


def _dw(g, sink, name, a, b, mm_name):
    if sink is None:
        g[name] = mm(a, b, 'tn', mm_name)
    else:
        sink.put(name, a, b, mm_name)


def layer_a_bwd(dyb, w, p, sv, comm=None, sink=None):
    g = {}
    dpo = mm(dyb, w['a_w_out'], 'nt', 'a_dpo')
    _dw(g, sink, 'a_w_out', sv['po'], dyb, 'a_dwout')
    proj = sv['proj']

    def f1(dpo_, yg, gl, z, bg):
        sg = sigmoid(gl + bg)
        sz = silu(z)
        dm = dpo_ * sz
        dz = dpo_ * (yg * sg) * silu_grad(z)
        dgl = dm * yg * sg * (1.0 - sg)
        return [dz, dm * sg, dgl], [jnp.sum(dgl, axis=0, keepdims=True)]
    (dz, dyg1, dgl), (db_glu,) = rowwise(f1, [rw(dpo), rw(sv['yg']), rw(sv['gl']), rw(proj, 1024, 1)], [p['a_b_glu']],
                                          [(1024, F32), (1024, F32), (1024, BF16)], [(1, 1024)], 256, 'a_gate_bwd')
    g['a_b_glu'] = db_glu
    _dw(g, sink, 'a_w_glu', sv['yg'], dgl, 'a_dwglu')
    dyg2 = mm(dgl, w['a_w_glu'], 'nt', 'a_dyg2')

    if callable(comm):
        comm = comm()
    (du, dd, dBre, dBim, dCre, dCim, da_re, da_im), g['carried'] = s5_bwd(
        proj, dyg1, dyg2, sv['y'], p['a_d'], sv['s_re'], sv['s_im'], sv['Bre'], sv['Bim'], sv['Cre'], sv['Cim'],
        sv['prr'], sv['pir'], comm=comm)
    g['a_d'] = dd
    dCim = -dCim

    def f3(du_, dz_):
        return [jnp.concatenate([du_, dz_], axis=1)], []
    (dproj,), _ = rowwise(f3, [rw(du), rw(dz)], [], [(2048, BF16)], [], 256, 'a_dproj')
    dlr, dli, dldt, dbr, dbi = sv['disc_vjp']((da_re.reshape(SSM_G, SSM_P), da_im.reshape(SSM_G, SSM_P),
                                               _b_untile(dBre), _b_untile(dBim)))
    g['a_lam_re'], g['a_lam_im'], g['a_log_dt'] = dlr[None], dli[None], dldt[None]
    g['a_b_re'], g['a_b_im'] = dbr[None], dbi[None]
    g['a_c_re'], g['a_c_im'] = _c_untile(dCre)[None], _c_untile(dCim)[None]
    _dw(g, sink, 'a_w_in', sv['h'], dproj, 'a_dwin')
    dh = mm(dproj, w['a_w_in'], 'nt', 'a_dh')
    return dh, g


def _t5_bucket_np():
    qi = np.arange(WINDOW)[:, None]
    kj = np.arange(2 * WINDOW)[None, :]
    dist = np.maximum(qi + WINDOW - kj, 0)
    max_exact = REL_BUCKETS // 2
    dist_f = np.maximum(dist, 1).astype(np.float32)
    large = max_exact + (np.log(dist_f / np.float32(max_exact)) / np.float32(math.log(REL_MAX_DIST / max_exact))
                         * np.float32(REL_BUCKETS - max_exact)).astype(np.int32)
    large = np.minimum(large, REL_BUCKETS - 1)
    return np.where(dist < max_exact, dist, large).astype(np.int32)


SWA_GRP = SWA_HEADS // SWA_KV


def _swa_kv(kvp, kvc, kvh):
    kb = jnp.concatenate([kvp[:, kvh * 64:(kvh + 1) * 64], kvc[:, kvh * 64:(kvh + 1) * 64]], 0).astype(BF16)
    vb = jnp.concatenate([kvp[:, 128 + kvh * 64:128 + (kvh + 1) * 64], kvc[:, 128 + kvh * 64:128 + (kvh + 1) * 64]],
                         0).astype(BF16)
    return kb, vb


def _swa_stack(x, kvh):
    return jnp.concatenate([x[:, (kvh * SWA_GRP + g) * 64:(kvh * SWA_GRP + g + 1) * 64] for g in range(SWA_GRP)],
                           axis=0).astype(BF16)


def _swa_group(bias_ref, kvh):
    return bias_ref[kvh * SWA_GRP:(kvh + 1) * SWA_GRP].reshape(SWA_GRP * WINDOW, 2 * WINDOW)


def _swa_sinks(sink_ref, kvh):
    return jnp.concatenate([jnp.broadcast_to(sink_ref[0:1, kvh * SWA_GRP + g:kvh * SWA_GRP + g + 1], (WINDOW, 1))
                            for g in range(SWA_GRP)], axis=0)


def _swa_probs(q, kb, bias_h, sink, valid):
    s = lax.dot_general(q, kb, (((1,), (1,)), ((), ())), preferred_element_type=F32) * (HEAD_DIM ** -0.5)
    s = jnp.where(valid, s + bias_h, NEG_INF)
    m = jnp.maximum(jnp.max(s, axis=-1, keepdims=True), sink)
    e = jnp.exp(s - m)
    es = jnp.exp(sink - m)
    den = jnp.sum(e, axis=-1, keepdims=True) + es
    return e / den, es / den


def _swa_valid(n):
    qi = lax.broadcasted_iota(jnp.int32, (SWA_GRP * WINDOW, 2 * WINDOW), 0) & (WINDOW - 1)
    kj = lax.broadcasted_iota(jnp.int32, (SWA_GRP * WINDOW, 2 * WINDOW), 1)
    dist = qi + WINDOW - kj
    return (dist >= 0) & (dist < WINDOW) & ((kj >= WINDOW) | (n > 0))


def swa_fwd(proj, bias, sinks, comm=None):
    L = proj.shape[0]

    def body(z_ref, q_ref, kvc_ref, kvp_ref, bias_ref, sink_ref, o_ref, po_ref):
        n = pl.program_id(0)
        valid = _swa_valid(n)
        q, kvc, kvp = q_ref[...], kvc_ref[...], kvp_ref[...]
        outs = []
        for kvh in range(SWA_KV):
            kb, vb = _swa_kv(kvp, kvc, kvh)
            p, _ = _swa_probs(_swa_stack(q, kvh), kb, _swa_group(bias_ref, kvh), _swa_sinks(sink_ref, kvh), valid)
            o8 = jnp.dot(p.astype(BF16), vb, preferred_element_type=F32)
            outs += [o8[g * WINDOW:(g + 1) * WINDOW] for g in range(SWA_GRP)]
        o = jnp.concatenate(outs, axis=1)
        o_ref[...] = o
        po_ref[...] = (o * silu(z_ref[...])).astype(po_ref.dtype)

    return carried(
        body, comm, grid=(L // WINDOW,),
        in_specs=[pl.BlockSpec((WINDOW, 1024), lambda n: (n, 0)), pl.BlockSpec((WINDOW, 1024), lambda n: (n, 1)),
                  pl.BlockSpec((WINDOW, 256), lambda n: (n, 8)),
                  pl.BlockSpec((WINDOW, 256), lambda n: (jnp.maximum(n - 1, 0), 8)),
                  pl.BlockSpec((SWA_HEADS, WINDOW, 2 * WINDOW), lambda n: (0, 0, 0)),
                  pl.BlockSpec((1, SWA_HEADS), lambda n: (0, 0))],
        out_specs=[pl.BlockSpec((WINDOW, 1024), lambda n: (n, 0))] * 2,
        out_shape=[jax.ShapeDtypeStruct((L, 1024), F32), jax.ShapeDtypeStruct((L, 1024), BF16)],
        semantics=("parallel",), name='b_attn')(proj, proj, proj, proj, bias, sinks)


def swa_bwd(proj, do, bias, sinks, comm=None):
    L = proj.shape[0]

    def body(q_ref, kvc_ref, kvp_ref, do_ref, bias_ref, sink_ref, dq_ref, dkv_ref, dbias_ref, dsink_ref):
        n = pl.program_id(0)

        @pl.when(n == 0)
        def _():
            dkv_ref[...] = jnp.zeros_like(dkv_ref)
            dbias_ref[...] = jnp.zeros_like(dbias_ref)
            dsink_ref[...] = jnp.zeros_like(dsink_ref)

        valid = _swa_valid(n)
        q, kvc, kvp, do_ = q_ref[...], kvc_ref[...], kvp_ref[...], do_ref[...]
        dqs, dks, dvs, dsk = [], [], [], []
        for kvh in range(SWA_KV):
            kb, vb = _swa_kv(kvp, kvc, kvh)
            q8, do8 = _swa_stack(q, kvh), _swa_stack(do_, kvh)
            p, ps = _swa_probs(q8, kb, _swa_group(bias_ref, kvh), _swa_sinks(sink_ref, kvh), valid)
            dp = lax.dot_general(do8, vb, (((1,), (1,)), ((), ())), preferred_element_type=F32)
            delta = jnp.sum(p * dp, axis=-1, keepdims=True)
            ds = p * (dp - delta)
            col = -ps * delta
            dsk += [jnp.sum(col[g * WINDOW:(g + 1) * WINDOW], axis=0, keepdims=True) for g in range(SWA_GRP)]
            dbias_ref[kvh * SWA_GRP:(kvh + 1) * SWA_GRP] += ds.reshape(SWA_GRP, WINDOW, 2 * WINDOW)
            dsb = (ds * (HEAD_DIM ** -0.5)).astype(BF16)
            dq8 = jnp.dot(dsb, kb, preferred_element_type=F32)
            dqs += [dq8[g * WINDOW:(g + 1) * WINDOW] for g in range(SWA_GRP)]
            dks.append(lax.dot_general(dsb, q8, (((0,), (0,)), ((), ())), preferred_element_type=F32))
            dvs.append(lax.dot_general(p.astype(BF16), do8, (((0,), (0,)), ((), ())), preferred_element_type=F32))
        dq_ref[...] = jnp.concatenate(dqs, axis=1)
        dsink_ref[...] += jnp.concatenate(dsk, axis=1)
        both = jnp.concatenate(dks + dvs, axis=1)
        r_cur = pl.multiple_of(n * WINDOW, WINDOW)
        r_prev = pl.multiple_of(jnp.maximum(n - 1, 0) * WINDOW, WINDOW)
        dkv_ref[pl.ds(r_prev, WINDOW), :] += both[:WINDOW]
        dkv_ref[pl.ds(r_cur, WINDOW), :] += both[WINDOW:]

    return carried(
        body, comm, grid=(L // WINDOW,),
        in_specs=[pl.BlockSpec((WINDOW, 1024), lambda n: (n, 1)), pl.BlockSpec((WINDOW, 256), lambda n: (n, 8)),
                  pl.BlockSpec((WINDOW, 256), lambda n: (jnp.maximum(n - 1, 0), 8)),
                  pl.BlockSpec((WINDOW, 1024), lambda n: (n, 0)),
                  pl.BlockSpec((SWA_HEADS, WINDOW, 2 * WINDOW), lambda n: (0, 0, 0)),
                  pl.BlockSpec((1, SWA_HEADS), lambda n: (0, 0))],
        out_specs=[pl.BlockSpec((WINDOW, 1024), lambda n: (n, 0)), pl.BlockSpec((L, 256), lambda n: (0, 0)),
                   pl.BlockSpec((SWA_HEADS, WINDOW, 2 * WINDOW), lambda n: (0, 0, 0)),
                   pl.BlockSpec((1, SWA_HEADS), lambda n: (0, 0))],
        out_shape=[jax.ShapeDtypeStruct((L, 1024), F32), jax.ShapeDtypeStruct((L, 256), F32),
                   jax.ShapeDtypeStruct((SWA_HEADS, WINDOW, 2 * WINDOW), F32), jax.ShapeDtypeStruct((1, SWA_HEADS), F32)],
        semantics=("arbitrary",), name='b_attn_bwd')(proj, proj, proj, do, bias, sinks)


def swa_bias(rel_bias):
    def body(bk_ref, rb_ref, o_ref):
        bk = bk_ref[...]
        for h in range(SWA_HEADS):
            acc = jnp.zeros((WINDOW, 2 * WINDOW), F32)
            for b in range(REL_BUCKETS):
                acc = jnp.where(bk == b, rb_ref[b, h], acc)
            o_ref[h] = acc

    return pl.pallas_call(
        body, out_shape=jax.ShapeDtypeStruct((SWA_HEADS, WINDOW, 2 * WINDOW), F32),
        in_specs=[pl.BlockSpec(memory_space=pltpu.VMEM), pl.BlockSpec(memory_space=pltpu.SMEM)],
        out_specs=pl.BlockSpec(memory_space=pltpu.VMEM), name='b_bias')(jnp.asarray(_t5_bucket_np()), rel_bias)


def layer_b_fwd(h, w, p, comm=None):
    proj = mm(h, w['b_w_in'], 'nn', 'b_proj')
    bias = swa_bias(p['rel_bias'])
    (o, po), carried_out = swa_fwd(proj, bias, p['b_sinks'], comm=comm)
    yb = mm(po, w['b_w_out'], 'nn', 'b_out')
    return yb, dict(carried=carried_out, h=h, proj=proj, bias=bias, o=o, po=po)


def layer_b_bwd(dyb, w, p, sv, comm=None, sink=None):
    g = {}
    dpo = mm(dyb, w['b_w_out'], 'nt', 'b_dpo')
    _dw(g, sink, 'b_w_out', sv['po'], dyb, 'b_dwout')
    proj = sv['proj']

    def f1(dpo_, o, z):
        return [dpo_ * silu(z), dpo_ * o * silu_grad(z)], []
    (do, dz), _ = rowwise(f1, [rw(dpo), rw(sv['o']), rw(proj, 1024, 0)], [], [(1024, BF16), (1024, F32)], [], 256, 'b_gate_bwd')
    (dq, dkv, dbias, dsinks), g['carried'] = swa_bwd(proj, do, sv['bias'], p['b_sinks'], comm=comm)
    g['b_sinks'] = dsinks
    onehot = jnp.asarray(np.eye(REL_BUCKETS, dtype=np.float32)[_t5_bucket_np().reshape(-1)])

    def f2(db, oh):
        return [], [lax.dot_general(db, oh, (((1,), (0,)), ((), ())), preferred_element_type=F32,
                                    precision=lax.Precision.HIGHEST)]
    _, (drel,) = rowwise(f2, [(dbias.reshape(SWA_HEADS, -1), pl.BlockSpec((SWA_HEADS, 4096), lambda i: (0, i))),
                              (onehot, pl.BlockSpec((4096, REL_BUCKETS), lambda i: (i, 0)))], [], [],
                         [(SWA_HEADS, REL_BUCKETS)], 4096, 'b_drel', n_steps=(2 * WINDOW * WINDOW) // 4096)
    g['rel_bias'] = drel.T

    def f3(dz_, dq_, dkv_):
        return [jnp.concatenate([dz_, dq_, dkv_], axis=1)], []
    (dproj,), _ = rowwise(f3, [rw(dz), rw(dq), rw(dkv)], [], [(2304, BF16)], [], 256, 'b_dproj')
    _dw(g, sink, 'b_w_in', sv['h'], dproj, 'b_dwin')
    dh = mm(dproj, w['b_w_in'], 'nt', 'b_dh')
    return dh, g


MLA_SCALE = (MLA_NOPE + MLA_ROPE) ** -0.5


def _rope_tables(L):
    inv = ROPE_BASE ** (-jnp.arange(0, MLA_ROPE, 2, dtype=F32) / MLA_ROPE)
    ang = jnp.arange(L, dtype=F32)[:, None] * inv[None, :]
    c, s = jnp.cos(ang), jnp.sin(ang)
    one, zero, pad = jnp.ones((L, 128), F32), jnp.zeros((L, 128), F32), jnp.zeros((L, 64), F32)
    return (jnp.concatenate([one, c, c, c, c, pad], 1), jnp.concatenate([zero, s, s, s, s, pad], 1))


def _rot(x, transpose=False):
    w = x.shape[1]
    lane = lax.broadcasted_iota(jnp.int32, x.shape, 1)
    up = pltpu.roll(x, w - 16, 1)
    dn = pltpu.roll(x, 16, 1)
    first = (lane % 32) < 16
    return jnp.where(first, up, -dn) if transpose else jnp.where(first, -up, dn)


MLA_QT = 512


def _mla_exp(qf, kf, t, qt):
    n_k = kf.shape[0]
    s = lax.dot_general(qf, kf, (((1,), (1,)), ((), ())), preferred_element_type=F32) * MLA_SCALE
    qpos = t * qt + lax.broadcasted_iota(jnp.int32, (qt, n_k), 0)
    kpos = lax.broadcasted_iota(jnp.int32, (qt, n_k), 1)
    s = jnp.where(kpos <= qpos, s, NEG_INF)
    e = jnp.exp(s - jnp.max(s, axis=-1, keepdims=True))
    return e, jnp.sum(e, axis=-1, keepdims=True)


def _mla_heads(q, kv, kr):
    out = []
    for j in range(2):
        qf = jnp.concatenate([q[:, j * 64:(j + 1) * 64], q[:, 128 + j * 32:128 + (j + 1) * 32]], axis=1)
        kf = jnp.concatenate([kv[:, j * 64:(j + 1) * 64], kr], axis=1)
        out.append((qf, kf, kv[:, 128 + j * 64:128 + (j + 1) * 64]))
    return out


def mla_fwd(q, kv, kr, comm=None):
    L = q.shape[0]
    qt = min(MLA_QT, L)
    nq = L // qt

    def body(q_ref, kv_ref, kr_ref, o_ref):
        for t in range(nq):
            @pl.when(pl.program_id(1) == t)
            def _(t=t):
                n_k = (t + 1) * qt
                outs = []
                for qf, kf, v in _mla_heads(q_ref[...], kv_ref[0:n_k, :], kr_ref[0:n_k, 0:MLA_ROPE]):
                    e, den = _mla_exp(qf, kf, t, qt)
                    outs.append(jnp.dot(e.astype(BF16), v, preferred_element_type=F32) / den)
                o_ref[...] = jnp.concatenate(outs, axis=1)

    return carried(
        body, comm, grid=(MLA_HEADS // 2, nq),
        in_specs=[pl.BlockSpec((qt, 256), lambda hp, n: (n, hp)), pl.BlockSpec((L, 256), lambda hp, n: (0, hp)),
                  pl.BlockSpec((L, 128), lambda hp, n: (0, 0))],
        out_specs=pl.BlockSpec((qt, 128), lambda hp, n: (n, hp)), out_shape=jax.ShapeDtypeStruct((L, 1024), F32),
        semantics=("parallel", "parallel"), name='c_attn')(q, kv, kr)


def mla_bwd(q, kv, kr, do, comm=None):
    L = q.shape[0]
    qt = min(MLA_QT, L)
    nq = L // qt

    def body(q_ref, kv_ref, kr_ref, do_ref, dq_ref, dkv_ref, dkr_ref):
        @pl.when(pl.program_id(1) == 0)
        def _():
            dkv_ref[...] = jnp.zeros_like(dkv_ref)
            dkr_ref[...] = jnp.zeros_like(dkr_ref)

        for t in range(nq):
            @pl.when(pl.program_id(1) == t)
            def _(t=t):
                n_k = (t + 1) * qt
                do_ = do_ref[...]
                dqn, dqr, dkn, dvs = [], [], [], []
                dkr = jnp.zeros((n_k, MLA_ROPE), F32)
                for j, (qf, kf, v) in enumerate(_mla_heads(q_ref[...], kv_ref[0:n_k, :], kr_ref[0:n_k, 0:MLA_ROPE])):
                    doh = do_[:, j * 64:(j + 1) * 64]
                    e, den = _mla_exp(qf, kf, t, qt)
                    p = e * (1.0 / den)
                    dp = lax.dot_general(doh, v, (((1,), (1,)), ((), ())), preferred_element_type=F32)
                    ds = (p * (dp - jnp.sum(p * dp, axis=-1, keepdims=True)) * MLA_SCALE).astype(BF16)
                    dqf = jnp.dot(ds, kf, preferred_element_type=F32)
                    dkf = lax.dot_general(ds, qf, (((0,), (0,)), ((), ())), preferred_element_type=F32)
                    dvs.append(lax.dot_general(p.astype(BF16), doh, (((0,), (0,)), ((), ())), preferred_element_type=F32))
                    dqn.append(dqf[:, :MLA_NOPE])
                    dqr.append(dqf[:, MLA_NOPE:])
                    dkn.append(dkf[:, :MLA_NOPE])
                    dkr = dkr + dkf[:, MLA_NOPE:]
                dq_ref[...] = jnp.concatenate(dqn + dqr + [jnp.zeros((qt, 64), F32)], axis=1)
                dkv_ref[0:n_k, :] += jnp.concatenate(dkn + dvs, axis=1)
                dkr_ref[0, 0:n_k, :] += jnp.concatenate([dkr, jnp.zeros((n_k, 128 - MLA_ROPE), F32)], axis=1)

    return carried(
        body, comm, grid=(MLA_HEADS // 2, nq),
        in_specs=[pl.BlockSpec((qt, 256), lambda hp, n: (n, hp)), pl.BlockSpec((L, 256), lambda hp, n: (0, hp)),
                  pl.BlockSpec((L, 128), lambda hp, n: (0, 0)), pl.BlockSpec((qt, 128), lambda hp, n: (n, hp))],
        out_specs=[pl.BlockSpec((qt, 256), lambda hp, n: (n, hp)), pl.BlockSpec((L, 256), lambda hp, n: (0, hp)),
                   pl.BlockSpec((1, L, 128), lambda hp, n: (hp, 0, 0))],
        out_shape=[jax.ShapeDtypeStruct((L, 2048), F32), jax.ShapeDtypeStruct((L, 2048), F32),
                   jax.ShapeDtypeStruct((MLA_HEADS // 2, L, 128), F32)],
        semantics=("parallel", "arbitrary"), name='c_attn_bwd')(q, kv, kr, do)


def layer_c_fwd(h, w, p, comm=None):
    L = h.shape[0]
    proj = mm(h, w['c_w_in'], 'nn', 'c_proj')

    def f1(c, gq, gk):
        return [rms_fwd(c[:, :768], gq), rms_fwd(c[:, 768:], gk)], []
    (cqn, ckvn), _ = rowwise(f1, [rw(proj, 1024, 1)], [p['c_q_norm'], p['c_kv_norm']], [(768, BF16), (256, BF16)], [],
                             256, 'c_norms')
    qf = mm(cqn, w['c_w_uq'], 'nn', 'c_uq')
    kvf = mm(ckvn, w['c_w_ukv'], 'nn', 'c_ukv', out_dtype=BF16)
    cos, sin = _rope_tables(L)

    def f2(q_, kr_, c, s):
        c8, s8 = jnp.tile(c, (1, 8)), jnp.tile(s, (1, 8))
        return [q_ * c8 + _rot(q_) * s8, kr_ * c[:, 128:] + _rot(kr_) * s[:, 128:]], []
    (q, kr), _ = rowwise(f2, [rw(qf), rw(proj, 128, 16), rw(cos), rw(sin)], [], [(2048, BF16), (128, BF16)], [], 256,
                         'c_rope')
    o, carried_out = mla_fwd(q, kvf, kr, comm=comm)

    def f3(o_, z):
        return [o_ * silu(z)], []
    (po,), _ = rowwise(f3, [rw(o), rw(proj, 1024, 0)], [], [(1024, BF16)], [], 256, 'c_gate')
    yb = mm(po, w['c_w_out'], 'nn', 'c_out')
    return yb, dict(carried=carried_out, h=h, proj=proj, cqn=cqn, ckvn=ckvn, q=q, kv=kvf, kr=kr, o=o, po=po, cos=cos, sin=sin)


def layer_c_bwd(dyb, w, p, sv, comm=None, sink=None):
    g = {}
    dpo = mm(dyb, w['c_w_out'], 'nt', 'c_dpo')
    _dw(g, sink, 'c_w_out', sv['po'], dyb, 'c_dwout')
    proj = sv['proj']
    L = proj.shape[0]

    def f1(dpo_, o, z):
        return [dpo_ * silu(z), dpo_ * o * silu_grad(z)], []
    (do, dz), _ = rowwise(f1, [rw(dpo), rw(sv['o']), rw(proj, 1024, 0)], [], [(1024, BF16), (1024, F32)], [], 256,
                          'c_gate_bwd')
    (dq, dkvf, dkr8), g['carried'] = mla_bwd(sv['q'], sv['kv'], sv['kr'], do, comm=comm)

    def f2(dq_, dkr_, c, s):
        c8, s8 = jnp.tile(c, (1, 8)), jnp.tile(s, (1, 8))
        dk = jnp.sum(dkr_, axis=0)
        return [dq_ * c8 + _rot(dq_ * s8, True), dk * c[:, 128:] + _rot(dk * s[:, 128:], True)], []
    tl = 256
    (dqf, dkr), _ = rowwise(f2, [rw(dq), (dkr8, pl.BlockSpec((8, tl, 128), lambda i: (0, i, 0))), rw(sv['cos']),
                                 rw(sv['sin'])], [], [(2048, BF16), (128, F32)], [], tl, 'c_rope_bwd')
    _dw(g, sink, 'c_w_uq', sv['cqn'], dqf, 'c_dwuq')
    _dw(g, sink, 'c_w_ukv', sv['ckvn'], dkvf, 'c_dwukv')
    dcqn = mm(dqf, w['c_w_uq'], 'nt', 'c_dcqn')
    dckvn = mm(dkvf, w['c_w_ukv'], 'nt', 'c_dckvn')

    def f3(c, dq_, dk_, dz_, dkr_, gq, gk):
        dcq, dgq = rms_bwd(c[:, :768], gq, dq_)
        dckv, dgk = rms_bwd(c[:, 768:], gk, dk_)
        return [jnp.concatenate([dz_, dcq, dckv, dkr_], axis=1)], [dgq, dgk]
    (dproj,), (dgq, dgk) = rowwise(f3, [rw(proj, 1024, 1), rw(dcqn), rw(dckvn), rw(dz), rw(dkr)],
                                   [p['c_q_norm'], p['c_kv_norm']], [(2176, BF16)], [(1, 768), (1, 256)], 256, 'c_dproj')
    g['c_q_norm'], g['c_kv_norm'] = dgq, dgk
    _dw(g, sink, 'c_w_in', sv['h'], dproj, 'c_dwin')
    dh = mm(dproj, w['c_w_in'], 'nt', 'c_dh')
    return dh, g


def _sgu_mix(wm, v, transpose):
    outs = []
    dims = (((0,), (0,)), ((), ())) if transpose else (((1,), (0,)), ((), ()))
    for gi in range(SGU_G):
        outs.append(lax.dot_general(wm[gi], v[:, gi * SGU_C:(gi + 1) * SGU_C].astype(BF16), dims,
                                    preferred_element_type=F32))
    return jnp.concatenate(outs, axis=1)


def _sgu_wmask(ws):
    t = lax.broadcasted_iota(jnp.int32, (SGU_T, SGU_T), 0)
    s = lax.broadcasted_iota(jnp.int32, (SGU_T, SGU_T), 1)
    return jnp.where((s <= t)[None], ws, 0.0).astype(BF16)


def _ln_stats(v):
    mu = jnp.mean(v, axis=-1, keepdims=True)
    vc = v - mu
    rstd = lax.rsqrt(jnp.mean(vc * vc, axis=-1, keepdims=True) + EPS)
    return vc * rstd, rstd


def layer_d_fwd(h, w, p):
    proj = mm(h, w['d_w_in'], 'nn', 'd_proj')
    bias = jnp.repeat(p['d_b_s'][0].T, SGU_C, axis=1)

    def f1(u_, v_, z, ws, lg, lb, bs):
        xh, _ = _ln_stats(gelu(v_))
        s = _sgu_mix(_sgu_wmask(ws), xh * lg + lb, False) + bs
        return [gelu(u_) * s * silu(z)], []
    (po,), _ = rowwise(f1, [rw(proj, 1024, 0), rw(proj, 1024, 1), rw(proj, 1024, 2)],
                       [p['d_w_s'][0], p['d_ln_g'], p['d_ln_b'], bias], [(1024, BF16)], [], SGU_T, 'd_mix')
    yb = mm(po, w['d_w_out'], 'nn', 'd_out')
    return yb, dict(h=h, proj=proj, po=po, bias=bias)


def layer_d_bwd(dyb, w, p, sv, sink=None):
    g = {}
    dpo = mm(dyb, w['d_w_out'], 'nt', 'd_dpo')
    _dw(g, sink, 'd_w_out', sv['po'], dyb, 'd_dwout')
    proj = sv['proj']

    def f1(dpo_, u_, v_, z, ws, lg, lb, bs):
        wm = _sgu_wmask(ws)
        gv = gelu(v_)
        xh, rstd = _ln_stats(gv)
        vn = xh * lg + lb
        s = _sgu_mix(wm, vn, False) + bs
        gu, sz = gelu(u_), silu(z)
        du = dpo_ * s * sz
        ds = dpo_ * gu * sz
        dz = dpo_ * gu * s * silu_grad(z)
        dsb = ds.astype(BF16)
        dws = jnp.stack([lax.dot_general(dsb[:, gi * SGU_C:(gi + 1) * SGU_C], vn[:, gi * SGU_C:(gi + 1) * SGU_C].astype(BF16),
                                         (((1,), (1,)), ((), ())), preferred_element_type=F32) for gi in range(SGU_G)])
        dvn = _sgu_mix(wm, ds, True)
        dlg = jnp.sum(dvn * xh, axis=0, keepdims=True)
        dlb = jnp.sum(dvn, axis=0, keepdims=True)
        dxh = dvn * lg
        dgv = rstd * (dxh - jnp.mean(dxh, axis=-1, keepdims=True) - xh * jnp.mean(dxh * xh, axis=-1, keepdims=True))
        return ([jnp.concatenate([du * gelu_grad(u_), dgv * gelu_grad(v_), dz], axis=1)], [dws, ds, dlg, dlb])
    (dproj,), (dws, dbs, dlg, dlb) = rowwise(
        f1, [rw(dpo), rw(proj, 1024, 0), rw(proj, 1024, 1), rw(proj, 1024, 2)],
        [p['d_w_s'][0], p['d_ln_g'], p['d_ln_b'], sv['bias']], [(3072, BF16)],
        [(SGU_G, SGU_T, SGU_T), (SGU_T, 1024), (1, 1024), (1, 1024)], SGU_T, 'd_mix_bwd')
    tril = np.tril(np.ones((SGU_T, SGU_T), dtype=bool))
    g['d_w_s'] = jnp.where(tril[None], dws, 0.0)[None]
    g['d_b_s'] = dbs.reshape(SGU_T, SGU_G, SGU_C).sum(-1).T[None]
    g['d_ln_g'], g['d_ln_b'] = dlg, dlb
    _dw(g, sink, 'd_w_in', sv['h'], dproj, 'd_dwin')
    dh = mm(dproj, w['d_w_in'], 'nt', 'd_dh')
    return dh, g


def _coords():
    return lax.axis_index("x"), lax.axis_index("y"), lax.axis_index("c")


class AllGather:
    def __init__(self, x):
        self.ins = [x]
        self.outs = [jax.ShapeDtypeStruct((N_DEV,) + x.shape, x.dtype)]
        self.scratch = [pltpu.SemaphoreType.DMA((7,)), pltpu.SemaphoreType.DMA((7,)), pltpu.SemaphoreType.DMA(())]

    def hooks(self, n_steps):
        return [(0, functools.partial(self.phase, 0), False), (n_steps - 1, functools.partial(self.phase, 1), True),
                (n_steps - 1, functools.partial(self.phase, 2), True)]

    @staticmethod
    def phase(which, ins, outs, scratch):
        (x_ref,), (out_ref,), (send_sems, recv_sems, local_sem) = ins, outs, scratch
        x_, y_, c_ = _coords()
        me, sibling = (x_, y_, c_), (x_, y_, 1 - c_)
        chips = [(1 - x_, y_), (x_, 1 - y_), (1 - x_, 1 - y_)]

        def slot(px, py, pc):
            return out_ref.at[4 * px + 2 * py + pc]

        def copy(k, block, to, src=None):
            return pltpu.make_async_remote_copy(src_ref=slot(*block) if src is None else src, dst_ref=slot(*block),
                                                send_sem=send_sems.at[k], recv_sem=recv_sems.at[k], device_id=to,
                                                device_id_type=MESH)

        mine = pltpu.make_async_copy(x_ref, slot(*me), local_sem)
        first = [copy(0, me, sibling, src=x_ref)]
        first += [copy(1 + j, me, (*chip, c_), src=x_ref) for j, chip in enumerate(chips)]
        passed = [copy(4 + j, (*chip, c_), sibling) for j, chip in enumerate(chips)]
        if which == 0:
            mine.start()
            for cp in first:
                cp.start()
        elif which == 1:
            for j, chip in enumerate(chips):
                copy(1 + j, (*chip, c_), me).wait_recv()
                passed[j].start()
        else:
            copy(0, sibling, me).wait_recv()
            for j, chip in enumerate(chips):
                copy(4 + j, (*chip, 1 - c_), me).wait_recv()
            for cp in first + passed:
                cp.wait_send()
            mine.wait()


class ChipExchange:
    def __init__(self, part):
        self.ins = [part]
        self.outs = [jax.ShapeDtypeStruct((3,) + part.shape[1:], part.dtype)]
        self.scratch = [pltpu.SemaphoreType.DMA((3,)), pltpu.SemaphoreType.DMA((3,))]

    def hooks(self, n_steps):
        return [(0, functools.partial(self.phase, 0), False), (n_steps - 1, functools.partial(self.phase, 1), True)]

    @staticmethod
    def phase(which, ins, outs, scratch):
        (p_ref,), (land_ref,), (send_sems, recv_sems) = ins, outs, scratch
        x_, y_, c_ = _coords()
        copies = []
        for r, (fx, fy) in enumerate([(1, 0), (0, 1), (1, 1)]):
            tx = jnp.where(fx == 1, 1 - x_, x_)
            ty = jnp.where(fy == 1, 1 - y_, y_)
            copies.append(pltpu.make_async_remote_copy(src_ref=p_ref.at[2 * tx + ty], dst_ref=land_ref.at[r],
                                                       send_sem=send_sems.at[r], recv_sem=recv_sems.at[r],
                                                       device_id=(tx, ty, c_), device_id_type=MESH))
        if which == 0:
            for cp in copies:
                cp.start()
        else:
            for cp in copies:
                cp.wait_recv()
            for cp in copies:
                cp.wait_send()


class Both:
    def __init__(self, a, b):
        self.parts = (a, b)
        self.ins, self.outs, self.scratch = a.ins + b.ins, a.outs + b.outs, a.scratch + b.scratch

    def hooks(self, n_steps):
        res, oi, oo, osc = [], 0, 0, 0
        for p in self.parts:
            sl = (slice(oi, oi + len(p.ins)), slice(oo, oo + len(p.outs)), slice(osc, osc + len(p.scratch)))
            res += [(at, functools.partial(self.sub, fn, sl), after) for at, fn, after in p.hooks(n_steps)]
            oi, oo, osc = oi + len(p.ins), oo + len(p.outs), osc + len(p.scratch)
        return res

    @staticmethod
    def sub(fn, sl, ins, outs, scratch):
        fn(ins[sl[0]], outs[sl[1]], scratch[sl[2]])


def run_comm(comm, name):
    def body(*refs):
        ci, co = len(comm.ins), len(comm.outs)
        for _, fn, _ in comm.hooks(1):
            fn(refs[:ci], refs[ci:ci + co], refs[ci + co:])

    return pl.pallas_call(body, out_shape=list(comm.outs), in_specs=[ANY] * len(comm.ins),
                          out_specs=[ANY] * len(comm.outs), scratch_shapes=list(comm.scratch), name=name)(*comm.ins)


def all_gather(x, name):
    return run_comm(AllGather(x), name)[0]


class SiblingExchange:
    def __init__(self, gfull):
        self.ins = [gfull]
        self.outs = [jax.ShapeDtypeStruct((4,) + gfull.shape[1:], gfull.dtype)]
        self.scratch = [pltpu.SemaphoreType.DMA((4,)), pltpu.SemaphoreType.DMA((4,))]

    def hooks(self, n_steps):
        return [(0, functools.partial(self.phase, 0), False), (n_steps - 1, functools.partial(self.phase, 1), True)]

    @staticmethod
    def phase(which, ins, outs, scratch):
        (g_ref,), (land_ref,), (send_sems, recv_sems) = ins, outs, scratch
        x_, y_, c_ = _coords()
        copies = [pltpu.make_async_remote_copy(src_ref=g_ref.at[2 * k + 1 - c_], dst_ref=land_ref.at[k],
                                               send_sem=send_sems.at[k], recv_sem=recv_sems.at[k],
                                               device_id=(x_, y_, 1 - c_), device_id_type=MESH) for k in range(4)]
        if which == 0:
            for cp in copies:
                cp.start()
        else:
            for cp in copies:
                cp.wait_recv()
            for cp in copies:
                cp.wait_send()


def rs_sibling(gfull, tag):
    return run_comm(SiblingExchange(gfull), 'rs_sibling_' + tag)[0]


def rs_pair_add(gfull, land, core, tag):
    _, R, C = gfull.shape
    tl = R

    def body(c_ref, g_ref, l_ref, o_ref):
        o_ref[...] = (g_ref[...].astype(F32) + l_ref[...].astype(F32)).astype(BF16)

    return pl.pallas_call(
        body, out_shape=jax.ShapeDtypeStruct((4, R, C), BF16),
        grid_spec=pltpu.PrefetchScalarGridSpec(
            num_scalar_prefetch=1, grid=(4, R // tl),
            in_specs=[pl.BlockSpec((1, tl, C), lambda k, i, c: (2 * k + c[0], i, 0)),
                      pl.BlockSpec((1, tl, C), lambda k, i, c: (k, i, 0))],
            out_specs=pl.BlockSpec((1, tl, C), lambda k, i, c: (k, i, 0))),
        compiler_params=pltpu.CompilerParams(dimension_semantics=("parallel", "parallel")), name='rs_pair_add_' + tag)(
            core, gfull, land)


def rs_chips(part, tag):
    return run_comm(ChipExchange(part), 'rs_chips_' + tag)[0]


def _adam(wv, gv, mv, vv):
    m = ADAM_B1 * mv + (1.0 - ADAM_B1) * gv
    v = ADAM_B2 * vv + (1.0 - ADAM_B2) * (gv * gv)
    m_hat = m / (1.0 - ADAM_B1 ** ADAM_STEP)
    v_hat = v / (1.0 - ADAM_B2 ** ADAM_STEP)
    delta = -ADAM_LR * (m_hat / (jnp.sqrt(v_hat) + ADAM_EPS) + ADAM_WD * wv)
    return delta, m, v


def _sum4(p_ref, l_ref):
    return ((p_ref[0].astype(F32) + l_ref[0].astype(F32)) + l_ref[1].astype(F32)) + l_ref[2].astype(F32)


def rs_rep_sum(part, land, chip):
    def body(c_ref, p_ref, l_ref, o_ref):
        o_ref[...] = _sum4(p_ref, l_ref).astype(BF16)

    return pl.pallas_call(
        body, out_shape=jax.ShapeDtypeStruct((REP_SLOT, LANES), BF16),
        grid_spec=pltpu.PrefetchScalarGridSpec(
            num_scalar_prefetch=1, grid=(1,),
            in_specs=[pl.BlockSpec((1, REP_SLOT, LANES), lambda i, c: (c[0], 0, 0)),
                      pl.BlockSpec((3, REP_SLOT, LANES), lambda i, c: (0, 0, 0))],
            out_specs=pl.BlockSpec((REP_SLOT, LANES), lambda i, c: (0, 0))),
        compiler_params=pltpu.CompilerParams(dimension_semantics=("parallel",)), name='rs_rep')(chip, part, land)


def adam_param(name, shape, off, w, m, v, chip, part=None, land=None, grep=None):
    r, c = shape
    rp, nt, rb = _tiles(shape)
    rbw = min(r, rb)
    n_src = 2 if grep is None else 1
    ns = w.shape
    assert int(np.prod(ns[:-1])) == r and ns[-1] == c
    if len(ns) == 2:
        nat_block, nat_map = (rbw, c), lambda i, cr: (i, 0)
    elif int(np.prod(ns[:-2])) == 1:
        nat_block, nat_map = (1,) * (len(ns) - 2) + (rbw, c), lambda i, cr: (0,) * (len(ns) - 2) + (i, 0)
    else:
        assert len(ns) == 4 and ns[0] == 1 and rbw % ns[2] == 0
        nat_block, nat_map = (1, rbw // ns[2], ns[2], c), lambda i, cr: (0, i, 0, 0)

    def body(c_ref, *refs):
        srcs = refs[:n_src * nt]
        w_ref, m_ref, v_ref, g_ref, d_ref, nm_ref, nv_ref = refs[n_src * nt:]
        if grep is None:
            tiles = [_sum4(srcs[2 * t], srcs[2 * t + 1]) for t in range(nt)]
        else:
            tiles = [srcs[t][...].astype(F32) for t in range(nt)]
        g = (tiles[0] if nt == 1 else jnp.concatenate(tiles, axis=1))[:rbw, :c]
        g_ref[...] = g.reshape(nat_block)
        res = _adam(w_ref[...].reshape(rbw, c), g, m_ref[...].reshape(rbw, c), v_ref[...].reshape(rbw, c))
        for ref, val in zip((d_ref, nm_ref, nv_ref), res):
            ref[...] = val.reshape(nat_block)

    in_specs, args = [], []
    for t in range(nt):
        b0 = (off + t * rp) // rb
        assert (off + t * rp) % rb == 0
        if grep is None:
            in_specs += [pl.BlockSpec((1, rb, LANES), functools.partial(lambda i, cr, b0: (cr[0], b0 + i, 0), b0=b0)),
                         pl.BlockSpec((3, rb, LANES), functools.partial(lambda i, cr, b0: (0, b0 + i, 0), b0=b0))]
            args += [part, land]
        else:
            in_specs.append(pl.BlockSpec((rb, LANES), functools.partial(lambda i, cr, b0: (b0 + i, 0), b0=b0)))
            args.append(grep)
    nat = pl.BlockSpec(nat_block, nat_map)
    return pl.pallas_call(
        body, out_shape=[jax.ShapeDtypeStruct(ns, F32)] * 4,
        grid_spec=pltpu.PrefetchScalarGridSpec(num_scalar_prefetch=1, grid=(rp // rb,), in_specs=in_specs + [nat] * 3,
                                               out_specs=[nat] * 4),
        compiler_params=pltpu.CompilerParams(dimension_semantics=("parallel",)), name='adam_' + name)(
            chip, *args, w, m, v)


def adam_small(names, grep, P, M, V):
    in_specs, args, out_specs, out_shape, meta = [], [], [], [], []
    for n in names:
        s = REP_SHAPE[n]
        rp, nt, _ = _tiles(s)
        ns = P[n].shape
        for t in range(nt):
            b0 = (REP_OFF[n] + t * rp) // rp
            assert (REP_OFF[n] + t * rp) % rp == 0
            in_specs.append(pl.BlockSpec((rp, LANES), functools.partial(lambda i, b0: (b0, 0), b0=b0)))
            args.append(grep)
        nat = pl.BlockSpec(ns, functools.partial(lambda i, nd: (0,) * nd, nd=len(ns)))
        in_specs += [nat] * 3
        args += [P[n], M[n], V[n]]
        out_specs += [nat] * 4
        out_shape += [jax.ShapeDtypeStruct(ns, F32)] * 4
        meta.append((s, nt, ns))
    n_in = len(in_specs)

    def body(*refs):
        ins, outs = refs[:n_in], refs[n_in:]
        k = 0
        for p, ((r, c), nt, ns) in enumerate(meta):
            tiles = [ins[k + t][...].astype(F32) for t in range(nt)]
            w_ref, m_ref, v_ref = ins[k + nt:k + nt + 3]
            k += nt + 3
            g = (tiles[0] if nt == 1 else jnp.concatenate(tiles, axis=1))[:r, :c]
            res = (g,) + _adam(w_ref[...].reshape(r, c), g, m_ref[...].reshape(r, c), v_ref[...].reshape(r, c))
            for ref, val in zip(outs[4 * p:4 * p + 4], res):
                ref[...] = val.reshape(ns)

    res = pl.pallas_call(body, grid=(1,), in_specs=in_specs, out_specs=out_specs, out_shape=out_shape,
                         compiler_params=pltpu.CompilerParams(dimension_semantics=("arbitrary",)), name='adam_small')(*args)
    return {n: tuple(res[4 * p:4 * p + 4]) for p, n in enumerate(names)}


VM = pl.BlockSpec(memory_space=pltpu.VMEM)


def _tile_value(w, t, rp):
    r, c = w.shape
    wt = min(LANES, c - t * LANES)
    tile = w[:, t * LANES:t * LANES + wt]
    if wt < LANES:
        tile = jnp.concatenate([tile, jnp.zeros((r, LANES - wt), tile.dtype)], axis=1)
    if rp > r:
        tile = jnp.concatenate([tile, jnp.zeros((rp - r, LANES), tile.dtype)], axis=0)
    return tile


def pack_layer(layer, blocks):
    names = LAYER_PARAMS[layer]

    def body(*refs):
        tiles = []
        for ref, n in zip(refs[:-1], names):
            rp, nt, _ = _tiles(_block_shape(n))
            w = ref[...].reshape(_block_shape(n))
            tiles += [_tile_value(w, t, rp) for t in range(nt)]
        refs[-1][...] = jnp.concatenate(tiles, axis=0).astype(BF16)

    return pl.pallas_call(body, out_shape=jax.ShapeDtypeStruct((LAYER_ROWS[layer], LANES), BF16),
                          in_specs=[VM] * len(names), out_specs=VM, name='pack_' + layer)(*[blocks[n] for n in names])


def assemble(name, gathered):
    (rf, cf), ax = SHARDED[name]
    r, c = _block_shape(name)
    rp, nt, _ = _tiles((r, c))
    off = SH_OFF[name]
    out_cols = cf if ax == 0 else len(perm_index(name))

    def body(g_ref, o_ref, buf, sem):
        cp = pltpu.make_async_copy(g_ref.at[:, pl.ds(off, nt * rp), :], buf, sem)
        cp.start()
        cp.wait()
        if ax == 0:
            for j in range(N_DEV):
                o_ref[j * r:(j + 1) * r, :] = jnp.concatenate([buf[j, t * rp:(t + 1) * rp, :] for t in range(nt)], axis=1)
            return
        pieces = []
        for p in PERM[name]:
            if p[0] == 'z':
                pieces.append(jnp.zeros((r, p[1]), BF16))
                continue
            n0, w = p
            while w > 0:
                j, cb = divmod(n0, c)
                t, lane = divmod(cb, LANES)
                wl = min(w, LANES - lane, c - cb)
                pieces.append(buf[j, t * rp:t * rp + r, lane:lane + wl])
                n0, w = n0 + wl, w - wl
        o_ref[...] = jnp.concatenate(pieces, axis=1)

    return pl.pallas_call(
        body, out_shape=jax.ShapeDtypeStruct((rf, out_cols), BF16), in_specs=[ANY], out_specs=VM,
        scratch_shapes=[pltpu.VMEM((N_DEV, nt * rp, LANES), BF16), pltpu.SemaphoreType.DMA(())], name='asm_' + name)(
            gathered)


def chunk_grad(layer, name, dw, gfull):
    (rf, cf), ax = SHARDED[name]
    r, c = _block_shape(name)
    rp, nt, _ = _tiles((r, c))
    off = SH_OFF[name]
    if ax == 1:
        idx = perm_index(name) if name in PERM else np.arange(cf)
        inv = np.full(cf, -1)
        inv[idx[idx >= 0]] = np.nonzero(idx >= 0)[0]

    def body(*refs):
        dw_ref, o_ref, buf, sem = refs[0], refs[-3], refs[-2], refs[-1]
        for j in range(N_DEV):
            for t in range(nt):
                if ax == 0:
                    tile = dw_ref[j * r:(j + 1) * r, t * LANES:(t + 1) * LANES]
                else:
                    cols = inv[j * c + t * LANES:j * c + min((t + 1) * LANES, c)]
                    cuts = [0] + [k for k in range(1, len(cols)) if cols[k] != cols[k - 1] + 1] + [len(cols)]
                    pieces = [dw_ref[:, int(cols[a]):int(cols[b - 1]) + 1] for a, b in zip(cuts[:-1], cuts[1:])]
                    if len(cols) < LANES:
                        pieces.append(jnp.zeros((r, LANES - len(cols)), F32))
                    tile = pieces[0] if len(pieces) == 1 else jnp.concatenate(pieces, axis=1)
                    if rp > r:
                        tile = jnp.concatenate([tile, jnp.zeros((rp - r, LANES), F32)], axis=0)
                buf[j, t * rp:(t + 1) * rp, :] = tile.astype(BF16)
        cp = pltpu.make_async_copy(buf, o_ref.at[:, pl.ds(off, nt * rp), :], sem)
        cp.start()
        cp.wait()

    shape = jax.ShapeDtypeStruct((N_DEV, LAYER_ROWS[layer], LANES), BF16)
    scratch = [pltpu.VMEM((N_DEV, nt * rp, LANES), BF16), pltpu.SemaphoreType.DMA(())]
    if gfull is None:
        return pl.pallas_call(body, out_shape=shape, in_specs=[VM], out_specs=ANY, scratch_shapes=scratch,
                              name='chunk_' + name)(dw)
    return pl.pallas_call(body, out_shape=shape, in_specs=[VM, ANY], out_specs=ANY, scratch_shapes=scratch,
                          input_output_aliases={1: 0}, name='chunk_' + name)(dw, gfull)


class GradSink:
    def __init__(self):
        self.bufs = {}

    def put(self, name, a, b, mm_name):
        (rf, cf), ax = SHARDED[name]
        r, c = _block_shape(name)
        group = GROUP_OF[name]
        direct = ax == 0 or (c % LANES == 0 and PERM[name] == [(0, cf)])
        if direct:
            self.bufs[group] = mm_tn_chunked(a, b, mm_name, group, name, self.bufs.get(group))
        else:
            self.add(name, mm(a, b, 'tn', mm_name))

    def add(self, name, dw):
        group = GROUP_OF[name]
        self.bufs[group] = chunk_grad(group, name, dw, self.bufs.get(group))


def mm_tn_chunked(a, b, mm_name, layer, wname, gfull):
    (rf, cf), ax = SHARDED[wname]
    r, c = _block_shape(wname)
    rp, nt, _ = _tiles((r, c))
    off = SH_OFF[wname]
    K, M = a.shape
    N = b.shape[1]
    assert (M, N) == (rf, cf) and rp == r
    if ax == 0:
        tn = 4 * LANES
        grid, bspec = (N // tn,), pl.BlockSpec((K, tn), lambda g: (0, g))
        ospec = pl.BlockSpec((N_DEV, 4 * r, LANES), lambda g: (0, off // (4 * r) + g, 0))
        assert off % (4 * r) == 0 and nt % 4 == 0

        def store(res, o_ref):
            for j in range(N_DEV):
                for q in range(4):
                    o_ref[j, q * r:(q + 1) * r, :] = res[j * r:(j + 1) * r, q * LANES:(q + 1) * LANES].astype(BF16)
    else:
        tn = c
        grid, bspec = (N_DEV,), pl.BlockSpec((K, tn), lambda g: (0, g))
        ospec = pl.BlockSpec((1, nt * r, LANES), lambda g: (g, off // (nt * r), 0))
        assert off % (nt * r) == 0

        def store(res, o_ref):
            for t in range(nt):
                o_ref[0, t * r:(t + 1) * r, :] = res[:, t * LANES:(t + 1) * LANES].astype(BF16)

    def body(*refs):
        a_ref, b_ref, o_ref = refs[0], refs[1], refs[-1]
        store(lax.dot_general(a_ref[...].astype(BF16), b_ref[...].astype(BF16), _TN, preferred_element_type=F32), o_ref)

    shape = jax.ShapeDtypeStruct((N_DEV, LAYER_ROWS[layer], LANES), BF16)
    aspec = pl.BlockSpec((K, M), lambda g: (0, 0))
    params = pltpu.CompilerParams(dimension_semantics=("parallel",))
    if gfull is None:
        return pl.pallas_call(body, grid=grid, in_specs=[aspec, bspec], out_specs=ospec, out_shape=shape,
                              compiler_params=params, name=mm_name)(a, b)
    return pl.pallas_call(body, grid=grid, in_specs=[aspec, bspec, ANY], out_specs=ospec, out_shape=shape,
                          input_output_aliases={2: 0}, compiler_params=params, name=mm_name)(a, b, gfull)


def pack_rep(G):
    def body(*refs):
        tiles = []
        for ref, s in zip(refs[:-1], REP_SHAPE.values()):
            rp, nt, _ = _tiles(s)
            g = ref[...]
            tiles += [_tile_value(g, t, rp) for t in range(nt)]
        rows = sum(t.shape[0] for t in tiles)
        if rows < REP_ROWS:
            tiles.append(jnp.zeros((REP_ROWS - rows, LANES), F32))
        full = jnp.concatenate(tiles, axis=0)
        for j in range(N_DEV):
            refs[-1][j] = full[j * REP_CHUNK:(j + 1) * REP_CHUNK]

    return pl.pallas_call(body, out_shape=jax.ShapeDtypeStruct((N_DEV, REP_SLOT, LANES), F32),
                          in_specs=[VM] * len(REP_SHAPE), out_specs=VM, name='pack_rep')(
                              *[G[n].reshape(s) for n, s in REP_SHAPE.items()])


def _pack_small(blocks, order, rows, width, dtype):
    flat = jnp.concatenate([blocks[n].reshape(-1).astype(dtype) for n in order])
    return jnp.pad(flat, (0, rows * width - flat.shape[0])).reshape(rows, width)


def kernel(x, pre_norm, post_norm, rel_bias, a_w_in, a_lam_re, a_lam_im, a_log_dt, a_b_re, a_b_im, a_c_re, a_c_im, a_d, a_w_glu, a_b_glu, a_w_out, b_w_in, b_sinks, b_w_out, c_w_in, c_q_norm, c_kv_norm, c_w_uq, c_w_ukv, c_w_out, d_w_in, d_ln_g, d_ln_b, d_w_s, d_b_s, d_w_out, loss_target, m_pre_norm, m_post_norm, m_rel_bias, m_a_w_in, m_a_lam_re, m_a_lam_im, m_a_log_dt, m_a_b_re, m_a_b_im, m_a_c_re, m_a_c_im, m_a_d, m_a_w_glu, m_a_b_glu, m_a_w_out, m_b_w_in, m_b_sinks, m_b_w_out, m_c_w_in, m_c_q_norm, m_c_kv_norm, m_c_w_uq, m_c_w_ukv, m_c_w_out, m_d_w_in, m_d_ln_g, m_d_ln_b, m_d_w_s, m_d_b_s, m_d_w_out, v_pre_norm, v_post_norm, v_rel_bias, v_a_w_in, v_a_lam_re, v_a_lam_im, v_a_log_dt, v_a_b_re, v_a_b_im, v_a_c_re, v_a_c_im, v_a_d, v_a_w_glu, v_a_b_glu, v_a_w_out, v_b_w_in, v_b_sinks, v_b_w_out, v_c_w_in, v_c_q_norm, v_c_kv_norm, v_c_w_uq, v_c_w_ukv, v_c_w_out, v_d_w_in, v_d_ln_g, v_d_ln_b, v_d_w_s, v_d_b_s, v_d_w_out):
    loc = locals()
    P = {n: loc[n] for n in WEIGHTS}
    M = {n: loc['m_' + n] for n in WEIGHTS}
    V = {n: loc['v_' + n] for n in WEIGHTS}
    xs = x[0]
    L = xs.shape[0]

    blocks = {n: P[n].reshape(_block_shape(n)) for n in SHARDED}
    packed = {layer: pack_layer(layer, P) for layer in LAYER_PARAMS}
    W = {}

    def assemble_layer(layer, gathered):
        for n in LAYER_PARAMS[layer]:
            if n not in SHARDED_F32:
                W[n] = assemble(n, gathered)

    assemble_layer('a1', all_gather(packed['a1'], 'ag_a1'))
    small = all_gather(_pack_small(blocks, SHARDED_F32, SMALL_ROWS, 128, F32), 'ag_small')
    Pl = dict(P)
    for n in SHARDED_F32:
        c = SHARDED[n][0][1]
        bc = c // N_DEV
        Pl[n] = small.reshape(N_DEV, -1)[:, SMALL_OFF[n]:SMALL_OFF[n] + bc].reshape(1, c)
    cx, cy, cc = _coords()
    core = jnp.reshape(cc, (1,)).astype(jnp.int32)
    chip = jnp.reshape(2 * cx + cy, (1,)).astype(jnp.int32)

    def pair_sums(gfull, tag):
        return rs_pair_add(gfull, rs_sibling(gfull, tag), core, tag)

    fwd = [layer_a_fwd, layer_b_fwd, layer_c_fwd, layer_d_fwd]
    bwd = [layer_a_bwd, layer_b_bwd, layer_c_bwd, layer_d_bwd]
    saved = []
    xc = xs

    def fpre(x_, g_):
        return [rms_fwd(x_, g_)], []
    (h,), _ = rowwise(fpre, [rw(xc)], [P['pre_norm'][0:1]], [(D_MODEL, BF16)], [], 256, 'pre_norm0')
    for i in range(4):
        if i == 0:
            yb, sv = fwd[i](h, W, Pl, comm=Both(AllGather(packed['a2']), AllGather(packed['b'])),
                            on_carried=lambda got: assemble_layer('a2', got[0]))
            assemble_layer('b', sv['carried'][1])
        elif i < 3:
            nxt = 'abcd'[i + 1]
            yb, sv = fwd[i](h, W, Pl, comm=AllGather(packed[nxt]))
            assemble_layer(nxt, sv['carried'][0])
        else:
            yb, sv = fwd[i](h, W, Pl)

        sv['x'], sv['yb'] = xc, yb
        saved.append(sv)
        if i < 3:

            def fpost(x_, y_, gpost, gpre):
                xn_ = x_ + rms_fwd(y_, gpost)
                return [xn_, rms_fwd(xn_, gpre)], []
            (xc, h), _ = rowwise(fpost, [rw(xc), rw(yb)], [P['post_norm'][i:i + 1], P['pre_norm'][i + 1:i + 2]],
                                 [(D_MODEL, F32), (D_MODEL, BF16)], [], 256, f'post_pre_norm{i}')
        else:

            def floss(x_, y_, t_, gpost):
                d = x_ + rms_fwd(y_, gpost) - t_
                return [d * (1.0 / D_MODEL)], [0.5 * jnp.sum(jnp.sum(d * d, axis=-1, keepdims=True) * (1.0 / D_MODEL),
                                                             axis=0, keepdims=True)]
            (dx,), (loss_loc,) = rowwise(floss, [rw(xc), rw(yb), rw(loss_target[0])], [P['post_norm'][i:i + 1]],
                                         [(D_MODEL, F32)], [(1, 1)], 256, 'post_norm_loss')
    loss = lax.psum(loss_loc[0, 0], ("x", "y", "c"))

    G, out = {}, {}
    dpre, dpost = [None] * 4, [None] * 4

    def adam_layer(layer, part, land2):
        for n in LAYER_PARAMS[layer]:
            s = _block_shape(n)
            out[n] = adam_param(n, s, SH_OFF[n], P[n], M[n], V[n], chip, part=part, land=land2)

    def fpost_b(y_, d_, g_):
        dy, dg = rms_bwd(y_, g_, d_)
        return [dy], [dg]
    (dyb,), (dpost[3],) = rowwise(fpost_b, [rw(saved[3]['yb']), rw(dx)], [P['post_norm'][3:4]], [(D_MODEL, BF16)],
                                  [(1, D_MODEL)], 256, 'post_norm_bwd3')
    pending = None
    sink = GradSink()
    for i in reversed(range(4)):
        sv = saved[i]
        if pending is None:
            dh, g = bwd[i](dyb, W, Pl, sv, sink=sink)
        elif i > 0:
            dh, g = bwd[i](dyb, W, Pl, sv, comm=ChipExchange(pending[1]), sink=sink)
            adam_layer(pending[0], pending[1], g['carried'][0])
        else:
            early = {}

            def both():
                early['part'] = pair_sums(sink.bufs['a2'], 'a2')
                return Both(ChipExchange(pending[1]), ChipExchange(early['part']))
            dh, g = bwd[i](dyb, W, Pl, sv, comm=both, sink=sink)
            adam_layer(pending[0], pending[1], g['carried'][0])
            adam_layer('a2', early['part'], g['carried'][1])
        g.pop('carried', None)
        G.update(g)
        group = LAYER_GROUPS['abcd'[i]][0]
        for n in LAYER_PARAMS[group]:
            if n in g:
                sink.add(n, g[n])
        swap = SiblingExchange(sink.bufs[group])

        if i > 0:

            def fpre_b(x_, dh_, d_, y_, gpre, gpost):
                dxl, dg = rms_bwd(x_, gpre, dh_)
                dy, dgp = rms_bwd(y_, gpost, d_ + dxl)
                return [d_ + dxl, dy], [dg, dgp]
            (dx, dyb), (dpre[i], dpost[i - 1]), (land,) = rowwise(
                fpre_b, [rw(sv['x']), rw(dh), rw(dx), rw(saved[i - 1]['yb'])],
                [P['pre_norm'][i:i + 1], P['post_norm'][i - 1:i]], [(D_MODEL, F32), (D_MODEL, BF16)],
                [(1, D_MODEL), (1, D_MODEL)], 256, f'pre_post_norm_bwd{i}', comm=swap)
        else:

            def fpre_b0(x_, dh_, d_, g_):
                dxl, dg = rms_bwd(x_, g_, dh_)
                return [d_ + dxl], [dg]
            (dx,), (dpre[i],), (land,) = rowwise(fpre_b0, [rw(sv['x']), rw(dh), rw(dx)], [P['pre_norm'][i:i + 1]],
                                                 [(D_MODEL, F32)], [(1, D_MODEL)], 256, 'pre_norm_bwd0', comm=swap)
        pending = (group, rs_pair_add(sink.bufs[group], land, core, group))
    adam_layer(pending[0], pending[1], rs_chips(pending[1], pending[0]))
    G['pre_norm'] = jnp.concatenate(dpre, axis=0)
    G['post_norm'] = jnp.concatenate(dpost, axis=0)

    part = pair_sums(pack_rep(G), 'rep')
    land2 = rs_chips(part, 'rep')
    grep = all_gather(rs_rep_sum(part, land2, chip), 'ag_rep')[:, :REP_CHUNK].reshape(REP_ROWS, LANES)
    small_names = [n for n, s in REP_SHAPE.items() if s[0] <= 64]
    out.update(adam_small(small_names, grep, P, M, V))
    for n, s in REP_SHAPE.items():
        if n not in small_names:
            out[n] = adam_param(n, s, REP_OFF[n], P[n], M[n], V[n], chip, grep=grep)
    res = [loss, dx[None]]
    for kind in range(4):
        res += [out[n][kind].reshape(P[n].shape) for n in WEIGHTS]
    return tuple(res)
```

```python
import functools
import math

import numpy as np
import jax
import jax.numpy as jnp
from jax import lax
from jax.experimental import pallas as pl
from jax.experimental.pallas import tpu as pltpu

F32 = jnp.float32
BF16 = jnp.bfloat16
MESH = pl.DeviceIdType.MESH
ANY = pl.BlockSpec(memory_space=pl.ANY)

N_DEV = 8
D_MODEL = 1024
EPS = 1e-6
NEG_INF = -1e30
SSM_G, SSM_P, SSM_H = 64, 64, 16
SSM_T = 256
SSM_TS = 8
SSM_WC = 512
HEAD_DIM = 64
SWA_HEADS, SWA_KV = 16, 2
WINDOW = 128
REL_BUCKETS, REL_MAX_DIST = 32, 128
MLA_HEADS, MLA_NOPE, MLA_ROPE, MLA_V = 16, 64, 32, 64
MLA_Q_RANK, MLA_KV_RANK = 768, 256
ROPE_BASE = 10000.0
SGU_G, SGU_C, SGU_T = 16, 64, 128
ADAM_LR, ADAM_B1, ADAM_B2, ADAM_EPS, ADAM_WD, ADAM_STEP = 0.001, 0.9, 0.999, 1e-08, 0.01, 10

WEIGHTS = ['pre_norm', 'post_norm', 'rel_bias', 'a_w_in', 'a_lam_re', 'a_lam_im', 'a_log_dt', 'a_b_re', 'a_b_im',
           'a_c_re', 'a_c_im', 'a_d', 'a_w_glu', 'a_b_glu', 'a_w_out', 'b_w_in', 'b_sinks', 'b_w_out', 'c_w_in',
           'c_q_norm', 'c_kv_norm', 'c_w_uq', 'c_w_ukv', 'c_w_out', 'd_w_in', 'd_ln_g', 'd_ln_b', 'd_w_s', 'd_b_s',
           'd_w_out']
SHARDED = {'a_w_in': ((1024, 2048), 1), 'a_w_glu': ((1024, 1024), 0), 'a_w_out': ((1024, 1024), 0),
           'b_w_in': ((1024, 2304), 1), 'b_w_out': ((1024, 1024), 0), 'c_w_in': ((1024, 2080), 1),
           'c_q_norm': ((1, 768), 1), 'c_kv_norm': ((1, 256), 1), 'c_w_uq': ((768, 1536), 1),
           'c_w_ukv': ((256, 2048), 1), 'c_w_out': ((1024, 1024), 0), 'd_w_in': ((1024, 3072), 1),
           'd_ln_g': ((1, 1024), 1), 'd_ln_b': ((1, 1024), 1), 'd_w_out': ((1024, 1024), 0)}
SHARDED_F32 = ['c_q_norm', 'c_kv_norm', 'd_ln_g', 'd_ln_b']
REPLICATED = [n for n in WEIGHTS if n not in SHARDED]


def _cdiv(a, b):
    return -(-a // b)


def _block_shape(name):
    (r, c), ax = SHARDED[name]
    return (r // N_DEV, c) if ax == 0 else (r, c // N_DEV)


LANES = 128
LAYER_PARAMS = {'a1': ['a_w_in'], 'a2': ['a_w_glu', 'a_w_out'], 'b': ['b_w_in', 'b_w_out'],
                'c': ['c_w_in', 'c_w_uq', 'c_w_ukv', 'c_w_out', 'c_q_norm', 'c_kv_norm'],
                'd': ['d_w_in', 'd_w_out', 'd_ln_g', 'd_ln_b']}


def _tiles(shape):
    r, c = shape
    rp = max(r, 16)
    rb = 512 if rp % 512 == 0 else 256 if rp % 256 == 0 else rp
    return rp, _cdiv(c, LANES), rb


SH_OFF, LAYER_ROWS = {}, {}
for _l, _names in LAYER_PARAMS.items():
    _o = 0
    for _n in _names:
        _rp, _nt, _rb = _tiles(_block_shape(_n))
        assert _o % _rb == 0
        SH_OFF[_n] = _o
        _o += _rp * _nt
    assert _o % 16 == 0
    LAYER_ROWS[_l] = _o
GROUP_OF = {_n: _l for _l, _names in LAYER_PARAMS.items() for _n in _names}
LAYER_GROUPS = {'a': ['a1', 'a2'], 'b': ['b'], 'c': ['c'], 'd': ['d']}

REP_SHAPE = {'d_w_s': (2048, 128), 'a_b_re': (4096, 16), 'a_b_im': (4096, 16), 'a_c_re': (1024, 64),
             'a_c_im': (1024, 64), 'pre_norm': (4, 1024), 'post_norm': (4, 1024), 'a_lam_re': (64, 64),
             'a_lam_im': (64, 64), 'a_d': (1, 1024), 'a_b_glu': (1, 1024), 'rel_bias': (32, 16), 'd_b_s': (16, 128),
             'a_log_dt': (1, 64), 'b_sinks': (1, 16)}
REP_FOLD = {'a_b_re': 8, 'a_b_im': 8, 'a_c_re': 2, 'a_c_im': 2}


def _rep_packed_shape(name):
    (r, c), f = REP_SHAPE[name], REP_FOLD.get(name, 1)
    return (r // f, c * f)


REP_OFF = {}
_o = 0
for _n in REP_SHAPE:
    _rp, _nt, _rb = _tiles(_rep_packed_shape(_n))
    assert _o % _rb == 0
    REP_OFF[_n] = _o
    _o += _rp * _nt
REP_ROWS = _cdiv(_o, 16 * N_DEV) * 16 * N_DEV
REP_CHUNK = REP_ROWS // N_DEV
REP_SLOT = REP_CHUNK

PERM = {'a_w_in': [(0, 2048)], 'd_w_in': [(0, 3072)], 'b_w_in': [(1280, 1024), (0, 1280)],
        'c_w_in': [(1056, 1024), (0, 1056), ('z', 96)],
        'c_w_uq': sum([[(2 * hp * 96, 64), ((2 * hp + 1) * 96, 64), (2 * hp * 96 + 64, 32), ((2 * hp + 1) * 96 + 64, 32),
                        ('z', 64)] for hp in range(8)], []),
        'c_w_ukv': sum([[(2 * hp * 128, 64), ((2 * hp + 1) * 128, 64), (2 * hp * 128 + 64, 64),
                         ((2 * hp + 1) * 128 + 64, 64)] for hp in range(8)], [])}


def perm_index(name):
    return np.concatenate([np.full(p[1], -1) if p[0] == 'z' else np.arange(p[0], p[0] + p[1]) for p in PERM[name]])


SMALL_OFF = {}
_o = 0
for _n in SHARDED_F32:
    SMALL_OFF[_n] = _o
    _o += int(np.prod(_block_shape(_n)))
SMALL_ROWS = _cdiv(_o, 128 * 8) * 8


def _pick(n, cands):
    for c in cands:
        if n % c == 0:
            return c
    return n


def mm(a, b, mode, name, out_dtype=F32, comm=None):
    if mode == 'nn':
        (M, K), (K2, N) = a.shape, b.shape
    elif mode == 'nt':
        (M, K), (N, K2) = a.shape, b.shape
    else:
        (K, M), (K2, N) = a.shape, b.shape
    assert K == K2, (name, a.shape, b.shape)
    tm = _pick(M, (1024, 768, 512, 256, 128))
    tn = _pick(N, (512, 384, 256))
    dims = {'nn': ((1,), (0,)), 'nt': ((1,), (1,)), 'tn': ((0,), (0,))}[mode]

    def body(a_ref, b_ref, o_ref):
        o_ref[...] = lax.dot_general(a_ref[...].astype(BF16), b_ref[...].astype(BF16), (dims, ((), ())),
                                     preferred_element_type=F32).astype(out_dtype)

    a_spec = pl.BlockSpec((K, tm), lambda i, j: (0, i)) if mode == 'tn' else pl.BlockSpec((tm, K), lambda i, j: (i, 0))
    b_spec = pl.BlockSpec((tn, K), lambda i, j: (j, 0)) if mode == 'nt' else pl.BlockSpec((K, tn), lambda i, j: (0, j))
    res = carried(body, comm, grid=(M // tm, N // tn), in_specs=[a_spec, b_spec],
                  out_specs=pl.BlockSpec((tm, tn), lambda i, j: (i, j)), out_shape=jax.ShapeDtypeStruct((M, N), out_dtype),
                  semantics=("parallel", "parallel"), name=name)(a, b)
    return res[0] if comm is None else res


def rw(arr, width=None, cb=0):
    return (arr, arr.shape[1] if width is None else width, cb)


def rowwise(fn, rows, consts, outs, accs, tl, name, n_steps=None, comm=None):
    if n_steps is None:
        n_steps = [r[0].shape[0] for r in rows if not isinstance(r[1], pl.BlockSpec)][0] // tl
    L = n_steps * tl
    nr, nc, no, na = len(rows), len(consts), len(outs), len(accs)
    in_specs, args = [], []
    for r in rows:
        if isinstance(r[1], pl.BlockSpec):
            in_specs.append(r[1])
        else:
            in_specs.append(pl.BlockSpec((tl, r[1]), functools.partial(lambda i, cb: (i, cb), cb=r[2])))
        args.append(r[0])
    for c in consts:
        in_specs.append(pl.BlockSpec(c.shape, functools.partial(lambda i, nd: (0,) * nd, nd=c.ndim)))
        args.append(c)
    out_specs = [pl.BlockSpec((tl, w), lambda i: (i, 0)) for w, _ in outs]
    out_shape = [jax.ShapeDtypeStruct((L, w), dt) for w, dt in outs]
    for s in accs:
        out_specs.append(pl.BlockSpec(s, functools.partial(lambda i, nd: (0,) * nd, nd=len(s))))
        out_shape.append(jax.ShapeDtypeStruct(s, F32))

    def body(*refs):
        ins = [r[...] for r in refs[:nr + nc]]
        o_refs = refs[nr + nc:nr + nc + no]
        a_refs = refs[nr + nc + no:]
        o_vals, a_vals = fn(*ins)
        for ref, val in zip(o_refs, o_vals):
            ref[...] = val.astype(ref.dtype)
        if na:
            @pl.when(pl.program_id(0) == 0)
            def _():
                for ref in a_refs:
                    ref[...] = jnp.zeros_like(ref)
            for ref, val in zip(a_refs, a_vals):
                ref[...] += val

    res, carried_out = carried(body, comm, grid=(n_steps,), in_specs=in_specs, out_specs=out_specs, out_shape=out_shape,
                               name=name, semantics=("arbitrary",))(*args)
    if comm is None:
        return res[:no], res[no:]
    return res[:no], res[no:], carried_out


def carried(body, comm, *, grid, in_specs, out_specs, out_shape, name, semantics, scratch_shapes=()):
    single = not isinstance(out_shape, (list, tuple))
    o_specs = [out_specs] if single else list(out_specs)
    o_shape = [out_shape] if single else list(out_shape)
    if comm is None:
        call = pl.pallas_call(body, grid=grid, in_specs=in_specs, out_specs=out_specs, out_shape=out_shape,
                              scratch_shapes=list(scratch_shapes),
                              compiler_params=pltpu.CompilerParams(dimension_semantics=semantics), name=name)
        return lambda *args: (call(*args), None)
    n_in, n_out, n_sc = len(in_specs), len(o_specs), len(scratch_shapes)
    ci, co = len(comm.ins), len(comm.outs)
    n_steps = int(np.prod(grid))
    hooks = comm.hooks(n_steps)

    def wrapped(*refs):
        ins, cins = refs[:n_in], refs[n_in:n_in + ci]
        outs, couts = refs[n_in + ci:n_in + ci + n_out], refs[n_in + ci + n_out:n_in + ci + n_out + co]
        sc, csc = refs[n_in + ci + n_out + co:n_in + ci + n_out + co + n_sc], refs[n_in + ci + n_out + co + n_sc:]
        step = pl.program_id(0)
        for ax in range(1, len(grid)):
            step = step * grid[ax] + pl.program_id(ax)
        for at, fn, after in hooks:
            if not after:
                pl.when(step == at)(functools.partial(fn, cins, couts, csc))
        body(*ins, *outs, *sc)
        for at, fn, after in hooks:
            if after:
                pl.when(step == at)(functools.partial(fn, cins, couts, csc))

    call = pl.pallas_call(wrapped, grid=grid, in_specs=list(in_specs) + [ANY] * ci, out_specs=o_specs + [ANY] * co,
                          out_shape=o_shape + list(comm.outs), scratch_shapes=list(scratch_shapes) + list(comm.scratch),
                          compiler_params=pltpu.CompilerParams(dimension_semantics=("arbitrary",) * len(grid)), name=name)

    def run(*args):
        res = call(*args, *comm.ins)
        return (res[0] if single else res[:n_out]), res[n_out:]
    return run


_K0 = math.sqrt(2.0 / math.pi)
_K1 = 0.044715


def gelu(x):
    return x * (0.5 * (1.0 + jnp.tanh(_K0 * (x + _K1 * (x * x * x)))))


def gelu_grad(x):
    t = jnp.tanh(_K0 * (x + _K1 * (x * x * x)))
    return 0.5 * (1.0 + t) + 0.5 * x * (1.0 - t * t) * (_K0 * (1.0 + 3.0 * _K1 * x * x))


def sigmoid(x):
    return 1.0 / (1.0 + jnp.exp(-x))


def silu(z):
    return z * sigmoid(z)


def silu_grad(z):
    s = sigmoid(z)
    return s * (1.0 + z * (1.0 - s))


def rms_fwd(x, g):
    r = lax.rsqrt(jnp.mean(x * x, axis=-1, keepdims=True) + EPS)
    return x * r * g


def rms_bwd(x, g, dy):
    r = lax.rsqrt(jnp.mean(x * x, axis=-1, keepdims=True) + EPS)
    xh = x * r
    dg = jnp.sum(dy * xh, axis=0, keepdims=True)
    dxh = dy * g
    dx = r * (dxh - xh * jnp.mean(dxh * xh, axis=-1, keepdims=True))
    return dx, dg


def _scan_chunk(a_r, a_i, pr_ref, pi_ref, cr, ci, T, reverse):
    ts = min(SSM_TS, T)
    sgn = -1.0 if reverse else 1.0
    row = lax.broadcasted_iota(jnp.int32, (ts, a_r.shape[1]), 0)
    pw = (lambda e: T - e) if reverse else (lambda e: e - 1)
    if reverse:
        wr_c, wi_c = pr_ref[T - ts:T, :], sgn * pi_ref[T - ts:T, :]
    else:
        wr_c, wi_c = pr_ref[0:ts, :], sgn * pi_ref[0:ts, :]
    c_r, c_i = cr[...], ci[...]
    outs = []
    subs = range(T // ts)
    for sub in (reversed(subs) if reverse else subs):
        v_r, v_i = a_r[sub * ts:(sub + 1) * ts], a_i[sub * ts:(sub + 1) * ts]
        d = 1
        while d < ts:
            wr = pr_ref[pw(d):pw(d) + 1, :]
            wi = sgn * pi_ref[pw(d):pw(d) + 1, :]
            if reverse:
                yr, yi, keep = pltpu.roll(v_r, ts - d, 0), pltpu.roll(v_i, ts - d, 0), row < ts - d
            else:
                yr, yi, keep = pltpu.roll(v_r, d, 0), pltpu.roll(v_i, d, 0), row >= d
            v_r, v_i = (v_r + jnp.where(keep, wr * yr - wi * yi, 0.0), v_i + jnp.where(keep, wr * yi + wi * yr, 0.0))
            d *= 2
        v_r, v_i = v_r + (wr_c * c_r - wi_c * c_i), v_i + (wr_c * c_i + wi_c * c_r)
        k = 0 if reverse else ts - 1
        c_r, c_i = v_r[k:k + 1, :], v_i[k:k + 1, :]
        outs.append((v_r, v_i))
    if reverse:
        outs = outs[::-1]
    cr[...] = c_r
    ci[...] = c_i
    return jnp.concatenate([o[0] for o in outs], axis=0), jnp.concatenate([o[1] for o in outs], axis=0)


_NT = (((1,), (1,)), ((), ()))
_TN = (((0,), (0,)), ((), ()))


def s5_fwd(proj, d_skip, Bre, Bim, Cre, Cim, pr, pi, comm=None):
    L = proj.shape[0]
    T, WC = min(SSM_T, L), SSM_WC
    nT = L // T

    def body(u_ref, d_ref, bre_ref, bim_ref, cre_ref, cim_ref, pr_ref, pi_ref, y_ref, yg_ref, sr_ref, si_ref, cr, ci):
        @pl.when(pl.program_id(1) == 0)
        def _():
            cr[...] = jnp.zeros_like(cr)
            ci[...] = jnp.zeros_like(ci)

        u = u_ref[...]
        ub = u.astype(BF16)
        a_r = jnp.dot(ub, bre_ref[0].astype(BF16), preferred_element_type=F32)
        a_i = jnp.dot(ub, bim_ref[0].astype(BF16), preferred_element_type=F32)
        a_r, a_i = _scan_chunk(a_r, a_i, pr_ref, pi_ref, cr, ci, T, False)
        sr_ref[...] = a_r
        si_ref[...] = a_i
        y = (jnp.dot(a_r.astype(BF16), cre_ref[0].astype(BF16), preferred_element_type=F32)
             + jnp.dot(a_i.astype(BF16), cim_ref[0].astype(BF16), preferred_element_type=F32) + d_ref[...] * u)
        y_ref[...] = y
        yg_ref[...] = gelu(y)

    uspec = pl.BlockSpec((T, 128), lambda k, i: (i, k))
    sspec = pl.BlockSpec((T, WC), lambda k, i: (i, k))
    return carried(
        body, comm, grid=(8, nT),
        in_specs=[uspec, pl.BlockSpec((1, 128), lambda k, i: (0, k)),
                  pl.BlockSpec((1, 128, WC), lambda k, i: (k, 0, 0)), pl.BlockSpec((1, 128, WC), lambda k, i: (k, 0, 0)),
                  pl.BlockSpec((1, WC, 128), lambda k, i: (k, 0, 0)), pl.BlockSpec((1, WC, 128), lambda k, i: (k, 0, 0)),
                  pl.BlockSpec((T, WC), lambda k, i: (0, k)), pl.BlockSpec((T, WC), lambda k, i: (0, k))],
        out_specs=[uspec, uspec, sspec, sspec],
        out_shape=[jax.ShapeDtypeStruct((L, 1024), F32)] * 2 + [jax.ShapeDtypeStruct((L, 8 * WC), F32)] * 2,
        scratch_shapes=[pltpu.VMEM((1, WC), F32), pltpu.VMEM((1, WC), F32)],
        semantics=("parallel", "arbitrary"), name='a_ssm')(proj, d_skip, Bre, Bim, Cre, Cim, pr, pi)


def s5_bwd(proj, dyg1, dyg2, y, d_skip, s_re, s_im, Bre, Bim, Cre, Cim, prr, pir, comm=None):
    L = proj.shape[0]
    T, WC = min(SSM_T, L), SSM_WC
    nT = L // T

    def body(u_ref, g1_ref, g2_ref, y_ref, d_ref, sr_ref, si_ref, spr_ref, spi_ref, bre_ref, bim_ref, cre_ref, cim_ref,
             pr_ref, pi_ref, du_ref, dd_ref, dbre_ref, dbim_ref, dcre_ref, dcim_ref, dar_ref, dai_ref, cr, ci):
        i = pl.program_id(1)

        @pl.when(i == 0)
        def _():
            for ref in (cr, ci, dd_ref, dbre_ref, dbim_ref, dcre_ref, dcim_ref, dar_ref, dai_ref):
                ref[...] = jnp.zeros_like(ref)

        u = u_ref[...]
        dy = (g1_ref[...] + g2_ref[...]) * gelu_grad(y_ref[...])
        dd_ref[...] += jnp.sum(dy * u, axis=0, keepdims=True)
        dyb, ub = dy.astype(BF16), u.astype(BF16)
        bre, bim, cre, cim = (r[0].astype(BF16) for r in (bre_ref, bim_ref, cre_ref, cim_ref))
        g_r = lax.dot_general(dyb, cre, _NT, preferred_element_type=F32)
        g_i = lax.dot_general(dyb, cim, _NT, preferred_element_type=F32)
        g_r, g_i = _scan_chunk(g_r, g_i, pr_ref, pi_ref, cr, ci, T, True)
        s_r, s_i = sr_ref[...], si_ref[...]
        row = lax.broadcasted_iota(jnp.int32, (T, WC), 0)
        first = (nT - 1 - i) == 0
        sp_r = jnp.where(row == 0, jnp.where(first, 0.0, spr_ref[7:8, :]), pltpu.roll(s_r, 1, 0))
        sp_i = jnp.where(row == 0, jnp.where(first, 0.0, spi_ref[7:8, :]), pltpu.roll(s_i, 1, 0))
        dar_ref[...] += jnp.sum(g_r * sp_r + g_i * sp_i, axis=0, keepdims=True)
        dai_ref[...] += jnp.sum(g_i * sp_r - g_r * sp_i, axis=0, keepdims=True)
        grb, gib = g_r.astype(BF16), g_i.astype(BF16)
        dcre_ref[0] += lax.dot_general(s_r.astype(BF16), dyb, _TN, preferred_element_type=F32)
        dcim_ref[0] += lax.dot_general(s_i.astype(BF16), dyb, _TN, preferred_element_type=F32)
        dbre_ref[0] += lax.dot_general(ub, grb, _TN, preferred_element_type=F32)
        dbim_ref[0] += lax.dot_general(ub, gib, _TN, preferred_element_type=F32)
        du_ref[...] = (dy * d_ref[...] + lax.dot_general(grb, bre, _NT, preferred_element_type=F32)
                       + lax.dot_general(gib, bim, _NT, preferred_element_type=F32))

    uspec = pl.BlockSpec((T, 128), lambda k, i: (nT - 1 - i, k))
    sspec = pl.BlockSpec((T, WC), lambda k, i: (nT - 1 - i, k))
    pspec = pl.BlockSpec((8, WC), lambda k, i: (jnp.maximum((nT - 1 - i) * (T // 8) - 1, 0), k))
    tab = pl.BlockSpec((T, WC), lambda k, i: (0, k))
    bspec = pl.BlockSpec((1, 128, WC), lambda k, i: (k, 0, 0))
    cspec = pl.BlockSpec((1, WC, 128), lambda k, i: (k, 0, 0))
    return carried(
        body, comm, grid=(8, nT),
        in_specs=[uspec, uspec, uspec, uspec, pl.BlockSpec((1, 128), lambda k, i: (0, k)), sspec, sspec, pspec, pspec,
                  bspec, bspec, cspec, cspec, tab, tab],
        out_specs=[uspec, pl.BlockSpec((1, 128), lambda k, i: (0, k)), bspec, bspec, cspec, cspec,
                   pl.BlockSpec((1, WC), lambda k, i: (0, k)), pl.BlockSpec((1, WC), lambda k, i: (0, k))],
        out_shape=[jax.ShapeDtypeStruct((L, 1024), F32), jax.ShapeDtypeStruct((1, 1024), F32),
                   jax.ShapeDtypeStruct((8, 128, WC), F32), jax.ShapeDtypeStruct((8, 128, WC), F32),
                   jax.ShapeDtypeStruct((8, WC, 128), F32), jax.ShapeDtypeStruct((8, WC, 128), F32),
                   jax.ShapeDtypeStruct((1, 8 * WC), F32), jax.ShapeDtypeStruct((1, 8 * WC), F32)],
        scratch_shapes=[pltpu.VMEM((1, WC), F32), pltpu.VMEM((1, WC), F32)],
        semantics=("parallel", "arbitrary"), name='a_ssm_bwd')(
            proj, dyg1, dyg2, y, d_skip, s_re, s_im, s_re, s_im, Bre, Bim, Cre, Cim, prr, pir)


def s5_discretize(lam_re, lam_im, log_dt, b_re, b_im):
    dt = jnp.exp(log_dt)[:, None]
    mag = jnp.exp(lam_re * dt)
    ab_re = mag * jnp.cos(lam_im * dt)
    ab_im = mag * jnp.sin(lam_im * dt)
    den = lam_re * lam_re + lam_im * lam_im
    nr = ab_re - 1.0
    f_re = (nr * lam_re + ab_im * lam_im) / den
    f_im = (ab_im * lam_re - nr * lam_im) / den
    bb_re = f_re[..., None] * b_re - f_im[..., None] * b_im
    bb_im = f_re[..., None] * b_im + f_im[..., None] * b_re
    return ab_re, ab_im, bb_re, bb_im


_EYE8 = np.eye(8, dtype=np.float32)


def _b_tiles(bb):
    t = bb.transpose(0, 2, 1).reshape(8, 8, SSM_H, SSM_P)
    return jnp.einsum('kghp,gG->kghGp', t, _EYE8).reshape(8, 8 * SSM_H, 8 * SSM_P)


def _b_untile(d):
    t = jnp.einsum('kghGp,gG->kghp', d.reshape(8, 8, SSM_H, 8, SSM_P), _EYE8)
    return t.reshape(SSM_G, SSM_H, SSM_P).transpose(0, 2, 1)


def _c_tiles(c):
    t = c.transpose(0, 2, 1).reshape(8, 8, SSM_P, SSM_H)
    return jnp.einsum('kgph,gG->kgpGh', t, _EYE8).reshape(8, 8 * SSM_P, 8 * SSM_H)


def _c_untile(d):
    t = jnp.einsum('kgpGh,gG->kgph', d.reshape(8, 8, SSM_P, 8, SSM_H), _EYE8)
    return t.reshape(SSM_G, SSM_P, SSM_H).transpose(0, 2, 1)


def s5_powers(ar, ai, T):
    W = ar.shape[1]

    def body(ar_ref, ai_ref, fr_ref, fi_ref, rr_ref, ri_ref):
        fr_ref[0:1, :] = ar_ref[...]
        fi_ref[0:1, :] = ai_ref[...]
        rr_ref[T - 1:T, :] = ar_ref[...]
        ri_ref[T - 1:T, :] = ai_ref[...]
        n = 1
        while n < T:
            cr, ci = fr_ref[0:n, :], fi_ref[0:n, :]
            lr, li = fr_ref[n - 1:n, :], fi_ref[n - 1:n, :]
            fr_ref[n:2 * n, :] = cr * lr - ci * li
            fi_ref[n:2 * n, :] = cr * li + ci * lr
            cr, ci = rr_ref[T - n:T, :], ri_ref[T - n:T, :]
            rr_ref[T - 2 * n:T - n, :] = cr * lr - ci * li
            ri_ref[T - 2 * n:T - n, :] = cr * li + ci * lr
            n *= 2

    spec = pl.BlockSpec((T, SSM_WC), lambda j: (0, j))
    aspec = pl.BlockSpec((1, SSM_WC), lambda j: (0, j))
    return pl.pallas_call(
        body, grid=(W // SSM_WC,), in_specs=[aspec, aspec], out_specs=[spec] * 4,
        out_shape=[jax.ShapeDtypeStruct((T, W), F32)] * 4,
        compiler_params=pltpu.CompilerParams(dimension_semantics=("parallel",)), name='a_powers')(ar, ai)


def layer_a_fwd(h, w, p, comm=None, on_carried=None):
    L = h.shape[0]
    proj = mm(h, w['a_w_in'], 'nn', 'a_proj')
    disc = lambda *a: s5_discretize(*a)
    (ab_re, ab_im, bb_re, bb_im), disc_vjp = jax.vjp(disc, p['a_lam_re'][0], p['a_lam_im'][0], p['a_log_dt'][0],
                                                     p['a_b_re'][0], p['a_b_im'][0])
    Bre, Bim = _b_tiles(bb_re), _b_tiles(bb_im)
    Cre, Cim = _c_tiles(p['a_c_re'][0]), -_c_tiles(p['a_c_im'][0])
    T = min(SSM_T, L)
    pr, pi, prr, pir = s5_powers(ab_re.reshape(1, -1), ab_im.reshape(1, -1), T)
    (y, yg, s_re, s_im), carried_out = s5_fwd(proj, p['a_d'], Bre, Bim, Cre, Cim, pr, pi, comm=comm)
    if on_carried is not None:
        on_carried(carried_out)
    gl = mm(yg, w['a_w_glu'], 'nn', 'a_glu')

    def f2(yg_, gl_, z, bg):
        return [yg_ * sigmoid(gl_ + bg) * silu(z)], []
    (po,), _ = rowwise(f2, [rw(yg), rw(gl), rw(proj, 1024, 1)], [p['a_b_glu']], [(1024, BF16)], [], 256, 'a_gate')
    yb = mm(po, w['a_w_out'], 'nn', 'a_out')
    saved = dict(carried=carried_out, h=h, proj=proj, disc_vjp=disc_vjp, Bre=Bre, Bim=Bim, Cre=Cre, Cim=Cim, prr=prr, pir=pir, s_re=s_re,
                 s_im=s_im, y=y, yg=yg, gl=gl, po=po)
    return yb, saved


def _dw(g, sink, name, a, b, mm_name):
    if sink is None:
        g[name] = mm(a, b, 'tn', mm_name)
    else:
        sink.put(name, a, b, mm_name)


def layer_a_bwd(dyb, w, p, sv, comm=None, sink=None):
    g = {}
    dpo = mm(dyb, w['a_w_out'], 'nt', 'a_dpo')
    _dw(g, sink, 'a_w_out', sv['po'], dyb, 'a_dwout')
    proj = sv['proj']

    def f1(dpo_, yg, gl, z, bg):
        sg = sigmoid(gl + bg)
        sz = silu(z)
        dm = dpo_ * sz
        dz = dpo_ * (yg * sg) * silu_grad(z)
        dgl = dm * yg * sg * (1.0 - sg)
        return [dz, dm * sg, dgl], [jnp.sum(dgl, axis=0, keepdims=True)]
    (dz, dyg1, dgl), (db_glu,) = rowwise(f1, [rw(dpo), rw(sv['yg']), rw(sv['gl']), rw(proj, 1024, 1)], [p['a_b_glu']],
                                          [(1024, F32), (1024, F32), (1024, BF16)], [(1, 1024)], 256, 'a_gate_bwd')
    g['a_b_glu'] = db_glu
    _dw(g, sink, 'a_w_glu', sv['yg'], dgl, 'a_dwglu')
    dyg2 = mm(dgl, w['a_w_glu'], 'nt', 'a_dyg2')

    if callable(comm):
        comm = comm()
    (du, dd, dBre, dBim, dCre, dCim, da_re, da_im), g['carried'] = s5_bwd(
        proj, dyg1, dyg2, sv['y'], p['a_d'], sv['s_re'], sv['s_im'], sv['Bre'], sv['Bim'], sv['Cre'], sv['Cim'],
        sv['prr'], sv['pir'], comm=comm)
    g['a_d'] = dd
    dCim = -dCim

    def f3(du_, dz_):
        return [jnp.concatenate([du_, dz_], axis=1)], []
    (dproj,), _ = rowwise(f3, [rw(du), rw(dz)], [], [(2048, BF16)], [], 256, 'a_dproj')
    dlr, dli, dldt, dbr, dbi = sv['disc_vjp']((da_re.reshape(SSM_G, SSM_P), da_im.reshape(SSM_G, SSM_P),
                                               _b_untile(dBre), _b_untile(dBim)))
    g['a_lam_re'], g['a_lam_im'], g['a_log_dt'] = dlr[None], dli[None], dldt[None]
    g['a_b_re'], g['a_b_im'] = dbr[None], dbi[None]
    g['a_c_re'], g['a_c_im'] = _c_untile(dCre)[None], _c_untile(dCim)[None]
    _dw(g, sink, 'a_w_in', sv['h'], dproj, 'a_dwin')
    if sink is None:
        dh = mm(dproj, w['a_w_in'], 'nt', 'a_dh')
    else:
        dh, (g['land_a1'],) = mm(dproj, w['a_w_in'], 'nt', 'a_dh', comm=SiblingExchange(sink.bufs['a1']))
    return dh, g


def _t5_bucket_np():
    qi = np.arange(WINDOW)[:, None]
    kj = np.arange(2 * WINDOW)[None, :]
    dist = np.maximum(qi + WINDOW - kj, 0)
    max_exact = REL_BUCKETS // 2
    dist_f = np.maximum(dist, 1).astype(np.float32)
    large = max_exact + (np.log(dist_f / np.float32(max_exact)) / np.float32(math.log(REL_MAX_DIST / max_exact))
                         * np.float32(REL_BUCKETS - max_exact)).astype(np.int32)
    large = np.minimum(large, REL_BUCKETS - 1)
    return np.where(dist < max_exact, dist, large).astype(np.int32)


SWA_GRP = SWA_HEADS // SWA_KV


def _swa_kv(kvp, kvc, kvh):
    kb = jnp.concatenate([kvp[:, kvh * 64:(kvh + 1) * 64], kvc[:, kvh * 64:(kvh + 1) * 64]], 0).astype(BF16)
    vb = jnp.concatenate([kvp[:, 128 + kvh * 64:128 + (kvh + 1) * 64], kvc[:, 128 + kvh * 64:128 + (kvh + 1) * 64]],
                         0).astype(BF16)
    return kb, vb


def _swa_stack(x, kvh):
    return jnp.concatenate([x[:, (kvh * SWA_GRP + g) * 64:(kvh * SWA_GRP + g + 1) * 64] for g in range(SWA_GRP)],
                           axis=0).astype(BF16)


def _swa_group(bias_ref, kvh):
    return bias_ref[kvh * SWA_GRP:(kvh + 1) * SWA_GRP].reshape(SWA_GRP * WINDOW, 2 * WINDOW)


def _swa_sinks(sink_ref, kvh):
    return jnp.concatenate([jnp.broadcast_to(sink_ref[0:1, kvh * SWA_GRP + g:kvh * SWA_GRP + g + 1], (WINDOW, 1))
                            for g in range(SWA_GRP)], axis=0)


def _swa_probs(q, kb, bias_h, sink, valid):
    s = lax.dot_general(q, kb, (((1,), (1,)), ((), ())), preferred_element_type=F32) * (HEAD_DIM ** -0.5)
    s = jnp.where(valid, s + bias_h, NEG_INF)
    m = jnp.maximum(jnp.max(s, axis=-1, keepdims=True), sink)
    e = jnp.exp(s - m)
    es = jnp.exp(sink - m)
    den = jnp.sum(e, axis=-1, keepdims=True) + es
    return e / den, es / den


def _swa_valid(n):
    qi = lax.broadcasted_iota(jnp.int32, (SWA_GRP * WINDOW, 2 * WINDOW), 0) & (WINDOW - 1)
    kj = lax.broadcasted_iota(jnp.int32, (SWA_GRP * WINDOW, 2 * WINDOW), 1)
    dist = qi + WINDOW - kj
    return (dist >= 0) & (dist < WINDOW) & ((kj >= WINDOW) | (n > 0))


def swa_fwd(proj, bias, sinks, comm=None):
    L = proj.shape[0]

    def body(z_ref, q_ref, kvc_ref, kvp_ref, bias_ref, sink_ref, o_ref, po_ref):
        n = pl.program_id(0)
        valid = _swa_valid(n)
        q, kvc, kvp = q_ref[...], kvc_ref[...], kvp_ref[...]
        outs = []
        for kvh in range(SWA_KV):
            kb, vb = _swa_kv(kvp, kvc, kvh)
            p, _ = _swa_probs(_swa_stack(q, kvh), kb, _swa_group(bias_ref, kvh), _swa_sinks(sink_ref, kvh), valid)
            o8 = jnp.dot(p.astype(BF16), vb, preferred_element_type=F32)
            outs += [o8[g * WINDOW:(g + 1) * WINDOW] for g in range(SWA_GRP)]
        o = jnp.concatenate(outs, axis=1)
        o_ref[...] = o
        po_ref[...] = (o * silu(z_ref[...])).astype(po_ref.dtype)

    return carried(
        body, comm, grid=(L // WINDOW,),
        in_specs=[pl.BlockSpec((WINDOW, 1024), lambda n: (n, 0)), pl.BlockSpec((WINDOW, 1024), lambda n: (n, 1)),
                  pl.BlockSpec((WINDOW, 256), lambda n: (n, 8)),
                  pl.BlockSpec((WINDOW, 256), lambda n: (jnp.maximum(n - 1, 0), 8)),
                  pl.BlockSpec((SWA_HEADS, WINDOW, 2 * WINDOW), lambda n: (0, 0, 0)),
                  pl.BlockSpec((1, SWA_HEADS), lambda n: (0, 0))],
        out_specs=[pl.BlockSpec((WINDOW, 1024), lambda n: (n, 0))] * 2,
        out_shape=[jax.ShapeDtypeStruct((L, 1024), F32), jax.ShapeDtypeStruct((L, 1024), BF16)],
        semantics=("parallel",), name='b_attn')(proj, proj, proj, proj, bias, sinks)


def swa_bwd(proj, do, bias, sinks, comm=None):
    L = proj.shape[0]

    def body(q_ref, kvc_ref, kvp_ref, do_ref, bias_ref, sink_ref, dq_ref, dkv_ref, dbias_ref, dsink_ref):
        n = pl.program_id(0)

        @pl.when(n == 0)
        def _():
            dkv_ref[...] = jnp.zeros_like(dkv_ref)
            dbias_ref[...] = jnp.zeros_like(dbias_ref)
            dsink_ref[...] = jnp.zeros_like(dsink_ref)

        valid = _swa_valid(n)
        q, kvc, kvp, do_ = q_ref[...], kvc_ref[...], kvp_ref[...], do_ref[...]
        dqs, dks, dvs, dsk = [], [], [], []
        for kvh in range(SWA_KV):
            kb, vb = _swa_kv(kvp, kvc, kvh)
            q8, do8 = _swa_stack(q, kvh), _swa_stack(do_, kvh)
            p, ps = _swa_probs(q8, kb, _swa_group(bias_ref, kvh), _swa_sinks(sink_ref, kvh), valid)
            dp = lax.dot_general(do8, vb, (((1,), (1,)), ((), ())), preferred_element_type=F32)
            delta = jnp.sum(p * dp, axis=-1, keepdims=True)
            ds = p * (dp - delta)
            col = -ps * delta
            dsk += [jnp.sum(col[g * WINDOW:(g + 1) * WINDOW], axis=0, keepdims=True) for g in range(SWA_GRP)]
            dbias_ref[kvh * SWA_GRP:(kvh + 1) * SWA_GRP] += ds.reshape(SWA_GRP, WINDOW, 2 * WINDOW)
            dsb = (ds * (HEAD_DIM ** -0.5)).astype(BF16)
            dq8 = jnp.dot(dsb, kb, preferred_element_type=F32)
            dqs += [dq8[g * WINDOW:(g + 1) * WINDOW] for g in range(SWA_GRP)]
            dks.append(lax.dot_general(dsb, q8, (((0,), (0,)), ((), ())), preferred_element_type=F32))
            dvs.append(lax.dot_general(p.astype(BF16), do8, (((0,), (0,)), ((), ())), preferred_element_type=F32))
        dq_ref[...] = jnp.concatenate(dqs, axis=1)
        dsink_ref[...] += jnp.concatenate(dsk, axis=1)
        both = jnp.concatenate(dks + dvs, axis=1)
        r_cur = pl.multiple_of(n * WINDOW, WINDOW)
        r_prev = pl.multiple_of(jnp.maximum(n - 1, 0) * WINDOW, WINDOW)
        dkv_ref[pl.ds(r_prev, WINDOW), :] += both[:WINDOW]
        dkv_ref[pl.ds(r_cur, WINDOW), :] += both[WINDOW:]

    return carried(
        body, comm, grid=(L // WINDOW,),
        in_specs=[pl.BlockSpec((WINDOW, 1024), lambda n: (n, 1)), pl.BlockSpec((WINDOW, 256), lambda n: (n, 8)),
                  pl.BlockSpec((WINDOW, 256), lambda n: (jnp.maximum(n - 1, 0), 8)),
                  pl.BlockSpec((WINDOW, 1024), lambda n: (n, 0)),
                  pl.BlockSpec((SWA_HEADS, WINDOW, 2 * WINDOW), lambda n: (0, 0, 0)),
                  pl.BlockSpec((1, SWA_HEADS), lambda n: (0, 0))],
        out_specs=[pl.BlockSpec((WINDOW, 1024), lambda n: (n, 0)), pl.BlockSpec((L, 256), lambda n: (0, 0)),
                   pl.BlockSpec((SWA_HEADS, WINDOW, 2 * WINDOW), lambda n: (0, 0, 0)),
                   pl.BlockSpec((1, SWA_HEADS), lambda n: (0, 0))],
        out_shape=[jax.ShapeDtypeStruct((L, 1024), F32), jax.ShapeDtypeStruct((L, 256), F32),
                   jax.ShapeDtypeStruct((SWA_HEADS, WINDOW, 2 * WINDOW), F32), jax.ShapeDtypeStruct((1, SWA_HEADS), F32)],
        semantics=("arbitrary",), name='b_attn_bwd')(proj, proj, proj, do, bias, sinks)


def swa_bias(rel_bias):
    def body(bk_ref, rb_ref, o_ref):
        bk = bk_ref[...]
        for h in range(SWA_HEADS):
            acc = jnp.zeros((WINDOW, 2 * WINDOW), F32)
            for b in range(REL_BUCKETS):
                acc = jnp.where(bk == b, rb_ref[b, h], acc)
            o_ref[h] = acc

    return pl.pallas_call(
        body, out_shape=jax.ShapeDtypeStruct((SWA_HEADS, WINDOW, 2 * WINDOW), F32),
        in_specs=[pl.BlockSpec(memory_space=pltpu.VMEM), pl.BlockSpec(memory_space=pltpu.SMEM)],
        out_specs=pl.BlockSpec(memory_space=pltpu.VMEM), name='b_bias')(jnp.asarray(_t5_bucket_np()), rel_bias)


def layer_b_fwd(h, w, p, comm=None):
    proj = mm(h, w['b_w_in'], 'nn', 'b_proj')
    bias = swa_bias(p['rel_bias'])
    (o, po), carried_out = swa_fwd(proj, bias, p['b_sinks'], comm=comm)
    yb = mm(po, w['b_w_out'], 'nn', 'b_out')
    return yb, dict(carried=carried_out, h=h, proj=proj, bias=bias, o=o, po=po)


def layer_b_bwd(dyb, w, p, sv, comm=None, sink=None):
    g = {}
    dpo = mm(dyb, w['b_w_out'], 'nt', 'b_dpo')
    _dw(g, sink, 'b_w_out', sv['po'], dyb, 'b_dwout')
    proj = sv['proj']

    def f1(dpo_, o, z):
        return [dpo_ * silu(z), dpo_ * o * silu_grad(z)], []
    (do, dz), _ = rowwise(f1, [rw(dpo), rw(sv['o']), rw(proj, 1024, 0)], [], [(1024, BF16), (1024, F32)], [], 256, 'b_gate_bwd')
    (dq, dkv, dbias, dsinks), g['carried'] = swa_bwd(proj, do, sv['bias'], p['b_sinks'], comm=comm)
    g['b_sinks'] = dsinks
    onehot = jnp.asarray(np.eye(REL_BUCKETS, dtype=np.float32)[_t5_bucket_np().reshape(-1)])

    def f2(db, oh):
        return [], [lax.dot_general(db, oh, (((1,), (0,)), ((), ())), preferred_element_type=F32,
                                    precision=lax.Precision.HIGHEST)]
    _, (drel,) = rowwise(f2, [(dbias.reshape(SWA_HEADS, -1), pl.BlockSpec((SWA_HEADS, 4096), lambda i: (0, i))),
                              (onehot, pl.BlockSpec((4096, REL_BUCKETS), lambda i: (i, 0)))], [], [],
                         [(SWA_HEADS, REL_BUCKETS)], 4096, 'b_drel', n_steps=(2 * WINDOW * WINDOW) // 4096)
    g['rel_bias'] = drel.T

    def f3(dz_, dq_, dkv_):
        return [jnp.concatenate([dz_, dq_, dkv_], axis=1)], []
    (dproj,), _ = rowwise(f3, [rw(dz), rw(dq), rw(dkv)], [], [(2304, BF16)], [], 256, 'b_dproj')
    _dw(g, sink, 'b_w_in', sv['h'], dproj, 'b_dwin')
    dh = mm(dproj, w['b_w_in'], 'nt', 'b_dh')
    return dh, g


MLA_SCALE = (MLA_NOPE + MLA_ROPE) ** -0.5


def _rope_tables(L):
    inv = ROPE_BASE ** (-jnp.arange(0, MLA_ROPE, 2, dtype=F32) / MLA_ROPE)
    ang = jnp.arange(L, dtype=F32)[:, None] * inv[None, :]
    c, s = jnp.cos(ang), jnp.sin(ang)
    one, zero, pad = jnp.ones((L, 128), F32), jnp.zeros((L, 128), F32), jnp.zeros((L, 64), F32)
    return (jnp.concatenate([one, c, c, c, c, pad], 1), jnp.concatenate([zero, s, s, s, s, pad], 1))


def _rot(x, transpose=False):
    w = x.shape[1]
    lane = lax.broadcasted_iota(jnp.int32, x.shape, 1)
    up = pltpu.roll(x, w - 16, 1)
    dn = pltpu.roll(x, 16, 1)
    first = (lane % 32) < 16
    return jnp.where(first, up, -dn) if transpose else jnp.where(first, -up, dn)


MLA_QT = 512


def _mla_exp(qf, kf, t, qt):
    n_k = kf.shape[0]
    s = lax.dot_general(qf, kf, (((1,), (1,)), ((), ())), preferred_element_type=F32) * MLA_SCALE
    qpos = t * qt + lax.broadcasted_iota(jnp.int32, (qt, n_k), 0)
    kpos = lax.broadcasted_iota(jnp.int32, (qt, n_k), 1)
    s = jnp.where(kpos <= qpos, s, NEG_INF)
    e = jnp.exp(s - jnp.max(s, axis=-1, keepdims=True))
    return e, jnp.sum(e, axis=-1, keepdims=True)


def _mla_heads(q, kv, kr):
    out = []
    for j in range(2):
        qf = jnp.concatenate([q[:, j * 64:(j + 1) * 64], q[:, 128 + j * 32:128 + (j + 1) * 32]], axis=1)
        kf = jnp.concatenate([kv[:, j * 64:(j + 1) * 64], kr], axis=1)
        out.append((qf, kf, kv[:, 128 + j * 64:128 + (j + 1) * 64]))
    return out


def mla_fwd(q, kv, kr, comm=None):
    L = q.shape[0]
    qt = min(MLA_QT, L)
    nq = L // qt

    def body(q_ref, kv_ref, kr_ref, o_ref):
        for t in range(nq):
            @pl.when(pl.program_id(1) == t)
            def _(t=t):
                n_k = (t + 1) * qt
                outs = []
                for qf, kf, v in _mla_heads(q_ref[...], kv_ref[0:n_k, :], kr_ref[0:n_k, 0:MLA_ROPE]):
                    e, den = _mla_exp(qf, kf, t, qt)
                    outs.append(jnp.dot(e.astype(BF16), v, preferred_element_type=F32) / den)
                o_ref[...] = jnp.concatenate(outs, axis=1)

    return carried(
        body, comm, grid=(MLA_HEADS // 2, nq),
        in_specs=[pl.BlockSpec((qt, 256), lambda hp, n: (n, hp)), pl.BlockSpec((L, 256), lambda hp, n: (0, hp)),
                  pl.BlockSpec((L, 128), lambda hp, n: (0, 0))],
        out_specs=pl.BlockSpec((qt, 128), lambda hp, n: (n, hp)), out_shape=jax.ShapeDtypeStruct((L, 1024), F32),
        semantics=("parallel", "parallel"), name='c_attn')(q, kv, kr)


def mla_bwd(q, kv, kr, do, comm=None):
    L = q.shape[0]
    qt = min(MLA_QT, L)
    nq = L // qt

    def body(q_ref, kv_ref, kr_ref, do_ref, dq_ref, dkv_ref, dkr_ref):
        @pl.when(pl.program_id(1) == 0)
        def _():
            dkv_ref[...] = jnp.zeros_like(dkv_ref)
            dkr_ref[...] = jnp.zeros_like(dkr_ref)

        for t in range(nq):
            @pl.when(pl.program_id(1) == t)
            def _(t=t):
                n_k = (t + 1) * qt
                do_ = do_ref[...]
                dqn, dqr, dkn, dvs = [], [], [], []
                dkr = jnp.zeros((n_k, MLA_ROPE), F32)
                for j, (qf, kf, v) in enumerate(_mla_heads(q_ref[...], kv_ref[0:n_k, :], kr_ref[0:n_k, 0:MLA_ROPE])):
                    doh = do_[:, j * 64:(j + 1) * 64]
                    e, den = _mla_exp(qf, kf, t, qt)
                    p = e * (1.0 / den)
                    dp = lax.dot_general(doh, v, (((1,), (1,)), ((), ())), preferred_element_type=F32)
                    ds = (p * (dp - jnp.sum(p * dp, axis=-1, keepdims=True)) * MLA_SCALE).astype(BF16)
                    dqf = jnp.dot(ds, kf, preferred_element_type=F32)
                    dkf = lax.dot_general(ds, qf, (((0,), (0,)), ((), ())), preferred_element_type=F32)
                    dvs.append(lax.dot_general(p.astype(BF16), doh, (((0,), (0,)), ((), ())), preferred_element_type=F32))
                    dqn.append(dqf[:, :MLA_NOPE])
                    dqr.append(dqf[:, MLA_NOPE:])
                    dkn.append(dkf[:, :MLA_NOPE])
                    dkr = dkr + dkf[:, MLA_NOPE:]
                dq_ref[...] = jnp.concatenate(dqn + dqr + [jnp.zeros((qt, 64), F32)], axis=1)
                dkv_ref[0:n_k, :] += jnp.concatenate(dkn + dvs, axis=1)
                dkr_ref[0, 0:n_k, :] += jnp.concatenate([dkr, jnp.zeros((n_k, 128 - MLA_ROPE), F32)], axis=1)

    return carried(
        body, comm, grid=(MLA_HEADS // 2, nq),
        in_specs=[pl.BlockSpec((qt, 256), lambda hp, n: (n, hp)), pl.BlockSpec((L, 256), lambda hp, n: (0, hp)),
                  pl.BlockSpec((L, 128), lambda hp, n: (0, 0)), pl.BlockSpec((qt, 128), lambda hp, n: (n, hp))],
        out_specs=[pl.BlockSpec((qt, 256), lambda hp, n: (n, hp)), pl.BlockSpec((L, 256), lambda hp, n: (0, hp)),
                   pl.BlockSpec((1, L, 128), lambda hp, n: (hp, 0, 0))],
        out_shape=[jax.ShapeDtypeStruct((L, 2048), F32), jax.ShapeDtypeStruct((L, 2048), F32),
                   jax.ShapeDtypeStruct((MLA_HEADS // 2, L, 128), F32)],
        semantics=("parallel", "arbitrary"), name='c_attn_bwd')(q, kv, kr, do)


def layer_c_fwd(h, w, p, comm=None):
    L = h.shape[0]
    proj = mm(h, w['c_w_in'], 'nn', 'c_proj')

    def f1(c, gq, gk):
        return [rms_fwd(c[:, :768], gq), rms_fwd(c[:, 768:], gk)], []
    (cqn, ckvn), _ = rowwise(f1, [rw(proj, 1024, 1)], [p['c_q_norm'], p['c_kv_norm']], [(768, BF16), (256, BF16)], [],
                             256, 'c_norms')
    qf = mm(cqn, w['c_w_uq'], 'nn', 'c_uq')
    kvf = mm(ckvn, w['c_w_ukv'], 'nn', 'c_ukv', out_dtype=BF16)
    cos, sin = _rope_tables(L)

    def f2(q_, kr_, c, s):
        c8, s8 = jnp.tile(c, (1, 8)), jnp.tile(s, (1, 8))
        return [q_ * c8 + _rot(q_) * s8, kr_ * c[:, 128:] + _rot(kr_) * s[:, 128:]], []
    (q, kr), _ = rowwise(f2, [rw(qf), rw(proj, 128, 16), rw(cos), rw(sin)], [], [(2048, BF16), (128, BF16)], [], 256,
                         'c_rope')
    o, carried_out = mla_fwd(q, kvf, kr, comm=comm)

    def f3(o_, z):
        return [o_ * silu(z)], []
    (po,), _ = rowwise(f3, [rw(o), rw(proj, 1024, 0)], [], [(1024, BF16)], [], 256, 'c_gate')
    yb = mm(po, w['c_w_out'], 'nn', 'c_out')
    return yb, dict(carried=carried_out, h=h, proj=proj, cqn=cqn, ckvn=ckvn, q=q, kv=kvf, kr=kr, o=o, po=po, cos=cos, sin=sin)


def layer_c_bwd(dyb, w, p, sv, comm=None, sink=None):
    g = {}
    dpo = mm(dyb, w['c_w_out'], 'nt', 'c_dpo')
    _dw(g, sink, 'c_w_out', sv['po'], dyb, 'c_dwout')
    proj = sv['proj']
    L = proj.shape[0]

    def f1(dpo_, o, z):
        return [dpo_ * silu(z), dpo_ * o * silu_grad(z)], []
    (do, dz), _ = rowwise(f1, [rw(dpo), rw(sv['o']), rw(proj, 1024, 0)], [], [(1024, BF16), (1024, F32)], [], 256,
                          'c_gate_bwd')
    (dq, dkvf, dkr8), g['carried'] = mla_bwd(sv['q'], sv['kv'], sv['kr'], do, comm=comm)

    def f2(dq_, dkr_, c, s):
        c8, s8 = jnp.tile(c, (1, 8)), jnp.tile(s, (1, 8))
        dk = jnp.sum(dkr_, axis=0)
        return [dq_ * c8 + _rot(dq_ * s8, True), dk * c[:, 128:] + _rot(dk * s[:, 128:], True)], []
    tl = 256
    (dqf, dkr), _ = rowwise(f2, [rw(dq), (dkr8, pl.BlockSpec((8, tl, 128), lambda i: (0, i, 0))), rw(sv['cos']),
                                 rw(sv['sin'])], [], [(2048, BF16), (128, F32)], [], tl, 'c_rope_bwd')
    _dw(g, sink, 'c_w_uq', sv['cqn'], dqf, 'c_dwuq')
    _dw(g, sink, 'c_w_ukv', sv['ckvn'], dkvf, 'c_dwukv')
    dcqn = mm(dqf, w['c_w_uq'], 'nt', 'c_dcqn')
    dckvn = mm(dkvf, w['c_w_ukv'], 'nt', 'c_dckvn')

    def f3(c, dq_, dk_, dz_, dkr_, gq, gk):
        dcq, dgq = rms_bwd(c[:, :768], gq, dq_)
        dckv, dgk = rms_bwd(c[:, 768:], gk, dk_)
        return [jnp.concatenate([dz_, dcq, dckv, dkr_], axis=1)], [dgq, dgk]
    (dproj,), (dgq, dgk) = rowwise(f3, [rw(proj, 1024, 1), rw(dcqn), rw(dckvn), rw(dz), rw(dkr)],
                                   [p['c_q_norm'], p['c_kv_norm']], [(2176, BF16)], [(1, 768), (1, 256)], 256, 'c_dproj')
    g['c_q_norm'], g['c_kv_norm'] = dgq, dgk
    _dw(g, sink, 'c_w_in', sv['h'], dproj, 'c_dwin')
    dh = mm(dproj, w['c_w_in'], 'nt', 'c_dh')
    return dh, g


def _sgu_mix(wm, v, transpose):
    outs = []
    dims = (((0,), (0,)), ((), ())) if transpose else (((1,), (0,)), ((), ()))
    for gi in range(SGU_G):
        outs.append(lax.dot_general(wm[gi], v[:, gi * SGU_C:(gi + 1) * SGU_C].astype(BF16), dims,
                                    preferred_element_type=F32))
    return jnp.concatenate(outs, axis=1)


def _sgu_wmask(ws):
    t = lax.broadcasted_iota(jnp.int32, (SGU_T, SGU_T), 0)
    s = lax.broadcasted_iota(jnp.int32, (SGU_T, SGU_T), 1)
    return jnp.where((s <= t)[None], ws, 0.0).astype(BF16)


def _ln_stats(v):
    mu = jnp.mean(v, axis=-1, keepdims=True)
    vc = v - mu
    rstd = lax.rsqrt(jnp.mean(vc * vc, axis=-1, keepdims=True) + EPS)
    return vc * rstd, rstd


def layer_d_fwd(h, w, p):
    proj = mm(h, w['d_w_in'], 'nn', 'd_proj')
    bias = jnp.repeat(p['d_b_s'][0].T, SGU_C, axis=1)

    def f1(u_, v_, z, ws, lg, lb, bs):
        xh, _ = _ln_stats(gelu(v_))
        s = _sgu_mix(_sgu_wmask(ws), xh * lg + lb, False) + bs
        return [gelu(u_) * s * silu(z)], []
    (po,), _ = rowwise(f1, [rw(proj, 1024, 0), rw(proj, 1024, 1), rw(proj, 1024, 2)],
                       [p['d_w_s'][0], p['d_ln_g'], p['d_ln_b'], bias], [(1024, BF16)], [], SGU_T, 'd_mix')
    yb = mm(po, w['d_w_out'], 'nn', 'd_out')
    return yb, dict(h=h, proj=proj, po=po, bias=bias)


def layer_d_bwd(dyb, w, p, sv, sink=None):
    g = {}
    dpo = mm(dyb, w['d_w_out'], 'nt', 'd_dpo')
    _dw(g, sink, 'd_w_out', sv['po'], dyb, 'd_dwout')
    proj = sv['proj']

    def f1(dpo_, u_, v_, z, ws, lg, lb, bs):
        wm = _sgu_wmask(ws)
        gv = gelu(v_)
        xh, rstd = _ln_stats(gv)
        vn = xh * lg + lb
        s = _sgu_mix(wm, vn, False) + bs
        gu, sz = gelu(u_), silu(z)
        du = dpo_ * s * sz
        ds = dpo_ * gu * sz
        dz = dpo_ * gu * s * silu_grad(z)
        dsb = ds.astype(BF16)
        dws = jnp.stack([lax.dot_general(dsb[:, gi * SGU_C:(gi + 1) * SGU_C], vn[:, gi * SGU_C:(gi + 1) * SGU_C].astype(BF16),
                                         (((1,), (1,)), ((), ())), preferred_element_type=F32) for gi in range(SGU_G)])
        dvn = _sgu_mix(wm, ds, True)
        dlg = jnp.sum(dvn * xh, axis=0, keepdims=True)
        dlb = jnp.sum(dvn, axis=0, keepdims=True)
        dxh = dvn * lg
        dgv = rstd * (dxh - jnp.mean(dxh, axis=-1, keepdims=True) - xh * jnp.mean(dxh * xh, axis=-1, keepdims=True))
        return ([jnp.concatenate([du * gelu_grad(u_), dgv * gelu_grad(v_), dz], axis=1)], [dws, ds, dlg, dlb])
    (dproj,), (dws, dbs, dlg, dlb) = rowwise(
        f1, [rw(dpo), rw(proj, 1024, 0), rw(proj, 1024, 1), rw(proj, 1024, 2)],
        [p['d_w_s'][0], p['d_ln_g'], p['d_ln_b'], sv['bias']], [(3072, BF16)],
        [(SGU_G, SGU_T, SGU_T), (SGU_T, 1024), (1, 1024), (1, 1024)], SGU_T, 'd_mix_bwd')
    tril = np.tril(np.ones((SGU_T, SGU_T), dtype=bool))
    g['d_w_s'] = jnp.where(tril[None], dws, 0.0)[None]
    g['d_b_s'] = dbs.reshape(SGU_T, SGU_G, SGU_C).sum(-1).T[None]
    g['d_ln_g'], g['d_ln_b'] = dlg, dlb
    _dw(g, sink, 'd_w_in', sv['h'], dproj, 'd_dwin')
    dh = mm(dproj, w['d_w_in'], 'nt', 'd_dh')
    return dh, g


def _coords():
    return lax.axis_index("x"), lax.axis_index("y"), lax.axis_index("c")


class AllGather:
    def __init__(self, x):
        self.ins = [x]
        self.outs = [jax.ShapeDtypeStruct((N_DEV,) + x.shape, x.dtype)]
        self.scratch = [pltpu.SemaphoreType.DMA((7,)), pltpu.SemaphoreType.DMA((7,)), pltpu.SemaphoreType.DMA(())]

    def hooks(self, n_steps):
        return [(0, functools.partial(self.phase, 0), False), (n_steps - 1, functools.partial(self.phase, 1), True),
                (n_steps - 1, functools.partial(self.phase, 2), True)]

    @staticmethod
    def phase(which, ins, outs, scratch):
        (x_ref,), (out_ref,), (send_sems, recv_sems, local_sem) = ins, outs, scratch
        x_, y_, c_ = _coords()
        me, sibling = (x_, y_, c_), (x_, y_, 1 - c_)
        chips = [(1 - x_, y_), (x_, 1 - y_), (1 - x_, 1 - y_)]

        def slot(px, py, pc):
            return out_ref.at[4 * px + 2 * py + pc]

        def copy(k, block, to, src=None):
            return pltpu.make_async_remote_copy(src_ref=slot(*block) if src is None else src, dst_ref=slot(*block),
                                                send_sem=send_sems.at[k], recv_sem=recv_sems.at[k], device_id=to,
                                                device_id_type=MESH)

        mine = pltpu.make_async_copy(x_ref, slot(*me), local_sem)
        first = [copy(0, me, sibling, src=x_ref)]
        first += [copy(1 + j, me, (*chip, c_), src=x_ref) for j, chip in enumerate(chips)]
        passed = [copy(4 + j, (*chip, c_), sibling) for j, chip in enumerate(chips)]
        if which == 0:
            mine.start()
            for cp in first:
                cp.start()
        elif which == 1:
            for j, chip in enumerate(chips):
                copy(1 + j, (*chip, c_), me).wait_recv()
                passed[j].start()
        else:
            copy(0, sibling, me).wait_recv()
            for j, chip in enumerate(chips):
                copy(4 + j, (*chip, 1 - c_), me).wait_recv()
            for cp in first + passed:
                cp.wait_send()
            mine.wait()


class ChipExchange:
    def __init__(self, part):
        self.ins = [part]
        self.outs = [jax.ShapeDtypeStruct((3,) + part.shape[1:], part.dtype)]
        self.scratch = [pltpu.SemaphoreType.DMA((3,)), pltpu.SemaphoreType.DMA((3,))]

    def hooks(self, n_steps):
        return [(0, functools.partial(self.phase, 0), False), (n_steps - 1, functools.partial(self.phase, 1), True)]

    @staticmethod
    def phase(which, ins, outs, scratch):
        (p_ref,), (land_ref,), (send_sems, recv_sems) = ins, outs, scratch
        x_, y_, c_ = _coords()
        copies = []
        for r, (fx, fy) in enumerate([(1, 0), (0, 1), (1, 1)]):
            tx = jnp.where(fx == 1, 1 - x_, x_)
            ty = jnp.where(fy == 1, 1 - y_, y_)
            copies.append(pltpu.make_async_remote_copy(src_ref=p_ref.at[2 * tx + ty], dst_ref=land_ref.at[r],
                                                       send_sem=send_sems.at[r], recv_sem=recv_sems.at[r],
                                                       device_id=(tx, ty, c_), device_id_type=MESH))
        if which == 0:
            for cp in copies:
                cp.start()
        else:
            for cp in copies:
                cp.wait_recv()
            for cp in copies:
                cp.wait_send()


class Both:
    def __init__(self, a, b):
        self.parts = (a, b)
        self.ins, self.outs, self.scratch = a.ins + b.ins, a.outs + b.outs, a.scratch + b.scratch

    def hooks(self, n_steps):
        res, oi, oo, osc = [], 0, 0, 0
        for p in self.parts:
            sl = (slice(oi, oi + len(p.ins)), slice(oo, oo + len(p.outs)), slice(osc, osc + len(p.scratch)))
            res += [(at, functools.partial(self.sub, fn, sl), after) for at, fn, after in p.hooks(n_steps)]
            oi, oo, osc = oi + len(p.ins), oo + len(p.outs), osc + len(p.scratch)
        return res

    @staticmethod
    def sub(fn, sl, ins, outs, scratch):
        fn(ins[sl[0]], outs[sl[1]], scratch[sl[2]])


def run_comm(comm, name):
    def body(*refs):
        ci, co = len(comm.ins), len(comm.outs)
        for _, fn, _ in comm.hooks(1):
            fn(refs[:ci], refs[ci:ci + co], refs[ci + co:])

    return pl.pallas_call(body, out_shape=list(comm.outs), in_specs=[ANY] * len(comm.ins),
                          out_specs=[ANY] * len(comm.outs), scratch_shapes=list(comm.scratch), name=name)(*comm.ins)


def all_gather(x, name):
    return run_comm(AllGather(x), name)[0]


class SiblingExchange:
    def __init__(self, gfull):
        self.ins = [gfull]
        self.outs = [jax.ShapeDtypeStruct((4,) + gfull.shape[1:], gfull.dtype)]
        self.scratch = [pltpu.SemaphoreType.DMA((4,)), pltpu.SemaphoreType.DMA((4,))]

    def hooks(self, n_steps):
        return [(0, functools.partial(self.phase, 0), False), (n_steps - 1, functools.partial(self.phase, 1), True)]

    @staticmethod
    def phase(which, ins, outs, scratch):
        (g_ref,), (land_ref,), (send_sems, recv_sems) = ins, outs, scratch
        x_, y_, c_ = _coords()
        copies = [pltpu.make_async_remote_copy(src_ref=g_ref.at[2 * k + 1 - c_], dst_ref=land_ref.at[k],
                                               send_sem=send_sems.at[k], recv_sem=recv_sems.at[k],
                                               device_id=(x_, y_, 1 - c_), device_id_type=MESH) for k in range(4)]
        if which == 0:
            for cp in copies:
                cp.start()
        else:
            for cp in copies:
                cp.wait_recv()
            for cp in copies:
                cp.wait_send()


def rs_sibling(gfull, tag):
    return run_comm(SiblingExchange(gfull), 'rs_sibling_' + tag)[0]


def rs_pair_add(gfull, land, core, tag):
    _, R, C = gfull.shape
    tl = R

    def body(c_ref, g_ref, l_ref, o_ref):
        o_ref[...] = (g_ref[...].astype(F32) + l_ref[...].astype(F32)).astype(BF16)

    return pl.pallas_call(
        body, out_shape=jax.ShapeDtypeStruct((4, R, C), BF16),
        grid_spec=pltpu.PrefetchScalarGridSpec(
            num_scalar_prefetch=1, grid=(4, R // tl),
            in_specs=[pl.BlockSpec((1, tl, C), lambda k, i, c: (2 * k + c[0], i, 0)),
                      pl.BlockSpec((1, tl, C), lambda k, i, c: (k, i, 0))],
            out_specs=pl.BlockSpec((1, tl, C), lambda k, i, c: (k, i, 0))),
        compiler_params=pltpu.CompilerParams(dimension_semantics=("parallel", "parallel")), name='rs_pair_add_' + tag)(
            core, gfull, land)


def rs_chips(part, tag):
    return run_comm(ChipExchange(part), 'rs_chips_' + tag)[0]


def _adam(wv, gv, mv, vv):
    m = ADAM_B1 * mv + (1.0 - ADAM_B1) * gv
    v = ADAM_B2 * vv + (1.0 - ADAM_B2) * (gv * gv)
    m_hat = m / (1.0 - ADAM_B1 ** ADAM_STEP)
    v_hat = v / (1.0 - ADAM_B2 ** ADAM_STEP)
    delta = -ADAM_LR * (m_hat / (jnp.sqrt(v_hat) + ADAM_EPS) + ADAM_WD * wv)
    return delta, m, v


def _sum4(p_ref, l_ref):
    return ((p_ref[0].astype(F32) + l_ref[0].astype(F32)) + l_ref[1].astype(F32)) + l_ref[2].astype(F32)


def rs_rep_sum(part, land, chip):
    def body(c_ref, p_ref, l_ref, o_ref):
        o_ref[...] = _sum4(p_ref, l_ref).astype(BF16)

    return pl.pallas_call(
        body, out_shape=jax.ShapeDtypeStruct((REP_SLOT, LANES), BF16),
        grid_spec=pltpu.PrefetchScalarGridSpec(
            num_scalar_prefetch=1, grid=(1,),
            in_specs=[pl.BlockSpec((1, REP_SLOT, LANES), lambda i, c: (c[0], 0, 0)),
                      pl.BlockSpec((3, REP_SLOT, LANES), lambda i, c: (0, 0, 0))],
            out_specs=pl.BlockSpec((REP_SLOT, LANES), lambda i, c: (0, 0))),
        compiler_params=pltpu.CompilerParams(dimension_semantics=("parallel",)), name='rs_rep')(chip, part, land)


def adam_param(name, shape, off, w, m, v, chip, part=None, land=None, grep=None, fold=1):
    r, c = shape
    rp, nt, rb = _tiles((r // fold, c * fold))
    rbw = min(r, rb) if fold == 1 else r
    n_src = 2 if grep is None else 1
    ns = w.shape
    assert int(np.prod(ns[:-1])) == r and ns[-1] == c and (fold == 1 or (rb == rp and nt == 1))
    if fold > 1:
        nat_block, nat_map = ns, lambda i, cr: (0,) * len(ns)
    elif len(ns) == 2:
        nat_block, nat_map = (rbw, c), lambda i, cr: (i, 0)
    elif int(np.prod(ns[:-2])) == 1:
        nat_block, nat_map = (1,) * (len(ns) - 2) + (rbw, c), lambda i, cr: (0,) * (len(ns) - 2) + (i, 0)
    else:
        assert len(ns) == 4 and ns[0] == 1 and rbw % ns[2] == 0
        nat_block, nat_map = (1, rbw // ns[2], ns[2], c), lambda i, cr: (0, i, 0, 0)

    def body(c_ref, *refs):
        srcs = refs[:n_src * nt]
        w_ref, m_ref, v_ref, g_ref, d_ref, nm_ref, nv_ref = refs[n_src * nt:]
        if grep is None:
            tiles = [_sum4(srcs[2 * t], srcs[2 * t + 1]) for t in range(nt)]
        else:
            tiles = [srcs[t][...].astype(F32) for t in range(nt)]
        if fold > 1:
            g = jnp.concatenate([tiles[0][:, q * c:(q + 1) * c] for q in range(fold)], axis=0)
        else:
            g = (tiles[0] if nt == 1 else jnp.concatenate(tiles, axis=1))[:rbw, :c]
        g_ref[...] = g.reshape(nat_block)
        res = _adam(w_ref[...].reshape(rbw, c), g, m_ref[...].reshape(rbw, c), v_ref[...].reshape(rbw, c))
        for ref, val in zip((d_ref, nm_ref, nv_ref), res):
            ref[...] = val.reshape(nat_block)

    in_specs, args = [], []
    for t in range(nt):
        b0 = (off + t * rp) // rb
        assert (off + t * rp) % rb == 0
        if grep is None:
            in_specs += [pl.BlockSpec((1, rb, LANES), functools.partial(lambda i, cr, b0: (cr[0], b0 + i, 0), b0=b0)),
                         pl.BlockSpec((3, rb, LANES), functools.partial(lambda i, cr, b0: (0, b0 + i, 0), b0=b0))]
            args += [part, land]
        else:
            in_specs.append(pl.BlockSpec((rb, LANES), functools.partial(lambda i, cr, b0: (b0 + i, 0), b0=b0)))
            args.append(grep)
    nat = pl.BlockSpec(nat_block, nat_map)
    return pl.pallas_call(
        body, out_shape=[jax.ShapeDtypeStruct(ns, F32)] * 4,
        grid_spec=pltpu.PrefetchScalarGridSpec(num_scalar_prefetch=1, grid=(rp // rb,), in_specs=in_specs + [nat] * 3,
                                               out_specs=[nat] * 4),
        compiler_params=pltpu.CompilerParams(dimension_semantics=("parallel",)), name='adam_' + name)(
            chip, *args, w, m, v)


def adam_small(names, grep, P, M, V):
    in_specs, args, out_specs, out_shape, meta = [], [], [], [], []
    for n in names:
        s = REP_SHAPE[n]
        rp, nt, _ = _tiles(s)
        ns = P[n].shape
        for t in range(nt):
            b0 = (REP_OFF[n] + t * rp) // rp
            assert (REP_OFF[n] + t * rp) % rp == 0
            in_specs.append(pl.BlockSpec((rp, LANES), functools.partial(lambda i, b0: (b0, 0), b0=b0)))
            args.append(grep)
        nat = pl.BlockSpec(ns, functools.partial(lambda i, nd: (0,) * nd, nd=len(ns)))
        in_specs += [nat] * 3
        args += [P[n], M[n], V[n]]
        out_specs += [nat] * 4
        out_shape += [jax.ShapeDtypeStruct(ns, F32)] * 4
        meta.append((s, nt, ns))
    n_in = len(in_specs)

    def body(*refs):
        ins, outs = refs[:n_in], refs[n_in:]
        k = 0
        for p, ((r, c), nt, ns) in enumerate(meta):
            tiles = [ins[k + t][...].astype(F32) for t in range(nt)]
            w_ref, m_ref, v_ref = ins[k + nt:k + nt + 3]
            k += nt + 3
            g = (tiles[0] if nt == 1 else jnp.concatenate(tiles, axis=1))[:r, :c]
            res = (g,) + _adam(w_ref[...].reshape(r, c), g, m_ref[...].reshape(r, c), v_ref[...].reshape(r, c))
            for ref, val in zip(outs[4 * p:4 * p + 4], res):
                ref[...] = val.reshape(ns)

    res = pl.pallas_call(body, grid=(1,), in_specs=in_specs, out_specs=out_specs, out_shape=out_shape,
                         compiler_params=pltpu.CompilerParams(dimension_semantics=("arbitrary",)), name='adam_small')(*args)
    return {n: tuple(res[4 * p:4 * p + 4]) for p, n in enumerate(names)}


VM = pl.BlockSpec(memory_space=pltpu.VMEM)


def _tile_value(w, t, rp):
    r, c = w.shape
    wt = min(LANES, c - t * LANES)
    tile = w[:, t * LANES:t * LANES + wt]
    if wt < LANES:
        tile = jnp.concatenate([tile, jnp.zeros((r, LANES - wt), tile.dtype)], axis=1)
    if rp > r:
        tile = jnp.concatenate([tile, jnp.zeros((rp - r, LANES), tile.dtype)], axis=0)
    return tile


def pack_layer(layer, blocks):
    names = LAYER_PARAMS[layer]

    def body(*refs):
        tiles = []
        for ref, n in zip(refs[:-1], names):
            rp, nt, _ = _tiles(_block_shape(n))
            w = ref[...].reshape(_block_shape(n))
            tiles += [_tile_value(w, t, rp) for t in range(nt)]
        refs[-1][...] = jnp.concatenate(tiles, axis=0).astype(BF16)

    return pl.pallas_call(body, out_shape=jax.ShapeDtypeStruct((LAYER_ROWS[layer], LANES), BF16),
                          in_specs=[VM] * len(names), out_specs=VM, name='pack_' + layer)(*[blocks[n] for n in names])


def assemble(name, gathered):
    (rf, cf), ax = SHARDED[name]
    r, c = _block_shape(name)
    rp, nt, _ = _tiles((r, c))
    off = SH_OFF[name]
    out_cols = cf if ax == 0 else len(perm_index(name))

    def body(g_ref, o_ref, buf, sem):
        cp = pltpu.make_async_copy(g_ref.at[:, pl.ds(off, nt * rp), :], buf, sem)
        cp.start()
        cp.wait()
        if ax == 0:
            for j in range(N_DEV):
                o_ref[j * r:(j + 1) * r, :] = jnp.concatenate([buf[j, t * rp:(t + 1) * rp, :] for t in range(nt)], axis=1)
            return
        pieces = []
        for p in PERM[name]:
            if p[0] == 'z':
                pieces.append(jnp.zeros((r, p[1]), BF16))
                continue
            n0, w = p
            while w > 0:
                j, cb = divmod(n0, c)
                t, lane = divmod(cb, LANES)
                wl = min(w, LANES - lane, c - cb)
                pieces.append(buf[j, t * rp:t * rp + r, lane:lane + wl])
                n0, w = n0 + wl, w - wl
        o_ref[...] = jnp.concatenate(pieces, axis=1)

    return pl.pallas_call(
        body, out_shape=jax.ShapeDtypeStruct((rf, out_cols), BF16), in_specs=[ANY], out_specs=VM,
        scratch_shapes=[pltpu.VMEM((N_DEV, nt * rp, LANES), BF16), pltpu.SemaphoreType.DMA(())], name='asm_' + name)(
            gathered)


def chunk_grad(layer, name, dw, gfull):
    (rf, cf), ax = SHARDED[name]
    r, c = _block_shape(name)
    rp, nt, _ = _tiles((r, c))
    off = SH_OFF[name]
    if ax == 1:
        idx = perm_index(name) if name in PERM else np.arange(cf)
        inv = np.full(cf, -1)
        inv[idx[idx >= 0]] = np.nonzero(idx >= 0)[0]

    def body(*refs):
        dw_ref, o_ref, buf, sem = refs[0], refs[-3], refs[-2], refs[-1]
        for j in range(N_DEV):
            for t in range(nt):
                if ax == 0:
                    tile = dw_ref[j * r:(j + 1) * r, t * LANES:(t + 1) * LANES]
                else:
                    cols = inv[j * c + t * LANES:j * c + min((t + 1) * LANES, c)]
                    cuts = [0] + [k for k in range(1, len(cols)) if cols[k] != cols[k - 1] + 1] + [len(cols)]
                    pieces = [dw_ref[:, int(cols[a]):int(cols[b - 1]) + 1] for a, b in zip(cuts[:-1], cuts[1:])]
                    if len(cols) < LANES:
                        pieces.append(jnp.zeros((r, LANES - len(cols)), F32))
                    tile = pieces[0] if len(pieces) == 1 else jnp.concatenate(pieces, axis=1)
                    if rp > r:
                        tile = jnp.concatenate([tile, jnp.zeros((rp - r, LANES), F32)], axis=0)
                buf[j, t * rp:(t + 1) * rp, :] = tile.astype(BF16)
        cp = pltpu.make_async_copy(buf, o_ref.at[:, pl.ds(off, nt * rp), :], sem)
        cp.start()
        cp.wait()

    shape = jax.ShapeDtypeStruct((N_DEV, LAYER_ROWS[layer], LANES), BF16)
    scratch = [pltpu.VMEM((N_DEV, nt * rp, LANES), BF16), pltpu.SemaphoreType.DMA(())]
    if gfull is None:
        return pl.pallas_call(body, out_shape=shape, in_specs=[VM], out_specs=ANY, scratch_shapes=scratch,
                              name='chunk_' + name)(dw)
    return pl.pallas_call(body, out_shape=shape, in_specs=[VM, ANY], out_specs=ANY, scratch_shapes=scratch,
                          input_output_aliases={1: 0}, name='chunk_' + name)(dw, gfull)


class GradSink:
    def __init__(self):
        self.bufs = {}

    def put(self, name, a, b, mm_name):
        (rf, cf), ax = SHARDED[name]
        r, c = _block_shape(name)
        group = GROUP_OF[name]
        direct = ax == 0 or (c % LANES == 0 and PERM[name] == [(0, cf)])
        if direct:
            self.bufs[group] = mm_tn_chunked(a, b, mm_name, group, name, self.bufs.get(group))
        else:
            self.add(name, mm(a, b, 'tn', mm_name))

    def add(self, name, dw):
        group = GROUP_OF[name]
        self.bufs[group] = chunk_grad(group, name, dw, self.bufs.get(group))


def mm_tn_chunked(a, b, mm_name, layer, wname, gfull):
    (rf, cf), ax = SHARDED[wname]
    r, c = _block_shape(wname)
    rp, nt, _ = _tiles((r, c))
    off = SH_OFF[wname]
    K, M = a.shape
    N = b.shape[1]
    assert (M, N) == (rf, cf) and rp == r
    if ax == 0:
        tn = 4 * LANES
        grid, bspec = (N // tn,), pl.BlockSpec((K, tn), lambda g: (0, g))
        ospec = pl.BlockSpec((N_DEV, 4 * r, LANES), lambda g: (0, off // (4 * r) + g, 0))
        assert off % (4 * r) == 0 and nt % 4 == 0

        def store(res, o_ref):
            for j in range(N_DEV):
                for q in range(4):
                    o_ref[j, q * r:(q + 1) * r, :] = res[j * r:(j + 1) * r, q * LANES:(q + 1) * LANES].astype(BF16)
    else:
        tn = c
        grid, bspec = (N_DEV,), pl.BlockSpec((K, tn), lambda g: (0, g))
        ospec = pl.BlockSpec((1, nt * r, LANES), lambda g: (g, off // (nt * r), 0))
        assert off % (nt * r) == 0

        def store(res, o_ref):
            for t in range(nt):
                o_ref[0, t * r:(t + 1) * r, :] = res[:, t * LANES:(t + 1) * LANES].astype(BF16)

    def body(*refs):
        a_ref, b_ref, o_ref = refs[0], refs[1], refs[-1]
        store(lax.dot_general(a_ref[...].astype(BF16), b_ref[...].astype(BF16), _TN, preferred_element_type=F32), o_ref)

    shape = jax.ShapeDtypeStruct((N_DEV, LAYER_ROWS[layer], LANES), BF16)
    aspec = pl.BlockSpec((K, M), lambda g: (0, 0))
    params = pltpu.CompilerParams(dimension_semantics=("parallel",))
    if gfull is None:
        return pl.pallas_call(body, grid=grid, in_specs=[aspec, bspec], out_specs=ospec, out_shape=shape,
                              compiler_params=params, name=mm_name)(a, b)
    return pl.pallas_call(body, grid=grid, in_specs=[aspec, bspec, ANY], out_specs=ospec, out_shape=shape,
                          input_output_aliases={2: 0}, compiler_params=params, name=mm_name)(a, b, gfull)


def pack_rep(G):
    def body(*refs):
        tiles = []
        for ref, n in zip(refs[:-1], REP_SHAPE):
            rp, nt, _ = _tiles(_rep_packed_shape(n))
            g = ref[...]
            fold = REP_FOLD.get(n, 1)
            if fold > 1:
                rr = g.shape[0] // fold
                g = jnp.concatenate([g[q * rr:(q + 1) * rr] for q in range(fold)], axis=1)
            tiles += [_tile_value(g, t, rp) for t in range(nt)]
        rows = sum(t.shape[0] for t in tiles)
        if rows < REP_ROWS:
            tiles.append(jnp.zeros((REP_ROWS - rows, LANES), F32))
        full = jnp.concatenate(tiles, axis=0)
        for j in range(N_DEV):
            refs[-1][j] = full[j * REP_CHUNK:(j + 1) * REP_CHUNK]

    return pl.pallas_call(body, out_shape=jax.ShapeDtypeStruct((N_DEV, REP_SLOT, LANES), F32),
                          in_specs=[VM] * len(REP_SHAPE), out_specs=VM, name='pack_rep')(
                              *[G[n].reshape(s) for n, s in REP_SHAPE.items()])


def _pack_small(blocks, order, rows, width, dtype):
    flat = jnp.concatenate([blocks[n].reshape(-1).astype(dtype) for n in order])
    return jnp.pad(flat, (0, rows * width - flat.shape[0])).reshape(rows, width)


def kernel(x, pre_norm, post_norm, rel_bias, a_w_in, a_lam_re, a_lam_im, a_log_dt, a_b_re, a_b_im, a_c_re, a_c_im, a_d, a_w_glu, a_b_glu, a_w_out, b_w_in, b_sinks, b_w_out, c_w_in, c_q_norm, c_kv_norm, c_w_uq, c_w_ukv, c_w_out, d_w_in, d_ln_g, d_ln_b, d_w_s, d_b_s, d_w_out, loss_target, m_pre_norm, m_post_norm, m_rel_bias, m_a_w_in, m_a_lam_re, m_a_lam_im, m_a_log_dt, m_a_b_re, m_a_b_im, m_a_c_re, m_a_c_im, m_a_d, m_a_w_glu, m_a_b_glu, m_a_w_out, m_b_w_in, m_b_sinks, m_b_w_out, m_c_w_in, m_c_q_norm, m_c_kv_norm, m_c_w_uq, m_c_w_ukv, m_c_w_out, m_d_w_in, m_d_ln_g, m_d_ln_b, m_d_w_s, m_d_b_s, m_d_w_out, v_pre_norm, v_post_norm, v_rel_bias, v_a_w_in, v_a_lam_re, v_a_lam_im, v_a_log_dt, v_a_b_re, v_a_b_im, v_a_c_re, v_a_c_im, v_a_d, v_a_w_glu, v_a_b_glu, v_a_w_out, v_b_w_in, v_b_sinks, v_b_w_out, v_c_w_in, v_c_q_norm, v_c_kv_norm, v_c_w_uq, v_c_w_ukv, v_c_w_out, v_d_w_in, v_d_ln_g, v_d_ln_b, v_d_w_s, v_d_b_s, v_d_w_out):
    loc = locals()
    P = {n: loc[n] for n in WEIGHTS}
    M = {n: loc['m_' + n] for n in WEIGHTS}
    V = {n: loc['v_' + n] for n in WEIGHTS}
    xs = x[0]
    L = xs.shape[0]

    blocks = {n: P[n].reshape(_block_shape(n)) for n in SHARDED}
    packed = {layer: pack_layer(layer, P) for layer in LAYER_PARAMS}
    W = {}

    def assemble_layer(layer, gathered):
        for n in LAYER_PARAMS[layer]:
            if n not in SHARDED_F32:
                W[n] = assemble(n, gathered)

    assemble_layer('a1', all_gather(packed['a1'], 'ag_a1'))
    small = all_gather(_pack_small(blocks, SHARDED_F32, SMALL_ROWS, 128, F32), 'ag_small')
    Pl = dict(P)
    for n in SHARDED_F32:
        c = SHARDED[n][0][1]
        bc = c // N_DEV
        Pl[n] = small.reshape(N_DEV, -1)[:, SMALL_OFF[n]:SMALL_OFF[n] + bc].reshape(1, c)
    cx, cy, cc = _coords()
    core = jnp.reshape(cc, (1,)).astype(jnp.int32)
    chip = jnp.reshape(2 * cx + cy, (1,)).astype(jnp.int32)

    def pair_sums(gfull, tag):
        return rs_pair_add(gfull, rs_sibling(gfull, tag), core, tag)

    fwd = [layer_a_fwd, layer_b_fwd, layer_c_fwd, layer_d_fwd]
    bwd = [layer_a_bwd, layer_b_bwd, layer_c_bwd, layer_d_bwd]
    saved = []
    xc = xs

    def fpre(x_, g_):
        return [rms_fwd(x_, g_)], []
    (h,), _ = rowwise(fpre, [rw(xc)], [P['pre_norm'][0:1]], [(D_MODEL, BF16)], [], 256, 'pre_norm0')
    for i in range(4):
        if i == 0:
            yb, sv = fwd[i](h, W, Pl, comm=Both(AllGather(packed['a2']), AllGather(packed['b'])),
                            on_carried=lambda got: assemble_layer('a2', got[0]))
            assemble_layer('b', sv['carried'][1])
        elif i < 3:
            nxt = 'abcd'[i + 1]
            yb, sv = fwd[i](h, W, Pl, comm=AllGather(packed[nxt]))
            assemble_layer(nxt, sv['carried'][0])
        else:
            yb, sv = fwd[i](h, W, Pl)

        sv['x'], sv['yb'] = xc, yb
        saved.append(sv)
        if i < 3:

            def fpost(x_, y_, gpost, gpre):
                xn_ = x_ + rms_fwd(y_, gpost)
                return [xn_, rms_fwd(xn_, gpre)], []
            (xc, h), _ = rowwise(fpost, [rw(xc), rw(yb)], [P['post_norm'][i:i + 1], P['pre_norm'][i + 1:i + 2]],
                                 [(D_MODEL, F32), (D_MODEL, BF16)], [], 256, f'post_pre_norm{i}')
        else:

            def floss(x_, y_, t_, gpost):
                d = x_ + rms_fwd(y_, gpost) - t_
                return [d * (1.0 / D_MODEL)], [0.5 * jnp.sum(jnp.sum(d * d, axis=-1, keepdims=True) * (1.0 / D_MODEL),
                                                             axis=0, keepdims=True)]
            (dx,), (loss_loc,) = rowwise(floss, [rw(xc), rw(yb), rw(loss_target[0])], [P['post_norm'][i:i + 1]],
                                         [(D_MODEL, F32)], [(1, 1)], 256, 'post_norm_loss')
    loss = lax.psum(loss_loc[0, 0], ("x", "y", "c"))

    G, out = {}, {}
    dpre, dpost = [None] * 4, [None] * 4

    def adam_layer(layer, part, land2):
        for n in LAYER_PARAMS[layer]:
            s = _block_shape(n)
            out[n] = adam_param(n, s, SH_OFF[n], P[n], M[n], V[n], chip, part=part, land=land2)

    def fpost_b(y_, d_, g_):
        dy, dg = rms_bwd(y_, g_, d_)
        return [dy], [dg]
    (dyb,), (dpost[3],) = rowwise(fpost_b, [rw(saved[3]['yb']), rw(dx)], [P['post_norm'][3:4]], [(D_MODEL, BF16)],
                                  [(1, D_MODEL)], 256, 'post_norm_bwd3')
    pending = None
    sink = GradSink()
    for i in reversed(range(4)):
        sv = saved[i]
        if pending is None:
            dh, g = bwd[i](dyb, W, Pl, sv, sink=sink)
        elif i > 0:
            dh, g = bwd[i](dyb, W, Pl, sv, comm=ChipExchange(pending[1]), sink=sink)
            adam_layer(pending[0], pending[1], g['carried'][0])
        else:
            early = {}

            def both():
                early['part'] = pair_sums(sink.bufs['a2'], 'a2')
                return Both(ChipExchange(pending[1]), ChipExchange(early['part']))
            dh, g = bwd[i](dyb, W, Pl, sv, comm=both, sink=sink)
            adam_layer(pending[0], pending[1], g['carried'][0])
            adam_layer('a2', early['part'], g['carried'][1])
        g.pop('carried', None)
        land_a1 = g.pop('land_a1', None)
        G.update(g)
        group = LAYER_GROUPS['abcd'[i]][0]
        for n in LAYER_PARAMS[group]:
            if n in g:
                sink.add(n, g[n])
        if i > 0:
            swap = SiblingExchange(sink.bufs[group])
        else:
            part_a1 = rs_pair_add(sink.bufs[group], land_a1, core, group)
            swap = ChipExchange(part_a1)

        if i > 0:

            def fpre_b(x_, dh_, d_, y_, gpre, gpost):
                dxl, dg = rms_bwd(x_, gpre, dh_)
                dy, dgp = rms_bwd(y_, gpost, d_ + dxl)
                return [d_ + dxl, dy], [dg, dgp]
            (dx, dyb), (dpre[i], dpost[i - 1]), (land,) = rowwise(
                fpre_b, [rw(sv['x']), rw(dh), rw(dx), rw(saved[i - 1]['yb'])],
                [P['pre_norm'][i:i + 1], P['post_norm'][i - 1:i]], [(D_MODEL, F32), (D_MODEL, BF16)],
                [(1, D_MODEL), (1, D_MODEL)], 256, f'pre_post_norm_bwd{i}', comm=swap)
        else:

            def fpre_b0(x_, dh_, d_, g_):
                dxl, dg = rms_bwd(x_, g_, dh_)
                return [d_ + dxl], [dg]
            (dx,), (dpre[i],), (land2_a1,) = rowwise(fpre_b0, [rw(sv['x']), rw(dh), rw(dx)], [P['pre_norm'][i:i + 1]],
                                                     [(D_MODEL, F32)], [(1, D_MODEL)], 256, 'pre_norm_bwd0', comm=swap)
            adam_layer('a1', part_a1, land2_a1)
            break
        pending = (group, rs_pair_add(sink.bufs[group], land, core, group))
    G['pre_norm'] = jnp.concatenate(dpre, axis=0)
    G['post_norm'] = jnp.concatenate(dpost, axis=0)

    part = pair_sums(pack_rep(G), 'rep')
    land2 = rs_chips(part, 'rep')
    grep = all_gather(rs_rep_sum(part, land2, chip), 'ag_rep')[:, :REP_CHUNK].reshape(REP_ROWS, LANES)
    small_names = [n for n, s in REP_SHAPE.items() if s[0] <= 64]
    out.update(adam_small(small_names, grep, P, M, V))
    for n, s in REP_SHAPE.items():
        if n not in small_names:
            out[n] = adam_param(n, s, REP_OFF[n], P[n], M[n], V[n], chip, grep=grep, fold=REP_FOLD.get(n, 1))
    res = [loss, dx[None]]
    for kind in range(4):
        res += [out[n][kind].reshape(P[n].shape) for n in WEIGHTS]
    return tuple(res)
```

```python
import functools
import math

import numpy as np
import jax
import jax.numpy as jnp
from jax import lax
from jax.experimental import pallas as pl
from jax.experimental.pallas import tpu as pltpu

F32 = jnp.float32
BF16 = jnp.bfloat16
MESH = pl.DeviceIdType.MESH
ANY = pl.BlockSpec(memory_space=pl.ANY)

N_DEV = 8
D_MODEL = 1024
EPS = 1e-6
NEG_INF = -1e30
SSM_G, SSM_P, SSM_H = 64, 64, 16
SSM_T = 256
SSM_TS = 8
SSM_WC = 512
HEAD_DIM = 64
SWA_HEADS, SWA_KV = 16, 2
WINDOW = 128
REL_BUCKETS, REL_MAX_DIST = 32, 128
MLA_HEADS, MLA_NOPE, MLA_ROPE, MLA_V = 16, 64, 32, 64
MLA_Q_RANK, MLA_KV_RANK = 768, 256
ROPE_BASE = 10000.0
SGU_G, SGU_C, SGU_T = 16, 64, 128
ADAM_LR, ADAM_B1, ADAM_B2, ADAM_EPS, ADAM_WD, ADAM_STEP = 0.001, 0.9, 0.999, 1e-08, 0.01, 10

WEIGHTS = ['pre_norm', 'post_norm', 'rel_bias', 'a_w_in', 'a_lam_re', 'a_lam_im', 'a_log_dt', 'a_b_re', 'a_b_im',
           'a_c_re', 'a_c_im', 'a_d', 'a_w_glu', 'a_b_glu', 'a_w_out', 'b_w_in', 'b_sinks', 'b_w_out', 'c_w_in',
           'c_q_norm', 'c_kv_norm', 'c_w_uq', 'c_w_ukv', 'c_w_out', 'd_w_in', 'd_ln_g', 'd_ln_b', 'd_w_s', 'd_b_s',
           'd_w_out']
SHARDED = {'a_w_in': ((1024, 2048), 1), 'a_w_glu': ((1024, 1024), 0), 'a_w_out': ((1024, 1024), 0),
           'b_w_in': ((1024, 2304), 1), 'b_w_out': ((1024, 1024), 0), 'c_w_in': ((1024, 2080), 1),
           'c_q_norm': ((1, 768), 1), 'c_kv_norm': ((1, 256), 1), 'c_w_uq': ((768, 1536), 1),
           'c_w_ukv': ((256, 2048), 1), 'c_w_out': ((1024, 1024), 0), 'd_w_in': ((1024, 3072), 1),
           'd_ln_g': ((1, 1024), 1), 'd_ln_b': ((1, 1024), 1), 'd_w_out': ((1024, 1024), 0)}
SHARDED_F32 = ['c_q_norm', 'c_kv_norm', 'd_ln_g', 'd_ln_b']
REPLICATED = [n for n in WEIGHTS if n not in SHARDED]


def _cdiv(a, b):
    return -(-a // b)


def _block_shape(name):
    (r, c), ax = SHARDED[name]
    return (r // N_DEV, c) if ax == 0 else (r, c // N_DEV)


LANES = 128
LAYER_PARAMS = {'a1': ['a_w_in'], 'a2': ['a_w_glu', 'a_w_out'], 'b': ['b_w_in', 'b_w_out'],
                'c': ['c_w_in', 'c_w_uq', 'c_w_ukv', 'c_w_out', 'c_q_norm', 'c_kv_norm'],
                'd': ['d_w_in', 'd_w_out', 'd_ln_g', 'd_ln_b']}


def _tiles(shape):
    r, c = shape
    rp = max(r, 16)
    rb = 512 if rp % 512 == 0 else 256 if rp % 256 == 0 else rp
    return rp, _cdiv(c, LANES), rb


SH_OFF, LAYER_ROWS = {}, {}
for _l, _names in LAYER_PARAMS.items():
    _o = 0
    for _n in _names:
        _rp, _nt, _rb = _tiles(_block_shape(_n))
        assert _o % _rb == 0
        SH_OFF[_n] = _o
        _o += _rp * _nt
    assert _o % 16 == 0
    LAYER_ROWS[_l] = _o
GROUP_OF = {_n: _l for _l, _names in LAYER_PARAMS.items() for _n in _names}
LAYER_GROUPS = {'a': ['a1', 'a2'], 'b': ['b'], 'c': ['c'], 'd': ['d']}

REP_SHAPE = {'d_w_s': (2048, 128), 'a_b_re': (4096, 16), 'a_b_im': (4096, 16), 'a_c_re': (1024, 64),
             'a_c_im': (1024, 64), 'pre_norm': (4, 1024), 'post_norm': (4, 1024), 'a_lam_re': (64, 64),
             'a_lam_im': (64, 64), 'a_d': (1, 1024), 'a_b_glu': (1, 1024), 'rel_bias': (32, 16), 'd_b_s': (16, 128),
             'a_log_dt': (1, 64), 'b_sinks': (1, 16)}
REP_FOLD = {'a_b_re': 8, 'a_b_im': 8, 'a_c_re': 2, 'a_c_im': 2}


def _rep_packed_shape(name):
    (r, c), f = REP_SHAPE[name], REP_FOLD.get(name, 1)
    return (r // f, c * f)


REP_OFF = {}
_o = 0
for _n in REP_SHAPE:
    _rp, _nt, _rb = _tiles(_rep_packed_shape(_n))
    assert _o % _rb == 0
    REP_OFF[_n] = _o
    _o += _rp * _nt
REP_ROWS = _cdiv(_o, 16 * N_DEV) * 16 * N_DEV
REP_CHUNK = REP_ROWS // N_DEV
REP_SLOT = REP_CHUNK

PERM = {'a_w_in': [(0, 2048)], 'd_w_in': [(0, 3072)], 'b_w_in': [(1280, 1024), (0, 1280)],
        'c_w_in': [(1056, 1024), (0, 1056), ('z', 96)],
        'c_w_uq': sum([[(2 * hp * 96, 64), ((2 * hp + 1) * 96, 64), (2 * hp * 96 + 64, 32), ((2 * hp + 1) * 96 + 64, 32),
                        ('z', 64)] for hp in range(8)], []),
        'c_w_ukv': sum([[(2 * hp * 128, 64), ((2 * hp + 1) * 128, 64), (2 * hp * 128 + 64, 64),
                         ((2 * hp + 1) * 128 + 64, 64)] for hp in range(8)], [])}


def perm_index(name):
    return np.concatenate([np.full(p[1], -1) if p[0] == 'z' else np.arange(p[0], p[0] + p[1]) for p in PERM[name]])


SMALL_OFF = {}
_o = 0
for _n in SHARDED_F32:
    SMALL_OFF[_n] = _o
    _o += int(np.prod(_block_shape(_n)))
SMALL_ROWS = _cdiv(_o, 128 * 8) * 8


def _pick(n, cands):
    for c in cands:
        if n % c == 0:
            return c
    return n


def mm(a, b, mode, name, out_dtype=F32, comm=None):
    if mode == 'nn':
        (M, K), (K2, N) = a.shape, b.shape
    elif mode == 'nt':
        (M, K), (N, K2) = a.shape, b.shape
    else:
        (K, M), (K2, N) = a.shape, b.shape
    assert K == K2, (name, a.shape, b.shape)
    tm = _pick(M, (1024, 768, 512, 256, 128))
    tn = _pick(N, (512, 384, 256))
    dims = {'nn': ((1,), (0,)), 'nt': ((1,), (1,)), 'tn': ((0,), (0,))}[mode]

    def body(a_ref, b_ref, o_ref):
        o_ref[...] = lax.dot_general(a_ref[...].astype(BF16), b_ref[...].astype(BF16), (dims, ((), ())),
                                     preferred_element_type=F32).astype(out_dtype)

    a_spec = pl.BlockSpec((K, tm), lambda i, j: (0, i)) if mode == 'tn' else pl.BlockSpec((tm, K), lambda i, j: (i, 0))
    b_spec = pl.BlockSpec((tn, K), lambda i, j: (j, 0)) if mode == 'nt' else pl.BlockSpec((K, tn), lambda i, j: (0, j))
    res = carried(body, comm, grid=(M // tm, N // tn), in_specs=[a_spec, b_spec],
                  out_specs=pl.BlockSpec((tm, tn), lambda i, j: (i, j)), out_shape=jax.ShapeDtypeStruct((M, N), out_dtype),
                  semantics=("parallel", "parallel"), name=name)(a, b)
    return res[0] if comm is None else res


def rw(arr, width=None, cb=0):
    return (arr, arr.shape[1] if width is None else width, cb)


def rowwise(fn, rows, consts, outs, accs, tl, name, n_steps=None, comm=None):
    if n_steps is None:
        n_steps = [r[0].shape[0] for r in rows if not isinstance(r[1], pl.BlockSpec)][0] // tl
    L = n_steps * tl
    nr, nc, no, na = len(rows), len(consts), len(outs), len(accs)
    in_specs, args = [], []
    for r in rows:
        if isinstance(r[1], pl.BlockSpec):
            in_specs.append(r[1])
        else:
            in_specs.append(pl.BlockSpec((tl, r[1]), functools.partial(lambda i, cb: (i, cb), cb=r[2])))
        args.append(r[0])
    for c in consts:
        in_specs.append(pl.BlockSpec(c.shape, functools.partial(lambda i, nd: (0,) * nd, nd=c.ndim)))
        args.append(c)
    out_specs = [pl.BlockSpec((tl, w), lambda i: (i, 0)) for w, _ in outs]
    out_shape = [jax.ShapeDtypeStruct((L, w), dt) for w, dt in outs]
    for s in accs:
        out_specs.append(pl.BlockSpec(s, functools.partial(lambda i, nd: (0,) * nd, nd=len(s))))
        out_shape.append(jax.ShapeDtypeStruct(s, F32))

    def body(*refs):
        ins = [r[...] for r in refs[:nr + nc]]
        o_refs = refs[nr + nc:nr + nc + no]
        a_refs = refs[nr + nc + no:]
        o_vals, a_vals = fn(*ins)
        for ref, val in zip(o_refs, o_vals):
            ref[...] = val.astype(ref.dtype)
        if na:
            @pl.when(pl.program_id(0) == 0)
            def _():
                for ref in a_refs:
                    ref[...] = jnp.zeros_like(ref)
            for ref, val in zip(a_refs, a_vals):
                ref[...] += val

    res, carried_out = carried(body, comm, grid=(n_steps,), in_specs=in_specs, out_specs=out_specs, out_shape=out_shape,
                               name=name, semantics=("arbitrary",))(*args)
    if comm is None:
        return res[:no], res[no:]
    return res[:no], res[no:], carried_out


def carried(body, comm, *, grid, in_specs, out_specs, out_shape, name, semantics, scratch_shapes=()):
    single = not isinstance(out_shape, (list, tuple))
    o_specs = [out_specs] if single else list(out_specs)
    o_shape = [out_shape] if single else list(out_shape)
    if comm is None:
        call = pl.pallas_call(body, grid=grid, in_specs=in_specs, out_specs=out_specs, out_shape=out_shape,
                              scratch_shapes=list(scratch_shapes),
                              compiler_params=pltpu.CompilerParams(dimension_semantics=semantics), name=name)
        return lambda *args: (call(*args), None)
    n_in, n_out, n_sc = len(in_specs), len(o_specs), len(scratch_shapes)
    ci, co = len(comm.ins), len(comm.outs)
    n_steps = int(np.prod(grid))
    hooks = comm.hooks(n_steps)

    def wrapped(*refs):
        ins, cins = refs[:n_in], refs[n_in:n_in + ci]
        outs, couts = refs[n_in + ci:n_in + ci + n_out], refs[n_in + ci + n_out:n_in + ci + n_out + co]
        sc, csc = refs[n_in + ci + n_out + co:n_in + ci + n_out + co + n_sc], refs[n_in + ci + n_out + co + n_sc:]
        step = pl.program_id(0)
        for ax in range(1, len(grid)):
            step = step * grid[ax] + pl.program_id(ax)
        for at, fn, after in hooks:
            if not after:
                pl.when(step == at)(functools.partial(fn, cins, couts, csc))
        body(*ins, *outs, *sc)
        for at, fn, after in hooks:
            if after:
                pl.when(step == at)(functools.partial(fn, cins, couts, csc))

    call = pl.pallas_call(wrapped, grid=grid, in_specs=list(in_specs) + [ANY] * ci, out_specs=o_specs + [ANY] * co,
                          out_shape=o_shape + list(comm.outs), scratch_shapes=list(scratch_shapes) + list(comm.scratch),
                          compiler_params=pltpu.CompilerParams(dimension_semantics=("arbitrary",) * len(grid)), name=name)

    def run(*args):
        res = call(*args, *comm.ins)
        return (res[0] if single else res[:n_out]), res[n_out:]
    return run


_K0 = math.sqrt(2.0 / math.pi)
_K1 = 0.044715


def gelu(x):
    return x * (0.5 * (1.0 + jnp.tanh(_K0 * (x + _K1 * (x * x * x)))))


def gelu_grad(x):
    t = jnp.tanh(_K0 * (x + _K1 * (x * x * x)))
    return 0.5 * (1.0 + t) + 0.5 * x * (1.0 - t * t) * (_K0 * (1.0 + 3.0 * _K1 * x * x))


def sigmoid(x):
    return 1.0 / (1.0 + jnp.exp(-x))


def silu(z):
    return z * sigmoid(z)


def silu_grad(z):
    s = sigmoid(z)
    return s * (1.0 + z * (1.0 - s))


def rms_fwd(x, g):
    r = lax.rsqrt(jnp.mean(x * x, axis=-1, keepdims=True) + EPS)
    return x * r * g


def rms_bwd(x, g, dy):
    r = lax.rsqrt(jnp.mean(x * x, axis=-1, keepdims=True) + EPS)
    xh = x * r
    dg = jnp.sum(dy * xh, axis=0, keepdims=True)
    dxh = dy * g
    dx = r * (dxh - xh * jnp.mean(dxh * xh, axis=-1, keepdims=True))
    return dx, dg


def _scan_chunk(a_r, a_i, pr_ref, pi_ref, cr, ci, T, reverse):
    ts = min(SSM_TS, T)
    sgn = -1.0 if reverse else 1.0
    row = lax.broadcasted_iota(jnp.int32, (ts, a_r.shape[1]), 0)
    pw = (lambda e: T - e) if reverse else (lambda e: e - 1)
    if reverse:
        wr_c, wi_c = pr_ref[T - ts:T, :], sgn * pi_ref[T - ts:T, :]
    else:
        wr_c, wi_c = pr_ref[0:ts, :], sgn * pi_ref[0:ts, :]
    c_r, c_i = cr[...], ci[...]
    outs = []
    subs = range(T // ts)
    for sub in (reversed(subs) if reverse else subs):
        v_r, v_i = a_r[sub * ts:(sub + 1) * ts], a_i[sub * ts:(sub + 1) * ts]
        d = 1
        while d < ts:
            wr = pr_ref[pw(d):pw(d) + 1, :]
            wi = sgn * pi_ref[pw(d):pw(d) + 1, :]
            if reverse:
                yr, yi, keep = pltpu.roll(v_r, ts - d, 0), pltpu.roll(v_i, ts - d, 0), row < ts - d
            else:
                yr, yi, keep = pltpu.roll(v_r, d, 0), pltpu.roll(v_i, d, 0), row >= d
            v_r, v_i = (v_r + jnp.where(keep, wr * yr - wi * yi, 0.0), v_i + jnp.where(keep, wr * yi + wi * yr, 0.0))
            d *= 2
        v_r, v_i = v_r + (wr_c * c_r - wi_c * c_i), v_i + (wr_c * c_i + wi_c * c_r)
        k = 0 if reverse else ts - 1
        c_r, c_i = v_r[k:k + 1, :], v_i[k:k + 1, :]
        outs.append((v_r, v_i))
    if reverse:
        outs = outs[::-1]
    cr[...] = c_r
    ci[...] = c_i
    return jnp.concatenate([o[0] for o in outs], axis=0), jnp.concatenate([o[1] for o in outs], axis=0)


_NT = (((1,), (1,)), ((), ()))
_TN = (((0,), (0,)), ((), ()))


def s5_fwd(proj, d_skip, Bre, Bim, Cre, Cim, pr, pi, comm=None):
    L = proj.shape[0]
    T, WC = min(SSM_T, L), SSM_WC
    nT = L // T

    def body(u_ref, d_ref, bre_ref, bim_ref, cre_ref, cim_ref, pr_ref, pi_ref, y_ref, yg_ref, sr_ref, si_ref, cr, ci):
        @pl.when(pl.program_id(1) == 0)
        def _():
            cr[...] = jnp.zeros_like(cr)
            ci[...] = jnp.zeros_like(ci)

        u = u_ref[...]
        ub = u.astype(BF16)
        a_r = lax.dot_general(ub, bre_ref[0].astype(BF16), _NT, preferred_element_type=F32)
        a_i = lax.dot_general(ub, bim_ref[0].astype(BF16), _NT, preferred_element_type=F32)
        a_r, a_i = _scan_chunk(a_r, a_i, pr_ref, pi_ref, cr, ci, T, False)
        sr_ref[...] = a_r
        si_ref[...] = a_i
        y = (lax.dot_general(a_r.astype(BF16), cre_ref[0].astype(BF16), _NT, preferred_element_type=F32)
             + lax.dot_general(a_i.astype(BF16), cim_ref[0].astype(BF16), _NT, preferred_element_type=F32)
             + d_ref[...] * u)
        y_ref[...] = y
        yg_ref[...] = gelu(y)

    uspec = pl.BlockSpec((T, 128), lambda k, i: (i, k))
    sspec = pl.BlockSpec((T, WC), lambda k, i: (i, k))
    return carried(
        body, comm, grid=(8, nT),
        in_specs=[uspec, pl.BlockSpec((1, 128), lambda k, i: (0, k)),
                  pl.BlockSpec((1, WC, 128), lambda k, i: (k, 0, 0)), pl.BlockSpec((1, WC, 128), lambda k, i: (k, 0, 0)),
                  pl.BlockSpec((1, 128, WC), lambda k, i: (k, 0, 0)), pl.BlockSpec((1, 128, WC), lambda k, i: (k, 0, 0)),
                  pl.BlockSpec((T, WC), lambda k, i: (0, k)), pl.BlockSpec((T, WC), lambda k, i: (0, k))],
        out_specs=[uspec, uspec, sspec, sspec],
        out_shape=[jax.ShapeDtypeStruct((L, 1024), F32)] * 2 + [jax.ShapeDtypeStruct((L, 8 * WC), F32)] * 2,
        scratch_shapes=[pltpu.VMEM((1, WC), F32), pltpu.VMEM((1, WC), F32)],
        semantics=("parallel", "arbitrary"), name='a_ssm')(proj, d_skip, Bre, Bim, Cre, Cim, pr, pi)


def s5_bwd(proj, dyg1, dyg2, y, d_skip, s_re, s_im, Bre, Bim, Cre, Cim, prr, pir, comm=None):
    L = proj.shape[0]
    T, WC = min(SSM_T, L), SSM_WC
    nT = L // T

    def body(u_ref, g1_ref, g2_ref, y_ref, d_ref, sr_ref, si_ref, spr_ref, spi_ref, bre_ref, bim_ref, cre_ref, cim_ref,
             pr_ref, pi_ref, du_ref, dd_ref, dbre_ref, dbim_ref, dcre_ref, dcim_ref, dar_ref, dai_ref, cr, ci):
        i = pl.program_id(1)

        @pl.when(i == 0)
        def _():
            for ref in (cr, ci, dd_ref, dbre_ref, dbim_ref, dcre_ref, dcim_ref, dar_ref, dai_ref):
                ref[...] = jnp.zeros_like(ref)

        u = u_ref[...]
        dy = (g1_ref[...] + g2_ref[...]) * gelu_grad(y_ref[...])
        dd_ref[...] += jnp.sum(dy * u, axis=0, keepdims=True)
        dyb, ub = dy.astype(BF16), u.astype(BF16)
        bre, bim, cre, cim = (r[0].astype(BF16) for r in (bre_ref, bim_ref, cre_ref, cim_ref))
        g_r = jnp.dot(dyb, cre, preferred_element_type=F32)
        g_i = jnp.dot(dyb, cim, preferred_element_type=F32)
        g_r, g_i = _scan_chunk(g_r, g_i, pr_ref, pi_ref, cr, ci, T, True)
        s_r, s_i = sr_ref[...], si_ref[...]
        row = lax.broadcasted_iota(jnp.int32, (T, WC), 0)
        first = (nT - 1 - i) == 0
        sp_r = jnp.where(row == 0, jnp.where(first, 0.0, spr_ref[7:8, :]), pltpu.roll(s_r, 1, 0))
        sp_i = jnp.where(row == 0, jnp.where(first, 0.0, spi_ref[7:8, :]), pltpu.roll(s_i, 1, 0))
        dar_ref[...] += jnp.sum(g_r * sp_r + g_i * sp_i, axis=0, keepdims=True)
        dai_ref[...] += jnp.sum(g_i * sp_r - g_r * sp_i, axis=0, keepdims=True)
        grb, gib = g_r.astype(BF16), g_i.astype(BF16)
        dcre_ref[0] += lax.dot_general(dyb, s_r.astype(BF16), _TN, preferred_element_type=F32)
        dcim_ref[0] += lax.dot_general(dyb, s_i.astype(BF16), _TN, preferred_element_type=F32)
        dbre_ref[0] += lax.dot_general(grb, ub, _TN, preferred_element_type=F32)
        dbim_ref[0] += lax.dot_general(gib, ub, _TN, preferred_element_type=F32)
        du_ref[...] = (dy * d_ref[...] + jnp.dot(grb, bre, preferred_element_type=F32)
                       + jnp.dot(gib, bim, preferred_element_type=F32))

    uspec = pl.BlockSpec((T, 128), lambda k, i: (nT - 1 - i, k))
    sspec = pl.BlockSpec((T, WC), lambda k, i: (nT - 1 - i, k))
    pspec = pl.BlockSpec((8, WC), lambda k, i: (jnp.maximum((nT - 1 - i) * (T // 8) - 1, 0), k))
    tab = pl.BlockSpec((T, WC), lambda k, i: (0, k))
    bspec = pl.BlockSpec((1, WC, 128), lambda k, i: (k, 0, 0))
    cspec = pl.BlockSpec((1, 128, WC), lambda k, i: (k, 0, 0))
    return carried(
        body, comm, grid=(8, nT),
        in_specs=[uspec, uspec, uspec, uspec, pl.BlockSpec((1, 128), lambda k, i: (0, k)), sspec, sspec, pspec, pspec,
                  bspec, bspec, cspec, cspec, tab, tab],
        out_specs=[uspec, pl.BlockSpec((1, 128), lambda k, i: (0, k)), bspec, bspec, cspec, cspec,
                   pl.BlockSpec((1, WC), lambda k, i: (0, k)), pl.BlockSpec((1, WC), lambda k, i: (0, k))],
        out_shape=[jax.ShapeDtypeStruct((L, 1024), F32), jax.ShapeDtypeStruct((1, 1024), F32),
                   jax.ShapeDtypeStruct((8, WC, 128), F32), jax.ShapeDtypeStruct((8, WC, 128), F32),
                   jax.ShapeDtypeStruct((8, 128, WC), F32), jax.ShapeDtypeStruct((8, 128, WC), F32),
                   jax.ShapeDtypeStruct((1, 8 * WC), F32), jax.ShapeDtypeStruct((1, 8 * WC), F32)],
        scratch_shapes=[pltpu.VMEM((1, WC), F32), pltpu.VMEM((1, WC), F32)],
        semantics=("parallel", "arbitrary"), name='a_ssm_bwd')(
            proj, dyg1, dyg2, y, d_skip, s_re, s_im, s_re, s_im, Bre, Bim, Cre, Cim, prr, pir)


def s5_discretize(lam_re, lam_im, log_dt, b_re, b_im):
    dt = jnp.exp(log_dt)[:, None]
    mag = jnp.exp(lam_re * dt)
    ab_re = mag * jnp.cos(lam_im * dt)
    ab_im = mag * jnp.sin(lam_im * dt)
    den = lam_re * lam_re + lam_im * lam_im
    nr = ab_re - 1.0
    f_re = (nr * lam_re + ab_im * lam_im) / den
    f_im = (ab_im * lam_re - nr * lam_im) / den
    bb_re = f_re[..., None] * b_re - f_im[..., None] * b_im
    bb_im = f_re[..., None] * b_im + f_im[..., None] * b_re
    return ab_re, ab_im, bb_re, bb_im


def s5_prep(bb_re, bb_im, c_re, c_im, ar, ai, T, comm=None):
    W = ar.shape[1]

    def body(bbr_ref, bbi_ref, cre_ref, cim_ref, ar_ref, ai_ref, btr_ref, bti_ref, ctr_ref, cti_ref, fr_ref, fi_ref,
             rr_ref, ri_ref):
        for ref in (btr_ref, bti_ref, ctr_ref, cti_ref):
            ref[...] = jnp.zeros_like(ref)
        for g in range(8):
            rows, cols = slice(g * SSM_P, (g + 1) * SSM_P), slice(g * SSM_H, (g + 1) * SSM_H)
            btr_ref[0, rows, cols] = bbr_ref[g]
            bti_ref[0, rows, cols] = bbi_ref[g]
            ctr_ref[0, cols, rows] = cre_ref[g]
            cti_ref[0, cols, rows] = -cim_ref[g]
        fr_ref[0:1, :] = ar_ref[...]
        fi_ref[0:1, :] = ai_ref[...]
        rr_ref[T - 1:T, :] = ar_ref[...]
        ri_ref[T - 1:T, :] = ai_ref[...]
        n = 1
        while n < T:
            cr, ci = fr_ref[0:n, :], fi_ref[0:n, :]
            lr, li = fr_ref[n - 1:n, :], fi_ref[n - 1:n, :]
            fr_ref[n:2 * n, :] = cr * lr - ci * li
            fi_ref[n:2 * n, :] = cr * li + ci * lr
            cr, ci = rr_ref[T - n:T, :], ri_ref[T - n:T, :]
            rr_ref[T - 2 * n:T - n, :] = cr * lr - ci * li
            ri_ref[T - 2 * n:T - n, :] = cr * li + ci * lr
            n *= 2

    spec = pl.BlockSpec((T, SSM_WC), lambda j: (0, j))
    aspec = pl.BlockSpec((1, SSM_WC), lambda j: (0, j))
    bspec, cspec = pl.BlockSpec((8, SSM_P, SSM_H), lambda j: (j, 0, 0)), pl.BlockSpec((8, SSM_H, SSM_P), lambda j: (j, 0, 0))
    btspec = pl.BlockSpec((1, SSM_WC, 128), lambda j: (j, 0, 0))
    ctspec = pl.BlockSpec((1, 128, SSM_WC), lambda j: (j, 0, 0))
    return carried(
        body, comm, grid=(W // SSM_WC,), in_specs=[bspec, bspec, cspec, cspec, aspec, aspec],
        out_specs=[btspec, btspec, ctspec, ctspec] + [spec] * 4,
        out_shape=[jax.ShapeDtypeStruct((8, SSM_WC, 128), F32)] * 2 + [jax.ShapeDtypeStruct((8, 128, SSM_WC), F32)] * 2
        + [jax.ShapeDtypeStruct((T, W), F32)] * 4,
        semantics=("parallel",), name='a_prep')(bb_re, bb_im, c_re, c_im, ar, ai)


def s5_untile(dbtr, dbti, dctr, dcti):
    def body(dbtr_ref, dbti_ref, dctr_ref, dcti_ref, br_ref, bi_ref, cr_ref, ci_ref):
        for g in range(8):
            rows, cols = slice(g * SSM_P, (g + 1) * SSM_P), slice(g * SSM_H, (g + 1) * SSM_H)
            br_ref[g] = dbtr_ref[0, rows, cols]
            bi_ref[g] = dbti_ref[0, rows, cols]
            cr_ref[g] = dctr_ref[0, cols, rows]
            ci_ref[g] = -dcti_ref[0, cols, rows]

    bspec, cspec = pl.BlockSpec((8, SSM_P, SSM_H), lambda j: (j, 0, 0)), pl.BlockSpec((8, SSM_H, SSM_P), lambda j: (j, 0, 0))
    btspec = pl.BlockSpec((1, SSM_WC, 128), lambda j: (j, 0, 0))
    ctspec = pl.BlockSpec((1, 128, SSM_WC), lambda j: (j, 0, 0))
    return pl.pallas_call(
        body, grid=(8,), in_specs=[btspec, btspec, ctspec, ctspec], out_specs=[bspec, bspec, cspec, cspec],
        out_shape=[jax.ShapeDtypeStruct((SSM_G, SSM_P, SSM_H), F32)] * 2 + [jax.ShapeDtypeStruct((SSM_G, SSM_H, SSM_P), F32)] * 2,
        compiler_params=pltpu.CompilerParams(dimension_semantics=("parallel",)), name='a_untile')(dbtr, dbti, dctr, dcti)


def layer_a_fwd(h, w, p, comm=None, on_carried=None, prep_comm=None, on_prep=None):
    L = h.shape[0]
    disc = lambda *a: s5_discretize(*a)
    (ab_re, ab_im, bb_re, bb_im), disc_vjp = jax.vjp(disc, p['a_lam_re'][0], p['a_lam_im'][0], p['a_log_dt'][0],
                                                     p['a_b_re'][0], p['a_b_im'][0])
    T = min(SSM_T, L)
    (Bre, Bim, Cre, Cim, pr, pi, prr, pir), prepped = s5_prep(bb_re, bb_im, p['a_c_re'][0], p['a_c_im'][0],
                                                              ab_re.reshape(1, -1), ab_im.reshape(1, -1), T,
                                                              comm=prep_comm)
    if on_prep is not None:
        on_prep(prepped)
    proj = mm(h, w['a_w_in'], 'nn', 'a_proj')
    (y, yg, s_re, s_im), carried_out = s5_fwd(proj, p['a_d'], Bre, Bim, Cre, Cim, pr, pi, comm=comm)
    if on_carried is not None:
        on_carried(carried_out)
    gl = mm(yg, w['a_w_glu'], 'nn', 'a_glu')

    def f2(yg_, gl_, z, bg):
        return [yg_ * sigmoid(gl_ + bg) * silu(z)], []
    (po,), _ = rowwise(f2, [rw(yg), rw(gl), rw(proj, 1024, 1)], [p['a_b_glu']], [(1024, BF16)], [], 256, 'a_gate')
    yb = mm(po, w['a_w_out'], 'nn', 'a_out')
    saved = dict(carried=carried_out, h=h, proj=proj, disc_vjp=disc_vjp, Bre=Bre, Bim=Bim, Cre=Cre, Cim=Cim, prr=prr, pir=pir, s_re=s_re,
                 s_im=s_im, y=y, yg=yg, gl=gl, po=po)
    return yb, saved


def _dw(g, sink, name, a, b, mm_name):
    if sink is None:
        g[name] = mm(a, b, 'tn', mm_name)
    else:
        sink.put(name, a, b, mm_name)


def layer_a_bwd(dyb, w, p, sv, comm=None, sink=None):
    g = {}
    dpo = mm(dyb, w['a_w_out'], 'nt', 'a_dpo')
    _dw(g, sink, 'a_w_out', sv['po'], dyb, 'a_dwout')
    proj = sv['proj']

    def f1(dpo_, yg, gl, z, bg):
        sg = sigmoid(gl + bg)
        sz = silu(z)
        dm = dpo_ * sz
        dz = dpo_ * (yg * sg) * silu_grad(z)
        dgl = dm * yg * sg * (1.0 - sg)
        return [dz, dm * sg, dgl], [jnp.sum(dgl, axis=0, keepdims=True)]
    (dz, dyg1, dgl), (db_glu,) = rowwise(f1, [rw(dpo), rw(sv['yg']), rw(sv['gl']), rw(proj, 1024, 1)], [p['a_b_glu']],
                                          [(1024, F32), (1024, F32), (1024, BF16)], [(1, 1024)], 256, 'a_gate_bwd')
    g['a_b_glu'] = db_glu
    _dw(g, sink, 'a_w_glu', sv['yg'], dgl, 'a_dwglu')
    dyg2 = mm(dgl, w['a_w_glu'], 'nt', 'a_dyg2')

    if callable(comm):
        comm = comm()
    (du, dd, dBre, dBim, dCre, dCim, da_re, da_im), g['carried'] = s5_bwd(
        proj, dyg1, dyg2, sv['y'], p['a_d'], sv['s_re'], sv['s_im'], sv['Bre'], sv['Bim'], sv['Cre'], sv['Cim'],
        sv['prr'], sv['pir'], comm=comm)
    g['a_d'] = dd

    def f3(du_, dz_):
        return [jnp.concatenate([du_, dz_], axis=1)], []
    (dproj,), _ = rowwise(f3, [rw(du), rw(dz)], [], [(2048, BF16)], [], 256, 'a_dproj')
    dbb_re, dbb_im, dc_re, dc_im = s5_untile(dBre, dBim, dCre, dCim)
    dlr, dli, dldt, dbr, dbi = sv['disc_vjp']((da_re.reshape(SSM_G, SSM_P), da_im.reshape(SSM_G, SSM_P), dbb_re, dbb_im))
    g['a_lam_re'], g['a_lam_im'], g['a_log_dt'] = dlr[None], dli[None], dldt[None]
    g['a_b_re'], g['a_b_im'] = dbr[None], dbi[None]
    g['a_c_re'], g['a_c_im'] = dc_re[None], dc_im[None]
    _dw(g, sink, 'a_w_in', sv['h'], dproj, 'a_dwin')
    if sink is None:
        dh = mm(dproj, w['a_w_in'], 'nt', 'a_dh')
    else:
        dh, (g['land_a1'],) = mm(dproj, w['a_w_in'], 'nt', 'a_dh', comm=SiblingExchange(sink.bufs['a1']))
    return dh, g


def _t5_bucket_np():
    qi = np.arange(WINDOW)[:, None]
    kj = np.arange(2 * WINDOW)[None, :]
    dist = np.maximum(qi + WINDOW - kj, 0)
    max_exact = REL_BUCKETS // 2
    dist_f = np.maximum(dist, 1).astype(np.float32)
    large = max_exact + (np.log(dist_f / np.float32(max_exact)) / np.float32(math.log(REL_MAX_DIST / max_exact))
                         * np.float32(REL_BUCKETS - max_exact)).astype(np.int32)
    large = np.minimum(large, REL_BUCKETS - 1)
    return np.where(dist < max_exact, dist, large).astype(np.int32)


SWA_GRP = SWA_HEADS // SWA_KV


def _swa_kv(kvp, kvc, kvh):
    kb = jnp.concatenate([kvp[:, kvh * 64:(kvh + 1) * 64], kvc[:, kvh * 64:(kvh + 1) * 64]], 0).astype(BF16)
    vb = jnp.concatenate([kvp[:, 128 + kvh * 64:128 + (kvh + 1) * 64], kvc[:, 128 + kvh * 64:128 + (kvh + 1) * 64]],
                         0).astype(BF16)
    return kb, vb


def _swa_stack(x, kvh):
    return jnp.concatenate([x[:, (kvh * SWA_GRP + g) * 64:(kvh * SWA_GRP + g + 1) * 64] for g in range(SWA_GRP)],
                           axis=0).astype(BF16)


def _swa_group(bias_ref, kvh):
    return bias_ref[kvh * SWA_GRP:(kvh + 1) * SWA_GRP].reshape(SWA_GRP * WINDOW, 2 * WINDOW)


def _swa_sinks(sink_ref, kvh):
    return jnp.concatenate([jnp.broadcast_to(sink_ref[0:1, kvh * SWA_GRP + g:kvh * SWA_GRP + g + 1], (WINDOW, 1))
                            for g in range(SWA_GRP)], axis=0)


def _swa_probs(q, kb, bias_h, sink, valid):
    s = lax.dot_general(q, kb, (((1,), (1,)), ((), ())), preferred_element_type=F32) * (HEAD_DIM ** -0.5)
    s = jnp.where(valid, s + bias_h, NEG_INF)
    m = jnp.maximum(jnp.max(s, axis=-1, keepdims=True), sink)
    e = jnp.exp(s - m)
    es = jnp.exp(sink - m)
    den = jnp.sum(e, axis=-1, keepdims=True) + es
    return e / den, es / den


def _swa_valid(n):
    qi = lax.broadcasted_iota(jnp.int32, (SWA_GRP * WINDOW, 2 * WINDOW), 0) & (WINDOW - 1)
    kj = lax.broadcasted_iota(jnp.int32, (SWA_GRP * WINDOW, 2 * WINDOW), 1)
    dist = qi + WINDOW - kj
    return (dist >= 0) & (dist < WINDOW) & ((kj >= WINDOW) | (n > 0))


def swa_fwd(proj, bias, sinks, comm=None):
    L = proj.shape[0]

    def body(z_ref, q_ref, kvc_ref, kvp_ref, bias_ref, sink_ref, o_ref, po_ref):
        n = pl.program_id(0)
        valid = _swa_valid(n)
        q, kvc, kvp = q_ref[...], kvc_ref[...], kvp_ref[...]
        outs = []
        for kvh in range(SWA_KV):
            kb, vb = _swa_kv(kvp, kvc, kvh)
            p, _ = _swa_probs(_swa_stack(q, kvh), kb, _swa_group(bias_ref, kvh), _swa_sinks(sink_ref, kvh), valid)
            o8 = jnp.dot(p.astype(BF16), vb, preferred_element_type=F32)
            outs += [o8[g * WINDOW:(g + 1) * WINDOW] for g in range(SWA_GRP)]
        o = jnp.concatenate(outs, axis=1)
        o_ref[...] = o
        po_ref[...] = (o * silu(z_ref[...])).astype(po_ref.dtype)

    return carried(
        body, comm, grid=(L // WINDOW,),
        in_specs=[pl.BlockSpec((WINDOW, 1024), lambda n: (n, 0)), pl.BlockSpec((WINDOW, 1024), lambda n: (n, 1)),
                  pl.BlockSpec((WINDOW, 256), lambda n: (n, 8)),
                  pl.BlockSpec((WINDOW, 256), lambda n: (jnp.maximum(n - 1, 0), 8)),
                  pl.BlockSpec((SWA_HEADS, WINDOW, 2 * WINDOW), lambda n: (0, 0, 0)),
                  pl.BlockSpec((1, SWA_HEADS), lambda n: (0, 0))],
        out_specs=[pl.BlockSpec((WINDOW, 1024), lambda n: (n, 0))] * 2,
        out_shape=[jax.ShapeDtypeStruct((L, 1024), F32), jax.ShapeDtypeStruct((L, 1024), BF16)],
        semantics=("parallel",), name='b_attn')(proj, proj, proj, proj, bias, sinks)


def swa_bwd(proj, do, bias, sinks, comm=None):
    L = proj.shape[0]

    def body(q_ref, kvc_ref, kvp_ref, do_ref, bias_ref, sink_ref, dq_ref, dkv_ref, dbias_ref, dsink_ref):
        n = pl.program_id(0)

        @pl.when(n == 0)
        def _():
            dkv_ref[...] = jnp.zeros_like(dkv_ref)
            dbias_ref[...] = jnp.zeros_like(dbias_ref)
            dsink_ref[...] = jnp.zeros_like(dsink_ref)

        valid = _swa_valid(n)
        q, kvc, kvp, do_ = q_ref[...], kvc_ref[...], kvp_ref[...], do_ref[...]
        dqs, dks, dvs, dsk = [], [], [], []
        for kvh in range(SWA_KV):
            kb, vb = _swa_kv(kvp, kvc, kvh)
            q8, do8 = _swa_stack(q, kvh), _swa_stack(do_, kvh)
            p, ps = _swa_probs(q8, kb, _swa_group(bias_ref, kvh), _swa_sinks(sink_ref, kvh), valid)
            dp = lax.dot_general(do8, vb, (((1,), (1,)), ((), ())), preferred_element_type=F32)
            delta = jnp.sum(p * dp, axis=-1, keepdims=True)
            ds = p * (dp - delta)
            col = -ps * delta
            dsk += [jnp.sum(col[g * WINDOW:(g + 1) * WINDOW], axis=0, keepdims=True) for g in range(SWA_GRP)]
            dbias_ref[kvh * SWA_GRP:(kvh + 1) * SWA_GRP] += ds.reshape(SWA_GRP, WINDOW, 2 * WINDOW)
            dsb = (ds * (HEAD_DIM ** -0.5)).astype(BF16)
            dq8 = jnp.dot(dsb, kb, preferred_element_type=F32)
            dqs += [dq8[g * WINDOW:(g + 1) * WINDOW] for g in range(SWA_GRP)]
            dks.append(lax.dot_general(dsb, q8, (((0,), (0,)), ((), ())), preferred_element_type=F32))
            dvs.append(lax.dot_general(p.astype(BF16), do8, (((0,), (0,)), ((), ())), preferred_element_type=F32))
        dq_ref[...] = jnp.concatenate(dqs, axis=1)
        dsink_ref[...] += jnp.concatenate(dsk, axis=1)
        both = jnp.concatenate(dks + dvs, axis=1)
        r_cur = pl.multiple_of(n * WINDOW, WINDOW)
        r_prev = pl.multiple_of(jnp.maximum(n - 1, 0) * WINDOW, WINDOW)
        dkv_ref[pl.ds(r_prev, WINDOW), :] += both[:WINDOW]
        dkv_ref[pl.ds(r_cur, WINDOW), :] += both[WINDOW:]

    return carried(
        body, comm, grid=(L // WINDOW,),
        in_specs=[pl.BlockSpec((WINDOW, 1024), lambda n: (n, 1)), pl.BlockSpec((WINDOW, 256), lambda n: (n, 8)),
                  pl.BlockSpec((WINDOW, 256), lambda n: (jnp.maximum(n - 1, 0), 8)),
                  pl.BlockSpec((WINDOW, 1024), lambda n: (n, 0)),
                  pl.BlockSpec((SWA_HEADS, WINDOW, 2 * WINDOW), lambda n: (0, 0, 0)),
                  pl.BlockSpec((1, SWA_HEADS), lambda n: (0, 0))],
        out_specs=[pl.BlockSpec((WINDOW, 1024), lambda n: (n, 0)), pl.BlockSpec((L, 256), lambda n: (0, 0)),
                   pl.BlockSpec((SWA_HEADS, WINDOW, 2 * WINDOW), lambda n: (0, 0, 0)),
                   pl.BlockSpec((1, SWA_HEADS), lambda n: (0, 0))],
        out_shape=[jax.ShapeDtypeStruct((L, 1024), F32), jax.ShapeDtypeStruct((L, 256), F32),
                   jax.ShapeDtypeStruct((SWA_HEADS, WINDOW, 2 * WINDOW), F32), jax.ShapeDtypeStruct((1, SWA_HEADS), F32)],
        semantics=("arbitrary",), name='b_attn_bwd')(proj, proj, proj, do, bias, sinks)


def swa_bias(rel_bias):
    def body(bk_ref, rb_ref, o_ref):
        bk = bk_ref[...]
        for h in range(SWA_HEADS):
            acc = jnp.zeros((WINDOW, 2 * WINDOW), F32)
            for b in range(REL_BUCKETS):
                acc = jnp.where(bk == b, rb_ref[b, h], acc)
            o_ref[h] = acc

    return pl.pallas_call(
        body, out_shape=jax.ShapeDtypeStruct((SWA_HEADS, WINDOW, 2 * WINDOW), F32),
        in_specs=[pl.BlockSpec(memory_space=pltpu.VMEM), pl.BlockSpec(memory_space=pltpu.SMEM)],
        out_specs=pl.BlockSpec(memory_space=pltpu.VMEM), name='b_bias')(jnp.asarray(_t5_bucket_np()), rel_bias)


def layer_b_fwd(h, w, p, comm=None):
    proj = mm(h, w['b_w_in'], 'nn', 'b_proj')
    bias = swa_bias(p['rel_bias'])
    (o, po), carried_out = swa_fwd(proj, bias, p['b_sinks'], comm=comm)
    yb = mm(po, w['b_w_out'], 'nn', 'b_out')
    return yb, dict(carried=carried_out, h=h, proj=proj, bias=bias, o=o, po=po)


def layer_b_bwd(dyb, w, p, sv, comm=None, sink=None):
    g = {}
    dpo = mm(dyb, w['b_w_out'], 'nt', 'b_dpo')
    _dw(g, sink, 'b_w_out', sv['po'], dyb, 'b_dwout')
    proj = sv['proj']

    def f1(dpo_, o, z):
        return [dpo_ * silu(z), dpo_ * o * silu_grad(z)], []
    (do, dz), _ = rowwise(f1, [rw(dpo), rw(sv['o']), rw(proj, 1024, 0)], [], [(1024, BF16), (1024, F32)], [], 256, 'b_gate_bwd')
    (dq, dkv, dbias, dsinks), g['carried'] = swa_bwd(proj, do, sv['bias'], p['b_sinks'], comm=comm)
    g['b_sinks'] = dsinks
    onehot = jnp.asarray(np.eye(REL_BUCKETS, dtype=np.float32)[_t5_bucket_np().reshape(-1)])

    def f2(db, oh):
        return [], [lax.dot_general(db, oh, (((1,), (0,)), ((), ())), preferred_element_type=F32,
                                    precision=lax.Precision.HIGHEST)]
    _, (drel,) = rowwise(f2, [(dbias.reshape(SWA_HEADS, -1), pl.BlockSpec((SWA_HEADS, 4096), lambda i: (0, i))),
                              (onehot, pl.BlockSpec((4096, REL_BUCKETS), lambda i: (i, 0)))], [], [],
                         [(SWA_HEADS, REL_BUCKETS)], 4096, 'b_drel', n_steps=(2 * WINDOW * WINDOW) // 4096)
    g['rel_bias'] = drel.T

    def f3(dz_, dq_, dkv_):
        return [jnp.concatenate([dz_, dq_, dkv_], axis=1)], []
    (dproj,), _ = rowwise(f3, [rw(dz), rw(dq), rw(dkv)], [], [(2304, BF16)], [], 256, 'b_dproj')
    _dw(g, sink, 'b_w_in', sv['h'], dproj, 'b_dwin')
    dh = mm(dproj, w['b_w_in'], 'nt', 'b_dh')
    return dh, g


MLA_SCALE = (MLA_NOPE + MLA_ROPE) ** -0.5


def _rope_tables(L):
    inv = ROPE_BASE ** (-jnp.arange(0, MLA_ROPE, 2, dtype=F32) / MLA_ROPE)
    ang = jnp.arange(L, dtype=F32)[:, None] * inv[None, :]
    c, s = jnp.cos(ang), jnp.sin(ang)
    one, zero, pad = jnp.ones((L, 128), F32), jnp.zeros((L, 128), F32), jnp.zeros((L, 64), F32)
    return (jnp.concatenate([one, c, c, c, c, pad], 1), jnp.concatenate([zero, s, s, s, s, pad], 1))


def _rot(x, transpose=False):
    w = x.shape[1]
    lane = lax.broadcasted_iota(jnp.int32, x.shape, 1)
    up = pltpu.roll(x, w - 16, 1)
    dn = pltpu.roll(x, 16, 1)
    first = (lane % 32) < 16
    return jnp.where(first, up, -dn) if transpose else jnp.where(first, -up, dn)


MLA_QT = 512


def _mla_exp(qf, kf, t, qt):
    n_k = kf.shape[0]
    s = lax.dot_general(qf, kf, (((1,), (1,)), ((), ())), preferred_element_type=F32) * MLA_SCALE
    qpos = t * qt + lax.broadcasted_iota(jnp.int32, (qt, n_k), 0)
    kpos = lax.broadcasted_iota(jnp.int32, (qt, n_k), 1)
    s = jnp.where(kpos <= qpos, s, NEG_INF)
    e = jnp.exp(s - jnp.max(s, axis=-1, keepdims=True))
    return e, jnp.sum(e, axis=-1, keepdims=True)


def _mla_heads(q, kv, kr):
    out = []
    for j in range(2):
        qf = jnp.concatenate([q[:, j * 64:(j + 1) * 64], q[:, 128 + j * 32:128 + (j + 1) * 32]], axis=1)
        kf = jnp.concatenate([kv[:, j * 64:(j + 1) * 64], kr], axis=1)
        out.append((qf, kf, kv[:, 128 + j * 64:128 + (j + 1) * 64]))
    return out


def mla_fwd(q, kv, kr, comm=None):
    L = q.shape[0]
    qt = min(MLA_QT, L)
    nq = L // qt

    def body(q_ref, kv_ref, kr_ref, o_ref):
        for t in range(nq):
            @pl.when(pl.program_id(1) == t)
            def _(t=t):
                n_k = (t + 1) * qt
                outs = []
                for qf, kf, v in _mla_heads(q_ref[...], kv_ref[0:n_k, :], kr_ref[0:n_k, 0:MLA_ROPE]):
                    e, den = _mla_exp(qf, kf, t, qt)
                    outs.append(jnp.dot(e.astype(BF16), v, preferred_element_type=F32) / den)
                o_ref[...] = jnp.concatenate(outs, axis=1)

    return carried(
        body, comm, grid=(MLA_HEADS // 2, nq),
        in_specs=[pl.BlockSpec((qt, 256), lambda hp, n: (n, hp)), pl.BlockSpec((L, 256), lambda hp, n: (0, hp)),
                  pl.BlockSpec((L, 128), lambda hp, n: (0, 0))],
        out_specs=pl.BlockSpec((qt, 128), lambda hp, n: (n, hp)), out_shape=jax.ShapeDtypeStruct((L, 1024), F32),
        semantics=("parallel", "parallel"), name='c_attn')(q, kv, kr)


def mla_bwd(q, kv, kr, do, comm=None):
    L = q.shape[0]
    qt = min(MLA_QT, L)
    nq = L // qt

    def body(q_ref, kv_ref, kr_ref, do_ref, dq_ref, dkv_ref, dkr_ref):
        @pl.when(pl.program_id(1) == 0)
        def _():
            dkv_ref[...] = jnp.zeros_like(dkv_ref)
            dkr_ref[...] = jnp.zeros_like(dkr_ref)

        for t in range(nq):
            @pl.when(pl.program_id(1) == t)
            def _(t=t):
                n_k = (t + 1) * qt
                do_ = do_ref[...]
                dqn, dqr, dkn, dvs = [], [], [], []
                dkr = jnp.zeros((n_k, MLA_ROPE), F32)
                for j, (qf, kf, v) in enumerate(_mla_heads(q_ref[...], kv_ref[0:n_k, :], kr_ref[0:n_k, 0:MLA_ROPE])):
                    doh = do_[:, j * 64:(j + 1) * 64]
                    e, den = _mla_exp(qf, kf, t, qt)
                    p = e * (1.0 / den)
                    dp = lax.dot_general(doh, v, (((1,), (1,)), ((), ())), preferred_element_type=F32)
                    ds = (p * (dp - jnp.sum(p * dp, axis=-1, keepdims=True)) * MLA_SCALE).astype(BF16)
                    dqf = jnp.dot(ds, kf, preferred_element_type=F32)
                    dkf = lax.dot_general(ds, qf, (((0,), (0,)), ((), ())), preferred_element_type=F32)
                    dvs.append(lax.dot_general(p.astype(BF16), doh, (((0,), (0,)), ((), ())), preferred_element_type=F32))
                    dqn.append(dqf[:, :MLA_NOPE])
                    dqr.append(dqf[:, MLA_NOPE:])
                    dkn.append(dkf[:, :MLA_NOPE])
                    dkr = dkr + dkf[:, MLA_NOPE:]
                dq_ref[...] = jnp.concatenate(dqn + dqr + [jnp.zeros((qt, 64), F32)], axis=1)
                dkv_ref[0:n_k, :] += jnp.concatenate(dkn + dvs, axis=1)
                dkr_ref[0, 0:n_k, :] += jnp.concatenate([dkr, jnp.zeros((n_k, 128 - MLA_ROPE), F32)], axis=1)

    return carried(
        body, comm, grid=(MLA_HEADS // 2, nq),
        in_specs=[pl.BlockSpec((qt, 256), lambda hp, n: (n, hp)), pl.BlockSpec((L, 256), lambda hp, n: (0, hp)),
                  pl.BlockSpec((L, 128), lambda hp, n: (0, 0)), pl.BlockSpec((qt, 128), lambda hp, n: (n, hp))],
        out_specs=[pl.BlockSpec((qt, 256), lambda hp, n: (n, hp)), pl.BlockSpec((L, 256), lambda hp, n: (0, hp)),
                   pl.BlockSpec((1, L, 128), lambda hp, n: (hp, 0, 0))],
        out_shape=[jax.ShapeDtypeStruct((L, 2048), F32), jax.ShapeDtypeStruct((L, 2048), F32),
                   jax.ShapeDtypeStruct((MLA_HEADS // 2, L, 128), F32)],
        semantics=("parallel", "arbitrary"), name='c_attn_bwd')(q, kv, kr, do)


def layer_c_fwd(h, w, p, comm=None):
    L = h.shape[0]
    proj = mm(h, w['c_w_in'], 'nn', 'c_proj')

    def f1(c, gq, gk):
        return [rms_fwd(c[:, :768], gq), rms_fwd(c[:, 768:], gk)], []
    (cqn, ckvn), _ = rowwise(f1, [rw(proj, 1024, 1)], [p['c_q_norm'], p['c_kv_norm']], [(768, BF16), (256, BF16)], [],
                             256, 'c_norms')
    qf = mm(cqn, w['c_w_uq'], 'nn', 'c_uq')
    kvf = mm(ckvn, w['c_w_ukv'], 'nn', 'c_ukv', out_dtype=BF16)
    cos, sin = _rope_tables(L)

    def f2(q_, kr_, c, s):
        c8, s8 = jnp.tile(c, (1, 8)), jnp.tile(s, (1, 8))
        return [q_ * c8 + _rot(q_) * s8, kr_ * c[:, 128:] + _rot(kr_) * s[:, 128:]], []
    (q, kr), _ = rowwise(f2, [rw(qf), rw(proj, 128, 16), rw(cos), rw(sin)], [], [(2048, BF16), (128, BF16)], [], 256,
                         'c_rope')
    o, carried_out = mla_fwd(q, kvf, kr, comm=comm)

    def f3(o_, z):
        return [o_ * silu(z)], []
    (po,), _ = rowwise(f3, [rw(o), rw(proj, 1024, 0)], [], [(1024, BF16)], [], 256, 'c_gate')
    yb = mm(po, w['c_w_out'], 'nn', 'c_out')
    return yb, dict(carried=carried_out, h=h, proj=proj, cqn=cqn, ckvn=ckvn, q=q, kv=kvf, kr=kr, o=o, po=po, cos=cos, sin=sin)


def layer_c_bwd(dyb, w, p, sv, comm=None, sink=None):
    g = {}
    dpo = mm(dyb, w['c_w_out'], 'nt', 'c_dpo')
    _dw(g, sink, 'c_w_out', sv['po'], dyb, 'c_dwout')
    proj = sv['proj']
    L = proj.shape[0]

    def f1(dpo_, o, z):
        return [dpo_ * silu(z), dpo_ * o * silu_grad(z)], []
    (do, dz), _ = rowwise(f1, [rw(dpo), rw(sv['o']), rw(proj, 1024, 0)], [], [(1024, BF16), (1024, F32)], [], 256,
                          'c_gate_bwd')
    (dq, dkvf, dkr8), g['carried'] = mla_bwd(sv['q'], sv['kv'], sv['kr'], do, comm=comm)

    def f2(dq_, dkr_, c, s):
        c8, s8 = jnp.tile(c, (1, 8)), jnp.tile(s, (1, 8))
        dk = jnp.sum(dkr_, axis=0)
        return [dq_ * c8 + _rot(dq_ * s8, True), dk * c[:, 128:] + _rot(dk * s[:, 128:], True)], []
    tl = 256
    (dqf, dkr), _ = rowwise(f2, [rw(dq), (dkr8, pl.BlockSpec((8, tl, 128), lambda i: (0, i, 0))), rw(sv['cos']),
                                 rw(sv['sin'])], [], [(2048, BF16), (128, F32)], [], tl, 'c_rope_bwd')
    _dw(g, sink, 'c_w_uq', sv['cqn'], dqf, 'c_dwuq')
    _dw(g, sink, 'c_w_ukv', sv['ckvn'], dkvf, 'c_dwukv')
    dcqn = mm(dqf, w['c_w_uq'], 'nt', 'c_dcqn')
    dckvn = mm(dkvf, w['c_w_ukv'], 'nt', 'c_dckvn')

    def f3(c, dq_, dk_, dz_, dkr_, gq, gk):
        dcq, dgq = rms_bwd(c[:, :768], gq, dq_)
        dckv, dgk = rms_bwd(c[:, 768:], gk, dk_)
        return [jnp.concatenate([dz_, dcq, dckv, dkr_], axis=1)], [dgq, dgk]
    (dproj,), (dgq, dgk) = rowwise(f3, [rw(proj, 1024, 1), rw(dcqn), rw(dckvn), rw(dz), rw(dkr)],
                                   [p['c_q_norm'], p['c_kv_norm']], [(2176, BF16)], [(1, 768), (1, 256)], 256, 'c_dproj')
    g['c_q_norm'], g['c_kv_norm'] = dgq, dgk
    _dw(g, sink, 'c_w_in', sv['h'], dproj, 'c_dwin')
    dh = mm(dproj, w['c_w_in'], 'nt', 'c_dh')
    return dh, g


def _sgu_mix(wm, v, transpose):
    outs = []
    dims = (((0,), (0,)), ((), ())) if transpose else (((1,), (0,)), ((), ()))
    for gi in range(SGU_G):
        outs.append(lax.dot_general(wm[gi], v[:, gi * SGU_C:(gi + 1) * SGU_C].astype(BF16), dims,
                                    preferred_element_type=F32))
    return jnp.concatenate(outs, axis=1)


def _sgu_wmask(ws):
    t = lax.broadcasted_iota(jnp.int32, (SGU_T, SGU_T), 0)
    s = lax.broadcasted_iota(jnp.int32, (SGU_T, SGU_T), 1)
    return jnp.where((s <= t)[None], ws, 0.0).astype(BF16)


def _ln_stats(v):
    mu = jnp.mean(v, axis=-1, keepdims=True)
    vc = v - mu
    rstd = lax.rsqrt(jnp.mean(vc * vc, axis=-1, keepdims=True) + EPS)
    return vc * rstd, rstd


def layer_d_fwd(h, w, p):
    proj = mm(h, w['d_w_in'], 'nn', 'd_proj')
    bias = jnp.repeat(p['d_b_s'][0].T, SGU_C, axis=1)

    def f1(u_, v_, z, ws, lg, lb, bs):
        xh, _ = _ln_stats(gelu(v_))
        s = _sgu_mix(_sgu_wmask(ws), xh * lg + lb, False) + bs
        return [gelu(u_) * s * silu(z)], []
    (po,), _ = rowwise(f1, [rw(proj, 1024, 0), rw(proj, 1024, 1), rw(proj, 1024, 2)],
                       [p['d_w_s'][0], p['d_ln_g'], p['d_ln_b'], bias], [(1024, BF16)], [], SGU_T, 'd_mix')
    yb = mm(po, w['d_w_out'], 'nn', 'd_out')
    return yb, dict(h=h, proj=proj, po=po, bias=bias)


def layer_d_bwd(dyb, w, p, sv, sink=None):
    g = {}
    dpo = mm(dyb, w['d_w_out'], 'nt', 'd_dpo')
    _dw(g, sink, 'd_w_out', sv['po'], dyb, 'd_dwout')
    proj = sv['proj']

    def f1(dpo_, u_, v_, z, ws, lg, lb, bs):
        wm = _sgu_wmask(ws)
        gv = gelu(v_)
        xh, rstd = _ln_stats(gv)
        vn = xh * lg + lb
        s = _sgu_mix(wm, vn, False) + bs
        gu, sz = gelu(u_), silu(z)
        du = dpo_ * s * sz
        ds = dpo_ * gu * sz
        dz = dpo_ * gu * s * silu_grad(z)
        dsb = ds.astype(BF16)
        dws = jnp.stack([lax.dot_general(dsb[:, gi * SGU_C:(gi + 1) * SGU_C], vn[:, gi * SGU_C:(gi + 1) * SGU_C].astype(BF16),
                                         (((1,), (1,)), ((), ())), preferred_element_type=F32) for gi in range(SGU_G)])
        dvn = _sgu_mix(wm, ds, True)
        dlg = jnp.sum(dvn * xh, axis=0, keepdims=True)
        dlb = jnp.sum(dvn, axis=0, keepdims=True)
        dxh = dvn * lg
        dgv = rstd * (dxh - jnp.mean(dxh, axis=-1, keepdims=True) - xh * jnp.mean(dxh * xh, axis=-1, keepdims=True))
        return ([jnp.concatenate([du * gelu_grad(u_), dgv * gelu_grad(v_), dz], axis=1)], [dws, ds, dlg, dlb])
    (dproj,), (dws, dbs, dlg, dlb) = rowwise(
        f1, [rw(dpo), rw(proj, 1024, 0), rw(proj, 1024, 1), rw(proj, 1024, 2)],
        [p['d_w_s'][0], p['d_ln_g'], p['d_ln_b'], sv['bias']], [(3072, BF16)],
        [(SGU_G, SGU_T, SGU_T), (SGU_T, 1024), (1, 1024), (1, 1024)], SGU_T, 'd_mix_bwd')
    tril = np.tril(np.ones((SGU_T, SGU_T), dtype=bool))
    g['d_w_s'] = jnp.where(tril[None], dws, 0.0)[None]
    g['d_b_s'] = dbs.reshape(SGU_T, SGU_G, SGU_C).sum(-1).T[None]
    g['d_ln_g'], g['d_ln_b'] = dlg, dlb
    _dw(g, sink, 'd_w_in', sv['h'], dproj, 'd_dwin')
    dh = mm(dproj, w['d_w_in'], 'nt', 'd_dh')
    return dh, g


def _coords():
    return lax.axis_index("x"), lax.axis_index("y"), lax.axis_index("c")


class AllGather:
    def __init__(self, x):
        self.ins = [x]
        self.outs = [jax.ShapeDtypeStruct((N_DEV,) + x.shape, x.dtype)]
        self.scratch = [pltpu.SemaphoreType.DMA((7,)), pltpu.SemaphoreType.DMA((7,)), pltpu.SemaphoreType.DMA(())]

    def hooks(self, n_steps):
        return [(0, functools.partial(self.phase, 0), False), (n_steps - 1, functools.partial(self.phase, 1), True),
                (n_steps - 1, functools.partial(self.phase, 2), True)]

    @staticmethod
    def phase(which, ins, outs, scratch):
        (x_ref,), (out_ref,), (send_sems, recv_sems, local_sem) = ins, outs, scratch
        x_, y_, c_ = _coords()
        me, sibling = (x_, y_, c_), (x_, y_, 1 - c_)
        chips = [(1 - x_, y_), (x_, 1 - y_), (1 - x_, 1 - y_)]

        def slot(px, py, pc):
            return out_ref.at[4 * px + 2 * py + pc]

        def copy(k, block, to, src=None):
            return pltpu.make_async_remote_copy(src_ref=slot(*block) if src is None else src, dst_ref=slot(*block),
                                                send_sem=send_sems.at[k], recv_sem=recv_sems.at[k], device_id=to,
                                                device_id_type=MESH)

        mine = pltpu.make_async_copy(x_ref, slot(*me), local_sem)
        first = [copy(0, me, sibling, src=x_ref)]
        first += [copy(1 + j, me, (*chip, c_), src=x_ref) for j, chip in enumerate(chips)]
        passed = [copy(4 + j, (*chip, c_), sibling) for j, chip in enumerate(chips)]
        if which == 0:
            mine.start()
            for cp in first:
                cp.start()
        elif which == 1:
            for j, chip in enumerate(chips):
                copy(1 + j, (*chip, c_), me).wait_recv()
                passed[j].start()
        else:
            copy(0, sibling, me).wait_recv()
            for j, chip in enumerate(chips):
                copy(4 + j, (*chip, 1 - c_), me).wait_recv()
            for cp in first + passed:
                cp.wait_send()
            mine.wait()


class ChipExchange:
    def __init__(self, part):
        self.ins = [part]
        self.outs = [jax.ShapeDtypeStruct((3,) + part.shape[1:], part.dtype)]
        self.scratch = [pltpu.SemaphoreType.DMA((3,)), pltpu.SemaphoreType.DMA((3,))]

    def hooks(self, n_steps):
        return [(0, functools.partial(self.phase, 0), False), (n_steps - 1, functools.partial(self.phase, 1), True)]

    @staticmethod
    def phase(which, ins, outs, scratch):
        (p_ref,), (land_ref,), (send_sems, recv_sems) = ins, outs, scratch
        x_, y_, c_ = _coords()
        copies = []
        for r, (fx, fy) in enumerate([(1, 0), (0, 1), (1, 1)]):
            tx = jnp.where(fx == 1, 1 - x_, x_)
            ty = jnp.where(fy == 1, 1 - y_, y_)
            copies.append(pltpu.make_async_remote_copy(src_ref=p_ref.at[2 * tx + ty], dst_ref=land_ref.at[r],
                                                       send_sem=send_sems.at[r], recv_sem=recv_sems.at[r],
                                                       device_id=(tx, ty, c_), device_id_type=MESH))
        if which == 0:
            for cp in copies:
                cp.start()
        else:
            for cp in copies:
                cp.wait_recv()
            for cp in copies:
                cp.wait_send()


class Both:
    def __init__(self, a, b):
        self.parts = (a, b)
        self.ins, self.outs, self.scratch = a.ins + b.ins, a.outs + b.outs, a.scratch + b.scratch

    def hooks(self, n_steps):
        res, oi, oo, osc = [], 0, 0, 0
        for p in self.parts:
            sl = (slice(oi, oi + len(p.ins)), slice(oo, oo + len(p.outs)), slice(osc, osc + len(p.scratch)))
            res += [(at, functools.partial(self.sub, fn, sl), after) for at, fn, after in p.hooks(n_steps)]
            oi, oo, osc = oi + len(p.ins), oo + len(p.outs), osc + len(p.scratch)
        return res

    @staticmethod
    def sub(fn, sl, ins, outs, scratch):
        fn(ins[sl[0]], outs[sl[1]], scratch[sl[2]])


def run_comm(comm, name):
    def body(*refs):
        ci, co = len(comm.ins), len(comm.outs)
        for _, fn, _ in comm.hooks(1):
            fn(refs[:ci], refs[ci:ci + co], refs[ci + co:])

    return pl.pallas_call(body, out_shape=list(comm.outs), in_specs=[ANY] * len(comm.ins),
                          out_specs=[ANY] * len(comm.outs), scratch_shapes=list(comm.scratch), name=name)(*comm.ins)


def all_gather(x, name):
    return run_comm(AllGather(x), name)[0]


class SiblingExchange:
    def __init__(self, gfull):
        self.ins = [gfull]
        self.outs = [jax.ShapeDtypeStruct((4,) + gfull.shape[1:], gfull.dtype)]
        self.scratch = [pltpu.SemaphoreType.DMA((4,)), pltpu.SemaphoreType.DMA((4,))]

    def hooks(self, n_steps):
        return [(0, functools.partial(self.phase, 0), False), (n_steps - 1, functools.partial(self.phase, 1), True)]

    @staticmethod
    def phase(which, ins, outs, scratch):
        (g_ref,), (land_ref,), (send_sems, recv_sems) = ins, outs, scratch
        x_, y_, c_ = _coords()
        copies = [pltpu.make_async_remote_copy(src_ref=g_ref.at[2 * k + 1 - c_], dst_ref=land_ref.at[k],
                                               send_sem=send_sems.at[k], recv_sem=recv_sems.at[k],
                                               device_id=(x_, y_, 1 - c_), device_id_type=MESH) for k in range(4)]
        if which == 0:
            for cp in copies:
                cp.start()
        else:
            for cp in copies:
                cp.wait_recv()
            for cp in copies:
                cp.wait_send()


def rs_sibling(gfull, tag):
    return run_comm(SiblingExchange(gfull), 'rs_sibling_' + tag)[0]


def rs_pair_add(gfull, land, core, tag):
    _, R, C = gfull.shape
    tl = R

    def body(c_ref, g_ref, l_ref, o_ref):
        o_ref[...] = (g_ref[...].astype(F32) + l_ref[...].astype(F32)).astype(BF16)

    return pl.pallas_call(
        body, out_shape=jax.ShapeDtypeStruct((4, R, C), BF16),
        grid_spec=pltpu.PrefetchScalarGridSpec(
            num_scalar_prefetch=1, grid=(4, R // tl),
            in_specs=[pl.BlockSpec((1, tl, C), lambda k, i, c: (2 * k + c[0], i, 0)),
                      pl.BlockSpec((1, tl, C), lambda k, i, c: (k, i, 0))],
            out_specs=pl.BlockSpec((1, tl, C), lambda k, i, c: (k, i, 0))),
        compiler_params=pltpu.CompilerParams(dimension_semantics=("parallel", "parallel")), name='rs_pair_add_' + tag)(
            core, gfull, land)


def rs_chips(part, tag):
    return run_comm(ChipExchange(part), 'rs_chips_' + tag)[0]


def _adam(wv, gv, mv, vv):
    m = ADAM_B1 * mv + (1.0 - ADAM_B1) * gv
    v = ADAM_B2 * vv + (1.0 - ADAM_B2) * (gv * gv)
    m_hat = m / (1.0 - ADAM_B1 ** ADAM_STEP)
    v_hat = v / (1.0 - ADAM_B2 ** ADAM_STEP)
    delta = -ADAM_LR * (m_hat / (jnp.sqrt(v_hat) + ADAM_EPS) + ADAM_WD * wv)
    return delta, m, v


def _sum4(p_ref, l_ref):
    return ((p_ref[0].astype(F32) + l_ref[0].astype(F32)) + l_ref[1].astype(F32)) + l_ref[2].astype(F32)


def rs_rep_sum(part, land, chip):
    def body(c_ref, p_ref, l_ref, o_ref):
        o_ref[...] = _sum4(p_ref, l_ref).astype(BF16)

    return pl.pallas_call(
        body, out_shape=jax.ShapeDtypeStruct((REP_SLOT, LANES), BF16),
        grid_spec=pltpu.PrefetchScalarGridSpec(
            num_scalar_prefetch=1, grid=(1,),
            in_specs=[pl.BlockSpec((1, REP_SLOT, LANES), lambda i, c: (c[0], 0, 0)),
                      pl.BlockSpec((3, REP_SLOT, LANES), lambda i, c: (0, 0, 0))],
            out_specs=pl.BlockSpec((REP_SLOT, LANES), lambda i, c: (0, 0))),
        compiler_params=pltpu.CompilerParams(dimension_semantics=("parallel",)), name='rs_rep')(chip, part, land)


def adam_param(name, shape, off, w, m, v, chip, part=None, land=None, grep=None, fold=1):
    r, c = shape
    rp, nt, rb = _tiles((r // fold, c * fold))
    rbw = min(r, rb) if fold == 1 else r
    n_src = 2 if grep is None else 1
    ns = w.shape
    assert int(np.prod(ns[:-1])) == r and ns[-1] == c and (fold == 1 or (rb == rp and nt == 1))
    if fold > 1:
        nat_block, nat_map = ns, lambda i, cr: (0,) * len(ns)
    elif len(ns) == 2:
        nat_block, nat_map = (rbw, c), lambda i, cr: (i, 0)
    elif int(np.prod(ns[:-2])) == 1:
        nat_block, nat_map = (1,) * (len(ns) - 2) + (rbw, c), lambda i, cr: (0,) * (len(ns) - 2) + (i, 0)
    else:
        assert len(ns) == 4 and ns[0] == 1 and rbw % ns[2] == 0
        nat_block, nat_map = (1, rbw // ns[2], ns[2], c), lambda i, cr: (0, i, 0, 0)

    def body(c_ref, *refs):
        srcs = refs[:n_src * nt]
        w_ref, m_ref, v_ref, g_ref, d_ref, nm_ref, nv_ref = refs[n_src * nt:]
        if grep is None:
            tiles = [_sum4(srcs[2 * t], srcs[2 * t + 1]) for t in range(nt)]
        else:
            tiles = [srcs[t][...].astype(F32) for t in range(nt)]
        if fold > 1:
            g = jnp.concatenate([tiles[0][:, q * c:(q + 1) * c] for q in range(fold)], axis=0)
        else:
            g = (tiles[0] if nt == 1 else jnp.concatenate(tiles, axis=1))[:rbw, :c]
        g_ref[...] = g.reshape(nat_block)
        res = _adam(w_ref[...].reshape(rbw, c), g, m_ref[...].reshape(rbw, c), v_ref[...].reshape(rbw, c))
        for ref, val in zip((d_ref, nm_ref, nv_ref), res):
            ref[...] = val.reshape(nat_block)

    in_specs, args = [], []
    for t in range(nt):
        b0 = (off + t * rp) // rb
        assert (off + t * rp) % rb == 0
        if grep is None:
            in_specs += [pl.BlockSpec((1, rb, LANES), functools.partial(lambda i, cr, b0: (cr[0], b0 + i, 0), b0=b0)),
                         pl.BlockSpec((3, rb, LANES), functools.partial(lambda i, cr, b0: (0, b0 + i, 0), b0=b0))]
            args += [part, land]
        else:
            in_specs.append(pl.BlockSpec((rb, LANES), functools.partial(lambda i, cr, b0: (b0 + i, 0), b0=b0)))
            args.append(grep)
    nat = pl.BlockSpec(nat_block, nat_map)
    return pl.pallas_call(
        body, out_shape=[jax.ShapeDtypeStruct(ns, F32)] * 4,
        grid_spec=pltpu.PrefetchScalarGridSpec(num_scalar_prefetch=1, grid=(rp // rb,), in_specs=in_specs + [nat] * 3,
                                               out_specs=[nat] * 4),
        compiler_params=pltpu.CompilerParams(dimension_semantics=("parallel",)), name='adam_' + name)(
            chip, *args, w, m, v)


def adam_small(names, grep, P, M, V):
    in_specs, args, out_specs, out_shape, meta = [], [], [], [], []
    for n in names:
        s = REP_SHAPE[n]
        rp, nt, _ = _tiles(s)
        ns = P[n].shape
        for t in range(nt):
            b0 = (REP_OFF[n] + t * rp) // rp
            assert (REP_OFF[n] + t * rp) % rp == 0
            in_specs.append(pl.BlockSpec((rp, LANES), functools.partial(lambda i, b0: (b0, 0), b0=b0)))
            args.append(grep)
        nat = pl.BlockSpec(ns, functools.partial(lambda i, nd: (0,) * nd, nd=len(ns)))
        in_specs += [nat] * 3
        args += [P[n], M[n], V[n]]
        out_specs += [nat] * 4
        out_shape += [jax.ShapeDtypeStruct(ns, F32)] * 4
        meta.append((s, nt, ns))
    n_in = len(in_specs)

    def body(*refs):
        ins, outs = refs[:n_in], refs[n_in:]
        k = 0
        for p, ((r, c), nt, ns) in enumerate(meta):
            tiles = [ins[k + t][...].astype(F32) for t in range(nt)]
            w_ref, m_ref, v_ref = ins[k + nt:k + nt + 3]
            k += nt + 3
            g = (tiles[0] if nt == 1 else jnp.concatenate(tiles, axis=1))[:r, :c]
            res = (g,) + _adam(w_ref[...].reshape(r, c), g, m_ref[...].reshape(r, c), v_ref[...].reshape(r, c))
            for ref, val in zip(outs[4 * p:4 * p + 4], res):
                ref[...] = val.reshape(ns)

    res = pl.pallas_call(body, grid=(1,), in_specs=in_specs, out_specs=out_specs, out_shape=out_shape,
                         compiler_params=pltpu.CompilerParams(dimension_semantics=("arbitrary",)), name='adam_small')(*args)
    return {n: tuple(res[4 * p:4 * p + 4]) for p, n in enumerate(names)}


VM = pl.BlockSpec(memory_space=pltpu.VMEM)


def _tile_value(w, t, rp):
    r, c = w.shape
    wt = min(LANES, c - t * LANES)
    tile = w[:, t * LANES:t * LANES + wt]
    if wt < LANES:
        tile = jnp.concatenate([tile, jnp.zeros((r, LANES - wt), tile.dtype)], axis=1)
    if rp > r:
        tile = jnp.concatenate([tile, jnp.zeros((rp - r, LANES), tile.dtype)], axis=0)
    return tile


def pack_layer(layer, blocks):
    names = LAYER_PARAMS[layer]

    def body(*refs):
        tiles = []
        for ref, n in zip(refs[:-1], names):
            rp, nt, _ = _tiles(_block_shape(n))
            w = ref[...].reshape(_block_shape(n))
            tiles += [_tile_value(w, t, rp) for t in range(nt)]
        refs[-1][...] = jnp.concatenate(tiles, axis=0).astype(BF16)

    return pl.pallas_call(body, out_shape=jax.ShapeDtypeStruct((LAYER_ROWS[layer], LANES), BF16),
                          in_specs=[VM] * len(names), out_specs=VM, name='pack_' + layer)(*[blocks[n] for n in names])


def assemble(name, gathered):
    (rf, cf), ax = SHARDED[name]
    r, c = _block_shape(name)
    rp, nt, _ = _tiles((r, c))
    off = SH_OFF[name]
    out_cols = cf if ax == 0 else len(perm_index(name))

    def body(g_ref, o_ref, buf, sem):
        cp = pltpu.make_async_copy(g_ref.at[:, pl.ds(off, nt * rp), :], buf, sem)
        cp.start()
        cp.wait()
        if ax == 0:
            for j in range(N_DEV):
                o_ref[j * r:(j + 1) * r, :] = jnp.concatenate([buf[j, t * rp:(t + 1) * rp, :] for t in range(nt)], axis=1)
            return
        pieces = []
        for p in PERM[name]:
            if p[0] == 'z':
                pieces.append(jnp.zeros((r, p[1]), BF16))
                continue
            n0, w = p
            while w > 0:
                j, cb = divmod(n0, c)
                t, lane = divmod(cb, LANES)
                wl = min(w, LANES - lane, c - cb)
                pieces.append(buf[j, t * rp:t * rp + r, lane:lane + wl])
                n0, w = n0 + wl, w - wl
        o_ref[...] = jnp.concatenate(pieces, axis=1)

    return pl.pallas_call(
        body, out_shape=jax.ShapeDtypeStruct((rf, out_cols), BF16), in_specs=[ANY], out_specs=VM,
        scratch_shapes=[pltpu.VMEM((N_DEV, nt * rp, LANES), BF16), pltpu.SemaphoreType.DMA(())], name='asm_' + name)(
            gathered)


def chunk_grad(layer, name, dw, gfull):
    (rf, cf), ax = SHARDED[name]
    r, c = _block_shape(name)
    rp, nt, _ = _tiles((r, c))
    off = SH_OFF[name]
    if ax == 1:
        idx = perm_index(name) if name in PERM else np.arange(cf)
        inv = np.full(cf, -1)
        inv[idx[idx >= 0]] = np.nonzero(idx >= 0)[0]

    def body(*refs):
        dw_ref, o_ref, buf, sem = refs[0], refs[-3], refs[-2], refs[-1]
        for j in range(N_DEV):
            for t in range(nt):
                if ax == 0:
                    tile = dw_ref[j * r:(j + 1) * r, t * LANES:(t + 1) * LANES]
                else:
                    cols = inv[j * c + t * LANES:j * c + min((t + 1) * LANES, c)]
                    cuts = [0] + [k for k in range(1, len(cols)) if cols[k] != cols[k - 1] + 1] + [len(cols)]
                    pieces = [dw_ref[:, int(cols[a]):int(cols[b - 1]) + 1] for a, b in zip(cuts[:-1], cuts[1:])]
                    if len(cols) < LANES:
                        pieces.append(jnp.zeros((r, LANES - len(cols)), F32))
                    tile = pieces[0] if len(pieces) == 1 else jnp.concatenate(pieces, axis=1)
                    if rp > r:
                        tile = jnp.concatenate([tile, jnp.zeros((rp - r, LANES), F32)], axis=0)
                buf[j, t * rp:(t + 1) * rp, :] = tile.astype(BF16)
        cp = pltpu.make_async_copy(buf, o_ref.at[:, pl.ds(off, nt * rp), :], sem)
        cp.start()
        cp.wait()

    shape = jax.ShapeDtypeStruct((N_DEV, LAYER_ROWS[layer], LANES), BF16)
    scratch = [pltpu.VMEM((N_DEV, nt * rp, LANES), BF16), pltpu.SemaphoreType.DMA(())]
    if gfull is None:
        return pl.pallas_call(body, out_shape=shape, in_specs=[VM], out_specs=ANY, scratch_shapes=scratch,
                              name='chunk_' + name)(dw)
    return pl.pallas_call(body, out_shape=shape, in_specs=[VM, ANY], out_specs=ANY, scratch_shapes=scratch,
                          input_output_aliases={1: 0}, name='chunk_' + name)(dw, gfull)


class GradSink:
    def __init__(self):
        self.bufs = {}

    def put(self, name, a, b, mm_name):
        (rf, cf), ax = SHARDED[name]
        r, c = _block_shape(name)
        group = GROUP_OF[name]
        direct = ax == 0 or (c % LANES == 0 and PERM[name] == [(0, cf)])
        if direct:
            self.bufs[group] = mm_tn_chunked(a, b, mm_name, group, name, self.bufs.get(group))
        else:
            self.add(name, mm(a, b, 'tn', mm_name))

    def add(self, name, dw):
        group = GROUP_OF[name]
        self.bufs[group] = chunk_grad(group, name, dw, self.bufs.get(group))


def mm_tn_chunked(a, b, mm_name, layer, wname, gfull):
    (rf, cf), ax = SHARDED[wname]
    r, c = _block_shape(wname)
    rp, nt, _ = _tiles((r, c))
    off = SH_OFF[wname]
    K, M = a.shape
    N = b.shape[1]
    assert (M, N) == (rf, cf) and rp == r
    if ax == 0:
        tn = 4 * LANES
        grid, bspec = (N // tn,), pl.BlockSpec((K, tn), lambda g: (0, g))
        ospec = pl.BlockSpec((N_DEV, 4 * r, LANES), lambda g: (0, off // (4 * r) + g, 0))
        assert off % (4 * r) == 0 and nt % 4 == 0

        def store(res, o_ref):
            for j in range(N_DEV):
                for q in range(4):
                    o_ref[j, q * r:(q + 1) * r, :] = res[j * r:(j + 1) * r, q * LANES:(q + 1) * LANES].astype(BF16)
    else:
        tn = c
        grid, bspec = (N_DEV,), pl.BlockSpec((K, tn), lambda g: (0, g))
        ospec = pl.BlockSpec((1, nt * r, LANES), lambda g: (g, off // (nt * r), 0))
        assert off % (nt * r) == 0

        def store(res, o_ref):
            for t in range(nt):
                o_ref[0, t * r:(t + 1) * r, :] = res[:, t * LANES:(t + 1) * LANES].astype(BF16)

    def body(*refs):
        a_ref, b_ref, o_ref = refs[0], refs[1], refs[-1]
        store(lax.dot_general(a_ref[...].astype(BF16), b_ref[...].astype(BF16), _TN, preferred_element_type=F32), o_ref)

    shape = jax.ShapeDtypeStruct((N_DEV, LAYER_ROWS[layer], LANES), BF16)
    aspec = pl.BlockSpec((K, M), lambda g: (0, 0))
    params = pltpu.CompilerParams(dimension_semantics=("parallel",))
    if gfull is None:
        return pl.pallas_call(body, grid=grid, in_specs=[aspec, bspec], out_specs=ospec, out_shape=shape,
                              compiler_params=params, name=mm_name)(a, b)
    return pl.pallas_call(body, grid=grid, in_specs=[aspec, bspec, ANY], out_specs=ospec, out_shape=shape,
                          input_output_aliases={2: 0}, compiler_params=params, name=mm_name)(a, b, gfull)


def pack_rep(G):
    def body(*refs):
        tiles = []
        for ref, n in zip(refs[:-1], REP_SHAPE):
            rp, nt, _ = _tiles(_rep_packed_shape(n))
            g = ref[...]
            fold = REP_FOLD.get(n, 1)
            if fold > 1:
                rr = g.shape[0] // fold
                g = jnp.concatenate([g[q * rr:(q + 1) * rr] for q in range(fold)], axis=1)
            tiles += [_tile_value(g, t, rp) for t in range(nt)]
        rows = sum(t.shape[0] for t in tiles)
        if rows < REP_ROWS:
            tiles.append(jnp.zeros((REP_ROWS - rows, LANES), F32))
        full = jnp.concatenate(tiles, axis=0)
        for j in range(N_DEV):
            refs[-1][j] = full[j * REP_CHUNK:(j + 1) * REP_CHUNK]

    return pl.pallas_call(body, out_shape=jax.ShapeDtypeStruct((N_DEV, REP_SLOT, LANES), F32),
                          in_specs=[VM] * len(REP_SHAPE), out_specs=VM, name='pack_rep')(
                              *[G[n].reshape(s) for n, s in REP_SHAPE.items()])


def _pack_small(blocks, order, rows, width, dtype):
    flat = jnp.concatenate([blocks[n].reshape(-1).astype(dtype) for n in order])
    return jnp.pad(flat, (0, rows * width - flat.shape[0])).reshape(rows, width)


def kernel(x, pre_norm, post_norm, rel_bias, a_w_in, a_lam_re, a_lam_im, a_log_dt, a_b_re, a_b_im, a_c_re, a_c_im, a_d, a_w_glu, a_b_glu, a_w_out, b_w_in, b_sinks, b_w_out, c_w_in, c_q_norm, c_kv_norm, c_w_uq, c_w_ukv, c_w_out, d_w_in, d_ln_g, d_ln_b, d_w_s, d_b_s, d_w_out, loss_target, m_pre_norm, m_post_norm, m_rel_bias, m_a_w_in, m_a_lam_re, m_a_lam_im, m_a_log_dt, m_a_b_re, m_a_b_im, m_a_c_re, m_a_c_im, m_a_d, m_a_w_glu, m_a_b_glu, m_a_w_out, m_b_w_in, m_b_sinks, m_b_w_out, m_c_w_in, m_c_q_norm, m_c_kv_norm, m_c_w_uq, m_c_w_ukv, m_c_w_out, m_d_w_in, m_d_ln_g, m_d_ln_b, m_d_w_s, m_d_b_s, m_d_w_out, v_pre_norm, v_post_norm, v_rel_bias, v_a_w_in, v_a_lam_re, v_a_lam_im, v_a_log_dt, v_a_b_re, v_a_b_im, v_a_c_re, v_a_c_im, v_a_d, v_a_w_glu, v_a_b_glu, v_a_w_out, v_b_w_in, v_b_sinks, v_b_w_out, v_c_w_in, v_c_q_norm, v_c_kv_norm, v_c_w_uq, v_c_w_ukv, v_c_w_out, v_d_w_in, v_d_ln_g, v_d_ln_b, v_d_w_s, v_d_b_s, v_d_w_out):
    loc = locals()
    P = {n: loc[n] for n in WEIGHTS}
    M = {n: loc['m_' + n] for n in WEIGHTS}
    V = {n: loc['v_' + n] for n in WEIGHTS}
    xs = x[0]
    L = xs.shape[0]

    blocks = {n: P[n].reshape(_block_shape(n)) for n in SHARDED}
    packed = {layer: pack_layer(layer, P) for layer in LAYER_PARAMS}
    W = {}

    def assemble_layer(layer, gathered):
        for n in LAYER_PARAMS[layer]:
            if n not in SHARDED_F32:
                W[n] = assemble(n, gathered)

    small =all_gather(_pack_small(blocks, SHARDED_F32, SMALL_ROWS, 128, F32), 'ag_small')
    Pl = dict(P)
    for n in SHARDED_F32:
        c = SHARDED[n][0][1]
        bc = c // N_DEV
        Pl[n] = small.reshape(N_DEV, -1)[:, SMALL_OFF[n]:SMALL_OFF[n] + bc].reshape(1, c)
    cx, cy, cc = _coords()
    core = jnp.reshape(cc, (1,)).astype(jnp.int32)
    chip = jnp.reshape(2 * cx + cy, (1,)).astype(jnp.int32)

    def pair_sums(gfull, tag):
        return rs_pair_add(gfull, rs_sibling(gfull, tag), core, tag)

    fwd = [layer_a_fwd, layer_b_fwd, layer_c_fwd, layer_d_fwd]
    bwd = [layer_a_bwd, layer_b_bwd, layer_c_bwd, layer_d_bwd]
    saved = []
    xc = xs

    def fpre(x_, g_):
        return [rms_fwd(x_, g_)], []
    (h,), _ = rowwise(fpre, [rw(xc)], [P['pre_norm'][0:1]], [(D_MODEL, BF16)], [], 256, 'pre_norm0')
    for i in range(4):
        if i == 0:
            yb, sv = fwd[i](h, W, Pl, comm=Both(AllGather(packed['a2']), AllGather(packed['b'])),
                            on_carried=lambda got: assemble_layer('a2', got[0]),
                            prep_comm=AllGather(packed['a1']), on_prep=lambda got: assemble_layer('a1', got[0]))
            assemble_layer('b', sv['carried'][1])
        elif i < 3:
            nxt = 'abcd'[i + 1]
            yb, sv = fwd[i](h, W, Pl, comm=AllGather(packed[nxt]))
            assemble_layer(nxt, sv['carried'][0])
        else:
            yb, sv = fwd[i](h, W, Pl)

        sv['x'], sv['yb'] = xc, yb
        saved.append(sv)
        if i < 3:

            def fpost(x_, y_, gpost, gpre):
                xn_ = x_ + rms_fwd(y_, gpost)
                return [xn_, rms_fwd(xn_, gpre)], []
            (xc, h), _ = rowwise(fpost, [rw(xc), rw(yb)], [P['post_norm'][i:i + 1], P['pre_norm'][i + 1:i + 2]],
                                 [(D_MODEL, F32), (D_MODEL, BF16)], [], 256, f'post_pre_norm{i}')
        else:

            def floss(x_, y_, t_, gpost):
                d = x_ + rms_fwd(y_, gpost) - t_
                return [d * (1.0 / D_MODEL)], [0.5 * jnp.sum(jnp.sum(d * d, axis=-1, keepdims=True) * (1.0 / D_MODEL),
                                                             axis=0, keepdims=True)]
            (dx,), (loss_loc,) = rowwise(floss, [rw(xc), rw(yb), rw(loss_target[0])], [P['post_norm'][i:i + 1]],
                                         [(D_MODEL, F32)], [(1, 1)], 256, 'post_norm_loss')
    loss = lax.psum(loss_loc[0, 0], ("x", "y", "c"))

    G, out = {}, {}
    dpre, dpost = [None] * 4, [None] * 4

    def adam_layer(layer, part, land2):
        for n in LAYER_PARAMS[layer]:
            s = _block_shape(n)
            out[n] = adam_param(n, s, SH_OFF[n], P[n], M[n], V[n], chip, part=part, land=land2)

    def fpost_b(y_, d_, g_):
        dy, dg = rms_bwd(y_, g_, d_)
        return [dy], [dg]
    (dyb,), (dpost[3],) = rowwise(fpost_b, [rw(saved[3]['yb']), rw(dx)], [P['post_norm'][3:4]], [(D_MODEL, BF16)],
                                  [(1, D_MODEL)], 256, 'post_norm_bwd3')
    pending = None
    sink = GradSink()
    for i in reversed(range(4)):
        sv = saved[i]
        if pending is None:
            dh, g = bwd[i](dyb, W, Pl, sv, sink=sink)
        elif i > 0:
            dh, g = bwd[i](dyb, W, Pl, sv, comm=ChipExchange(pending[1]), sink=sink)
            adam_layer(pending[0], pending[1], g['carried'][0])
        else:
            early = {}

            def both():
                early['part'] = pair_sums(sink.bufs['a2'], 'a2')
                return Both(ChipExchange(pending[1]), ChipExchange(early['part']))
            dh, g = bwd[i](dyb, W, Pl, sv, comm=both, sink=sink)
            adam_layer(pending[0], pending[1], g['carried'][0])
            adam_layer('a2', early['part'], g['carried'][1])
        g.pop('carried', None)
        land_a1 = g.pop('land_a1', None)
        G.update(g)
        group = LAYER_GROUPS['abcd'[i]][0]
        for n in LAYER_PARAMS[group]:
            if n in g:
                sink.add(n, g[n])
        if i > 0:
            swap = SiblingExchange(sink.bufs[group])
        else:
            part_a1 = rs_pair_add(sink.bufs[group], land_a1, core, group)
            swap = ChipExchange(part_a1)

        if i > 0:

            def fpre_b(x_, dh_, d_, y_, gpre, gpost):
                dxl, dg = rms_bwd(x_, gpre, dh_)
                dy, dgp = rms_bwd(y_, gpost, d_ + dxl)
                return [d_ + dxl, dy], [dg, dgp]
            (dx, dyb), (dpre[i], dpost[i - 1]), (land,) = rowwise(
                fpre_b, [rw(sv['x']), rw(dh), rw(dx), rw(saved[i - 1]['yb'])],
                [P['pre_norm'][i:i + 1], P['post_norm'][i - 1:i]], [(D_MODEL, F32), (D_MODEL, BF16)],
                [(1, D_MODEL), (1, D_MODEL)], 256, f'pre_post_norm_bwd{i}', comm=swap)
        else:

            def fpre_b0(x_, dh_, d_, g_):
                dxl, dg = rms_bwd(x_, g_, dh_)
                return [d_ + dxl], [dg]
            (dx,), (dpre[i],), (land2_a1,) = rowwise(fpre_b0, [rw(sv['x']), rw(dh), rw(dx)], [P['pre_norm'][i:i + 1]],
                                                     [(D_MODEL, F32)], [(1, D_MODEL)], 256, 'pre_norm_bwd0', comm=swap)
            adam_layer('a1', part_a1, land2_a1)
            break
        pending = (group, rs_pair_add(sink.bufs[group], land, core, group))
    G['pre_norm'] = jnp.concatenate(dpre, axis=0)
    G['post_norm'] = jnp.concatenate(dpost, axis=0)

    part = pair_sums(pack_rep(G), 'rep')
    land2 = rs_chips(part, 'rep')
    grep = all_gather(rs_rep_sum(part, land2, chip), 'ag_rep')[:, :REP_CHUNK].reshape(REP_ROWS, LANES)
    small_names = [n for n, s in REP_SHAPE.items() if s[0] <= 64]
    out.update(adam_small(small_names, grep, P, M, V))
    for n, s in REP_SHAPE.items():
        if n not in small_names:
            out[n] = adam_param(n, s, REP_OFF[n], P[n], M[n], V[n], chip, grep=grep, fold=REP_FOLD.get(n, 1))
    res = [loss, dx[None]]
    for kind in range(4):
        res += [out[n][kind].reshape(P[n].shape) for n in WEIGHTS]
    return tuple(res)
```

```python
import functools
import math

import numpy as np
import jax
import jax.numpy as jnp
from jax import lax
from jax.experimental import pallas as pl
from jax.experimental.pallas import tpu as pltpu

F32 = jnp.float32
BF16 = jnp.bfloat16
MESH = pl.DeviceIdType.MESH
ANY = pl.BlockSpec(memory_space=pl.ANY)

N_DEV = 8
D_MODEL = 1024
EPS = 1e-6
NEG_INF = -1e30
SSM_G, SSM_P, SSM_H = 64, 64, 16
SSM_T = 256
SSM_TS = 8
SSM_WC = 512
HEAD_DIM = 64
SWA_HEADS, SWA_KV = 16, 2
WINDOW = 128
REL_BUCKETS, REL_MAX_DIST = 32, 128
MLA_HEADS, MLA_NOPE, MLA_ROPE, MLA_V = 16, 64, 32, 64
MLA_Q_RANK, MLA_KV_RANK = 768, 256
ROPE_BASE = 10000.0
SGU_G, SGU_C, SGU_T = 16, 64, 128
ADAM_LR, ADAM_B1, ADAM_B2, ADAM_EPS, ADAM_WD, ADAM_STEP = 0.001, 0.9, 0.999, 1e-08, 0.01, 10

WEIGHTS = ['pre_norm', 'post_norm', 'rel_bias', 'a_w_in', 'a_lam_re', 'a_lam_im', 'a_log_dt', 'a_b_re', 'a_b_im',
           'a_c_re', 'a_c_im', 'a_d', 'a_w_glu', 'a_b_glu', 'a_w_out', 'b_w_in', 'b_sinks', 'b_w_out', 'c_w_in',
           'c_q_norm', 'c_kv_norm', 'c_w_uq', 'c_w_ukv', 'c_w_out', 'd_w_in', 'd_ln_g', 'd_ln_b', 'd_w_s', 'd_b_s',
           'd_w_out']
SHARDED = {'a_w_in': ((1024, 2048), 1), 'a_w_glu': ((1024, 1024), 0), 'a_w_out': ((1024, 1024), 0),
           'b_w_in': ((1024, 2304), 1), 'b_w_out': ((1024, 1024), 0), 'c_w_in': ((1024, 2080), 1),
           'c_q_norm': ((1, 768), 1), 'c_kv_norm': ((1, 256), 1), 'c_w_uq': ((768, 1536), 1),
           'c_w_ukv': ((256, 2048), 1), 'c_w_out': ((1024, 1024), 0), 'd_w_in': ((1024, 3072), 1),
           'd_ln_g': ((1, 1024), 1), 'd_ln_b': ((1, 1024), 1), 'd_w_out': ((1024, 1024), 0)}
SHARDED_F32 = ['c_q_norm', 'c_kv_norm', 'd_ln_g', 'd_ln_b']
REPLICATED = [n for n in WEIGHTS if n not in SHARDED]


def _cdiv(a, b):
    return -(-a // b)


def _block_shape(name):
    (r, c), ax = SHARDED[name]
    return (r // N_DEV, c) if ax == 0 else (r, c // N_DEV)


LANES = 128
LAYER_PARAMS = {'a1': ['a_w_in'], 'a2': ['a_w_glu', 'a_w_out'], 'b': ['b_w_in', 'b_w_out'],
                'c': ['c_w_in', 'c_w_uq', 'c_w_ukv', 'c_w_out', 'c_q_norm', 'c_kv_norm'],
                'd': ['d_w_in', 'd_w_out', 'd_ln_g', 'd_ln_b']}


def _tiles(shape):
    r, c = shape
    rp = max(r, 16)
    rb = 512 if rp % 512 == 0 else 256 if rp % 256 == 0 else rp
    return rp, _cdiv(c, LANES), rb


SH_OFF, LAYER_ROWS = {}, {}
for _l, _names in LAYER_PARAMS.items():
    _o = 0
    for _n in _names:
        _rp, _nt, _rb = _tiles(_block_shape(_n))
        assert _o % _rb == 0
        SH_OFF[_n] = _o
        _o += _rp * _nt
    assert _o % 16 == 0
    LAYER_ROWS[_l] = _o
GROUP_OF = {_n: _l for _l, _names in LAYER_PARAMS.items() for _n in _names}
LAYER_GROUPS = {'a': ['a1', 'a2'], 'b': ['b'], 'c': ['c'], 'd': ['d']}

REP_SHAPE = {'d_w_s': (2048, 128), 'a_b_re': (4096, 16), 'a_b_im': (4096, 16), 'a_c_re': (1024, 64),
             'a_c_im': (1024, 64), 'pre_norm': (4, 1024), 'post_norm': (4, 1024), 'a_lam_re': (64, 64),
             'a_lam_im': (64, 64), 'a_d': (1, 1024), 'a_b_glu': (1, 1024), 'rel_bias': (32, 16), 'd_b_s': (16, 128),
             'a_log_dt': (1, 64), 'b_sinks': (1, 16)}
REP_FOLD = {'a_b_re': 8, 'a_b_im': 8, 'a_c_re': 2, 'a_c_im': 2}


def _rep_packed_shape(name):
    (r, c), f = REP_SHAPE[name], REP_FOLD.get(name, 1)
    return (r // f, c * f)


REP_OFF = {}
_o = 0
for _n in REP_SHAPE:
    _rp, _nt, _rb = _tiles(_rep_packed_shape(_n))
    assert _o % _rb == 0
    REP_OFF[_n] = _o
    _o += _rp * _nt
REP_ROWS = _cdiv(_o, 16 * N_DEV) * 16 * N_DEV
REP_CHUNK = REP_ROWS // N_DEV
REP_SLOT = REP_CHUNK

PERM = {'a_w_in': [(0, 2048)], 'd_w_in': [(0, 3072)], 'b_w_in': [(1280, 1024), (0, 1280)],
        'c_w_in': [(1056, 1024), (0, 1056), ('z', 96)],
        'c_w_uq': sum([[(2 * hp * 96, 64), ((2 * hp + 1) * 96, 64), (2 * hp * 96 + 64, 32), ((2 * hp + 1) * 96 + 64, 32),
                        ('z', 64)] for hp in range(8)], []),
        'c_w_ukv': sum([[(2 * hp * 128, 64), ((2 * hp + 1) * 128, 64), (2 * hp * 128 + 64, 64),
                         ((2 * hp + 1) * 128 + 64, 64)] for hp in range(8)], [])}


def perm_index(name):
    return np.concatenate([np.full(p[1], -1) if p[0] == 'z' else np.arange(p[0], p[0] + p[1]) for p in PERM[name]])


SMALL_OFF = {}
_o = 0
for _n in SHARDED_F32:
    SMALL_OFF[_n] = _o
    _o += int(np.prod(_block_shape(_n)))
SMALL_ROWS = _cdiv(_o, 128 * 8) * 8


def _pick(n, cands):
    for c in cands:
        if n % c == 0:
            return c
    return n


def mm(a, b, mode, name, out_dtype=F32, comm=None):
    if mode == 'nn':
        (M, K), (K2, N) = a.shape, b.shape
    elif mode == 'nt':
        (M, K), (N, K2) = a.shape, b.shape
    else:
        (K, M), (K2, N) = a.shape, b.shape
    assert K == K2, (name, a.shape, b.shape)
    tm = _pick(M, (1024, 768, 512, 256, 128))
    tn = _pick(N, (512, 384, 256))
    dims = {'nn': ((1,), (0,)), 'nt': ((1,), (1,)), 'tn': ((0,), (0,))}[mode]

    def body(a_ref, b_ref, o_ref):
        o_ref[...] = lax.dot_general(a_ref[...].astype(BF16), b_ref[...].astype(BF16), (dims, ((), ())),
                                     preferred_element_type=F32).astype(out_dtype)

    a_spec = pl.BlockSpec((K, tm), lambda i, j: (0, i)) if mode == 'tn' else pl.BlockSpec((tm, K), lambda i, j: (i, 0))
    b_spec = pl.BlockSpec((tn, K), lambda i, j: (j, 0)) if mode == 'nt' else pl.BlockSpec((K, tn), lambda i, j: (0, j))
    res = carried(body, comm, grid=(M // tm, N // tn), in_specs=[a_spec, b_spec],
                  out_specs=pl.BlockSpec((tm, tn), lambda i, j: (i, j)), out_shape=jax.ShapeDtypeStruct((M, N), out_dtype),
                  semantics=("parallel", "parallel"), name=name)(a, b)
    return res[0] if comm is None else res


def rw(arr, width=None, cb=0):
    return (arr, arr.shape[1] if width is None else width, cb)


def rowwise(fn, rows, consts, outs, accs, tl, name, n_steps=None, comm=None):
    if n_steps is None:
        n_steps = [r[0].shape[0] for r in rows if not isinstance(r[1], pl.BlockSpec)][0] // tl
    L = n_steps * tl
    nr, nc, no, na = len(rows), len(consts), len(outs), len(accs)
    in_specs, args = [], []
    for r in rows:
        if isinstance(r[1], pl.BlockSpec):
            in_specs.append(r[1])
        else:
            in_specs.append(pl.BlockSpec((tl, r[1]), functools.partial(lambda i, cb: (i, cb), cb=r[2])))
        args.append(r[0])
    for c in consts:
        in_specs.append(pl.BlockSpec(c.shape, functools.partial(lambda i, nd: (0,) * nd, nd=c.ndim)))
        args.append(c)
    out_specs = [pl.BlockSpec((tl, w), lambda i: (i, 0)) for w, _ in outs]
    out_shape = [jax.ShapeDtypeStruct((L, w), dt) for w, dt in outs]
    for s in accs:
        out_specs.append(pl.BlockSpec(s, functools.partial(lambda i, nd: (0,) * nd, nd=len(s))))
        out_shape.append(jax.ShapeDtypeStruct(s, F32))

    def body(*refs):
        ins = [r[...] for r in refs[:nr + nc]]
        o_refs = refs[nr + nc:nr + nc + no]
        a_refs = refs[nr + nc + no:]
        o_vals, a_vals = fn(*ins)
        for ref, val in zip(o_refs, o_vals):
            ref[...] = val.astype(ref.dtype)
        if na:
            @pl.when(pl.program_id(0) == 0)
            def _():
                for ref in a_refs:
                    ref[...] = jnp.zeros_like(ref)
            for ref, val in zip(a_refs, a_vals):
                ref[...] += val

    res, carried_out = carried(body, comm, grid=(n_steps,), in_specs=in_specs, out_specs=out_specs, out_shape=out_shape,
                               name=name, semantics=("arbitrary",))(*args)
    if comm is None:
        return res[:no], res[no:]
    return res[:no], res[no:], carried_out


def carried(body, comm, *, grid, in_specs, out_specs, out_shape, name, semantics, scratch_shapes=()):
    single = not isinstance(out_shape, (list, tuple))
    o_specs = [out_specs] if single else list(out_specs)
    o_shape = [out_shape] if single else list(out_shape)
    if comm is None:
        call = pl.pallas_call(body, grid=grid, in_specs=in_specs, out_specs=out_specs, out_shape=out_shape,
                              scratch_shapes=list(scratch_shapes),
                              compiler_params=pltpu.CompilerParams(dimension_semantics=semantics), name=name)
        return lambda *args: (call(*args), None)
    n_in, n_out, n_sc = len(in_specs), len(o_specs), len(scratch_shapes)
    ci, co = len(comm.ins), len(comm.outs)
    n_steps = int(np.prod(grid))
    hooks = comm.hooks(n_steps)

    def wrapped(*refs):
        ins, cins = refs[:n_in], refs[n_in:n_in + ci]
        outs, couts = refs[n_in + ci:n_in + ci + n_out], refs[n_in + ci + n_out:n_in + ci + n_out + co]
        sc, csc = refs[n_in + ci + n_out + co:n_in + ci + n_out + co + n_sc], refs[n_in + ci + n_out + co + n_sc:]
        step = pl.program_id(0)
        for ax in range(1, len(grid)):
            step = step * grid[ax] + pl.program_id(ax)
        for at, fn, after in hooks:
            if not after:
                pl.when(step == at)(functools.partial(fn, cins, couts, csc))
        body(*ins, *outs, *sc)
        for at, fn, after in hooks:
            if after:
                pl.when(step == at)(functools.partial(fn, cins, couts, csc))

    call = pl.pallas_call(wrapped, grid=grid, in_specs=list(in_specs) + [ANY] * ci, out_specs=o_specs + [ANY] * co,
                          out_shape=o_shape + list(comm.outs), scratch_shapes=list(scratch_shapes) + list(comm.scratch),
                          compiler_params=pltpu.CompilerParams(dimension_semantics=("arbitrary",) * len(grid)), name=name)

    def run(*args):
        res = call(*args, *comm.ins)
        return (res[0] if single else res[:n_out]), res[n_out:]
    return run


_K0 = math.sqrt(2.0 / math.pi)
_K1 = 0.044715


def gelu(x):
    return x * (0.5 * (1.0 + jnp.tanh(_K0 * (x + _K1 * (x * x * x)))))


def gelu_grad(x):
    t = jnp.tanh(_K0 * (x + _K1 * (x * x * x)))
    return 0.5 * (1.0 + t) + 0.5 * x * (1.0 - t * t) * (_K0 * (1.0 + 3.0 * _K1 * x * x))


def sigmoid(x):
    return 1.0 / (1.0 + jnp.exp(-x))


def silu(z):
    return z * sigmoid(z)


def silu_grad(z):
    s = sigmoid(z)
    return s * (1.0 + z * (1.0 - s))


def rms_fwd(x, g):
    r = lax.rsqrt(jnp.mean(x * x, axis=-1, keepdims=True) + EPS)
    return x * r * g


def rms_bwd(x, g, dy):
    r = lax.rsqrt(jnp.mean(x * x, axis=-1, keepdims=True) + EPS)
    xh = x * r
    dg = jnp.sum(dy * xh, axis=0, keepdims=True)
    dxh = dy * g
    dx = r * (dxh - xh * jnp.mean(dxh * xh, axis=-1, keepdims=True))
    return dx, dg


def _scan_chunk(a_r, a_i, pr_ref, pi_ref, cr, ci, T, reverse):
    ts = min(SSM_TS, T)
    sgn = -1.0 if reverse else 1.0
    row = lax.broadcasted_iota(jnp.int32, (ts, a_r.shape[1]), 0)
    pw = (lambda e: T - e) if reverse else (lambda e: e - 1)
    if reverse:
        wr_c, wi_c = pr_ref[T - ts:T, :], sgn * pi_ref[T - ts:T, :]
    else:
        wr_c, wi_c = pr_ref[0:ts, :], sgn * pi_ref[0:ts, :]
    c_r, c_i = cr[...], ci[...]
    outs = []
    subs = range(T // ts)
    for sub in (reversed(subs) if reverse else subs):
        v_r, v_i = a_r[sub * ts:(sub + 1) * ts], a_i[sub * ts:(sub + 1) * ts]
        d = 1
        while d < ts:
            wr = pr_ref[pw(d):pw(d) + 1, :]
            wi = sgn * pi_ref[pw(d):pw(d) + 1, :]
            if reverse:
                yr, yi, keep = pltpu.roll(v_r, ts - d, 0), pltpu.roll(v_i, ts - d, 0), row < ts - d
            else:
                yr, yi, keep = pltpu.roll(v_r, d, 0), pltpu.roll(v_i, d, 0), row >= d
            v_r, v_i = (v_r + jnp.where(keep, wr * yr - wi * yi, 0.0), v_i + jnp.where(keep, wr * yi + wi * yr, 0.0))
            d *= 2
        v_r, v_i = v_r + (wr_c * c_r - wi_c * c_i), v_i + (wr_c * c_i + wi_c * c_r)
        k = 0 if reverse else ts - 1
        c_r, c_i = v_r[k:k + 1, :], v_i[k:k + 1, :]
        outs.append((v_r, v_i))
    if reverse:
        outs = outs[::-1]
    cr[...] = c_r
    ci[...] = c_i
    return jnp.concatenate([o[0] for o in outs], axis=0), jnp.concatenate([o[1] for o in outs], axis=0)


_NT = (((1,), (1,)), ((), ()))
_TN = (((0,), (0,)), ((), ()))


def s5_fwd(proj, d_skip, Bre, Bim, Cre, Cim, pr, pi, comm=None):
    L = proj.shape[0]
    T, WC = min(SSM_T, L), SSM_WC
    nT = L // T

    def body(u_ref, d_ref, bre_ref, bim_ref, cre_ref, cim_ref, pr_ref, pi_ref, y_ref, yg_ref, sr_ref, si_ref, cr, ci):
        @pl.when(pl.program_id(1) == 0)
        def _():
            cr[...] = jnp.zeros_like(cr)
            ci[...] = jnp.zeros_like(ci)

        u = u_ref[...]
        ub = u.astype(BF16)
        a_r = lax.dot_general(ub, bre_ref[0].astype(BF16), _NT, preferred_element_type=F32)
        a_i = lax.dot_general(ub, bim_ref[0].astype(BF16), _NT, preferred_element_type=F32)
        a_r, a_i = _scan_chunk(a_r, a_i, pr_ref, pi_ref, cr, ci, T, False)
        sr_ref[...] = a_r
        si_ref[...] = a_i
        y = (lax.dot_general(a_r.astype(BF16), cre_ref[0].astype(BF16), _NT, preferred_element_type=F32)
             + lax.dot_general(a_i.astype(BF16), cim_ref[0].astype(BF16), _NT, preferred_element_type=F32)
             + d_ref[...] * u)
        y_ref[...] = y
        yg_ref[...] = gelu(y)

    uspec = pl.BlockSpec((T, 128), lambda k, i: (i, k))
    sspec = pl.BlockSpec((T, WC), lambda k, i: (i, k))
    return carried(
        body, comm, grid=(8, nT),
        in_specs=[uspec, pl.BlockSpec((1, 128), lambda k, i: (0, k)),
                  pl.BlockSpec((1, WC, 128), lambda k, i: (k, 0, 0)), pl.BlockSpec((1, WC, 128), lambda k, i: (k, 0, 0)),
                  pl.BlockSpec((1, 128, WC), lambda k, i: (k, 0, 0)), pl.BlockSpec((1, 128, WC), lambda k, i: (k, 0, 0)),
                  pl.BlockSpec((T, WC), lambda k, i: (0, k)), pl.BlockSpec((T, WC), lambda k, i: (0, k))],
        out_specs=[uspec, uspec, sspec, sspec],
        out_shape=[jax.ShapeDtypeStruct((L, 1024), F32)] * 2 + [jax.ShapeDtypeStruct((L, 8 * WC), F32)] * 2,
        scratch_shapes=[pltpu.VMEM((1, WC), F32), pltpu.VMEM((1, WC), F32)],
        semantics=("parallel", "arbitrary"), name='a_ssm')(proj, d_skip, Bre, Bim, Cre, Cim, pr, pi)


def s5_bwd(proj, dyg1, dyg2, y, d_skip, s_re, s_im, Bre, Bim, Cre, Cim, prr, pir, comm=None):
    L = proj.shape[0]
    T, WC = min(SSM_T, L), SSM_WC
    nT = L // T

    def body(u_ref, g1_ref, g2_ref, y_ref, d_ref, sr_ref, si_ref, spr_ref, spi_ref, bre_ref, bim_ref, cre_ref, cim_ref,
             pr_ref, pi_ref, du_ref, dd_ref, dbre_ref, dbim_ref, dcre_ref, dcim_ref, dar_ref, dai_ref, cr, ci):
        i = pl.program_id(1)

        @pl.when(i == 0)
        def _():
            for ref in (cr, ci, dd_ref, dbre_ref, dbim_ref, dcre_ref, dcim_ref, dar_ref, dai_ref):
                ref[...] = jnp.zeros_like(ref)

        u = u_ref[...]
        dy = (g1_ref[...] + g2_ref[...]) * gelu_grad(y_ref[...])
        dd_ref[...] += jnp.sum(dy * u, axis=0, keepdims=True)
        dyb, ub = dy.astype(BF16), u.astype(BF16)
        bre, bim, cre, cim = (r[0].astype(BF16) for r in (bre_ref, bim_ref, cre_ref, cim_ref))
        g_r = jnp.dot(dyb, cre, preferred_element_type=F32)
        g_i = jnp.dot(dyb, cim, preferred_element_type=F32)
        g_r, g_i = _scan_chunk(g_r, g_i, pr_ref, pi_ref, cr, ci, T, True)
        s_r, s_i = sr_ref[...], si_ref[...]
        row = lax.broadcasted_iota(jnp.int32, (T, WC), 0)
        first = (nT - 1 - i) == 0
        sp_r = jnp.where(row == 0, jnp.where(first, 0.0, spr_ref[7:8, :]), pltpu.roll(s_r, 1, 0))
        sp_i = jnp.where(row == 0, jnp.where(first, 0.0, spi_ref[7:8, :]), pltpu.roll(s_i, 1, 0))
        dar_ref[...] += jnp.sum(g_r * sp_r + g_i * sp_i, axis=0, keepdims=True)
        dai_ref[...] += jnp.sum(g_i * sp_r - g_r * sp_i, axis=0, keepdims=True)
        grb, gib = g_r.astype(BF16), g_i.astype(BF16)
        dcre_ref[0] += lax.dot_general(dyb, s_r.astype(BF16), _TN, preferred_element_type=F32)
        dcim_ref[0] += lax.dot_general(dyb, s_i.astype(BF16), _TN, preferred_element_type=F32)
        dbre_ref[0] += lax.dot_general(grb, ub, _TN, preferred_element_type=F32)
        dbim_ref[0] += lax.dot_general(gib, ub, _TN, preferred_element_type=F32)
        du_ref[...] = (dy * d_ref[...] + jnp.dot(grb, bre, preferred_element_type=F32)
                       + jnp.dot(gib, bim, preferred_element_type=F32))

    uspec = pl.BlockSpec((T, 128), lambda k, i: (nT - 1 - i, k))
    sspec = pl.BlockSpec((T, WC), lambda k, i: (nT - 1 - i, k))
    pspec = pl.BlockSpec((8, WC), lambda k, i: (jnp.maximum((nT - 1 - i) * (T // 8) - 1, 0), k))
    tab = pl.BlockSpec((T, WC), lambda k, i: (0, k))
    bspec = pl.BlockSpec((1, WC, 128), lambda k, i: (k, 0, 0))
    cspec = pl.BlockSpec((1, 128, WC), lambda k, i: (k, 0, 0))
    return carried(
        body, comm, grid=(8, nT),
        in_specs=[uspec, uspec, uspec, uspec, pl.BlockSpec((1, 128), lambda k, i: (0, k)), sspec, sspec, pspec, pspec,
                  bspec, bspec, cspec, cspec, tab, tab],
        out_specs=[uspec, pl.BlockSpec((1, 128), lambda k, i: (0, k)), bspec, bspec, cspec, cspec,
                   pl.BlockSpec((1, WC), lambda k, i: (0, k)), pl.BlockSpec((1, WC), lambda k, i: (0, k))],
        out_shape=[jax.ShapeDtypeStruct((L, 1024), F32), jax.ShapeDtypeStruct((1, 1024), F32),
                   jax.ShapeDtypeStruct((8, WC, 128), F32), jax.ShapeDtypeStruct((8, WC, 128), F32),
                   jax.ShapeDtypeStruct((8, 128, WC), F32), jax.ShapeDtypeStruct((8, 128, WC), F32),
                   jax.ShapeDtypeStruct((1, 8 * WC), F32), jax.ShapeDtypeStruct((1, 8 * WC), F32)],
        scratch_shapes=[pltpu.VMEM((1, WC), F32), pltpu.VMEM((1, WC), F32)],
        semantics=("parallel", "arbitrary"), name='a_ssm_bwd')(
            proj, dyg1, dyg2, y, d_skip, s_re, s_im, s_re, s_im, Bre, Bim, Cre, Cim, prr, pir)


def s5_discretize(lam_re, lam_im, log_dt, b_re, b_im):
    dt = jnp.exp(log_dt)[:, None]
    mag = jnp.exp(lam_re * dt)
    ab_re = mag * jnp.cos(lam_im * dt)
    ab_im = mag * jnp.sin(lam_im * dt)
    den = lam_re * lam_re + lam_im * lam_im
    nr = ab_re - 1.0
    f_re = (nr * lam_re + ab_im * lam_im) / den
    f_im = (ab_im * lam_re - nr * lam_im) / den
    bb_re = f_re[..., None] * b_re - f_im[..., None] * b_im
    bb_im = f_re[..., None] * b_im + f_im[..., None] * b_re
    return ab_re, ab_im, bb_re, bb_im


def s5_prep(bb_re, bb_im, c_re, c_im, ar, ai, T, comm=None):
    W = ar.shape[1]

    def body(bbr_ref, bbi_ref, cre_ref, cim_ref, ar_ref, ai_ref, btr_ref, bti_ref, ctr_ref, cti_ref, fr_ref, fi_ref,
             rr_ref, ri_ref):
        for ref in (btr_ref, bti_ref, ctr_ref, cti_ref):
            ref[...] = jnp.zeros_like(ref)
        for g in range(8):
            rows, cols = slice(g * SSM_P, (g + 1) * SSM_P), slice(g * SSM_H, (g + 1) * SSM_H)
            btr_ref[0, rows, cols] = bbr_ref[g]
            bti_ref[0, rows, cols] = bbi_ref[g]
            ctr_ref[0, cols, rows] = cre_ref[g]
            cti_ref[0, cols, rows] = -cim_ref[g]
        fr_ref[0:1, :] = ar_ref[...]
        fi_ref[0:1, :] = ai_ref[...]
        rr_ref[T - 1:T, :] = ar_ref[...]
        ri_ref[T - 1:T, :] = ai_ref[...]
        n = 1
        while n < T:
            cr, ci = fr_ref[0:n, :], fi_ref[0:n, :]
            lr, li = fr_ref[n - 1:n, :], fi_ref[n - 1:n, :]
            fr_ref[n:2 * n, :] = cr * lr - ci * li
            fi_ref[n:2 * n, :] = cr * li + ci * lr
            cr, ci = rr_ref[T - n:T, :], ri_ref[T - n:T, :]
            rr_ref[T - 2 * n:T - n, :] = cr * lr - ci * li
            ri_ref[T - 2 * n:T - n, :] = cr * li + ci * lr
            n *= 2

    spec = pl.BlockSpec((T, SSM_WC), lambda j: (0, j))
    aspec = pl.BlockSpec((1, SSM_WC), lambda j: (0, j))
    bspec, cspec = pl.BlockSpec((8, SSM_P, SSM_H), lambda j: (j, 0, 0)), pl.BlockSpec((8, SSM_H, SSM_P), lambda j: (j, 0, 0))
    btspec = pl.BlockSpec((1, SSM_WC, 128), lambda j: (j, 0, 0))
    ctspec = pl.BlockSpec((1, 128, SSM_WC), lambda j: (j, 0, 0))
    return carried(
        body, comm, grid=(W // SSM_WC,), in_specs=[bspec, bspec, cspec, cspec, aspec, aspec],
        out_specs=[btspec, btspec, ctspec, ctspec] + [spec] * 4,
        out_shape=[jax.ShapeDtypeStruct((8, SSM_WC, 128), F32)] * 2 + [jax.ShapeDtypeStruct((8, 128, SSM_WC), F32)] * 2
        + [jax.ShapeDtypeStruct((T, W), F32)] * 4,
        semantics=("parallel",), name='a_prep')(bb_re, bb_im, c_re, c_im, ar, ai)


def s5_untile(dbtr, dbti, dctr, dcti):
    def body(dbtr_ref, dbti_ref, dctr_ref, dcti_ref, br_ref, bi_ref, cr_ref, ci_ref):
        for g in range(8):
            rows, cols = slice(g * SSM_P, (g + 1) * SSM_P), slice(g * SSM_H, (g + 1) * SSM_H)
            br_ref[g] = dbtr_ref[0, rows, cols]
            bi_ref[g] = dbti_ref[0, rows, cols]
            cr_ref[g] = dctr_ref[0, cols, rows]
            ci_ref[g] = -dcti_ref[0, cols, rows]

    bspec, cspec = pl.BlockSpec((8, SSM_P, SSM_H), lambda j: (j, 0, 0)), pl.BlockSpec((8, SSM_H, SSM_P), lambda j: (j, 0, 0))
    btspec = pl.BlockSpec((1, SSM_WC, 128), lambda j: (j, 0, 0))
    ctspec = pl.BlockSpec((1, 128, SSM_WC), lambda j: (j, 0, 0))
    return pl.pallas_call(
        body, grid=(8,), in_specs=[btspec, btspec, ctspec, ctspec], out_specs=[bspec, bspec, cspec, cspec],
        out_shape=[jax.ShapeDtypeStruct((SSM_G, SSM_P, SSM_H), F32)] * 2 + [jax.ShapeDtypeStruct((SSM_G, SSM_H, SSM_P), F32)] * 2,
        compiler_params=pltpu.CompilerParams(dimension_semantics=("parallel",)), name='a_untile')(dbtr, dbti, dctr, dcti)


def layer_a_fwd(h, w, p, comm=None, on_carried=None, prep_comm=None, on_prep=None):
    L = h.shape[0]
    disc = lambda *a: s5_discretize(*a)
    (ab_re, ab_im, bb_re, bb_im), disc_vjp = jax.vjp(disc, p['a_lam_re'][0], p['a_lam_im'][0], p['a_log_dt'][0],
                                                     p['a_b_re'][0], p['a_b_im'][0])
    T = min(SSM_T, L)
    (Bre, Bim, Cre, Cim, pr, pi, prr, pir), prepped = s5_prep(bb_re, bb_im, p['a_c_re'][0], p['a_c_im'][0],
                                                              ab_re.reshape(1, -1), ab_im.reshape(1, -1), T,
                                                              comm=prep_comm)
    if on_prep is not None:
        on_prep(prepped)
    proj = mm(h, w['a_w_in'], 'nn', 'a_proj')
    (y, yg, s_re, s_im), carried_out = s5_fwd(proj, p['a_d'], Bre, Bim, Cre, Cim, pr, pi, comm=comm)
    if on_carried is not None:
        on_carried(carried_out)
    gl = mm(yg, w['a_w_glu'], 'nn', 'a_glu')

    def f2(yg_, gl_, z, bg):
        return [yg_ * sigmoid(gl_ + bg) * silu(z)], []
    (po,), _ = rowwise(f2, [rw(yg), rw(gl), rw(proj, 1024, 1)], [p['a_b_glu']], [(1024, BF16)], [], 256, 'a_gate')
    yb = mm(po, w['a_w_out'], 'nn', 'a_out')
    saved = dict(carried=carried_out, h=h, proj=proj, disc_vjp=disc_vjp, Bre=Bre, Bim=Bim, Cre=Cre, Cim=Cim, prr=prr, pir=pir, s_re=s_re,
                 s_im=s_im, y=y, yg=yg, gl=gl, po=po)
    return yb, saved


def _dw(g, sink, name, a, b, mm_name):
    if sink is None:
        g[name] = mm(a, b, 'tn', mm_name)
    else:
        sink.put(name, a, b, mm_name)


def layer_a_bwd(dyb, w, p, sv, comm=None, sink=None):
    g = {}
    dpo = mm(dyb, w['a_w_out'], 'nt', 'a_dpo')
    _dw(g, sink, 'a_w_out', sv['po'], dyb, 'a_dwout')
    proj = sv['proj']

    def f1(dpo_, yg, gl, z, bg):
        sg = sigmoid(gl + bg)
        sz = silu(z)
        dm = dpo_ * sz
        dz = dpo_ * (yg * sg) * silu_grad(z)
        dgl = dm * yg * sg * (1.0 - sg)
        return [dz, dm * sg, dgl], [jnp.sum(dgl, axis=0, keepdims=True)]
    (dz, dyg1, dgl), (db_glu,) = rowwise(f1, [rw(dpo), rw(sv['yg']), rw(sv['gl']), rw(proj, 1024, 1)], [p['a_b_glu']],
                                          [(1024, F32), (1024, F32), (1024, BF16)], [(1, 1024)], 256, 'a_gate_bwd')
    g['a_b_glu'] = db_glu
    _dw(g, sink, 'a_w_glu', sv['yg'], dgl, 'a_dwglu')
    dyg2 = mm(dgl, w['a_w_glu'], 'nt', 'a_dyg2')

    if callable(comm):
        comm = comm()
    (du, dd, dBre, dBim, dCre, dCim, da_re, da_im), g['carried'] = s5_bwd(
        proj, dyg1, dyg2, sv['y'], p['a_d'], sv['s_re'], sv['s_im'], sv['Bre'], sv['Bim'], sv['Cre'], sv['Cim'],
        sv['prr'], sv['pir'], comm=comm)
    g['a_d'] = dd

    def f3(du_, dz_):
        return [jnp.concatenate([du_, dz_], axis=1)], []
    (dproj,), _ = rowwise(f3, [rw(du), rw(dz)], [], [(2048, BF16)], [], 256, 'a_dproj')
    dbb_re, dbb_im, dc_re, dc_im = s5_untile(dBre, dBim, dCre, dCim)
    dlr, dli, dldt, dbr, dbi = sv['disc_vjp']((da_re.reshape(SSM_G, SSM_P), da_im.reshape(SSM_G, SSM_P), dbb_re, dbb_im))
    g['a_lam_re'], g['a_lam_im'], g['a_log_dt'] = dlr[None], dli[None], dldt[None]
    g['a_b_re'], g['a_b_im'] = dbr[None], dbi[None]
    g['a_c_re'], g['a_c_im'] = dc_re[None], dc_im[None]
    _dw(g, sink, 'a_w_in', sv['h'], dproj, 'a_dwin')
    if sink is None:
        dh = mm(dproj, w['a_w_in'], 'nt', 'a_dh')
    else:
        dh, (g['land_a1'],) = mm(dproj, w['a_w_in'], 'nt', 'a_dh', comm=SiblingExchange(sink.bufs['a1']))
    return dh, g


def _t5_bucket_np():
    qi = np.arange(WINDOW)[:, None]
    kj = np.arange(2 * WINDOW)[None, :]
    dist = np.maximum(qi + WINDOW - kj, 0)
    max_exact = REL_BUCKETS // 2
    dist_f = np.maximum(dist, 1).astype(np.float32)
    large = max_exact + (np.log(dist_f / np.float32(max_exact)) / np.float32(math.log(REL_MAX_DIST / max_exact))
                         * np.float32(REL_BUCKETS - max_exact)).astype(np.int32)
    large = np.minimum(large, REL_BUCKETS - 1)
    return np.where(dist < max_exact, dist, large).astype(np.int32)


SWA_GRP = SWA_HEADS // SWA_KV


def _swa_kv(kvp, kvc, kvh):
    kb = jnp.concatenate([kvp[:, kvh * 64:(kvh + 1) * 64], kvc[:, kvh * 64:(kvh + 1) * 64]], 0).astype(BF16)
    vb = jnp.concatenate([kvp[:, 128 + kvh * 64:128 + (kvh + 1) * 64], kvc[:, 128 + kvh * 64:128 + (kvh + 1) * 64]],
                         0).astype(BF16)
    return kb, vb


def _swa_stack(x, kvh):
    return jnp.concatenate([x[:, (kvh * SWA_GRP + g) * 64:(kvh * SWA_GRP + g + 1) * 64] for g in range(SWA_GRP)],
                           axis=0).astype(BF16)


def _swa_group(bias_ref, kvh):
    return bias_ref[kvh * SWA_GRP:(kvh + 1) * SWA_GRP].reshape(SWA_GRP * WINDOW, 2 * WINDOW)


def _swa_sinks(sink_ref, kvh):
    return jnp.concatenate([jnp.broadcast_to(sink_ref[0:1, kvh * SWA_GRP + g:kvh * SWA_GRP + g + 1], (WINDOW, 1))
                            for g in range(SWA_GRP)], axis=0)


def _swa_probs(q, kb, bias_h, sink, valid):
    s = lax.dot_general(q, kb, (((1,), (1,)), ((), ())), preferred_element_type=F32) * (HEAD_DIM ** -0.5)
    s = jnp.where(valid, s + bias_h, NEG_INF)
    m = jnp.maximum(jnp.max(s, axis=-1, keepdims=True), sink)
    e = jnp.exp(s - m)
    es = jnp.exp(sink - m)
    den = jnp.sum(e, axis=-1, keepdims=True) + es
    return e / den, es / den


def _swa_valid(n):
    qi = lax.broadcasted_iota(jnp.int32, (SWA_GRP * WINDOW, 2 * WINDOW), 0) & (WINDOW - 1)
    kj = lax.broadcasted_iota(jnp.int32, (SWA_GRP * WINDOW, 2 * WINDOW), 1)
    dist = qi + WINDOW - kj
    return (dist >= 0) & (dist < WINDOW) & ((kj >= WINDOW) | (n > 0))


def swa_fwd(proj, bias, sinks, comm=None):
    L = proj.shape[0]

    def body(z_ref, q_ref, kvc_ref, kvp_ref, bias_ref, sink_ref, o_ref, po_ref):
        n = pl.program_id(0)
        valid = _swa_valid(n)
        q, kvc, kvp = q_ref[...], kvc_ref[...], kvp_ref[...]
        outs = []
        for kvh in range(SWA_KV):
            kb, vb = _swa_kv(kvp, kvc, kvh)
            p, _ = _swa_probs(_swa_stack(q, kvh), kb, _swa_group(bias_ref, kvh), _swa_sinks(sink_ref, kvh), valid)
            o8 = jnp.dot(p.astype(BF16), vb, preferred_element_type=F32)
            outs += [o8[g * WINDOW:(g + 1) * WINDOW] for g in range(SWA_GRP)]
        o = jnp.concatenate(outs, axis=1)
        o_ref[...] = o
        po_ref[...] = (o * silu(z_ref[...])).astype(po_ref.dtype)

    return carried(
        body, comm, grid=(L // WINDOW,),
        in_specs=[pl.BlockSpec((WINDOW, 1024), lambda n: (n, 0)), pl.BlockSpec((WINDOW, 1024), lambda n: (n, 1)),
                  pl.BlockSpec((WINDOW, 256), lambda n: (n, 8)),
                  pl.BlockSpec((WINDOW, 256), lambda n: (jnp.maximum(n - 1, 0), 8)),
                  pl.BlockSpec((SWA_HEADS, WINDOW, 2 * WINDOW), lambda n: (0, 0, 0)),
                  pl.BlockSpec((1, SWA_HEADS), lambda n: (0, 0))],
        out_specs=[pl.BlockSpec((WINDOW, 1024), lambda n: (n, 0))] * 2,
        out_shape=[jax.ShapeDtypeStruct((L, 1024), F32), jax.ShapeDtypeStruct((L, 1024), BF16)],
        semantics=("parallel",), name='b_attn')(proj, proj, proj, proj, bias, sinks)


def swa_bwd(proj, do, bias, sinks, comm=None):
    L = proj.shape[0]

    def body(q_ref, kvc_ref, kvp_ref, do_ref, bias_ref, sink_ref, dq_ref, dkv_ref, dbias_ref, dsink_ref):
        n = pl.program_id(0)

        @pl.when(n == 0)
        def _():
            dkv_ref[...] = jnp.zeros_like(dkv_ref)
            dbias_ref[...] = jnp.zeros_like(dbias_ref)
            dsink_ref[...] = jnp.zeros_like(dsink_ref)

        valid = _swa_valid(n)
        q, kvc, kvp, do_ = q_ref[...], kvc_ref[...], kvp_ref[...], do_ref[...]
        dqs, dks, dvs, dsk = [], [], [], []
        for kvh in range(SWA_KV):
            kb, vb = _swa_kv(kvp, kvc, kvh)
            q8, do8 = _swa_stack(q, kvh), _swa_stack(do_, kvh)
            p, ps = _swa_probs(q8, kb, _swa_group(bias_ref, kvh), _swa_sinks(sink_ref, kvh), valid)
            dp = lax.dot_general(do8, vb, (((1,), (1,)), ((), ())), preferred_element_type=F32)
            delta = jnp.sum(p * dp, axis=-1, keepdims=True)
            ds = p * (dp - delta)
            col = -ps * delta
            dsk += [jnp.sum(col[g * WINDOW:(g + 1) * WINDOW], axis=0, keepdims=True) for g in range(SWA_GRP)]
            dbias_ref[kvh * SWA_GRP:(kvh + 1) * SWA_GRP] += ds.reshape(SWA_GRP, WINDOW, 2 * WINDOW)
            dsb = (ds * (HEAD_DIM ** -0.5)).astype(BF16)
            dq8 = jnp.dot(dsb, kb, preferred_element_type=F32)
            dqs += [dq8[g * WINDOW:(g + 1) * WINDOW] for g in range(SWA_GRP)]
            dks.append(lax.dot_general(dsb, q8, (((0,), (0,)), ((), ())), preferred_element_type=F32))
            dvs.append(lax.dot_general(p.astype(BF16), do8, (((0,), (0,)), ((), ())), preferred_element_type=F32))
        dq_ref[...] = jnp.concatenate(dqs, axis=1)
        dsink_ref[...] += jnp.concatenate(dsk, axis=1)
        both = jnp.concatenate(dks + dvs, axis=1)
        r_cur = pl.multiple_of(n * WINDOW, WINDOW)
        r_prev = pl.multiple_of(jnp.maximum(n - 1, 0) * WINDOW, WINDOW)
        dkv_ref[pl.ds(r_prev, WINDOW), :] += both[:WINDOW]
        dkv_ref[pl.ds(r_cur, WINDOW), :] += both[WINDOW:]

    return carried(
        body, comm, grid=(L // WINDOW,),
        in_specs=[pl.BlockSpec((WINDOW, 1024), lambda n: (n, 1)), pl.BlockSpec((WINDOW, 256), lambda n: (n, 8)),
                  pl.BlockSpec((WINDOW, 256), lambda n: (jnp.maximum(n - 1, 0), 8)),
                  pl.BlockSpec((WINDOW, 1024), lambda n: (n, 0)),
                  pl.BlockSpec((SWA_HEADS, WINDOW, 2 * WINDOW), lambda n: (0, 0, 0)),
                  pl.BlockSpec((1, SWA_HEADS), lambda n: (0, 0))],
        out_specs=[pl.BlockSpec((WINDOW, 1024), lambda n: (n, 0)), pl.BlockSpec((L, 256), lambda n: (0, 0)),
                   pl.BlockSpec((SWA_HEADS, WINDOW, 2 * WINDOW), lambda n: (0, 0, 0)),
                   pl.BlockSpec((1, SWA_HEADS), lambda n: (0, 0))],
        out_shape=[jax.ShapeDtypeStruct((L, 1024), F32), jax.ShapeDtypeStruct((L, 256), F32),
                   jax.ShapeDtypeStruct((SWA_HEADS, WINDOW, 2 * WINDOW), F32), jax.ShapeDtypeStruct((1, SWA_HEADS), F32)],
        semantics=("arbitrary",), name='b_attn_bwd')(proj, proj, proj, do, bias, sinks)


def swa_bias(rel_bias):
    def body(bk_ref, rb_ref, o_ref):
        bk = bk_ref[...]
        for h in range(SWA_HEADS):
            acc = jnp.zeros((WINDOW, 2 * WINDOW), F32)
            for b in range(REL_BUCKETS):
                acc = jnp.where(bk == b, rb_ref[b, h], acc)
            o_ref[h] = acc

    return pl.pallas_call(
        body, out_shape=jax.ShapeDtypeStruct((SWA_HEADS, WINDOW, 2 * WINDOW), F32),
        in_specs=[pl.BlockSpec(memory_space=pltpu.VMEM), pl.BlockSpec(memory_space=pltpu.SMEM)],
        out_specs=pl.BlockSpec(memory_space=pltpu.VMEM), name='b_bias')(jnp.asarray(_t5_bucket_np()), rel_bias)


def layer_b_fwd(h, w, p, comm=None):
    proj = mm(h, w['b_w_in'], 'nn', 'b_proj')
    bias = swa_bias(p['rel_bias'])
    (o, po), carried_out = swa_fwd(proj, bias, p['b_sinks'], comm=comm)
    yb = mm(po, w['b_w_out'], 'nn', 'b_out')
    return yb, dict(carried=carried_out, h=h, proj=proj, bias=bias, o=o, po=po)


def layer_b_bwd(dyb, w, p, sv, comm=None, sink=None):
    g = {}
    dpo = mm(dyb, w['b_w_out'], 'nt', 'b_dpo')
    _dw(g, sink, 'b_w_out', sv['po'], dyb, 'b_dwout')
    proj = sv['proj']

    def f1(dpo_, o, z):
        return [dpo_ * silu(z), dpo_ * o * silu_grad(z)], []
    (do, dz), _ = rowwise(f1, [rw(dpo), rw(sv['o']), rw(proj, 1024, 0)], [], [(1024, BF16), (1024, F32)], [], 256, 'b_gate_bwd')
    (dq, dkv, dbias, dsinks), g['carried'] = swa_bwd(proj, do, sv['bias'], p['b_sinks'], comm=comm)
    g['b_sinks'] = dsinks
    onehot = jnp.asarray(np.eye(REL_BUCKETS, dtype=np.float32)[_t5_bucket_np().reshape(-1)])

    def f2(db, oh):
        return [], [lax.dot_general(db, oh, (((1,), (0,)), ((), ())), preferred_element_type=F32,
                                    precision=lax.Precision.HIGHEST)]
    _, (drel,) = rowwise(f2, [(dbias.reshape(SWA_HEADS, -1), pl.BlockSpec((SWA_HEADS, 4096), lambda i: (0, i))),
                              (onehot, pl.BlockSpec((4096, REL_BUCKETS), lambda i: (i, 0)))], [], [],
                         [(SWA_HEADS, REL_BUCKETS)], 4096, 'b_drel', n_steps=(2 * WINDOW * WINDOW) // 4096)
    g['rel_bias'] = drel.T

    def f3(dz_, dq_, dkv_):
        return [jnp.concatenate([dz_, dq_, dkv_], axis=1)], []
    (dproj,), _ = rowwise(f3, [rw(dz), rw(dq), rw(dkv)], [], [(2304, BF16)], [], 256, 'b_dproj')
    _dw(g, sink, 'b_w_in', sv['h'], dproj, 'b_dwin')
    dh = mm(dproj, w['b_w_in'], 'nt', 'b_dh')
    return dh, g


MLA_SCALE = (MLA_NOPE + MLA_ROPE) ** -0.5
_LOG2E = math.log2(math.e)


def _rope_tables(L):
    inv = ROPE_BASE ** (-jnp.arange(0, MLA_ROPE, 2, dtype=F32) / MLA_ROPE)
    ang = jnp.arange(L, dtype=F32)[:, None] * inv[None, :]
    c, s = jnp.cos(ang), jnp.sin(ang)
    one, zero, pad = jnp.ones((L, 128), F32), jnp.zeros((L, 128), F32), jnp.zeros((L, 64), F32)
    return (jnp.concatenate([one, c, c, c, c, pad], 1), jnp.concatenate([zero, s, s, s, s, pad], 1))


def _rot(x, transpose=False):
    w = x.shape[1]
    lane = lax.broadcasted_iota(jnp.int32, x.shape, 1)
    up = pltpu.roll(x, w - 16, 1)
    dn = pltpu.roll(x, 16, 1)
    first = (lane % 32) < 16
    return jnp.where(first, up, -dn) if transpose else jnp.where(first, -up, dn)


MLA_QT = 512


def _mla_exp(qf, kf, t, qt):
    s = lax.dot_general(qf, kf, (((1,), (1,)), ((), ())), preferred_element_type=F32)
    causal = lax.broadcasted_iota(jnp.int32, (qt, qt), 1) <= lax.broadcasted_iota(jnp.int32, (qt, qt), 0)
    last = jnp.where(causal, s[:, t * qt:], NEG_INF)
    s = last if t == 0 else jnp.concatenate([s[:, :t * qt], last], axis=1)
    e = jnp.exp2((s - jnp.max(s, axis=-1, keepdims=True)) * (MLA_SCALE * _LOG2E))
    return e, jnp.sum(e, axis=-1, keepdims=True)


def _mla_heads(q, kv, kr):
    out = []
    for j in range(2):
        qf = jnp.concatenate([q[:, j * 64:(j + 1) * 64], q[:, 128 + j * 32:128 + (j + 1) * 32]], axis=1)
        kf = jnp.concatenate([kv[:, j * 64:(j + 1) * 64], kr], axis=1)
        out.append((qf, kf, kv[:, 128 + j * 64:128 + (j + 1) * 64]))
    return out


def mla_fwd(q, kv, kr, comm=None):
    L = q.shape[0]
    qt = min(MLA_QT, L)
    nq = L // qt

    def body(q_ref, kv_ref, kr_ref, o_ref):
        for t in range(nq):
            @pl.when(pl.program_id(1) == t)
            def _(t=t):
                n_k = (t + 1) * qt
                outs = []
                for qf, kf, v in _mla_heads(q_ref[...], kv_ref[0:n_k, :], kr_ref[0:n_k, 0:MLA_ROPE]):
                    e, den = _mla_exp(qf, kf, t, qt)
                    outs.append(jnp.dot(e.astype(BF16), v, preferred_element_type=F32) / den)
                o_ref[...] = jnp.concatenate(outs, axis=1)

    return carried(
        body, comm, grid=(MLA_HEADS // 2, nq),
        in_specs=[pl.BlockSpec((qt, 256), lambda hp, n: (n, hp)), pl.BlockSpec((L, 256), lambda hp, n: (0, hp)),
                  pl.BlockSpec((L, 128), lambda hp, n: (0, 0))],
        out_specs=pl.BlockSpec((qt, 128), lambda hp, n: (n, hp)), out_shape=jax.ShapeDtypeStruct((L, 1024), F32),
        semantics=("parallel", "parallel"), name='c_attn')(q, kv, kr)


def mla_bwd(q, kv, kr, do, o, comm=None):
    L = q.shape[0]
    qt = min(MLA_QT, L)
    nq = L // qt

    def body(q_ref, kv_ref, kr_ref, do_ref, o_ref, dq_ref, dkv_ref, dkr_ref):
        @pl.when(pl.program_id(1) == 0)
        def _():
            dkv_ref[...] = jnp.zeros_like(dkv_ref)
            dkr_ref[...] = jnp.zeros_like(dkr_ref)

        for t in range(nq):
            @pl.when(pl.program_id(1) == t)
            def _(t=t):
                n_k = (t + 1) * qt
                do_, o_ = do_ref[...], o_ref[...]
                dqn, dqr, dkn, dvs = [], [], [], []
                dkr = jnp.zeros((n_k, MLA_ROPE), F32)
                for j, (qf, kf, v) in enumerate(_mla_heads(q_ref[...], kv_ref[0:n_k, :], kr_ref[0:n_k, 0:MLA_ROPE])):
                    doh = do_[:, j * 64:(j + 1) * 64]
                    dof = doh.astype(F32)
                    e, den = _mla_exp(qf, kf, t, qt)
                    inv = 1.0 / den
                    dp = lax.dot_general(doh, v, (((1,), (1,)), ((), ())), preferred_element_type=F32)
                    delta = jnp.sum(dof * o_[:, j * 64:(j + 1) * 64], axis=-1, keepdims=True)
                    ds = (e * ((dp - delta) * (inv * MLA_SCALE))).astype(BF16)
                    dqf = jnp.dot(ds, kf, preferred_element_type=F32)
                    dkf = lax.dot_general(ds, qf, (((0,), (0,)), ((), ())), preferred_element_type=F32)
                    dvs.append(lax.dot_general(e.astype(BF16), (dof * inv).astype(BF16), (((0,), (0,)), ((), ())),
                                               preferred_element_type=F32))
                    dqn.append(dqf[:, :MLA_NOPE])
                    dqr.append(dqf[:, MLA_NOPE:])
                    dkn.append(dkf[:, :MLA_NOPE])
                    dkr = dkr + dkf[:, MLA_NOPE:]
                dq_ref[...] = jnp.concatenate(dqn + dqr + [jnp.zeros((qt, 64), F32)], axis=1)
                dkv_ref[0:n_k, :] += jnp.concatenate(dkn + dvs, axis=1)
                dkr_ref[0, 0:n_k, :] += jnp.concatenate([dkr, jnp.zeros((n_k, 128 - MLA_ROPE), F32)], axis=1)

    return carried(
        body, comm, grid=(MLA_HEADS // 2, nq),
        in_specs=[pl.BlockSpec((qt, 256), lambda hp, n: (n, hp)), pl.BlockSpec((L, 256), lambda hp, n: (0, hp)),
                  pl.BlockSpec((L, 128), lambda hp, n: (0, 0)), pl.BlockSpec((qt, 128), lambda hp, n: (n, hp)),
                  pl.BlockSpec((qt, 128), lambda hp, n: (n, hp))],
        out_specs=[pl.BlockSpec((qt, 256), lambda hp, n: (n, hp)), pl.BlockSpec((L, 256), lambda hp, n: (0, hp)),
                   pl.BlockSpec((1, L, 128), lambda hp, n: (hp, 0, 0))],
        out_shape=[jax.ShapeDtypeStruct((L, 2048), F32), jax.ShapeDtypeStruct((L, 2048), F32),
                   jax.ShapeDtypeStruct((MLA_HEADS // 2, L, 128), F32)],
        semantics=("parallel", "arbitrary"), name='c_attn_bwd')(q, kv, kr, do, o)


def layer_c_fwd(h, w, p, comm=None):
    L = h.shape[0]
    proj = mm(h, w['c_w_in'], 'nn', 'c_proj')

    def f1(c, gq, gk):
        return [rms_fwd(c[:, :768], gq), rms_fwd(c[:, 768:], gk)], []
    (cqn, ckvn), _ = rowwise(f1, [rw(proj, 1024, 1)], [p['c_q_norm'], p['c_kv_norm']], [(768, BF16), (256, BF16)], [],
                             256, 'c_norms')
    qf = mm(cqn, w['c_w_uq'], 'nn', 'c_uq')
    kvf = mm(ckvn, w['c_w_ukv'], 'nn', 'c_ukv', out_dtype=BF16)
    cos, sin = _rope_tables(L)

    def f2(q_, kr_, c, s):
        c8, s8 = jnp.tile(c, (1, 8)), jnp.tile(s, (1, 8))
        return [q_ * c8 + _rot(q_) * s8, kr_ * c[:, 128:] + _rot(kr_) * s[:, 128:]], []
    (q, kr), _ = rowwise(f2, [rw(qf), rw(proj, 128, 16), rw(cos), rw(sin)], [], [(2048, BF16), (128, BF16)], [], 256,
                         'c_rope')
    o, carried_out = mla_fwd(q, kvf, kr, comm=comm)

    def f3(o_, z):
        return [o_ * silu(z)], []
    (po,), _ = rowwise(f3, [rw(o), rw(proj, 1024, 0)], [], [(1024, BF16)], [], 256, 'c_gate')
    yb = mm(po, w['c_w_out'], 'nn', 'c_out')
    return yb, dict(carried=carried_out, h=h, proj=proj, cqn=cqn, ckvn=ckvn, q=q, kv=kvf, kr=kr, o=o, po=po, cos=cos, sin=sin)


def layer_c_bwd(dyb, w, p, sv, comm=None, sink=None):
    g = {}
    dpo = mm(dyb, w['c_w_out'], 'nt', 'c_dpo')
    _dw(g, sink, 'c_w_out', sv['po'], dyb, 'c_dwout')
    proj = sv['proj']
    L = proj.shape[0]

    def f1(dpo_, o, z):
        return [dpo_ * silu(z), dpo_ * o * silu_grad(z)], []
    (do, dz), _ = rowwise(f1, [rw(dpo), rw(sv['o']), rw(proj, 1024, 0)], [], [(1024, BF16), (1024, F32)], [], 256,
                          'c_gate_bwd')
    (dq, dkvf, dkr8), g['carried'] = mla_bwd(sv['q'], sv['kv'], sv['kr'], do, sv['o'], comm=comm)

    def f2(dq_, dkr_, c, s):
        c8, s8 = jnp.tile(c, (1, 8)), jnp.tile(s, (1, 8))
        dk = jnp.sum(dkr_, axis=0)
        return [dq_ * c8 + _rot(dq_ * s8, True), dk * c[:, 128:] + _rot(dk * s[:, 128:], True)], []
    tl = 256
    (dqf, dkr), _ = rowwise(f2, [rw(dq), (dkr8, pl.BlockSpec((8, tl, 128), lambda i: (0, i, 0))), rw(sv['cos']),
                                 rw(sv['sin'])], [], [(2048, BF16), (128, F32)], [], tl, 'c_rope_bwd')
    _dw(g, sink, 'c_w_uq', sv['cqn'], dqf, 'c_dwuq')
    _dw(g, sink, 'c_w_ukv', sv['ckvn'], dkvf, 'c_dwukv')
    dcqn = mm(dqf, w['c_w_uq'], 'nt', 'c_dcqn')
    dckvn = mm(dkvf, w['c_w_ukv'], 'nt', 'c_dckvn')

    def f3(c, dq_, dk_, dz_, dkr_, gq, gk):
        dcq, dgq = rms_bwd(c[:, :768], gq, dq_)
        dckv, dgk = rms_bwd(c[:, 768:], gk, dk_)
        return [jnp.concatenate([dz_, dcq, dckv, dkr_], axis=1)], [dgq, dgk]
    (dproj,), (dgq, dgk) = rowwise(f3, [rw(proj, 1024, 1), rw(dcqn), rw(dckvn), rw(dz), rw(dkr)],
                                   [p['c_q_norm'], p['c_kv_norm']], [(2176, BF16)], [(1, 768), (1, 256)], 256, 'c_dproj')
    g['c_q_norm'], g['c_kv_norm'] = dgq, dgk
    _dw(g, sink, 'c_w_in', sv['h'], dproj, 'c_dwin')
    dh = mm(dproj, w['c_w_in'], 'nt', 'c_dh')
    return dh, g


def _sgu_mix(wm, v, transpose):
    outs = []
    dims = (((0,), (0,)), ((), ())) if transpose else (((1,), (0,)), ((), ()))
    for gi in range(SGU_G):
        outs.append(lax.dot_general(wm[gi], v[:, gi * SGU_C:(gi + 1) * SGU_C].astype(BF16), dims,
                                    preferred_element_type=F32))
    return jnp.concatenate(outs, axis=1)


def _sgu_wmask(ws):
    t = lax.broadcasted_iota(jnp.int32, (SGU_T, SGU_T), 0)
    s = lax.broadcasted_iota(jnp.int32, (SGU_T, SGU_T), 1)
    return jnp.where((s <= t)[None], ws, 0.0).astype(BF16)


def _ln_stats(v):
    mu = jnp.mean(v, axis=-1, keepdims=True)
    vc = v - mu
    rstd = lax.rsqrt(jnp.mean(vc * vc, axis=-1, keepdims=True) + EPS)
    return vc * rstd, rstd


def layer_d_fwd(h, w, p):
    proj = mm(h, w['d_w_in'], 'nn', 'd_proj')
    bias = jnp.repeat(p['d_b_s'][0].T, SGU_C, axis=1)

    def f1(u_, v_, z, ws, lg, lb, bs):
        xh, _ = _ln_stats(gelu(v_))
        s = _sgu_mix(_sgu_wmask(ws), xh * lg + lb, False) + bs
        return [gelu(u_) * s * silu(z)], []
    (po,), _ = rowwise(f1, [rw(proj, 1024, 0), rw(proj, 1024, 1), rw(proj, 1024, 2)],
                       [p['d_w_s'][0], p['d_ln_g'], p['d_ln_b'], bias], [(1024, BF16)], [], SGU_T, 'd_mix')
    yb = mm(po, w['d_w_out'], 'nn', 'd_out')
    return yb, dict(h=h, proj=proj, po=po, bias=bias)


def layer_d_bwd(dyb, w, p, sv, sink=None):
    g = {}
    dpo = mm(dyb, w['d_w_out'], 'nt', 'd_dpo')
    _dw(g, sink, 'd_w_out', sv['po'], dyb, 'd_dwout')
    proj = sv['proj']

    def f1(dpo_, u_, v_, z, ws, lg, lb, bs):
        wm = _sgu_wmask(ws)
        gv = gelu(v_)
        xh, rstd = _ln_stats(gv)
        vn = xh * lg + lb
        s = _sgu_mix(wm, vn, False) + bs
        gu, sz = gelu(u_), silu(z)
        du = dpo_ * s * sz
        ds = dpo_ * gu * sz
        dz = dpo_ * gu * s * silu_grad(z)
        dsb = ds.astype(BF16)
        dws = jnp.stack([lax.dot_general(dsb[:, gi * SGU_C:(gi + 1) * SGU_C], vn[:, gi * SGU_C:(gi + 1) * SGU_C].astype(BF16),
                                         (((1,), (1,)), ((), ())), preferred_element_type=F32) for gi in range(SGU_G)])
        dvn = _sgu_mix(wm, ds, True)
        dlg = jnp.sum(dvn * xh, axis=0, keepdims=True)
        dlb = jnp.sum(dvn, axis=0, keepdims=True)
        dxh = dvn * lg
        dgv = rstd * (dxh - jnp.mean(dxh, axis=-1, keepdims=True) - xh * jnp.mean(dxh * xh, axis=-1, keepdims=True))
        return ([jnp.concatenate([du * gelu_grad(u_), dgv * gelu_grad(v_), dz], axis=1)], [dws, ds, dlg, dlb])
    (dproj,), (dws, dbs, dlg, dlb) = rowwise(
        f1, [rw(dpo), rw(proj, 1024, 0), rw(proj, 1024, 1), rw(proj, 1024, 2)],
        [p['d_w_s'][0], p['d_ln_g'], p['d_ln_b'], sv['bias']], [(3072, BF16)],
        [(SGU_G, SGU_T, SGU_T), (SGU_T, 1024), (1, 1024), (1, 1024)], SGU_T, 'd_mix_bwd')
    tril = np.tril(np.ones((SGU_T, SGU_T), dtype=bool))
    g['d_w_s'] = jnp.where(tril[None], dws, 0.0)[None]
    g['d_b_s'] = dbs.reshape(SGU_T, SGU_G, SGU_C).sum(-1).T[None]
    g['d_ln_g'], g['d_ln_b'] = dlg, dlb
    _dw(g, sink, 'd_w_in', sv['h'], dproj, 'd_dwin')
    dh = mm(dproj, w['d_w_in'], 'nt', 'd_dh')
    return dh, g


def _coords():
    return lax.axis_index("x"), lax.axis_index("y"), lax.axis_index("c")


class AllGather:
    def __init__(self, x):
        self.ins = [x]
        self.outs = [jax.ShapeDtypeStruct((N_DEV,) + x.shape, x.dtype)]
        self.scratch = [pltpu.SemaphoreType.DMA((7,)), pltpu.SemaphoreType.DMA((7,)), pltpu.SemaphoreType.DMA(())]

    def hooks(self, n_steps):
        return [(0, functools.partial(self.phase, 0), False), (n_steps - 1, functools.partial(self.phase, 1), True),
                (n_steps - 1, functools.partial(self.phase, 2), True)]

    @staticmethod
    def phase(which, ins, outs, scratch):
        (x_ref,), (out_ref,), (send_sems, recv_sems, local_sem) = ins, outs, scratch
        x_, y_, c_ = _coords()
        me, sibling = (x_, y_, c_), (x_, y_, 1 - c_)
        chips = [(1 - x_, y_), (x_, 1 - y_), (1 - x_, 1 - y_)]

        def slot(px, py, pc):
            return out_ref.at[4 * px + 2 * py + pc]

        def copy(k, block, to, src=None):
            return pltpu.make_async_remote_copy(src_ref=slot(*block) if src is None else src, dst_ref=slot(*block),
                                                send_sem=send_sems.at[k], recv_sem=recv_sems.at[k], device_id=to,
                                                device_id_type=MESH)

        mine = pltpu.make_async_copy(x_ref, slot(*me), local_sem)
        first = [copy(0, me, sibling, src=x_ref)]
        first += [copy(1 + j, me, (*chip, c_), src=x_ref) for j, chip in enumerate(chips)]
        passed = [copy(4 + j, (*chip, c_), sibling) for j, chip in enumerate(chips)]
        if which == 0:
            mine.start()
            for cp in first:
                cp.start()
        elif which == 1:
            for j, chip in enumerate(chips):
                copy(1 + j, (*chip, c_), me).wait_recv()
                passed[j].start()
        else:
            copy(0, sibling, me).wait_recv()
            for j, chip in enumerate(chips):
                copy(4 + j, (*chip, 1 - c_), me).wait_recv()
            for cp in first + passed:
                cp.wait_send()
            mine.wait()


class ChipExchange:
    def __init__(self, part):
        self.ins = [part]
        self.outs = [jax.ShapeDtypeStruct((3,) + part.shape[1:], part.dtype)]
        self.scratch = [pltpu.SemaphoreType.DMA((3,)), pltpu.SemaphoreType.DMA((3,))]

    def hooks(self, n_steps):
        return [(0, functools.partial(self.phase, 0), False), (n_steps - 1, functools.partial(self.phase, 1), True)]

    @staticmethod
    def phase(which, ins, outs, scratch):
        (p_ref,), (land_ref,), (send_sems, recv_sems) = ins, outs, scratch
        x_, y_, c_ = _coords()
        copies = []
        for r, (fx, fy) in enumerate([(1, 0), (0, 1), (1, 1)]):
            tx = jnp.where(fx == 1, 1 - x_, x_)
            ty = jnp.where(fy == 1, 1 - y_, y_)
            copies.append(pltpu.make_async_remote_copy(src_ref=p_ref.at[2 * tx + ty], dst_ref=land_ref.at[r],
                                                       send_sem=send_sems.at[r], recv_sem=recv_sems.at[r],
                                                       device_id=(tx, ty, c_), device_id_type=MESH))
        if which == 0:
            for cp in copies:
                cp.start()
        else:
            for cp in copies:
                cp.wait_recv()
            for cp in copies:
                cp.wait_send()


class Both:
    def __init__(self, a, b):
        self.parts = (a, b)
        self.ins, self.outs, self.scratch = a.ins + b.ins, a.outs + b.outs, a.scratch + b.scratch

    def hooks(self, n_steps):
        res, oi, oo, osc = [], 0, 0, 0
        for p in self.parts:
            sl = (slice(oi, oi + len(p.ins)), slice(oo, oo + len(p.outs)), slice(osc, osc + len(p.scratch)))
            res += [(at, functools.partial(self.sub, fn, sl), after) for at, fn, after in p.hooks(n_steps)]
            oi, oo, osc = oi + len(p.ins), oo + len(p.outs), osc + len(p.scratch)
        return res

    @staticmethod
    def sub(fn, sl, ins, outs, scratch):
        fn(ins[sl[0]], outs[sl[1]], scratch[sl[2]])


def run_comm(comm, name):
    def body(*refs):
        ci, co = len(comm.ins), len(comm.outs)
        for _, fn, _ in comm.hooks(1):
            fn(refs[:ci], refs[ci:ci + co], refs[ci + co:])

    return pl.pallas_call(body, out_shape=list(comm.outs), in_specs=[ANY] * len(comm.ins),
                          out_specs=[ANY] * len(comm.outs), scratch_shapes=list(comm.scratch), name=name)(*comm.ins)


def all_gather(x, name):
    return run_comm(AllGather(x), name)[0]


class SiblingExchange:
    def __init__(self, gfull):
        self.ins = [gfull]
        self.outs = [jax.ShapeDtypeStruct((4,) + gfull.shape[1:], gfull.dtype)]
        self.scratch = [pltpu.SemaphoreType.DMA((4,)), pltpu.SemaphoreType.DMA((4,))]

    def hooks(self, n_steps):
        return [(0, functools.partial(self.phase, 0), False), (n_steps - 1, functools.partial(self.phase, 1), True)]

    @staticmethod
    def phase(which, ins, outs, scratch):
        (g_ref,), (land_ref,), (send_sems, recv_sems) = ins, outs, scratch
        x_, y_, c_ = _coords()
        copies = [pltpu.make_async_remote_copy(src_ref=g_ref.at[2 * k + 1 - c_], dst_ref=land_ref.at[k],
                                               send_sem=send_sems.at[k], recv_sem=recv_sems.at[k],
                                               device_id=(x_, y_, 1 - c_), device_id_type=MESH) for k in range(4)]
        if which == 0:
            for cp in copies:
                cp.start()
        else:
            for cp in copies:
                cp.wait_recv()
            for cp in copies:
                cp.wait_send()


def rs_sibling(gfull, tag):
    return run_comm(SiblingExchange(gfull), 'rs_sibling_' + tag)[0]


def rs_pair_add(gfull, land, core, tag):
    _, R, C = gfull.shape
    tl = R

    def body(c_ref, g_ref, l_ref, o_ref):
        o_ref[...] = (g_ref[...].astype(F32) + l_ref[...].astype(F32)).astype(BF16)

    return pl.pallas_call(
        body, out_shape=jax.ShapeDtypeStruct((4, R, C), BF16),
        grid_spec=pltpu.PrefetchScalarGridSpec(
            num_scalar_prefetch=1, grid=(4, R // tl),
            in_specs=[pl.BlockSpec((1, tl, C), lambda k, i, c: (2 * k + c[0], i, 0)),
                      pl.BlockSpec((1, tl, C), lambda k, i, c: (k, i, 0))],
            out_specs=pl.BlockSpec((1, tl, C), lambda k, i, c: (k, i, 0))),
        compiler_params=pltpu.CompilerParams(dimension_semantics=("parallel", "parallel")), name='rs_pair_add_' + tag)(
            core, gfull, land)


def rs_chips(part, tag):
    return run_comm(ChipExchange(part), 'rs_chips_' + tag)[0]


def _adam(wv, gv, mv, vv):
    m = ADAM_B1 * mv + (1.0 - ADAM_B1) * gv
    v = ADAM_B2 * vv + (1.0 - ADAM_B2) * (gv * gv)
    m_hat = m / (1.0 - ADAM_B1 ** ADAM_STEP)
    v_hat = v / (1.0 - ADAM_B2 ** ADAM_STEP)
    delta = -ADAM_LR * (m_hat / (jnp.sqrt(v_hat) + ADAM_EPS) + ADAM_WD * wv)
    return delta, m, v


def _sum4(p_ref, l_ref):
    return ((p_ref[0].astype(F32) + l_ref[0].astype(F32)) + l_ref[1].astype(F32)) + l_ref[2].astype(F32)


def rs_rep_sum(part, land, chip):
    def body(c_ref, p_ref, l_ref, o_ref):
        o_ref[...] = _sum4(p_ref, l_ref).astype(BF16)

    return pl.pallas_call(
        body, out_shape=jax.ShapeDtypeStruct((REP_SLOT, LANES), BF16),
        grid_spec=pltpu.PrefetchScalarGridSpec(
            num_scalar_prefetch=1, grid=(1,),
            in_specs=[pl.BlockSpec((1, REP_SLOT, LANES), lambda i, c: (c[0], 0, 0)),
                      pl.BlockSpec((3, REP_SLOT, LANES), lambda i, c: (0, 0, 0))],
            out_specs=pl.BlockSpec((REP_SLOT, LANES), lambda i, c: (0, 0))),
        compiler_params=pltpu.CompilerParams(dimension_semantics=("parallel",)), name='rs_rep')(chip, part, land)


def adam_param(name, shape, off, w, m, v, chip, part=None, land=None, grep=None, fold=1):
    r, c = shape
    rp, nt, rb = _tiles((r // fold, c * fold))
    rbw = min(r, rb) if fold == 1 else r
    n_src = 2 if grep is None else 1
    ns = w.shape
    assert int(np.prod(ns[:-1])) == r and ns[-1] == c and (fold == 1 or (rb == rp and nt == 1))
    if fold > 1:
        nat_block, nat_map = ns, lambda i, cr: (0,) * len(ns)
    elif len(ns) == 2:
        nat_block, nat_map = (rbw, c), lambda i, cr: (i, 0)
    elif int(np.prod(ns[:-2])) == 1:
        nat_block, nat_map = (1,) * (len(ns) - 2) + (rbw, c), lambda i, cr: (0,) * (len(ns) - 2) + (i, 0)
    else:
        assert len(ns) == 4 and ns[0] == 1 and rbw % ns[2] == 0
        nat_block, nat_map = (1, rbw // ns[2], ns[2], c), lambda i, cr: (0, i, 0, 0)

    def body(c_ref, *refs):
        srcs = refs[:n_src * nt]
        w_ref, m_ref, v_ref, g_ref, d_ref, nm_ref, nv_ref = refs[n_src * nt:]
        if grep is None:
            tiles = [_sum4(srcs[2 * t], srcs[2 * t + 1]) for t in range(nt)]
        else:
            tiles = [srcs[t][...].astype(F32) for t in range(nt)]
        if fold > 1:
            g = jnp.concatenate([tiles[0][:, q * c:(q + 1) * c] for q in range(fold)], axis=0)
        else:
            g = (tiles[0] if nt == 1 else jnp.concatenate(tiles, axis=1))[:rbw, :c]
        g_ref[...] = g.reshape(nat_block)
        res = _adam(w_ref[...].reshape(rbw, c), g, m_ref[...].reshape(rbw, c), v_ref[...].reshape(rbw, c))
        for ref, val in zip((d_ref, nm_ref, nv_ref), res):
            ref[...] = val.reshape(nat_block)

    in_specs, args = [], []
    for t in range(nt):
        b0 = (off + t * rp) // rb
        assert (off + t * rp) % rb == 0
        if grep is None:
            in_specs += [pl.BlockSpec((1, rb, LANES), functools.partial(lambda i, cr, b0: (cr[0], b0 + i, 0), b0=b0)),
                         pl.BlockSpec((3, rb, LANES), functools.partial(lambda i, cr, b0: (0, b0 + i, 0), b0=b0))]
            args += [part, land]
        else:
            in_specs.append(pl.BlockSpec((rb, LANES), functools.partial(lambda i, cr, b0: (b0 + i, 0), b0=b0)))
            args.append(grep)
    nat = pl.BlockSpec(nat_block, nat_map)
    return pl.pallas_call(
        body, out_shape=[jax.ShapeDtypeStruct(ns, F32)] * 4,
        grid_spec=pltpu.PrefetchScalarGridSpec(num_scalar_prefetch=1, grid=(rp // rb,), in_specs=in_specs + [nat] * 3,
                                               out_specs=[nat] * 4),
        compiler_params=pltpu.CompilerParams(dimension_semantics=("parallel",)), name='adam_' + name)(
            chip, *args, w, m, v)


def adam_small(names, grep, P, M, V):
    in_specs, args, out_specs, out_shape, meta = [], [], [], [], []
    for n in names:
        s = REP_SHAPE[n]
        rp, nt, _ = _tiles(s)
        ns = P[n].shape
        for t in range(nt):
            b0 = (REP_OFF[n] + t * rp) // rp
            assert (REP_OFF[n] + t * rp) % rp == 0
            in_specs.append(pl.BlockSpec((rp, LANES), functools.partial(lambda i, b0: (b0, 0), b0=b0)))
            args.append(grep)
        nat = pl.BlockSpec(ns, functools.partial(lambda i, nd: (0,) * nd, nd=len(ns)))
        in_specs += [nat] * 3
        args += [P[n], M[n], V[n]]
        out_specs += [nat] * 4
        out_shape += [jax.ShapeDtypeStruct(ns, F32)] * 4
        meta.append((s, nt, ns))
    n_in = len(in_specs)

    def body(*refs):
        ins, outs = refs[:n_in], refs[n_in:]
        k = 0
        for p, ((r, c), nt, ns) in enumerate(meta):
            tiles = [ins[k + t][...].astype(F32) for t in range(nt)]
            w_ref, m_ref, v_ref = ins[k + nt:k + nt + 3]
            k += nt + 3
            g = (tiles[0] if nt == 1 else jnp.concatenate(tiles, axis=1))[:r, :c]
            res = (g,) + _adam(w_ref[...].reshape(r, c), g, m_ref[...].reshape(r, c), v_ref[...].reshape(r, c))
            for ref, val in zip(outs[4 * p:4 * p + 4], res):
                ref[...] = val.reshape(ns)

    res = pl.pallas_call(body, grid=(1,), in_specs=in_specs, out_specs=out_specs, out_shape=out_shape,
                         compiler_params=pltpu.CompilerParams(dimension_semantics=("arbitrary",)), name='adam_small')(*args)
    return {n: tuple(res[4 * p:4 * p + 4]) for p, n in enumerate(names)}


VM = pl.BlockSpec(memory_space=pltpu.VMEM)


def _tile_value(w, t, rp):
    r, c = w.shape
    wt = min(LANES, c - t * LANES)
    tile = w[:, t * LANES:t * LANES + wt]
    if wt < LANES:
        tile = jnp.concatenate([tile, jnp.zeros((r, LANES - wt), tile.dtype)], axis=1)
    if rp > r:
        tile = jnp.concatenate([tile, jnp.zeros((rp - r, LANES), tile.dtype)], axis=0)
    return tile


def pack_layer(layer, blocks):
    names = LAYER_PARAMS[layer]

    def body(*refs):
        tiles = []
        for ref, n in zip(refs[:-1], names):
            rp, nt, _ = _tiles(_block_shape(n))
            w = ref[...].reshape(_block_shape(n))
            tiles += [_tile_value(w, t, rp) for t in range(nt)]
        refs[-1][...] = jnp.concatenate(tiles, axis=0).astype(BF16)

    return pl.pallas_call(body, out_shape=jax.ShapeDtypeStruct((LAYER_ROWS[layer], LANES), BF16),
                          in_specs=[VM] * len(names), out_specs=VM, name='pack_' + layer)(*[blocks[n] for n in names])


def assemble(name, gathered):
    (rf, cf), ax = SHARDED[name]
    r, c = _block_shape(name)
    rp, nt, _ = _tiles((r, c))
    off = SH_OFF[name]
    out_cols = cf if ax == 0 else len(perm_index(name))

    def body(g_ref, o_ref, buf, sem):
        cp = pltpu.make_async_copy(g_ref.at[:, pl.ds(off, nt * rp), :], buf, sem)
        cp.start()
        cp.wait()
        if ax == 0:
            for j in range(N_DEV):
                o_ref[j * r:(j + 1) * r, :] = jnp.concatenate([buf[j, t * rp:(t + 1) * rp, :] for t in range(nt)], axis=1)
            return
        pieces = []
        for p in PERM[name]:
            if p[0] == 'z':
                pieces.append(jnp.zeros((r, p[1]), BF16))
                continue
            n0, w = p
            while w > 0:
                j, cb = divmod(n0, c)
                t, lane = divmod(cb, LANES)
                wl = min(w, LANES - lane, c - cb)
                pieces.append(buf[j, t * rp:t * rp + r, lane:lane + wl])
                n0, w = n0 + wl, w - wl
        o_ref[...] = jnp.concatenate(pieces, axis=1)

    return pl.pallas_call(
        body, out_shape=jax.ShapeDtypeStruct((rf, out_cols), BF16), in_specs=[ANY], out_specs=VM,
        scratch_shapes=[pltpu.VMEM((N_DEV, nt * rp, LANES), BF16), pltpu.SemaphoreType.DMA(())], name='asm_' + name)(
            gathered)


def chunk_grad(layer, name, dw, gfull):
    (rf, cf), ax = SHARDED[name]
    r, c = _block_shape(name)
    rp, nt, _ = _tiles((r, c))
    off = SH_OFF[name]
    if ax == 1:
        idx = perm_index(name) if name in PERM else np.arange(cf)
        inv = np.full(cf, -1)
        inv[idx[idx >= 0]] = np.nonzero(idx >= 0)[0]

    def body(*refs):
        dw_ref, o_ref, buf, sem = refs[0], refs[-3], refs[-2], refs[-1]
        for j in range(N_DEV):
            for t in range(nt):
                if ax == 0:
                    tile = dw_ref[j * r:(j + 1) * r, t * LANES:(t + 1) * LANES]
                else:
                    cols = inv[j * c + t * LANES:j * c + min((t + 1) * LANES, c)]
                    cuts = [0] + [k for k in range(1, len(cols)) if cols[k] != cols[k - 1] + 1] + [len(cols)]
                    pieces = [dw_ref[:, int(cols[a]):int(cols[b - 1]) + 1] for a, b in zip(cuts[:-1], cuts[1:])]
                    if len(cols) < LANES:
                        pieces.append(jnp.zeros((r, LANES - len(cols)), F32))
                    tile = pieces[0] if len(pieces) == 1 else jnp.concatenate(pieces, axis=1)
                    if rp > r:
                        tile = jnp.concatenate([tile, jnp.zeros((rp - r, LANES), F32)], axis=0)
                buf[j, t * rp:(t + 1) * rp, :] = tile.astype(BF16)
        cp = pltpu.make_async_copy(buf, o_ref.at[:, pl.ds(off, nt * rp), :], sem)
        cp.start()
        cp.wait()

    shape = jax.ShapeDtypeStruct((N_DEV, LAYER_ROWS[layer], LANES), BF16)
    scratch = [pltpu.VMEM((N_DEV, nt * rp, LANES), BF16), pltpu.SemaphoreType.DMA(())]
    if gfull is None:
        return pl.pallas_call(body, out_shape=shape, in_specs=[VM], out_specs=ANY, scratch_shapes=scratch,
                              name='chunk_' + name)(dw)
    return pl.pallas_call(body, out_shape=shape, in_specs=[VM, ANY], out_specs=ANY, scratch_shapes=scratch,
                          input_output_aliases={1: 0}, name='chunk_' + name)(dw, gfull)


class GradSink:
    def __init__(self):
        self.bufs = {}

    def put(self, name, a, b, mm_name):
        (rf, cf), ax = SHARDED[name]
        r, c = _block_shape(name)
        group = GROUP_OF[name]
        direct = ax == 0 or (c % LANES == 0 and PERM[name] == [(0, cf)])
        if direct:
            self.bufs[group] = mm_tn_chunked(a, b, mm_name, group, name, self.bufs.get(group))
        else:
            self.add(name, mm(a, b, 'tn', mm_name))

    def add(self, name, dw):
        group = GROUP_OF[name]
        self.bufs[group] = chunk_grad(group, name, dw, self.bufs.get(group))


def mm_tn_chunked(a, b, mm_name, layer, wname, gfull):
    (rf, cf), ax = SHARDED[wname]
    r, c = _block_shape(wname)
    rp, nt, _ = _tiles((r, c))
    off = SH_OFF[wname]
    K, M = a.shape
    N = b.shape[1]
    assert (M, N) == (rf, cf) and rp == r
    if ax == 0:
        tn = 4 * LANES
        grid, bspec = (N // tn,), pl.BlockSpec((K, tn), lambda g: (0, g))
        ospec = pl.BlockSpec((N_DEV, 4 * r, LANES), lambda g: (0, off // (4 * r) + g, 0))
        assert off % (4 * r) == 0 and nt % 4 == 0

        def store(res, o_ref):
            for j in range(N_DEV):
                for q in range(4):
                    o_ref[j, q * r:(q + 1) * r, :] = res[j * r:(j + 1) * r, q * LANES:(q + 1) * LANES].astype(BF16)
    else:
        tn = c
        grid, bspec = (N_DEV,), pl.BlockSpec((K, tn), lambda g: (0, g))
        ospec = pl.BlockSpec((1, nt * r, LANES), lambda g: (g, off // (nt * r), 0))
        assert off % (nt * r) == 0

        def store(res, o_ref):
            for t in range(nt):
                o_ref[0, t * r:(t + 1) * r, :] = res[:, t * LANES:(t + 1) * LANES].astype(BF16)

    def body(*refs):
        a_ref, b_ref, o_ref = refs[0], refs[1], refs[-1]
        store(lax.dot_general(a_ref[...].astype(BF16), b_ref[...].astype(BF16), _TN, preferred_element_type=F32), o_ref)

    shape = jax.ShapeDtypeStruct((N_DEV, LAYER_ROWS[layer], LANES), BF16)
    aspec = pl.BlockSpec((K, M), lambda g: (0, 0))
    params = pltpu.CompilerParams(dimension_semantics=("parallel",))
    if gfull is None:
        return pl.pallas_call(body, grid=grid, in_specs=[aspec, bspec], out_specs=ospec, out_shape=shape,
                              compiler_params=params, name=mm_name)(a, b)
    return pl.pallas_call(body, grid=grid, in_specs=[aspec, bspec, ANY], out_specs=ospec, out_shape=shape,
                          input_output_aliases={2: 0}, compiler_params=params, name=mm_name)(a, b, gfull)


def pack_rep(G):
    def body(*refs):
        tiles = []
        for ref, n in zip(refs[:-1], REP_SHAPE):
            rp, nt, _ = _tiles(_rep_packed_shape(n))
            g = ref[...]
            fold = REP_FOLD.get(n, 1)
            if fold > 1:
                rr = g.shape[0] // fold
                g = jnp.concatenate([g[q * rr:(q + 1) * rr] for q in range(fold)], axis=1)
            tiles += [_tile_value(g, t, rp) for t in range(nt)]
        rows = sum(t.shape[0] for t in tiles)
        if rows < REP_ROWS:
            tiles.append(jnp.zeros((REP_ROWS - rows, LANES), F32))
        full = jnp.concatenate(tiles, axis=0)
        for j in range(N_DEV):
            refs[-1][j] = full[j * REP_CHUNK:(j + 1) * REP_CHUNK]

    return pl.pallas_call(body, out_shape=jax.ShapeDtypeStruct((N_DEV, REP_SLOT, LANES), F32),
                          in_specs=[VM] * len(REP_SHAPE), out_specs=VM, name='pack_rep')(
                              *[G[n].reshape(s) for n, s in REP_SHAPE.items()])


def _pack_small(blocks, order, rows, width, dtype):
    flat = jnp.concatenate([blocks[n].reshape(-1).astype(dtype) for n in order])
    return jnp.pad(flat, (0, rows * width - flat.shape[0])).reshape(rows, width)


def kernel(x, pre_norm, post_norm, rel_bias, a_w_in, a_lam_re, a_lam_im, a_log_dt, a_b_re, a_b_im, a_c_re, a_c_im, a_d, a_w_glu, a_b_glu, a_w_out, b_w_in, b_sinks, b_w_out, c_w_in, c_q_norm, c_kv_norm, c_w_uq, c_w_ukv, c_w_out, d_w_in, d_ln_g, d_ln_b, d_w_s, d_b_s, d_w_out, loss_target, m_pre_norm, m_post_norm, m_rel_bias, m_a_w_in, m_a_lam_re, m_a_lam_im, m_a_log_dt, m_a_b_re, m_a_b_im, m_a_c_re, m_a_c_im, m_a_d, m_a_w_glu, m_a_b_glu, m_a_w_out, m_b_w_in, m_b_sinks, m_b_w_out, m_c_w_in, m_c_q_norm, m_c_kv_norm, m_c_w_uq, m_c_w_ukv, m_c_w_out, m_d_w_in, m_d_ln_g, m_d_ln_b, m_d_w_s, m_d_b_s, m_d_w_out, v_pre_norm, v_post_norm, v_rel_bias, v_a_w_in, v_a_lam_re, v_a_lam_im, v_a_log_dt, v_a_b_re, v_a_b_im, v_a_c_re, v_a_c_im, v_a_d, v_a_w_glu, v_a_b_glu, v_a_w_out, v_b_w_in, v_b_sinks, v_b_w_out, v_c_w_in, v_c_q_norm, v_c_kv_norm, v_c_w_uq, v_c_w_ukv, v_c_w_out, v_d_w_in, v_d_ln_g, v_d_ln_b, v_d_w_s, v_d_b_s, v_d_w_out):
    loc = locals()
    P = {n: loc[n] for n in WEIGHTS}
    M = {n: loc['m_' + n] for n in WEIGHTS}
    V = {n: loc['v_' + n] for n in WEIGHTS}
    xs = x[0]
    L = xs.shape[0]

    blocks = {n: P[n].reshape(_block_shape(n)) for n in SHARDED}
    packed = {layer: pack_layer(layer, P) for layer in LAYER_PARAMS}
    W = {}

    def assemble_layer(layer, gathered):
        for n in LAYER_PARAMS[layer]:
            if n not in SHARDED_F32:
                W[n] = assemble(n, gathered)

    small =all_gather(_pack_small(blocks, SHARDED_F32, SMALL_ROWS, 128, F32), 'ag_small')
    Pl = dict(P)
    for n in SHARDED_F32:
        c = SHARDED[n][0][1]
        bc = c // N_DEV
        Pl[n] = small.reshape(N_DEV, -1)[:, SMALL_OFF[n]:SMALL_OFF[n] + bc].reshape(1, c)
    cx, cy, cc = _coords()
    core = jnp.reshape(cc, (1,)).astype(jnp.int32)
    chip = jnp.reshape(2 * cx + cy, (1,)).astype(jnp.int32)

    def pair_sums(gfull, tag):
        return rs_pair_add(gfull, rs_sibling(gfull, tag), core, tag)

    fwd = [layer_a_fwd, layer_b_fwd, layer_c_fwd, layer_d_fwd]
    bwd = [layer_a_bwd, layer_b_bwd, layer_c_bwd, layer_d_bwd]
    saved = []
    xc = xs

    def fpre(x_, g_):
        return [rms_fwd(x_, g_)], []
    (h,), _ = rowwise(fpre, [rw(xc)], [P['pre_norm'][0:1]], [(D_MODEL, BF16)], [], 256, 'pre_norm0')
    for i in range(4):
        if i == 0:
            yb, sv = fwd[i](h, W, Pl, comm=Both(AllGather(packed['a2']), AllGather(packed['b'])),
                            on_carried=lambda got: assemble_layer('a2', got[0]),
                            prep_comm=AllGather(packed['a1']), on_prep=lambda got: assemble_layer('a1', got[0]))
            assemble_layer('b', sv['carried'][1])
        elif i < 3:
            nxt = 'abcd'[i + 1]
            yb, sv = fwd[i](h, W, Pl, comm=AllGather(packed[nxt]))
            assemble_layer(nxt, sv['carried'][0])
        else:
            yb, sv = fwd[i](h, W, Pl)

        sv['x'], sv['yb'] = xc, yb
        saved.append(sv)
        if i < 3:

            def fpost(x_, y_, gpost, gpre):
                xn_ = x_ + rms_fwd(y_, gpost)
                return [xn_, rms_fwd(xn_, gpre)], []
            (xc, h), _ = rowwise(fpost, [rw(xc), rw(yb)], [P['post_norm'][i:i + 1], P['pre_norm'][i + 1:i + 2]],
                                 [(D_MODEL, F32), (D_MODEL, BF16)], [], 256, f'post_pre_norm{i}')
        else:

            def floss(x_, y_, t_, gpost):
                d = x_ + rms_fwd(y_, gpost) - t_
                return [d * (1.0 / D_MODEL)], [0.5 * jnp.sum(jnp.sum(d * d, axis=-1, keepdims=True) * (1.0 / D_MODEL),
                                                             axis=0, keepdims=True)]
            (dx,), (loss_loc,) = rowwise(floss, [rw(xc), rw(yb), rw(loss_target[0])], [P['post_norm'][i:i + 1]],
                                         [(D_MODEL, F32)], [(1, 1)], 256, 'post_norm_loss')
    loss = lax.psum(loss_loc[0, 0], ("x", "y", "c"))

    G, out = {}, {}
    dpre, dpost = [None] * 4, [None] * 4

    def adam_layer(layer, part, land2):
        for n in LAYER_PARAMS[layer]:
            s = _block_shape(n)
            out[n] = adam_param(n, s, SH_OFF[n], P[n], M[n], V[n], chip, part=part, land=land2)

    def fpost_b(y_, d_, g_):
        dy, dg = rms_bwd(y_, g_, d_)
        return [dy], [dg]
    (dyb,), (dpost[3],) = rowwise(fpost_b, [rw(saved[3]['yb']), rw(dx)], [P['post_norm'][3:4]], [(D_MODEL, BF16)],
                                  [(1, D_MODEL)], 256, 'post_norm_bwd3')
    pending = None
    sink = GradSink()
    for i in reversed(range(4)):
        sv = saved[i]
        if pending is None:
            dh, g = bwd[i](dyb, W, Pl, sv, sink=sink)
        elif i > 0:
            dh, g = bwd[i](dyb, W, Pl, sv, comm=ChipExchange(pending[1]), sink=sink)
            adam_layer(pending[0], pending[1], g['carried'][0])
        else:
            early = {}

            def both():
                early['part'] = pair_sums(sink.bufs['a2'], 'a2')
                return Both(ChipExchange(pending[1]), ChipExchange(early['part']))
            dh, g = bwd[i](dyb, W, Pl, sv, comm=both, sink=sink)
            adam_layer(pending[0], pending[1], g['carried'][0])
            adam_layer('a2', early['part'], g['carried'][1])
        g.pop('carried', None)
        land_a1 = g.pop('land_a1', None)
        G.update(g)
        group = LAYER_GROUPS['abcd'[i]][0]
        for n in LAYER_PARAMS[group]:
            if n in g:
                sink.add(n, g[n])
        if i > 0:
            swap = SiblingExchange(sink.bufs[group])
        else:
            part_a1 = rs_pair_add(sink.bufs[group], land_a1, core, group)
            swap = ChipExchange(part_a1)

        if i > 0:

            def fpre_b(x_, dh_, d_, y_, gpre, gpost):
                dxl, dg = rms_bwd(x_, gpre, dh_)
                dy, dgp = rms_bwd(y_, gpost, d_ + dxl)
                return [d_ + dxl, dy], [dg, dgp]
            (dx, dyb), (dpre[i], dpost[i - 1]), (land,) = rowwise(
                fpre_b, [rw(sv['x']), rw(dh), rw(dx), rw(saved[i - 1]['yb'])],
                [P['pre_norm'][i:i + 1], P['post_norm'][i - 1:i]], [(D_MODEL, F32), (D_MODEL, BF16)],
                [(1, D_MODEL), (1, D_MODEL)], 256, f'pre_post_norm_bwd{i}', comm=swap)
        else:

            def fpre_b0(x_, dh_, d_, g_):
                dxl, dg = rms_bwd(x_, g_, dh_)
                return [d_ + dxl], [dg]
            (dx,), (dpre[i],), (land2_a1,) = rowwise(fpre_b0, [rw(sv['x']), rw(dh), rw(dx)], [P['pre_norm'][i:i + 1]],
                                                     [(D_MODEL, F32)], [(1, D_MODEL)], 256, 'pre_norm_bwd0', comm=swap)
            adam_layer('a1', part_a1, land2_a1)
            break
        pending = (group, rs_pair_add(sink.bufs[group], land, core, group))
    G['pre_norm'] = jnp.concatenate(dpre, axis=0)
    G['post_norm'] = jnp.concatenate(dpost, axis=0)

    part = pair_sums(pack_rep(G), 'rep')
    land2 = rs_chips(part, 'rep')
    grep = all_gather(rs_rep_sum(part, land2, chip), 'ag_rep')[:, :REP_CHUNK].reshape(REP_ROWS, LANES)
    small_names = [n for n, s in REP_SHAPE.items() if s[0] <= 64]
    out.update(adam_small(small_names, grep, P, M, V))
    for n, s in REP_SHAPE.items():
        if n not in small_names:
            out[n] = adam_param(n, s, REP_OFF[n], P[n], M[n], V[n], chip, grep=grep, fold=REP_FOLD.get(n, 1))
    res = [loss, dx[None]]
    for kind in range(4):
        res += [out[n][kind].reshape(P[n].shape) for n in WEIGHTS]
    return tuple(res)
```

```python
import functools
import math

import numpy as np
import jax
import jax.numpy as jnp
from jax import lax
from jax.experimental import pallas as pl
from jax.experimental.pallas import tpu as pltpu

F32 = jnp.float32
BF16 = jnp.bfloat16
MESH = pl.DeviceIdType.MESH
ANY = pl.BlockSpec(memory_space=pl.ANY)

N_DEV = 8
D_MODEL = 1024
EPS = 1e-6
NEG_INF = -1e30
SSM_G, SSM_P, SSM_H = 64, 64, 16
SSM_T = 256
SSM_TS = 8
SSM_WC = 512
HEAD_DIM = 64
SWA_HEADS, SWA_KV = 16, 2
WINDOW = 128
REL_BUCKETS, REL_MAX_DIST = 32, 128
MLA_HEADS, MLA_NOPE, MLA_ROPE, MLA_V = 16, 64, 32, 64
MLA_Q_RANK, MLA_KV_RANK = 768, 256
ROPE_BASE = 10000.0
SGU_G, SGU_C, SGU_T = 16, 64, 128
ADAM_LR, ADAM_B1, ADAM_B2, ADAM_EPS, ADAM_WD, ADAM_STEP = 0.001, 0.9, 0.999, 1e-08, 0.01, 10

WEIGHTS = ['pre_norm', 'post_norm', 'rel_bias', 'a_w_in', 'a_lam_re', 'a_lam_im', 'a_log_dt', 'a_b_re', 'a_b_im',
           'a_c_re', 'a_c_im', 'a_d', 'a_w_glu', 'a_b_glu', 'a_w_out', 'b_w_in', 'b_sinks', 'b_w_out', 'c_w_in',
           'c_q_norm', 'c_kv_norm', 'c_w_uq', 'c_w_ukv', 'c_w_out', 'd_w_in', 'd_ln_g', 'd_ln_b', 'd_w_s', 'd_b_s',
           'd_w_out']
SHARDED = {'a_w_in': ((1024, 2048), 1), 'a_w_glu': ((1024, 1024), 0), 'a_w_out': ((1024, 1024), 0),
           'b_w_in': ((1024, 2304), 1), 'b_w_out': ((1024, 1024), 0), 'c_w_in': ((1024, 2080), 1),
           'c_q_norm': ((1, 768), 1), 'c_kv_norm': ((1, 256), 1), 'c_w_uq': ((768, 1536), 1),
           'c_w_ukv': ((256, 2048), 1), 'c_w_out': ((1024, 1024), 0), 'd_w_in': ((1024, 3072), 1),
           'd_ln_g': ((1, 1024), 1), 'd_ln_b': ((1, 1024), 1), 'd_w_out': ((1024, 1024), 0)}
SHARDED_F32 = ['c_q_norm', 'c_kv_norm', 'd_ln_g', 'd_ln_b']
REPLICATED = [n for n in WEIGHTS if n not in SHARDED]


def _cdiv(a, b):
    return -(-a // b)


def _block_shape(name):
    (r, c), ax = SHARDED[name]
    return (r // N_DEV, c) if ax == 0 else (r, c // N_DEV)


LANES = 128
LAYER_PARAMS = {'a1': ['a_w_in'], 'a2': ['a_w_glu', 'a_w_out'], 'b': ['b_w_in', 'b_w_out'],
                'c': ['c_w_in', 'c_w_uq', 'c_w_ukv', 'c_w_out', 'c_q_norm', 'c_kv_norm'],
                'd': ['d_w_in', 'd_w_out', 'd_ln_g', 'd_ln_b']}


def _tiles(shape):
    r, c = shape
    rp = max(r, 16)
    rb = 512 if rp % 512 == 0 else 256 if rp % 256 == 0 else rp
    return rp, _cdiv(c, LANES), rb


SH_OFF, LAYER_ROWS = {}, {}
for _l, _names in LAYER_PARAMS.items():
    _o = 0
    for _n in _names:
        _rp, _nt, _rb = _tiles(_block_shape(_n))
        assert _o % _rb == 0
        SH_OFF[_n] = _o
        _o += _rp * _nt
    assert _o % 16 == 0
    LAYER_ROWS[_l] = _o
GROUP_OF = {_n: _l for _l, _names in LAYER_PARAMS.items() for _n in _names}
LAYER_GROUPS = {'a': ['a1', 'a2'], 'b': ['b'], 'c': ['c'], 'd': ['d']}

REP_SHAPE = {'d_w_s': (2048, 128), 'a_b_re': (4096, 16), 'a_b_im': (4096, 16), 'a_c_re': (1024, 64),
             'a_c_im': (1024, 64), 'pre_norm': (4, 1024), 'post_norm': (4, 1024), 'a_lam_re': (64, 64),
             'a_lam_im': (64, 64), 'a_d': (1, 1024), 'a_b_glu': (1, 1024), 'rel_bias': (32, 16), 'd_b_s': (16, 128),
             'a_log_dt': (1, 64), 'b_sinks': (1, 16)}
REP_FOLD = {'a_b_re': 8, 'a_b_im': 8, 'a_c_re': 2, 'a_c_im': 2}


def _rep_packed_shape(name):
    (r, c), f = REP_SHAPE[name], REP_FOLD.get(name, 1)
    return (r // f, c * f)


REP_OFF = {}
_o = 0
for _n in REP_SHAPE:
    _rp, _nt, _rb = _tiles(_rep_packed_shape(_n))
    assert _o % _rb == 0
    REP_OFF[_n] = _o
    _o += _rp * _nt
REP_ROWS = _cdiv(_o, 16 * N_DEV) * 16 * N_DEV
REP_CHUNK = REP_ROWS // N_DEV
REP_SLOT = REP_CHUNK

PERM = {'a_w_in': [(0, 2048)], 'd_w_in': [(0, 3072)], 'b_w_in': [(1280, 1024), (0, 1280)],
        'c_w_in': [(1056, 1024), (0, 1056), ('z', 96)],
        'c_w_uq': sum([[(2 * hp * 96, 64), ((2 * hp + 1) * 96, 64), (2 * hp * 96 + 64, 32), ((2 * hp + 1) * 96 + 64, 32),
                        ('z', 64)] for hp in range(8)], []),
        'c_w_ukv': sum([[(2 * hp * 128, 64), ((2 * hp + 1) * 128, 64), (2 * hp * 128 + 64, 64),
                         ((2 * hp + 1) * 128 + 64, 64)] for hp in range(8)], [])}


def perm_index(name):
    return np.concatenate([np.full(p[1], -1) if p[0] == 'z' else np.arange(p[0], p[0] + p[1]) for p in PERM[name]])


SMALL_OFF = {}
_o = 0
for _n in SHARDED_F32:
    SMALL_OFF[_n] = _o
    _o += int(np.prod(_block_shape(_n)))
SMALL_ROWS = _cdiv(_o, 128 * 8) * 8


def _pick(n, cands):
    for c in cands:
        if n % c == 0:
            return c
    return n


def mm(a, b, mode, name, out_dtype=F32, comm=None):
    if mode == 'nn':
        (M, K), (K2, N) = a.shape, b.shape
    elif mode == 'nt':
        (M, K), (N, K2) = a.shape, b.shape
    else:
        (K, M), (K2, N) = a.shape, b.shape
    assert K == K2, (name, a.shape, b.shape)
    tm = _pick(M, (1024, 768, 512, 256, 128))
    tn = _pick(N, (512, 384, 256))
    dims = {'nn': ((1,), (0,)), 'nt': ((1,), (1,)), 'tn': ((0,), (0,))}[mode]

    def body(a_ref, b_ref, o_ref):
        o_ref[...] = lax.dot_general(a_ref[...].astype(BF16), b_ref[...].astype(BF16), (dims, ((), ())),
                                     preferred_element_type=F32).astype(out_dtype)

    a_spec = pl.BlockSpec((K, tm), lambda i, j: (0, i)) if mode == 'tn' else pl.BlockSpec((tm, K), lambda i, j: (i, 0))
    b_spec = pl.BlockSpec((tn, K), lambda i, j: (j, 0)) if mode == 'nt' else pl.BlockSpec((K, tn), lambda i, j: (0, j))
    res = carried(body, comm, grid=(M // tm, N // tn), in_specs=[a_spec, b_spec],
                  out_specs=pl.BlockSpec((tm, tn), lambda i, j: (i, j)), out_shape=jax.ShapeDtypeStruct((M, N), out_dtype),
                  semantics=("parallel", "parallel"), name=name)(a, b)
    return res[0] if comm is None else res


def rw(arr, width=None, cb=0):
    return (arr, arr.shape[1] if width is None else width, cb)


def rowwise(fn, rows, consts, outs, accs, tl, name, n_steps=None, comm=None):
    if n_steps is None:
        n_steps = [r[0].shape[0] for r in rows if not isinstance(r[1], pl.BlockSpec)][0] // tl
    L = n_steps * tl
    nr, nc, no, na = len(rows), len(consts), len(outs), len(accs)
    in_specs, args = [], []
    for r in rows:
        if isinstance(r[1], pl.BlockSpec):
            in_specs.append(r[1])
        else:
            in_specs.append(pl.BlockSpec((tl, r[1]), functools.partial(lambda i, cb: (i, cb), cb=r[2])))
        args.append(r[0])
    for c in consts:
        in_specs.append(pl.BlockSpec(c.shape, functools.partial(lambda i, nd: (0,) * nd, nd=c.ndim)))
        args.append(c)
    out_specs = [pl.BlockSpec((tl, w), lambda i: (i, 0)) for w, _ in outs]
    out_shape = [jax.ShapeDtypeStruct((L, w), dt) for w, dt in outs]
    for s in accs:
        out_specs.append(pl.BlockSpec(s, functools.partial(lambda i, nd: (0,) * nd, nd=len(s))))
        out_shape.append(jax.ShapeDtypeStruct(s, F32))

    def body(*refs):
        ins = [r[...] for r in refs[:nr + nc]]
        o_refs = refs[nr + nc:nr + nc + no]
        a_refs = refs[nr + nc + no:]
        o_vals, a_vals = fn(*ins)
        for ref, val in zip(o_refs, o_vals):
            ref[...] = val.astype(ref.dtype)
        if na:
            @pl.when(pl.program_id(0) == 0)
            def _():
                for ref in a_refs:
                    ref[...] = jnp.zeros_like(ref)
            for ref, val in zip(a_refs, a_vals):
                ref[...] += val

    res, carried_out = carried(body, comm, grid=(n_steps,), in_specs=in_specs, out_specs=out_specs, out_shape=out_shape,
                               name=name, semantics=("arbitrary",))(*args)
    if comm is None:
        return res[:no], res[no:]
    return res[:no], res[no:], carried_out


def carried(body, comm, *, grid, in_specs, out_specs, out_shape, name, semantics, scratch_shapes=()):
    single = not isinstance(out_shape, (list, tuple))
    o_specs = [out_specs] if single else list(out_specs)
    o_shape = [out_shape] if single else list(out_shape)
    if comm is None:
        call = pl.pallas_call(body, grid=grid, in_specs=in_specs, out_specs=out_specs, out_shape=out_shape,
                              scratch_shapes=list(scratch_shapes),
                              compiler_params=pltpu.CompilerParams(dimension_semantics=semantics), name=name)
        return lambda *args: (call(*args), None)
    n_in, n_out, n_sc = len(in_specs), len(o_specs), len(scratch_shapes)
    ci, co = len(comm.ins), len(comm.outs)
    n_steps = int(np.prod(grid))
    hooks = comm.hooks(n_steps)

    def wrapped(*refs):
        ins, cins = refs[:n_in], refs[n_in:n_in + ci]
        outs, couts = refs[n_in + ci:n_in + ci + n_out], refs[n_in + ci + n_out:n_in + ci + n_out + co]
        sc, csc = refs[n_in + ci + n_out + co:n_in + ci + n_out + co + n_sc], refs[n_in + ci + n_out + co + n_sc:]
        step = pl.program_id(0)
        for ax in range(1, len(grid)):
            step = step * grid[ax] + pl.program_id(ax)
        for at, fn, after in hooks:
            if not after:
                pl.when(step == at)(functools.partial(fn, cins, couts, csc))
        body(*ins, *outs, *sc)
        for at, fn, after in hooks:
            if after:
                pl.when(step == at)(functools.partial(fn, cins, couts, csc))

    call = pl.pallas_call(wrapped, grid=grid, in_specs=list(in_specs) + [ANY] * ci, out_specs=o_specs + [ANY] * co,
                          out_shape=o_shape + list(comm.outs), scratch_shapes=list(scratch_shapes) + list(comm.scratch),
                          compiler_params=pltpu.CompilerParams(dimension_semantics=("arbitrary",) * len(grid)), name=name)

    def run(*args):
        res = call(*args, *comm.ins)
        return (res[0] if single else res[:n_out]), res[n_out:]
    return run


_K0 = math.sqrt(2.0 / math.pi)
_K1 = 0.044715


def gelu(x):
    return x * (0.5 * (1.0 + jnp.tanh(_K0 * (x + _K1 * (x * x * x)))))


def gelu_grad(x):
    t = jnp.tanh(_K0 * (x + _K1 * (x * x * x)))
    return 0.5 * (1.0 + t) + 0.5 * x * (1.0 - t * t) * (_K0 * (1.0 + 3.0 * _K1 * x * x))


def sigmoid(x):
    return 1.0 / (1.0 + jnp.exp(-x))


def silu(z):
    return z * sigmoid(z)


def silu_grad(z):
    s = sigmoid(z)
    return s * (1.0 + z * (1.0 - s))


def rms_fwd(x, g):
    r = lax.rsqrt(jnp.mean(x * x, axis=-1, keepdims=True) + EPS)
    return x * r * g


def rms_bwd(x, g, dy):
    r = lax.rsqrt(jnp.mean(x * x, axis=-1, keepdims=True) + EPS)
    xh = x * r
    dg = jnp.sum(dy * xh, axis=0, keepdims=True)
    dxh = dy * g
    dx = r * (dxh - xh * jnp.mean(dxh * xh, axis=-1, keepdims=True))
    return dx, dg


def _scan_chunk(a_r, a_i, pr_ref, pi_ref, cr, ci, T, reverse):
    ts = min(SSM_TS, T)
    sgn = -1.0 if reverse else 1.0
    row = lax.broadcasted_iota(jnp.int32, (ts, a_r.shape[1]), 0)
    pw = (lambda e: T - e) if reverse else (lambda e: e - 1)
    if reverse:
        wr_c, wi_c = pr_ref[T - ts:T, :], sgn * pi_ref[T - ts:T, :]
    else:
        wr_c, wi_c = pr_ref[0:ts, :], sgn * pi_ref[0:ts, :]
    c_r, c_i = cr[...], ci[...]
    outs = []
    subs = range(T // ts)
    for sub in (reversed(subs) if reverse else subs):
        v_r, v_i = a_r[sub * ts:(sub + 1) * ts], a_i[sub * ts:(sub + 1) * ts]
        d = 1
        while d < ts:
            wr = pr_ref[pw(d):pw(d) + 1, :]
            wi = sgn * pi_ref[pw(d):pw(d) + 1, :]
            if reverse:
                yr, yi, keep = pltpu.roll(v_r, ts - d, 0), pltpu.roll(v_i, ts - d, 0), row < ts - d
            else:
                yr, yi, keep = pltpu.roll(v_r, d, 0), pltpu.roll(v_i, d, 0), row >= d
            v_r, v_i = (v_r + jnp.where(keep, wr * yr - wi * yi, 0.0), v_i + jnp.where(keep, wr * yi + wi * yr, 0.0))
            d *= 2
        v_r, v_i = v_r + (wr_c * c_r - wi_c * c_i), v_i + (wr_c * c_i + wi_c * c_r)
        k = 0 if reverse else ts - 1
        c_r, c_i = v_r[k:k + 1, :], v_i[k:k + 1, :]
        outs.append((v_r, v_i))
    if reverse:
        outs = outs[::-1]
    cr[...] = c_r
    ci[...] = c_i
    return jnp.concatenate([o[0] for o in outs], axis=0), jnp.concatenate([o[1] for o in outs], axis=0)


_NT = (((1,), (1,)), ((), ()))
_TN = (((0,), (0,)), ((), ()))


def s5_fwd(proj, d_skip, Bre, Bim, Cre, Cim, pr, pi, comm=None):
    L = proj.shape[0]
    T, WC = min(SSM_T, L), SSM_WC
    nT = L // T

    def body(u_ref, d_ref, bre_ref, bim_ref, cre_ref, cim_ref, pr_ref, pi_ref, y_ref, yg_ref, sr_ref, si_ref, cr, ci):
        @pl.when(pl.program_id(1) == 0)
        def _():
            cr[...] = jnp.zeros_like(cr)
            ci[...] = jnp.zeros_like(ci)

        u = u_ref[...]
        ub = u.astype(BF16)
        a_r = lax.dot_general(ub, bre_ref[0].astype(BF16), _NT, preferred_element_type=F32)
        a_i = lax.dot_general(ub, bim_ref[0].astype(BF16), _NT, preferred_element_type=F32)
        a_r, a_i = _scan_chunk(a_r, a_i, pr_ref, pi_ref, cr, ci, T, False)
        sr_ref[...] = a_r
        si_ref[...] = a_i
        y = (lax.dot_general(a_r.astype(BF16), cre_ref[0].astype(BF16), _NT, preferred_element_type=F32)
             + lax.dot_general(a_i.astype(BF16), cim_ref[0].astype(BF16), _NT, preferred_element_type=F32)
             + d_ref[...] * u)
        y_ref[...] = y
        yg_ref[...] = gelu(y)

    uspec = pl.BlockSpec((T, 128), lambda k, i: (i, k))
    sspec = pl.BlockSpec((T, WC), lambda k, i: (i, k))
    return carried(
        body, comm, grid=(8, nT),
        in_specs=[uspec, pl.BlockSpec((1, 128), lambda k, i: (0, k)),
                  pl.BlockSpec((1, WC, 128), lambda k, i: (k, 0, 0)), pl.BlockSpec((1, WC, 128), lambda k, i: (k, 0, 0)),
                  pl.BlockSpec((1, 128, WC), lambda k, i: (k, 0, 0)), pl.BlockSpec((1, 128, WC), lambda k, i: (k, 0, 0)),
                  pl.BlockSpec((T, WC), lambda k, i: (0, k)), pl.BlockSpec((T, WC), lambda k, i: (0, k))],
        out_specs=[uspec, uspec, sspec, sspec],
        out_shape=[jax.ShapeDtypeStruct((L, 1024), F32)] * 2 + [jax.ShapeDtypeStruct((L, 8 * WC), F32)] * 2,
        scratch_shapes=[pltpu.VMEM((1, WC), F32), pltpu.VMEM((1, WC), F32)],
        semantics=("parallel", "arbitrary"), name='a_ssm')(proj, d_skip, Bre, Bim, Cre, Cim, pr, pi)


def s5_bwd(proj, dyg1, dyg2, y, d_skip, s_re, s_im, Bre, Bim, Cre, Cim, prr, pir, comm=None):
    L = proj.shape[0]
    T, WC = min(SSM_T, L), SSM_WC
    nT = L // T

    def body(u_ref, g1_ref, g2_ref, y_ref, d_ref, sr_ref, si_ref, spr_ref, spi_ref, bre_ref, bim_ref, cre_ref, cim_ref,
             pr_ref, pi_ref, du_ref, dd_ref, dbre_ref, dbim_ref, dcre_ref, dcim_ref, dar_ref, dai_ref, cr, ci):
        i = pl.program_id(1)

        @pl.when(i == 0)
        def _():
            for ref in (cr, ci, dd_ref, dbre_ref, dbim_ref, dcre_ref, dcim_ref, dar_ref, dai_ref):
                ref[...] = jnp.zeros_like(ref)

        u = u_ref[...]
        dy = (g1_ref[...] + g2_ref[...]) * gelu_grad(y_ref[...])
        dd_ref[...] += jnp.sum(dy * u, axis=0, keepdims=True)
        dyb, ub = dy.astype(BF16), u.astype(BF16)
        bre, bim, cre, cim = (r[0].astype(BF16) for r in (bre_ref, bim_ref, cre_ref, cim_ref))
        g_r = jnp.dot(dyb, cre, preferred_element_type=F32)
        g_i = jnp.dot(dyb, cim, preferred_element_type=F32)
        g_r, g_i = _scan_chunk(g_r, g_i, pr_ref, pi_ref, cr, ci, T, True)
        s_r, s_i = sr_ref[...], si_ref[...]
        row = lax.broadcasted_iota(jnp.int32, (T, WC), 0)
        first = (nT - 1 - i) == 0
        sp_r = jnp.where(row == 0, jnp.where(first, 0.0, spr_ref[7:8, :]), pltpu.roll(s_r, 1, 0))
        sp_i = jnp.where(row == 0, jnp.where(first, 0.0, spi_ref[7:8, :]), pltpu.roll(s_i, 1, 0))
        dar_ref[...] += jnp.sum(g_r * sp_r + g_i * sp_i, axis=0, keepdims=True)
        dai_ref[...] += jnp.sum(g_i * sp_r - g_r * sp_i, axis=0, keepdims=True)
        grb, gib = g_r.astype(BF16), g_i.astype(BF16)
        dcre_ref[0] += lax.dot_general(dyb, s_r.astype(BF16), _TN, preferred_element_type=F32)
        dcim_ref[0] += lax.dot_general(dyb, s_i.astype(BF16), _TN, preferred_element_type=F32)
        dbre_ref[0] += lax.dot_general(grb, ub, _TN, preferred_element_type=F32)
        dbim_ref[0] += lax.dot_general(gib, ub, _TN, preferred_element_type=F32)
        du_ref[...] = (dy * d_ref[...] + jnp.dot(grb, bre, preferred_element_type=F32)
                       + jnp.dot(gib, bim, preferred_element_type=F32))

    uspec = pl.BlockSpec((T, 128), lambda k, i: (nT - 1 - i, k))
    sspec = pl.BlockSpec((T, WC), lambda k, i: (nT - 1 - i, k))
    pspec = pl.BlockSpec((8, WC), lambda k, i: (jnp.maximum((nT - 1 - i) * (T // 8) - 1, 0), k))
    tab = pl.BlockSpec((T, WC), lambda k, i: (0, k))
    bspec = pl.BlockSpec((1, WC, 128), lambda k, i: (k, 0, 0))
    cspec = pl.BlockSpec((1, 128, WC), lambda k, i: (k, 0, 0))
    return carried(
        body, comm, grid=(8, nT),
        in_specs=[uspec, uspec, uspec, uspec, pl.BlockSpec((1, 128), lambda k, i: (0, k)), sspec, sspec, pspec, pspec,
                  bspec, bspec, cspec, cspec, tab, tab],
        out_specs=[uspec, pl.BlockSpec((1, 128), lambda k, i: (0, k)), bspec, bspec, cspec, cspec,
                   pl.BlockSpec((1, WC), lambda k, i: (0, k)), pl.BlockSpec((1, WC), lambda k, i: (0, k))],
        out_shape=[jax.ShapeDtypeStruct((L, 1024), F32), jax.ShapeDtypeStruct((1, 1024), F32),
                   jax.ShapeDtypeStruct((8, WC, 128), F32), jax.ShapeDtypeStruct((8, WC, 128), F32),
                   jax.ShapeDtypeStruct((8, 128, WC), F32), jax.ShapeDtypeStruct((8, 128, WC), F32),
                   jax.ShapeDtypeStruct((1, 8 * WC), F32), jax.ShapeDtypeStruct((1, 8 * WC), F32)],
        scratch_shapes=[pltpu.VMEM((1, WC), F32), pltpu.VMEM((1, WC), F32)],
        semantics=("parallel", "arbitrary"), name='a_ssm_bwd')(
            proj, dyg1, dyg2, y, d_skip, s_re, s_im, s_re, s_im, Bre, Bim, Cre, Cim, prr, pir)


def s5_discretize(lam_re, lam_im, log_dt, b_re, b_im):
    dt = jnp.exp(log_dt)[:, None]
    mag = jnp.exp(lam_re * dt)
    ab_re = mag * jnp.cos(lam_im * dt)
    ab_im = mag * jnp.sin(lam_im * dt)
    den = lam_re * lam_re + lam_im * lam_im
    nr = ab_re - 1.0
    f_re = (nr * lam_re + ab_im * lam_im) / den
    f_im = (ab_im * lam_re - nr * lam_im) / den
    bb_re = f_re[..., None] * b_re - f_im[..., None] * b_im
    bb_im = f_re[..., None] * b_im + f_im[..., None] * b_re
    return ab_re, ab_im, bb_re, bb_im


def s5_prep(bb_re, bb_im, c_re, c_im, ar, ai, T, comm=None):
    W = ar.shape[1]

    def body(bbr_ref, bbi_ref, cre_ref, cim_ref, ar_ref, ai_ref, btr_ref, bti_ref, ctr_ref, cti_ref, fr_ref, fi_ref,
             rr_ref, ri_ref):
        for ref in (btr_ref, bti_ref, ctr_ref, cti_ref):
            ref[...] = jnp.zeros_like(ref)
        for g in range(8):
            rows, cols = slice(g * SSM_P, (g + 1) * SSM_P), slice(g * SSM_H, (g + 1) * SSM_H)
            btr_ref[0, rows, cols] = bbr_ref[g]
            bti_ref[0, rows, cols] = bbi_ref[g]
            ctr_ref[0, cols, rows] = cre_ref[g]
            cti_ref[0, cols, rows] = -cim_ref[g]
        fr_ref[0:1, :] = ar_ref[...]
        fi_ref[0:1, :] = ai_ref[...]
        rr_ref[T - 1:T, :] = ar_ref[...]
        ri_ref[T - 1:T, :] = ai_ref[...]
        n = 1
        while n < T:
            cr, ci = fr_ref[0:n, :], fi_ref[0:n, :]
            lr, li = fr_ref[n - 1:n, :], fi_ref[n - 1:n, :]
            fr_ref[n:2 * n, :] = cr * lr - ci * li
            fi_ref[n:2 * n, :] = cr * li + ci * lr
            cr, ci = rr_ref[T - n:T, :], ri_ref[T - n:T, :]
            rr_ref[T - 2 * n:T - n, :] = cr * lr - ci * li
            ri_ref[T - 2 * n:T - n, :] = cr * li + ci * lr
            n *= 2

    spec = pl.BlockSpec((T, SSM_WC), lambda j: (0, j))
    aspec = pl.BlockSpec((1, SSM_WC), lambda j: (0, j))
    bspec, cspec = pl.BlockSpec((8, SSM_P, SSM_H), lambda j: (j, 0, 0)), pl.BlockSpec((8, SSM_H, SSM_P), lambda j: (j, 0, 0))
    btspec = pl.BlockSpec((1, SSM_WC, 128), lambda j: (j, 0, 0))
    ctspec = pl.BlockSpec((1, 128, SSM_WC), lambda j: (j, 0, 0))
    return carried(
        body, comm, grid=(W // SSM_WC,), in_specs=[bspec, bspec, cspec, cspec, aspec, aspec],
        out_specs=[btspec, btspec, ctspec, ctspec] + [spec] * 4,
        out_shape=[jax.ShapeDtypeStruct((8, SSM_WC, 128), F32)] * 2 + [jax.ShapeDtypeStruct((8, 128, SSM_WC), F32)] * 2
        + [jax.ShapeDtypeStruct((T, W), F32)] * 4,
        semantics=("parallel",), name='a_prep')(bb_re, bb_im, c_re, c_im, ar, ai)


def s5_untile(dbtr, dbti, dctr, dcti):
    def body(dbtr_ref, dbti_ref, dctr_ref, dcti_ref, br_ref, bi_ref, cr_ref, ci_ref):
        for g in range(8):
            rows, cols = slice(g * SSM_P, (g + 1) * SSM_P), slice(g * SSM_H, (g + 1) * SSM_H)
            br_ref[g] = dbtr_ref[0, rows, cols]
            bi_ref[g] = dbti_ref[0, rows, cols]
            cr_ref[g] = dctr_ref[0, cols, rows]
            ci_ref[g] = -dcti_ref[0, cols, rows]

    bspec, cspec = pl.BlockSpec((8, SSM_P, SSM_H), lambda j: (j, 0, 0)), pl.BlockSpec((8, SSM_H, SSM_P), lambda j: (j, 0, 0))
    btspec = pl.BlockSpec((1, SSM_WC, 128), lambda j: (j, 0, 0))
    ctspec = pl.BlockSpec((1, 128, SSM_WC), lambda j: (j, 0, 0))
    return pl.pallas_call(
        body, grid=(8,), in_specs=[btspec, btspec, ctspec, ctspec], out_specs=[bspec, bspec, cspec, cspec],
        out_shape=[jax.ShapeDtypeStruct((SSM_G, SSM_P, SSM_H), F32)] * 2 + [jax.ShapeDtypeStruct((SSM_G, SSM_H, SSM_P), F32)] * 2,
        compiler_params=pltpu.CompilerParams(dimension_semantics=("parallel",)), name='a_untile')(dbtr, dbti, dctr, dcti)


def layer_a_fwd(h, w, p, comm=None, on_carried=None, prep_comm=None, on_prep=None):
    L = h.shape[0]
    disc = lambda *a: s5_discretize(*a)
    (ab_re, ab_im, bb_re, bb_im), disc_vjp = jax.vjp(disc, p['a_lam_re'][0], p['a_lam_im'][0], p['a_log_dt'][0],
                                                     p['a_b_re'][0], p['a_b_im'][0])
    T = min(SSM_T, L)
    (Bre, Bim, Cre, Cim, pr, pi, prr, pir), prepped = s5_prep(bb_re, bb_im, p['a_c_re'][0], p['a_c_im'][0],
                                                              ab_re.reshape(1, -1), ab_im.reshape(1, -1), T,
                                                              comm=prep_comm)
    if on_prep is not None:
        on_prep(prepped)
    proj = mm(h, w['a_w_in'], 'nn', 'a_proj')
    (y, yg, s_re, s_im), carried_out = s5_fwd(proj, p['a_d'], Bre, Bim, Cre, Cim, pr, pi, comm=comm)
    if on_carried is not None:
        on_carried(carried_out)
    gl = mm(yg, w['a_w_glu'], 'nn', 'a_glu')

    def f2(yg_, gl_, z, bg):
        return [yg_ * sigmoid(gl_ + bg) * silu(z)], []
    (po,), _ = rowwise(f2, [rw(yg), rw(gl), rw(proj, 1024, 1)], [p['a_b_glu']], [(1024, BF16)], [], 256, 'a_gate')
    yb = mm(po, w['a_w_out'], 'nn', 'a_out')
    saved = dict(carried=carried_out, h=h, proj=proj, disc_vjp=disc_vjp, Bre=Bre, Bim=Bim, Cre=Cre, Cim=Cim, prr=prr, pir=pir, s_re=s_re,
                 s_im=s_im, y=y, yg=yg, gl=gl, po=po)
    return yb, saved


def _dw(g, sink, name, a, b, mm_name):
    if sink is None:
        g[name] = mm(a, b, 'tn', mm_name)
    else:
        sink.put(name, a, b, mm_name)


def layer_a_bwd(dyb, w, p, sv, comm=None, sink=None):
    g = {}
    dpo = mm(dyb, w['a_w_out'], 'nt', 'a_dpo')
    _dw(g, sink, 'a_w_out', sv['po'], dyb, 'a_dwout')
    proj = sv['proj']

    def f1(dpo_, yg, gl, z, bg):
        sg = sigmoid(gl + bg)
        sz = silu(z)
        dm = dpo_ * sz
        dz = dpo_ * (yg * sg) * silu_grad(z)
        dgl = dm * yg * sg * (1.0 - sg)
        return [dz, dm * sg, dgl], [jnp.sum(dgl, axis=0, keepdims=True)]
    (dz, dyg1, dgl), (db_glu,) = rowwise(f1, [rw(dpo), rw(sv['yg']), rw(sv['gl']), rw(proj, 1024, 1)], [p['a_b_glu']],
                                          [(1024, F32), (1024, F32), (1024, BF16)], [(1, 1024)], 256, 'a_gate_bwd')
    g['a_b_glu'] = db_glu
    _dw(g, sink, 'a_w_glu', sv['yg'], dgl, 'a_dwglu')
    dyg2 = mm(dgl, w['a_w_glu'], 'nt', 'a_dyg2')

    if callable(comm):
        comm = comm()
    (du, dd, dBre, dBim, dCre, dCim, da_re, da_im), g['carried'] = s5_bwd(
        proj, dyg1, dyg2, sv['y'], p['a_d'], sv['s_re'], sv['s_im'], sv['Bre'], sv['Bim'], sv['Cre'], sv['Cim'],
        sv['prr'], sv['pir'], comm=comm)
    g['a_d'] = dd

    def f3(du_, dz_):
        return [jnp.concatenate([du_, dz_], axis=1)], []
    (dproj,), _ = rowwise(f3, [rw(du), rw(dz)], [], [(2048, BF16)], [], 256, 'a_dproj')
    dbb_re, dbb_im, dc_re, dc_im = s5_untile(dBre, dBim, dCre, dCim)
    dlr, dli, dldt, dbr, dbi = sv['disc_vjp']((da_re.reshape(SSM_G, SSM_P), da_im.reshape(SSM_G, SSM_P), dbb_re, dbb_im))
    g['a_lam_re'], g['a_lam_im'], g['a_log_dt'] = dlr[None], dli[None], dldt[None]
    g['a_b_re'], g['a_b_im'] = dbr[None], dbi[None]
    g['a_c_re'], g['a_c_im'] = dc_re[None], dc_im[None]
    _dw(g, sink, 'a_w_in', sv['h'], dproj, 'a_dwin')
    if sink is None:
        dh = mm(dproj, w['a_w_in'], 'nt', 'a_dh')
    else:
        dh, (g['land_a1'],) = mm(dproj, w['a_w_in'], 'nt', 'a_dh', comm=SiblingExchange(sink.bufs['a1']))
    return dh, g


def _t5_bucket_np():
    qi = np.arange(WINDOW)[:, None]
    kj = np.arange(2 * WINDOW)[None, :]
    dist = np.maximum(qi + WINDOW - kj, 0)
    max_exact = REL_BUCKETS // 2
    dist_f = np.maximum(dist, 1).astype(np.float32)
    large = max_exact + (np.log(dist_f / np.float32(max_exact)) / np.float32(math.log(REL_MAX_DIST / max_exact))
                         * np.float32(REL_BUCKETS - max_exact)).astype(np.int32)
    large = np.minimum(large, REL_BUCKETS - 1)
    return np.where(dist < max_exact, dist, large).astype(np.int32)


SWA_GRP = SWA_HEADS // SWA_KV


def _swa_kv(kvp, kvc, kvh):
    kb = jnp.concatenate([kvp[:, kvh * 64:(kvh + 1) * 64], kvc[:, kvh * 64:(kvh + 1) * 64]], 0).astype(BF16)
    vb = jnp.concatenate([kvp[:, 128 + kvh * 64:128 + (kvh + 1) * 64], kvc[:, 128 + kvh * 64:128 + (kvh + 1) * 64]],
                         0).astype(BF16)
    return kb, vb


def _swa_stack(x, kvh):
    return jnp.concatenate([x[:, (kvh * SWA_GRP + g) * 64:(kvh * SWA_GRP + g + 1) * 64] for g in range(SWA_GRP)],
                           axis=0).astype(BF16)


def _swa_group(bias_ref, kvh):
    return bias_ref[kvh * SWA_GRP:(kvh + 1) * SWA_GRP].reshape(SWA_GRP * WINDOW, 2 * WINDOW)


def _swa_sinks(sink_ref, kvh):
    return jnp.concatenate([jnp.broadcast_to(sink_ref[0:1, kvh * SWA_GRP + g:kvh * SWA_GRP + g + 1], (WINDOW, 1))
                            for g in range(SWA_GRP)], axis=0)


def _swa_probs(q, kb, bias_h, sink, valid):
    s = lax.dot_general(q, kb, (((1,), (1,)), ((), ())), preferred_element_type=F32) * (HEAD_DIM ** -0.5)
    s = jnp.where(valid, s + bias_h, NEG_INF)
    m = jnp.maximum(jnp.max(s, axis=-1, keepdims=True), sink)
    e = jnp.exp(s - m)
    es = jnp.exp(sink - m)
    den = jnp.sum(e, axis=-1, keepdims=True) + es
    return e / den, es / den


def _swa_valid(n):
    qi = lax.broadcasted_iota(jnp.int32, (SWA_GRP * WINDOW, 2 * WINDOW), 0) & (WINDOW - 1)
    kj = lax.broadcasted_iota(jnp.int32, (SWA_GRP * WINDOW, 2 * WINDOW), 1)
    dist = qi + WINDOW - kj
    return (dist >= 0) & (dist < WINDOW) & ((kj >= WINDOW) | (n > 0))


def swa_fwd(proj, bias, sinks, comm=None):
    L = proj.shape[0]

    def body(z_ref, q_ref, kvc_ref, kvp_ref, bias_ref, sink_ref, o_ref, po_ref):
        n = pl.program_id(0)
        valid = _swa_valid(n)
        q, kvc, kvp = q_ref[...], kvc_ref[...], kvp_ref[...]
        outs = []
        for kvh in range(SWA_KV):
            kb, vb = _swa_kv(kvp, kvc, kvh)
            p, _ = _swa_probs(_swa_stack(q, kvh), kb, _swa_group(bias_ref, kvh), _swa_sinks(sink_ref, kvh), valid)
            o8 = jnp.dot(p.astype(BF16), vb, preferred_element_type=F32)
            outs += [o8[g * WINDOW:(g + 1) * WINDOW] for g in range(SWA_GRP)]
        o = jnp.concatenate(outs, axis=1)
        o_ref[...] = o
        po_ref[...] = (o * silu(z_ref[...])).astype(po_ref.dtype)

    return carried(
        body, comm, grid=(L // WINDOW,),
        in_specs=[pl.BlockSpec((WINDOW, 1024), lambda n: (n, 0)), pl.BlockSpec((WINDOW, 1024), lambda n: (n, 1)),
                  pl.BlockSpec((WINDOW, 256), lambda n: (n, 8)),
                  pl.BlockSpec((WINDOW, 256), lambda n: (jnp.maximum(n - 1, 0), 8)),
                  pl.BlockSpec((SWA_HEADS, WINDOW, 2 * WINDOW), lambda n: (0, 0, 0)),
                  pl.BlockSpec((1, SWA_HEADS), lambda n: (0, 0))],
        out_specs=[pl.BlockSpec((WINDOW, 1024), lambda n: (n, 0))] * 2,
        out_shape=[jax.ShapeDtypeStruct((L, 1024), F32), jax.ShapeDtypeStruct((L, 1024), BF16)],
        semantics=("parallel",), name='b_attn')(proj, proj, proj, proj, bias, sinks)


def swa_bwd(proj, do, bias, sinks, comm=None):
    L = proj.shape[0]

    def body(q_ref, kvc_ref, kvp_ref, do_ref, bias_ref, sink_ref, dq_ref, dkv_ref, dbias_ref, dsink_ref):
        n = pl.program_id(0)

        @pl.when(n == 0)
        def _():
            dkv_ref[...] = jnp.zeros_like(dkv_ref)
            dbias_ref[...] = jnp.zeros_like(dbias_ref)
            dsink_ref[...] = jnp.zeros_like(dsink_ref)

        valid = _swa_valid(n)
        q, kvc, kvp, do_ = q_ref[...], kvc_ref[...], kvp_ref[...], do_ref[...]
        dqs, dks, dvs, dsk = [], [], [], []
        for kvh in range(SWA_KV):
            kb, vb = _swa_kv(kvp, kvc, kvh)
            q8, do8 = _swa_stack(q, kvh), _swa_stack(do_, kvh)
            p, ps = _swa_probs(q8, kb, _swa_group(bias_ref, kvh), _swa_sinks(sink_ref, kvh), valid)
            dp = lax.dot_general(do8, vb, (((1,), (1,)), ((), ())), preferred_element_type=F32)
            delta = jnp.sum(p * dp, axis=-1, keepdims=True)
            ds = p * (dp - delta)
            col = -ps * delta
            dsk += [jnp.sum(col[g * WINDOW:(g + 1) * WINDOW], axis=0, keepdims=True) for g in range(SWA_GRP)]
            dbias_ref[kvh * SWA_GRP:(kvh + 1) * SWA_GRP] += ds.reshape(SWA_GRP, WINDOW, 2 * WINDOW)
            dsb = (ds * (HEAD_DIM ** -0.5)).astype(BF16)
            dq8 = jnp.dot(dsb, kb, preferred_element_type=F32)
            dqs += [dq8[g * WINDOW:(g + 1) * WINDOW] for g in range(SWA_GRP)]
            dks.append(lax.dot_general(dsb, q8, (((0,), (0,)), ((), ())), preferred_element_type=F32))
            dvs.append(lax.dot_general(p.astype(BF16), do8, (((0,), (0,)), ((), ())), preferred_element_type=F32))
        dq_ref[...] = jnp.concatenate(dqs, axis=1)
        dsink_ref[...] += jnp.concatenate(dsk, axis=1)
        both = jnp.concatenate(dks + dvs, axis=1)
        r_cur = pl.multiple_of(n * WINDOW, WINDOW)
        r_prev = pl.multiple_of(jnp.maximum(n - 1, 0) * WINDOW, WINDOW)
        dkv_ref[pl.ds(r_prev, WINDOW), :] += both[:WINDOW]
        dkv_ref[pl.ds(r_cur, WINDOW), :] += both[WINDOW:]

    return carried(
        body, comm, grid=(L // WINDOW,),
        in_specs=[pl.BlockSpec((WINDOW, 1024), lambda n: (n, 1)), pl.BlockSpec((WINDOW, 256), lambda n: (n, 8)),
                  pl.BlockSpec((WINDOW, 256), lambda n: (jnp.maximum(n - 1, 0), 8)),
                  pl.BlockSpec((WINDOW, 1024), lambda n: (n, 0)),
                  pl.BlockSpec((SWA_HEADS, WINDOW, 2 * WINDOW), lambda n: (0, 0, 0)),
                  pl.BlockSpec((1, SWA_HEADS), lambda n: (0, 0))],
        out_specs=[pl.BlockSpec((WINDOW, 1024), lambda n: (n, 0)), pl.BlockSpec((L, 256), lambda n: (0, 0)),
                   pl.BlockSpec((SWA_HEADS, WINDOW, 2 * WINDOW), lambda n: (0, 0, 0)),
                   pl.BlockSpec((1, SWA_HEADS), lambda n: (0, 0))],
        out_shape=[jax.ShapeDtypeStruct((L, 1024), F32), jax.ShapeDtypeStruct((L, 256), F32),
                   jax.ShapeDtypeStruct((SWA_HEADS, WINDOW, 2 * WINDOW), F32), jax.ShapeDtypeStruct((1, SWA_HEADS), F32)],
        semantics=("arbitrary",), name='b_attn_bwd')(proj, proj, proj, do, bias, sinks)


def swa_bias(rel_bias):
    def body(bk_ref, rb_ref, o_ref):
        bk = bk_ref[...]
        for h in range(SWA_HEADS):
            acc = jnp.zeros((WINDOW, 2 * WINDOW), F32)
            for b in range(REL_BUCKETS):
                acc = jnp.where(bk == b, rb_ref[b, h], acc)
            o_ref[h] = acc

    return pl.pallas_call(
        body, out_shape=jax.ShapeDtypeStruct((SWA_HEADS, WINDOW, 2 * WINDOW), F32),
        in_specs=[pl.BlockSpec(memory_space=pltpu.VMEM), pl.BlockSpec(memory_space=pltpu.SMEM)],
        out_specs=pl.BlockSpec(memory_space=pltpu.VMEM), name='b_bias')(jnp.asarray(_t5_bucket_np()), rel_bias)


def layer_b_fwd(h, w, p, comm=None):
    proj = mm(h, w['b_w_in'], 'nn', 'b_proj')
    bias = swa_bias(p['rel_bias'])
    (o, po), carried_out = swa_fwd(proj, bias, p['b_sinks'], comm=comm)
    yb = mm(po, w['b_w_out'], 'nn', 'b_out')
    return yb, dict(carried=carried_out, h=h, proj=proj, bias=bias, o=o, po=po)


def layer_b_bwd(dyb, w, p, sv, comm=None, sink=None):
    g = {}
    dpo = mm(dyb, w['b_w_out'], 'nt', 'b_dpo')
    _dw(g, sink, 'b_w_out', sv['po'], dyb, 'b_dwout')
    proj = sv['proj']

    def f1(dpo_, o, z):
        return [dpo_ * silu(z), dpo_ * o * silu_grad(z)], []
    (do, dz), _ = rowwise(f1, [rw(dpo), rw(sv['o']), rw(proj, 1024, 0)], [], [(1024, BF16), (1024, F32)], [], 256, 'b_gate_bwd')
    (dq, dkv, dbias, dsinks), g['carried'] = swa_bwd(proj, do, sv['bias'], p['b_sinks'], comm=comm)
    g['b_sinks'] = dsinks
    onehot = jnp.asarray(np.eye(REL_BUCKETS, dtype=np.float32)[_t5_bucket_np().reshape(-1)])

    def f2(db, oh):
        return [], [lax.dot_general(db, oh, (((1,), (0,)), ((), ())), preferred_element_type=F32,
                                    precision=lax.Precision.HIGHEST)]
    _, (drel,) = rowwise(f2, [(dbias.reshape(SWA_HEADS, -1), pl.BlockSpec((SWA_HEADS, 4096), lambda i: (0, i))),
                              (onehot, pl.BlockSpec((4096, REL_BUCKETS), lambda i: (i, 0)))], [], [],
                         [(SWA_HEADS, REL_BUCKETS)], 4096, 'b_drel', n_steps=(2 * WINDOW * WINDOW) // 4096)
    g['rel_bias'] = drel.T

    def f3(dz_, dq_, dkv_):
        return [jnp.concatenate([dz_, dq_, dkv_], axis=1)], []
    (dproj,), _ = rowwise(f3, [rw(dz), rw(dq), rw(dkv)], [], [(2304, BF16)], [], 256, 'b_dproj')
    _dw(g, sink, 'b_w_in', sv['h'], dproj, 'b_dwin')
    dh = mm(dproj, w['b_w_in'], 'nt', 'b_dh')
    return dh, g


MLA_SCALE = (MLA_NOPE + MLA_ROPE) ** -0.5
_LOG2E = math.log2(math.e)


def _rope_tables(L):
    inv = ROPE_BASE ** (-jnp.arange(0, MLA_ROPE, 2, dtype=F32) / MLA_ROPE)
    ang = jnp.arange(L, dtype=F32)[:, None] * inv[None, :]
    c, s = jnp.cos(ang), jnp.sin(ang)
    one, zero, pad = jnp.ones((L, 128), F32), jnp.zeros((L, 128), F32), jnp.zeros((L, 64), F32)
    return (jnp.concatenate([one, c, c, c, c, pad], 1), jnp.concatenate([zero, s, s, s, s, pad], 1))


def _rot(x, transpose=False):
    w = x.shape[1]
    lane = lax.broadcasted_iota(jnp.int32, x.shape, 1)
    up = pltpu.roll(x, w - 16, 1)
    dn = pltpu.roll(x, 16, 1)
    first = (lane % 32) < 16
    return jnp.where(first, up, -dn) if transpose else jnp.where(first, -up, dn)


MLA_QT = 512


def _mla_exp(qf, kf, t, qt):
    s = lax.dot_general(qf, kf, (((1,), (1,)), ((), ())), preferred_element_type=F32)
    causal = lax.broadcasted_iota(jnp.int32, (qt, qt), 1) <= lax.broadcasted_iota(jnp.int32, (qt, qt), 0)
    last = jnp.where(causal, s[:, t * qt:], NEG_INF)
    s = last if t == 0 else jnp.concatenate([s[:, :t * qt], last], axis=1)
    e = jnp.exp2((s - jnp.max(s, axis=-1, keepdims=True)) * (MLA_SCALE * _LOG2E))
    return e, jnp.sum(e, axis=-1, keepdims=True)


def _mla_heads(q, kv, kr):
    out = []
    for j in range(2):
        qf = jnp.concatenate([q[:, j * 64:(j + 1) * 64], q[:, 128 + j * 32:128 + (j + 1) * 32]], axis=1)
        kf = jnp.concatenate([kv[:, j * 64:(j + 1) * 64], kr], axis=1)
        out.append((qf, kf, kv[:, 128 + j * 64:128 + (j + 1) * 64]))
    return out


def mla_fwd(q, kv, kr, comm=None):
    L = q.shape[0]
    qt = min(MLA_QT, L)
    nq = L // qt

    def body(q_ref, kv_ref, kr_ref, o_ref):
        for t in range(nq):
            @pl.when(pl.program_id(1) == t)
            def _(t=t):
                n_k = (t + 1) * qt
                outs = []
                for qf, kf, v in _mla_heads(q_ref[...], kv_ref[0:n_k, :], kr_ref[0:n_k, 0:MLA_ROPE]):
                    e, den = _mla_exp(qf, kf, t, qt)
                    outs.append(jnp.dot(e.astype(BF16), v, preferred_element_type=F32) / den)
                o_ref[...] = jnp.concatenate(outs, axis=1)

    return carried(
        body, comm, grid=(MLA_HEADS // 2, nq),
        in_specs=[pl.BlockSpec((qt, 256), lambda hp, n: (n, hp)), pl.BlockSpec((L, 256), lambda hp, n: (0, hp)),
                  pl.BlockSpec((L, 128), lambda hp, n: (0, 0))],
        out_specs=pl.BlockSpec((qt, 128), lambda hp, n: (n, hp)), out_shape=jax.ShapeDtypeStruct((L, 1024), F32),
        semantics=("parallel", "parallel"), name='c_attn')(q, kv, kr)


def mla_bwd(q, kv, kr, do, o, comm=None):
    L = q.shape[0]
    qt = min(MLA_QT, L)
    nq = L // qt

    def body(q_ref, kv_ref, kr_ref, do_ref, o_ref, dq_ref, dkv_ref, dkr_ref):
        @pl.when(pl.program_id(1) == 0)
        def _():
            dkv_ref[...] = jnp.zeros_like(dkv_ref)
            dkr_ref[...] = jnp.zeros_like(dkr_ref)

        for t in range(nq):
            @pl.when(pl.program_id(1) == t)
            def _(t=t):
                n_k = (t + 1) * qt
                do_, o_ = do_ref[...], o_ref[...]
                dqn, dqr, dkn, dvs = [], [], [], []
                dkr = jnp.zeros((MLA_ROPE, n_k), F32)
                wide = lambda x: jnp.concatenate([x, jnp.zeros((qt, 128 - x.shape[1]), x.dtype)], axis=1)
                for j, (qf, kf, v) in enumerate(_mla_heads(q_ref[...], kv_ref[0:n_k, :], kr_ref[0:n_k, 0:MLA_ROPE])):
                    doh = do_[:, j * 64:(j + 1) * 64]
                    dof = doh.astype(F32)
                    e, den = _mla_exp(qf, kf, t, qt)
                    inv = 1.0 / den
                    dp = lax.dot_general(doh, v, (((1,), (1,)), ((), ())), preferred_element_type=F32)
                    delta = jnp.sum(dof * o_[:, j * 64:(j + 1) * 64], axis=-1, keepdims=True)
                    ds = (e * ((dp - delta) * (inv * MLA_SCALE))).astype(BF16)
                    dqf = jnp.dot(ds, kf, preferred_element_type=F32)
                    dkf = lax.dot_general(wide(qf), ds, _TN, preferred_element_type=F32)
                    dvf = lax.dot_general(wide((dof * inv).astype(BF16)), e.astype(BF16), _TN,
                                          preferred_element_type=F32)
                    dqn.append(dqf[:, :MLA_NOPE])
                    dqr.append(dqf[:, MLA_NOPE:])
                    dkn.append(dkf[:MLA_NOPE])
                    dvs.append(dvf[:MLA_V])
                    dkr = dkr + dkf[MLA_NOPE:MLA_NOPE + MLA_ROPE]
                dq_ref[...] = jnp.concatenate(dqn + dqr + [jnp.zeros((qt, 64), F32)], axis=1)
                dkv_ref[0:n_k, :] += jnp.concatenate(dkn + dvs, axis=0).T
                dkr_ref[0, 0:n_k, :] += jnp.concatenate([dkr, jnp.zeros((128 - MLA_ROPE, n_k), F32)], axis=0).T

    return carried(
        body, comm, grid=(MLA_HEADS // 2, nq),
        in_specs=[pl.BlockSpec((qt, 256), lambda hp, n: (n, hp)), pl.BlockSpec((L, 256), lambda hp, n: (0, hp)),
                  pl.BlockSpec((L, 128), lambda hp, n: (0, 0)), pl.BlockSpec((qt, 128), lambda hp, n: (n, hp)),
                  pl.BlockSpec((qt, 128), lambda hp, n: (n, hp))],
        out_specs=[pl.BlockSpec((qt, 256), lambda hp, n: (n, hp)), pl.BlockSpec((L, 256), lambda hp, n: (0, hp)),
                   pl.BlockSpec((1, L, 128), lambda hp, n: (hp, 0, 0))],
        out_shape=[jax.ShapeDtypeStruct((L, 2048), F32), jax.ShapeDtypeStruct((L, 2048), F32),
                   jax.ShapeDtypeStruct((MLA_HEADS // 2, L, 128), F32)],
        semantics=("parallel", "arbitrary"), name='c_attn_bwd')(q, kv, kr, do, o)


def layer_c_fwd(h, w, p, comm=None):
    L = h.shape[0]
    proj = mm(h, w['c_w_in'], 'nn', 'c_proj')

    def f1(c, gq, gk):
        return [rms_fwd(c[:, :768], gq), rms_fwd(c[:, 768:], gk)], []
    (cqn, ckvn), _ = rowwise(f1, [rw(proj, 1024, 1)], [p['c_q_norm'], p['c_kv_norm']], [(768, BF16), (256, BF16)], [],
                             256, 'c_norms')
    qf = mm(cqn, w['c_w_uq'], 'nn', 'c_uq')
    kvf = mm(ckvn, w['c_w_ukv'], 'nn', 'c_ukv', out_dtype=BF16)
    cos, sin = _rope_tables(L)

    def f2(q_, kr_, c, s):
        c8, s8 = jnp.tile(c, (1, 8)), jnp.tile(s, (1, 8))
        return [q_ * c8 + _rot(q_) * s8, kr_ * c[:, 128:] + _rot(kr_) * s[:, 128:]], []
    (q, kr), _ = rowwise(f2, [rw(qf), rw(proj, 128, 16), rw(cos), rw(sin)], [], [(2048, BF16), (128, BF16)], [], 256,
                         'c_rope')
    o, carried_out = mla_fwd(q, kvf, kr, comm=comm)

    def f3(o_, z):
        return [o_ * silu(z)], []
    (po,), _ = rowwise(f3, [rw(o), rw(proj, 1024, 0)], [], [(1024, BF16)], [], 256, 'c_gate')
    yb = mm(po, w['c_w_out'], 'nn', 'c_out')
    return yb, dict(carried=carried_out, h=h, proj=proj, cqn=cqn, ckvn=ckvn, q=q, kv=kvf, kr=kr, o=o, po=po, cos=cos, sin=sin)


def layer_c_bwd(dyb, w, p, sv, comm=None, sink=None):
    g = {}
    dpo = mm(dyb, w['c_w_out'], 'nt', 'c_dpo')
    _dw(g, sink, 'c_w_out', sv['po'], dyb, 'c_dwout')
    proj = sv['proj']
    L = proj.shape[0]

    def f1(dpo_, o, z):
        return [dpo_ * silu(z), dpo_ * o * silu_grad(z)], []
    (do, dz), _ = rowwise(f1, [rw(dpo), rw(sv['o']), rw(proj, 1024, 0)], [], [(1024, BF16), (1024, F32)], [], 256,
                          'c_gate_bwd')
    (dq, dkvf, dkr8), g['carried'] = mla_bwd(sv['q'], sv['kv'], sv['kr'], do, sv['o'], comm=comm)

    def f2(dq_, dkr_, c, s):
        c8, s8 = jnp.tile(c, (1, 8)), jnp.tile(s, (1, 8))
        dk = jnp.sum(dkr_, axis=0)
        return [dq_ * c8 + _rot(dq_ * s8, True), dk * c[:, 128:] + _rot(dk * s[:, 128:], True)], []
    tl = 256
    (dqf, dkr), _ = rowwise(f2, [rw(dq), (dkr8, pl.BlockSpec((8, tl, 128), lambda i: (0, i, 0))), rw(sv['cos']),
                                 rw(sv['sin'])], [], [(2048, BF16), (128, F32)], [], tl, 'c_rope_bwd')
    _dw(g, sink, 'c_w_uq', sv['cqn'], dqf, 'c_dwuq')
    _dw(g, sink, 'c_w_ukv', sv['ckvn'], dkvf, 'c_dwukv')
    dcqn = mm(dqf, w['c_w_uq'], 'nt', 'c_dcqn')
    dckvn = mm(dkvf, w['c_w_ukv'], 'nt', 'c_dckvn')

    def f3(c, dq_, dk_, dz_, dkr_, gq, gk):
        dcq, dgq = rms_bwd(c[:, :768], gq, dq_)
        dckv, dgk = rms_bwd(c[:, 768:], gk, dk_)
        return [jnp.concatenate([dz_, dcq, dckv, dkr_], axis=1)], [dgq, dgk]
    (dproj,), (dgq, dgk) = rowwise(f3, [rw(proj, 1024, 1), rw(dcqn), rw(dckvn), rw(dz), rw(dkr)],
                                   [p['c_q_norm'], p['c_kv_norm']], [(2176, BF16)], [(1, 768), (1, 256)], 256, 'c_dproj')
    g['c_q_norm'], g['c_kv_norm'] = dgq, dgk
    _dw(g, sink, 'c_w_in', sv['h'], dproj, 'c_dwin')
    dh = mm(dproj, w['c_w_in'], 'nt', 'c_dh')
    return dh, g


def _sgu_mix(wm, v, transpose):
    outs = []
    dims = (((0,), (0,)), ((), ())) if transpose else (((1,), (0,)), ((), ()))
    for gi in range(SGU_G):
        outs.append(lax.dot_general(wm[gi], v[:, gi * SGU_C:(gi + 1) * SGU_C].astype(BF16), dims,
                                    preferred_element_type=F32))
    return jnp.concatenate(outs, axis=1)


def _sgu_wmask(ws):
    t = lax.broadcasted_iota(jnp.int32, (SGU_T, SGU_T), 0)
    s = lax.broadcasted_iota(jnp.int32, (SGU_T, SGU_T), 1)
    return jnp.where((s <= t)[None], ws, 0.0).astype(BF16)


def _ln_stats(v):
    mu = jnp.mean(v, axis=-1, keepdims=True)
    vc = v - mu
    rstd = lax.rsqrt(jnp.mean(vc * vc, axis=-1, keepdims=True) + EPS)
    return vc * rstd, rstd


def layer_d_fwd(h, w, p):
    proj = mm(h, w['d_w_in'], 'nn', 'd_proj')
    bias = jnp.repeat(p['d_b_s'][0].T, SGU_C, axis=1)

    def f1(u_, v_, z, ws, lg, lb, bs):
        xh, _ = _ln_stats(gelu(v_))
        s = _sgu_mix(_sgu_wmask(ws), xh * lg + lb, False) + bs
        return [gelu(u_) * s * silu(z)], []
    (po,), _ = rowwise(f1, [rw(proj, 1024, 0), rw(proj, 1024, 1), rw(proj, 1024, 2)],
                       [p['d_w_s'][0], p['d_ln_g'], p['d_ln_b'], bias], [(1024, BF16)], [], SGU_T, 'd_mix')
    yb = mm(po, w['d_w_out'], 'nn', 'd_out')
    return yb, dict(h=h, proj=proj, po=po, bias=bias)


def layer_d_bwd(dyb, w, p, sv, sink=None):
    g = {}
    dpo = mm(dyb, w['d_w_out'], 'nt', 'd_dpo')
    _dw(g, sink, 'd_w_out', sv['po'], dyb, 'd_dwout')
    proj = sv['proj']

    def f1(dpo_, u_, v_, z, ws, lg, lb, bs):
        wm = _sgu_wmask(ws)
        gv = gelu(v_)
        xh, rstd = _ln_stats(gv)
        vn = xh * lg + lb
        s = _sgu_mix(wm, vn, False) + bs
        gu, sz = gelu(u_), silu(z)
        du = dpo_ * s * sz
        ds = dpo_ * gu * sz
        dz = dpo_ * gu * s * silu_grad(z)
        dsb = ds.astype(BF16)
        dws = jnp.stack([lax.dot_general(dsb[:, gi * SGU_C:(gi + 1) * SGU_C], vn[:, gi * SGU_C:(gi + 1) * SGU_C].astype(BF16),
                                         (((1,), (1,)), ((), ())), preferred_element_type=F32) for gi in range(SGU_G)])
        dvn = _sgu_mix(wm, ds, True)
        dlg = jnp.sum(dvn * xh, axis=0, keepdims=True)
        dlb = jnp.sum(dvn, axis=0, keepdims=True)
        dxh = dvn * lg
        dgv = rstd * (dxh - jnp.mean(dxh, axis=-1, keepdims=True) - xh * jnp.mean(dxh * xh, axis=-1, keepdims=True))
        return ([jnp.concatenate([du * gelu_grad(u_), dgv * gelu_grad(v_), dz], axis=1)], [dws, ds, dlg, dlb])
    (dproj,), (dws, dbs, dlg, dlb) = rowwise(
        f1, [rw(dpo), rw(proj, 1024, 0), rw(proj, 1024, 1), rw(proj, 1024, 2)],
        [p['d_w_s'][0], p['d_ln_g'], p['d_ln_b'], sv['bias']], [(3072, BF16)],
        [(SGU_G, SGU_T, SGU_T), (SGU_T, 1024), (1, 1024), (1, 1024)], SGU_T, 'd_mix_bwd')
    tril = np.tril(np.ones((SGU_T, SGU_T), dtype=bool))
    g['d_w_s'] = jnp.where(tril[None], dws, 0.0)[None]
    g['d_b_s'] = dbs.reshape(SGU_T, SGU_G, SGU_C).sum(-1).T[None]
    g['d_ln_g'], g['d_ln_b'] = dlg, dlb
    _dw(g, sink, 'd_w_in', sv['h'], dproj, 'd_dwin')
    dh = mm(dproj, w['d_w_in'], 'nt', 'd_dh')
    return dh, g


def _coords():
    return lax.axis_index("x"), lax.axis_index("y"), lax.axis_index("c")


class AllGather:
    def __init__(self, x):
        self.ins = [x]
        self.outs = [jax.ShapeDtypeStruct((N_DEV,) + x.shape, x.dtype)]
        self.scratch = [pltpu.SemaphoreType.DMA((7,)), pltpu.SemaphoreType.DMA((7,)), pltpu.SemaphoreType.DMA(())]

    def hooks(self, n_steps):
        return [(0, functools.partial(self.phase, 0), False), (n_steps - 1, functools.partial(self.phase, 1), True),
                (n_steps - 1, functools.partial(self.phase, 2), True)]

    @staticmethod
    def phase(which, ins, outs, scratch):
        (x_ref,), (out_ref,), (send_sems, recv_sems, local_sem) = ins, outs, scratch
        x_, y_, c_ = _coords()
        me, sibling = (x_, y_, c_), (x_, y_, 1 - c_)
        chips = [(1 - x_, y_), (x_, 1 - y_), (1 - x_, 1 - y_)]

        def slot(px, py, pc):
            return out_ref.at[4 * px + 2 * py + pc]

        def copy(k, block, to, src=None):
            return pltpu.make_async_remote_copy(src_ref=slot(*block) if src is None else src, dst_ref=slot(*block),
                                                send_sem=send_sems.at[k], recv_sem=recv_sems.at[k], device_id=to,
                                                device_id_type=MESH)

        mine = pltpu.make_async_copy(x_ref, slot(*me), local_sem)
        first = [copy(0, me, sibling, src=x_ref)]
        first += [copy(1 + j, me, (*chip, c_), src=x_ref) for j, chip in enumerate(chips)]
        passed = [copy(4 + j, (*chip, c_), sibling) for j, chip in enumerate(chips)]
        if which == 0:
            mine.start()
            for cp in first:
                cp.start()
        elif which == 1:
            for j, chip in enumerate(chips):
                copy(1 + j, (*chip, c_), me).wait_recv()
                passed[j].start()
        else:
            copy(0, sibling, me).wait_recv()
            for j, chip in enumerate(chips):
                copy(4 + j, (*chip, 1 - c_), me).wait_recv()
            for cp in first + passed:
                cp.wait_send()
            mine.wait()


class ChipExchange:
    def __init__(self, part):
        self.ins = [part]
        self.outs = [jax.ShapeDtypeStruct((3,) + part.shape[1:], part.dtype)]
        self.scratch = [pltpu.SemaphoreType.DMA((3,)), pltpu.SemaphoreType.DMA((3,))]

    def hooks(self, n_steps):
        return [(0, functools.partial(self.phase, 0), False), (n_steps - 1, functools.partial(self.phase, 1), True)]

    @staticmethod
    def phase(which, ins, outs, scratch):
        (p_ref,), (land_ref,), (send_sems, recv_sems) = ins, outs, scratch
        x_, y_, c_ = _coords()
        copies = []
        for r, (fx, fy) in enumerate([(1, 0), (0, 1), (1, 1)]):
            tx = jnp.where(fx == 1, 1 - x_, x_)
            ty = jnp.where(fy == 1, 1 - y_, y_)
            copies.append(pltpu.make_async_remote_copy(src_ref=p_ref.at[2 * tx + ty], dst_ref=land_ref.at[r],
                                                       send_sem=send_sems.at[r], recv_sem=recv_sems.at[r],
                                                       device_id=(tx, ty, c_), device_id_type=MESH))
        if which == 0:
            for cp in copies:
                cp.start()
        else:
            for cp in copies:
                cp.wait_recv()
            for cp in copies:
                cp.wait_send()


class Both:
    def __init__(self, a, b):
        self.parts = (a, b)
        self.ins, self.outs, self.scratch = a.ins + b.ins, a.outs + b.outs, a.scratch + b.scratch

    def hooks(self, n_steps):
        res, oi, oo, osc = [], 0, 0, 0
        for p in self.parts:
            sl = (slice(oi, oi + len(p.ins)), slice(oo, oo + len(p.outs)), slice(osc, osc + len(p.scratch)))
            res += [(at, functools.partial(self.sub, fn, sl), after) for at, fn, after in p.hooks(n_steps)]
            oi, oo, osc = oi + len(p.ins), oo + len(p.outs), osc + len(p.scratch)
        return res

    @staticmethod
    def sub(fn, sl, ins, outs, scratch):
        fn(ins[sl[0]], outs[sl[1]], scratch[sl[2]])


def run_comm(comm, name):
    def body(*refs):
        ci, co = len(comm.ins), len(comm.outs)
        for _, fn, _ in comm.hooks(1):
            fn(refs[:ci], refs[ci:ci + co], refs[ci + co:])

    return pl.pallas_call(body, out_shape=list(comm.outs), in_specs=[ANY] * len(comm.ins),
                          out_specs=[ANY] * len(comm.outs), scratch_shapes=list(comm.scratch), name=name)(*comm.ins)


def all_gather(x, name):
    return run_comm(AllGather(x), name)[0]


class SiblingExchange:
    def __init__(self, gfull):
        self.ins = [gfull]
        self.outs = [jax.ShapeDtypeStruct((4,) + gfull.shape[1:], gfull.dtype)]
        self.scratch = [pltpu.SemaphoreType.DMA((4,)), pltpu.SemaphoreType.DMA((4,))]

    def hooks(self, n_steps):
        return [(0, functools.partial(self.phase, 0), False), (n_steps - 1, functools.partial(self.phase, 1), True)]

    @staticmethod
    def phase(which, ins, outs, scratch):
        (g_ref,), (land_ref,), (send_sems, recv_sems) = ins, outs, scratch
        x_, y_, c_ = _coords()
        copies = [pltpu.make_async_remote_copy(src_ref=g_ref.at[2 * k + 1 - c_], dst_ref=land_ref.at[k],
                                               send_sem=send_sems.at[k], recv_sem=recv_sems.at[k],
                                               device_id=(x_, y_, 1 - c_), device_id_type=MESH) for k in range(4)]
        if which == 0:
            for cp in copies:
                cp.start()
        else:
            for cp in copies:
                cp.wait_recv()
            for cp in copies:
                cp.wait_send()


def rs_sibling(gfull, tag):
    return run_comm(SiblingExchange(gfull), 'rs_sibling_' + tag)[0]


def rs_pair_add(gfull, land, core, tag):
    _, R, C = gfull.shape
    tl = R

    def body(c_ref, g_ref, l_ref, o_ref):
        o_ref[...] = (g_ref[...].astype(F32) + l_ref[...].astype(F32)).astype(BF16)

    return pl.pallas_call(
        body, out_shape=jax.ShapeDtypeStruct((4, R, C), BF16),
        grid_spec=pltpu.PrefetchScalarGridSpec(
            num_scalar_prefetch=1, grid=(4, R // tl),
            in_specs=[pl.BlockSpec((1, tl, C), lambda k, i, c: (2 * k + c[0], i, 0)),
                      pl.BlockSpec((1, tl, C), lambda k, i, c: (k, i, 0))],
            out_specs=pl.BlockSpec((1, tl, C), lambda k, i, c: (k, i, 0))),
        compiler_params=pltpu.CompilerParams(dimension_semantics=("parallel", "parallel")), name='rs_pair_add_' + tag)(
            core, gfull, land)


def rs_chips(part, tag):
    return run_comm(ChipExchange(part), 'rs_chips_' + tag)[0]


def _adam(wv, gv, mv, vv):
    m = ADAM_B1 * mv + (1.0 - ADAM_B1) * gv
    v = ADAM_B2 * vv + (1.0 - ADAM_B2) * (gv * gv)
    m_hat = m / (1.0 - ADAM_B1 ** ADAM_STEP)
    v_hat = v / (1.0 - ADAM_B2 ** ADAM_STEP)
    delta = -ADAM_LR * (m_hat / (jnp.sqrt(v_hat) + ADAM_EPS) + ADAM_WD * wv)
    return delta, m, v


def _sum4(p_ref, l_ref):
    return ((p_ref[0].astype(F32) + l_ref[0].astype(F32)) + l_ref[1].astype(F32)) + l_ref[2].astype(F32)


def rs_rep_sum(part, land, chip):
    def body(c_ref, p_ref, l_ref, o_ref):
        o_ref[...] = _sum4(p_ref, l_ref).astype(BF16)

    return pl.pallas_call(
        body, out_shape=jax.ShapeDtypeStruct((REP_SLOT, LANES), BF16),
        grid_spec=pltpu.PrefetchScalarGridSpec(
            num_scalar_prefetch=1, grid=(1,),
            in_specs=[pl.BlockSpec((1, REP_SLOT, LANES), lambda i, c: (c[0], 0, 0)),
                      pl.BlockSpec((3, REP_SLOT, LANES), lambda i, c: (0, 0, 0))],
            out_specs=pl.BlockSpec((REP_SLOT, LANES), lambda i, c: (0, 0))),
        compiler_params=pltpu.CompilerParams(dimension_semantics=("parallel",)), name='rs_rep')(chip, part, land)


def adam_param(name, shape, off, w, m, v, chip, part=None, land=None, grep=None, fold=1):
    r, c = shape
    rp, nt, rb = _tiles((r // fold, c * fold))
    rbw = min(r, rb) if fold == 1 else r
    n_src = 2 if grep is None else 1
    ns = w.shape
    assert int(np.prod(ns[:-1])) == r and ns[-1] == c and (fold == 1 or (rb == rp and nt == 1))
    if fold > 1:
        nat_block, nat_map = ns, lambda i, cr: (0,) * len(ns)
    elif len(ns) == 2:
        nat_block, nat_map = (rbw, c), lambda i, cr: (i, 0)
    elif int(np.prod(ns[:-2])) == 1:
        nat_block, nat_map = (1,) * (len(ns) - 2) + (rbw, c), lambda i, cr: (0,) * (len(ns) - 2) + (i, 0)
    else:
        assert len(ns) == 4 and ns[0] == 1 and rbw % ns[2] == 0
        nat_block, nat_map = (1, rbw // ns[2], ns[2], c), lambda i, cr: (0, i, 0, 0)

    def body(c_ref, *refs):
        srcs = refs[:n_src * nt]
        w_ref, m_ref, v_ref, g_ref, d_ref, nm_ref, nv_ref = refs[n_src * nt:]
        if grep is None:
            tiles = [_sum4(srcs[2 * t], srcs[2 * t + 1]) for t in range(nt)]
        else:
            tiles = [srcs[t][...].astype(F32) for t in range(nt)]
        if fold > 1:
            g = jnp.concatenate([tiles[0][:, q * c:(q + 1) * c] for q in range(fold)], axis=0)
        else:
            g = (tiles[0] if nt == 1 else jnp.concatenate(tiles, axis=1))[:rbw, :c]
        g_ref[...] = g.reshape(nat_block)
        res = _adam(w_ref[...].reshape(rbw, c), g, m_ref[...].reshape(rbw, c), v_ref[...].reshape(rbw, c))
        for ref, val in zip((d_ref, nm_ref, nv_ref), res):
            ref[...] = val.reshape(nat_block)

    in_specs, args = [], []
    for t in range(nt):
        b0 = (off + t * rp) // rb
        assert (off + t * rp) % rb == 0
        if grep is None:
            in_specs += [pl.BlockSpec((1, rb, LANES), functools.partial(lambda i, cr, b0: (cr[0], b0 + i, 0), b0=b0)),
                         pl.BlockSpec((3, rb, LANES), functools.partial(lambda i, cr, b0: (0, b0 + i, 0), b0=b0))]
            args += [part, land]
        else:
            in_specs.append(pl.BlockSpec((rb, LANES), functools.partial(lambda i, cr, b0: (b0 + i, 0), b0=b0)))
            args.append(grep)
    nat = pl.BlockSpec(nat_block, nat_map)
    return pl.pallas_call(
        body, out_shape=[jax.ShapeDtypeStruct(ns, F32)] * 4,
        grid_spec=pltpu.PrefetchScalarGridSpec(num_scalar_prefetch=1, grid=(rp // rb,), in_specs=in_specs + [nat] * 3,
                                               out_specs=[nat] * 4),
        compiler_params=pltpu.CompilerParams(dimension_semantics=("parallel",)), name='adam_' + name)(
            chip, *args, w, m, v)


def adam_small(names, grep, P, M, V):
    in_specs, args, out_specs, out_shape, meta = [], [], [], [], []
    for n in names:
        s = REP_SHAPE[n]
        rp, nt, _ = _tiles(s)
        ns = P[n].shape
        for t in range(nt):
            b0 = (REP_OFF[n] + t * rp) // rp
            assert (REP_OFF[n] + t * rp) % rp == 0
            in_specs.append(pl.BlockSpec((rp, LANES), functools.partial(lambda i, b0: (b0, 0), b0=b0)))
            args.append(grep)
        nat = pl.BlockSpec(ns, functools.partial(lambda i, nd: (0,) * nd, nd=len(ns)))
        in_specs += [nat] * 3
        args += [P[n], M[n], V[n]]
        out_specs += [nat] * 4
        out_shape += [jax.ShapeDtypeStruct(ns, F32)] * 4
        meta.append((s, nt, ns))
    n_in = len(in_specs)

    def body(*refs):
        ins, outs = refs[:n_in], refs[n_in:]
        k = 0
        for p, ((r, c), nt, ns) in enumerate(meta):
            tiles = [ins[k + t][...].astype(F32) for t in range(nt)]
            w_ref, m_ref, v_ref = ins[k + nt:k + nt + 3]
            k += nt + 3
            g = (tiles[0] if nt == 1 else jnp.concatenate(tiles, axis=1))[:r, :c]
            res = (g,) + _adam(w_ref[...].reshape(r, c), g, m_ref[...].reshape(r, c), v_ref[...].reshape(r, c))
            for ref, val in zip(outs[4 * p:4 * p + 4], res):
                ref[...] = val.reshape(ns)

    res = pl.pallas_call(body, grid=(1,), in_specs=in_specs, out_specs=out_specs, out_shape=out_shape,
                         compiler_params=pltpu.CompilerParams(dimension_semantics=("arbitrary",)), name='adam_small')(*args)
    return {n: tuple(res[4 * p:4 * p + 4]) for p, n in enumerate(names)}


VM = pl.BlockSpec(memory_space=pltpu.VMEM)


def _tile_value(w, t, rp):
    r, c = w.shape
    wt = min(LANES, c - t * LANES)
    tile = w[:, t * LANES:t * LANES + wt]
    if wt < LANES:
        tile = jnp.concatenate([tile, jnp.zeros((r, LANES - wt), tile.dtype)], axis=1)
    if rp > r:
        tile = jnp.concatenate([tile, jnp.zeros((rp - r, LANES), tile.dtype)], axis=0)
    return tile


def pack_layer(layer, blocks):
    names = LAYER_PARAMS[layer]

    def body(*refs):
        tiles = []
        for ref, n in zip(refs[:-1], names):
            rp, nt, _ = _tiles(_block_shape(n))
            w = ref[...].reshape(_block_shape(n))
            tiles += [_tile_value(w, t, rp) for t in range(nt)]
        refs[-1][...] = jnp.concatenate(tiles, axis=0).astype(BF16)

    return pl.pallas_call(body, out_shape=jax.ShapeDtypeStruct((LAYER_ROWS[layer], LANES), BF16),
                          in_specs=[VM] * len(names), out_specs=VM, name='pack_' + layer)(*[blocks[n] for n in names])


def assemble(name, gathered):
    (rf, cf), ax = SHARDED[name]
    r, c = _block_shape(name)
    rp, nt, _ = _tiles((r, c))
    off = SH_OFF[name]
    out_cols = cf if ax == 0 else len(perm_index(name))

    def body(g_ref, o_ref, buf, sem):
        cp = pltpu.make_async_copy(g_ref.at[:, pl.ds(off, nt * rp), :], buf, sem)
        cp.start()
        cp.wait()
        if ax == 0:
            for j in range(N_DEV):
                o_ref[j * r:(j + 1) * r, :] = jnp.concatenate([buf[j, t * rp:(t + 1) * rp, :] for t in range(nt)], axis=1)
            return
        pieces = []
        for p in PERM[name]:
            if p[0] == 'z':
                pieces.append(jnp.zeros((r, p[1]), BF16))
                continue
            n0, w = p
            while w > 0:
                j, cb = divmod(n0, c)
                t, lane = divmod(cb, LANES)
                wl = min(w, LANES - lane, c - cb)
                pieces.append(buf[j, t * rp:t * rp + r, lane:lane + wl])
                n0, w = n0 + wl, w - wl
        o_ref[...] = jnp.concatenate(pieces, axis=1)

    return pl.pallas_call(
        body, out_shape=jax.ShapeDtypeStruct((rf, out_cols), BF16), in_specs=[ANY], out_specs=VM,
        scratch_shapes=[pltpu.VMEM((N_DEV, nt * rp, LANES), BF16), pltpu.SemaphoreType.DMA(())], name='asm_' + name)(
            gathered)


def chunk_grad(layer, name, dw, gfull):
    (rf, cf), ax = SHARDED[name]
    r, c = _block_shape(name)
    rp, nt, _ = _tiles((r, c))
    off = SH_OFF[name]
    if ax == 1:
        idx = perm_index(name) if name in PERM else np.arange(cf)
        inv = np.full(cf, -1)
        inv[idx[idx >= 0]] = np.nonzero(idx >= 0)[0]

    def body(*refs):
        dw_ref, o_ref, buf, sem = refs[0], refs[-3], refs[-2], refs[-1]
        for j in range(N_DEV):
            for t in range(nt):
                if ax == 0:
                    tile = dw_ref[j * r:(j + 1) * r, t * LANES:(t + 1) * LANES]
                else:
                    cols = inv[j * c + t * LANES:j * c + min((t + 1) * LANES, c)]
                    cuts = [0] + [k for k in range(1, len(cols)) if cols[k] != cols[k - 1] + 1] + [len(cols)]
                    pieces = [dw_ref[:, int(cols[a]):int(cols[b - 1]) + 1] for a, b in zip(cuts[:-1], cuts[1:])]
                    if len(cols) < LANES:
                        pieces.append(jnp.zeros((r, LANES - len(cols)), F32))
                    tile = pieces[0] if len(pieces) == 1 else jnp.concatenate(pieces, axis=1)
                    if rp > r:
                        tile = jnp.concatenate([tile, jnp.zeros((rp - r, LANES), F32)], axis=0)
                buf[j, t * rp:(t + 1) * rp, :] = tile.astype(BF16)
        cp = pltpu.make_async_copy(buf, o_ref.at[:, pl.ds(off, nt * rp), :], sem)
        cp.start()
        cp.wait()

    shape = jax.ShapeDtypeStruct((N_DEV, LAYER_ROWS[layer], LANES), BF16)
    scratch = [pltpu.VMEM((N_DEV, nt * rp, LANES), BF16), pltpu.SemaphoreType.DMA(())]
    if gfull is None:
        return pl.pallas_call(body, out_shape=shape, in_specs=[VM], out_specs=ANY, scratch_shapes=scratch,
                              name='chunk_' + name)(dw)
    return pl.pallas_call(body, out_shape=shape, in_specs=[VM, ANY], out_specs=ANY, scratch_shapes=scratch,
                          input_output_aliases={1: 0}, name='chunk_' + name)(dw, gfull)


class GradSink:
    def __init__(self):
        self.bufs = {}

    def put(self, name, a, b, mm_name):
        (rf, cf), ax = SHARDED[name]
        r, c = _block_shape(name)
        group = GROUP_OF[name]
        direct = ax == 0 or (c % LANES == 0 and PERM[name] == [(0, cf)])
        if direct:
            self.bufs[group] = mm_tn_chunked(a, b, mm_name, group, name, self.bufs.get(group))
        else:
            self.add(name, mm(a, b, 'tn', mm_name))

    def add(self, name, dw):
        group = GROUP_OF[name]
        self.bufs[group] = chunk_grad(group, name, dw, self.bufs.get(group))


def mm_tn_chunked(a, b, mm_name, layer, wname, gfull):
    (rf, cf), ax = SHARDED[wname]
    r, c = _block_shape(wname)
    rp, nt, _ = _tiles((r, c))
    off = SH_OFF[wname]
    K, M = a.shape
    N = b.shape[1]
    assert (M, N) == (rf, cf) and rp == r
    if ax == 0:
        tn = 4 * LANES
        grid, bspec = (N // tn,), pl.BlockSpec((K, tn), lambda g: (0, g))
        ospec = pl.BlockSpec((N_DEV, 4 * r, LANES), lambda g: (0, off // (4 * r) + g, 0))
        assert off % (4 * r) == 0 and nt % 4 == 0

        def store(res, o_ref):
            for j in range(N_DEV):
                for q in range(4):
                    o_ref[j, q * r:(q + 1) * r, :] = res[j * r:(j + 1) * r, q * LANES:(q + 1) * LANES].astype(BF16)
    else:
        tn = c
        grid, bspec = (N_DEV,), pl.BlockSpec((K, tn), lambda g: (0, g))
        ospec = pl.BlockSpec((1, nt * r, LANES), lambda g: (g, off // (nt * r), 0))
        assert off % (nt * r) == 0

        def store(res, o_ref):
            for t in range(nt):
                o_ref[0, t * r:(t + 1) * r, :] = res[:, t * LANES:(t + 1) * LANES].astype(BF16)

    def body(*refs):
        a_ref, b_ref, o_ref = refs[0], refs[1], refs[-1]
        store(lax.dot_general(a_ref[...].astype(BF16), b_ref[...].astype(BF16), _TN, preferred_element_type=F32), o_ref)

    shape = jax.ShapeDtypeStruct((N_DEV, LAYER_ROWS[layer], LANES), BF16)
    aspec = pl.BlockSpec((K, M), lambda g: (0, 0))
    params = pltpu.CompilerParams(dimension_semantics=("parallel",))
    if gfull is None:
        return pl.pallas_call(body, grid=grid, in_specs=[aspec, bspec], out_specs=ospec, out_shape=shape,
                              compiler_params=params, name=mm_name)(a, b)
    return pl.pallas_call(body, grid=grid, in_specs=[aspec, bspec, ANY], out_specs=ospec, out_shape=shape,
                          input_output_aliases={2: 0}, compiler_params=params, name=mm_name)(a, b, gfull)


def pack_rep(G):
    def body(*refs):
        tiles = []
        for ref, n in zip(refs[:-1], REP_SHAPE):
            rp, nt, _ = _tiles(_rep_packed_shape(n))
            g = ref[...]
            fold = REP_FOLD.get(n, 1)
            if fold > 1:
                rr = g.shape[0] // fold
                g = jnp.concatenate([g[q * rr:(q + 1) * rr] for q in range(fold)], axis=1)
            tiles += [_tile_value(g, t, rp) for t in range(nt)]
        rows = sum(t.shape[0] for t in tiles)
        if rows < REP_ROWS:
            tiles.append(jnp.zeros((REP_ROWS - rows, LANES), F32))
        full = jnp.concatenate(tiles, axis=0)
        for j in range(N_DEV):
            refs[-1][j] = full[j * REP_CHUNK:(j + 1) * REP_CHUNK]

    return pl.pallas_call(body, out_shape=jax.ShapeDtypeStruct((N_DEV, REP_SLOT, LANES), F32),
                          in_specs=[VM] * len(REP_SHAPE), out_specs=VM, name='pack_rep')(
                              *[G[n].reshape(s) for n, s in REP_SHAPE.items()])


def _pack_small(blocks, order, rows, width, dtype):
    flat = jnp.concatenate([blocks[n].reshape(-1).astype(dtype) for n in order])
    return jnp.pad(flat, (0, rows * width - flat.shape[0])).reshape(rows, width)


def kernel(x, pre_norm, post_norm, rel_bias, a_w_in, a_lam_re, a_lam_im, a_log_dt, a_b_re, a_b_im, a_c_re, a_c_im, a_d, a_w_glu, a_b_glu, a_w_out, b_w_in, b_sinks, b_w_out, c_w_in, c_q_norm, c_kv_norm, c_w_uq, c_w_ukv, c_w_out, d_w_in, d_ln_g, d_ln_b, d_w_s, d_b_s, d_w_out, loss_target, m_pre_norm, m_post_norm, m_rel_bias, m_a_w_in, m_a_lam_re, m_a_lam_im, m_a_log_dt, m_a_b_re, m_a_b_im, m_a_c_re, m_a_c_im, m_a_d, m_a_w_glu, m_a_b_glu, m_a_w_out, m_b_w_in, m_b_sinks, m_b_w_out, m_c_w_in, m_c_q_norm, m_c_kv_norm, m_c_w_uq, m_c_w_ukv, m_c_w_out, m_d_w_in, m_d_ln_g, m_d_ln_b, m_d_w_s, m_d_b_s, m_d_w_out, v_pre_norm, v_post_norm, v_rel_bias, v_a_w_in, v_a_lam_re, v_a_lam_im, v_a_log_dt, v_a_b_re, v_a_b_im, v_a_c_re, v_a_c_im, v_a_d, v_a_w_glu, v_a_b_glu, v_a_w_out, v_b_w_in, v_b_sinks, v_b_w_out, v_c_w_in, v_c_q_norm, v_c_kv_norm, v_c_w_uq, v_c_w_ukv, v_c_w_out, v_d_w_in, v_d_ln_g, v_d_ln_b, v_d_w_s, v_d_b_s, v_d_w_out):
    loc = locals()
    P = {n: loc[n] for n in WEIGHTS}
    M = {n: loc['m_' + n] for n in WEIGHTS}
    V = {n: loc['v_' + n] for n in WEIGHTS}
    xs = x[0]
    L = xs.shape[0]

    blocks = {n: P[n].reshape(_block_shape(n)) for n in SHARDED}
    packed = {layer: pack_layer(layer, P) for layer in LAYER_PARAMS}
    W = {}

    def assemble_layer(layer, gathered):
        for n in LAYER_PARAMS[layer]:
            if n not in SHARDED_F32:
                W[n] = assemble(n, gathered)

    small =all_gather(_pack_small(blocks, SHARDED_F32, SMALL_ROWS, 128, F32), 'ag_small')
    Pl = dict(P)
    for n in SHARDED_F32:
        c = SHARDED[n][0][1]
        bc = c // N_DEV
        Pl[n] = small.reshape(N_DEV, -1)[:, SMALL_OFF[n]:SMALL_OFF[n] + bc].reshape(1, c)
    cx, cy, cc = _coords()
    core = jnp.reshape(cc, (1,)).astype(jnp.int32)
    chip = jnp.reshape(2 * cx + cy, (1,)).astype(jnp.int32)

    def pair_sums(gfull, tag):
        return rs_pair_add(gfull, rs_sibling(gfull, tag), core, tag)

    fwd = [layer_a_fwd, layer_b_fwd, layer_c_fwd, layer_d_fwd]
    bwd = [layer_a_bwd, layer_b_bwd, layer_c_bwd, layer_d_bwd]
    saved = []
    xc = xs

    def fpre(x_, g_):
        return [rms_fwd(x_, g_)], []
    (h,), _ = rowwise(fpre, [rw(xc)], [P['pre_norm'][0:1]], [(D_MODEL, BF16)], [], 256, 'pre_norm0')
    for i in range(4):
        if i == 0:
            yb, sv = fwd[i](h, W, Pl, comm=Both(AllGather(packed['a2']), AllGather(packed['b'])),
                            on_carried=lambda got: assemble_layer('a2', got[0]),
                            prep_comm=AllGather(packed['a1']), on_prep=lambda got: assemble_layer('a1', got[0]))
            assemble_layer('b', sv['carried'][1])
        elif i < 3:
            nxt = 'abcd'[i + 1]
            yb, sv = fwd[i](h, W, Pl, comm=AllGather(packed[nxt]))
            assemble_layer(nxt, sv['carried'][0])
        else:
            yb, sv = fwd[i](h, W, Pl)

        sv['x'], sv['yb'] = xc, yb
        saved.append(sv)
        if i < 3:

            def fpost(x_, y_, gpost, gpre):
                xn_ = x_ + rms_fwd(y_, gpost)
                return [xn_, rms_fwd(xn_, gpre)], []
            (xc, h), _ = rowwise(fpost, [rw(xc), rw(yb)], [P['post_norm'][i:i + 1], P['pre_norm'][i + 1:i + 2]],
                                 [(D_MODEL, F32), (D_MODEL, BF16)], [], 256, f'post_pre_norm{i}')
        else:

            def floss(x_, y_, t_, gpost):
                d = x_ + rms_fwd(y_, gpost) - t_
                return [d * (1.0 / D_MODEL)], [0.5 * jnp.sum(jnp.sum(d * d, axis=-1, keepdims=True) * (1.0 / D_MODEL),
                                                             axis=0, keepdims=True)]
            (dx,), (loss_loc,) = rowwise(floss, [rw(xc), rw(yb), rw(loss_target[0])], [P['post_norm'][i:i + 1]],
                                         [(D_MODEL, F32)], [(1, 1)], 256, 'post_norm_loss')
    loss = lax.psum(loss_loc[0, 0], ("x", "y", "c"))

    G, out = {}, {}
    dpre, dpost = [None] * 4, [None] * 4

    def adam_layer(layer, part, land2):
        for n in LAYER_PARAMS[layer]:
            s = _block_shape(n)
            out[n] = adam_param(n, s, SH_OFF[n], P[n], M[n], V[n], chip, part=part, land=land2)

    def fpost_b(y_, d_, g_):
        dy, dg = rms_bwd(y_, g_, d_)
        return [dy], [dg]
    (dyb,), (dpost[3],) = rowwise(fpost_b, [rw(saved[3]['yb']), rw(dx)], [P['post_norm'][3:4]], [(D_MODEL, BF16)],
                                  [(1, D_MODEL)], 256, 'post_norm_bwd3')
    pending = None
    sink = GradSink()
    for i in reversed(range(4)):
        sv = saved[i]
        if pending is None:
            dh, g = bwd[i](dyb, W, Pl, sv, sink=sink)
        elif i > 0:
            dh, g = bwd[i](dyb, W, Pl, sv, comm=ChipExchange(pending[1]), sink=sink)
            adam_layer(pending[0], pending[1], g['carried'][0])
        else:
            early = {}

            def both():
                early['part'] = pair_sums(sink.bufs['a2'], 'a2')
                return Both(ChipExchange(pending[1]), ChipExchange(early['part']))
            dh, g = bwd[i](dyb, W, Pl, sv, comm=both, sink=sink)
            adam_layer(pending[0], pending[1], g['carried'][0])
            adam_layer('a2', early['part'], g['carried'][1])
        g.pop('carried', None)
        land_a1 = g.pop('land_a1', None)
        G.update(g)
        group = LAYER_GROUPS['abcd'[i]][0]
        for n in LAYER_PARAMS[group]:
            if n in g:
                sink.add(n, g[n])
        if i > 0:
            swap = SiblingExchange(sink.bufs[group])
        else:
            part_a1 = rs_pair_add(sink.bufs[group], land_a1, core, group)
            swap = ChipExchange(part_a1)

        if i > 0:

            def fpre_b(x_, dh_, d_, y_, gpre, gpost):
                dxl, dg = rms_bwd(x_, gpre, dh_)
                dy, dgp = rms_bwd(y_, gpost, d_ + dxl)
                return [d_ + dxl, dy], [dg, dgp]
            (dx, dyb), (dpre[i], dpost[i - 1]), (land,) = rowwise(
                fpre_b, [rw(sv['x']), rw(dh), rw(dx), rw(saved[i - 1]['yb'])],
                [P['pre_norm'][i:i + 1], P['post_norm'][i - 1:i]], [(D_MODEL, F32), (D_MODEL, BF16)],
                [(1, D_MODEL), (1, D_MODEL)], 256, f'pre_post_norm_bwd{i}', comm=swap)
        else:

            def fpre_b0(x_, dh_, d_, g_):
                dxl, dg = rms_bwd(x_, g_, dh_)
                return [d_ + dxl], [dg]
            (dx,), (dpre[i],), (land2_a1,) = rowwise(fpre_b0, [rw(sv['x']), rw(dh), rw(dx)], [P['pre_norm'][i:i + 1]],
                                                     [(D_MODEL, F32)], [(1, D_MODEL)], 256, 'pre_norm_bwd0', comm=swap)
            adam_layer('a1', part_a1, land2_a1)
            break
        pending = (group, rs_pair_add(sink.bufs[group], land, core, group))
    G['pre_norm'] = jnp.concatenate(dpre, axis=0)
    G['post_norm'] = jnp.concatenate(dpost, axis=0)

    part = pair_sums(pack_rep(G), 'rep')
    land2 = rs_chips(part, 'rep')
    grep = all_gather(rs_rep_sum(part, land2, chip), 'ag_rep')[:, :REP_CHUNK].reshape(REP_ROWS, LANES)
    small_names = [n for n, s in REP_SHAPE.items() if s[0] <= 64]
    out.update(adam_small(small_names, grep, P, M, V))
    for n, s in REP_SHAPE.items():
        if n not in small_names:
            out[n] = adam_param(n, s, REP_OFF[n], P[n], M[n], V[n], chip, grep=grep, fold=REP_FOLD.get(n, 1))
    res = [loss, dx[None]]
    for kind in range(4):
        res += [out[n][kind].reshape(P[n].shape) for n in WEIGHTS]
    return tuple(res)
```

```python
import functools
import math

import numpy as np
import jax
import jax.numpy as jnp
from jax import lax
from jax.experimental import pallas as pl
from jax.experimental.pallas import tpu as pltpu

F32 = jnp.float32
BF16 = jnp.bfloat16
MESH = pl.DeviceIdType.MESH
ANY = pl.BlockSpec(memory_space=pl.ANY)

N_DEV = 8
D_MODEL = 1024
EPS = 1e-6
NEG_INF = -1e30
SSM_G, SSM_P, SSM_H = 64, 64, 16
SSM_T = 256
SSM_TS = 8
SSM_WC = 512
HEAD_DIM = 64
SWA_HEADS, SWA_KV = 16, 2
WINDOW = 128
REL_BUCKETS, REL_MAX_DIST = 32, 128
MLA_HEADS, MLA_NOPE, MLA_ROPE, MLA_V = 16, 64, 32, 64
MLA_Q_RANK, MLA_KV_RANK = 768, 256
ROPE_BASE = 10000.0
SGU_G, SGU_C, SGU_T = 16, 64, 128
ADAM_LR, ADAM_B1, ADAM_B2, ADAM_EPS, ADAM_WD, ADAM_STEP = 0.001, 0.9, 0.999, 1e-08, 0.01, 10

WEIGHTS = ['pre_norm', 'post_norm', 'rel_bias', 'a_w_in', 'a_lam_re', 'a_lam_im', 'a_log_dt', 'a_b_re', 'a_b_im',
           'a_c_re', 'a_c_im', 'a_d', 'a_w_glu', 'a_b_glu', 'a_w_out', 'b_w_in', 'b_sinks', 'b_w_out', 'c_w_in',
           'c_q_norm', 'c_kv_norm', 'c_w_uq', 'c_w_ukv', 'c_w_out', 'd_w_in', 'd_ln_g', 'd_ln_b', 'd_w_s', 'd_b_s',
           'd_w_out']
SHARDED = {'a_w_in': ((1024, 2048), 1), 'a_w_glu': ((1024, 1024), 0), 'a_w_out': ((1024, 1024), 0),
           'b_w_in': ((1024, 2304), 1), 'b_w_out': ((1024, 1024), 0), 'c_w_in': ((1024, 2080), 1),
           'c_q_norm': ((1, 768), 1), 'c_kv_norm': ((1, 256), 1), 'c_w_uq': ((768, 1536), 1),
           'c_w_ukv': ((256, 2048), 1), 'c_w_out': ((1024, 1024), 0), 'd_w_in': ((1024, 3072), 1),
           'd_ln_g': ((1, 1024), 1), 'd_ln_b': ((1, 1024), 1), 'd_w_out': ((1024, 1024), 0)}
SHARDED_F32 = ['c_q_norm', 'c_kv_norm', 'd_ln_g', 'd_ln_b']
REPLICATED = [n for n in WEIGHTS if n not in SHARDED]


def _cdiv(a, b):
    return -(-a // b)


def _block_shape(name):
    (r, c), ax = SHARDED[name]
    return (r // N_DEV, c) if ax == 0 else (r, c // N_DEV)


LANES = 128
LAYER_PARAMS = {'a1': ['a_w_in'], 'a2': ['a_w_glu', 'a_w_out'], 'b': ['b_w_in', 'b_w_out'],
                'c': ['c_w_in', 'c_w_uq', 'c_w_ukv', 'c_w_out', 'c_q_norm', 'c_kv_norm'],
                'd': ['d_w_in', 'd_w_out', 'd_ln_g', 'd_ln_b']}


def _tiles(shape):
    r, c = shape
    rp = max(r, 16)
    rb = 512 if rp % 512 == 0 else 256 if rp % 256 == 0 else rp
    return rp, _cdiv(c, LANES), rb


SH_OFF, LAYER_ROWS = {}, {}
for _l, _names in LAYER_PARAMS.items():
    _o = 0
    for _n in _names:
        _rp, _nt, _rb = _tiles(_block_shape(_n))
        assert _o % _rb == 0
        SH_OFF[_n] = _o
        _o += _rp * _nt
    assert _o % 16 == 0
    LAYER_ROWS[_l] = _o
GROUP_OF = {_n: _l for _l, _names in LAYER_PARAMS.items() for _n in _names}
LAYER_GROUPS = {'a': ['a1', 'a2'], 'b': ['b'], 'c': ['c'], 'd': ['d']}

REP_SHAPE = {'d_w_s': (2048, 128), 'a_b_re': (4096, 16), 'a_b_im': (4096, 16), 'a_c_re': (1024, 64),
             'a_c_im': (1024, 64), 'pre_norm': (4, 1024), 'post_norm': (4, 1024), 'a_lam_re': (64, 64),
             'a_lam_im': (64, 64), 'a_d': (1, 1024), 'a_b_glu': (1, 1024), 'rel_bias': (32, 16), 'd_b_s': (16, 128),
             'a_log_dt': (1, 64), 'b_sinks': (1, 16)}
REP_FOLD = {'a_b_re': 8, 'a_b_im': 8, 'a_c_re': 2, 'a_c_im': 2}


def _rep_packed_shape(name):
    (r, c), f = REP_SHAPE[name], REP_FOLD.get(name, 1)
    return (r // f, c * f)


REP_OFF = {}
_o = 0
for _n in REP_SHAPE:
    _rp, _nt, _rb = _tiles(_rep_packed_shape(_n))
    assert _o % _rb == 0
    REP_OFF[_n] = _o
    _o += _rp * _nt
REP_ROWS = _cdiv(_o, 16 * N_DEV) * 16 * N_DEV
REP_CHUNK = REP_ROWS // N_DEV
REP_SLOT = REP_CHUNK

PERM = {'a_w_in': [(0, 2048)], 'd_w_in': [(0, 3072)], 'b_w_in': [(1280, 1024), (0, 1280)],
        'c_w_in': [(1056, 1024), (0, 1056), ('z', 96)],
        'c_w_uq': sum([[(2 * hp * 96, 64), ((2 * hp + 1) * 96, 64), (2 * hp * 96 + 64, 32), ((2 * hp + 1) * 96 + 64, 32),
                        ('z', 64)] for hp in range(8)], []),
        'c_w_ukv': sum([[(2 * hp * 128, 64), ((2 * hp + 1) * 128, 64), (2 * hp * 128 + 64, 64),
                         ((2 * hp + 1) * 128 + 64, 64)] for hp in range(8)], [])}


def perm_index(name):
    return np.concatenate([np.full(p[1], -1) if p[0] == 'z' else np.arange(p[0], p[0] + p[1]) for p in PERM[name]])


SMALL_OFF = {}
_o = 0
for _n in SHARDED_F32:
    SMALL_OFF[_n] = _o
    _o += int(np.prod(_block_shape(_n)))
SMALL_ROWS = _cdiv(_o, 128 * 8) * 8


def _pick(n, cands):
    for c in cands:
        if n % c == 0:
            return c
    return n


def mm(a, b, mode, name, out_dtype=F32, comm=None):
    if mode == 'nn':
        (M, K), (K2, N) = a.shape, b.shape
    elif mode == 'nt':
        (M, K), (N, K2) = a.shape, b.shape
    else:
        (K, M), (K2, N) = a.shape, b.shape
    assert K == K2, (name, a.shape, b.shape)
    tm = _pick(M, (1024, 768, 512, 256, 128))
    tn = _pick(N, (512, 384, 256))
    dims = {'nn': ((1,), (0,)), 'nt': ((1,), (1,)), 'tn': ((0,), (0,))}[mode]

    def body(a_ref, b_ref, o_ref):
        o_ref[...] = lax.dot_general(a_ref[...].astype(BF16), b_ref[...].astype(BF16), (dims, ((), ())),
                                     preferred_element_type=F32).astype(out_dtype)

    a_spec = pl.BlockSpec((K, tm), lambda i, j: (0, i)) if mode == 'tn' else pl.BlockSpec((tm, K), lambda i, j: (i, 0))
    b_spec = pl.BlockSpec((tn, K), lambda i, j: (j, 0)) if mode == 'nt' else pl.BlockSpec((K, tn), lambda i, j: (0, j))
    res = carried(body, comm, grid=(M // tm, N // tn), in_specs=[a_spec, b_spec],
                  out_specs=pl.BlockSpec((tm, tn), lambda i, j: (i, j)), out_shape=jax.ShapeDtypeStruct((M, N), out_dtype),
                  semantics=("parallel", "parallel"), name=name)(a, b)
    return res[0] if comm is None else res


def rw(arr, width=None, cb=0):
    return (arr, arr.shape[1] if width is None else width, cb)


def rowwise(fn, rows, consts, outs, accs, tl, name, n_steps=None, comm=None):
    if n_steps is None:
        n_steps = [r[0].shape[0] for r in rows if not isinstance(r[1], pl.BlockSpec)][0] // tl
    L = n_steps * tl
    nr, nc, no, na = len(rows), len(consts), len(outs), len(accs)
    in_specs, args = [], []
    for r in rows:
        if isinstance(r[1], pl.BlockSpec):
            in_specs.append(r[1])
        else:
            in_specs.append(pl.BlockSpec((tl, r[1]), functools.partial(lambda i, cb: (i, cb), cb=r[2])))
        args.append(r[0])
    for c in consts:
        in_specs.append(pl.BlockSpec(c.shape, functools.partial(lambda i, nd: (0,) * nd, nd=c.ndim)))
        args.append(c)
    out_specs = [pl.BlockSpec((tl, w), lambda i: (i, 0)) for w, _ in outs]
    out_shape = [jax.ShapeDtypeStruct((L, w), dt) for w, dt in outs]
    for s in accs:
        out_specs.append(pl.BlockSpec(s, functools.partial(lambda i, nd: (0,) * nd, nd=len(s))))
        out_shape.append(jax.ShapeDtypeStruct(s, F32))

    def body(*refs):
        ins = [r[...] for r in refs[:nr + nc]]
        o_refs = refs[nr + nc:nr + nc + no]
        a_refs = refs[nr + nc + no:]
        o_vals, a_vals = fn(*ins)
        for ref, val in zip(o_refs, o_vals):
            ref[...] = val.astype(ref.dtype)
        if na:
            @pl.when(pl.program_id(0) == 0)
            def _():
                for ref in a_refs:
                    ref[...] = jnp.zeros_like(ref)
            for ref, val in zip(a_refs, a_vals):
                ref[...] += val

    res, carried_out = carried(body, comm, grid=(n_steps,), in_specs=in_specs, out_specs=out_specs, out_shape=out_shape,
                               name=name, semantics=("arbitrary",))(*args)
    if comm is None:
        return res[:no], res[no:]
    return res[:no], res[no:], carried_out


def carried(body, comm, *, grid, in_specs, out_specs, out_shape, name, semantics, scratch_shapes=()):
    single = not isinstance(out_shape, (list, tuple))
    o_specs = [out_specs] if single else list(out_specs)
    o_shape = [out_shape] if single else list(out_shape)
    if comm is None:
        call = pl.pallas_call(body, grid=grid, in_specs=in_specs, out_specs=out_specs, out_shape=out_shape,
                              scratch_shapes=list(scratch_shapes),
                              compiler_params=pltpu.CompilerParams(dimension_semantics=semantics), name=name)
        return lambda *args: (call(*args), None)
    n_in, n_out, n_sc = len(in_specs), len(o_specs), len(scratch_shapes)
    ci, co = len(comm.ins), len(comm.outs)
    n_steps = int(np.prod(grid))
    hooks = comm.hooks(n_steps)

    def wrapped(*refs):
        ins, cins = refs[:n_in], refs[n_in:n_in + ci]
        outs, couts = refs[n_in + ci:n_in + ci + n_out], refs[n_in + ci + n_out:n_in + ci + n_out + co]
        sc, csc = refs[n_in + ci + n_out + co:n_in + ci + n_out + co + n_sc], refs[n_in + ci + n_out + co + n_sc:]
        step = pl.program_id(0)
        for ax in range(1, len(grid)):
            step = step * grid[ax] + pl.program_id(ax)
        for at, fn, after in hooks:
            if not after:
                pl.when(step == at)(functools.partial(fn, cins, couts, csc))
        body(*ins, *outs, *sc)
        for at, fn, after in hooks:
            if after:
                pl.when(step == at)(functools.partial(fn, cins, couts, csc))

    call = pl.pallas_call(wrapped, grid=grid, in_specs=list(in_specs) + [ANY] * ci, out_specs=o_specs + [ANY] * co,
                          out_shape=o_shape + list(comm.outs), scratch_shapes=list(scratch_shapes) + list(comm.scratch),
                          compiler_params=pltpu.CompilerParams(dimension_semantics=("arbitrary",) * len(grid)), name=name)

    def run(*args):
        res = call(*args, *comm.ins)
        return (res[0] if single else res[:n_out]), res[n_out:]
    return run


_K0 = math.sqrt(2.0 / math.pi)
_K1 = 0.044715


def gelu(x):
    return x * (0.5 * (1.0 + jnp.tanh(_K0 * (x + _K1 * (x * x * x)))))


def gelu_grad(x):
    t = jnp.tanh(_K0 * (x + _K1 * (x * x * x)))
    return 0.5 * (1.0 + t) + 0.5 * x * (1.0 - t * t) * (_K0 * (1.0 + 3.0 * _K1 * x * x))


def sigmoid(x):
    return 1.0 / (1.0 + jnp.exp(-x))


def silu(z):
    return z * sigmoid(z)


def silu_grad(z):
    s = sigmoid(z)
    return s * (1.0 + z * (1.0 - s))


def rms_fwd(x, g):
    r = lax.rsqrt(jnp.mean(x * x, axis=-1, keepdims=True) + EPS)
    return x * r * g


def rms_bwd(x, g, dy):
    r = lax.rsqrt(jnp.mean(x * x, axis=-1, keepdims=True) + EPS)
    xh = x * r
    dg = jnp.sum(dy * xh, axis=0, keepdims=True)
    dxh = dy * g
    dx = r * (dxh - xh * jnp.mean(dxh * xh, axis=-1, keepdims=True))
    return dx, dg


def _scan_chunk(a_r, a_i, pr_ref, pi_ref, cr, ci, T, reverse):
    ts = min(SSM_TS, T)
    sgn = -1.0 if reverse else 1.0
    row = lax.broadcasted_iota(jnp.int32, (ts, a_r.shape[1]), 0)
    pw = (lambda e: T - e) if reverse else (lambda e: e - 1)
    if reverse:
        wr_c, wi_c = pr_ref[T - ts:T, :], sgn * pi_ref[T - ts:T, :]
    else:
        wr_c, wi_c = pr_ref[0:ts, :], sgn * pi_ref[0:ts, :]
    c_r, c_i = cr[...], ci[...]
    outs = []
    subs = range(T // ts)
    for sub in (reversed(subs) if reverse else subs):
        v_r, v_i = a_r[sub * ts:(sub + 1) * ts], a_i[sub * ts:(sub + 1) * ts]
        d = 1
        while d < ts:
            wr = pr_ref[pw(d):pw(d) + 1, :]
            wi = sgn * pi_ref[pw(d):pw(d) + 1, :]
            if reverse:
                yr, yi, keep = pltpu.roll(v_r, ts - d, 0), pltpu.roll(v_i, ts - d, 0), row < ts - d
            else:
                yr, yi, keep = pltpu.roll(v_r, d, 0), pltpu.roll(v_i, d, 0), row >= d
            v_r, v_i = (v_r + jnp.where(keep, wr * yr - wi * yi, 0.0), v_i + jnp.where(keep, wr * yi + wi * yr, 0.0))
            d *= 2
        v_r, v_i = v_r + (wr_c * c_r - wi_c * c_i), v_i + (wr_c * c_i + wi_c * c_r)
        k = 0 if reverse else ts - 1
        c_r, c_i = v_r[k:k + 1, :], v_i[k:k + 1, :]
        outs.append((v_r, v_i))
    if reverse:
        outs = outs[::-1]
    cr[...] = c_r
    ci[...] = c_i
    return jnp.concatenate([o[0] for o in outs], axis=0), jnp.concatenate([o[1] for o in outs], axis=0)


_NT = (((1,), (1,)), ((), ()))
_TN = (((0,), (0,)), ((), ()))


def s5_fwd(proj, d_skip, Bre, Bim, Cre, Cim, pr, pi, comm=None):
    L = proj.shape[0]
    T, WC = min(SSM_T, L), SSM_WC
    nT = L // T

    def body(u_ref, d_ref, bre_ref, bim_ref, cre_ref, cim_ref, pr_ref, pi_ref, y_ref, yg_ref, sr_ref, si_ref, cr, ci):
        @pl.when(pl.program_id(1) == 0)
        def _():
            cr[...] = jnp.zeros_like(cr)
            ci[...] = jnp.zeros_like(ci)

        u = u_ref[...]
        ub = u.astype(BF16)
        a_r = lax.dot_general(ub, bre_ref[0].astype(BF16), _NT, preferred_element_type=F32)
        a_i = lax.dot_general(ub, bim_ref[0].astype(BF16), _NT, preferred_element_type=F32)
        a_r, a_i = _scan_chunk(a_r, a_i, pr_ref, pi_ref, cr, ci, T, False)
        sr_ref[...] = a_r
        si_ref[...] = a_i
        y = (lax.dot_general(a_r.astype(BF16), cre_ref[0].astype(BF16), _NT, preferred_element_type=F32)
             + lax.dot_general(a_i.astype(BF16), cim_ref[0].astype(BF16), _NT, preferred_element_type=F32)
             + d_ref[...] * u)
        y_ref[...] = y
        yg_ref[...] = gelu(y)

    uspec = pl.BlockSpec((T, 128), lambda k, i: (i, k))
    sspec = pl.BlockSpec((T, WC), lambda k, i: (i, k))
    return carried(
        body, comm, grid=(8, nT),
        in_specs=[uspec, pl.BlockSpec((1, 128), lambda k, i: (0, k)),
                  pl.BlockSpec((1, WC, 128), lambda k, i: (k, 0, 0)), pl.BlockSpec((1, WC, 128), lambda k, i: (k, 0, 0)),
                  pl.BlockSpec((1, 128, WC), lambda k, i: (k, 0, 0)), pl.BlockSpec((1, 128, WC), lambda k, i: (k, 0, 0)),
                  pl.BlockSpec((T, WC), lambda k, i: (0, k)), pl.BlockSpec((T, WC), lambda k, i: (0, k))],
        out_specs=[uspec, uspec, sspec, sspec],
        out_shape=[jax.ShapeDtypeStruct((L, 1024), F32)] * 2 + [jax.ShapeDtypeStruct((L, 8 * WC), F32)] * 2,
        scratch_shapes=[pltpu.VMEM((1, WC), F32), pltpu.VMEM((1, WC), F32)],
        semantics=("parallel", "arbitrary"), name='a_ssm')(proj, d_skip, Bre, Bim, Cre, Cim, pr, pi)


def s5_bwd(proj, dyg1, dyg2, y, d_skip, s_re, s_im, Bre, Bim, Cre, Cim, prr, pir, comm=None):
    L = proj.shape[0]
    T, WC = min(SSM_T, L), SSM_WC
    nT = L // T

    def body(u_ref, g1_ref, g2_ref, y_ref, d_ref, sr_ref, si_ref, spr_ref, spi_ref, bre_ref, bim_ref, cre_ref, cim_ref,
             pr_ref, pi_ref, du_ref, dd_ref, dbre_ref, dbim_ref, dcre_ref, dcim_ref, dar_ref, dai_ref, cr, ci):
        i = pl.program_id(1)

        @pl.when(i == 0)
        def _():
            for ref in (cr, ci, dd_ref, dbre_ref, dbim_ref, dcre_ref, dcim_ref, dar_ref, dai_ref):
                ref[...] = jnp.zeros_like(ref)

        u = u_ref[...]
        dy = (g1_ref[...] + g2_ref[...]) * gelu_grad(y_ref[...])
        dd_ref[...] += jnp.sum(dy * u, axis=0, keepdims=True)
        dyb, ub = dy.astype(BF16), u.astype(BF16)
        bre, bim, cre, cim = (r[0].astype(BF16) for r in (bre_ref, bim_ref, cre_ref, cim_ref))
        g_r = jnp.dot(dyb, cre, preferred_element_type=F32)
        g_i = jnp.dot(dyb, cim, preferred_element_type=F32)
        g_r, g_i = _scan_chunk(g_r, g_i, pr_ref, pi_ref, cr, ci, T, True)
        s_r, s_i = sr_ref[...], si_ref[...]
        row = lax.broadcasted_iota(jnp.int32, (T, WC), 0)
        first = (nT - 1 - i) == 0
        sp_r = jnp.where(row == 0, jnp.where(first, 0.0, spr_ref[7:8, :]), pltpu.roll(s_r, 1, 0))
        sp_i = jnp.where(row == 0, jnp.where(first, 0.0, spi_ref[7:8, :]), pltpu.roll(s_i, 1, 0))
        dar_ref[...] += jnp.sum(g_r * sp_r + g_i * sp_i, axis=0, keepdims=True)
        dai_ref[...] += jnp.sum(g_i * sp_r - g_r * sp_i, axis=0, keepdims=True)
        grb, gib = g_r.astype(BF16), g_i.astype(BF16)
        dcre_ref[0] += lax.dot_general(dyb, s_r.astype(BF16), _TN, preferred_element_type=F32)
        dcim_ref[0] += lax.dot_general(dyb, s_i.astype(BF16), _TN, preferred_element_type=F32)
        dbre_ref[0] += lax.dot_general(grb, ub, _TN, preferred_element_type=F32)
        dbim_ref[0] += lax.dot_general(gib, ub, _TN, preferred_element_type=F32)
        du_ref[...] = (dy * d_ref[...] + jnp.dot(grb, bre, preferred_element_type=F32)
                       + jnp.dot(gib, bim, preferred_element_type=F32))

    uspec = pl.BlockSpec((T, 128), lambda k, i: (nT - 1 - i, k))
    sspec = pl.BlockSpec((T, WC), lambda k, i: (nT - 1 - i, k))
    pspec = pl.BlockSpec((8, WC), lambda k, i: (jnp.maximum((nT - 1 - i) * (T // 8) - 1, 0), k))
    tab = pl.BlockSpec((T, WC), lambda k, i: (0, k))
    bspec = pl.BlockSpec((1, WC, 128), lambda k, i: (k, 0, 0))
    cspec = pl.BlockSpec((1, 128, WC), lambda k, i: (k, 0, 0))
    return carried(
        body, comm, grid=(8, nT),
        in_specs=[uspec, uspec, uspec, uspec, pl.BlockSpec((1, 128), lambda k, i: (0, k)), sspec, sspec, pspec, pspec,
                  bspec, bspec, cspec, cspec, tab, tab],
        out_specs=[uspec, pl.BlockSpec((1, 128), lambda k, i: (0, k)), bspec, bspec, cspec, cspec,
                   pl.BlockSpec((1, WC), lambda k, i: (0, k)), pl.BlockSpec((1, WC), lambda k, i: (0, k))],
        out_shape=[jax.ShapeDtypeStruct((L, 1024), F32), jax.ShapeDtypeStruct((1, 1024), F32),
                   jax.ShapeDtypeStruct((8, WC, 128), F32), jax.ShapeDtypeStruct((8, WC, 128), F32),
                   jax.ShapeDtypeStruct((8, 128, WC), F32), jax.ShapeDtypeStruct((8, 128, WC), F32),
                   jax.ShapeDtypeStruct((1, 8 * WC), F32), jax.ShapeDtypeStruct((1, 8 * WC), F32)],
        scratch_shapes=[pltpu.VMEM((1, WC), F32), pltpu.VMEM((1, WC), F32)],
        semantics=("parallel", "arbitrary"), name='a_ssm_bwd')(
            proj, dyg1, dyg2, y, d_skip, s_re, s_im, s_re, s_im, Bre, Bim, Cre, Cim, prr, pir)


def s5_discretize(lam_re, lam_im, log_dt, b_re, b_im):
    dt = jnp.exp(log_dt)[:, None]
    mag = jnp.exp(lam_re * dt)
    ab_re = mag * jnp.cos(lam_im * dt)
    ab_im = mag * jnp.sin(lam_im * dt)
    den = lam_re * lam_re + lam_im * lam_im
    nr = ab_re - 1.0
    f_re = (nr * lam_re + ab_im * lam_im) / den
    f_im = (ab_im * lam_re - nr * lam_im) / den
    bb_re = f_re[..., None] * b_re - f_im[..., None] * b_im
    bb_im = f_re[..., None] * b_im + f_im[..., None] * b_re
    return ab_re, ab_im, bb_re, bb_im


def s5_prep(bb_re, bb_im, c_re, c_im, ar, ai, T, comm=None):
    W = ar.shape[1]

    def body(bbr_ref, bbi_ref, cre_ref, cim_ref, ar_ref, ai_ref, btr_ref, bti_ref, ctr_ref, cti_ref, fr_ref, fi_ref,
             rr_ref, ri_ref):
        for ref in (btr_ref, bti_ref, ctr_ref, cti_ref):
            ref[...] = jnp.zeros_like(ref)
        for g in range(8):
            rows, cols = slice(g * SSM_P, (g + 1) * SSM_P), slice(g * SSM_H, (g + 1) * SSM_H)
            btr_ref[0, rows, cols] = bbr_ref[g]
            bti_ref[0, rows, cols] = bbi_ref[g]
            ctr_ref[0, cols, rows] = cre_ref[g]
            cti_ref[0, cols, rows] = -cim_ref[g]
        fr_ref[0:1, :] = ar_ref[...]
        fi_ref[0:1, :] = ai_ref[...]
        rr_ref[T - 1:T, :] = ar_ref[...]
        ri_ref[T - 1:T, :] = ai_ref[...]
        n = 1
        while n < T:
            cr, ci = fr_ref[0:n, :], fi_ref[0:n, :]
            lr, li = fr_ref[n - 1:n, :], fi_ref[n - 1:n, :]
            fr_ref[n:2 * n, :] = cr * lr - ci * li
            fi_ref[n:2 * n, :] = cr * li + ci * lr
            cr, ci = rr_ref[T - n:T, :], ri_ref[T - n:T, :]
            rr_ref[T - 2 * n:T - n, :] = cr * lr - ci * li
            ri_ref[T - 2 * n:T - n, :] = cr * li + ci * lr
            n *= 2

    spec = pl.BlockSpec((T, SSM_WC), lambda j: (0, j))
    aspec = pl.BlockSpec((1, SSM_WC), lambda j: (0, j))
    bspec, cspec = pl.BlockSpec((8, SSM_P, SSM_H), lambda j: (j, 0, 0)), pl.BlockSpec((8, SSM_H, SSM_P), lambda j: (j, 0, 0))
    btspec = pl.BlockSpec((1, SSM_WC, 128), lambda j: (j, 0, 0))
    ctspec = pl.BlockSpec((1, 128, SSM_WC), lambda j: (j, 0, 0))
    return carried(
        body, comm, grid=(W // SSM_WC,), in_specs=[bspec, bspec, cspec, cspec, aspec, aspec],
        out_specs=[btspec, btspec, ctspec, ctspec] + [spec] * 4,
        out_shape=[jax.ShapeDtypeStruct((8, SSM_WC, 128), F32)] * 2 + [jax.ShapeDtypeStruct((8, 128, SSM_WC), F32)] * 2
        + [jax.ShapeDtypeStruct((T, W), F32)] * 4,
        semantics=("parallel",), name='a_prep')(bb_re, bb_im, c_re, c_im, ar, ai)


def s5_untile(dbtr, dbti, dctr, dcti):
    def body(dbtr_ref, dbti_ref, dctr_ref, dcti_ref, br_ref, bi_ref, cr_ref, ci_ref):
        for g in range(8):
            rows, cols = slice(g * SSM_P, (g + 1) * SSM_P), slice(g * SSM_H, (g + 1) * SSM_H)
            br_ref[g] = dbtr_ref[0, rows, cols]
            bi_ref[g] = dbti_ref[0, rows, cols]
            cr_ref[g] = dctr_ref[0, cols, rows]
            ci_ref[g] = -dcti_ref[0, cols, rows]

    bspec, cspec = pl.BlockSpec((8, SSM_P, SSM_H), lambda j: (j, 0, 0)), pl.BlockSpec((8, SSM_H, SSM_P), lambda j: (j, 0, 0))
    btspec = pl.BlockSpec((1, SSM_WC, 128), lambda j: (j, 0, 0))
    ctspec = pl.BlockSpec((1, 128, SSM_WC), lambda j: (j, 0, 0))
    return pl.pallas_call(
        body, grid=(8,), in_specs=[btspec, btspec, ctspec, ctspec], out_specs=[bspec, bspec, cspec, cspec],
        out_shape=[jax.ShapeDtypeStruct((SSM_G, SSM_P, SSM_H), F32)] * 2 + [jax.ShapeDtypeStruct((SSM_G, SSM_H, SSM_P), F32)] * 2,
        compiler_params=pltpu.CompilerParams(dimension_semantics=("parallel",)), name='a_untile')(dbtr, dbti, dctr, dcti)


def layer_a_fwd(h, w, p, comm=None, on_carried=None, prep_comm=None, on_prep=None):
    L = h.shape[0]
    disc = lambda *a: s5_discretize(*a)
    (ab_re, ab_im, bb_re, bb_im), disc_vjp = jax.vjp(disc, p['a_lam_re'][0], p['a_lam_im'][0], p['a_log_dt'][0],
                                                     p['a_b_re'][0], p['a_b_im'][0])
    T = min(SSM_T, L)
    (Bre, Bim, Cre, Cim, pr, pi, prr, pir), prepped = s5_prep(bb_re, bb_im, p['a_c_re'][0], p['a_c_im'][0],
                                                              ab_re.reshape(1, -1), ab_im.reshape(1, -1), T,
                                                              comm=prep_comm)
    if on_prep is not None:
        on_prep(prepped)
    proj = mm(h, w['a_w_in'], 'nn', 'a_proj')
    (y, yg, s_re, s_im), carried_out = s5_fwd(proj, p['a_d'], Bre, Bim, Cre, Cim, pr, pi, comm=comm)
    if on_carried is not None:
        on_carried(carried_out)
    gl = mm(yg, w['a_w_glu'], 'nn', 'a_glu')

    def f2(yg_, gl_, z, bg):
        return [yg_ * sigmoid(gl_ + bg) * silu(z)], []
    (po,), _ = rowwise(f2, [rw(yg), rw(gl), rw(proj, 1024, 1)], [p['a_b_glu']], [(1024, BF16)], [], 256, 'a_gate')
    yb = mm(po, w['a_w_out'], 'nn', 'a_out')
    saved = dict(carried=carried_out, h=h, proj=proj, disc_vjp=disc_vjp, Bre=Bre, Bim=Bim, Cre=Cre, Cim=Cim, prr=prr, pir=pir, s_re=s_re,
                 s_im=s_im, y=y, yg=yg, gl=gl, po=po)
    return yb, saved


def _dw(g, sink, name, a, b, mm_name):
    if sink is None:
        g[name] = mm(a, b, 'tn', mm_name)
    else:
        sink.put(name, a, b, mm_name)


def layer_a_bwd(dyb, w, p, sv, comm=None, sink=None):
    g = {}
    dpo = mm(dyb, w['a_w_out'], 'nt', 'a_dpo')
    _dw(g, sink, 'a_w_out', sv['po'], dyb, 'a_dwout')
    proj = sv['proj']

    def f1(dpo_, yg, gl, z, bg):
        sg = sigmoid(gl + bg)
        sz = silu(z)
        dm = dpo_ * sz
        dz = dpo_ * (yg * sg) * silu_grad(z)
        dgl = dm * yg * sg * (1.0 - sg)
        return [dz, dm * sg, dgl], [jnp.sum(dgl, axis=0, keepdims=True)]
    (dz, dyg1, dgl), (db_glu,) = rowwise(f1, [rw(dpo), rw(sv['yg']), rw(sv['gl']), rw(proj, 1024, 1)], [p['a_b_glu']],
                                          [(1024, F32), (1024, F32), (1024, BF16)], [(1, 1024)], 256, 'a_gate_bwd')
    g['a_b_glu'] = db_glu
    _dw(g, sink, 'a_w_glu', sv['yg'], dgl, 'a_dwglu')
    dyg2 = mm(dgl, w['a_w_glu'], 'nt', 'a_dyg2')

    if callable(comm):
        comm = comm()
    (du, dd, dBre, dBim, dCre, dCim, da_re, da_im), g['carried'] = s5_bwd(
        proj, dyg1, dyg2, sv['y'], p['a_d'], sv['s_re'], sv['s_im'], sv['Bre'], sv['Bim'], sv['Cre'], sv['Cim'],
        sv['prr'], sv['pir'], comm=comm)
    g['a_d'] = dd

    def f3(du_, dz_):
        return [jnp.concatenate([du_, dz_], axis=1)], []
    (dproj,), _ = rowwise(f3, [rw(du), rw(dz)], [], [(2048, BF16)], [], 256, 'a_dproj')
    dbb_re, dbb_im, dc_re, dc_im = s5_untile(dBre, dBim, dCre, dCim)
    dlr, dli, dldt, dbr, dbi = sv['disc_vjp']((da_re.reshape(SSM_G, SSM_P), da_im.reshape(SSM_G, SSM_P), dbb_re, dbb_im))
    g['a_lam_re'], g['a_lam_im'], g['a_log_dt'] = dlr[None], dli[None], dldt[None]
    g['a_b_re'], g['a_b_im'] = dbr[None], dbi[None]
    g['a_c_re'], g['a_c_im'] = dc_re[None], dc_im[None]
    _dw(g, sink, 'a_w_in', sv['h'], dproj, 'a_dwin')
    if sink is None:
        dh = mm(dproj, w['a_w_in'], 'nt', 'a_dh')
    else:
        dh, (g['land_a1'],) = mm(dproj, w['a_w_in'], 'nt', 'a_dh', comm=SiblingExchange(sink.bufs['a1']))
    return dh, g


def _t5_bucket_np():
    qi = np.arange(WINDOW)[:, None]
    kj = np.arange(2 * WINDOW)[None, :]
    dist = np.maximum(qi + WINDOW - kj, 0)
    max_exact = REL_BUCKETS // 2
    dist_f = np.maximum(dist, 1).astype(np.float32)
    large = max_exact + (np.log(dist_f / np.float32(max_exact)) / np.float32(math.log(REL_MAX_DIST / max_exact))
                         * np.float32(REL_BUCKETS - max_exact)).astype(np.int32)
    large = np.minimum(large, REL_BUCKETS - 1)
    return np.where(dist < max_exact, dist, large).astype(np.int32)


SWA_GRP = SWA_HEADS // SWA_KV


def _swa_kv(kvp, kvc, kvh):
    kb = jnp.concatenate([kvp[:, kvh * 64:(kvh + 1) * 64], kvc[:, kvh * 64:(kvh + 1) * 64]], 0).astype(BF16)
    vb = jnp.concatenate([kvp[:, 128 + kvh * 64:128 + (kvh + 1) * 64], kvc[:, 128 + kvh * 64:128 + (kvh + 1) * 64]],
                         0).astype(BF16)
    return kb, vb


def _swa_stack(x, kvh):
    return jnp.concatenate([x[:, (kvh * SWA_GRP + g) * 64:(kvh * SWA_GRP + g + 1) * 64] for g in range(SWA_GRP)],
                           axis=0).astype(BF16)


def _swa_group(bias_ref, kvh):
    return bias_ref[kvh * SWA_GRP:(kvh + 1) * SWA_GRP].reshape(SWA_GRP * WINDOW, 2 * WINDOW)


def _swa_sinks(sink_ref, kvh):
    return jnp.concatenate([jnp.broadcast_to(sink_ref[0:1, kvh * SWA_GRP + g:kvh * SWA_GRP + g + 1], (WINDOW, 1))
                            for g in range(SWA_GRP)], axis=0)


def _swa_probs(q, kb, bias_h, sink, valid):
    s = lax.dot_general(q, kb, (((1,), (1,)), ((), ())), preferred_element_type=F32) * (HEAD_DIM ** -0.5)
    s = jnp.where(valid, s + bias_h, NEG_INF)
    m = jnp.maximum(jnp.max(s, axis=-1, keepdims=True), sink)
    e = jnp.exp(s - m)
    es = jnp.exp(sink - m)
    den = jnp.sum(e, axis=-1, keepdims=True) + es
    return e / den, es / den


def _swa_valid(n):
    qi = lax.broadcasted_iota(jnp.int32, (SWA_GRP * WINDOW, 2 * WINDOW), 0) & (WINDOW - 1)
    kj = lax.broadcasted_iota(jnp.int32, (SWA_GRP * WINDOW, 2 * WINDOW), 1)
    dist = qi + WINDOW - kj
    return (dist >= 0) & (dist < WINDOW) & ((kj >= WINDOW) | (n > 0))


def swa_fwd(proj, bias, sinks, comm=None):
    L = proj.shape[0]

    def body(z_ref, q_ref, kvc_ref, kvp_ref, bias_ref, sink_ref, o_ref, po_ref):
        n = pl.program_id(0)
        valid = _swa_valid(n)
        q, kvc, kvp = q_ref[...], kvc_ref[...], kvp_ref[...]
        outs = []
        for kvh in range(SWA_KV):
            kb, vb = _swa_kv(kvp, kvc, kvh)
            p, _ = _swa_probs(_swa_stack(q, kvh), kb, _swa_group(bias_ref, kvh), _swa_sinks(sink_ref, kvh), valid)
            o8 = jnp.dot(p.astype(BF16), vb, preferred_element_type=F32)
            outs += [o8[g * WINDOW:(g + 1) * WINDOW] for g in range(SWA_GRP)]
        o = jnp.concatenate(outs, axis=1)
        o_ref[...] = o
        po_ref[...] = (o * silu(z_ref[...])).astype(po_ref.dtype)

    return carried(
        body, comm, grid=(L // WINDOW,),
        in_specs=[pl.BlockSpec((WINDOW, 1024), lambda n: (n, 0)), pl.BlockSpec((WINDOW, 1024), lambda n: (n, 1)),
                  pl.BlockSpec((WINDOW, 256), lambda n: (n, 8)),
                  pl.BlockSpec((WINDOW, 256), lambda n: (jnp.maximum(n - 1, 0), 8)),
                  pl.BlockSpec((SWA_HEADS, WINDOW, 2 * WINDOW), lambda n: (0, 0, 0)),
                  pl.BlockSpec((1, SWA_HEADS), lambda n: (0, 0))],
        out_specs=[pl.BlockSpec((WINDOW, 1024), lambda n: (n, 0))] * 2,
        out_shape=[jax.ShapeDtypeStruct((L, 1024), F32), jax.ShapeDtypeStruct((L, 1024), BF16)],
        semantics=("parallel",), name='b_attn')(proj, proj, proj, proj, bias, sinks)


def swa_bwd(proj, do, bias, sinks, comm=None):
    L = proj.shape[0]

    def body(q_ref, kvc_ref, kvp_ref, do_ref, bias_ref, sink_ref, dq_ref, dkv_ref, dbias_ref, dsink_ref):
        n = pl.program_id(0)

        @pl.when(n == 0)
        def _():
            dkv_ref[...] = jnp.zeros_like(dkv_ref)
            dbias_ref[...] = jnp.zeros_like(dbias_ref)
            dsink_ref[...] = jnp.zeros_like(dsink_ref)

        valid = _swa_valid(n)
        q, kvc, kvp, do_ = q_ref[...], kvc_ref[...], kvp_ref[...], do_ref[...]
        dqs, dks, dvs, dsk = [], [], [], []
        for kvh in range(SWA_KV):
            kb, vb = _swa_kv(kvp, kvc, kvh)
            q8, do8 = _swa_stack(q, kvh), _swa_stack(do_, kvh)
            p, ps = _swa_probs(q8, kb, _swa_group(bias_ref, kvh), _swa_sinks(sink_ref, kvh), valid)
            dp = lax.dot_general(do8, vb, (((1,), (1,)), ((), ())), preferred_element_type=F32)
            delta = jnp.sum(p * dp, axis=-1, keepdims=True)
            ds = p * (dp - delta)
            col = -ps * delta
            dsk += [jnp.sum(col[g * WINDOW:(g + 1) * WINDOW], axis=0, keepdims=True) for g in range(SWA_GRP)]
            dbias_ref[kvh * SWA_GRP:(kvh + 1) * SWA_GRP] += ds.reshape(SWA_GRP, WINDOW, 2 * WINDOW)
            dsb = (ds * (HEAD_DIM ** -0.5)).astype(BF16)
            dq8 = jnp.dot(dsb, kb, preferred_element_type=F32)
            dqs += [dq8[g * WINDOW:(g + 1) * WINDOW] for g in range(SWA_GRP)]
            dks.append(lax.dot_general(dsb, q8, (((0,), (0,)), ((), ())), preferred_element_type=F32))
            dvs.append(lax.dot_general(p.astype(BF16), do8, (((0,), (0,)), ((), ())), preferred_element_type=F32))
        dq_ref[...] = jnp.concatenate(dqs, axis=1)
        dsink_ref[...] += jnp.concatenate(dsk, axis=1)
        both = jnp.concatenate(dks + dvs, axis=1)
        r_cur = pl.multiple_of(n * WINDOW, WINDOW)
        r_prev = pl.multiple_of(jnp.maximum(n - 1, 0) * WINDOW, WINDOW)
        dkv_ref[pl.ds(r_prev, WINDOW), :] += both[:WINDOW]
        dkv_ref[pl.ds(r_cur, WINDOW), :] += both[WINDOW:]

    return carried(
        body, comm, grid=(L // WINDOW,),
        in_specs=[pl.BlockSpec((WINDOW, 1024), lambda n: (n, 1)), pl.BlockSpec((WINDOW, 256), lambda n: (n, 8)),
                  pl.BlockSpec((WINDOW, 256), lambda n: (jnp.maximum(n - 1, 0), 8)),
                  pl.BlockSpec((WINDOW, 1024), lambda n: (n, 0)),
                  pl.BlockSpec((SWA_HEADS, WINDOW, 2 * WINDOW), lambda n: (0, 0, 0)),
                  pl.BlockSpec((1, SWA_HEADS), lambda n: (0, 0))],
        out_specs=[pl.BlockSpec((WINDOW, 1024), lambda n: (n, 0)), pl.BlockSpec((L, 256), lambda n: (0, 0)),
                   pl.BlockSpec((SWA_HEADS, WINDOW, 2 * WINDOW), lambda n: (0, 0, 0)),
                   pl.BlockSpec((1, SWA_HEADS), lambda n: (0, 0))],
        out_shape=[jax.ShapeDtypeStruct((L, 1024), F32), jax.ShapeDtypeStruct((L, 256), F32),
                   jax.ShapeDtypeStruct((SWA_HEADS, WINDOW, 2 * WINDOW), F32), jax.ShapeDtypeStruct((1, SWA_HEADS), F32)],
        semantics=("arbitrary",), name='b_attn_bwd')(proj, proj, proj, do, bias, sinks)


def swa_bias(rel_bias):
    def body(bk_ref, rb_ref, o_ref):
        bk = bk_ref[...]
        for h in range(SWA_HEADS):
            acc = jnp.zeros((WINDOW, 2 * WINDOW), F32)
            for b in range(REL_BUCKETS):
                acc = jnp.where(bk == b, rb_ref[b, h], acc)
            o_ref[h] = acc

    return pl.pallas_call(
        body, out_shape=jax.ShapeDtypeStruct((SWA_HEADS, WINDOW, 2 * WINDOW), F32),
        in_specs=[pl.BlockSpec(memory_space=pltpu.VMEM), pl.BlockSpec(memory_space=pltpu.SMEM)],
        out_specs=pl.BlockSpec(memory_space=pltpu.VMEM), name='b_bias')(jnp.asarray(_t5_bucket_np()), rel_bias)


def layer_b_fwd(h, w, p, comm=None):
    proj = mm(h, w['b_w_in'], 'nn', 'b_proj')
    bias = swa_bias(p['rel_bias'])
    (o, po), carried_out = swa_fwd(proj, bias, p['b_sinks'], comm=comm)
    yb = mm(po, w['b_w_out'], 'nn', 'b_out')
    return yb, dict(carried=carried_out, h=h, proj=proj, bias=bias, o=o, po=po)


def layer_b_bwd(dyb, w, p, sv, comm=None, sink=None):
    g = {}
    dpo = mm(dyb, w['b_w_out'], 'nt', 'b_dpo')
    _dw(g, sink, 'b_w_out', sv['po'], dyb, 'b_dwout')
    proj = sv['proj']

    def f1(dpo_, o, z):
        return [dpo_ * silu(z), dpo_ * o * silu_grad(z)], []
    (do, dz), _ = rowwise(f1, [rw(dpo), rw(sv['o']), rw(proj, 1024, 0)], [], [(1024, BF16), (1024, F32)], [], 256, 'b_gate_bwd')
    (dq, dkv, dbias, dsinks), g['carried'] = swa_bwd(proj, do, sv['bias'], p['b_sinks'], comm=comm)
    g['b_sinks'] = dsinks
    onehot = jnp.asarray(np.eye(REL_BUCKETS, dtype=np.float32)[_t5_bucket_np().reshape(-1)])

    def f2(db, oh):
        return [], [lax.dot_general(db, oh, (((1,), (0,)), ((), ())), preferred_element_type=F32,
                                    precision=lax.Precision.HIGHEST)]
    _, (drel,) = rowwise(f2, [(dbias.reshape(SWA_HEADS, -1), pl.BlockSpec((SWA_HEADS, 4096), lambda i: (0, i))),
                              (onehot, pl.BlockSpec((4096, REL_BUCKETS), lambda i: (i, 0)))], [], [],
                         [(SWA_HEADS, REL_BUCKETS)], 4096, 'b_drel', n_steps=(2 * WINDOW * WINDOW) // 4096)
    g['rel_bias'] = drel.T

    def f3(dz_, dq_, dkv_):
        return [jnp.concatenate([dz_, dq_, dkv_], axis=1)], []
    (dproj,), _ = rowwise(f3, [rw(dz), rw(dq), rw(dkv)], [], [(2304, BF16)], [], 256, 'b_dproj')
    _dw(g, sink, 'b_w_in', sv['h'], dproj, 'b_dwin')
    dh = mm(dproj, w['b_w_in'], 'nt', 'b_dh')
    return dh, g


MLA_SCALE = (MLA_NOPE + MLA_ROPE) ** -0.5
_LOG2E = math.log2(math.e)


def _rope_tables(L):
    inv = ROPE_BASE ** (-jnp.arange(0, MLA_ROPE, 2, dtype=F32) / MLA_ROPE)
    ang = jnp.arange(L, dtype=F32)[:, None] * inv[None, :]
    c, s = jnp.cos(ang), jnp.sin(ang)
    one, zero, pad = jnp.ones((L, 128), F32), jnp.zeros((L, 128), F32), jnp.zeros((L, 64), F32)
    return (jnp.concatenate([one, c, c, c, c, pad], 1), jnp.concatenate([zero, s, s, s, s, pad], 1))


def _rot(x, transpose=False):
    w = x.shape[1]
    lane = lax.broadcasted_iota(jnp.int32, x.shape, 1)
    up = pltpu.roll(x, w - 16, 1)
    dn = pltpu.roll(x, 16, 1)
    first = (lane % 32) < 16
    return jnp.where(first, up, -dn) if transpose else jnp.where(first, -up, dn)


MLA_QT = 512


def _mla_exp(qf, kf, t, qt):
    s = lax.dot_general(qf, kf, (((1,), (1,)), ((), ())), preferred_element_type=F32)
    causal = lax.broadcasted_iota(jnp.int32, (qt, qt), 1) <= lax.broadcasted_iota(jnp.int32, (qt, qt), 0)
    last = jnp.where(causal, s[:, t * qt:], NEG_INF)
    s = last if t == 0 else jnp.concatenate([s[:, :t * qt], last], axis=1)
    e = jnp.exp2((s - jnp.max(s, axis=-1, keepdims=True)) * (MLA_SCALE * _LOG2E))
    return e, jnp.sum(e, axis=-1, keepdims=True)


def _mla_heads(q, kv, kr):
    out = []
    for j in range(2):
        qf = jnp.concatenate([q[:, j * 64:(j + 1) * 64], q[:, 128 + j * 32:128 + (j + 1) * 32]], axis=1)
        kf = jnp.concatenate([kv[:, j * 64:(j + 1) * 64], kr], axis=1)
        out.append((qf, kf, kv[:, 128 + j * 64:128 + (j + 1) * 64]))
    return out


def mla_fwd(q, kv, kr, comm=None):
    L = q.shape[0]
    qt = min(MLA_QT, L)
    nq = L // qt

    def body(q_ref, kv_ref, kr_ref, o_ref):
        for t in range(nq):
            @pl.when(pl.program_id(1) == t)
            def _(t=t):
                n_k = (t + 1) * qt
                outs = []
                for qf, kf, v in _mla_heads(q_ref[...], kv_ref[0:n_k, :], kr_ref[0:n_k, 0:MLA_ROPE]):
                    e, den = _mla_exp(qf, kf, t, qt)
                    outs.append(jnp.dot(e.astype(BF16), v, preferred_element_type=F32) / den)
                o_ref[...] = jnp.concatenate(outs, axis=1)

    return carried(
        body, comm, grid=(MLA_HEADS // 2, nq),
        in_specs=[pl.BlockSpec((qt, 256), lambda hp, n: (n, hp)), pl.BlockSpec((L, 256), lambda hp, n: (0, hp)),
                  pl.BlockSpec((L, 128), lambda hp, n: (0, 0))],
        out_specs=pl.BlockSpec((qt, 128), lambda hp, n: (n, hp)), out_shape=jax.ShapeDtypeStruct((L, 1024), F32),
        semantics=("parallel", "parallel"), name='c_attn')(q, kv, kr)


def mla_bwd(q, kv, kr, do, o, comm=None):
    L = q.shape[0]
    qt = min(MLA_QT, L)
    nq = L // qt

    def body(q_ref, kv_ref, kr_ref, do_ref, o_ref, dq_ref, dkv_ref, dkr_ref):
        @pl.when(pl.program_id(1) == 0)
        def _():
            dkv_ref[...] = jnp.zeros_like(dkv_ref)
            dkr_ref[...] = jnp.zeros_like(dkr_ref)

        for t in range(nq):
            @pl.when(pl.program_id(1) == t)
            def _(t=t):
                n_k = (t + 1) * qt
                do_, o_ = do_ref[...], o_ref[...]
                dqn, dqr, dkn, dvs = [], [], [], []
                dkr = jnp.zeros((MLA_ROPE, n_k), F32)
                wide = lambda x: jnp.concatenate([x, jnp.zeros((qt, 128 - x.shape[1]), x.dtype)], axis=1)
                for j, (qf, kf, v) in enumerate(_mla_heads(q_ref[...], kv_ref[0:n_k, :], kr_ref[0:n_k, 0:MLA_ROPE])):
                    doh = do_[:, j * 64:(j + 1) * 64]
                    dof = doh.astype(F32)
                    e, den = _mla_exp(qf, kf, t, qt)
                    inv = 1.0 / den
                    dp = lax.dot_general(doh, v, (((1,), (1,)), ((), ())), preferred_element_type=F32)
                    delta = jnp.sum(dof * o_[:, j * 64:(j + 1) * 64], axis=-1, keepdims=True)
                    ds = (e * ((dp - delta) * (inv * MLA_SCALE))).astype(BF16)
                    dqf = jnp.dot(ds, kf, preferred_element_type=F32)
                    dkf = lax.dot_general(wide(qf), ds, _TN, preferred_element_type=F32)
                    dvf = lax.dot_general(wide((dof * inv).astype(BF16)), e.astype(BF16), _TN,
                                          preferred_element_type=F32)
                    dqn.append(dqf[:, :MLA_NOPE])
                    dqr.append(dqf[:, MLA_NOPE:])
                    dkn.append(dkf[:MLA_NOPE])
                    dvs.append(dvf[:MLA_V])
                    dkr = dkr + dkf[MLA_NOPE:MLA_NOPE + MLA_ROPE]
                dq_ref[...] = jnp.concatenate(dqn + dqr + [jnp.zeros((qt, 64), F32)], axis=1)
                dkv_ref[0:n_k, :] += jnp.concatenate(dkn + dvs, axis=0).T
                dkr_ref[0, 0:n_k, :] += jnp.concatenate([dkr, jnp.zeros((128 - MLA_ROPE, n_k), F32)], axis=0).T

    return carried(
        body, comm, grid=(MLA_HEADS // 2, nq),
        in_specs=[pl.BlockSpec((qt, 256), lambda hp, n: (n, hp)), pl.BlockSpec((L, 256), lambda hp, n: (0, hp)),
                  pl.BlockSpec((L, 128), lambda hp, n: (0, 0)), pl.BlockSpec((qt, 128), lambda hp, n: (n, hp)),
                  pl.BlockSpec((qt, 128), lambda hp, n: (n, hp))],
        out_specs=[pl.BlockSpec((qt, 256), lambda hp, n: (n, hp)), pl.BlockSpec((L, 256), lambda hp, n: (0, hp)),
                   pl.BlockSpec((1, L, 128), lambda hp, n: (hp, 0, 0))],
        out_shape=[jax.ShapeDtypeStruct((L, 2048), F32), jax.ShapeDtypeStruct((L, 2048), F32),
                   jax.ShapeDtypeStruct((MLA_HEADS // 2, L, 128), F32)],
        semantics=("parallel", "arbitrary"), name='c_attn_bwd')(q, kv, kr, do, o)


def layer_c_fwd(h, w, p, comm=None):
    L = h.shape[0]
    proj = mm(h, w['c_w_in'], 'nn', 'c_proj')

    def f1(c, gq, gk):
        return [rms_fwd(c[:, :768], gq), rms_fwd(c[:, 768:], gk)], []
    (cqn, ckvn), _ = rowwise(f1, [rw(proj, 1024, 1)], [p['c_q_norm'], p['c_kv_norm']], [(768, BF16), (256, BF16)], [],
                             256, 'c_norms')
    qf = mm(cqn, w['c_w_uq'], 'nn', 'c_uq')
    kvf = mm(ckvn, w['c_w_ukv'], 'nn', 'c_ukv', out_dtype=BF16)
    cos, sin = _rope_tables(L)

    def f2(q_, kr_, c, s):
        c8, s8 = jnp.tile(c, (1, 8)), jnp.tile(s, (1, 8))
        return [q_ * c8 + _rot(q_) * s8, kr_ * c[:, 128:] + _rot(kr_) * s[:, 128:]], []
    (q, kr), _ = rowwise(f2, [rw(qf), rw(proj, 128, 16), rw(cos), rw(sin)], [], [(2048, BF16), (128, BF16)], [], 256,
                         'c_rope')
    o, carried_out = mla_fwd(q, kvf, kr, comm=comm)

    def f3(o_, z):
        return [o_ * silu(z)], []
    (po,), _ = rowwise(f3, [rw(o), rw(proj, 1024, 0)], [], [(1024, BF16)], [], 256, 'c_gate')
    yb = mm(po, w['c_w_out'], 'nn', 'c_out')
    return yb, dict(carried=carried_out, h=h, proj=proj, cqn=cqn, ckvn=ckvn, q=q, kv=kvf, kr=kr, o=o, po=po, cos=cos, sin=sin)


def layer_c_bwd(dyb, w, p, sv, comm=None, sink=None):
    g = {}
    dpo = mm(dyb, w['c_w_out'], 'nt', 'c_dpo')
    _dw(g, sink, 'c_w_out', sv['po'], dyb, 'c_dwout')
    proj = sv['proj']
    L = proj.shape[0]

    def f1(dpo_, o, z):
        return [dpo_ * silu(z), dpo_ * o * silu_grad(z)], []
    (do, dz), _ = rowwise(f1, [rw(dpo), rw(sv['o']), rw(proj, 1024, 0)], [], [(1024, BF16), (1024, F32)], [], 256,
                          'c_gate_bwd')
    (dq, dkvf, dkr8), g['carried'] = mla_bwd(sv['q'], sv['kv'], sv['kr'], do, sv['o'], comm=comm)

    def f2(dq_, dkr_, c, s):
        c8, s8 = jnp.tile(c, (1, 8)), jnp.tile(s, (1, 8))
        dk = jnp.sum(dkr_, axis=0)
        return [dq_ * c8 + _rot(dq_ * s8, True), dk * c[:, 128:] + _rot(dk * s[:, 128:], True)], []
    tl = 256
    (dqf, dkr), _ = rowwise(f2, [rw(dq), (dkr8, pl.BlockSpec((8, tl, 128), lambda i: (0, i, 0))), rw(sv['cos']),
                                 rw(sv['sin'])], [], [(2048, BF16), (128, F32)], [], tl, 'c_rope_bwd')
    _dw(g, sink, 'c_w_uq', sv['cqn'], dqf, 'c_dwuq')
    _dw(g, sink, 'c_w_ukv', sv['ckvn'], dkvf, 'c_dwukv')
    dcqn = mm(dqf, w['c_w_uq'], 'nt', 'c_dcqn')
    dckvn = mm(dkvf, w['c_w_ukv'], 'nt', 'c_dckvn')

    def f3(c, dq_, dk_, dz_, dkr_, gq, gk):
        dcq, dgq = rms_bwd(c[:, :768], gq, dq_)
        dckv, dgk = rms_bwd(c[:, 768:], gk, dk_)
        return [jnp.concatenate([dz_, dcq, dckv, dkr_], axis=1)], [dgq, dgk]
    (dproj,), (dgq, dgk) = rowwise(f3, [rw(proj, 1024, 1), rw(dcqn), rw(dckvn), rw(dz), rw(dkr)],
                                   [p['c_q_norm'], p['c_kv_norm']], [(2176, BF16)], [(1, 768), (1, 256)], 256, 'c_dproj')
    g['c_q_norm'], g['c_kv_norm'] = dgq, dgk
    _dw(g, sink, 'c_w_in', sv['h'], dproj, 'c_dwin')
    dh = mm(dproj, w['c_w_in'], 'nt', 'c_dh')
    return dh, g


def _sgu_mix(wm, v, transpose):
    outs = []
    dims = (((0,), (0,)), ((), ())) if transpose else (((1,), (0,)), ((), ()))
    for gi in range(SGU_G):
        outs.append(lax.dot_general(wm[gi], v[:, gi * SGU_C:(gi + 1) * SGU_C].astype(BF16), dims,
                                    preferred_element_type=F32))
    return jnp.concatenate(outs, axis=1)


def _sgu_wmask(ws):
    t = lax.broadcasted_iota(jnp.int32, (SGU_T, SGU_T), 0)
    s = lax.broadcasted_iota(jnp.int32, (SGU_T, SGU_T), 1)
    return jnp.where((s <= t)[None], ws, 0.0).astype(BF16)


def _ln_stats(v):
    mu = jnp.mean(v, axis=-1, keepdims=True)
    vc = v - mu
    rstd = lax.rsqrt(jnp.mean(vc * vc, axis=-1, keepdims=True) + EPS)
    return vc * rstd, rstd


def layer_d_fwd(h, w, p):
    proj = mm(h, w['d_w_in'], 'nn', 'd_proj')
    bias = jnp.repeat(p['d_b_s'][0].T, SGU_C, axis=1)

    def f1(u_, v_, z, ws, lg, lb, bs):
        xh, _ = _ln_stats(gelu(v_))
        s = _sgu_mix(_sgu_wmask(ws), xh * lg + lb, False) + bs
        return [gelu(u_) * s * silu(z)], []
    (po,), _ = rowwise(f1, [rw(proj, 1024, 0), rw(proj, 1024, 1), rw(proj, 1024, 2)],
                       [p['d_w_s'][0], p['d_ln_g'], p['d_ln_b'], bias], [(1024, BF16)], [], SGU_T, 'd_mix')
    yb = mm(po, w['d_w_out'], 'nn', 'd_out')
    return yb, dict(h=h, proj=proj, po=po, bias=bias)


def layer_d_bwd(dyb, w, p, sv, sink=None):
    g = {}
    dpo = mm(dyb, w['d_w_out'], 'nt', 'd_dpo')
    _dw(g, sink, 'd_w_out', sv['po'], dyb, 'd_dwout')
    proj = sv['proj']

    def f1(dpo_, u_, v_, z, ws, lg, lb, bs):
        wm = _sgu_wmask(ws)
        gv = gelu(v_)
        xh, rstd = _ln_stats(gv)
        vn = xh * lg + lb
        s = _sgu_mix(wm, vn, False) + bs
        gu, sz = gelu(u_), silu(z)
        du = dpo_ * s * sz
        ds = dpo_ * gu * sz
        dz = dpo_ * gu * s * silu_grad(z)
        dsb = ds.astype(BF16)
        dws = jnp.stack([lax.dot_general(dsb[:, gi * SGU_C:(gi + 1) * SGU_C], vn[:, gi * SGU_C:(gi + 1) * SGU_C].astype(BF16),
                                         (((1,), (1,)), ((), ())), preferred_element_type=F32) for gi in range(SGU_G)])
        dvn = _sgu_mix(wm, ds, True)
        dlg = jnp.sum(dvn * xh, axis=0, keepdims=True)
        dlb = jnp.sum(dvn, axis=0, keepdims=True)
        dxh = dvn * lg
        dgv = rstd * (dxh - jnp.mean(dxh, axis=-1, keepdims=True) - xh * jnp.mean(dxh * xh, axis=-1, keepdims=True))
        return ([jnp.concatenate([du * gelu_grad(u_), dgv * gelu_grad(v_), dz], axis=1)], [dws, ds, dlg, dlb])
    (dproj,), (dws, dbs, dlg, dlb) = rowwise(
        f1, [rw(dpo), rw(proj, 1024, 0), rw(proj, 1024, 1), rw(proj, 1024, 2)],
        [p['d_w_s'][0], p['d_ln_g'], p['d_ln_b'], sv['bias']], [(3072, BF16)],
        [(SGU_G, SGU_T, SGU_T), (SGU_T, 1024), (1, 1024), (1, 1024)], SGU_T, 'd_mix_bwd')
    tril = np.tril(np.ones((SGU_T, SGU_T), dtype=bool))
    g['d_w_s'] = jnp.where(tril[None], dws, 0.0)[None]
    g['d_b_s'] = dbs.reshape(SGU_T, SGU_G, SGU_C).sum(-1).T[None]
    g['d_ln_g'], g['d_ln_b'] = dlg, dlb
    _dw(g, sink, 'd_w_in', sv['h'], dproj, 'd_dwin')
    dh = mm(dproj, w['d_w_in'], 'nt', 'd_dh')
    return dh, g


def _coords():
    return lax.axis_index("x"), lax.axis_index("y"), lax.axis_index("c")


class AllGather:
    def __init__(self, x):
        self.ins = [x]
        self.outs = [jax.ShapeDtypeStruct((N_DEV,) + x.shape, x.dtype)]
        self.scratch = [pltpu.SemaphoreType.DMA((7,)), pltpu.SemaphoreType.DMA((7,)), pltpu.SemaphoreType.DMA(())]

    def hooks(self, n_steps):
        return [(0, functools.partial(self.phase, 0), False), (n_steps - 1, functools.partial(self.phase, 1), True),
                (n_steps - 1, functools.partial(self.phase, 2), True)]

    @staticmethod
    def phase(which, ins, outs, scratch):
        (x_ref,), (out_ref,), (send_sems, recv_sems, local_sem) = ins, outs, scratch
        x_, y_, c_ = _coords()
        me, sibling = (x_, y_, c_), (x_, y_, 1 - c_)
        chips = [(1 - x_, y_), (x_, 1 - y_), (1 - x_, 1 - y_)]

        def slot(px, py, pc):
            return out_ref.at[4 * px + 2 * py + pc]

        def copy(k, block, to, src=None):
            return pltpu.make_async_remote_copy(src_ref=slot(*block) if src is None else src, dst_ref=slot(*block),
                                                send_sem=send_sems.at[k], recv_sem=recv_sems.at[k], device_id=to,
                                                device_id_type=MESH)

        mine = pltpu.make_async_copy(x_ref, slot(*me), local_sem)
        first = [copy(0, me, sibling, src=x_ref)]
        first += [copy(1 + j, me, (*chip, c_), src=x_ref) for j, chip in enumerate(chips)]
        passed = [copy(4 + j, (*chip, c_), sibling) for j, chip in enumerate(chips)]
        if which == 0:
            mine.start()
            for cp in first:
                cp.start()
        elif which == 1:
            for j, chip in enumerate(chips):
                copy(1 + j, (*chip, c_), me).wait_recv()
                passed[j].start()
        else:
            copy(0, sibling, me).wait_recv()
            for j, chip in enumerate(chips):
                copy(4 + j, (*chip, 1 - c_), me).wait_recv()
            for cp in first + passed:
                cp.wait_send()
            mine.wait()


class ChipExchange:
    def __init__(self, part):
        self.ins = [part]
        self.outs = [jax.ShapeDtypeStruct((3,) + part.shape[1:], part.dtype)]
        self.scratch = [pltpu.SemaphoreType.DMA((3,)), pltpu.SemaphoreType.DMA((3,))]

    def hooks(self, n_steps):
        return [(0, functools.partial(self.phase, 0), False), (n_steps - 1, functools.partial(self.phase, 1), True)]

    @staticmethod
    def phase(which, ins, outs, scratch):
        (p_ref,), (land_ref,), (send_sems, recv_sems) = ins, outs, scratch
        x_, y_, c_ = _coords()
        copies = []
        for r, (fx, fy) in enumerate([(1, 0), (0, 1), (1, 1)]):
            tx = jnp.where(fx == 1, 1 - x_, x_)
            ty = jnp.where(fy == 1, 1 - y_, y_)
            copies.append(pltpu.make_async_remote_copy(src_ref=p_ref.at[2 * tx + ty], dst_ref=land_ref.at[r],
                                                       send_sem=send_sems.at[r], recv_sem=recv_sems.at[r],
                                                       device_id=(tx, ty, c_), device_id_type=MESH))
        if which == 0:
            for cp in copies:
                cp.start()
        else:
            for cp in copies:
                cp.wait_recv()
            for cp in copies:
                cp.wait_send()


class Both:
    def __init__(self, a, b):
        self.parts = (a, b)
        self.ins, self.outs, self.scratch = a.ins + b.ins, a.outs + b.outs, a.scratch + b.scratch

    def hooks(self, n_steps):
        res, oi, oo, osc = [], 0, 0, 0
        for p in self.parts:
            sl = (slice(oi, oi + len(p.ins)), slice(oo, oo + len(p.outs)), slice(osc, osc + len(p.scratch)))
            res += [(at, functools.partial(self.sub, fn, sl), after) for at, fn, after in p.hooks(n_steps)]
            oi, oo, osc = oi + len(p.ins), oo + len(p.outs), osc + len(p.scratch)
        return res

    @staticmethod
    def sub(fn, sl, ins, outs, scratch):
        fn(ins[sl[0]], outs[sl[1]], scratch[sl[2]])


def run_comm(comm, name):
    def body(*refs):
        ci, co = len(comm.ins), len(comm.outs)
        for _, fn, _ in comm.hooks(1):
            fn(refs[:ci], refs[ci:ci + co], refs[ci + co:])

    return pl.pallas_call(body, out_shape=list(comm.outs), in_specs=[ANY] * len(comm.ins),
                          out_specs=[ANY] * len(comm.outs), scratch_shapes=list(comm.scratch), name=name)(*comm.ins)


def all_gather(x, name):
    return run_comm(AllGather(x), name)[0]


class SiblingExchange:
    def __init__(self, gfull):
        self.ins = [gfull]
        self.outs = [jax.ShapeDtypeStruct((4,) + gfull.shape[1:], gfull.dtype)]
        self.scratch = [pltpu.SemaphoreType.DMA((4,)), pltpu.SemaphoreType.DMA((4,))]

    def hooks(self, n_steps):
        return [(0, functools.partial(self.phase, 0), False), (n_steps - 1, functools.partial(self.phase, 1), True)]

    @staticmethod
    def phase(which, ins, outs, scratch):
        (g_ref,), (land_ref,), (send_sems, recv_sems) = ins, outs, scratch
        x_, y_, c_ = _coords()
        copies = [pltpu.make_async_remote_copy(src_ref=g_ref.at[2 * k + 1 - c_], dst_ref=land_ref.at[k],
                                               send_sem=send_sems.at[k], recv_sem=recv_sems.at[k],
                                               device_id=(x_, y_, 1 - c_), device_id_type=MESH) for k in range(4)]
        if which == 0:
            for cp in copies:
                cp.start()
        else:
            for cp in copies:
                cp.wait_recv()
            for cp in copies:
                cp.wait_send()


def rs_sibling(gfull, tag):
    return run_comm(SiblingExchange(gfull), 'rs_sibling_' + tag)[0]


def rs_pair_add(gfull, land, core, tag):
    _, R, C = gfull.shape
    tl = R

    def body(c_ref, g_ref, l_ref, o_ref):
        o_ref[...] = (g_ref[...].astype(F32) + l_ref[...].astype(F32)).astype(BF16)

    return pl.pallas_call(
        body, out_shape=jax.ShapeDtypeStruct((4, R, C), BF16),
        grid_spec=pltpu.PrefetchScalarGridSpec(
            num_scalar_prefetch=1, grid=(4, R // tl),
            in_specs=[pl.BlockSpec((1, tl, C), lambda k, i, c: (2 * k + c[0], i, 0)),
                      pl.BlockSpec((1, tl, C), lambda k, i, c: (k, i, 0))],
            out_specs=pl.BlockSpec((1, tl, C), lambda k, i, c: (k, i, 0))),
        compiler_params=pltpu.CompilerParams(dimension_semantics=("parallel", "parallel")), name='rs_pair_add_' + tag)(
            core, gfull, land)


def rs_chips(part, tag):
    return run_comm(ChipExchange(part), 'rs_chips_' + tag)[0]


def _adam(wv, gv, mv, vv):
    m = ADAM_B1 * mv + (1.0 - ADAM_B1) * gv
    v = ADAM_B2 * vv + (1.0 - ADAM_B2) * (gv * gv)
    m_hat = m / (1.0 - ADAM_B1 ** ADAM_STEP)
    v_hat = v / (1.0 - ADAM_B2 ** ADAM_STEP)
    delta = -ADAM_LR * (m_hat / (jnp.sqrt(v_hat) + ADAM_EPS) + ADAM_WD * wv)
    return delta, m, v


def _sum4(p_ref, l_ref):
    return ((p_ref[0].astype(F32) + l_ref[0].astype(F32)) + l_ref[1].astype(F32)) + l_ref[2].astype(F32)


def rs_rep_sum(part, land, chip):
    def body(c_ref, p_ref, l_ref, o_ref):
        o_ref[...] = _sum4(p_ref, l_ref).astype(BF16)

    return pl.pallas_call(
        body, out_shape=jax.ShapeDtypeStruct((REP_SLOT, LANES), BF16),
        grid_spec=pltpu.PrefetchScalarGridSpec(
            num_scalar_prefetch=1, grid=(1,),
            in_specs=[pl.BlockSpec((1, REP_SLOT, LANES), lambda i, c: (c[0], 0, 0)),
                      pl.BlockSpec((3, REP_SLOT, LANES), lambda i, c: (0, 0, 0))],
            out_specs=pl.BlockSpec((REP_SLOT, LANES), lambda i, c: (0, 0))),
        compiler_params=pltpu.CompilerParams(dimension_semantics=("parallel",)), name='rs_rep')(chip, part, land)


def adam_param(name, shape, off, w, m, v, chip, part=None, land=None, grep=None, fold=1):
    r, c = shape
    rp, nt, rb = _tiles((r // fold, c * fold))
    rbw = min(r, rb) if fold == 1 else r
    n_src = 2 if grep is None else 1
    ns = w.shape
    assert int(np.prod(ns[:-1])) == r and ns[-1] == c and (fold == 1 or (rb == rp and nt == 1))
    if fold > 1:
        nat_block, nat_map = ns, lambda i, cr: (0,) * len(ns)
    elif len(ns) == 2:
        nat_block, nat_map = (rbw, c), lambda i, cr: (i, 0)
    elif int(np.prod(ns[:-2])) == 1:
        nat_block, nat_map = (1,) * (len(ns) - 2) + (rbw, c), lambda i, cr: (0,) * (len(ns) - 2) + (i, 0)
    else:
        assert len(ns) == 4 and ns[0] == 1 and rbw % ns[2] == 0
        nat_block, nat_map = (1, rbw // ns[2], ns[2], c), lambda i, cr: (0, i, 0, 0)

    def body(c_ref, *refs):
        srcs = refs[:n_src * nt]
        w_ref, m_ref, v_ref, g_ref, d_ref, nm_ref, nv_ref = refs[n_src * nt:]
        if grep is None:
            tiles = [_sum4(srcs[2 * t], srcs[2 * t + 1]) for t in range(nt)]
        else:
            tiles = [srcs[t][...].astype(F32) for t in range(nt)]
        if fold > 1:
            g = jnp.concatenate([tiles[0][:, q * c:(q + 1) * c] for q in range(fold)], axis=0)
        else:
            g = (tiles[0] if nt == 1 else jnp.concatenate(tiles, axis=1))[:rbw, :c]
        g_ref[...] = g.reshape(nat_block)
        res = _adam(w_ref[...].reshape(rbw, c), g, m_ref[...].reshape(rbw, c), v_ref[...].reshape(rbw, c))
        for ref, val in zip((d_ref, nm_ref, nv_ref), res):
            ref[...] = val.reshape(nat_block)

    in_specs, args = [], []
    for t in range(nt):
        b0 = (off + t * rp) // rb
        assert (off + t * rp) % rb == 0
        if grep is None:
            in_specs += [pl.BlockSpec((1, rb, LANES), functools.partial(lambda i, cr, b0: (cr[0], b0 + i, 0), b0=b0)),
                         pl.BlockSpec((3, rb, LANES), functools.partial(lambda i, cr, b0: (0, b0 + i, 0), b0=b0))]
            args += [part, land]
        else:
            in_specs.append(pl.BlockSpec((rb, LANES), functools.partial(lambda i, cr, b0: (b0 + i, 0), b0=b0)))
            args.append(grep)
    nat = pl.BlockSpec(nat_block, nat_map)
    return pl.pallas_call(
        body, out_shape=[jax.ShapeDtypeStruct(ns, F32)] * 4,
        grid_spec=pltpu.PrefetchScalarGridSpec(num_scalar_prefetch=1, grid=(rp // rb,), in_specs=in_specs + [nat] * 3,
                                               out_specs=[nat] * 4),
        compiler_params=pltpu.CompilerParams(dimension_semantics=("parallel",)), name='adam_' + name)(
            chip, *args, w, m, v)


def adam_small(names, grep, P, M, V):
    in_specs, args, out_specs, out_shape, meta = [], [], [], [], []
    for n in names:
        s = REP_SHAPE[n]
        rp, nt, _ = _tiles(s)
        ns = P[n].shape
        for t in range(nt):
            b0 = (REP_OFF[n] + t * rp) // rp
            assert (REP_OFF[n] + t * rp) % rp == 0
            in_specs.append(pl.BlockSpec((rp, LANES), functools.partial(lambda i, b0: (b0, 0), b0=b0)))
            args.append(grep)
        nat = pl.BlockSpec(ns, functools.partial(lambda i, nd: (0,) * nd, nd=len(ns)))
        in_specs += [nat] * 3
        args += [P[n], M[n], V[n]]
        out_specs += [nat] * 4
        out_shape += [jax.ShapeDtypeStruct(ns, F32)] * 4
        meta.append((s, nt, ns))
    n_in = len(in_specs)

    def body(*refs):
        ins, outs = refs[:n_in], refs[n_in:]
        k = 0
        for p, ((r, c), nt, ns) in enumerate(meta):
            tiles = [ins[k + t][...].astype(F32) for t in range(nt)]
            w_ref, m_ref, v_ref = ins[k + nt:k + nt + 3]
            k += nt + 3
            g = (tiles[0] if nt == 1 else jnp.concatenate(tiles, axis=1))[:r, :c]
            res = (g,) + _adam(w_ref[...].reshape(r, c), g, m_ref[...].reshape(r, c), v_ref[...].reshape(r, c))
            for ref, val in zip(outs[4 * p:4 * p + 4], res):
                ref[...] = val.reshape(ns)

    res = pl.pallas_call(body, grid=(1,), in_specs=in_specs, out_specs=out_specs, out_shape=out_shape,
                         compiler_params=pltpu.CompilerParams(dimension_semantics=("arbitrary",)), name='adam_small')(*args)
    return {n: tuple(res[4 * p:4 * p + 4]) for p, n in enumerate(names)}


VM = pl.BlockSpec(memory_space=pltpu.VMEM)


def _tile_value(w, t, rp):
    r, c = w.shape
    wt = min(LANES, c - t * LANES)
    tile = w[:, t * LANES:t * LANES + wt]
    if wt < LANES:
        tile = jnp.concatenate([tile, jnp.zeros((r, LANES - wt), tile.dtype)], axis=1)
    if rp > r:
        tile = jnp.concatenate([tile, jnp.zeros((rp - r, LANES), tile.dtype)], axis=0)
    return tile


def pack_layer(layer, blocks):
    names = LAYER_PARAMS[layer]

    def body(*refs):
        tiles = []
        for ref, n in zip(refs[:-1], names):
            rp, nt, _ = _tiles(_block_shape(n))
            w = ref[...].reshape(_block_shape(n))
            tiles += [_tile_value(w, t, rp) for t in range(nt)]
        refs[-1][...] = jnp.concatenate(tiles, axis=0).astype(BF16)

    return pl.pallas_call(body, out_shape=jax.ShapeDtypeStruct((LAYER_ROWS[layer], LANES), BF16),
                          in_specs=[VM] * len(names), out_specs=VM, name='pack_' + layer)(*[blocks[n] for n in names])


def assemble(name, gathered):
    (rf, cf), ax = SHARDED[name]
    r, c = _block_shape(name)
    rp, nt, _ = _tiles((r, c))
    off = SH_OFF[name]
    out_cols = cf if ax == 0 else len(perm_index(name))

    def body(g_ref, o_ref, buf, sem):
        cp = pltpu.make_async_copy(g_ref.at[:, pl.ds(off, nt * rp), :], buf, sem)
        cp.start()
        cp.wait()
        if ax == 0:
            for j in range(N_DEV):
                o_ref[j * r:(j + 1) * r, :] = jnp.concatenate([buf[j, t * rp:(t + 1) * rp, :] for t in range(nt)], axis=1)
            return
        pieces = []
        for p in PERM[name]:
            if p[0] == 'z':
                pieces.append(jnp.zeros((r, p[1]), BF16))
                continue
            n0, w = p
            while w > 0:
                j, cb = divmod(n0, c)
                t, lane = divmod(cb, LANES)
                wl = min(w, LANES - lane, c - cb)
                pieces.append(buf[j, t * rp:t * rp + r, lane:lane + wl])
                n0, w = n0 + wl, w - wl
        o_ref[...] = jnp.concatenate(pieces, axis=1)

    return pl.pallas_call(
        body, out_shape=jax.ShapeDtypeStruct((rf, out_cols), BF16), in_specs=[ANY], out_specs=VM,
        scratch_shapes=[pltpu.VMEM((N_DEV, nt * rp, LANES), BF16), pltpu.SemaphoreType.DMA(())], name='asm_' + name)(
            gathered)


def chunk_grad(layer, name, dw, gfull):
    (rf, cf), ax = SHARDED[name]
    r, c = _block_shape(name)
    rp, nt, _ = _tiles((r, c))
    off = SH_OFF[name]
    if ax == 1:
        idx = perm_index(name) if name in PERM else np.arange(cf)
        inv = np.full(cf, -1)
        inv[idx[idx >= 0]] = np.nonzero(idx >= 0)[0]

    def body(*refs):
        dw_ref, o_ref, buf, sem = refs[0], refs[-3], refs[-2], refs[-1]
        for j in range(N_DEV):
            for t in range(nt):
                if ax == 0:
                    tile = dw_ref[j * r:(j + 1) * r, t * LANES:(t + 1) * LANES]
                else:
                    cols = inv[j * c + t * LANES:j * c + min((t + 1) * LANES, c)]
                    cuts = [0] + [k for k in range(1, len(cols)) if cols[k] != cols[k - 1] + 1] + [len(cols)]
                    pieces = [dw_ref[:, int(cols[a]):int(cols[b - 1]) + 1] for a, b in zip(cuts[:-1], cuts[1:])]
                    if len(cols) < LANES:
                        pieces.append(jnp.zeros((r, LANES - len(cols)), F32))
                    tile = pieces[0] if len(pieces) == 1 else jnp.concatenate(pieces, axis=1)
                    if rp > r:
                        tile = jnp.concatenate([tile, jnp.zeros((rp - r, LANES), F32)], axis=0)
                buf[j, t * rp:(t + 1) * rp, :] = tile.astype(BF16)
        cp = pltpu.make_async_copy(buf, o_ref.at[:, pl.ds(off, nt * rp), :], sem)
        cp.start()
        cp.wait()

    shape = jax.ShapeDtypeStruct((N_DEV, LAYER_ROWS[layer], LANES), BF16)
    scratch = [pltpu.VMEM((N_DEV, nt * rp, LANES), BF16), pltpu.SemaphoreType.DMA(())]
    if gfull is None:
        return pl.pallas_call(body, out_shape=shape, in_specs=[VM], out_specs=ANY, scratch_shapes=scratch,
                              name='chunk_' + name)(dw)
    return pl.pallas_call(body, out_shape=shape, in_specs=[VM, ANY], out_specs=ANY, scratch_shapes=scratch,
                          input_output_aliases={1: 0}, name='chunk_' + name)(dw, gfull)


class GradSink:
    def __init__(self):
        self.bufs = {}

    def put(self, name, a, b, mm_name):
        (rf, cf), ax = SHARDED[name]
        r, c = _block_shape(name)
        group = GROUP_OF[name]
        direct = ax == 0 or (c % LANES == 0 and PERM[name] == [(0, cf)])
        if direct:
            self.bufs[group] = mm_tn_chunked(a, b, mm_name, group, name, self.bufs.get(group))
        else:
            self.add(name, mm(a, b, 'tn', mm_name))

    def add(self, name, dw):
        group = GROUP_OF[name]
        self.bufs[group] = chunk_grad(group, name, dw, self.bufs.get(group))


def mm_tn_chunked(a, b, mm_name, layer, wname, gfull):
    (rf, cf), ax = SHARDED[wname]
    r, c = _block_shape(wname)
    rp, nt, _ = _tiles((r, c))
    off = SH_OFF[wname]
    K, M = a.shape
    N = b.shape[1]
    assert (M, N) == (rf, cf) and rp == r
    if ax == 0:
        tn = 4 * LANES
        grid, bspec = (N // tn,), pl.BlockSpec((K, tn), lambda g: (0, g))
        ospec = pl.BlockSpec((N_DEV, 4 * r, LANES), lambda g: (0, off // (4 * r) + g, 0))
        assert off % (4 * r) == 0 and nt % 4 == 0

        def store(res, o_ref):
            for j in range(N_DEV):
                for q in range(4):
                    o_ref[j, q * r:(q + 1) * r, :] = res[j * r:(j + 1) * r, q * LANES:(q + 1) * LANES].astype(BF16)
    else:
        tn = c
        grid, bspec = (N_DEV,), pl.BlockSpec((K, tn), lambda g: (0, g))
        ospec = pl.BlockSpec((1, nt * r, LANES), lambda g: (g, off // (nt * r), 0))
        assert off % (nt * r) == 0

        def store(res, o_ref):
            for t in range(nt):
                o_ref[0, t * r:(t + 1) * r, :] = res[:, t * LANES:(t + 1) * LANES].astype(BF16)

    def body(*refs):
        a_ref, b_ref, o_ref = refs[0], refs[1], refs[-1]
        store(lax.dot_general(a_ref[...].astype(BF16), b_ref[...].astype(BF16), _TN, preferred_element_type=F32), o_ref)

    shape = jax.ShapeDtypeStruct((N_DEV, LAYER_ROWS[layer], LANES), BF16)
    aspec = pl.BlockSpec((K, M), lambda g: (0, 0))
    params = pltpu.CompilerParams(dimension_semantics=("parallel",))
    if gfull is None:
        return pl.pallas_call(body, grid=grid, in_specs=[aspec, bspec], out_specs=ospec, out_shape=shape,
                              compiler_params=params, name=mm_name)(a, b)
    return pl.pallas_call(body, grid=grid, in_specs=[aspec, bspec, ANY], out_specs=ospec, out_shape=shape,
                          input_output_aliases={2: 0}, compiler_params=params, name=mm_name)(a, b, gfull)


def pack_rep(G):
    def body(*refs):
        tiles = []
        for ref, n in zip(refs[:-1], REP_SHAPE):
            rp, nt, _ = _tiles(_rep_packed_shape(n))
            g = ref[...]
            fold = REP_FOLD.get(n, 1)
            if fold > 1:
                rr = g.shape[0] // fold
                g = jnp.concatenate([g[q * rr:(q + 1) * rr] for q in range(fold)], axis=1)
            tiles += [_tile_value(g, t, rp) for t in range(nt)]
        rows = sum(t.shape[0] for t in tiles)
        if rows < REP_ROWS:
            tiles.append(jnp.zeros((REP_ROWS - rows, LANES), F32))
        full = jnp.concatenate(tiles, axis=0)
        for j in range(N_DEV):
            refs[-1][j] = full[j * REP_CHUNK:(j + 1) * REP_CHUNK]

    return pl.pallas_call(body, out_shape=jax.ShapeDtypeStruct((N_DEV, REP_SLOT, LANES), F32),
                          in_specs=[VM] * len(REP_SHAPE), out_specs=VM, name='pack_rep')(
                              *[G[n].reshape(s) for n, s in REP_SHAPE.items()])


def _pack_small(blocks, order, rows, width, dtype):
    flat = jnp.concatenate([blocks[n].reshape(-1).astype(dtype) for n in order])
    return jnp.pad(flat, (0, rows * width - flat.shape[0])).reshape(rows, width)


def kernel(x, pre_norm, post_norm, rel_bias, a_w_in, a_lam_re, a_lam_im, a_log_dt, a_b_re, a_b_im, a_c_re, a_c_im, a_d, a_w_glu, a_b_glu, a_w_out, b_w_in, b_sinks, b_w_out, c_w_in, c_q_norm, c_kv_norm, c_w_uq, c_w_ukv, c_w_out, d_w_in, d_ln_g, d_ln_b, d_w_s, d_b_s, d_w_out, loss_target, m_pre_norm, m_post_norm, m_rel_bias, m_a_w_in, m_a_lam_re, m_a_lam_im, m_a_log_dt, m_a_b_re, m_a_b_im, m_a_c_re, m_a_c_im, m_a_d, m_a_w_glu, m_a_b_glu, m_a_w_out, m_b_w_in, m_b_sinks, m_b_w_out, m_c_w_in, m_c_q_norm, m_c_kv_norm, m_c_w_uq, m_c_w_ukv, m_c_w_out, m_d_w_in, m_d_ln_g, m_d_ln_b, m_d_w_s, m_d_b_s, m_d_w_out, v_pre_norm, v_post_norm, v_rel_bias, v_a_w_in, v_a_lam_re, v_a_lam_im, v_a_log_dt, v_a_b_re, v_a_b_im, v_a_c_re, v_a_c_im, v_a_d, v_a_w_glu, v_a_b_glu, v_a_w_out, v_b_w_in, v_b_sinks, v_b_w_out, v_c_w_in, v_c_q_norm, v_c_kv_norm, v_c_w_uq, v_c_w_ukv, v_c_w_out, v_d_w_in, v_d_ln_g, v_d_ln_b, v_d_w_s, v_d_b_s, v_d_w_out):
    loc = locals()
    P = {n: loc[n] for n in WEIGHTS}
    M = {n: loc['m_' + n] for n in WEIGHTS}
    V = {n: loc['v_' + n] for n in WEIGHTS}
    xs = x[0]
    L = xs.shape[0]

    blocks = {n: P[n].reshape(_block_shape(n)) for n in SHARDED}
    packed = {layer: pack_layer(layer, P) for layer in LAYER_PARAMS}
    W = {}

    def assemble_layer(layer, gathered):
        for n in LAYER_PARAMS[layer]:
            if n not in SHARDED_F32:
                W[n] = assemble(n, gathered)

    Pl = dict(P)

    def arrived_first(got):
        assemble_layer('a1', got[0])
        for n in SHARDED_F32:
            c = SHARDED[n][0][1]
            bc = c // N_DEV
            Pl[n] = got[1].reshape(N_DEV, -1)[:, SMALL_OFF[n]:SMALL_OFF[n] + bc].reshape(1, c)
    cx, cy, cc = _coords()
    core = jnp.reshape(cc, (1,)).astype(jnp.int32)
    chip = jnp.reshape(2 * cx + cy, (1,)).astype(jnp.int32)

    def pair_sums(gfull, tag):
        return rs_pair_add(gfull, rs_sibling(gfull, tag), core, tag)

    fwd = [layer_a_fwd, layer_b_fwd, layer_c_fwd, layer_d_fwd]
    bwd = [layer_a_bwd, layer_b_bwd, layer_c_bwd, layer_d_bwd]
    saved = []
    xc = xs

    def fpre(x_, g_):
        return [rms_fwd(x_, g_)], []
    (h,), _ = rowwise(fpre, [rw(xc)], [P['pre_norm'][0:1]], [(D_MODEL, BF16)], [], 256, 'pre_norm0')
    for i in range(4):
        if i == 0:
            yb, sv = fwd[i](h, W, Pl, comm=Both(AllGather(packed['a2']), AllGather(packed['b'])),
                            on_carried=lambda got: assemble_layer('a2', got[0]),
                            prep_comm=Both(AllGather(packed['a1']),
                                           AllGather(_pack_small(blocks, SHARDED_F32, SMALL_ROWS, 128, F32))),
                            on_prep=arrived_first)
            assemble_layer('b', sv['carried'][1])
        elif i < 3:
            nxt = 'abcd'[i + 1]
            yb, sv = fwd[i](h, W, Pl, comm=AllGather(packed[nxt]))
            assemble_layer(nxt, sv['carried'][0])
        else:
            yb, sv = fwd[i](h, W, Pl)

        sv['x'], sv['yb'] = xc, yb
        saved.append(sv)
        if i < 3:

            def fpost(x_, y_, gpost, gpre):
                xn_ = x_ + rms_fwd(y_, gpost)
                return [xn_, rms_fwd(xn_, gpre)], []
            (xc, h), _ = rowwise(fpost, [rw(xc), rw(yb)], [P['post_norm'][i:i + 1], P['pre_norm'][i + 1:i + 2]],
                                 [(D_MODEL, F32), (D_MODEL, BF16)], [], 256, f'post_pre_norm{i}')
        else:

            def floss(x_, y_, t_, gpost):
                d = x_ + rms_fwd(y_, gpost) - t_
                return [d * (1.0 / D_MODEL)], [0.5 * jnp.sum(jnp.sum(d * d, axis=-1, keepdims=True) * (1.0 / D_MODEL),
                                                             axis=0, keepdims=True)]
            (dx,), (loss_loc,) = rowwise(floss, [rw(xc), rw(yb), rw(loss_target[0])], [P['post_norm'][i:i + 1]],
                                         [(D_MODEL, F32)], [(1, 1)], 256, 'post_norm_loss')

    G, out = {}, {}
    dpre, dpost = [None] * 4, [None] * 4

    def adam_layer(layer, part, land2):
        for n in LAYER_PARAMS[layer]:
            s = _block_shape(n)
            out[n] = adam_param(n, s, SH_OFF[n], P[n], M[n], V[n], chip, part=part, land=land2)

    def fpost_b(y_, d_, g_):
        dy, dg = rms_bwd(y_, g_, d_)
        return [dy], [dg]
    (dyb,), (dpost[3],), (loss_all,) = rowwise(
        fpost_b, [rw(saved[3]['yb']), rw(dx)], [P['post_norm'][3:4]], [(D_MODEL, BF16)], [(1, D_MODEL)], 256,
        'post_norm_bwd3', comm=AllGather(jnp.broadcast_to(loss_loc, (8, LANES))))
    loss = jnp.sum(loss_all[:, 0, 0])
    pending = None
    sink = GradSink()
    for i in reversed(range(4)):
        sv = saved[i]
        if pending is None:
            dh, g = bwd[i](dyb, W, Pl, sv, sink=sink)
        elif i > 0:
            dh, g = bwd[i](dyb, W, Pl, sv, comm=ChipExchange(pending[1]), sink=sink)
            adam_layer(pending[0], pending[1], g['carried'][0])
        else:
            early = {}

            def both():
                early['part'] = pair_sums(sink.bufs['a2'], 'a2')
                return Both(ChipExchange(pending[1]), ChipExchange(early['part']))
            dh, g = bwd[i](dyb, W, Pl, sv, comm=both, sink=sink)
            adam_layer(pending[0], pending[1], g['carried'][0])
            adam_layer('a2', early['part'], g['carried'][1])
        g.pop('carried', None)
        land_a1 = g.pop('land_a1', None)
        G.update(g)
        group = LAYER_GROUPS['abcd'[i]][0]
        for n in LAYER_PARAMS[group]:
            if n in g:
                sink.add(n, g[n])
        if i > 0:
            swap = SiblingExchange(sink.bufs[group])
        else:
            part_a1 = rs_pair_add(sink.bufs[group], land_a1, core, group)
            swap = ChipExchange(part_a1)

        if i > 0:

            def fpre_b(x_, dh_, d_, y_, gpre, gpost):
                dxl, dg = rms_bwd(x_, gpre, dh_)
                dy, dgp = rms_bwd(y_, gpost, d_ + dxl)
                return [d_ + dxl, dy], [dg, dgp]
            (dx, dyb), (dpre[i], dpost[i - 1]), (land,) = rowwise(
                fpre_b, [rw(sv['x']), rw(dh), rw(dx), rw(saved[i - 1]['yb'])],
                [P['pre_norm'][i:i + 1], P['post_norm'][i - 1:i]], [(D_MODEL, F32), (D_MODEL, BF16)],
                [(1, D_MODEL), (1, D_MODEL)], 256, f'pre_post_norm_bwd{i}', comm=swap)
        else:

            def fpre_b0(x_, dh_, d_, g_):
                dxl, dg = rms_bwd(x_, g_, dh_)
                return [d_ + dxl], [dg]
            (dx,), (dpre[i],), (land2_a1,) = rowwise(fpre_b0, [rw(sv['x']), rw(dh), rw(dx)], [P['pre_norm'][i:i + 1]],
                                                     [(D_MODEL, F32)], [(1, D_MODEL)], 256, 'pre_norm_bwd0', comm=swap)
            adam_layer('a1', part_a1, land2_a1)
            break
        pending = (group, rs_pair_add(sink.bufs[group], land, core, group))
    G['pre_norm'] = jnp.concatenate(dpre, axis=0)
    G['post_norm'] = jnp.concatenate(dpost, axis=0)

    part = pair_sums(pack_rep(G), 'rep')
    land2 = rs_chips(part, 'rep')
    grep = all_gather(rs_rep_sum(part, land2, chip), 'ag_rep')[:, :REP_CHUNK].reshape(REP_ROWS, LANES)
    small_names = [n for n, s in REP_SHAPE.items() if s[0] <= 64]
    out.update(adam_small(small_names, grep, P, M, V))
    for n, s in REP_SHAPE.items():
        if n not in small_names:
            out[n] = adam_param(n, s, REP_OFF[n], P[n], M[n], V[n], chip, grep=grep, fold=REP_FOLD.get(n, 1))
    res = [loss, dx[None]]
    for kind in range(4):
        res += [out[n][kind].reshape(P[n].shape) for n in WEIGHTS]
    return tuple(res)
```

```python
import functools
import math

import numpy as np
import jax
import jax.numpy as jnp
from jax import lax
from jax.experimental import pallas as pl
from jax.experimental.pallas import tpu as pltpu

F32 = jnp.float32
BF16 = jnp.bfloat16
MESH = pl.DeviceIdType.MESH
ANY = pl.BlockSpec(memory_space=pl.ANY)

N_DEV = 8
D_MODEL = 1024
EPS = 1e-6
NEG_INF = -1e30
SSM_G, SSM_P, SSM_H = 64, 64, 16
SSM_T = 256
SSM_TS = 8
SSM_WC = 512
HEAD_DIM = 64
SWA_HEADS, SWA_KV = 16, 2
WINDOW = 128
REL_BUCKETS, REL_MAX_DIST = 32, 128
MLA_HEADS, MLA_NOPE, MLA_ROPE, MLA_V = 16, 64, 32, 64
MLA_Q_RANK, MLA_KV_RANK = 768, 256
ROPE_BASE = 10000.0
SGU_G, SGU_C, SGU_T = 16, 64, 128
ADAM_LR, ADAM_B1, ADAM_B2, ADAM_EPS, ADAM_WD, ADAM_STEP = 0.001, 0.9, 0.999, 1e-08, 0.01, 10

WEIGHTS = ['pre_norm', 'post_norm', 'rel_bias', 'a_w_in', 'a_lam_re', 'a_lam_im', 'a_log_dt', 'a_b_re', 'a_b_im',
           'a_c_re', 'a_c_im', 'a_d', 'a_w_glu', 'a_b_glu', 'a_w_out', 'b_w_in', 'b_sinks', 'b_w_out', 'c_w_in',
           'c_q_norm', 'c_kv_norm', 'c_w_uq', 'c_w_ukv', 'c_w_out', 'd_w_in', 'd_ln_g', 'd_ln_b', 'd_w_s', 'd_b_s',
           'd_w_out']
SHARDED = {'a_w_in': ((1024, 2048), 1), 'a_w_glu': ((1024, 1024), 0), 'a_w_out': ((1024, 1024), 0),
           'b_w_in': ((1024, 2304), 1), 'b_w_out': ((1024, 1024), 0), 'c_w_in': ((1024, 2080), 1),
           'c_q_norm': ((1, 768), 1), 'c_kv_norm': ((1, 256), 1), 'c_w_uq': ((768, 1536), 1),
           'c_w_ukv': ((256, 2048), 1), 'c_w_out': ((1024, 1024), 0), 'd_w_in': ((1024, 3072), 1),
           'd_ln_g': ((1, 1024), 1), 'd_ln_b': ((1, 1024), 1), 'd_w_out': ((1024, 1024), 0)}
SHARDED_F32 = ['c_q_norm', 'c_kv_norm', 'd_ln_g', 'd_ln_b']
REPLICATED = [n for n in WEIGHTS if n not in SHARDED]


def _cdiv(a, b):
    return -(-a // b)


def _block_shape(name):
    (r, c), ax = SHARDED[name]
    return (r // N_DEV, c) if ax == 0 else (r, c // N_DEV)


LANES = 128
LAYER_PARAMS = {'a1': ['a_w_in'], 'a2': ['a_w_glu', 'a_w_out'], 'b': ['b_w_in', 'b_w_out'],
                'c': ['c_w_in', 'c_w_uq', 'c_w_ukv', 'c_w_out', 'c_q_norm', 'c_kv_norm'],
                'd': ['d_w_in', 'd_w_out', 'd_ln_g', 'd_ln_b']}


def _tiles(shape):
    r, c = shape
    rp = max(r, 16)
    rb = 512 if rp % 512 == 0 else 256 if rp % 256 == 0 else rp
    return rp, _cdiv(c, LANES), rb


SH_OFF, LAYER_ROWS = {}, {}
for _l, _names in LAYER_PARAMS.items():
    _o = 0
    for _n in _names:
        _rp, _nt, _rb = _tiles(_block_shape(_n))
        assert _o % _rb == 0
        SH_OFF[_n] = _o
        _o += _rp * _nt
    assert _o % 16 == 0
    LAYER_ROWS[_l] = _o
GROUP_OF = {_n: _l for _l, _names in LAYER_PARAMS.items() for _n in _names}
LAYER_GROUPS = {'a': ['a1', 'a2'], 'b': ['b'], 'c': ['c'], 'd': ['d']}

REP_SHAPE = {'d_w_s': (2048, 128), 'a_b_re': (4096, 16), 'a_b_im': (4096, 16), 'a_c_re': (1024, 64),
             'a_c_im': (1024, 64), 'pre_norm': (4, 1024), 'post_norm': (4, 1024), 'a_lam_re': (64, 64),
             'a_lam_im': (64, 64), 'a_d': (1, 1024), 'a_b_glu': (1, 1024), 'rel_bias': (32, 16), 'd_b_s': (16, 128),
             'a_log_dt': (1, 64), 'b_sinks': (1, 16)}
REP_FOLD = {'a_b_re': 8, 'a_b_im': 8, 'a_c_re': 2, 'a_c_im': 2}


def _rep_packed_shape(name):
    (r, c), f = REP_SHAPE[name], REP_FOLD.get(name, 1)
    return (r // f, c * f)


REP_OFF = {}
_o = 0
for _n in REP_SHAPE:
    _rp, _nt, _rb = _tiles(_rep_packed_shape(_n))
    assert _o % _rb == 0
    REP_OFF[_n] = _o
    _o += _rp * _nt
REP_ROWS = _cdiv(_o, 16 * N_DEV) * 16 * N_DEV
REP_CHUNK = REP_ROWS // N_DEV
REP_SLOT = REP_CHUNK

PERM = {'a_w_in': [(0, 2048)], 'd_w_in': [(0, 3072)], 'b_w_in': [(1280, 1024), (0, 1280)],
        'c_w_in': [(1056, 1024), (0, 1056), ('z', 96)],
        'c_w_uq': sum([[(2 * hp * 96, 64), ((2 * hp + 1) * 96, 64), (2 * hp * 96 + 64, 32), ((2 * hp + 1) * 96 + 64, 32),
                        ('z', 64)] for hp in range(8)], []),
        'c_w_ukv': sum([[(2 * hp * 128, 64), ((2 * hp + 1) * 128, 64), (2 * hp * 128 + 64, 64),
                         ((2 * hp + 1) * 128 + 64, 64)] for hp in range(8)], [])}


def perm_index(name):
    return np.concatenate([np.full(p[1], -1) if p[0] == 'z' else np.arange(p[0], p[0] + p[1]) for p in PERM[name]])


SMALL_OFF = {}
_o = 0
for _n in SHARDED_F32:
    SMALL_OFF[_n] = _o
    _o += int(np.prod(_block_shape(_n)))
SMALL_ROWS = _cdiv(_o, 128 * 8) * 8


def _pick(n, cands):
    for c in cands:
        if n % c == 0:
            return c
    return n


def mm(a, b, mode, name, out_dtype=F32, comm=None):
    if mode == 'nn':
        (M, K), (K2, N) = a.shape, b.shape
    elif mode == 'nt':
        (M, K), (N, K2) = a.shape, b.shape
    else:
        (K, M), (K2, N) = a.shape, b.shape
    assert K == K2, (name, a.shape, b.shape)
    tm = _pick(M, (1024, 768, 512, 256, 128))
    tn = _pick(N, (512, 384, 256))
    dims = {'nn': ((1,), (0,)), 'nt': ((1,), (1,)), 'tn': ((0,), (0,))}[mode]

    def body(a_ref, b_ref, o_ref):
        o_ref[...] = lax.dot_general(a_ref[...].astype(BF16), b_ref[...].astype(BF16), (dims, ((), ())),
                                     preferred_element_type=F32).astype(out_dtype)

    a_spec = pl.BlockSpec((K, tm), lambda i, j: (0, i)) if mode == 'tn' else pl.BlockSpec((tm, K), lambda i, j: (i, 0))
    b_spec = pl.BlockSpec((tn, K), lambda i, j: (j, 0)) if mode == 'nt' else pl.BlockSpec((K, tn), lambda i, j: (0, j))
    res = carried(body, comm, grid=(M // tm, N // tn), in_specs=[a_spec, b_spec],
                  out_specs=pl.BlockSpec((tm, tn), lambda i, j: (i, j)), out_shape=jax.ShapeDtypeStruct((M, N), out_dtype),
                  semantics=("parallel", "parallel"), name=name)(a, b)
    return res[0] if comm is None else res


def rw(arr, width=None, cb=0):
    return (arr, arr.shape[1] if width is None else width, cb)


def rowwise(fn, rows, consts, outs, accs, tl, name, n_steps=None, comm=None):
    if n_steps is None:
        n_steps = [r[0].shape[0] for r in rows if not isinstance(r[1], pl.BlockSpec)][0] // tl
    L = n_steps * tl
    nr, nc, no, na = len(rows), len(consts), len(outs), len(accs)
    in_specs, args = [], []
    for r in rows:
        if isinstance(r[1], pl.BlockSpec):
            in_specs.append(r[1])
        else:
            in_specs.append(pl.BlockSpec((tl, r[1]), functools.partial(lambda i, cb: (i, cb), cb=r[2])))
        args.append(r[0])
    for c in consts:
        in_specs.append(pl.BlockSpec(c.shape, functools.partial(lambda i, nd: (0,) * nd, nd=c.ndim)))
        args.append(c)
    out_specs = [pl.BlockSpec((tl, w), lambda i: (i, 0)) for w, _ in outs]
    out_shape = [jax.ShapeDtypeStruct((L, w), dt) for w, dt in outs]
    for s in accs:
        out_specs.append(pl.BlockSpec(s, functools.partial(lambda i, nd: (0,) * nd, nd=len(s))))
        out_shape.append(jax.ShapeDtypeStruct(s, F32))

    def body(*refs):
        ins = [r[...] for r in refs[:nr + nc]]
        o_refs = refs[nr + nc:nr + nc + no]
        a_refs = refs[nr + nc + no:]
        o_vals, a_vals = fn(*ins)
        for ref, val in zip(o_refs, o_vals):
            ref[...] = val.astype(ref.dtype)
        if na:
            @pl.when(pl.program_id(0) == 0)
            def _():
                for ref in a_refs:
                    ref[...] = jnp.zeros_like(ref)
            for ref, val in zip(a_refs, a_vals):
                ref[...] += val

    res, carried_out = carried(body, comm, grid=(n_steps,), in_specs=in_specs, out_specs=out_specs, out_shape=out_shape,
                               name=name, semantics=("arbitrary",))(*args)
    if comm is None:
        return res[:no], res[no:]
    return res[:no], res[no:], carried_out


def carried(body, comm, *, grid, in_specs, out_specs, out_shape, name, semantics, scratch_shapes=()):
    single = not isinstance(out_shape, (list, tuple))
    o_specs = [out_specs] if single else list(out_specs)
    o_shape = [out_shape] if single else list(out_shape)
    if comm is None:
        call = pl.pallas_call(body, grid=grid, in_specs=in_specs, out_specs=out_specs, out_shape=out_shape,
                              scratch_shapes=list(scratch_shapes),
                              compiler_params=pltpu.CompilerParams(dimension_semantics=semantics), name=name)
        return lambda *args: (call(*args), None)
    n_in, n_out, n_sc = len(in_specs), len(o_specs), len(scratch_shapes)
    ci, co = len(comm.ins), len(comm.outs)
    n_steps = int(np.prod(grid))
    hooks = comm.hooks(n_steps)

    def wrapped(*refs):
        ins, cins = refs[:n_in], refs[n_in:n_in + ci]
        outs, couts = refs[n_in + ci:n_in + ci + n_out], refs[n_in + ci + n_out:n_in + ci + n_out + co]
        sc, csc = refs[n_in + ci + n_out + co:n_in + ci + n_out + co + n_sc], refs[n_in + ci + n_out + co + n_sc:]
        step = pl.program_id(0)
        for ax in range(1, len(grid)):
            step = step * grid[ax] + pl.program_id(ax)
        for at, fn, after in hooks:
            if not after:
                pl.when(step == at)(functools.partial(fn, cins, couts, csc))
        body(*ins, *outs, *sc)
        for at, fn, after in hooks:
            if after:
                pl.when(step == at)(functools.partial(fn, cins, couts, csc))

    call = pl.pallas_call(wrapped, grid=grid, in_specs=list(in_specs) + [ANY] * ci, out_specs=o_specs + [ANY] * co,
                          out_shape=o_shape + list(comm.outs), scratch_shapes=list(scratch_shapes) + list(comm.scratch),
                          compiler_params=pltpu.CompilerParams(dimension_semantics=("arbitrary",) * len(grid)), name=name)

    def run(*args):
        res = call(*args, *comm.ins)
        return (res[0] if single else res[:n_out]), res[n_out:]
    return run


_K0 = math.sqrt(2.0 / math.pi)
_K1 = 0.044715


def gelu(x):
    return x * (0.5 * (1.0 + jnp.tanh(_K0 * (x + _K1 * (x * x * x)))))


def gelu_grad(x):
    t = jnp.tanh(_K0 * (x + _K1 * (x * x * x)))
    return 0.5 * (1.0 + t) + 0.5 * x * (1.0 - t * t) * (_K0 * (1.0 + 3.0 * _K1 * x * x))


def sigmoid(x):
    return 1.0 / (1.0 + jnp.exp(-x))


def silu(z):
    return z * sigmoid(z)


def silu_grad(z):
    s = sigmoid(z)
    return s * (1.0 + z * (1.0 - s))


def rms_fwd(x, g):
    r = lax.rsqrt(jnp.mean(x * x, axis=-1, keepdims=True) + EPS)
    return x * r * g


def rms_bwd(x, g, dy):
    r = lax.rsqrt(jnp.mean(x * x, axis=-1, keepdims=True) + EPS)
    xh = x * r
    dg = jnp.sum(dy * xh, axis=0, keepdims=True)
    dxh = dy * g
    dx = r * (dxh - xh * jnp.mean(dxh * xh, axis=-1, keepdims=True))
    return dx, dg


def _scan_chunk(a_r, a_i, pr_ref, pi_ref, cr, ci, T, reverse):
    ts = min(SSM_TS, T)
    sgn = -1.0 if reverse else 1.0
    row = lax.broadcasted_iota(jnp.int32, (ts, a_r.shape[1]), 0)
    pw = (lambda e: T - e) if reverse else (lambda e: e - 1)
    if reverse:
        wr_c, wi_c = pr_ref[T - ts:T, :], sgn * pi_ref[T - ts:T, :]
    else:
        wr_c, wi_c = pr_ref[0:ts, :], sgn * pi_ref[0:ts, :]
    c_r, c_i = cr[...], ci[...]
    outs = []
    subs = range(T // ts)
    for sub in (reversed(subs) if reverse else subs):
        v_r, v_i = a_r[sub * ts:(sub + 1) * ts], a_i[sub * ts:(sub + 1) * ts]
        d = 1
        while d < ts:
            wr = pr_ref[pw(d):pw(d) + 1, :]
            wi = sgn * pi_ref[pw(d):pw(d) + 1, :]
            if reverse:
                yr, yi, keep = pltpu.roll(v_r, ts - d, 0), pltpu.roll(v_i, ts - d, 0), row < ts - d
            else:
                yr, yi, keep = pltpu.roll(v_r, d, 0), pltpu.roll(v_i, d, 0), row >= d
            v_r, v_i = (v_r + jnp.where(keep, wr * yr - wi * yi, 0.0), v_i + jnp.where(keep, wr * yi + wi * yr, 0.0))
            d *= 2
        v_r, v_i = v_r + (wr_c * c_r - wi_c * c_i), v_i + (wr_c * c_i + wi_c * c_r)
        k = 0 if reverse else ts - 1
        c_r, c_i = v_r[k:k + 1, :], v_i[k:k + 1, :]
        outs.append((v_r, v_i))
    if reverse:
        outs = outs[::-1]
    cr[...] = c_r
    ci[...] = c_i
    return jnp.concatenate([o[0] for o in outs], axis=0), jnp.concatenate([o[1] for o in outs], axis=0)


_NT = (((1,), (1,)), ((), ()))
_TN = (((0,), (0,)), ((), ()))


def s5_fwd(proj, d_skip, Bre, Bim, Cre, Cim, pr, pi, comm=None):
    L = proj.shape[0]
    T, WC = min(SSM_T, L), SSM_WC
    nT = L // T

    def body(u_ref, d_ref, bre_ref, bim_ref, cre_ref, cim_ref, pr_ref, pi_ref, y_ref, yg_ref, sr_ref, si_ref, cr, ci):
        @pl.when(pl.program_id(1) == 0)
        def _():
            cr[...] = jnp.zeros_like(cr)
            ci[...] = jnp.zeros_like(ci)

        u = u_ref[...]
        ub = u.astype(BF16)
        a_r = lax.dot_general(ub, bre_ref[0].astype(BF16), _NT, preferred_element_type=F32)
        a_i = lax.dot_general(ub, bim_ref[0].astype(BF16), _NT, preferred_element_type=F32)
        a_r, a_i = _scan_chunk(a_r, a_i, pr_ref, pi_ref, cr, ci, T, False)
        sr_ref[...] = a_r
        si_ref[...] = a_i
        y = (lax.dot_general(a_r.astype(BF16), cre_ref[0].astype(BF16), _NT, preferred_element_type=F32)
             + lax.dot_general(a_i.astype(BF16), cim_ref[0].astype(BF16), _NT, preferred_element_type=F32)
             + d_ref[...] * u)
        y_ref[...] = y
        yg_ref[...] = gelu(y)

    uspec = pl.BlockSpec((T, 128), lambda k, i: (i, k))
    sspec = pl.BlockSpec((T, WC), lambda k, i: (i, k))
    return carried(
        body, comm, grid=(8, nT),
        in_specs=[uspec, pl.BlockSpec((1, 128), lambda k, i: (0, k)),
                  pl.BlockSpec((1, WC, 128), lambda k, i: (k, 0, 0)), pl.BlockSpec((1, WC, 128), lambda k, i: (k, 0, 0)),
                  pl.BlockSpec((1, 128, WC), lambda k, i: (k, 0, 0)), pl.BlockSpec((1, 128, WC), lambda k, i: (k, 0, 0)),
                  pl.BlockSpec((T, WC), lambda k, i: (0, k)), pl.BlockSpec((T, WC), lambda k, i: (0, k))],
        out_specs=[uspec, uspec, sspec, sspec],
        out_shape=[jax.ShapeDtypeStruct((L, 1024), F32)] * 2 + [jax.ShapeDtypeStruct((L, 8 * WC), F32)] * 2,
        scratch_shapes=[pltpu.VMEM((1, WC), F32), pltpu.VMEM((1, WC), F32)],
        semantics=("parallel", "arbitrary"), name='a_ssm')(proj, d_skip, Bre, Bim, Cre, Cim, pr, pi)


def s5_bwd(proj, dyg1, dyg2, y, d_skip, s_re, s_im, Bre, Bim, Cre, Cim, prr, pir, comm=None):
    L = proj.shape[0]
    T, WC = min(SSM_T, L), SSM_WC
    nT = L // T

    def body(u_ref, g1_ref, g2_ref, y_ref, d_ref, sr_ref, si_ref, spr_ref, spi_ref, bre_ref, bim_ref, cre_ref, cim_ref,
             pr_ref, pi_ref, du_ref, dd_ref, dbre_ref, dbim_ref, dcre_ref, dcim_ref, dar_ref, dai_ref, cr, ci):
        i = pl.program_id(1)

        @pl.when(i == 0)
        def _():
            for ref in (cr, ci, dd_ref, dbre_ref, dbim_ref, dcre_ref, dcim_ref, dar_ref, dai_ref):
                ref[...] = jnp.zeros_like(ref)

        u = u_ref[...]
        dy = (g1_ref[...] + g2_ref[...]) * gelu_grad(y_ref[...])
        dd_ref[...] += jnp.sum(dy * u, axis=0, keepdims=True)
        dyb, ub = dy.astype(BF16), u.astype(BF16)
        bre, bim, cre, cim = (r[0].astype(BF16) for r in (bre_ref, bim_ref, cre_ref, cim_ref))
        g_r = jnp.dot(dyb, cre, preferred_element_type=F32)
        g_i = jnp.dot(dyb, cim, preferred_element_type=F32)
        g_r, g_i = _scan_chunk(g_r, g_i, pr_ref, pi_ref, cr, ci, T, True)
        s_r, s_i = sr_ref[...], si_ref[...]
        row = lax.broadcasted_iota(jnp.int32, (T, WC), 0)
        first = (nT - 1 - i) == 0
        sp_r = jnp.where(row == 0, jnp.where(first, 0.0, spr_ref[7:8, :]), pltpu.roll(s_r, 1, 0))
        sp_i = jnp.where(row == 0, jnp.where(first, 0.0, spi_ref[7:8, :]), pltpu.roll(s_i, 1, 0))
        dar_ref[...] += jnp.sum(g_r * sp_r + g_i * sp_i, axis=0, keepdims=True)
        dai_ref[...] += jnp.sum(g_i * sp_r - g_r * sp_i, axis=0, keepdims=True)
        grb, gib = g_r.astype(BF16), g_i.astype(BF16)
        dcre_ref[0] += lax.dot_general(dyb, s_r.astype(BF16), _TN, preferred_element_type=F32)
        dcim_ref[0] += lax.dot_general(dyb, s_i.astype(BF16), _TN, preferred_element_type=F32)
        dbre_ref[0] += lax.dot_general(grb, ub, _TN, preferred_element_type=F32)
        dbim_ref[0] += lax.dot_general(gib, ub, _TN, preferred_element_type=F32)
        du_ref[...] = (dy * d_ref[...] + jnp.dot(grb, bre, preferred_element_type=F32)
                       + jnp.dot(gib, bim, preferred_element_type=F32))

    uspec = pl.BlockSpec((T, 128), lambda k, i: (nT - 1 - i, k))
    sspec = pl.BlockSpec((T, WC), lambda k, i: (nT - 1 - i, k))
    pspec = pl.BlockSpec((8, WC), lambda k, i: (jnp.maximum((nT - 1 - i) * (T // 8) - 1, 0), k))
    tab = pl.BlockSpec((T, WC), lambda k, i: (0, k))
    bspec = pl.BlockSpec((1, WC, 128), lambda k, i: (k, 0, 0))
    cspec = pl.BlockSpec((1, 128, WC), lambda k, i: (k, 0, 0))
    return carried(
        body, comm, grid=(8, nT),
        in_specs=[uspec, uspec, uspec, uspec, pl.BlockSpec((1, 128), lambda k, i: (0, k)), sspec, sspec, pspec, pspec,
                  bspec, bspec, cspec, cspec, tab, tab],
        out_specs=[uspec, pl.BlockSpec((1, 128), lambda k, i: (0, k)), bspec, bspec, cspec, cspec,
                   pl.BlockSpec((1, WC), lambda k, i: (0, k)), pl.BlockSpec((1, WC), lambda k, i: (0, k))],
        out_shape=[jax.ShapeDtypeStruct((L, 1024), F32), jax.ShapeDtypeStruct((1, 1024), F32),
                   jax.ShapeDtypeStruct((8, WC, 128), F32), jax.ShapeDtypeStruct((8, WC, 128), F32),
                   jax.ShapeDtypeStruct((8, 128, WC), F32), jax.ShapeDtypeStruct((8, 128, WC), F32),
                   jax.ShapeDtypeStruct((1, 8 * WC), F32), jax.ShapeDtypeStruct((1, 8 * WC), F32)],
        scratch_shapes=[pltpu.VMEM((1, WC), F32), pltpu.VMEM((1, WC), F32)],
        semantics=("parallel", "arbitrary"), name='a_ssm_bwd')(
            proj, dyg1, dyg2, y, d_skip, s_re, s_im, s_re, s_im, Bre, Bim, Cre, Cim, prr, pir)


def s5_discretize(lam_re, lam_im, log_dt, b_re, b_im):
    dt = jnp.exp(log_dt)[:, None]
    mag = jnp.exp(lam_re * dt)
    ab_re = mag * jnp.cos(lam_im * dt)
    ab_im = mag * jnp.sin(lam_im * dt)
    den = lam_re * lam_re + lam_im * lam_im
    nr = ab_re - 1.0
    f_re = (nr * lam_re + ab_im * lam_im) / den
    f_im = (ab_im * lam_re - nr * lam_im) / den
    bb_re = f_re[..., None] * b_re - f_im[..., None] * b_im
    bb_im = f_re[..., None] * b_im + f_im[..., None] * b_re
    return ab_re, ab_im, bb_re, bb_im


def s5_prep(bb_re, bb_im, c_re, c_im, ar, ai, T, comm=None):
    W = ar.shape[1]

    def body(bbr_ref, bbi_ref, cre_ref, cim_ref, ar_ref, ai_ref, btr_ref, bti_ref, ctr_ref, cti_ref, fr_ref, fi_ref,
             rr_ref, ri_ref):
        for ref in (btr_ref, bti_ref, ctr_ref, cti_ref):
            ref[...] = jnp.zeros_like(ref)
        for g in range(8):
            rows, cols = slice(g * SSM_P, (g + 1) * SSM_P), slice(g * SSM_H, (g + 1) * SSM_H)
            btr_ref[0, rows, cols] = bbr_ref[g]
            bti_ref[0, rows, cols] = bbi_ref[g]
            ctr_ref[0, cols, rows] = cre_ref[g]
            cti_ref[0, cols, rows] = -cim_ref[g]
        fr_ref[0:1, :] = ar_ref[...]
        fi_ref[0:1, :] = ai_ref[...]
        rr_ref[T - 1:T, :] = ar_ref[...]
        ri_ref[T - 1:T, :] = ai_ref[...]
        n = 1
        while n < T:
            cr, ci = fr_ref[0:n, :], fi_ref[0:n, :]
            lr, li = fr_ref[n - 1:n, :], fi_ref[n - 1:n, :]
            fr_ref[n:2 * n, :] = cr * lr - ci * li
            fi_ref[n:2 * n, :] = cr * li + ci * lr
            cr, ci = rr_ref[T - n:T, :], ri_ref[T - n:T, :]
            rr_ref[T - 2 * n:T - n, :] = cr * lr - ci * li
            ri_ref[T - 2 * n:T - n, :] = cr * li + ci * lr
            n *= 2

    spec = pl.BlockSpec((T, SSM_WC), lambda j: (0, j))
    aspec = pl.BlockSpec((1, SSM_WC), lambda j: (0, j))
    bspec, cspec = pl.BlockSpec((8, SSM_P, SSM_H), lambda j: (j, 0, 0)), pl.BlockSpec((8, SSM_H, SSM_P), lambda j: (j, 0, 0))
    btspec = pl.BlockSpec((1, SSM_WC, 128), lambda j: (j, 0, 0))
    ctspec = pl.BlockSpec((1, 128, SSM_WC), lambda j: (j, 0, 0))
    return carried(
        body, comm, grid=(W // SSM_WC,), in_specs=[bspec, bspec, cspec, cspec, aspec, aspec],
        out_specs=[btspec, btspec, ctspec, ctspec] + [spec] * 4,
        out_shape=[jax.ShapeDtypeStruct((8, SSM_WC, 128), F32)] * 2 + [jax.ShapeDtypeStruct((8, 128, SSM_WC), F32)] * 2
        + [jax.ShapeDtypeStruct((T, W), F32)] * 4,
        semantics=("parallel",), name='a_prep')(bb_re, bb_im, c_re, c_im, ar, ai)


def s5_untile(dbtr, dbti, dctr, dcti):
    def body(dbtr_ref, dbti_ref, dctr_ref, dcti_ref, br_ref, bi_ref, cr_ref, ci_ref):
        for g in range(8):
            rows, cols = slice(g * SSM_P, (g + 1) * SSM_P), slice(g * SSM_H, (g + 1) * SSM_H)
            br_ref[g] = dbtr_ref[0, rows, cols]
            bi_ref[g] = dbti_ref[0, rows, cols]
            cr_ref[g] = dctr_ref[0, cols, rows]
            ci_ref[g] = -dcti_ref[0, cols, rows]

    bspec, cspec = pl.BlockSpec((8, SSM_P, SSM_H), lambda j: (j, 0, 0)), pl.BlockSpec((8, SSM_H, SSM_P), lambda j: (j, 0, 0))
    btspec = pl.BlockSpec((1, SSM_WC, 128), lambda j: (j, 0, 0))
    ctspec = pl.BlockSpec((1, 128, SSM_WC), lambda j: (j, 0, 0))
    return pl.pallas_call(
        body, grid=(8,), in_specs=[btspec, btspec, ctspec, ctspec], out_specs=[bspec, bspec, cspec, cspec],
        out_shape=[jax.ShapeDtypeStruct((SSM_G, SSM_P, SSM_H), F32)] * 2 + [jax.ShapeDtypeStruct((SSM_G, SSM_H, SSM_P), F32)] * 2,
        compiler_params=pltpu.CompilerParams(dimension_semantics=("parallel",)), name='a_untile')(dbtr, dbti, dctr, dcti)


def layer_a_fwd(h, w, p, comm=None, on_carried=None, prep_comm=None, on_prep=None):
    L = h.shape[0]
    disc = lambda *a: s5_discretize(*a)
    (ab_re, ab_im, bb_re, bb_im), disc_vjp = jax.vjp(disc, p['a_lam_re'][0], p['a_lam_im'][0], p['a_log_dt'][0],
                                                     p['a_b_re'][0], p['a_b_im'][0])
    T = min(SSM_T, L)
    (Bre, Bim, Cre, Cim, pr, pi, prr, pir), prepped = s5_prep(bb_re, bb_im, p['a_c_re'][0], p['a_c_im'][0],
                                                              ab_re.reshape(1, -1), ab_im.reshape(1, -1), T,
                                                              comm=prep_comm)
    if on_prep is not None:
        on_prep(prepped)
    proj = mm(h, w['a_w_in'], 'nn', 'a_proj')
    (y, yg, s_re, s_im), carried_out = s5_fwd(proj, p['a_d'], Bre, Bim, Cre, Cim, pr, pi, comm=comm)
    if on_carried is not None:
        on_carried(carried_out)
    gl = mm(yg, w['a_w_glu'], 'nn', 'a_glu')

    def f2(yg_, gl_, z, bg):
        return [yg_ * sigmoid(gl_ + bg) * silu(z)], []
    (po,), _ = rowwise(f2, [rw(yg), rw(gl), rw(proj, 1024, 1)], [p['a_b_glu']], [(1024, BF16)], [], 256, 'a_gate')
    yb = mm(po, w['a_w_out'], 'nn', 'a_out')
    saved = dict(carried=carried_out, h=h, proj=proj, disc_vjp=disc_vjp, Bre=Bre, Bim=Bim, Cre=Cre, Cim=Cim, prr=prr, pir=pir, s_re=s_re,
                 s_im=s_im, y=y, yg=yg, gl=gl, po=po)
    return yb, saved


def _dw(g, sink, name, a, b, mm_name):
    if sink is None:
        g[name] = mm(a, b, 'tn', mm_name)
    else:
        sink.put(name, a, b, mm_name)


def layer_a_bwd(dyb, w, p, sv, comm=None, sink=None):
    g = {}
    dpo = mm(dyb, w['a_w_out'], 'nt', 'a_dpo')
    _dw(g, sink, 'a_w_out', sv['po'], dyb, 'a_dwout')
    proj = sv['proj']

    def f1(dpo_, yg, gl, z, bg):
        sg = sigmoid(gl + bg)
        sz = silu(z)
        dm = dpo_ * sz
        dz = dpo_ * (yg * sg) * silu_grad(z)
        dgl = dm * yg * sg * (1.0 - sg)
        return [dz, dm * sg, dgl], [jnp.sum(dgl, axis=0, keepdims=True)]
    (dz, dyg1, dgl), (db_glu,) = rowwise(f1, [rw(dpo), rw(sv['yg']), rw(sv['gl']), rw(proj, 1024, 1)], [p['a_b_glu']],
                                          [(1024, F32), (1024, F32), (1024, BF16)], [(1, 1024)], 256, 'a_gate_bwd')
    g['a_b_glu'] = db_glu
    _dw(g, sink, 'a_w_glu', sv['yg'], dgl, 'a_dwglu')
    dyg2 = mm(dgl, w['a_w_glu'], 'nt', 'a_dyg2')

    if callable(comm):
        comm = comm()
    (du, dd, dBre, dBim, dCre, dCim, da_re, da_im), g['carried'] = s5_bwd(
        proj, dyg1, dyg2, sv['y'], p['a_d'], sv['s_re'], sv['s_im'], sv['Bre'], sv['Bim'], sv['Cre'], sv['Cim'],
        sv['prr'], sv['pir'], comm=comm)
    g['a_d'] = dd

    def f3(du_, dz_):
        return [jnp.concatenate([du_, dz_], axis=1)], []
    (dproj,), _ = rowwise(f3, [rw(du), rw(dz)], [], [(2048, BF16)], [], 256, 'a_dproj')
    dbb_re, dbb_im, dc_re, dc_im = s5_untile(dBre, dBim, dCre, dCim)
    dlr, dli, dldt, dbr, dbi = sv['disc_vjp']((da_re.reshape(SSM_G, SSM_P), da_im.reshape(SSM_G, SSM_P), dbb_re, dbb_im))
    g['a_lam_re'], g['a_lam_im'], g['a_log_dt'] = dlr[None], dli[None], dldt[None]
    g['a_b_re'], g['a_b_im'] = dbr[None], dbi[None]
    g['a_c_re'], g['a_c_im'] = dc_re[None], dc_im[None]
    _dw(g, sink, 'a_w_in', sv['h'], dproj, 'a_dwin')
    if sink is None:
        dh = mm(dproj, w['a_w_in'], 'nt', 'a_dh')
    else:
        dh, (g['land_a1'],) = mm(dproj, w['a_w_in'], 'nt', 'a_dh', comm=SiblingExchange(sink.bufs['a1']))
    return dh, g


def _t5_bucket_np():
    qi = np.arange(WINDOW)[:, None]
    kj = np.arange(2 * WINDOW)[None, :]
    dist = np.maximum(qi + WINDOW - kj, 0)
    max_exact = REL_BUCKETS // 2
    dist_f = np.maximum(dist, 1).astype(np.float32)
    large = max_exact + (np.log(dist_f / np.float32(max_exact)) / np.float32(math.log(REL_MAX_DIST / max_exact))
                         * np.float32(REL_BUCKETS - max_exact)).astype(np.int32)
    large = np.minimum(large, REL_BUCKETS - 1)
    return np.where(dist < max_exact, dist, large).astype(np.int32)


SWA_GRP = SWA_HEADS // SWA_KV


def _swa_kv(kvp, kvc, kvh):
    kb = jnp.concatenate([kvp[:, kvh * 64:(kvh + 1) * 64], kvc[:, kvh * 64:(kvh + 1) * 64]], 0).astype(BF16)
    vb = jnp.concatenate([kvp[:, 128 + kvh * 64:128 + (kvh + 1) * 64], kvc[:, 128 + kvh * 64:128 + (kvh + 1) * 64]],
                         0).astype(BF16)
    return kb, vb


def _swa_stack(x, kvh):
    return jnp.concatenate([x[:, (kvh * SWA_GRP + g) * 64:(kvh * SWA_GRP + g + 1) * 64] for g in range(SWA_GRP)],
                           axis=0).astype(BF16)


def _swa_group(bias_ref, kvh):
    return bias_ref[kvh * SWA_GRP:(kvh + 1) * SWA_GRP].reshape(SWA_GRP * WINDOW, 2 * WINDOW)


def _swa_sinks(sink_ref, kvh):
    return jnp.concatenate([jnp.broadcast_to(sink_ref[0:1, kvh * SWA_GRP + g:kvh * SWA_GRP + g + 1], (WINDOW, 1))
                            for g in range(SWA_GRP)], axis=0)


def _swa_probs(q, kb, bias_h, sink, valid):
    s = lax.dot_general(q, kb, (((1,), (1,)), ((), ())), preferred_element_type=F32) * (HEAD_DIM ** -0.5)
    s = jnp.where(valid, s + bias_h, NEG_INF)
    m = jnp.maximum(jnp.max(s, axis=-1, keepdims=True), sink)
    e = jnp.exp(s - m)
    es = jnp.exp(sink - m)
    den = jnp.sum(e, axis=-1, keepdims=True) + es
    return e / den, es / den


def _swa_valid(n):
    qi = lax.broadcasted_iota(jnp.int32, (SWA_GRP * WINDOW, 2 * WINDOW), 0) & (WINDOW - 1)
    kj = lax.broadcasted_iota(jnp.int32, (SWA_GRP * WINDOW, 2 * WINDOW), 1)
    dist = qi + WINDOW - kj
    return (dist >= 0) & (dist < WINDOW) & ((kj >= WINDOW) | (n > 0))


def swa_fwd(proj, bias, sinks, comm=None):
    L = proj.shape[0]

    def body(z_ref, q_ref, kvc_ref, kvp_ref, bias_ref, sink_ref, o_ref, po_ref):
        n = pl.program_id(0)
        valid = _swa_valid(n)
        q, kvc, kvp = q_ref[...], kvc_ref[...], kvp_ref[...]
        outs = []
        for kvh in range(SWA_KV):
            kb, vb = _swa_kv(kvp, kvc, kvh)
            p, _ = _swa_probs(_swa_stack(q, kvh), kb, _swa_group(bias_ref, kvh), _swa_sinks(sink_ref, kvh), valid)
            o8 = jnp.dot(p.astype(BF16), vb, preferred_element_type=F32)
            outs += [o8[g * WINDOW:(g + 1) * WINDOW] for g in range(SWA_GRP)]
        o = jnp.concatenate(outs, axis=1)
        o_ref[...] = o
        po_ref[...] = (o * silu(z_ref[...])).astype(po_ref.dtype)

    return carried(
        body, comm, grid=(L // WINDOW,),
        in_specs=[pl.BlockSpec((WINDOW, 1024), lambda n: (n, 0)), pl.BlockSpec((WINDOW, 1024), lambda n: (n, 1)),
                  pl.BlockSpec((WINDOW, 256), lambda n: (n, 8)),
                  pl.BlockSpec((WINDOW, 256), lambda n: (jnp.maximum(n - 1, 0), 8)),
                  pl.BlockSpec((SWA_HEADS, WINDOW, 2 * WINDOW), lambda n: (0, 0, 0)),
                  pl.BlockSpec((1, SWA_HEADS), lambda n: (0, 0))],
        out_specs=[pl.BlockSpec((WINDOW, 1024), lambda n: (n, 0))] * 2,
        out_shape=[jax.ShapeDtypeStruct((L, 1024), F32), jax.ShapeDtypeStruct((L, 1024), BF16)],
        semantics=("parallel",), name='b_attn')(proj, proj, proj, proj, bias, sinks)


def swa_bwd(proj, do, bias, sinks, comm=None):
    L = proj.shape[0]

    def body(q_ref, kvc_ref, kvp_ref, do_ref, bias_ref, sink_ref, dq_ref, dkv_ref, dbias_ref, dsink_ref):
        n = pl.program_id(0)

        @pl.when(n == 0)
        def _():
            dkv_ref[...] = jnp.zeros_like(dkv_ref)
            dbias_ref[...] = jnp.zeros_like(dbias_ref)
            dsink_ref[...] = jnp.zeros_like(dsink_ref)

        valid = _swa_valid(n)
        q, kvc, kvp, do_ = q_ref[...], kvc_ref[...], kvp_ref[...], do_ref[...]
        dqs, dks, dvs, dsk = [], [], [], []
        for kvh in range(SWA_KV):
            kb, vb = _swa_kv(kvp, kvc, kvh)
            q8, do8 = _swa_stack(q, kvh), _swa_stack(do_, kvh)
            p, ps = _swa_probs(q8, kb, _swa_group(bias_ref, kvh), _swa_sinks(sink_ref, kvh), valid)
            dp = lax.dot_general(do8, vb, (((1,), (1,)), ((), ())), preferred_element_type=F32)
            delta = jnp.sum(p * dp, axis=-1, keepdims=True)
            ds = p * (dp - delta)
            col = -ps * delta
            dsk += [jnp.sum(col[g * WINDOW:(g + 1) * WINDOW], axis=0, keepdims=True) for g in range(SWA_GRP)]
            dbias_ref[kvh * SWA_GRP:(kvh + 1) * SWA_GRP] += ds.reshape(SWA_GRP, WINDOW, 2 * WINDOW)
            dsb = (ds * (HEAD_DIM ** -0.5)).astype(BF16)
            dq8 = jnp.dot(dsb, kb, preferred_element_type=F32)
            dqs += [dq8[g * WINDOW:(g + 1) * WINDOW] for g in range(SWA_GRP)]
            dks.append(lax.dot_general(dsb, q8, (((0,), (0,)), ((), ())), preferred_element_type=F32))
            dvs.append(lax.dot_general(p.astype(BF16), do8, (((0,), (0,)), ((), ())), preferred_element_type=F32))
        dq_ref[...] = jnp.concatenate(dqs, axis=1)
        dsink_ref[...] += jnp.concatenate(dsk, axis=1)
        both = jnp.concatenate(dks + dvs, axis=1)
        r_cur = pl.multiple_of(n * WINDOW, WINDOW)
        r_prev = pl.multiple_of(jnp.maximum(n - 1, 0) * WINDOW, WINDOW)
        dkv_ref[pl.ds(r_prev, WINDOW), :] += both[:WINDOW]
        dkv_ref[pl.ds(r_cur, WINDOW), :] += both[WINDOW:]

    return carried(
        body, comm, grid=(L // WINDOW,),
        in_specs=[pl.BlockSpec((WINDOW, 1024), lambda n: (n, 1)), pl.BlockSpec((WINDOW, 256), lambda n: (n, 8)),
                  pl.BlockSpec((WINDOW, 256), lambda n: (jnp.maximum(n - 1, 0), 8)),
                  pl.BlockSpec((WINDOW, 1024), lambda n: (n, 0)),
                  pl.BlockSpec((SWA_HEADS, WINDOW, 2 * WINDOW), lambda n: (0, 0, 0)),
                  pl.BlockSpec((1, SWA_HEADS), lambda n: (0, 0))],
        out_specs=[pl.BlockSpec((WINDOW, 1024), lambda n: (n, 0)), pl.BlockSpec((L, 256), lambda n: (0, 0)),
                   pl.BlockSpec((SWA_HEADS, WINDOW, 2 * WINDOW), lambda n: (0, 0, 0)),
                   pl.BlockSpec((1, SWA_HEADS), lambda n: (0, 0))],
        out_shape=[jax.ShapeDtypeStruct((L, 1024), F32), jax.ShapeDtypeStruct((L, 256), F32),
                   jax.ShapeDtypeStruct((SWA_HEADS, WINDOW, 2 * WINDOW), F32), jax.ShapeDtypeStruct((1, SWA_HEADS), F32)],
        semantics=("arbitrary",), name='b_attn_bwd')(proj, proj, proj, do, bias, sinks)


def swa_bias(rel_bias):
    def body(bk_ref, rb_ref, o_ref):
        bk = bk_ref[...]
        for h in range(SWA_HEADS):
            acc = jnp.zeros((WINDOW, 2 * WINDOW), F32)
            for b in range(REL_BUCKETS):
                acc = jnp.where(bk == b, rb_ref[b, h], acc)
            o_ref[h] = acc

    return pl.pallas_call(
        body, out_shape=jax.ShapeDtypeStruct((SWA_HEADS, WINDOW, 2 * WINDOW), F32),
        in_specs=[pl.BlockSpec(memory_space=pltpu.VMEM), pl.BlockSpec(memory_space=pltpu.SMEM)],
        out_specs=pl.BlockSpec(memory_space=pltpu.VMEM), name='b_bias')(jnp.asarray(_t5_bucket_np()), rel_bias)


def layer_b_fwd(h, w, p, comm=None):
    proj = mm(h, w['b_w_in'], 'nn', 'b_proj')
    bias = swa_bias(p['rel_bias'])
    (o, po), carried_out = swa_fwd(proj, bias, p['b_sinks'], comm=comm)
    yb = mm(po, w['b_w_out'], 'nn', 'b_out')
    return yb, dict(carried=carried_out, h=h, proj=proj, bias=bias, o=o, po=po)


def layer_b_bwd(dyb, w, p, sv, comm=None, sink=None):
    g = {}
    dpo = mm(dyb, w['b_w_out'], 'nt', 'b_dpo')
    _dw(g, sink, 'b_w_out', sv['po'], dyb, 'b_dwout')
    proj = sv['proj']

    def f1(dpo_, o, z):
        return [dpo_ * silu(z), dpo_ * o * silu_grad(z)], []
    (do, dz), _ = rowwise(f1, [rw(dpo), rw(sv['o']), rw(proj, 1024, 0)], [], [(1024, BF16), (1024, F32)], [], 256, 'b_gate_bwd')
    (dq, dkv, dbias, dsinks), g['carried'] = swa_bwd(proj, do, sv['bias'], p['b_sinks'], comm=comm)
    g['b_sinks'] = dsinks
    onehot = jnp.asarray(np.eye(REL_BUCKETS, dtype=np.float32)[_t5_bucket_np().reshape(-1)])

    def f2(db, oh):
        return [], [lax.dot_general(db, oh, (((1,), (0,)), ((), ())), preferred_element_type=F32,
                                    precision=lax.Precision.HIGHEST)]
    _, (drel,) = rowwise(f2, [(dbias.reshape(SWA_HEADS, -1), pl.BlockSpec((SWA_HEADS, 4096), lambda i: (0, i))),
                              (onehot, pl.BlockSpec((4096, REL_BUCKETS), lambda i: (i, 0)))], [], [],
                         [(SWA_HEADS, REL_BUCKETS)], 4096, 'b_drel', n_steps=(2 * WINDOW * WINDOW) // 4096)
    g['rel_bias'] = drel.T

    def f3(dz_, dq_, dkv_):
        return [jnp.concatenate([dz_, dq_, dkv_], axis=1)], []
    (dproj,), _ = rowwise(f3, [rw(dz), rw(dq), rw(dkv)], [], [(2304, BF16)], [], 256, 'b_dproj')
    _dw(g, sink, 'b_w_in', sv['h'], dproj, 'b_dwin')
    dh = mm(dproj, w['b_w_in'], 'nt', 'b_dh')
    return dh, g


MLA_SCALE = (MLA_NOPE + MLA_ROPE) ** -0.5
_LOG2E = math.log2(math.e)


def _rope_tables(L):
    inv = ROPE_BASE ** (-jnp.arange(0, MLA_ROPE, 2, dtype=F32) / MLA_ROPE)
    ang = jnp.arange(L, dtype=F32)[:, None] * inv[None, :]
    c, s = jnp.cos(ang), jnp.sin(ang)
    one, zero, pad = jnp.ones((L, 128), F32), jnp.zeros((L, 128), F32), jnp.zeros((L, 64), F32)
    return (jnp.concatenate([one, c, c, c, c, pad], 1), jnp.concatenate([zero, s, s, s, s, pad], 1))


def _rot(x, transpose=False):
    w = x.shape[1]
    lane = lax.broadcasted_iota(jnp.int32, x.shape, 1)
    up = pltpu.roll(x, w - 16, 1)
    dn = pltpu.roll(x, 16, 1)
    first = (lane % 32) < 16
    return jnp.where(first, up, -dn) if transpose else jnp.where(first, -up, dn)


MLA_QT = 512


def _mla_exp(qf, kf, t, qt):
    s = lax.dot_general(qf, kf, (((1,), (1,)), ((), ())), preferred_element_type=F32)
    causal = lax.broadcasted_iota(jnp.int32, (qt, qt), 1) <= lax.broadcasted_iota(jnp.int32, (qt, qt), 0)
    last = jnp.where(causal, s[:, t * qt:], NEG_INF)
    s = last if t == 0 else jnp.concatenate([s[:, :t * qt], last], axis=1)
    e = jnp.exp2((s - jnp.max(s, axis=-1, keepdims=True)) * (MLA_SCALE * _LOG2E))
    return e, jnp.sum(e, axis=-1, keepdims=True)


def _mla_heads(q, kv, kr):
    out = []
    for j in range(2):
        qf = jnp.concatenate([q[:, j * 64:(j + 1) * 64], q[:, 128 + j * 32:128 + (j + 1) * 32]], axis=1)
        kf = jnp.concatenate([kv[:, j * 64:(j + 1) * 64], kr], axis=1)
        out.append((qf, kf, kv[:, 128 + j * 64:128 + (j + 1) * 64]))
    return out


def mla_fwd(q, kv, kr, comm=None):
    L = q.shape[0]
    qt = min(MLA_QT, L)
    nq = L // qt

    def body(q_ref, kv_ref, kr_ref, o_ref):
        for t in range(nq):
            @pl.when(pl.program_id(1) == t)
            def _(t=t):
                n_k = (t + 1) * qt
                outs = []
                for qf, kf, v in _mla_heads(q_ref[...], kv_ref[0:n_k, :], kr_ref[0:n_k, 0:MLA_ROPE]):
                    e, den = _mla_exp(qf, kf, t, qt)
                    outs.append(jnp.dot(e.astype(BF16), v, preferred_element_type=F32) / den)
                o_ref[...] = jnp.concatenate(outs, axis=1)

    return carried(
        body, comm, grid=(MLA_HEADS // 2, nq),
        in_specs=[pl.BlockSpec((qt, 256), lambda hp, n: (n, hp)), pl.BlockSpec((L, 256), lambda hp, n: (0, hp)),
                  pl.BlockSpec((L, 128), lambda hp, n: (0, 0))],
        out_specs=pl.BlockSpec((qt, 128), lambda hp, n: (n, hp)), out_shape=jax.ShapeDtypeStruct((L, 1024), F32),
        semantics=("parallel", "parallel"), name='c_attn')(q, kv, kr)


def mla_bwd(q, kv, kr, do, o, comm=None):
    L = q.shape[0]
    qt = min(MLA_QT, L)
    nq = L // qt

    def body(q_ref, kv_ref, kr_ref, do_ref, o_ref, dq_ref, dkv_ref, dkr_ref):
        @pl.when(pl.program_id(1) == 0)
        def _():
            dkv_ref[...] = jnp.zeros_like(dkv_ref)
            dkr_ref[...] = jnp.zeros_like(dkr_ref)

        for t in range(nq):
            @pl.when(pl.program_id(1) == t)
            def _(t=t):
                n_k = (t + 1) * qt
                do_, o_ = do_ref[...], o_ref[...]
                dqn, dqr, dkn, dvs = [], [], [], []
                dkr = jnp.zeros((MLA_ROPE, n_k), F32)
                wide = lambda x: jnp.concatenate([x, jnp.zeros((qt, 128 - x.shape[1]), x.dtype)], axis=1)
                for j, (qf, kf, v) in enumerate(_mla_heads(q_ref[...], kv_ref[0:n_k, :], kr_ref[0:n_k, 0:MLA_ROPE])):
                    doh = do_[:, j * 64:(j + 1) * 64]
                    dof = doh.astype(F32)
                    e, den = _mla_exp(qf, kf, t, qt)
                    inv = 1.0 / den
                    dp = lax.dot_general(doh, v, (((1,), (1,)), ((), ())), preferred_element_type=F32)
                    delta = jnp.sum(dof * o_[:, j * 64:(j + 1) * 64], axis=-1, keepdims=True)
                    ds = (e * ((dp - delta) * (inv * MLA_SCALE))).astype(BF16)
                    dqf = jnp.dot(ds, kf, preferred_element_type=F32)
                    dkf = lax.dot_general(wide(qf), ds, _TN, preferred_element_type=F32)
                    dvf = lax.dot_general(wide((dof * inv).astype(BF16)), e.astype(BF16), _TN,
                                          preferred_element_type=F32)
                    dqn.append(dqf[:, :MLA_NOPE])
                    dqr.append(dqf[:, MLA_NOPE:])
                    dkn.append(dkf[:MLA_NOPE])
                    dvs.append(dvf[:MLA_V])
                    dkr = dkr + dkf[MLA_NOPE:MLA_NOPE + MLA_ROPE]
                dq_ref[...] = jnp.concatenate(dqn + dqr + [jnp.zeros((qt, 64), F32)], axis=1)
                dkv_ref[0:n_k, :] += jnp.concatenate(dkn + dvs, axis=0).T
                dkr_ref[0, 0:n_k, :] += jnp.concatenate([dkr, jnp.zeros((128 - MLA_ROPE, n_k), F32)], axis=0).T

    return carried(
        body, comm, grid=(MLA_HEADS // 2, nq),
        in_specs=[pl.BlockSpec((qt, 256), lambda hp, n: (n, hp)), pl.BlockSpec((L, 256), lambda hp, n: (0, hp)),
                  pl.BlockSpec((L, 128), lambda hp, n: (0, 0)), pl.BlockSpec((qt, 128), lambda hp, n: (n, hp)),
                  pl.BlockSpec((qt, 128), lambda hp, n: (n, hp))],
        out_specs=[pl.BlockSpec((qt, 256), lambda hp, n: (n, hp)), pl.BlockSpec((L, 256), lambda hp, n: (0, hp)),
                   pl.BlockSpec((1, L, 128), lambda hp, n: (hp, 0, 0))],
        out_shape=[jax.ShapeDtypeStruct((L, 2048), F32), jax.ShapeDtypeStruct((L, 2048), F32),
                   jax.ShapeDtypeStruct((MLA_HEADS // 2, L, 128), F32)],
        semantics=("parallel", "arbitrary"), name='c_attn_bwd')(q, kv, kr, do, o)


def layer_c_fwd(h, w, p, comm=None):
    L = h.shape[0]
    proj = mm(h, w['c_w_in'], 'nn', 'c_proj')

    def f1(c, gq, gk):
        return [rms_fwd(c[:, :768], gq), rms_fwd(c[:, 768:], gk)], []
    (cqn, ckvn), _ = rowwise(f1, [rw(proj, 1024, 1)], [p['c_q_norm'], p['c_kv_norm']], [(768, BF16), (256, BF16)], [],
                             256, 'c_norms')
    qf = mm(cqn, w['c_w_uq'], 'nn', 'c_uq')
    kvf = mm(ckvn, w['c_w_ukv'], 'nn', 'c_ukv', out_dtype=BF16)
    cos, sin = _rope_tables(L)

    def f2(q_, kr_, c, s):
        c8, s8 = jnp.tile(c, (1, 8)), jnp.tile(s, (1, 8))
        return [q_ * c8 + _rot(q_) * s8, kr_ * c[:, 128:] + _rot(kr_) * s[:, 128:]], []
    (q, kr), _ = rowwise(f2, [rw(qf), rw(proj, 128, 16), rw(cos), rw(sin)], [], [(2048, BF16), (128, BF16)], [], 256,
                         'c_rope')
    o, carried_out = mla_fwd(q, kvf, kr, comm=comm)

    def f3(o_, z):
        return [o_ * silu(z)], []
    (po,), _ = rowwise(f3, [rw(o), rw(proj, 1024, 0)], [], [(1024, BF16)], [], 256, 'c_gate')
    yb = mm(po, w['c_w_out'], 'nn', 'c_out')
    return yb, dict(carried=carried_out, h=h, proj=proj, cqn=cqn, ckvn=ckvn, q=q, kv=kvf, kr=kr, o=o, po=po, cos=cos, sin=sin)


def layer_c_bwd(dyb, w, p, sv, comm=None, sink=None):
    g = {}
    dpo = mm(dyb, w['c_w_out'], 'nt', 'c_dpo')
    _dw(g, sink, 'c_w_out', sv['po'], dyb, 'c_dwout')
    proj = sv['proj']
    L = proj.shape[0]

    def f1(dpo_, o, z):
        return [dpo_ * silu(z), dpo_ * o * silu_grad(z)], []
    (do, dz), _ = rowwise(f1, [rw(dpo), rw(sv['o']), rw(proj, 1024, 0)], [], [(1024, BF16), (1024, F32)], [], 256,
                          'c_gate_bwd')
    (dq, dkvf, dkr8), g['carried'] = mla_bwd(sv['q'], sv['kv'], sv['kr'], do, sv['o'], comm=comm)

    def f2(dq_, dkr_, c, s):
        c8, s8 = jnp.tile(c, (1, 8)), jnp.tile(s, (1, 8))
        dk = jnp.sum(dkr_, axis=0)
        return [dq_ * c8 + _rot(dq_ * s8, True), dk * c[:, 128:] + _rot(dk * s[:, 128:], True)], []
    tl = 256
    (dqf, dkr), _ = rowwise(f2, [rw(dq), (dkr8, pl.BlockSpec((8, tl, 128), lambda i: (0, i, 0))), rw(sv['cos']),
                                 rw(sv['sin'])], [], [(2048, BF16), (128, F32)], [], tl, 'c_rope_bwd')
    _dw(g, sink, 'c_w_uq', sv['cqn'], dqf, 'c_dwuq')
    _dw(g, sink, 'c_w_ukv', sv['ckvn'], dkvf, 'c_dwukv')
    dcqn = mm(dqf, w['c_w_uq'], 'nt', 'c_dcqn')
    dckvn = mm(dkvf, w['c_w_ukv'], 'nt', 'c_dckvn')

    def f3(c, dq_, dk_, dz_, dkr_, gq, gk):
        dcq, dgq = rms_bwd(c[:, :768], gq, dq_)
        dckv, dgk = rms_bwd(c[:, 768:], gk, dk_)
        return [jnp.concatenate([dz_, dcq, dckv, dkr_], axis=1)], [dgq, dgk]
    (dproj,), (dgq, dgk) = rowwise(f3, [rw(proj, 1024, 1), rw(dcqn), rw(dckvn), rw(dz), rw(dkr)],
                                   [p['c_q_norm'], p['c_kv_norm']], [(2176, BF16)], [(1, 768), (1, 256)], 256, 'c_dproj')
    g['c_q_norm'], g['c_kv_norm'] = dgq, dgk
    _dw(g, sink, 'c_w_in', sv['h'], dproj, 'c_dwin')
    dh = mm(dproj, w['c_w_in'], 'nt', 'c_dh')
    return dh, g


def _sgu_mix(wm, v, transpose):
    outs = []
    dims = (((0,), (0,)), ((), ())) if transpose else (((1,), (0,)), ((), ()))
    for gi in range(SGU_G):
        outs.append(lax.dot_general(wm[gi], v[:, gi * SGU_C:(gi + 1) * SGU_C].astype(BF16), dims,
                                    preferred_element_type=F32))
    return jnp.concatenate(outs, axis=1)


def _sgu_wmask(ws):
    t = lax.broadcasted_iota(jnp.int32, (SGU_T, SGU_T), 0)
    s = lax.broadcasted_iota(jnp.int32, (SGU_T, SGU_T), 1)
    return jnp.where((s <= t)[None], ws, 0.0).astype(BF16)


def _ln_stats(v):
    mu = jnp.mean(v, axis=-1, keepdims=True)
    vc = v - mu
    rstd = lax.rsqrt(jnp.mean(vc * vc, axis=-1, keepdims=True) + EPS)
    return vc * rstd, rstd


def layer_d_fwd(h, w, p):
    proj = mm(h, w['d_w_in'], 'nn', 'd_proj')
    bias = jnp.repeat(p['d_b_s'][0].T, SGU_C, axis=1)

    def f1(u_, v_, z, ws, lg, lb, bs):
        xh, _ = _ln_stats(gelu(v_))
        s = _sgu_mix(_sgu_wmask(ws), xh * lg + lb, False) + bs
        return [gelu(u_) * s * silu(z)], []
    (po,), _ = rowwise(f1, [rw(proj, 1024, 0), rw(proj, 1024, 1), rw(proj, 1024, 2)],
                       [p['d_w_s'][0], p['d_ln_g'], p['d_ln_b'], bias], [(1024, BF16)], [], SGU_T, 'd_mix')
    yb = mm(po, w['d_w_out'], 'nn', 'd_out')
    return yb, dict(h=h, proj=proj, po=po, bias=bias)


def layer_d_bwd(dyb, w, p, sv, sink=None):
    g = {}
    dpo = mm(dyb, w['d_w_out'], 'nt', 'd_dpo')
    _dw(g, sink, 'd_w_out', sv['po'], dyb, 'd_dwout')
    proj = sv['proj']

    def f1(dpo_, u_, v_, z, ws, lg, lb, bs):
        wm = _sgu_wmask(ws)
        gv = gelu(v_)
        xh, rstd = _ln_stats(gv)
        vn = xh * lg + lb
        s = _sgu_mix(wm, vn, False) + bs
        gu, sz = gelu(u_), silu(z)
        du = dpo_ * s * sz
        ds = dpo_ * gu * sz
        dz = dpo_ * gu * s * silu_grad(z)
        dsb = ds.astype(BF16)
        dws = jnp.stack([lax.dot_general(dsb[:, gi * SGU_C:(gi + 1) * SGU_C], vn[:, gi * SGU_C:(gi + 1) * SGU_C].astype(BF16),
                                         (((1,), (1,)), ((), ())), preferred_element_type=F32) for gi in range(SGU_G)])
        dvn = _sgu_mix(wm, ds, True)
        dlg = jnp.sum(dvn * xh, axis=0, keepdims=True)
        dlb = jnp.sum(dvn, axis=0, keepdims=True)
        dxh = dvn * lg
        dgv = rstd * (dxh - jnp.mean(dxh, axis=-1, keepdims=True) - xh * jnp.mean(dxh * xh, axis=-1, keepdims=True))
        return ([jnp.concatenate([du * gelu_grad(u_), dgv * gelu_grad(v_), dz], axis=1)], [dws, ds, dlg, dlb])
    (dproj,), (dws, dbs, dlg, dlb) = rowwise(
        f1, [rw(dpo), rw(proj, 1024, 0), rw(proj, 1024, 1), rw(proj, 1024, 2)],
        [p['d_w_s'][0], p['d_ln_g'], p['d_ln_b'], sv['bias']], [(3072, BF16)],
        [(SGU_G, SGU_T, SGU_T), (SGU_T, 1024), (1, 1024), (1, 1024)], SGU_T, 'd_mix_bwd')
    tril = np.tril(np.ones((SGU_T, SGU_T), dtype=bool))
    g['d_w_s'] = jnp.where(tril[None], dws, 0.0)[None]
    g['d_b_s'] = dbs.reshape(SGU_T, SGU_G, SGU_C).sum(-1).T[None]
    g['d_ln_g'], g['d_ln_b'] = dlg, dlb
    _dw(g, sink, 'd_w_in', sv['h'], dproj, 'd_dwin')
    dh = mm(dproj, w['d_w_in'], 'nt', 'd_dh')
    return dh, g


def _coords():
    return lax.axis_index("x"), lax.axis_index("y"), lax.axis_index("c")


class AllGather:
    def __init__(self, x):
        self.ins = [x]
        self.outs = [jax.ShapeDtypeStruct((N_DEV,) + x.shape, x.dtype)]
        self.scratch = [pltpu.SemaphoreType.DMA((7,)), pltpu.SemaphoreType.DMA((7,)), pltpu.SemaphoreType.DMA(())]

    def hooks(self, n_steps):
        return [(0, functools.partial(self.phase, 0), False), (n_steps - 1, functools.partial(self.phase, 1), True),
                (n_steps - 1, functools.partial(self.phase, 2), True)]

    @staticmethod
    def phase(which, ins, outs, scratch):
        (x_ref,), (out_ref,), (send_sems, recv_sems, local_sem) = ins, outs, scratch
        x_, y_, c_ = _coords()
        me, sibling = (x_, y_, c_), (x_, y_, 1 - c_)
        chips = [(1 - x_, y_), (x_, 1 - y_), (1 - x_, 1 - y_)]

        def slot(px, py, pc):
            return out_ref.at[4 * px + 2 * py + pc]

        def copy(k, block, to, src=None):
            return pltpu.make_async_remote_copy(src_ref=slot(*block) if src is None else src, dst_ref=slot(*block),
                                                send_sem=send_sems.at[k], recv_sem=recv_sems.at[k], device_id=to,
                                                device_id_type=MESH)

        mine = pltpu.make_async_copy(x_ref, slot(*me), local_sem)
        first = [copy(0, me, sibling, src=x_ref)]
        first += [copy(1 + j, me, (*chip, c_), src=x_ref) for j, chip in enumerate(chips)]
        passed = [copy(4 + j, (*chip, c_), sibling) for j, chip in enumerate(chips)]
        if which == 0:
            mine.start()
            for cp in first:
                cp.start()
        elif which == 1:
            for j, chip in enumerate(chips):
                copy(1 + j, (*chip, c_), me).wait_recv()
                passed[j].start()
        else:
            copy(0, sibling, me).wait_recv()
            for j, chip in enumerate(chips):
                copy(4 + j, (*chip, 1 - c_), me).wait_recv()
            for cp in first + passed:
                cp.wait_send()
            mine.wait()


class ChipExchange:
    def __init__(self, part):
        self.ins = [part]
        self.outs = [jax.ShapeDtypeStruct((3,) + part.shape[1:], part.dtype)]
        self.scratch = [pltpu.SemaphoreType.DMA((3,)), pltpu.SemaphoreType.DMA((3,))]

    def hooks(self, n_steps):
        return [(0, functools.partial(self.phase, 0), False), (n_steps - 1, functools.partial(self.phase, 1), True)]

    @staticmethod
    def phase(which, ins, outs, scratch):
        (p_ref,), (land_ref,), (send_sems, recv_sems) = ins, outs, scratch
        x_, y_, c_ = _coords()
        copies = []
        for r, (fx, fy) in enumerate([(1, 0), (0, 1), (1, 1)]):
            tx = jnp.where(fx == 1, 1 - x_, x_)
            ty = jnp.where(fy == 1, 1 - y_, y_)
            copies.append(pltpu.make_async_remote_copy(src_ref=p_ref.at[2 * tx + ty], dst_ref=land_ref.at[r],
                                                       send_sem=send_sems.at[r], recv_sem=recv_sems.at[r],
                                                       device_id=(tx, ty, c_), device_id_type=MESH))
        if which == 0:
            for cp in copies:
                cp.start()
        else:
            for cp in copies:
                cp.wait_recv()
            for cp in copies:
                cp.wait_send()


class Both:
    def __init__(self, a, b):
        self.parts = (a, b)
        self.ins, self.outs, self.scratch = a.ins + b.ins, a.outs + b.outs, a.scratch + b.scratch

    def hooks(self, n_steps):
        res, oi, oo, osc = [], 0, 0, 0
        for p in self.parts:
            sl = (slice(oi, oi + len(p.ins)), slice(oo, oo + len(p.outs)), slice(osc, osc + len(p.scratch)))
            res += [(at, functools.partial(self.sub, fn, sl), after) for at, fn, after in p.hooks(n_steps)]
            oi, oo, osc = oi + len(p.ins), oo + len(p.outs), osc + len(p.scratch)
        return res

    @staticmethod
    def sub(fn, sl, ins, outs, scratch):
        fn(ins[sl[0]], outs[sl[1]], scratch[sl[2]])


def run_comm(comm, name):
    def body(*refs):
        ci, co = len(comm.ins), len(comm.outs)
        for _, fn, _ in comm.hooks(1):
            fn(refs[:ci], refs[ci:ci + co], refs[ci + co:])

    return pl.pallas_call(body, out_shape=list(comm.outs), in_specs=[ANY] * len(comm.ins),
                          out_specs=[ANY] * len(comm.outs), scratch_shapes=list(comm.scratch), name=name)(*comm.ins)


def all_gather(x, name):
    return run_comm(AllGather(x), name)[0]


class SiblingExchange:
    def __init__(self, gfull):
        self.ins = [gfull]
        self.outs = [jax.ShapeDtypeStruct((4,) + gfull.shape[1:], gfull.dtype)]
        self.scratch = [pltpu.SemaphoreType.DMA((4,)), pltpu.SemaphoreType.DMA((4,))]

    def hooks(self, n_steps):
        return [(0, functools.partial(self.phase, 0), False), (n_steps - 1, functools.partial(self.phase, 1), True)]

    @staticmethod
    def phase(which, ins, outs, scratch):
        (g_ref,), (land_ref,), (send_sems, recv_sems) = ins, outs, scratch
        x_, y_, c_ = _coords()
        copies = [pltpu.make_async_remote_copy(src_ref=g_ref.at[2 * k + 1 - c_], dst_ref=land_ref.at[k],
                                               send_sem=send_sems.at[k], recv_sem=recv_sems.at[k],
                                               device_id=(x_, y_, 1 - c_), device_id_type=MESH) for k in range(4)]
        if which == 0:
            for cp in copies:
                cp.start()
        else:
            for cp in copies:
                cp.wait_recv()
            for cp in copies:
                cp.wait_send()


def rs_sibling(gfull, tag):
    return run_comm(SiblingExchange(gfull), 'rs_sibling_' + tag)[0]


def rs_pair_add(gfull, land, core, tag):
    _, R, C = gfull.shape
    tl = R

    def body(c_ref, g_ref, l_ref, o_ref):
        o_ref[...] = (g_ref[...].astype(F32) + l_ref[...].astype(F32)).astype(BF16)

    return pl.pallas_call(
        body, out_shape=jax.ShapeDtypeStruct((4, R, C), BF16),
        grid_spec=pltpu.PrefetchScalarGridSpec(
            num_scalar_prefetch=1, grid=(4, R // tl),
            in_specs=[pl.BlockSpec((1, tl, C), lambda k, i, c: (2 * k + c[0], i, 0)),
                      pl.BlockSpec((1, tl, C), lambda k, i, c: (k, i, 0))],
            out_specs=pl.BlockSpec((1, tl, C), lambda k, i, c: (k, i, 0))),
        compiler_params=pltpu.CompilerParams(dimension_semantics=("parallel", "parallel")), name='rs_pair_add_' + tag)(
            core, gfull, land)


def rs_chips(part, tag):
    return run_comm(ChipExchange(part), 'rs_chips_' + tag)[0]


def _adam(wv, gv, mv, vv):
    m = ADAM_B1 * mv + (1.0 - ADAM_B1) * gv
    v = ADAM_B2 * vv + (1.0 - ADAM_B2) * (gv * gv)
    m_hat = m / (1.0 - ADAM_B1 ** ADAM_STEP)
    v_hat = v / (1.0 - ADAM_B2 ** ADAM_STEP)
    delta = -ADAM_LR * (m_hat / (jnp.sqrt(v_hat) + ADAM_EPS) + ADAM_WD * wv)
    return delta, m, v


def _sum4(p_ref, l_ref):
    return ((p_ref[0].astype(F32) + l_ref[0].astype(F32)) + l_ref[1].astype(F32)) + l_ref[2].astype(F32)


def rs_rep_sum(part, land, chip):
    def body(c_ref, p_ref, l_ref, o_ref):
        o_ref[...] = _sum4(p_ref, l_ref).astype(BF16)

    return pl.pallas_call(
        body, out_shape=jax.ShapeDtypeStruct((REP_SLOT, LANES), BF16),
        grid_spec=pltpu.PrefetchScalarGridSpec(
            num_scalar_prefetch=1, grid=(1,),
            in_specs=[pl.BlockSpec((1, REP_SLOT, LANES), lambda i, c: (c[0], 0, 0)),
                      pl.BlockSpec((3, REP_SLOT, LANES), lambda i, c: (0, 0, 0))],
            out_specs=pl.BlockSpec((REP_SLOT, LANES), lambda i, c: (0, 0))),
        compiler_params=pltpu.CompilerParams(dimension_semantics=("parallel",)), name='rs_rep')(chip, part, land)


def adam_param(name, shape, off, w, m, v, chip, part=None, land=None, grep=None, fold=1):
    r, c = shape
    rp, nt, rb = _tiles((r // fold, c * fold))
    rbw = min(r, rb) if fold == 1 else r
    n_src = 2 if grep is None else 1
    ns = w.shape
    assert int(np.prod(ns[:-1])) == r and ns[-1] == c and (fold == 1 or (rb == rp and nt == 1))
    if fold > 1:
        nat_block, nat_map = ns, lambda i, cr: (0,) * len(ns)
    elif len(ns) == 2:
        nat_block, nat_map = (rbw, c), lambda i, cr: (i, 0)
    elif int(np.prod(ns[:-2])) == 1:
        nat_block, nat_map = (1,) * (len(ns) - 2) + (rbw, c), lambda i, cr: (0,) * (len(ns) - 2) + (i, 0)
    else:
        assert len(ns) == 4 and ns[0] == 1 and rbw % ns[2] == 0
        nat_block, nat_map = (1, rbw // ns[2], ns[2], c), lambda i, cr: (0, i, 0, 0)

    def body(c_ref, *refs):
        srcs = refs[:n_src * nt]
        w_ref, m_ref, v_ref, g_ref, d_ref, nm_ref, nv_ref = refs[n_src * nt:]
        if grep is None:
            tiles = [_sum4(srcs[2 * t], srcs[2 * t + 1]) for t in range(nt)]
        else:
            tiles = [srcs[t][...].astype(F32) for t in range(nt)]
        if fold > 1:
            g = jnp.concatenate([tiles[0][:, q * c:(q + 1) * c] for q in range(fold)], axis=0)
        else:
            g = (tiles[0] if nt == 1 else jnp.concatenate(tiles, axis=1))[:rbw, :c]
        g_ref[...] = g.reshape(nat_block)
        res = _adam(w_ref[...].reshape(rbw, c), g, m_ref[...].reshape(rbw, c), v_ref[...].reshape(rbw, c))
        for ref, val in zip((d_ref, nm_ref, nv_ref), res):
            ref[...] = val.reshape(nat_block)

    in_specs, args = [], []
    for t in range(nt):
        b0 = (off + t * rp) // rb
        assert (off + t * rp) % rb == 0
        if grep is None:
            in_specs += [pl.BlockSpec((1, rb, LANES), functools.partial(lambda i, cr, b0: (cr[0], b0 + i, 0), b0=b0)),
                         pl.BlockSpec((3, rb, LANES), functools.partial(lambda i, cr, b0: (0, b0 + i, 0), b0=b0))]
            args += [part, land]
        else:
            in_specs.append(pl.BlockSpec((rb, LANES), functools.partial(lambda i, cr, b0: (b0 + i, 0), b0=b0)))
            args.append(grep)
    nat = pl.BlockSpec(nat_block, nat_map)
    return pl.pallas_call(
        body, out_shape=[jax.ShapeDtypeStruct(ns, F32)] * 4,
        grid_spec=pltpu.PrefetchScalarGridSpec(num_scalar_prefetch=1, grid=(rp // rb,), in_specs=in_specs + [nat] * 3,
                                               out_specs=[nat] * 4),
        compiler_params=pltpu.CompilerParams(dimension_semantics=("parallel",)), name='adam_' + name)(
            chip, *args, w, m, v)


def adam_small(names, grep, P, M, V):
    in_specs, args, out_specs, out_shape, meta = [], [], [], [], []
    for n in names:
        s = REP_SHAPE[n]
        rp, nt, _ = _tiles(s)
        ns = P[n].shape
        for t in range(nt):
            b0 = (REP_OFF[n] + t * rp) // rp
            assert (REP_OFF[n] + t * rp) % rp == 0
            in_specs.append(pl.BlockSpec((rp, LANES), functools.partial(lambda i, b0: (b0, 0), b0=b0)))
            args.append(grep)
        nat = pl.BlockSpec(ns, functools.partial(lambda i, nd: (0,) * nd, nd=len(ns)))
        in_specs += [nat] * 3
        args += [P[n], M[n], V[n]]
        out_specs += [nat] * 4
        out_shape += [jax.ShapeDtypeStruct(ns, F32)] * 4
        meta.append((s, nt, ns))
    n_in = len(in_specs)

    def body(*refs):
        ins, outs = refs[:n_in], refs[n_in:]
        k = 0
        for p, ((r, c), nt, ns) in enumerate(meta):
            tiles = [ins[k + t][...].astype(F32) for t in range(nt)]
            w_ref, m_ref, v_ref = ins[k + nt:k + nt + 3]
            k += nt + 3
            g = (tiles[0] if nt == 1 else jnp.concatenate(tiles, axis=1))[:r, :c]
            res = (g,) + _adam(w_ref[...].reshape(r, c), g, m_ref[...].reshape(r, c), v_ref[...].reshape(r, c))
            for ref, val in zip(outs[4 * p:4 * p + 4], res):
                ref[...] = val.reshape(ns)

    res = pl.pallas_call(body, grid=(1,), in_specs=in_specs, out_specs=out_specs, out_shape=out_shape,
                         compiler_params=pltpu.CompilerParams(dimension_semantics=("arbitrary",)), name='adam_small')(*args)
    return {n: tuple(res[4 * p:4 * p + 4]) for p, n in enumerate(names)}


VM = pl.BlockSpec(memory_space=pltpu.VMEM)


def _tile_value(w, t, rp):
    r, c = w.shape
    wt = min(LANES, c - t * LANES)
    tile = w[:, t * LANES:t * LANES + wt]
    if wt < LANES:
        tile = jnp.concatenate([tile, jnp.zeros((r, LANES - wt), tile.dtype)], axis=1)
    if rp > r:
        tile = jnp.concatenate([tile, jnp.zeros((rp - r, LANES), tile.dtype)], axis=0)
    return tile


def pack_layer(layer, blocks):
    names = LAYER_PARAMS[layer]

    def body(*refs):
        tiles = []
        for ref, n in zip(refs[:-1], names):
            rp, nt, _ = _tiles(_block_shape(n))
            w = ref[...].reshape(_block_shape(n))
            tiles += [_tile_value(w, t, rp) for t in range(nt)]
        refs[-1][...] = jnp.concatenate(tiles, axis=0).astype(BF16)

    return pl.pallas_call(body, out_shape=jax.ShapeDtypeStruct((LAYER_ROWS[layer], LANES), BF16),
                          in_specs=[VM] * len(names), out_specs=VM, name='pack_' + layer)(*[blocks[n] for n in names])


def assemble(name, gathered):
    (rf, cf), ax = SHARDED[name]
    r, c = _block_shape(name)
    rp, nt, _ = _tiles((r, c))
    off = SH_OFF[name]
    out_cols = cf if ax == 0 else len(perm_index(name))

    def body(g_ref, o_ref, buf, sem):
        cp = pltpu.make_async_copy(g_ref.at[:, pl.ds(off, nt * rp), :], buf, sem)
        cp.start()
        cp.wait()
        if ax == 0:
            for j in range(N_DEV):
                o_ref[j * r:(j + 1) * r, :] = jnp.concatenate([buf[j, t * rp:(t + 1) * rp, :] for t in range(nt)], axis=1)
            return
        pieces = []
        for p in PERM[name]:
            if p[0] == 'z':
                pieces.append(jnp.zeros((r, p[1]), BF16))
                continue
            n0, w = p
            while w > 0:
                j, cb = divmod(n0, c)
                t, lane = divmod(cb, LANES)
                wl = min(w, LANES - lane, c - cb)
                pieces.append(buf[j, t * rp:t * rp + r, lane:lane + wl])
                n0, w = n0 + wl, w - wl
        o_ref[...] = jnp.concatenate(pieces, axis=1)

    return pl.pallas_call(
        body, out_shape=jax.ShapeDtypeStruct((rf, out_cols), BF16), in_specs=[ANY], out_specs=VM,
        scratch_shapes=[pltpu.VMEM((N_DEV, nt * rp, LANES), BF16), pltpu.SemaphoreType.DMA(())], name='asm_' + name)(
            gathered)


def chunk_grad(layer, name, dw, gfull):
    (rf, cf), ax = SHARDED[name]
    r, c = _block_shape(name)
    rp, nt, _ = _tiles((r, c))
    off = SH_OFF[name]
    if ax == 1:
        idx = perm_index(name) if name in PERM else np.arange(cf)
        inv = np.full(cf, -1)
        inv[idx[idx >= 0]] = np.nonzero(idx >= 0)[0]

    def body(*refs):
        dw_ref, o_ref, buf, sem = refs[0], refs[-3], refs[-2], refs[-1]
        for j in range(N_DEV):
            for t in range(nt):
                if ax == 0:
                    tile = dw_ref[j * r:(j + 1) * r, t * LANES:(t + 1) * LANES]
                else:
                    cols = inv[j * c + t * LANES:j * c + min((t + 1) * LANES, c)]
                    cuts = [0] + [k for k in range(1, len(cols)) if cols[k] != cols[k - 1] + 1] + [len(cols)]
                    pieces = [dw_ref[:, int(cols[a]):int(cols[b - 1]) + 1] for a, b in zip(cuts[:-1], cuts[1:])]
                    if len(cols) < LANES:
                        pieces.append(jnp.zeros((r, LANES - len(cols)), F32))
                    tile = pieces[0] if len(pieces) == 1 else jnp.concatenate(pieces, axis=1)
                    if rp > r:
                        tile = jnp.concatenate([tile, jnp.zeros((rp - r, LANES), F32)], axis=0)
                buf[j, t * rp:(t + 1) * rp, :] = tile.astype(BF16)
        cp = pltpu.make_async_copy(buf, o_ref.at[:, pl.ds(off, nt * rp), :], sem)
        cp.start()
        cp.wait()

    shape = jax.ShapeDtypeStruct((N_DEV, LAYER_ROWS[layer], LANES), BF16)
    scratch = [pltpu.VMEM((N_DEV, nt * rp, LANES), BF16), pltpu.SemaphoreType.DMA(())]
    if gfull is None:
        return pl.pallas_call(body, out_shape=shape, in_specs=[VM], out_specs=ANY, scratch_shapes=scratch,
                              name='chunk_' + name)(dw)
    return pl.pallas_call(body, out_shape=shape, in_specs=[VM, ANY], out_specs=ANY, scratch_shapes=scratch,
                          input_output_aliases={1: 0}, name='chunk_' + name)(dw, gfull)


class GradSink:
    def __init__(self):
        self.bufs = {}

    def put(self, name, a, b, mm_name):
        (rf, cf), ax = SHARDED[name]
        r, c = _block_shape(name)
        group = GROUP_OF[name]
        direct = ax == 0 or (c % LANES == 0 and PERM[name] == [(0, cf)])
        if direct:
            self.bufs[group] = mm_tn_chunked(a, b, mm_name, group, name, self.bufs.get(group))
        else:
            self.add(name, mm(a, b, 'tn', mm_name))

    def add(self, name, dw):
        group = GROUP_OF[name]
        self.bufs[group] = chunk_grad(group, name, dw, self.bufs.get(group))


def mm_tn_chunked(a, b, mm_name, layer, wname, gfull):
    (rf, cf), ax = SHARDED[wname]
    r, c = _block_shape(wname)
    rp, nt, _ = _tiles((r, c))
    off = SH_OFF[wname]
    K, M = a.shape
    N = b.shape[1]
    assert (M, N) == (rf, cf) and rp == r
    if ax == 0:
        tn = 4 * LANES
        grid, bspec = (N // tn,), pl.BlockSpec((K, tn), lambda g: (0, g))
        ospec = pl.BlockSpec((N_DEV, 4 * r, LANES), lambda g: (0, off // (4 * r) + g, 0))
        assert off % (4 * r) == 0 and nt % 4 == 0

        def store(res, o_ref):
            for j in range(N_DEV):
                for q in range(4):
                    o_ref[j, q * r:(q + 1) * r, :] = res[j * r:(j + 1) * r, q * LANES:(q + 1) * LANES].astype(BF16)
    else:
        tn = c
        grid, bspec = (N_DEV,), pl.BlockSpec((K, tn), lambda g: (0, g))
        ospec = pl.BlockSpec((1, nt * r, LANES), lambda g: (g, off // (nt * r), 0))
        assert off % (nt * r) == 0

        def store(res, o_ref):
            for t in range(nt):
                o_ref[0, t * r:(t + 1) * r, :] = res[:, t * LANES:(t + 1) * LANES].astype(BF16)

    def body(*refs):
        a_ref, b_ref, o_ref = refs[0], refs[1], refs[-1]
        store(lax.dot_general(a_ref[...].astype(BF16), b_ref[...].astype(BF16), _TN, preferred_element_type=F32), o_ref)

    shape = jax.ShapeDtypeStruct((N_DEV, LAYER_ROWS[layer], LANES), BF16)
    aspec = pl.BlockSpec((K, M), lambda g: (0, 0))
    params = pltpu.CompilerParams(dimension_semantics=("parallel",))
    if gfull is None:
        return pl.pallas_call(body, grid=grid, in_specs=[aspec, bspec], out_specs=ospec, out_shape=shape,
                              compiler_params=params, name=mm_name)(a, b)
    return pl.pallas_call(body, grid=grid, in_specs=[aspec, bspec, ANY], out_specs=ospec, out_shape=shape,
                          input_output_aliases={2: 0}, compiler_params=params, name=mm_name)(a, b, gfull)


def pack_rep(G):
    def body(*refs):
        tiles = []
        for ref, n in zip(refs[:-1], REP_SHAPE):
            rp, nt, _ = _tiles(_rep_packed_shape(n))
            g = ref[...]
            fold = REP_FOLD.get(n, 1)
            if fold > 1:
                rr = g.shape[0] // fold
                g = jnp.concatenate([g[q * rr:(q + 1) * rr] for q in range(fold)], axis=1)
            tiles += [_tile_value(g, t, rp) for t in range(nt)]
        rows = sum(t.shape[0] for t in tiles)
        if rows < REP_ROWS:
            tiles.append(jnp.zeros((REP_ROWS - rows, LANES), F32))
        full = jnp.concatenate(tiles, axis=0)
        for j in range(N_DEV):
            refs[-1][j] = full[j * REP_CHUNK:(j + 1) * REP_CHUNK]

    return pl.pallas_call(body, out_shape=jax.ShapeDtypeStruct((N_DEV, REP_SLOT, LANES), F32),
                          in_specs=[VM] * len(REP_SHAPE), out_specs=VM, name='pack_rep')(
                              *[G[n].reshape(s) for n, s in REP_SHAPE.items()])


def _pack_small(blocks, order, rows, width, dtype):
    flat = jnp.concatenate([blocks[n].reshape(-1).astype(dtype) for n in order])
    return jnp.pad(flat, (0, rows * width - flat.shape[0])).reshape(rows, width)


def kernel(x, pre_norm, post_norm, rel_bias, a_w_in, a_lam_re, a_lam_im, a_log_dt, a_b_re, a_b_im, a_c_re, a_c_im, a_d, a_w_glu, a_b_glu, a_w_out, b_w_in, b_sinks, b_w_out, c_w_in, c_q_norm, c_kv_norm, c_w_uq, c_w_ukv, c_w_out, d_w_in, d_ln_g, d_ln_b, d_w_s, d_b_s, d_w_out, loss_target, m_pre_norm, m_post_norm, m_rel_bias, m_a_w_in, m_a_lam_re, m_a_lam_im, m_a_log_dt, m_a_b_re, m_a_b_im, m_a_c_re, m_a_c_im, m_a_d, m_a_w_glu, m_a_b_glu, m_a_w_out, m_b_w_in, m_b_sinks, m_b_w_out, m_c_w_in, m_c_q_norm, m_c_kv_norm, m_c_w_uq, m_c_w_ukv, m_c_w_out, m_d_w_in, m_d_ln_g, m_d_ln_b, m_d_w_s, m_d_b_s, m_d_w_out, v_pre_norm, v_post_norm, v_rel_bias, v_a_w_in, v_a_lam_re, v_a_lam_im, v_a_log_dt, v_a_b_re, v_a_b_im, v_a_c_re, v_a_c_im, v_a_d, v_a_w_glu, v_a_b_glu, v_a_w_out, v_b_w_in, v_b_sinks, v_b_w_out, v_c_w_in, v_c_q_norm, v_c_kv_norm, v_c_w_uq, v_c_w_ukv, v_c_w_out, v_d_w_in, v_d_ln_g, v_d_ln_b, v_d_w_s, v_d_b_s, v_d_w_out):
    loc = locals()
    P = {n: loc[n] for n in WEIGHTS}
    M = {n: loc['m_' + n] for n in WEIGHTS}
    V = {n: loc['v_' + n] for n in WEIGHTS}
    xs = x[0]
    L = xs.shape[0]

    blocks = {n: P[n].reshape(_block_shape(n)) for n in SHARDED}
    packed = {layer: pack_layer(layer, P) for layer in LAYER_PARAMS}
    W = {}

    def assemble_layer(layer, gathered):
        for n in LAYER_PARAMS[layer]:
            if n not in SHARDED_F32:
                W[n] = assemble(n, gathered)

    Pl = dict(P)

    def arrived_first(got):
        assemble_layer('a1', got[0])
        for n in SHARDED_F32:
            c = SHARDED[n][0][1]
            bc = c // N_DEV
            Pl[n] = got[1].reshape(N_DEV, -1)[:, SMALL_OFF[n]:SMALL_OFF[n] + bc].reshape(1, c)
    cx, cy, cc = _coords()
    core = jnp.reshape(cc, (1,)).astype(jnp.int32)
    chip = jnp.reshape(2 * cx + cy, (1,)).astype(jnp.int32)

    def pair_sums(gfull, tag):
        return rs_pair_add(gfull, rs_sibling(gfull, tag), core, tag)

    fwd = [layer_a_fwd, layer_b_fwd, layer_c_fwd, layer_d_fwd]
    bwd = [layer_a_bwd, layer_b_bwd, layer_c_bwd, layer_d_bwd]
    saved = []
    xc = xs

    def fpre(x_, g_):
        return [rms_fwd(x_, g_)], []
    (h,), _ = rowwise(fpre, [rw(xc)], [P['pre_norm'][0:1]], [(D_MODEL, BF16)], [], 256, 'pre_norm0')
    for i in range(4):
        if i == 0:
            yb, sv = fwd[i](h, W, Pl, comm=Both(AllGather(packed['a2']), AllGather(packed['b'])),
                            on_carried=lambda got: assemble_layer('a2', got[0]),
                            prep_comm=Both(AllGather(packed['a1']),
                                           AllGather(_pack_small(blocks, SHARDED_F32, SMALL_ROWS, 128, F32))),
                            on_prep=arrived_first)
            assemble_layer('b', sv['carried'][1])
        elif i < 3:
            nxt = 'abcd'[i + 1]
            yb, sv = fwd[i](h, W, Pl, comm=AllGather(packed[nxt]))
            assemble_layer(nxt, sv['carried'][0])
        else:
            yb, sv = fwd[i](h, W, Pl)

        sv['x'], sv['yb'] = xc, yb
        saved.append(sv)
        if i < 3:

            def fpost(x_, y_, gpost, gpre):
                xn_ = x_ + rms_fwd(y_, gpost)
                return [xn_, rms_fwd(xn_, gpre)], []
            (xc, h), _ = rowwise(fpost, [rw(xc), rw(yb)], [P['post_norm'][i:i + 1], P['pre_norm'][i + 1:i + 2]],
                                 [(D_MODEL, F32), (D_MODEL, BF16)], [], 256, f'post_pre_norm{i}')
        else:

            def floss(x_, y_, t_, gpost):
                d = x_ + rms_fwd(y_, gpost) - t_
                return [d * (1.0 / D_MODEL)], [0.5 * jnp.sum(jnp.sum(d * d, axis=-1, keepdims=True) * (1.0 / D_MODEL),
                                                             axis=0, keepdims=True)]
            (dx,), (loss_loc,) = rowwise(floss, [rw(xc), rw(yb), rw(loss_target[0])], [P['post_norm'][i:i + 1]],
                                         [(D_MODEL, F32)], [(1, 1)], 256, 'post_norm_loss')

    G, out = {}, {}
    dpre, dpost = [None] * 4, [None] * 4

    def adam_layer(layer, part, land2):
        for n in LAYER_PARAMS[layer]:
            s = _block_shape(n)
            out[n] = adam_param(n, s, SH_OFF[n], P[n], M[n], V[n], chip, part=part, land=land2)

    def fpost_b(y_, d_, g_):
        dy, dg = rms_bwd(y_, g_, d_)
        return [dy], [dg]
    (dyb,), (dpost[3],) = rowwise(fpost_b, [rw(saved[3]['yb']), rw(dx)], [P['post_norm'][3:4]], [(D_MODEL, BF16)],
                                  [(1, D_MODEL)], 256, 'post_norm_bwd3')
    pending = None
    sink = GradSink()
    for i in reversed(range(4)):
        sv = saved[i]
        if pending is None:
            dh, g = bwd[i](dyb, W, Pl, sv, sink=sink)
        elif i > 0:
            dh, g = bwd[i](dyb, W, Pl, sv, comm=ChipExchange(pending[1]), sink=sink)
            adam_layer(pending[0], pending[1], g['carried'][0])
        else:
            early = {}

            def both():
                early['part'] = pair_sums(sink.bufs['a2'], 'a2')
                return Both(ChipExchange(pending[1]), ChipExchange(early['part']))
            dh, g = bwd[i](dyb, W, Pl, sv, comm=both, sink=sink)
            adam_layer(pending[0], pending[1], g['carried'][0])
            adam_layer('a2', early['part'], g['carried'][1])
        g.pop('carried', None)
        land_a1 = g.pop('land_a1', None)
        G.update(g)
        group = LAYER_GROUPS['abcd'[i]][0]
        for n in LAYER_PARAMS[group]:
            if n in g:
                sink.add(n, g[n])
        if i == 3:
            swap = Both(SiblingExchange(sink.bufs[group]), AllGather(jnp.broadcast_to(loss_loc, (8, LANES))))
        elif i > 0:
            swap = SiblingExchange(sink.bufs[group])
        else:
            part_a1 = rs_pair_add(sink.bufs[group], land_a1, core, group)
            swap = ChipExchange(part_a1)

        if i > 0:

            def fpre_b(x_, dh_, d_, y_, gpre, gpost):
                dxl, dg = rms_bwd(x_, gpre, dh_)
                dy, dgp = rms_bwd(y_, gpost, d_ + dxl)
                return [d_ + dxl, dy], [dg, dgp]
            (dx, dyb), (dpre[i], dpost[i - 1]), (land, *loss_all) = rowwise(
                fpre_b, [rw(sv['x']), rw(dh), rw(dx), rw(saved[i - 1]['yb'])],
                [P['pre_norm'][i:i + 1], P['post_norm'][i - 1:i]], [(D_MODEL, F32), (D_MODEL, BF16)],
                [(1, D_MODEL), (1, D_MODEL)], 256, f'pre_post_norm_bwd{i}', comm=swap)
            if loss_all:
                loss = jnp.sum(loss_all[0][:, 0, 0])
        else:

            def fpre_b0(x_, dh_, d_, g_):
                dxl, dg = rms_bwd(x_, g_, dh_)
                return [d_ + dxl], [dg]
            (dx,), (dpre[i],), (land2_a1,) = rowwise(fpre_b0, [rw(sv['x']), rw(dh), rw(dx)], [P['pre_norm'][i:i + 1]],
                                                     [(D_MODEL, F32)], [(1, D_MODEL)], 256, 'pre_norm_bwd0', comm=swap)
            adam_layer('a1', part_a1, land2_a1)
            break
        pending = (group, rs_pair_add(sink.bufs[group], land, core, group))
    G['pre_norm'] = jnp.concatenate(dpre, axis=0)
    G['post_norm'] = jnp.concatenate(dpost, axis=0)

    part = pair_sums(pack_rep(G), 'rep')
    land2 = rs_chips(part, 'rep')
    grep = all_gather(rs_rep_sum(part, land2, chip), 'ag_rep')[:, :REP_CHUNK].reshape(REP_ROWS, LANES)
    small_names = [n for n, s in REP_SHAPE.items() if s[0] <= 64]
    out.update(adam_small(small_names, grep, P, M, V))
    for n, s in REP_SHAPE.items():
        if n not in small_names:
            out[n] = adam_param(n, s, REP_OFF[n], P[n], M[n], V[n], chip, grep=grep, fold=REP_FOLD.get(n, 1))
    res = [loss, dx[None]]
    for kind in range(4):
        res += [out[n][kind].reshape(P[n].shape) for n in WEIGHTS]
    return tuple(res)
```

```python
import functools
import math

import numpy as np
import jax
import jax.numpy as jnp
from jax import lax
from jax.experimental import pallas as pl
from jax.experimental.pallas import tpu as pltpu

F32 = jnp.float32
BF16 = jnp.bfloat16
MESH = pl.DeviceIdType.MESH
ANY = pl.BlockSpec(memory_space=pl.ANY)

N_DEV = 8
D_MODEL = 1024
EPS = 1e-6
NEG_INF = -1e30
SSM_G, SSM_P, SSM_H = 64, 64, 16
SSM_T = 256
SSM_TS = 8
SSM_WC = 512
HEAD_DIM = 64
SWA_HEADS, SWA_KV = 16, 2
WINDOW = 128
REL_BUCKETS, REL_MAX_DIST = 32, 128
MLA_HEADS, MLA_NOPE, MLA_ROPE, MLA_V = 16, 64, 32, 64
MLA_Q_RANK, MLA_KV_RANK = 768, 256
ROPE_BASE = 10000.0
SGU_G, SGU_C, SGU_T = 16, 64, 128
ADAM_LR, ADAM_B1, ADAM_B2, ADAM_EPS, ADAM_WD, ADAM_STEP = 0.001, 0.9, 0.999, 1e-08, 0.01, 10

WEIGHTS = ['pre_norm', 'post_norm', 'rel_bias', 'a_w_in', 'a_lam_re', 'a_lam_im', 'a_log_dt', 'a_b_re', 'a_b_im',
           'a_c_re', 'a_c_im', 'a_d', 'a_w_glu', 'a_b_glu', 'a_w_out', 'b_w_in', 'b_sinks', 'b_w_out', 'c_w_in',
           'c_q_norm', 'c_kv_norm', 'c_w_uq', 'c_w_ukv', 'c_w_out', 'd_w_in', 'd_ln_g', 'd_ln_b', 'd_w_s', 'd_b_s',
           'd_w_out']
SHARDED = {'a_w_in': ((1024, 2048), 1), 'a_w_glu': ((1024, 1024), 0), 'a_w_out': ((1024, 1024), 0),
           'b_w_in': ((1024, 2304), 1), 'b_w_out': ((1024, 1024), 0), 'c_w_in': ((1024, 2080), 1),
           'c_q_norm': ((1, 768), 1), 'c_kv_norm': ((1, 256), 1), 'c_w_uq': ((768, 1536), 1),
           'c_w_ukv': ((256, 2048), 1), 'c_w_out': ((1024, 1024), 0), 'd_w_in': ((1024, 3072), 1),
           'd_ln_g': ((1, 1024), 1), 'd_ln_b': ((1, 1024), 1), 'd_w_out': ((1024, 1024), 0)}
SHARDED_F32 = ['c_q_norm', 'c_kv_norm', 'd_ln_g', 'd_ln_b']
REPLICATED = [n for n in WEIGHTS if n not in SHARDED]


def _cdiv(a, b):
    return -(-a // b)


def _block_shape(name):
    (r, c), ax = SHARDED[name]
    return (r // N_DEV, c) if ax == 0 else (r, c // N_DEV)


LANES = 128
LAYER_PARAMS = {'a1': ['a_w_in'], 'a2': ['a_w_glu', 'a_w_out'], 'b': ['b_w_in', 'b_w_out'],
                'c': ['c_w_in', 'c_w_uq', 'c_w_ukv', 'c_w_out', 'c_q_norm', 'c_kv_norm'],
                'd': ['d_w_in', 'd_w_out', 'd_ln_g', 'd_ln_b']}


def _tiles(shape):
    r, c = shape
    rp = max(r, 16)
    rb = 512 if rp % 512 == 0 else 256 if rp % 256 == 0 else rp
    return rp, _cdiv(c, LANES), rb


SH_OFF, LAYER_ROWS = {}, {}
for _l, _names in LAYER_PARAMS.items():
    _o = 0
    for _n in _names:
        _rp, _nt, _rb = _tiles(_block_shape(_n))
        assert _o % _rb == 0
        SH_OFF[_n] = _o
        _o += _rp * _nt
    assert _o % 16 == 0
    LAYER_ROWS[_l] = _o
GROUP_OF = {_n: _l for _l, _names in LAYER_PARAMS.items() for _n in _names}
LAYER_GROUPS = {'a': ['a1', 'a2'], 'b': ['b'], 'c': ['c'], 'd': ['d']}

REP_SHAPE = {'d_w_s': (2048, 128), 'a_b_re': (4096, 16), 'a_b_im': (4096, 16), 'a_c_re': (1024, 64),
             'a_c_im': (1024, 64), 'pre_norm': (4, 1024), 'post_norm': (4, 1024), 'a_lam_re': (64, 64),
             'a_lam_im': (64, 64), 'a_d': (1, 1024), 'a_b_glu': (1, 1024), 'rel_bias': (32, 16), 'd_b_s': (16, 128),
             'a_log_dt': (1, 64), 'b_sinks': (1, 16)}
REP_FOLD = {'a_b_re': 8, 'a_b_im': 8, 'a_c_re': 2, 'a_c_im': 2}


def _rep_packed_shape(name):
    (r, c), f = REP_SHAPE[name], REP_FOLD.get(name, 1)
    return (r // f, c * f)


REP_OFF = {}
_o = 0
for _n in REP_SHAPE:
    _rp, _nt, _rb = _tiles(_rep_packed_shape(_n))
    assert _o % _rb == 0
    REP_OFF[_n] = _o
    _o += _rp * _nt
REP_ROWS = _cdiv(_o, 16 * N_DEV) * 16 * N_DEV
REP_CHUNK = REP_ROWS // N_DEV
REP_SLOT = REP_CHUNK

PERM = {'a_w_in': [(0, 2048)], 'd_w_in': [(0, 3072)], 'b_w_in': [(1280, 1024), (0, 1280)],
        'c_w_in': [(1056, 1024), (0, 1056), ('z', 96)],
        'c_w_uq': sum([[(2 * hp * 96, 64), ((2 * hp + 1) * 96, 64), (2 * hp * 96 + 64, 32), ((2 * hp + 1) * 96 + 64, 32),
                        ('z', 64)] for hp in range(8)], []),
        'c_w_ukv': sum([[(2 * hp * 128, 64), ((2 * hp + 1) * 128, 64), (2 * hp * 128 + 64, 64),
                         ((2 * hp + 1) * 128 + 64, 64)] for hp in range(8)], [])}


def perm_index(name):
    return np.concatenate([np.full(p[1], -1) if p[0] == 'z' else np.arange(p[0], p[0] + p[1]) for p in PERM[name]])


SMALL_OFF = {}
_o = 0
for _n in SHARDED_F32:
    SMALL_OFF[_n] = _o
    _o += int(np.prod(_block_shape(_n)))
SMALL_ROWS = _cdiv(_o, 128 * 8) * 8


def _pick(n, cands):
    for c in cands:
        if n % c == 0:
            return c
    return n


def mm(a, b, mode, name, out_dtype=F32, comm=None):
    if mode == 'nn':
        (M, K), (K2, N) = a.shape, b.shape
    elif mode == 'nt':
        (M, K), (N, K2) = a.shape, b.shape
    else:
        (K, M), (K2, N) = a.shape, b.shape
    assert K == K2, (name, a.shape, b.shape)
    tm = _pick(M, (1024, 768, 512, 256, 128))
    tn = _pick(N, (512, 384, 256))
    dims = {'nn': ((1,), (0,)), 'nt': ((1,), (1,)), 'tn': ((0,), (0,))}[mode]

    def body(a_ref, b_ref, o_ref):
        o_ref[...] = lax.dot_general(a_ref[...].astype(BF16), b_ref[...].astype(BF16), (dims, ((), ())),
                                     preferred_element_type=F32).astype(out_dtype)

    a_spec = pl.BlockSpec((K, tm), lambda i, j: (0, i)) if mode == 'tn' else pl.BlockSpec((tm, K), lambda i, j: (i, 0))
    b_spec = pl.BlockSpec((tn, K), lambda i, j: (j, 0)) if mode == 'nt' else pl.BlockSpec((K, tn), lambda i, j: (0, j))
    res = carried(body, comm, grid=(M // tm, N // tn), in_specs=[a_spec, b_spec],
                  out_specs=pl.BlockSpec((tm, tn), lambda i, j: (i, j)), out_shape=jax.ShapeDtypeStruct((M, N), out_dtype),
                  semantics=("parallel", "parallel"), name=name)(a, b)
    return res[0] if comm is None else res


def rw(arr, width=None, cb=0):
    return (arr, arr.shape[1] if width is None else width, cb)


def rowwise(fn, rows, consts, outs, accs, tl, name, n_steps=None, comm=None):
    if n_steps is None:
        n_steps = [r[0].shape[0] for r in rows if not isinstance(r[1], pl.BlockSpec)][0] // tl
    L = n_steps * tl
    nr, nc, no, na = len(rows), len(consts), len(outs), len(accs)
    in_specs, args = [], []
    for r in rows:
        if isinstance(r[1], pl.BlockSpec):
            in_specs.append(r[1])
        else:
            in_specs.append(pl.BlockSpec((tl, r[1]), functools.partial(lambda i, cb: (i, cb), cb=r[2])))
        args.append(r[0])
    for c in consts:
        in_specs.append(pl.BlockSpec(c.shape, functools.partial(lambda i, nd: (0,) * nd, nd=c.ndim)))
        args.append(c)
    out_specs = [pl.BlockSpec((tl, w), lambda i: (i, 0)) for w, _ in outs]
    out_shape = [jax.ShapeDtypeStruct((L, w), dt) for w, dt in outs]
    for s in accs:
        out_specs.append(pl.BlockSpec(s, functools.partial(lambda i, nd: (0,) * nd, nd=len(s))))
        out_shape.append(jax.ShapeDtypeStruct(s, F32))

    def body(*refs):
        ins = [r[...] for r in refs[:nr + nc]]
        o_refs = refs[nr + nc:nr + nc + no]
        a_refs = refs[nr + nc + no:]
        o_vals, a_vals = fn(*ins)
        for ref, val in zip(o_refs, o_vals):
            ref[...] = val.astype(ref.dtype)
        if na:
            @pl.when(pl.program_id(0) == 0)
            def _():
                for ref in a_refs:
                    ref[...] = jnp.zeros_like(ref)
            for ref, val in zip(a_refs, a_vals):
                ref[...] += val

    res, carried_out = carried(body, comm, grid=(n_steps,), in_specs=in_specs, out_specs=out_specs, out_shape=out_shape,
                               name=name, semantics=("arbitrary",))(*args)
    if comm is None:
        return res[:no], res[no:]
    return res[:no], res[no:], carried_out


def carried(body, comm, *, grid, in_specs, out_specs, out_shape, name, semantics, scratch_shapes=()):
    single = not isinstance(out_shape, (list, tuple))
    o_specs = [out_specs] if single else list(out_specs)
    o_shape = [out_shape] if single else list(out_shape)
    if comm is None:
        call = pl.pallas_call(body, grid=grid, in_specs=in_specs, out_specs=out_specs, out_shape=out_shape,
                              scratch_shapes=list(scratch_shapes),
                              compiler_params=pltpu.CompilerParams(dimension_semantics=semantics), name=name)
        return lambda *args: (call(*args), None)
    n_in, n_out, n_sc = len(in_specs), len(o_specs), len(scratch_shapes)
    ci, co = len(comm.ins), len(comm.outs)
    n_steps = int(np.prod(grid))
    hooks = comm.hooks(n_steps)

    def wrapped(*refs):
        ins, cins = refs[:n_in], refs[n_in:n_in + ci]
        outs, couts = refs[n_in + ci:n_in + ci + n_out], refs[n_in + ci + n_out:n_in + ci + n_out + co]
        sc, csc = refs[n_in + ci + n_out + co:n_in + ci + n_out + co + n_sc], refs[n_in + ci + n_out + co + n_sc:]
        step = pl.program_id(0)
        for ax in range(1, len(grid)):
            step = step * grid[ax] + pl.program_id(ax)
        for at, fn, after in hooks:
            if not after:
                pl.when(step == at)(functools.partial(fn, cins, couts, csc))
        body(*ins, *outs, *sc)
        for at, fn, after in hooks:
            if after:
                pl.when(step == at)(functools.partial(fn, cins, couts, csc))

    call = pl.pallas_call(wrapped, grid=grid, in_specs=list(in_specs) + [ANY] * ci, out_specs=o_specs + [ANY] * co,
                          out_shape=o_shape + list(comm.outs), scratch_shapes=list(scratch_shapes) + list(comm.scratch),
                          compiler_params=pltpu.CompilerParams(dimension_semantics=("arbitrary",) * len(grid)), name=name)

    def run(*args):
        res = call(*args, *comm.ins)
        return (res[0] if single else res[:n_out]), res[n_out:]
    return run


_K0 = math.sqrt(2.0 / math.pi)
_K1 = 0.044715


def gelu(x):
    return x * (0.5 * (1.0 + jnp.tanh(_K0 * (x + _K1 * (x * x * x)))))


def gelu_grad(x):
    t = jnp.tanh(_K0 * (x + _K1 * (x * x * x)))
    return 0.5 * (1.0 + t) + 0.5 * x * (1.0 - t * t) * (_K0 * (1.0 + 3.0 * _K1 * x * x))


def sigmoid(x):
    return 1.0 / (1.0 + jnp.exp(-x))


def silu(z):
    return z * sigmoid(z)


def silu_grad(z):
    s = sigmoid(z)
    return s * (1.0 + z * (1.0 - s))


def rms_fwd(x, g):
    r = lax.rsqrt(jnp.mean(x * x, axis=-1, keepdims=True) + EPS)
    return x * r * g


def rms_bwd(x, g, dy):
    r = lax.rsqrt(jnp.mean(x * x, axis=-1, keepdims=True) + EPS)
    xh = x * r
    dg = jnp.sum(dy * xh, axis=0, keepdims=True)
    dxh = dy * g
    dx = r * (dxh - xh * jnp.mean(dxh * xh, axis=-1, keepdims=True))
    return dx, dg


def _scan_chunk(a_r, a_i, pr_ref, pi_ref, cr, ci, T, reverse):
    ts = min(SSM_TS, T)
    sgn = -1.0 if reverse else 1.0
    row = lax.broadcasted_iota(jnp.int32, (ts, a_r.shape[1]), 0)
    pw = (lambda e: T - e) if reverse else (lambda e: e - 1)
    if reverse:
        wr_c, wi_c = pr_ref[T - ts:T, :], sgn * pi_ref[T - ts:T, :]
    else:
        wr_c, wi_c = pr_ref[0:ts, :], sgn * pi_ref[0:ts, :]
    c_r, c_i = cr[...], ci[...]
    outs = []
    subs = range(T // ts)
    for sub in (reversed(subs) if reverse else subs):
        v_r, v_i = a_r[sub * ts:(sub + 1) * ts], a_i[sub * ts:(sub + 1) * ts]
        d = 1
        while d < ts:
            wr = pr_ref[pw(d):pw(d) + 1, :]
            wi = sgn * pi_ref[pw(d):pw(d) + 1, :]
            if reverse:
                yr, yi, keep = pltpu.roll(v_r, ts - d, 0), pltpu.roll(v_i, ts - d, 0), row < ts - d
            else:
                yr, yi, keep = pltpu.roll(v_r, d, 0), pltpu.roll(v_i, d, 0), row >= d
            v_r, v_i = (v_r + jnp.where(keep, wr * yr - wi * yi, 0.0), v_i + jnp.where(keep, wr * yi + wi * yr, 0.0))
            d *= 2
        v_r, v_i = v_r + (wr_c * c_r - wi_c * c_i), v_i + (wr_c * c_i + wi_c * c_r)
        k = 0 if reverse else ts - 1
        c_r, c_i = v_r[k:k + 1, :], v_i[k:k + 1, :]
        outs.append((v_r, v_i))
    if reverse:
        outs = outs[::-1]
    cr[...] = c_r
    ci[...] = c_i
    return jnp.concatenate([o[0] for o in outs], axis=0), jnp.concatenate([o[1] for o in outs], axis=0)


_NT = (((1,), (1,)), ((), ()))
_TN = (((0,), (0,)), ((), ()))


def s5_fwd(proj, d_skip, Bre, Bim, Cre, Cim, pr, pi, comm=None):
    L = proj.shape[0]
    T, WC = min(SSM_T, L), SSM_WC
    nT = L // T

    def body(u_ref, d_ref, bre_ref, bim_ref, cre_ref, cim_ref, pr_ref, pi_ref, y_ref, yg_ref, sr_ref, si_ref, cr, ci):
        @pl.when(pl.program_id(1) == 0)
        def _():
            cr[...] = jnp.zeros_like(cr)
            ci[...] = jnp.zeros_like(ci)

        u = u_ref[...]
        ub = u.astype(BF16)
        a_r = lax.dot_general(ub, bre_ref[0].astype(BF16), _NT, preferred_element_type=F32)
        a_i = lax.dot_general(ub, bim_ref[0].astype(BF16), _NT, preferred_element_type=F32)
        a_r, a_i = _scan_chunk(a_r, a_i, pr_ref, pi_ref, cr, ci, T, False)
        sr_ref[...] = a_r
        si_ref[...] = a_i
        y = (lax.dot_general(a_r.astype(BF16), cre_ref[0].astype(BF16), _NT, preferred_element_type=F32)
             + lax.dot_general(a_i.astype(BF16), cim_ref[0].astype(BF16), _NT, preferred_element_type=F32)
             + d_ref[...] * u)
        y_ref[...] = y
        yg_ref[...] = gelu(y)

    uspec = pl.BlockSpec((T, 128), lambda k, i: (i, k))
    sspec = pl.BlockSpec((T, WC), lambda k, i: (i, k))
    return carried(
        body, comm, grid=(8, nT),
        in_specs=[uspec, pl.BlockSpec((1, 128), lambda k, i: (0, k)),
                  pl.BlockSpec((1, WC, 128), lambda k, i: (k, 0, 0)), pl.BlockSpec((1, WC, 128), lambda k, i: (k, 0, 0)),
                  pl.BlockSpec((1, 128, WC), lambda k, i: (k, 0, 0)), pl.BlockSpec((1, 128, WC), lambda k, i: (k, 0, 0)),
                  pl.BlockSpec((T, WC), lambda k, i: (0, k)), pl.BlockSpec((T, WC), lambda k, i: (0, k))],
        out_specs=[uspec, uspec, sspec, sspec],
        out_shape=[jax.ShapeDtypeStruct((L, 1024), F32)] * 2 + [jax.ShapeDtypeStruct((L, 8 * WC), F32)] * 2,
        scratch_shapes=[pltpu.VMEM((1, WC), F32), pltpu.VMEM((1, WC), F32)],
        semantics=("parallel", "arbitrary"), name='a_ssm')(proj, d_skip, Bre, Bim, Cre, Cim, pr, pi)


def s5_bwd(proj, dyg1, dyg2, y, d_skip, s_re, s_im, Bre, Bim, Cre, Cim, prr, pir, comm=None):
    L = proj.shape[0]
    T, WC = min(SSM_T, L), SSM_WC
    nT = L // T

    def body(u_ref, g1_ref, g2_ref, y_ref, d_ref, sr_ref, si_ref, spr_ref, spi_ref, bre_ref, bim_ref, cre_ref, cim_ref,
             pr_ref, pi_ref, du_ref, dd_ref, dbre_ref, dbim_ref, dcre_ref, dcim_ref, dar_ref, dai_ref, cr, ci):
        i = pl.program_id(1)

        @pl.when(i == 0)
        def _():
            for ref in (cr, ci, dd_ref, dbre_ref, dbim_ref, dcre_ref, dcim_ref, dar_ref, dai_ref):
                ref[...] = jnp.zeros_like(ref)

        u = u_ref[...]
        dy = (g1_ref[...] + g2_ref[...]) * gelu_grad(y_ref[...])
        dd_ref[...] += jnp.sum(dy * u, axis=0, keepdims=True)
        dyb, ub = dy.astype(BF16), u.astype(BF16)
        bre, bim, cre, cim = (r[0].astype(BF16) for r in (bre_ref, bim_ref, cre_ref, cim_ref))
        g_r = jnp.dot(dyb, cre, preferred_element_type=F32)
        g_i = jnp.dot(dyb, cim, preferred_element_type=F32)
        g_r, g_i = _scan_chunk(g_r, g_i, pr_ref, pi_ref, cr, ci, T, True)
        s_r, s_i = sr_ref[...], si_ref[...]
        row = lax.broadcasted_iota(jnp.int32, (T, WC), 0)
        first = (nT - 1 - i) == 0
        sp_r = jnp.where(row == 0, jnp.where(first, 0.0, spr_ref[7:8, :]), pltpu.roll(s_r, 1, 0))
        sp_i = jnp.where(row == 0, jnp.where(first, 0.0, spi_ref[7:8, :]), pltpu.roll(s_i, 1, 0))
        dar_ref[...] += jnp.sum(g_r * sp_r + g_i * sp_i, axis=0, keepdims=True)
        dai_ref[...] += jnp.sum(g_i * sp_r - g_r * sp_i, axis=0, keepdims=True)
        grb, gib = g_r.astype(BF16), g_i.astype(BF16)
        dcre_ref[0] += lax.dot_general(dyb, s_r.astype(BF16), _TN, preferred_element_type=F32)
        dcim_ref[0] += lax.dot_general(dyb, s_i.astype(BF16), _TN, preferred_element_type=F32)
        dbre_ref[0] += lax.dot_general(grb, ub, _TN, preferred_element_type=F32)
        dbim_ref[0] += lax.dot_general(gib, ub, _TN, preferred_element_type=F32)
        du_ref[...] = (dy * d_ref[...] + jnp.dot(grb, bre, preferred_element_type=F32)
                       + jnp.dot(gib, bim, preferred_element_type=F32))

    uspec = pl.BlockSpec((T, 128), lambda k, i: (nT - 1 - i, k))
    sspec = pl.BlockSpec((T, WC), lambda k, i: (nT - 1 - i, k))
    pspec = pl.BlockSpec((8, WC), lambda k, i: (jnp.maximum((nT - 1 - i) * (T // 8) - 1, 0), k))
    tab = pl.BlockSpec((T, WC), lambda k, i: (0, k))
    bspec = pl.BlockSpec((1, WC, 128), lambda k, i: (k, 0, 0))
    cspec = pl.BlockSpec((1, 128, WC), lambda k, i: (k, 0, 0))
    return carried(
        body, comm, grid=(8, nT),
        in_specs=[uspec, uspec, uspec, uspec, pl.BlockSpec((1, 128), lambda k, i: (0, k)), sspec, sspec, pspec, pspec,
                  bspec, bspec, cspec, cspec, tab, tab],
        out_specs=[uspec, pl.BlockSpec((1, 128), lambda k, i: (0, k)), bspec, bspec, cspec, cspec,
                   pl.BlockSpec((1, WC), lambda k, i: (0, k)), pl.BlockSpec((1, WC), lambda k, i: (0, k))],
        out_shape=[jax.ShapeDtypeStruct((L, 1024), F32), jax.ShapeDtypeStruct((1, 1024), F32),
                   jax.ShapeDtypeStruct((8, WC, 128), F32), jax.ShapeDtypeStruct((8, WC, 128), F32),
                   jax.ShapeDtypeStruct((8, 128, WC), F32), jax.ShapeDtypeStruct((8, 128, WC), F32),
                   jax.ShapeDtypeStruct((1, 8 * WC), F32), jax.ShapeDtypeStruct((1, 8 * WC), F32)],
        scratch_shapes=[pltpu.VMEM((1, WC), F32), pltpu.VMEM((1, WC), F32)],
        semantics=("parallel", "arbitrary"), name='a_ssm_bwd')(
            proj, dyg1, dyg2, y, d_skip, s_re, s_im, s_re, s_im, Bre, Bim, Cre, Cim, prr, pir)


def s5_discretize(lam_re, lam_im, log_dt, b_re, b_im):
    dt = jnp.exp(log_dt)[:, None]
    mag = jnp.exp(lam_re * dt)
    ab_re = mag * jnp.cos(lam_im * dt)
    ab_im = mag * jnp.sin(lam_im * dt)
    den = lam_re * lam_re + lam_im * lam_im
    nr = ab_re - 1.0
    f_re = (nr * lam_re + ab_im * lam_im) / den
    f_im = (ab_im * lam_re - nr * lam_im) / den
    bb_re = f_re[..., None] * b_re - f_im[..., None] * b_im
    bb_im = f_re[..., None] * b_im + f_im[..., None] * b_re
    return ab_re, ab_im, bb_re, bb_im


def s5_prep(bb_re, bb_im, c_re, c_im, ar, ai, T, comm=None):
    W = ar.shape[1]

    def body(bbr_ref, bbi_ref, cre_ref, cim_ref, ar_ref, ai_ref, btr_ref, bti_ref, ctr_ref, cti_ref, fr_ref, fi_ref,
             rr_ref, ri_ref):
        for ref in (btr_ref, bti_ref, ctr_ref, cti_ref):
            ref[...] = jnp.zeros_like(ref)
        for g in range(8):
            rows, cols = slice(g * SSM_P, (g + 1) * SSM_P), slice(g * SSM_H, (g + 1) * SSM_H)
            btr_ref[0, rows, cols] = bbr_ref[g]
            bti_ref[0, rows, cols] = bbi_ref[g]
            ctr_ref[0, cols, rows] = cre_ref[g]
            cti_ref[0, cols, rows] = -cim_ref[g]
        fr_ref[0:1, :] = ar_ref[...]
        fi_ref[0:1, :] = ai_ref[...]
        rr_ref[T - 1:T, :] = ar_ref[...]
        ri_ref[T - 1:T, :] = ai_ref[...]
        n = 1
        while n < T:
            cr, ci = fr_ref[0:n, :], fi_ref[0:n, :]
            lr, li = fr_ref[n - 1:n, :], fi_ref[n - 1:n, :]
            fr_ref[n:2 * n, :] = cr * lr - ci * li
            fi_ref[n:2 * n, :] = cr * li + ci * lr
            cr, ci = rr_ref[T - n:T, :], ri_ref[T - n:T, :]
            rr_ref[T - 2 * n:T - n, :] = cr * lr - ci * li
            ri_ref[T - 2 * n:T - n, :] = cr * li + ci * lr
            n *= 2

    spec = pl.BlockSpec((T, SSM_WC), lambda j: (0, j))
    aspec = pl.BlockSpec((1, SSM_WC), lambda j: (0, j))
    bspec, cspec = pl.BlockSpec((8, SSM_P, SSM_H), lambda j: (j, 0, 0)), pl.BlockSpec((8, SSM_H, SSM_P), lambda j: (j, 0, 0))
    btspec = pl.BlockSpec((1, SSM_WC, 128), lambda j: (j, 0, 0))
    ctspec = pl.BlockSpec((1, 128, SSM_WC), lambda j: (j, 0, 0))
    return carried(
        body, comm, grid=(W // SSM_WC,), in_specs=[bspec, bspec, cspec, cspec, aspec, aspec],
        out_specs=[btspec, btspec, ctspec, ctspec] + [spec] * 4,
        out_shape=[jax.ShapeDtypeStruct((8, SSM_WC, 128), F32)] * 2 + [jax.ShapeDtypeStruct((8, 128, SSM_WC), F32)] * 2
        + [jax.ShapeDtypeStruct((T, W), F32)] * 4,
        semantics=("parallel",), name='a_prep')(bb_re, bb_im, c_re, c_im, ar, ai)


def s5_untile(dbtr, dbti, dctr, dcti):
    def body(dbtr_ref, dbti_ref, dctr_ref, dcti_ref, br_ref, bi_ref, cr_ref, ci_ref):
        for g in range(8):
            rows, cols = slice(g * SSM_P, (g + 1) * SSM_P), slice(g * SSM_H, (g + 1) * SSM_H)
            br_ref[g] = dbtr_ref[0, rows, cols]
            bi_ref[g] = dbti_ref[0, rows, cols]
            cr_ref[g] = dctr_ref[0, cols, rows]
            ci_ref[g] = -dcti_ref[0, cols, rows]

    bspec, cspec = pl.BlockSpec((8, SSM_P, SSM_H), lambda j: (j, 0, 0)), pl.BlockSpec((8, SSM_H, SSM_P), lambda j: (j, 0, 0))
    btspec = pl.BlockSpec((1, SSM_WC, 128), lambda j: (j, 0, 0))
    ctspec = pl.BlockSpec((1, 128, SSM_WC), lambda j: (j, 0, 0))
    return pl.pallas_call(
        body, grid=(8,), in_specs=[btspec, btspec, ctspec, ctspec], out_specs=[bspec, bspec, cspec, cspec],
        out_shape=[jax.ShapeDtypeStruct((SSM_G, SSM_P, SSM_H), F32)] * 2 + [jax.ShapeDtypeStruct((SSM_G, SSM_H, SSM_P), F32)] * 2,
        compiler_params=pltpu.CompilerParams(dimension_semantics=("parallel",)), name='a_untile')(dbtr, dbti, dctr, dcti)


def layer_a_fwd(h, w, p, comm=None, on_carried=None, prep_comm=None, on_prep=None):
    L = h.shape[0]
    disc = lambda *a: s5_discretize(*a)
    (ab_re, ab_im, bb_re, bb_im), disc_vjp = jax.vjp(disc, p['a_lam_re'][0], p['a_lam_im'][0], p['a_log_dt'][0],
                                                     p['a_b_re'][0], p['a_b_im'][0])
    T = min(SSM_T, L)
    (Bre, Bim, Cre, Cim, pr, pi, prr, pir), prepped = s5_prep(bb_re, bb_im, p['a_c_re'][0], p['a_c_im'][0],
                                                              ab_re.reshape(1, -1), ab_im.reshape(1, -1), T,
                                                              comm=prep_comm)
    if on_prep is not None:
        on_prep(prepped)
    proj = mm(h, w['a_w_in'], 'nn', 'a_proj')
    (y, yg, s_re, s_im), carried_out = s5_fwd(proj, p['a_d'], Bre, Bim, Cre, Cim, pr, pi, comm=comm)
    if on_carried is not None:
        on_carried(carried_out)
    gl = mm(yg, w['a_w_glu'], 'nn', 'a_glu')

    def f2(yg_, gl_, z, bg):
        return [yg_ * sigmoid(gl_ + bg) * silu(z)], []
    (po,), _ = rowwise(f2, [rw(yg), rw(gl), rw(proj, 1024, 1)], [p['a_b_glu']], [(1024, BF16)], [], 256, 'a_gate')
    yb = mm(po, w['a_w_out'], 'nn', 'a_out')
    saved = dict(carried=carried_out, h=h, proj=proj, disc_vjp=disc_vjp, Bre=Bre, Bim=Bim, Cre=Cre, Cim=Cim, prr=prr, pir=pir, s_re=s_re,
                 s_im=s_im, y=y, yg=yg, gl=gl, po=po)
    return yb, saved


def _dw(g, sink, name, a, b, mm_name):
    if sink is None:
        g[name] = mm(a, b, 'tn', mm_name)
    else:
        sink.put(name, a, b, mm_name)


def layer_a_bwd(dyb, w, p, sv, comm=None, sink=None):
    g = {}
    dpo = mm(dyb, w['a_w_out'], 'nt', 'a_dpo')
    _dw(g, sink, 'a_w_out', sv['po'], dyb, 'a_dwout')
    proj = sv['proj']

    def f1(dpo_, yg, gl, z, bg):
        sg = sigmoid(gl + bg)
        sz = silu(z)
        dm = dpo_ * sz
        dz = dpo_ * (yg * sg) * silu_grad(z)
        dgl = dm * yg * sg * (1.0 - sg)
        return [dz, dm * sg, dgl], [jnp.sum(dgl, axis=0, keepdims=True)]
    (dz, dyg1, dgl), (db_glu,) = rowwise(f1, [rw(dpo), rw(sv['yg']), rw(sv['gl']), rw(proj, 1024, 1)], [p['a_b_glu']],
                                          [(1024, F32), (1024, F32), (1024, BF16)], [(1, 1024)], 256, 'a_gate_bwd')
    g['a_b_glu'] = db_glu
    _dw(g, sink, 'a_w_glu', sv['yg'], dgl, 'a_dwglu')
    dyg2 = mm(dgl, w['a_w_glu'], 'nt', 'a_dyg2')

    if callable(comm):
        comm = comm()
    (du, dd, dBre, dBim, dCre, dCim, da_re, da_im), g['carried'] = s5_bwd(
        proj, dyg1, dyg2, sv['y'], p['a_d'], sv['s_re'], sv['s_im'], sv['Bre'], sv['Bim'], sv['Cre'], sv['Cim'],
        sv['prr'], sv['pir'], comm=comm)
    g['a_d'] = dd

    def f3(du_, dz_):
        return [jnp.concatenate([du_, dz_], axis=1)], []
    (dproj,), _ = rowwise(f3, [rw(du), rw(dz)], [], [(2048, BF16)], [], 256, 'a_dproj')
    dbb_re, dbb_im, dc_re, dc_im = s5_untile(dBre, dBim, dCre, dCim)
    dlr, dli, dldt, dbr, dbi = sv['disc_vjp']((da_re.reshape(SSM_G, SSM_P), da_im.reshape(SSM_G, SSM_P), dbb_re, dbb_im))
    g['a_lam_re'], g['a_lam_im'], g['a_log_dt'] = dlr[None], dli[None], dldt[None]
    g['a_b_re'], g['a_b_im'] = dbr[None], dbi[None]
    g['a_c_re'], g['a_c_im'] = dc_re[None], dc_im[None]
    _dw(g, sink, 'a_w_in', sv['h'], dproj, 'a_dwin')
    if sink is None:
        dh = mm(dproj, w['a_w_in'], 'nt', 'a_dh')
    else:
        dh, (g['land_a1'],) = mm(dproj, w['a_w_in'], 'nt', 'a_dh', comm=SiblingExchange(sink.bufs['a1']))
    return dh, g


def _t5_bucket_np():
    qi = np.arange(WINDOW)[:, None]
    kj = np.arange(2 * WINDOW)[None, :]
    dist = np.maximum(qi + WINDOW - kj, 0)
    max_exact = REL_BUCKETS // 2
    dist_f = np.maximum(dist, 1).astype(np.float32)
    large = max_exact + (np.log(dist_f / np.float32(max_exact)) / np.float32(math.log(REL_MAX_DIST / max_exact))
                         * np.float32(REL_BUCKETS - max_exact)).astype(np.int32)
    large = np.minimum(large, REL_BUCKETS - 1)
    return np.where(dist < max_exact, dist, large).astype(np.int32)


SWA_GRP = SWA_HEADS // SWA_KV


def _swa_kv(kvp, kvc, kvh):
    kb = jnp.concatenate([kvp[:, kvh * 64:(kvh + 1) * 64], kvc[:, kvh * 64:(kvh + 1) * 64]], 0).astype(BF16)
    vb = jnp.concatenate([kvp[:, 128 + kvh * 64:128 + (kvh + 1) * 64], kvc[:, 128 + kvh * 64:128 + (kvh + 1) * 64]],
                         0).astype(BF16)
    return kb, vb


def _swa_stack(x, kvh):
    return jnp.concatenate([x[:, (kvh * SWA_GRP + g) * 64:(kvh * SWA_GRP + g + 1) * 64] for g in range(SWA_GRP)],
                           axis=0).astype(BF16)


def _swa_key_major(bias):
    return bias.reshape(SWA_KV, SWA_GRP, WINDOW, 2 * WINDOW).transpose(0, 3, 1, 2).reshape(SWA_KV, 2 * WINDOW,
                                                                                           SWA_GRP * WINDOW)


def _swa_probs(q8, kb, bias_h, sink_ref, kvh, valid):
    s = lax.dot_general(kb, q8, _NT, preferred_element_type=F32) * (HEAD_DIM ** -0.5)
    s = jnp.where(valid, s + bias_h, NEG_INF)
    sink = jnp.concatenate([jnp.broadcast_to(sink_ref[0:1, kvh * SWA_GRP + g:kvh * SWA_GRP + g + 1], (1, WINDOW))
                            for g in range(SWA_GRP)], axis=1)
    m = jnp.maximum(jnp.max(s, axis=0, keepdims=True), sink)
    e = jnp.exp(s - m)
    es = jnp.exp(sink - m)
    inv = 1.0 / (jnp.sum(e, axis=0, keepdims=True) + es)
    return e * inv, es * inv


def _swa_valid(n):
    kj = lax.broadcasted_iota(jnp.int32, (2 * WINDOW, SWA_GRP * WINDOW), 0)
    qi = lax.broadcasted_iota(jnp.int32, (2 * WINDOW, SWA_GRP * WINDOW), 1) & (WINDOW - 1)
    dist = qi + WINDOW - kj
    return (dist >= 0) & (dist < WINDOW) & ((kj >= WINDOW) | (n > 0))


def swa_fwd(proj, bias, sinks, comm=None):
    L = proj.shape[0]

    def body(z_ref, q_ref, kvc_ref, kvp_ref, bias_ref, sink_ref, o_ref, po_ref):
        valid = _swa_valid(pl.program_id(0))
        q, kvc, kvp = q_ref[...], kvc_ref[...], kvp_ref[...]
        outs = []
        for kvh in range(SWA_KV):
            kb, vb = _swa_kv(kvp, kvc, kvh)
            p, _ = _swa_probs(_swa_stack(q, kvh), kb, bias_ref[kvh], sink_ref, kvh, valid)
            o8 = lax.dot_general(p.astype(BF16), vb, _TN, preferred_element_type=F32)
            outs += [o8[g * WINDOW:(g + 1) * WINDOW] for g in range(SWA_GRP)]
        o = jnp.concatenate(outs, axis=1)
        o_ref[...] = o
        po_ref[...] = (o * silu(z_ref[...])).astype(po_ref.dtype)

    return carried(
        body, comm, grid=(L // WINDOW,),
        in_specs=[pl.BlockSpec((WINDOW, 1024), lambda n: (n, 0)), pl.BlockSpec((WINDOW, 1024), lambda n: (n, 1)),
                  pl.BlockSpec((WINDOW, 256), lambda n: (n, 8)),
                  pl.BlockSpec((WINDOW, 256), lambda n: (jnp.maximum(n - 1, 0), 8)),
                  pl.BlockSpec((SWA_KV, 2 * WINDOW, SWA_GRP * WINDOW), lambda n: (0, 0, 0)),
                  pl.BlockSpec((1, SWA_HEADS), lambda n: (0, 0))],
        out_specs=[pl.BlockSpec((WINDOW, 1024), lambda n: (n, 0))] * 2,
        out_shape=[jax.ShapeDtypeStruct((L, 1024), F32), jax.ShapeDtypeStruct((L, 1024), BF16)],
        semantics=("parallel",), name='b_attn')(proj, proj, proj, proj, _swa_key_major(bias), sinks)


def swa_bwd(proj, do, bias, sinks, comm=None):
    L = proj.shape[0]

    def body(q_ref, kvc_ref, kvp_ref, do_ref, bias_ref, sink_ref, dq_ref, dkv_ref, dbias_ref, dsink_ref):
        n = pl.program_id(0)

        @pl.when(n == 0)
        def _():
            dkv_ref[...] = jnp.zeros_like(dkv_ref)
            dbias_ref[...] = jnp.zeros_like(dbias_ref)
            dsink_ref[...] = jnp.zeros_like(dsink_ref)

        valid = _swa_valid(n)
        q, kvc, kvp, do_ = q_ref[...], kvc_ref[...], kvp_ref[...], do_ref[...]
        dqs, dks, dvs, dsk = [], [], [], []
        for kvh in range(SWA_KV):
            kb, vb = _swa_kv(kvp, kvc, kvh)
            q8, do8 = _swa_stack(q, kvh), _swa_stack(do_, kvh)
            p, ps = _swa_probs(q8, kb, bias_ref[kvh], sink_ref, kvh, valid)
            dp = lax.dot_general(vb, do8, _NT, preferred_element_type=F32)
            delta = jnp.sum(p * dp, axis=0, keepdims=True)
            ds = p * (dp - delta)
            col = -ps * delta
            dsk += [jnp.sum(col[:, g * WINDOW:(g + 1) * WINDOW], axis=1, keepdims=True) for g in range(SWA_GRP)]
            dbias_ref[kvh] += ds
            dsb = (ds * (HEAD_DIM ** -0.5)).astype(BF16)
            dq8 = lax.dot_general(dsb, kb, _TN, preferred_element_type=F32)
            dqs += [dq8[g * WINDOW:(g + 1) * WINDOW] for g in range(SWA_GRP)]
            dks.append(jnp.dot(dsb, q8, preferred_element_type=F32))
            dvs.append(jnp.dot(p.astype(BF16), do8, preferred_element_type=F32))
        dq_ref[...] = jnp.concatenate(dqs, axis=1)
        dsink_ref[...] += jnp.concatenate(dsk, axis=1)
        both = jnp.concatenate(dks + dvs, axis=1)
        r_cur = pl.multiple_of(n * WINDOW, WINDOW)
        r_prev = pl.multiple_of(jnp.maximum(n - 1, 0) * WINDOW, WINDOW)
        dkv_ref[pl.ds(r_prev, WINDOW), :] += both[:WINDOW]
        dkv_ref[pl.ds(r_cur, WINDOW), :] += both[WINDOW:]

    key_major = pl.BlockSpec((SWA_KV, 2 * WINDOW, SWA_GRP * WINDOW), lambda n: (0, 0, 0))
    (dq, dkv, dbias_t, dsinks), got = carried(
        body, comm, grid=(L // WINDOW,),
        in_specs=[pl.BlockSpec((WINDOW, 1024), lambda n: (n, 1)), pl.BlockSpec((WINDOW, 256), lambda n: (n, 8)),
                  pl.BlockSpec((WINDOW, 256), lambda n: (jnp.maximum(n - 1, 0), 8)),
                  pl.BlockSpec((WINDOW, 1024), lambda n: (n, 0)), key_major,
                  pl.BlockSpec((1, SWA_HEADS), lambda n: (0, 0))],
        out_specs=[pl.BlockSpec((WINDOW, 1024), lambda n: (n, 0)), pl.BlockSpec((L, 256), lambda n: (0, 0)), key_major,
                   pl.BlockSpec((1, SWA_HEADS), lambda n: (0, 0))],
        out_shape=[jax.ShapeDtypeStruct((L, 1024), F32), jax.ShapeDtypeStruct((L, 256), F32),
                   jax.ShapeDtypeStruct((SWA_KV, 2 * WINDOW, SWA_GRP * WINDOW), F32),
                   jax.ShapeDtypeStruct((1, SWA_HEADS), F32)],
        semantics=("arbitrary",), name='b_attn_bwd')(proj, proj, proj, do, _swa_key_major(bias), sinks)
    dbias = dbias_t.reshape(SWA_KV, 2 * WINDOW, SWA_GRP, WINDOW).transpose(0, 2, 3, 1).reshape(SWA_HEADS, WINDOW, 2 * WINDOW)
    return (dq, dkv, dbias, dsinks), got


def swa_bias(rel_bias):
    def body(bk_ref, rb_ref, o_ref):
        bk = bk_ref[...]
        for h in range(SWA_HEADS):
            acc = jnp.zeros((WINDOW, 2 * WINDOW), F32)
            for b in range(REL_BUCKETS):
                acc = jnp.where(bk == b, rb_ref[b, h], acc)
            o_ref[h] = acc

    return pl.pallas_call(
        body, out_shape=jax.ShapeDtypeStruct((SWA_HEADS, WINDOW, 2 * WINDOW), F32),
        in_specs=[pl.BlockSpec(memory_space=pltpu.VMEM), pl.BlockSpec(memory_space=pltpu.SMEM)],
        out_specs=pl.BlockSpec(memory_space=pltpu.VMEM), name='b_bias')(jnp.asarray(_t5_bucket_np()), rel_bias)


def layer_b_fwd(h, w, p, comm=None):
    proj = mm(h, w['b_w_in'], 'nn', 'b_proj')
    bias = swa_bias(p['rel_bias'])
    (o, po), carried_out = swa_fwd(proj, bias, p['b_sinks'], comm=comm)
    yb = mm(po, w['b_w_out'], 'nn', 'b_out')
    return yb, dict(carried=carried_out, h=h, proj=proj, bias=bias, o=o, po=po)


def layer_b_bwd(dyb, w, p, sv, comm=None, sink=None):
    g = {}
    dpo = mm(dyb, w['b_w_out'], 'nt', 'b_dpo')
    _dw(g, sink, 'b_w_out', sv['po'], dyb, 'b_dwout')
    proj = sv['proj']

    def f1(dpo_, o, z):
        return [dpo_ * silu(z), dpo_ * o * silu_grad(z)], []
    (do, dz), _ = rowwise(f1, [rw(dpo), rw(sv['o']), rw(proj, 1024, 0)], [], [(1024, BF16), (1024, F32)], [], 256, 'b_gate_bwd')
    (dq, dkv, dbias, dsinks), g['carried'] = swa_bwd(proj, do, sv['bias'], p['b_sinks'], comm=comm)
    g['b_sinks'] = dsinks
    onehot = jnp.asarray(np.eye(REL_BUCKETS, dtype=np.float32)[_t5_bucket_np().reshape(-1)])

    def f2(db, oh):
        return [], [lax.dot_general(db, oh, (((1,), (0,)), ((), ())), preferred_element_type=F32,
                                    precision=lax.Precision.HIGHEST)]
    _, (drel,) = rowwise(f2, [(dbias.reshape(SWA_HEADS, -1), pl.BlockSpec((SWA_HEADS, 4096), lambda i: (0, i))),
                              (onehot, pl.BlockSpec((4096, REL_BUCKETS), lambda i: (i, 0)))], [], [],
                         [(SWA_HEADS, REL_BUCKETS)], 4096, 'b_drel', n_steps=(2 * WINDOW * WINDOW) // 4096)
    g['rel_bias'] = drel.T

    def f3(dz_, dq_, dkv_):
        return [jnp.concatenate([dz_, dq_, dkv_], axis=1)], []
    (dproj,), _ = rowwise(f3, [rw(dz), rw(dq), rw(dkv)], [], [(2304, BF16)], [], 256, 'b_dproj')
    _dw(g, sink, 'b_w_in', sv['h'], dproj, 'b_dwin')
    dh = mm(dproj, w['b_w_in'], 'nt', 'b_dh')
    return dh, g


MLA_SCALE = (MLA_NOPE + MLA_ROPE) ** -0.5
_LOG2E = math.log2(math.e)


def _rope_tables(L):
    inv = ROPE_BASE ** (-jnp.arange(0, MLA_ROPE, 2, dtype=F32) / MLA_ROPE)
    ang = jnp.arange(L, dtype=F32)[:, None] * inv[None, :]
    c, s = jnp.cos(ang), jnp.sin(ang)
    one, zero, pad = jnp.ones((L, 128), F32), jnp.zeros((L, 128), F32), jnp.zeros((L, 64), F32)
    return (jnp.concatenate([one, c, c, c, c, pad], 1), jnp.concatenate([zero, s, s, s, s, pad], 1))


def _rot(x, transpose=False):
    w = x.shape[1]
    lane = lax.broadcasted_iota(jnp.int32, x.shape, 1)
    up = pltpu.roll(x, w - 16, 1)
    dn = pltpu.roll(x, 16, 1)
    first = (lane % 32) < 16
    return jnp.where(first, up, -dn) if transpose else jnp.where(first, -up, dn)


MLA_QT = 512


def _mla_exp(qf, kf, t, qt):
    s = lax.dot_general(qf, kf, (((1,), (1,)), ((), ())), preferred_element_type=F32)
    causal = lax.broadcasted_iota(jnp.int32, (qt, qt), 1) <= lax.broadcasted_iota(jnp.int32, (qt, qt), 0)
    last = jnp.where(causal, s[:, t * qt:], NEG_INF)
    s = last if t == 0 else jnp.concatenate([s[:, :t * qt], last], axis=1)
    e = jnp.exp2((s - jnp.max(s, axis=-1, keepdims=True)) * (MLA_SCALE * _LOG2E))
    return e, jnp.sum(e, axis=-1, keepdims=True)


def _mla_heads(q, kv, kr):
    out = []
    for j in range(2):
        qf = jnp.concatenate([q[:, j * 64:(j + 1) * 64], q[:, 128 + j * 32:128 + (j + 1) * 32]], axis=1)
        kf = jnp.concatenate([kv[:, j * 64:(j + 1) * 64], kr], axis=1)
        out.append((qf, kf, kv[:, 128 + j * 64:128 + (j + 1) * 64]))
    return out


def mla_fwd(q, kv, kr, comm=None):
    L = q.shape[0]
    qt = min(MLA_QT, L)
    nq = L // qt

    def body(q_ref, kv_ref, kr_ref, o_ref):
        for t in range(nq):
            @pl.when(pl.program_id(1) == t)
            def _(t=t):
                n_k = (t + 1) * qt
                outs = []
                for qf, kf, v in _mla_heads(q_ref[...], kv_ref[0:n_k, :], kr_ref[0:n_k, 0:MLA_ROPE]):
                    e, den = _mla_exp(qf, kf, t, qt)
                    outs.append(jnp.dot(e.astype(BF16), v, preferred_element_type=F32) / den)
                o_ref[...] = jnp.concatenate(outs, axis=1)

    return carried(
        body, comm, grid=(MLA_HEADS // 2, nq),
        in_specs=[pl.BlockSpec((qt, 256), lambda hp, n: (n, hp)), pl.BlockSpec((L, 256), lambda hp, n: (0, hp)),
                  pl.BlockSpec((L, 128), lambda hp, n: (0, 0))],
        out_specs=pl.BlockSpec((qt, 128), lambda hp, n: (n, hp)), out_shape=jax.ShapeDtypeStruct((L, 1024), F32),
        semantics=("parallel", "parallel"), name='c_attn')(q, kv, kr)


def mla_bwd(q, kv, kr, do, o, comm=None):
    L = q.shape[0]
    qt = min(MLA_QT, L)
    nq = L // qt

    def body(q_ref, kv_ref, kr_ref, do_ref, o_ref, dq_ref, dkv_ref, dkr_ref):
        @pl.when(pl.program_id(1) == 0)
        def _():
            dkv_ref[...] = jnp.zeros_like(dkv_ref)
            dkr_ref[...] = jnp.zeros_like(dkr_ref)

        for t in range(nq):
            @pl.when(pl.program_id(1) == t)
            def _(t=t):
                n_k = (t + 1) * qt
                do_, o_ = do_ref[...], o_ref[...]
                dqn, dqr, dkn, dvs = [], [], [], []
                dkr = jnp.zeros((MLA_ROPE, n_k), F32)
                wide = lambda x: jnp.concatenate([x, jnp.zeros((qt, 128 - x.shape[1]), x.dtype)], axis=1)
                for j, (qf, kf, v) in enumerate(_mla_heads(q_ref[...], kv_ref[0:n_k, :], kr_ref[0:n_k, 0:MLA_ROPE])):
                    doh = do_[:, j * 64:(j + 1) * 64]
                    dof = doh.astype(F32)
                    e, den = _mla_exp(qf, kf, t, qt)
                    inv = 1.0 / den
                    dp = lax.dot_general(doh, v, (((1,), (1,)), ((), ())), preferred_element_type=F32)
                    delta = jnp.sum(dof * o_[:, j * 64:(j + 1) * 64], axis=-1, keepdims=True)
                    ds = (e * ((dp - delta) * (inv * MLA_SCALE))).astype(BF16)
                    dqf = jnp.dot(ds, kf, preferred_element_type=F32)
                    dkf = lax.dot_general(wide(qf), ds, _TN, preferred_element_type=F32)
                    dvf = lax.dot_general(wide((dof * inv).astype(BF16)), e.astype(BF16), _TN,
                                          preferred_element_type=F32)
                    dqn.append(dqf[:, :MLA_NOPE])
                    dqr.append(dqf[:, MLA_NOPE:])
                    dkn.append(dkf[:MLA_NOPE])
                    dvs.append(dvf[:MLA_V])
                    dkr = dkr + dkf[MLA_NOPE:MLA_NOPE + MLA_ROPE]
                dq_ref[...] = jnp.concatenate(dqn + dqr + [jnp.zeros((qt, 64), F32)], axis=1)
                dkv_ref[0:n_k, :] += jnp.concatenate(dkn + dvs, axis=0).T
                dkr_ref[0, 0:n_k, :] += jnp.concatenate([dkr, jnp.zeros((128 - MLA_ROPE, n_k), F32)], axis=0).T

    return carried(
        body, comm, grid=(MLA_HEADS // 2, nq),
        in_specs=[pl.BlockSpec((qt, 256), lambda hp, n: (n, hp)), pl.BlockSpec((L, 256), lambda hp, n: (0, hp)),
                  pl.BlockSpec((L, 128), lambda hp, n: (0, 0)), pl.BlockSpec((qt, 128), lambda hp, n: (n, hp)),
                  pl.BlockSpec((qt, 128), lambda hp, n: (n, hp))],
        out_specs=[pl.BlockSpec((qt, 256), lambda hp, n: (n, hp)), pl.BlockSpec((L, 256), lambda hp, n: (0, hp)),
                   pl.BlockSpec((1, L, 128), lambda hp, n: (hp, 0, 0))],
        out_shape=[jax.ShapeDtypeStruct((L, 2048), F32), jax.ShapeDtypeStruct((L, 2048), F32),
                   jax.ShapeDtypeStruct((MLA_HEADS // 2, L, 128), F32)],
        semantics=("parallel", "arbitrary"), name='c_attn_bwd')(q, kv, kr, do, o)


def layer_c_fwd(h, w, p, comm=None):
    L = h.shape[0]
    proj = mm(h, w['c_w_in'], 'nn', 'c_proj')

    def f1(c, gq, gk):
        return [rms_fwd(c[:, :768], gq), rms_fwd(c[:, 768:], gk)], []
    (cqn, ckvn), _ = rowwise(f1, [rw(proj, 1024, 1)], [p['c_q_norm'], p['c_kv_norm']], [(768, BF16), (256, BF16)], [],
                             256, 'c_norms')
    qf = mm(cqn, w['c_w_uq'], 'nn', 'c_uq')
    kvf = mm(ckvn, w['c_w_ukv'], 'nn', 'c_ukv', out_dtype=BF16)
    cos, sin = _rope_tables(L)

    def f2(q_, kr_, c, s):
        c8, s8 = jnp.tile(c, (1, 8)), jnp.tile(s, (1, 8))
        return [q_ * c8 + _rot(q_) * s8, kr_ * c[:, 128:] + _rot(kr_) * s[:, 128:]], []
    (q, kr), _ = rowwise(f2, [rw(qf), rw(proj, 128, 16), rw(cos), rw(sin)], [], [(2048, BF16), (128, BF16)], [], 256,
                         'c_rope')
    o, carried_out = mla_fwd(q, kvf, kr, comm=comm)

    def f3(o_, z):
        return [o_ * silu(z)], []
    (po,), _ = rowwise(f3, [rw(o), rw(proj, 1024, 0)], [], [(1024, BF16)], [], 256, 'c_gate')
    yb = mm(po, w['c_w_out'], 'nn', 'c_out')
    return yb, dict(carried=carried_out, h=h, proj=proj, cqn=cqn, ckvn=ckvn, q=q, kv=kvf, kr=kr, o=o, po=po, cos=cos, sin=sin)


def layer_c_bwd(dyb, w, p, sv, comm=None, sink=None):
    g = {}
    dpo = mm(dyb, w['c_w_out'], 'nt', 'c_dpo')
    _dw(g, sink, 'c_w_out', sv['po'], dyb, 'c_dwout')
    proj = sv['proj']
    L = proj.shape[0]

    def f1(dpo_, o, z):
        return [dpo_ * silu(z), dpo_ * o * silu_grad(z)], []
    (do, dz), _ = rowwise(f1, [rw(dpo), rw(sv['o']), rw(proj, 1024, 0)], [], [(1024, BF16), (1024, F32)], [], 256,
                          'c_gate_bwd')
    (dq, dkvf, dkr8), g['carried'] = mla_bwd(sv['q'], sv['kv'], sv['kr'], do, sv['o'], comm=comm)

    def f2(dq_, dkr_, c, s):
        c8, s8 = jnp.tile(c, (1, 8)), jnp.tile(s, (1, 8))
        dk = jnp.sum(dkr_, axis=0)
        return [dq_ * c8 + _rot(dq_ * s8, True), dk * c[:, 128:] + _rot(dk * s[:, 128:], True)], []
    tl = 256
    (dqf, dkr), _ = rowwise(f2, [rw(dq), (dkr8, pl.BlockSpec((8, tl, 128), lambda i: (0, i, 0))), rw(sv['cos']),
                                 rw(sv['sin'])], [], [(2048, BF16), (128, F32)], [], tl, 'c_rope_bwd')
    _dw(g, sink, 'c_w_uq', sv['cqn'], dqf, 'c_dwuq')
    _dw(g, sink, 'c_w_ukv', sv['ckvn'], dkvf, 'c_dwukv')
    dcqn = mm(dqf, w['c_w_uq'], 'nt', 'c_dcqn')
    dckvn = mm(dkvf, w['c_w_ukv'], 'nt', 'c_dckvn')

    def f3(c, dq_, dk_, dz_, dkr_, gq, gk):
        dcq, dgq = rms_bwd(c[:, :768], gq, dq_)
        dckv, dgk = rms_bwd(c[:, 768:], gk, dk_)
        return [jnp.concatenate([dz_, dcq, dckv, dkr_], axis=1)], [dgq, dgk]
    (dproj,), (dgq, dgk) = rowwise(f3, [rw(proj, 1024, 1), rw(dcqn), rw(dckvn), rw(dz), rw(dkr)],
                                   [p['c_q_norm'], p['c_kv_norm']], [(2176, BF16)], [(1, 768), (1, 256)], 256, 'c_dproj')
    g['c_q_norm'], g['c_kv_norm'] = dgq, dgk
    _dw(g, sink, 'c_w_in', sv['h'], dproj, 'c_dwin')
    dh = mm(dproj, w['c_w_in'], 'nt', 'c_dh')
    return dh, g


def _sgu_mix(wm, v, transpose):
    outs = []
    dims = (((0,), (0,)), ((), ())) if transpose else (((1,), (0,)), ((), ()))
    for gi in range(SGU_G):
        outs.append(lax.dot_general(wm[gi], v[:, gi * SGU_C:(gi + 1) * SGU_C].astype(BF16), dims,
                                    preferred_element_type=F32))
    return jnp.concatenate(outs, axis=1)


def _sgu_wmask(ws):
    t = lax.broadcasted_iota(jnp.int32, (SGU_T, SGU_T), 0)
    s = lax.broadcasted_iota(jnp.int32, (SGU_T, SGU_T), 1)
    return jnp.where((s <= t)[None], ws, 0.0).astype(BF16)


def _ln_stats(v):
    mu = jnp.mean(v, axis=-1, keepdims=True)
    vc = v - mu
    rstd = lax.rsqrt(jnp.mean(vc * vc, axis=-1, keepdims=True) + EPS)
    return vc * rstd, rstd


def layer_d_fwd(h, w, p):
    proj = mm(h, w['d_w_in'], 'nn', 'd_proj')
    bias = jnp.repeat(p['d_b_s'][0].T, SGU_C, axis=1)

    def f1(u_, v_, z, ws, lg, lb, bs):
        xh, _ = _ln_stats(gelu(v_))
        s = _sgu_mix(_sgu_wmask(ws), xh * lg + lb, False) + bs
        return [gelu(u_) * s * silu(z)], []
    (po,), _ = rowwise(f1, [rw(proj, 1024, 0), rw(proj, 1024, 1), rw(proj, 1024, 2)],
                       [p['d_w_s'][0], p['d_ln_g'], p['d_ln_b'], bias], [(1024, BF16)], [], SGU_T, 'd_mix')
    yb = mm(po, w['d_w_out'], 'nn', 'd_out')
    return yb, dict(h=h, proj=proj, po=po, bias=bias)


def layer_d_bwd(dyb, w, p, sv, sink=None):
    g = {}
    dpo = mm(dyb, w['d_w_out'], 'nt', 'd_dpo')
    _dw(g, sink, 'd_w_out', sv['po'], dyb, 'd_dwout')
    proj = sv['proj']

    def f1(dpo_, u_, v_, z, ws, lg, lb, bs):
        wm = _sgu_wmask(ws)
        gv = gelu(v_)
        xh, rstd = _ln_stats(gv)
        vn = xh * lg + lb
        s = _sgu_mix(wm, vn, False) + bs
        gu, sz = gelu(u_), silu(z)
        du = dpo_ * s * sz
        ds = dpo_ * gu * sz
        dz = dpo_ * gu * s * silu_grad(z)
        dsb = ds.astype(BF16)
        dws = jnp.stack([lax.dot_general(dsb[:, gi * SGU_C:(gi + 1) * SGU_C], vn[:, gi * SGU_C:(gi + 1) * SGU_C].astype(BF16),
                                         (((1,), (1,)), ((), ())), preferred_element_type=F32) for gi in range(SGU_G)])
        dvn = _sgu_mix(wm, ds, True)
        dlg = jnp.sum(dvn * xh, axis=0, keepdims=True)
        dlb = jnp.sum(dvn, axis=0, keepdims=True)
        dxh = dvn * lg
        dgv = rstd * (dxh - jnp.mean(dxh, axis=-1, keepdims=True) - xh * jnp.mean(dxh * xh, axis=-1, keepdims=True))
        return ([jnp.concatenate([du * gelu_grad(u_), dgv * gelu_grad(v_), dz], axis=1)], [dws, ds, dlg, dlb])
    (dproj,), (dws, dbs, dlg, dlb) = rowwise(
        f1, [rw(dpo), rw(proj, 1024, 0), rw(proj, 1024, 1), rw(proj, 1024, 2)],
        [p['d_w_s'][0], p['d_ln_g'], p['d_ln_b'], sv['bias']], [(3072, BF16)],
        [(SGU_G, SGU_T, SGU_T), (SGU_T, 1024), (1, 1024), (1, 1024)], SGU_T, 'd_mix_bwd')
    tril = np.tril(np.ones((SGU_T, SGU_T), dtype=bool))
    g['d_w_s'] = jnp.where(tril[None], dws, 0.0)[None]
    g['d_b_s'] = dbs.reshape(SGU_T, SGU_G, SGU_C).sum(-1).T[None]
    g['d_ln_g'], g['d_ln_b'] = dlg, dlb
    _dw(g, sink, 'd_w_in', sv['h'], dproj, 'd_dwin')
    dh = mm(dproj, w['d_w_in'], 'nt', 'd_dh')
    return dh, g


def _coords():
    return lax.axis_index("x"), lax.axis_index("y"), lax.axis_index("c")


class AllGather:
    def __init__(self, x):
        self.ins = [x]
        self.outs = [jax.ShapeDtypeStruct((N_DEV,) + x.shape, x.dtype)]
        self.scratch = [pltpu.SemaphoreType.DMA((7,)), pltpu.SemaphoreType.DMA((7,)), pltpu.SemaphoreType.DMA(())]

    def hooks(self, n_steps):
        return [(0, functools.partial(self.phase, 0), False), (n_steps - 1, functools.partial(self.phase, 1), True),
                (n_steps - 1, functools.partial(self.phase, 2), True)]

    @staticmethod
    def phase(which, ins, outs, scratch):
        (x_ref,), (out_ref,), (send_sems, recv_sems, local_sem) = ins, outs, scratch
        x_, y_, c_ = _coords()
        me, sibling = (x_, y_, c_), (x_, y_, 1 - c_)
        chips = [(1 - x_, y_), (x_, 1 - y_), (1 - x_, 1 - y_)]

        def slot(px, py, pc):
            return out_ref.at[4 * px + 2 * py + pc]

        def copy(k, block, to, src=None):
            return pltpu.make_async_remote_copy(src_ref=slot(*block) if src is None else src, dst_ref=slot(*block),
                                                send_sem=send_sems.at[k], recv_sem=recv_sems.at[k], device_id=to,
                                                device_id_type=MESH)

        mine = pltpu.make_async_copy(x_ref, slot(*me), local_sem)
        first = [copy(0, me, sibling, src=x_ref)]
        first += [copy(1 + j, me, (*chip, c_), src=x_ref) for j, chip in enumerate(chips)]
        passed = [copy(4 + j, (*chip, c_), sibling) for j, chip in enumerate(chips)]
        if which == 0:
            mine.start()
            for cp in first:
                cp.start()
        elif which == 1:
            for j, chip in enumerate(chips):
                copy(1 + j, (*chip, c_), me).wait_recv()
                passed[j].start()
        else:
            copy(0, sibling, me).wait_recv()
            for j, chip in enumerate(chips):
                copy(4 + j, (*chip, 1 - c_), me).wait_recv()
            for cp in first + passed:
                cp.wait_send()
            mine.wait()


class ChipExchange:
    def __init__(self, part):
        self.ins = [part]
        self.outs = [jax.ShapeDtypeStruct((3,) + part.shape[1:], part.dtype)]
        self.scratch = [pltpu.SemaphoreType.DMA((3,)), pltpu.SemaphoreType.DMA((3,))]

    def hooks(self, n_steps):
        return [(0, functools.partial(self.phase, 0), False), (n_steps - 1, functools.partial(self.phase, 1), True)]

    @staticmethod
    def phase(which, ins, outs, scratch):
        (p_ref,), (land_ref,), (send_sems, recv_sems) = ins, outs, scratch
        x_, y_, c_ = _coords()
        copies = []
        for r, (fx, fy) in enumerate([(1, 0), (0, 1), (1, 1)]):
            tx = jnp.where(fx == 1, 1 - x_, x_)
            ty = jnp.where(fy == 1, 1 - y_, y_)
            copies.append(pltpu.make_async_remote_copy(src_ref=p_ref.at[2 * tx + ty], dst_ref=land_ref.at[r],
                                                       send_sem=send_sems.at[r], recv_sem=recv_sems.at[r],
                                                       device_id=(tx, ty, c_), device_id_type=MESH))
        if which == 0:
            for cp in copies:
                cp.start()
        else:
            for cp in copies:
                cp.wait_recv()
            for cp in copies:
                cp.wait_send()


class Both:
    def __init__(self, a, b):
        self.parts = (a, b)
        self.ins, self.outs, self.scratch = a.ins + b.ins, a.outs + b.outs, a.scratch + b.scratch

    def hooks(self, n_steps):
        res, oi, oo, osc = [], 0, 0, 0
        for p in self.parts:
            sl = (slice(oi, oi + len(p.ins)), slice(oo, oo + len(p.outs)), slice(osc, osc + len(p.scratch)))
            res += [(at, functools.partial(self.sub, fn, sl), after) for at, fn, after in p.hooks(n_steps)]
            oi, oo, osc = oi + len(p.ins), oo + len(p.outs), osc + len(p.scratch)
        return res

    @staticmethod
    def sub(fn, sl, ins, outs, scratch):
        fn(ins[sl[0]], outs[sl[1]], scratch[sl[2]])


def run_comm(comm, name):
    def body(*refs):
        ci, co = len(comm.ins), len(comm.outs)
        for _, fn, _ in comm.hooks(1):
            fn(refs[:ci], refs[ci:ci + co], refs[ci + co:])

    return pl.pallas_call(body, out_shape=list(comm.outs), in_specs=[ANY] * len(comm.ins),
                          out_specs=[ANY] * len(comm.outs), scratch_shapes=list(comm.scratch), name=name)(*comm.ins)


def all_gather(x, name):
    return run_comm(AllGather(x), name)[0]


class SiblingExchange:
    def __init__(self, gfull):
        self.ins = [gfull]
        self.outs = [jax.ShapeDtypeStruct((4,) + gfull.shape[1:], gfull.dtype)]
        self.scratch = [pltpu.SemaphoreType.DMA((4,)), pltpu.SemaphoreType.DMA((4,))]

    def hooks(self, n_steps):
        return [(0, functools.partial(self.phase, 0), False), (n_steps - 1, functools.partial(self.phase, 1), True)]

    @staticmethod
    def phase(which, ins, outs, scratch):
        (g_ref,), (land_ref,), (send_sems, recv_sems) = ins, outs, scratch
        x_, y_, c_ = _coords()
        copies = [pltpu.make_async_remote_copy(src_ref=g_ref.at[2 * k + 1 - c_], dst_ref=land_ref.at[k],
                                               send_sem=send_sems.at[k], recv_sem=recv_sems.at[k],
                                               device_id=(x_, y_, 1 - c_), device_id_type=MESH) for k in range(4)]
        if which == 0:
            for cp in copies:
                cp.start()
        else:
            for cp in copies:
                cp.wait_recv()
            for cp in copies:
                cp.wait_send()


def rs_sibling(gfull, tag):
    return run_comm(SiblingExchange(gfull), 'rs_sibling_' + tag)[0]


def rs_pair_add(gfull, land, core, tag):
    _, R, C = gfull.shape
    tl = R

    def body(c_ref, g_ref, l_ref, o_ref):
        o_ref[...] = (g_ref[...].astype(F32) + l_ref[...].astype(F32)).astype(BF16)

    return pl.pallas_call(
        body, out_shape=jax.ShapeDtypeStruct((4, R, C), BF16),
        grid_spec=pltpu.PrefetchScalarGridSpec(
            num_scalar_prefetch=1, grid=(4, R // tl),
            in_specs=[pl.BlockSpec((1, tl, C), lambda k, i, c: (2 * k + c[0], i, 0)),
                      pl.BlockSpec((1, tl, C), lambda k, i, c: (k, i, 0))],
            out_specs=pl.BlockSpec((1, tl, C), lambda k, i, c: (k, i, 0))),
        compiler_params=pltpu.CompilerParams(dimension_semantics=("parallel", "parallel")), name='rs_pair_add_' + tag)(
            core, gfull, land)


def rs_chips(part, tag):
    return run_comm(ChipExchange(part), 'rs_chips_' + tag)[0]


def _adam(wv, gv, mv, vv):
    m = ADAM_B1 * mv + (1.0 - ADAM_B1) * gv
    v = ADAM_B2 * vv + (1.0 - ADAM_B2) * (gv * gv)
    m_hat = m / (1.0 - ADAM_B1 ** ADAM_STEP)
    v_hat = v / (1.0 - ADAM_B2 ** ADAM_STEP)
    delta = -ADAM_LR * (m_hat / (jnp.sqrt(v_hat) + ADAM_EPS) + ADAM_WD * wv)
    return delta, m, v


def _sum4(p_ref, l_ref):
    return ((p_ref[0].astype(F32) + l_ref[0].astype(F32)) + l_ref[1].astype(F32)) + l_ref[2].astype(F32)


def rs_rep_sum(part, land, chip):
    def body(c_ref, p_ref, l_ref, o_ref):
        o_ref[...] = _sum4(p_ref, l_ref).astype(BF16)

    return pl.pallas_call(
        body, out_shape=jax.ShapeDtypeStruct((REP_SLOT, LANES), BF16),
        grid_spec=pltpu.PrefetchScalarGridSpec(
            num_scalar_prefetch=1, grid=(1,),
            in_specs=[pl.BlockSpec((1, REP_SLOT, LANES), lambda i, c: (c[0], 0, 0)),
                      pl.BlockSpec((3, REP_SLOT, LANES), lambda i, c: (0, 0, 0))],
            out_specs=pl.BlockSpec((REP_SLOT, LANES), lambda i, c: (0, 0))),
        compiler_params=pltpu.CompilerParams(dimension_semantics=("parallel",)), name='rs_rep')(chip, part, land)


def adam_param(name, shape, off, w, m, v, chip, part=None, land=None, grep=None, fold=1):
    r, c = shape
    rp, nt, rb = _tiles((r // fold, c * fold))
    rbw = min(r, rb) if fold == 1 else r
    n_src = 2 if grep is None else 1
    ns = w.shape
    assert int(np.prod(ns[:-1])) == r and ns[-1] == c and (fold == 1 or (rb == rp and nt == 1))
    if fold > 1:
        nat_block, nat_map = ns, lambda i, cr: (0,) * len(ns)
    elif len(ns) == 2:
        nat_block, nat_map = (rbw, c), lambda i, cr: (i, 0)
    elif int(np.prod(ns[:-2])) == 1:
        nat_block, nat_map = (1,) * (len(ns) - 2) + (rbw, c), lambda i, cr: (0,) * (len(ns) - 2) + (i, 0)
    else:
        assert len(ns) == 4 and ns[0] == 1 and rbw % ns[2] == 0
        nat_block, nat_map = (1, rbw // ns[2], ns[2], c), lambda i, cr: (0, i, 0, 0)

    def body(c_ref, *refs):
        srcs = refs[:n_src * nt]
        w_ref, m_ref, v_ref, g_ref, d_ref, nm_ref, nv_ref = refs[n_src * nt:]
        if grep is None:
            tiles = [_sum4(srcs[2 * t], srcs[2 * t + 1]) for t in range(nt)]
        else:
            tiles = [srcs[t][...].astype(F32) for t in range(nt)]
        if fold > 1:
            g = jnp.concatenate([tiles[0][:, q * c:(q + 1) * c] for q in range(fold)], axis=0)
        else:
            g = (tiles[0] if nt == 1 else jnp.concatenate(tiles, axis=1))[:rbw, :c]
        g_ref[...] = g.reshape(nat_block)
        res = _adam(w_ref[...].reshape(rbw, c), g, m_ref[...].reshape(rbw, c), v_ref[...].reshape(rbw, c))
        for ref, val in zip((d_ref, nm_ref, nv_ref), res):
            ref[...] = val.reshape(nat_block)

    in_specs, args = [], []
    for t in range(nt):
        b0 = (off + t * rp) // rb
        assert (off + t * rp) % rb == 0
        if grep is None:
            in_specs += [pl.BlockSpec((1, rb, LANES), functools.partial(lambda i, cr, b0: (cr[0], b0 + i, 0), b0=b0)),
                         pl.BlockSpec((3, rb, LANES), functools.partial(lambda i, cr, b0: (0, b0 + i, 0), b0=b0))]
            args += [part, land]
        else:
            in_specs.append(pl.BlockSpec((rb, LANES), functools.partial(lambda i, cr, b0: (b0 + i, 0), b0=b0)))
            args.append(grep)
    nat = pl.BlockSpec(nat_block, nat_map)
    return pl.pallas_call(
        body, out_shape=[jax.ShapeDtypeStruct(ns, F32)] * 4,
        grid_spec=pltpu.PrefetchScalarGridSpec(num_scalar_prefetch=1, grid=(rp // rb,), in_specs=in_specs + [nat] * 3,
                                               out_specs=[nat] * 4),
        compiler_params=pltpu.CompilerParams(dimension_semantics=("parallel",)), name='adam_' + name)(
            chip, *args, w, m, v)


def adam_small(names, grep, P, M, V):
    in_specs, args, out_specs, out_shape, meta = [], [], [], [], []
    for n in names:
        s = REP_SHAPE[n]
        rp, nt, _ = _tiles(s)
        ns = P[n].shape
        for t in range(nt):
            b0 = (REP_OFF[n] + t * rp) // rp
            assert (REP_OFF[n] + t * rp) % rp == 0
            in_specs.append(pl.BlockSpec((rp, LANES), functools.partial(lambda i, b0: (b0, 0), b0=b0)))
            args.append(grep)
        nat = pl.BlockSpec(ns, functools.partial(lambda i, nd: (0,) * nd, nd=len(ns)))
        in_specs += [nat] * 3
        args += [P[n], M[n], V[n]]
        out_specs += [nat] * 4
        out_shape += [jax.ShapeDtypeStruct(ns, F32)] * 4
        meta.append((s, nt, ns))
    n_in = len(in_specs)

    def body(*refs):
        ins, outs = refs[:n_in], refs[n_in:]
        k = 0
        for p, ((r, c), nt, ns) in enumerate(meta):
            tiles = [ins[k + t][...].astype(F32) for t in range(nt)]
            w_ref, m_ref, v_ref = ins[k + nt:k + nt + 3]
            k += nt + 3
            g = (tiles[0] if nt == 1 else jnp.concatenate(tiles, axis=1))[:r, :c]
            res = (g,) + _adam(w_ref[...].reshape(r, c), g, m_ref[...].reshape(r, c), v_ref[...].reshape(r, c))
            for ref, val in zip(outs[4 * p:4 * p + 4], res):
                ref[...] = val.reshape(ns)

    res = pl.pallas_call(body, grid=(1,), in_specs=in_specs, out_specs=out_specs, out_shape=out_shape,
                         compiler_params=pltpu.CompilerParams(dimension_semantics=("arbitrary",)), name='adam_small')(*args)
    return {n: tuple(res[4 * p:4 * p + 4]) for p, n in enumerate(names)}


VM = pl.BlockSpec(memory_space=pltpu.VMEM)


def _tile_value(w, t, rp):
    r, c = w.shape
    wt = min(LANES, c - t * LANES)
    tile = w[:, t * LANES:t * LANES + wt]
    if wt < LANES:
        tile = jnp.concatenate([tile, jnp.zeros((r, LANES - wt), tile.dtype)], axis=1)
    if rp > r:
        tile = jnp.concatenate([tile, jnp.zeros((rp - r, LANES), tile.dtype)], axis=0)
    return tile


def pack_layer(layer, blocks):
    names = LAYER_PARAMS[layer]

    def body(*refs):
        tiles = []
        for ref, n in zip(refs[:-1], names):
            rp, nt, _ = _tiles(_block_shape(n))
            w = ref[...].reshape(_block_shape(n))
            tiles += [_tile_value(w, t, rp) for t in range(nt)]
        refs[-1][...] = jnp.concatenate(tiles, axis=0).astype(BF16)

    return pl.pallas_call(body, out_shape=jax.ShapeDtypeStruct((LAYER_ROWS[layer], LANES), BF16),
                          in_specs=[VM] * len(names), out_specs=VM, name='pack_' + layer)(*[blocks[n] for n in names])


def assemble(name, gathered):
    (rf, cf), ax = SHARDED[name]
    r, c = _block_shape(name)
    rp, nt, _ = _tiles((r, c))
    off = SH_OFF[name]
    out_cols = cf if ax == 0 else len(perm_index(name))

    def body(g_ref, o_ref, buf, sem):
        cp = pltpu.make_async_copy(g_ref.at[:, pl.ds(off, nt * rp), :], buf, sem)
        cp.start()
        cp.wait()
        if ax == 0:
            for j in range(N_DEV):
                o_ref[j * r:(j + 1) * r, :] = jnp.concatenate([buf[j, t * rp:(t + 1) * rp, :] for t in range(nt)], axis=1)
            return
        pieces = []
        for p in PERM[name]:
            if p[0] == 'z':
                pieces.append(jnp.zeros((r, p[1]), BF16))
                continue
            n0, w = p
            while w > 0:
                j, cb = divmod(n0, c)
                t, lane = divmod(cb, LANES)
                wl = min(w, LANES - lane, c - cb)
                pieces.append(buf[j, t * rp:t * rp + r, lane:lane + wl])
                n0, w = n0 + wl, w - wl
        o_ref[...] = jnp.concatenate(pieces, axis=1)

    return pl.pallas_call(
        body, out_shape=jax.ShapeDtypeStruct((rf, out_cols), BF16), in_specs=[ANY], out_specs=VM,
        scratch_shapes=[pltpu.VMEM((N_DEV, nt * rp, LANES), BF16), pltpu.SemaphoreType.DMA(())], name='asm_' + name)(
            gathered)


def chunk_grad(layer, name, dw, gfull):
    (rf, cf), ax = SHARDED[name]
    r, c = _block_shape(name)
    rp, nt, _ = _tiles((r, c))
    off = SH_OFF[name]
    if ax == 1:
        idx = perm_index(name) if name in PERM else np.arange(cf)
        inv = np.full(cf, -1)
        inv[idx[idx >= 0]] = np.nonzero(idx >= 0)[0]

    def body(*refs):
        dw_ref, o_ref, buf, sem = refs[0], refs[-3], refs[-2], refs[-1]
        for j in range(N_DEV):
            for t in range(nt):
                if ax == 0:
                    tile = dw_ref[j * r:(j + 1) * r, t * LANES:(t + 1) * LANES]
                else:
                    cols = inv[j * c + t * LANES:j * c + min((t + 1) * LANES, c)]
                    cuts = [0] + [k for k in range(1, len(cols)) if cols[k] != cols[k - 1] + 1] + [len(cols)]
                    pieces = [dw_ref[:, int(cols[a]):int(cols[b - 1]) + 1] for a, b in zip(cuts[:-1], cuts[1:])]
                    if len(cols) < LANES:
                        pieces.append(jnp.zeros((r, LANES - len(cols)), F32))
                    tile = pieces[0] if len(pieces) == 1 else jnp.concatenate(pieces, axis=1)
                    if rp > r:
                        tile = jnp.concatenate([tile, jnp.zeros((rp - r, LANES), F32)], axis=0)
                buf[j, t * rp:(t + 1) * rp, :] = tile.astype(BF16)
        cp = pltpu.make_async_copy(buf, o_ref.at[:, pl.ds(off, nt * rp), :], sem)
        cp.start()
        cp.wait()

    shape = jax.ShapeDtypeStruct((N_DEV, LAYER_ROWS[layer], LANES), BF16)
    scratch = [pltpu.VMEM((N_DEV, nt * rp, LANES), BF16), pltpu.SemaphoreType.DMA(())]
    if gfull is None:
        return pl.pallas_call(body, out_shape=shape, in_specs=[VM], out_specs=ANY, scratch_shapes=scratch,
                              name='chunk_' + name)(dw)
    return pl.pallas_call(body, out_shape=shape, in_specs=[VM, ANY], out_specs=ANY, scratch_shapes=scratch,
                          input_output_aliases={1: 0}, name='chunk_' + name)(dw, gfull)


class GradSink:
    def __init__(self):
        self.bufs = {}

    def put(self, name, a, b, mm_name):
        (rf, cf), ax = SHARDED[name]
        r, c = _block_shape(name)
        group = GROUP_OF[name]
        direct = ax == 0 or (c % LANES == 0 and PERM[name] == [(0, cf)])
        if direct:
            self.bufs[group] = mm_tn_chunked(a, b, mm_name, group, name, self.bufs.get(group))
        else:
            self.add(name, mm(a, b, 'tn', mm_name))

    def add(self, name, dw):
        group = GROUP_OF[name]
        self.bufs[group] = chunk_grad(group, name, dw, self.bufs.get(group))


def mm_tn_chunked(a, b, mm_name, layer, wname, gfull):
    (rf, cf), ax = SHARDED[wname]
    r, c = _block_shape(wname)
    rp, nt, _ = _tiles((r, c))
    off = SH_OFF[wname]
    K, M = a.shape
    N = b.shape[1]
    assert (M, N) == (rf, cf) and rp == r
    if ax == 0:
        tn = 4 * LANES
        grid, bspec = (N // tn,), pl.BlockSpec((K, tn), lambda g: (0, g))
        ospec = pl.BlockSpec((N_DEV, 4 * r, LANES), lambda g: (0, off // (4 * r) + g, 0))
        assert off % (4 * r) == 0 and nt % 4 == 0

        def store(res, o_ref):
            for j in range(N_DEV):
                for q in range(4):
                    o_ref[j, q * r:(q + 1) * r, :] = res[j * r:(j + 1) * r, q * LANES:(q + 1) * LANES].astype(BF16)
    else:
        tn = c
        grid, bspec = (N_DEV,), pl.BlockSpec((K, tn), lambda g: (0, g))
        ospec = pl.BlockSpec((1, nt * r, LANES), lambda g: (g, off // (nt * r), 0))
        assert off % (nt * r) == 0

        def store(res, o_ref):
            for t in range(nt):
                o_ref[0, t * r:(t + 1) * r, :] = res[:, t * LANES:(t + 1) * LANES].astype(BF16)

    def body(*refs):
        a_ref, b_ref, o_ref = refs[0], refs[1], refs[-1]
        store(lax.dot_general(a_ref[...].astype(BF16), b_ref[...].astype(BF16), _TN, preferred_element_type=F32), o_ref)

    shape = jax.ShapeDtypeStruct((N_DEV, LAYER_ROWS[layer], LANES), BF16)
    aspec = pl.BlockSpec((K, M), lambda g: (0, 0))
    params = pltpu.CompilerParams(dimension_semantics=("parallel",))
    if gfull is None:
        return pl.pallas_call(body, grid=grid, in_specs=[aspec, bspec], out_specs=ospec, out_shape=shape,
                              compiler_params=params, name=mm_name)(a, b)
    return pl.pallas_call(body, grid=grid, in_specs=[aspec, bspec, ANY], out_specs=ospec, out_shape=shape,
                          input_output_aliases={2: 0}, compiler_params=params, name=mm_name)(a, b, gfull)


def pack_rep(G):
    def body(*refs):
        tiles = []
        for ref, n in zip(refs[:-1], REP_SHAPE):
            rp, nt, _ = _tiles(_rep_packed_shape(n))
            g = ref[...]
            fold = REP_FOLD.get(n, 1)
            if fold > 1:
                rr = g.shape[0] // fold
                g = jnp.concatenate([g[q * rr:(q + 1) * rr] for q in range(fold)], axis=1)
            tiles += [_tile_value(g, t, rp) for t in range(nt)]
        rows = sum(t.shape[0] for t in tiles)
        if rows < REP_ROWS:
            tiles.append(jnp.zeros((REP_ROWS - rows, LANES), F32))
        full = jnp.concatenate(tiles, axis=0)
        for j in range(N_DEV):
            refs[-1][j] = full[j * REP_CHUNK:(j + 1) * REP_CHUNK]

    return pl.pallas_call(body, out_shape=jax.ShapeDtypeStruct((N_DEV, REP_SLOT, LANES), F32),
                          in_specs=[VM] * len(REP_SHAPE), out_specs=VM, name='pack_rep')(
                              *[G[n].reshape(s) for n, s in REP_SHAPE.items()])


def _pack_small(blocks, order, rows, width, dtype):
    flat = jnp.concatenate([blocks[n].reshape(-1).astype(dtype) for n in order])
    return jnp.pad(flat, (0, rows * width - flat.shape[0])).reshape(rows, width)


def kernel(x, pre_norm, post_norm, rel_bias, a_w_in, a_lam_re, a_lam_im, a_log_dt, a_b_re, a_b_im, a_c_re, a_c_im, a_d, a_w_glu, a_b_glu, a_w_out, b_w_in, b_sinks, b_w_out, c_w_in, c_q_norm, c_kv_norm, c_w_uq, c_w_ukv, c_w_out, d_w_in, d_ln_g, d_ln_b, d_w_s, d_b_s, d_w_out, loss_target, m_pre_norm, m_post_norm, m_rel_bias, m_a_w_in, m_a_lam_re, m_a_lam_im, m_a_log_dt, m_a_b_re, m_a_b_im, m_a_c_re, m_a_c_im, m_a_d, m_a_w_glu, m_a_b_glu, m_a_w_out, m_b_w_in, m_b_sinks, m_b_w_out, m_c_w_in, m_c_q_norm, m_c_kv_norm, m_c_w_uq, m_c_w_ukv, m_c_w_out, m_d_w_in, m_d_ln_g, m_d_ln_b, m_d_w_s, m_d_b_s, m_d_w_out, v_pre_norm, v_post_norm, v_rel_bias, v_a_w_in, v_a_lam_re, v_a_lam_im, v_a_log_dt, v_a_b_re, v_a_b_im, v_a_c_re, v_a_c_im, v_a_d, v_a_w_glu, v_a_b_glu, v_a_w_out, v_b_w_in, v_b_sinks, v_b_w_out, v_c_w_in, v_c_q_norm, v_c_kv_norm, v_c_w_uq, v_c_w_ukv, v_c_w_out, v_d_w_in, v_d_ln_g, v_d_ln_b, v_d_w_s, v_d_b_s, v_d_w_out):
    loc = locals()
    P = {n: loc[n] for n in WEIGHTS}
    M = {n: loc['m_' + n] for n in WEIGHTS}
    V = {n: loc['v_' + n] for n in WEIGHTS}
    xs = x[0]
    L = xs.shape[0]

    blocks = {n: P[n].reshape(_block_shape(n)) for n in SHARDED}
    packed = {layer: pack_layer(layer, P) for layer in LAYER_PARAMS}
    W = {}

    def assemble_layer(layer, gathered):
        for n in LAYER_PARAMS[layer]:
            if n not in SHARDED_F32:
                W[n] = assemble(n, gathered)

    Pl = dict(P)

    def arrived_first(got):
        assemble_layer('a1', got[0])
        for n in SHARDED_F32:
            c = SHARDED[n][0][1]
            bc = c // N_DEV
            Pl[n] = got[1].reshape(N_DEV, -1)[:, SMALL_OFF[n]:SMALL_OFF[n] + bc].reshape(1, c)
    cx, cy, cc = _coords()
    core = jnp.reshape(cc, (1,)).astype(jnp.int32)
    chip = jnp.reshape(2 * cx + cy, (1,)).astype(jnp.int32)

    def pair_sums(gfull, tag):
        return rs_pair_add(gfull, rs_sibling(gfull, tag), core, tag)

    fwd = [layer_a_fwd, layer_b_fwd, layer_c_fwd, layer_d_fwd]
    bwd = [layer_a_bwd, layer_b_bwd, layer_c_bwd, layer_d_bwd]
    saved = []
    xc = xs

    def fpre(x_, g_):
        return [rms_fwd(x_, g_)], []
    (h,), _ = rowwise(fpre, [rw(xc)], [P['pre_norm'][0:1]], [(D_MODEL, BF16)], [], 256, 'pre_norm0')
    for i in range(4):
        if i == 0:
            yb, sv = fwd[i](h, W, Pl, comm=Both(AllGather(packed['a2']), AllGather(packed['b'])),
                            on_carried=lambda got: assemble_layer('a2', got[0]),
                            prep_comm=Both(AllGather(packed['a1']),
                                           AllGather(_pack_small(blocks, SHARDED_F32, SMALL_ROWS, 128, F32))),
                            on_prep=arrived_first)
            assemble_layer('b', sv['carried'][1])
        elif i < 3:
            nxt = 'abcd'[i + 1]
            yb, sv = fwd[i](h, W, Pl, comm=AllGather(packed[nxt]))
            assemble_layer(nxt, sv['carried'][0])
        else:
            yb, sv = fwd[i](h, W, Pl)

        sv['x'], sv['yb'] = xc, yb
        saved.append(sv)
        if i < 3:

            def fpost(x_, y_, gpost, gpre):
                xn_ = x_ + rms_fwd(y_, gpost)
                return [xn_, rms_fwd(xn_, gpre)], []
            (xc, h), _ = rowwise(fpost, [rw(xc), rw(yb)], [P['post_norm'][i:i + 1], P['pre_norm'][i + 1:i + 2]],
                                 [(D_MODEL, F32), (D_MODEL, BF16)], [], 256, f'post_pre_norm{i}')
        else:

            def floss(x_, y_, t_, gpost):
                d = x_ + rms_fwd(y_, gpost) - t_
                return [d * (1.0 / D_MODEL)], [0.5 * jnp.sum(jnp.sum(d * d, axis=-1, keepdims=True) * (1.0 / D_MODEL),
                                                             axis=0, keepdims=True)]
            (dx,), (loss_loc,) = rowwise(floss, [rw(xc), rw(yb), rw(loss_target[0])], [P['post_norm'][i:i + 1]],
                                         [(D_MODEL, F32)], [(1, 1)], 256, 'post_norm_loss')

    G, out = {}, {}
    dpre, dpost = [None] * 4, [None] * 4

    def adam_layer(layer, part, land2):
        for n in LAYER_PARAMS[layer]:
            s = _block_shape(n)
            out[n] = adam_param(n, s, SH_OFF[n], P[n], M[n], V[n], chip, part=part, land=land2)

    def fpost_b(y_, d_, g_):
        dy, dg = rms_bwd(y_, g_, d_)
        return [dy], [dg]
    (dyb,), (dpost[3],) = rowwise(fpost_b, [rw(saved[3]['yb']), rw(dx)], [P['post_norm'][3:4]], [(D_MODEL, BF16)],
                                  [(1, D_MODEL)], 256, 'post_norm_bwd3')
    pending = None
    sink = GradSink()
    for i in reversed(range(4)):
        sv = saved[i]
        if pending is None:
            dh, g = bwd[i](dyb, W, Pl, sv, sink=sink)
        elif i > 0:
            dh, g = bwd[i](dyb, W, Pl, sv, comm=ChipExchange(pending[1]), sink=sink)
            adam_layer(pending[0], pending[1], g['carried'][0])
        else:
            early = {}

            def both():
                early['part'] = pair_sums(sink.bufs['a2'], 'a2')
                return Both(ChipExchange(pending[1]), ChipExchange(early['part']))
            dh, g = bwd[i](dyb, W, Pl, sv, comm=both, sink=sink)
            adam_layer(pending[0], pending[1], g['carried'][0])
            adam_layer('a2', early['part'], g['carried'][1])
        g.pop('carried', None)
        land_a1 = g.pop('land_a1', None)
        G.update(g)
        group = LAYER_GROUPS['abcd'[i]][0]
        for n in LAYER_PARAMS[group]:
            if n in g:
                sink.add(n, g[n])
        if i == 3:
            swap = Both(SiblingExchange(sink.bufs[group]), AllGather(jnp.broadcast_to(loss_loc, (8, LANES))))
        elif i > 0:
            swap = SiblingExchange(sink.bufs[group])
        else:
            part_a1 = rs_pair_add(sink.bufs[group], land_a1, core, group)
            swap = ChipExchange(part_a1)

        if i > 0:

            def fpre_b(x_, dh_, d_, y_, gpre, gpost):
                dxl, dg = rms_bwd(x_, gpre, dh_)
                dy, dgp = rms_bwd(y_, gpost, d_ + dxl)
                return [d_ + dxl, dy], [dg, dgp]
            (dx, dyb), (dpre[i], dpost[i - 1]), (land, *loss_all) = rowwise(
                fpre_b, [rw(sv['x']), rw(dh), rw(dx), rw(saved[i - 1]['yb'])],
                [P['pre_norm'][i:i + 1], P['post_norm'][i - 1:i]], [(D_MODEL, F32), (D_MODEL, BF16)],
                [(1, D_MODEL), (1, D_MODEL)], 256, f'pre_post_norm_bwd{i}', comm=swap)
            if loss_all:
                loss = jnp.sum(loss_all[0][:, 0, 0])
        else:

            def fpre_b0(x_, dh_, d_, g_):
                dxl, dg = rms_bwd(x_, g_, dh_)
                return [d_ + dxl], [dg]
            (dx,), (dpre[i],), (land2_a1,) = rowwise(fpre_b0, [rw(sv['x']), rw(dh), rw(dx)], [P['pre_norm'][i:i + 1]],
                                                     [(D_MODEL, F32)], [(1, D_MODEL)], 256, 'pre_norm_bwd0', comm=swap)
            adam_layer('a1', part_a1, land2_a1)
            break
        pending = (group, rs_pair_add(sink.bufs[group], land, core, group))
    G['pre_norm'] = jnp.concatenate(dpre, axis=0)
    G['post_norm'] = jnp.concatenate(dpost, axis=0)

    part = pair_sums(pack_rep(G), 'rep')
    land2 = rs_chips(part, 'rep')
    grep = all_gather(rs_rep_sum(part, land2, chip), 'ag_rep')[:, :REP_CHUNK].reshape(REP_ROWS, LANES)
    small_names = [n for n, s in REP_SHAPE.items() if s[0] <= 64]
    out.update(adam_small(small_names, grep, P, M, V))
    for n, s in REP_SHAPE.items():
        if n not in small_names:
            out[n] = adam_param(n, s, REP_OFF[n], P[n], M[n], V[n], chip, grep=grep, fold=REP_FOLD.get(n, 1))
    res = [loss, dx[None]]
    for kind in range(4):
        res += [out[n][kind].reshape(P[n].shape) for n in WEIGHTS]
    return tuple(res)
```

```python
import functools
import math

import numpy as np
import jax
import jax.numpy as jnp
from jax import lax
from jax.experimental import pallas as pl
from jax.experimental.pallas import tpu as pltpu

F32 = jnp.float32
BF16 = jnp.bfloat16
MESH = pl.DeviceIdType.MESH
ANY = pl.BlockSpec(memory_space=pl.ANY)

N_DEV = 8
D_MODEL = 1024
EPS = 1e-6
NEG_INF = -1e30
SSM_G, SSM_P, SSM_H = 64, 64, 16
SSM_T = 256
SSM_TS = 8
SSM_WC = 512
HEAD_DIM = 64
SWA_HEADS, SWA_KV = 16, 2
WINDOW = 128
REL_BUCKETS, REL_MAX_DIST = 32, 128
MLA_HEADS, MLA_NOPE, MLA_ROPE, MLA_V = 16, 64, 32, 64
MLA_Q_RANK, MLA_KV_RANK = 768, 256
ROPE_BASE = 10000.0
SGU_G, SGU_C, SGU_T = 16, 64, 128
ADAM_LR, ADAM_B1, ADAM_B2, ADAM_EPS, ADAM_WD, ADAM_STEP = 0.001, 0.9, 0.999, 1e-08, 0.01, 10

WEIGHTS = ['pre_norm', 'post_norm', 'rel_bias', 'a_w_in', 'a_lam_re', 'a_lam_im', 'a_log_dt', 'a_b_re', 'a_b_im',
           'a_c_re', 'a_c_im', 'a_d', 'a_w_glu', 'a_b_glu', 'a_w_out', 'b_w_in', 'b_sinks', 'b_w_out', 'c_w_in',
           'c_q_norm', 'c_kv_norm', 'c_w_uq', 'c_w_ukv', 'c_w_out', 'd_w_in', 'd_ln_g', 'd_ln_b', 'd_w_s', 'd_b_s',
           'd_w_out']
SHARDED = {'a_w_in': ((1024, 2048), 1), 'a_w_glu': ((1024, 1024), 0), 'a_w_out': ((1024, 1024), 0),
           'b_w_in': ((1024, 2304), 1), 'b_w_out': ((1024, 1024), 0), 'c_w_in': ((1024, 2080), 1),
           'c_q_norm': ((1, 768), 1), 'c_kv_norm': ((1, 256), 1), 'c_w_uq': ((768, 1536), 1),
           'c_w_ukv': ((256, 2048), 1), 'c_w_out': ((1024, 1024), 0), 'd_w_in': ((1024, 3072), 1),
           'd_ln_g': ((1, 1024), 1), 'd_ln_b': ((1, 1024), 1), 'd_w_out': ((1024, 1024), 0)}
SHARDED_F32 = ['c_q_norm', 'c_kv_norm', 'd_ln_g', 'd_ln_b']
REPLICATED = [n for n in WEIGHTS if n not in SHARDED]


def _cdiv(a, b):
    return -(-a // b)


def _block_shape(name):
    (r, c), ax = SHARDED[name]
    return (r // N_DEV, c) if ax == 0 else (r, c // N_DEV)


LANES = 128
LAYER_PARAMS = {'a1': ['a_w_in'], 'a2': ['a_w_glu', 'a_w_out'], 'b': ['b_w_in', 'b_w_out'],
                'c': ['c_w_in', 'c_w_uq', 'c_w_ukv', 'c_w_out', 'c_q_norm', 'c_kv_norm'],
                'd': ['d_w_in', 'd_w_out', 'd_ln_g', 'd_ln_b']}


def _tiles(shape):
    r, c = shape
    rp = max(r, 16)
    rb = 512 if rp % 512 == 0 else 256 if rp % 256 == 0 else rp
    return rp, _cdiv(c, LANES), rb


SH_OFF, LAYER_ROWS = {}, {}
for _l, _names in LAYER_PARAMS.items():
    _o = 0
    for _n in _names:
        _rp, _nt, _rb = _tiles(_block_shape(_n))
        assert _o % _rb == 0
        SH_OFF[_n] = _o
        _o += _rp * _nt
    assert _o % 16 == 0
    LAYER_ROWS[_l] = _o
GROUP_OF = {_n: _l for _l, _names in LAYER_PARAMS.items() for _n in _names}
LAYER_GROUPS = {'a': ['a1', 'a2'], 'b': ['b'], 'c': ['c'], 'd': ['d']}
C_EARLY_ROWS = 2608
assert C_EARLY_ROWS % 16 == 0 and C_EARLY_ROWS < LAYER_ROWS['c']

REP_SHAPE = {'d_w_s': (2048, 128), 'a_b_re': (4096, 16), 'a_b_im': (4096, 16), 'a_c_re': (1024, 64),
             'a_c_im': (1024, 64), 'pre_norm': (4, 1024), 'post_norm': (4, 1024), 'a_lam_re': (64, 64),
             'a_lam_im': (64, 64), 'a_d': (1, 1024), 'a_b_glu': (1, 1024), 'rel_bias': (32, 16), 'd_b_s': (16, 128),
             'a_log_dt': (1, 64), 'b_sinks': (1, 16)}
REP_FOLD = {'a_b_re': 8, 'a_b_im': 8, 'a_c_re': 2, 'a_c_im': 2}


def _rep_packed_shape(name):
    (r, c), f = REP_SHAPE[name], REP_FOLD.get(name, 1)
    return (r // f, c * f)


REP_OFF = {}
_o = 0
for _n in REP_SHAPE:
    _rp, _nt, _rb = _tiles(_rep_packed_shape(_n))
    assert _o % _rb == 0
    REP_OFF[_n] = _o
    _o += _rp * _nt
REP_ROWS = _cdiv(_o, 16 * N_DEV) * 16 * N_DEV
REP_CHUNK = REP_ROWS // N_DEV
REP_SLOT = REP_CHUNK

PERM = {'a_w_in': [(0, 2048)], 'd_w_in': [(0, 3072)], 'b_w_in': [(1280, 1024), (0, 1280)],
        'c_w_in': [(1056, 1024), (0, 1056), ('z', 96)],
        'c_w_uq': sum([[(2 * hp * 96, 64), ((2 * hp + 1) * 96, 64), (2 * hp * 96 + 64, 32), ((2 * hp + 1) * 96 + 64, 32),
                        ('z', 64)] for hp in range(8)], []),
        'c_w_ukv': sum([[(2 * hp * 128, 64), ((2 * hp + 1) * 128, 64), (2 * hp * 128 + 64, 64),
                         ((2 * hp + 1) * 128 + 64, 64)] for hp in range(8)], [])}


def perm_index(name):
    return np.concatenate([np.full(p[1], -1) if p[0] == 'z' else np.arange(p[0], p[0] + p[1]) for p in PERM[name]])


SMALL_OFF = {}
_o = 0
for _n in SHARDED_F32:
    SMALL_OFF[_n] = _o
    _o += int(np.prod(_block_shape(_n)))
SMALL_ROWS = _cdiv(_o, 128 * 8) * 8


def _pick(n, cands):
    for c in cands:
        if n % c == 0:
            return c
    return n


def mm(a, b, mode, name, out_dtype=F32, comm=None):
    if mode == 'nn':
        (M, K), (K2, N) = a.shape, b.shape
    elif mode == 'nt':
        (M, K), (N, K2) = a.shape, b.shape
    else:
        (K, M), (K2, N) = a.shape, b.shape
    assert K == K2, (name, a.shape, b.shape)
    tm = _pick(M, (1024, 768, 512, 256, 128))
    tn = _pick(N, (512, 384, 256))
    dims = {'nn': ((1,), (0,)), 'nt': ((1,), (1,)), 'tn': ((0,), (0,))}[mode]

    def body(a_ref, b_ref, o_ref):
        o_ref[...] = lax.dot_general(a_ref[...].astype(BF16), b_ref[...].astype(BF16), (dims, ((), ())),
                                     preferred_element_type=F32).astype(out_dtype)

    a_spec = pl.BlockSpec((K, tm), lambda i, j: (0, i)) if mode == 'tn' else pl.BlockSpec((tm, K), lambda i, j: (i, 0))
    b_spec = pl.BlockSpec((tn, K), lambda i, j: (j, 0)) if mode == 'nt' else pl.BlockSpec((K, tn), lambda i, j: (0, j))
    res = carried(body, comm, grid=(M // tm, N // tn), in_specs=[a_spec, b_spec],
                  out_specs=pl.BlockSpec((tm, tn), lambda i, j: (i, j)), out_shape=jax.ShapeDtypeStruct((M, N), out_dtype),
                  semantics=("parallel", "parallel"), name=name)(a, b)
    return res[0] if comm is None else res


def rw(arr, width=None, cb=0):
    return (arr, arr.shape[1] if width is None else width, cb)


def rowwise(fn, rows, consts, outs, accs, tl, name, n_steps=None, comm=None):
    if n_steps is None:
        n_steps = [r[0].shape[0] for r in rows if not isinstance(r[1], pl.BlockSpec)][0] // tl
    L = n_steps * tl
    nr, nc, no, na = len(rows), len(consts), len(outs), len(accs)
    in_specs, args = [], []
    for r in rows:
        if isinstance(r[1], pl.BlockSpec):
            in_specs.append(r[1])
        else:
            in_specs.append(pl.BlockSpec((tl, r[1]), functools.partial(lambda i, cb: (i, cb), cb=r[2])))
        args.append(r[0])
    for c in consts:
        in_specs.append(pl.BlockSpec(c.shape, functools.partial(lambda i, nd: (0,) * nd, nd=c.ndim)))
        args.append(c)
    out_specs = [pl.BlockSpec((tl, w), lambda i: (i, 0)) for w, _ in outs]
    out_shape = [jax.ShapeDtypeStruct((L, w), dt) for w, dt in outs]
    for s in accs:
        out_specs.append(pl.BlockSpec(s, functools.partial(lambda i, nd: (0,) * nd, nd=len(s))))
        out_shape.append(jax.ShapeDtypeStruct(s, F32))

    def body(*refs):
        ins = [r[...] for r in refs[:nr + nc]]
        o_refs = refs[nr + nc:nr + nc + no]
        a_refs = refs[nr + nc + no:]
        o_vals, a_vals = fn(*ins)
        for ref, val in zip(o_refs, o_vals):
            ref[...] = val.astype(ref.dtype)
        if na:
            @pl.when(pl.program_id(0) == 0)
            def _():
                for ref in a_refs:
                    ref[...] = jnp.zeros_like(ref)
            for ref, val in zip(a_refs, a_vals):
                ref[...] += val

    res, carried_out = carried(body, comm, grid=(n_steps,), in_specs=in_specs, out_specs=out_specs, out_shape=out_shape,
                               name=name, semantics=("arbitrary",))(*args)
    if comm is None:
        return res[:no], res[no:]
    return res[:no], res[no:], carried_out


def carried(body, comm, *, grid, in_specs, out_specs, out_shape, name, semantics, scratch_shapes=()):
    single = not isinstance(out_shape, (list, tuple))
    o_specs = [out_specs] if single else list(out_specs)
    o_shape = [out_shape] if single else list(out_shape)
    if comm is None:
        call = pl.pallas_call(body, grid=grid, in_specs=in_specs, out_specs=out_specs, out_shape=out_shape,
                              scratch_shapes=list(scratch_shapes),
                              compiler_params=pltpu.CompilerParams(dimension_semantics=semantics), name=name)
        return lambda *args: (call(*args), None)
    n_in, n_out, n_sc = len(in_specs), len(o_specs), len(scratch_shapes)
    ci, co = len(comm.ins), len(comm.outs)
    n_steps = int(np.prod(grid))
    hooks = comm.hooks(n_steps)

    def wrapped(*refs):
        ins, cins = refs[:n_in], refs[n_in:n_in + ci]
        outs, couts = refs[n_in + ci:n_in + ci + n_out], refs[n_in + ci + n_out:n_in + ci + n_out + co]
        sc, csc = refs[n_in + ci + n_out + co:n_in + ci + n_out + co + n_sc], refs[n_in + ci + n_out + co + n_sc:]
        step = pl.program_id(0)
        for ax in range(1, len(grid)):
            step = step * grid[ax] + pl.program_id(ax)
        for at, fn, after in hooks:
            if not after:
                pl.when(step == at)(functools.partial(fn, cins, couts, csc))
        body(*ins, *outs, *sc)
        for at, fn, after in hooks:
            if after:
                pl.when(step == at)(functools.partial(fn, cins, couts, csc))

    call = pl.pallas_call(wrapped, grid=grid, in_specs=list(in_specs) + [ANY] * ci, out_specs=o_specs + [ANY] * co,
                          out_shape=o_shape + list(comm.outs), scratch_shapes=list(scratch_shapes) + list(comm.scratch),
                          compiler_params=pltpu.CompilerParams(dimension_semantics=("arbitrary",) * len(grid)), name=name)

    def run(*args):
        res = call(*args, *comm.ins)
        return (res[0] if single else res[:n_out]), res[n_out:]
    return run


_K0 = math.sqrt(2.0 / math.pi)
_K1 = 0.044715


def gelu(x):
    return x * (0.5 * (1.0 + jnp.tanh(_K0 * (x + _K1 * (x * x * x)))))


def gelu_grad(x):
    t = jnp.tanh(_K0 * (x + _K1 * (x * x * x)))
    return 0.5 * (1.0 + t) + 0.5 * x * (1.0 - t * t) * (_K0 * (1.0 + 3.0 * _K1 * x * x))


def sigmoid(x):
    return 1.0 / (1.0 + jnp.exp(-x))


def silu(z):
    return z * sigmoid(z)


def silu_grad(z):
    s = sigmoid(z)
    return s * (1.0 + z * (1.0 - s))


def rms_fwd(x, g):
    r = lax.rsqrt(jnp.mean(x * x, axis=-1, keepdims=True) + EPS)
    return x * r * g


def rms_bwd(x, g, dy):
    r = lax.rsqrt(jnp.mean(x * x, axis=-1, keepdims=True) + EPS)
    xh = x * r
    dg = jnp.sum(dy * xh, axis=0, keepdims=True)
    dxh = dy * g
    dx = r * (dxh - xh * jnp.mean(dxh * xh, axis=-1, keepdims=True))
    return dx, dg


def _scan_chunk(a_r, a_i, pr_ref, pi_ref, cr, ci, T, reverse):
    ts = min(SSM_TS, T)
    sgn = -1.0 if reverse else 1.0
    row = lax.broadcasted_iota(jnp.int32, (ts, a_r.shape[1]), 0)
    pw = (lambda e: T - e) if reverse else (lambda e: e - 1)
    if reverse:
        wr_c, wi_c = pr_ref[T - ts:T, :], sgn * pi_ref[T - ts:T, :]
    else:
        wr_c, wi_c = pr_ref[0:ts, :], sgn * pi_ref[0:ts, :]
    c_r, c_i = cr[...], ci[...]
    outs = []
    subs = range(T // ts)
    for sub in (reversed(subs) if reverse else subs):
        v_r, v_i = a_r[sub * ts:(sub + 1) * ts], a_i[sub * ts:(sub + 1) * ts]
        d = 1
        while d < ts:
            wr = pr_ref[pw(d):pw(d) + 1, :]
            wi = sgn * pi_ref[pw(d):pw(d) + 1, :]
            if reverse:
                yr, yi, keep = pltpu.roll(v_r, ts - d, 0), pltpu.roll(v_i, ts - d, 0), row < ts - d
            else:
                yr, yi, keep = pltpu.roll(v_r, d, 0), pltpu.roll(v_i, d, 0), row >= d
            v_r, v_i = (v_r + jnp.where(keep, wr * yr - wi * yi, 0.0), v_i + jnp.where(keep, wr * yi + wi * yr, 0.0))
            d *= 2
        v_r, v_i = v_r + (wr_c * c_r - wi_c * c_i), v_i + (wr_c * c_i + wi_c * c_r)
        k = 0 if reverse else ts - 1
        c_r, c_i = v_r[k:k + 1, :], v_i[k:k + 1, :]
        outs.append((v_r, v_i))
    if reverse:
        outs = outs[::-1]
    cr[...] = c_r
    ci[...] = c_i
    return jnp.concatenate([o[0] for o in outs], axis=0), jnp.concatenate([o[1] for o in outs], axis=0)


_NT = (((1,), (1,)), ((), ()))
_TN = (((0,), (0,)), ((), ()))


def s5_fwd(proj, d_skip, Bre, Bim, Cre, Cim, pr, pi, comm=None):
    L = proj.shape[0]
    T, WC = min(SSM_T, L), SSM_WC
    nT = L // T

    def body(u_ref, d_ref, bre_ref, bim_ref, cre_ref, cim_ref, pr_ref, pi_ref, y_ref, yg_ref, sr_ref, si_ref, cr, ci):
        @pl.when(pl.program_id(1) == 0)
        def _():
            cr[...] = jnp.zeros_like(cr)
            ci[...] = jnp.zeros_like(ci)

        u = u_ref[...]
        ub = u.astype(BF16)
        a_r = lax.dot_general(ub, bre_ref[0].astype(BF16), _NT, preferred_element_type=F32)
        a_i = lax.dot_general(ub, bim_ref[0].astype(BF16), _NT, preferred_element_type=F32)
        a_r, a_i = _scan_chunk(a_r, a_i, pr_ref, pi_ref, cr, ci, T, False)
        sr_ref[...] = a_r
        si_ref[...] = a_i
        y = (lax.dot_general(a_r.astype(BF16), cre_ref[0].astype(BF16), _NT, preferred_element_type=F32)
             + lax.dot_general(a_i.astype(BF16), cim_ref[0].astype(BF16), _NT, preferred_element_type=F32)
             + d_ref[...] * u)
        y_ref[...] = y
        yg_ref[...] = gelu(y)

    uspec = pl.BlockSpec((T, 128), lambda k, i: (i, k))
    sspec = pl.BlockSpec((T, WC), lambda k, i: (i, k))
    return carried(
        body, comm, grid=(8, nT),
        in_specs=[uspec, pl.BlockSpec((1, 128), lambda k, i: (0, k)),
                  pl.BlockSpec((1, WC, 128), lambda k, i: (k, 0, 0)), pl.BlockSpec((1, WC, 128), lambda k, i: (k, 0, 0)),
                  pl.BlockSpec((1, 128, WC), lambda k, i: (k, 0, 0)), pl.BlockSpec((1, 128, WC), lambda k, i: (k, 0, 0)),
                  pl.BlockSpec((T, WC), lambda k, i: (0, k)), pl.BlockSpec((T, WC), lambda k, i: (0, k))],
        out_specs=[uspec, uspec, sspec, sspec],
        out_shape=[jax.ShapeDtypeStruct((L, 1024), F32)] * 2 + [jax.ShapeDtypeStruct((L, 8 * WC), F32)] * 2,
        scratch_shapes=[pltpu.VMEM((1, WC), F32), pltpu.VMEM((1, WC), F32)],
        semantics=("parallel", "arbitrary"), name='a_ssm')(proj, d_skip, Bre, Bim, Cre, Cim, pr, pi)


def s5_bwd(proj, dyg1, dyg2, y, d_skip, s_re, s_im, Bre, Bim, Cre, Cim, prr, pir, comm=None):
    L = proj.shape[0]
    T, WC = min(SSM_T, L), SSM_WC
    nT = L // T

    def body(u_ref, g1_ref, g2_ref, y_ref, d_ref, sr_ref, si_ref, spr_ref, spi_ref, bre_ref, bim_ref, cre_ref, cim_ref,
             pr_ref, pi_ref, du_ref, dd_ref, dbre_ref, dbim_ref, dcre_ref, dcim_ref, dar_ref, dai_ref, cr, ci):
        i = pl.program_id(1)

        @pl.when(i == 0)
        def _():
            for ref in (cr, ci, dd_ref, dbre_ref, dbim_ref, dcre_ref, dcim_ref, dar_ref, dai_ref):
                ref[...] = jnp.zeros_like(ref)

        u = u_ref[...]
        dy = (g1_ref[...] + g2_ref[...]) * gelu_grad(y_ref[...])
        dd_ref[...] += jnp.sum(dy * u, axis=0, keepdims=True)
        dyb, ub = dy.astype(BF16), u.astype(BF16)
        bre, bim, cre, cim = (r[0].astype(BF16) for r in (bre_ref, bim_ref, cre_ref, cim_ref))
        g_r = jnp.dot(dyb, cre, preferred_element_type=F32)
        g_i = jnp.dot(dyb, cim, preferred_element_type=F32)
        g_r, g_i = _scan_chunk(g_r, g_i, pr_ref, pi_ref, cr, ci, T, True)
        s_r, s_i = sr_ref[...], si_ref[...]
        row = lax.broadcasted_iota(jnp.int32, (T, WC), 0)
        first = (nT - 1 - i) == 0
        sp_r = jnp.where(row == 0, jnp.where(first, 0.0, spr_ref[7:8, :]), pltpu.roll(s_r, 1, 0))
        sp_i = jnp.where(row == 0, jnp.where(first, 0.0, spi_ref[7:8, :]), pltpu.roll(s_i, 1, 0))
        dar_ref[...] += jnp.sum(g_r * sp_r + g_i * sp_i, axis=0, keepdims=True)
        dai_ref[...] += jnp.sum(g_i * sp_r - g_r * sp_i, axis=0, keepdims=True)
        grb, gib = g_r.astype(BF16), g_i.astype(BF16)
        dcre_ref[0] += lax.dot_general(dyb, s_r.astype(BF16), _TN, preferred_element_type=F32)
        dcim_ref[0] += lax.dot_general(dyb, s_i.astype(BF16), _TN, preferred_element_type=F32)
        dbre_ref[0] += lax.dot_general(grb, ub, _TN, preferred_element_type=F32)
        dbim_ref[0] += lax.dot_general(gib, ub, _TN, preferred_element_type=F32)
        du_ref[...] = (dy * d_ref[...] + jnp.dot(grb, bre, preferred_element_type=F32)
                       + jnp.dot(gib, bim, preferred_element_type=F32))

    uspec = pl.BlockSpec((T, 128), lambda k, i: (nT - 1 - i, k))
    sspec = pl.BlockSpec((T, WC), lambda k, i: (nT - 1 - i, k))
    pspec = pl.BlockSpec((8, WC), lambda k, i: (jnp.maximum((nT - 1 - i) * (T // 8) - 1, 0), k))
    tab = pl.BlockSpec((T, WC), lambda k, i: (0, k))
    bspec = pl.BlockSpec((1, WC, 128), lambda k, i: (k, 0, 0))
    cspec = pl.BlockSpec((1, 128, WC), lambda k, i: (k, 0, 0))
    return carried(
        body, comm, grid=(8, nT),
        in_specs=[uspec, uspec, uspec, uspec, pl.BlockSpec((1, 128), lambda k, i: (0, k)), sspec, sspec, pspec, pspec,
                  bspec, bspec, cspec, cspec, tab, tab],
        out_specs=[uspec, pl.BlockSpec((1, 128), lambda k, i: (0, k)), bspec, bspec, cspec, cspec,
                   pl.BlockSpec((1, WC), lambda k, i: (0, k)), pl.BlockSpec((1, WC), lambda k, i: (0, k))],
        out_shape=[jax.ShapeDtypeStruct((L, 1024), F32), jax.ShapeDtypeStruct((1, 1024), F32),
                   jax.ShapeDtypeStruct((8, WC, 128), F32), jax.ShapeDtypeStruct((8, WC, 128), F32),
                   jax.ShapeDtypeStruct((8, 128, WC), F32), jax.ShapeDtypeStruct((8, 128, WC), F32),
                   jax.ShapeDtypeStruct((1, 8 * WC), F32), jax.ShapeDtypeStruct((1, 8 * WC), F32)],
        scratch_shapes=[pltpu.VMEM((1, WC), F32), pltpu.VMEM((1, WC), F32)],
        semantics=("parallel", "arbitrary"), name='a_ssm_bwd')(
            proj, dyg1, dyg2, y, d_skip, s_re, s_im, s_re, s_im, Bre, Bim, Cre, Cim, prr, pir)


def s5_discretize(lam_re, lam_im, log_dt, b_re, b_im):
    dt = jnp.exp(log_dt)[:, None]
    mag = jnp.exp(lam_re * dt)
    ab_re = mag * jnp.cos(lam_im * dt)
    ab_im = mag * jnp.sin(lam_im * dt)
    den = lam_re * lam_re + lam_im * lam_im
    nr = ab_re - 1.0
    f_re = (nr * lam_re + ab_im * lam_im) / den
    f_im = (ab_im * lam_re - nr * lam_im) / den
    bb_re = f_re[..., None] * b_re - f_im[..., None] * b_im
    bb_im = f_re[..., None] * b_im + f_im[..., None] * b_re
    return ab_re, ab_im, bb_re, bb_im


def s5_prep(bb_re, bb_im, c_re, c_im, ar, ai, T, comm=None):
    W = ar.shape[1]

    def body(bbr_ref, bbi_ref, cre_ref, cim_ref, ar_ref, ai_ref, btr_ref, bti_ref, ctr_ref, cti_ref, fr_ref, fi_ref,
             rr_ref, ri_ref):
        for ref in (btr_ref, bti_ref, ctr_ref, cti_ref):
            ref[...] = jnp.zeros_like(ref)
        for g in range(8):
            rows, cols = slice(g * SSM_P, (g + 1) * SSM_P), slice(g * SSM_H, (g + 1) * SSM_H)
            btr_ref[0, rows, cols] = bbr_ref[g]
            bti_ref[0, rows, cols] = bbi_ref[g]
            ctr_ref[0, cols, rows] = cre_ref[g]
            cti_ref[0, cols, rows] = -cim_ref[g]
        fr_ref[0:1, :] = ar_ref[...]
        fi_ref[0:1, :] = ai_ref[...]
        rr_ref[T - 1:T, :] = ar_ref[...]
        ri_ref[T - 1:T, :] = ai_ref[...]
        n = 1
        while n < T:
            cr, ci = fr_ref[0:n, :], fi_ref[0:n, :]
            lr, li = fr_ref[n - 1:n, :], fi_ref[n - 1:n, :]
            fr_ref[n:2 * n, :] = cr * lr - ci * li
            fi_ref[n:2 * n, :] = cr * li + ci * lr
            cr, ci = rr_ref[T - n:T, :], ri_ref[T - n:T, :]
            rr_ref[T - 2 * n:T - n, :] = cr * lr - ci * li
            ri_ref[T - 2 * n:T - n, :] = cr * li + ci * lr
            n *= 2

    spec = pl.BlockSpec((T, SSM_WC), lambda j: (0, j))
    aspec = pl.BlockSpec((1, SSM_WC), lambda j: (0, j))
    bspec, cspec = pl.BlockSpec((8, SSM_P, SSM_H), lambda j: (j, 0, 0)), pl.BlockSpec((8, SSM_H, SSM_P), lambda j: (j, 0, 0))
    btspec = pl.BlockSpec((1, SSM_WC, 128), lambda j: (j, 0, 0))
    ctspec = pl.BlockSpec((1, 128, SSM_WC), lambda j: (j, 0, 0))
    return carried(
        body, comm, grid=(W // SSM_WC,), in_specs=[bspec, bspec, cspec, cspec, aspec, aspec],
        out_specs=[btspec, btspec, ctspec, ctspec] + [spec] * 4,
        out_shape=[jax.ShapeDtypeStruct((8, SSM_WC, 128), F32)] * 2 + [jax.ShapeDtypeStruct((8, 128, SSM_WC), F32)] * 2
        + [jax.ShapeDtypeStruct((T, W), F32)] * 4,
        semantics=("parallel",), name='a_prep')(bb_re, bb_im, c_re, c_im, ar, ai)


def s5_untile(dbtr, dbti, dctr, dcti):
    def body(dbtr_ref, dbti_ref, dctr_ref, dcti_ref, br_ref, bi_ref, cr_ref, ci_ref):
        for g in range(8):
            rows, cols = slice(g * SSM_P, (g + 1) * SSM_P), slice(g * SSM_H, (g + 1) * SSM_H)
            br_ref[g] = dbtr_ref[0, rows, cols]
            bi_ref[g] = dbti_ref[0, rows, cols]
            cr_ref[g] = dctr_ref[0, cols, rows]
            ci_ref[g] = -dcti_ref[0, cols, rows]

    bspec, cspec = pl.BlockSpec((8, SSM_P, SSM_H), lambda j: (j, 0, 0)), pl.BlockSpec((8, SSM_H, SSM_P), lambda j: (j, 0, 0))
    btspec = pl.BlockSpec((1, SSM_WC, 128), lambda j: (j, 0, 0))
    ctspec = pl.BlockSpec((1, 128, SSM_WC), lambda j: (j, 0, 0))
    return pl.pallas_call(
        body, grid=(8,), in_specs=[btspec, btspec, ctspec, ctspec], out_specs=[bspec, bspec, cspec, cspec],
        out_shape=[jax.ShapeDtypeStruct((SSM_G, SSM_P, SSM_H), F32)] * 2 + [jax.ShapeDtypeStruct((SSM_G, SSM_H, SSM_P), F32)] * 2,
        compiler_params=pltpu.CompilerParams(dimension_semantics=("parallel",)), name='a_untile')(dbtr, dbti, dctr, dcti)


def layer_a_fwd(h, w, p, comm=None, on_carried=None, prep_comm=None, on_prep=None):
    L = h.shape[0]
    disc = lambda *a: s5_discretize(*a)
    (ab_re, ab_im, bb_re, bb_im), disc_vjp = jax.vjp(disc, p['a_lam_re'][0], p['a_lam_im'][0], p['a_log_dt'][0],
                                                     p['a_b_re'][0], p['a_b_im'][0])
    T = min(SSM_T, L)
    (Bre, Bim, Cre, Cim, pr, pi, prr, pir), prepped = s5_prep(bb_re, bb_im, p['a_c_re'][0], p['a_c_im'][0],
                                                              ab_re.reshape(1, -1), ab_im.reshape(1, -1), T,
                                                              comm=prep_comm)
    if on_prep is not None:
        on_prep(prepped)
    proj = mm(h, w['a_w_in'], 'nn', 'a_proj')
    (y, yg, s_re, s_im), carried_out = s5_fwd(proj, p['a_d'], Bre, Bim, Cre, Cim, pr, pi, comm=comm)
    if on_carried is not None:
        on_carried(carried_out)
    gl = mm(yg, w['a_w_glu'], 'nn', 'a_glu')

    def f2(yg_, gl_, z, bg):
        return [yg_ * sigmoid(gl_ + bg) * silu(z)], []
    (po,), _ = rowwise(f2, [rw(yg), rw(gl), rw(proj, 1024, 1)], [p['a_b_glu']], [(1024, BF16)], [], 256, 'a_gate')
    yb = mm(po, w['a_w_out'], 'nn', 'a_out')
    saved = dict(carried=carried_out, h=h, proj=proj, disc_vjp=disc_vjp, Bre=Bre, Bim=Bim, Cre=Cre, Cim=Cim, prr=prr, pir=pir, s_re=s_re,
                 s_im=s_im, y=y, yg=yg, gl=gl, po=po)
    return yb, saved


def _dw(g, sink, name, a, b, mm_name):
    if sink is None:
        g[name] = mm(a, b, 'tn', mm_name)
    else:
        sink.put(name, a, b, mm_name)


def layer_a_bwd(dyb, w, p, sv, comm=None, sink=None):
    g = {}
    dpo = mm(dyb, w['a_w_out'], 'nt', 'a_dpo')
    _dw(g, sink, 'a_w_out', sv['po'], dyb, 'a_dwout')
    proj = sv['proj']

    def f1(dpo_, yg, gl, z, bg):
        sg = sigmoid(gl + bg)
        sz = silu(z)
        dm = dpo_ * sz
        dz = dpo_ * (yg * sg) * silu_grad(z)
        dgl = dm * yg * sg * (1.0 - sg)
        return [dz, dm * sg, dgl], [jnp.sum(dgl, axis=0, keepdims=True)]
    (dz, dyg1, dgl), (db_glu,) = rowwise(f1, [rw(dpo), rw(sv['yg']), rw(sv['gl']), rw(proj, 1024, 1)], [p['a_b_glu']],
                                          [(1024, F32), (1024, F32), (1024, BF16)], [(1, 1024)], 256, 'a_gate_bwd')
    g['a_b_glu'] = db_glu
    _dw(g, sink, 'a_w_glu', sv['yg'], dgl, 'a_dwglu')
    dyg2 = mm(dgl, w['a_w_glu'], 'nt', 'a_dyg2')

    if callable(comm):
        comm = comm()
    (du, dd, dBre, dBim, dCre, dCim, da_re, da_im), g['carried'] = s5_bwd(
        proj, dyg1, dyg2, sv['y'], p['a_d'], sv['s_re'], sv['s_im'], sv['Bre'], sv['Bim'], sv['Cre'], sv['Cim'],
        sv['prr'], sv['pir'], comm=comm)
    g['a_d'] = dd

    def f3(du_, dz_):
        return [jnp.concatenate([du_, dz_], axis=1)], []
    (dproj,), _ = rowwise(f3, [rw(du), rw(dz)], [], [(2048, BF16)], [], 256, 'a_dproj')
    dbb_re, dbb_im, dc_re, dc_im = s5_untile(dBre, dBim, dCre, dCim)
    dlr, dli, dldt, dbr, dbi = sv['disc_vjp']((da_re.reshape(SSM_G, SSM_P), da_im.reshape(SSM_G, SSM_P), dbb_re, dbb_im))
    g['a_lam_re'], g['a_lam_im'], g['a_log_dt'] = dlr[None], dli[None], dldt[None]
    g['a_b_re'], g['a_b_im'] = dbr[None], dbi[None]
    g['a_c_re'], g['a_c_im'] = dc_re[None], dc_im[None]
    _dw(g, sink, 'a_w_in', sv['h'], dproj, 'a_dwin')
    if sink is None:
        dh = mm(dproj, w['a_w_in'], 'nt', 'a_dh')
    else:
        dh, (g['land_a1'],) = mm(dproj, w['a_w_in'], 'nt', 'a_dh', comm=SiblingExchange(sink.bufs['a1']))
    return dh, g


def _t5_bucket_np():
    qi = np.arange(WINDOW)[:, None]
    kj = np.arange(2 * WINDOW)[None, :]
    dist = np.maximum(qi + WINDOW - kj, 0)
    max_exact = REL_BUCKETS // 2
    dist_f = np.maximum(dist, 1).astype(np.float32)
    large = max_exact + (np.log(dist_f / np.float32(max_exact)) / np.float32(math.log(REL_MAX_DIST / max_exact))
                         * np.float32(REL_BUCKETS - max_exact)).astype(np.int32)
    large = np.minimum(large, REL_BUCKETS - 1)
    return np.where(dist < max_exact, dist, large).astype(np.int32)


SWA_GRP = SWA_HEADS // SWA_KV


def _swa_kv(kvp, kvc, kvh):
    kb = jnp.concatenate([kvp[:, kvh * 64:(kvh + 1) * 64], kvc[:, kvh * 64:(kvh + 1) * 64]], 0).astype(BF16)
    vb = jnp.concatenate([kvp[:, 128 + kvh * 64:128 + (kvh + 1) * 64], kvc[:, 128 + kvh * 64:128 + (kvh + 1) * 64]],
                         0).astype(BF16)
    return kb, vb


def _swa_stack(x, kvh):
    return jnp.concatenate([x[:, (kvh * SWA_GRP + g) * 64:(kvh * SWA_GRP + g + 1) * 64] for g in range(SWA_GRP)],
                           axis=0).astype(BF16)


def _swa_key_major(bias):
    return bias.reshape(SWA_KV, SWA_GRP, WINDOW, 2 * WINDOW).transpose(0, 3, 1, 2).reshape(SWA_KV, 2 * WINDOW,
                                                                                           SWA_GRP * WINDOW)


def _swa_probs(q8, kb, bias_h, sink_ref, kvh, valid):
    s = lax.dot_general(kb, q8, _NT, preferred_element_type=F32) * (HEAD_DIM ** -0.5)
    s = jnp.where(valid, s + bias_h, NEG_INF)
    sink = jnp.concatenate([jnp.broadcast_to(sink_ref[0:1, kvh * SWA_GRP + g:kvh * SWA_GRP + g + 1], (1, WINDOW))
                            for g in range(SWA_GRP)], axis=1)
    m = jnp.maximum(jnp.max(s, axis=0, keepdims=True), sink)
    e = jnp.exp(s - m)
    es = jnp.exp(sink - m)
    inv = 1.0 / (jnp.sum(e, axis=0, keepdims=True) + es)
    return e * inv, es * inv


def _swa_valid(n):
    kj = lax.broadcasted_iota(jnp.int32, (2 * WINDOW, SWA_GRP * WINDOW), 0)
    qi = lax.broadcasted_iota(jnp.int32, (2 * WINDOW, SWA_GRP * WINDOW), 1) & (WINDOW - 1)
    dist = qi + WINDOW - kj
    return (dist >= 0) & (dist < WINDOW) & ((kj >= WINDOW) | (n > 0))


def swa_fwd(proj, bias, sinks, comm=None):
    L = proj.shape[0]

    def body(z_ref, q_ref, kvc_ref, kvp_ref, bias_ref, sink_ref, o_ref, po_ref):
        valid = _swa_valid(pl.program_id(0))
        q, kvc, kvp = q_ref[...], kvc_ref[...], kvp_ref[...]
        outs = []
        for kvh in range(SWA_KV):
            kb, vb = _swa_kv(kvp, kvc, kvh)
            p, _ = _swa_probs(_swa_stack(q, kvh), kb, bias_ref[kvh], sink_ref, kvh, valid)
            o8 = lax.dot_general(p.astype(BF16), vb, _TN, preferred_element_type=F32)
            outs += [o8[g * WINDOW:(g + 1) * WINDOW] for g in range(SWA_GRP)]
        o = jnp.concatenate(outs, axis=1)
        o_ref[...] = o
        po_ref[...] = (o * silu(z_ref[...])).astype(po_ref.dtype)

    return carried(
        body, comm, grid=(L // WINDOW,),
        in_specs=[pl.BlockSpec((WINDOW, 1024), lambda n: (n, 0)), pl.BlockSpec((WINDOW, 1024), lambda n: (n, 1)),
                  pl.BlockSpec((WINDOW, 256), lambda n: (n, 8)),
                  pl.BlockSpec((WINDOW, 256), lambda n: (jnp.maximum(n - 1, 0), 8)),
                  pl.BlockSpec((SWA_KV, 2 * WINDOW, SWA_GRP * WINDOW), lambda n: (0, 0, 0)),
                  pl.BlockSpec((1, SWA_HEADS), lambda n: (0, 0))],
        out_specs=[pl.BlockSpec((WINDOW, 1024), lambda n: (n, 0))] * 2,
        out_shape=[jax.ShapeDtypeStruct((L, 1024), F32), jax.ShapeDtypeStruct((L, 1024), BF16)],
        semantics=("parallel",), name='b_attn')(proj, proj, proj, proj, _swa_key_major(bias), sinks)


def swa_bwd(proj, do, bias, sinks, comm=None):
    L = proj.shape[0]

    def body(q_ref, kvc_ref, kvp_ref, do_ref, bias_ref, sink_ref, dq_ref, dkv_ref, dbias_ref, dsink_ref):
        n = pl.program_id(0)

        @pl.when(n == 0)
        def _():
            dkv_ref[...] = jnp.zeros_like(dkv_ref)
            dbias_ref[...] = jnp.zeros_like(dbias_ref)
            dsink_ref[...] = jnp.zeros_like(dsink_ref)

        valid = _swa_valid(n)
        q, kvc, kvp, do_ = q_ref[...], kvc_ref[...], kvp_ref[...], do_ref[...]
        dqs, dks, dvs, dsk = [], [], [], []
        for kvh in range(SWA_KV):
            kb, vb = _swa_kv(kvp, kvc, kvh)
            q8, do8 = _swa_stack(q, kvh), _swa_stack(do_, kvh)
            p, ps = _swa_probs(q8, kb, bias_ref[kvh], sink_ref, kvh, valid)
            dp = lax.dot_general(vb, do8, _NT, preferred_element_type=F32)
            delta = jnp.sum(p * dp, axis=0, keepdims=True)
            ds = p * (dp - delta)
            col = -ps * delta
            dsk += [jnp.sum(col[:, g * WINDOW:(g + 1) * WINDOW], axis=1, keepdims=True) for g in range(SWA_GRP)]
            dbias_ref[kvh] += ds
            dsb = (ds * (HEAD_DIM ** -0.5)).astype(BF16)
            dq8 = lax.dot_general(dsb, kb, _TN, preferred_element_type=F32)
            dqs += [dq8[g * WINDOW:(g + 1) * WINDOW] for g in range(SWA_GRP)]
            dks.append(jnp.dot(dsb, q8, preferred_element_type=F32))
            dvs.append(jnp.dot(p.astype(BF16), do8, preferred_element_type=F32))
        dq_ref[...] = jnp.concatenate(dqs, axis=1)
        dsink_ref[...] += jnp.concatenate(dsk, axis=1)
        both = jnp.concatenate(dks + dvs, axis=1)
        r_cur = pl.multiple_of(n * WINDOW, WINDOW)
        r_prev = pl.multiple_of(jnp.maximum(n - 1, 0) * WINDOW, WINDOW)
        dkv_ref[pl.ds(r_prev, WINDOW), :] += both[:WINDOW]
        dkv_ref[pl.ds(r_cur, WINDOW), :] += both[WINDOW:]

    key_major = pl.BlockSpec((SWA_KV, 2 * WINDOW, SWA_GRP * WINDOW), lambda n: (0, 0, 0))
    (dq, dkv, dbias_t, dsinks), got = carried(
        body, comm, grid=(L // WINDOW,),
        in_specs=[pl.BlockSpec((WINDOW, 1024), lambda n: (n, 1)), pl.BlockSpec((WINDOW, 256), lambda n: (n, 8)),
                  pl.BlockSpec((WINDOW, 256), lambda n: (jnp.maximum(n - 1, 0), 8)),
                  pl.BlockSpec((WINDOW, 1024), lambda n: (n, 0)), key_major,
                  pl.BlockSpec((1, SWA_HEADS), lambda n: (0, 0))],
        out_specs=[pl.BlockSpec((WINDOW, 1024), lambda n: (n, 0)), pl.BlockSpec((L, 256), lambda n: (0, 0)), key_major,
                   pl.BlockSpec((1, SWA_HEADS), lambda n: (0, 0))],
        out_shape=[jax.ShapeDtypeStruct((L, 1024), F32), jax.ShapeDtypeStruct((L, 256), F32),
                   jax.ShapeDtypeStruct((SWA_KV, 2 * WINDOW, SWA_GRP * WINDOW), F32),
                   jax.ShapeDtypeStruct((1, SWA_HEADS), F32)],
        semantics=("arbitrary",), name='b_attn_bwd')(proj, proj, proj, do, _swa_key_major(bias), sinks)
    dbias = dbias_t.reshape(SWA_KV, 2 * WINDOW, SWA_GRP, WINDOW).transpose(0, 2, 3, 1).reshape(SWA_HEADS, WINDOW, 2 * WINDOW)
    return (dq, dkv, dbias, dsinks), got


def swa_bias(rel_bias):
    def body(bk_ref, rb_ref, o_ref):
        bk = bk_ref[...]
        for h in range(SWA_HEADS):
            acc = jnp.zeros((WINDOW, 2 * WINDOW), F32)
            for b in range(REL_BUCKETS):
                acc = jnp.where(bk == b, rb_ref[b, h], acc)
            o_ref[h] = acc

    return pl.pallas_call(
        body, out_shape=jax.ShapeDtypeStruct((SWA_HEADS, WINDOW, 2 * WINDOW), F32),
        in_specs=[pl.BlockSpec(memory_space=pltpu.VMEM), pl.BlockSpec(memory_space=pltpu.SMEM)],
        out_specs=pl.BlockSpec(memory_space=pltpu.VMEM), name='b_bias')(jnp.asarray(_t5_bucket_np()), rel_bias)


def layer_b_fwd(h, w, p, comm=None):
    proj = mm(h, w['b_w_in'], 'nn', 'b_proj')
    bias = swa_bias(p['rel_bias'])
    (o, po), carried_out = swa_fwd(proj, bias, p['b_sinks'], comm=comm)
    yb = mm(po, w['b_w_out'], 'nn', 'b_out')
    return yb, dict(carried=carried_out, h=h, proj=proj, bias=bias, o=o, po=po)


def layer_b_bwd(dyb, w, p, sv, comm=None, sink=None):
    g = {}
    dpo = mm(dyb, w['b_w_out'], 'nt', 'b_dpo')
    _dw(g, sink, 'b_w_out', sv['po'], dyb, 'b_dwout')
    proj = sv['proj']

    def f1(dpo_, o, z):
        return [dpo_ * silu(z), dpo_ * o * silu_grad(z)], []
    (do, dz), _ = rowwise(f1, [rw(dpo), rw(sv['o']), rw(proj, 1024, 0)], [], [(1024, BF16), (1024, F32)], [], 256, 'b_gate_bwd')
    (dq, dkv, dbias, dsinks), g['carried'] = swa_bwd(proj, do, sv['bias'], p['b_sinks'], comm=comm)
    g['b_sinks'] = dsinks
    onehot = jnp.asarray(np.eye(REL_BUCKETS, dtype=np.float32)[_t5_bucket_np().reshape(-1)])

    def f2(db, oh):
        return [], [lax.dot_general(db, oh, (((1,), (0,)), ((), ())), preferred_element_type=F32,
                                    precision=lax.Precision.HIGHEST)]
    _, (drel,) = rowwise(f2, [(dbias.reshape(SWA_HEADS, -1), pl.BlockSpec((SWA_HEADS, 4096), lambda i: (0, i))),
                              (onehot, pl.BlockSpec((4096, REL_BUCKETS), lambda i: (i, 0)))], [], [],
                         [(SWA_HEADS, REL_BUCKETS)], 4096, 'b_drel', n_steps=(2 * WINDOW * WINDOW) // 4096)
    g['rel_bias'] = drel.T

    def f3(dz_, dq_, dkv_):
        return [jnp.concatenate([dz_, dq_, dkv_], axis=1)], []
    (dproj,), _ = rowwise(f3, [rw(dz), rw(dq), rw(dkv)], [], [(2304, BF16)], [], 256, 'b_dproj')
    _dw(g, sink, 'b_w_in', sv['h'], dproj, 'b_dwin')
    dh = mm(dproj, w['b_w_in'], 'nt', 'b_dh')
    return dh, g


MLA_SCALE = (MLA_NOPE + MLA_ROPE) ** -0.5
_LOG2E = math.log2(math.e)


def _rope_tables(L):
    inv = ROPE_BASE ** (-jnp.arange(0, MLA_ROPE, 2, dtype=F32) / MLA_ROPE)
    ang = jnp.arange(L, dtype=F32)[:, None] * inv[None, :]
    c, s = jnp.cos(ang), jnp.sin(ang)
    one, zero, pad = jnp.ones((L, 128), F32), jnp.zeros((L, 128), F32), jnp.zeros((L, 64), F32)
    return (jnp.concatenate([one, c, c, c, c, pad], 1), jnp.concatenate([zero, s, s, s, s, pad], 1))


def _rot(x, transpose=False):
    w = x.shape[1]
    lane = lax.broadcasted_iota(jnp.int32, x.shape, 1)
    up = pltpu.roll(x, w - 16, 1)
    dn = pltpu.roll(x, 16, 1)
    first = (lane % 32) < 16
    return jnp.where(first, up, -dn) if transpose else jnp.where(first, -up, dn)


MLA_QT = 512


def _mla_exp(qf, kf, t, qt):
    s = lax.dot_general(qf, kf, (((1,), (1,)), ((), ())), preferred_element_type=F32)
    causal = lax.broadcasted_iota(jnp.int32, (qt, qt), 1) <= lax.broadcasted_iota(jnp.int32, (qt, qt), 0)
    last = jnp.where(causal, s[:, t * qt:], NEG_INF)
    s = last if t == 0 else jnp.concatenate([s[:, :t * qt], last], axis=1)
    e = jnp.exp2((s - jnp.max(s, axis=-1, keepdims=True)) * (MLA_SCALE * _LOG2E))
    return e, jnp.sum(e, axis=-1, keepdims=True)


def _mla_heads(q, kv, kr):
    out = []
    for j in range(2):
        qf = jnp.concatenate([q[:, j * 64:(j + 1) * 64], q[:, 128 + j * 32:128 + (j + 1) * 32]], axis=1)
        kf = jnp.concatenate([kv[:, j * 64:(j + 1) * 64], kr], axis=1)
        out.append((qf, kf, kv[:, 128 + j * 64:128 + (j + 1) * 64]))
    return out


def mla_fwd(q, kv, kr, comm=None):
    L = q.shape[0]
    qt = min(MLA_QT, L)
    nq = L // qt

    def body(q_ref, kv_ref, kr_ref, o_ref):
        for t in range(nq):
            @pl.when(pl.program_id(1) == t)
            def _(t=t):
                n_k = (t + 1) * qt
                outs = []
                for qf, kf, v in _mla_heads(q_ref[...], kv_ref[0:n_k, :], kr_ref[0:n_k, 0:MLA_ROPE]):
                    e, den = _mla_exp(qf, kf, t, qt)
                    outs.append(jnp.dot(e.astype(BF16), v, preferred_element_type=F32) / den)
                o_ref[...] = jnp.concatenate(outs, axis=1)

    return carried(
        body, comm, grid=(MLA_HEADS // 2, nq),
        in_specs=[pl.BlockSpec((qt, 256), lambda hp, n: (n, hp)), pl.BlockSpec((L, 256), lambda hp, n: (0, hp)),
                  pl.BlockSpec((L, 128), lambda hp, n: (0, 0))],
        out_specs=pl.BlockSpec((qt, 128), lambda hp, n: (n, hp)), out_shape=jax.ShapeDtypeStruct((L, 1024), F32),
        semantics=("parallel", "parallel"), name='c_attn')(q, kv, kr)


def mla_bwd(q, kv, kr, do, o, comm=None):
    L = q.shape[0]
    qt = min(MLA_QT, L)
    nq = L // qt

    def body(q_ref, kv_ref, kr_ref, do_ref, o_ref, dq_ref, dkv_ref, dkr_ref):
        @pl.when(pl.program_id(1) == 0)
        def _():
            dkv_ref[...] = jnp.zeros_like(dkv_ref)
            dkr_ref[...] = jnp.zeros_like(dkr_ref)

        for t in range(nq):
            @pl.when(pl.program_id(1) == t)
            def _(t=t):
                n_k = (t + 1) * qt
                do_, o_ = do_ref[...], o_ref[...]
                dqn, dqr, dkn, dvs = [], [], [], []
                dkr = jnp.zeros((MLA_ROPE, n_k), F32)
                wide = lambda x: jnp.concatenate([x, jnp.zeros((qt, 128 - x.shape[1]), x.dtype)], axis=1)
                for j, (qf, kf, v) in enumerate(_mla_heads(q_ref[...], kv_ref[0:n_k, :], kr_ref[0:n_k, 0:MLA_ROPE])):
                    doh = do_[:, j * 64:(j + 1) * 64]
                    dof = doh.astype(F32)
                    e, den = _mla_exp(qf, kf, t, qt)
                    inv = 1.0 / den
                    dp = lax.dot_general(doh, v, (((1,), (1,)), ((), ())), preferred_element_type=F32)
                    delta = jnp.sum(dof * o_[:, j * 64:(j + 1) * 64], axis=-1, keepdims=True)
                    ds = (e * ((dp - delta) * (inv * MLA_SCALE))).astype(BF16)
                    dqf = jnp.dot(ds, kf, preferred_element_type=F32)
                    dkf = lax.dot_general(wide(qf), ds, _TN, preferred_element_type=F32)
                    dvf = lax.dot_general(wide((dof * inv).astype(BF16)), e.astype(BF16), _TN,
                                          preferred_element_type=F32)
                    dqn.append(dqf[:, :MLA_NOPE])
                    dqr.append(dqf[:, MLA_NOPE:])
                    dkn.append(dkf[:MLA_NOPE])
                    dvs.append(dvf[:MLA_V])
                    dkr = dkr + dkf[MLA_NOPE:MLA_NOPE + MLA_ROPE]
                dq_ref[...] = jnp.concatenate(dqn + dqr + [jnp.zeros((qt, 64), F32)], axis=1)
                dkv_ref[0:n_k, :] += jnp.concatenate(dkn + dvs, axis=0).T
                dkr_ref[0, 0:n_k, :] += jnp.concatenate([dkr, jnp.zeros((128 - MLA_ROPE, n_k), F32)], axis=0).T

    return carried(
        body, comm, grid=(MLA_HEADS // 2, nq),
        in_specs=[pl.BlockSpec((qt, 256), lambda hp, n: (n, hp)), pl.BlockSpec((L, 256), lambda hp, n: (0, hp)),
                  pl.BlockSpec((L, 128), lambda hp, n: (0, 0)), pl.BlockSpec((qt, 128), lambda hp, n: (n, hp)),
                  pl.BlockSpec((qt, 128), lambda hp, n: (n, hp))],
        out_specs=[pl.BlockSpec((qt, 256), lambda hp, n: (n, hp)), pl.BlockSpec((L, 256), lambda hp, n: (0, hp)),
                   pl.BlockSpec((1, L, 128), lambda hp, n: (hp, 0, 0))],
        out_shape=[jax.ShapeDtypeStruct((L, 2048), F32), jax.ShapeDtypeStruct((L, 2048), F32),
                   jax.ShapeDtypeStruct((MLA_HEADS // 2, L, 128), F32)],
        semantics=("parallel", "arbitrary"), name='c_attn_bwd')(q, kv, kr, do, o)


def layer_c_fwd(h, w, p, comm=None):
    L = h.shape[0]
    proj = mm(h, w['c_w_in'], 'nn', 'c_proj')

    def f1(c, gq, gk):
        return [rms_fwd(c[:, :768], gq), rms_fwd(c[:, 768:], gk)], []
    (cqn, ckvn), _ = rowwise(f1, [rw(proj, 1024, 1)], [p['c_q_norm'], p['c_kv_norm']], [(768, BF16), (256, BF16)], [],
                             256, 'c_norms')
    qf = mm(cqn, w['c_w_uq'], 'nn', 'c_uq')
    kvf = mm(ckvn, w['c_w_ukv'], 'nn', 'c_ukv', out_dtype=BF16)
    cos, sin = _rope_tables(L)

    def f2(q_, kr_, c, s):
        c8, s8 = jnp.tile(c, (1, 8)), jnp.tile(s, (1, 8))
        return [q_ * c8 + _rot(q_) * s8, kr_ * c[:, 128:] + _rot(kr_) * s[:, 128:]], []
    (q, kr), _ = rowwise(f2, [rw(qf), rw(proj, 128, 16), rw(cos), rw(sin)], [], [(2048, BF16), (128, BF16)], [], 256,
                         'c_rope')
    o, carried_out = mla_fwd(q, kvf, kr, comm=comm)

    def f3(o_, z):
        return [o_ * silu(z)], []
    (po,), _ = rowwise(f3, [rw(o), rw(proj, 1024, 0)], [], [(1024, BF16)], [], 256, 'c_gate')
    yb = mm(po, w['c_w_out'], 'nn', 'c_out')
    return yb, dict(carried=carried_out, h=h, proj=proj, cqn=cqn, ckvn=ckvn, q=q, kv=kvf, kr=kr, o=o, po=po, cos=cos, sin=sin)


def layer_c_bwd(dyb, w, p, sv, comm=None, sink=None):
    g = {}
    dpo = mm(dyb, w['c_w_out'], 'nt', 'c_dpo')
    _dw(g, sink, 'c_w_out', sv['po'], dyb, 'c_dwout')
    proj = sv['proj']
    L = proj.shape[0]

    def f1(dpo_, o, z):
        return [dpo_ * silu(z), dpo_ * o * silu_grad(z)], []
    (do, dz), _ = rowwise(f1, [rw(dpo), rw(sv['o']), rw(proj, 1024, 0)], [], [(1024, BF16), (1024, F32)], [], 256,
                          'c_gate_bwd')
    (dq, dkvf, dkr8), g['carried'] = mla_bwd(sv['q'], sv['kv'], sv['kr'], do, sv['o'], comm=comm)

    def f2(dq_, dkr_, c, s):
        c8, s8 = jnp.tile(c, (1, 8)), jnp.tile(s, (1, 8))
        dk = jnp.sum(dkr_, axis=0)
        return [dq_ * c8 + _rot(dq_ * s8, True), dk * c[:, 128:] + _rot(dk * s[:, 128:], True)], []
    tl = 256
    (dqf, dkr), _ = rowwise(f2, [rw(dq), (dkr8, pl.BlockSpec((8, tl, 128), lambda i: (0, i, 0))), rw(sv['cos']),
                                 rw(sv['sin'])], [], [(2048, BF16), (128, F32)], [], tl, 'c_rope_bwd')
    _dw(g, sink, 'c_w_uq', sv['cqn'], dqf, 'c_dwuq')
    _dw(g, sink, 'c_w_ukv', sv['ckvn'], dkvf, 'c_dwukv')
    dcqn = mm(dqf, w['c_w_uq'], 'nt', 'c_dcqn')
    dckvn = mm(dkvf, w['c_w_ukv'], 'nt', 'c_dckvn')

    def f3(c, dq_, dk_, dz_, dkr_, gq, gk):
        dcq, dgq = rms_bwd(c[:, :768], gq, dq_)
        dckv, dgk = rms_bwd(c[:, 768:], gk, dk_)
        return [jnp.concatenate([dz_, dcq, dckv, dkr_], axis=1)], [dgq, dgk]
    (dproj,), (dgq, dgk) = rowwise(f3, [rw(proj, 1024, 1), rw(dcqn), rw(dckvn), rw(dz), rw(dkr)],
                                   [p['c_q_norm'], p['c_kv_norm']], [(2176, BF16)], [(1, 768), (1, 256)], 256, 'c_dproj')
    g['c_q_norm'], g['c_kv_norm'] = dgq, dgk
    _dw(g, sink, 'c_w_in', sv['h'], dproj, 'c_dwin')
    dh = mm(dproj, w['c_w_in'], 'nt', 'c_dh')
    return dh, g


def _sgu_mix(wm, v, transpose):
    outs = []
    dims = (((0,), (0,)), ((), ())) if transpose else (((1,), (0,)), ((), ()))
    for gi in range(SGU_G):
        outs.append(lax.dot_general(wm[gi], v[:, gi * SGU_C:(gi + 1) * SGU_C].astype(BF16), dims,
                                    preferred_element_type=F32))
    return jnp.concatenate(outs, axis=1)


def _sgu_wmask(ws):
    t = lax.broadcasted_iota(jnp.int32, (SGU_T, SGU_T), 0)
    s = lax.broadcasted_iota(jnp.int32, (SGU_T, SGU_T), 1)
    return jnp.where((s <= t)[None], ws, 0.0).astype(BF16)


def _ln_stats(v):
    mu = jnp.mean(v, axis=-1, keepdims=True)
    vc = v - mu
    rstd = lax.rsqrt(jnp.mean(vc * vc, axis=-1, keepdims=True) + EPS)
    return vc * rstd, rstd


def layer_d_fwd(h, w, p):
    proj = mm(h, w['d_w_in'], 'nn', 'd_proj')
    bias = jnp.repeat(p['d_b_s'][0].T, SGU_C, axis=1)

    def f1(u_, v_, z, ws, lg, lb, bs):
        xh, _ = _ln_stats(gelu(v_))
        s = _sgu_mix(_sgu_wmask(ws), xh * lg + lb, False) + bs
        return [gelu(u_) * s * silu(z)], []
    (po,), _ = rowwise(f1, [rw(proj, 1024, 0), rw(proj, 1024, 1), rw(proj, 1024, 2)],
                       [p['d_w_s'][0], p['d_ln_g'], p['d_ln_b'], bias], [(1024, BF16)], [], SGU_T, 'd_mix')
    yb = mm(po, w['d_w_out'], 'nn', 'd_out')
    return yb, dict(h=h, proj=proj, po=po, bias=bias)


def layer_d_bwd(dyb, w, p, sv, sink=None):
    g = {}
    dpo = mm(dyb, w['d_w_out'], 'nt', 'd_dpo')
    _dw(g, sink, 'd_w_out', sv['po'], dyb, 'd_dwout')
    proj = sv['proj']

    def f1(dpo_, u_, v_, z, ws, lg, lb, bs):
        wm = _sgu_wmask(ws)
        gv = gelu(v_)
        xh, rstd = _ln_stats(gv)
        vn = xh * lg + lb
        s = _sgu_mix(wm, vn, False) + bs
        gu, sz = gelu(u_), silu(z)
        du = dpo_ * s * sz
        ds = dpo_ * gu * sz
        dz = dpo_ * gu * s * silu_grad(z)
        dsb = ds.astype(BF16)
        dws = jnp.stack([lax.dot_general(dsb[:, gi * SGU_C:(gi + 1) * SGU_C], vn[:, gi * SGU_C:(gi + 1) * SGU_C].astype(BF16),
                                         (((1,), (1,)), ((), ())), preferred_element_type=F32) for gi in range(SGU_G)])
        dvn = _sgu_mix(wm, ds, True)
        dlg = jnp.sum(dvn * xh, axis=0, keepdims=True)
        dlb = jnp.sum(dvn, axis=0, keepdims=True)
        dxh = dvn * lg
        dgv = rstd * (dxh - jnp.mean(dxh, axis=-1, keepdims=True) - xh * jnp.mean(dxh * xh, axis=-1, keepdims=True))
        return ([jnp.concatenate([du * gelu_grad(u_), dgv * gelu_grad(v_), dz], axis=1)], [dws, ds, dlg, dlb])
    (dproj,), (dws, dbs, dlg, dlb) = rowwise(
        f1, [rw(dpo), rw(proj, 1024, 0), rw(proj, 1024, 1), rw(proj, 1024, 2)],
        [p['d_w_s'][0], p['d_ln_g'], p['d_ln_b'], sv['bias']], [(3072, BF16)],
        [(SGU_G, SGU_T, SGU_T), (SGU_T, 1024), (1, 1024), (1, 1024)], SGU_T, 'd_mix_bwd')
    tril = np.tril(np.ones((SGU_T, SGU_T), dtype=bool))
    g['d_w_s'] = jnp.where(tril[None], dws, 0.0)[None]
    g['d_b_s'] = dbs.reshape(SGU_T, SGU_G, SGU_C).sum(-1).T[None]
    g['d_ln_g'], g['d_ln_b'] = dlg, dlb
    _dw(g, sink, 'd_w_in', sv['h'], dproj, 'd_dwin')
    dh = mm(dproj, w['d_w_in'], 'nt', 'd_dh')
    return dh, g


def _coords():
    return lax.axis_index("x"), lax.axis_index("y"), lax.axis_index("c")


class AllGather:
    def __init__(self, x):
        self.ins = [x]
        self.outs = [jax.ShapeDtypeStruct((N_DEV,) + x.shape, x.dtype)]
        self.scratch = [pltpu.SemaphoreType.DMA((7,)), pltpu.SemaphoreType.DMA((7,)), pltpu.SemaphoreType.DMA(())]

    def hooks(self, n_steps):
        return [(0, functools.partial(self.phase, 0), False), (n_steps - 1, functools.partial(self.phase, 1), True),
                (n_steps - 1, functools.partial(self.phase, 2), True)]

    @staticmethod
    def phase(which, ins, outs, scratch):
        (x_ref,), (out_ref,), (send_sems, recv_sems, local_sem) = ins, outs, scratch
        x_, y_, c_ = _coords()
        me, sibling = (x_, y_, c_), (x_, y_, 1 - c_)
        chips = [(1 - x_, y_), (x_, 1 - y_), (1 - x_, 1 - y_)]

        def slot(px, py, pc):
            return out_ref.at[4 * px + 2 * py + pc]

        def copy(k, block, to, src=None):
            return pltpu.make_async_remote_copy(src_ref=slot(*block) if src is None else src, dst_ref=slot(*block),
                                                send_sem=send_sems.at[k], recv_sem=recv_sems.at[k], device_id=to,
                                                device_id_type=MESH)

        mine = pltpu.make_async_copy(x_ref, slot(*me), local_sem)
        first = [copy(0, me, sibling, src=x_ref)]
        first += [copy(1 + j, me, (*chip, c_), src=x_ref) for j, chip in enumerate(chips)]
        passed = [copy(4 + j, (*chip, c_), sibling) for j, chip in enumerate(chips)]
        if which == 0:
            mine.start()
            for cp in first:
                cp.start()
        elif which == 1:
            for j, chip in enumerate(chips):
                copy(1 + j, (*chip, c_), me).wait_recv()
                passed[j].start()
        else:
            copy(0, sibling, me).wait_recv()
            for j, chip in enumerate(chips):
                copy(4 + j, (*chip, 1 - c_), me).wait_recv()
            for cp in first + passed:
                cp.wait_send()
            mine.wait()


class ChipExchange:
    def __init__(self, part):
        self.ins = [part]
        self.outs = [jax.ShapeDtypeStruct((3,) + part.shape[1:], part.dtype)]
        self.scratch = [pltpu.SemaphoreType.DMA((3,)), pltpu.SemaphoreType.DMA((3,))]

    def hooks(self, n_steps):
        return [(0, functools.partial(self.phase, 0), False), (n_steps - 1, functools.partial(self.phase, 1), True)]

    @staticmethod
    def phase(which, ins, outs, scratch):
        (p_ref,), (land_ref,), (send_sems, recv_sems) = ins, outs, scratch
        x_, y_, c_ = _coords()
        copies = []
        for r, (fx, fy) in enumerate([(1, 0), (0, 1), (1, 1)]):
            tx = jnp.where(fx == 1, 1 - x_, x_)
            ty = jnp.where(fy == 1, 1 - y_, y_)
            copies.append(pltpu.make_async_remote_copy(src_ref=p_ref.at[2 * tx + ty], dst_ref=land_ref.at[r],
                                                       send_sem=send_sems.at[r], recv_sem=recv_sems.at[r],
                                                       device_id=(tx, ty, c_), device_id_type=MESH))
        if which == 0:
            for cp in copies:
                cp.start()
        else:
            for cp in copies:
                cp.wait_recv()
            for cp in copies:
                cp.wait_send()


class Both:
    def __init__(self, a, b):
        self.parts = (a, b)
        self.ins, self.outs, self.scratch = a.ins + b.ins, a.outs + b.outs, a.scratch + b.scratch

    def hooks(self, n_steps):
        res, oi, oo, osc = [], 0, 0, 0
        for p in self.parts:
            sl = (slice(oi, oi + len(p.ins)), slice(oo, oo + len(p.outs)), slice(osc, osc + len(p.scratch)))
            res += [(at, functools.partial(self.sub, fn, sl), after) for at, fn, after in p.hooks(n_steps)]
            oi, oo, osc = oi + len(p.ins), oo + len(p.outs), osc + len(p.scratch)
        return res

    @staticmethod
    def sub(fn, sl, ins, outs, scratch):
        fn(ins[sl[0]], outs[sl[1]], scratch[sl[2]])


def run_comm(comm, name):
    def body(*refs):
        ci, co = len(comm.ins), len(comm.outs)
        for _, fn, _ in comm.hooks(1):
            fn(refs[:ci], refs[ci:ci + co], refs[ci + co:])

    return pl.pallas_call(body, out_shape=list(comm.outs), in_specs=[ANY] * len(comm.ins),
                          out_specs=[ANY] * len(comm.outs), scratch_shapes=list(comm.scratch), name=name)(*comm.ins)


def all_gather(x, name):
    return run_comm(AllGather(x), name)[0]


class SiblingExchange:
    def __init__(self, gfull):
        self.ins = [gfull]
        self.outs = [jax.ShapeDtypeStruct((4,) + gfull.shape[1:], gfull.dtype)]
        self.scratch = [pltpu.SemaphoreType.DMA((4,)), pltpu.SemaphoreType.DMA((4,))]

    def hooks(self, n_steps):
        return [(0, functools.partial(self.phase, 0), False), (n_steps - 1, functools.partial(self.phase, 1), True)]

    @staticmethod
    def phase(which, ins, outs, scratch):
        (g_ref,), (land_ref,), (send_sems, recv_sems) = ins, outs, scratch
        x_, y_, c_ = _coords()
        copies = [pltpu.make_async_remote_copy(src_ref=g_ref.at[2 * k + 1 - c_], dst_ref=land_ref.at[k],
                                               send_sem=send_sems.at[k], recv_sem=recv_sems.at[k],
                                               device_id=(x_, y_, 1 - c_), device_id_type=MESH) for k in range(4)]
        if which == 0:
            for cp in copies:
                cp.start()
        else:
            for cp in copies:
                cp.wait_recv()
            for cp in copies:
                cp.wait_send()


def rs_sibling(gfull, tag):
    return run_comm(SiblingExchange(gfull), 'rs_sibling_' + tag)[0]


def rs_pair_add(gfull, land, core, tag):
    _, R, C = gfull.shape
    tl = R

    def body(c_ref, g_ref, l_ref, o_ref):
        o_ref[...] = (g_ref[...].astype(F32) + l_ref[...].astype(F32)).astype(BF16)

    return pl.pallas_call(
        body, out_shape=jax.ShapeDtypeStruct((4, R, C), BF16),
        grid_spec=pltpu.PrefetchScalarGridSpec(
            num_scalar_prefetch=1, grid=(4, R // tl),
            in_specs=[pl.BlockSpec((1, tl, C), lambda k, i, c: (2 * k + c[0], i, 0)),
                      pl.BlockSpec((1, tl, C), lambda k, i, c: (k, i, 0))],
            out_specs=pl.BlockSpec((1, tl, C), lambda k, i, c: (k, i, 0))),
        compiler_params=pltpu.CompilerParams(dimension_semantics=("parallel", "parallel")), name='rs_pair_add_' + tag)(
            core, gfull, land)


def rs_chips(part, tag):
    return run_comm(ChipExchange(part), 'rs_chips_' + tag)[0]


def _adam(wv, gv, mv, vv):
    m = ADAM_B1 * mv + (1.0 - ADAM_B1) * gv
    v = ADAM_B2 * vv + (1.0 - ADAM_B2) * (gv * gv)
    m_hat = m / (1.0 - ADAM_B1 ** ADAM_STEP)
    v_hat = v / (1.0 - ADAM_B2 ** ADAM_STEP)
    delta = -ADAM_LR * (m_hat / (jnp.sqrt(v_hat) + ADAM_EPS) + ADAM_WD * wv)
    return delta, m, v


def _sum4(p_ref, l_ref):
    return ((p_ref[0].astype(F32) + l_ref[0].astype(F32)) + l_ref[1].astype(F32)) + l_ref[2].astype(F32)


def rs_rep_sum(part, land, chip):
    def body(c_ref, p_ref, l_ref, o_ref):
        o_ref[...] = _sum4(p_ref, l_ref).astype(BF16)

    return pl.pallas_call(
        body, out_shape=jax.ShapeDtypeStruct((REP_SLOT, LANES), BF16),
        grid_spec=pltpu.PrefetchScalarGridSpec(
            num_scalar_prefetch=1, grid=(1,),
            in_specs=[pl.BlockSpec((1, REP_SLOT, LANES), lambda i, c: (c[0], 0, 0)),
                      pl.BlockSpec((3, REP_SLOT, LANES), lambda i, c: (0, 0, 0))],
            out_specs=pl.BlockSpec((REP_SLOT, LANES), lambda i, c: (0, 0))),
        compiler_params=pltpu.CompilerParams(dimension_semantics=("parallel",)), name='rs_rep')(chip, part, land)


def adam_param(name, shape, off, w, m, v, chip, part=None, land=None, grep=None, fold=1):
    r, c = shape
    rp, nt, rb = _tiles((r // fold, c * fold))
    rbw = min(r, rb) if fold == 1 else r
    n_src = 2 if grep is None else 1
    ns = w.shape
    assert int(np.prod(ns[:-1])) == r and ns[-1] == c and (fold == 1 or (rb == rp and nt == 1))
    if fold > 1:
        nat_block, nat_map = ns, lambda i, cr: (0,) * len(ns)
    elif len(ns) == 2:
        nat_block, nat_map = (rbw, c), lambda i, cr: (i, 0)
    elif int(np.prod(ns[:-2])) == 1:
        nat_block, nat_map = (1,) * (len(ns) - 2) + (rbw, c), lambda i, cr: (0,) * (len(ns) - 2) + (i, 0)
    else:
        assert len(ns) == 4 and ns[0] == 1 and rbw % ns[2] == 0
        nat_block, nat_map = (1, rbw // ns[2], ns[2], c), lambda i, cr: (0, i, 0, 0)

    def body(c_ref, *refs):
        srcs = refs[:n_src * nt]
        w_ref, m_ref, v_ref, g_ref, d_ref, nm_ref, nv_ref = refs[n_src * nt:]
        if grep is None:
            tiles = [_sum4(srcs[2 * t], srcs[2 * t + 1]) for t in range(nt)]
        else:
            tiles = [srcs[t][...].astype(F32) for t in range(nt)]
        if fold > 1:
            g = jnp.concatenate([tiles[0][:, q * c:(q + 1) * c] for q in range(fold)], axis=0)
        else:
            g = (tiles[0] if nt == 1 else jnp.concatenate(tiles, axis=1))[:rbw, :c]
        g_ref[...] = g.reshape(nat_block)
        res = _adam(w_ref[...].reshape(rbw, c), g, m_ref[...].reshape(rbw, c), v_ref[...].reshape(rbw, c))
        for ref, val in zip((d_ref, nm_ref, nv_ref), res):
            ref[...] = val.reshape(nat_block)

    in_specs, args = [], []
    for t in range(nt):
        b0 = (off + t * rp) // rb
        assert (off + t * rp) % rb == 0
        if grep is None:
            in_specs += [pl.BlockSpec((1, rb, LANES), functools.partial(lambda i, cr, b0: (cr[0], b0 + i, 0), b0=b0)),
                         pl.BlockSpec((3, rb, LANES), functools.partial(lambda i, cr, b0: (0, b0 + i, 0), b0=b0))]
            args += [part, land]
        else:
            in_specs.append(pl.BlockSpec((rb, LANES), functools.partial(lambda i, cr, b0: (b0 + i, 0), b0=b0)))
            args.append(grep)
    nat = pl.BlockSpec(nat_block, nat_map)
    return pl.pallas_call(
        body, out_shape=[jax.ShapeDtypeStruct(ns, F32)] * 4,
        grid_spec=pltpu.PrefetchScalarGridSpec(num_scalar_prefetch=1, grid=(rp // rb,), in_specs=in_specs + [nat] * 3,
                                               out_specs=[nat] * 4),
        compiler_params=pltpu.CompilerParams(dimension_semantics=("parallel",)), name='adam_' + name)(
            chip, *args, w, m, v)


def adam_small(names, grep, P, M, V):
    in_specs, args, out_specs, out_shape, meta = [], [], [], [], []
    for n in names:
        s = REP_SHAPE[n]
        rp, nt, _ = _tiles(s)
        ns = P[n].shape
        for t in range(nt):
            b0 = (REP_OFF[n] + t * rp) // rp
            assert (REP_OFF[n] + t * rp) % rp == 0
            in_specs.append(pl.BlockSpec((rp, LANES), functools.partial(lambda i, b0: (b0, 0), b0=b0)))
            args.append(grep)
        nat = pl.BlockSpec(ns, functools.partial(lambda i, nd: (0,) * nd, nd=len(ns)))
        in_specs += [nat] * 3
        args += [P[n], M[n], V[n]]
        out_specs += [nat] * 4
        out_shape += [jax.ShapeDtypeStruct(ns, F32)] * 4
        meta.append((s, nt, ns))
    n_in = len(in_specs)

    def body(*refs):
        ins, outs = refs[:n_in], refs[n_in:]
        k = 0
        for p, ((r, c), nt, ns) in enumerate(meta):
            tiles = [ins[k + t][...].astype(F32) for t in range(nt)]
            w_ref, m_ref, v_ref = ins[k + nt:k + nt + 3]
            k += nt + 3
            g = (tiles[0] if nt == 1 else jnp.concatenate(tiles, axis=1))[:r, :c]
            res = (g,) + _adam(w_ref[...].reshape(r, c), g, m_ref[...].reshape(r, c), v_ref[...].reshape(r, c))
            for ref, val in zip(outs[4 * p:4 * p + 4], res):
                ref[...] = val.reshape(ns)

    res = pl.pallas_call(body, grid=(1,), in_specs=in_specs, out_specs=out_specs, out_shape=out_shape,
                         compiler_params=pltpu.CompilerParams(dimension_semantics=("arbitrary",)), name='adam_small')(*args)
    return {n: tuple(res[4 * p:4 * p + 4]) for p, n in enumerate(names)}


VM = pl.BlockSpec(memory_space=pltpu.VMEM)


def _tile_value(w, t, rp):
    r, c = w.shape
    wt = min(LANES, c - t * LANES)
    tile = w[:, t * LANES:t * LANES + wt]
    if wt < LANES:
        tile = jnp.concatenate([tile, jnp.zeros((r, LANES - wt), tile.dtype)], axis=1)
    if rp > r:
        tile = jnp.concatenate([tile, jnp.zeros((rp - r, LANES), tile.dtype)], axis=0)
    return tile


def pack_layer(layer, blocks):
    names = LAYER_PARAMS[layer]

    def body(*refs):
        tiles = []
        for ref, n in zip(refs[:-1], names):
            rp, nt, _ = _tiles(_block_shape(n))
            w = ref[...].reshape(_block_shape(n))
            tiles += [_tile_value(w, t, rp) for t in range(nt)]
        refs[-1][...] = jnp.concatenate(tiles, axis=0).astype(BF16)

    return pl.pallas_call(body, out_shape=jax.ShapeDtypeStruct((LAYER_ROWS[layer], LANES), BF16),
                          in_specs=[VM] * len(names), out_specs=VM, name='pack_' + layer)(*[blocks[n] for n in names])


def assemble(name, gathered):
    (rf, cf), ax = SHARDED[name]
    r, c = _block_shape(name)
    rp, nt, _ = _tiles((r, c))
    off = SH_OFF[name]
    out_cols = cf if ax == 0 else len(perm_index(name))

    def body(g_ref, o_ref, buf, sem):
        cp = pltpu.make_async_copy(g_ref.at[:, pl.ds(off, nt * rp), :], buf, sem)
        cp.start()
        cp.wait()
        if ax == 0:
            for j in range(N_DEV):
                o_ref[j * r:(j + 1) * r, :] = jnp.concatenate([buf[j, t * rp:(t + 1) * rp, :] for t in range(nt)], axis=1)
            return
        pieces = []
        for p in PERM[name]:
            if p[0] == 'z':
                pieces.append(jnp.zeros((r, p[1]), BF16))
                continue
            n0, w = p
            while w > 0:
                j, cb = divmod(n0, c)
                t, lane = divmod(cb, LANES)
                wl = min(w, LANES - lane, c - cb)
                pieces.append(buf[j, t * rp:t * rp + r, lane:lane + wl])
                n0, w = n0 + wl, w - wl
        o_ref[...] = jnp.concatenate(pieces, axis=1)

    return pl.pallas_call(
        body, out_shape=jax.ShapeDtypeStruct((rf, out_cols), BF16), in_specs=[ANY], out_specs=VM,
        scratch_shapes=[pltpu.VMEM((N_DEV, nt * rp, LANES), BF16), pltpu.SemaphoreType.DMA(())], name='asm_' + name)(
            gathered)


def chunk_grad(layer, name, dw, gfull):
    (rf, cf), ax = SHARDED[name]
    r, c = _block_shape(name)
    rp, nt, _ = _tiles((r, c))
    off = SH_OFF[name]
    if ax == 1:
        idx = perm_index(name) if name in PERM else np.arange(cf)
        inv = np.full(cf, -1)
        inv[idx[idx >= 0]] = np.nonzero(idx >= 0)[0]

    def body(*refs):
        dw_ref, o_ref, buf, sem = refs[0], refs[-3], refs[-2], refs[-1]
        for j in range(N_DEV):
            for t in range(nt):
                if ax == 0:
                    tile = dw_ref[j * r:(j + 1) * r, t * LANES:(t + 1) * LANES]
                else:
                    cols = inv[j * c + t * LANES:j * c + min((t + 1) * LANES, c)]
                    cuts = [0] + [k for k in range(1, len(cols)) if cols[k] != cols[k - 1] + 1] + [len(cols)]
                    pieces = [dw_ref[:, int(cols[a]):int(cols[b - 1]) + 1] for a, b in zip(cuts[:-1], cuts[1:])]
                    if len(cols) < LANES:
                        pieces.append(jnp.zeros((r, LANES - len(cols)), F32))
                    tile = pieces[0] if len(pieces) == 1 else jnp.concatenate(pieces, axis=1)
                    if rp > r:
                        tile = jnp.concatenate([tile, jnp.zeros((rp - r, LANES), F32)], axis=0)
                buf[j, t * rp:(t + 1) * rp, :] = tile.astype(BF16)
        cp = pltpu.make_async_copy(buf, o_ref.at[:, pl.ds(off, nt * rp), :], sem)
        cp.start()
        cp.wait()

    shape = jax.ShapeDtypeStruct((N_DEV, LAYER_ROWS[layer], LANES), BF16)
    scratch = [pltpu.VMEM((N_DEV, nt * rp, LANES), BF16), pltpu.SemaphoreType.DMA(())]
    if gfull is None:
        return pl.pallas_call(body, out_shape=shape, in_specs=[VM], out_specs=ANY, scratch_shapes=scratch,
                              name='chunk_' + name)(dw)
    return pl.pallas_call(body, out_shape=shape, in_specs=[VM, ANY], out_specs=ANY, scratch_shapes=scratch,
                          input_output_aliases={1: 0}, name='chunk_' + name)(dw, gfull)


class GradSink:
    def __init__(self):
        self.bufs = {}

    def put(self, name, a, b, mm_name):
        (rf, cf), ax = SHARDED[name]
        r, c = _block_shape(name)
        group = GROUP_OF[name]
        direct = ax == 0 or (c % LANES == 0 and PERM[name] == [(0, cf)])
        if direct:
            self.bufs[group] = mm_tn_chunked(a, b, mm_name, group, name, self.bufs.get(group))
        else:
            self.add(name, mm(a, b, 'tn', mm_name))

    def add(self, name, dw):
        group = GROUP_OF[name]
        self.bufs[group] = chunk_grad(group, name, dw, self.bufs.get(group))


def mm_tn_chunked(a, b, mm_name, layer, wname, gfull):
    (rf, cf), ax = SHARDED[wname]
    r, c = _block_shape(wname)
    rp, nt, _ = _tiles((r, c))
    off = SH_OFF[wname]
    K, M = a.shape
    N = b.shape[1]
    assert (M, N) == (rf, cf) and rp == r
    if ax == 0:
        tn = 4 * LANES
        grid, bspec = (N // tn,), pl.BlockSpec((K, tn), lambda g: (0, g))
        ospec = pl.BlockSpec((N_DEV, 4 * r, LANES), lambda g: (0, off // (4 * r) + g, 0))
        assert off % (4 * r) == 0 and nt % 4 == 0

        def store(res, o_ref):
            for j in range(N_DEV):
                for q in range(4):
                    o_ref[j, q * r:(q + 1) * r, :] = res[j * r:(j + 1) * r, q * LANES:(q + 1) * LANES].astype(BF16)
    else:
        tn = c
        grid, bspec = (N_DEV,), pl.BlockSpec((K, tn), lambda g: (0, g))
        ospec = pl.BlockSpec((1, nt * r, LANES), lambda g: (g, off // (nt * r), 0))
        assert off % (nt * r) == 0

        def store(res, o_ref):
            for t in range(nt):
                o_ref[0, t * r:(t + 1) * r, :] = res[:, t * LANES:(t + 1) * LANES].astype(BF16)

    def body(*refs):
        a_ref, b_ref, o_ref = refs[0], refs[1], refs[-1]
        store(lax.dot_general(a_ref[...].astype(BF16), b_ref[...].astype(BF16), _TN, preferred_element_type=F32), o_ref)

    shape = jax.ShapeDtypeStruct((N_DEV, LAYER_ROWS[layer], LANES), BF16)
    aspec = pl.BlockSpec((K, M), lambda g: (0, 0))
    params = pltpu.CompilerParams(dimension_semantics=("parallel",))
    if gfull is None:
        return pl.pallas_call(body, grid=grid, in_specs=[aspec, bspec], out_specs=ospec, out_shape=shape,
                              compiler_params=params, name=mm_name)(a, b)
    return pl.pallas_call(body, grid=grid, in_specs=[aspec, bspec, ANY], out_specs=ospec, out_shape=shape,
                          input_output_aliases={2: 0}, compiler_params=params, name=mm_name)(a, b, gfull)


def pack_rep(G):
    def body(*refs):
        tiles = []
        for ref, n in zip(refs[:-1], REP_SHAPE):
            rp, nt, _ = _tiles(_rep_packed_shape(n))
            g = ref[...]
            fold = REP_FOLD.get(n, 1)
            if fold > 1:
                rr = g.shape[0] // fold
                g = jnp.concatenate([g[q * rr:(q + 1) * rr] for q in range(fold)], axis=1)
            tiles += [_tile_value(g, t, rp) for t in range(nt)]
        rows = sum(t.shape[0] for t in tiles)
        if rows < REP_ROWS:
            tiles.append(jnp.zeros((REP_ROWS - rows, LANES), F32))
        full = jnp.concatenate(tiles, axis=0)
        for j in range(N_DEV):
            refs[-1][j] = full[j * REP_CHUNK:(j + 1) * REP_CHUNK]

    return pl.pallas_call(body, out_shape=jax.ShapeDtypeStruct((N_DEV, REP_SLOT, LANES), F32),
                          in_specs=[VM] * len(REP_SHAPE), out_specs=VM, name='pack_rep')(
                              *[G[n].reshape(s) for n, s in REP_SHAPE.items()])


def _pack_small(blocks, order, rows, width, dtype):
    flat = jnp.concatenate([blocks[n].reshape(-1).astype(dtype) for n in order])
    return jnp.pad(flat, (0, rows * width - flat.shape[0])).reshape(rows, width)


def kernel(x, pre_norm, post_norm, rel_bias, a_w_in, a_lam_re, a_lam_im, a_log_dt, a_b_re, a_b_im, a_c_re, a_c_im, a_d, a_w_glu, a_b_glu, a_w_out, b_w_in, b_sinks, b_w_out, c_w_in, c_q_norm, c_kv_norm, c_w_uq, c_w_ukv, c_w_out, d_w_in, d_ln_g, d_ln_b, d_w_s, d_b_s, d_w_out, loss_target, m_pre_norm, m_post_norm, m_rel_bias, m_a_w_in, m_a_lam_re, m_a_lam_im, m_a_log_dt, m_a_b_re, m_a_b_im, m_a_c_re, m_a_c_im, m_a_d, m_a_w_glu, m_a_b_glu, m_a_w_out, m_b_w_in, m_b_sinks, m_b_w_out, m_c_w_in, m_c_q_norm, m_c_kv_norm, m_c_w_uq, m_c_w_ukv, m_c_w_out, m_d_w_in, m_d_ln_g, m_d_ln_b, m_d_w_s, m_d_b_s, m_d_w_out, v_pre_norm, v_post_norm, v_rel_bias, v_a_w_in, v_a_lam_re, v_a_lam_im, v_a_log_dt, v_a_b_re, v_a_b_im, v_a_c_re, v_a_c_im, v_a_d, v_a_w_glu, v_a_b_glu, v_a_w_out, v_b_w_in, v_b_sinks, v_b_w_out, v_c_w_in, v_c_q_norm, v_c_kv_norm, v_c_w_uq, v_c_w_ukv, v_c_w_out, v_d_w_in, v_d_ln_g, v_d_ln_b, v_d_w_s, v_d_b_s, v_d_w_out):
    loc = locals()
    P = {n: loc[n] for n in WEIGHTS}
    M = {n: loc['m_' + n] for n in WEIGHTS}
    V = {n: loc['v_' + n] for n in WEIGHTS}
    xs = x[0]
    L = xs.shape[0]

    blocks = {n: P[n].reshape(_block_shape(n)) for n in SHARDED}
    packed = {layer: pack_layer(layer, P) for layer in LAYER_PARAMS}
    W = {}

    def assemble_layer(layer, gathered):
        for n in LAYER_PARAMS[layer]:
            if n not in SHARDED_F32:
                W[n] = assemble(n, gathered)

    Pl = dict(P)

    def arrived_first(got):
        assemble_layer('a1', got[0])
        for n in SHARDED_F32:
            c = SHARDED[n][0][1]
            bc = c // N_DEV
            Pl[n] = got[1].reshape(N_DEV, -1)[:, SMALL_OFF[n]:SMALL_OFF[n] + bc].reshape(1, c)
    cx, cy, cc = _coords()
    core = jnp.reshape(cc, (1,)).astype(jnp.int32)
    chip = jnp.reshape(2 * cx + cy, (1,)).astype(jnp.int32)

    def pair_sums(gfull, tag):
        return rs_pair_add(gfull, rs_sibling(gfull, tag), core, tag)

    fwd = [layer_a_fwd, layer_b_fwd, layer_c_fwd, layer_d_fwd]
    bwd = [layer_a_bwd, layer_b_bwd, layer_c_bwd, layer_d_bwd]
    saved = []
    xc = xs

    def fpre(x_, g_):
        return [rms_fwd(x_, g_)], []
    (h,), _ = rowwise(fpre, [rw(xc)], [P['pre_norm'][0:1]], [(D_MODEL, BF16)], [], 256, 'pre_norm0')
    for i in range(4):
        if i == 0:
            yb, sv = fwd[i](h, W, Pl, comm=Both(AllGather(packed['a2']),
                                                Both(AllGather(packed['b']), AllGather(packed['c'][:C_EARLY_ROWS]))),
                            on_carried=lambda got: assemble_layer('a2', got[0]),
                            prep_comm=Both(AllGather(packed['a1']),
                                           AllGather(_pack_small(blocks, SHARDED_F32, SMALL_ROWS, 128, F32))),
                            on_prep=arrived_first)
            assemble_layer('b', sv['carried'][1])
            c_early = sv['carried'][2]
        elif i == 1:
            yb, sv = fwd[i](h, W, Pl, comm=AllGather(packed['c'][C_EARLY_ROWS:]))
            assemble_layer('c', jnp.concatenate([c_early, sv['carried'][0]], axis=1))
        elif i < 3:
            nxt = 'abcd'[i + 1]
            yb, sv = fwd[i](h, W, Pl, comm=AllGather(packed[nxt]))
            assemble_layer(nxt, sv['carried'][0])
        else:
            yb, sv = fwd[i](h, W, Pl)

        sv['x'], sv['yb'] = xc, yb
        saved.append(sv)
        if i < 3:

            def fpost(x_, y_, gpost, gpre):
                xn_ = x_ + rms_fwd(y_, gpost)
                return [xn_, rms_fwd(xn_, gpre)], []
            (xc, h), _ = rowwise(fpost, [rw(xc), rw(yb)], [P['post_norm'][i:i + 1], P['pre_norm'][i + 1:i + 2]],
                                 [(D_MODEL, F32), (D_MODEL, BF16)], [], 256, f'post_pre_norm{i}')
        else:

            def floss(x_, y_, t_, gpost):
                d = x_ + rms_fwd(y_, gpost) - t_
                return [d * (1.0 / D_MODEL)], [0.5 * jnp.sum(jnp.sum(d * d, axis=-1, keepdims=True) * (1.0 / D_MODEL),
                                                             axis=0, keepdims=True)]
            (dx,), (loss_loc,) = rowwise(floss, [rw(xc), rw(yb), rw(loss_target[0])], [P['post_norm'][i:i + 1]],
                                         [(D_MODEL, F32)], [(1, 1)], 256, 'post_norm_loss')

    G, out = {}, {}
    dpre, dpost = [None] * 4, [None] * 4

    def adam_layer(layer, part, land2):
        for n in LAYER_PARAMS[layer]:
            s = _block_shape(n)
            out[n] = adam_param(n, s, SH_OFF[n], P[n], M[n], V[n], chip, part=part, land=land2)

    def fpost_b(y_, d_, g_):
        dy, dg = rms_bwd(y_, g_, d_)
        return [dy], [dg]
    (dyb,), (dpost[3],) = rowwise(fpost_b, [rw(saved[3]['yb']), rw(dx)], [P['post_norm'][3:4]], [(D_MODEL, BF16)],
                                  [(1, D_MODEL)], 256, 'post_norm_bwd3')
    pending = None
    sink = GradSink()
    for i in reversed(range(4)):
        sv = saved[i]
        if pending is None:
            dh, g = bwd[i](dyb, W, Pl, sv, sink=sink)
        elif i > 0:
            dh, g = bwd[i](dyb, W, Pl, sv, comm=ChipExchange(pending[1]), sink=sink)
            adam_layer(pending[0], pending[1], g['carried'][0])
        else:
            early = {}

            def both():
                early['part'] = pair_sums(sink.bufs['a2'], 'a2')
                return Both(ChipExchange(pending[1]), ChipExchange(early['part']))
            dh, g = bwd[i](dyb, W, Pl, sv, comm=both, sink=sink)
            adam_layer(pending[0], pending[1], g['carried'][0])
            adam_layer('a2', early['part'], g['carried'][1])
        g.pop('carried', None)
        land_a1 = g.pop('land_a1', None)
        G.update(g)
        group = LAYER_GROUPS['abcd'[i]][0]
        for n in LAYER_PARAMS[group]:
            if n in g:
                sink.add(n, g[n])
        if i == 3:
            swap = Both(SiblingExchange(sink.bufs[group]), AllGather(jnp.broadcast_to(loss_loc, (8, LANES))))
        elif i > 0:
            swap = SiblingExchange(sink.bufs[group])
        else:
            part_a1 = rs_pair_add(sink.bufs[group], land_a1, core, group)
            swap = ChipExchange(part_a1)

        if i > 0:

            def fpre_b(x_, dh_, d_, y_, gpre, gpost):
                dxl, dg = rms_bwd(x_, gpre, dh_)
                dy, dgp = rms_bwd(y_, gpost, d_ + dxl)
                return [d_ + dxl, dy], [dg, dgp]
            (dx, dyb), (dpre[i], dpost[i - 1]), (land, *loss_all) = rowwise(
                fpre_b, [rw(sv['x']), rw(dh), rw(dx), rw(saved[i - 1]['yb'])],
                [P['pre_norm'][i:i + 1], P['post_norm'][i - 1:i]], [(D_MODEL, F32), (D_MODEL, BF16)],
                [(1, D_MODEL), (1, D_MODEL)], 256, f'pre_post_norm_bwd{i}', comm=swap)
            if loss_all:
                loss = jnp.sum(loss_all[0][:, 0, 0])
        else:

            def fpre_b0(x_, dh_, d_, g_):
                dxl, dg = rms_bwd(x_, g_, dh_)
                return [d_ + dxl], [dg]
            (dx,), (dpre[i],), (land2_a1,) = rowwise(fpre_b0, [rw(sv['x']), rw(dh), rw(dx)], [P['pre_norm'][i:i + 1]],
                                                     [(D_MODEL, F32)], [(1, D_MODEL)], 256, 'pre_norm_bwd0', comm=swap)
            adam_layer('a1', part_a1, land2_a1)
            break
        pending = (group, rs_pair_add(sink.bufs[group], land, core, group))
    G['pre_norm'] = jnp.concatenate(dpre, axis=0)
    G['post_norm'] = jnp.concatenate(dpost, axis=0)

    part = pair_sums(pack_rep(G), 'rep')
    land2 = rs_chips(part, 'rep')
    grep = all_gather(rs_rep_sum(part, land2, chip), 'ag_rep')[:, :REP_CHUNK].reshape(REP_ROWS, LANES)
    small_names = [n for n, s in REP_SHAPE.items() if s[0] <= 64]
    out.update(adam_small(small_names, grep, P, M, V))
    for n, s in REP_SHAPE.items():
        if n not in small_names:
            out[n] = adam_param(n, s, REP_OFF[n], P[n], M[n], V[n], chip, grep=grep, fold=REP_FOLD.get(n, 1))
    res = [loss, dx[None]]
    for kind in range(4):
        res += [out[n][kind].reshape(P[n].shape) for n in WEIGHTS]
    return tuple(res)
```
